```python
import math
import jax, jax.numpy as jnp
from jax import lax
import numpy as np

D_MODEL = 1024
BATCH = 8
SEQ = 4096
DEPTH = 1

CHUNK = 64
SSM_WIDTH = D_MODEL // 2
SSM_GROUP = 16
SSM_GROUPS = SSM_WIDTH // SSM_GROUP
SSM_STATE = 64
CONV_WIDTH = D_MODEL // 2
CONV_K = 3
FFN_HIDDEN = ((8 * D_MODEL // 3 + 255) // 256) * 256
IN_COLS = SSM_WIDTH + 3 * CONV_WIDTH + 2 * D_MODEL
ALPHA = (2.0 * DEPTH) ** 0.25
BETA = (8.0 * DEPTH) ** -0.25
DT_MIN = 0.001
DT_MAX = 0.1
LN_EPS = 1e-5

kernel_name = "hybrid_s5_shortconv_gated_deepnorm_block"


def layer_norm(x, g, b):
    xf = x.astype(jnp.float32)
    mu = jnp.mean(xf, axis=-1, keepdims=True)
    var = jnp.mean(jnp.square(xf - mu), axis=-1, keepdims=True)
    y = (xf - mu) * lax.rsqrt(var + LN_EPS) * g.astype(jnp.float32) + b.astype(jnp.float32)
    return y.astype(x.dtype)


def _complex_affine_combine(earlier, later):
    ar_i, ai_i, br_i, bi_i = earlier
    ar_j, ai_j, br_j, bi_j = later
    ar = ar_j * ar_i - ai_j * ai_i
    ai = ar_j * ai_i + ai_j * ar_i
    br = ar_j * br_i - ai_j * bi_i + br_j
    bi = ar_j * bi_i + ai_j * br_i + bi_j
    return (ar, ai, br, bi)


def s5_ssm(u, lam_re, lam_im, log_dt, b_re, b_im, c_re, c_im, d_skip):
    f32 = jnp.float32
    bsz, slen, _ = u.shape
    uf = u.astype(f32).reshape(bsz, slen, SSM_GROUPS, SSM_GROUP).transpose(1, 0, 2, 3)
    lr = lam_re.astype(f32)
    li = lam_im.astype(f32)
    dt = jnp.exp(log_dt.astype(f32))[:, None]
    mag = jnp.exp(lr * dt)
    lb_re = mag * jnp.cos(li * dt)
    lb_im = mag * jnp.sin(li * dt)
    den = lr * lr + li * li
    num_re = lb_re - 1.0
    fr = (num_re * lr + lb_im * li) / den
    fi = (lb_im * lr - num_re * li) / den
    br = b_re.astype(f32)
    bi = b_im.astype(f32)
    bb_re = fr[..., None] * br - fi[..., None] * bi
    bb_im = fr[..., None] * bi + fi[..., None] * br
    bu_re = jnp.einsum('sbgc,gpc->sbgp', uf, bb_re)
    bu_im = jnp.einsum('sbgc,gpc->sbgp', uf, bb_im)
    a_re = jnp.broadcast_to(lb_re[None, None], (slen, 1, SSM_GROUPS, SSM_STATE))
    a_im = jnp.broadcast_to(lb_im[None, None], (slen, 1, SSM_GROUPS, SSM_STATE))
    _, _, xs_re, xs_im = lax.associative_scan(
        _complex_affine_combine, (a_re, a_im, bu_re, bu_im), axis=0)
    y = (jnp.einsum('sbgp,gcp->sbgc', xs_re, c_re.astype(f32))
         - jnp.einsum('sbgp,gcp->sbgc', xs_im, c_im.astype(f32))
         + d_skip.astype(f32).reshape(SSM_GROUPS, SSM_GROUP) * uf)
    y = y.transpose(1, 0, 2, 3).reshape(bsz, slen, SSM_WIDTH)
    return y.astype(u.dtype)


def causal_depthwise_conv(z, w):
    return lax.conv_general_dilated(
        z, w[:, None, :].astype(z.dtype), window_strides=(1,), padding=[(CONV_K - 1, 0)],
        dimension_numbers=('NWC', 'WIO', 'NWC'), feature_group_count=CONV_WIDTH)


def hybrid_layer(x, w_in, b_in, ssm_lambda_re, ssm_lambda_im, ssm_log_dt, ssm_b_re, ssm_b_im,
                 ssm_c_re, ssm_c_im, ssm_d, glu_w, glu_b, w_ssm_out, conv_w, w_conv_out, w_o,
                 ln1_g, ln1_b, w_gate, w_up, w_down, ln2_g, ln2_b):
    proj = jnp.einsum('bsd,dn->bsn', x, w_in) + b_in
    o1 = SSM_WIDTH
    o2 = o1 + CONV_WIDTH
    o3 = o2 + CONV_WIDTH
    o4 = o3 + CONV_WIDTH
    o5 = o4 + D_MODEL
    u, h, c_gate, b_gate, gate_a, gate_b = jnp.split(proj, [o1, o2, o3, o4, o5], axis=-1)

    y_a = s5_ssm(u, ssm_lambda_re, ssm_lambda_im, ssm_log_dt, ssm_b_re, ssm_b_im,
                 ssm_c_re, ssm_c_im, ssm_d)
    g = jax.nn.gelu(y_a)
    y_a = g * jax.nn.sigmoid(jnp.einsum('bsc,ce->bse', g, glu_w) + glu_b)
    y_a = jnp.einsum('bsc,cd->bsd', y_a, w_ssm_out)

    z = causal_depthwise_conv(c_gate * h, conv_w)
    y_b = jnp.einsum('bsc,cd->bsd', b_gate * z, w_conv_out)

    merged = jax.nn.sigmoid(gate_a) * y_a + jax.nn.sigmoid(gate_b) * y_b
    mix = jnp.einsum('bsd,de->bse', merged, w_o)
    x = layer_norm(ALPHA * x + mix, ln1_g, ln1_b)

    hid = jax.nn.silu(jnp.einsum('bsd,df->bsf', x, w_gate)) * jnp.einsum('bsd,df->bsf', x, w_up)
    ffn = jnp.einsum('bsf,fd->bsd', hid, w_down)
    x = layer_norm(ALPHA * x + ffn, ln2_g, ln2_b)
    return x


def _fwd_setup_inputs(seed: int = 0) -> dict:
    key = jax.random.key(seed)
    ks = jax.random.split(key, 24)
    L = DEPTH
    f32 = jnp.float32
    nrm = lambda k, shape, s: jax.random.normal(k, shape, f32) * s
    x = jax.random.normal(ks[0], (BATCH, SEQ, D_MODEL), f32)
    w_in = nrm(ks[1], (L, D_MODEL, IN_COLS), D_MODEL ** -0.5)
    b_in = nrm(ks[2], (L, IN_COLS), 0.01)
    n_idx = jnp.arange(SSM_STATE, dtype=f32)
    ssm_lambda_re = -0.5 + nrm(ks[3], (L, SSM_GROUPS, SSM_STATE), 0.01)
    ssm_lambda_im = math.pi * n_idx[None, None, :] + nrm(ks[4], (L, SSM_GROUPS, SSM_STATE), 0.01)
    ssm_log_dt = jax.random.uniform(ks[5], (L, SSM_GROUPS), f32,
                                    minval=math.log(DT_MIN), maxval=math.log(DT_MAX))
    ssm_b_re = nrm(ks[6], (L, SSM_GROUPS, SSM_STATE, SSM_GROUP), (2.0 * SSM_GROUP) ** -0.5)
    ssm_b_im = nrm(ks[7], (L, SSM_GROUPS, SSM_STATE, SSM_GROUP), (2.0 * SSM_GROUP) ** -0.5)
    ssm_c_re = nrm(ks[8], (L, SSM_GROUPS, SSM_GROUP, SSM_STATE), SSM_STATE ** -0.5)
    ssm_c_im = nrm(ks[9], (L, SSM_GROUPS, SSM_GROUP, SSM_STATE), SSM_STATE ** -0.5)
    ssm_d = nrm(ks[10], (L, SSM_WIDTH), 1.0)
    glu_w = nrm(ks[11], (L, SSM_WIDTH, SSM_WIDTH), SSM_WIDTH ** -0.5)
    glu_b = nrm(ks[12], (L, SSM_WIDTH), 0.01)
    w_ssm_out = nrm(ks[13], (L, SSM_WIDTH, D_MODEL), BETA * SSM_WIDTH ** -0.5)
    conv_w = nrm(ks[14], (L, CONV_K, CONV_WIDTH), CONV_K ** -0.5)
    w_conv_out = nrm(ks[15], (L, CONV_WIDTH, D_MODEL), BETA * CONV_WIDTH ** -0.5)
    w_o = nrm(ks[16], (L, D_MODEL, D_MODEL), BETA * D_MODEL ** -0.5)
    ln1_g = 1.0 + nrm(ks[17], (L, D_MODEL), 0.01)
    ln1_b = nrm(ks[18], (L, D_MODEL), 0.01)
    w_gate = nrm(ks[19], (L, D_MODEL, FFN_HIDDEN), D_MODEL ** -0.5)
    w_up = nrm(ks[20], (L, D_MODEL, FFN_HIDDEN), D_MODEL ** -0.5)
    w_down = nrm(ks[21], (L, FFN_HIDDEN, D_MODEL), BETA * FFN_HIDDEN ** -0.5)
    ln2_g = 1.0 + nrm(ks[22], (L, D_MODEL), 0.01)
    ln2_b = nrm(ks[23], (L, D_MODEL), 0.01)
    return {"x": x, "w_in": w_in, "b_in": b_in,
            "ssm_lambda_re": ssm_lambda_re, "ssm_lambda_im": ssm_lambda_im,
            "ssm_log_dt": ssm_log_dt, "ssm_b_re": ssm_b_re, "ssm_b_im": ssm_b_im,
            "ssm_c_re": ssm_c_re, "ssm_c_im": ssm_c_im, "ssm_d": ssm_d,
            "glu_w": glu_w, "glu_b": glu_b, "w_ssm_out": w_ssm_out,
            "conv_w": conv_w, "w_conv_out": w_conv_out, "w_o": w_o,
            "ln1_g": ln1_g, "ln1_b": ln1_b, "w_gate": w_gate, "w_up": w_up,
            "w_down": w_down, "ln2_g": ln2_g, "ln2_b": ln2_b}


def _fwd_reference(x, w_in, b_in, ssm_lambda_re, ssm_lambda_im, ssm_log_dt, ssm_b_re, ssm_b_im,
              ssm_c_re, ssm_c_im, ssm_d, glu_w, glu_b, w_ssm_out, conv_w, w_conv_out, w_o,
              ln1_g, ln1_b, w_gate, w_up, w_down, ln2_g, ln2_b):
    for l in range(DEPTH):
        x = hybrid_layer(x, w_in[l], b_in[l], ssm_lambda_re[l], ssm_lambda_im[l], ssm_log_dt[l],
                         ssm_b_re[l], ssm_b_im[l], ssm_c_re[l], ssm_c_im[l], ssm_d[l],
                         glu_w[l], glu_b[l], w_ssm_out[l], conv_w[l], w_conv_out[l], w_o[l],
                         ln1_g[l], ln1_b[l], w_gate[l], w_up[l], w_down[l], ln2_g[l], ln2_b[l])
    return x


import jax as _jax
import jax.numpy as _jnp

TWIN_FORMAT = 'train_step'
FWD_PARAMS = ['x', 'w_in', 'b_in', 'ssm_lambda_re', 'ssm_lambda_im', 'ssm_log_dt', 'ssm_b_re', 'ssm_b_im', 'ssm_c_re', 'ssm_c_im', 'ssm_d', 'glu_w', 'glu_b', 'w_ssm_out', 'conv_w', 'w_conv_out', 'w_o', 'ln1_g', 'ln1_b', 'w_gate', 'w_up', 'w_down', 'ln2_g', 'ln2_b']
TWIN_WEIGHTS = ['w_in', 'b_in', 'ssm_lambda_re', 'ssm_lambda_im', 'ssm_log_dt', 'ssm_b_re', 'ssm_b_im', 'ssm_c_re', 'ssm_c_im', 'ssm_d', 'glu_w', 'glu_b', 'w_ssm_out', 'conv_w', 'w_conv_out', 'w_o', 'ln1_g', 'ln1_b', 'w_gate', 'w_up', 'w_down', 'ln2_g', 'ln2_b']
TWIN_DIFF_INPUT = 'x'
TWIN_INPUTS = ['x', 'w_in', 'b_in', 'ssm_lambda_re', 'ssm_lambda_im', 'ssm_log_dt', 'ssm_b_re', 'ssm_b_im', 'ssm_c_re', 'ssm_c_im', 'ssm_d', 'glu_w', 'glu_b', 'w_ssm_out', 'conv_w', 'w_conv_out', 'w_o', 'ln1_g', 'ln1_b', 'w_gate', 'w_up', 'w_down', 'ln2_g', 'ln2_b', 'loss_target', 'm_w_in', 'm_b_in', 'm_ssm_lambda_re', 'm_ssm_lambda_im', 'm_ssm_log_dt', 'm_ssm_b_re', 'm_ssm_b_im', 'm_ssm_c_re', 'm_ssm_c_im', 'm_ssm_d', 'm_glu_w', 'm_glu_b', 'm_w_ssm_out', 'm_conv_w', 'm_w_conv_out', 'm_w_o', 'm_ln1_g', 'm_ln1_b', 'm_w_gate', 'm_w_up', 'm_w_down', 'm_ln2_g', 'm_ln2_b', 'v_w_in', 'v_b_in', 'v_ssm_lambda_re', 'v_ssm_lambda_im', 'v_ssm_log_dt', 'v_ssm_b_re', 'v_ssm_b_im', 'v_ssm_c_re', 'v_ssm_c_im', 'v_ssm_d', 'v_glu_w', 'v_glu_b', 'v_w_ssm_out', 'v_conv_w', 'v_w_conv_out', 'v_w_o', 'v_ln1_g', 'v_ln1_b', 'v_w_gate', 'v_w_up', 'v_w_down', 'v_ln2_g', 'v_ln2_b']
TWIN_OUTPUTS = ['loss', 'grad_x', 'grad_w_in', 'grad_b_in', 'grad_ssm_lambda_re', 'grad_ssm_lambda_im', 'grad_ssm_log_dt', 'grad_ssm_b_re', 'grad_ssm_b_im', 'grad_ssm_c_re', 'grad_ssm_c_im', 'grad_ssm_d', 'grad_glu_w', 'grad_glu_b', 'grad_w_ssm_out', 'grad_conv_w', 'grad_w_conv_out', 'grad_w_o', 'grad_ln1_g', 'grad_ln1_b', 'grad_w_gate', 'grad_w_up', 'grad_w_down', 'grad_ln2_g', 'grad_ln2_b', 'delta_w_in', 'delta_b_in', 'delta_ssm_lambda_re', 'delta_ssm_lambda_im', 'delta_ssm_log_dt', 'delta_ssm_b_re', 'delta_ssm_b_im', 'delta_ssm_c_re', 'delta_ssm_c_im', 'delta_ssm_d', 'delta_glu_w', 'delta_glu_b', 'delta_w_ssm_out', 'delta_conv_w', 'delta_w_conv_out', 'delta_w_o', 'delta_ln1_g', 'delta_ln1_b', 'delta_w_gate', 'delta_w_up', 'delta_w_down', 'delta_ln2_g', 'delta_ln2_b', 'new_m_w_in', 'new_m_b_in', 'new_m_ssm_lambda_re', 'new_m_ssm_lambda_im', 'new_m_ssm_log_dt', 'new_m_ssm_b_re', 'new_m_ssm_b_im', 'new_m_ssm_c_re', 'new_m_ssm_c_im', 'new_m_ssm_d', 'new_m_glu_w', 'new_m_glu_b', 'new_m_w_ssm_out', 'new_m_conv_w', 'new_m_w_conv_out', 'new_m_w_o', 'new_m_ln1_g', 'new_m_ln1_b', 'new_m_w_gate', 'new_m_w_up', 'new_m_w_down', 'new_m_ln2_g', 'new_m_ln2_b', 'new_v_w_in', 'new_v_b_in', 'new_v_ssm_lambda_re', 'new_v_ssm_lambda_im', 'new_v_ssm_log_dt', 'new_v_ssm_b_re', 'new_v_ssm_b_im', 'new_v_ssm_c_re', 'new_v_ssm_c_im', 'new_v_ssm_d', 'new_v_glu_w', 'new_v_glu_b', 'new_v_w_ssm_out', 'new_v_conv_w', 'new_v_w_conv_out', 'new_v_w_o', 'new_v_ln1_g', 'new_v_ln1_b', 'new_v_w_gate', 'new_v_w_up', 'new_v_w_down', 'new_v_ln2_g', 'new_v_ln2_b']
TWIN_LEAF_KINDS = {'loss': 'loss', 'grad_x': 'grad_x', 'grad_w_in': 'grad_w', 'grad_b_in': 'grad_w', 'grad_ssm_lambda_re': 'grad_w', 'grad_ssm_lambda_im': 'grad_w', 'grad_ssm_log_dt': 'grad_w', 'grad_ssm_b_re': 'grad_w', 'grad_ssm_b_im': 'grad_w', 'grad_ssm_c_re': 'grad_w', 'grad_ssm_c_im': 'grad_w', 'grad_ssm_d': 'grad_w', 'grad_glu_w': 'grad_w', 'grad_glu_b': 'grad_w', 'grad_w_ssm_out': 'grad_w', 'grad_conv_w': 'grad_w', 'grad_w_conv_out': 'grad_w', 'grad_w_o': 'grad_w', 'grad_ln1_g': 'grad_w', 'grad_ln1_b': 'grad_w', 'grad_w_gate': 'grad_w', 'grad_w_up': 'grad_w', 'grad_w_down': 'grad_w', 'grad_ln2_g': 'grad_w', 'grad_ln2_b': 'grad_w', 'delta_w_in': 'delta_w', 'delta_b_in': 'delta_w', 'delta_ssm_lambda_re': 'delta_w', 'delta_ssm_lambda_im': 'delta_w', 'delta_ssm_log_dt': 'delta_w', 'delta_ssm_b_re': 'delta_w', 'delta_ssm_b_im': 'delta_w', 'delta_ssm_c_re': 'delta_w', 'delta_ssm_c_im': 'delta_w', 'delta_ssm_d': 'delta_w', 'delta_glu_w': 'delta_w', 'delta_glu_b': 'delta_w', 'delta_w_ssm_out': 'delta_w', 'delta_conv_w': 'delta_w', 'delta_w_conv_out': 'delta_w', 'delta_w_o': 'delta_w', 'delta_ln1_g': 'delta_w', 'delta_ln1_b': 'delta_w', 'delta_w_gate': 'delta_w', 'delta_w_up': 'delta_w', 'delta_w_down': 'delta_w', 'delta_ln2_g': 'delta_w', 'delta_ln2_b': 'delta_w', 'new_m_w_in': 'new_m', 'new_m_b_in': 'new_m', 'new_m_ssm_lambda_re': 'new_m', 'new_m_ssm_lambda_im': 'new_m', 'new_m_ssm_log_dt': 'new_m', 'new_m_ssm_b_re': 'new_m', 'new_m_ssm_b_im': 'new_m', 'new_m_ssm_c_re': 'new_m', 'new_m_ssm_c_im': 'new_m', 'new_m_ssm_d': 'new_m', 'new_m_glu_w': 'new_m', 'new_m_glu_b': 'new_m', 'new_m_w_ssm_out': 'new_m', 'new_m_conv_w': 'new_m', 'new_m_w_conv_out': 'new_m', 'new_m_w_o': 'new_m', 'new_m_ln1_g': 'new_m', 'new_m_ln1_b': 'new_m', 'new_m_w_gate': 'new_m', 'new_m_w_up': 'new_m', 'new_m_w_down': 'new_m', 'new_m_ln2_g': 'new_m', 'new_m_ln2_b': 'new_m', 'new_v_w_in': 'new_v', 'new_v_b_in': 'new_v', 'new_v_ssm_lambda_re': 'new_v', 'new_v_ssm_lambda_im': 'new_v', 'new_v_ssm_log_dt': 'new_v', 'new_v_ssm_b_re': 'new_v', 'new_v_ssm_b_im': 'new_v', 'new_v_ssm_c_re': 'new_v', 'new_v_ssm_c_im': 'new_v', 'new_v_ssm_d': 'new_v', 'new_v_glu_w': 'new_v', 'new_v_glu_b': 'new_v', 'new_v_w_ssm_out': 'new_v', 'new_v_conv_w': 'new_v', 'new_v_w_conv_out': 'new_v', 'new_v_w_o': 'new_v', 'new_v_ln1_g': 'new_v', 'new_v_ln1_b': 'new_v', 'new_v_w_gate': 'new_v', 'new_v_w_up': 'new_v', 'new_v_w_down': 'new_v', 'new_v_ln2_g': 'new_v', 'new_v_ln2_b': 'new_v'}


def _forward(args):
    return _fwd_reference(*[args[k] for k in FWD_PARAMS])


def _output_shape():
    out = _jax.eval_shape(lambda: _forward(_fwd_setup_inputs(0)))
    return out.shape, out.dtype

N_MICROBATCH = 1
ADAM_LR = 0.001
ADAM_B1 = 0.9
ADAM_B2 = 0.999
ADAM_EPS = 1e-08
ADAM_WD = 0.01
ADAM_STEP = 10
PER_EXAMPLE_BATCH_AXIS = {'x': 0, 'loss_target': 0}
SHARED_INPUTS = []
_WEIGHT_DTYPES = {'w_in': _jnp.float32, 'b_in': _jnp.float32, 'ssm_lambda_re': _jnp.float32, 'ssm_lambda_im': _jnp.float32, 'ssm_log_dt': _jnp.float32, 'ssm_b_re': _jnp.float32, 'ssm_b_im': _jnp.float32, 'ssm_c_re': _jnp.float32, 'ssm_c_im': _jnp.float32, 'ssm_d': _jnp.float32, 'glu_w': _jnp.float32, 'glu_b': _jnp.float32, 'w_ssm_out': _jnp.float32, 'conv_w': _jnp.float32, 'w_conv_out': _jnp.float32, 'w_o': _jnp.float32, 'ln1_g': _jnp.float32, 'ln1_b': _jnp.float32, 'w_gate': _jnp.float32, 'w_up': _jnp.float32, 'w_down': _jnp.float32, 'ln2_g': _jnp.float32, 'ln2_b': _jnp.float32}
MOMENT_SCALE = {'w_in': 2.681438e-02, 'b_in': 2.691505e-02, 'ssm_lambda_re': 1.040015e-03, 'ssm_lambda_im': 1.214372e-03, 'ssm_log_dt': 5.227751e-01, 'ssm_b_re': 7.709019e-04, 'ssm_b_im': 7.715790e-04, 'ssm_c_re': 1.056524e-03, 'ssm_c_im': 1.052922e-03, 'ssm_d': 1.618259e-02, 'glu_w': 4.611262e-03, 'glu_b': 6.652604e-03, 'w_ssm_out': 1.867865e-02, 'conv_w': 4.217860e-02, 'w_conv_out': 4.955229e-02, 'w_o': 5.293203e-02, 'ln1_g': 4.186561e-01, 'ln1_b': 2.488804e-01, 'w_gate': 3.146334e-02, 'w_up': 3.058733e-02, 'w_down': 8.502245e-02, 'ln2_g': 3.196122e+01, 'ln2_b': 6.612008e-01}


def _to_microbatches(a, axis):
    t = _jnp.moveaxis(a, axis, 0)
    t = t.reshape((N_MICROBATCH, t.shape[0] // N_MICROBATCH) + t.shape[1:])
    return _jnp.moveaxis(t, 1, axis + 1)


def setup_inputs(seed: int = 0) -> dict:
    inp = _fwd_setup_inputs(seed)
    key = _jax.random.fold_in(_jax.random.key(seed), 7919)
    shape, _ = _output_shape()
    out = dict(inp)
    out["loss_target"] = _jax.random.normal(_jax.random.fold_in(key, 0), shape, _jnp.float32)
    for i, name in enumerate(TWIN_WEIGHTS):
        w = inp[name].astype(_jnp.float32)
        if MOMENT_SCALE is None:
            s = _jnp.sqrt(_jnp.mean(_jnp.square(w)) + 1e-30)
        else:
            s = MOMENT_SCALE[name]
        km, kv = _jax.random.split(_jax.random.fold_in(key, i + 1))
        out[name] = w
        out["m_" + name] = s * _jax.random.normal(km, w.shape, _jnp.float32)
        out["v_" + name] = (s * s) * _jax.random.uniform(kv, w.shape, _jnp.float32, 0.5, 1.5)
    if N_MICROBATCH > 1:
        for name, axis in PER_EXAMPLE_BATCH_AXIS.items():
            out[name] = _to_microbatches(out[name], axis)
    return {'x': out['x'], 'w_in': out['w_in'], 'b_in': out['b_in'], 'ssm_lambda_re': out['ssm_lambda_re'], 'ssm_lambda_im': out['ssm_lambda_im'], 'ssm_log_dt': out['ssm_log_dt'], 'ssm_b_re': out['ssm_b_re'], 'ssm_b_im': out['ssm_b_im'], 'ssm_c_re': out['ssm_c_re'], 'ssm_c_im': out['ssm_c_im'], 'ssm_d': out['ssm_d'], 'glu_w': out['glu_w'], 'glu_b': out['glu_b'], 'w_ssm_out': out['w_ssm_out'], 'conv_w': out['conv_w'], 'w_conv_out': out['w_conv_out'], 'w_o': out['w_o'], 'ln1_g': out['ln1_g'], 'ln1_b': out['ln1_b'], 'w_gate': out['w_gate'], 'w_up': out['w_up'], 'w_down': out['w_down'], 'ln2_g': out['ln2_g'], 'ln2_b': out['ln2_b'], 'loss_target': out['loss_target'], 'm_w_in': out['m_w_in'], 'm_b_in': out['m_b_in'], 'm_ssm_lambda_re': out['m_ssm_lambda_re'], 'm_ssm_lambda_im': out['m_ssm_lambda_im'], 'm_ssm_log_dt': out['m_ssm_log_dt'], 'm_ssm_b_re': out['m_ssm_b_re'], 'm_ssm_b_im': out['m_ssm_b_im'], 'm_ssm_c_re': out['m_ssm_c_re'], 'm_ssm_c_im': out['m_ssm_c_im'], 'm_ssm_d': out['m_ssm_d'], 'm_glu_w': out['m_glu_w'], 'm_glu_b': out['m_glu_b'], 'm_w_ssm_out': out['m_w_ssm_out'], 'm_conv_w': out['m_conv_w'], 'm_w_conv_out': out['m_w_conv_out'], 'm_w_o': out['m_w_o'], 'm_ln1_g': out['m_ln1_g'], 'm_ln1_b': out['m_ln1_b'], 'm_w_gate': out['m_w_gate'], 'm_w_up': out['m_w_up'], 'm_w_down': out['m_w_down'], 'm_ln2_g': out['m_ln2_g'], 'm_ln2_b': out['m_ln2_b'], 'v_w_in': out['v_w_in'], 'v_b_in': out['v_b_in'], 'v_ssm_lambda_re': out['v_ssm_lambda_re'], 'v_ssm_lambda_im': out['v_ssm_lambda_im'], 'v_ssm_log_dt': out['v_ssm_log_dt'], 'v_ssm_b_re': out['v_ssm_b_re'], 'v_ssm_b_im': out['v_ssm_b_im'], 'v_ssm_c_re': out['v_ssm_c_re'], 'v_ssm_c_im': out['v_ssm_c_im'], 'v_ssm_d': out['v_ssm_d'], 'v_glu_w': out['v_glu_w'], 'v_glu_b': out['v_glu_b'], 'v_w_ssm_out': out['v_w_ssm_out'], 'v_conv_w': out['v_conv_w'], 'v_w_conv_out': out['v_w_conv_out'], 'v_w_o': out['v_w_o'], 'v_ln1_g': out['v_ln1_g'], 'v_ln1_b': out['v_ln1_b'], 'v_w_gate': out['v_w_gate'], 'v_w_up': out['v_w_up'], 'v_w_down': out['v_w_down'], 'v_ln2_g': out['v_ln2_g'], 'v_ln2_b': out['v_ln2_b']}


def _loss(weights, diff, rest, loss_target):
    with _jax.named_scope("forward"):
        args = {**rest, TWIN_DIFF_INPUT: diff, **{k: w.astype(_WEIGHT_DTYPES[k]) for k, w in weights.items()}}
        y = _forward(args)
    with _jax.named_scope("loss_head"):
        err = _jnp.square(y.astype(_jnp.float32) - loss_target)
        return 0.5 * _jnp.sum(_jnp.mean(err, axis=-1)) if err.ndim else 0.5 * err


def _adamw(w, g, m, v):
    m = ADAM_B1 * m + (1.0 - ADAM_B1) * g
    v = ADAM_B2 * v + (1.0 - ADAM_B2) * _jnp.square(g)
    m_hat = m / (1.0 - ADAM_B1 ** ADAM_STEP)
    v_hat = v / (1.0 - ADAM_B2 ** ADAM_STEP)
    delta = -ADAM_LR * (m_hat / (_jnp.sqrt(v_hat) + ADAM_EPS) + ADAM_WD * w)
    return delta, m, v


def reference(x, w_in, b_in, ssm_lambda_re, ssm_lambda_im, ssm_log_dt, ssm_b_re, ssm_b_im, ssm_c_re, ssm_c_im, ssm_d, glu_w, glu_b, w_ssm_out, conv_w, w_conv_out, w_o, ln1_g, ln1_b, w_gate, w_up, w_down, ln2_g, ln2_b, loss_target, m_w_in, m_b_in, m_ssm_lambda_re, m_ssm_lambda_im, m_ssm_log_dt, m_ssm_b_re, m_ssm_b_im, m_ssm_c_re, m_ssm_c_im, m_ssm_d, m_glu_w, m_glu_b, m_w_ssm_out, m_conv_w, m_w_conv_out, m_w_o, m_ln1_g, m_ln1_b, m_w_gate, m_w_up, m_w_down, m_ln2_g, m_ln2_b, v_w_in, v_b_in, v_ssm_lambda_re, v_ssm_lambda_im, v_ssm_log_dt, v_ssm_b_re, v_ssm_b_im, v_ssm_c_re, v_ssm_c_im, v_ssm_d, v_glu_w, v_glu_b, v_w_ssm_out, v_conv_w, v_w_conv_out, v_w_o, v_ln1_g, v_ln1_b, v_w_gate, v_w_up, v_w_down, v_ln2_g, v_ln2_b):
    given = dict(x=x, w_in=w_in, b_in=b_in, ssm_lambda_re=ssm_lambda_re, ssm_lambda_im=ssm_lambda_im, ssm_log_dt=ssm_log_dt, ssm_b_re=ssm_b_re, ssm_b_im=ssm_b_im, ssm_c_re=ssm_c_re, ssm_c_im=ssm_c_im, ssm_d=ssm_d, glu_w=glu_w, glu_b=glu_b, w_ssm_out=w_ssm_out, conv_w=conv_w, w_conv_out=w_conv_out, w_o=w_o, ln1_g=ln1_g, ln1_b=ln1_b, w_gate=w_gate, w_up=w_up, w_down=w_down, ln2_g=ln2_g, ln2_b=ln2_b, loss_target=loss_target, m_w_in=m_w_in, m_b_in=m_b_in, m_ssm_lambda_re=m_ssm_lambda_re, m_ssm_lambda_im=m_ssm_lambda_im, m_ssm_log_dt=m_ssm_log_dt, m_ssm_b_re=m_ssm_b_re, m_ssm_b_im=m_ssm_b_im, m_ssm_c_re=m_ssm_c_re, m_ssm_c_im=m_ssm_c_im, m_ssm_d=m_ssm_d, m_glu_w=m_glu_w, m_glu_b=m_glu_b, m_w_ssm_out=m_w_ssm_out, m_conv_w=m_conv_w, m_w_conv_out=m_w_conv_out, m_w_o=m_w_o, m_ln1_g=m_ln1_g, m_ln1_b=m_ln1_b, m_w_gate=m_w_gate, m_w_up=m_w_up, m_w_down=m_w_down, m_ln2_g=m_ln2_g, m_ln2_b=m_ln2_b, v_w_in=v_w_in, v_b_in=v_b_in, v_ssm_lambda_re=v_ssm_lambda_re, v_ssm_lambda_im=v_ssm_lambda_im, v_ssm_log_dt=v_ssm_log_dt, v_ssm_b_re=v_ssm_b_re, v_ssm_b_im=v_ssm_b_im, v_ssm_c_re=v_ssm_c_re, v_ssm_c_im=v_ssm_c_im, v_ssm_d=v_ssm_d, v_glu_w=v_glu_w, v_glu_b=v_glu_b, v_w_ssm_out=v_w_ssm_out, v_conv_w=v_conv_w, v_w_conv_out=v_w_conv_out, v_w_o=v_w_o, v_ln1_g=v_ln1_g, v_ln1_b=v_ln1_b, v_w_gate=v_w_gate, v_w_up=v_w_up, v_w_down=v_w_down, v_ln2_g=v_ln2_g, v_ln2_b=v_ln2_b)
    weights = {n: given[n] for n in TWIN_WEIGHTS}
    shared = {n: given[n] for n in SHARED_INPUTS}
    per_example = {n: given[n] for n in ['x']}
    grad_fn = _jax.value_and_grad(_loss, argnums=(0, 1))

    def one_microbatch(ex, loss_target):
        ex = dict(ex)
        diff = ex.pop(TWIN_DIFF_INPUT)
        return grad_fn(weights, diff, {**shared, **ex}, loss_target)

    if N_MICROBATCH == 1:
        loss, (grad_w, grad_x) = one_microbatch(per_example, given["loss_target"])
    else:
        def body(carry, xs):
            loss_sum, grad_sum = carry
            l_k, (gw_k, gx_k) = one_microbatch(xs[0], xs[1])
            with _jax.named_scope("update"):
                return (loss_sum + l_k, _jax.tree.map(_jnp.add, grad_sum, gw_k)), gx_k

        init = (_jnp.zeros((), _jnp.float32), _jax.tree.map(_jnp.zeros_like, weights))
        (loss, grad_w), grad_x = _jax.lax.scan(body, init, (per_example, given["loss_target"]))
    with _jax.named_scope("update"):
        delta_w, new_m, new_v = {}, {}, {}
        for n in TWIN_WEIGHTS:
            delta_w[n], new_m[n], new_v[n] = _adamw(weights[n], grad_w[n], given["m_" + n], given["v_" + n])
    return (loss, grad_x, *[grad_w[n] for n in TWIN_WEIGHTS], *[delta_w[n] for n in TWIN_WEIGHTS],
            *[new_m[n] for n in TWIN_WEIGHTS], *[new_v[n] for n in TWIN_WEIGHTS])
```

```python
import math

import jax
import jax.numpy as jnp
from jax import lax
from jax.experimental import pallas as pl
from jax.experimental.pallas import tpu as pltpu

f32, bf16 = jnp.float32, jnp.bfloat16
S = jax.ShapeDtypeStruct
MESH = pl.DeviceIdType.MESH

D = 1024
W = 512
NG, NP, GC = 32, 64, 16
F = 2816
NDEV = 8
FS = F // NDEV
IN_COLS = 8 * W
ALPHA = 2.0 ** 0.25
LN_EPS = 1e-5
ADAM_LR, ADAM_B1, ADAM_B2, ADAM_EPS, ADAM_WD, ADAM_STEP = 0.001, 0.9, 0.999, 1e-08, 0.01, 10
NC = 32
LANE = 128
SW = 4 * LANE
VMEM_LIMIT = 56 * 1024 * 1024
GRAD_DT = bf16


def _cp(sem=None, vmem=None):
    return pltpu.CompilerParams(dimension_semantics=sem, vmem_limit_bytes=vmem)


def _dot(a, b):
    return jnp.dot(a, b, preferred_element_type=f32)


def _dot_nt(a, b):
    return lax.dot_general(a, b, (((1,), (1,)), ((), ())), preferred_element_type=f32)


def _dot_tn(a, b):
    return lax.dot_general(a, b, (((0,), (0,)), ((), ())), preferred_element_type=f32)


def _sigmoid(x):
    return 1.0 / (1.0 + jnp.exp(-x))


_GK = math.sqrt(2.0 / math.pi)


def _gelu(x):
    return 0.5 * x * (1.0 + jnp.tanh(_GK * (x + 0.044715 * x * x * x)))


def _gelu_grad(x):
    th = jnp.tanh(_GK * (x + 0.044715 * x * x * x))
    return 0.5 * (1.0 + th) + 0.5 * x * (1.0 - th * th) * _GK * (1.0 + 3.0 * 0.044715 * x * x)


def _ln_stats(r):
    mu = jnp.mean(r, axis=-1, keepdims=True)
    xc = r - mu
    var = jnp.mean(xc * xc, axis=-1, keepdims=True)
    rstd = lax.rsqrt(var + LN_EPS)
    return xc * rstd, rstd


def _ln_bwd(dy, xhat, rstd, g):
    dxh = dy * g
    m1 = jnp.mean(dxh, axis=-1, keepdims=True)
    m2 = jnp.mean(dxh * xhat, axis=-1, keepdims=True)
    return rstd * (dxh - m1 - xhat * m2)


def _coords():
    return lax.axis_index("x"), lax.axis_index("y"), lax.axis_index("c")


def all_gather(arrs, name):
    n = len(arrs)

    def body(*refs):
        ins, outs = refs[:n], refs[n:2 * n]
        send_sems, recv_sems, loc_sems = refs[2 * n:]
        x, y, c = _coords()
        me, sib = (x, y, c), (x, y, 1 - c)
        chips = [(1 - x, y), (x, 1 - y), (1 - x, 1 - y)]

        def slot(a, dev):
            return outs[a].at[4 * dev[0] + 2 * dev[1] + dev[2]]

        def copy(a, k, block, to, src=None):
            return pltpu.make_async_remote_copy(
                src_ref=slot(a, block) if src is None else src, dst_ref=slot(a, block),
                send_sem=send_sems.at[a, k], recv_sem=recv_sems.at[a, k], device_id=to, device_id_type=MESH)

        mine = [pltpu.make_async_copy(ins[a], slot(a, me), loc_sems.at[a]) for a in range(n)]
        for cp in mine:
            cp.start()
        first = [copy(a, 0, me, sib, src=ins[a]) for a in range(n)]
        first += [copy(a, 1 + j, me, (*chip, c), src=ins[a]) for j, chip in enumerate(chips) for a in range(n)]
        for cp in first:
            cp.start()
        passed = []
        for j, chip in enumerate(chips):
            for a in range(n):
                copy(a, 1 + j, (*chip, c), me).wait_recv()
                fwd = copy(a, 4 + j, (*chip, c), sib)
                fwd.start()
                passed.append(fwd)
        for a in range(n):
            copy(a, 0, sib, me).wait_recv()
        for j, chip in enumerate(chips):
            for a in range(n):
                copy(a, 4 + j, (*chip, 1 - c), me).wait_recv()
        for cp in first + passed:
            cp.wait_send()
        for cp in mine:
            cp.wait()

    any_spec = pl.BlockSpec(memory_space=pl.ANY)
    return pl.pallas_call(
        body, name=name,
        in_specs=[any_spec] * n, out_specs=[any_spec] * n,
        out_shape=[S((NDEV,) + a.shape, a.dtype) for a in arrs],
        scratch_shapes=[pltpu.SemaphoreType.DMA((n, 7)), pltpu.SemaphoreType.DMA((n, 7)), pltpu.SemaphoreType.DMA((n,))],
    )(*arrs)


def scatter_blocks(gs, name):
    n = len(gs)

    def body(*refs):
        ins, outs = refs[:n], refs[n:2 * n]
        send_sems, recv_sems, loc_sems = refs[2 * n:]
        x, y, c = _coords()
        me = 4 * x + 2 * y + c
        copies = [pltpu.make_async_copy(ins[a].at[me], outs[a].at[me], loc_sems.at[a]) for a in range(n)]
        for m in range(1, NDEV):
            px = 1 - x if m & 4 else x
            py = 1 - y if m & 2 else y
            pc = 1 - c if m & 1 else c
            for a in range(n):
                copies.append(pltpu.make_async_remote_copy(
                    src_ref=ins[a].at[4 * px + 2 * py + pc], dst_ref=outs[a].at[me],
                    send_sem=send_sems.at[a, m - 1], recv_sem=recv_sems.at[a, m - 1],
                    device_id=(px, py, pc), device_id_type=MESH))
        for cp in copies:
            cp.start()
        for cp in copies:
            cp.wait()

    any_spec = pl.BlockSpec(memory_space=pl.ANY)
    return pl.pallas_call(
        body, name=name,
        in_specs=[any_spec] * n, out_specs=[any_spec] * n,
        out_shape=[S(g.shape, g.dtype) for g in gs],
        scratch_shapes=[pltpu.SemaphoreType.DMA((n, 7)), pltpu.SemaphoreType.DMA((n, 7)), pltpu.SemaphoreType.DMA((n,))],
    )(*gs)


def mm_tn(a, b, name, out_dtype=GRAD_DT, tk=256, tn=512):
    T, K = a.shape
    N = b.shape[1]
    tk, tn = min(tk, K), min(tn, N)

    def body(a_ref, b_ref, o_ref):
        o_ref[...] = _dot_tn(a_ref[...], b_ref[...]).astype(o_ref.dtype)

    return pl.pallas_call(
        body, name=name, grid=(K // tk, N // tn),
        in_specs=[pl.BlockSpec((T, tk), lambda i, j: (0, i)), pl.BlockSpec((T, tn), lambda i, j: (0, j))],
        out_specs=pl.BlockSpec((tk, tn), lambda i, j: (i, j)),
        out_shape=S((K, N), out_dtype), compiler_params=_cp(("parallel", "parallel"), VMEM_LIMIT),
    )(a, b)


def in_proj(x, win_g, b_in):
    T = x.shape[0]
    tm = min(1024, T)

    def body(x_ref, w_ref, b_ref, o_ref, xb_ref):
        xb = x_ref[...].astype(bf16)
        o_ref[...] = _dot(xb, w_ref[...]) + b_ref[...]

        @pl.when(pl.program_id(1) == 0)
        def _():
            xb_ref[...] = xb

    return pl.pallas_call(
        body, name="in_proj", grid=(T // tm, NDEV),
        in_specs=[pl.BlockSpec((tm, D), lambda i, k: (i, 0)), pl.BlockSpec((None, D, W), lambda i, k: (k, 0, 0)),
                  pl.BlockSpec((1, W), lambda i, k: (0, k))],
        out_specs=[pl.BlockSpec((tm, W), lambda i, k: (i, k)), pl.BlockSpec((tm, D), lambda i, k: (i, 0))],
        out_shape=[S((T, IN_COLS), f32), S((T, D), bf16)], compiler_params=_cp(("parallel", "arbitrary"), VMEM_LIMIT),
    )(x, win_g, b_in)


def to_perm(a, cb0, name):
    T = a.shape[0]
    L = T // NC

    def body(a_ref, o_ref):
        def step(j, carry):
            for q in range(NC // 8):
                o_ref[pl.ds(pl.multiple_of(j * NC, NC) + 8 * q, 8), :] = a_ref[pl.ds(q * 8 * L + j, 8, stride=L), :]
            return carry

        lax.fori_loop(0, L, step, 0)

    return pl.pallas_call(
        body, name=name, grid=(W // LANE,),
        in_specs=[pl.BlockSpec((T, LANE), lambda k: (0, cb0 + k))], out_specs=pl.BlockSpec((T, LANE), lambda k: (0, k)),
        out_shape=S((T, W), f32), compiler_params=_cp(("parallel",), VMEM_LIMIT),
    )(a)


def from_perm(a, name, out_dtype=f32):
    T = a.shape[0]
    L = T // NC

    def body(a_ref, o_ref):
        def step(i, carry):
            c, jb = i // (L // 16), i % (L // 16)
            t0 = a_ref[pl.ds(jb * 16 * NC + c, 8, stride=NC), :]
            t1 = a_ref[pl.ds((jb * 16 + 8) * NC + c, 8, stride=NC), :]
            o_ref[pl.ds(pl.multiple_of(i * 16, 16), 16), :] = jnp.concatenate([t0, t1], axis=0).astype(out_dtype)
            return carry

        lax.fori_loop(0, T // 16, step, 0)

    return pl.pallas_call(
        body, name=name, grid=(W // LANE,),
        in_specs=[pl.BlockSpec((T, LANE), lambda k: (0, k))], out_specs=pl.BlockSpec((T, LANE), lambda k: (0, k)),
        out_shape=S((T, W), out_dtype), compiler_params=_cp(("parallel",), VMEM_LIMIT),
    )(a)


def ssm_disc(lam_re, lam_im, log_dt):
    def body(lr_ref, li_ref, ldt_ref, lbr_ref, lbi_ref, fr_ref, fi_ref):
        lbr_ref[...], lbi_ref[...], fr_ref[...], fi_ref[...] = _disc(lr_ref[...], li_ref[...], ldt_ref[...])

    return pl.pallas_call(body, name="ssm_disc", out_shape=[S((NG, NP), f32)] * 4)(lam_re, lam_im, log_dt)


def _disc(lr, li, ldt):
    dt = jnp.exp(ldt)
    mag = jnp.exp(lr * dt)
    lbr = mag * jnp.cos(li * dt)
    lbi = mag * jnp.sin(li * dt)
    den = lr * lr + li * li
    nr = lbr - 1.0
    return lbr, lbi, (nr * lr + lbi * li) / den, (lbi * lr - nr * li) / den


def ssm_bbar(fr, fi, br, bi):
    def body(fr_ref, fi_ref, br_ref, bi_ref, or_ref, oi_ref):
        fr_, fi_, br_, bi_ = fr_ref[...], fi_ref[...], br_ref[...], bi_ref[...]
        or_ref[...] = fr_ * br_ - fi_ * bi_
        oi_ref[...] = fr_ * bi_ + fi_ * br_

    return pl.pallas_call(body, name="ssm_bbar", out_shape=[S((NG * NP, GC), f32)] * 2)(fr, fi, br, bi)


def _scan_body(T):
    L = T // NC
    RB = min(512, T)
    nsq = int(round(math.log2(L)))
    assert 2 ** nsq == L and T % RB == 0 and L % 16 == 0

    def rows(i):
        return pl.ds(pl.multiple_of(i * RB, RB), RB)

    def tile(j):
        return pl.ds(pl.multiple_of(j * NC, NC), NC)

    def forward_states(u_ref, wb_ref, lbr_ref, lbi_ref, sre, sim, ere, eim):
        def bproj(i, carry):
            bu = _dot(u_ref[rows(i), :].astype(bf16), wb_ref[...])
            sre[rows(i), :] = bu[:, :SW]
            sim[rows(i), :] = bu[:, SW:]
            return carry

        lax.fori_loop(0, T // RB, bproj, 0)
        for lb in range(SW // LANE):
            ls = slice(lb * LANE, (lb + 1) * LANE)
            ar = jnp.broadcast_to(lbr_ref[:, ls], (NC, LANE))
            ai = jnp.broadcast_to(lbi_ref[:, ls], (NC, LANE))

            def step(j, carry):
                xr, xi = carry
                nr = ar * xr - ai * xi + sre[tile(j), ls]
                ni = ar * xi + ai * xr + sim[tile(j), ls]
                sre[tile(j), ls] = nr
                sim[tile(j), ls] = ni
                return nr, ni

            zero = jnp.zeros((NC, LANE), f32)
            lax.fori_loop(0, L, step, (zero, zero))
            pr, pi = lbr_ref[:, ls], lbi_ref[:, ls]
            for _ in range(nsq):
                pr, pi = pr * pr - pi * pi, 2.0 * pr * pi
            er = jnp.zeros((1, LANE), f32)
            ei = er
            ere[0:1, ls] = er
            eim[0:1, ls] = ei
            base = (L - 1) * NC
            for c in range(1, NC):
                lr_ = sre[base + c - 1:base + c, ls]
                li_ = sim[base + c - 1:base + c, ls]
                er, ei = lr_ + pr * er - pi * ei, li_ + pr * ei + pi * er
                ere[c:c + 1, ls] = er
                eim[c:c + 1, ls] = ei
            e_r, e_i = ere[:, ls], eim[:, ls]

            def fix(j, carry):
                pwr, pwi = carry
                sre[tile(j), ls] += pwr * e_r - pwi * e_i
                sim[tile(j), ls] += pwr * e_i + pwi * e_r
                return pwr * ar - pwi * ai, pwr * ai + pwi * ar

            lax.fori_loop(0, L, fix, (ar, ai))

    return L, RB, nsq, rows, tile, forward_states


def ssm_fwd(u_p, wb, wc, lbr, lbi, dsk):
    T = u_p.shape[0]
    L, RB, nsq, rows, tile, forward_states = _scan_body(T)

    def body(u_ref, wb_ref, wc_ref, lbr_ref, lbi_ref, d_ref, y_ref, sre, sim, ere, eim):
        forward_states(u_ref, wb_ref, lbr_ref, lbi_ref, sre, sim, ere, eim)

        def cproj(i, carry):
            y = _dot(sre[rows(i), :].astype(bf16), wc_ref[0:SW, :]) + _dot(sim[rows(i), :].astype(bf16), wc_ref[SW:, :])
            y_ref[rows(i), :] = y + d_ref[...] * u_ref[rows(i), :]
            return carry

        lax.fori_loop(0, T // RB, cproj, 0)

    slab = pl.BlockSpec((T, LANE), lambda k: (0, k))
    return pl.pallas_call(
        body, name="ssm_fwd", grid=(W // LANE,),
        in_specs=[slab, pl.BlockSpec((None, LANE, 2 * SW), lambda k: (k, 0, 0)),
                  pl.BlockSpec((None, 2 * SW, LANE), lambda k: (k, 0, 0)),
                  pl.BlockSpec((None, 1, SW), lambda k: (k, 0, 0)), pl.BlockSpec((None, 1, SW), lambda k: (k, 0, 0)),
                  pl.BlockSpec((None, 1, LANE), lambda k: (k, 0, 0))],
        out_specs=slab, out_shape=S((T, W), f32),
        scratch_shapes=[pltpu.VMEM((T, SW), f32), pltpu.VMEM((T, SW), f32), pltpu.VMEM((NC, SW), f32), pltpu.VMEM((NC, SW), f32)],
        compiler_params=_cp(("parallel",), VMEM_LIMIT),
    )(u_p, wb, wc, lbr, lbi, dsk)


def ssm_bwd(u_p, dy_p, wb, wbT, wcT, lbr, lbi, dsk):
    T = u_p.shape[0]
    L, RB, nsq, rows, tile, forward_states = _scan_body(T)

    def body(u_ref, dy_ref, wb_ref, wbT_ref, wcT_ref, lbr_ref, lbi_ref, d_ref,
             du_ref, dwb_ref, dwc_ref, dlr_ref, dli_ref, dd_ref, sre, sim, gre, gim, ere, eim):
        forward_states(u_ref, wb_ref, lbr_ref, lbi_ref, sre, sim, ere, eim)

        def dstate(i, carry):
            g = _dot(dy_ref[rows(i), :].astype(bf16), wcT_ref[...])
            gre[rows(i), :] = g[:, :SW]
            gim[rows(i), :] = g[:, SW:]
            return carry

        lax.fori_loop(0, T // RB, dstate, 0)
        row = lax.broadcasted_iota(jnp.int32, (NC, LANE), 0)
        for lb in range(SW // LANE):
            ls = slice(lb * LANE, (lb + 1) * LANE)
            ar = jnp.broadcast_to(lbr_ref[:, ls], (NC, LANE))
            ai = jnp.broadcast_to(lbi_ref[:, ls], (NC, LANE))

            def step(i, carry):
                gr, gi = carry
                j = L - 1 - i
                nr = ar * gr + ai * gi + gre[tile(j), ls]
                ni = ar * gi - ai * gr + gim[tile(j), ls]
                gre[tile(j), ls] = nr
                gim[tile(j), ls] = ni
                return nr, ni

            zero = jnp.zeros((NC, LANE), f32)
            lax.fori_loop(0, L, step, (zero, zero))
            pr, pi = lbr_ref[:, ls], -lbi_ref[:, ls]
            for _ in range(nsq):
                pr, pi = pr * pr - pi * pi, 2.0 * pr * pi
            er = jnp.zeros((1, LANE), f32)
            ei = er
            ere[NC - 1:NC, ls] = er
            eim[NC - 1:NC, ls] = ei
            for c in range(NC - 2, -1, -1):
                lr_ = gre[c + 1:c + 2, ls]
                li_ = gim[c + 1:c + 2, ls]
                er, ei = lr_ + pr * er - pi * ei, li_ + pr * ei + pi * er
                ere[c:c + 1, ls] = er
                eim[c:c + 1, ls] = ei
            e_r, e_i = ere[:, ls], eim[:, ls]

            def fixed(j, pwr, pwi):
                gr = gre[tile(j), ls] + pwr * e_r - pwi * e_i
                gi = gim[tile(j), ls] + pwr * e_i + pwi * e_r
                gre[tile(j), ls] = gr
                gim[tile(j), ls] = gi
                return gr, gi

            def fix(i, carry):
                pwr, pwi, accr, acci = carry
                j = L - 1 - i
                gr, gi = fixed(j, pwr, pwi)
                xr, xi = sre[tile(j - 1), ls], sim[tile(j - 1), ls]
                return (pwr * ar + pwi * ai, pwi * ar - pwr * ai,
                        accr + gr * xr + gi * xi, acci + gi * xr - gr * xi)

            pwr, pwi, accr, acci = lax.fori_loop(0, L - 1, fix, (ar, -ai, zero, zero))
            gr, gi = fixed(0, pwr, pwi)
            xr = jnp.where(row == 0, 0.0, pltpu.roll(sre[tile(L - 1), ls], 1, axis=0))
            xi = jnp.where(row == 0, 0.0, pltpu.roll(sim[tile(L - 1), ls], 1, axis=0))
            accr = accr + gr * xr + gi * xi
            acci = acci + gi * xr - gr * xi
            dlr_ref[:, ls] = jnp.sum(accr, axis=0, keepdims=True)
            dli_ref[:, ls] = jnp.sum(acci, axis=0, keepdims=True)

        dwb_ref[...] = jnp.zeros_like(dwb_ref)
        dwc_ref[...] = jnp.zeros_like(dwc_ref)
        dd_ref[...] = jnp.zeros_like(dd_ref)

        def finish(i, carry):
            u32, dy32 = u_ref[rows(i), :], dy_ref[rows(i), :]
            ub, dyb = u32.astype(bf16), dy32.astype(bf16)
            gr, gi = gre[rows(i), :].astype(bf16), gim[rows(i), :].astype(bf16)
            du_ref[rows(i), :] = _dot(gr, wbT_ref[0:SW, :]) + _dot(gi, wbT_ref[SW:, :]) + dy32 * d_ref[...]
            dwb_ref[:, 0:SW] += _dot_tn(ub, gr)
            dwb_ref[:, SW:] += _dot_tn(ub, gi)
            dwc_ref[:, 0:SW] += _dot_tn(dyb, sre[rows(i), :].astype(bf16))
            dwc_ref[:, SW:] += _dot_tn(dyb, sim[rows(i), :].astype(bf16))
            dd_ref[...] += jnp.sum(dy32 * u32, axis=0, keepdims=True)
            return carry

        lax.fori_loop(0, T // RB, finish, 0)

    slab = pl.BlockSpec((T, LANE), lambda k: (0, k))
    wide = pl.BlockSpec((None, LANE, 2 * SW), lambda k: (k, 0, 0))
    tall = pl.BlockSpec((None, 2 * SW, LANE), lambda k: (k, 0, 0))
    vec = pl.BlockSpec((None, 1, SW), lambda k: (k, 0, 0))
    vecd = pl.BlockSpec((None, 1, LANE), lambda k: (k, 0, 0))
    nslab = W // LANE
    return pl.pallas_call(
        body, name="ssm_bwd", grid=(nslab,),
        in_specs=[slab, slab, wide, tall, wide, vec, vec, vecd],
        out_specs=[slab, wide, wide, vec, vec, vecd],
        out_shape=[S((T, W), f32), S((nslab, LANE, 2 * SW), f32), S((nslab, LANE, 2 * SW), f32),
                   S((nslab, 1, SW), f32), S((nslab, 1, SW), f32), S((nslab, 1, LANE), f32)],
        scratch_shapes=[pltpu.VMEM((T, SW), f32)] * 4 + [pltpu.VMEM((NC, SW), f32)] * 2,
        compiler_params=_cp(("parallel",), VMEM_LIMIT),
    )(u_p, dy_p, wb, wbT, wcT, lbr, lbi, dsk)


def glu_fwd(yn, glu_w, glu_b):
    T = yn.shape[0]
    tm = min(512, T)

    def body(y_ref, w_ref, b_ref, o_ref):
        g = _gelu(y_ref[...])
        o_ref[...] = (g * _sigmoid(_dot(g.astype(bf16), w_ref[...]) + b_ref[...])).astype(bf16)

    return pl.pallas_call(
        body, name="glu_fwd", grid=(T // tm,),
        in_specs=[pl.BlockSpec((tm, W), lambda i: (i, 0)), pl.BlockSpec((W, W), lambda i: (0, 0)), pl.BlockSpec((1, W), lambda i: (0, 0))],
        out_specs=pl.BlockSpec((tm, W), lambda i: (i, 0)), out_shape=S((T, W), bf16), compiler_params=_cp(("parallel",)),
    )(yn, glu_w, glu_b)


def _shift_rows(cur, prev8, k):
    return pltpu.roll(jnp.concatenate([prev8, cur], axis=0), k, axis=0)[8:]


def _lift_rows(cur, next8, k):
    n = cur.shape[0]
    return pltpu.roll(jnp.concatenate([cur, next8], axis=0), n + 8 - k, axis=0)[:n]


def conv_fwd(proj, conv_w):
    T = proj.shape[0]
    RB = min(512, T)

    def body(h_ref, c_ref, b_ref, w_ref, o_ref):
        w0, w1, w2 = w_ref[0:1, :], w_ref[1:2, :], w_ref[2:3, :]

        def blk(i, carry):
            r0 = pl.multiple_of(i * RB, RB)
            rs = pl.ds(r0, RB)
            ch = c_ref[rs, :] * h_ref[rs, :]
            pr = pl.ds(jnp.maximum(r0 - 8, 0), 8)
            prev = jnp.where(i > 0, c_ref[pr, :] * h_ref[pr, :], 0.0)
            z = w2 * ch + w1 * _shift_rows(ch, prev, 1) + w0 * _shift_rows(ch, prev, 2)
            o_ref[rs, :] = (b_ref[rs, :] * z).astype(bf16)
            return carry

        lax.fori_loop(0, T // RB, blk, 0)

    nb = W // LANE
    return pl.pallas_call(
        body, name="conv_fwd", grid=(nb,),
        in_specs=[pl.BlockSpec((T, LANE), lambda k: (0, nb + k)), pl.BlockSpec((T, LANE), lambda k: (0, 2 * nb + k)),
                  pl.BlockSpec((T, LANE), lambda k: (0, 3 * nb + k)), pl.BlockSpec((3, LANE), lambda k: (0, k))],
        out_specs=pl.BlockSpec((T, LANE), lambda k: (0, k)), out_shape=S((T, W), bf16),
        compiler_params=_cp(("parallel",), VMEM_LIMIT),
    )(proj, proj, proj, conv_w)


def merge_fwd(ya, yb, wso, wco, proj):
    T = ya.shape[0]
    tm = min(1024, T)

    def body(ya_ref, yb_ref, wa_ref, wb_ref, ga_ref, gb_ref, o_ref):
        o_ref[...] = (_sigmoid(ga_ref[...]) * _dot(ya_ref[...], wa_ref[...])
                      + _sigmoid(gb_ref[...]) * _dot(yb_ref[...], wb_ref[...])).astype(bf16)

    act = pl.BlockSpec((tm, W), lambda i, k: (i, 0))
    wsp = pl.BlockSpec((None, W, LANE), lambda i, k: (k, 0, 0))
    return pl.pallas_call(
        body, name="merge_fwd", grid=(T // tm, NDEV),
        in_specs=[act, act, wsp, wsp, pl.BlockSpec((tm, LANE), lambda i, k: (i, 16 + k)), pl.BlockSpec((tm, LANE), lambda i, k: (i, 24 + k))],
        out_specs=pl.BlockSpec((tm, LANE), lambda i, k: (i, k)), out_shape=S((T, D), bf16),
        compiler_params=_cp(("parallel", "arbitrary"), VMEM_LIMIT),
    )(ya, yb, wso, wco, proj, proj)


def mix_ln1(merged, w_o, x, g1, b1):
    T = x.shape[0]
    tm = min(512, T)

    def body(m_ref, w_ref, x_ref, g_ref, b_ref, r_ref, x1_ref):
        r = ALPHA * x_ref[...] + _dot(m_ref[...], w_ref[...])
        r_ref[...] = r
        xhat, _ = _ln_stats(r)
        x1_ref[...] = (xhat * g_ref[...] + b_ref[...]).astype(bf16)

    row = pl.BlockSpec((tm, D), lambda i: (i, 0))
    vec = pl.BlockSpec((1, D), lambda i: (0, 0))
    return pl.pallas_call(
        body, name="mix_ln1", grid=(T // tm,),
        in_specs=[row, pl.BlockSpec((D, D), lambda i: (0, 0)), row, vec, vec],
        out_specs=[row, row], out_shape=[S((T, D), f32), S((T, D), bf16)], compiler_params=_cp(("parallel",), VMEM_LIMIT),
    )(merged, w_o, x, g1, b1)


FT = 256


def gate_up(x1b, wgT, wuT):
    T = x1b.shape[0]
    tm = min(1024, T)

    def body(x_ref, wg_ref, wu_ref, g_ref, u_ref, h_ref):
        g = _dot_nt(x_ref[...], wg_ref[...])
        u = _dot_nt(x_ref[...], wu_ref[...])
        g_ref[...] = g.astype(bf16)
        u_ref[...] = u.astype(bf16)
        h_ref[...] = (g * _sigmoid(g) * u).astype(bf16)

    wsp = pl.BlockSpec((FT, D), lambda i, n: (n, 0))
    osp = pl.BlockSpec((tm, FT), lambda i, n: (i, n))
    return pl.pallas_call(
        body, name="gate_up", grid=(T // tm, F // FT),
        in_specs=[pl.BlockSpec((tm, D), lambda i, n: (i, 0)), wsp, wsp],
        out_specs=[osp, osp, osp], out_shape=[S((T, F), bf16)] * 3, compiler_params=_cp(("parallel", "arbitrary"), VMEM_LIMIT),
    )(x1b, wgT, wuT)


def down_loss(hid, w_down, r1, g1, b1, g2, b2, target):
    T = hid.shape[0]
    tm = min(256, T)

    def body(h_ref, w_ref, r1_ref, g1_ref, b1_ref, g2_ref, b2_ref, t_ref, dr_ref, drb_ref, loss_ref, dg_ref, db_ref):
        @pl.when(pl.program_id(0) == 0)
        def _():
            loss_ref[...] = jnp.zeros_like(loss_ref)
            dg_ref[...] = jnp.zeros_like(dg_ref)
            db_ref[...] = jnp.zeros_like(db_ref)

        xh1, _ = _ln_stats(r1_ref[...])
        x1 = xh1 * g1_ref[...] + b1_ref[...]
        r2 = ALPHA * x1 + _dot(h_ref[...], w_ref[...])
        xh2, rstd2 = _ln_stats(r2)
        err = xh2 * g2_ref[...] + b2_ref[...] - t_ref[...]
        loss_ref[...] += jnp.sum(jnp.mean(err * err, axis=-1, keepdims=True), axis=0, keepdims=True)
        dy = err * (1.0 / D)
        dg_ref[...] += jnp.sum(dy * xh2, axis=0, keepdims=True)
        db_ref[...] += jnp.sum(dy, axis=0, keepdims=True)
        dr = _ln_bwd(dy, xh2, rstd2, g2_ref[...])
        dr_ref[...] = dr
        drb_ref[...] = dr.astype(bf16)

    row = pl.BlockSpec((tm, D), lambda i: (i, 0))
    vec = pl.BlockSpec((1, D), lambda i: (0, 0))
    return pl.pallas_call(
        body, name="down_loss", grid=(T // tm,),
        in_specs=[pl.BlockSpec((tm, F), lambda i: (i, 0)), pl.BlockSpec((F, D), lambda i: (0, 0)), row, vec, vec, vec, vec, row],
        out_specs=[row, row, pl.BlockSpec((1, 1), lambda i: (0, 0)), vec, vec],
        out_shape=[S((T, D), f32), S((T, D), bf16), S((1, 1), f32), S((1, D), f32), S((1, D), f32)],
        compiler_params=_cp(("arbitrary",), VMEM_LIMIT),
    )(hid, w_down, r1, g1, b1, g2, b2, target)


def ffn_bwd_act(dffn, w_down, gate, up):
    T = dffn.shape[0]
    tm = min(1024, T)

    def body(d_ref, w_ref, g_ref, u_ref, dg_ref, du_ref):
        dh = _dot_nt(d_ref[...], w_ref[...])
        g, u = g_ref[...].astype(f32), u_ref[...].astype(f32)
        sg = _sigmoid(g)
        du_ref[...] = (dh * g * sg).astype(bf16)
        dg_ref[...] = (dh * u * sg * (1.0 + g * (1.0 - sg))).astype(bf16)

    osp = pl.BlockSpec((tm, FT), lambda i, n: (i, n))
    return pl.pallas_call(
        body, name="ffn_bwd_act", grid=(T // tm, F // FT),
        in_specs=[pl.BlockSpec((tm, D), lambda i, n: (i, 0)), pl.BlockSpec((FT, D), lambda i, n: (n, 0)), osp, osp],
        out_specs=[osp, osp], out_shape=[S((T, F), bf16)] * 2, compiler_params=_cp(("parallel", "arbitrary"), VMEM_LIMIT),
    )(dffn, w_down, gate, up)


def ffn_bwd_x(dgate, dup, wgT, wuT, dr2, r1, g1):
    T = dr2.shape[0]
    tm = min(256, T)

    def body(dg_ref, du_ref, wg_ref, wu_ref, dr2_ref, r1_ref, g1_ref, dr_ref, drb_ref, dgam_ref, dbet_ref):
        @pl.when(pl.program_id(0) == 0)
        def _():
            dgam_ref[...] = jnp.zeros_like(dgam_ref)
            dbet_ref[...] = jnp.zeros_like(dbet_ref)

        dx1 = ALPHA * dr2_ref[...] + _dot(dg_ref[...], wg_ref[...]) + _dot(du_ref[...], wu_ref[...])
        xh, rstd = _ln_stats(r1_ref[...])
        dgam_ref[...] += jnp.sum(dx1 * xh, axis=0, keepdims=True)
        dbet_ref[...] += jnp.sum(dx1, axis=0, keepdims=True)
        dr = _ln_bwd(dx1, xh, rstd, g1_ref[...])
        dr_ref[...] = dr
        drb_ref[...] = dr.astype(bf16)

    row = pl.BlockSpec((tm, D), lambda i: (i, 0))
    wide = pl.BlockSpec((tm, F), lambda i: (i, 0))
    wsp = pl.BlockSpec((F, D), lambda i: (0, 0))
    vec = pl.BlockSpec((1, D), lambda i: (0, 0))
    return pl.pallas_call(
        body, name="ffn_bwd_x", grid=(T // tm,),
        in_specs=[wide, wide, wsp, wsp, row, row, vec],
        out_specs=[row, row, vec, vec], out_shape=[S((T, D), f32), S((T, D), bf16), S((1, D), f32), S((1, D), f32)],
        compiler_params=_cp(("arbitrary",), VMEM_LIMIT),
    )(dgate, dup, wgT, wuT, dr2, r1, g1)


def merge_bwd(dmix, w_o, ya, yb, wso, wco, proj):
    T = dmix.shape[0]
    tm = min(1024, T)

    def body(dm_ref, wo_ref, ya_ref, yb_ref, wa_ref, wb_ref, ga_ref, gb_ref, dya_ref, dyb_ref, dga_ref, dgb_ref):
        dmer = _dot_nt(dm_ref[...], wo_ref[...])
        sa, sb = _sigmoid(ga_ref[...]), _sigmoid(gb_ref[...])
        dya_ref[...] = (dmer * sa).astype(bf16)
        dyb_ref[...] = (dmer * sb).astype(bf16)
        dga_ref[...] = (dmer * _dot(ya_ref[...], wa_ref[...]) * sa * (1.0 - sa)).astype(bf16)
        dgb_ref[...] = (dmer * _dot(yb_ref[...], wb_ref[...]) * sb * (1.0 - sb)).astype(bf16)

    act = pl.BlockSpec((tm, W), lambda i, k: (i, 0))
    wsp = pl.BlockSpec((None, W, LANE), lambda i, k: (k, 0, 0))
    osp = pl.BlockSpec((tm, LANE), lambda i, k: (i, k))
    return pl.pallas_call(
        body, name="merge_bwd", grid=(T // tm, NDEV),
        in_specs=[pl.BlockSpec((tm, D), lambda i, k: (i, 0)), pl.BlockSpec((LANE, D), lambda i, k: (k, 0)), act, act, wsp, wsp,
                  pl.BlockSpec((tm, LANE), lambda i, k: (i, 16 + k)), pl.BlockSpec((tm, LANE), lambda i, k: (i, 24 + k))],
        out_specs=[osp, osp, osp, osp], out_shape=[S((T, D), bf16)] * 4,
        compiler_params=_cp(("parallel", "arbitrary"), VMEM_LIMIT),
    )(dmix, w_o, ya, yb, wso, wco, proj, proj)


def branch_bwd_x(dY, wblk, name):
    T = dY.shape[0]
    tm = min(1024, T)

    def body(d_ref, w_ref, o_ref):
        @pl.when(pl.program_id(1) == 0)
        def _():
            o_ref[...] = jnp.zeros_like(o_ref)

        o_ref[...] += _dot_nt(d_ref[...], w_ref[...])

    return pl.pallas_call(
        body, name=name, grid=(T // tm, NDEV),
        in_specs=[pl.BlockSpec((tm, LANE), lambda i, k: (i, k)), pl.BlockSpec((None, W, LANE), lambda i, k: (k, 0, 0))],
        out_specs=pl.BlockSpec((tm, W), lambda i, k: (i, 0)), out_shape=S((T, W), f32),
        compiler_params=_cp(("parallel", "arbitrary"), VMEM_LIMIT),
    )(dY, wblk)


def branch_bwd_w(act, dY, name):
    T = act.shape[0]

    def body(a_ref, d_ref, o_ref):
        o_ref[...] = _dot_tn(a_ref[...], d_ref[...]).astype(o_ref.dtype)

    return pl.pallas_call(
        body, name=name, grid=(NDEV,),
        in_specs=[pl.BlockSpec((T, W), lambda k: (0, 0)), pl.BlockSpec((T, LANE), lambda k: (0, k))],
        out_specs=pl.BlockSpec((None, W, LANE), lambda k: (k, 0, 0)), out_shape=S((NDEV, W, LANE), GRAD_DT),
        compiler_params=_cp(("parallel",), VMEM_LIMIT),
    )(act, dY)


def glu_bwd(yn, dya, glu_w, glu_b):
    T = yn.shape[0]
    tm = min(512, T)

    def body(y_ref, d_ref, w_ref, b_ref, dy_ref, dsp_ref, g_ref, db_ref):
        @pl.when(pl.program_id(0) == 0)
        def _():
            db_ref[...] = jnp.zeros_like(db_ref)

        y, dya_ = y_ref[...], d_ref[...]
        g = _gelu(y)
        gb = g.astype(bf16)
        s = _sigmoid(_dot(gb, w_ref[...]) + b_ref[...])
        dsp = dya_ * g * s * (1.0 - s)
        dspb = dsp.astype(bf16)
        dg = dya_ * s + _dot_nt(dspb, w_ref[...])
        dy_ref[...] = dg * _gelu_grad(y)
        dsp_ref[...] = dspb
        g_ref[...] = gb
        db_ref[...] += jnp.sum(dsp, axis=0, keepdims=True)

    row = pl.BlockSpec((tm, W), lambda i: (i, 0))
    vec = pl.BlockSpec((1, W), lambda i: (0, 0))
    return pl.pallas_call(
        body, name="glu_bwd", grid=(T // tm,),
        in_specs=[row, row, pl.BlockSpec((W, W), lambda i: (0, 0)), vec],
        out_specs=[row, row, row, vec], out_shape=[S((T, W), f32), S((T, W), bf16), S((T, W), bf16), S((1, W), f32)],
        compiler_params=_cp(("arbitrary",)),
    )(yn, dya, glu_w, glu_b)


def conv_bwd(proj, dyb, conv_w):
    T = proj.shape[0]
    RB = min(512, T)
    nrb = T // RB

    def body(h_ref, c_ref, b_ref, d_ref, w_ref, dh_ref, dc_ref, db_ref, dw_ref):
        w0, w1, w2 = w_ref[0:1, :], w_ref[1:2, :], w_ref[2:3, :]

        def blk(i, carry):
            a0, a1, a2 = carry
            r0 = pl.multiple_of(i * RB, RB)
            rs = pl.ds(r0, RB)
            h, cg, bg, dyb_ = h_ref[rs, :], c_ref[rs, :], b_ref[rs, :], d_ref[rs, :]
            ch = cg * h
            pr = pl.ds(jnp.maximum(r0 - 8, 0), 8)
            prev = jnp.where(i > 0, c_ref[pr, :] * h_ref[pr, :], 0.0)
            ch1, ch2 = _shift_rows(ch, prev, 1), _shift_rows(ch, prev, 2)
            db_ref[rs, :] = (dyb_ * (w2 * ch + w1 * ch1 + w0 * ch2)).astype(bf16)
            dz = dyb_ * bg
            nx = pl.ds(jnp.minimum(r0 + RB, T - 8), 8)
            nxt = jnp.where(i < nrb - 1, d_ref[nx, :] * b_ref[nx, :], 0.0)
            dch = w2 * dz + w1 * _lift_rows(dz, nxt, 1) + w0 * _lift_rows(dz, nxt, 2)
            dc_ref[rs, :] = (dch * h).astype(bf16)
            dh_ref[rs, :] = (dch * cg).astype(bf16)
            return (a0 + jnp.sum(dz * ch2, axis=0, keepdims=True), a1 + jnp.sum(dz * ch1, axis=0, keepdims=True),
                    a2 + jnp.sum(dz * ch, axis=0, keepdims=True))

        zero = jnp.zeros((1, LANE), f32)
        a0, a1, a2 = lax.fori_loop(0, nrb, blk, (zero, zero, zero))
        dw_ref[0:1, :] = a0
        dw_ref[1:2, :] = a1
        dw_ref[2:3, :] = a2

    nb = W // LANE
    slab = pl.BlockSpec((T, LANE), lambda k: (0, k))
    return pl.pallas_call(
        body, name="conv_bwd", grid=(nb,),
        in_specs=[pl.BlockSpec((T, LANE), lambda k: (0, nb + k)), pl.BlockSpec((T, LANE), lambda k: (0, 2 * nb + k)),
                  pl.BlockSpec((T, LANE), lambda k: (0, 3 * nb + k)), slab, pl.BlockSpec((3, LANE), lambda k: (0, k))],
        out_specs=[slab, slab, slab, pl.BlockSpec((3, LANE), lambda k: (0, k))],
        out_shape=[S((T, W), bf16)] * 3 + [S((3, W), f32)], compiler_params=_cp(("parallel",), VMEM_LIMIT),
    )(proj, proj, proj, dyb, conv_w)


def in_proj_bwd_x(parts, win_g, dr1):
    T = dr1.shape[0]
    tm = min(512, T)

    def body(*refs):
        p_refs, w_ref, dr_ref, gx_ref, db_ref = refs[:NDEV], refs[NDEV], refs[NDEV + 1], refs[NDEV + 2], refs[NDEV + 3]

        @pl.when(pl.program_id(0) == 0)
        def _():
            db_ref[...] = jnp.zeros_like(db_ref)

        acc = ALPHA * dr_ref[...]
        for k in range(NDEV):
            p = p_refs[k][...]
            acc += _dot_nt(p, w_ref[k])
            db_ref[:, k * W:(k + 1) * W] += jnp.sum(p.astype(f32), axis=0, keepdims=True)
        gx_ref[...] = acc

    row = pl.BlockSpec((tm, D), lambda i: (i, 0))
    p_specs = [pl.BlockSpec((tm, W), (lambda i, cb=cb: (i, cb))) for _, cb in parts]
    return pl.pallas_call(
        body, name="in_proj_bwd_x", grid=(T // tm,),
        in_specs=p_specs + [pl.BlockSpec((NDEV, D, W), lambda i: (0, 0, 0)), row],
        out_specs=[row, pl.BlockSpec((1, IN_COLS), lambda i: (0, 0))],
        out_shape=[S((T, D), f32), S((1, IN_COLS), f32)], compiler_params=_cp(("arbitrary",), VMEM_LIMIT),
    )(*[a for a, _ in parts], win_g, dr1)


def ssm_param_bwd_b(fr, fi, br, bi, dbbr, dbbi):
    def body(fr_ref, fi_ref, br_ref, bi_ref, dr_ref, di_ref, dbr_ref, dbi_ref, dfr_ref, dfi_ref):
        fr_, fi_, br_, bi_, dr, di = fr_ref[...], fi_ref[...], br_ref[...], bi_ref[...], dr_ref[...], di_ref[...]
        dbr_ref[...] = fr_ * dr + fi_ * di
        dbi_ref[...] = fr_ * di - fi_ * dr
        dfr_ref[...] = jnp.sum(dr * br_ + di * bi_, axis=-1, keepdims=True)
        dfi_ref[...] = jnp.sum(di * br_ - dr * bi_, axis=-1, keepdims=True)

    n = NG * NP
    return pl.pallas_call(body, name="ssm_param_bwd_b",
                          out_shape=[S((n, GC), f32), S((n, GC), f32), S((n, 1), f32), S((n, 1), f32)])(fr, fi, br, bi, dbbr, dbbi)


def ssm_param_bwd_lam(lam_re, lam_im, log_dt, dlbr, dlbi, dfr, dfi):
    def body(lr_ref, li_ref, ldt_ref, a_ref, b_ref, c_ref, d_ref, dlr_ref, dli_ref, dldt_ref):
        _, vjp = jax.vjp(_disc, lr_ref[...], li_ref[...], ldt_ref[...])
        dlr_ref[...], dli_ref[...], dldt_ref[...] = vjp((a_ref[...], b_ref[...], c_ref[...], d_ref[...]))

    return pl.pallas_call(body, name="ssm_param_bwd_lam", out_shape=[S((NG, NP), f32), S((NG, NP), f32), S((NG, 1), f32)])(
        lam_re, lam_im, log_dt, dlbr, dlbi, dfr, dfi)


def _adam(w, g, m, v):
    m = ADAM_B1 * m + (1.0 - ADAM_B1) * g
    v = ADAM_B2 * v + (1.0 - ADAM_B2) * (g * g)
    m_hat = m / (1.0 - ADAM_B1 ** ADAM_STEP)
    v_hat = v / (1.0 - ADAM_B2 ** ADAM_STEP)
    return -ADAM_LR * (m_hat / (jnp.sqrt(v_hat) + ADAM_EPS) + ADAM_WD * w), m, v


def adam_update(w, m, v, contrib, name, rows_per_block=None):
    R, C = w.shape
    n = contrib.shape[0]
    tr = rows_per_block or R
    tr = min(tr, R)

    def body(w_ref, m_ref, v_ref, c_ref, g_ref, d_ref, nm_ref, nv_ref):
        g = c_ref[0].astype(f32)
        for k in range(1, n):
            g = g + c_ref[k].astype(f32)
        g_ref[...] = g
        d_ref[...], nm_ref[...], nv_ref[...] = _adam(w_ref[...], g, m_ref[...], v_ref[...])

    blk = pl.BlockSpec((tr, C), lambda i: (i, 0))
    return pl.pallas_call(
        body, name=name, grid=(R // tr,),
        in_specs=[blk, blk, blk, pl.BlockSpec((n, tr, C), lambda i: (0, i, 0))],
        out_specs=[blk] * 4, out_shape=[S((R, C), f32)] * 4, compiler_params=_cp(("parallel",), VMEM_LIMIT),
    )(w, m, v, contrib)


def sum_blocks(contrib, name, rows_per_block):
    n, R, C = contrib.shape
    tr = min(rows_per_block, R)

    def body(c_ref, o_ref):
        g = c_ref[0].astype(f32)
        for k in range(1, n):
            g = g + c_ref[k].astype(f32)
        o_ref[...] = g

    return pl.pallas_call(
        body, name=name, grid=(R // tr,),
        in_specs=[pl.BlockSpec((n, tr, C), lambda i: (0, i, 0))], out_specs=pl.BlockSpec((tr, C), lambda i: (i, 0)),
        out_shape=S((R, C), f32), compiler_params=_cp(("parallel",), VMEM_LIMIT),
    )(contrib)


def _block_diag(wgt):
    eye = jnp.eye(8, dtype=wgt.dtype)
    out = wgt[:, :, :, None, :] * eye[None, :, None, :, None]
    return out.reshape(4, 8 * wgt.shape[2], 8 * wgt.shape[3])


def _diag_blocks(m, a, b):
    m = m.reshape(4, 8, a, 8, b)
    idx = jnp.arange(8)
    return m[:, idx, :, idx, :].transpose(1, 0, 2, 3)


_SMALL = (("b_in", (IN_COLS,)), ("ssm_lambda_re", (NG, NP)), ("ssm_lambda_im", (NG, NP)), ("ssm_log_dt", (NG,)),
          ("ssm_b_re", (NG, NP, GC)), ("ssm_b_im", (NG, NP, GC)), ("ssm_c_re", (NG, GC, NP)), ("ssm_c_im", (NG, GC, NP)),
          ("ssm_d", (W,)), ("glu_b", (W,)), ("ln1_g", (D,)), ("ln1_b", (D,)), ("ln2_g", (D,)), ("ln2_b", (D,)))
_SMALL_N = sum(math.prod(s) for _, s in _SMALL)
_CONV_N = 3 * W
_PACK_ROWS = -(-(_SMALL_N + _CONV_N) // (8 * LANE)) * 8


def _pack(vals, extra=None):
    flat = [vals[n].reshape(-1) for n, _ in _SMALL]
    flat.append(jnp.zeros((_CONV_N,), f32) if extra is None else extra.reshape(-1))
    used = _SMALL_N + _CONV_N
    flat.append(jnp.zeros((_PACK_ROWS * LANE - used,), f32))
    return jnp.concatenate(flat).reshape(_PACK_ROWS, LANE)


def _unpack(packed):
    flat = packed.reshape(-1)
    out, off = {}, 0
    for n, s in _SMALL:
        k = math.prod(s)
        out[n] = flat[off:off + k].reshape((1,) + s)
        off += k
    return out, flat[off:off + _CONV_N].reshape(3, W)


def kernel(x, w_in, b_in, ssm_lambda_re, ssm_lambda_im, ssm_log_dt, ssm_b_re, ssm_b_im, ssm_c_re, ssm_c_im, ssm_d, glu_w, glu_b, w_ssm_out, conv_w, w_conv_out, w_o, ln1_g, ln1_b, w_gate, w_up, w_down, ln2_g, ln2_b, loss_target, m_w_in, m_b_in, m_ssm_lambda_re, m_ssm_lambda_im, m_ssm_log_dt, m_ssm_b_re, m_ssm_b_im, m_ssm_c_re, m_ssm_c_im, m_ssm_d, m_glu_w, m_glu_b, m_w_ssm_out, m_conv_w, m_w_conv_out, m_w_o, m_ln1_g, m_ln1_b, m_w_gate, m_w_up, m_w_down, m_ln2_g, m_ln2_b, v_w_in, v_b_in, v_ssm_lambda_re, v_ssm_lambda_im, v_ssm_log_dt, v_ssm_b_re, v_ssm_b_im, v_ssm_c_re, v_ssm_c_im, v_ssm_d, v_glu_w, v_glu_b, v_w_ssm_out, v_conv_w, v_w_conv_out, v_w_o, v_ln1_g, v_ln1_b, v_w_gate, v_w_up, v_w_down, v_ln2_g, v_ln2_b):
    given = dict(locals())
    xs = x[0]
    T = xs.shape[0]
    target = loss_target[0]

    shards = [w_in[0].astype(bf16), glu_w[0].astype(bf16), w_ssm_out[0].astype(bf16), w_conv_out[0].astype(bf16),
              w_o[0].astype(bf16), w_gate[0].T.astype(bf16), w_up[0].T.astype(bf16), w_down[0].astype(bf16), conv_w[0]]
    win_g, glu_g, wso_g, wco_g, wo_g, wgT_g, wuT_g, wd_g, conv_g = all_gather(shards, "gather_weights")
    glu_f = glu_g.reshape(W, W)
    wo_f = wo_g.reshape(D, D)
    wgT = wgT_g.reshape(F, D)
    wuT = wuT_g.reshape(F, D)
    wd_f = wd_g.reshape(F, D)
    conv_f = conv_g.transpose(1, 0, 2).reshape(3, W)

    lam_re, lam_im = ssm_lambda_re[0], ssm_lambda_im[0]
    ldt = ssm_log_dt[0].reshape(NG, 1)
    lbr, lbi, fr, fi = ssm_disc(lam_re, lam_im, ldt)
    br2, bi2 = ssm_b_re[0].reshape(NG * NP, GC), ssm_b_im[0].reshape(NG * NP, GC)
    fr2, fi2 = fr.reshape(NG * NP, 1), fi.reshape(NG * NP, 1)
    bbr, bbi = ssm_bbar(fr2, fi2, br2, bi2)
    bb_t = lambda b: b.reshape(4, 8, NP, GC).transpose(0, 1, 3, 2)
    wb = jnp.concatenate([_block_diag(bb_t(bbr)), _block_diag(bb_t(bbi))], axis=2)
    c_t = lambda c: c.reshape(4, 8, GC, NP).transpose(0, 1, 3, 2)
    wc = jnp.concatenate([_block_diag(c_t(ssm_c_re[0])), -_block_diag(c_t(ssm_c_im[0]))], axis=1)
    wbT, wcT = wb.transpose(0, 2, 1), wc.transpose(0, 2, 1)
    wb, wc, wbT, wcT = wb.astype(bf16), wc.astype(bf16), wbT.astype(bf16), wcT.astype(bf16)
    lbr_s, lbi_s = lbr.reshape(4, 1, SW), lbi.reshape(4, 1, SW)
    dsk = ssm_d[0].reshape(4, 1, LANE)

    proj, xb = in_proj(xs, win_g, b_in)
    u_p = to_perm(proj, 0, "perm_u")
    y_p = ssm_fwd(u_p, wb, wc, lbr_s, lbi_s, dsk)
    yn = from_perm(y_p, "unperm_y")
    ya = glu_fwd(yn, glu_f, glu_b)
    yb = conv_fwd(proj, conv_f)
    merged = merge_fwd(ya, yb, wso_g, wco_g, proj)
    r1, x1b = mix_ln1(merged, wo_f, xs, ln1_g, ln1_b)
    gate, up, hid = gate_up(x1b, wgT, wuT)
    dr2, dffn, sqerr, dln2_g, dln2_b = down_loss(hid, wd_f, r1, ln1_g, ln1_b, ln2_g, ln2_b, target)
    loss = lax.psum(0.5 * sqerr[0, 0], ("x", "y", "c"))

    dgate, dup = ffn_bwd_act(dffn, wd_f, gate, up)
    dwd = mm_tn(hid, dffn, "grad_w_down")
    dwgT = mm_tn(dgate, x1b, "grad_w_gate")
    dwuT = mm_tn(dup, x1b, "grad_w_up")
    dr1, dmix, dln1_g, dln1_b = ffn_bwd_x(dgate, dup, wgT, wuT, dr2, r1, ln1_g)
    dYA, dYB, dga, dgb = merge_bwd(dmix, wo_f, ya, yb, wso_g, wco_g, proj)
    dwo = mm_tn(merged, dmix, "grad_w_o")
    dya = branch_bwd_x(dYA, wso_g, "ssm_out_bwd_x")
    dyb = branch_bwd_x(dYB, wco_g, "conv_out_bwd_x")
    dwso = branch_bwd_w(ya, dYA, "grad_w_ssm_out")
    dwco = branch_bwd_w(yb, dYB, "grad_w_conv_out")
    dyn, dsp, gb, dglu_b = glu_bwd(yn, dya, glu_f, glu_b)
    dglu = mm_tn(gb, dsp, "grad_glu_w")
    dy_p = to_perm(dyn, 0, "perm_dy")
    du_p, dwb, dwcT, dlbr_s, dlbi_s, dd = ssm_bwd(u_p, dy_p, wb, wbT, wcT, lbr_s, lbi_s, dsk)
    du = from_perm(du_p, "unperm_du", bf16)
    dh, dcg, dbg, dconv = conv_bwd(proj, dyb, conv_f)
    parts = [(du, 0), (dh, 0), (dcg, 0), (dbg, 0), (dga, 0), (dga, 1), (dgb, 0), (dgb, 1)]
    grad_x, db_in = in_proj_bwd_x(parts, win_g, dr1)
    dproj = jnp.concatenate([du, dh, dcg, dbg, dga, dgb], axis=1)
    dwin = mm_tn(xb, dproj, "grad_w_in", tk=512, tn=512)
    dwin = dwin.reshape(D, NDEV, W).transpose(1, 0, 2)

    dbb = lambda m: _diag_blocks(m, GC, NP).transpose(0, 1, 3, 2).reshape(NG * NP, GC)
    dbr2, dbi2, dfr2, dfi2 = ssm_param_bwd_b(fr2, fi2, br2, bi2, dbb(dwb[:, :, :SW]), dbb(dwb[:, :, SW:]))
    dlam_re, dlam_im, dldt = ssm_param_bwd_lam(lam_re, lam_im, ldt, dlbr_s.reshape(NG, NP), dlbi_s.reshape(NG, NP),
                                               dfr2.reshape(NG, NP), dfi2.reshape(NG, NP))
    dc = lambda m: _diag_blocks(m, GC, NP).reshape(NG, GC, NP)
    small_grads = dict(b_in=db_in, ssm_lambda_re=dlam_re, ssm_lambda_im=dlam_im, ssm_log_dt=dldt,
                       ssm_b_re=dbr2, ssm_b_im=dbi2, ssm_c_re=dc(dwcT[:, :, :SW]), ssm_c_im=-dc(dwcT[:, :, SW:]),
                       ssm_d=dd, glu_b=dglu_b, ln1_g=dln1_g, ln1_b=dln1_b, ln2_g=dln2_g, ln2_b=dln2_b)

    sends = [dwin, dglu.reshape(NDEV, W // NDEV, W), dwso, dwco, dwo.reshape(NDEV, D // NDEV, D),
             dwgT.reshape(NDEV, FS, D), dwuT.reshape(NDEV, FS, D), dwd.reshape(NDEV, FS, D)]
    r_win, r_glu, r_wso, r_wco, r_wo, r_wgT, r_wuT, r_wd = scatter_blocks(sends, "scatter_grads")
    (small_all,) = all_gather([_pack(small_grads, dconv)], "gather_small_grads")

    out = {}

    def put(name, res):
        out["grad_" + name], out["delta_" + name], out["new_m_" + name], out["new_v_" + name] = [r[None] for r in res]

    put("w_in", adam_update(w_in[0], m_w_in[0], v_w_in[0], r_win, "adam_w_in", 256))
    put("glu_w", adam_update(glu_w[0], m_glu_w[0], v_glu_w[0], r_glu, "adam_glu_w"))
    put("w_ssm_out", adam_update(w_ssm_out[0], m_w_ssm_out[0], v_w_ssm_out[0], r_wso, "adam_w_ssm_out"))
    put("w_conv_out", adam_update(w_conv_out[0], m_w_conv_out[0], v_w_conv_out[0], r_wco, "adam_w_conv_out"))
    put("w_o", adam_update(w_o[0], m_w_o[0], v_w_o[0], r_wo, "adam_w_o"))
    put("w_down", adam_update(w_down[0], m_w_down[0], v_w_down[0], r_wd, "adam_w_down", 176))
    g_gate = sum_blocks(r_wgT, "sum_w_gate", 176).T
    g_up = sum_blocks(r_wuT, "sum_w_up", 176).T
    put("w_gate", adam_update(w_gate[0], m_w_gate[0], v_w_gate[0], g_gate[None], "adam_w_gate", 256))
    put("w_up", adam_update(w_up[0], m_w_up[0], v_w_up[0], g_up[None], "adam_w_up", 256))
    small = adam_update(_pack({n: given[n] for n, _ in _SMALL}), _pack({n: given["m_" + n] for n, _ in _SMALL}),
                        _pack({n: given["v_" + n] for n, _ in _SMALL}), small_all, "adam_small", 88)
    unpacked = [_unpack(p) for p in small]
    for n, _ in _SMALL:
        put(n, [u[0][n][0] for u in unpacked])
    col = (4 * lax.axis_index("x") + 2 * lax.axis_index("y") + lax.axis_index("c")) * (W // NDEV)
    g_conv = lax.dynamic_slice(unpacked[0][1], (0, col), (3, W // NDEV))
    put("conv_w", adam_update(conv_w[0], m_conv_w[0], v_conv_w[0], g_conv[None], "adam_conv_w"))

    names = ["w_in", "b_in", "ssm_lambda_re", "ssm_lambda_im", "ssm_log_dt", "ssm_b_re", "ssm_b_im", "ssm_c_re", "ssm_c_im",
             "ssm_d", "glu_w", "glu_b", "w_ssm_out", "conv_w", "w_conv_out", "w_o", "ln1_g", "ln1_b", "w_gate", "w_up",
             "w_down", "ln2_g", "ln2_b"]
    return (loss, grad_x[None], *[out[p + n] for p in ("grad_", "delta_", "new_m_", "new_v_") for n in names])
```

```python
import functools
import math

import jax
import jax.numpy as jnp
from jax import lax
from jax.experimental import pallas as pl
from jax.experimental.pallas import tpu as pltpu

f32, bf16 = jnp.float32, jnp.bfloat16
S = jax.ShapeDtypeStruct
MESH = pl.DeviceIdType.MESH
HIGHEST = lax.Precision.HIGHEST

D = 1024
W = 512
NG, NP, GC = 32, 64, 16
F = 2816
NDEV = 8
FS = F // NDEV
IN_COLS = 8 * W
ALPHA = 2.0 ** 0.25
LN_EPS = 1e-5
ADAM_LR, ADAM_B1, ADAM_B2, ADAM_EPS, ADAM_WD, ADAM_STEP = 0.001, 0.9, 0.999, 1e-08, 0.01, 10
NC = 32
LANE = 128
SW = 4 * LANE
VMEM_LIMIT = 56 * 1024 * 1024
GRAD_DT = bf16
ANY = pl.BlockSpec(memory_space=pl.ANY)


def _cp(sem=None, vmem=None):
    return pltpu.CompilerParams(dimension_semantics=sem, vmem_limit_bytes=vmem)


def _dot(a, b):
    return jnp.dot(a, b, preferred_element_type=f32)


def _dot_nt(a, b):
    return lax.dot_general(a, b, (((1,), (1,)), ((), ())), preferred_element_type=f32)


def _dot_tn(a, b):
    return lax.dot_general(a, b, (((0,), (0,)), ((), ())), preferred_element_type=f32)


def _eye(n):
    return (lax.broadcasted_iota(jnp.int32, (n, n), 0) == lax.broadcasted_iota(jnp.int32, (n, n), 1)).astype(f32)


def _transpose_exact(a):
    return lax.dot_general(a, _eye(a.shape[0]), (((0,), (0,)), ((), ())), precision=HIGHEST, preferred_element_type=f32)


def _sigmoid(x):
    return 1.0 / (1.0 + jnp.exp(-x))


_GK = math.sqrt(2.0 / math.pi)


def _gelu(x):
    return 0.5 * x * (1.0 + jnp.tanh(_GK * (x + 0.044715 * x * x * x)))


def _gelu_grad(x):
    th = jnp.tanh(_GK * (x + 0.044715 * x * x * x))
    return 0.5 * (1.0 + th) + 0.5 * x * (1.0 - th * th) * _GK * (1.0 + 3.0 * 0.044715 * x * x)


def _ln_stats(r):
    mu = jnp.mean(r, axis=-1, keepdims=True)
    xc = r - mu
    var = jnp.mean(xc * xc, axis=-1, keepdims=True)
    rstd = lax.rsqrt(var + LN_EPS)
    return xc * rstd, rstd


def _ln_bwd(dy, xhat, rstd, g):
    dxh = dy * g
    m1 = jnp.mean(dxh, axis=-1, keepdims=True)
    m2 = jnp.mean(dxh * xhat, axis=-1, keepdims=True)
    return rstd * (dxh - m1 - xhat * m2)


def _coords():
    return lax.axis_index("x"), lax.axis_index("y"), lax.axis_index("c")


class GatherPlan:
    def __init__(self, arrs):
        self.inputs = list(arrs)
        n = len(arrs)
        self.out_shape = [S((NDEV,) + a.shape, a.dtype) for a in arrs]
        self.sems = [pltpu.SemaphoreType.DMA((n, 7)), pltpu.SemaphoreType.DMA((n, 7)), pltpu.SemaphoreType.DMA((n,))]

    def _parts(self, ins, outs, sems):
        n = len(ins)
        send_sems, recv_sems, loc_sems = sems
        x, y, c = _coords()
        me, sib = (x, y, c), (x, y, 1 - c)
        chips = [(1 - x, y), (x, 1 - y), (1 - x, 1 - y)]

        def slot(a, dev):
            return outs[a].at[4 * dev[0] + 2 * dev[1] + dev[2]]

        def copy(a, k, block, to, src=None):
            return pltpu.make_async_remote_copy(
                src_ref=slot(a, block) if src is None else src, dst_ref=slot(a, block),
                send_sem=send_sems.at[a, k], recv_sem=recv_sems.at[a, k], device_id=to, device_id_type=MESH)

        each = [(j, chip, a) for j, chip in enumerate(chips) for a in range(n)]
        return dict(
            mine=lambda: [pltpu.make_async_copy(ins[a], slot(a, me), loc_sems.at[a]) for a in range(n)],
            first=lambda: ([copy(a, 0, me, sib, src=ins[a]) for a in range(n)]
                           + [copy(a, 1 + j, me, (*chip, c), src=ins[a]) for j, chip, a in each]),
            landed=lambda: [copy(a, 1 + j, (*chip, c), me) for j, chip, a in each],
            passed=lambda: [copy(a, 4 + j, (*chip, c), sib) for j, chip, a in each],
            from_sib=lambda: ([copy(a, 0, sib, me) for a in range(n)]
                              + [copy(a, 4 + j, (*chip, 1 - c), me) for j, chip, a in each]))

    def start(self, ins, outs, sems):
        p = self._parts(ins, outs, sems)
        for cp in p["mine"]() + p["first"]():
            cp.start()

    def forward(self, ins, outs, sems):
        p = self._parts(ins, outs, sems)
        for got, fwd in zip(p["landed"](), p["passed"]()):
            got.wait_recv()
            fwd.start()

    def finish(self, ins, outs, sems):
        p = self._parts(ins, outs, sems)
        for cp in p["from_sib"]():
            cp.wait_recv()
        for cp in p["first"]() + p["passed"]():
            cp.wait_send()
        for cp in p["mine"]():
            cp.wait()


class ScatterPlan:
    def __init__(self, gs):
        self.inputs = list(gs)
        n = len(gs)
        self.out_shape = [S(g.shape, g.dtype) for g in gs]
        self.sems = [pltpu.SemaphoreType.DMA((n, 7)), pltpu.SemaphoreType.DMA((n, 7)), pltpu.SemaphoreType.DMA((n,))]

    def _copies(self, ins, outs, sems):
        n = len(ins)
        send_sems, recv_sems, loc_sems = sems
        x, y, c = _coords()
        me = 4 * x + 2 * y + c
        copies = [pltpu.make_async_copy(ins[a].at[me], outs[a].at[me], loc_sems.at[a]) for a in range(n)]
        for m in range(1, NDEV):
            px = 1 - x if m & 4 else x
            py = 1 - y if m & 2 else y
            pc = 1 - c if m & 1 else c
            for a in range(n):
                copies.append(pltpu.make_async_remote_copy(
                    src_ref=ins[a].at[4 * px + 2 * py + pc], dst_ref=outs[a].at[me],
                    send_sem=send_sems.at[a, m - 1], recv_sem=recv_sems.at[a, m - 1],
                    device_id=(px, py, pc), device_id_type=MESH))
        return copies

    def start(self, ins, outs, sems):
        for cp in self._copies(ins, outs, sems):
            cp.start()

    def forward(self, ins, outs, sems):
        pass

    def finish(self, ins, outs, sems):
        for cp in self._copies(ins, outs, sems):
            cp.wait()


class Plans:
    def __init__(self, plans):
        self.plans = plans
        self.inputs = [a for p in plans for a in p.inputs]
        self.out_shape = [s for p in plans for s in p.out_shape]
        self.sems = [s for p in plans for s in p.sems]

    def _each(self, what, ins, outs, sems):
        i = o = s = 0
        for p in self.plans:
            ni, no, ns = len(p.inputs), len(p.out_shape), len(p.sems)
            getattr(p, what)(ins[i:i + ni], outs[o:o + no], sems[s:s + ns])
            i, o, s = i + ni, o + no, s + ns

    def start(self, ins, outs, sems):
        self._each("start", ins, outs, sems)

    def forward(self, ins, outs, sems):
        self._each("forward", ins, outs, sems)

    def finish(self, ins, outs, sems):
        self._each("finish", ins, outs, sems)


def _call(body, args, *, name, grid, in_specs, out_specs, out_shape, scratch=(), sem=None, vmem=None, plan=None,
          aliases=None):
    aliases = aliases or {}
    if plan is None:
        outs = pl.pallas_call(body, name=name, grid=grid, in_specs=list(in_specs), out_specs=list(out_specs),
                              out_shape=list(out_shape), scratch_shapes=list(scratch), input_output_aliases=aliases,
                              compiler_params=_cp(sem, vmem))(*args)
        return list(outs), []
    ni, no, ns = len(in_specs), len(out_specs), len(scratch)
    pi, po = len(plan.inputs), len(plan.out_shape)

    def wrapped(*refs):
        main_in, p_in = refs[:ni], refs[ni:ni + pi]
        main_out, p_out = refs[ni + pi:ni + pi + no], refs[ni + pi + no:ni + pi + no + po]
        main_scr, p_sems = refs[ni + pi + no + po:ni + pi + no + po + ns], refs[ni + pi + no + po + ns:]
        ids = [pl.program_id(d) for d in range(len(grid))]
        first = functools.reduce(jnp.logical_and, [i == 0 for i in ids])
        last = functools.reduce(jnp.logical_and, [i == g - 1 for i, g in zip(ids, grid)])

        @pl.when(first)
        def _():
            plan.start(p_in, p_out, p_sems)

        @pl.when(last)
        def _():
            plan.forward(p_in, p_out, p_sems)

        body(*main_in, *main_out, *main_scr)

        @pl.when(last)
        def _():
            plan.finish(p_in, p_out, p_sems)

    outs = pl.pallas_call(
        wrapped, name=name, grid=grid, in_specs=list(in_specs) + [ANY] * pi, out_specs=list(out_specs) + [ANY] * po,
        out_shape=list(out_shape) + list(plan.out_shape), scratch_shapes=list(scratch) + list(plan.sems),
        input_output_aliases=aliases, compiler_params=_cp(("arbitrary",) * len(grid), vmem),
    )(*args, *plan.inputs)
    return list(outs[:no]), list(outs[no:])


def run_plan(plan, name):
    def body(*refs):
        ins, outs, sems = refs[:len(plan.inputs)], refs[len(plan.inputs):len(plan.inputs) + len(plan.out_shape)], \
            refs[len(plan.inputs) + len(plan.out_shape):]
        plan.start(ins, outs, sems)
        plan.forward(ins, outs, sems)
        plan.finish(ins, outs, sems)

    return pl.pallas_call(body, name=name, in_specs=[ANY] * len(plan.inputs), out_specs=[ANY] * len(plan.out_shape),
                          out_shape=list(plan.out_shape), scratch_shapes=list(plan.sems))(*plan.inputs)


def mm_tn(a, b, name, tk=256, tn=512, into=None, block0=0, nblocks=None):
    T, K = a.shape
    N = b.shape[1]
    tk, tn = min(tk, K), min(tn, N)
    nblocks = nblocks or (N // tn if into is None else into.shape[0])

    def body(a_ref, b_ref, *rest):
        rest[-1][...] = _dot_tn(a_ref[...], b_ref[...]).astype(GRAD_DT)

    args, in_specs, aliases = [a, b], [pl.BlockSpec((T, tk), lambda i, j: (0, i)), pl.BlockSpec((T, tn), lambda i, j: (0, j))], {}
    if into is not None:
        args.append(into)
        in_specs.append(ANY)
        aliases = {2: 0}
    (out,), _ = _call(body, args, name=name, grid=(K // tk, N // tn), in_specs=in_specs,
                      out_specs=[pl.BlockSpec((None, tk, tn), lambda i, j: (block0 + j, i, 0))],
                      out_shape=[S((nblocks, K, tn), GRAD_DT)], sem=("parallel", "parallel"), vmem=VMEM_LIMIT, aliases=aliases)
    return out


def mm_tn_rows(a, b, name, tk=256, tn=512):
    T, K = a.shape
    N = b.shape[1]
    tk, tn = min(tk, K), min(tn, N)

    def body(a_ref, b_ref, o_ref):
        o_ref[...] = _dot_tn(a_ref[...], b_ref[...]).astype(GRAD_DT)

    (out,), _ = _call(body, [a, b], name=name, grid=(K // tk, N // tn),
                      in_specs=[pl.BlockSpec((T, tk), lambda i, j: (0, i)), pl.BlockSpec((T, tn), lambda i, j: (0, j))],
                      out_specs=[pl.BlockSpec((tk, tn), lambda i, j: (i, j))], out_shape=[S((K, N), GRAD_DT)],
                      sem=("parallel", "parallel"), vmem=VMEM_LIMIT)
    return out


def prep_weights(w_in, glu_w, w_ssm_out, w_conv_out, w_o, w_gate, w_up, w_down):
    ws = [w_in, glu_w, w_ssm_out, w_conv_out, w_o, w_gate, w_up, w_down]
    flip = [False, False, False, False, False, True, True, False]

    def body(*refs):
        for i, t in enumerate(flip):
            v = refs[i][0]
            refs[len(ws) + i][...] = (v.T if t else v).astype(bf16)

    shapes = [S(w.shape[1:][::-1] if t else w.shape[1:], bf16) for w, t in zip(ws, flip)]
    return pl.pallas_call(body, name="prep_weights", out_shape=shapes, compiler_params=_cp(None, VMEM_LIMIT))(*ws)


def in_proj(x, win_g, b_in, plan):
    T = x.shape[0]
    tm = min(1024, T)

    def body(x_ref, w_ref, b_ref, o_ref, xb_ref):
        xb = x_ref[...].astype(bf16)
        o_ref[...] = _dot(xb, w_ref[...]) + b_ref[...]

        @pl.when(pl.program_id(1) == 0)
        def _():
            xb_ref[...] = xb

    return _call(
        body, [x, win_g, b_in], name="in_proj", grid=(T // tm, NDEV),
        in_specs=[pl.BlockSpec((tm, D), lambda i, k: (i, 0)), pl.BlockSpec((None, D, W), lambda i, k: (k, 0, 0)),
                  pl.BlockSpec((1, W), lambda i, k: (0, k))],
        out_specs=[pl.BlockSpec((tm, W), lambda i, k: (i, k)), pl.BlockSpec((tm, D), lambda i, k: (i, 0))],
        out_shape=[S((T, IN_COLS), f32), S((T, D), bf16)], vmem=VMEM_LIMIT, plan=plan)


def to_perm(a, cb0, name):
    T = a.shape[0]
    L = T // NC

    def body(a_ref, o_ref):
        def step(j, carry):
            for q in range(NC // 8):
                o_ref[pl.ds(pl.multiple_of(j * NC, NC) + 8 * q, 8), :] = a_ref[pl.ds(q * 8 * L + j, 8, stride=L), :]
            return carry

        lax.fori_loop(0, L, step, 0)

    return pl.pallas_call(
        body, name=name, grid=(W // LANE,),
        in_specs=[pl.BlockSpec((T, LANE), lambda k: (0, cb0 + k))], out_specs=pl.BlockSpec((T, LANE), lambda k: (0, k)),
        out_shape=S((T, W), f32), compiler_params=_cp(("parallel",), VMEM_LIMIT),
    )(a)


def from_perm(a, name, out_dtype=f32):
    T = a.shape[0]
    L = T // NC

    def body(a_ref, o_ref, s_ref):
        def step(i, acc):
            c, jb = i // (L // 16), i % (L // 16)
            t0 = a_ref[pl.ds(jb * 16 * NC + c, 8, stride=NC), :]
            t1 = a_ref[pl.ds((jb * 16 + 8) * NC + c, 8, stride=NC), :]
            o_ref[pl.ds(pl.multiple_of(i * 16, 16), 16), :] = jnp.concatenate([t0, t1], axis=0).astype(out_dtype)
            return acc + t0 + t1

        acc = lax.fori_loop(0, T // 16, step, jnp.zeros((8, LANE), f32))
        s_ref[...] = jnp.sum(acc, axis=0, keepdims=True)

    return pl.pallas_call(
        body, name=name, grid=(W // LANE,),
        in_specs=[pl.BlockSpec((T, LANE), lambda k: (0, k))],
        out_specs=[pl.BlockSpec((T, LANE), lambda k: (0, k)), pl.BlockSpec((1, LANE), lambda k: (0, k))],
        out_shape=[S((T, W), out_dtype), S((1, W), f32)], compiler_params=_cp(("parallel",), VMEM_LIMIT),
    )(a)


def _disc(lr, li, ldt):
    dt = jnp.exp(ldt)
    mag = jnp.exp(lr * dt)
    lbr = mag * jnp.cos(li * dt)
    lbi = mag * jnp.sin(li * dt)
    den = lr * lr + li * li
    nr = lbr - 1.0
    return lbr, lbi, (nr * lr + lbi * li) / den, (lbi * lr - nr * li) / den


def ssm_disc(lam_re, lam_im, log_dt):
    def body(lr_ref, li_ref, ldt_ref, lbr_ref, lbi_ref, fr_ref, fi_ref):
        lbr_ref[...], lbi_ref[...], fr_ref[...], fi_ref[...] = _disc(lr_ref[...], li_ref[...], ldt_ref[...])

    return pl.pallas_call(body, name="ssm_disc", out_shape=[S((NG, NP), f32)] * 4)(lam_re, lam_im, log_dt)


def ssm_bbar(fr, fi, br, bi):
    def body(fr_ref, fi_ref, br_ref, bi_ref, or_ref, oi_ref):
        fr_, fi_, br_, bi_ = fr_ref[...], fi_ref[...], br_ref[...], bi_ref[...]
        or_ref[...] = fr_ * br_ - fi_ * bi_
        oi_ref[...] = fr_ * bi_ + fi_ * br_

    return pl.pallas_call(body, name="ssm_bbar", out_shape=[S((NG * NP, GC), f32)] * 2)(fr, fi, br, bi)


def _scan_body(T):
    L = T // NC
    RB = min(512, T)
    nsq = int(round(math.log2(L)))
    assert 2 ** nsq == L and T % RB == 0 and L % 16 == 0

    def rows(i):
        return pl.ds(pl.multiple_of(i * RB, RB), RB)

    def tile(j):
        return pl.ds(pl.multiple_of(j * NC, NC), NC)

    def forward_states(u_ref, wb_ref, lbr_ref, lbi_ref, sre, sim, ere, eim):
        def bproj(i, carry):
            bu = _dot(u_ref[rows(i), :].astype(bf16), wb_ref[...])
            sre[rows(i), :] = bu[:, :SW]
            sim[rows(i), :] = bu[:, SW:]
            return carry

        lax.fori_loop(0, T // RB, bproj, 0)
        for lb in range(SW // LANE):
            ls = slice(lb * LANE, (lb + 1) * LANE)
            ar = jnp.broadcast_to(lbr_ref[:, ls], (NC, LANE))
            ai = jnp.broadcast_to(lbi_ref[:, ls], (NC, LANE))

            def step(j, carry):
                xr, xi = carry
                nr = ar * xr - ai * xi + sre[tile(j), ls]
                ni = ar * xi + ai * xr + sim[tile(j), ls]
                sre[tile(j), ls] = nr
                sim[tile(j), ls] = ni
                return nr, ni

            zero = jnp.zeros((NC, LANE), f32)
            lax.fori_loop(0, L, step, (zero, zero))
            pr, pi = lbr_ref[:, ls], lbi_ref[:, ls]
            for _ in range(nsq):
                pr, pi = pr * pr - pi * pi, 2.0 * pr * pi
            er = jnp.zeros((1, LANE), f32)
            ei = er
            ere[0:1, ls] = er
            eim[0:1, ls] = ei
            base = (L - 1) * NC
            for c in range(1, NC):
                lr_ = sre[base + c - 1:base + c, ls]
                li_ = sim[base + c - 1:base + c, ls]
                er, ei = lr_ + pr * er - pi * ei, li_ + pr * ei + pi * er
                ere[c:c + 1, ls] = er
                eim[c:c + 1, ls] = ei
            e_r, e_i = ere[:, ls], eim[:, ls]

            def fix(j, carry):
                pwr, pwi = carry
                sre[tile(j), ls] += pwr * e_r - pwi * e_i
                sim[tile(j), ls] += pwr * e_i + pwi * e_r
                return pwr * ar - pwi * ai, pwr * ai + pwi * ar

            lax.fori_loop(0, L, fix, (ar, ai))

    return L, RB, nsq, rows, tile, forward_states


def ssm_fwd(u_p, wb, wc, lbr, lbi, dsk, plan):
    T = u_p.shape[0]
    L, RB, nsq, rows, tile, forward_states = _scan_body(T)

    def body(u_ref, wb_ref, wc_ref, lbr_ref, lbi_ref, d_ref, y_ref, sre, sim, ere, eim):
        forward_states(u_ref, wb_ref, lbr_ref, lbi_ref, sre, sim, ere, eim)

        def cproj(i, carry):
            y = _dot(sre[rows(i), :].astype(bf16), wc_ref[0:SW, :]) + _dot(sim[rows(i), :].astype(bf16), wc_ref[SW:, :])
            y_ref[rows(i), :] = y + d_ref[...] * u_ref[rows(i), :]
            return carry

        lax.fori_loop(0, T // RB, cproj, 0)

    slab = pl.BlockSpec((T, LANE), lambda k: (0, k))
    return _call(
        body, [u_p, wb, wc, lbr, lbi, dsk], name="ssm_fwd", grid=(W // LANE,),
        in_specs=[slab, pl.BlockSpec((None, LANE, 2 * SW), lambda k: (k, 0, 0)),
                  pl.BlockSpec((None, 2 * SW, LANE), lambda k: (k, 0, 0)),
                  pl.BlockSpec((None, 1, SW), lambda k: (k, 0, 0)), pl.BlockSpec((None, 1, SW), lambda k: (k, 0, 0)),
                  pl.BlockSpec((None, 1, LANE), lambda k: (k, 0, 0))],
        out_specs=[slab], out_shape=[S((T, W), f32)],
        scratch=[pltpu.VMEM((T, SW), f32), pltpu.VMEM((T, SW), f32), pltpu.VMEM((NC, SW), f32), pltpu.VMEM((NC, SW), f32)],
        vmem=VMEM_LIMIT, plan=plan)


def ssm_bwd(u_p, dy_p, wb, wbT, wcT, lbr, lbi, dsk, plan):
    T = u_p.shape[0]
    L, RB, nsq, rows, tile, forward_states = _scan_body(T)

    def body(u_ref, dy_ref, wb_ref, wbT_ref, wcT_ref, lbr_ref, lbi_ref, d_ref,
             du_ref, dwb_ref, dwc_ref, dlr_ref, dli_ref, dd_ref, sre, sim, gre, gim, ere, eim):
        forward_states(u_ref, wb_ref, lbr_ref, lbi_ref, sre, sim, ere, eim)

        def dstate(i, carry):
            g = _dot(dy_ref[rows(i), :].astype(bf16), wcT_ref[...])
            gre[rows(i), :] = g[:, :SW]
            gim[rows(i), :] = g[:, SW:]
            return carry

        lax.fori_loop(0, T // RB, dstate, 0)
        row = lax.broadcasted_iota(jnp.int32, (NC, LANE), 0)
        for lb in range(SW // LANE):
            ls = slice(lb * LANE, (lb + 1) * LANE)
            ar = jnp.broadcast_to(lbr_ref[:, ls], (NC, LANE))
            ai = jnp.broadcast_to(lbi_ref[:, ls], (NC, LANE))

            def step(i, carry):
                gr, gi = carry
                j = L - 1 - i
                nr = ar * gr + ai * gi + gre[tile(j), ls]
                ni = ar * gi - ai * gr + gim[tile(j), ls]
                gre[tile(j), ls] = nr
                gim[tile(j), ls] = ni
                return nr, ni

            zero = jnp.zeros((NC, LANE), f32)
            lax.fori_loop(0, L, step, (zero, zero))
            pr, pi = lbr_ref[:, ls], -lbi_ref[:, ls]
            for _ in range(nsq):
                pr, pi = pr * pr - pi * pi, 2.0 * pr * pi
            er = jnp.zeros((1, LANE), f32)
            ei = er
            ere[NC - 1:NC, ls] = er
            eim[NC - 1:NC, ls] = ei
            for c in range(NC - 2, -1, -1):
                lr_ = gre[c + 1:c + 2, ls]
                li_ = gim[c + 1:c + 2, ls]
                er, ei = lr_ + pr * er - pi * ei, li_ + pr * ei + pi * er
                ere[c:c + 1, ls] = er
                eim[c:c + 1, ls] = ei
            e_r, e_i = ere[:, ls], eim[:, ls]

            def fixed(j, pwr, pwi):
                gr = gre[tile(j), ls] + pwr * e_r - pwi * e_i
                gi = gim[tile(j), ls] + pwr * e_i + pwi * e_r
                gre[tile(j), ls] = gr
                gim[tile(j), ls] = gi
                return gr, gi

            def fix(i, carry):
                pwr, pwi, accr, acci = carry
                j = L - 1 - i
                gr, gi = fixed(j, pwr, pwi)
                xr, xi = sre[tile(j - 1), ls], sim[tile(j - 1), ls]
                return (pwr * ar + pwi * ai, pwi * ar - pwr * ai,
                        accr + gr * xr + gi * xi, acci + gi * xr - gr * xi)

            pwr, pwi, accr, acci = lax.fori_loop(0, L - 1, fix, (ar, -ai, zero, zero))
            gr, gi = fixed(0, pwr, pwi)
            xr = jnp.where(row == 0, 0.0, pltpu.roll(sre[tile(L - 1), ls], 1, axis=0))
            xi = jnp.where(row == 0, 0.0, pltpu.roll(sim[tile(L - 1), ls], 1, axis=0))
            accr = accr + gr * xr + gi * xi
            acci = acci + gi * xr - gr * xi
            dlr_ref[:, ls] = jnp.sum(accr, axis=0, keepdims=True)
            dli_ref[:, ls] = jnp.sum(acci, axis=0, keepdims=True)

        dwb_ref[...] = jnp.zeros_like(dwb_ref)
        dwc_ref[...] = jnp.zeros_like(dwc_ref)
        dd_ref[...] = jnp.zeros_like(dd_ref)

        def finish(i, carry):
            u32, dy32 = u_ref[rows(i), :], dy_ref[rows(i), :]
            ub, dyb = u32.astype(bf16), dy32.astype(bf16)
            gr, gi = gre[rows(i), :].astype(bf16), gim[rows(i), :].astype(bf16)
            du_ref[rows(i), :] = _dot(gr, wbT_ref[0:SW, :]) + _dot(gi, wbT_ref[SW:, :]) + dy32 * d_ref[...]
            dwb_ref[:, 0:SW] += _dot_tn(ub, gr)
            dwb_ref[:, SW:] += _dot_tn(ub, gi)
            dwc_ref[:, 0:SW] += _dot_tn(dyb, sre[rows(i), :].astype(bf16))
            dwc_ref[:, SW:] += _dot_tn(dyb, sim[rows(i), :].astype(bf16))
            dd_ref[...] += jnp.sum(dy32 * u32, axis=0, keepdims=True)
            return carry

        lax.fori_loop(0, T // RB, finish, 0)

    slab = pl.BlockSpec((T, LANE), lambda k: (0, k))
    wide = pl.BlockSpec((None, LANE, 2 * SW), lambda k: (k, 0, 0))
    tall = pl.BlockSpec((None, 2 * SW, LANE), lambda k: (k, 0, 0))
    vec = pl.BlockSpec((None, 1, SW), lambda k: (k, 0, 0))
    vecd = pl.BlockSpec((None, 1, LANE), lambda k: (k, 0, 0))
    nslab = W // LANE
    return _call(
        body, [u_p, dy_p, wb, wbT, wcT, lbr, lbi, dsk], name="ssm_bwd", grid=(nslab,),
        in_specs=[slab, slab, wide, tall, wide, vec, vec, vecd],
        out_specs=[slab, wide, wide, vec, vec, vecd],
        out_shape=[S((T, W), f32), S((nslab, LANE, 2 * SW), f32), S((nslab, LANE, 2 * SW), f32),
                   S((nslab, 1, SW), f32), S((nslab, 1, SW), f32), S((nslab, 1, LANE), f32)],
        scratch=[pltpu.VMEM((T, SW), f32)] * 4 + [pltpu.VMEM((NC, SW), f32)] * 2, vmem=VMEM_LIMIT, plan=plan)


def glu_fwd(yn, glu_w, glu_b):
    T = yn.shape[0]
    tm = min(512, T)

    def body(y_ref, w_ref, b_ref, o_ref):
        g = _gelu(y_ref[...])
        o_ref[...] = (g * _sigmoid(_dot(g.astype(bf16), w_ref[...]) + b_ref[...])).astype(bf16)

    return pl.pallas_call(
        body, name="glu_fwd", grid=(T // tm,),
        in_specs=[pl.BlockSpec((tm, W), lambda i: (i, 0)), pl.BlockSpec((W, W), lambda i: (0, 0)), pl.BlockSpec((1, W), lambda i: (0, 0))],
        out_specs=pl.BlockSpec((tm, W), lambda i: (i, 0)), out_shape=S((T, W), bf16), compiler_params=_cp(("parallel",)),
    )(yn, glu_w, glu_b)


def _shift_rows(cur, prev8, k):
    return pltpu.roll(jnp.concatenate([prev8, cur], axis=0), k, axis=0)[8:]


def _lift_rows(cur, next8, k):
    n = cur.shape[0]
    return pltpu.roll(jnp.concatenate([cur, next8], axis=0), n + 8 - k, axis=0)[:n]


def conv_fwd(proj, conv_w):
    T = proj.shape[0]
    RB = min(512, T)

    def body(h_ref, c_ref, b_ref, w_ref, o_ref):
        w0, w1, w2 = w_ref[0:1, :], w_ref[1:2, :], w_ref[2:3, :]

        def blk(i, carry):
            r0 = pl.multiple_of(i * RB, RB)
            rs = pl.ds(r0, RB)
            ch = c_ref[rs, :] * h_ref[rs, :]
            pr = pl.ds(jnp.maximum(r0 - 8, 0), 8)
            prev = jnp.where(i > 0, c_ref[pr, :] * h_ref[pr, :], 0.0)
            z = w2 * ch + w1 * _shift_rows(ch, prev, 1) + w0 * _shift_rows(ch, prev, 2)
            o_ref[rs, :] = (b_ref[rs, :] * z).astype(bf16)
            return carry

        lax.fori_loop(0, T // RB, blk, 0)

    nb = W // LANE
    return pl.pallas_call(
        body, name="conv_fwd", grid=(nb,),
        in_specs=[pl.BlockSpec((T, LANE), lambda k: (0, nb + k)), pl.BlockSpec((T, LANE), lambda k: (0, 2 * nb + k)),
                  pl.BlockSpec((T, LANE), lambda k: (0, 3 * nb + k)), pl.BlockSpec((3, LANE), lambda k: (0, k))],
        out_specs=pl.BlockSpec((T, LANE), lambda k: (0, k)), out_shape=S((T, W), bf16),
        compiler_params=_cp(("parallel",), VMEM_LIMIT),
    )(proj, proj, proj, conv_w)


def merge_fwd(ya, yb, wso, wco, proj, plan):
    T = ya.shape[0]
    tm = min(1024, T)

    def body(ya_ref, yb_ref, wa_ref, wb_ref, ga_ref, gb_ref, o_ref):
        o_ref[...] = (_sigmoid(ga_ref[...]) * _dot(ya_ref[...], wa_ref[...])
                      + _sigmoid(gb_ref[...]) * _dot(yb_ref[...], wb_ref[...])).astype(bf16)

    act = pl.BlockSpec((tm, W), lambda i, k: (i, 0))
    wsp = pl.BlockSpec((None, W, LANE), lambda i, k: (k, 0, 0))
    return _call(
        body, [ya, yb, wso, wco, proj, proj], name="merge_fwd", grid=(T // tm, NDEV),
        in_specs=[act, act, wsp, wsp, pl.BlockSpec((tm, LANE), lambda i, k: (i, 16 + k)), pl.BlockSpec((tm, LANE), lambda i, k: (i, 24 + k))],
        out_specs=[pl.BlockSpec((tm, LANE), lambda i, k: (i, k))], out_shape=[S((T, D), bf16)], vmem=VMEM_LIMIT, plan=plan)


def mix_ln1(merged, w_o, x, g1, b1):
    T = x.shape[0]
    tm = min(512, T)

    def body(m_ref, w_ref, x_ref, g_ref, b_ref, r_ref, x1_ref):
        r = ALPHA * x_ref[...] + _dot(m_ref[...], w_ref[...])
        r_ref[...] = r
        xhat, _ = _ln_stats(r)
        x1_ref[...] = (xhat * g_ref[...] + b_ref[...]).astype(bf16)

    row = pl.BlockSpec((tm, D), lambda i: (i, 0))
    vec = pl.BlockSpec((1, D), lambda i: (0, 0))
    return pl.pallas_call(
        body, name="mix_ln1", grid=(T // tm,),
        in_specs=[row, pl.BlockSpec((D, D), lambda i: (0, 0)), row, vec, vec],
        out_specs=[row, row], out_shape=[S((T, D), f32), S((T, D), bf16)], compiler_params=_cp(("parallel",), VMEM_LIMIT),
    )(merged, w_o, x, g1, b1)


FT = 256


def gate_up(x1b, wgT, wuT):
    T = x1b.shape[0]
    tm = min(1024, T)

    def body(x_ref, wg_ref, wu_ref, g_ref, u_ref, h_ref):
        g = _dot_nt(x_ref[...], wg_ref[...])
        u = _dot_nt(x_ref[...], wu_ref[...])
        g_ref[...] = g.astype(bf16)
        u_ref[...] = u.astype(bf16)
        h_ref[...] = (g * _sigmoid(g) * u).astype(bf16)

    wsp = pl.BlockSpec((FT, D), lambda i, n: (n, 0))
    osp = pl.BlockSpec((tm, FT), lambda i, n: (i, n))
    return pl.pallas_call(
        body, name="gate_up", grid=(T // tm, F // FT),
        in_specs=[pl.BlockSpec((tm, D), lambda i, n: (i, 0)), wsp, wsp],
        out_specs=[osp, osp, osp], out_shape=[S((T, F), bf16)] * 3, compiler_params=_cp(("parallel", "arbitrary"), VMEM_LIMIT),
    )(x1b, wgT, wuT)


def down_loss(hid, w_down, r1, g1, b1, g2, b2, target):
    T = hid.shape[0]
    tm = min(256, T)

    def body(h_ref, w_ref, r1_ref, g1_ref, b1_ref, g2_ref, b2_ref, t_ref, dr_ref, drb_ref, loss_ref, dg_ref, db_ref):
        @pl.when(pl.program_id(0) == 0)
        def _():
            loss_ref[...] = jnp.zeros_like(loss_ref)
            dg_ref[...] = jnp.zeros_like(dg_ref)
            db_ref[...] = jnp.zeros_like(db_ref)

        xh1, _ = _ln_stats(r1_ref[...])
        x1 = xh1 * g1_ref[...] + b1_ref[...]
        r2 = ALPHA * x1 + _dot(h_ref[...], w_ref[...])
        xh2, rstd2 = _ln_stats(r2)
        err = xh2 * g2_ref[...] + b2_ref[...] - t_ref[...]
        loss_ref[...] += jnp.sum(jnp.mean(err * err, axis=-1, keepdims=True), axis=0, keepdims=True)
        dy = err * (1.0 / D)
        dg_ref[...] += jnp.sum(dy * xh2, axis=0, keepdims=True)
        db_ref[...] += jnp.sum(dy, axis=0, keepdims=True)
        dr = _ln_bwd(dy, xh2, rstd2, g2_ref[...])
        dr_ref[...] = dr
        drb_ref[...] = dr.astype(bf16)

    row = pl.BlockSpec((tm, D), lambda i: (i, 0))
    vec = pl.BlockSpec((1, D), lambda i: (0, 0))
    return pl.pallas_call(
        body, name="down_loss", grid=(T // tm,),
        in_specs=[pl.BlockSpec((tm, F), lambda i: (i, 0)), pl.BlockSpec((F, D), lambda i: (0, 0)), row, vec, vec, vec, vec, row],
        out_specs=[row, row, pl.BlockSpec((1, 1), lambda i: (0, 0)), vec, vec],
        out_shape=[S((T, D), f32), S((T, D), bf16), S((1, 1), f32), S((1, D), f32), S((1, D), f32)],
        compiler_params=_cp(("arbitrary",), VMEM_LIMIT),
    )(hid, w_down, r1, g1, b1, g2, b2, target)


def ffn_bwd_act(dffn, w_down, gate, up):
    T = dffn.shape[0]
    tm = min(1024, T)

    def body(d_ref, w_ref, g_ref, u_ref, dg_ref, du_ref):
        dh = _dot_nt(d_ref[...], w_ref[...])
        g, u = g_ref[...].astype(f32), u_ref[...].astype(f32)
        sg = _sigmoid(g)
        du_ref[...] = (dh * g * sg).astype(bf16)
        dg_ref[...] = (dh * u * sg * (1.0 + g * (1.0 - sg))).astype(bf16)

    osp = pl.BlockSpec((tm, FT), lambda i, n: (i, n))
    return pl.pallas_call(
        body, name="ffn_bwd_act", grid=(T // tm, F // FT),
        in_specs=[pl.BlockSpec((tm, D), lambda i, n: (i, 0)), pl.BlockSpec((FT, D), lambda i, n: (n, 0)), osp, osp],
        out_specs=[osp, osp], out_shape=[S((T, F), bf16)] * 2, compiler_params=_cp(("parallel", "arbitrary"), VMEM_LIMIT),
    )(dffn, w_down, gate, up)


def ffn_bwd_x(dgate, dup, wgT, wuT, dr2, r1, g1, plan):
    T = dr2.shape[0]
    tm = min(256, T)

    def body(dg_ref, du_ref, wg_ref, wu_ref, dr2_ref, r1_ref, g1_ref, dr_ref, drb_ref, dgam_ref, dbet_ref):
        @pl.when(pl.program_id(0) == 0)
        def _():
            dgam_ref[...] = jnp.zeros_like(dgam_ref)
            dbet_ref[...] = jnp.zeros_like(dbet_ref)

        dx1 = ALPHA * dr2_ref[...] + _dot(dg_ref[...], wg_ref[...]) + _dot(du_ref[...], wu_ref[...])
        xh, rstd = _ln_stats(r1_ref[...])
        dgam_ref[...] += jnp.sum(dx1 * xh, axis=0, keepdims=True)
        dbet_ref[...] += jnp.sum(dx1, axis=0, keepdims=True)
        dr = _ln_bwd(dx1, xh, rstd, g1_ref[...])
        dr_ref[...] = dr
        drb_ref[...] = dr.astype(bf16)

    row = pl.BlockSpec((tm, D), lambda i: (i, 0))
    wide = pl.BlockSpec((tm, F), lambda i: (i, 0))
    wsp = pl.BlockSpec((F, D), lambda i: (0, 0))
    vec = pl.BlockSpec((1, D), lambda i: (0, 0))
    return _call(
        body, [dgate, dup, wgT, wuT, dr2, r1, g1], name="ffn_bwd_x", grid=(T // tm,),
        in_specs=[wide, wide, wsp, wsp, row, row, vec],
        out_specs=[row, row, vec, vec], out_shape=[S((T, D), f32), S((T, D), bf16), S((1, D), f32), S((1, D), f32)],
        vmem=VMEM_LIMIT, plan=plan)


def merge_bwd(dmix, w_o, ya, yb, wso, wco, proj, plan):
    T = dmix.shape[0]
    tm = min(1024, T)

    def body(dm_ref, wo_ref, ya_ref, yb_ref, wa_ref, wb_ref, ga_ref, gb_ref, dya_ref, dyb_ref, dga_ref, dgb_ref, sa_ref, sb_ref):
        dmer = _dot_nt(dm_ref[...], wo_ref[...])
        sa, sb = _sigmoid(ga_ref[...]), _sigmoid(gb_ref[...])
        dya_ref[...] = (dmer * sa).astype(bf16)
        dyb_ref[...] = (dmer * sb).astype(bf16)
        dga = dmer * _dot(ya_ref[...], wa_ref[...]) * sa * (1.0 - sa)
        dgb = dmer * _dot(yb_ref[...], wb_ref[...]) * sb * (1.0 - sb)
        dga_ref[...] = dga.astype(bf16)
        dgb_ref[...] = dgb.astype(bf16)
        sa_ref[...] = jnp.sum(dga, axis=0, keepdims=True)
        sb_ref[...] = jnp.sum(dgb, axis=0, keepdims=True)

    act = pl.BlockSpec((tm, W), lambda i, k: (i, 0))
    wsp = pl.BlockSpec((None, W, LANE), lambda i, k: (k, 0, 0))
    osp = pl.BlockSpec((tm, LANE), lambda i, k: (i, k))
    ssp = pl.BlockSpec((None, 1, LANE), lambda i, k: (i, 0, k))
    return _call(
        body, [dmix, w_o, ya, yb, wso, wco, proj, proj], name="merge_bwd", grid=(T // tm, NDEV),
        in_specs=[pl.BlockSpec((tm, D), lambda i, k: (i, 0)), pl.BlockSpec((LANE, D), lambda i, k: (k, 0)), act, act, wsp, wsp,
                  pl.BlockSpec((tm, LANE), lambda i, k: (i, 16 + k)), pl.BlockSpec((tm, LANE), lambda i, k: (i, 24 + k))],
        out_specs=[osp, osp, osp, osp, ssp, ssp],
        out_shape=[S((T, D), bf16)] * 4 + [S((T // tm, 1, D), f32)] * 2, vmem=VMEM_LIMIT, plan=plan)


def branch_bwd_x(dY, wblk, name):
    T = dY.shape[0]
    tm = min(1024, T)

    def body(d_ref, w_ref, o_ref):
        @pl.when(pl.program_id(1) == 0)
        def _():
            o_ref[...] = jnp.zeros_like(o_ref)

        o_ref[...] += _dot_nt(d_ref[...], w_ref[...])

    return pl.pallas_call(
        body, name=name, grid=(T // tm, NDEV),
        in_specs=[pl.BlockSpec((tm, LANE), lambda i, k: (i, k)), pl.BlockSpec((None, W, LANE), lambda i, k: (k, 0, 0))],
        out_specs=pl.BlockSpec((tm, W), lambda i, k: (i, 0)), out_shape=S((T, W), f32),
        compiler_params=_cp(("parallel", "arbitrary"), VMEM_LIMIT),
    )(dY, wblk)


def branch_bwd_w(act, dY, name):
    T = act.shape[0]

    def body(a_ref, d_ref, o_ref):
        o_ref[...] = _dot_tn(a_ref[...], d_ref[...]).astype(o_ref.dtype)

    return pl.pallas_call(
        body, name=name, grid=(NDEV,),
        in_specs=[pl.BlockSpec((T, W), lambda k: (0, 0)), pl.BlockSpec((T, LANE), lambda k: (0, k))],
        out_specs=pl.BlockSpec((None, W, LANE), lambda k: (k, 0, 0)), out_shape=S((NDEV, W, LANE), GRAD_DT),
        compiler_params=_cp(("parallel",), VMEM_LIMIT),
    )(act, dY)


def glu_bwd(yn, dya, glu_w, glu_b):
    T = yn.shape[0]
    tm = min(512, T)

    def body(y_ref, d_ref, w_ref, b_ref, dy_ref, dsp_ref, g_ref, db_ref):
        @pl.when(pl.program_id(0) == 0)
        def _():
            db_ref[...] = jnp.zeros_like(db_ref)

        y, dya_ = y_ref[...], d_ref[...]
        g = _gelu(y)
        gb = g.astype(bf16)
        s = _sigmoid(_dot(gb, w_ref[...]) + b_ref[...])
        dsp = dya_ * g * s * (1.0 - s)
        dspb = dsp.astype(bf16)
        dg = dya_ * s + _dot_nt(dspb, w_ref[...])
        dy_ref[...] = dg * _gelu_grad(y)
        dsp_ref[...] = dspb
        g_ref[...] = gb
        db_ref[...] += jnp.sum(dsp, axis=0, keepdims=True)

    row = pl.BlockSpec((tm, W), lambda i: (i, 0))
    vec = pl.BlockSpec((1, W), lambda i: (0, 0))
    return pl.pallas_call(
        body, name="glu_bwd", grid=(T // tm,),
        in_specs=[row, row, pl.BlockSpec((W, W), lambda i: (0, 0)), vec],
        out_specs=[row, row, row, vec], out_shape=[S((T, W), f32), S((T, W), bf16), S((T, W), bf16), S((1, W), f32)],
        compiler_params=_cp(("arbitrary",)),
    )(yn, dya, glu_w, glu_b)


def conv_bwd(proj, dyb, conv_w):
    T = proj.shape[0]
    RB = min(512, T)
    nrb = T // RB

    def body(h_ref, c_ref, b_ref, d_ref, w_ref, dh_ref, dc_ref, db_ref, dw_ref, s_ref):
        w0, w1, w2 = w_ref[0:1, :], w_ref[1:2, :], w_ref[2:3, :]

        def blk(i, carry):
            a0, a1, a2, sh, sc, sb = carry
            r0 = pl.multiple_of(i * RB, RB)
            rs = pl.ds(r0, RB)
            h, cg, bg, dyb_ = h_ref[rs, :], c_ref[rs, :], b_ref[rs, :], d_ref[rs, :]
            ch = cg * h
            pr = pl.ds(jnp.maximum(r0 - 8, 0), 8)
            prev = jnp.where(i > 0, c_ref[pr, :] * h_ref[pr, :], 0.0)
            ch1, ch2 = _shift_rows(ch, prev, 1), _shift_rows(ch, prev, 2)
            dbg = dyb_ * (w2 * ch + w1 * ch1 + w0 * ch2)
            db_ref[rs, :] = dbg.astype(bf16)
            dz = dyb_ * bg
            nx = pl.ds(jnp.minimum(r0 + RB, T - 8), 8)
            nxt = jnp.where(i < nrb - 1, d_ref[nx, :] * b_ref[nx, :], 0.0)
            dch = w2 * dz + w1 * _lift_rows(dz, nxt, 1) + w0 * _lift_rows(dz, nxt, 2)
            dcg, dh = dch * h, dch * cg
            dc_ref[rs, :] = dcg.astype(bf16)
            dh_ref[rs, :] = dh.astype(bf16)
            col = lambda v: jnp.sum(v, axis=0, keepdims=True)
            return (a0 + col(dz * ch2), a1 + col(dz * ch1), a2 + col(dz * ch), sh + col(dh), sc + col(dcg), sb + col(dbg))

        zero = jnp.zeros((1, LANE), f32)
        a0, a1, a2, sh, sc, sb = lax.fori_loop(0, nrb, blk, (zero,) * 6)
        dw_ref[0:1, :] = a0
        dw_ref[1:2, :] = a1
        dw_ref[2:3, :] = a2
        s_ref[0:1, :] = sh
        s_ref[1:2, :] = sc
        s_ref[2:3, :] = sb

    nb = W // LANE
    slab = pl.BlockSpec((T, LANE), lambda k: (0, k))
    three = pl.BlockSpec((3, LANE), lambda k: (0, k))
    return pl.pallas_call(
        body, name="conv_bwd", grid=(nb,),
        in_specs=[pl.BlockSpec((T, LANE), lambda k: (0, nb + k)), pl.BlockSpec((T, LANE), lambda k: (0, 2 * nb + k)),
                  pl.BlockSpec((T, LANE), lambda k: (0, 3 * nb + k)), slab, three],
        out_specs=[slab, slab, slab, three, three],
        out_shape=[S((T, W), bf16)] * 3 + [S((3, W), f32)] * 2, compiler_params=_cp(("parallel",), VMEM_LIMIT),
    )(proj, proj, proj, dyb, conv_w)


def in_proj_bwd_x(parts, win_g, dr1, plan):
    T = dr1.shape[0]
    tm = min(512, T)

    def body(*refs):
        p_refs, w_ref, dr_ref, gx_ref = refs[:NDEV], refs[NDEV], refs[NDEV + 1], refs[NDEV + 2]
        acc = ALPHA * dr_ref[...]
        for k in range(NDEV):
            acc += _dot_nt(p_refs[k][...], w_ref[k])
        gx_ref[...] = acc

    row = pl.BlockSpec((tm, D), lambda i: (i, 0))
    p_specs = [pl.BlockSpec((tm, W), (lambda i, cb=cb: (i, cb))) for _, cb in parts]
    return _call(
        body, [a for a, _ in parts] + [win_g, dr1], name="in_proj_bwd_x", grid=(T // tm,),
        in_specs=p_specs + [pl.BlockSpec((NDEV, D, W), lambda i: (0, 0, 0)), row],
        out_specs=[row], out_shape=[S((T, D), f32)], vmem=VMEM_LIMIT, plan=plan)


def ssm_param_bwd_b(fr, fi, br, bi, dbbr, dbbi):
    def body(fr_ref, fi_ref, br_ref, bi_ref, dr_ref, di_ref, dbr_ref, dbi_ref, dfr_ref, dfi_ref):
        fr_, fi_, br_, bi_, dr, di = fr_ref[...], fi_ref[...], br_ref[...], bi_ref[...], dr_ref[...], di_ref[...]
        eye = _eye(GC)
        nt = lambda m: lax.dot_general(eye, m, (((1,), (1,)), ((), ())), precision=HIGHEST, preferred_element_type=f32)
        dbr_ref[...] = nt(fr_ * dr + fi_ * di)
        dbi_ref[...] = nt(fr_ * di - fi_ * dr)
        dfr_ref[...] = jnp.sum(dr * br_ + di * bi_, axis=-1, keepdims=True)
        dfi_ref[...] = jnp.sum(di * br_ - dr * bi_, axis=-1, keepdims=True)

    n = NG * NP
    return pl.pallas_call(body, name="ssm_param_bwd_b",
                          out_shape=[S((GC, n), f32), S((GC, n), f32), S((n, 1), f32), S((n, 1), f32)])(fr, fi, br, bi, dbbr, dbbi)


def ssm_param_bwd_lam(lam_re, lam_im, log_dt, dlbr, dlbi, dfr, dfi):
    def body(lr_ref, li_ref, ldt_ref, a_ref, b_ref, c_ref, d_ref, dlr_ref, dli_ref, dldt_ref):
        _, vjp = jax.vjp(_disc, lr_ref[...], li_ref[...], ldt_ref[...])
        dlr_ref[...], dli_ref[...], dldt = vjp((a_ref[...], b_ref[...], c_ref[...], d_ref[...]))
        dldt_ref[...] = _transpose_exact(dldt)

    return pl.pallas_call(body, name="ssm_param_bwd_lam", out_shape=[S((NG, NP), f32), S((NG, NP), f32), S((1, NG), f32)])(
        lam_re, lam_im, log_dt, dlbr, dlbi, dfr, dfi)


def _adam(w, g, m, v):
    m = ADAM_B1 * m + (1.0 - ADAM_B1) * g
    v = ADAM_B2 * v + (1.0 - ADAM_B2) * (g * g)
    m_hat = m / (1.0 - ADAM_B1 ** ADAM_STEP)
    v_hat = v / (1.0 - ADAM_B2 ** ADAM_STEP)
    return -ADAM_LR * (m_hat / (jnp.sqrt(v_hat) + ADAM_EPS) + ADAM_WD * w), m, v


def adam_update(w, m, v, contrib, name, rows_per_block=None, transposed=False):
    _, R, C = w.shape
    n = contrib.shape[0]
    tr = min(rows_per_block or R, R)
    assert not transposed or tr == R

    def body(w_ref, m_ref, v_ref, c_ref, g_ref, d_ref, nm_ref, nv_ref):
        g = c_ref[0].astype(f32)
        for k in range(1, n):
            g = g + c_ref[k].astype(f32)
        if transposed:
            g = g.T
        g_ref[0] = g
        d_ref[0], nm_ref[0], nv_ref[0] = _adam(w_ref[0], g, m_ref[0], v_ref[0])

    blk = pl.BlockSpec((1, tr, C), lambda i: (0, i, 0))
    cblk = pl.BlockSpec((n, C, R), lambda i: (0, 0, 0)) if transposed else pl.BlockSpec((n, tr, C), lambda i: (0, i, 0))
    return pl.pallas_call(
        body, name=name, grid=(R // tr,), in_specs=[blk, blk, blk, cblk],
        out_specs=[blk] * 4, out_shape=[S((1, R, C), f32)] * 4, compiler_params=_cp(("parallel",), VMEM_LIMIT),
    )(w, m, v, contrib)


_ROWVEC = (("b_in", IN_COLS), ("ssm_d", W), ("glu_b", W), ("ln1_g", D), ("ln1_b", D), ("ln2_g", D), ("ln2_b", D))
_PACK = {}
_r = 0
for _n, _k in _ROWVEC:
    _PACK[_n] = _r
    _r += _k // LANE
for _n, _rows in (("ssm_lambda_re", NG), ("ssm_lambda_im", NG), ("ssm_log_dt", 8), ("ssm_b_re", GC * (NG * NP // LANE)),
                  ("ssm_b_im", GC * (NG * NP // LANE)), ("ssm_c_re", NG * GC), ("ssm_c_im", NG * GC), ("conv_w", 16)):
    _PACK[_n] = _r
    _r += _rows
PACK_ROWS = _r
assert PACK_ROWS % 8 == 0
_SMALL = ("b_in", "ssm_lambda_re", "ssm_lambda_im", "ssm_log_dt", "ssm_b_re", "ssm_b_im", "ssm_c_re", "ssm_c_im",
          "ssm_d", "glu_b", "ln1_g", "ln1_b", "ln2_g", "ln2_b")


def pack_grads(su, shcb, sga, sgb, dd, dglu_b, dln1_g, dln1_b, dln2_g, dln2_b, dlam_re, dlam_im, dldt, dbrT, dbiT,
               dc_re, dc_im, dconv):
    nI = sga.shape[0]

    def body(su_ref, sh_ref, sga_ref, sgb_ref, dd_ref, gb_ref, l1g_ref, l1b_ref, l2g_ref, l2b_ref, lr_ref, li_ref, dt_ref,
             br_ref, bi_ref, cr_ref, ci_ref, cw_ref, o_ref):
        o_ref[...] = jnp.zeros_like(o_ref)

        def put_row(name, v):
            r0 = _PACK[name]
            for i in range(v.shape[1] // LANE):
                o_ref[r0 + i:r0 + i + 1, :] = v[:, i * LANE:(i + 1) * LANE]

        ga, gb = sga_ref[0], sgb_ref[0]
        for i in range(1, nI):
            ga, gb = ga + sga_ref[i], gb + sgb_ref[i]
        put_row("b_in", jnp.concatenate([su_ref[...], sh_ref[0:1, :], sh_ref[1:2, :], sh_ref[2:3, :], ga, gb], axis=1))
        put_row("ssm_d", jnp.concatenate([dd_ref[k] for k in range(W // LANE)], axis=1))
        put_row("glu_b", gb_ref[...])
        put_row("ln1_g", l1g_ref[...])
        put_row("ln1_b", l1b_ref[...])
        put_row("ln2_g", l2g_ref[...])
        put_row("ln2_b", l2b_ref[...])
        o_ref[_PACK["ssm_lambda_re"]:_PACK["ssm_lambda_re"] + NG, 0:NP] = lr_ref[...]
        o_ref[_PACK["ssm_lambda_im"]:_PACK["ssm_lambda_im"] + NG, 0:NP] = li_ref[...]
        o_ref[_PACK["ssm_log_dt"]:_PACK["ssm_log_dt"] + 1, 0:NG] = dt_ref[...]
        for name, ref in (("ssm_b_re", br_ref), ("ssm_b_im", bi_ref)):
            for cb in range(NG * NP // LANE):
                o_ref[_PACK[name] + GC * cb:_PACK[name] + GC * (cb + 1), :] = ref[:, cb * LANE:(cb + 1) * LANE]
        o_ref[_PACK["ssm_c_re"]:_PACK["ssm_c_re"] + NG * GC, 0:NP] = cr_ref[...]
        o_ref[_PACK["ssm_c_im"]:_PACK["ssm_c_im"] + NG * GC, 0:NP] = ci_ref[...]
        for cb in range(W // LANE):
            o_ref[_PACK["conv_w"] + 3 * cb:_PACK["conv_w"] + 3 * cb + 3, :] = cw_ref[:, cb * LANE:(cb + 1) * LANE]

    return pl.pallas_call(body, name="pack_grads", out_shape=S((PACK_ROWS, LANE), f32))(
        su, shcb, sga, sgb, dd, dglu_b, dln1_g, dln1_b, dln2_g, dln2_b, dlam_re, dlam_im, dldt, dbrT, dbiT, dc_re, dc_im, dconv)


def adam_small(packed_all, params):
    names = list(_SMALL) + ["conv_w"]
    flat = [a for n in names for a in params[n]]

    def body(*refs):
        p_ref = refs[0]
        ins = refs[1:1 + 3 * len(names)]
        outs = refs[1 + 3 * len(names):-1]
        g_ref = refs[-1]
        g_all = p_ref[0]
        for k in range(1, NDEV):
            g_all = g_all + p_ref[k]
        g_ref[...] = g_all

        def rows(name, r0, n, lanes=LANE):
            return g_ref[_PACK[name] + r0:_PACK[name] + r0 + n, 0:lanes]

        def grad_of(name, shape):
            if name in dict(_ROWVEC):
                return jnp.concatenate([rows(name, i, 1) for i in range(dict(_ROWVEC)[name] // LANE)], axis=1)
            if name in ("ssm_lambda_re", "ssm_lambda_im"):
                return rows(name, 0, NG, NP)[None]
            if name == "ssm_log_dt":
                return rows(name, 0, 1, NG)
            if name in ("ssm_b_re", "ssm_b_im"):
                gT = jnp.concatenate([rows(name, GC * cb, GC) for cb in range(NG * NP // LANE)], axis=1)
                return _transpose_exact(gT).reshape(1, NG, NP, GC)
            if name in ("ssm_c_re", "ssm_c_im"):
                return rows(name, 0, NG * GC, NP).reshape(1, NG, GC, NP)
            full = jnp.concatenate([rows("conv_w", 3 * cb, 3) for cb in range(W // LANE)], axis=1)
            x, y, c = _coords()
            col0 = (4 * x + 2 * y + c) * (W // NDEV)
            sel = (lax.broadcasted_iota(jnp.int32, (W, W // NDEV), 0)
                   == lax.broadcasted_iota(jnp.int32, (W, W // NDEV), 1) + col0).astype(f32)
            return jnp.dot(full, sel, precision=HIGHEST, preferred_element_type=f32)[None]

        for i, name in enumerate(names):
            w_ref, m_ref, v_ref = ins[3 * i:3 * i + 3]
            g = grad_of(name, w_ref.shape)
            d, m, v = _adam(w_ref[...], g, m_ref[...], v_ref[...])
            outs[4 * i][...] = g
            outs[4 * i + 1][...] = d
            outs[4 * i + 2][...] = m
            outs[4 * i + 3][...] = v

    out_shape = [S(params[n][0].shape, f32) for n in names for _ in range(4)]
    res = pl.pallas_call(body, name="adam_small", out_shape=out_shape, scratch_shapes=[pltpu.VMEM((PACK_ROWS, LANE), f32)],
                         compiler_params=_cp(None, VMEM_LIMIT))(packed_all, *flat)
    return {n: res[4 * i:4 * i + 4] for i, n in enumerate(names)}


def _block_diag(wgt):
    eye = jnp.eye(8, dtype=wgt.dtype)
    out = wgt[:, :, :, None, :] * eye[None, :, None, :, None]
    return out.reshape(4, 8 * wgt.shape[2], 8 * wgt.shape[3])


def _diag_blocks(m, a, b):
    m = m.reshape(4, 8, a, 8, b)
    idx = jnp.arange(8)
    return m[:, idx, :, idx, :].transpose(1, 0, 2, 3)


def kernel(x, w_in, b_in, ssm_lambda_re, ssm_lambda_im, ssm_log_dt, ssm_b_re, ssm_b_im, ssm_c_re, ssm_c_im, ssm_d, glu_w, glu_b, w_ssm_out, conv_w, w_conv_out, w_o, ln1_g, ln1_b, w_gate, w_up, w_down, ln2_g, ln2_b, loss_target, m_w_in, m_b_in, m_ssm_lambda_re, m_ssm_lambda_im, m_ssm_log_dt, m_ssm_b_re, m_ssm_b_im, m_ssm_c_re, m_ssm_c_im, m_ssm_d, m_glu_w, m_glu_b, m_w_ssm_out, m_conv_w, m_w_conv_out, m_w_o, m_ln1_g, m_ln1_b, m_w_gate, m_w_up, m_w_down, m_ln2_g, m_ln2_b, v_w_in, v_b_in, v_ssm_lambda_re, v_ssm_lambda_im, v_ssm_log_dt, v_ssm_b_re, v_ssm_b_im, v_ssm_c_re, v_ssm_c_im, v_ssm_d, v_glu_w, v_glu_b, v_w_ssm_out, v_conv_w, v_w_conv_out, v_w_o, v_ln1_g, v_ln1_b, v_w_gate, v_w_up, v_w_down, v_ln2_g, v_ln2_b):
    given = dict(locals())
    xs = x[0]
    target = loss_target[0]

    win_s, glu_s, wso_s, wco_s, wo_s, wgT_s, wuT_s, wd_s = prep_weights(w_in, glu_w, w_ssm_out, w_conv_out, w_o, w_gate, w_up, w_down)
    win_g, conv_g = run_plan(GatherPlan([win_s, conv_w[0]]), "gather_w_in")
    conv_f = conv_g.transpose(1, 0, 2).reshape(3, W)

    lam_re, lam_im = ssm_lambda_re[0], ssm_lambda_im[0]
    ldt = ssm_log_dt[0].reshape(NG, 1)
    lbr, lbi, fr, fi = ssm_disc(lam_re, lam_im, ldt)
    br2, bi2 = ssm_b_re[0].reshape(NG * NP, GC), ssm_b_im[0].reshape(NG * NP, GC)
    fr2, fi2 = fr.reshape(NG * NP, 1), fi.reshape(NG * NP, 1)
    bbr, bbi = ssm_bbar(fr2, fi2, br2, bi2)
    bb_t = lambda b: b.reshape(4, 8, NP, GC).transpose(0, 1, 3, 2)
    wb = jnp.concatenate([_block_diag(bb_t(bbr)), _block_diag(bb_t(bbi))], axis=2)
    c_t = lambda c: c.reshape(4, 8, GC, NP).transpose(0, 1, 3, 2)
    wc = jnp.concatenate([_block_diag(c_t(ssm_c_re[0])), -_block_diag(c_t(ssm_c_im[0]))], axis=1)
    wbT, wcT = wb.transpose(0, 2, 1), wc.transpose(0, 2, 1)
    wb, wc, wbT, wcT = wb.astype(bf16), wc.astype(bf16), wbT.astype(bf16), wcT.astype(bf16)
    lbr_s, lbi_s = lbr.reshape(4, 1, SW), lbi.reshape(4, 1, SW)
    dsk = ssm_d[0].reshape(4, 1, LANE)

    (proj, xb), (glu_g, wso_g, wco_g, wo_g) = in_proj(xs, win_g, b_in, GatherPlan([glu_s, wso_s, wco_s, wo_s]))
    glu_f, wo_f = glu_g.reshape(W, W), wo_g.reshape(D, D)
    u_p = to_perm(proj, 0, "perm_u")
    (y_p,), (wgT_g, wuT_g) = ssm_fwd(u_p, wb, wc, lbr_s, lbi_s, dsk, GatherPlan([wgT_s, wuT_s]))
    wgT, wuT = wgT_g.reshape(F, D), wuT_g.reshape(F, D)
    yn, _ = from_perm(y_p, "unperm_y")
    ya = glu_fwd(yn, glu_f, glu_b)
    yb = conv_fwd(proj, conv_f)
    (merged,), (wd_g,) = merge_fwd(ya, yb, wso_g, wco_g, proj, GatherPlan([wd_s]))
    wd_f = wd_g.reshape(F, D)
    r1, x1b = mix_ln1(merged, wo_f, xs, ln1_g, ln1_b)
    gate, up, hid = gate_up(x1b, wgT, wuT)
    dr2, dffn, sqerr, dln2_g, dln2_b = down_loss(hid, wd_f, r1, ln1_g, ln1_b, ln2_g, ln2_b, target)
    loss = lax.psum(0.5 * sqerr[0, 0], ("x", "y", "c"))

    dgate, dup = ffn_bwd_act(dffn, wd_f, gate, up)
    dwd = mm_tn_rows(hid, dffn, "grad_w_down").reshape(NDEV, FS, D)
    dwgT = mm_tn_rows(dgate, x1b, "grad_w_gate").reshape(NDEV, FS, D)
    dwuT = mm_tn_rows(dup, x1b, "grad_w_up").reshape(NDEV, FS, D)
    (dr1, dmix, dln1_g, dln1_b), (r_wd,) = ffn_bwd_x(dgate, dup, wgT, wuT, dr2, r1, ln1_g, ScatterPlan([dwd]))
    (dYA, dYB, dga, dgb, sga, sgb), (r_wgT,) = merge_bwd(dmix, wo_f, ya, yb, wso_g, wco_g, proj, ScatterPlan([dwgT]))
    dwo = mm_tn_rows(merged, dmix, "grad_w_o").reshape(NDEV, D // NDEV, D)
    dya = branch_bwd_x(dYA, wso_g, "ssm_out_bwd_x")
    dyb = branch_bwd_x(dYB, wco_g, "conv_out_bwd_x")
    dwso = branch_bwd_w(ya, dYA, "grad_w_ssm_out")
    dwco = branch_bwd_w(yb, dYB, "grad_w_conv_out")
    dyn, dsp, gb, dglu_b = glu_bwd(yn, dya, glu_f, glu_b)
    dglu = mm_tn_rows(gb, dsp, "grad_glu_w").reshape(NDEV, W // NDEV, W)
    dh, dcg, dbg, dconv, shcb = conv_bwd(proj, dyb, conv_f)
    dwin = mm_tn(xb, dgb, "grad_w_in_gb", tk=512, block0=6, nblocks=NDEV)
    dwin = mm_tn(xb, dga, "grad_w_in_ga", tk=512, block0=4, into=dwin)
    dwin = mm_tn(xb, dbg, "grad_w_in_bg", tk=512, block0=3, into=dwin)
    dwin = mm_tn(xb, dcg, "grad_w_in_cg", tk=512, block0=2, into=dwin)
    dwin = mm_tn(xb, dh, "grad_w_in_h", tk=512, block0=1, into=dwin)
    dy_p = to_perm(dyn, 0, "perm_dy")
    (du_p, dwb, dwcT, dlbr_s, dlbi_s, dd), (r_wuT, r_wo, r_wso, r_wco, r_glu) = ssm_bwd(
        u_p, dy_p, wb, wbT, wcT, lbr_s, lbi_s, dsk, ScatterPlan([dwuT, dwo, dwso, dwco, dglu]))
    du, su = from_perm(du_p, "unperm_du", bf16)
    dwin = mm_tn(xb, du, "grad_w_in_u", tk=512, block0=0, into=dwin)

    dbb = lambda m: _diag_blocks(m, GC, NP).transpose(0, 1, 3, 2).reshape(NG * NP, GC)
    dbrT, dbiT, dfr2, dfi2 = ssm_param_bwd_b(fr2, fi2, br2, bi2, dbb(dwb[:, :, :SW]), dbb(dwb[:, :, SW:]))
    dlam_re, dlam_im, dldt = ssm_param_bwd_lam(lam_re, lam_im, ldt, dlbr_s.reshape(NG, NP), dlbi_s.reshape(NG, NP),
                                               dfr2.reshape(NG, NP), dfi2.reshape(NG, NP))
    dc = lambda m: _diag_blocks(m, GC, NP).reshape(NG * GC, NP)
    packed = pack_grads(su, shcb, sga, sgb, dd, dglu_b, dln1_g, dln1_b, dln2_g, dln2_b, dlam_re, dlam_im, dldt, dbrT, dbiT,
                        dc(dwcT[:, :, :SW]), -dc(dwcT[:, :, SW:]), dconv)

    parts = [(du, 0), (dh, 0), (dcg, 0), (dbg, 0), (dga, 0), (dga, 1), (dgb, 0), (dgb, 1)]
    (grad_x,), (r_win, small_all) = in_proj_bwd_x(parts, win_g, dr1, Plans([ScatterPlan([dwin]), GatherPlan([packed])]))

    out = {}

    def put(name, res):
        out["grad_" + name], out["delta_" + name], out["new_m_" + name], out["new_v_" + name] = res

    put("w_in", adam_update(w_in, m_w_in, v_w_in, r_win, "adam_w_in", 256))
    put("glu_w", adam_update(glu_w, m_glu_w, v_glu_w, r_glu, "adam_glu_w"))
    put("w_ssm_out", adam_update(w_ssm_out, m_w_ssm_out, v_w_ssm_out, r_wso, "adam_w_ssm_out"))
    put("w_conv_out", adam_update(w_conv_out, m_w_conv_out, v_w_conv_out, r_wco, "adam_w_conv_out"))
    put("w_o", adam_update(w_o, m_w_o, v_w_o, r_wo, "adam_w_o"))
    put("w_down", adam_update(w_down, m_w_down, v_w_down, r_wd, "adam_w_down", 176))
    put("w_gate", adam_update(w_gate, m_w_gate, v_w_gate, r_wgT, "adam_w_gate", transposed=True))
    put("w_up", adam_update(w_up, m_w_up, v_w_up, r_wuT, "adam_w_up", transposed=True))
    small = adam_small(small_all, {n: (given[n], given["m_" + n], given["v_" + n]) for n in list(_SMALL) + ["conv_w"]})
    for n, res in small.items():
        put(n, res)

    names = ["w_in", "b_in", "ssm_lambda_re", "ssm_lambda_im", "ssm_log_dt", "ssm_b_re", "ssm_b_im", "ssm_c_re", "ssm_c_im",
             "ssm_d", "glu_w", "glu_b", "w_ssm_out", "conv_w", "w_conv_out", "w_o", "ln1_g", "ln1_b", "w_gate", "w_up",
             "w_down", "ln2_g", "ln2_b"]
    return (loss, grad_x[None], *[out[p + n] for p in ("grad_", "delta_", "new_m_", "new_v_") for n in names])
```

```python
import functools
import math

import jax
import jax.numpy as jnp
from jax import lax
from jax.experimental import pallas as pl
from jax.experimental.pallas import tpu as pltpu

f32, bf16 = jnp.float32, jnp.bfloat16
S = jax.ShapeDtypeStruct
MESH = pl.DeviceIdType.MESH
HIGHEST = lax.Precision.HIGHEST

D = 1024
W = 512
NG, NP, GC = 32, 64, 16
F = 2816
NDEV = 8
FS = F // NDEV
IN_COLS = 8 * W
ALPHA = 2.0 ** 0.25
LN_EPS = 1e-5
ADAM_LR, ADAM_B1, ADAM_B2, ADAM_EPS, ADAM_WD, ADAM_STEP = 0.001, 0.9, 0.999, 1e-08, 0.01, 10
NC = 32
LANE = 128
SW = 4 * LANE
VMEM_LIMIT = 56 * 1024 * 1024
GRAD_DT = bf16
ANY = pl.BlockSpec(memory_space=pl.ANY)


def _cp(sem=None, vmem=None):
    return pltpu.CompilerParams(dimension_semantics=sem, vmem_limit_bytes=vmem)


def _resident(shape):
    return pl.BlockSpec(shape, lambda i: (0,) * len(shape), pipeline_mode=pl.Buffered(1))


def _dot(a, b):
    return jnp.dot(a, b, preferred_element_type=f32)


def _dot_nt(a, b):
    return lax.dot_general(a, b, (((1,), (1,)), ((), ())), preferred_element_type=f32)


def _dot_tn(a, b):
    return lax.dot_general(a, b, (((0,), (0,)), ((), ())), preferred_element_type=f32)


def _eye(n):
    return (lax.broadcasted_iota(jnp.int32, (n, n), 0) == lax.broadcasted_iota(jnp.int32, (n, n), 1)).astype(f32)


def _transpose_exact(a):
    return lax.dot_general(a, _eye(a.shape[0]), (((0,), (0,)), ((), ())), precision=HIGHEST, preferred_element_type=f32)


def _sigmoid(x):
    return 1.0 / (1.0 + jnp.exp(-x))


_GK = math.sqrt(2.0 / math.pi)


def _gelu(x):
    return 0.5 * x * (1.0 + jnp.tanh(_GK * (x + 0.044715 * x * x * x)))


def _gelu_grad(x):
    th = jnp.tanh(_GK * (x + 0.044715 * x * x * x))
    return 0.5 * (1.0 + th) + 0.5 * x * (1.0 - th * th) * _GK * (1.0 + 3.0 * 0.044715 * x * x)


def _ln_stats(r):
    mu = jnp.mean(r, axis=-1, keepdims=True)
    xc = r - mu
    var = jnp.mean(xc * xc, axis=-1, keepdims=True)
    rstd = lax.rsqrt(var + LN_EPS)
    return xc * rstd, rstd


def _ln_bwd(dy, xhat, rstd, g):
    dxh = dy * g
    m1 = jnp.mean(dxh, axis=-1, keepdims=True)
    m2 = jnp.mean(dxh * xhat, axis=-1, keepdims=True)
    return rstd * (dxh - m1 - xhat * m2)


def _coords():
    return lax.axis_index("x"), lax.axis_index("y"), lax.axis_index("c")


class GatherPlan:
    def __init__(self, arrs):
        self.inputs = list(arrs)
        n = len(arrs)
        self.out_shape = [S((NDEV,) + a.shape, a.dtype) for a in arrs]
        self.sems = [pltpu.SemaphoreType.DMA((n, 7)), pltpu.SemaphoreType.DMA((n, 7)), pltpu.SemaphoreType.DMA((n,))]

    def _parts(self, ins, outs, sems):
        n = len(ins)
        send_sems, recv_sems, loc_sems = sems
        x, y, c = _coords()
        me, sib = (x, y, c), (x, y, 1 - c)
        chips = [(1 - x, y), (x, 1 - y), (1 - x, 1 - y)]

        def slot(a, dev):
            return outs[a].at[4 * dev[0] + 2 * dev[1] + dev[2]]

        def copy(a, k, block, to, src=None):
            return pltpu.make_async_remote_copy(
                src_ref=slot(a, block) if src is None else src, dst_ref=slot(a, block),
                send_sem=send_sems.at[a, k], recv_sem=recv_sems.at[a, k], device_id=to, device_id_type=MESH)

        each = [(j, chip, a) for j, chip in enumerate(chips) for a in range(n)]
        return dict(
            mine=lambda: [pltpu.make_async_copy(ins[a], slot(a, me), loc_sems.at[a]) for a in range(n)],
            first=lambda: ([copy(a, 0, me, sib, src=ins[a]) for a in range(n)]
                           + [copy(a, 1 + j, me, (*chip, c), src=ins[a]) for j, chip, a in each]),
            landed=lambda: [copy(a, 1 + j, (*chip, c), me) for j, chip, a in each],
            passed=lambda: [copy(a, 4 + j, (*chip, c), sib) for j, chip, a in each],
            from_sib=lambda: ([copy(a, 0, sib, me) for a in range(n)]
                              + [copy(a, 4 + j, (*chip, 1 - c), me) for j, chip, a in each]))

    def start(self, ins, outs, sems):
        p = self._parts(ins, outs, sems)
        for cp in p["mine"]() + p["first"]():
            cp.start()

    def forward(self, ins, outs, sems):
        p = self._parts(ins, outs, sems)
        for got, fwd in zip(p["landed"](), p["passed"]()):
            got.wait_recv()
            fwd.start()

    def finish(self, ins, outs, sems):
        p = self._parts(ins, outs, sems)
        for cp in p["from_sib"]():
            cp.wait_recv()
        for cp in p["first"]() + p["passed"]():
            cp.wait_send()
        for cp in p["mine"]():
            cp.wait()


class ScatterPlan:
    def __init__(self, gs):
        self.inputs = list(gs)
        n = len(gs)
        self.out_shape = [S(g.shape, g.dtype) for g in gs]
        self.sems = [pltpu.SemaphoreType.DMA((n, 7)), pltpu.SemaphoreType.DMA((n, 7)), pltpu.SemaphoreType.DMA((n,))]

    def _copies(self, ins, outs, sems):
        n = len(ins)
        send_sems, recv_sems, loc_sems = sems
        x, y, c = _coords()
        me = 4 * x + 2 * y + c
        copies = [pltpu.make_async_copy(ins[a].at[me], outs[a].at[me], loc_sems.at[a]) for a in range(n)]
        for m in range(1, NDEV):
            px = 1 - x if m & 4 else x
            py = 1 - y if m & 2 else y
            pc = 1 - c if m & 1 else c
            for a in range(n):
                copies.append(pltpu.make_async_remote_copy(
                    src_ref=ins[a].at[4 * px + 2 * py + pc], dst_ref=outs[a].at[me],
                    send_sem=send_sems.at[a, m - 1], recv_sem=recv_sems.at[a, m - 1],
                    device_id=(px, py, pc), device_id_type=MESH))
        return copies

    def start(self, ins, outs, sems):
        for cp in self._copies(ins, outs, sems):
            cp.start()

    def forward(self, ins, outs, sems):
        pass

    def finish(self, ins, outs, sems):
        for cp in self._copies(ins, outs, sems):
            cp.wait()


class Plans:
    def __init__(self, plans):
        self.plans = plans
        self.inputs = [a for p in plans for a in p.inputs]
        self.out_shape = [s for p in plans for s in p.out_shape]
        self.sems = [s for p in plans for s in p.sems]

    def _each(self, what, ins, outs, sems):
        i = o = s = 0
        for p in self.plans:
            ni, no, ns = len(p.inputs), len(p.out_shape), len(p.sems)
            getattr(p, what)(ins[i:i + ni], outs[o:o + no], sems[s:s + ns])
            i, o, s = i + ni, o + no, s + ns

    def start(self, ins, outs, sems):
        self._each("start", ins, outs, sems)

    def forward(self, ins, outs, sems):
        self._each("forward", ins, outs, sems)

    def finish(self, ins, outs, sems):
        self._each("finish", ins, outs, sems)


def _call(body, args, *, name, grid, in_specs, out_specs, out_shape, scratch=(), sem=None, vmem=None, plan=None,
          aliases=None):
    aliases = aliases or {}
    if plan is None:
        outs = pl.pallas_call(body, name=name, grid=grid, in_specs=list(in_specs), out_specs=list(out_specs),
                              out_shape=list(out_shape), scratch_shapes=list(scratch), input_output_aliases=aliases,
                              compiler_params=_cp(sem, vmem))(*args)
        return list(outs), []
    ni, no, ns = len(in_specs), len(out_specs), len(scratch)
    pi, po = len(plan.inputs), len(plan.out_shape)

    def wrapped(*refs):
        main_in, p_in = refs[:ni], refs[ni:ni + pi]
        main_out, p_out = refs[ni + pi:ni + pi + no], refs[ni + pi + no:ni + pi + no + po]
        main_scr, p_sems = refs[ni + pi + no + po:ni + pi + no + po + ns], refs[ni + pi + no + po + ns:]
        ids = [pl.program_id(d) for d in range(len(grid))]
        first = functools.reduce(jnp.logical_and, [i == 0 for i in ids])
        last = functools.reduce(jnp.logical_and, [i == g - 1 for i, g in zip(ids, grid)])

        @pl.when(first)
        def _():
            plan.start(p_in, p_out, p_sems)

        @pl.when(last)
        def _():
            plan.forward(p_in, p_out, p_sems)

        body(*main_in, *main_out, *main_scr)

        @pl.when(last)
        def _():
            plan.finish(p_in, p_out, p_sems)

    outs = pl.pallas_call(
        wrapped, name=name, grid=grid, in_specs=list(in_specs) + [ANY] * pi, out_specs=list(out_specs) + [ANY] * po,
        out_shape=list(out_shape) + list(plan.out_shape), scratch_shapes=list(scratch) + list(plan.sems),
        input_output_aliases=aliases, compiler_params=_cp(("arbitrary",) * len(grid), vmem),
    )(*args, *plan.inputs)
    return list(outs[:no]), list(outs[no:])


def run_plan(plan, name):
    def body(*refs):
        ins, outs, sems = refs[:len(plan.inputs)], refs[len(plan.inputs):len(plan.inputs) + len(plan.out_shape)], \
            refs[len(plan.inputs) + len(plan.out_shape):]
        plan.start(ins, outs, sems)
        plan.forward(ins, outs, sems)
        plan.finish(ins, outs, sems)

    return pl.pallas_call(body, name=name, in_specs=[ANY] * len(plan.inputs), out_specs=[ANY] * len(plan.out_shape),
                          out_shape=list(plan.out_shape), scratch_shapes=list(plan.sems))(*plan.inputs)


def mm_tn(a, b, name, tk=256, tn=512, into=None, block0=0, nblocks=None):
    T, K = a.shape
    N = b.shape[1]
    tk, tn = min(tk, K), min(tn, N)
    nblocks = nblocks or (N // tn if into is None else into.shape[0])

    def body(a_ref, b_ref, *rest):
        rest[-1][...] = _dot_tn(a_ref[...], b_ref[...]).astype(GRAD_DT)

    args, in_specs, aliases = [a, b], [pl.BlockSpec((T, tk), lambda i, j: (0, i)), pl.BlockSpec((T, tn), lambda i, j: (0, j))], {}
    if into is not None:
        args.append(into)
        in_specs.append(ANY)
        aliases = {2: 0}
    (out,), _ = _call(body, args, name=name, grid=(K // tk, N // tn), in_specs=in_specs,
                      out_specs=[pl.BlockSpec((None, tk, tn), lambda i, j: (block0 + j, i, 0))],
                      out_shape=[S((nblocks, K, tn), GRAD_DT)], sem=("parallel", "parallel"), vmem=VMEM_LIMIT, aliases=aliases)
    return out


def mm_tn_rows(a, b, name, tk=256, tn=512):
    T, K = a.shape
    N = b.shape[1]
    tk, tn = min(tk, K), min(tn, N)

    def body(a_ref, b_ref, o_ref):
        o_ref[...] = _dot_tn(a_ref[...], b_ref[...]).astype(GRAD_DT)

    (out,), _ = _call(body, [a, b], name=name, grid=(K // tk, N // tn),
                      in_specs=[pl.BlockSpec((T, tk), lambda i, j: (0, i)), pl.BlockSpec((T, tn), lambda i, j: (0, j))],
                      out_specs=[pl.BlockSpec((tk, tn), lambda i, j: (i, j))], out_shape=[S((K, N), GRAD_DT)],
                      sem=("parallel", "parallel"), vmem=VMEM_LIMIT)
    return out


def prep_weights(ws):
    def body(*refs):
        for i in range(len(ws)):
            refs[len(ws) + i][...] = refs[i][...].astype(bf16)

    return pl.pallas_call(body, name="prep_weights", out_shape=[S(w.shape, bf16) for w in ws],
                          compiler_params=_cp(None, VMEM_LIMIT))(*ws)


def in_proj(x, win_g, b_in, plan):
    T = x.shape[0]
    tm = min(512, T)

    def body(x_ref, w_ref, b_ref, o_ref, xb_ref):
        xb = x_ref[...].astype(bf16)
        xb_ref[...] = xb
        for k in range(NDEV):
            cs = slice(k * W, (k + 1) * W)
            o_ref[:, cs] = _dot(xb, w_ref[k]) + b_ref[:, cs]

    return _call(
        body, [x, win_g, b_in], name="in_proj", grid=(T // tm,),
        in_specs=[pl.BlockSpec((tm, D), lambda i: (i, 0)), _resident((NDEV, D, W)), _resident((1, IN_COLS))],
        out_specs=[pl.BlockSpec((tm, IN_COLS), lambda i: (i, 0)), pl.BlockSpec((tm, D), lambda i: (i, 0))],
        out_shape=[S((T, IN_COLS), f32), S((T, D), bf16)], vmem=VMEM_LIMIT, plan=plan)


def to_perm(a, cb0, name):
    T = a.shape[0]
    L = T // NC

    def body(a_ref, o_ref):
        def step(j, carry):
            for q in range(NC // 8):
                o_ref[pl.ds(pl.multiple_of(j * NC, NC) + 8 * q, 8), :] = a_ref[pl.ds(q * 8 * L + j, 8, stride=L), :]
            return carry

        lax.fori_loop(0, L, step, 0)

    return pl.pallas_call(
        body, name=name, grid=(W // LANE,),
        in_specs=[pl.BlockSpec((T, LANE), lambda k: (0, cb0 + k))], out_specs=pl.BlockSpec((T, LANE), lambda k: (0, k)),
        out_shape=S((T, W), f32), compiler_params=_cp(("parallel",), VMEM_LIMIT),
    )(a)


def from_perm(a, name, out_dtype=f32):
    T = a.shape[0]
    L = T // NC

    def body(a_ref, o_ref, s_ref):
        def step(i, acc):
            c, jb = i // (L // 16), i % (L // 16)
            t0 = a_ref[pl.ds(jb * 16 * NC + c, 8, stride=NC), :]
            t1 = a_ref[pl.ds((jb * 16 + 8) * NC + c, 8, stride=NC), :]
            o_ref[pl.ds(pl.multiple_of(i * 16, 16), 16), :] = jnp.concatenate([t0, t1], axis=0).astype(out_dtype)
            return acc + t0 + t1

        acc = lax.fori_loop(0, T // 16, step, jnp.zeros((8, LANE), f32))
        s_ref[...] = jnp.sum(acc, axis=0, keepdims=True)

    return pl.pallas_call(
        body, name=name, grid=(W // LANE,),
        in_specs=[pl.BlockSpec((T, LANE), lambda k: (0, k))],
        out_specs=[pl.BlockSpec((T, LANE), lambda k: (0, k)), pl.BlockSpec((1, LANE), lambda k: (0, k))],
        out_shape=[S((T, W), out_dtype), S((1, W), f32)], compiler_params=_cp(("parallel",), VMEM_LIMIT),
    )(a)


def _disc(lr, li, ldt):
    dt = jnp.exp(ldt)
    mag = jnp.exp(lr * dt)
    lbr = mag * jnp.cos(li * dt)
    lbi = mag * jnp.sin(li * dt)
    den = lr * lr + li * li
    nr = lbr - 1.0
    return lbr, lbi, (nr * lr + lbi * li) / den, (lbi * lr - nr * li) / den


def _per_channel(f):
    return jnp.broadcast_to(f[:, None, :], (NG, GC, NP)).reshape(NG * GC, NP)


def ssm_params(lam_re, lam_im, log_dt, br, bi):
    def body(lr_ref, li_ref, ldt_ref, br_ref, bi_ref, lbr_ref, lbi_ref, fr_ref, fi_ref, bbr_ref, bbi_ref):
        lbr, lbi, fr, fi = _disc(lr_ref[...], li_ref[...], ldt_ref[...])
        lbr_ref[...], lbi_ref[...], fr_ref[...], fi_ref[...] = lbr, lbi, fr, fi
        fr_, fi_, br_, bi_ = _per_channel(fr), _per_channel(fi), br_ref[...], bi_ref[...]
        bbr_ref[...] = fr_ * br_ - fi_ * bi_
        bbi_ref[...] = fr_ * bi_ + fi_ * br_

    return pl.pallas_call(body, name="ssm_params", out_shape=[S((NG, NP), f32)] * 4 + [S((NG * GC, NP), f32)] * 2)(
        lam_re, lam_im, log_dt, br, bi)


def _scan_body(T):
    L = T // NC
    RB = min(512, T)
    nsq = int(round(math.log2(L)))
    assert 2 ** nsq == L and T % RB == 0 and L % 16 == 0

    def rows(i):
        return pl.ds(pl.multiple_of(i * RB, RB), RB)

    def tile(j):
        return pl.ds(pl.multiple_of(j * NC, NC), NC)

    def forward_states(u_ref, wb_ref, lbr_ref, lbi_ref, sre, sim, ere, eim):
        def bproj(i, carry):
            bu = _dot(u_ref[rows(i), :].astype(bf16), wb_ref[...])
            sre[rows(i), :] = bu[:, :SW]
            sim[rows(i), :] = bu[:, SW:]
            return carry

        lax.fori_loop(0, T // RB, bproj, 0)
        for lb in range(SW // LANE):
            ls = slice(lb * LANE, (lb + 1) * LANE)
            ar = jnp.broadcast_to(lbr_ref[:, ls], (NC, LANE))
            ai = jnp.broadcast_to(lbi_ref[:, ls], (NC, LANE))

            def step(j, carry):
                xr, xi = carry
                nr = ar * xr - ai * xi + sre[tile(j), ls]
                ni = ar * xi + ai * xr + sim[tile(j), ls]
                sre[tile(j), ls] = nr
                sim[tile(j), ls] = ni
                return nr, ni

            zero = jnp.zeros((NC, LANE), f32)
            lax.fori_loop(0, L, step, (zero, zero))
            pr, pi = lbr_ref[:, ls], lbi_ref[:, ls]
            for _ in range(nsq):
                pr, pi = pr * pr - pi * pi, 2.0 * pr * pi
            er = jnp.zeros((1, LANE), f32)
            ei = er
            ere[0:1, ls] = er
            eim[0:1, ls] = ei
            base = (L - 1) * NC
            for c in range(1, NC):
                lr_ = sre[base + c - 1:base + c, ls]
                li_ = sim[base + c - 1:base + c, ls]
                er, ei = lr_ + pr * er - pi * ei, li_ + pr * ei + pi * er
                ere[c:c + 1, ls] = er
                eim[c:c + 1, ls] = ei
            e_r, e_i = ere[:, ls], eim[:, ls]

            def fix(j, carry):
                pwr, pwi = carry
                sre[tile(j), ls] += pwr * e_r - pwi * e_i
                sim[tile(j), ls] += pwr * e_i + pwi * e_r
                return pwr * ar - pwi * ai, pwr * ai + pwi * ar

            lax.fori_loop(0, L, fix, (ar, ai))

    return L, RB, nsq, rows, tile, forward_states


def ssm_fwd(u_p, wb, wc, lbr, lbi, dsk, plan):
    T = u_p.shape[0]
    L, RB, nsq, rows, tile, forward_states = _scan_body(T)

    def body(u_ref, wb_ref, wc_ref, lbr_ref, lbi_ref, d_ref, y_ref, sre, sim, ere, eim):
        forward_states(u_ref, wb_ref, lbr_ref, lbi_ref, sre, sim, ere, eim)

        def cproj(i, carry):
            y = _dot(sre[rows(i), :].astype(bf16), wc_ref[0:SW, :]) + _dot(sim[rows(i), :].astype(bf16), wc_ref[SW:, :])
            y_ref[rows(i), :] = y + d_ref[...] * u_ref[rows(i), :]
            return carry

        lax.fori_loop(0, T // RB, cproj, 0)

    slab = pl.BlockSpec((T, LANE), lambda k: (0, k))
    return _call(
        body, [u_p, wb, wc, lbr, lbi, dsk], name="ssm_fwd", grid=(W // LANE,),
        in_specs=[slab, pl.BlockSpec((None, LANE, 2 * SW), lambda k: (k, 0, 0)),
                  pl.BlockSpec((None, 2 * SW, LANE), lambda k: (k, 0, 0)),
                  pl.BlockSpec((None, 1, SW), lambda k: (k, 0, 0)), pl.BlockSpec((None, 1, SW), lambda k: (k, 0, 0)),
                  pl.BlockSpec((None, 1, LANE), lambda k: (k, 0, 0))],
        out_specs=[slab], out_shape=[S((T, W), f32)],
        scratch=[pltpu.VMEM((T, SW), f32), pltpu.VMEM((T, SW), f32), pltpu.VMEM((NC, SW), f32), pltpu.VMEM((NC, SW), f32)],
        vmem=VMEM_LIMIT, plan=plan)


def ssm_bwd(u_p, dy_p, wb, wbT, wcT, lbr, lbi, dsk, plan):
    T = u_p.shape[0]
    L, RB, nsq, rows, tile, forward_states = _scan_body(T)

    def body(u_ref, dy_ref, wb_ref, wbT_ref, wcT_ref, lbr_ref, lbi_ref, d_ref,
             du_ref, dwb_ref, dwc_ref, dlr_ref, dli_ref, dd_ref, sre, sim, gre, gim, ere, eim):
        forward_states(u_ref, wb_ref, lbr_ref, lbi_ref, sre, sim, ere, eim)

        def dstate(i, carry):
            g = _dot(dy_ref[rows(i), :].astype(bf16), wcT_ref[...])
            gre[rows(i), :] = g[:, :SW]
            gim[rows(i), :] = g[:, SW:]
            return carry

        lax.fori_loop(0, T // RB, dstate, 0)
        row = lax.broadcasted_iota(jnp.int32, (NC, LANE), 0)
        for lb in range(SW // LANE):
            ls = slice(lb * LANE, (lb + 1) * LANE)
            ar = jnp.broadcast_to(lbr_ref[:, ls], (NC, LANE))
            ai = jnp.broadcast_to(lbi_ref[:, ls], (NC, LANE))

            def step(i, carry):
                gr, gi = carry
                j = L - 1 - i
                nr = ar * gr + ai * gi + gre[tile(j), ls]
                ni = ar * gi - ai * gr + gim[tile(j), ls]
                gre[tile(j), ls] = nr
                gim[tile(j), ls] = ni
                return nr, ni

            zero = jnp.zeros((NC, LANE), f32)
            lax.fori_loop(0, L, step, (zero, zero))
            pr, pi = lbr_ref[:, ls], -lbi_ref[:, ls]
            for _ in range(nsq):
                pr, pi = pr * pr - pi * pi, 2.0 * pr * pi
            er = jnp.zeros((1, LANE), f32)
            ei = er
            ere[NC - 1:NC, ls] = er
            eim[NC - 1:NC, ls] = ei
            for c in range(NC - 2, -1, -1):
                lr_ = gre[c + 1:c + 2, ls]
                li_ = gim[c + 1:c + 2, ls]
                er, ei = lr_ + pr * er - pi * ei, li_ + pr * ei + pi * er
                ere[c:c + 1, ls] = er
                eim[c:c + 1, ls] = ei
            e_r, e_i = ere[:, ls], eim[:, ls]

            def fixed(j, pwr, pwi):
                gr = gre[tile(j), ls] + pwr * e_r - pwi * e_i
                gi = gim[tile(j), ls] + pwr * e_i + pwi * e_r
                gre[tile(j), ls] = gr
                gim[tile(j), ls] = gi
                return gr, gi

            def fix(i, carry):
                pwr, pwi, accr, acci = carry
                j = L - 1 - i
                gr, gi = fixed(j, pwr, pwi)
                xr, xi = sre[tile(j - 1), ls], sim[tile(j - 1), ls]
                return (pwr * ar + pwi * ai, pwi * ar - pwr * ai,
                        accr + gr * xr + gi * xi, acci + gi * xr - gr * xi)

            pwr, pwi, accr, acci = lax.fori_loop(0, L - 1, fix, (ar, -ai, zero, zero))
            gr, gi = fixed(0, pwr, pwi)
            xr = jnp.where(row == 0, 0.0, pltpu.roll(sre[tile(L - 1), ls], 1, axis=0))
            xi = jnp.where(row == 0, 0.0, pltpu.roll(sim[tile(L - 1), ls], 1, axis=0))
            accr = accr + gr * xr + gi * xi
            acci = acci + gi * xr - gr * xi
            dlr_ref[:, ls] = jnp.sum(accr, axis=0, keepdims=True)
            dli_ref[:, ls] = jnp.sum(acci, axis=0, keepdims=True)

        dwb_ref[...] = jnp.zeros_like(dwb_ref)
        dwc_ref[...] = jnp.zeros_like(dwc_ref)
        dd_ref[...] = jnp.zeros_like(dd_ref)

        def finish(i, carry):
            u32, dy32 = u_ref[rows(i), :], dy_ref[rows(i), :]
            ub, dyb = u32.astype(bf16), dy32.astype(bf16)
            gr, gi = gre[rows(i), :].astype(bf16), gim[rows(i), :].astype(bf16)
            du_ref[rows(i), :] = _dot(gr, wbT_ref[0:SW, :]) + _dot(gi, wbT_ref[SW:, :]) + dy32 * d_ref[...]
            dwb_ref[:, 0:SW] += _dot_tn(ub, gr)
            dwb_ref[:, SW:] += _dot_tn(ub, gi)
            dwc_ref[:, 0:SW] += _dot_tn(dyb, sre[rows(i), :].astype(bf16))
            dwc_ref[:, SW:] += _dot_tn(dyb, sim[rows(i), :].astype(bf16))
            dd_ref[...] += jnp.sum(dy32 * u32, axis=0, keepdims=True)
            return carry

        lax.fori_loop(0, T // RB, finish, 0)

    slab = pl.BlockSpec((T, LANE), lambda k: (0, k))
    wide = pl.BlockSpec((None, LANE, 2 * SW), lambda k: (k, 0, 0))
    tall = pl.BlockSpec((None, 2 * SW, LANE), lambda k: (k, 0, 0))
    vec = pl.BlockSpec((None, 1, SW), lambda k: (k, 0, 0))
    vecd = pl.BlockSpec((None, 1, LANE), lambda k: (k, 0, 0))
    nslab = W // LANE
    return _call(
        body, [u_p, dy_p, wb, wbT, wcT, lbr, lbi, dsk], name="ssm_bwd", grid=(nslab,),
        in_specs=[slab, slab, wide, tall, wide, vec, vec, vecd],
        out_specs=[slab, wide, wide, vec, vec, vecd],
        out_shape=[S((T, W), f32), S((nslab, LANE, 2 * SW), f32), S((nslab, LANE, 2 * SW), f32),
                   S((nslab, 1, SW), f32), S((nslab, 1, SW), f32), S((nslab, 1, LANE), f32)],
        scratch=[pltpu.VMEM((T, SW), f32)] * 4 + [pltpu.VMEM((NC, SW), f32)] * 2, vmem=VMEM_LIMIT, plan=plan)


def glu_fwd(yn, glu_w, glu_b):
    T = yn.shape[0]
    tm = min(512, T)

    def body(y_ref, w_ref, b_ref, o_ref):
        g = _gelu(y_ref[...])
        o_ref[...] = (g * _sigmoid(_dot(g.astype(bf16), w_ref[...]) + b_ref[...])).astype(bf16)

    return pl.pallas_call(
        body, name="glu_fwd", grid=(T // tm,),
        in_specs=[pl.BlockSpec((tm, W), lambda i: (i, 0)), pl.BlockSpec((W, W), lambda i: (0, 0)), pl.BlockSpec((1, W), lambda i: (0, 0))],
        out_specs=pl.BlockSpec((tm, W), lambda i: (i, 0)), out_shape=S((T, W), bf16), compiler_params=_cp(("parallel",)),
    )(yn, glu_w, glu_b)


def _shift_rows(cur, prev8, k):
    return pltpu.roll(jnp.concatenate([prev8, cur], axis=0), k, axis=0)[8:]


def _lift_rows(cur, next8, k):
    n = cur.shape[0]
    return pltpu.roll(jnp.concatenate([cur, next8], axis=0), n + 8 - k, axis=0)[:n]


def conv_fwd(proj, conv_w):
    T = proj.shape[0]
    RB = min(512, T)

    def body(h_ref, c_ref, b_ref, w_ref, o_ref):
        w0, w1, w2 = w_ref[0:1, :], w_ref[1:2, :], w_ref[2:3, :]

        def blk(i, carry):
            r0 = pl.multiple_of(i * RB, RB)
            rs = pl.ds(r0, RB)
            ch = c_ref[rs, :] * h_ref[rs, :]
            pr = pl.ds(jnp.maximum(r0 - 8, 0), 8)
            prev = jnp.where(i > 0, c_ref[pr, :] * h_ref[pr, :], 0.0)
            z = w2 * ch + w1 * _shift_rows(ch, prev, 1) + w0 * _shift_rows(ch, prev, 2)
            o_ref[rs, :] = (b_ref[rs, :] * z).astype(bf16)
            return carry

        lax.fori_loop(0, T // RB, blk, 0)

    nb = W // LANE
    return pl.pallas_call(
        body, name="conv_fwd", grid=(nb,),
        in_specs=[pl.BlockSpec((T, LANE), lambda k: (0, nb + k)), pl.BlockSpec((T, LANE), lambda k: (0, 2 * nb + k)),
                  pl.BlockSpec((T, LANE), lambda k: (0, 3 * nb + k)), pl.BlockSpec((3, LANE), lambda k: (0, k))],
        out_specs=pl.BlockSpec((T, LANE), lambda k: (0, k)), out_shape=S((T, W), bf16),
        compiler_params=_cp(("parallel",), VMEM_LIMIT),
    )(proj, proj, proj, conv_w)


def _dense_columns(blocks_ref, dense_ref):
    for k in range(NDEV):
        dense_ref[:, k * LANE:(k + 1) * LANE] = blocks_ref[k]


def merge_fwd(ya, yb, wso, wco, proj, plan):
    T = ya.shape[0]
    tm = min(1024, T)

    def body(ya_ref, yb_ref, wa_ref, wb_ref, ga_ref, gb_ref, o_ref, wa_s, wb_s):
        @pl.when(pl.program_id(0) == 0)
        def _():
            _dense_columns(wa_ref, wa_s)
            _dense_columns(wb_ref, wb_s)

        o_ref[...] = (_sigmoid(ga_ref[...]) * _dot(ya_ref[...], wa_s[...])
                      + _sigmoid(gb_ref[...]) * _dot(yb_ref[...], wb_s[...])).astype(bf16)

    act = pl.BlockSpec((tm, W), lambda i: (i, 0))
    return _call(
        body, [ya, yb, wso, wco, proj, proj], name="merge_fwd", grid=(T // tm,),
        in_specs=[act, act, _resident((NDEV, W, LANE)), _resident((NDEV, W, LANE)),
                  pl.BlockSpec((tm, D), lambda i: (i, 2)), pl.BlockSpec((tm, D), lambda i: (i, 3))],
        out_specs=[pl.BlockSpec((tm, D), lambda i: (i, 0))], out_shape=[S((T, D), bf16)],
        scratch=[pltpu.VMEM((W, D), bf16), pltpu.VMEM((W, D), bf16)], vmem=VMEM_LIMIT, plan=plan)


def mix_ln1(merged, w_o, x, g1, b1):
    T = x.shape[0]
    tm = min(512, T)

    def body(m_ref, w_ref, x_ref, g_ref, b_ref, r_ref, x1_ref):
        r = ALPHA * x_ref[...] + _dot(m_ref[...], w_ref[...])
        r_ref[...] = r
        xhat, _ = _ln_stats(r)
        x1_ref[...] = (xhat * g_ref[...] + b_ref[...]).astype(bf16)

    row = pl.BlockSpec((tm, D), lambda i: (i, 0))
    vec = pl.BlockSpec((1, D), lambda i: (0, 0))
    return pl.pallas_call(
        body, name="mix_ln1", grid=(T // tm,),
        in_specs=[row, _resident((D, D)), row, vec, vec],
        out_specs=[row, row], out_shape=[S((T, D), f32), S((T, D), bf16)], compiler_params=_cp(("parallel",), VMEM_LIMIT),
    )(merged, w_o, x, g1, b1)


FT = 256


def gate_up(x1b, wgT, wuT):
    T = x1b.shape[0]
    tm = min(512, T)

    def body(x_ref, wg_ref, wu_ref, g_ref, u_ref, h_ref):
        x = x_ref[...]
        for n in range(F // FT):
            cs = slice(n * FT, (n + 1) * FT)
            g = _dot_nt(x, wg_ref[cs, :])
            u = _dot_nt(x, wu_ref[cs, :])
            g_ref[:, cs] = g.astype(bf16)
            u_ref[:, cs] = u.astype(bf16)
            h_ref[:, cs] = (g * _sigmoid(g) * u).astype(bf16)

    osp = pl.BlockSpec((tm, F), lambda i: (i, 0))
    return pl.pallas_call(
        body, name="gate_up", grid=(T // tm,),
        in_specs=[pl.BlockSpec((tm, D), lambda i: (i, 0)), _resident((F, D)), _resident((F, D))],
        out_specs=[osp, osp, osp], out_shape=[S((T, F), bf16)] * 3, compiler_params=_cp(("parallel",), VMEM_LIMIT),
    )(x1b, wgT, wuT)


def down_loss(hid, w_down, r1, g1, b1, g2, b2, target):
    T = hid.shape[0]
    tm = min(512, T)

    def body(h_ref, w_ref, r1_ref, g1_ref, b1_ref, g2_ref, b2_ref, t_ref, dr_ref, drb_ref, loss_ref, dg_ref, db_ref):
        @pl.when(pl.program_id(0) == 0)
        def _():
            loss_ref[...] = jnp.zeros_like(loss_ref)
            dg_ref[...] = jnp.zeros_like(dg_ref)
            db_ref[...] = jnp.zeros_like(db_ref)

        xh1, _ = _ln_stats(r1_ref[...])
        x1 = xh1 * g1_ref[...] + b1_ref[...]
        r2 = ALPHA * x1 + _dot(h_ref[...], w_ref[...])
        xh2, rstd2 = _ln_stats(r2)
        err = xh2 * g2_ref[...] + b2_ref[...] - t_ref[...]
        loss_ref[...] += jnp.sum(jnp.mean(err * err, axis=-1, keepdims=True), axis=0, keepdims=True)
        dy = err * (1.0 / D)
        dg_ref[...] += jnp.sum(dy * xh2, axis=0, keepdims=True)
        db_ref[...] += jnp.sum(dy, axis=0, keepdims=True)
        dr = _ln_bwd(dy, xh2, rstd2, g2_ref[...])
        dr_ref[...] = dr
        drb_ref[...] = dr.astype(bf16)

    row = pl.BlockSpec((tm, D), lambda i: (i, 0))
    vec = pl.BlockSpec((1, D), lambda i: (0, 0))
    return pl.pallas_call(
        body, name="down_loss", grid=(T // tm,),
        in_specs=[pl.BlockSpec((tm, F), lambda i: (i, 0)), _resident((F, D)), row, vec, vec, vec, vec, row],
        out_specs=[row, row, pl.BlockSpec((1, 1), lambda i: (0, 0)), vec, vec],
        out_shape=[S((T, D), f32), S((T, D), bf16), S((1, 1), f32), S((1, D), f32), S((1, D), f32)],
        compiler_params=_cp(("arbitrary",), VMEM_LIMIT),
    )(hid, w_down, r1, g1, b1, g2, b2, target)


def ffn_bwd_act(dffn, w_down, gate, up):
    T = dffn.shape[0]
    tm = min(512, T)

    def body(d_ref, w_ref, g_ref, u_ref, dg_ref, du_ref):
        d = d_ref[...]
        for n in range(F // FT):
            cs = slice(n * FT, (n + 1) * FT)
            dh = _dot_nt(d, w_ref[cs, :])
            g, u = g_ref[:, cs].astype(f32), u_ref[:, cs].astype(f32)
            sg = _sigmoid(g)
            du_ref[:, cs] = (dh * g * sg).astype(bf16)
            dg_ref[:, cs] = (dh * u * sg * (1.0 + g * (1.0 - sg))).astype(bf16)

    osp = pl.BlockSpec((tm, F), lambda i: (i, 0))
    return pl.pallas_call(
        body, name="ffn_bwd_act", grid=(T // tm,),
        in_specs=[pl.BlockSpec((tm, D), lambda i: (i, 0)), _resident((F, D)), osp, osp],
        out_specs=[osp, osp], out_shape=[S((T, F), bf16)] * 2, compiler_params=_cp(("parallel",), VMEM_LIMIT),
    )(dffn, w_down, gate, up)


def ffn_bwd_x(dgate, dup, wgT, wuT, dr2, r1, g1, plan):
    T = dr2.shape[0]
    tm = min(512, T)

    def body(dg_ref, du_ref, wg_ref, wu_ref, dr2_ref, r1_ref, g1_ref, dr_ref, drb_ref, dgam_ref, dbet_ref):
        @pl.when(pl.program_id(0) == 0)
        def _():
            dgam_ref[...] = jnp.zeros_like(dgam_ref)
            dbet_ref[...] = jnp.zeros_like(dbet_ref)

        dx1 = ALPHA * dr2_ref[...] + _dot(dg_ref[...], wg_ref[...]) + _dot(du_ref[...], wu_ref[...])
        xh, rstd = _ln_stats(r1_ref[...])
        dgam_ref[...] += jnp.sum(dx1 * xh, axis=0, keepdims=True)
        dbet_ref[...] += jnp.sum(dx1, axis=0, keepdims=True)
        dr = _ln_bwd(dx1, xh, rstd, g1_ref[...])
        dr_ref[...] = dr
        drb_ref[...] = dr.astype(bf16)

    row = pl.BlockSpec((tm, D), lambda i: (i, 0))
    wide = pl.BlockSpec((tm, F), lambda i: (i, 0))
    wsp = _resident((F, D))
    vec = pl.BlockSpec((1, D), lambda i: (0, 0))
    return _call(
        body, [dgate, dup, wgT, wuT, dr2, r1, g1], name="ffn_bwd_x", grid=(T // tm,),
        in_specs=[wide, wide, wsp, wsp, row, row, vec],
        out_specs=[row, row, vec, vec], out_shape=[S((T, D), f32), S((T, D), bf16), S((1, D), f32), S((1, D), f32)],
        vmem=VMEM_LIMIT, plan=plan)


def merge_bwd(dmix, w_o, ya, yb, wso, wco, proj, plan):
    T = dmix.shape[0]
    tm = min(512, T)

    def body(dm_ref, wo_ref, ya_ref, yb_ref, wa_ref, wb_ref, ga_ref, gb_ref, dya_ref, dyb_ref, dga_ref, dgb_ref, sa_ref, sb_ref,
             wa_s, wb_s):
        @pl.when(pl.program_id(0) == 0)
        def _():
            _dense_columns(wa_ref, wa_s)
            _dense_columns(wb_ref, wb_s)

        dmer = _dot_nt(dm_ref[...], wo_ref[...])
        sa, sb = _sigmoid(ga_ref[...]), _sigmoid(gb_ref[...])
        dya_ref[...] = (dmer * sa).astype(bf16)
        dyb_ref[...] = (dmer * sb).astype(bf16)
        dga = dmer * _dot(ya_ref[...], wa_s[...]) * sa * (1.0 - sa)
        dgb = dmer * _dot(yb_ref[...], wb_s[...]) * sb * (1.0 - sb)
        dga_ref[...] = dga.astype(bf16)
        dgb_ref[...] = dgb.astype(bf16)
        sa_ref[...] = jnp.sum(dga, axis=0, keepdims=True)
        sb_ref[...] = jnp.sum(dgb, axis=0, keepdims=True)

    act = pl.BlockSpec((tm, W), lambda i: (i, 0))
    osp = pl.BlockSpec((tm, D), lambda i: (i, 0))
    ssp = pl.BlockSpec((None, 1, D), lambda i: (i, 0, 0))
    return _call(
        body, [dmix, w_o, ya, yb, wso, wco, proj, proj], name="merge_bwd", grid=(T // tm,),
        in_specs=[osp, _resident((D, D)), act, act, _resident((NDEV, W, LANE)), _resident((NDEV, W, LANE)),
                  pl.BlockSpec((tm, D), lambda i: (i, 2)), pl.BlockSpec((tm, D), lambda i: (i, 3))],
        out_specs=[osp, osp, osp, osp, ssp, ssp],
        out_shape=[S((T, D), bf16)] * 4 + [S((T // tm, 1, D), f32)] * 2,
        scratch=[pltpu.VMEM((W, D), bf16), pltpu.VMEM((W, D), bf16)], vmem=VMEM_LIMIT, plan=plan)


def branches_bwd_x(dYA, dYB, wso, wco, plan):
    T = dYA.shape[0]
    tm = min(1024, T)

    def body(da_ref, db_ref, wa_ref, wb_ref, oa_ref, ob_ref, wa_s, wb_s):
        @pl.when(pl.program_id(0) == 0)
        def _():
            _dense_columns(wa_ref, wa_s)
            _dense_columns(wb_ref, wb_s)

        oa_ref[...] = _dot_nt(da_ref[...], wa_s[...])
        ob_ref[...] = _dot_nt(db_ref[...], wb_s[...])

    row = pl.BlockSpec((tm, D), lambda i: (i, 0))
    osp = pl.BlockSpec((tm, W), lambda i: (i, 0))
    return _call(
        body, [dYA, dYB, wso, wco], name="branches_bwd_x", grid=(T // tm,),
        in_specs=[row, row, _resident((NDEV, W, LANE)), _resident((NDEV, W, LANE))],
        out_specs=[osp, osp], out_shape=[S((T, W), f32)] * 2,
        scratch=[pltpu.VMEM((W, D), bf16), pltpu.VMEM((W, D), bf16)], vmem=VMEM_LIMIT, plan=plan)


def branch_bwd_w(act, dY, name):
    T = act.shape[0]
    tk = W // 2

    def body(a_ref, d_ref, o_ref):
        res = _dot_tn(a_ref[...], d_ref[...])
        for k in range(NDEV):
            o_ref[k] = res[:, k * LANE:(k + 1) * LANE].astype(o_ref.dtype)

    return pl.pallas_call(
        body, name=name, grid=(W // tk,),
        in_specs=[pl.BlockSpec((T, tk), lambda i: (0, i)), _resident((T, D))],
        out_specs=pl.BlockSpec((NDEV, tk, LANE), lambda i: (0, i, 0)), out_shape=S((NDEV, W, LANE), GRAD_DT),
        compiler_params=_cp(("parallel",), VMEM_LIMIT),
    )(act, dY)


def glu_bwd(yn, dya, glu_w, glu_b):
    T = yn.shape[0]
    tm = min(512, T)

    def body(y_ref, d_ref, w_ref, b_ref, dy_ref, dsp_ref, g_ref, db_ref):
        @pl.when(pl.program_id(0) == 0)
        def _():
            db_ref[...] = jnp.zeros_like(db_ref)

        y, dya_ = y_ref[...], d_ref[...]
        g = _gelu(y)
        gb = g.astype(bf16)
        s = _sigmoid(_dot(gb, w_ref[...]) + b_ref[...])
        dsp = dya_ * g * s * (1.0 - s)
        dspb = dsp.astype(bf16)
        dg = dya_ * s + _dot_nt(dspb, w_ref[...])
        dy_ref[...] = dg * _gelu_grad(y)
        dsp_ref[...] = dspb
        g_ref[...] = gb
        db_ref[...] += jnp.sum(dsp, axis=0, keepdims=True)

    row = pl.BlockSpec((tm, W), lambda i: (i, 0))
    vec = pl.BlockSpec((1, W), lambda i: (0, 0))
    return pl.pallas_call(
        body, name="glu_bwd", grid=(T // tm,),
        in_specs=[row, row, pl.BlockSpec((W, W), lambda i: (0, 0)), vec],
        out_specs=[row, row, row, vec], out_shape=[S((T, W), f32), S((T, W), bf16), S((T, W), bf16), S((1, W), f32)],
        compiler_params=_cp(("arbitrary",)),
    )(yn, dya, glu_w, glu_b)


def conv_bwd(proj, dyb, conv_w):
    T = proj.shape[0]
    RB = min(512, T)
    nrb = T // RB

    def body(h_ref, c_ref, b_ref, d_ref, w_ref, dh_ref, dc_ref, db_ref, dw_ref, s_ref):
        w0, w1, w2 = w_ref[0:1, :], w_ref[1:2, :], w_ref[2:3, :]

        def blk(i, carry):
            a0, a1, a2, sh, sc, sb = carry
            r0 = pl.multiple_of(i * RB, RB)
            rs = pl.ds(r0, RB)
            h, cg, bg, dyb_ = h_ref[rs, :], c_ref[rs, :], b_ref[rs, :], d_ref[rs, :]
            ch = cg * h
            pr = pl.ds(jnp.maximum(r0 - 8, 0), 8)
            prev = jnp.where(i > 0, c_ref[pr, :] * h_ref[pr, :], 0.0)
            ch1, ch2 = _shift_rows(ch, prev, 1), _shift_rows(ch, prev, 2)
            dbg = dyb_ * (w2 * ch + w1 * ch1 + w0 * ch2)
            db_ref[rs, :] = dbg.astype(bf16)
            dz = dyb_ * bg
            nx = pl.ds(jnp.minimum(r0 + RB, T - 8), 8)
            nxt = jnp.where(i < nrb - 1, d_ref[nx, :] * b_ref[nx, :], 0.0)
            dch = w2 * dz + w1 * _lift_rows(dz, nxt, 1) + w0 * _lift_rows(dz, nxt, 2)
            dcg, dh = dch * h, dch * cg
            dc_ref[rs, :] = dcg.astype(bf16)
            dh_ref[rs, :] = dh.astype(bf16)
            col = lambda v: jnp.sum(v, axis=0, keepdims=True)
            return (a0 + col(dz * ch2), a1 + col(dz * ch1), a2 + col(dz * ch), sh + col(dh), sc + col(dcg), sb + col(dbg))

        zero = jnp.zeros((1, LANE), f32)
        a0, a1, a2, sh, sc, sb = lax.fori_loop(0, nrb, blk, (zero,) * 6)
        dw_ref[0:1, :] = a0
        dw_ref[1:2, :] = a1
        dw_ref[2:3, :] = a2
        s_ref[0:1, :] = sh
        s_ref[1:2, :] = sc
        s_ref[2:3, :] = sb

    nb = W // LANE
    slab = pl.BlockSpec((T, LANE), lambda k: (0, k))
    three = pl.BlockSpec((3, LANE), lambda k: (0, k))
    return pl.pallas_call(
        body, name="conv_bwd", grid=(nb,),
        in_specs=[pl.BlockSpec((T, LANE), lambda k: (0, nb + k)), pl.BlockSpec((T, LANE), lambda k: (0, 2 * nb + k)),
                  pl.BlockSpec((T, LANE), lambda k: (0, 3 * nb + k)), slab, three],
        out_specs=[slab, slab, slab, three, three],
        out_shape=[S((T, W), bf16)] * 3 + [S((3, W), f32)] * 2, compiler_params=_cp(("parallel",), VMEM_LIMIT),
    )(proj, proj, proj, dyb, conv_w)


def in_proj_bwd_x(parts, win_g, base, scale, name, plan=None):
    T = base.shape[0]
    tm = min(512, T)
    n = len(parts)

    def body(*refs):
        p_refs, w_ref, b_ref, o_ref = refs[:n], refs[n], refs[n + 1], refs[n + 2]
        acc = scale * b_ref[...]
        for p_ref, (_, _, k) in zip(p_refs, parts):
            acc += _dot_nt(p_ref[...], w_ref[k])
        o_ref[...] = acc

    row = pl.BlockSpec((tm, D), lambda i: (i, 0))
    p_specs = [pl.BlockSpec((tm, W), (lambda i, cb=cb: (i, cb))) for _, cb, _ in parts]
    return _call(
        body, [a for a, _, _ in parts] + [win_g, base], name=name, grid=(T // tm,),
        in_specs=p_specs + [_resident((NDEV, D, W)), row],
        out_specs=[row], out_shape=[S((T, D), f32)], vmem=VMEM_LIMIT, plan=plan)


def ssm_param_bwd(lam_re, lam_im, log_dt, fr, fi, br, bi, dbbr, dbbi, dlbr, dlbi):
    def body(lr_ref, li_ref, ldt_ref, fr_ref, fi_ref, br_ref, bi_ref, dr_ref, di_ref, dlbr_ref, dlbi_ref,
             dbr_ref, dbi_ref, dlr_ref, dli_ref, dldt_ref):
        fr_, fi_ = _per_channel(fr_ref[...]), _per_channel(fi_ref[...])
        br_, bi_, dr, di = br_ref[...], bi_ref[...], dr_ref[...], di_ref[...]
        dbr_ref[...] = fr_ * dr + fi_ * di
        dbi_ref[...] = fr_ * di - fi_ * dr
        dfr = jnp.sum((dr * br_ + di * bi_).reshape(NG, GC, NP), axis=1)
        dfi = jnp.sum((di * br_ - dr * bi_).reshape(NG, GC, NP), axis=1)
        _, vjp = jax.vjp(_disc, lr_ref[...], li_ref[...], ldt_ref[...])
        dlr_ref[...], dli_ref[...], dldt = vjp((dlbr_ref[...], dlbi_ref[...], dfr, dfi))
        dldt_ref[...] = _transpose_exact(dldt)

    return pl.pallas_call(
        body, name="ssm_param_bwd",
        out_shape=[S((NG * GC, NP), f32)] * 2 + [S((NG, NP), f32)] * 2 + [S((1, NG), f32)])(
        lam_re, lam_im, log_dt, fr, fi, br, bi, dbbr, dbbi, dlbr, dlbi)


def _adam(w, g, m, v):
    m = ADAM_B1 * m + (1.0 - ADAM_B1) * g
    v = ADAM_B2 * v + (1.0 - ADAM_B2) * (g * g)
    m_hat = m / (1.0 - ADAM_B1 ** ADAM_STEP)
    v_hat = v / (1.0 - ADAM_B2 ** ADAM_STEP)
    return -ADAM_LR * (m_hat / (jnp.sqrt(v_hat) + ADAM_EPS) + ADAM_WD * w), m, v


def adam_update(w, m, v, contrib, name, rows_per_block=None):
    R, C = w.shape
    n = contrib.shape[0]
    tr = min(rows_per_block or R, R)

    def body(w_ref, m_ref, v_ref, c_ref, g_ref, d_ref, nm_ref, nv_ref):
        g = c_ref[0].astype(f32)
        for k in range(1, n):
            g = g + c_ref[k].astype(f32)
        g_ref[...] = g
        d_ref[...], nm_ref[...], nv_ref[...] = _adam(w_ref[...], g, m_ref[...], v_ref[...])

    blk = pl.BlockSpec((tr, C), lambda i: (i, 0))
    return pl.pallas_call(
        body, name=name, grid=(R // tr,), in_specs=[blk, blk, blk, pl.BlockSpec((n, tr, C), lambda i: (0, i, 0))],
        out_specs=[blk] * 4, out_shape=[S((R, C), f32)] * 4, compiler_params=_cp(("parallel",), VMEM_LIMIT),
    )(w, m, v, contrib)


_ROWVEC = (("b_in", IN_COLS), ("ssm_d", W), ("glu_b", W), ("ln1_g", D), ("ln1_b", D), ("ln2_g", D), ("ln2_b", D))
_HALF = NG * GC // 2
_PACK = {}
_r = 0
for _n, _k in _ROWVEC:
    _PACK[_n] = _r
    _r += _k // LANE
for _n, _rows in (("ssm_lambda", NG), ("scalars", 8), ("ssm_b_re", _HALF), ("ssm_b_im", _HALF), ("ssm_c_re", _HALF),
                  ("ssm_c_im", _HALF), ("conv_w", 16)):
    _PACK[_n] = _r
    _r += _rows
PACK_ROWS = _r
assert PACK_ROWS % 8 == 0
_SMALL = ("b_in", "ssm_lambda_re", "ssm_lambda_im", "ssm_log_dt", "ssm_b_re", "ssm_b_im", "ssm_c_re", "ssm_c_im",
          "ssm_d", "glu_b", "ln1_g", "ln1_b", "ln2_g", "ln2_b")


def pack_grads(su, shcb, sga, sgb, dd, dglu_b, dln1_g, dln1_b, dln2_g, dln2_b, dlam_re, dlam_im, dldt, sqerr, dbr, dbi,
               dc_re, dc_im, dconv):
    nI = sga.shape[0]

    def body(su_ref, sh_ref, sga_ref, sgb_ref, dd_ref, gb_ref, l1g_ref, l1b_ref, l2g_ref, l2b_ref, lr_ref, li_ref, dt_ref,
             sq_ref, br_ref, bi_ref, cr_ref, ci_ref, cw_ref, o_ref):
        o_ref[...] = jnp.zeros_like(o_ref)

        def put_row(name, v):
            r0 = _PACK[name]
            for i in range(v.shape[1] // LANE):
                o_ref[r0 + i:r0 + i + 1, :] = v[:, i * LANE:(i + 1) * LANE]

        ga, gb = sga_ref[0], sgb_ref[0]
        for i in range(1, nI):
            ga, gb = ga + sga_ref[i], gb + sgb_ref[i]
        put_row("b_in", jnp.concatenate([su_ref[...], sh_ref[0:1, :], sh_ref[1:2, :], sh_ref[2:3, :], ga, gb], axis=1))
        put_row("ssm_d", jnp.concatenate([dd_ref[k] for k in range(W // LANE)], axis=1))
        put_row("glu_b", gb_ref[...])
        put_row("ln1_g", l1g_ref[...])
        put_row("ln1_b", l1b_ref[...])
        put_row("ln2_g", l2g_ref[...])
        put_row("ln2_b", l2b_ref[...])
        r0 = _PACK["ssm_lambda"]
        o_ref[r0:r0 + NG, 0:NP] = lr_ref[...]
        o_ref[r0:r0 + NG, NP:2 * NP] = li_ref[...]
        r0 = _PACK["scalars"]
        o_ref[r0:r0 + 1, 0:NG] = dt_ref[...]
        o_ref[r0 + 1:r0 + 2, 0:1] = sq_ref[...]
        for name, ref in (("ssm_b_re", br_ref), ("ssm_b_im", bi_ref), ("ssm_c_re", cr_ref), ("ssm_c_im", ci_ref)):
            r0 = _PACK[name]
            o_ref[r0:r0 + _HALF, 0:NP] = ref[0:_HALF, :]
            o_ref[r0:r0 + _HALF, NP:2 * NP] = ref[_HALF:2 * _HALF, :]
        for cb in range(W // LANE):
            o_ref[_PACK["conv_w"] + 3 * cb:_PACK["conv_w"] + 3 * cb + 3, :] = cw_ref[:, cb * LANE:(cb + 1) * LANE]

    return pl.pallas_call(body, name="pack_grads", out_shape=S((PACK_ROWS, LANE), f32))(
        su, shcb, sga, sgb, dd, dglu_b, dln1_g, dln1_b, dln2_g, dln2_b, dlam_re, dlam_im, dldt, sqerr, dbr, dbi, dc_re, dc_im,
        dconv)


def adam_small(packed_all, params):
    names = list(_SMALL) + ["conv_w"]
    flat = [a for n in names for a in params[n]]

    def body(*refs):
        p_ref = refs[0]
        ins = refs[1:1 + 3 * len(names)]
        outs = refs[1 + 3 * len(names):-2]
        loss_ref, g_ref = refs[-2], refs[-1]
        g_all = p_ref[0]
        for k in range(1, NDEV):
            g_all = g_all + p_ref[k]
        g_ref[...] = g_all

        def rows(name, r0, n, l0=0, lanes=LANE):
            return g_ref[_PACK[name] + r0:_PACK[name] + r0 + n, l0:l0 + lanes]

        def grad_of(name):
            if name in dict(_ROWVEC):
                return jnp.concatenate([rows(name, i, 1) for i in range(dict(_ROWVEC)[name] // LANE)], axis=1)
            if name in ("ssm_lambda_re", "ssm_lambda_im"):
                return rows("ssm_lambda", 0, NG, NP * (name == "ssm_lambda_im"), NP)[None]
            if name == "ssm_log_dt":
                return rows("scalars", 0, 1, 0, NG)
            if name in ("ssm_b_re", "ssm_b_im", "ssm_c_re", "ssm_c_im"):
                return jnp.concatenate([rows(name, 0, _HALF, 0, NP), rows(name, 0, _HALF, NP, NP)], axis=0).reshape(1, NG, GC, NP)
            full = jnp.concatenate([rows("conv_w", 3 * cb, 3) for cb in range(W // LANE)], axis=1)
            x, y, c = _coords()
            col0 = (4 * x + 2 * y + c) * (W // NDEV)
            sel = (lax.broadcasted_iota(jnp.int32, (W, W // NDEV), 0)
                   == lax.broadcasted_iota(jnp.int32, (W, W // NDEV), 1) + col0).astype(f32)
            return jnp.dot(full, sel, precision=HIGHEST, preferred_element_type=f32)[None]

        loss_ref[...] = 0.5 * rows("scalars", 1, 1, 0, 1)
        for i, name in enumerate(names):
            w_ref, m_ref, v_ref = ins[3 * i:3 * i + 3]
            g = grad_of(name)
            d, m, v = _adam(w_ref[...], g, m_ref[...], v_ref[...])
            outs[4 * i][...] = g
            outs[4 * i + 1][...] = d
            outs[4 * i + 2][...] = m
            outs[4 * i + 3][...] = v

    out_shape = [S(params[n][0].shape, f32) for n in names for _ in range(4)] + [S((1, 1), f32)]
    res = pl.pallas_call(body, name="adam_small", out_shape=out_shape, scratch_shapes=[pltpu.VMEM((PACK_ROWS, LANE), f32)],
                         compiler_params=_cp(None, VMEM_LIMIT))(packed_all, *flat)
    return {n: res[4 * i:4 * i + 4] for i, n in enumerate(names)}, res[-1]


def _block_diag(wgt):
    eye = jnp.eye(8, dtype=wgt.dtype)
    out = wgt[:, :, :, None, :] * eye[None, :, None, :, None]
    return out.reshape(4, 8 * wgt.shape[2], 8 * wgt.shape[3])


def _diag_blocks(m, a, b):
    m = m.reshape(4, 8, a, 8, b)
    idx = jnp.arange(8)
    return m[:, idx, :, idx, :].transpose(1, 0, 2, 3)


def kernel(x, w_in, b_in, ssm_lambda_re, ssm_lambda_im, ssm_log_dt, ssm_b_re, ssm_b_im, ssm_c_re, ssm_c_im, ssm_d, glu_w, glu_b, w_ssm_out, conv_w, w_conv_out, w_o, ln1_g, ln1_b, w_gate, w_up, w_down, ln2_g, ln2_b, loss_target, m_w_in, m_b_in, m_ssm_lambda_re, m_ssm_lambda_im, m_ssm_log_dt, m_ssm_b_re, m_ssm_b_im, m_ssm_c_re, m_ssm_c_im, m_ssm_d, m_glu_w, m_glu_b, m_w_ssm_out, m_conv_w, m_w_conv_out, m_w_o, m_ln1_g, m_ln1_b, m_w_gate, m_w_up, m_w_down, m_ln2_g, m_ln2_b, v_w_in, v_b_in, v_ssm_lambda_re, v_ssm_lambda_im, v_ssm_log_dt, v_ssm_b_re, v_ssm_b_im, v_ssm_c_re, v_ssm_c_im, v_ssm_d, v_glu_w, v_glu_b, v_w_ssm_out, v_conv_w, v_w_conv_out, v_w_o, v_ln1_g, v_ln1_b, v_w_gate, v_w_up, v_w_down, v_ln2_g, v_ln2_b):
    given = dict(locals())
    xs = x[0]
    target = loss_target[0]

    tr = lambda a: jnp.swapaxes(a[0], 0, 1)
    win_s, glu_s, wso_s, wco_s, wo_s, wgT_s, wuT_s, wd_s = prep_weights(
        [w_in[0], glu_w[0], w_ssm_out[0], w_conv_out[0], w_o[0], tr(w_gate), tr(w_up), w_down[0]])
    win_g, conv_g = run_plan(GatherPlan([win_s, conv_w[0]]), "gather_w_in")
    conv_f = conv_g.transpose(1, 0, 2).reshape(3, W)

    lam_re, lam_im = ssm_lambda_re[0], ssm_lambda_im[0]
    ldt = ssm_log_dt[0].reshape(NG, 1)
    br2 = jnp.swapaxes(ssm_b_re[0], 1, 2).reshape(NG * GC, NP)
    bi2 = jnp.swapaxes(ssm_b_im[0], 1, 2).reshape(NG * GC, NP)
    lbr, lbi, fr, fi, bbr, bbi = ssm_params(lam_re, lam_im, ldt, br2, bi2)
    bb_t = lambda b: b.reshape(4, 8, GC, NP)
    wb = jnp.concatenate([_block_diag(bb_t(bbr)), _block_diag(bb_t(bbi))], axis=2)
    c_t = lambda c: c.reshape(4, 8, GC, NP).transpose(0, 1, 3, 2)
    wc = jnp.concatenate([_block_diag(c_t(ssm_c_re[0])), -_block_diag(c_t(ssm_c_im[0]))], axis=1)
    wbT, wcT = wb.transpose(0, 2, 1), wc.transpose(0, 2, 1)
    wb, wc, wbT, wcT = wb.astype(bf16), wc.astype(bf16), wbT.astype(bf16), wcT.astype(bf16)
    lbr_s, lbi_s = lbr.reshape(4, 1, SW), lbi.reshape(4, 1, SW)
    dsk = ssm_d[0].reshape(4, 1, LANE)

    (proj, xb), (glu_g, wso_g, wco_g, wo_g) = in_proj(xs, win_g, b_in, GatherPlan([glu_s, wso_s, wco_s, wo_s]))
    glu_f, wo_f = glu_g.reshape(W, W), wo_g.reshape(D, D)
    u_p = to_perm(proj, 0, "perm_u")
    (y_p,), (wgT_g, wuT_g) = ssm_fwd(u_p, wb, wc, lbr_s, lbi_s, dsk, GatherPlan([wgT_s, wuT_s]))
    wgT, wuT = wgT_g.reshape(F, D), wuT_g.reshape(F, D)
    yn, _ = from_perm(y_p, "unperm_y")
    ya = glu_fwd(yn, glu_f, glu_b)
    yb = conv_fwd(proj, conv_f)
    (merged,), (wd_g,) = merge_fwd(ya, yb, wso_g, wco_g, proj, GatherPlan([wd_s]))
    wd_f = wd_g.reshape(F, D)
    r1, x1b = mix_ln1(merged, wo_f, xs, ln1_g, ln1_b)
    gate, up, hid = gate_up(x1b, wgT, wuT)
    dr2, dffn, sqerr, dln2_g, dln2_b = down_loss(hid, wd_f, r1, ln1_g, ln1_b, ln2_g, ln2_b, target)

    dgate, dup = ffn_bwd_act(dffn, wd_f, gate, up)
    dwd = mm_tn_rows(hid, dffn, "grad_w_down").reshape(NDEV, FS, D)
    dwgT = mm_tn_rows(dgate, x1b, "grad_w_gate").reshape(NDEV, FS, D)
    dwuT = mm_tn_rows(dup, x1b, "grad_w_up").reshape(NDEV, FS, D)
    (dr1, dmix, dln1_g, dln1_b), (r_wd,) = ffn_bwd_x(dgate, dup, wgT, wuT, dr2, r1, ln1_g, ScatterPlan([dwd]))
    (dYA, dYB, dga, dgb, sga, sgb), (r_wgT,) = merge_bwd(dmix, wo_f, ya, yb, wso_g, wco_g, proj, ScatterPlan([dwgT]))
    dwo = mm_tn_rows(merged, dmix, "grad_w_o").reshape(NDEV, D // NDEV, D)
    (dya, dyb), (r_wuT,) = branches_bwd_x(dYA, dYB, wso_g, wco_g, ScatterPlan([dwuT]))
    dwso = branch_bwd_w(ya, dYA, "grad_w_ssm_out")
    dwco = branch_bwd_w(yb, dYB, "grad_w_conv_out")
    dyn, dsp, gb, dglu_b = glu_bwd(yn, dya, glu_f, glu_b)
    dglu = mm_tn_rows(gb, dsp, "grad_glu_w").reshape(NDEV, W // NDEV, W)
    dh, dcg, dbg, dconv, shcb = conv_bwd(proj, dyb, conv_f)
    dwin = mm_tn(xb, dgb, "grad_w_in_gb", tk=512, block0=6, nblocks=NDEV)
    dwin = mm_tn(xb, dga, "grad_w_in_ga", tk=512, block0=4, into=dwin)
    dwin = mm_tn(xb, dbg, "grad_w_in_bg", tk=512, block0=3, into=dwin)
    dwin = mm_tn(xb, dcg, "grad_w_in_cg", tk=512, block0=2, into=dwin)
    dwin = mm_tn(xb, dh, "grad_w_in_h", tk=512, block0=1, into=dwin)
    dy_p = to_perm(dyn, 0, "perm_dy")
    (du_p, dwb, dwcT, dlbr_s, dlbi_s, dd), (r_wo, r_wso, r_wco, r_glu) = ssm_bwd(
        u_p, dy_p, wb, wbT, wcT, lbr_s, lbi_s, dsk, ScatterPlan([dwo, dwso, dwco, dglu]))
    du, su = from_perm(du_p, "unperm_du", bf16)
    dwin = mm_tn(xb, du, "grad_w_in_u", tk=512, block0=0, into=dwin)

    dbb = lambda m: _diag_blocks(m, GC, NP).reshape(NG * GC, NP)
    dbr2, dbi2, dlam_re, dlam_im, dldt = ssm_param_bwd(
        lam_re, lam_im, ldt, fr, fi, br2, bi2, dbb(dwb[:, :, :SW]), dbb(dwb[:, :, SW:]),
        dlbr_s.reshape(NG, NP), dlbi_s.reshape(NG, NP))
    packed = pack_grads(su, shcb, sga, sgb, dd, dglu_b, dln1_g, dln1_b, dln2_g, dln2_b, dlam_re, dlam_im, dldt, sqerr,
                        dbr2, dbi2, dbb(dwcT[:, :, :SW]), -dbb(dwcT[:, :, SW:]), dconv)

    rest = [(dh, 0, 1), (dcg, 0, 2), (dbg, 0, 3), (dga, 0, 4), (dga, 1, 5), (dgb, 0, 6), (dgb, 1, 7)]
    (gx_rest,), (r_win, small_all) = in_proj_bwd_x(rest, win_g, dr1, ALPHA, "in_proj_bwd_x_rest",
                                                  Plans([ScatterPlan([dwin]), GatherPlan([packed])]))
    (grad_x,), _ = in_proj_bwd_x([(du, 0, 0)], win_g, gx_rest, 1.0, "in_proj_bwd_x_u")

    out = {}

    def put(name, res, back=lambda a: a[None]):
        out["grad_" + name], out["delta_" + name], out["new_m_" + name], out["new_v_" + name] = [back(r) for r in res]

    put("w_in", adam_update(w_in[0], m_w_in[0], v_w_in[0], r_win, "adam_w_in", 256))
    put("glu_w", adam_update(glu_w[0], m_glu_w[0], v_glu_w[0], r_glu, "adam_glu_w"))
    put("w_ssm_out", adam_update(w_ssm_out[0], m_w_ssm_out[0], v_w_ssm_out[0], r_wso, "adam_w_ssm_out"))
    put("w_conv_out", adam_update(w_conv_out[0], m_w_conv_out[0], v_w_conv_out[0], r_wco, "adam_w_conv_out"))
    put("w_o", adam_update(w_o[0], m_w_o[0], v_w_o[0], r_wo, "adam_w_o"))
    put("w_down", adam_update(w_down[0], m_w_down[0], v_w_down[0], r_wd, "adam_w_down", 176))
    untr = lambda a: jnp.swapaxes(a, 0, 1)[None]
    put("w_gate", adam_update(tr(w_gate), tr(m_w_gate), tr(v_w_gate), r_wgT, "adam_w_gate", 176), untr)
    put("w_up", adam_update(tr(w_up), tr(m_w_up), tr(v_w_up), r_wuT, "adam_w_up", 176), untr)
    as_c = lambda a: jnp.swapaxes(a, 2, 3)
    params = {n: (given[n], given["m_" + n], given["v_" + n]) for n in list(_SMALL) + ["conv_w"]}
    for n in ("ssm_b_re", "ssm_b_im"):
        params[n] = tuple(as_c(a) for a in params[n])
    small, loss = adam_small(small_all, params)
    for n, res in small.items():
        put(n, res, as_c if n in ("ssm_b_re", "ssm_b_im") else (lambda a: a))

    names = ["w_in", "b_in", "ssm_lambda_re", "ssm_lambda_im", "ssm_log_dt", "ssm_b_re", "ssm_b_im", "ssm_c_re", "ssm_c_im",
             "ssm_d", "glu_w", "glu_b", "w_ssm_out", "conv_w", "w_conv_out", "w_o", "ln1_g", "ln1_b", "w_gate", "w_up",
             "w_down", "ln2_g", "ln2_b"]
    return (loss.reshape(()), grad_x[None], *[out[p + n] for p in ("grad_", "delta_", "new_m_", "new_v_") for n in names])
```

```python
import functools
import math

import jax
import jax.numpy as jnp
from jax import lax
from jax.experimental import pallas as pl
from jax.experimental.pallas import tpu as pltpu

f32, bf16 = jnp.float32, jnp.bfloat16
S = jax.ShapeDtypeStruct
MESH = pl.DeviceIdType.MESH
HIGHEST = lax.Precision.HIGHEST

D = 1024
W = 512
NG, NP, GC = 32, 64, 16
F = 2816
NDEV = 8
FS = F // NDEV
IN_COLS = 8 * W
ALPHA = 2.0 ** 0.25
LN_EPS = 1e-5
ADAM_LR, ADAM_B1, ADAM_B2, ADAM_EPS, ADAM_WD, ADAM_STEP = 0.001, 0.9, 0.999, 1e-08, 0.01, 10
NC = 32
LANE = 128
SW = 4 * LANE
VMEM_LIMIT = 56 * 1024 * 1024
GRAD_DT = bf16
ANY = pl.BlockSpec(memory_space=pl.ANY)


def _cp(sem=None, vmem=None):
    return pltpu.CompilerParams(dimension_semantics=sem, vmem_limit_bytes=vmem)


def _resident(shape):
    return pl.BlockSpec(shape, lambda i: (0,) * len(shape), pipeline_mode=pl.Buffered(1))


def _dot(a, b):
    return jnp.dot(a, b, preferred_element_type=f32)


def _dot_nt(a, b):
    return lax.dot_general(a, b, (((1,), (1,)), ((), ())), preferred_element_type=f32)


def _dot_tn(a, b):
    return lax.dot_general(a, b, (((0,), (0,)), ((), ())), preferred_element_type=f32)


def _eye(n):
    return (lax.broadcasted_iota(jnp.int32, (n, n), 0) == lax.broadcasted_iota(jnp.int32, (n, n), 1)).astype(f32)


def _transpose_exact(a):
    return lax.dot_general(a, _eye(a.shape[0]), (((0,), (0,)), ((), ())), precision=HIGHEST, preferred_element_type=f32)


def _sigmoid(x):
    return 1.0 / (1.0 + jnp.exp(-x))


_GK = math.sqrt(2.0 / math.pi)


def _gelu(x):
    return 0.5 * x * (1.0 + jnp.tanh(_GK * (x + 0.044715 * x * x * x)))


def _gelu_grad(x):
    th = jnp.tanh(_GK * (x + 0.044715 * x * x * x))
    return 0.5 * (1.0 + th) + 0.5 * x * (1.0 - th * th) * _GK * (1.0 + 3.0 * 0.044715 * x * x)


def _ln_stats(r):
    mu = jnp.mean(r, axis=-1, keepdims=True)
    xc = r - mu
    var = jnp.mean(xc * xc, axis=-1, keepdims=True)
    rstd = lax.rsqrt(var + LN_EPS)
    return xc * rstd, rstd


def _ln_bwd(dy, xhat, rstd, g):
    dxh = dy * g
    m1 = jnp.mean(dxh, axis=-1, keepdims=True)
    m2 = jnp.mean(dxh * xhat, axis=-1, keepdims=True)
    return rstd * (dxh - m1 - xhat * m2)


def _coords():
    return lax.axis_index("x"), lax.axis_index("y"), lax.axis_index("c")


class GatherPlan:
    aliases = ()

    def __init__(self, arrs):
        self.inputs = list(arrs)
        n = len(arrs)
        self.out_shape = [S((NDEV,) + a.shape, a.dtype) for a in arrs]
        self.sems = [pltpu.SemaphoreType.DMA((n, 7)), pltpu.SemaphoreType.DMA((n, 7)), pltpu.SemaphoreType.DMA((n,))]

    def _parts(self, ins, outs, sems):
        n = len(ins)
        send_sems, recv_sems, loc_sems = sems
        x, y, c = _coords()
        me, sib = (x, y, c), (x, y, 1 - c)
        chips = [(1 - x, y), (x, 1 - y), (1 - x, 1 - y)]

        def slot(a, dev):
            return outs[a].at[4 * dev[0] + 2 * dev[1] + dev[2]]

        def copy(a, k, block, to, src=None):
            return pltpu.make_async_remote_copy(
                src_ref=slot(a, block) if src is None else src, dst_ref=slot(a, block),
                send_sem=send_sems.at[a, k], recv_sem=recv_sems.at[a, k], device_id=to, device_id_type=MESH)

        each = [(j, chip, a) for j, chip in enumerate(chips) for a in range(n)]
        return dict(
            mine=lambda: [pltpu.make_async_copy(ins[a], slot(a, me), loc_sems.at[a]) for a in range(n)],
            first=lambda: ([copy(a, 0, me, sib, src=ins[a]) for a in range(n)]
                           + [copy(a, 1 + j, me, (*chip, c), src=ins[a]) for j, chip, a in each]),
            landed=lambda: [copy(a, 1 + j, (*chip, c), me) for j, chip, a in each],
            passed=lambda: [copy(a, 4 + j, (*chip, c), sib) for j, chip, a in each],
            from_sib=lambda: ([copy(a, 0, sib, me) for a in range(n)]
                              + [copy(a, 4 + j, (*chip, 1 - c), me) for j, chip, a in each]))

    def start(self, ins, outs, sems):
        p = self._parts(ins, outs, sems)
        for cp in p["mine"]() + p["first"]():
            cp.start()

    def forward(self, ins, outs, sems):
        p = self._parts(ins, outs, sems)
        for got, fwd in zip(p["landed"](), p["passed"]()):
            got.wait_recv()
            fwd.start()

    def finish(self, ins, outs, sems):
        p = self._parts(ins, outs, sems)
        for cp in p["from_sib"]():
            cp.wait_recv()
        for cp in p["first"]() + p["passed"]():
            cp.wait_send()
        for cp in p["mine"]():
            cp.wait()


class ScatterPlan:
    aliases = ()

    def __init__(self, gs, only=None, into=None):
        n = self.n = len(gs)
        self.only = only
        self.inputs = list(gs) + list(into or [])
        if into:
            self.aliases = tuple((n + a, a) for a in range(n))
        self.out_shape = [S(g.shape, g.dtype) for g in gs]
        self.sems = [pltpu.SemaphoreType.DMA((n, 7)), pltpu.SemaphoreType.DMA((n, 7)), pltpu.SemaphoreType.DMA((n,))]

    def _owner(self, idx):
        if self.only is None:
            return True
        return functools.reduce(jnp.logical_or, [idx == b for b in self.only])

    def _copies(self, ins, outs, sems):
        n = self.n
        send_sems, recv_sems, loc_sems = sems
        x, y, c = _coords()
        me = 4 * x + 2 * y + c
        mine = self._owner(me)
        copies = [(pltpu.make_async_copy(ins[a].at[me], outs[a].at[me], loc_sems.at[a]), mine, None) for a in range(n)]
        for m in range(1, NDEV):
            px = 1 - x if m & 4 else x
            py = 1 - y if m & 2 else y
            pc = 1 - c if m & 1 else c
            peer = 4 * px + 2 * py + pc
            for a in range(n):
                copies.append((pltpu.make_async_remote_copy(
                    src_ref=ins[a].at[peer], dst_ref=outs[a].at[me],
                    send_sem=send_sems.at[a, m - 1], recv_sem=recv_sems.at[a, m - 1],
                    device_id=(px, py, pc), device_id_type=MESH), self._owner(peer), mine))
        return copies

    @staticmethod
    def _when(cond, fn):
        if cond is True:
            fn()
        else:
            pl.when(cond)(fn)

    def start(self, ins, outs, sems):
        for cp, sends, _ in self._copies(ins, outs, sems):
            self._when(sends, cp.start)

    def forward(self, ins, outs, sems):
        pass

    def finish(self, ins, outs, sems):
        for cp, sends, receives in self._copies(ins, outs, sems):
            if receives is None:
                self._when(sends, cp.wait)
            else:
                self._when(sends, cp.wait_send)
                self._when(receives, cp.wait_recv)


class Plans:
    def __init__(self, plans):
        self.plans = plans
        self.inputs = [a for p in plans for a in p.inputs]
        self.out_shape = [s for p in plans for s in p.out_shape]
        self.sems = [s for p in plans for s in p.sems]
        self.aliases, i, o = [], 0, 0
        for p in plans:
            self.aliases += [(i + a, o + b) for a, b in p.aliases]
            i, o = i + len(p.inputs), o + len(p.out_shape)

    def _each(self, what, ins, outs, sems):
        i = o = s = 0
        for p in self.plans:
            ni, no, ns = len(p.inputs), len(p.out_shape), len(p.sems)
            getattr(p, what)(ins[i:i + ni], outs[o:o + no], sems[s:s + ns])
            i, o, s = i + ni, o + no, s + ns

    def start(self, ins, outs, sems):
        self._each("start", ins, outs, sems)

    def forward(self, ins, outs, sems):
        self._each("forward", ins, outs, sems)

    def finish(self, ins, outs, sems):
        self._each("finish", ins, outs, sems)


def _call(body, args, *, name, grid, in_specs, out_specs, out_shape, scratch=(), sem=None, vmem=None, plan=None,
          aliases=None):
    aliases = aliases or {}
    if plan is None:
        outs = pl.pallas_call(body, name=name, grid=grid, in_specs=list(in_specs), out_specs=list(out_specs),
                              out_shape=list(out_shape), scratch_shapes=list(scratch), input_output_aliases=aliases,
                              compiler_params=_cp(sem, vmem))(*args)
        return list(outs), []
    ni, no, ns = len(in_specs), len(out_specs), len(scratch)
    pi, po = len(plan.inputs), len(plan.out_shape)
    aliases = {**aliases, **{ni + a: no + b for a, b in plan.aliases}}

    def wrapped(*refs):
        main_in, p_in = refs[:ni], refs[ni:ni + pi]
        main_out, p_out = refs[ni + pi:ni + pi + no], refs[ni + pi + no:ni + pi + no + po]
        main_scr, p_sems = refs[ni + pi + no + po:ni + pi + no + po + ns], refs[ni + pi + no + po + ns:]
        ids = [pl.program_id(d) for d in range(len(grid))]
        first = functools.reduce(jnp.logical_and, [i == 0 for i in ids])
        last = functools.reduce(jnp.logical_and, [i == g - 1 for i, g in zip(ids, grid)])

        @pl.when(first)
        def _():
            plan.start(p_in, p_out, p_sems)

        @pl.when(last)
        def _():
            plan.forward(p_in, p_out, p_sems)

        body(*main_in, *main_out, *main_scr)

        @pl.when(last)
        def _():
            plan.finish(p_in, p_out, p_sems)

    outs = pl.pallas_call(
        wrapped, name=name, grid=grid, in_specs=list(in_specs) + [ANY] * pi, out_specs=list(out_specs) + [ANY] * po,
        out_shape=list(out_shape) + list(plan.out_shape), scratch_shapes=list(scratch) + list(plan.sems),
        input_output_aliases=aliases, compiler_params=_cp(("arbitrary",) * len(grid), vmem),
    )(*args, *plan.inputs)
    return list(outs[:no]), list(outs[no:])


def run_plan(plan, name):
    def body(*refs):
        ins, outs, sems = refs[:len(plan.inputs)], refs[len(plan.inputs):len(plan.inputs) + len(plan.out_shape)], \
            refs[len(plan.inputs) + len(plan.out_shape):]
        plan.start(ins, outs, sems)
        plan.forward(ins, outs, sems)
        plan.finish(ins, outs, sems)

    return pl.pallas_call(body, name=name, in_specs=[ANY] * len(plan.inputs), out_specs=[ANY] * len(plan.out_shape),
                          out_shape=list(plan.out_shape), scratch_shapes=list(plan.sems))(*plan.inputs)


def mm_tn(a, b, name, tn=512, into=None, block0=0, nblocks=None):
    T, K = a.shape
    N = b.shape[1]
    tn = min(tn, N)
    nblocks = nblocks or (N // tn if into is None else into.shape[0])

    def body(a_ref, b_ref, *rest):
        rest[-1][...] = _dot_tn(a_ref[...], b_ref[...]).astype(GRAD_DT)

    args, in_specs, aliases = [a, b], [_resident((T, K)), pl.BlockSpec((T, tn), lambda j: (0, j))], {}
    if into is not None:
        args.append(into)
        in_specs.append(ANY)
        aliases = {2: 0}
    (out,), _ = _call(body, args, name=name, grid=(N // tn,), in_specs=in_specs,
                      out_specs=[pl.BlockSpec((None, K, tn), lambda j: (block0 + j, 0, 0))],
                      out_shape=[S((nblocks, K, tn), GRAD_DT)], sem=("parallel",), vmem=VMEM_LIMIT, aliases=aliases)
    return out


def mm_tn_rows(a, b, name, tk=256, plan=None):
    T, K = a.shape
    N = b.shape[1]
    tk = min(tk, K)

    def body(a_ref, b_ref, o_ref):
        o_ref[...] = _dot_tn(a_ref[...], b_ref[...]).astype(GRAD_DT)

    (out,), sent = _call(body, [a, b], name=name, grid=(K // tk,),
                         in_specs=[pl.BlockSpec((T, tk), lambda i: (0, i)), _resident((T, N))],
                         out_specs=[pl.BlockSpec((tk, N), lambda i: (i, 0))], out_shape=[S((K, N), GRAD_DT)],
                         sem=("parallel",), vmem=VMEM_LIMIT, plan=plan)
    return out, sent


def prep_weights(ws):
    def body(*refs):
        for i in range(len(ws)):
            refs[len(ws) + i][...] = refs[i][...].astype(bf16)

    return pl.pallas_call(body, name="prep_weights", out_shape=[S(w.shape, bf16) for w in ws],
                          compiler_params=_cp(None, VMEM_LIMIT))(*ws)


def in_proj(x, win_g, b_in, plan):
    T = x.shape[0]
    tm = min(512, T)

    def body(x_ref, w_ref, b_ref, o_ref, xb_ref):
        xb = x_ref[...].astype(bf16)
        xb_ref[...] = xb
        for k in range(NDEV):
            cs = slice(k * W, (k + 1) * W)
            o_ref[:, cs] = _dot(xb, w_ref[k]) + b_ref[:, cs]

    return _call(
        body, [x, win_g, b_in], name="in_proj", grid=(T // tm,),
        in_specs=[pl.BlockSpec((tm, D), lambda i: (i, 0)), _resident((NDEV, D, W)), _resident((1, IN_COLS))],
        out_specs=[pl.BlockSpec((tm, IN_COLS), lambda i: (i, 0)), pl.BlockSpec((tm, D), lambda i: (i, 0))],
        out_shape=[S((T, IN_COLS), f32), S((T, D), bf16)], vmem=VMEM_LIMIT, plan=plan)


def to_perm(a, cb0, name):
    T = a.shape[0]
    L = T // NC

    def body(a_ref, o_ref):
        def step(j, carry):
            for q in range(NC // 8):
                o_ref[pl.ds(pl.multiple_of(j * NC, NC) + 8 * q, 8), :] = a_ref[pl.ds(q * 8 * L + j, 8, stride=L), :]
            return carry

        lax.fori_loop(0, L, step, 0)

    return pl.pallas_call(
        body, name=name, grid=(W // LANE,),
        in_specs=[pl.BlockSpec((T, LANE), lambda k: (0, cb0 + k))], out_specs=pl.BlockSpec((T, LANE), lambda k: (0, k)),
        out_shape=S((T, W), f32), compiler_params=_cp(("parallel",), VMEM_LIMIT),
    )(a)


def from_perm(a, name, out_dtype=f32):
    T = a.shape[0]
    L = T // NC

    def body(a_ref, o_ref, s_ref):
        def step(i, acc):
            c, jb = i // (L // 16), i % (L // 16)
            t0 = a_ref[pl.ds(jb * 16 * NC + c, 8, stride=NC), :]
            t1 = a_ref[pl.ds((jb * 16 + 8) * NC + c, 8, stride=NC), :]
            o_ref[pl.ds(pl.multiple_of(i * 16, 16), 16), :] = jnp.concatenate([t0, t1], axis=0).astype(out_dtype)
            return acc + t0 + t1

        acc = lax.fori_loop(0, T // 16, step, jnp.zeros((8, LANE), f32))
        s_ref[...] = jnp.sum(acc, axis=0, keepdims=True)

    return pl.pallas_call(
        body, name=name, grid=(W // LANE,),
        in_specs=[pl.BlockSpec((T, LANE), lambda k: (0, k))],
        out_specs=[pl.BlockSpec((T, LANE), lambda k: (0, k)), pl.BlockSpec((1, LANE), lambda k: (0, k))],
        out_shape=[S((T, W), out_dtype), S((1, W), f32)], compiler_params=_cp(("parallel",), VMEM_LIMIT),
    )(a)


def _disc(lr, li, ldt):
    dt = jnp.exp(ldt)
    mag = jnp.exp(lr * dt)
    lbr = mag * jnp.cos(li * dt)
    lbi = mag * jnp.sin(li * dt)
    den = lr * lr + li * li
    nr = lbr - 1.0
    return lbr, lbi, (nr * lr + lbi * li) / den, (lbi * lr - nr * li) / den


def _per_channel(f):
    return jnp.broadcast_to(f[:, None, :], (NG, GC, NP)).reshape(NG * GC, NP)


def ssm_params(lam_re, lam_im, log_dt, br, bi):
    def body(lr_ref, li_ref, ldt_ref, br_ref, bi_ref, lbr_ref, lbi_ref, fr_ref, fi_ref, bbr_ref, bbi_ref):
        lbr, lbi, fr, fi = _disc(lr_ref[...], li_ref[...], ldt_ref[...])
        lbr_ref[...], lbi_ref[...], fr_ref[...], fi_ref[...] = lbr, lbi, fr, fi
        fr_, fi_, br_, bi_ = _per_channel(fr), _per_channel(fi), br_ref[...], bi_ref[...]
        bbr_ref[...] = fr_ * br_ - fi_ * bi_
        bbi_ref[...] = fr_ * bi_ + fi_ * br_

    return pl.pallas_call(body, name="ssm_params", out_shape=[S((NG, NP), f32)] * 4 + [S((NG * GC, NP), f32)] * 2)(
        lam_re, lam_im, log_dt, br, bi)


def _scan_body(T):
    L = T // NC
    RB = min(512, T)
    nsq = int(round(math.log2(L)))
    assert 2 ** nsq == L and T % RB == 0 and L % 16 == 0

    def rows(i):
        return pl.ds(pl.multiple_of(i * RB, RB), RB)

    def tile(j):
        return pl.ds(pl.multiple_of(j * NC, NC), NC)

    def forward_states(u_ref, wb_ref, lbr_ref, lbi_ref, sre, sim, ere, eim):
        def bproj(i, carry):
            bu = _dot(u_ref[rows(i), :].astype(bf16), wb_ref[...])
            sre[rows(i), :] = bu[:, :SW]
            sim[rows(i), :] = bu[:, SW:]
            return carry

        lax.fori_loop(0, T // RB, bproj, 0)
        for lb in range(SW // LANE):
            ls = slice(lb * LANE, (lb + 1) * LANE)
            ar = jnp.broadcast_to(lbr_ref[:, ls], (NC, LANE))
            ai = jnp.broadcast_to(lbi_ref[:, ls], (NC, LANE))

            def step(j, carry):
                xr, xi = carry
                nr = ar * xr - ai * xi + sre[tile(j), ls]
                ni = ar * xi + ai * xr + sim[tile(j), ls]
                sre[tile(j), ls] = nr
                sim[tile(j), ls] = ni
                return nr, ni

            zero = jnp.zeros((NC, LANE), f32)
            lax.fori_loop(0, L, step, (zero, zero), unroll=2)
            pr, pi = lbr_ref[:, ls], lbi_ref[:, ls]
            for _ in range(nsq):
                pr, pi = pr * pr - pi * pi, 2.0 * pr * pi
            er = jnp.zeros((1, LANE), f32)
            ei = er
            ere[0:1, ls] = er
            eim[0:1, ls] = ei
            base = (L - 1) * NC
            for c in range(1, NC):
                lr_ = sre[base + c - 1:base + c, ls]
                li_ = sim[base + c - 1:base + c, ls]
                er, ei = lr_ + pr * er - pi * ei, li_ + pr * ei + pi * er
                ere[c:c + 1, ls] = er
                eim[c:c + 1, ls] = ei
            e_r, e_i = ere[:, ls].reshape(NC // 8, 8, LANE), eim[:, ls].reshape(NC // 8, 8, LANE)
            ar8, ai8 = ar[0:8], ai[0:8]

            def fix(j, carry):
                pwr, pwi = carry
                xr = sre[tile(j), ls].reshape(NC // 8, 8, LANE) + (pwr * e_r - pwi * e_i)
                xi = sim[tile(j), ls].reshape(NC // 8, 8, LANE) + (pwr * e_i + pwi * e_r)
                sre[tile(j), ls] = xr.reshape(NC, LANE)
                sim[tile(j), ls] = xi.reshape(NC, LANE)
                return pwr * ar8 - pwi * ai8, pwr * ai8 + pwi * ar8

            lax.fori_loop(0, L, fix, (ar8, ai8), unroll=2)

    return L, RB, nsq, rows, tile, forward_states


def ssm_fwd(u_p, wb, wc, lbr, lbi, dsk, plan):
    T = u_p.shape[0]
    L, RB, nsq, rows, tile, forward_states = _scan_body(T)

    def body(u_ref, wb_ref, wc_ref, lbr_ref, lbi_ref, d_ref, y_ref, sre, sim, ere, eim):
        forward_states(u_ref, wb_ref, lbr_ref, lbi_ref, sre, sim, ere, eim)

        def cproj(i, carry):
            y = _dot(sre[rows(i), :].astype(bf16), wc_ref[0:SW, :]) + _dot(sim[rows(i), :].astype(bf16), wc_ref[SW:, :])
            y_ref[rows(i), :] = y + d_ref[...] * u_ref[rows(i), :]
            return carry

        lax.fori_loop(0, T // RB, cproj, 0)

    slab = pl.BlockSpec((T, LANE), lambda k: (0, k))
    return _call(
        body, [u_p, wb, wc, lbr, lbi, dsk], name="ssm_fwd", grid=(W // LANE,),
        in_specs=[slab, pl.BlockSpec((None, LANE, 2 * SW), lambda k: (k, 0, 0)),
                  pl.BlockSpec((None, 2 * SW, LANE), lambda k: (k, 0, 0)),
                  pl.BlockSpec((None, 1, SW), lambda k: (k, 0, 0)), pl.BlockSpec((None, 1, SW), lambda k: (k, 0, 0)),
                  pl.BlockSpec((None, 1, LANE), lambda k: (k, 0, 0))],
        out_specs=[slab], out_shape=[S((T, W), f32)],
        scratch=[pltpu.VMEM((T, SW), f32), pltpu.VMEM((T, SW), f32), pltpu.VMEM((NC, SW), f32), pltpu.VMEM((NC, SW), f32)],
        vmem=VMEM_LIMIT, plan=plan)


def ssm_bwd(u_p, dy_p, wb, wbT, wcT, lbr, lbi, dsk, plan):
    T = u_p.shape[0]
    L, RB, nsq, rows, tile, forward_states = _scan_body(T)

    def body(u_ref, dy_ref, wb_ref, wbT_ref, wcT_ref, lbr_ref, lbi_ref, d_ref,
             du_ref, dwb_ref, dwc_ref, dlr_ref, dli_ref, dd_ref, sre, sim, gre, gim, ere, eim):
        forward_states(u_ref, wb_ref, lbr_ref, lbi_ref, sre, sim, ere, eim)

        def dstate(i, carry):
            g = _dot(dy_ref[rows(i), :].astype(bf16), wcT_ref[...])
            gre[rows(i), :] = g[:, :SW]
            gim[rows(i), :] = g[:, SW:]
            return carry

        lax.fori_loop(0, T // RB, dstate, 0)
        row = lax.broadcasted_iota(jnp.int32, (NC, LANE), 0)
        for lb in range(SW // LANE):
            ls = slice(lb * LANE, (lb + 1) * LANE)
            ar = jnp.broadcast_to(lbr_ref[:, ls], (NC, LANE))
            ai = jnp.broadcast_to(lbi_ref[:, ls], (NC, LANE))

            def step(i, carry):
                gr, gi = carry
                j = L - 1 - i
                nr = ar * gr + ai * gi + gre[tile(j), ls]
                ni = ar * gi - ai * gr + gim[tile(j), ls]
                gre[tile(j), ls] = nr
                gim[tile(j), ls] = ni
                return nr, ni

            zero = jnp.zeros((NC, LANE), f32)
            lax.fori_loop(0, L, step, (zero, zero), unroll=2)
            pr, pi = lbr_ref[:, ls], -lbi_ref[:, ls]
            for _ in range(nsq):
                pr, pi = pr * pr - pi * pi, 2.0 * pr * pi
            er = jnp.zeros((1, LANE), f32)
            ei = er
            ere[NC - 1:NC, ls] = er
            eim[NC - 1:NC, ls] = ei
            for c in range(NC - 2, -1, -1):
                lr_ = gre[c + 1:c + 2, ls]
                li_ = gim[c + 1:c + 2, ls]
                er, ei = lr_ + pr * er - pi * ei, li_ + pr * ei + pi * er
                ere[c:c + 1, ls] = er
                eim[c:c + 1, ls] = ei
            e_r, e_i = ere[:, ls].reshape(NC // 8, 8, LANE), eim[:, ls].reshape(NC // 8, 8, LANE)
            ar8, ai8 = ar[0:8], ai[0:8]

            def fixed(j, pwr, pwi):
                gr = (gre[tile(j), ls].reshape(NC // 8, 8, LANE) + (pwr * e_r - pwi * e_i)).reshape(NC, LANE)
                gi = (gim[tile(j), ls].reshape(NC // 8, 8, LANE) + (pwr * e_i + pwi * e_r)).reshape(NC, LANE)
                gre[tile(j), ls] = gr
                gim[tile(j), ls] = gi
                return gr, gi

            def fix(i, carry):
                pwr, pwi, accr, acci = carry
                j = L - 1 - i
                gr, gi = fixed(j, pwr, pwi)
                xr, xi = sre[tile(j - 1), ls], sim[tile(j - 1), ls]
                return (pwr * ar8 + pwi * ai8, pwi * ar8 - pwr * ai8,
                        accr + gr * xr + gi * xi, acci + gi * xr - gr * xi)

            pwr, pwi, accr, acci = lax.fori_loop(0, L - 1, fix, (ar8, -ai8, zero, zero))
            gr, gi = fixed(0, pwr, pwi)
            xr = jnp.where(row == 0, 0.0, pltpu.roll(sre[tile(L - 1), ls], 1, axis=0))
            xi = jnp.where(row == 0, 0.0, pltpu.roll(sim[tile(L - 1), ls], 1, axis=0))
            accr = accr + gr * xr + gi * xi
            acci = acci + gi * xr - gr * xi
            dlr_ref[:, ls] = jnp.sum(accr, axis=0, keepdims=True)
            dli_ref[:, ls] = jnp.sum(acci, axis=0, keepdims=True)

        dwb_ref[...] = jnp.zeros_like(dwb_ref)
        dwc_ref[...] = jnp.zeros_like(dwc_ref)
        dd_ref[...] = jnp.zeros_like(dd_ref)

        def finish(i, carry):
            u32, dy32 = u_ref[rows(i), :], dy_ref[rows(i), :]
            ub, dyb = u32.astype(bf16), dy32.astype(bf16)
            gr, gi = gre[rows(i), :].astype(bf16), gim[rows(i), :].astype(bf16)
            du_ref[rows(i), :] = _dot(gr, wbT_ref[0:SW, :]) + _dot(gi, wbT_ref[SW:, :]) + dy32 * d_ref[...]
            dwb_ref[:, 0:SW] += _dot_tn(ub, gr)
            dwb_ref[:, SW:] += _dot_tn(ub, gi)
            dwc_ref[:, 0:SW] += _dot_tn(dyb, sre[rows(i), :].astype(bf16))
            dwc_ref[:, SW:] += _dot_tn(dyb, sim[rows(i), :].astype(bf16))
            dd_ref[...] += jnp.sum(dy32 * u32, axis=0, keepdims=True)
            return carry

        lax.fori_loop(0, T // RB, finish, 0)

    slab = pl.BlockSpec((T, LANE), lambda k: (0, k))
    wide = pl.BlockSpec((None, LANE, 2 * SW), lambda k: (k, 0, 0))
    tall = pl.BlockSpec((None, 2 * SW, LANE), lambda k: (k, 0, 0))
    vec = pl.BlockSpec((None, 1, SW), lambda k: (k, 0, 0))
    vecd = pl.BlockSpec((None, 1, LANE), lambda k: (k, 0, 0))
    nslab = W // LANE
    return _call(
        body, [u_p, dy_p, wb, wbT, wcT, lbr, lbi, dsk], name="ssm_bwd", grid=(nslab,),
        in_specs=[slab, slab, wide, tall, wide, vec, vec, vecd],
        out_specs=[slab, wide, wide, vec, vec, vecd],
        out_shape=[S((T, W), f32), S((nslab, LANE, 2 * SW), f32), S((nslab, LANE, 2 * SW), f32),
                   S((nslab, 1, SW), f32), S((nslab, 1, SW), f32), S((nslab, 1, LANE), f32)],
        scratch=[pltpu.VMEM((T, SW), f32)] * 4 + [pltpu.VMEM((NC, SW), f32)] * 2, vmem=VMEM_LIMIT, plan=plan)


def glu_fwd(yn, glu_w, glu_b):
    T = yn.shape[0]
    tm = min(512, T)

    def body(y_ref, w_ref, b_ref, o_ref):
        g = _gelu(y_ref[...])
        o_ref[...] = (g * _sigmoid(_dot(g.astype(bf16), w_ref[...]) + b_ref[...])).astype(bf16)

    return pl.pallas_call(
        body, name="glu_fwd", grid=(T // tm,),
        in_specs=[pl.BlockSpec((tm, W), lambda i: (i, 0)), pl.BlockSpec((W, W), lambda i: (0, 0)), pl.BlockSpec((1, W), lambda i: (0, 0))],
        out_specs=pl.BlockSpec((tm, W), lambda i: (i, 0)), out_shape=S((T, W), bf16), compiler_params=_cp(("parallel",)),
    )(yn, glu_w, glu_b)


def _shift_rows(cur, prev8, k):
    return pltpu.roll(jnp.concatenate([prev8, cur], axis=0), k, axis=0)[8:]


def _lift_rows(cur, next8, k):
    n = cur.shape[0]
    return pltpu.roll(jnp.concatenate([cur, next8], axis=0), n + 8 - k, axis=0)[:n]


def conv_fwd(proj, conv_w):
    T = proj.shape[0]
    RB = min(512, T)

    def body(h_ref, c_ref, b_ref, w_ref, o_ref):
        w0, w1, w2 = w_ref[0:1, :], w_ref[1:2, :], w_ref[2:3, :]

        def blk(i, carry):
            r0 = pl.multiple_of(i * RB, RB)
            rs = pl.ds(r0, RB)
            ch = c_ref[rs, :] * h_ref[rs, :]
            pr = pl.ds(jnp.maximum(r0 - 8, 0), 8)
            prev = jnp.where(i > 0, c_ref[pr, :] * h_ref[pr, :], 0.0)
            z = w2 * ch + w1 * _shift_rows(ch, prev, 1) + w0 * _shift_rows(ch, prev, 2)
            o_ref[rs, :] = (b_ref[rs, :] * z).astype(bf16)
            return carry

        lax.fori_loop(0, T // RB, blk, 0)

    nb = W // LANE
    return pl.pallas_call(
        body, name="conv_fwd", grid=(nb,),
        in_specs=[pl.BlockSpec((T, LANE), lambda k: (0, nb + k)), pl.BlockSpec((T, LANE), lambda k: (0, 2 * nb + k)),
                  pl.BlockSpec((T, LANE), lambda k: (0, 3 * nb + k)), pl.BlockSpec((3, LANE), lambda k: (0, k))],
        out_specs=pl.BlockSpec((T, LANE), lambda k: (0, k)), out_shape=S((T, W), bf16),
        compiler_params=_cp(("parallel",), VMEM_LIMIT),
    )(proj, proj, proj, conv_w)


def _dense_columns(blocks_ref, dense_ref):
    for k in range(NDEV):
        dense_ref[:, k * LANE:(k + 1) * LANE] = blocks_ref[k]


def merge_fwd(ya, yb, wso, wco, proj, plan):
    T = ya.shape[0]
    tm = min(1024, T)

    def body(ya_ref, yb_ref, wa_ref, wb_ref, ga_ref, gb_ref, o_ref, wa_s, wb_s):
        @pl.when(pl.program_id(0) == 0)
        def _():
            _dense_columns(wa_ref, wa_s)
            _dense_columns(wb_ref, wb_s)

        o_ref[...] = (_sigmoid(ga_ref[...]) * _dot(ya_ref[...], wa_s[...])
                      + _sigmoid(gb_ref[...]) * _dot(yb_ref[...], wb_s[...])).astype(bf16)

    act = pl.BlockSpec((tm, W), lambda i: (i, 0))
    return _call(
        body, [ya, yb, wso, wco, proj, proj], name="merge_fwd", grid=(T // tm,),
        in_specs=[act, act, _resident((NDEV, W, LANE)), _resident((NDEV, W, LANE)),
                  pl.BlockSpec((tm, D), lambda i: (i, 2)), pl.BlockSpec((tm, D), lambda i: (i, 3))],
        out_specs=[pl.BlockSpec((tm, D), lambda i: (i, 0))], out_shape=[S((T, D), bf16)],
        scratch=[pltpu.VMEM((W, D), bf16), pltpu.VMEM((W, D), bf16)], vmem=VMEM_LIMIT, plan=plan)


def mix_ln1(merged, w_o, x, g1, b1):
    T = x.shape[0]
    tm = min(512, T)

    def body(m_ref, w_ref, x_ref, g_ref, b_ref, r_ref, x1_ref):
        r = ALPHA * x_ref[...] + _dot(m_ref[...], w_ref[...])
        r_ref[...] = r
        xhat, _ = _ln_stats(r)
        x1_ref[...] = (xhat * g_ref[...] + b_ref[...]).astype(bf16)

    row = pl.BlockSpec((tm, D), lambda i: (i, 0))
    vec = pl.BlockSpec((1, D), lambda i: (0, 0))
    return pl.pallas_call(
        body, name="mix_ln1", grid=(T // tm,),
        in_specs=[row, _resident((D, D)), row, vec, vec],
        out_specs=[row, row], out_shape=[S((T, D), f32), S((T, D), bf16)], compiler_params=_cp(("parallel",), VMEM_LIMIT),
    )(merged, w_o, x, g1, b1)


FT = 256


def gate_up(x1b, wgT, wuT):
    T = x1b.shape[0]
    tm = min(512, T)

    def body(x_ref, wg_ref, wu_ref, g_ref, u_ref, h_ref):
        x = x_ref[...]
        for n in range(F // FT):
            cs = slice(n * FT, (n + 1) * FT)
            g = _dot_nt(x, wg_ref[cs, :])
            u = _dot_nt(x, wu_ref[cs, :])
            g_ref[:, cs] = g.astype(bf16)
            u_ref[:, cs] = u.astype(bf16)
            h_ref[:, cs] = (g * _sigmoid(g) * u).astype(bf16)

    osp = pl.BlockSpec((tm, F), lambda i: (i, 0))
    return pl.pallas_call(
        body, name="gate_up", grid=(T // tm,),
        in_specs=[pl.BlockSpec((tm, D), lambda i: (i, 0)), _resident((F, D)), _resident((F, D))],
        out_specs=[osp, osp, osp], out_shape=[S((T, F), bf16)] * 3, compiler_params=_cp(("parallel",), VMEM_LIMIT),
    )(x1b, wgT, wuT)


def down_loss(hid, w_down, r1, g1, b1, g2, b2, target):
    T = hid.shape[0]
    tm = min(512, T)

    def body(h_ref, w_ref, r1_ref, g1_ref, b1_ref, g2_ref, b2_ref, t_ref, dr_ref, drb_ref, loss_ref, dg_ref, db_ref):
        @pl.when(pl.program_id(0) == 0)
        def _():
            loss_ref[...] = jnp.zeros_like(loss_ref)
            dg_ref[...] = jnp.zeros_like(dg_ref)
            db_ref[...] = jnp.zeros_like(db_ref)

        xh1, _ = _ln_stats(r1_ref[...])
        x1 = xh1 * g1_ref[...] + b1_ref[...]
        r2 = ALPHA * x1 + _dot(h_ref[...], w_ref[...])
        xh2, rstd2 = _ln_stats(r2)
        err = xh2 * g2_ref[...] + b2_ref[...] - t_ref[...]
        loss_ref[...] += jnp.sum(jnp.mean(err * err, axis=-1, keepdims=True), axis=0, keepdims=True)
        dy = err * (1.0 / D)
        dg_ref[...] += jnp.sum(dy * xh2, axis=0, keepdims=True)
        db_ref[...] += jnp.sum(dy, axis=0, keepdims=True)
        dr = _ln_bwd(dy, xh2, rstd2, g2_ref[...])
        dr_ref[...] = dr
        drb_ref[...] = dr.astype(bf16)

    row = pl.BlockSpec((tm, D), lambda i: (i, 0))
    vec = pl.BlockSpec((1, D), lambda i: (0, 0))
    return pl.pallas_call(
        body, name="down_loss", grid=(T // tm,),
        in_specs=[pl.BlockSpec((tm, F), lambda i: (i, 0)), _resident((F, D)), row, vec, vec, vec, vec, row],
        out_specs=[row, row, pl.BlockSpec((1, 1), lambda i: (0, 0)), vec, vec],
        out_shape=[S((T, D), f32), S((T, D), bf16), S((1, 1), f32), S((1, D), f32), S((1, D), f32)],
        compiler_params=_cp(("arbitrary",), VMEM_LIMIT),
    )(hid, w_down, r1, g1, b1, g2, b2, target)


def ffn_bwd_act(dffn, w_down, gate, up):
    T = dffn.shape[0]
    tm = min(512, T)

    def body(d_ref, w_ref, g_ref, u_ref, dg_ref, du_ref):
        d = d_ref[...]
        for n in range(F // FT):
            cs = slice(n * FT, (n + 1) * FT)
            dh = _dot_nt(d, w_ref[cs, :])
            g, u = g_ref[:, cs].astype(f32), u_ref[:, cs].astype(f32)
            sg = _sigmoid(g)
            du_ref[:, cs] = (dh * g * sg).astype(bf16)
            dg_ref[:, cs] = (dh * u * sg * (1.0 + g * (1.0 - sg))).astype(bf16)

    osp = pl.BlockSpec((tm, F), lambda i: (i, 0))
    return pl.pallas_call(
        body, name="ffn_bwd_act", grid=(T // tm,),
        in_specs=[pl.BlockSpec((tm, D), lambda i: (i, 0)), _resident((F, D)), osp, osp],
        out_specs=[osp, osp], out_shape=[S((T, F), bf16)] * 2, compiler_params=_cp(("parallel",), VMEM_LIMIT),
    )(dffn, w_down, gate, up)


def ffn_bwd_x(dgate, dup, wgT, wuT, dr2, r1, g1, plan):
    T = dr2.shape[0]
    tm = min(512, T)

    def body(dg_ref, du_ref, wg_ref, wu_ref, dr2_ref, r1_ref, g1_ref, dr_ref, drb_ref, dgam_ref, dbet_ref):
        @pl.when(pl.program_id(0) == 0)
        def _():
            dgam_ref[...] = jnp.zeros_like(dgam_ref)
            dbet_ref[...] = jnp.zeros_like(dbet_ref)

        dx1 = ALPHA * dr2_ref[...] + _dot(dg_ref[...], wg_ref[...]) + _dot(du_ref[...], wu_ref[...])
        xh, rstd = _ln_stats(r1_ref[...])
        dgam_ref[...] += jnp.sum(dx1 * xh, axis=0, keepdims=True)
        dbet_ref[...] += jnp.sum(dx1, axis=0, keepdims=True)
        dr = _ln_bwd(dx1, xh, rstd, g1_ref[...])
        dr_ref[...] = dr
        drb_ref[...] = dr.astype(bf16)

    row = pl.BlockSpec((tm, D), lambda i: (i, 0))
    wide = pl.BlockSpec((tm, F), lambda i: (i, 0))
    wsp = _resident((F, D))
    vec = pl.BlockSpec((1, D), lambda i: (0, 0))
    return _call(
        body, [dgate, dup, wgT, wuT, dr2, r1, g1], name="ffn_bwd_x", grid=(T // tm,),
        in_specs=[wide, wide, wsp, wsp, row, row, vec],
        out_specs=[row, row, vec, vec], out_shape=[S((T, D), f32), S((T, D), bf16), S((1, D), f32), S((1, D), f32)],
        vmem=VMEM_LIMIT, plan=plan)


def merge_bwd(dmix, w_o, ya, yb, wso, wco, proj, plan):
    T = dmix.shape[0]
    tm = min(512, T)

    def body(dm_ref, wo_ref, ya_ref, yb_ref, wa_ref, wb_ref, ga_ref, gb_ref, dya_ref, dyb_ref, dga_ref, dgb_ref, sa_ref, sb_ref,
             wa_s, wb_s):
        @pl.when(pl.program_id(0) == 0)
        def _():
            _dense_columns(wa_ref, wa_s)
            _dense_columns(wb_ref, wb_s)

        dmer = _dot_nt(dm_ref[...], wo_ref[...])
        sa, sb = _sigmoid(ga_ref[...]), _sigmoid(gb_ref[...])
        dya_ref[...] = (dmer * sa).astype(bf16)
        dyb_ref[...] = (dmer * sb).astype(bf16)
        dga = dmer * _dot(ya_ref[...], wa_s[...]) * sa * (1.0 - sa)
        dgb = dmer * _dot(yb_ref[...], wb_s[...]) * sb * (1.0 - sb)
        dga_ref[...] = dga.astype(bf16)
        dgb_ref[...] = dgb.astype(bf16)
        sa_ref[...] = jnp.sum(dga, axis=0, keepdims=True)
        sb_ref[...] = jnp.sum(dgb, axis=0, keepdims=True)

    act = pl.BlockSpec((tm, W), lambda i: (i, 0))
    osp = pl.BlockSpec((tm, D), lambda i: (i, 0))
    ssp = pl.BlockSpec((None, 1, D), lambda i: (i, 0, 0))
    return _call(
        body, [dmix, w_o, ya, yb, wso, wco, proj, proj], name="merge_bwd", grid=(T // tm,),
        in_specs=[osp, _resident((D, D)), act, act, _resident((NDEV, W, LANE)), _resident((NDEV, W, LANE)),
                  pl.BlockSpec((tm, D), lambda i: (i, 2)), pl.BlockSpec((tm, D), lambda i: (i, 3))],
        out_specs=[osp, osp, osp, osp, ssp, ssp],
        out_shape=[S((T, D), bf16)] * 4 + [S((T // tm, 1, D), f32)] * 2,
        scratch=[pltpu.VMEM((W, D), bf16), pltpu.VMEM((W, D), bf16)], vmem=VMEM_LIMIT, plan=plan)


def branches_bwd_x(dYA, dYB, wso, wco, plan):
    T = dYA.shape[0]
    tm = min(1024, T)

    def body(da_ref, db_ref, wa_ref, wb_ref, oa_ref, ob_ref, wa_s, wb_s):
        @pl.when(pl.program_id(0) == 0)
        def _():
            _dense_columns(wa_ref, wa_s)
            _dense_columns(wb_ref, wb_s)

        oa_ref[...] = _dot_nt(da_ref[...], wa_s[...])
        ob_ref[...] = _dot_nt(db_ref[...], wb_s[...])

    row = pl.BlockSpec((tm, D), lambda i: (i, 0))
    osp = pl.BlockSpec((tm, W), lambda i: (i, 0))
    return _call(
        body, [dYA, dYB, wso, wco], name="branches_bwd_x", grid=(T // tm,),
        in_specs=[row, row, _resident((NDEV, W, LANE)), _resident((NDEV, W, LANE))],
        out_specs=[osp, osp], out_shape=[S((T, W), f32)] * 2,
        scratch=[pltpu.VMEM((W, D), bf16), pltpu.VMEM((W, D), bf16)], vmem=VMEM_LIMIT, plan=plan)


def branch_bwd_w(act, dY, name):
    T = act.shape[0]
    tk = W // 2

    def body(a_ref, d_ref, o_ref):
        res = _dot_tn(a_ref[...], d_ref[...])
        for k in range(NDEV):
            o_ref[k] = res[:, k * LANE:(k + 1) * LANE].astype(o_ref.dtype)

    return pl.pallas_call(
        body, name=name, grid=(W // tk,),
        in_specs=[pl.BlockSpec((T, tk), lambda i: (0, i)), _resident((T, D))],
        out_specs=pl.BlockSpec((NDEV, tk, LANE), lambda i: (0, i, 0)), out_shape=S((NDEV, W, LANE), GRAD_DT),
        compiler_params=_cp(("parallel",), VMEM_LIMIT),
    )(act, dY)


def glu_bwd(yn, dya, glu_w, glu_b):
    T = yn.shape[0]
    tm = min(512, T)

    def body(y_ref, d_ref, w_ref, b_ref, dy_ref, dsp_ref, g_ref, db_ref):
        @pl.when(pl.program_id(0) == 0)
        def _():
            db_ref[...] = jnp.zeros_like(db_ref)

        y, dya_ = y_ref[...], d_ref[...]
        g = _gelu(y)
        gb = g.astype(bf16)
        s = _sigmoid(_dot(gb, w_ref[...]) + b_ref[...])
        dsp = dya_ * g * s * (1.0 - s)
        dspb = dsp.astype(bf16)
        dg = dya_ * s + _dot_nt(dspb, w_ref[...])
        dy_ref[...] = dg * _gelu_grad(y)
        dsp_ref[...] = dspb
        g_ref[...] = gb
        db_ref[...] += jnp.sum(dsp, axis=0, keepdims=True)

    row = pl.BlockSpec((tm, W), lambda i: (i, 0))
    vec = pl.BlockSpec((1, W), lambda i: (0, 0))
    return pl.pallas_call(
        body, name="glu_bwd", grid=(T // tm,),
        in_specs=[row, row, pl.BlockSpec((W, W), lambda i: (0, 0)), vec],
        out_specs=[row, row, row, vec], out_shape=[S((T, W), f32), S((T, W), bf16), S((T, W), bf16), S((1, W), f32)],
        compiler_params=_cp(("arbitrary",)),
    )(yn, dya, glu_w, glu_b)


def conv_bwd(proj, dyb, conv_w):
    T = proj.shape[0]
    RB = min(512, T)
    nrb = T // RB

    def body(h_ref, c_ref, b_ref, d_ref, w_ref, dh_ref, dc_ref, db_ref, dw_ref, s_ref):
        w0, w1, w2 = w_ref[0:1, :], w_ref[1:2, :], w_ref[2:3, :]

        def blk(i, carry):
            a0, a1, a2, sh, sc, sb = carry
            r0 = pl.multiple_of(i * RB, RB)
            rs = pl.ds(r0, RB)
            h, cg, bg, dyb_ = h_ref[rs, :], c_ref[rs, :], b_ref[rs, :], d_ref[rs, :]
            ch = cg * h
            pr = pl.ds(jnp.maximum(r0 - 8, 0), 8)
            prev = jnp.where(i > 0, c_ref[pr, :] * h_ref[pr, :], 0.0)
            ch1, ch2 = _shift_rows(ch, prev, 1), _shift_rows(ch, prev, 2)
            dbg = dyb_ * (w2 * ch + w1 * ch1 + w0 * ch2)
            db_ref[rs, :] = dbg.astype(bf16)
            dz = dyb_ * bg
            nx = pl.ds(jnp.minimum(r0 + RB, T - 8), 8)
            nxt = jnp.where(i < nrb - 1, d_ref[nx, :] * b_ref[nx, :], 0.0)
            dch = w2 * dz + w1 * _lift_rows(dz, nxt, 1) + w0 * _lift_rows(dz, nxt, 2)
            dcg, dh = dch * h, dch * cg
            dc_ref[rs, :] = dcg.astype(bf16)
            dh_ref[rs, :] = dh.astype(bf16)
            col = lambda v: jnp.sum(v, axis=0, keepdims=True)
            return (a0 + col(dz * ch2), a1 + col(dz * ch1), a2 + col(dz * ch), sh + col(dh), sc + col(dcg), sb + col(dbg))

        zero = jnp.zeros((1, LANE), f32)
        a0, a1, a2, sh, sc, sb = lax.fori_loop(0, nrb, blk, (zero,) * 6)
        dw_ref[0:1, :] = a0
        dw_ref[1:2, :] = a1
        dw_ref[2:3, :] = a2
        s_ref[0:1, :] = sh
        s_ref[1:2, :] = sc
        s_ref[2:3, :] = sb

    nb = W // LANE
    slab = pl.BlockSpec((T, LANE), lambda k: (0, k))
    three = pl.BlockSpec((3, LANE), lambda k: (0, k))
    return pl.pallas_call(
        body, name="conv_bwd", grid=(nb,),
        in_specs=[pl.BlockSpec((T, LANE), lambda k: (0, nb + k)), pl.BlockSpec((T, LANE), lambda k: (0, 2 * nb + k)),
                  pl.BlockSpec((T, LANE), lambda k: (0, 3 * nb + k)), slab, three],
        out_specs=[slab, slab, slab, three, three],
        out_shape=[S((T, W), bf16)] * 3 + [S((3, W), f32)] * 2, compiler_params=_cp(("parallel",), VMEM_LIMIT),
    )(proj, proj, proj, dyb, conv_w)


def in_proj_bwd_x(parts, win_g, base, scale, name, plan=None):
    T = base.shape[0]
    tm = min(512, T)
    n = len(parts)

    def body(*refs):
        p_refs, w_ref, b_ref, o_ref = refs[:n], refs[n], refs[n + 1], refs[n + 2]
        acc = scale * b_ref[...]
        for p_ref, (_, _, k) in zip(p_refs, parts):
            acc += _dot_nt(p_ref[...], w_ref[k])
        o_ref[...] = acc

    row = pl.BlockSpec((tm, D), lambda i: (i, 0))
    p_specs = [pl.BlockSpec((tm, W), (lambda i, cb=cb: (i, cb))) for _, cb, _ in parts]
    return _call(
        body, [a for a, _, _ in parts] + [win_g, base], name=name, grid=(T // tm,),
        in_specs=p_specs + [_resident((NDEV, D, W)), row],
        out_specs=[row], out_shape=[S((T, D), f32)], vmem=VMEM_LIMIT, plan=plan)


def ssm_param_bwd(lam_re, lam_im, log_dt, fr, fi, br, bi, dbbr, dbbi, dlbr, dlbi):
    def body(lr_ref, li_ref, ldt_ref, fr_ref, fi_ref, br_ref, bi_ref, dr_ref, di_ref, dlbr_ref, dlbi_ref,
             dbr_ref, dbi_ref, dlr_ref, dli_ref, dldt_ref):
        fr_, fi_ = _per_channel(fr_ref[...]), _per_channel(fi_ref[...])
        br_, bi_, dr, di = br_ref[...], bi_ref[...], dr_ref[...], di_ref[...]
        dbr_ref[...] = fr_ * dr + fi_ * di
        dbi_ref[...] = fr_ * di - fi_ * dr
        dfr = jnp.sum((dr * br_ + di * bi_).reshape(NG, GC, NP), axis=1)
        dfi = jnp.sum((di * br_ - dr * bi_).reshape(NG, GC, NP), axis=1)
        _, vjp = jax.vjp(_disc, lr_ref[...], li_ref[...], ldt_ref[...])
        dlr_ref[...], dli_ref[...], dldt = vjp((dlbr_ref[...], dlbi_ref[...], dfr, dfi))
        dldt_ref[...] = _transpose_exact(dldt)

    return pl.pallas_call(
        body, name="ssm_param_bwd",
        out_shape=[S((NG * GC, NP), f32)] * 2 + [S((NG, NP), f32)] * 2 + [S((1, NG), f32)])(
        lam_re, lam_im, log_dt, fr, fi, br, bi, dbbr, dbbi, dlbr, dlbi)


def _adam(w, g, m, v):
    m = ADAM_B1 * m + (1.0 - ADAM_B1) * g
    v = ADAM_B2 * v + (1.0 - ADAM_B2) * (g * g)
    m_hat = m / (1.0 - ADAM_B1 ** ADAM_STEP)
    v_hat = v / (1.0 - ADAM_B2 ** ADAM_STEP)
    return -ADAM_LR * (m_hat / (jnp.sqrt(v_hat) + ADAM_EPS) + ADAM_WD * w), m, v


def adam_update(w, m, v, contrib, name, rows_per_block=None):
    R, C = w.shape
    n = contrib.shape[0]
    tr = min(rows_per_block or R, R)

    def body(w_ref, m_ref, v_ref, c_ref, g_ref, d_ref, nm_ref, nv_ref):
        g = c_ref[0].astype(f32)
        for k in range(1, n):
            g = g + c_ref[k].astype(f32)
        g_ref[...] = g
        d_ref[...], nm_ref[...], nv_ref[...] = _adam(w_ref[...], g, m_ref[...], v_ref[...])

    blk = pl.BlockSpec((tr, C), lambda i: (i, 0))
    return pl.pallas_call(
        body, name=name, grid=(R // tr,), in_specs=[blk, blk, blk, pl.BlockSpec((n, tr, C), lambda i: (0, i, 0))],
        out_specs=[blk] * 4, out_shape=[S((R, C), f32)] * 4, compiler_params=_cp(("parallel",), VMEM_LIMIT),
    )(w, m, v, contrib)


_ROWVEC = (("b_in", IN_COLS), ("ssm_d", W), ("glu_b", W), ("ln1_g", D), ("ln1_b", D), ("ln2_g", D), ("ln2_b", D))
_HALF = NG * GC // 2
_PACK = {}
_r = 0
for _n, _k in _ROWVEC:
    _PACK[_n] = _r
    _r += _k // LANE
for _n, _rows in (("ssm_lambda", NG), ("scalars", 8), ("ssm_b_re", _HALF), ("ssm_b_im", _HALF), ("ssm_c_re", _HALF),
                  ("ssm_c_im", _HALF), ("conv_w", 16)):
    _PACK[_n] = _r
    _r += _rows
PACK_ROWS = _r
assert PACK_ROWS % 8 == 0
_SMALL = ("b_in", "ssm_lambda_re", "ssm_lambda_im", "ssm_log_dt", "ssm_b_re", "ssm_b_im", "ssm_c_re", "ssm_c_im",
          "ssm_d", "glu_b", "ln1_g", "ln1_b", "ln2_g", "ln2_b")


def pack_grads(su, shcb, sga, sgb, dd, dglu_b, dln1_g, dln1_b, dln2_g, dln2_b, dlam_re, dlam_im, dldt, sqerr, dbr, dbi,
               dc_re, dc_im, dconv):
    nI = sga.shape[0]

    def body(su_ref, sh_ref, sga_ref, sgb_ref, dd_ref, gb_ref, l1g_ref, l1b_ref, l2g_ref, l2b_ref, lr_ref, li_ref, dt_ref,
             sq_ref, br_ref, bi_ref, cr_ref, ci_ref, cw_ref, o_ref):
        o_ref[...] = jnp.zeros_like(o_ref)

        def put_row(name, v):
            r0 = _PACK[name]
            for i in range(v.shape[1] // LANE):
                o_ref[r0 + i:r0 + i + 1, :] = v[:, i * LANE:(i + 1) * LANE]

        ga, gb = sga_ref[0], sgb_ref[0]
        for i in range(1, nI):
            ga, gb = ga + sga_ref[i], gb + sgb_ref[i]
        put_row("b_in", jnp.concatenate([su_ref[...], sh_ref[0:1, :], sh_ref[1:2, :], sh_ref[2:3, :], ga, gb], axis=1))
        put_row("ssm_d", jnp.concatenate([dd_ref[k] for k in range(W // LANE)], axis=1))
        put_row("glu_b", gb_ref[...])
        put_row("ln1_g", l1g_ref[...])
        put_row("ln1_b", l1b_ref[...])
        put_row("ln2_g", l2g_ref[...])
        put_row("ln2_b", l2b_ref[...])
        r0 = _PACK["ssm_lambda"]
        o_ref[r0:r0 + NG, 0:NP] = lr_ref[...]
        o_ref[r0:r0 + NG, NP:2 * NP] = li_ref[...]
        r0 = _PACK["scalars"]
        o_ref[r0:r0 + 1, 0:NG] = dt_ref[...]
        o_ref[r0 + 1:r0 + 2, 0:1] = sq_ref[...]
        for name, ref in (("ssm_b_re", br_ref), ("ssm_b_im", bi_ref), ("ssm_c_re", cr_ref), ("ssm_c_im", ci_ref)):
            r0 = _PACK[name]
            o_ref[r0:r0 + _HALF, 0:NP] = ref[0:_HALF, :]
            o_ref[r0:r0 + _HALF, NP:2 * NP] = ref[_HALF:2 * _HALF, :]
        for cb in range(W // LANE):
            o_ref[_PACK["conv_w"] + 3 * cb:_PACK["conv_w"] + 3 * cb + 3, :] = cw_ref[:, cb * LANE:(cb + 1) * LANE]

    return pl.pallas_call(body, name="pack_grads", out_shape=S((PACK_ROWS, LANE), f32))(
        su, shcb, sga, sgb, dd, dglu_b, dln1_g, dln1_b, dln2_g, dln2_b, dlam_re, dlam_im, dldt, sqerr, dbr, dbi, dc_re, dc_im,
        dconv)


def adam_small(packed_all, params):
    names = list(_SMALL) + ["conv_w"]
    flat = [a for n in names for a in params[n]]

    def body(*refs):
        p_ref = refs[0]
        ins = refs[1:1 + 3 * len(names)]
        outs = refs[1 + 3 * len(names):-2]
        loss_ref, g_ref = refs[-2], refs[-1]
        g_all = p_ref[0]
        for k in range(1, NDEV):
            g_all = g_all + p_ref[k]
        g_ref[...] = g_all

        def rows(name, r0, n, l0=0, lanes=LANE):
            return g_ref[_PACK[name] + r0:_PACK[name] + r0 + n, l0:l0 + lanes]

        def grad_of(name):
            if name in dict(_ROWVEC):
                return jnp.concatenate([rows(name, i, 1) for i in range(dict(_ROWVEC)[name] // LANE)], axis=1)
            if name in ("ssm_lambda_re", "ssm_lambda_im"):
                return rows("ssm_lambda", 0, NG, NP * (name == "ssm_lambda_im"), NP)[None]
            if name == "ssm_log_dt":
                return rows("scalars", 0, 1, 0, NG)
            if name in ("ssm_b_re", "ssm_b_im", "ssm_c_re", "ssm_c_im"):
                return jnp.concatenate([rows(name, 0, _HALF, 0, NP), rows(name, 0, _HALF, NP, NP)], axis=0).reshape(1, NG, GC, NP)
            full = jnp.concatenate([rows("conv_w", 3 * cb, 3) for cb in range(W // LANE)], axis=1)
            x, y, c = _coords()
            col0 = (4 * x + 2 * y + c) * (W // NDEV)
            sel = (lax.broadcasted_iota(jnp.int32, (W, W // NDEV), 0)
                   == lax.broadcasted_iota(jnp.int32, (W, W // NDEV), 1) + col0).astype(f32)
            return jnp.dot(full, sel, precision=HIGHEST, preferred_element_type=f32)[None]

        loss_ref[...] = 0.5 * rows("scalars", 1, 1, 0, 1)
        for i, name in enumerate(names):
            w_ref, m_ref, v_ref = ins[3 * i:3 * i + 3]
            g = grad_of(name)
            d, m, v = _adam(w_ref[...], g, m_ref[...], v_ref[...])
            outs[4 * i][...] = g
            outs[4 * i + 1][...] = d
            outs[4 * i + 2][...] = m
            outs[4 * i + 3][...] = v

    out_shape = [S(params[n][0].shape, f32) for n in names for _ in range(4)] + [S((1, 1), f32)]
    res = pl.pallas_call(body, name="adam_small", out_shape=out_shape, scratch_shapes=[pltpu.VMEM((PACK_ROWS, LANE), f32)],
                         compiler_params=_cp(None, VMEM_LIMIT))(packed_all, *flat)
    return {n: res[4 * i:4 * i + 4] for i, n in enumerate(names)}, res[-1]


def _block_diag(wgt):
    eye = jnp.eye(8, dtype=wgt.dtype)
    out = wgt[:, :, :, None, :] * eye[None, :, None, :, None]
    return out.reshape(4, 8 * wgt.shape[2], 8 * wgt.shape[3])


def _diag_blocks(m, a, b):
    m = m.reshape(4, 8, a, 8, b)
    idx = jnp.arange(8)
    return m[:, idx, :, idx, :].transpose(1, 0, 2, 3)


def kernel(x, w_in, b_in, ssm_lambda_re, ssm_lambda_im, ssm_log_dt, ssm_b_re, ssm_b_im, ssm_c_re, ssm_c_im, ssm_d, glu_w, glu_b, w_ssm_out, conv_w, w_conv_out, w_o, ln1_g, ln1_b, w_gate, w_up, w_down, ln2_g, ln2_b, loss_target, m_w_in, m_b_in, m_ssm_lambda_re, m_ssm_lambda_im, m_ssm_log_dt, m_ssm_b_re, m_ssm_b_im, m_ssm_c_re, m_ssm_c_im, m_ssm_d, m_glu_w, m_glu_b, m_w_ssm_out, m_conv_w, m_w_conv_out, m_w_o, m_ln1_g, m_ln1_b, m_w_gate, m_w_up, m_w_down, m_ln2_g, m_ln2_b, v_w_in, v_b_in, v_ssm_lambda_re, v_ssm_lambda_im, v_ssm_log_dt, v_ssm_b_re, v_ssm_b_im, v_ssm_c_re, v_ssm_c_im, v_ssm_d, v_glu_w, v_glu_b, v_w_ssm_out, v_conv_w, v_w_conv_out, v_w_o, v_ln1_g, v_ln1_b, v_w_gate, v_w_up, v_w_down, v_ln2_g, v_ln2_b):
    given = dict(locals())
    xs = x[0]
    target = loss_target[0]

    tr = lambda a: jnp.swapaxes(a[0], 0, 1)
    win_s, glu_s, wso_s, wco_s, wo_s, wgT_s, wuT_s, wd_s = prep_weights(
        [w_in[0], glu_w[0], w_ssm_out[0], w_conv_out[0], w_o[0], tr(w_gate), tr(w_up), w_down[0]])
    win_g, conv_g = run_plan(GatherPlan([win_s, conv_w[0]]), "gather_w_in")
    conv_f = conv_g.transpose(1, 0, 2).reshape(3, W)

    lam_re, lam_im = ssm_lambda_re[0], ssm_lambda_im[0]
    ldt = ssm_log_dt[0].reshape(NG, 1)
    br2 = jnp.swapaxes(ssm_b_re[0], 1, 2).reshape(NG * GC, NP)
    bi2 = jnp.swapaxes(ssm_b_im[0], 1, 2).reshape(NG * GC, NP)
    lbr, lbi, fr, fi, bbr, bbi = ssm_params(lam_re, lam_im, ldt, br2, bi2)
    bb_t = lambda b: b.reshape(4, 8, GC, NP)
    wb = jnp.concatenate([_block_diag(bb_t(bbr)), _block_diag(bb_t(bbi))], axis=2)
    c_t = lambda c: c.reshape(4, 8, GC, NP).transpose(0, 1, 3, 2)
    wc = jnp.concatenate([_block_diag(c_t(ssm_c_re[0])), -_block_diag(c_t(ssm_c_im[0]))], axis=1)
    wbT, wcT = wb.transpose(0, 2, 1), wc.transpose(0, 2, 1)
    wb, wc, wbT, wcT = wb.astype(bf16), wc.astype(bf16), wbT.astype(bf16), wcT.astype(bf16)
    lbr_s, lbi_s = lbr.reshape(4, 1, SW), lbi.reshape(4, 1, SW)
    dsk = ssm_d[0].reshape(4, 1, LANE)

    (proj, xb), (glu_g, wso_g, wco_g, wo_g, wgT_g) = in_proj(xs, win_g, b_in, GatherPlan([glu_s, wso_s, wco_s, wo_s, wgT_s]))
    glu_f, wo_f = glu_g.reshape(W, W), wo_g.reshape(D, D)
    u_p = to_perm(proj, 0, "perm_u")
    (y_p,), (wuT_g,) = ssm_fwd(u_p, wb, wc, lbr_s, lbi_s, dsk, GatherPlan([wuT_s]))
    wgT, wuT = wgT_g.reshape(F, D), wuT_g.reshape(F, D)
    yn, _ = from_perm(y_p, "unperm_y")
    ya = glu_fwd(yn, glu_f, glu_b)
    yb = conv_fwd(proj, conv_f)
    (merged,), (wd_g,) = merge_fwd(ya, yb, wso_g, wco_g, proj, GatherPlan([wd_s]))
    wd_f = wd_g.reshape(F, D)
    r1, x1b = mix_ln1(merged, wo_f, xs, ln1_g, ln1_b)
    gate, up, hid = gate_up(x1b, wgT, wuT)
    dr2, dffn, sqerr, dln2_g, dln2_b = down_loss(hid, wd_f, r1, ln1_g, ln1_b, ln2_g, ln2_b, target)

    dgate, dup = ffn_bwd_act(dffn, wd_f, gate, up)
    dwd, _ = mm_tn_rows(hid, dffn, "grad_w_down")
    dwgT, (r_wd,) = mm_tn_rows(dgate, x1b, "grad_w_gate", plan=ScatterPlan([dwd.reshape(NDEV, FS, D)]))
    dwuT, (r_wgT,) = mm_tn_rows(dup, x1b, "grad_w_up", plan=ScatterPlan([dwgT.reshape(NDEV, FS, D)]))
    (dr1, dmix, dln1_g, dln1_b), (r_wuT,) = ffn_bwd_x(dgate, dup, wgT, wuT, dr2, r1, ln1_g,
                                                     ScatterPlan([dwuT.reshape(NDEV, FS, D)]))
    (dYA, dYB, dga, dgb, sga, sgb), _ = merge_bwd(dmix, wo_f, ya, yb, wso_g, wco_g, proj, None)
    dwo, _ = mm_tn_rows(merged, dmix, "grad_w_o")
    (dya, dyb), (r_wo,) = branches_bwd_x(dYA, dYB, wso_g, wco_g, ScatterPlan([dwo.reshape(NDEV, D // NDEV, D)]))
    dwso = branch_bwd_w(ya, dYA, "grad_w_ssm_out")
    dwco = branch_bwd_w(yb, dYB, "grad_w_conv_out")
    dyn, dsp, gb, dglu_b = glu_bwd(yn, dya, glu_f, glu_b)
    dglu = mm_tn_rows(gb, dsp, "grad_glu_w")[0].reshape(NDEV, W // NDEV, W)
    dh, dcg, dbg, dconv, shcb = conv_bwd(proj, dyb, conv_f)
    dwin = mm_tn(xb, dgb, "grad_w_in_gb", block0=6, nblocks=NDEV)
    dwin = mm_tn(xb, dga, "grad_w_in_ga", block0=4, into=dwin)
    dwin = mm_tn(xb, dbg, "grad_w_in_bg", block0=3, into=dwin)
    dwin = mm_tn(xb, dcg, "grad_w_in_cg", block0=2, into=dwin)
    dwin = mm_tn(xb, dh, "grad_w_in_h", block0=1, into=dwin)
    dy_p = to_perm(dyn, 0, "perm_dy")
    (du_p, dwb, dwcT, dlbr_s, dlbi_s, dd), (r_wso, r_wco, r_glu, r_win) = ssm_bwd(
        u_p, dy_p, wb, wbT, wcT, lbr_s, lbi_s, dsk,
        Plans([ScatterPlan([dwso, dwco, dglu]), ScatterPlan([dwin], only=tuple(range(1, NDEV)))]))
    du, su = from_perm(du_p, "unperm_du", bf16)
    dwin = mm_tn(xb, du, "grad_w_in_u", block0=0, into=dwin)

    dbb = lambda m: _diag_blocks(m, GC, NP).reshape(NG * GC, NP)
    dbr2, dbi2, dlam_re, dlam_im, dldt = ssm_param_bwd(
        lam_re, lam_im, ldt, fr, fi, br2, bi2, dbb(dwb[:, :, :SW]), dbb(dwb[:, :, SW:]),
        dlbr_s.reshape(NG, NP), dlbi_s.reshape(NG, NP))
    packed = pack_grads(su, shcb, sga, sgb, dd, dglu_b, dln1_g, dln1_b, dln2_g, dln2_b, dlam_re, dlam_im, dldt, sqerr,
                        dbr2, dbi2, dbb(dwcT[:, :, :SW]), -dbb(dwcT[:, :, SW:]), dconv)

    rest = [(dh, 0, 1), (dcg, 0, 2), (dbg, 0, 3), (dga, 0, 4), (dga, 1, 5), (dgb, 0, 6), (dgb, 1, 7)]
    (gx_rest,), (r_win, small_all) = in_proj_bwd_x(rest, win_g, dr1, ALPHA, "in_proj_bwd_x_rest",
                                                  Plans([ScatterPlan([dwin], only=(0,), into=[r_win]), GatherPlan([packed])]))
    (grad_x,), _ = in_proj_bwd_x([(du, 0, 0)], win_g, gx_rest, 1.0, "in_proj_bwd_x_u")

    out = {}

    def put(name, res, back=lambda a: a[None]):
        out["grad_" + name], out["delta_" + name], out["new_m_" + name], out["new_v_" + name] = [back(r) for r in res]

    put("w_in", adam_update(w_in[0], m_w_in[0], v_w_in[0], r_win, "adam_w_in", 256))
    put("glu_w", adam_update(glu_w[0], m_glu_w[0], v_glu_w[0], r_glu, "adam_glu_w"))
    put("w_ssm_out", adam_update(w_ssm_out[0], m_w_ssm_out[0], v_w_ssm_out[0], r_wso, "adam_w_ssm_out"))
    put("w_conv_out", adam_update(w_conv_out[0], m_w_conv_out[0], v_w_conv_out[0], r_wco, "adam_w_conv_out"))
    put("w_o", adam_update(w_o[0], m_w_o[0], v_w_o[0], r_wo, "adam_w_o"))
    put("w_down", adam_update(w_down[0], m_w_down[0], v_w_down[0], r_wd, "adam_w_down", 176))
    untr = lambda a: jnp.swapaxes(a, 0, 1)[None]
    put("w_gate", adam_update(tr(w_gate), tr(m_w_gate), tr(v_w_gate), r_wgT, "adam_w_gate", 176), untr)
    put("w_up", adam_update(tr(w_up), tr(m_w_up), tr(v_w_up), r_wuT, "adam_w_up", 176), untr)
    as_c = lambda a: jnp.swapaxes(a, 2, 3)
    params = {n: (given[n], given["m_" + n], given["v_" + n]) for n in list(_SMALL) + ["conv_w"]}
    for n in ("ssm_b_re", "ssm_b_im"):
        params[n] = tuple(as_c(a) for a in params[n])
    small, loss = adam_small(small_all, params)
    for n, res in small.items():
        put(n, res, as_c if n in ("ssm_b_re", "ssm_b_im") else (lambda a: a))

    names = ["w_in", "b_in", "ssm_lambda_re", "ssm_lambda_im", "ssm_log_dt", "ssm_b_re", "ssm_b_im", "ssm_c_re", "ssm_c_im",
             "ssm_d", "glu_w", "glu_b", "w_ssm_out", "conv_w", "w_conv_out", "w_o", "ln1_g", "ln1_b", "w_gate", "w_up",
             "w_down", "ln2_g", "ln2_b"]
    return (loss.reshape(()), grad_x[None], *[out[p + n] for p in ("grad_", "delta_", "new_m_", "new_v_") for n in names])
```

```python
import functools
import math

import jax
import jax.numpy as jnp
from jax import lax
from jax.experimental import pallas as pl
from jax.experimental.pallas import tpu as pltpu

f32, bf16 = jnp.float32, jnp.bfloat16
S = jax.ShapeDtypeStruct
MESH = pl.DeviceIdType.MESH
HIGHEST = lax.Precision.HIGHEST

D = 1024
W = 512
NG, NP, GC = 32, 64, 16
F = 2816
NDEV = 8
FS = F // NDEV
IN_COLS = 8 * W
ALPHA = 2.0 ** 0.25
LN_EPS = 1e-5
ADAM_LR, ADAM_B1, ADAM_B2, ADAM_EPS, ADAM_WD, ADAM_STEP = 0.001, 0.9, 0.999, 1e-08, 0.01, 10
NC = 32
LANE = 128
SW = 4 * LANE
VMEM_LIMIT = 56 * 1024 * 1024
GRAD_DT = bf16
ANY = pl.BlockSpec(memory_space=pl.ANY)


def _cp(sem=None, vmem=None):
    return pltpu.CompilerParams(dimension_semantics=sem, vmem_limit_bytes=vmem)


def _resident(shape):
    return pl.BlockSpec(shape, lambda i: (0,) * len(shape), pipeline_mode=pl.Buffered(1))


def _dot(a, b):
    return jnp.dot(a, b, preferred_element_type=f32)


def _dot_nt(a, b):
    return lax.dot_general(a, b, (((1,), (1,)), ((), ())), preferred_element_type=f32)


def _dot_tn(a, b):
    return lax.dot_general(a, b, (((0,), (0,)), ((), ())), preferred_element_type=f32)


def _eye(n):
    return (lax.broadcasted_iota(jnp.int32, (n, n), 0) == lax.broadcasted_iota(jnp.int32, (n, n), 1)).astype(f32)


def _transpose_exact(a):
    return lax.dot_general(a, _eye(a.shape[0]), (((0,), (0,)), ((), ())), precision=HIGHEST, preferred_element_type=f32)


def _sigmoid(x):
    return 1.0 / (1.0 + jnp.exp(-x))


_GK = math.sqrt(2.0 / math.pi)


def _gelu(x):
    return 0.5 * x * (1.0 + jnp.tanh(_GK * (x + 0.044715 * x * x * x)))


def _gelu_grad(x):
    th = jnp.tanh(_GK * (x + 0.044715 * x * x * x))
    return 0.5 * (1.0 + th) + 0.5 * x * (1.0 - th * th) * _GK * (1.0 + 3.0 * 0.044715 * x * x)


def _ln_stats(r):
    mu = jnp.mean(r, axis=-1, keepdims=True)
    xc = r - mu
    var = jnp.mean(xc * xc, axis=-1, keepdims=True)
    rstd = lax.rsqrt(var + LN_EPS)
    return xc * rstd, rstd


def _ln_bwd(dy, xhat, rstd, g):
    dxh = dy * g
    m1 = jnp.mean(dxh, axis=-1, keepdims=True)
    m2 = jnp.mean(dxh * xhat, axis=-1, keepdims=True)
    return rstd * (dxh - m1 - xhat * m2)


def _coords():
    return lax.axis_index("x"), lax.axis_index("y"), lax.axis_index("c")


class GatherPlan:
    aliases = ()

    def __init__(self, arrs):
        self.inputs = list(arrs)
        n = len(arrs)
        self.out_shape = [S((NDEV,) + a.shape, a.dtype) for a in arrs]
        self.sems = [pltpu.SemaphoreType.DMA((n, 7)), pltpu.SemaphoreType.DMA((n, 7)), pltpu.SemaphoreType.DMA((n,))]

    def _parts(self, ins, outs, sems):
        n = len(ins)
        send_sems, recv_sems, loc_sems = sems
        x, y, c = _coords()
        me, sib = (x, y, c), (x, y, 1 - c)
        chips = [(1 - x, y), (x, 1 - y), (1 - x, 1 - y)]

        def slot(a, dev):
            return outs[a].at[4 * dev[0] + 2 * dev[1] + dev[2]]

        def copy(a, k, block, to, src=None):
            return pltpu.make_async_remote_copy(
                src_ref=slot(a, block) if src is None else src, dst_ref=slot(a, block),
                send_sem=send_sems.at[a, k], recv_sem=recv_sems.at[a, k], device_id=to, device_id_type=MESH)

        each = [(j, chip, a) for j, chip in enumerate(chips) for a in range(n)]
        return dict(
            mine=lambda: [pltpu.make_async_copy(ins[a], slot(a, me), loc_sems.at[a]) for a in range(n)],
            first=lambda: ([copy(a, 0, me, sib, src=ins[a]) for a in range(n)]
                           + [copy(a, 1 + j, me, (*chip, c), src=ins[a]) for j, chip, a in each]),
            landed=lambda: [copy(a, 1 + j, (*chip, c), me) for j, chip, a in each],
            passed=lambda: [copy(a, 4 + j, (*chip, c), sib) for j, chip, a in each],
            from_sib=lambda: ([copy(a, 0, sib, me) for a in range(n)]
                              + [copy(a, 4 + j, (*chip, 1 - c), me) for j, chip, a in each]))

    def start(self, ins, outs, sems):
        p = self._parts(ins, outs, sems)
        for cp in p["mine"]() + p["first"]():
            cp.start()

    def forward(self, ins, outs, sems):
        p = self._parts(ins, outs, sems)
        for got, fwd in zip(p["landed"](), p["passed"]()):
            got.wait_recv()
            fwd.start()

    def finish(self, ins, outs, sems):
        p = self._parts(ins, outs, sems)
        for cp in p["from_sib"]():
            cp.wait_recv()
        for cp in p["first"]() + p["passed"]():
            cp.wait_send()
        for cp in p["mine"]():
            cp.wait()


class ScatterPlan:
    aliases = ()

    def __init__(self, gs, only=None, into=None):
        n = self.n = len(gs)
        self.only = only
        self.inputs = list(gs) + list(into or [])
        if into:
            self.aliases = tuple((n + a, a) for a in range(n))
        self.out_shape = [S(g.shape, g.dtype) for g in gs]
        self.sems = [pltpu.SemaphoreType.DMA((n, 7)), pltpu.SemaphoreType.DMA((n, 7)), pltpu.SemaphoreType.DMA((n,))]

    def _owner(self, idx):
        if self.only is None:
            return True
        return functools.reduce(jnp.logical_or, [idx == b for b in self.only])

    def _copies(self, ins, outs, sems):
        n = self.n
        send_sems, recv_sems, loc_sems = sems
        x, y, c = _coords()
        me = 4 * x + 2 * y + c
        mine = self._owner(me)
        copies = [(pltpu.make_async_copy(ins[a].at[me], outs[a].at[me], loc_sems.at[a]), mine, None) for a in range(n)]
        for m in range(1, NDEV):
            px = 1 - x if m & 4 else x
            py = 1 - y if m & 2 else y
            pc = 1 - c if m & 1 else c
            peer = 4 * px + 2 * py + pc
            for a in range(n):
                copies.append((pltpu.make_async_remote_copy(
                    src_ref=ins[a].at[peer], dst_ref=outs[a].at[me],
                    send_sem=send_sems.at[a, m - 1], recv_sem=recv_sems.at[a, m - 1],
                    device_id=(px, py, pc), device_id_type=MESH), self._owner(peer), mine))
        return copies

    @staticmethod
    def _when(cond, fn):
        if cond is True:
            fn()
        else:
            pl.when(cond)(fn)

    def start(self, ins, outs, sems):
        for cp, sends, _ in self._copies(ins, outs, sems):
            self._when(sends, cp.start)

    def forward(self, ins, outs, sems):
        pass

    def finish(self, ins, outs, sems):
        for cp, sends, receives in self._copies(ins, outs, sems):
            if receives is None:
                self._when(sends, cp.wait)
            else:
                self._when(sends, cp.wait_send)
                self._when(receives, cp.wait_recv)


class Plans:
    def __init__(self, plans):
        self.plans = plans
        self.inputs = [a for p in plans for a in p.inputs]
        self.out_shape = [s for p in plans for s in p.out_shape]
        self.sems = [s for p in plans for s in p.sems]
        self.aliases, i, o = [], 0, 0
        for p in plans:
            self.aliases += [(i + a, o + b) for a, b in p.aliases]
            i, o = i + len(p.inputs), o + len(p.out_shape)

    def _each(self, what, ins, outs, sems):
        i = o = s = 0
        for p in self.plans:
            ni, no, ns = len(p.inputs), len(p.out_shape), len(p.sems)
            getattr(p, what)(ins[i:i + ni], outs[o:o + no], sems[s:s + ns])
            i, o, s = i + ni, o + no, s + ns

    def start(self, ins, outs, sems):
        self._each("start", ins, outs, sems)

    def forward(self, ins, outs, sems):
        self._each("forward", ins, outs, sems)

    def finish(self, ins, outs, sems):
        self._each("finish", ins, outs, sems)


def _call(body, args, *, name, grid, in_specs, out_specs, out_shape, scratch=(), sem=None, vmem=None, plan=None,
          aliases=None):
    aliases = aliases or {}
    if plan is None:
        outs = pl.pallas_call(body, name=name, grid=grid, in_specs=list(in_specs), out_specs=list(out_specs),
                              out_shape=list(out_shape), scratch_shapes=list(scratch), input_output_aliases=aliases,
                              compiler_params=_cp(sem, vmem))(*args)
        return list(outs), []
    ni, no, ns = len(in_specs), len(out_specs), len(scratch)
    pi, po = len(plan.inputs), len(plan.out_shape)
    aliases = {**aliases, **{ni + a: no + b for a, b in plan.aliases}}

    def wrapped(*refs):
        main_in, p_in = refs[:ni], refs[ni:ni + pi]
        main_out, p_out = refs[ni + pi:ni + pi + no], refs[ni + pi + no:ni + pi + no + po]
        main_scr, p_sems = refs[ni + pi + no + po:ni + pi + no + po + ns], refs[ni + pi + no + po + ns:]
        ids = [pl.program_id(d) for d in range(len(grid))]
        first = functools.reduce(jnp.logical_and, [i == 0 for i in ids])
        last = functools.reduce(jnp.logical_and, [i == g - 1 for i, g in zip(ids, grid)])

        @pl.when(first)
        def _():
            plan.start(p_in, p_out, p_sems)

        @pl.when(last)
        def _():
            plan.forward(p_in, p_out, p_sems)

        body(*main_in, *main_out, *main_scr)

        @pl.when(last)
        def _():
            plan.finish(p_in, p_out, p_sems)

    outs = pl.pallas_call(
        wrapped, name=name, grid=grid, in_specs=list(in_specs) + [ANY] * pi, out_specs=list(out_specs) + [ANY] * po,
        out_shape=list(out_shape) + list(plan.out_shape), scratch_shapes=list(scratch) + list(plan.sems),
        input_output_aliases=aliases, compiler_params=_cp(("arbitrary",) * len(grid), vmem),
    )(*args, *plan.inputs)
    return list(outs[:no]), list(outs[no:])


def run_plan(plan, name):
    def body(*refs):
        ins, outs, sems = refs[:len(plan.inputs)], refs[len(plan.inputs):len(plan.inputs) + len(plan.out_shape)], \
            refs[len(plan.inputs) + len(plan.out_shape):]
        plan.start(ins, outs, sems)
        plan.forward(ins, outs, sems)
        plan.finish(ins, outs, sems)

    return pl.pallas_call(body, name=name, in_specs=[ANY] * len(plan.inputs), out_specs=[ANY] * len(plan.out_shape),
                          out_shape=list(plan.out_shape), scratch_shapes=list(plan.sems))(*plan.inputs)


def mm_tn(a, b, name, tn=512, into=None, block0=0, nblocks=None):
    T, K = a.shape
    N = b.shape[1]
    tn = min(tn, N)
    nblocks = nblocks or (N // tn if into is None else into.shape[0])

    def body(a_ref, b_ref, *rest):
        rest[-1][...] = _dot_tn(a_ref[...], b_ref[...]).astype(GRAD_DT)

    args, in_specs, aliases = [a, b], [_resident((T, K)), pl.BlockSpec((T, tn), lambda j: (0, j))], {}
    if into is not None:
        args.append(into)
        in_specs.append(ANY)
        aliases = {2: 0}
    (out,), _ = _call(body, args, name=name, grid=(N // tn,), in_specs=in_specs,
                      out_specs=[pl.BlockSpec((None, K, tn), lambda j: (block0 + j, 0, 0))],
                      out_shape=[S((nblocks, K, tn), GRAD_DT)], sem=("parallel",), vmem=VMEM_LIMIT, aliases=aliases)
    return out


def mm_tn_rows(a, b, name, tk=256, plan=None):
    T, K = a.shape
    N = b.shape[1]
    tk = min(tk, K)

    def body(a_ref, b_ref, o_ref):
        o_ref[...] = _dot_tn(a_ref[...], b_ref[...]).astype(GRAD_DT)

    (out,), sent = _call(body, [a, b], name=name, grid=(K // tk,),
                         in_specs=[pl.BlockSpec((T, tk), lambda i: (0, i)), _resident((T, N))],
                         out_specs=[pl.BlockSpec((tk, N), lambda i: (i, 0))], out_shape=[S((K, N), GRAD_DT)],
                         sem=("parallel",), vmem=VMEM_LIMIT, plan=plan)
    return out, sent


def prep_weights(ws):
    def body(*refs):
        for i in range(len(ws)):
            refs[len(ws) + i][...] = refs[i][...].astype(bf16)

    return pl.pallas_call(body, name="prep_weights", out_shape=[S(w.shape, bf16) for w in ws],
                          compiler_params=_cp(None, VMEM_LIMIT))(*ws)


def in_proj(x, win_g, b_in, plan):
    T = x.shape[0]
    tm = min(512, T)

    def body(x_ref, w_ref, b_ref, o_ref, xb_ref):
        xb = x_ref[...].astype(bf16)
        xb_ref[...] = xb
        for k in range(NDEV):
            cs = slice(k * W, (k + 1) * W)
            o_ref[:, cs] = _dot(xb, w_ref[k]) + b_ref[:, cs]

    return _call(
        body, [x, win_g, b_in], name="in_proj", grid=(T // tm,),
        in_specs=[pl.BlockSpec((tm, D), lambda i: (i, 0)), _resident((NDEV, D, W)), _resident((1, IN_COLS))],
        out_specs=[pl.BlockSpec((tm, IN_COLS), lambda i: (i, 0)), pl.BlockSpec((tm, D), lambda i: (i, 0))],
        out_shape=[S((T, IN_COLS), f32), S((T, D), bf16)], vmem=VMEM_LIMIT, plan=plan)


def to_perm(a, cb0, name):
    T = a.shape[0]
    L = T // NC

    def body(a_ref, o_ref):
        def step(j, carry):
            for q in range(NC // 8):
                o_ref[pl.ds(pl.multiple_of(j * NC, NC) + 8 * q, 8), :] = a_ref[pl.ds(q * 8 * L + j, 8, stride=L), :]
            return carry

        lax.fori_loop(0, L, step, 0)

    return pl.pallas_call(
        body, name=name, grid=(W // LANE,),
        in_specs=[pl.BlockSpec((T, LANE), lambda k: (0, cb0 + k))], out_specs=pl.BlockSpec((T, LANE), lambda k: (0, k)),
        out_shape=S((T, W), f32), compiler_params=_cp(("parallel",), VMEM_LIMIT),
    )(a)


def from_perm(a, name, out_dtype=f32, plan=None):
    T = a.shape[0]
    L = T // NC

    def body(a_ref, o_ref):
        def step(i, carry):
            c, jb = i // (L // 16), i % (L // 16)
            t0 = a_ref[pl.ds(jb * 16 * NC + c, 8, stride=NC), :]
            t1 = a_ref[pl.ds((jb * 16 + 8) * NC + c, 8, stride=NC), :]
            o_ref[pl.ds(pl.multiple_of(i * 16, 16), 16), :] = jnp.concatenate([t0, t1], axis=0).astype(out_dtype)
            return carry

        lax.fori_loop(0, T // 16, step, 0)

    slab = pl.BlockSpec((T, LANE), lambda k: (0, k))
    return _call(body, [a], name=name, grid=(W // LANE,), in_specs=[slab], out_specs=[slab],
                 out_shape=[S((T, W), out_dtype)], sem=("parallel",), vmem=VMEM_LIMIT, plan=plan)


def _disc(lr, li, ldt):
    dt = jnp.exp(ldt)
    mag = jnp.exp(lr * dt)
    lbr = mag * jnp.cos(li * dt)
    lbi = mag * jnp.sin(li * dt)
    den = lr * lr + li * li
    nr = lbr - 1.0
    return lbr, lbi, (nr * lr + lbi * li) / den, (lbi * lr - nr * li) / den


def _per_channel(f):
    return jnp.broadcast_to(f[:, None, :], (NG, GC, NP)).reshape(NG * GC, NP)


def ssm_params(lam_re, lam_im, log_dt, br, bi):
    def body(lr_ref, li_ref, ldt_ref, br_ref, bi_ref, lbr_ref, lbi_ref, fr_ref, fi_ref, bbr_ref, bbi_ref):
        lbr, lbi, fr, fi = _disc(lr_ref[...], li_ref[...], ldt_ref[...])
        lbr_ref[...], lbi_ref[...], fr_ref[...], fi_ref[...] = lbr, lbi, fr, fi
        fr_, fi_, br_, bi_ = _per_channel(fr), _per_channel(fi), br_ref[...], bi_ref[...]
        bbr_ref[...] = fr_ * br_ - fi_ * bi_
        bbi_ref[...] = fr_ * bi_ + fi_ * br_

    return pl.pallas_call(body, name="ssm_params", out_shape=[S((NG, NP), f32)] * 4 + [S((NG * GC, NP), f32)] * 2)(
        lam_re, lam_im, log_dt, br, bi)


SCAN_UNROLL = 4


def _steps(n, body, carry):
    main = n // SCAN_UNROLL

    def trip(t, c):
        for q in range(SCAN_UNROLL):
            c = body(t * SCAN_UNROLL + q, c)
        return c

    carry = lax.fori_loop(0, main, trip, carry)
    for i in range(main * SCAN_UNROLL, n):
        carry = body(i, carry)
    return carry


def _scan_body(T):
    L = T // NC
    RB = min(512, T)
    nsq = int(round(math.log2(L)))
    assert 2 ** nsq == L and T % RB == 0 and L % 16 == 0

    def rows(i):
        return pl.ds(pl.multiple_of(i * RB, RB), RB)

    def tile(j):
        return pl.ds(j * NC if isinstance(j, int) else pl.multiple_of(j * NC, NC), NC)

    def forward_states(u_ref, wb_ref, lbr_ref, lbi_ref, sre, sim, ere, eim):
        def bproj(i, carry):
            bu = _dot(u_ref[rows(i), :].astype(bf16), wb_ref[...])
            sre[rows(i), :] = bu[:, :SW]
            sim[rows(i), :] = bu[:, SW:]
            return carry

        lax.fori_loop(0, T // RB, bproj, 0)
        for lb in range(SW // LANE):
            ls = slice(lb * LANE, (lb + 1) * LANE)
            ar = jnp.broadcast_to(lbr_ref[:, ls], (NC, LANE))
            ai = jnp.broadcast_to(lbi_ref[:, ls], (NC, LANE))

            def step(j, carry):
                xr, xi = carry
                nr = ar * xr - ai * xi + sre[tile(j), ls]
                ni = ar * xi + ai * xr + sim[tile(j), ls]
                sre[tile(j), ls] = nr
                sim[tile(j), ls] = ni
                return nr, ni

            zero = jnp.zeros((NC, LANE), f32)
            _steps(L, step, (zero, zero))
            pr, pi = lbr_ref[:, ls], lbi_ref[:, ls]
            for _ in range(nsq):
                pr, pi = pr * pr - pi * pi, 2.0 * pr * pi
            er = jnp.zeros((1, LANE), f32)
            ei = er
            ere[0:1, ls] = er
            eim[0:1, ls] = ei
            base = (L - 1) * NC
            for c in range(1, NC):
                lr_ = sre[base + c - 1:base + c, ls]
                li_ = sim[base + c - 1:base + c, ls]
                er, ei = lr_ + pr * er - pi * ei, li_ + pr * ei + pi * er
                ere[c:c + 1, ls] = er
                eim[c:c + 1, ls] = ei
            e_r, e_i = ere[:, ls].reshape(NC // 8, 8, LANE), eim[:, ls].reshape(NC // 8, 8, LANE)
            ar8, ai8 = ar[0:8], ai[0:8]

            def fix(j, carry):
                pwr, pwi = carry
                xr = sre[tile(j), ls].reshape(NC // 8, 8, LANE) + (pwr * e_r - pwi * e_i)
                xi = sim[tile(j), ls].reshape(NC // 8, 8, LANE) + (pwr * e_i + pwi * e_r)
                sre[tile(j), ls] = xr.reshape(NC, LANE)
                sim[tile(j), ls] = xi.reshape(NC, LANE)
                return pwr * ar8 - pwi * ai8, pwr * ai8 + pwi * ar8

            _steps(L, fix, (ar8, ai8))

    return L, RB, nsq, rows, tile, forward_states


def ssm_fwd(u_p, wb, wc, lbr, lbi, dsk, plan):
    T = u_p.shape[0]
    L, RB, nsq, rows, tile, forward_states = _scan_body(T)

    def body(u_ref, wb_ref, wc_ref, lbr_ref, lbi_ref, d_ref, y_ref, sre, sim, ere, eim):
        forward_states(u_ref, wb_ref, lbr_ref, lbi_ref, sre, sim, ere, eim)

        def cproj(i, carry):
            y = _dot(sre[rows(i), :].astype(bf16), wc_ref[0:SW, :]) + _dot(sim[rows(i), :].astype(bf16), wc_ref[SW:, :])
            y_ref[rows(i), :] = y + d_ref[...] * u_ref[rows(i), :]
            return carry

        lax.fori_loop(0, T // RB, cproj, 0)

    slab = pl.BlockSpec((T, LANE), lambda k: (0, k))
    return _call(
        body, [u_p, wb, wc, lbr, lbi, dsk], name="ssm_fwd", grid=(W // LANE,),
        in_specs=[slab, pl.BlockSpec((None, LANE, 2 * SW), lambda k: (k, 0, 0)),
                  pl.BlockSpec((None, 2 * SW, LANE), lambda k: (k, 0, 0)),
                  pl.BlockSpec((None, 1, SW), lambda k: (k, 0, 0)), pl.BlockSpec((None, 1, SW), lambda k: (k, 0, 0)),
                  pl.BlockSpec((None, 1, LANE), lambda k: (k, 0, 0))],
        out_specs=[slab], out_shape=[S((T, W), f32)],
        scratch=[pltpu.VMEM((T, SW), f32), pltpu.VMEM((T, SW), f32), pltpu.VMEM((NC, SW), f32), pltpu.VMEM((NC, SW), f32)],
        vmem=VMEM_LIMIT, plan=plan)


def ssm_bwd(u_p, dy_p, wb, wbT, wcT, lbr, lbi, dsk, plan):
    T = u_p.shape[0]
    L, RB, nsq, rows, tile, forward_states = _scan_body(T)

    def body(u_ref, dy_ref, wb_ref, wbT_ref, wcT_ref, lbr_ref, lbi_ref, d_ref,
             du_ref, dwb_ref, dwc_ref, dlr_ref, dli_ref, dd_ref, su_ref, sre, sim, gre, gim, ere, eim):
        forward_states(u_ref, wb_ref, lbr_ref, lbi_ref, sre, sim, ere, eim)

        def dstate(i, carry):
            g = _dot(dy_ref[rows(i), :].astype(bf16), wcT_ref[...])
            gre[rows(i), :] = g[:, :SW]
            gim[rows(i), :] = g[:, SW:]
            return carry

        lax.fori_loop(0, T // RB, dstate, 0)
        row = lax.broadcasted_iota(jnp.int32, (NC, LANE), 0)
        for lb in range(SW // LANE):
            ls = slice(lb * LANE, (lb + 1) * LANE)
            ar = jnp.broadcast_to(lbr_ref[:, ls], (NC, LANE))
            ai = jnp.broadcast_to(lbi_ref[:, ls], (NC, LANE))

            def step(i, carry):
                gr, gi = carry
                j = L - 1 - i
                nr = ar * gr + ai * gi + gre[tile(j), ls]
                ni = ar * gi - ai * gr + gim[tile(j), ls]
                gre[tile(j), ls] = nr
                gim[tile(j), ls] = ni
                return nr, ni

            zero = jnp.zeros((NC, LANE), f32)
            _steps(L, step, (zero, zero))
            pr, pi = lbr_ref[:, ls], -lbi_ref[:, ls]
            for _ in range(nsq):
                pr, pi = pr * pr - pi * pi, 2.0 * pr * pi
            er = jnp.zeros((1, LANE), f32)
            ei = er
            ere[NC - 1:NC, ls] = er
            eim[NC - 1:NC, ls] = ei
            for c in range(NC - 2, -1, -1):
                lr_ = gre[c + 1:c + 2, ls]
                li_ = gim[c + 1:c + 2, ls]
                er, ei = lr_ + pr * er - pi * ei, li_ + pr * ei + pi * er
                ere[c:c + 1, ls] = er
                eim[c:c + 1, ls] = ei
            e_r, e_i = ere[:, ls].reshape(NC // 8, 8, LANE), eim[:, ls].reshape(NC // 8, 8, LANE)
            ar8, ai8 = ar[0:8], ai[0:8]

            def fixed(j, pwr, pwi):
                gr = (gre[tile(j), ls].reshape(NC // 8, 8, LANE) + (pwr * e_r - pwi * e_i)).reshape(NC, LANE)
                gi = (gim[tile(j), ls].reshape(NC // 8, 8, LANE) + (pwr * e_i + pwi * e_r)).reshape(NC, LANE)
                gre[tile(j), ls] = gr
                gim[tile(j), ls] = gi
                return gr, gi

            def fix(i, carry):
                pwr, pwi, accr, acci = carry
                j = L - 1 - i
                gr, gi = fixed(j, pwr, pwi)
                xr, xi = sre[tile(j - 1), ls], sim[tile(j - 1), ls]
                return (pwr * ar8 + pwi * ai8, pwi * ar8 - pwr * ai8,
                        accr + gr * xr + gi * xi, acci + gi * xr - gr * xi)

            pwr, pwi, accr, acci = _steps(L - 1, fix, (ar8, -ai8, zero, zero))
            gr, gi = fixed(0, pwr, pwi)
            xr = jnp.where(row == 0, 0.0, pltpu.roll(sre[tile(L - 1), ls], 1, axis=0))
            xi = jnp.where(row == 0, 0.0, pltpu.roll(sim[tile(L - 1), ls], 1, axis=0))
            accr = accr + gr * xr + gi * xi
            acci = acci + gi * xr - gr * xi
            dlr_ref[:, ls] = jnp.sum(accr, axis=0, keepdims=True)
            dli_ref[:, ls] = jnp.sum(acci, axis=0, keepdims=True)

        dwb_ref[...] = jnp.zeros_like(dwb_ref)
        dwc_ref[...] = jnp.zeros_like(dwc_ref)
        dd_ref[...] = jnp.zeros_like(dd_ref)
        su_ref[...] = jnp.zeros_like(su_ref)

        def finish(i, carry):
            u32, dy32 = u_ref[rows(i), :], dy_ref[rows(i), :]
            ub, dyb = u32.astype(bf16), dy32.astype(bf16)
            gr, gi = gre[rows(i), :].astype(bf16), gim[rows(i), :].astype(bf16)
            du = _dot(gr, wbT_ref[0:SW, :]) + _dot(gi, wbT_ref[SW:, :]) + dy32 * d_ref[...]
            du_ref[rows(i), :] = du
            su_ref[...] += jnp.sum(du, axis=0, keepdims=True)
            dwb_ref[:, 0:SW] += _dot_tn(ub, gr)
            dwb_ref[:, SW:] += _dot_tn(ub, gi)
            dwc_ref[:, 0:SW] += _dot_tn(dyb, sre[rows(i), :].astype(bf16))
            dwc_ref[:, SW:] += _dot_tn(dyb, sim[rows(i), :].astype(bf16))
            dd_ref[...] += jnp.sum(dy32 * u32, axis=0, keepdims=True)
            return carry

        lax.fori_loop(0, T // RB, finish, 0)

    slab = pl.BlockSpec((T, LANE), lambda k: (0, k))
    wide = pl.BlockSpec((None, LANE, 2 * SW), lambda k: (k, 0, 0))
    tall = pl.BlockSpec((None, 2 * SW, LANE), lambda k: (k, 0, 0))
    vec = pl.BlockSpec((None, 1, SW), lambda k: (k, 0, 0))
    vecd = pl.BlockSpec((None, 1, LANE), lambda k: (k, 0, 0))
    nslab = W // LANE
    return _call(
        body, [u_p, dy_p, wb, wbT, wcT, lbr, lbi, dsk], name="ssm_bwd", grid=(nslab,),
        in_specs=[slab, slab, wide, tall, wide, vec, vec, vecd],
        out_specs=[slab, wide, wide, vec, vec, vecd, vecd],
        out_shape=[S((T, W), f32), S((nslab, LANE, 2 * SW), f32), S((nslab, LANE, 2 * SW), f32),
                   S((nslab, 1, SW), f32), S((nslab, 1, SW), f32), S((nslab, 1, LANE), f32), S((nslab, 1, LANE), f32)],
        scratch=[pltpu.VMEM((T, SW), f32)] * 4 + [pltpu.VMEM((NC, SW), f32)] * 2, vmem=VMEM_LIMIT, plan=plan)


def glu_fwd(yn, glu_w, glu_b):
    T = yn.shape[0]
    tm = min(512, T)

    def body(y_ref, w_ref, b_ref, o_ref):
        g = _gelu(y_ref[...])
        o_ref[...] = (g * _sigmoid(_dot(g.astype(bf16), w_ref[...]) + b_ref[...])).astype(bf16)

    return pl.pallas_call(
        body, name="glu_fwd", grid=(T // tm,),
        in_specs=[pl.BlockSpec((tm, W), lambda i: (i, 0)), pl.BlockSpec((W, W), lambda i: (0, 0)), pl.BlockSpec((1, W), lambda i: (0, 0))],
        out_specs=pl.BlockSpec((tm, W), lambda i: (i, 0)), out_shape=S((T, W), bf16), compiler_params=_cp(("parallel",)),
    )(yn, glu_w, glu_b)


def _shift_rows(cur, prev8, k):
    return pltpu.roll(jnp.concatenate([prev8, cur], axis=0), k, axis=0)[8:]


def _lift_rows(cur, next8, k):
    n = cur.shape[0]
    return pltpu.roll(jnp.concatenate([cur, next8], axis=0), n + 8 - k, axis=0)[:n]


def conv_fwd(proj, conv_w):
    T = proj.shape[0]
    RB = min(512, T)

    def body(h_ref, c_ref, b_ref, w_ref, o_ref):
        w0, w1, w2 = w_ref[0:1, :], w_ref[1:2, :], w_ref[2:3, :]

        def blk(i, carry):
            r0 = pl.multiple_of(i * RB, RB)
            rs = pl.ds(r0, RB)
            ch = c_ref[rs, :] * h_ref[rs, :]
            pr = pl.ds(jnp.maximum(r0 - 8, 0), 8)
            prev = jnp.where(i > 0, c_ref[pr, :] * h_ref[pr, :], 0.0)
            z = w2 * ch + w1 * _shift_rows(ch, prev, 1) + w0 * _shift_rows(ch, prev, 2)
            o_ref[rs, :] = (b_ref[rs, :] * z).astype(bf16)
            return carry

        lax.fori_loop(0, T // RB, blk, 0)

    nb = W // LANE
    return pl.pallas_call(
        body, name="conv_fwd", grid=(nb,),
        in_specs=[pl.BlockSpec((T, LANE), lambda k: (0, nb + k)), pl.BlockSpec((T, LANE), lambda k: (0, 2 * nb + k)),
                  pl.BlockSpec((T, LANE), lambda k: (0, 3 * nb + k)), pl.BlockSpec((3, LANE), lambda k: (0, k))],
        out_specs=pl.BlockSpec((T, LANE), lambda k: (0, k)), out_shape=S((T, W), bf16),
        compiler_params=_cp(("parallel",), VMEM_LIMIT),
    )(proj, proj, proj, conv_w)


def _dense_columns(blocks_ref, dense_ref):
    for k in range(NDEV):
        dense_ref[:, k * LANE:(k + 1) * LANE] = blocks_ref[k]


def merge_fwd(ya, yb, wso, wco, proj, plan):
    T = ya.shape[0]
    tm = min(1024, T)

    def body(ya_ref, yb_ref, wa_ref, wb_ref, ga_ref, gb_ref, o_ref, wa_s, wb_s):
        @pl.when(pl.program_id(0) == 0)
        def _():
            _dense_columns(wa_ref, wa_s)
            _dense_columns(wb_ref, wb_s)

        o_ref[...] = (_sigmoid(ga_ref[...]) * _dot(ya_ref[...], wa_s[...])
                      + _sigmoid(gb_ref[...]) * _dot(yb_ref[...], wb_s[...])).astype(bf16)

    act = pl.BlockSpec((tm, W), lambda i: (i, 0))
    return _call(
        body, [ya, yb, wso, wco, proj, proj], name="merge_fwd", grid=(T // tm,),
        in_specs=[act, act, _resident((NDEV, W, LANE)), _resident((NDEV, W, LANE)),
                  pl.BlockSpec((tm, D), lambda i: (i, 2)), pl.BlockSpec((tm, D), lambda i: (i, 3))],
        out_specs=[pl.BlockSpec((tm, D), lambda i: (i, 0))], out_shape=[S((T, D), bf16)],
        scratch=[pltpu.VMEM((W, D), bf16), pltpu.VMEM((W, D), bf16)], vmem=VMEM_LIMIT, plan=plan)


def mix_ln1(merged, w_o, x, g1, b1):
    T = x.shape[0]
    tm = min(512, T)

    def body(m_ref, w_ref, x_ref, g_ref, b_ref, r_ref, x1_ref):
        r = ALPHA * x_ref[...] + _dot(m_ref[...], w_ref[...])
        r_ref[...] = r
        xhat, _ = _ln_stats(r)
        x1_ref[...] = (xhat * g_ref[...] + b_ref[...]).astype(bf16)

    row = pl.BlockSpec((tm, D), lambda i: (i, 0))
    vec = pl.BlockSpec((1, D), lambda i: (0, 0))
    return pl.pallas_call(
        body, name="mix_ln1", grid=(T // tm,),
        in_specs=[row, _resident((D, D)), row, vec, vec],
        out_specs=[row, row], out_shape=[S((T, D), f32), S((T, D), bf16)], compiler_params=_cp(("parallel",), VMEM_LIMIT),
    )(merged, w_o, x, g1, b1)


FT = 256


def gate_up(x1b, wgT, wuT, plan):
    T = x1b.shape[0]
    tm = min(512, T)

    def body(x_ref, wg_ref, wu_ref, g_ref, u_ref, h_ref):
        x = x_ref[...]
        for n in range(F // FT):
            cs = slice(n * FT, (n + 1) * FT)
            g = _dot_nt(x, wg_ref[cs, :])
            u = _dot_nt(x, wu_ref[cs, :])
            g_ref[:, cs] = g.astype(bf16)
            u_ref[:, cs] = u.astype(bf16)
            h_ref[:, cs] = (g * _sigmoid(g) * u).astype(bf16)

    osp = pl.BlockSpec((tm, F), lambda i: (i, 0))
    return _call(
        body, [x1b, wgT, wuT], name="gate_up", grid=(T // tm,),
        in_specs=[pl.BlockSpec((tm, D), lambda i: (i, 0)), _resident((F, D)), _resident((F, D))],
        out_specs=[osp, osp, osp], out_shape=[S((T, F), bf16)] * 3, vmem=VMEM_LIMIT, plan=plan)


def down_loss(hid, w_down, r1, g1, b1, g2, b2, target):
    T = hid.shape[0]
    tm = min(512, T)

    def body(h_ref, w_ref, r1_ref, g1_ref, b1_ref, g2_ref, b2_ref, t_ref, dr_ref, drb_ref, loss_ref, dg_ref, db_ref):
        @pl.when(pl.program_id(0) == 0)
        def _():
            loss_ref[...] = jnp.zeros_like(loss_ref)
            dg_ref[...] = jnp.zeros_like(dg_ref)
            db_ref[...] = jnp.zeros_like(db_ref)

        xh1, _ = _ln_stats(r1_ref[...])
        x1 = xh1 * g1_ref[...] + b1_ref[...]
        r2 = ALPHA * x1 + _dot(h_ref[...], w_ref[...])
        xh2, rstd2 = _ln_stats(r2)
        err = xh2 * g2_ref[...] + b2_ref[...] - t_ref[...]
        loss_ref[...] += jnp.sum(jnp.mean(err * err, axis=-1, keepdims=True), axis=0, keepdims=True)
        dy = err * (1.0 / D)
        dg_ref[...] += jnp.sum(dy * xh2, axis=0, keepdims=True)
        db_ref[...] += jnp.sum(dy, axis=0, keepdims=True)
        dr = _ln_bwd(dy, xh2, rstd2, g2_ref[...])
        dr_ref[...] = dr
        drb_ref[...] = dr.astype(bf16)

    row = pl.BlockSpec((tm, D), lambda i: (i, 0))
    vec = pl.BlockSpec((1, D), lambda i: (0, 0))
    return pl.pallas_call(
        body, name="down_loss", grid=(T // tm,),
        in_specs=[pl.BlockSpec((tm, F), lambda i: (i, 0)), _resident((F, D)), row, vec, vec, vec, vec, row],
        out_specs=[row, row, pl.BlockSpec((1, 1), lambda i: (0, 0)), vec, vec],
        out_shape=[S((T, D), f32), S((T, D), bf16), S((1, 1), f32), S((1, D), f32), S((1, D), f32)],
        compiler_params=_cp(("arbitrary",), VMEM_LIMIT),
    )(hid, w_down, r1, g1, b1, g2, b2, target)


def ffn_bwd_act(dffn, w_down, gate, up):
    T = dffn.shape[0]
    tm = min(512, T)

    def body(d_ref, w_ref, g_ref, u_ref, dg_ref, du_ref):
        d = d_ref[...]
        for n in range(F // FT):
            cs = slice(n * FT, (n + 1) * FT)
            dh = _dot_nt(d, w_ref[cs, :])
            g, u = g_ref[:, cs].astype(f32), u_ref[:, cs].astype(f32)
            sg = _sigmoid(g)
            du_ref[:, cs] = (dh * g * sg).astype(bf16)
            dg_ref[:, cs] = (dh * u * sg * (1.0 + g * (1.0 - sg))).astype(bf16)

    osp = pl.BlockSpec((tm, F), lambda i: (i, 0))
    return pl.pallas_call(
        body, name="ffn_bwd_act", grid=(T // tm,),
        in_specs=[pl.BlockSpec((tm, D), lambda i: (i, 0)), _resident((F, D)), osp, osp],
        out_specs=[osp, osp], out_shape=[S((T, F), bf16)] * 2, compiler_params=_cp(("parallel",), VMEM_LIMIT),
    )(dffn, w_down, gate, up)


def ffn_bwd_x(dgate, dup, wgT, wuT, dr2, r1, g1, plan):
    T = dr2.shape[0]
    tm = min(512, T)

    def body(dg_ref, du_ref, wg_ref, wu_ref, dr2_ref, r1_ref, g1_ref, dr_ref, drb_ref, dgam_ref, dbet_ref):
        @pl.when(pl.program_id(0) == 0)
        def _():
            dgam_ref[...] = jnp.zeros_like(dgam_ref)
            dbet_ref[...] = jnp.zeros_like(dbet_ref)

        dx1 = ALPHA * dr2_ref[...] + _dot(dg_ref[...], wg_ref[...]) + _dot(du_ref[...], wu_ref[...])
        xh, rstd = _ln_stats(r1_ref[...])
        dgam_ref[...] += jnp.sum(dx1 * xh, axis=0, keepdims=True)
        dbet_ref[...] += jnp.sum(dx1, axis=0, keepdims=True)
        dr = _ln_bwd(dx1, xh, rstd, g1_ref[...])
        dr_ref[...] = dr
        drb_ref[...] = dr.astype(bf16)

    row = pl.BlockSpec((tm, D), lambda i: (i, 0))
    wide = pl.BlockSpec((tm, F), lambda i: (i, 0))
    wsp = _resident((F, D))
    vec = pl.BlockSpec((1, D), lambda i: (0, 0))
    return _call(
        body, [dgate, dup, wgT, wuT, dr2, r1, g1], name="ffn_bwd_x", grid=(T // tm,),
        in_specs=[wide, wide, wsp, wsp, row, row, vec],
        out_specs=[row, row, vec, vec], out_shape=[S((T, D), f32), S((T, D), bf16), S((1, D), f32), S((1, D), f32)],
        vmem=VMEM_LIMIT, plan=plan)


def merge_bwd(dmix, w_o, ya, yb, wso, wco, proj, plan):
    T = dmix.shape[0]
    tm = min(512, T)

    def body(dm_ref, wo_ref, ya_ref, yb_ref, wa_ref, wb_ref, ga_ref, gb_ref, dya_ref, dyb_ref, dga_ref, dgb_ref, sa_ref, sb_ref,
             wa_s, wb_s):
        @pl.when(pl.program_id(0) == 0)
        def _():
            _dense_columns(wa_ref, wa_s)
            _dense_columns(wb_ref, wb_s)

        dmer = _dot_nt(dm_ref[...], wo_ref[...])
        sa, sb = _sigmoid(ga_ref[...]), _sigmoid(gb_ref[...])
        dya_ref[...] = (dmer * sa).astype(bf16)
        dyb_ref[...] = (dmer * sb).astype(bf16)
        dga = dmer * _dot(ya_ref[...], wa_s[...]) * sa * (1.0 - sa)
        dgb = dmer * _dot(yb_ref[...], wb_s[...]) * sb * (1.0 - sb)
        dga_ref[...] = dga.astype(bf16)
        dgb_ref[...] = dgb.astype(bf16)
        sa_ref[...] = jnp.sum(dga, axis=0, keepdims=True)
        sb_ref[...] = jnp.sum(dgb, axis=0, keepdims=True)

    act = pl.BlockSpec((tm, W), lambda i: (i, 0))
    osp = pl.BlockSpec((tm, D), lambda i: (i, 0))
    ssp = pl.BlockSpec((None, 1, D), lambda i: (i, 0, 0))
    return _call(
        body, [dmix, w_o, ya, yb, wso, wco, proj, proj], name="merge_bwd", grid=(T // tm,),
        in_specs=[osp, _resident((D, D)), act, act, _resident((NDEV, W, LANE)), _resident((NDEV, W, LANE)),
                  pl.BlockSpec((tm, D), lambda i: (i, 2)), pl.BlockSpec((tm, D), lambda i: (i, 3))],
        out_specs=[osp, osp, osp, osp, ssp, ssp],
        out_shape=[S((T, D), bf16)] * 4 + [S((T // tm, 1, D), f32)] * 2,
        scratch=[pltpu.VMEM((W, D), bf16), pltpu.VMEM((W, D), bf16)], vmem=VMEM_LIMIT, plan=plan)


def branches_bwd_x(dYA, dYB, wso, wco, plan):
    T = dYA.shape[0]
    tm = min(1024, T)

    def body(da_ref, db_ref, wa_ref, wb_ref, oa_ref, ob_ref, wa_s, wb_s):
        @pl.when(pl.program_id(0) == 0)
        def _():
            _dense_columns(wa_ref, wa_s)
            _dense_columns(wb_ref, wb_s)

        oa_ref[...] = _dot_nt(da_ref[...], wa_s[...])
        ob_ref[...] = _dot_nt(db_ref[...], wb_s[...])

    row = pl.BlockSpec((tm, D), lambda i: (i, 0))
    osp = pl.BlockSpec((tm, W), lambda i: (i, 0))
    return _call(
        body, [dYA, dYB, wso, wco], name="branches_bwd_x", grid=(T // tm,),
        in_specs=[row, row, _resident((NDEV, W, LANE)), _resident((NDEV, W, LANE))],
        out_specs=[osp, osp], out_shape=[S((T, W), f32)] * 2,
        scratch=[pltpu.VMEM((W, D), bf16), pltpu.VMEM((W, D), bf16)], vmem=VMEM_LIMIT, plan=plan)


def branch_bwd_w(act, dY, name):
    T = act.shape[0]
    tk = W // 2

    def body(a_ref, d_ref, o_ref):
        res = _dot_tn(a_ref[...], d_ref[...])
        for k in range(NDEV):
            o_ref[k] = res[:, k * LANE:(k + 1) * LANE].astype(o_ref.dtype)

    return pl.pallas_call(
        body, name=name, grid=(W // tk,),
        in_specs=[pl.BlockSpec((T, tk), lambda i: (0, i)), _resident((T, D))],
        out_specs=pl.BlockSpec((NDEV, tk, LANE), lambda i: (0, i, 0)), out_shape=S((NDEV, W, LANE), GRAD_DT),
        compiler_params=_cp(("parallel",), VMEM_LIMIT),
    )(act, dY)


def glu_bwd(yn, dya, glu_w, glu_b):
    T = yn.shape[0]
    tm = min(512, T)

    def body(y_ref, d_ref, w_ref, b_ref, dy_ref, dsp_ref, g_ref, db_ref):
        @pl.when(pl.program_id(0) == 0)
        def _():
            db_ref[...] = jnp.zeros_like(db_ref)

        y, dya_ = y_ref[...], d_ref[...]
        g = _gelu(y)
        gb = g.astype(bf16)
        s = _sigmoid(_dot(gb, w_ref[...]) + b_ref[...])
        dsp = dya_ * g * s * (1.0 - s)
        dspb = dsp.astype(bf16)
        dg = dya_ * s + _dot_nt(dspb, w_ref[...])
        dy_ref[...] = dg * _gelu_grad(y)
        dsp_ref[...] = dspb
        g_ref[...] = gb
        db_ref[...] += jnp.sum(dsp, axis=0, keepdims=True)

    row = pl.BlockSpec((tm, W), lambda i: (i, 0))
    vec = pl.BlockSpec((1, W), lambda i: (0, 0))
    return pl.pallas_call(
        body, name="glu_bwd", grid=(T // tm,),
        in_specs=[row, row, pl.BlockSpec((W, W), lambda i: (0, 0)), vec],
        out_specs=[row, row, row, vec], out_shape=[S((T, W), f32), S((T, W), bf16), S((T, W), bf16), S((1, W), f32)],
        compiler_params=_cp(("arbitrary",)),
    )(yn, dya, glu_w, glu_b)


def conv_bwd(proj, dyb, conv_w):
    T = proj.shape[0]
    RB = min(512, T)
    nrb = T // RB

    def body(h_ref, c_ref, b_ref, d_ref, w_ref, dh_ref, dc_ref, db_ref, dw_ref, s_ref):
        w0, w1, w2 = w_ref[0:1, :], w_ref[1:2, :], w_ref[2:3, :]

        def blk(i, carry):
            a0, a1, a2, sh, sc, sb = carry
            r0 = pl.multiple_of(i * RB, RB)
            rs = pl.ds(r0, RB)
            h, cg, bg, dyb_ = h_ref[rs, :], c_ref[rs, :], b_ref[rs, :], d_ref[rs, :]
            ch = cg * h
            pr = pl.ds(jnp.maximum(r0 - 8, 0), 8)
            prev = jnp.where(i > 0, c_ref[pr, :] * h_ref[pr, :], 0.0)
            ch1, ch2 = _shift_rows(ch, prev, 1), _shift_rows(ch, prev, 2)
            dbg = dyb_ * (w2 * ch + w1 * ch1 + w0 * ch2)
            db_ref[rs, :] = dbg.astype(bf16)
            dz = dyb_ * bg
            nx = pl.ds(jnp.minimum(r0 + RB, T - 8), 8)
            nxt = jnp.where(i < nrb - 1, d_ref[nx, :] * b_ref[nx, :], 0.0)
            dch = w2 * dz + w1 * _lift_rows(dz, nxt, 1) + w0 * _lift_rows(dz, nxt, 2)
            dcg, dh = dch * h, dch * cg
            dc_ref[rs, :] = dcg.astype(bf16)
            dh_ref[rs, :] = dh.astype(bf16)
            col = lambda v: jnp.sum(v, axis=0, keepdims=True)
            return (a0 + col(dz * ch2), a1 + col(dz * ch1), a2 + col(dz * ch), sh + col(dh), sc + col(dcg), sb + col(dbg))

        zero = jnp.zeros((1, LANE), f32)
        a0, a1, a2, sh, sc, sb = lax.fori_loop(0, nrb, blk, (zero,) * 6)
        dw_ref[0:1, :] = a0
        dw_ref[1:2, :] = a1
        dw_ref[2:3, :] = a2
        s_ref[0:1, :] = sh
        s_ref[1:2, :] = sc
        s_ref[2:3, :] = sb

    nb = W // LANE
    slab = pl.BlockSpec((T, LANE), lambda k: (0, k))
    three = pl.BlockSpec((3, LANE), lambda k: (0, k))
    return pl.pallas_call(
        body, name="conv_bwd", grid=(nb,),
        in_specs=[pl.BlockSpec((T, LANE), lambda k: (0, nb + k)), pl.BlockSpec((T, LANE), lambda k: (0, 2 * nb + k)),
                  pl.BlockSpec((T, LANE), lambda k: (0, 3 * nb + k)), slab, three],
        out_specs=[slab, slab, slab, three, three],
        out_shape=[S((T, W), bf16)] * 3 + [S((3, W), f32)] * 2, compiler_params=_cp(("parallel",), VMEM_LIMIT),
    )(proj, proj, proj, dyb, conv_w)


def in_proj_bwd_x(parts, win_g, base, scale, name, plan=None):
    T = base.shape[0]
    tm = min(512, T)
    n = len(parts)

    def body(*refs):
        p_refs, w_ref, b_ref, o_ref = refs[:n], refs[n], refs[n + 1], refs[n + 2]
        acc = scale * b_ref[...]
        for p_ref, (_, _, k) in zip(p_refs, parts):
            acc += _dot_nt(p_ref[...], w_ref[k])
        o_ref[...] = acc

    row = pl.BlockSpec((tm, D), lambda i: (i, 0))
    p_specs = [pl.BlockSpec((tm, W), (lambda i, cb=cb: (i, cb))) for _, cb, _ in parts]
    return _call(
        body, [a for a, _, _ in parts] + [win_g, base], name=name, grid=(T // tm,),
        in_specs=p_specs + [_resident((NDEV, D, W)), row],
        out_specs=[row], out_shape=[S((T, D), f32)], vmem=VMEM_LIMIT, plan=plan)


def ssm_param_bwd(lam_re, lam_im, log_dt, fr, fi, br, bi, dbbr, dbbi, dlbr, dlbi):
    def body(lr_ref, li_ref, ldt_ref, fr_ref, fi_ref, br_ref, bi_ref, dr_ref, di_ref, dlbr_ref, dlbi_ref,
             dbr_ref, dbi_ref, dlr_ref, dli_ref, dldt_ref):
        fr_, fi_ = _per_channel(fr_ref[...]), _per_channel(fi_ref[...])
        br_, bi_, dr, di = br_ref[...], bi_ref[...], dr_ref[...], di_ref[...]
        dbr_ref[...] = fr_ * dr + fi_ * di
        dbi_ref[...] = fr_ * di - fi_ * dr
        dfr = jnp.sum((dr * br_ + di * bi_).reshape(NG, GC, NP), axis=1)
        dfi = jnp.sum((di * br_ - dr * bi_).reshape(NG, GC, NP), axis=1)
        _, vjp = jax.vjp(_disc, lr_ref[...], li_ref[...], ldt_ref[...])
        dlr_ref[...], dli_ref[...], dldt = vjp((dlbr_ref[...], dlbi_ref[...], dfr, dfi))
        dldt_ref[...] = _transpose_exact(dldt)

    return pl.pallas_call(
        body, name="ssm_param_bwd",
        out_shape=[S((NG * GC, NP), f32)] * 2 + [S((NG, NP), f32)] * 2 + [S((1, NG), f32)])(
        lam_re, lam_im, log_dt, fr, fi, br, bi, dbbr, dbbi, dlbr, dlbi)


def _adam(w, g, m, v):
    m = ADAM_B1 * m + (1.0 - ADAM_B1) * g
    v = ADAM_B2 * v + (1.0 - ADAM_B2) * (g * g)
    m_hat = m / (1.0 - ADAM_B1 ** ADAM_STEP)
    v_hat = v / (1.0 - ADAM_B2 ** ADAM_STEP)
    return -ADAM_LR * (m_hat / (jnp.sqrt(v_hat) + ADAM_EPS) + ADAM_WD * w), m, v


def adam_update(w, m, v, contrib, name, rows_per_block=None):
    R, C = w.shape
    n = contrib.shape[0]
    tr = min(rows_per_block or R, R)

    def body(w_ref, m_ref, v_ref, c_ref, g_ref, d_ref, nm_ref, nv_ref):
        g = c_ref[0].astype(f32)
        for k in range(1, n):
            g = g + c_ref[k].astype(f32)
        g_ref[...] = g
        d_ref[...], nm_ref[...], nv_ref[...] = _adam(w_ref[...], g, m_ref[...], v_ref[...])

    blk = pl.BlockSpec((tr, C), lambda i: (i, 0))
    return pl.pallas_call(
        body, name=name, grid=(R // tr,), in_specs=[blk, blk, blk, pl.BlockSpec((n, tr, C), lambda i: (0, i, 0))],
        out_specs=[blk] * 4, out_shape=[S((R, C), f32)] * 4, compiler_params=_cp(("parallel",), VMEM_LIMIT),
    )(w, m, v, contrib)


_ROWVEC = (("b_in", IN_COLS), ("ssm_d", W), ("glu_b", W), ("ln1_g", D), ("ln1_b", D), ("ln2_g", D), ("ln2_b", D))
_HALF = NG * GC // 2
_PACK = {}
_r = 0
for _n, _k in _ROWVEC:
    _PACK[_n] = _r
    _r += _k // LANE
for _n, _rows in (("ssm_lambda", NG), ("scalars", 8), ("ssm_b_re", _HALF), ("ssm_b_im", _HALF), ("ssm_c_re", _HALF),
                  ("ssm_c_im", _HALF), ("conv_w", 16)):
    _PACK[_n] = _r
    _r += _rows
PACK_ROWS = _r
assert PACK_ROWS % 8 == 0
_SMALL = ("b_in", "ssm_lambda_re", "ssm_lambda_im", "ssm_log_dt", "ssm_b_re", "ssm_b_im", "ssm_c_re", "ssm_c_im",
          "ssm_d", "glu_b", "ln1_g", "ln1_b", "ln2_g", "ln2_b")


def pack_grads(su, shcb, sga, sgb, dd, dglu_b, dln1_g, dln1_b, dln2_g, dln2_b, dlam_re, dlam_im, dldt, sqerr, dbr, dbi,
               dc_re, dc_im, dconv):
    nI = sga.shape[0]

    def body(su_ref, sh_ref, sga_ref, sgb_ref, dd_ref, gb_ref, l1g_ref, l1b_ref, l2g_ref, l2b_ref, lr_ref, li_ref, dt_ref,
             sq_ref, br_ref, bi_ref, cr_ref, ci_ref, cw_ref, o_ref):
        o_ref[...] = jnp.zeros_like(o_ref)

        def put_row(name, v):
            r0 = _PACK[name]
            for i in range(v.shape[1] // LANE):
                o_ref[r0 + i:r0 + i + 1, :] = v[:, i * LANE:(i + 1) * LANE]

        ga, gb = sga_ref[0], sgb_ref[0]
        for i in range(1, nI):
            ga, gb = ga + sga_ref[i], gb + sgb_ref[i]
        put_row("b_in", jnp.concatenate([su_ref[k] for k in range(W // LANE)]
                                        + [sh_ref[0:1, :], sh_ref[1:2, :], sh_ref[2:3, :], ga, gb], axis=1))
        put_row("ssm_d", jnp.concatenate([dd_ref[k] for k in range(W // LANE)], axis=1))
        put_row("glu_b", gb_ref[...])
        put_row("ln1_g", l1g_ref[...])
        put_row("ln1_b", l1b_ref[...])
        put_row("ln2_g", l2g_ref[...])
        put_row("ln2_b", l2b_ref[...])
        r0 = _PACK["ssm_lambda"]
        o_ref[r0:r0 + NG, 0:NP] = lr_ref[...]
        o_ref[r0:r0 + NG, NP:2 * NP] = li_ref[...]
        r0 = _PACK["scalars"]
        o_ref[r0:r0 + 1, 0:NG] = dt_ref[...]
        o_ref[r0 + 1:r0 + 2, 0:1] = sq_ref[...]
        for name, ref in (("ssm_b_re", br_ref), ("ssm_b_im", bi_ref), ("ssm_c_re", cr_ref), ("ssm_c_im", ci_ref)):
            r0 = _PACK[name]
            o_ref[r0:r0 + _HALF, 0:NP] = ref[0:_HALF, :]
            o_ref[r0:r0 + _HALF, NP:2 * NP] = ref[_HALF:2 * _HALF, :]
        for cb in range(W // LANE):
            o_ref[_PACK["conv_w"] + 3 * cb:_PACK["conv_w"] + 3 * cb + 3, :] = cw_ref[:, cb * LANE:(cb + 1) * LANE]

    return pl.pallas_call(body, name="pack_grads", out_shape=S((PACK_ROWS, LANE), f32))(
        su, shcb, sga, sgb, dd, dglu_b, dln1_g, dln1_b, dln2_g, dln2_b, dlam_re, dlam_im, dldt, sqerr, dbr, dbi, dc_re, dc_im,
        dconv)


def adam_small(packed_all, params):
    names = list(_SMALL) + ["conv_w"]
    flat = [a for n in names for a in params[n]]

    def body(*refs):
        p_ref = refs[0]
        ins = refs[1:1 + 3 * len(names)]
        outs = refs[1 + 3 * len(names):-2]
        loss_ref, g_ref = refs[-2], refs[-1]
        g_all = p_ref[0]
        for k in range(1, NDEV):
            g_all = g_all + p_ref[k]
        g_ref[...] = g_all

        def rows(name, r0, n, l0=0, lanes=LANE):
            return g_ref[_PACK[name] + r0:_PACK[name] + r0 + n, l0:l0 + lanes]

        def grad_of(name):
            if name in dict(_ROWVEC):
                return jnp.concatenate([rows(name, i, 1) for i in range(dict(_ROWVEC)[name] // LANE)], axis=1)
            if name in ("ssm_lambda_re", "ssm_lambda_im"):
                return rows("ssm_lambda", 0, NG, NP * (name == "ssm_lambda_im"), NP)[None]
            if name == "ssm_log_dt":
                return rows("scalars", 0, 1, 0, NG)
            if name in ("ssm_b_re", "ssm_b_im", "ssm_c_re", "ssm_c_im"):
                return jnp.concatenate([rows(name, 0, _HALF, 0, NP), rows(name, 0, _HALF, NP, NP)], axis=0).reshape(1, NG, GC, NP)
            full = jnp.concatenate([rows("conv_w", 3 * cb, 3) for cb in range(W // LANE)], axis=1)
            x, y, c = _coords()
            col0 = (4 * x + 2 * y + c) * (W // NDEV)
            sel = (lax.broadcasted_iota(jnp.int32, (W, W // NDEV), 0)
                   == lax.broadcasted_iota(jnp.int32, (W, W // NDEV), 1) + col0).astype(f32)
            return jnp.dot(full, sel, precision=HIGHEST, preferred_element_type=f32)[None]

        loss_ref[...] = 0.5 * rows("scalars", 1, 1, 0, 1)
        for i, name in enumerate(names):
            w_ref, m_ref, v_ref = ins[3 * i:3 * i + 3]
            g = grad_of(name)
            d, m, v = _adam(w_ref[...], g, m_ref[...], v_ref[...])
            outs[4 * i][...] = g
            outs[4 * i + 1][...] = d
            outs[4 * i + 2][...] = m
            outs[4 * i + 3][...] = v

    out_shape = [S(params[n][0].shape, f32) for n in names for _ in range(4)] + [S((1, 1), f32)]
    res = pl.pallas_call(body, name="adam_small", out_shape=out_shape, scratch_shapes=[pltpu.VMEM((PACK_ROWS, LANE), f32)],
                         compiler_params=_cp(None, VMEM_LIMIT))(packed_all, *flat)
    return {n: res[4 * i:4 * i + 4] for i, n in enumerate(names)}, res[-1]


def _block_diag(wgt):
    eye = jnp.eye(8, dtype=wgt.dtype)
    out = wgt[:, :, :, None, :] * eye[None, :, None, :, None]
    return out.reshape(4, 8 * wgt.shape[2], 8 * wgt.shape[3])


def _diag_blocks(m, a, b):
    m = m.reshape(4, 8, a, 8, b)
    idx = jnp.arange(8)
    return m[:, idx, :, idx, :].transpose(1, 0, 2, 3)


def kernel(x, w_in, b_in, ssm_lambda_re, ssm_lambda_im, ssm_log_dt, ssm_b_re, ssm_b_im, ssm_c_re, ssm_c_im, ssm_d, glu_w, glu_b, w_ssm_out, conv_w, w_conv_out, w_o, ln1_g, ln1_b, w_gate, w_up, w_down, ln2_g, ln2_b, loss_target, m_w_in, m_b_in, m_ssm_lambda_re, m_ssm_lambda_im, m_ssm_log_dt, m_ssm_b_re, m_ssm_b_im, m_ssm_c_re, m_ssm_c_im, m_ssm_d, m_glu_w, m_glu_b, m_w_ssm_out, m_conv_w, m_w_conv_out, m_w_o, m_ln1_g, m_ln1_b, m_w_gate, m_w_up, m_w_down, m_ln2_g, m_ln2_b, v_w_in, v_b_in, v_ssm_lambda_re, v_ssm_lambda_im, v_ssm_log_dt, v_ssm_b_re, v_ssm_b_im, v_ssm_c_re, v_ssm_c_im, v_ssm_d, v_glu_w, v_glu_b, v_w_ssm_out, v_conv_w, v_w_conv_out, v_w_o, v_ln1_g, v_ln1_b, v_w_gate, v_w_up, v_w_down, v_ln2_g, v_ln2_b):
    given = dict(locals())
    xs = x[0]
    target = loss_target[0]

    tr = lambda a: jnp.swapaxes(a[0], 0, 1)
    win_s, glu_s, wso_s, wco_s, wo_s, wgT_s, wuT_s, wd_s = prep_weights(
        [w_in[0], glu_w[0], w_ssm_out[0], w_conv_out[0], w_o[0], tr(w_gate), tr(w_up), w_down[0]])
    win_g, conv_g = run_plan(GatherPlan([win_s, conv_w[0]]), "gather_w_in")
    conv_f = conv_g.transpose(1, 0, 2).reshape(3, W)

    lam_re, lam_im = ssm_lambda_re[0], ssm_lambda_im[0]
    ldt = ssm_log_dt[0].reshape(NG, 1)
    br2 = jnp.swapaxes(ssm_b_re[0], 1, 2).reshape(NG * GC, NP)
    bi2 = jnp.swapaxes(ssm_b_im[0], 1, 2).reshape(NG * GC, NP)
    lbr, lbi, fr, fi, bbr, bbi = ssm_params(lam_re, lam_im, ldt, br2, bi2)
    bb_t = lambda b: b.reshape(4, 8, GC, NP)
    wb = jnp.concatenate([_block_diag(bb_t(bbr)), _block_diag(bb_t(bbi))], axis=2)
    c_t = lambda c: c.reshape(4, 8, GC, NP).transpose(0, 1, 3, 2)
    wc = jnp.concatenate([_block_diag(c_t(ssm_c_re[0])), -_block_diag(c_t(ssm_c_im[0]))], axis=1)
    wbT, wcT = wb.transpose(0, 2, 1), wc.transpose(0, 2, 1)
    wb, wc, wbT, wcT = wb.astype(bf16), wc.astype(bf16), wbT.astype(bf16), wcT.astype(bf16)
    lbr_s, lbi_s = lbr.reshape(4, 1, SW), lbi.reshape(4, 1, SW)
    dsk = ssm_d[0].reshape(4, 1, LANE)

    (proj, xb), (glu_g, wso_g, wco_g, wo_g) = in_proj(xs, win_g, b_in, GatherPlan([glu_s, wso_s, wco_s, wo_s]))
    glu_f, wo_f = glu_g.reshape(W, W), wo_g.reshape(D, D)
    u_p = to_perm(proj, 0, "perm_u")
    (y_p,), (wgT_g,) = ssm_fwd(u_p, wb, wc, lbr_s, lbi_s, dsk, GatherPlan([wgT_s]))
    (yn,), _ = from_perm(y_p, "unperm_y")
    ya = glu_fwd(yn, glu_f, glu_b)
    yb = conv_fwd(proj, conv_f)
    (merged,), (wuT_g,) = merge_fwd(ya, yb, wso_g, wco_g, proj, GatherPlan([wuT_s]))
    wgT, wuT = wgT_g.reshape(F, D), wuT_g.reshape(F, D)
    r1, x1b = mix_ln1(merged, wo_f, xs, ln1_g, ln1_b)
    (gate, up, hid), (wd_g,) = gate_up(x1b, wgT, wuT, GatherPlan([wd_s]))
    wd_f = wd_g.reshape(F, D)
    dr2, dffn, sqerr, dln2_g, dln2_b = down_loss(hid, wd_f, r1, ln1_g, ln1_b, ln2_g, ln2_b, target)

    half_a, half_b = (0, 3, 5, 6), (1, 2, 4, 7)
    dgate, dup = ffn_bwd_act(dffn, wd_f, gate, up)
    dwd, _ = mm_tn_rows(hid, dffn, "grad_w_down")
    dwd = dwd.reshape(NDEV, FS, D)
    dwgT, (r_wd,) = mm_tn_rows(dgate, x1b, "grad_w_gate", plan=ScatterPlan([dwd], only=half_a))
    dwuT, (r_wd,) = mm_tn_rows(dup, x1b, "grad_w_up", plan=ScatterPlan([dwd], only=half_b, into=[r_wd]))
    dwgT, dwuT = dwgT.reshape(NDEV, FS, D), dwuT.reshape(NDEV, FS, D)
    (dr1, dmix, dln1_g, dln1_b), (r_wgT,) = ffn_bwd_x(dgate, dup, wgT, wuT, dr2, r1, ln1_g, ScatterPlan([dwgT]))
    (dYA, dYB, dga, dgb, sga, sgb), (r_wuT,) = merge_bwd(dmix, wo_f, ya, yb, wso_g, wco_g, proj,
                                                         ScatterPlan([dwuT], only=half_a))
    dwo, _ = mm_tn_rows(merged, dmix, "grad_w_o")
    dwo = dwo.reshape(NDEV, D // NDEV, D)
    (dya, dyb), (r_wuT,) = branches_bwd_x(dYA, dYB, wso_g, wco_g, ScatterPlan([dwuT], only=half_b, into=[r_wuT]))
    dwso = branch_bwd_w(ya, dYA, "grad_w_ssm_out")
    dwco = branch_bwd_w(yb, dYB, "grad_w_conv_out")
    dyn, dsp, gb, dglu_b = glu_bwd(yn, dya, glu_f, glu_b)
    dglu = mm_tn_rows(gb, dsp, "grad_glu_w")[0].reshape(NDEV, W // NDEV, W)
    dh, dcg, dbg, dconv, shcb = conv_bwd(proj, dyb, conv_f)
    dwin = mm_tn(xb, dgb, "grad_w_in_gb", block0=6, nblocks=NDEV)
    dwin = mm_tn(xb, dga, "grad_w_in_ga", block0=4, into=dwin)
    dwin = mm_tn(xb, dbg, "grad_w_in_bg", block0=3, into=dwin)
    dwin = mm_tn(xb, dcg, "grad_w_in_cg", block0=2, into=dwin)
    dwin = mm_tn(xb, dh, "grad_w_in_h", block0=1, into=dwin)
    dy_p = to_perm(dyn, 0, "perm_dy")
    (du_p, dwb, dwcT, dlbr_s, dlbi_s, dd, su), (r_wo, r_wso, r_wco, r_glu, r_win) = ssm_bwd(
        u_p, dy_p, wb, wbT, wcT, lbr_s, lbi_s, dsk,
        Plans([ScatterPlan([dwo, dwso, dwco, dglu]), ScatterPlan([dwin], only=tuple(range(1, NDEV)))]))

    dbb = lambda m: _diag_blocks(m, GC, NP).reshape(NG * GC, NP)
    dbr2, dbi2, dlam_re, dlam_im, dldt = ssm_param_bwd(
        lam_re, lam_im, ldt, fr, fi, br2, bi2, dbb(dwb[:, :, :SW]), dbb(dwb[:, :, SW:]),
        dlbr_s.reshape(NG, NP), dlbi_s.reshape(NG, NP))
    packed = pack_grads(su, shcb, sga, sgb, dd, dglu_b, dln1_g, dln1_b, dln2_g, dln2_b, dlam_re, dlam_im, dldt, sqerr,
                        dbr2, dbi2, dbb(dwcT[:, :, :SW]), -dbb(dwcT[:, :, SW:]), dconv)
    (du,), (small_all,) = from_perm(du_p, "unperm_du", bf16, GatherPlan([packed]))
    dwin = mm_tn(xb, du, "grad_w_in_u", block0=0, into=dwin)

    rest = [(dh, 0, 1), (dcg, 0, 2), (dbg, 0, 3), (dga, 0, 4), (dga, 1, 5), (dgb, 0, 6), (dgb, 1, 7)]
    (gx_rest,), (r_win,) = in_proj_bwd_x(rest, win_g, dr1, ALPHA, "in_proj_bwd_x_rest",
                                        ScatterPlan([dwin], only=(0,), into=[r_win]))
    (grad_x,), _ = in_proj_bwd_x([(du, 0, 0)], win_g, gx_rest, 1.0, "in_proj_bwd_x_u")

    out = {}

    def put(name, res, back=lambda a: a[None]):
        out["grad_" + name], out["delta_" + name], out["new_m_" + name], out["new_v_" + name] = [back(r) for r in res]

    put("w_in", adam_update(w_in[0], m_w_in[0], v_w_in[0], r_win, "adam_w_in", 256))
    put("glu_w", adam_update(glu_w[0], m_glu_w[0], v_glu_w[0], r_glu, "adam_glu_w"))
    put("w_ssm_out", adam_update(w_ssm_out[0], m_w_ssm_out[0], v_w_ssm_out[0], r_wso, "adam_w_ssm_out"))
    put("w_conv_out", adam_update(w_conv_out[0], m_w_conv_out[0], v_w_conv_out[0], r_wco, "adam_w_conv_out"))
    put("w_o", adam_update(w_o[0], m_w_o[0], v_w_o[0], r_wo, "adam_w_o"))
    put("w_down", adam_update(w_down[0], m_w_down[0], v_w_down[0], r_wd, "adam_w_down", 176))
    untr = lambda a: jnp.swapaxes(a, 0, 1)[None]
    put("w_gate", adam_update(tr(w_gate), tr(m_w_gate), tr(v_w_gate), r_wgT, "adam_w_gate", 176), untr)
    put("w_up", adam_update(tr(w_up), tr(m_w_up), tr(v_w_up), r_wuT, "adam_w_up", 176), untr)
    as_c = lambda a: jnp.swapaxes(a, 2, 3)
    params = {n: (given[n], given["m_" + n], given["v_" + n]) for n in list(_SMALL) + ["conv_w"]}
    for n in ("ssm_b_re", "ssm_b_im"):
        params[n] = tuple(as_c(a) for a in params[n])
    small, loss = adam_small(small_all, params)
    for n, res in small.items():
        put(n, res, as_c if n in ("ssm_b_re", "ssm_b_im") else (lambda a: a))

    names = ["w_in", "b_in", "ssm_lambda_re", "ssm_lambda_im", "ssm_log_dt", "ssm_b_re", "ssm_b_im", "ssm_c_re", "ssm_c_im",
             "ssm_d", "glu_w", "glu_b", "w_ssm_out", "conv_w", "w_conv_out", "w_o", "ln1_g", "ln1_b", "w_gate", "w_up",
             "w_down", "ln2_g", "ln2_b"]
    return (loss.reshape(()), grad_x[None], *[out[p + n] for p in ("grad_", "delta_", "new_m_", "new_v_") for n in names])
```

```python
import functools
import math

import jax
import jax.numpy as jnp
from jax import lax
from jax.experimental import pallas as pl
from jax.experimental.pallas import tpu as pltpu

f32, bf16 = jnp.float32, jnp.bfloat16
S = jax.ShapeDtypeStruct
MESH = pl.DeviceIdType.MESH
HIGHEST = lax.Precision.HIGHEST

D = 1024
W = 512
NG, NP, GC = 32, 64, 16
F = 2816
NDEV = 8
FS = F // NDEV
IN_COLS = 8 * W
ALPHA = 2.0 ** 0.25
LN_EPS = 1e-5
ADAM_LR, ADAM_B1, ADAM_B2, ADAM_EPS, ADAM_WD, ADAM_STEP = 0.001, 0.9, 0.999, 1e-08, 0.01, 10
NC = 32
LANE = 128
SW = 4 * LANE
VMEM_LIMIT = 56 * 1024 * 1024
GRAD_DT = bf16
ANY = pl.BlockSpec(memory_space=pl.ANY)


def _cp(sem=None, vmem=None):
    return pltpu.CompilerParams(dimension_semantics=sem, vmem_limit_bytes=vmem)


def _resident(shape):
    return pl.BlockSpec(shape, lambda i: (0,) * len(shape), pipeline_mode=pl.Buffered(1))


def _dot(a, b):
    return jnp.dot(a, b, preferred_element_type=f32)


def _dot_nt(a, b):
    return lax.dot_general(a, b, (((1,), (1,)), ((), ())), preferred_element_type=f32)


def _dot_tn(a, b):
    return lax.dot_general(a, b, (((0,), (0,)), ((), ())), preferred_element_type=f32)


def _eye(n):
    return (lax.broadcasted_iota(jnp.int32, (n, n), 0) == lax.broadcasted_iota(jnp.int32, (n, n), 1)).astype(f32)


def _transpose_exact(a):
    return lax.dot_general(a, _eye(a.shape[0]), (((0,), (0,)), ((), ())), precision=HIGHEST, preferred_element_type=f32)


def _sigmoid(x):
    return 1.0 / (1.0 + jnp.exp(-x))


_GK = math.sqrt(2.0 / math.pi)


def _gelu(x):
    return 0.5 * x * (1.0 + jnp.tanh(_GK * (x + 0.044715 * x * x * x)))


def _gelu_grad(x):
    th = jnp.tanh(_GK * (x + 0.044715 * x * x * x))
    return 0.5 * (1.0 + th) + 0.5 * x * (1.0 - th * th) * _GK * (1.0 + 3.0 * 0.044715 * x * x)


ROW_PART = 256


def _row_parts(tm):
    return [slice(r, r + min(ROW_PART, tm)) for r in range(0, tm, min(ROW_PART, tm))]


def _ln_stats(r):
    mu = jnp.mean(r, axis=-1, keepdims=True)
    xc = r - mu
    var = jnp.mean(xc * xc, axis=-1, keepdims=True)
    rstd = lax.rsqrt(var + LN_EPS)
    return xc * rstd, rstd


def _ln_bwd(dy, xhat, rstd, g):
    dxh = dy * g
    m1 = jnp.mean(dxh, axis=-1, keepdims=True)
    m2 = jnp.mean(dxh * xhat, axis=-1, keepdims=True)
    return rstd * (dxh - m1 - xhat * m2)


def _coords():
    return lax.axis_index("x"), lax.axis_index("y"), lax.axis_index("c")


class GatherPlan:
    aliases = ()

    def __init__(self, arrs):
        self.inputs = list(arrs)
        n = len(arrs)
        self.out_shape = [S((NDEV,) + a.shape, a.dtype) for a in arrs]
        self.sems = [pltpu.SemaphoreType.DMA((n, 7)), pltpu.SemaphoreType.DMA((n, 7)), pltpu.SemaphoreType.DMA((n,))]

    def _parts(self, ins, outs, sems):
        n = len(ins)
        send_sems, recv_sems, loc_sems = sems
        x, y, c = _coords()
        me, sib = (x, y, c), (x, y, 1 - c)
        chips = [(1 - x, y), (x, 1 - y), (1 - x, 1 - y)]

        def slot(a, dev):
            return outs[a].at[4 * dev[0] + 2 * dev[1] + dev[2]]

        def copy(a, k, block, to, src=None):
            return pltpu.make_async_remote_copy(
                src_ref=slot(a, block) if src is None else src, dst_ref=slot(a, block),
                send_sem=send_sems.at[a, k], recv_sem=recv_sems.at[a, k], device_id=to, device_id_type=MESH)

        each = [(j, chip, a) for j, chip in enumerate(chips) for a in range(n)]
        return dict(
            mine=lambda: [pltpu.make_async_copy(ins[a], slot(a, me), loc_sems.at[a]) for a in range(n)],
            first=lambda: ([copy(a, 0, me, sib, src=ins[a]) for a in range(n)]
                           + [copy(a, 1 + j, me, (*chip, c), src=ins[a]) for j, chip, a in each]),
            landed=lambda: [copy(a, 1 + j, (*chip, c), me) for j, chip, a in each],
            passed=lambda: [copy(a, 4 + j, (*chip, c), sib) for j, chip, a in each],
            from_sib=lambda: ([copy(a, 0, sib, me) for a in range(n)]
                              + [copy(a, 4 + j, (*chip, 1 - c), me) for j, chip, a in each]))

    def start(self, ins, outs, sems):
        p = self._parts(ins, outs, sems)
        for cp in p["mine"]() + p["first"]():
            cp.start()

    def forward(self, ins, outs, sems):
        p = self._parts(ins, outs, sems)
        for got, fwd in zip(p["landed"](), p["passed"]()):
            got.wait_recv()
            fwd.start()

    def finish(self, ins, outs, sems):
        p = self._parts(ins, outs, sems)
        for cp in p["from_sib"]():
            cp.wait_recv()
        for cp in p["first"]() + p["passed"]():
            cp.wait_send()
        for cp in p["mine"]():
            cp.wait()


class ScatterPlan:
    aliases = ()

    def __init__(self, gs, only=None, into=None):
        n = self.n = len(gs)
        self.only = only
        self.inputs = list(gs) + list(into or [])
        if into:
            self.aliases = tuple((n + a, a) for a in range(n))
        self.out_shape = [S(g.shape, g.dtype) for g in gs]
        self.sems = [pltpu.SemaphoreType.DMA((n, 7)), pltpu.SemaphoreType.DMA((n, 7)), pltpu.SemaphoreType.DMA((n,))]

    def _owner(self, idx):
        if self.only is None:
            return True
        return functools.reduce(jnp.logical_or, [idx == b for b in self.only])

    def _copies(self, ins, outs, sems):
        n = self.n
        send_sems, recv_sems, loc_sems = sems
        x, y, c = _coords()
        me = 4 * x + 2 * y + c
        mine = self._owner(me)
        copies = [(pltpu.make_async_copy(ins[a].at[me], outs[a].at[me], loc_sems.at[a]), mine, None) for a in range(n)]
        for m in range(1, NDEV):
            px = 1 - x if m & 4 else x
            py = 1 - y if m & 2 else y
            pc = 1 - c if m & 1 else c
            peer = 4 * px + 2 * py + pc
            for a in range(n):
                copies.append((pltpu.make_async_remote_copy(
                    src_ref=ins[a].at[peer], dst_ref=outs[a].at[me],
                    send_sem=send_sems.at[a, m - 1], recv_sem=recv_sems.at[a, m - 1],
                    device_id=(px, py, pc), device_id_type=MESH), self._owner(peer), mine))
        return copies

    @staticmethod
    def _when(cond, fn):
        if cond is True:
            fn()
        else:
            pl.when(cond)(fn)

    def start(self, ins, outs, sems):
        for cp, sends, _ in self._copies(ins, outs, sems):
            self._when(sends, cp.start)

    def forward(self, ins, outs, sems):
        pass

    def finish(self, ins, outs, sems):
        for cp, sends, receives in self._copies(ins, outs, sems):
            if receives is None:
                self._when(sends, cp.wait)
            else:
                self._when(sends, cp.wait_send)
                self._when(receives, cp.wait_recv)


class Plans:
    def __init__(self, plans):
        self.plans = plans
        self.inputs = [a for p in plans for a in p.inputs]
        self.out_shape = [s for p in plans for s in p.out_shape]
        self.sems = [s for p in plans for s in p.sems]
        self.aliases, i, o = [], 0, 0
        for p in plans:
            self.aliases += [(i + a, o + b) for a, b in p.aliases]
            i, o = i + len(p.inputs), o + len(p.out_shape)

    def _each(self, what, ins, outs, sems):
        i = o = s = 0
        for p in self.plans:
            ni, no, ns = len(p.inputs), len(p.out_shape), len(p.sems)
            getattr(p, what)(ins[i:i + ni], outs[o:o + no], sems[s:s + ns])
            i, o, s = i + ni, o + no, s + ns

    def start(self, ins, outs, sems):
        self._each("start", ins, outs, sems)

    def forward(self, ins, outs, sems):
        self._each("forward", ins, outs, sems)

    def finish(self, ins, outs, sems):
        self._each("finish", ins, outs, sems)


def _call(body, args, *, name, grid, in_specs, out_specs, out_shape, scratch=(), sem=None, vmem=None, plan=None,
          aliases=None):
    aliases = aliases or {}
    if plan is None:
        outs = pl.pallas_call(body, name=name, grid=grid, in_specs=list(in_specs), out_specs=list(out_specs),
                              out_shape=list(out_shape), scratch_shapes=list(scratch), input_output_aliases=aliases,
                              compiler_params=_cp(sem, vmem))(*args)
        return list(outs), []
    ni, no, ns = len(in_specs), len(out_specs), len(scratch)
    pi, po = len(plan.inputs), len(plan.out_shape)
    aliases = {**aliases, **{ni + a: no + b for a, b in plan.aliases}}

    def wrapped(*refs):
        main_in, p_in = refs[:ni], refs[ni:ni + pi]
        main_out, p_out = refs[ni + pi:ni + pi + no], refs[ni + pi + no:ni + pi + no + po]
        main_scr, p_sems = refs[ni + pi + no + po:ni + pi + no + po + ns], refs[ni + pi + no + po + ns:]
        ids = [pl.program_id(d) for d in range(len(grid))]
        first = functools.reduce(jnp.logical_and, [i == 0 for i in ids])
        last = functools.reduce(jnp.logical_and, [i == g - 1 for i, g in zip(ids, grid)])

        @pl.when(first)
        def _():
            plan.start(p_in, p_out, p_sems)

        @pl.when(last)
        def _():
            plan.forward(p_in, p_out, p_sems)

        body(*main_in, *main_out, *main_scr)

        @pl.when(last)
        def _():
            plan.finish(p_in, p_out, p_sems)

    outs = pl.pallas_call(
        wrapped, name=name, grid=grid, in_specs=list(in_specs) + [ANY] * pi, out_specs=list(out_specs) + [ANY] * po,
        out_shape=list(out_shape) + list(plan.out_shape), scratch_shapes=list(scratch) + list(plan.sems),
        input_output_aliases=aliases, compiler_params=_cp(("arbitrary",) * len(grid), vmem),
    )(*args, *plan.inputs)
    return list(outs[:no]), list(outs[no:])


def run_plan(plan, name):
    def body(*refs):
        ins, outs, sems = refs[:len(plan.inputs)], refs[len(plan.inputs):len(plan.inputs) + len(plan.out_shape)], \
            refs[len(plan.inputs) + len(plan.out_shape):]
        plan.start(ins, outs, sems)
        plan.forward(ins, outs, sems)
        plan.finish(ins, outs, sems)

    return pl.pallas_call(body, name=name, in_specs=[ANY] * len(plan.inputs), out_specs=[ANY] * len(plan.out_shape),
                          out_shape=list(plan.out_shape), scratch_shapes=list(plan.sems))(*plan.inputs)


def mm_tn(a, b, name, tn=512, into=None, block0=0, nblocks=None):
    T, K = a.shape
    N = b.shape[1]
    tn = min(tn, N)
    nblocks = nblocks or (N // tn if into is None else into.shape[0])

    def body(a_ref, b_ref, *rest):
        rest[-1][...] = _dot_tn(a_ref[...], b_ref[...]).astype(GRAD_DT)

    args, in_specs, aliases = [a, b], [_resident((T, K)), pl.BlockSpec((T, tn), lambda j: (0, j))], {}
    if into is not None:
        args.append(into)
        in_specs.append(ANY)
        aliases = {2: 0}
    (out,), _ = _call(body, args, name=name, grid=(N // tn,), in_specs=in_specs,
                      out_specs=[pl.BlockSpec((None, K, tn), lambda j: (block0 + j, 0, 0))],
                      out_shape=[S((nblocks, K, tn), GRAD_DT)], sem=("parallel",), vmem=VMEM_LIMIT, aliases=aliases)
    return out


def mm_tn_rows(a, b, name, tk=256, plan=None):
    T, K = a.shape
    N = b.shape[1]
    tk = min(tk, K)

    def body(a_ref, b_ref, o_ref):
        o_ref[...] = _dot_tn(a_ref[...], b_ref[...]).astype(GRAD_DT)

    (out,), sent = _call(body, [a, b], name=name, grid=(K // tk,),
                         in_specs=[pl.BlockSpec((T, tk), lambda i: (0, i)), _resident((T, N))],
                         out_specs=[pl.BlockSpec((tk, N), lambda i: (i, 0))], out_shape=[S((K, N), GRAD_DT)],
                         sem=("parallel",), vmem=VMEM_LIMIT, plan=plan)
    return out, sent


def prep_weights(ws):
    def body(*refs):
        for i in range(len(ws)):
            refs[len(ws) + i][...] = refs[i][...].astype(bf16)

    return pl.pallas_call(body, name="prep_weights", out_shape=[S(w.shape, bf16) for w in ws],
                          compiler_params=_cp(None, VMEM_LIMIT))(*ws)


def in_proj(x, win_g, b_in, plan):
    T = x.shape[0]
    tm = min(512, T)

    def body(x_ref, w_ref, b_ref, o_ref, xb_ref):
        xb = x_ref[...].astype(bf16)
        xb_ref[...] = xb
        for k in range(NDEV):
            cs = slice(k * W, (k + 1) * W)
            o_ref[:, cs] = _dot(xb, w_ref[k]) + b_ref[:, cs]

    return _call(
        body, [x, win_g, b_in], name="in_proj", grid=(T // tm,),
        in_specs=[pl.BlockSpec((tm, D), lambda i: (i, 0)), _resident((NDEV, D, W)), _resident((1, IN_COLS))],
        out_specs=[pl.BlockSpec((tm, IN_COLS), lambda i: (i, 0)), pl.BlockSpec((tm, D), lambda i: (i, 0))],
        out_shape=[S((T, IN_COLS), f32), S((T, D), bf16)], vmem=VMEM_LIMIT, plan=plan)


def to_perm(a, cb0, name):
    T = a.shape[0]
    L = T // NC

    def body(a_ref, o_ref):
        def step(jb, carry):
            j0 = pl.multiple_of(jb * 8, 8)
            for q in range(NC // 8):
                x = jnp.stack([a_ref[pl.ds((8 * q + c) * L + j0, 8), :] for c in range(8)], axis=0)
                y = jnp.swapaxes(x, 0, 1)
                for j in range(8):
                    o_ref[pl.ds((j0 + j) * NC + 8 * q, 8), :] = y[j]
            return carry

        lax.fori_loop(0, L // 8, step, 0)

    return pl.pallas_call(
        body, name=name, grid=(W // LANE,),
        in_specs=[pl.BlockSpec((T, LANE), lambda k: (0, cb0 + k))], out_specs=pl.BlockSpec((T, LANE), lambda k: (0, k)),
        out_shape=S((T, W), f32), compiler_params=_cp(("parallel",), VMEM_LIMIT),
    )(a)


def from_perm(a, name, out_dtype=f32, plan=None):
    T = a.shape[0]
    L = T // NC

    def body(a_ref, o_ref):
        def step(jb, carry):
            j0 = pl.multiple_of(jb * 16, 16)
            for q in range(NC // 8):
                halves = []
                for h in range(2):
                    x = jnp.stack([a_ref[pl.ds((j0 + 8 * h + j) * NC + 8 * q, 8), :] for j in range(8)], axis=0)
                    halves.append(jnp.swapaxes(x, 0, 1))
                for c in range(8):
                    o_ref[pl.ds((8 * q + c) * L + j0, 16), :] = jnp.concatenate(
                        [halves[0][c], halves[1][c]], axis=0).astype(out_dtype)
            return carry

        lax.fori_loop(0, L // 16, step, 0)

    slab = pl.BlockSpec((T, LANE), lambda k: (0, k))
    return _call(body, [a], name=name, grid=(W // LANE,), in_specs=[slab], out_specs=[slab],
                 out_shape=[S((T, W), out_dtype)], sem=("parallel",), vmem=VMEM_LIMIT, plan=plan)


def _disc(lr, li, ldt):
    dt = jnp.exp(ldt)
    mag = jnp.exp(lr * dt)
    lbr = mag * jnp.cos(li * dt)
    lbi = mag * jnp.sin(li * dt)
    den = lr * lr + li * li
    nr = lbr - 1.0
    return lbr, lbi, (nr * lr + lbi * li) / den, (lbi * lr - nr * li) / den


def _per_channel(f):
    return jnp.broadcast_to(f[:, None, :], (NG, GC, NP)).reshape(NG * GC, NP)


def ssm_params(lam_re, lam_im, log_dt, br, bi):
    def body(lr_ref, li_ref, ldt_ref, br_ref, bi_ref, lbr_ref, lbi_ref, fr_ref, fi_ref, bbr_ref, bbi_ref):
        lbr, lbi, fr, fi = _disc(lr_ref[...], li_ref[...], ldt_ref[...])
        lbr_ref[...], lbi_ref[...], fr_ref[...], fi_ref[...] = lbr, lbi, fr, fi
        fr_, fi_, br_, bi_ = _per_channel(fr), _per_channel(fi), br_ref[...], bi_ref[...]
        bbr_ref[...] = fr_ * br_ - fi_ * bi_
        bbi_ref[...] = fr_ * bi_ + fi_ * br_

    return pl.pallas_call(body, name="ssm_params", out_shape=[S((NG, NP), f32)] * 4 + [S((NG * GC, NP), f32)] * 2)(
        lam_re, lam_im, log_dt, br, bi)


SCAN_UNROLL = 4


def _steps(n, body, carry):
    main = n // SCAN_UNROLL

    def trip(t, c):
        for q in range(SCAN_UNROLL):
            c = body(t * SCAN_UNROLL + q, c)
        return c

    carry = lax.fori_loop(0, main, trip, carry)
    for i in range(main * SCAN_UNROLL, n):
        carry = body(i, carry)
    return carry


def _scan_body(T):
    L = T // NC
    RB = min(512, T)
    nsq = int(round(math.log2(L)))
    assert 2 ** nsq == L and T % RB == 0 and L % 16 == 0

    def rows(i):
        return pl.ds(pl.multiple_of(i * RB, RB), RB)

    def tile(j):
        return pl.ds(j * NC if isinstance(j, int) else pl.multiple_of(j * NC, NC), NC)

    def forward_states(u_ref, wb_ref, lbr_ref, lbi_ref, sre, sim, ere, eim):
        def bproj(i, carry):
            bu = _dot(u_ref[rows(i), :].astype(bf16), wb_ref[...])
            sre[rows(i), :] = bu[:, :SW]
            sim[rows(i), :] = bu[:, SW:]
            return carry

        lax.fori_loop(0, T // RB, bproj, 0)
        for lb in range(SW // LANE):
            ls = slice(lb * LANE, (lb + 1) * LANE)
            ar = jnp.broadcast_to(lbr_ref[:, ls], (NC, LANE))
            ai = jnp.broadcast_to(lbi_ref[:, ls], (NC, LANE))

            def step(j, carry):
                xr, xi = carry
                nr = ar * xr - ai * xi + sre[tile(j), ls]
                ni = ar * xi + ai * xr + sim[tile(j), ls]
                sre[tile(j), ls] = nr
                sim[tile(j), ls] = ni
                return nr, ni

            zero = jnp.zeros((NC, LANE), f32)
            _steps(L, step, (zero, zero))
            pr, pi = lbr_ref[:, ls], lbi_ref[:, ls]
            for _ in range(nsq):
                pr, pi = pr * pr - pi * pi, 2.0 * pr * pi
            er = jnp.zeros((1, LANE), f32)
            ei = er
            ere[0:1, ls] = er
            eim[0:1, ls] = ei
            base = (L - 1) * NC
            for c in range(1, NC):
                lr_ = sre[base + c - 1:base + c, ls]
                li_ = sim[base + c - 1:base + c, ls]
                er, ei = lr_ + pr * er - pi * ei, li_ + pr * ei + pi * er
                ere[c:c + 1, ls] = er
                eim[c:c + 1, ls] = ei
            e_r, e_i = ere[:, ls].reshape(NC // 8, 8, LANE), eim[:, ls].reshape(NC // 8, 8, LANE)
            ar8, ai8 = ar[0:8], ai[0:8]

            def fix(j, carry):
                pwr, pwi = carry
                xr = sre[tile(j), ls].reshape(NC // 8, 8, LANE) + (pwr * e_r - pwi * e_i)
                xi = sim[tile(j), ls].reshape(NC // 8, 8, LANE) + (pwr * e_i + pwi * e_r)
                sre[tile(j), ls] = xr.reshape(NC, LANE)
                sim[tile(j), ls] = xi.reshape(NC, LANE)
                return pwr * ar8 - pwi * ai8, pwr * ai8 + pwi * ar8

            _steps(L, fix, (ar8, ai8))

    return L, RB, nsq, rows, tile, forward_states


def ssm_fwd(u_p, wb, wc, lbr, lbi, dsk, plan):
    T = u_p.shape[0]
    L, RB, nsq, rows, tile, forward_states = _scan_body(T)

    def body(u_ref, wb_ref, wc_ref, lbr_ref, lbi_ref, d_ref, y_ref, sre, sim, ere, eim):
        forward_states(u_ref, wb_ref, lbr_ref, lbi_ref, sre, sim, ere, eim)

        def cproj(i, carry):
            y = _dot(sre[rows(i), :].astype(bf16), wc_ref[0:SW, :]) + _dot(sim[rows(i), :].astype(bf16), wc_ref[SW:, :])
            y_ref[rows(i), :] = y + d_ref[...] * u_ref[rows(i), :]
            return carry

        lax.fori_loop(0, T // RB, cproj, 0)

    slab = pl.BlockSpec((T, LANE), lambda k: (0, k))
    return _call(
        body, [u_p, wb, wc, lbr, lbi, dsk], name="ssm_fwd", grid=(W // LANE,),
        in_specs=[slab, pl.BlockSpec((None, LANE, 2 * SW), lambda k: (k, 0, 0)),
                  pl.BlockSpec((None, 2 * SW, LANE), lambda k: (k, 0, 0)),
                  pl.BlockSpec((None, 1, SW), lambda k: (k, 0, 0)), pl.BlockSpec((None, 1, SW), lambda k: (k, 0, 0)),
                  pl.BlockSpec((None, 1, LANE), lambda k: (k, 0, 0))],
        out_specs=[slab], out_shape=[S((T, W), f32)],
        scratch=[pltpu.VMEM((T, SW), f32), pltpu.VMEM((T, SW), f32), pltpu.VMEM((NC, SW), f32), pltpu.VMEM((NC, SW), f32)],
        vmem=VMEM_LIMIT, plan=plan)


def ssm_bwd(u_p, dy_p, wb, wbT, wcT, lbr, lbi, dsk, plan):
    T = u_p.shape[0]
    L, RB, nsq, rows, tile, forward_states = _scan_body(T)

    def body(u_ref, dy_ref, wb_ref, wbT_ref, wcT_ref, lbr_ref, lbi_ref, d_ref,
             du_ref, dwb_ref, dwc_ref, dlr_ref, dli_ref, dd_ref, su_ref, sre, sim, gre, gim, ere, eim):
        forward_states(u_ref, wb_ref, lbr_ref, lbi_ref, sre, sim, ere, eim)

        def dstate(i, carry):
            g = _dot(dy_ref[rows(i), :].astype(bf16), wcT_ref[...])
            gre[rows(i), :] = g[:, :SW]
            gim[rows(i), :] = g[:, SW:]
            return carry

        lax.fori_loop(0, T // RB, dstate, 0)
        row = lax.broadcasted_iota(jnp.int32, (NC, LANE), 0)
        for lb in range(SW // LANE):
            ls = slice(lb * LANE, (lb + 1) * LANE)
            ar = jnp.broadcast_to(lbr_ref[:, ls], (NC, LANE))
            ai = jnp.broadcast_to(lbi_ref[:, ls], (NC, LANE))

            def step(i, carry):
                gr, gi = carry
                j = L - 1 - i
                nr = ar * gr + ai * gi + gre[tile(j), ls]
                ni = ar * gi - ai * gr + gim[tile(j), ls]
                gre[tile(j), ls] = nr
                gim[tile(j), ls] = ni
                return nr, ni

            zero = jnp.zeros((NC, LANE), f32)
            _steps(L, step, (zero, zero))
            pr, pi = lbr_ref[:, ls], -lbi_ref[:, ls]
            for _ in range(nsq):
                pr, pi = pr * pr - pi * pi, 2.0 * pr * pi
            er = jnp.zeros((1, LANE), f32)
            ei = er
            ere[NC - 1:NC, ls] = er
            eim[NC - 1:NC, ls] = ei
            for c in range(NC - 2, -1, -1):
                lr_ = gre[c + 1:c + 2, ls]
                li_ = gim[c + 1:c + 2, ls]
                er, ei = lr_ + pr * er - pi * ei, li_ + pr * ei + pi * er
                ere[c:c + 1, ls] = er
                eim[c:c + 1, ls] = ei
            e_r, e_i = ere[:, ls].reshape(NC // 8, 8, LANE), eim[:, ls].reshape(NC // 8, 8, LANE)
            ar8, ai8 = ar[0:8], ai[0:8]

            def fixed(j, pwr, pwi):
                gr = (gre[tile(j), ls].reshape(NC // 8, 8, LANE) + (pwr * e_r - pwi * e_i)).reshape(NC, LANE)
                gi = (gim[tile(j), ls].reshape(NC // 8, 8, LANE) + (pwr * e_i + pwi * e_r)).reshape(NC, LANE)
                gre[tile(j), ls] = gr
                gim[tile(j), ls] = gi
                return gr, gi

            def fix(i, carry):
                pwr, pwi, accr, acci = carry
                j = L - 1 - i
                gr, gi = fixed(j, pwr, pwi)
                xr, xi = sre[tile(j - 1), ls], sim[tile(j - 1), ls]
                return (pwr * ar8 + pwi * ai8, pwi * ar8 - pwr * ai8,
                        accr + gr * xr + gi * xi, acci + gi * xr - gr * xi)

            pwr, pwi, accr, acci = _steps(L - 1, fix, (ar8, -ai8, zero, zero))
            gr, gi = fixed(0, pwr, pwi)
            xr = jnp.where(row == 0, 0.0, pltpu.roll(sre[tile(L - 1), ls], 1, axis=0))
            xi = jnp.where(row == 0, 0.0, pltpu.roll(sim[tile(L - 1), ls], 1, axis=0))
            accr = accr + gr * xr + gi * xi
            acci = acci + gi * xr - gr * xi
            dlr_ref[:, ls] = jnp.sum(accr, axis=0, keepdims=True)
            dli_ref[:, ls] = jnp.sum(acci, axis=0, keepdims=True)

        dwb_ref[...] = jnp.zeros_like(dwb_ref)
        dwc_ref[...] = jnp.zeros_like(dwc_ref)
        dd_ref[...] = jnp.zeros_like(dd_ref)
        su_ref[...] = jnp.zeros_like(su_ref)

        def finish(i, carry):
            u32, dy32 = u_ref[rows(i), :], dy_ref[rows(i), :]
            ub, dyb = u32.astype(bf16), dy32.astype(bf16)
            gr, gi = gre[rows(i), :].astype(bf16), gim[rows(i), :].astype(bf16)
            du = _dot(gr, wbT_ref[0:SW, :]) + _dot(gi, wbT_ref[SW:, :]) + dy32 * d_ref[...]
            du_ref[rows(i), :] = du
            su_ref[...] += jnp.sum(du, axis=0, keepdims=True)
            dwb_ref[:, 0:SW] += _dot_tn(ub, gr)
            dwb_ref[:, SW:] += _dot_tn(ub, gi)
            dwc_ref[:, 0:SW] += _dot_tn(dyb, sre[rows(i), :].astype(bf16))
            dwc_ref[:, SW:] += _dot_tn(dyb, sim[rows(i), :].astype(bf16))
            dd_ref[...] += jnp.sum(dy32 * u32, axis=0, keepdims=True)
            return carry

        lax.fori_loop(0, T // RB, finish, 0)

    slab = pl.BlockSpec((T, LANE), lambda k: (0, k))
    wide = pl.BlockSpec((None, LANE, 2 * SW), lambda k: (k, 0, 0))
    tall = pl.BlockSpec((None, 2 * SW, LANE), lambda k: (k, 0, 0))
    vec = pl.BlockSpec((None, 1, SW), lambda k: (k, 0, 0))
    vecd = pl.BlockSpec((None, 1, LANE), lambda k: (k, 0, 0))
    nslab = W // LANE
    return _call(
        body, [u_p, dy_p, wb, wbT, wcT, lbr, lbi, dsk], name="ssm_bwd", grid=(nslab,),
        in_specs=[slab, slab, wide, tall, wide, vec, vec, vecd],
        out_specs=[slab, wide, wide, vec, vec, vecd, vecd],
        out_shape=[S((T, W), f32), S((nslab, LANE, 2 * SW), f32), S((nslab, LANE, 2 * SW), f32),
                   S((nslab, 1, SW), f32), S((nslab, 1, SW), f32), S((nslab, 1, LANE), f32), S((nslab, 1, LANE), f32)],
        scratch=[pltpu.VMEM((T, SW), f32)] * 4 + [pltpu.VMEM((NC, SW), f32)] * 2, vmem=VMEM_LIMIT, plan=plan)


def glu_fwd(yn, glu_w, glu_b):
    T = yn.shape[0]
    tm = min(512, T)

    def body(y_ref, w_ref, b_ref, o_ref):
        g = _gelu(y_ref[...])
        o_ref[...] = (g * _sigmoid(_dot(g.astype(bf16), w_ref[...]) + b_ref[...])).astype(bf16)

    return pl.pallas_call(
        body, name="glu_fwd", grid=(T // tm,),
        in_specs=[pl.BlockSpec((tm, W), lambda i: (i, 0)), pl.BlockSpec((W, W), lambda i: (0, 0)), pl.BlockSpec((1, W), lambda i: (0, 0))],
        out_specs=pl.BlockSpec((tm, W), lambda i: (i, 0)), out_shape=S((T, W), bf16), compiler_params=_cp(("parallel",)),
    )(yn, glu_w, glu_b)


def _shift_rows(cur, prev8, k):
    return pltpu.roll(jnp.concatenate([prev8, cur], axis=0), k, axis=0)[8:]


def _lift_rows(cur, next8, k):
    n = cur.shape[0]
    return pltpu.roll(jnp.concatenate([cur, next8], axis=0), n + 8 - k, axis=0)[:n]


def conv_fwd(proj, conv_w):
    T = proj.shape[0]
    RB = min(512, T)

    def body(h_ref, c_ref, b_ref, w_ref, o_ref):
        w0, w1, w2 = w_ref[0:1, :], w_ref[1:2, :], w_ref[2:3, :]

        def blk(i, carry):
            r0 = pl.multiple_of(i * RB, RB)
            rs = pl.ds(r0, RB)
            ch = c_ref[rs, :] * h_ref[rs, :]
            pr = pl.ds(jnp.maximum(r0 - 8, 0), 8)
            prev = jnp.where(i > 0, c_ref[pr, :] * h_ref[pr, :], 0.0)
            z = w2 * ch + w1 * _shift_rows(ch, prev, 1) + w0 * _shift_rows(ch, prev, 2)
            o_ref[rs, :] = (b_ref[rs, :] * z).astype(bf16)
            return carry

        lax.fori_loop(0, T // RB, blk, 0)

    nb = W // LANE
    return pl.pallas_call(
        body, name="conv_fwd", grid=(nb,),
        in_specs=[pl.BlockSpec((T, LANE), lambda k: (0, nb + k)), pl.BlockSpec((T, LANE), lambda k: (0, 2 * nb + k)),
                  pl.BlockSpec((T, LANE), lambda k: (0, 3 * nb + k)), pl.BlockSpec((3, LANE), lambda k: (0, k))],
        out_specs=pl.BlockSpec((T, LANE), lambda k: (0, k)), out_shape=S((T, W), bf16),
        compiler_params=_cp(("parallel",), VMEM_LIMIT),
    )(proj, proj, proj, conv_w)


def _dense_columns(blocks_ref, dense_ref):
    for k in range(NDEV):
        dense_ref[:, k * LANE:(k + 1) * LANE] = blocks_ref[k]


def merge_fwd(ya, yb, wso, wco, proj, plan):
    T = ya.shape[0]
    tm = min(1024, T)

    def body(ya_ref, yb_ref, wa_ref, wb_ref, ga_ref, gb_ref, o_ref, wa_s, wb_s):
        @pl.when(pl.program_id(0) == 0)
        def _():
            _dense_columns(wa_ref, wa_s)
            _dense_columns(wb_ref, wb_s)

        o_ref[...] = (_sigmoid(ga_ref[...]) * _dot(ya_ref[...], wa_s[...])
                      + _sigmoid(gb_ref[...]) * _dot(yb_ref[...], wb_s[...])).astype(bf16)

    act = pl.BlockSpec((tm, W), lambda i: (i, 0))
    return _call(
        body, [ya, yb, wso, wco, proj, proj], name="merge_fwd", grid=(T // tm,),
        in_specs=[act, act, _resident((NDEV, W, LANE)), _resident((NDEV, W, LANE)),
                  pl.BlockSpec((tm, D), lambda i: (i, 2)), pl.BlockSpec((tm, D), lambda i: (i, 3))],
        out_specs=[pl.BlockSpec((tm, D), lambda i: (i, 0))], out_shape=[S((T, D), bf16)],
        scratch=[pltpu.VMEM((W, D), bf16), pltpu.VMEM((W, D), bf16)], vmem=VMEM_LIMIT, plan=plan)


def mix_ln1(merged, w_o, x, g1, b1):
    T = x.shape[0]
    tm = min(512, T)

    def body(m_ref, w_ref, x_ref, g_ref, b_ref, r_ref, x1_ref):
        for rs in _row_parts(tm):
            r = ALPHA * x_ref[rs, :] + _dot(m_ref[rs, :], w_ref[...])
            r_ref[rs, :] = r
            xhat, _ = _ln_stats(r)
            x1_ref[rs, :] = (xhat * g_ref[...] + b_ref[...]).astype(bf16)

    row = pl.BlockSpec((tm, D), lambda i: (i, 0))
    vec = pl.BlockSpec((1, D), lambda i: (0, 0))
    return pl.pallas_call(
        body, name="mix_ln1", grid=(T // tm,),
        in_specs=[row, _resident((D, D)), row, vec, vec],
        out_specs=[row, row], out_shape=[S((T, D), f32), S((T, D), bf16)], compiler_params=_cp(("parallel",), VMEM_LIMIT),
    )(merged, w_o, x, g1, b1)


FT = 256


def gate_up(x1b, wgT, wuT, plan):
    T = x1b.shape[0]
    tm = min(512, T)

    def body(x_ref, wg_ref, wu_ref, g_ref, u_ref, h_ref):
        x = x_ref[...]
        for n in range(F // FT):
            cs = slice(n * FT, (n + 1) * FT)
            g = _dot_nt(x, wg_ref[cs, :])
            u = _dot_nt(x, wu_ref[cs, :])
            g_ref[:, cs] = g.astype(bf16)
            u_ref[:, cs] = u.astype(bf16)
            h_ref[:, cs] = (g * _sigmoid(g) * u).astype(bf16)

    osp = pl.BlockSpec((tm, F), lambda i: (i, 0))
    return _call(
        body, [x1b, wgT, wuT], name="gate_up", grid=(T // tm,),
        in_specs=[pl.BlockSpec((tm, D), lambda i: (i, 0)), _resident((F, D)), _resident((F, D))],
        out_specs=[osp, osp, osp], out_shape=[S((T, F), bf16)] * 3, vmem=VMEM_LIMIT, plan=plan)


def down_loss(hid, w_down, r1, g1, b1, g2, b2, target):
    T = hid.shape[0]
    tm = min(512, T)

    def body(h_ref, w_ref, r1_ref, g1_ref, b1_ref, g2_ref, b2_ref, t_ref, dr_ref, drb_ref, loss_ref, dg_ref, db_ref):
        @pl.when(pl.program_id(0) == 0)
        def _():
            loss_ref[...] = jnp.zeros_like(loss_ref)
            dg_ref[...] = jnp.zeros_like(dg_ref)
            db_ref[...] = jnp.zeros_like(db_ref)

        for rs in _row_parts(tm):
            xh1, _ = _ln_stats(r1_ref[rs, :])
            x1 = xh1 * g1_ref[...] + b1_ref[...]
            r2 = ALPHA * x1 + _dot(h_ref[rs, :], w_ref[...])
            xh2, rstd2 = _ln_stats(r2)
            err = xh2 * g2_ref[...] + b2_ref[...] - t_ref[rs, :]
            loss_ref[...] += jnp.sum(jnp.mean(err * err, axis=-1, keepdims=True), axis=0, keepdims=True)
            dy = err * (1.0 / D)
            dg_ref[...] += jnp.sum(dy * xh2, axis=0, keepdims=True)
            db_ref[...] += jnp.sum(dy, axis=0, keepdims=True)
            dr = _ln_bwd(dy, xh2, rstd2, g2_ref[...])
            dr_ref[rs, :] = dr
            drb_ref[rs, :] = dr.astype(bf16)

    row = pl.BlockSpec((tm, D), lambda i: (i, 0))
    vec = pl.BlockSpec((1, D), lambda i: (0, 0))
    return pl.pallas_call(
        body, name="down_loss", grid=(T // tm,),
        in_specs=[pl.BlockSpec((tm, F), lambda i: (i, 0)), _resident((F, D)), row, vec, vec, vec, vec, row],
        out_specs=[row, row, pl.BlockSpec((1, 1), lambda i: (0, 0)), vec, vec],
        out_shape=[S((T, D), f32), S((T, D), bf16), S((1, 1), f32), S((1, D), f32), S((1, D), f32)],
        compiler_params=_cp(("arbitrary",), VMEM_LIMIT),
    )(hid, w_down, r1, g1, b1, g2, b2, target)


def ffn_bwd_act(dffn, w_down, gate, up):
    T = dffn.shape[0]
    tm = min(512, T)

    def body(d_ref, w_ref, g_ref, u_ref, dg_ref, du_ref):
        d = d_ref[...]
        for n in range(F // FT):
            cs = slice(n * FT, (n + 1) * FT)
            dh = _dot_nt(d, w_ref[cs, :])
            g, u = g_ref[:, cs].astype(f32), u_ref[:, cs].astype(f32)
            sg = _sigmoid(g)
            du_ref[:, cs] = (dh * g * sg).astype(bf16)
            dg_ref[:, cs] = (dh * u * sg * (1.0 + g * (1.0 - sg))).astype(bf16)

    osp = pl.BlockSpec((tm, F), lambda i: (i, 0))
    return pl.pallas_call(
        body, name="ffn_bwd_act", grid=(T // tm,),
        in_specs=[pl.BlockSpec((tm, D), lambda i: (i, 0)), _resident((F, D)), osp, osp],
        out_specs=[osp, osp], out_shape=[S((T, F), bf16)] * 2, compiler_params=_cp(("parallel",), VMEM_LIMIT),
    )(dffn, w_down, gate, up)


def ffn_bwd_x(dgate, dup, wgT, wuT, dr2, r1, g1, plan):
    T = dr2.shape[0]
    tm = min(512, T)

    def body(dg_ref, du_ref, wg_ref, wu_ref, dr2_ref, r1_ref, g1_ref, dr_ref, drb_ref, dgam_ref, dbet_ref):
        @pl.when(pl.program_id(0) == 0)
        def _():
            dgam_ref[...] = jnp.zeros_like(dgam_ref)
            dbet_ref[...] = jnp.zeros_like(dbet_ref)

        for rs in _row_parts(tm):
            dx1 = ALPHA * dr2_ref[rs, :] + _dot(dg_ref[rs, :], wg_ref[...]) + _dot(du_ref[rs, :], wu_ref[...])
            xh, rstd = _ln_stats(r1_ref[rs, :])
            dgam_ref[...] += jnp.sum(dx1 * xh, axis=0, keepdims=True)
            dbet_ref[...] += jnp.sum(dx1, axis=0, keepdims=True)
            dr = _ln_bwd(dx1, xh, rstd, g1_ref[...])
            dr_ref[rs, :] = dr
            drb_ref[rs, :] = dr.astype(bf16)

    row = pl.BlockSpec((tm, D), lambda i: (i, 0))
    wide = pl.BlockSpec((tm, F), lambda i: (i, 0))
    wsp = _resident((F, D))
    vec = pl.BlockSpec((1, D), lambda i: (0, 0))
    return _call(
        body, [dgate, dup, wgT, wuT, dr2, r1, g1], name="ffn_bwd_x", grid=(T // tm,),
        in_specs=[wide, wide, wsp, wsp, row, row, vec],
        out_specs=[row, row, vec, vec], out_shape=[S((T, D), f32), S((T, D), bf16), S((1, D), f32), S((1, D), f32)],
        vmem=VMEM_LIMIT, plan=plan)


def merge_bwd(dmix, w_o, ya, yb, wso, wco, proj, plan):
    T = dmix.shape[0]
    tm = min(512, T)

    def body(dm_ref, wo_ref, ya_ref, yb_ref, wa_ref, wb_ref, ga_ref, gb_ref, dya_ref, dyb_ref, dga_ref, dgb_ref, sa_ref, sb_ref,
             wa_s, wb_s):
        @pl.when(pl.program_id(0) == 0)
        def _():
            _dense_columns(wa_ref, wa_s)
            _dense_columns(wb_ref, wb_s)

        dmer = _dot_nt(dm_ref[...], wo_ref[...])
        sa, sb = _sigmoid(ga_ref[...]), _sigmoid(gb_ref[...])
        dya_ref[...] = (dmer * sa).astype(bf16)
        dyb_ref[...] = (dmer * sb).astype(bf16)
        dga = dmer * _dot(ya_ref[...], wa_s[...]) * sa * (1.0 - sa)
        dgb = dmer * _dot(yb_ref[...], wb_s[...]) * sb * (1.0 - sb)
        dga_ref[...] = dga.astype(bf16)
        dgb_ref[...] = dgb.astype(bf16)
        sa_ref[...] = jnp.sum(dga, axis=0, keepdims=True)
        sb_ref[...] = jnp.sum(dgb, axis=0, keepdims=True)

    act = pl.BlockSpec((tm, W), lambda i: (i, 0))
    osp = pl.BlockSpec((tm, D), lambda i: (i, 0))
    ssp = pl.BlockSpec((None, 1, D), lambda i: (i, 0, 0))
    return _call(
        body, [dmix, w_o, ya, yb, wso, wco, proj, proj], name="merge_bwd", grid=(T // tm,),
        in_specs=[osp, _resident((D, D)), act, act, _resident((NDEV, W, LANE)), _resident((NDEV, W, LANE)),
                  pl.BlockSpec((tm, D), lambda i: (i, 2)), pl.BlockSpec((tm, D), lambda i: (i, 3))],
        out_specs=[osp, osp, osp, osp, ssp, ssp],
        out_shape=[S((T, D), bf16)] * 4 + [S((T // tm, 1, D), f32)] * 2,
        scratch=[pltpu.VMEM((W, D), bf16), pltpu.VMEM((W, D), bf16)], vmem=VMEM_LIMIT, plan=plan)


def branches_bwd_x(dYA, dYB, wso, wco, plan):
    T = dYA.shape[0]
    tm = min(1024, T)

    def body(da_ref, db_ref, wa_ref, wb_ref, oa_ref, ob_ref, wa_s, wb_s):
        @pl.when(pl.program_id(0) == 0)
        def _():
            _dense_columns(wa_ref, wa_s)
            _dense_columns(wb_ref, wb_s)

        oa_ref[...] = _dot_nt(da_ref[...], wa_s[...])
        ob_ref[...] = _dot_nt(db_ref[...], wb_s[...])

    row = pl.BlockSpec((tm, D), lambda i: (i, 0))
    osp = pl.BlockSpec((tm, W), lambda i: (i, 0))
    return _call(
        body, [dYA, dYB, wso, wco], name="branches_bwd_x", grid=(T // tm,),
        in_specs=[row, row, _resident((NDEV, W, LANE)), _resident((NDEV, W, LANE))],
        out_specs=[osp, osp], out_shape=[S((T, W), f32)] * 2,
        scratch=[pltpu.VMEM((W, D), bf16), pltpu.VMEM((W, D), bf16)], vmem=VMEM_LIMIT, plan=plan)


def branch_bwd_w(act, dY, name):
    T = act.shape[0]
    tk = W // 2

    def body(a_ref, d_ref, o_ref):
        res = _dot_tn(a_ref[...], d_ref[...])
        for k in range(NDEV):
            o_ref[k] = res[:, k * LANE:(k + 1) * LANE].astype(o_ref.dtype)

    return pl.pallas_call(
        body, name=name, grid=(W // tk,),
        in_specs=[pl.BlockSpec((T, tk), lambda i: (0, i)), _resident((T, D))],
        out_specs=pl.BlockSpec((NDEV, tk, LANE), lambda i: (0, i, 0)), out_shape=S((NDEV, W, LANE), GRAD_DT),
        compiler_params=_cp(("parallel",), VMEM_LIMIT),
    )(act, dY)


def glu_bwd(yn, dya, glu_w, glu_b):
    T = yn.shape[0]
    tm = min(512, T)

    def body(y_ref, d_ref, w_ref, b_ref, dy_ref, dsp_ref, g_ref, db_ref):
        @pl.when(pl.program_id(0) == 0)
        def _():
            db_ref[...] = jnp.zeros_like(db_ref)

        y, dya_ = y_ref[...], d_ref[...]
        g = _gelu(y)
        gb = g.astype(bf16)
        s = _sigmoid(_dot(gb, w_ref[...]) + b_ref[...])
        dsp = dya_ * g * s * (1.0 - s)
        dspb = dsp.astype(bf16)
        dg = dya_ * s + _dot_nt(dspb, w_ref[...])
        dy_ref[...] = dg * _gelu_grad(y)
        dsp_ref[...] = dspb
        g_ref[...] = gb
        db_ref[...] += jnp.sum(dsp, axis=0, keepdims=True)

    row = pl.BlockSpec((tm, W), lambda i: (i, 0))
    vec = pl.BlockSpec((1, W), lambda i: (0, 0))
    return pl.pallas_call(
        body, name="glu_bwd", grid=(T // tm,),
        in_specs=[row, row, pl.BlockSpec((W, W), lambda i: (0, 0)), vec],
        out_specs=[row, row, row, vec], out_shape=[S((T, W), f32), S((T, W), bf16), S((T, W), bf16), S((1, W), f32)],
        compiler_params=_cp(("arbitrary",)),
    )(yn, dya, glu_w, glu_b)


def conv_bwd(proj, dyb, conv_w):
    T = proj.shape[0]
    RB = min(512, T)
    nrb = T // RB

    def body(h_ref, c_ref, b_ref, d_ref, w_ref, dh_ref, dc_ref, db_ref, dw_ref, s_ref):
        w0, w1, w2 = w_ref[0:1, :], w_ref[1:2, :], w_ref[2:3, :]

        def blk(i, carry):
            a0, a1, a2, sh, sc, sb = carry
            r0 = pl.multiple_of(i * RB, RB)
            rs = pl.ds(r0, RB)
            h, cg, bg, dyb_ = h_ref[rs, :], c_ref[rs, :], b_ref[rs, :], d_ref[rs, :]
            ch = cg * h
            pr = pl.ds(jnp.maximum(r0 - 8, 0), 8)
            prev = jnp.where(i > 0, c_ref[pr, :] * h_ref[pr, :], 0.0)
            ch1, ch2 = _shift_rows(ch, prev, 1), _shift_rows(ch, prev, 2)
            dbg = dyb_ * (w2 * ch + w1 * ch1 + w0 * ch2)
            db_ref[rs, :] = dbg.astype(bf16)
            dz = dyb_ * bg
            nx = pl.ds(jnp.minimum(r0 + RB, T - 8), 8)
            nxt = jnp.where(i < nrb - 1, d_ref[nx, :] * b_ref[nx, :], 0.0)
            dch = w2 * dz + w1 * _lift_rows(dz, nxt, 1) + w0 * _lift_rows(dz, nxt, 2)
            dcg, dh = dch * h, dch * cg
            dc_ref[rs, :] = dcg.astype(bf16)
            dh_ref[rs, :] = dh.astype(bf16)
            col = lambda v: jnp.sum(v, axis=0, keepdims=True)
            return (a0 + col(dz * ch2), a1 + col(dz * ch1), a2 + col(dz * ch), sh + col(dh), sc + col(dcg), sb + col(dbg))

        zero = jnp.zeros((1, LANE), f32)
        a0, a1, a2, sh, sc, sb = lax.fori_loop(0, nrb, blk, (zero,) * 6)
        dw_ref[0:1, :] = a0
        dw_ref[1:2, :] = a1
        dw_ref[2:3, :] = a2
        s_ref[0:1, :] = sh
        s_ref[1:2, :] = sc
        s_ref[2:3, :] = sb

    nb = W // LANE
    slab = pl.BlockSpec((T, LANE), lambda k: (0, k))
    three = pl.BlockSpec((3, LANE), lambda k: (0, k))
    return pl.pallas_call(
        body, name="conv_bwd", grid=(nb,),
        in_specs=[pl.BlockSpec((T, LANE), lambda k: (0, nb + k)), pl.BlockSpec((T, LANE), lambda k: (0, 2 * nb + k)),
                  pl.BlockSpec((T, LANE), lambda k: (0, 3 * nb + k)), slab, three],
        out_specs=[slab, slab, slab, three, three],
        out_shape=[S((T, W), bf16)] * 3 + [S((3, W), f32)] * 2, compiler_params=_cp(("parallel",), VMEM_LIMIT),
    )(proj, proj, proj, dyb, conv_w)


def in_proj_bwd_x(parts, win_g, base, scale, name, plan=None):
    T = base.shape[0]
    tm = min(512, T)
    n = len(parts)

    def body(*refs):
        p_refs, w_ref, b_ref, o_ref = refs[:n], refs[n], refs[n + 1], refs[n + 2]
        acc = scale * b_ref[...]
        for p_ref, (_, _, k) in zip(p_refs, parts):
            acc += _dot_nt(p_ref[...], w_ref[k])
        o_ref[...] = acc

    row = pl.BlockSpec((tm, D), lambda i: (i, 0))
    p_specs = [pl.BlockSpec((tm, W), (lambda i, cb=cb: (i, cb))) for _, cb, _ in parts]
    return _call(
        body, [a for a, _, _ in parts] + [win_g, base], name=name, grid=(T // tm,),
        in_specs=p_specs + [_resident((NDEV, D, W)), row],
        out_specs=[row], out_shape=[S((T, D), f32)], vmem=VMEM_LIMIT, plan=plan)


def ssm_param_bwd(lam_re, lam_im, log_dt, fr, fi, br, bi, dbbr, dbbi, dlbr, dlbi):
    def body(lr_ref, li_ref, ldt_ref, fr_ref, fi_ref, br_ref, bi_ref, dr_ref, di_ref, dlbr_ref, dlbi_ref,
             dbr_ref, dbi_ref, dlr_ref, dli_ref, dldt_ref):
        fr_, fi_ = _per_channel(fr_ref[...]), _per_channel(fi_ref[...])
        br_, bi_, dr, di = br_ref[...], bi_ref[...], dr_ref[...], di_ref[...]
        dbr_ref[...] = fr_ * dr + fi_ * di
        dbi_ref[...] = fr_ * di - fi_ * dr
        dfr = jnp.sum((dr * br_ + di * bi_).reshape(NG, GC, NP), axis=1)
        dfi = jnp.sum((di * br_ - dr * bi_).reshape(NG, GC, NP), axis=1)
        _, vjp = jax.vjp(_disc, lr_ref[...], li_ref[...], ldt_ref[...])
        dlr_ref[...], dli_ref[...], dldt = vjp((dlbr_ref[...], dlbi_ref[...], dfr, dfi))
        dldt_ref[...] = _transpose_exact(dldt)

    return pl.pallas_call(
        body, name="ssm_param_bwd",
        out_shape=[S((NG * GC, NP), f32)] * 2 + [S((NG, NP), f32)] * 2 + [S((1, NG), f32)])(
        lam_re, lam_im, log_dt, fr, fi, br, bi, dbbr, dbbi, dlbr, dlbi)


def _adam(w, g, m, v):
    m = ADAM_B1 * m + (1.0 - ADAM_B1) * g
    v = ADAM_B2 * v + (1.0 - ADAM_B2) * (g * g)
    m_hat = m / (1.0 - ADAM_B1 ** ADAM_STEP)
    v_hat = v / (1.0 - ADAM_B2 ** ADAM_STEP)
    return -ADAM_LR * (m_hat / (jnp.sqrt(v_hat) + ADAM_EPS) + ADAM_WD * w), m, v


def adam_update(w, m, v, contrib, name, rows_per_block=None):
    R, C = w.shape
    n = contrib.shape[0]
    tr = min(rows_per_block or R, R)

    def body(w_ref, m_ref, v_ref, c_ref, g_ref, d_ref, nm_ref, nv_ref):
        g = c_ref[0].astype(f32)
        for k in range(1, n):
            g = g + c_ref[k].astype(f32)
        g_ref[...] = g
        d_ref[...], nm_ref[...], nv_ref[...] = _adam(w_ref[...], g, m_ref[...], v_ref[...])

    blk = pl.BlockSpec((tr, C), lambda i: (i, 0))
    return pl.pallas_call(
        body, name=name, grid=(R // tr,), in_specs=[blk, blk, blk, pl.BlockSpec((n, tr, C), lambda i: (0, i, 0))],
        out_specs=[blk] * 4, out_shape=[S((R, C), f32)] * 4, compiler_params=_cp(("parallel",), VMEM_LIMIT),
    )(w, m, v, contrib)


_ROWVEC = (("b_in", IN_COLS), ("ssm_d", W), ("glu_b", W), ("ln1_g", D), ("ln1_b", D), ("ln2_g", D), ("ln2_b", D))
_HALF = NG * GC // 2
_PACK = {}
_r = 0
for _n, _k in _ROWVEC:
    _PACK[_n] = _r
    _r += _k // LANE
for _n, _rows in (("ssm_lambda", NG), ("scalars", 8), ("ssm_b_re", _HALF), ("ssm_b_im", _HALF), ("ssm_c_re", _HALF),
                  ("ssm_c_im", _HALF), ("conv_w", 16)):
    _PACK[_n] = _r
    _r += _rows
PACK_ROWS = _r
assert PACK_ROWS % 8 == 0
_SMALL = ("b_in", "ssm_lambda_re", "ssm_lambda_im", "ssm_log_dt", "ssm_b_re", "ssm_b_im", "ssm_c_re", "ssm_c_im",
          "ssm_d", "glu_b", "ln1_g", "ln1_b", "ln2_g", "ln2_b")


def pack_grads(su, shcb, sga, sgb, dd, dglu_b, dln1_g, dln1_b, dln2_g, dln2_b, dlam_re, dlam_im, dldt, sqerr, dbr, dbi,
               dc_re, dc_im, dconv):
    nI = sga.shape[0]

    def body(su_ref, sh_ref, sga_ref, sgb_ref, dd_ref, gb_ref, l1g_ref, l1b_ref, l2g_ref, l2b_ref, lr_ref, li_ref, dt_ref,
             sq_ref, br_ref, bi_ref, cr_ref, ci_ref, cw_ref, o_ref):
        o_ref[...] = jnp.zeros_like(o_ref)

        def put_row(name, v):
            r0 = _PACK[name]
            for i in range(v.shape[1] // LANE):
                o_ref[r0 + i:r0 + i + 1, :] = v[:, i * LANE:(i + 1) * LANE]

        ga, gb = sga_ref[0], sgb_ref[0]
        for i in range(1, nI):
            ga, gb = ga + sga_ref[i], gb + sgb_ref[i]
        put_row("b_in", jnp.concatenate([su_ref[k] for k in range(W // LANE)]
                                        + [sh_ref[0:1, :], sh_ref[1:2, :], sh_ref[2:3, :], ga, gb], axis=1))
        put_row("ssm_d", jnp.concatenate([dd_ref[k] for k in range(W // LANE)], axis=1))
        put_row("glu_b", gb_ref[...])
        put_row("ln1_g", l1g_ref[...])
        put_row("ln1_b", l1b_ref[...])
        put_row("ln2_g", l2g_ref[...])
        put_row("ln2_b", l2b_ref[...])
        r0 = _PACK["ssm_lambda"]
        o_ref[r0:r0 + NG, 0:NP] = lr_ref[...]
        o_ref[r0:r0 + NG, NP:2 * NP] = li_ref[...]
        r0 = _PACK["scalars"]
        o_ref[r0:r0 + 1, 0:NG] = dt_ref[...]
        o_ref[r0 + 1:r0 + 2, 0:1] = sq_ref[...]
        for name, ref in (("ssm_b_re", br_ref), ("ssm_b_im", bi_ref), ("ssm_c_re", cr_ref), ("ssm_c_im", ci_ref)):
            r0 = _PACK[name]
            o_ref[r0:r0 + _HALF, 0:NP] = ref[0:_HALF, :]
            o_ref[r0:r0 + _HALF, NP:2 * NP] = ref[_HALF:2 * _HALF, :]
        for cb in range(W // LANE):
            o_ref[_PACK["conv_w"] + 3 * cb:_PACK["conv_w"] + 3 * cb + 3, :] = cw_ref[:, cb * LANE:(cb + 1) * LANE]

    return pl.pallas_call(body, name="pack_grads", out_shape=S((PACK_ROWS, LANE), f32))(
        su, shcb, sga, sgb, dd, dglu_b, dln1_g, dln1_b, dln2_g, dln2_b, dlam_re, dlam_im, dldt, sqerr, dbr, dbi, dc_re, dc_im,
        dconv)


def adam_small(packed_all, params):
    names = list(_SMALL) + ["conv_w"]
    flat = [a for n in names for a in params[n]]

    def body(*refs):
        p_ref = refs[0]
        ins = refs[1:1 + 3 * len(names)]
        outs = refs[1 + 3 * len(names):-2]
        loss_ref, g_ref = refs[-2], refs[-1]
        g_all = p_ref[0]
        for k in range(1, NDEV):
            g_all = g_all + p_ref[k]
        g_ref[...] = g_all

        def rows(name, r0, n, l0=0, lanes=LANE):
            return g_ref[_PACK[name] + r0:_PACK[name] + r0 + n, l0:l0 + lanes]

        def grad_of(name):
            if name in dict(_ROWVEC):
                return jnp.concatenate([rows(name, i, 1) for i in range(dict(_ROWVEC)[name] // LANE)], axis=1)
            if name in ("ssm_lambda_re", "ssm_lambda_im"):
                return rows("ssm_lambda", 0, NG, NP * (name == "ssm_lambda_im"), NP)[None]
            if name == "ssm_log_dt":
                return rows("scalars", 0, 1, 0, NG)
            if name in ("ssm_b_re", "ssm_b_im", "ssm_c_re", "ssm_c_im"):
                return jnp.concatenate([rows(name, 0, _HALF, 0, NP), rows(name, 0, _HALF, NP, NP)], axis=0).reshape(1, NG, GC, NP)
            full = jnp.concatenate([rows("conv_w", 3 * cb, 3) for cb in range(W // LANE)], axis=1)
            x, y, c = _coords()
            col0 = (4 * x + 2 * y + c) * (W // NDEV)
            sel = (lax.broadcasted_iota(jnp.int32, (W, W // NDEV), 0)
                   == lax.broadcasted_iota(jnp.int32, (W, W // NDEV), 1) + col0).astype(f32)
            return jnp.dot(full, sel, precision=HIGHEST, preferred_element_type=f32)[None]

        loss_ref[...] = 0.5 * rows("scalars", 1, 1, 0, 1)
        for i, name in enumerate(names):
            w_ref, m_ref, v_ref = ins[3 * i:3 * i + 3]
            g = grad_of(name)
            d, m, v = _adam(w_ref[...], g, m_ref[...], v_ref[...])
            outs[4 * i][...] = g
            outs[4 * i + 1][...] = d
            outs[4 * i + 2][...] = m
            outs[4 * i + 3][...] = v

    out_shape = [S(params[n][0].shape, f32) for n in names for _ in range(4)] + [S((1, 1), f32)]
    res = pl.pallas_call(body, name="adam_small", out_shape=out_shape, scratch_shapes=[pltpu.VMEM((PACK_ROWS, LANE), f32)],
                         compiler_params=_cp(None, VMEM_LIMIT))(packed_all, *flat)
    return {n: res[4 * i:4 * i + 4] for i, n in enumerate(names)}, res[-1]


def _block_diag(wgt):
    eye = jnp.eye(8, dtype=wgt.dtype)
    out = wgt[:, :, :, None, :] * eye[None, :, None, :, None]
    return out.reshape(4, 8 * wgt.shape[2], 8 * wgt.shape[3])


def _diag_blocks(m, a, b):
    m = m.reshape(4, 8, a, 8, b)
    idx = jnp.arange(8)
    return m[:, idx, :, idx, :].transpose(1, 0, 2, 3)


def kernel(x, w_in, b_in, ssm_lambda_re, ssm_lambda_im, ssm_log_dt, ssm_b_re, ssm_b_im, ssm_c_re, ssm_c_im, ssm_d, glu_w, glu_b, w_ssm_out, conv_w, w_conv_out, w_o, ln1_g, ln1_b, w_gate, w_up, w_down, ln2_g, ln2_b, loss_target, m_w_in, m_b_in, m_ssm_lambda_re, m_ssm_lambda_im, m_ssm_log_dt, m_ssm_b_re, m_ssm_b_im, m_ssm_c_re, m_ssm_c_im, m_ssm_d, m_glu_w, m_glu_b, m_w_ssm_out, m_conv_w, m_w_conv_out, m_w_o, m_ln1_g, m_ln1_b, m_w_gate, m_w_up, m_w_down, m_ln2_g, m_ln2_b, v_w_in, v_b_in, v_ssm_lambda_re, v_ssm_lambda_im, v_ssm_log_dt, v_ssm_b_re, v_ssm_b_im, v_ssm_c_re, v_ssm_c_im, v_ssm_d, v_glu_w, v_glu_b, v_w_ssm_out, v_conv_w, v_w_conv_out, v_w_o, v_ln1_g, v_ln1_b, v_w_gate, v_w_up, v_w_down, v_ln2_g, v_ln2_b):
    given = dict(locals())
    xs = x[0]
    target = loss_target[0]

    tr = lambda a: jnp.swapaxes(a[0], 0, 1)
    win_s, glu_s, wso_s, wco_s, wo_s, wgT_s, wuT_s, wd_s = prep_weights(
        [w_in[0], glu_w[0], w_ssm_out[0], w_conv_out[0], w_o[0], tr(w_gate), tr(w_up), w_down[0]])
    win_g, conv_g = run_plan(GatherPlan([win_s, conv_w[0]]), "gather_w_in")
    conv_f = conv_g.transpose(1, 0, 2).reshape(3, W)

    lam_re, lam_im = ssm_lambda_re[0], ssm_lambda_im[0]
    ldt = ssm_log_dt[0].reshape(NG, 1)
    br2 = jnp.swapaxes(ssm_b_re[0], 1, 2).reshape(NG * GC, NP)
    bi2 = jnp.swapaxes(ssm_b_im[0], 1, 2).reshape(NG * GC, NP)
    lbr, lbi, fr, fi, bbr, bbi = ssm_params(lam_re, lam_im, ldt, br2, bi2)
    bb_t = lambda b: b.reshape(4, 8, GC, NP)
    wb = jnp.concatenate([_block_diag(bb_t(bbr)), _block_diag(bb_t(bbi))], axis=2)
    c_t = lambda c: c.reshape(4, 8, GC, NP).transpose(0, 1, 3, 2)
    wc = jnp.concatenate([_block_diag(c_t(ssm_c_re[0])), -_block_diag(c_t(ssm_c_im[0]))], axis=1)
    wbT, wcT = wb.transpose(0, 2, 1), wc.transpose(0, 2, 1)
    wb, wc, wbT, wcT = wb.astype(bf16), wc.astype(bf16), wbT.astype(bf16), wcT.astype(bf16)
    lbr_s, lbi_s = lbr.reshape(4, 1, SW), lbi.reshape(4, 1, SW)
    dsk = ssm_d[0].reshape(4, 1, LANE)

    (proj, xb), (glu_g, wso_g, wco_g, wo_g) = in_proj(xs, win_g, b_in, GatherPlan([glu_s, wso_s, wco_s, wo_s]))
    glu_f, wo_f = glu_g.reshape(W, W), wo_g.reshape(D, D)
    u_p = to_perm(proj, 0, "perm_u")
    (y_p,), (wgT_g,) = ssm_fwd(u_p, wb, wc, lbr_s, lbi_s, dsk, GatherPlan([wgT_s]))
    (yn,), _ = from_perm(y_p, "unperm_y")
    ya = glu_fwd(yn, glu_f, glu_b)
    yb = conv_fwd(proj, conv_f)
    (merged,), (wuT_g,) = merge_fwd(ya, yb, wso_g, wco_g, proj, GatherPlan([wuT_s]))
    wgT, wuT = wgT_g.reshape(F, D), wuT_g.reshape(F, D)
    r1, x1b = mix_ln1(merged, wo_f, xs, ln1_g, ln1_b)
    (gate, up, hid), (wd_g,) = gate_up(x1b, wgT, wuT, GatherPlan([wd_s]))
    wd_f = wd_g.reshape(F, D)
    dr2, dffn, sqerr, dln2_g, dln2_b = down_loss(hid, wd_f, r1, ln1_g, ln1_b, ln2_g, ln2_b, target)

    half_a, half_b = (0, 3, 5, 6), (1, 2, 4, 7)
    dgate, dup = ffn_bwd_act(dffn, wd_f, gate, up)
    dwd, _ = mm_tn_rows(hid, dffn, "grad_w_down")
    dwd = dwd.reshape(NDEV, FS, D)
    dwgT, (r_wd,) = mm_tn_rows(dgate, x1b, "grad_w_gate", plan=ScatterPlan([dwd], only=half_a))
    dwuT, (r_wd,) = mm_tn_rows(dup, x1b, "grad_w_up", plan=ScatterPlan([dwd], only=half_b, into=[r_wd]))
    dwgT, dwuT = dwgT.reshape(NDEV, FS, D), dwuT.reshape(NDEV, FS, D)
    (dr1, dmix, dln1_g, dln1_b), (r_wgT,) = ffn_bwd_x(dgate, dup, wgT, wuT, dr2, r1, ln1_g, ScatterPlan([dwgT]))
    (dYA, dYB, dga, dgb, sga, sgb), (r_wuT,) = merge_bwd(dmix, wo_f, ya, yb, wso_g, wco_g, proj,
                                                         ScatterPlan([dwuT], only=half_a))
    dwo, _ = mm_tn_rows(merged, dmix, "grad_w_o")
    dwo = dwo.reshape(NDEV, D // NDEV, D)
    (dya, dyb), (r_wuT,) = branches_bwd_x(dYA, dYB, wso_g, wco_g, ScatterPlan([dwuT], only=half_b, into=[r_wuT]))
    dwso = branch_bwd_w(ya, dYA, "grad_w_ssm_out")
    dwco = branch_bwd_w(yb, dYB, "grad_w_conv_out")
    dyn, dsp, gb, dglu_b = glu_bwd(yn, dya, glu_f, glu_b)
    dglu = mm_tn_rows(gb, dsp, "grad_glu_w")[0].reshape(NDEV, W // NDEV, W)
    dh, dcg, dbg, dconv, shcb = conv_bwd(proj, dyb, conv_f)
    dwin = mm_tn(xb, dgb, "grad_w_in_gb", block0=6, nblocks=NDEV)
    dwin = mm_tn(xb, dga, "grad_w_in_ga", block0=4, into=dwin)
    dwin = mm_tn(xb, dbg, "grad_w_in_bg", block0=3, into=dwin)
    dwin = mm_tn(xb, dcg, "grad_w_in_cg", block0=2, into=dwin)
    dwin = mm_tn(xb, dh, "grad_w_in_h", block0=1, into=dwin)
    dy_p = to_perm(dyn, 0, "perm_dy")
    (du_p, dwb, dwcT, dlbr_s, dlbi_s, dd, su), (r_wo, r_wso, r_wco, r_glu, r_win) = ssm_bwd(
        u_p, dy_p, wb, wbT, wcT, lbr_s, lbi_s, dsk,
        Plans([ScatterPlan([dwo, dwso, dwco, dglu]), ScatterPlan([dwin], only=tuple(range(1, NDEV)))]))

    dbb = lambda m: _diag_blocks(m, GC, NP).reshape(NG * GC, NP)
    dbr2, dbi2, dlam_re, dlam_im, dldt = ssm_param_bwd(
        lam_re, lam_im, ldt, fr, fi, br2, bi2, dbb(dwb[:, :, :SW]), dbb(dwb[:, :, SW:]),
        dlbr_s.reshape(NG, NP), dlbi_s.reshape(NG, NP))
    packed = pack_grads(su, shcb, sga, sgb, dd, dglu_b, dln1_g, dln1_b, dln2_g, dln2_b, dlam_re, dlam_im, dldt, sqerr,
                        dbr2, dbi2, dbb(dwcT[:, :, :SW]), -dbb(dwcT[:, :, SW:]), dconv)
    (du,), (small_all,) = from_perm(du_p, "unperm_du", bf16, GatherPlan([packed]))
    dwin = mm_tn(xb, du, "grad_w_in_u", block0=0, into=dwin)

    rest = [(dh, 0, 1), (dcg, 0, 2), (dbg, 0, 3), (dga, 0, 4), (dga, 1, 5), (dgb, 0, 6), (dgb, 1, 7)]
    (gx_rest,), (r_win,) = in_proj_bwd_x(rest, win_g, dr1, ALPHA, "in_proj_bwd_x_rest",
                                        ScatterPlan([dwin], only=(0,), into=[r_win]))
    (grad_x,), _ = in_proj_bwd_x([(du, 0, 0)], win_g, gx_rest, 1.0, "in_proj_bwd_x_u")

    out = {}

    def put(name, res, back=lambda a: a[None]):
        out["grad_" + name], out["delta_" + name], out["new_m_" + name], out["new_v_" + name] = [back(r) for r in res]

    put("w_in", adam_update(w_in[0], m_w_in[0], v_w_in[0], r_win, "adam_w_in", 256))
    put("glu_w", adam_update(glu_w[0], m_glu_w[0], v_glu_w[0], r_glu, "adam_glu_w"))
    put("w_ssm_out", adam_update(w_ssm_out[0], m_w_ssm_out[0], v_w_ssm_out[0], r_wso, "adam_w_ssm_out"))
    put("w_conv_out", adam_update(w_conv_out[0], m_w_conv_out[0], v_w_conv_out[0], r_wco, "adam_w_conv_out"))
    put("w_o", adam_update(w_o[0], m_w_o[0], v_w_o[0], r_wo, "adam_w_o"))
    put("w_down", adam_update(w_down[0], m_w_down[0], v_w_down[0], r_wd, "adam_w_down", 176))
    untr = lambda a: jnp.swapaxes(a, 0, 1)[None]
    put("w_gate", adam_update(tr(w_gate), tr(m_w_gate), tr(v_w_gate), r_wgT, "adam_w_gate", 176), untr)
    put("w_up", adam_update(tr(w_up), tr(m_w_up), tr(v_w_up), r_wuT, "adam_w_up", 176), untr)
    as_c = lambda a: jnp.swapaxes(a, 2, 3)
    params = {n: (given[n], given["m_" + n], given["v_" + n]) for n in list(_SMALL) + ["conv_w"]}
    for n in ("ssm_b_re", "ssm_b_im"):
        params[n] = tuple(as_c(a) for a in params[n])
    small, loss = adam_small(small_all, params)
    for n, res in small.items():
        put(n, res, as_c if n in ("ssm_b_re", "ssm_b_im") else (lambda a: a))

    names = ["w_in", "b_in", "ssm_lambda_re", "ssm_lambda_im", "ssm_log_dt", "ssm_b_re", "ssm_b_im", "ssm_c_re", "ssm_c_im",
             "ssm_d", "glu_w", "glu_b", "w_ssm_out", "conv_w", "w_conv_out", "w_o", "ln1_g", "ln1_b", "w_gate", "w_up",
             "w_down", "ln2_g", "ln2_b"]
    return (loss.reshape(()), grad_x[None], *[out[p + n] for p in ("grad_", "delta_", "new_m_", "new_v_") for n in names])
```

```python
import functools
import math

import jax
import jax.numpy as jnp
from jax import lax
from jax.experimental import pallas as pl
from jax.experimental.pallas import tpu as pltpu

f32, bf16 = jnp.float32, jnp.bfloat16
S = jax.ShapeDtypeStruct
MESH = pl.DeviceIdType.MESH
HIGHEST = lax.Precision.HIGHEST

D = 1024
W = 512
NG, NP, GC = 32, 64, 16
F = 2816
NDEV = 8
FS = F // NDEV
IN_COLS = 8 * W
ALPHA = 2.0 ** 0.25
LN_EPS = 1e-5
ADAM_LR, ADAM_B1, ADAM_B2, ADAM_EPS, ADAM_WD, ADAM_STEP = 0.001, 0.9, 0.999, 1e-08, 0.01, 10
NC = 32
LANE = 128
SW = 4 * LANE
VMEM_LIMIT = 56 * 1024 * 1024
GRAD_DT = bf16
ANY = pl.BlockSpec(memory_space=pl.ANY)


def _cp(sem=None, vmem=None):
    return pltpu.CompilerParams(dimension_semantics=sem, vmem_limit_bytes=vmem)


def _resident(shape):
    return pl.BlockSpec(shape, lambda i: (0,) * len(shape), pipeline_mode=pl.Buffered(1))


def _dot(a, b):
    return jnp.dot(a, b, preferred_element_type=f32)


def _dot_nt(a, b):
    return lax.dot_general(a, b, (((1,), (1,)), ((), ())), preferred_element_type=f32)


def _dot_tn(a, b):
    return lax.dot_general(a, b, (((0,), (0,)), ((), ())), preferred_element_type=f32)


def _eye(n):
    return (lax.broadcasted_iota(jnp.int32, (n, n), 0) == lax.broadcasted_iota(jnp.int32, (n, n), 1)).astype(f32)


def _transpose_exact(a):
    return lax.dot_general(a, _eye(a.shape[0]), (((0,), (0,)), ((), ())), precision=HIGHEST, preferred_element_type=f32)


def _sigmoid(x):
    return 1.0 / (1.0 + jnp.exp(-x))


_GK = math.sqrt(2.0 / math.pi)


def _gelu(x):
    return 0.5 * x * (1.0 + jnp.tanh(_GK * (x + 0.044715 * x * x * x)))


def _gelu_grad(x):
    th = jnp.tanh(_GK * (x + 0.044715 * x * x * x))
    return 0.5 * (1.0 + th) + 0.5 * x * (1.0 - th * th) * _GK * (1.0 + 3.0 * 0.044715 * x * x)


ROW_PART = 256


def _row_parts(tm):
    return [slice(r, r + min(ROW_PART, tm)) for r in range(0, tm, min(ROW_PART, tm))]


def _ln_stats(r):
    mu = jnp.mean(r, axis=-1, keepdims=True)
    xc = r - mu
    var = jnp.mean(xc * xc, axis=-1, keepdims=True)
    rstd = lax.rsqrt(var + LN_EPS)
    return xc * rstd, rstd


def _ln_bwd(dy, xhat, rstd, g):
    dxh = dy * g
    m1 = jnp.mean(dxh, axis=-1, keepdims=True)
    m2 = jnp.mean(dxh * xhat, axis=-1, keepdims=True)
    return rstd * (dxh - m1 - xhat * m2)


def _coords():
    return lax.axis_index("x"), lax.axis_index("y"), lax.axis_index("c")


def _when(cond, fn):
    if cond is True:
        fn()
    else:
        pl.when(cond)(fn)


class GatherPlan:
    aliases = ()

    def __init__(self, arrs, srcs=None, into=None):
        n = self.n = len(arrs)
        self.srcs = srcs
        self.inputs = list(arrs) + list(into or [])
        if into:
            self.aliases = tuple((n + a, a) for a in range(n))
        self.out_shape = [S((NDEV,) + a.shape, a.dtype) for a in arrs]
        self.sems = [pltpu.SemaphoreType.DMA((n, 7)), pltpu.SemaphoreType.DMA((n, 7)), pltpu.SemaphoreType.DMA((n,))]

    def _has(self, dev):
        if self.srcs is None:
            return True
        idx = 4 * dev[0] + 2 * dev[1] + dev[2]
        return functools.reduce(jnp.logical_or, [idx == s for s in self.srcs])

    def _parts(self, ins, outs, sems):
        n = self.n
        send_sems, recv_sems, loc_sems = sems
        x, y, c = _coords()
        me, sib = (x, y, c), (x, y, 1 - c)
        chips = [(1 - x, y), (x, 1 - y), (1 - x, 1 - y)]

        def slot(a, dev):
            return outs[a].at[4 * dev[0] + 2 * dev[1] + dev[2]]

        def copy(a, k, block, to, src=None):
            return pltpu.make_async_remote_copy(
                src_ref=slot(a, block) if src is None else src, dst_ref=slot(a, block),
                send_sem=send_sems.at[a, k], recv_sem=recv_sems.at[a, k], device_id=to, device_id_type=MESH)

        each = [(j, chip, a) for j, chip in enumerate(chips) for a in range(n)]
        own = self._has(me)
        return dict(
            mine=lambda: [(pltpu.make_async_copy(ins[a], slot(a, me), loc_sems.at[a]), own) for a in range(n)],
            first=lambda: ([(copy(a, 0, me, sib, src=ins[a]), own) for a in range(n)]
                           + [(copy(a, 1 + j, me, (*chip, c), src=ins[a]), own) for j, chip, a in each]),
            landed=lambda: [(copy(a, 1 + j, (*chip, c), me), self._has((*chip, c))) for j, chip, a in each],
            passed=lambda: [(copy(a, 4 + j, (*chip, c), sib), self._has((*chip, c))) for j, chip, a in each],
            from_sib=lambda: ([(copy(a, 0, sib, me), self._has(sib)) for a in range(n)]
                              + [(copy(a, 4 + j, (*chip, 1 - c), me), self._has((*chip, 1 - c))) for j, chip, a in each]))

    def start(self, ins, outs, sems):
        p = self._parts(ins, outs, sems)
        for cp, cond in p["mine"]() + p["first"]():
            _when(cond, cp.start)

    def forward(self, ins, outs, sems):
        p = self._parts(ins, outs, sems)
        for (got, cond), (fwd, _) in zip(p["landed"](), p["passed"]()):
            def relay(got=got, fwd=fwd):
                got.wait_recv()
                fwd.start()

            _when(cond, relay)

    def finish(self, ins, outs, sems):
        p = self._parts(ins, outs, sems)
        for cp, cond in p["from_sib"]():
            _when(cond, cp.wait_recv)
        for cp, cond in p["first"]() + p["passed"]():
            _when(cond, cp.wait_send)
        for cp, cond in p["mine"]():
            _when(cond, cp.wait)


class ScatterPlan:
    aliases = ()

    def __init__(self, gs, only=None, into=None):
        n = self.n = len(gs)
        self.only = only
        self.inputs = list(gs) + list(into or [])
        if into:
            self.aliases = tuple((n + a, a) for a in range(n))
        self.out_shape = [S(g.shape, g.dtype) for g in gs]
        self.sems = [pltpu.SemaphoreType.DMA((n, 7)), pltpu.SemaphoreType.DMA((n, 7)), pltpu.SemaphoreType.DMA((n,))]

    def _owner(self, idx):
        if self.only is None:
            return True
        return functools.reduce(jnp.logical_or, [idx == b for b in self.only])

    def _copies(self, ins, outs, sems):
        n = self.n
        send_sems, recv_sems, loc_sems = sems
        x, y, c = _coords()
        me = 4 * x + 2 * y + c
        mine = self._owner(me)
        copies = [(pltpu.make_async_copy(ins[a].at[me], outs[a].at[me], loc_sems.at[a]), mine, None) for a in range(n)]
        for m in range(1, NDEV):
            px = 1 - x if m & 4 else x
            py = 1 - y if m & 2 else y
            pc = 1 - c if m & 1 else c
            peer = 4 * px + 2 * py + pc
            for a in range(n):
                copies.append((pltpu.make_async_remote_copy(
                    src_ref=ins[a].at[peer], dst_ref=outs[a].at[me],
                    send_sem=send_sems.at[a, m - 1], recv_sem=recv_sems.at[a, m - 1],
                    device_id=(px, py, pc), device_id_type=MESH), self._owner(peer), mine))
        return copies

    def start(self, ins, outs, sems):
        for cp, sends, _ in self._copies(ins, outs, sems):
            _when(sends, cp.start)

    def forward(self, ins, outs, sems):
        pass

    def finish(self, ins, outs, sems):
        for cp, sends, receives in self._copies(ins, outs, sems):
            if receives is None:
                _when(sends, cp.wait)
            else:
                _when(sends, cp.wait_send)
                _when(receives, cp.wait_recv)


class Plans:
    def __init__(self, plans):
        self.plans = plans
        self.inputs = [a for p in plans for a in p.inputs]
        self.out_shape = [s for p in plans for s in p.out_shape]
        self.sems = [s for p in plans for s in p.sems]
        self.aliases, i, o = [], 0, 0
        for p in plans:
            self.aliases += [(i + a, o + b) for a, b in p.aliases]
            i, o = i + len(p.inputs), o + len(p.out_shape)

    def _each(self, what, ins, outs, sems):
        i = o = s = 0
        for p in self.plans:
            ni, no, ns = len(p.inputs), len(p.out_shape), len(p.sems)
            getattr(p, what)(ins[i:i + ni], outs[o:o + no], sems[s:s + ns])
            i, o, s = i + ni, o + no, s + ns

    def start(self, ins, outs, sems):
        self._each("start", ins, outs, sems)

    def forward(self, ins, outs, sems):
        self._each("forward", ins, outs, sems)

    def finish(self, ins, outs, sems):
        self._each("finish", ins, outs, sems)


def _call(body, args, *, name, grid, in_specs, out_specs, out_shape, scratch=(), sem=None, vmem=None, plan=None,
          aliases=None):
    aliases = aliases or {}
    if plan is None:
        outs = pl.pallas_call(body, name=name, grid=grid, in_specs=list(in_specs), out_specs=list(out_specs),
                              out_shape=list(out_shape), scratch_shapes=list(scratch), input_output_aliases=aliases,
                              compiler_params=_cp(sem, vmem))(*args)
        return list(outs), []
    ni, no, ns = len(in_specs), len(out_specs), len(scratch)
    pi, po = len(plan.inputs), len(plan.out_shape)
    aliases = {**aliases, **{ni + a: no + b for a, b in plan.aliases}}

    def wrapped(*refs):
        main_in, p_in = refs[:ni], refs[ni:ni + pi]
        main_out, p_out = refs[ni + pi:ni + pi + no], refs[ni + pi + no:ni + pi + no + po]
        main_scr, p_sems = refs[ni + pi + no + po:ni + pi + no + po + ns], refs[ni + pi + no + po + ns:]
        ids = [pl.program_id(d) for d in range(len(grid))]
        first = functools.reduce(jnp.logical_and, [i == 0 for i in ids])
        last = functools.reduce(jnp.logical_and, [i == g - 1 for i, g in zip(ids, grid)])

        @pl.when(first)
        def _():
            plan.start(p_in, p_out, p_sems)

        @pl.when(last)
        def _():
            plan.forward(p_in, p_out, p_sems)

        body(*main_in, *main_out, *main_scr)

        @pl.when(last)
        def _():
            plan.finish(p_in, p_out, p_sems)

    outs = pl.pallas_call(
        wrapped, name=name, grid=grid, in_specs=list(in_specs) + [ANY] * pi, out_specs=list(out_specs) + [ANY] * po,
        out_shape=list(out_shape) + list(plan.out_shape), scratch_shapes=list(scratch) + list(plan.sems),
        input_output_aliases=aliases, compiler_params=_cp(("arbitrary",) * len(grid), vmem),
    )(*args, *plan.inputs)
    return list(outs[:no]), list(outs[no:])


def run_plan(plan, name):
    def body(*refs):
        ins, outs, sems = refs[:len(plan.inputs)], refs[len(plan.inputs):len(plan.inputs) + len(plan.out_shape)], \
            refs[len(plan.inputs) + len(plan.out_shape):]
        plan.start(ins, outs, sems)
        plan.forward(ins, outs, sems)
        plan.finish(ins, outs, sems)

    return pl.pallas_call(body, name=name, in_specs=[ANY] * len(plan.inputs), out_specs=[ANY] * len(plan.out_shape),
                          out_shape=list(plan.out_shape), scratch_shapes=list(plan.sems))(*plan.inputs)


def mm_tn(a, b, name, tn=512, into=None, block0=0, nblocks=None):
    T, K = a.shape
    N = b.shape[1]
    tn = min(tn, N)
    nblocks = nblocks or (N // tn if into is None else into.shape[0])

    def body(a_ref, b_ref, *rest):
        rest[-1][...] = _dot_tn(a_ref[...], b_ref[...]).astype(GRAD_DT)

    args, in_specs, aliases = [a, b], [_resident((T, K)), pl.BlockSpec((T, tn), lambda j: (0, j))], {}
    if into is not None:
        args.append(into)
        in_specs.append(ANY)
        aliases = {2: 0}
    (out,), _ = _call(body, args, name=name, grid=(N // tn,), in_specs=in_specs,
                      out_specs=[pl.BlockSpec((None, K, tn), lambda j: (block0 + j, 0, 0))],
                      out_shape=[S((nblocks, K, tn), GRAD_DT)], sem=("parallel",), vmem=VMEM_LIMIT, aliases=aliases)
    return out


def mm_tn_rows(a, b, name, tk=256, plan=None):
    T, K = a.shape
    N = b.shape[1]
    tk = min(tk, K)

    def body(a_ref, b_ref, o_ref):
        o_ref[...] = _dot_tn(a_ref[...], b_ref[...]).astype(GRAD_DT)

    (out,), sent = _call(body, [a, b], name=name, grid=(K // tk,),
                         in_specs=[pl.BlockSpec((T, tk), lambda i: (0, i)), _resident((T, N))],
                         out_specs=[pl.BlockSpec((tk, N), lambda i: (i, 0))], out_shape=[S((K, N), GRAD_DT)],
                         sem=("parallel",), vmem=VMEM_LIMIT, plan=plan)
    return out, sent


def prep_weights(ws):
    def body(*refs):
        for i in range(len(ws)):
            refs[len(ws) + i][...] = refs[i][...].astype(bf16)

    return pl.pallas_call(body, name="prep_weights", out_shape=[S(w.shape, bf16) for w in ws],
                          compiler_params=_cp(None, VMEM_LIMIT))(*ws)


REST_BLOCKS = (4, 5, 6, 7, 1, 2, 3)
REST_COLS = len(REST_BLOCKS) * W


def in_proj_u(x, win_g, b_in):
    T = x.shape[0]
    tm = min(1024, T)

    def body(x_ref, w_ref, b_ref, u_ref, xb_ref):
        xb = x_ref[...].astype(bf16)
        xb_ref[...] = xb
        u_ref[...] = _dot(xb, w_ref[...]) + b_ref[...]

    row = pl.BlockSpec((tm, D), lambda i: (i, 0))
    return pl.pallas_call(
        body, name="in_proj_u", grid=(T // tm,),
        in_specs=[row, pl.BlockSpec((None, D, W), lambda i: (0, 0, 0)), pl.BlockSpec((1, W), lambda i: (0, 0))],
        out_specs=[pl.BlockSpec((tm, W), lambda i: (i, 0)), row],
        out_shape=[S((T, W), f32), S((T, D), bf16)], compiler_params=_cp(("parallel",), VMEM_LIMIT),
    )(x, win_g, b_in)


def in_proj_rest(xb, win_g, b_in, plan):
    T = xb.shape[0]
    tm = min(512, T)

    def body(x_ref, w_ref, b_ref, o_ref):
        xb_ = x_ref[...]
        for i, k in enumerate(REST_BLOCKS):
            o_ref[:, i * W:(i + 1) * W] = _dot(xb_, w_ref[k]) + b_ref[:, k * W:(k + 1) * W]

    return _call(
        body, [xb, win_g, b_in], name="in_proj_rest", grid=(T // tm,),
        in_specs=[pl.BlockSpec((tm, D), lambda i: (i, 0)), _resident((NDEV, D, W)), _resident((1, IN_COLS))],
        out_specs=[pl.BlockSpec((tm, REST_COLS), lambda i: (i, 0))],
        out_shape=[S((T, REST_COLS), f32)], vmem=VMEM_LIMIT, plan=plan)


def to_perm(a, cb0, name):
    T = a.shape[0]
    L = T // NC

    def body(a_ref, o_ref):
        def step(jb, carry):
            j0 = pl.multiple_of(jb * 8, 8)
            for q in range(NC // 8):
                x = jnp.stack([a_ref[pl.ds((8 * q + c) * L + j0, 8), :] for c in range(8)], axis=0)
                y = jnp.swapaxes(x, 0, 1)
                for j in range(8):
                    o_ref[pl.ds((j0 + j) * NC + 8 * q, 8), :] = y[j]
            return carry

        lax.fori_loop(0, L // 8, step, 0)

    return pl.pallas_call(
        body, name=name, grid=(W // LANE,),
        in_specs=[pl.BlockSpec((T, LANE), lambda k: (0, cb0 + k))], out_specs=pl.BlockSpec((T, LANE), lambda k: (0, k)),
        out_shape=S((T, W), f32), compiler_params=_cp(("parallel",), VMEM_LIMIT),
    )(a)


def from_perm(a, name, out_dtype=f32, plan=None):
    T = a.shape[0]
    L = T // NC

    def body(a_ref, o_ref):
        def step(jb, carry):
            j0 = pl.multiple_of(jb * 16, 16)
            for q in range(NC // 8):
                halves = []
                for h in range(2):
                    x = jnp.stack([a_ref[pl.ds((j0 + 8 * h + j) * NC + 8 * q, 8), :] for j in range(8)], axis=0)
                    halves.append(jnp.swapaxes(x, 0, 1))
                for c in range(8):
                    o_ref[pl.ds((8 * q + c) * L + j0, 16), :] = jnp.concatenate(
                        [halves[0][c], halves[1][c]], axis=0).astype(out_dtype)
            return carry

        lax.fori_loop(0, L // 16, step, 0)

    slab = pl.BlockSpec((T, LANE), lambda k: (0, k))
    return _call(body, [a], name=name, grid=(W // LANE,), in_specs=[slab], out_specs=[slab],
                 out_shape=[S((T, W), out_dtype)], sem=("parallel",), vmem=VMEM_LIMIT, plan=plan)


def _disc(lr, li, ldt):
    dt = jnp.exp(ldt)
    mag = jnp.exp(lr * dt)
    lbr = mag * jnp.cos(li * dt)
    lbi = mag * jnp.sin(li * dt)
    den = lr * lr + li * li
    nr = lbr - 1.0
    return lbr, lbi, (nr * lr + lbi * li) / den, (lbi * lr - nr * li) / den


def _per_channel(f):
    return jnp.broadcast_to(f[:, None, :], (NG, GC, NP)).reshape(NG * GC, NP)


def ssm_params(lam_re, lam_im, log_dt, br, bi):
    def body(lr_ref, li_ref, ldt_ref, br_ref, bi_ref, lbr_ref, lbi_ref, fr_ref, fi_ref, bbr_ref, bbi_ref):
        lbr, lbi, fr, fi = _disc(lr_ref[...], li_ref[...], ldt_ref[...])
        lbr_ref[...], lbi_ref[...], fr_ref[...], fi_ref[...] = lbr, lbi, fr, fi
        fr_, fi_, br_, bi_ = _per_channel(fr), _per_channel(fi), br_ref[...], bi_ref[...]
        bbr_ref[...] = fr_ * br_ - fi_ * bi_
        bbi_ref[...] = fr_ * bi_ + fi_ * br_

    return pl.pallas_call(body, name="ssm_params", out_shape=[S((NG, NP), f32)] * 4 + [S((NG * GC, NP), f32)] * 2)(
        lam_re, lam_im, log_dt, br, bi)


SCAN_UNROLL = 4


def _steps(n, body, carry):
    main = n // SCAN_UNROLL

    def trip(t, c):
        for q in range(SCAN_UNROLL):
            c = body(t * SCAN_UNROLL + q, c)
        return c

    carry = lax.fori_loop(0, main, trip, carry)
    for i in range(main * SCAN_UNROLL, n):
        carry = body(i, carry)
    return carry


def _scan_body(T):
    L = T // NC
    RB = min(512, T)
    nsq = int(round(math.log2(L)))
    assert 2 ** nsq == L and T % RB == 0 and L % 16 == 0

    def rows(i):
        return pl.ds(pl.multiple_of(i * RB, RB), RB)

    def tile(j):
        return pl.ds(j * NC if isinstance(j, int) else pl.multiple_of(j * NC, NC), NC)

    def forward_states(u_ref, wb_ref, lbr_ref, lbi_ref, sre, sim, ere, eim):
        def bproj(i, carry):
            bu = _dot(u_ref[rows(i), :].astype(bf16), wb_ref[...])
            sre[rows(i), :] = bu[:, :SW]
            sim[rows(i), :] = bu[:, SW:]
            return carry

        lax.fori_loop(0, T // RB, bproj, 0)
        for lb in range(SW // LANE):
            ls = slice(lb * LANE, (lb + 1) * LANE)
            ar = jnp.broadcast_to(lbr_ref[:, ls], (NC, LANE))
            ai = jnp.broadcast_to(lbi_ref[:, ls], (NC, LANE))

            def step(j, carry):
                xr, xi = carry
                nr = ar * xr - ai * xi + sre[tile(j), ls]
                ni = ar * xi + ai * xr + sim[tile(j), ls]
                sre[tile(j), ls] = nr
                sim[tile(j), ls] = ni
                return nr, ni

            zero = jnp.zeros((NC, LANE), f32)
            _steps(L, step, (zero, zero))
            pr, pi = lbr_ref[:, ls], lbi_ref[:, ls]
            for _ in range(nsq):
                pr, pi = pr * pr - pi * pi, 2.0 * pr * pi
            er = jnp.zeros((1, LANE), f32)
            ei = er
            ere[0:1, ls] = er
            eim[0:1, ls] = ei
            base = (L - 1) * NC
            for c in range(1, NC):
                lr_ = sre[base + c - 1:base + c, ls]
                li_ = sim[base + c - 1:base + c, ls]
                er, ei = lr_ + pr * er - pi * ei, li_ + pr * ei + pi * er
                ere[c:c + 1, ls] = er
                eim[c:c + 1, ls] = ei
            e_r, e_i = ere[:, ls].reshape(NC // 8, 8, LANE), eim[:, ls].reshape(NC // 8, 8, LANE)
            ar8, ai8 = ar[0:8], ai[0:8]

            def fix(j, carry):
                pwr, pwi = carry
                xr = sre[tile(j), ls].reshape(NC // 8, 8, LANE) + (pwr * e_r - pwi * e_i)
                xi = sim[tile(j), ls].reshape(NC // 8, 8, LANE) + (pwr * e_i + pwi * e_r)
                sre[tile(j), ls] = xr.reshape(NC, LANE)
                sim[tile(j), ls] = xi.reshape(NC, LANE)
                return pwr * ar8 - pwi * ai8, pwr * ai8 + pwi * ar8

            _steps(L, fix, (ar8, ai8))

    return L, RB, nsq, rows, tile, forward_states


def ssm_fwd(u_p, wb, wc, lbr, lbi, dsk, plan):
    T = u_p.shape[0]
    L, RB, nsq, rows, tile, forward_states = _scan_body(T)

    def body(u_ref, wb_ref, wc_ref, lbr_ref, lbi_ref, d_ref, y_ref, sre, sim, ere, eim):
        forward_states(u_ref, wb_ref, lbr_ref, lbi_ref, sre, sim, ere, eim)

        def cproj(i, carry):
            y = _dot(sre[rows(i), :].astype(bf16), wc_ref[0:SW, :]) + _dot(sim[rows(i), :].astype(bf16), wc_ref[SW:, :])
            y_ref[rows(i), :] = y + d_ref[...] * u_ref[rows(i), :]
            return carry

        lax.fori_loop(0, T // RB, cproj, 0)

    slab = pl.BlockSpec((T, LANE), lambda k: (0, k))
    return _call(
        body, [u_p, wb, wc, lbr, lbi, dsk], name="ssm_fwd", grid=(W // LANE,),
        in_specs=[slab, pl.BlockSpec((None, LANE, 2 * SW), lambda k: (k, 0, 0)),
                  pl.BlockSpec((None, 2 * SW, LANE), lambda k: (k, 0, 0)),
                  pl.BlockSpec((None, 1, SW), lambda k: (k, 0, 0)), pl.BlockSpec((None, 1, SW), lambda k: (k, 0, 0)),
                  pl.BlockSpec((None, 1, LANE), lambda k: (k, 0, 0))],
        out_specs=[slab], out_shape=[S((T, W), f32)],
        scratch=[pltpu.VMEM((T, SW), f32), pltpu.VMEM((T, SW), f32), pltpu.VMEM((NC, SW), f32), pltpu.VMEM((NC, SW), f32)],
        vmem=VMEM_LIMIT, plan=plan)


def ssm_bwd(u_p, dy_p, wb, wbT, wcT, lbr, lbi, dsk, plan):
    T = u_p.shape[0]
    L, RB, nsq, rows, tile, forward_states = _scan_body(T)

    def body(u_ref, dy_ref, wb_ref, wbT_ref, wcT_ref, lbr_ref, lbi_ref, d_ref,
             du_ref, dwb_ref, dwc_ref, dlr_ref, dli_ref, dd_ref, su_ref, sre, sim, gre, gim, ere, eim):
        forward_states(u_ref, wb_ref, lbr_ref, lbi_ref, sre, sim, ere, eim)

        def dstate(i, carry):
            g = _dot(dy_ref[rows(i), :].astype(bf16), wcT_ref[...])
            gre[rows(i), :] = g[:, :SW]
            gim[rows(i), :] = g[:, SW:]
            return carry

        lax.fori_loop(0, T // RB, dstate, 0)
        row = lax.broadcasted_iota(jnp.int32, (NC, LANE), 0)
        for lb in range(SW // LANE):
            ls = slice(lb * LANE, (lb + 1) * LANE)
            ar = jnp.broadcast_to(lbr_ref[:, ls], (NC, LANE))
            ai = jnp.broadcast_to(lbi_ref[:, ls], (NC, LANE))

            def step(i, carry):
                gr, gi = carry
                j = L - 1 - i
                nr = ar * gr + ai * gi + gre[tile(j), ls]
                ni = ar * gi - ai * gr + gim[tile(j), ls]
                gre[tile(j), ls] = nr
                gim[tile(j), ls] = ni
                return nr, ni

            zero = jnp.zeros((NC, LANE), f32)
            _steps(L, step, (zero, zero))
            pr, pi = lbr_ref[:, ls], -lbi_ref[:, ls]
            for _ in range(nsq):
                pr, pi = pr * pr - pi * pi, 2.0 * pr * pi
            er = jnp.zeros((1, LANE), f32)
            ei = er
            ere[NC - 1:NC, ls] = er
            eim[NC - 1:NC, ls] = ei
            for c in range(NC - 2, -1, -1):
                lr_ = gre[c + 1:c + 2, ls]
                li_ = gim[c + 1:c + 2, ls]
                er, ei = lr_ + pr * er - pi * ei, li_ + pr * ei + pi * er
                ere[c:c + 1, ls] = er
                eim[c:c + 1, ls] = ei
            e_r, e_i = ere[:, ls].reshape(NC // 8, 8, LANE), eim[:, ls].reshape(NC // 8, 8, LANE)
            ar8, ai8 = ar[0:8], ai[0:8]

            def fixed(j, pwr, pwi):
                gr = (gre[tile(j), ls].reshape(NC // 8, 8, LANE) + (pwr * e_r - pwi * e_i)).reshape(NC, LANE)
                gi = (gim[tile(j), ls].reshape(NC // 8, 8, LANE) + (pwr * e_i + pwi * e_r)).reshape(NC, LANE)
                gre[tile(j), ls] = gr
                gim[tile(j), ls] = gi
                return gr, gi

            def fix(i, carry):
                pwr, pwi, accr, acci = carry
                j = L - 1 - i
                gr, gi = fixed(j, pwr, pwi)
                xr, xi = sre[tile(j - 1), ls], sim[tile(j - 1), ls]
                return (pwr * ar8 + pwi * ai8, pwi * ar8 - pwr * ai8,
                        accr + gr * xr + gi * xi, acci + gi * xr - gr * xi)

            pwr, pwi, accr, acci = _steps(L - 1, fix, (ar8, -ai8, zero, zero))
            gr, gi = fixed(0, pwr, pwi)
            xr = jnp.where(row == 0, 0.0, pltpu.roll(sre[tile(L - 1), ls], 1, axis=0))
            xi = jnp.where(row == 0, 0.0, pltpu.roll(sim[tile(L - 1), ls], 1, axis=0))
            accr = accr + gr * xr + gi * xi
            acci = acci + gi * xr - gr * xi
            dlr_ref[:, ls] = jnp.sum(accr, axis=0, keepdims=True)
            dli_ref[:, ls] = jnp.sum(acci, axis=0, keepdims=True)

        dwb_ref[...] = jnp.zeros_like(dwb_ref)
        dwc_ref[...] = jnp.zeros_like(dwc_ref)
        dd_ref[...] = jnp.zeros_like(dd_ref)
        su_ref[...] = jnp.zeros_like(su_ref)

        def finish(i, carry):
            u32, dy32 = u_ref[rows(i), :], dy_ref[rows(i), :]
            ub, dyb = u32.astype(bf16), dy32.astype(bf16)
            gr, gi = gre[rows(i), :].astype(bf16), gim[rows(i), :].astype(bf16)
            du = _dot(gr, wbT_ref[0:SW, :]) + _dot(gi, wbT_ref[SW:, :]) + dy32 * d_ref[...]
            du_ref[rows(i), :] = du
            su_ref[...] += jnp.sum(du, axis=0, keepdims=True)
            dwb_ref[:, 0:SW] += _dot_tn(ub, gr)
            dwb_ref[:, SW:] += _dot_tn(ub, gi)
            dwc_ref[:, 0:SW] += _dot_tn(dyb, sre[rows(i), :].astype(bf16))
            dwc_ref[:, SW:] += _dot_tn(dyb, sim[rows(i), :].astype(bf16))
            dd_ref[...] += jnp.sum(dy32 * u32, axis=0, keepdims=True)
            return carry

        lax.fori_loop(0, T // RB, finish, 0)

    slab = pl.BlockSpec((T, LANE), lambda k: (0, k))
    wide = pl.BlockSpec((None, LANE, 2 * SW), lambda k: (k, 0, 0))
    tall = pl.BlockSpec((None, 2 * SW, LANE), lambda k: (k, 0, 0))
    vec = pl.BlockSpec((None, 1, SW), lambda k: (k, 0, 0))
    vecd = pl.BlockSpec((None, 1, LANE), lambda k: (k, 0, 0))
    nslab = W // LANE
    return _call(
        body, [u_p, dy_p, wb, wbT, wcT, lbr, lbi, dsk], name="ssm_bwd", grid=(nslab,),
        in_specs=[slab, slab, wide, tall, wide, vec, vec, vecd],
        out_specs=[slab, wide, wide, vec, vec, vecd, vecd],
        out_shape=[S((T, W), f32), S((nslab, LANE, 2 * SW), f32), S((nslab, LANE, 2 * SW), f32),
                   S((nslab, 1, SW), f32), S((nslab, 1, SW), f32), S((nslab, 1, LANE), f32), S((nslab, 1, LANE), f32)],
        scratch=[pltpu.VMEM((T, SW), f32)] * 4 + [pltpu.VMEM((NC, SW), f32)] * 2, vmem=VMEM_LIMIT, plan=plan)


def glu_fwd(yn, glu_w, glu_b):
    T = yn.shape[0]
    tm = min(512, T)

    def body(y_ref, w_ref, b_ref, o_ref):
        g = _gelu(y_ref[...])
        o_ref[...] = (g * _sigmoid(_dot(g.astype(bf16), w_ref[...]) + b_ref[...])).astype(bf16)

    return pl.pallas_call(
        body, name="glu_fwd", grid=(T // tm,),
        in_specs=[pl.BlockSpec((tm, W), lambda i: (i, 0)), pl.BlockSpec((W, W), lambda i: (0, 0)), pl.BlockSpec((1, W), lambda i: (0, 0))],
        out_specs=pl.BlockSpec((tm, W), lambda i: (i, 0)), out_shape=S((T, W), bf16), compiler_params=_cp(("parallel",)),
    )(yn, glu_w, glu_b)


def _shift_rows(cur, prev8, k):
    return pltpu.roll(jnp.concatenate([prev8, cur], axis=0), k, axis=0)[8:]


def _lift_rows(cur, next8, k):
    n = cur.shape[0]
    return pltpu.roll(jnp.concatenate([cur, next8], axis=0), n + 8 - k, axis=0)[:n]


def conv_fwd(proj, conv_w):
    T = proj.shape[0]
    RB = min(512, T)

    def body(h_ref, c_ref, b_ref, w_ref, o_ref):
        w0, w1, w2 = w_ref[0:1, :], w_ref[1:2, :], w_ref[2:3, :]

        def blk(i, carry):
            r0 = pl.multiple_of(i * RB, RB)
            rs = pl.ds(r0, RB)
            ch = c_ref[rs, :] * h_ref[rs, :]
            pr = pl.ds(jnp.maximum(r0 - 8, 0), 8)
            prev = jnp.where(i > 0, c_ref[pr, :] * h_ref[pr, :], 0.0)
            z = w2 * ch + w1 * _shift_rows(ch, prev, 1) + w0 * _shift_rows(ch, prev, 2)
            o_ref[rs, :] = (b_ref[rs, :] * z).astype(bf16)
            return carry

        lax.fori_loop(0, T // RB, blk, 0)

    nb = W // LANE
    return pl.pallas_call(
        body, name="conv_fwd", grid=(nb,),
        in_specs=[pl.BlockSpec((T, LANE), lambda k: (0, 4 * nb + k)), pl.BlockSpec((T, LANE), lambda k: (0, 5 * nb + k)),
                  pl.BlockSpec((T, LANE), lambda k: (0, 6 * nb + k)),pl.BlockSpec((3, LANE), lambda k: (0, k))],
        out_specs=pl.BlockSpec((T, LANE), lambda k: (0, k)), out_shape=S((T, W), bf16),
        compiler_params=_cp(("parallel",), VMEM_LIMIT),
    )(proj, proj, proj, conv_w)


def _dense_columns(blocks_ref, dense_ref):
    for k in range(NDEV):
        dense_ref[:, k * LANE:(k + 1) * LANE] = blocks_ref[k]


def merge_fwd(ya, yb, wso, wco, proj, plan):
    T = ya.shape[0]
    tm = min(1024, T)

    def body(ya_ref, yb_ref, wa_ref, wb_ref, ga_ref, gb_ref, o_ref, wa_s, wb_s):
        @pl.when(pl.program_id(0) == 0)
        def _():
            _dense_columns(wa_ref, wa_s)
            _dense_columns(wb_ref, wb_s)

        o_ref[...] = (_sigmoid(ga_ref[...]) * _dot(ya_ref[...], wa_s[...])
                      + _sigmoid(gb_ref[...]) * _dot(yb_ref[...], wb_s[...])).astype(bf16)

    act = pl.BlockSpec((tm, W), lambda i: (i, 0))
    return _call(
        body, [ya, yb, wso, wco, proj, proj], name="merge_fwd", grid=(T // tm,),
        in_specs=[act, act, _resident((NDEV, W, LANE)), _resident((NDEV, W, LANE)),
                  pl.BlockSpec((tm, D), lambda i: (i, 0)), pl.BlockSpec((tm, D), lambda i: (i, 1))],
        out_specs=[pl.BlockSpec((tm, D), lambda i: (i, 0))], out_shape=[S((T, D), bf16)],
        scratch=[pltpu.VMEM((W, D), bf16), pltpu.VMEM((W, D), bf16)], vmem=VMEM_LIMIT, plan=plan)


def mix_ln1(merged, w_o, x, g1, b1):
    T = x.shape[0]
    tm = min(512, T)

    def body(m_ref, w_ref, x_ref, g_ref, b_ref, r_ref, x1_ref):
        for rs in _row_parts(tm):
            r = ALPHA * x_ref[rs, :] + _dot(m_ref[rs, :], w_ref[...])
            r_ref[rs, :] = r
            xhat, _ = _ln_stats(r)
            x1_ref[rs, :] = (xhat * g_ref[...] + b_ref[...]).astype(bf16)

    row = pl.BlockSpec((tm, D), lambda i: (i, 0))
    vec = pl.BlockSpec((1, D), lambda i: (0, 0))
    return pl.pallas_call(
        body, name="mix_ln1", grid=(T // tm,),
        in_specs=[row, _resident((D, D)), row, vec, vec],
        out_specs=[row, row], out_shape=[S((T, D), f32), S((T, D), bf16)], compiler_params=_cp(("parallel",), VMEM_LIMIT),
    )(merged, w_o, x, g1, b1)


FT = 256


def gate_up(x1b, wgT, wuT, plan):
    T = x1b.shape[0]
    tm = min(512, T)

    def body(x_ref, wg_ref, wu_ref, g_ref, u_ref, h_ref):
        x = x_ref[...]
        for n in range(F // FT):
            cs = slice(n * FT, (n + 1) * FT)
            g = _dot_nt(x, wg_ref[cs, :])
            u = _dot_nt(x, wu_ref[cs, :])
            g_ref[:, cs] = g.astype(bf16)
            u_ref[:, cs] = u.astype(bf16)
            h_ref[:, cs] = (g * _sigmoid(g) * u).astype(bf16)

    osp = pl.BlockSpec((tm, F), lambda i: (i, 0))
    return _call(
        body, [x1b, wgT, wuT], name="gate_up", grid=(T // tm,),
        in_specs=[pl.BlockSpec((tm, D), lambda i: (i, 0)), _resident((F, D)), _resident((F, D))],
        out_specs=[osp, osp, osp], out_shape=[S((T, F), bf16)] * 3, vmem=VMEM_LIMIT, plan=plan)


def down_loss(hid, w_down, r1, g1, b1, g2, b2, target):
    T = hid.shape[0]
    tm = min(512, T)

    def body(h_ref, w_ref, r1_ref, g1_ref, b1_ref, g2_ref, b2_ref, t_ref, dr_ref, drb_ref, loss_ref, dg_ref, db_ref):
        @pl.when(pl.program_id(0) == 0)
        def _():
            loss_ref[...] = jnp.zeros_like(loss_ref)
            dg_ref[...] = jnp.zeros_like(dg_ref)
            db_ref[...] = jnp.zeros_like(db_ref)

        for rs in _row_parts(tm):
            xh1, _ = _ln_stats(r1_ref[rs, :])
            x1 = xh1 * g1_ref[...] + b1_ref[...]
            r2 = ALPHA * x1 + _dot(h_ref[rs, :], w_ref[...])
            xh2, rstd2 = _ln_stats(r2)
            err = xh2 * g2_ref[...] + b2_ref[...] - t_ref[rs, :]
            loss_ref[...] += jnp.sum(jnp.mean(err * err, axis=-1, keepdims=True), axis=0, keepdims=True)
            dy = err * (1.0 / D)
            dg_ref[...] += jnp.sum(dy * xh2, axis=0, keepdims=True)
            db_ref[...] += jnp.sum(dy, axis=0, keepdims=True)
            dr = _ln_bwd(dy, xh2, rstd2, g2_ref[...])
            dr_ref[rs, :] = dr
            drb_ref[rs, :] = dr.astype(bf16)

    row = pl.BlockSpec((tm, D), lambda i: (i, 0))
    vec = pl.BlockSpec((1, D), lambda i: (0, 0))
    return pl.pallas_call(
        body, name="down_loss", grid=(T // tm,),
        in_specs=[pl.BlockSpec((tm, F), lambda i: (i, 0)), _resident((F, D)), row, vec, vec, vec, vec, row],
        out_specs=[row, row, pl.BlockSpec((1, 1), lambda i: (0, 0)), vec, vec],
        out_shape=[S((T, D), f32), S((T, D), bf16), S((1, 1), f32), S((1, D), f32), S((1, D), f32)],
        compiler_params=_cp(("arbitrary",), VMEM_LIMIT),
    )(hid, w_down, r1, g1, b1, g2, b2, target)


def ffn_bwd_act(dffn, w_down, gate, up):
    T = dffn.shape[0]
    tm = min(512, T)

    def body(d_ref, w_ref, g_ref, u_ref, dg_ref, du_ref):
        d = d_ref[...]
        for n in range(F // FT):
            cs = slice(n * FT, (n + 1) * FT)
            dh = _dot_nt(d, w_ref[cs, :])
            g, u = g_ref[:, cs].astype(f32), u_ref[:, cs].astype(f32)
            sg = _sigmoid(g)
            du_ref[:, cs] = (dh * g * sg).astype(bf16)
            dg_ref[:, cs] = (dh * u * sg * (1.0 + g * (1.0 - sg))).astype(bf16)

    osp = pl.BlockSpec((tm, F), lambda i: (i, 0))
    return pl.pallas_call(
        body, name="ffn_bwd_act", grid=(T // tm,),
        in_specs=[pl.BlockSpec((tm, D), lambda i: (i, 0)), _resident((F, D)), osp, osp],
        out_specs=[osp, osp], out_shape=[S((T, F), bf16)] * 2, compiler_params=_cp(("parallel",), VMEM_LIMIT),
    )(dffn, w_down, gate, up)


def ffn_bwd_x(dgate, dup, wgT, wuT, dr2, r1, g1, plan):
    T = dr2.shape[0]
    tm = min(512, T)

    def body(dg_ref, du_ref, wg_ref, wu_ref, dr2_ref, r1_ref, g1_ref, dr_ref, drb_ref, dgam_ref, dbet_ref):
        @pl.when(pl.program_id(0) == 0)
        def _():
            dgam_ref[...] = jnp.zeros_like(dgam_ref)
            dbet_ref[...] = jnp.zeros_like(dbet_ref)

        for rs in _row_parts(tm):
            dx1 = ALPHA * dr2_ref[rs, :] + _dot(dg_ref[rs, :], wg_ref[...]) + _dot(du_ref[rs, :], wu_ref[...])
            xh, rstd = _ln_stats(r1_ref[rs, :])
            dgam_ref[...] += jnp.sum(dx1 * xh, axis=0, keepdims=True)
            dbet_ref[...] += jnp.sum(dx1, axis=0, keepdims=True)
            dr = _ln_bwd(dx1, xh, rstd, g1_ref[...])
            dr_ref[rs, :] = dr
            drb_ref[rs, :] = dr.astype(bf16)

    row = pl.BlockSpec((tm, D), lambda i: (i, 0))
    wide = pl.BlockSpec((tm, F), lambda i: (i, 0))
    wsp = _resident((F, D))
    vec = pl.BlockSpec((1, D), lambda i: (0, 0))
    return _call(
        body, [dgate, dup, wgT, wuT, dr2, r1, g1], name="ffn_bwd_x", grid=(T // tm,),
        in_specs=[wide, wide, wsp, wsp, row, row, vec],
        out_specs=[row, row, vec, vec], out_shape=[S((T, D), f32), S((T, D), bf16), S((1, D), f32), S((1, D), f32)],
        vmem=VMEM_LIMIT, plan=plan)


def merge_bwd(dmix, w_o, ya, yb, wso, wco, proj, plan):
    T = dmix.shape[0]
    tm = min(512, T)

    def body(dm_ref, wo_ref, ya_ref, yb_ref, wa_ref, wb_ref, ga_ref, gb_ref, dya_ref, dyb_ref, dga_ref, dgb_ref, sa_ref, sb_ref,
             wa_s, wb_s):
        @pl.when(pl.program_id(0) == 0)
        def _():
            _dense_columns(wa_ref, wa_s)
            _dense_columns(wb_ref, wb_s)

        dmer = _dot_nt(dm_ref[...], wo_ref[...])
        sa, sb = _sigmoid(ga_ref[...]), _sigmoid(gb_ref[...])
        dya_ref[...] = (dmer * sa).astype(bf16)
        dyb_ref[...] = (dmer * sb).astype(bf16)
        dga = dmer * _dot(ya_ref[...], wa_s[...]) * sa * (1.0 - sa)
        dgb = dmer * _dot(yb_ref[...], wb_s[...]) * sb * (1.0 - sb)
        dga_ref[...] = dga.astype(bf16)
        dgb_ref[...] = dgb.astype(bf16)
        sa_ref[...] = jnp.sum(dga, axis=0, keepdims=True)
        sb_ref[...] = jnp.sum(dgb, axis=0, keepdims=True)

    act = pl.BlockSpec((tm, W), lambda i: (i, 0))
    osp = pl.BlockSpec((tm, D), lambda i: (i, 0))
    ssp = pl.BlockSpec((None, 1, D), lambda i: (i, 0, 0))
    return _call(
        body, [dmix, w_o, ya, yb, wso, wco, proj, proj], name="merge_bwd", grid=(T // tm,),
        in_specs=[osp, _resident((D, D)), act, act, _resident((NDEV, W, LANE)), _resident((NDEV, W, LANE)),
                  pl.BlockSpec((tm, D), lambda i: (i, 0)), pl.BlockSpec((tm, D), lambda i: (i, 1))],
        out_specs=[osp, osp, osp, osp, ssp, ssp],
        out_shape=[S((T, D), bf16)] * 4 + [S((T // tm, 1, D), f32)] * 2,
        scratch=[pltpu.VMEM((W, D), bf16), pltpu.VMEM((W, D), bf16)], vmem=VMEM_LIMIT, plan=plan)


def branches_bwd_x(dYA, dYB, wso, wco, plan):
    T = dYA.shape[0]
    tm = min(1024, T)

    def body(da_ref, db_ref, wa_ref, wb_ref, oa_ref, ob_ref, wa_s, wb_s):
        @pl.when(pl.program_id(0) == 0)
        def _():
            _dense_columns(wa_ref, wa_s)
            _dense_columns(wb_ref, wb_s)

        oa_ref[...] = _dot_nt(da_ref[...], wa_s[...])
        ob_ref[...] = _dot_nt(db_ref[...], wb_s[...])

    row = pl.BlockSpec((tm, D), lambda i: (i, 0))
    osp = pl.BlockSpec((tm, W), lambda i: (i, 0))
    return _call(
        body, [dYA, dYB, wso, wco], name="branches_bwd_x", grid=(T // tm,),
        in_specs=[row, row, _resident((NDEV, W, LANE)), _resident((NDEV, W, LANE))],
        out_specs=[osp, osp], out_shape=[S((T, W), f32)] * 2,
        scratch=[pltpu.VMEM((W, D), bf16), pltpu.VMEM((W, D), bf16)], vmem=VMEM_LIMIT, plan=plan)


def branch_bwd_w(act, dY, name):
    T = act.shape[0]
    tk = W // 2

    def body(a_ref, d_ref, o_ref):
        res = _dot_tn(a_ref[...], d_ref[...])
        for k in range(NDEV):
            o_ref[k] = res[:, k * LANE:(k + 1) * LANE].astype(o_ref.dtype)

    return pl.pallas_call(
        body, name=name, grid=(W // tk,),
        in_specs=[pl.BlockSpec((T, tk), lambda i: (0, i)), _resident((T, D))],
        out_specs=pl.BlockSpec((NDEV, tk, LANE), lambda i: (0, i, 0)), out_shape=S((NDEV, W, LANE), GRAD_DT),
        compiler_params=_cp(("parallel",), VMEM_LIMIT),
    )(act, dY)


def glu_bwd(yn, dya, glu_w, glu_b):
    T = yn.shape[0]
    tm = min(512, T)

    def body(y_ref, d_ref, w_ref, b_ref, dy_ref, dsp_ref, g_ref, db_ref):
        @pl.when(pl.program_id(0) == 0)
        def _():
            db_ref[...] = jnp.zeros_like(db_ref)

        y, dya_ = y_ref[...], d_ref[...]
        g = _gelu(y)
        gb = g.astype(bf16)
        s = _sigmoid(_dot(gb, w_ref[...]) + b_ref[...])
        dsp = dya_ * g * s * (1.0 - s)
        dspb = dsp.astype(bf16)
        dg = dya_ * s + _dot_nt(dspb, w_ref[...])
        dy_ref[...] = dg * _gelu_grad(y)
        dsp_ref[...] = dspb
        g_ref[...] = gb
        db_ref[...] += jnp.sum(dsp, axis=0, keepdims=True)

    row = pl.BlockSpec((tm, W), lambda i: (i, 0))
    vec = pl.BlockSpec((1, W), lambda i: (0, 0))
    return pl.pallas_call(
        body, name="glu_bwd", grid=(T // tm,),
        in_specs=[row, row, pl.BlockSpec((W, W), lambda i: (0, 0)), vec],
        out_specs=[row, row, row, vec], out_shape=[S((T, W), f32), S((T, W), bf16), S((T, W), bf16), S((1, W), f32)],
        compiler_params=_cp(("arbitrary",)),
    )(yn, dya, glu_w, glu_b)


def conv_bwd(proj, dyb, conv_w):
    T = proj.shape[0]
    RB = min(512, T)
    nrb = T // RB

    def body(h_ref, c_ref, b_ref, d_ref, w_ref, dh_ref, dc_ref, db_ref, dw_ref, s_ref):
        w0, w1, w2 = w_ref[0:1, :], w_ref[1:2, :], w_ref[2:3, :]

        def blk(i, carry):
            a0, a1, a2, sh, sc, sb = carry
            r0 = pl.multiple_of(i * RB, RB)
            rs = pl.ds(r0, RB)
            h, cg, bg, dyb_ = h_ref[rs, :], c_ref[rs, :], b_ref[rs, :], d_ref[rs, :]
            ch = cg * h
            pr = pl.ds(jnp.maximum(r0 - 8, 0), 8)
            prev = jnp.where(i > 0, c_ref[pr, :] * h_ref[pr, :], 0.0)
            ch1, ch2 = _shift_rows(ch, prev, 1), _shift_rows(ch, prev, 2)
            dbg = dyb_ * (w2 * ch + w1 * ch1 + w0 * ch2)
            db_ref[rs, :] = dbg.astype(bf16)
            dz = dyb_ * bg
            nx = pl.ds(jnp.minimum(r0 + RB, T - 8), 8)
            nxt = jnp.where(i < nrb - 1, d_ref[nx, :] * b_ref[nx, :], 0.0)
            dch = w2 * dz + w1 * _lift_rows(dz, nxt, 1) + w0 * _lift_rows(dz, nxt, 2)
            dcg, dh = dch * h, dch * cg
            dc_ref[rs, :] = dcg.astype(bf16)
            dh_ref[rs, :] = dh.astype(bf16)
            col = lambda v: jnp.sum(v, axis=0, keepdims=True)
            return (a0 + col(dz * ch2), a1 + col(dz * ch1), a2 + col(dz * ch), sh + col(dh), sc + col(dcg), sb + col(dbg))

        zero = jnp.zeros((1, LANE), f32)
        a0, a1, a2, sh, sc, sb = lax.fori_loop(0, nrb, blk, (zero,) * 6)
        dw_ref[0:1, :] = a0
        dw_ref[1:2, :] = a1
        dw_ref[2:3, :] = a2
        s_ref[0:1, :] = sh
        s_ref[1:2, :] = sc
        s_ref[2:3, :] = sb

    nb = W // LANE
    slab = pl.BlockSpec((T, LANE), lambda k: (0, k))
    three = pl.BlockSpec((3, LANE), lambda k: (0, k))
    return pl.pallas_call(
        body, name="conv_bwd", grid=(nb,),
        in_specs=[pl.BlockSpec((T, LANE), lambda k: (0, 4 * nb + k)), pl.BlockSpec((T, LANE), lambda k: (0, 5 * nb + k)),
                  pl.BlockSpec((T, LANE), lambda k: (0, 6 * nb + k)),slab, three],
        out_specs=[slab, slab, slab, three, three],
        out_shape=[S((T, W), bf16)] * 3 + [S((3, W), f32)] * 2, compiler_params=_cp(("parallel",), VMEM_LIMIT),
    )(proj, proj, proj, dyb, conv_w)


def in_proj_bwd_x(parts, win_g, base, scale, name, plan=None):
    T = base.shape[0]
    tm = min(512, T)
    n = len(parts)

    def body(*refs):
        p_refs, w_ref, b_ref, o_ref = refs[:n], refs[n], refs[n + 1], refs[n + 2]
        acc = scale * b_ref[...]
        for p_ref, (_, _, k) in zip(p_refs, parts):
            acc += _dot_nt(p_ref[...], w_ref[k])
        o_ref[...] = acc

    row = pl.BlockSpec((tm, D), lambda i: (i, 0))
    p_specs = [pl.BlockSpec((tm, W), (lambda i, cb=cb: (i, cb))) for _, cb, _ in parts]
    return _call(
        body, [a for a, _, _ in parts] + [win_g, base], name=name, grid=(T // tm,),
        in_specs=p_specs + [_resident((NDEV, D, W)), row],
        out_specs=[row], out_shape=[S((T, D), f32)], vmem=VMEM_LIMIT, plan=plan)


def ssm_param_bwd(lam_re, lam_im, log_dt, fr, fi, br, bi, dbbr, dbbi, dlbr, dlbi):
    def body(lr_ref, li_ref, ldt_ref, fr_ref, fi_ref, br_ref, bi_ref, dr_ref, di_ref, dlbr_ref, dlbi_ref,
             dbr_ref, dbi_ref, dlr_ref, dli_ref, dldt_ref):
        fr_, fi_ = _per_channel(fr_ref[...]), _per_channel(fi_ref[...])
        br_, bi_, dr, di = br_ref[...], bi_ref[...], dr_ref[...], di_ref[...]
        dbr_ref[...] = fr_ * dr + fi_ * di
        dbi_ref[...] = fr_ * di - fi_ * dr
        dfr = jnp.sum((dr * br_ + di * bi_).reshape(NG, GC, NP), axis=1)
        dfi = jnp.sum((di * br_ - dr * bi_).reshape(NG, GC, NP), axis=1)
        _, vjp = jax.vjp(_disc, lr_ref[...], li_ref[...], ldt_ref[...])
        dlr_ref[...], dli_ref[...], dldt = vjp((dlbr_ref[...], dlbi_ref[...], dfr, dfi))
        dldt_ref[...] = _transpose_exact(dldt)

    return pl.pallas_call(
        body, name="ssm_param_bwd",
        out_shape=[S((NG * GC, NP), f32)] * 2 + [S((NG, NP), f32)] * 2 + [S((1, NG), f32)])(
        lam_re, lam_im, log_dt, fr, fi, br, bi, dbbr, dbbi, dlbr, dlbi)


def _adam(w, g, m, v):
    m = ADAM_B1 * m + (1.0 - ADAM_B1) * g
    v = ADAM_B2 * v + (1.0 - ADAM_B2) * (g * g)
    m_hat = m / (1.0 - ADAM_B1 ** ADAM_STEP)
    v_hat = v / (1.0 - ADAM_B2 ** ADAM_STEP)
    return -ADAM_LR * (m_hat / (jnp.sqrt(v_hat) + ADAM_EPS) + ADAM_WD * w), m, v


def adam_update(w, m, v, contrib, name, rows_per_block=None):
    R, C = w.shape
    n = contrib.shape[0]
    tr = min(rows_per_block or R, R)

    def body(w_ref, m_ref, v_ref, c_ref, g_ref, d_ref, nm_ref, nv_ref):
        g = c_ref[0].astype(f32)
        for k in range(1, n):
            g = g + c_ref[k].astype(f32)
        g_ref[...] = g
        d_ref[...], nm_ref[...], nv_ref[...] = _adam(w_ref[...], g, m_ref[...], v_ref[...])

    blk = pl.BlockSpec((tr, C), lambda i: (i, 0))
    return pl.pallas_call(
        body, name=name, grid=(R // tr,), in_specs=[blk, blk, blk, pl.BlockSpec((n, tr, C), lambda i: (0, i, 0))],
        out_specs=[blk] * 4, out_shape=[S((R, C), f32)] * 4, compiler_params=_cp(("parallel",), VMEM_LIMIT),
    )(w, m, v, contrib)


_ROWVEC = (("b_in", IN_COLS), ("ssm_d", W), ("glu_b", W), ("ln1_g", D), ("ln1_b", D), ("ln2_g", D), ("ln2_b", D))
_HALF = NG * GC // 2
_BC_LANE = {"ssm_b_re": 0, "ssm_b_im": NP, "ssm_c_re": 0, "ssm_c_im": NP}
_PACK = {}
_r = 0
for _n, _k in _ROWVEC:
    _PACK[_n] = _r
    _r += _k // LANE
for _n, _rows in (("ssm_lambda", NG), ("scalars", 8), ("ssm_b", _HALF), ("ssm_c", _HALF), ("conv_w", 16)):
    _PACK[_n] = _r
    _r += _rows
for _n in _BC_LANE:
    _PACK[_n] = _PACK[_n[:5]]
PACK_ROWS = _r
assert PACK_ROWS % 8 == 0
_SMALL = ("b_in", "ssm_lambda_re", "ssm_lambda_im", "ssm_log_dt", "ssm_b_re", "ssm_b_im", "ssm_c_re", "ssm_c_im",
          "ssm_d", "glu_b", "ln1_g", "ln1_b", "ln2_g", "ln2_b")


def pack_grads(su, shcb, sga, sgb, dd, dglu_b, dln1_g, dln1_b, dln2_g, dln2_b, dlam_re, dlam_im, dldt, sqerr, dbr, dbi,
               dc_re, dc_im, dconv):
    nI = sga.shape[0]

    def body(su_ref, sh_ref, sga_ref, sgb_ref, dd_ref, gb_ref, l1g_ref, l1b_ref, l2g_ref, l2b_ref, lr_ref, li_ref, dt_ref,
             sq_ref, br_ref, bi_ref, cr_ref, ci_ref, cw_ref, o_ref):
        o_ref[...] = jnp.zeros_like(o_ref)

        def put_row(name, v):
            r0 = _PACK[name]
            for i in range(v.shape[1] // LANE):
                o_ref[r0 + i:r0 + i + 1, :] = v[:, i * LANE:(i + 1) * LANE]

        ga, gb = sga_ref[0], sgb_ref[0]
        for i in range(1, nI):
            ga, gb = ga + sga_ref[i], gb + sgb_ref[i]
        put_row("b_in", jnp.concatenate([su_ref[k] for k in range(W // LANE)]
                                        + [sh_ref[0:1, :], sh_ref[1:2, :], sh_ref[2:3, :], ga, gb], axis=1))
        put_row("ssm_d", jnp.concatenate([dd_ref[k] for k in range(W // LANE)], axis=1))
        put_row("glu_b", gb_ref[...])
        put_row("ln1_g", l1g_ref[...])
        put_row("ln1_b", l1b_ref[...])
        put_row("ln2_g", l2g_ref[...])
        put_row("ln2_b", l2b_ref[...])
        r0 = _PACK["ssm_lambda"]
        o_ref[r0:r0 + NG, 0:NP] = lr_ref[...]
        o_ref[r0:r0 + NG, NP:2 * NP] = li_ref[...]
        r0 = _PACK["scalars"]
        o_ref[r0:r0 + 1, 0:NG] = dt_ref[...]
        o_ref[r0 + 1:r0 + 2, 0:1] = sq_ref[...]
        for name, ref in (("ssm_b_re", br_ref), ("ssm_b_im", bi_ref), ("ssm_c_re", cr_ref), ("ssm_c_im", ci_ref)):
            r0, l0 = _PACK[name], _BC_LANE[name]
            o_ref[r0:r0 + _HALF, l0:l0 + NP] = pltpu.bitcast(ref[...].astype(bf16), f32)
        for cb in range(W // LANE):
            o_ref[_PACK["conv_w"] + 3 * cb:_PACK["conv_w"] + 3 * cb + 3, :] = cw_ref[:, cb * LANE:(cb + 1) * LANE]

    return pl.pallas_call(body, name="pack_grads", out_shape=S((PACK_ROWS, LANE), f32))(
        su, shcb, sga, sgb, dd, dglu_b, dln1_g, dln1_b, dln2_g, dln2_b, dlam_re, dlam_im, dldt, sqerr, dbr, dbi, dc_re, dc_im,
        dconv)


def adam_small(packed_all, params):
    names = list(_SMALL) + ["conv_w"]
    flat = [a for n in names for a in params[n]]

    def body(*refs):
        p_ref = refs[0]
        ins = refs[1:1 + 3 * len(names)]
        outs = refs[1 + 3 * len(names):-2]
        loss_ref, g_ref = refs[-2], refs[-1]
        g_all = p_ref[0]
        for k in range(1, NDEV):
            g_all = g_all + p_ref[k]
        g_ref[...] = g_all

        def rows(name, r0, n, l0=0, lanes=LANE):
            return g_ref[_PACK[name] + r0:_PACK[name] + r0 + n, l0:l0 + lanes]

        def grad_of(name):
            if name in dict(_ROWVEC):
                return jnp.concatenate([rows(name, i, 1) for i in range(dict(_ROWVEC)[name] // LANE)], axis=1)
            if name in ("ssm_lambda_re", "ssm_lambda_im"):
                return rows("ssm_lambda", 0, NG, NP * (name == "ssm_lambda_im"), NP)[None]
            if name == "ssm_log_dt":
                return rows("scalars", 0, 1, 0, NG)
            if name in _BC_LANE:
                r0, l0 = _PACK[name], _BC_LANE[name]
                g = pltpu.bitcast(p_ref[0, r0:r0 + _HALF, l0:l0 + NP], bf16).astype(f32)
                for k in range(1, NDEV):
                    g = g + pltpu.bitcast(p_ref[k, r0:r0 + _HALF, l0:l0 + NP], bf16).astype(f32)
                return g.reshape(1, NG, GC, NP)
            full = jnp.concatenate([rows("conv_w", 3 * cb, 3) for cb in range(W // LANE)], axis=1)
            x, y, c = _coords()
            col0 = (4 * x + 2 * y + c) * (W // NDEV)
            sel = (lax.broadcasted_iota(jnp.int32, (W, W // NDEV), 0)
                   == lax.broadcasted_iota(jnp.int32, (W, W // NDEV), 1) + col0).astype(f32)
            return jnp.dot(full, sel, precision=HIGHEST, preferred_element_type=f32)[None]

        loss_ref[...] = 0.5 * rows("scalars", 1, 1, 0, 1)
        for i, name in enumerate(names):
            w_ref, m_ref, v_ref = ins[3 * i:3 * i + 3]
            g = grad_of(name)
            d, m, v = _adam(w_ref[...], g, m_ref[...], v_ref[...])
            outs[4 * i][...] = g
            outs[4 * i + 1][...] = d
            outs[4 * i + 2][...] = m
            outs[4 * i + 3][...] = v

    out_shape = [S(params[n][0].shape, f32) for n in names for _ in range(4)] + [S((1, 1), f32)]
    res = pl.pallas_call(body, name="adam_small", out_shape=out_shape, scratch_shapes=[pltpu.VMEM((PACK_ROWS, LANE), f32)],
                         compiler_params=_cp(None, VMEM_LIMIT))(packed_all, *flat)
    return {n: res[4 * i:4 * i + 4] for i, n in enumerate(names)}, res[-1]


def _block_diag(wgt):
    eye = jnp.eye(8, dtype=wgt.dtype)
    out = wgt[:, :, :, None, :] * eye[None, :, None, :, None]
    return out.reshape(4, 8 * wgt.shape[2], 8 * wgt.shape[3])


def _diag_blocks(m, a, b):
    m = m.reshape(4, 8, a, 8, b)
    idx = jnp.arange(8)
    return m[:, idx, :, idx, :].transpose(1, 0, 2, 3)


def kernel(x, w_in, b_in, ssm_lambda_re, ssm_lambda_im, ssm_log_dt, ssm_b_re, ssm_b_im, ssm_c_re, ssm_c_im, ssm_d, glu_w, glu_b, w_ssm_out, conv_w, w_conv_out, w_o, ln1_g, ln1_b, w_gate, w_up, w_down, ln2_g, ln2_b, loss_target, m_w_in, m_b_in, m_ssm_lambda_re, m_ssm_lambda_im, m_ssm_log_dt, m_ssm_b_re, m_ssm_b_im, m_ssm_c_re, m_ssm_c_im, m_ssm_d, m_glu_w, m_glu_b, m_w_ssm_out, m_conv_w, m_w_conv_out, m_w_o, m_ln1_g, m_ln1_b, m_w_gate, m_w_up, m_w_down, m_ln2_g, m_ln2_b, v_w_in, v_b_in, v_ssm_lambda_re, v_ssm_lambda_im, v_ssm_log_dt, v_ssm_b_re, v_ssm_b_im, v_ssm_c_re, v_ssm_c_im, v_ssm_d, v_glu_w, v_glu_b, v_w_ssm_out, v_conv_w, v_w_conv_out, v_w_o, v_ln1_g, v_ln1_b, v_w_gate, v_w_up, v_w_down, v_ln2_g, v_ln2_b):
    given = dict(locals())
    xs = x[0]
    target = loss_target[0]

    tr = lambda a: jnp.swapaxes(a[0], 0, 1)
    win_s, glu_s, wso_s, wco_s, wo_s, wgT_s, wuT_s, wd_s = prep_weights(
        [w_in[0], glu_w[0], w_ssm_out[0], w_conv_out[0], w_o[0], tr(w_gate), tr(w_up), w_down[0]])
    (win_g,) = run_plan(GatherPlan([win_s], srcs=(0,)), "gather_w_in_u")

    lam_re, lam_im = ssm_lambda_re[0], ssm_lambda_im[0]
    ldt = ssm_log_dt[0].reshape(NG, 1)
    br2 = jnp.swapaxes(ssm_b_re[0], 1, 2).reshape(NG * GC, NP)
    bi2 = jnp.swapaxes(ssm_b_im[0], 1, 2).reshape(NG * GC, NP)
    lbr, lbi, fr, fi, bbr, bbi = ssm_params(lam_re, lam_im, ldt, br2, bi2)
    bb_t = lambda b: b.reshape(4, 8, GC, NP)
    wb = jnp.concatenate([_block_diag(bb_t(bbr)), _block_diag(bb_t(bbi))], axis=2)
    c_t = lambda c: c.reshape(4, 8, GC, NP).transpose(0, 1, 3, 2)
    wc = jnp.concatenate([_block_diag(c_t(ssm_c_re[0])), -_block_diag(c_t(ssm_c_im[0]))], axis=1)
    wbT, wcT = wb.transpose(0, 2, 1), wc.transpose(0, 2, 1)
    wb, wc, wbT, wcT = wb.astype(bf16), wc.astype(bf16), wbT.astype(bf16), wcT.astype(bf16)
    lbr_s, lbi_s = lbr.reshape(4, 1, SW), lbi.reshape(4, 1, SW)
    dsk = ssm_d[0].reshape(4, 1, LANE)

    u_nat, xb = in_proj_u(xs, win_g, b_in)
    u_p = to_perm(u_nat, 0, "perm_u")
    (y_p,), (win_g, conv_g, wgT_g) = ssm_fwd(
        u_p, wb, wc, lbr_s, lbi_s, dsk,
        Plans([GatherPlan([win_s], srcs=tuple(range(1, NDEV)), into=[win_g]), GatherPlan([conv_w[0], wgT_s])]))
    conv_f = conv_g.transpose(1, 0, 2).reshape(3, W)
    (proj,), (glu_g, wso_g, wco_g, wo_g) = in_proj_rest(xb, win_g, b_in, GatherPlan([glu_s, wso_s, wco_s, wo_s]))
    glu_f, wo_f = glu_g.reshape(W, W), wo_g.reshape(D, D)
    (yn,), _ = from_perm(y_p, "unperm_y")
    ya = glu_fwd(yn, glu_f, glu_b)
    yb = conv_fwd(proj, conv_f)
    (merged,), (wuT_g,) = merge_fwd(ya, yb, wso_g, wco_g, proj, GatherPlan([wuT_s]))
    wgT, wuT = wgT_g.reshape(F, D), wuT_g.reshape(F, D)
    r1, x1b = mix_ln1(merged, wo_f, xs, ln1_g, ln1_b)
    (gate, up, hid), (wd_g,) = gate_up(x1b, wgT, wuT, GatherPlan([wd_s]))
    wd_f = wd_g.reshape(F, D)
    dr2, dffn, sqerr, dln2_g, dln2_b = down_loss(hid, wd_f, r1, ln1_g, ln1_b, ln2_g, ln2_b, target)

    half_a, half_b = (0, 3, 5, 6), (1, 2, 4, 7)
    dgate, dup = ffn_bwd_act(dffn, wd_f, gate, up)
    dwd, _ = mm_tn_rows(hid, dffn, "grad_w_down")
    dwd = dwd.reshape(NDEV, FS, D)
    dwgT, (r_wd,) = mm_tn_rows(dgate, x1b, "grad_w_gate", plan=ScatterPlan([dwd], only=half_a))
    dwuT, (r_wd,) = mm_tn_rows(dup, x1b, "grad_w_up", plan=ScatterPlan([dwd], only=half_b, into=[r_wd]))
    dwgT, dwuT = dwgT.reshape(NDEV, FS, D), dwuT.reshape(NDEV, FS, D)
    (dr1, dmix, dln1_g, dln1_b), (r_wgT,) = ffn_bwd_x(dgate, dup, wgT, wuT, dr2, r1, ln1_g, ScatterPlan([dwgT]))
    (dYA, dYB, dga, dgb, sga, sgb), (r_wuT,) = merge_bwd(dmix, wo_f, ya, yb, wso_g, wco_g, proj,
                                                         ScatterPlan([dwuT], only=half_a))
    dwo, _ = mm_tn_rows(merged, dmix, "grad_w_o")
    dwo = dwo.reshape(NDEV, D // NDEV, D)
    (dya, dyb), (r_wuT,) = branches_bwd_x(dYA, dYB, wso_g, wco_g, ScatterPlan([dwuT], only=half_b, into=[r_wuT]))
    dwso = branch_bwd_w(ya, dYA, "grad_w_ssm_out")
    dwco = branch_bwd_w(yb, dYB, "grad_w_conv_out")
    dyn, dsp, gb, dglu_b = glu_bwd(yn, dya, glu_f, glu_b)
    dglu = mm_tn_rows(gb, dsp, "grad_glu_w")[0].reshape(NDEV, W // NDEV, W)
    dh, dcg, dbg, dconv, shcb = conv_bwd(proj, dyb, conv_f)
    dwin = mm_tn(xb, dgb, "grad_w_in_gb", block0=6, nblocks=NDEV)
    dwin = mm_tn(xb, dga, "grad_w_in_ga", block0=4, into=dwin)
    dwin = mm_tn(xb, dbg, "grad_w_in_bg", block0=3, into=dwin)
    dwin = mm_tn(xb, dcg, "grad_w_in_cg", block0=2, into=dwin)
    dwin = mm_tn(xb, dh, "grad_w_in_h", block0=1, into=dwin)
    dy_p = to_perm(dyn, 0, "perm_dy")
    (du_p, dwb, dwcT, dlbr_s, dlbi_s, dd, su), (r_wo, r_wso, r_wco, r_glu, r_win) = ssm_bwd(
        u_p, dy_p, wb, wbT, wcT, lbr_s, lbi_s, dsk,
        Plans([ScatterPlan([dwo, dwso, dwco, dglu]), ScatterPlan([dwin], only=tuple(range(1, NDEV)))]))

    dbb = lambda m: _diag_blocks(m, GC, NP).reshape(NG * GC, NP)
    dbr2, dbi2, dlam_re, dlam_im, dldt = ssm_param_bwd(
        lam_re, lam_im, ldt, fr, fi, br2, bi2, dbb(dwb[:, :, :SW]), dbb(dwb[:, :, SW:]),
        dlbr_s.reshape(NG, NP), dlbi_s.reshape(NG, NP))
    packed = pack_grads(su, shcb, sga, sgb, dd, dglu_b, dln1_g, dln1_b, dln2_g, dln2_b, dlam_re, dlam_im, dldt, sqerr,
                        dbr2, dbi2, dbb(dwcT[:, :, :SW]), -dbb(dwcT[:, :, SW:]), dconv)
    (du,), _ = from_perm(du_p, "unperm_du", bf16)
    dwin = mm_tn(xb, du, "grad_w_in_u", block0=0, into=dwin)

    rest = [(dh, 0, 1), (dcg, 0, 2), (dbg, 0, 3), (dga, 0, 4), (dga, 1, 5), (dgb, 0, 6), (dgb, 1, 7)]
    (gx_rest,), (r_win, small_all) = in_proj_bwd_x(
        rest, win_g, dr1, ALPHA, "in_proj_bwd_x_rest",
        Plans([ScatterPlan([dwin], only=(0,), into=[r_win]), GatherPlan([packed])]))
    (grad_x,), _ = in_proj_bwd_x([(du, 0, 0)], win_g, gx_rest, 1.0, "in_proj_bwd_x_u")

    out = {}

    def put(name, res, back=lambda a: a[None]):
        out["grad_" + name], out["delta_" + name], out["new_m_" + name], out["new_v_" + name] = [back(r) for r in res]

    put("w_in", adam_update(w_in[0], m_w_in[0], v_w_in[0], r_win, "adam_w_in", 256))
    put("glu_w", adam_update(glu_w[0], m_glu_w[0], v_glu_w[0], r_glu, "adam_glu_w"))
    put("w_ssm_out", adam_update(w_ssm_out[0], m_w_ssm_out[0], v_w_ssm_out[0], r_wso, "adam_w_ssm_out"))
    put("w_conv_out", adam_update(w_conv_out[0], m_w_conv_out[0], v_w_conv_out[0], r_wco, "adam_w_conv_out"))
    put("w_o", adam_update(w_o[0], m_w_o[0], v_w_o[0], r_wo, "adam_w_o"))
    put("w_down", adam_update(w_down[0], m_w_down[0], v_w_down[0], r_wd, "adam_w_down", 176))
    untr = lambda a: jnp.swapaxes(a, 0, 1)[None]
    put("w_gate", adam_update(tr(w_gate), tr(m_w_gate), tr(v_w_gate), r_wgT, "adam_w_gate", 176), untr)
    put("w_up", adam_update(tr(w_up), tr(m_w_up), tr(v_w_up), r_wuT, "adam_w_up", 176), untr)
    as_c = lambda a: jnp.swapaxes(a, 2, 3)
    params = {n: (given[n], given["m_" + n], given["v_" + n]) for n in list(_SMALL) + ["conv_w"]}
    for n in ("ssm_b_re", "ssm_b_im"):
        params[n] = tuple(as_c(a) for a in params[n])
    small, loss = adam_small(small_all, params)
    for n, res in small.items():
        put(n, res, as_c if n in ("ssm_b_re", "ssm_b_im") else (lambda a: a))

    names = ["w_in", "b_in", "ssm_lambda_re", "ssm_lambda_im", "ssm_log_dt", "ssm_b_re", "ssm_b_im", "ssm_c_re", "ssm_c_im",
             "ssm_d", "glu_w", "glu_b", "w_ssm_out", "conv_w", "w_conv_out", "w_o", "ln1_g", "ln1_b", "w_gate", "w_up",
             "w_down", "ln2_g", "ln2_b"]
    return (loss.reshape(()), grad_x[None], *[out[p + n] for p in ("grad_", "delta_", "new_m_", "new_v_") for n in names])
```

```python
import functools
import math

import jax
import jax.numpy as jnp
from jax import lax
from jax.experimental import pallas as pl
from jax.experimental.pallas import tpu as pltpu

f32, bf16 = jnp.float32, jnp.bfloat16
S = jax.ShapeDtypeStruct
MESH = pl.DeviceIdType.MESH
HIGHEST = lax.Precision.HIGHEST

D = 1024
W = 512
NG, NP, GC = 32, 64, 16
F = 2816
NDEV = 8
FS = F // NDEV
IN_COLS = 8 * W
ALPHA = 2.0 ** 0.25
LN_EPS = 1e-5
ADAM_LR, ADAM_B1, ADAM_B2, ADAM_EPS, ADAM_WD, ADAM_STEP = 0.001, 0.9, 0.999, 1e-08, 0.01, 10
NC = 32
LANE = 128
SW = 4 * LANE
VMEM_LIMIT = 56 * 1024 * 1024
GRAD_DT = bf16
ANY = pl.BlockSpec(memory_space=pl.ANY)


def _cp(sem=None, vmem=None):
    return pltpu.CompilerParams(dimension_semantics=sem, vmem_limit_bytes=vmem)


def _resident(shape):
    return pl.BlockSpec(shape, lambda i: (0,) * len(shape), pipeline_mode=pl.Buffered(1))


def _dot(a, b):
    return jnp.dot(a, b, preferred_element_type=f32)


def _dot_nt(a, b):
    return lax.dot_general(a, b, (((1,), (1,)), ((), ())), preferred_element_type=f32)


def _dot_tn(a, b):
    return lax.dot_general(a, b, (((0,), (0,)), ((), ())), preferred_element_type=f32)


def _eye(n):
    return (lax.broadcasted_iota(jnp.int32, (n, n), 0) == lax.broadcasted_iota(jnp.int32, (n, n), 1)).astype(f32)


def _transpose_exact(a):
    return lax.dot_general(a, _eye(a.shape[0]), (((0,), (0,)), ((), ())), precision=HIGHEST, preferred_element_type=f32)


def _sigmoid(x):
    return 1.0 / (1.0 + jnp.exp(-x))


_GK = math.sqrt(2.0 / math.pi)


def _gelu(x):
    return 0.5 * x * (1.0 + jnp.tanh(_GK * (x + 0.044715 * x * x * x)))


def _gelu_grad(x):
    th = jnp.tanh(_GK * (x + 0.044715 * x * x * x))
    return 0.5 * (1.0 + th) + 0.5 * x * (1.0 - th * th) * _GK * (1.0 + 3.0 * 0.044715 * x * x)


ROW_PART = 256


def _row_parts(tm):
    return [slice(r, r + min(ROW_PART, tm)) for r in range(0, tm, min(ROW_PART, tm))]


def _ln_stats(r):
    mu = jnp.mean(r, axis=-1, keepdims=True)
    xc = r - mu
    var = jnp.mean(xc * xc, axis=-1, keepdims=True)
    rstd = lax.rsqrt(var + LN_EPS)
    return xc * rstd, rstd


def _ln_bwd(dy, xhat, rstd, g):
    dxh = dy * g
    m1 = jnp.mean(dxh, axis=-1, keepdims=True)
    m2 = jnp.mean(dxh * xhat, axis=-1, keepdims=True)
    return rstd * (dxh - m1 - xhat * m2)


def _coords():
    return lax.axis_index("x"), lax.axis_index("y"), lax.axis_index("c")


def _when(cond, fn):
    if cond is True:
        fn()
    else:
        pl.when(cond)(fn)


class GatherPlan:
    aliases = ()

    def __init__(self, arrs, srcs=None, into=None):
        n = self.n = len(arrs)
        self.srcs = srcs
        self.inputs = list(arrs) + list(into or [])
        if into:
            self.aliases = tuple((n + a, a) for a in range(n))
        self.out_shape = [S((NDEV,) + a.shape, a.dtype) for a in arrs]
        self.sems = [pltpu.SemaphoreType.DMA((n, 7)), pltpu.SemaphoreType.DMA((n, 7)), pltpu.SemaphoreType.DMA((n,))]

    def _has(self, dev):
        if self.srcs is None:
            return True
        idx = 4 * dev[0] + 2 * dev[1] + dev[2]
        return functools.reduce(jnp.logical_or, [idx == s for s in self.srcs])

    def _parts(self, ins, outs, sems):
        n = self.n
        send_sems, recv_sems, loc_sems = sems
        x, y, c = _coords()
        me, sib = (x, y, c), (x, y, 1 - c)
        chips = [(1 - x, y), (x, 1 - y), (1 - x, 1 - y)]

        def slot(a, dev):
            return outs[a].at[4 * dev[0] + 2 * dev[1] + dev[2]]

        def copy(a, k, block, to, src=None):
            return pltpu.make_async_remote_copy(
                src_ref=slot(a, block) if src is None else src, dst_ref=slot(a, block),
                send_sem=send_sems.at[a, k], recv_sem=recv_sems.at[a, k], device_id=to, device_id_type=MESH)

        each = [(j, chip, a) for j, chip in enumerate(chips) for a in range(n)]
        own = self._has(me)
        return dict(
            mine=lambda: [(pltpu.make_async_copy(ins[a], slot(a, me), loc_sems.at[a]), own) for a in range(n)],
            first=lambda: ([(copy(a, 0, me, sib, src=ins[a]), own) for a in range(n)]
                           + [(copy(a, 1 + j, me, (*chip, c), src=ins[a]), own) for j, chip, a in each]),
            landed=lambda: [(copy(a, 1 + j, (*chip, c), me), self._has((*chip, c))) for j, chip, a in each],
            passed=lambda: [(copy(a, 4 + j, (*chip, c), sib), self._has((*chip, c))) for j, chip, a in each],
            from_sib=lambda: ([(copy(a, 0, sib, me), self._has(sib)) for a in range(n)]
                              + [(copy(a, 4 + j, (*chip, 1 - c), me), self._has((*chip, 1 - c))) for j, chip, a in each]))

    def start(self, ins, outs, sems):
        p = self._parts(ins, outs, sems)
        for cp, cond in p["mine"]() + p["first"]():
            _when(cond, cp.start)

    def forward(self, ins, outs, sems):
        p = self._parts(ins, outs, sems)
        for (got, cond), (fwd, _) in zip(p["landed"](), p["passed"]()):
            def relay(got=got, fwd=fwd):
                got.wait_recv()
                fwd.start()

            _when(cond, relay)

    def finish(self, ins, outs, sems):
        p = self._parts(ins, outs, sems)
        for cp, cond in p["from_sib"]():
            _when(cond, cp.wait_recv)
        for cp, cond in p["first"]() + p["passed"]():
            _when(cond, cp.wait_send)
        for cp, cond in p["mine"]():
            _when(cond, cp.wait)


class ScatterPlan:
    aliases = ()

    def __init__(self, gs, only=None, into=None):
        n = self.n = len(gs)
        self.only = only
        self.inputs = list(gs) + list(into or [])
        if into:
            self.aliases = tuple((n + a, a) for a in range(n))
        self.out_shape = [S(g.shape, g.dtype) for g in gs]
        self.sems = [pltpu.SemaphoreType.DMA((n, 7)), pltpu.SemaphoreType.DMA((n, 7)), pltpu.SemaphoreType.DMA((n,))]

    def _owner(self, idx):
        if self.only is None:
            return True
        return functools.reduce(jnp.logical_or, [idx == b for b in self.only])

    def _copies(self, ins, outs, sems):
        n = self.n
        send_sems, recv_sems, loc_sems = sems
        x, y, c = _coords()
        me = 4 * x + 2 * y + c
        mine = self._owner(me)
        copies = [(pltpu.make_async_copy(ins[a].at[me], outs[a].at[me], loc_sems.at[a]), mine, None) for a in range(n)]
        for m in range(1, NDEV):
            px = 1 - x if m & 4 else x
            py = 1 - y if m & 2 else y
            pc = 1 - c if m & 1 else c
            peer = 4 * px + 2 * py + pc
            for a in range(n):
                copies.append((pltpu.make_async_remote_copy(
                    src_ref=ins[a].at[peer], dst_ref=outs[a].at[me],
                    send_sem=send_sems.at[a, m - 1], recv_sem=recv_sems.at[a, m - 1],
                    device_id=(px, py, pc), device_id_type=MESH), self._owner(peer), mine))
        return copies

    def start(self, ins, outs, sems):
        for cp, sends, _ in self._copies(ins, outs, sems):
            _when(sends, cp.start)

    def forward(self, ins, outs, sems):
        pass

    def finish(self, ins, outs, sems):
        for cp, sends, receives in self._copies(ins, outs, sems):
            if receives is None:
                _when(sends, cp.wait)
            else:
                _when(sends, cp.wait_send)
                _when(receives, cp.wait_recv)


class Plans:
    def __init__(self, plans):
        self.plans = plans
        self.inputs = [a for p in plans for a in p.inputs]
        self.out_shape = [s for p in plans for s in p.out_shape]
        self.sems = [s for p in plans for s in p.sems]
        self.aliases, i, o = [], 0, 0
        for p in plans:
            self.aliases += [(i + a, o + b) for a, b in p.aliases]
            i, o = i + len(p.inputs), o + len(p.out_shape)

    def _each(self, what, ins, outs, sems):
        i = o = s = 0
        for p in self.plans:
            ni, no, ns = len(p.inputs), len(p.out_shape), len(p.sems)
            getattr(p, what)(ins[i:i + ni], outs[o:o + no], sems[s:s + ns])
            i, o, s = i + ni, o + no, s + ns

    def start(self, ins, outs, sems):
        self._each("start", ins, outs, sems)

    def forward(self, ins, outs, sems):
        self._each("forward", ins, outs, sems)

    def finish(self, ins, outs, sems):
        self._each("finish", ins, outs, sems)


def _call(body, args, *, name, grid, in_specs, out_specs, out_shape, scratch=(), sem=None, vmem=None, plan=None,
          aliases=None):
    aliases = aliases or {}
    if plan is None:
        outs = pl.pallas_call(body, name=name, grid=grid, in_specs=list(in_specs), out_specs=list(out_specs),
                              out_shape=list(out_shape), scratch_shapes=list(scratch), input_output_aliases=aliases,
                              compiler_params=_cp(sem, vmem))(*args)
        return list(outs), []
    ni, no, ns = len(in_specs), len(out_specs), len(scratch)
    pi, po = len(plan.inputs), len(plan.out_shape)
    aliases = {**aliases, **{ni + a: no + b for a, b in plan.aliases}}

    def wrapped(*refs):
        main_in, p_in = refs[:ni], refs[ni:ni + pi]
        main_out, p_out = refs[ni + pi:ni + pi + no], refs[ni + pi + no:ni + pi + no + po]
        main_scr, p_sems = refs[ni + pi + no + po:ni + pi + no + po + ns], refs[ni + pi + no + po + ns:]
        ids = [pl.program_id(d) for d in range(len(grid))]
        first = functools.reduce(jnp.logical_and, [i == 0 for i in ids])
        last = functools.reduce(jnp.logical_and, [i == g - 1 for i, g in zip(ids, grid)])

        @pl.when(first)
        def _():
            plan.start(p_in, p_out, p_sems)

        @pl.when(last)
        def _():
            plan.forward(p_in, p_out, p_sems)

        body(*main_in, *main_out, *main_scr)

        @pl.when(last)
        def _():
            plan.finish(p_in, p_out, p_sems)

    outs = pl.pallas_call(
        wrapped, name=name, grid=grid, in_specs=list(in_specs) + [ANY] * pi, out_specs=list(out_specs) + [ANY] * po,
        out_shape=list(out_shape) + list(plan.out_shape), scratch_shapes=list(scratch) + list(plan.sems),
        input_output_aliases=aliases, compiler_params=_cp(("arbitrary",) * len(grid), vmem),
    )(*args, *plan.inputs)
    return list(outs[:no]), list(outs[no:])


def run_plan(plan, name):
    def body(*refs):
        ins, outs, sems = refs[:len(plan.inputs)], refs[len(plan.inputs):len(plan.inputs) + len(plan.out_shape)], \
            refs[len(plan.inputs) + len(plan.out_shape):]
        plan.start(ins, outs, sems)
        plan.forward(ins, outs, sems)
        plan.finish(ins, outs, sems)

    return pl.pallas_call(body, name=name, in_specs=[ANY] * len(plan.inputs), out_specs=[ANY] * len(plan.out_shape),
                          out_shape=list(plan.out_shape), scratch_shapes=list(plan.sems))(*plan.inputs)


def mm_tn(a, b, name, tn=512, into=None, block0=0, nblocks=None):
    T, K = a.shape
    N = b.shape[1]
    tn = min(tn, N)
    nblocks = nblocks or (N // tn if into is None else into.shape[0])

    def body(a_ref, b_ref, *rest):
        rest[-1][...] = _dot_tn(a_ref[...], b_ref[...]).astype(GRAD_DT)

    args, in_specs, aliases = [a, b], [_resident((T, K)), pl.BlockSpec((T, tn), lambda j: (0, j))], {}
    if into is not None:
        args.append(into)
        in_specs.append(ANY)
        aliases = {2: 0}
    (out,), _ = _call(body, args, name=name, grid=(N // tn,), in_specs=in_specs,
                      out_specs=[pl.BlockSpec((None, K, tn), lambda j: (block0 + j, 0, 0))],
                      out_shape=[S((nblocks, K, tn), GRAD_DT)], sem=("parallel",), vmem=VMEM_LIMIT, aliases=aliases)
    return out


def mm_tn_rows(a, b, name, tk=256, plan=None):
    T, K = a.shape
    N = b.shape[1]
    tk = min(tk, K)

    def body(a_ref, b_ref, o_ref):
        o_ref[...] = _dot_tn(a_ref[...], b_ref[...]).astype(GRAD_DT)

    (out,), sent = _call(body, [a, b], name=name, grid=(K // tk,),
                         in_specs=[pl.BlockSpec((T, tk), lambda i: (0, i)), _resident((T, N))],
                         out_specs=[pl.BlockSpec((tk, N), lambda i: (i, 0))], out_shape=[S((K, N), GRAD_DT)],
                         sem=("parallel",), vmem=VMEM_LIMIT, plan=plan)
    return out, sent


def prep_weights(ws):
    def body(*refs):
        for i in range(len(ws)):
            refs[len(ws) + i][...] = refs[i][...].astype(bf16)

    return pl.pallas_call(body, name="prep_weights", out_shape=[S(w.shape, bf16) for w in ws],
                          compiler_params=_cp(None, VMEM_LIMIT))(*ws)


REST_BLOCKS = (4, 5, 6, 7, 1, 2, 3)
REST_COLS = len(REST_BLOCKS) * W


def in_proj_u(x, win_g, b_in):
    T = x.shape[0]
    tm = min(1024, T)

    def body(x_ref, w_ref, b_ref, u_ref, xb_ref):
        xb = x_ref[...].astype(bf16)
        xb_ref[...] = xb
        u_ref[...] = _dot(xb, w_ref[...]) + b_ref[...]

    row = pl.BlockSpec((tm, D), lambda i: (i, 0))
    return pl.pallas_call(
        body, name="in_proj_u", grid=(T // tm,),
        in_specs=[row, pl.BlockSpec((None, D, W), lambda i: (0, 0, 0)), pl.BlockSpec((1, W), lambda i: (0, 0))],
        out_specs=[pl.BlockSpec((tm, W), lambda i: (i, 0)), row],
        out_shape=[S((T, W), f32), S((T, D), bf16)], compiler_params=_cp(("parallel",), VMEM_LIMIT),
    )(x, win_g, b_in)


def in_proj_rest(xb, win_g, b_in, plan):
    T = xb.shape[0]
    tm = min(512, T)

    def body(x_ref, w_ref, b_ref, o_ref):
        xb_ = x_ref[...]
        for i, k in enumerate(REST_BLOCKS):
            o_ref[:, i * W:(i + 1) * W] = _dot(xb_, w_ref[k]) + b_ref[:, k * W:(k + 1) * W]

    return _call(
        body, [xb, win_g, b_in], name="in_proj_rest", grid=(T // tm,),
        in_specs=[pl.BlockSpec((tm, D), lambda i: (i, 0)), _resident((NDEV, D, W)), _resident((1, IN_COLS))],
        out_specs=[pl.BlockSpec((tm, REST_COLS), lambda i: (i, 0))],
        out_shape=[S((T, REST_COLS), f32)], vmem=VMEM_LIMIT, plan=plan)


def to_perm(a, cb0, name):
    T = a.shape[0]
    L = T // NC

    def body(a_ref, o_ref):
        def step(jb, carry):
            j0 = pl.multiple_of(jb * 8, 8)
            for q in range(NC // 8):
                x = jnp.stack([a_ref[pl.ds((8 * q + c) * L + j0, 8), :] for c in range(8)], axis=0)
                y = jnp.swapaxes(x, 0, 1)
                for j in range(8):
                    o_ref[pl.ds((j0 + j) * NC + 8 * q, 8), :] = y[j]
            return carry

        lax.fori_loop(0, L // 8, step, 0)

    return pl.pallas_call(
        body, name=name, grid=(W // LANE,),
        in_specs=[pl.BlockSpec((T, LANE), lambda k: (0, cb0 + k))], out_specs=pl.BlockSpec((T, LANE), lambda k: (0, k)),
        out_shape=S((T, W), f32), compiler_params=_cp(("parallel",), VMEM_LIMIT),
    )(a)


def from_perm(a, name, out_dtype=f32, plan=None):
    T = a.shape[0]
    L = T // NC

    def body(a_ref, o_ref):
        def step(jb, carry):
            j0 = pl.multiple_of(jb * 16, 16)
            for q in range(NC // 8):
                halves = []
                for h in range(2):
                    x = jnp.stack([a_ref[pl.ds((j0 + 8 * h + j) * NC + 8 * q, 8), :] for j in range(8)], axis=0)
                    halves.append(jnp.swapaxes(x, 0, 1))
                for c in range(8):
                    o_ref[pl.ds((8 * q + c) * L + j0, 16), :] = jnp.concatenate(
                        [halves[0][c], halves[1][c]], axis=0).astype(out_dtype)
            return carry

        lax.fori_loop(0, L // 16, step, 0)

    slab = pl.BlockSpec((T, LANE), lambda k: (0, k))
    return _call(body, [a], name=name, grid=(W // LANE,), in_specs=[slab], out_specs=[slab],
                 out_shape=[S((T, W), out_dtype)], sem=("parallel",), vmem=VMEM_LIMIT, plan=plan)


def _disc(lr, li, ldt):
    dt = jnp.exp(ldt)
    mag = jnp.exp(lr * dt)
    lbr = mag * jnp.cos(li * dt)
    lbi = mag * jnp.sin(li * dt)
    den = lr * lr + li * li
    nr = lbr - 1.0
    return lbr, lbi, (nr * lr + lbi * li) / den, (lbi * lr - nr * li) / den


def _per_channel(f):
    return jnp.broadcast_to(f[:, None, :], (NG, GC, NP)).reshape(NG * GC, NP)


def ssm_params(lam_re, lam_im, log_dt, br, bi):
    def body(lr_ref, li_ref, ldt_ref, br_ref, bi_ref, lbr_ref, lbi_ref, fr_ref, fi_ref, bbr_ref, bbi_ref):
        lbr, lbi, fr, fi = _disc(lr_ref[...], li_ref[...], ldt_ref[...])
        lbr_ref[...], lbi_ref[...], fr_ref[...], fi_ref[...] = lbr, lbi, fr, fi
        fr_, fi_, br_, bi_ = _per_channel(fr), _per_channel(fi), br_ref[...], bi_ref[...]
        bbr_ref[...] = fr_ * br_ - fi_ * bi_
        bbi_ref[...] = fr_ * bi_ + fi_ * br_

    return pl.pallas_call(body, name="ssm_params", out_shape=[S((NG, NP), f32)] * 4 + [S((NG * GC, NP), f32)] * 2)(
        lam_re, lam_im, log_dt, br, bi)


SCAN_UNROLL = 4


def _steps(n, body, carry):
    main = n // SCAN_UNROLL

    def trip(t, c):
        for q in range(SCAN_UNROLL):
            c = body(t * SCAN_UNROLL + q, c)
        return c

    carry = lax.fori_loop(0, main, trip, carry)
    for i in range(main * SCAN_UNROLL, n):
        carry = body(i, carry)
    return carry


def _scan_body(T):
    L = T // NC
    RB = min(512, T)
    nsq = int(round(math.log2(L)))
    assert 2 ** nsq == L and T % RB == 0 and L % 16 == 0

    def rows(i):
        return pl.ds(pl.multiple_of(i * RB, RB), RB)

    def tile(j):
        return pl.ds(j * NC if isinstance(j, int) else pl.multiple_of(j * NC, NC), NC)

    def forward_states(u_ref, wb_ref, lbr_ref, lbi_ref, sre, sim, ere, eim):
        def bproj(i, carry):
            bu = _dot(u_ref[rows(i), :].astype(bf16), wb_ref[...])
            sre[rows(i), :] = bu[:, :SW]
            sim[rows(i), :] = bu[:, SW:]
            return carry

        lax.fori_loop(0, T // RB, bproj, 0)
        for lb in range(SW // LANE):
            ls = slice(lb * LANE, (lb + 1) * LANE)
            ar = jnp.broadcast_to(lbr_ref[:, ls], (NC, LANE))
            ai = jnp.broadcast_to(lbi_ref[:, ls], (NC, LANE))

            def step(j, carry):
                xr, xi = carry
                nr = ar * xr - ai * xi + sre[tile(j), ls]
                ni = ar * xi + ai * xr + sim[tile(j), ls]
                sre[tile(j), ls] = nr
                sim[tile(j), ls] = ni
                return nr, ni

            zero = jnp.zeros((NC, LANE), f32)
            _steps(L, step, (zero, zero))
            pr, pi = lbr_ref[:, ls], lbi_ref[:, ls]
            for _ in range(nsq):
                pr, pi = pr * pr - pi * pi, 2.0 * pr * pi
            er = jnp.zeros((1, LANE), f32)
            ei = er
            ere[0:1, ls] = er
            eim[0:1, ls] = ei
            base = (L - 1) * NC
            for c in range(1, NC):
                lr_ = sre[base + c - 1:base + c, ls]
                li_ = sim[base + c - 1:base + c, ls]
                er, ei = lr_ + pr * er - pi * ei, li_ + pr * ei + pi * er
                ere[c:c + 1, ls] = er
                eim[c:c + 1, ls] = ei
            e_r, e_i = ere[:, ls].reshape(NC // 8, 8, LANE), eim[:, ls].reshape(NC // 8, 8, LANE)
            ar8, ai8 = ar[0:8], ai[0:8]

            def fix(j, carry):
                pwr, pwi = carry
                xr = sre[tile(j), ls].reshape(NC // 8, 8, LANE) + (pwr * e_r - pwi * e_i)
                xi = sim[tile(j), ls].reshape(NC // 8, 8, LANE) + (pwr * e_i + pwi * e_r)
                sre[tile(j), ls] = xr.reshape(NC, LANE)
                sim[tile(j), ls] = xi.reshape(NC, LANE)
                return pwr * ar8 - pwi * ai8, pwr * ai8 + pwi * ar8

            _steps(L, fix, (ar8, ai8))

    return L, RB, nsq, rows, tile, forward_states


def ssm_fwd(u_p, wb, wc, lbr, lbi, dsk, plan):
    T = u_p.shape[0]
    L, RB, nsq, rows, tile, forward_states = _scan_body(T)

    def body(u_ref, wb_ref, wc_ref, lbr_ref, lbi_ref, d_ref, y_ref, sre, sim, ere, eim):
        forward_states(u_ref, wb_ref, lbr_ref, lbi_ref, sre, sim, ere, eim)

        def cproj(i, carry):
            y = _dot(sre[rows(i), :].astype(bf16), wc_ref[0:SW, :]) + _dot(sim[rows(i), :].astype(bf16), wc_ref[SW:, :])
            y_ref[rows(i), :] = y + d_ref[...] * u_ref[rows(i), :]
            return carry

        lax.fori_loop(0, T // RB, cproj, 0)

    slab = pl.BlockSpec((T, LANE), lambda k: (0, k))
    return _call(
        body, [u_p, wb, wc, lbr, lbi, dsk], name="ssm_fwd", grid=(W // LANE,),
        in_specs=[slab, pl.BlockSpec((None, LANE, 2 * SW), lambda k: (k, 0, 0)),
                  pl.BlockSpec((None, 2 * SW, LANE), lambda k: (k, 0, 0)),
                  pl.BlockSpec((None, 1, SW), lambda k: (k, 0, 0)), pl.BlockSpec((None, 1, SW), lambda k: (k, 0, 0)),
                  pl.BlockSpec((None, 1, LANE), lambda k: (k, 0, 0))],
        out_specs=[slab], out_shape=[S((T, W), f32)],
        scratch=[pltpu.VMEM((T, SW), f32), pltpu.VMEM((T, SW), f32), pltpu.VMEM((NC, SW), f32), pltpu.VMEM((NC, SW), f32)],
        vmem=VMEM_LIMIT, plan=plan)


def ssm_bwd(u_p, dy_p, wb, wbT, wcT, lbr, lbi, dsk, plan):
    T = u_p.shape[0]
    L, RB, nsq, rows, tile, forward_states = _scan_body(T)

    def body(u_ref, dy_ref, wb_ref, wbT_ref, wcT_ref, lbr_ref, lbi_ref, d_ref,
             du_ref, dwb_ref, dwc_ref, dlr_ref, dli_ref, dd_ref, su_ref, sre, sim, gre, gim, ere, eim):
        forward_states(u_ref, wb_ref, lbr_ref, lbi_ref, sre, sim, ere, eim)

        def dstate(i, carry):
            g = _dot(dy_ref[rows(i), :].astype(bf16), wcT_ref[...])
            gre[rows(i), :] = g[:, :SW]
            gim[rows(i), :] = g[:, SW:]
            return carry

        lax.fori_loop(0, T // RB, dstate, 0)
        row = lax.broadcasted_iota(jnp.int32, (NC, LANE), 0)
        for lb in range(SW // LANE):
            ls = slice(lb * LANE, (lb + 1) * LANE)
            ar = jnp.broadcast_to(lbr_ref[:, ls], (NC, LANE))
            ai = jnp.broadcast_to(lbi_ref[:, ls], (NC, LANE))

            def step(i, carry):
                gr, gi = carry
                j = L - 1 - i
                nr = ar * gr + ai * gi + gre[tile(j), ls]
                ni = ar * gi - ai * gr + gim[tile(j), ls]
                gre[tile(j), ls] = nr
                gim[tile(j), ls] = ni
                return nr, ni

            zero = jnp.zeros((NC, LANE), f32)
            _steps(L, step, (zero, zero))
            pr, pi = lbr_ref[:, ls], -lbi_ref[:, ls]
            for _ in range(nsq):
                pr, pi = pr * pr - pi * pi, 2.0 * pr * pi
            er = jnp.zeros((1, LANE), f32)
            ei = er
            ere[NC - 1:NC, ls] = er
            eim[NC - 1:NC, ls] = ei
            for c in range(NC - 2, -1, -1):
                lr_ = gre[c + 1:c + 2, ls]
                li_ = gim[c + 1:c + 2, ls]
                er, ei = lr_ + pr * er - pi * ei, li_ + pr * ei + pi * er
                ere[c:c + 1, ls] = er
                eim[c:c + 1, ls] = ei
            e_r, e_i = ere[:, ls].reshape(NC // 8, 8, LANE), eim[:, ls].reshape(NC // 8, 8, LANE)
            ar8, ai8 = ar[0:8], ai[0:8]

            def fixed(j, pwr, pwi):
                gr = (gre[tile(j), ls].reshape(NC // 8, 8, LANE) + (pwr * e_r - pwi * e_i)).reshape(NC, LANE)
                gi = (gim[tile(j), ls].reshape(NC // 8, 8, LANE) + (pwr * e_i + pwi * e_r)).reshape(NC, LANE)
                gre[tile(j), ls] = gr
                gim[tile(j), ls] = gi
                return gr, gi

            def fix(i, carry):
                pwr, pwi, accr, acci = carry
                j = L - 1 - i
                gr, gi = fixed(j, pwr, pwi)
                xr, xi = sre[tile(j - 1), ls], sim[tile(j - 1), ls]
                return (pwr * ar8 + pwi * ai8, pwi * ar8 - pwr * ai8,
                        accr + gr * xr + gi * xi, acci + gi * xr - gr * xi)

            pwr, pwi, accr, acci = _steps(L - 1, fix, (ar8, -ai8, zero, zero))
            gr, gi = fixed(0, pwr, pwi)
            xr = jnp.where(row == 0, 0.0, pltpu.roll(sre[tile(L - 1), ls], 1, axis=0))
            xi = jnp.where(row == 0, 0.0, pltpu.roll(sim[tile(L - 1), ls], 1, axis=0))
            accr = accr + gr * xr + gi * xi
            acci = acci + gi * xr - gr * xi
            dlr_ref[:, ls] = jnp.sum(accr, axis=0, keepdims=True)
            dli_ref[:, ls] = jnp.sum(acci, axis=0, keepdims=True)

        dwb_ref[...] = jnp.zeros_like(dwb_ref)
        dwc_ref[...] = jnp.zeros_like(dwc_ref)
        dd_ref[...] = jnp.zeros_like(dd_ref)
        su_ref[...] = jnp.zeros_like(su_ref)

        def finish(i, carry):
            u32, dy32 = u_ref[rows(i), :], dy_ref[rows(i), :]
            ub, dyb = u32.astype(bf16), dy32.astype(bf16)
            gr, gi = gre[rows(i), :].astype(bf16), gim[rows(i), :].astype(bf16)
            du = _dot(gr, wbT_ref[0:SW, :]) + _dot(gi, wbT_ref[SW:, :]) + dy32 * d_ref[...]
            du_ref[rows(i), :] = du
            su_ref[...] += jnp.sum(du, axis=0, keepdims=True)
            dwb_ref[:, 0:SW] += _dot_tn(ub, gr)
            dwb_ref[:, SW:] += _dot_tn(ub, gi)
            dwc_ref[:, 0:SW] += _dot_tn(dyb, sre[rows(i), :].astype(bf16))
            dwc_ref[:, SW:] += _dot_tn(dyb, sim[rows(i), :].astype(bf16))
            dd_ref[...] += jnp.sum(dy32 * u32, axis=0, keepdims=True)
            return carry

        lax.fori_loop(0, T // RB, finish, 0)

    slab = pl.BlockSpec((T, LANE), lambda k: (0, k))
    wide = pl.BlockSpec((None, LANE, 2 * SW), lambda k: (k, 0, 0))
    tall = pl.BlockSpec((None, 2 * SW, LANE), lambda k: (k, 0, 0))
    vec = pl.BlockSpec((None, 1, SW), lambda k: (k, 0, 0))
    vecd = pl.BlockSpec((None, 1, LANE), lambda k: (k, 0, 0))
    nslab = W // LANE
    return _call(
        body, [u_p, dy_p, wb, wbT, wcT, lbr, lbi, dsk], name="ssm_bwd", grid=(nslab,),
        in_specs=[slab, slab, wide, tall, wide, vec, vec, vecd],
        out_specs=[slab, wide, wide, vec, vec, vecd, vecd],
        out_shape=[S((T, W), f32), S((nslab, LANE, 2 * SW), f32), S((nslab, LANE, 2 * SW), f32),
                   S((nslab, 1, SW), f32), S((nslab, 1, SW), f32), S((nslab, 1, LANE), f32), S((nslab, 1, LANE), f32)],
        scratch=[pltpu.VMEM((T, SW), f32)] * 4 + [pltpu.VMEM((NC, SW), f32)] * 2, vmem=VMEM_LIMIT, plan=plan)


def glu_fwd(yn, glu_w, glu_b):
    T = yn.shape[0]
    tm = min(512, T)

    def body(y_ref, w_ref, b_ref, o_ref):
        g = _gelu(y_ref[...])
        o_ref[...] = (g * _sigmoid(_dot(g.astype(bf16), w_ref[...]) + b_ref[...])).astype(bf16)

    return pl.pallas_call(
        body, name="glu_fwd", grid=(T // tm,),
        in_specs=[pl.BlockSpec((tm, W), lambda i: (i, 0)), pl.BlockSpec((W, W), lambda i: (0, 0)), pl.BlockSpec((1, W), lambda i: (0, 0))],
        out_specs=pl.BlockSpec((tm, W), lambda i: (i, 0)), out_shape=S((T, W), bf16), compiler_params=_cp(("parallel",)),
    )(yn, glu_w, glu_b)


def _shift_rows(cur, prev8, k):
    return pltpu.roll(jnp.concatenate([prev8, cur], axis=0), k, axis=0)[8:]


def _lift_rows(cur, next8, k):
    n = cur.shape[0]
    return pltpu.roll(jnp.concatenate([cur, next8], axis=0), n + 8 - k, axis=0)[:n]


def conv_fwd(proj, conv_w):
    T = proj.shape[0]
    RB = min(512, T)

    def body(h_ref, c_ref, b_ref, w_ref, o_ref):
        w0, w1, w2 = w_ref[0:1, :], w_ref[1:2, :], w_ref[2:3, :]

        def blk(i, carry):
            r0 = pl.multiple_of(i * RB, RB)
            rs = pl.ds(r0, RB)
            ch = c_ref[rs, :] * h_ref[rs, :]
            pr = pl.ds(jnp.maximum(r0 - 8, 0), 8)
            prev = jnp.where(i > 0, c_ref[pr, :] * h_ref[pr, :], 0.0)
            z = w2 * ch + w1 * _shift_rows(ch, prev, 1) + w0 * _shift_rows(ch, prev, 2)
            o_ref[rs, :] = (b_ref[rs, :] * z).astype(bf16)
            return carry

        lax.fori_loop(0, T // RB, blk, 0)

    nb = W // LANE
    return pl.pallas_call(
        body, name="conv_fwd", grid=(nb,),
        in_specs=[pl.BlockSpec((T, LANE), lambda k: (0, 4 * nb + k)), pl.BlockSpec((T, LANE), lambda k: (0, 5 * nb + k)),
                  pl.BlockSpec((T, LANE), lambda k: (0, 6 * nb + k)),pl.BlockSpec((3, LANE), lambda k: (0, k))],
        out_specs=pl.BlockSpec((T, LANE), lambda k: (0, k)), out_shape=S((T, W), bf16),
        compiler_params=_cp(("parallel",), VMEM_LIMIT),
    )(proj, proj, proj, conv_w)


def _dense_columns(blocks_ref, dense_ref):
    for k in range(NDEV):
        dense_ref[:, k * LANE:(k + 1) * LANE] = blocks_ref[k]


def merge_fwd(ya, yb, wso, wco, proj, plan):
    T = ya.shape[0]
    tm = min(1024, T)

    def body(ya_ref, yb_ref, wa_ref, wb_ref, ga_ref, gb_ref, o_ref, wa_s, wb_s):
        @pl.when(pl.program_id(0) == 0)
        def _():
            _dense_columns(wa_ref, wa_s)
            _dense_columns(wb_ref, wb_s)

        o_ref[...] = (_sigmoid(ga_ref[...]) * _dot(ya_ref[...], wa_s[...])
                      + _sigmoid(gb_ref[...]) * _dot(yb_ref[...], wb_s[...])).astype(bf16)

    act = pl.BlockSpec((tm, W), lambda i: (i, 0))
    return _call(
        body, [ya, yb, wso, wco, proj, proj], name="merge_fwd", grid=(T // tm,),
        in_specs=[act, act, _resident((NDEV, W, LANE)), _resident((NDEV, W, LANE)),
                  pl.BlockSpec((tm, D), lambda i: (i, 0)), pl.BlockSpec((tm, D), lambda i: (i, 1))],
        out_specs=[pl.BlockSpec((tm, D), lambda i: (i, 0))], out_shape=[S((T, D), bf16)],
        scratch=[pltpu.VMEM((W, D), bf16), pltpu.VMEM((W, D), bf16)], vmem=VMEM_LIMIT, plan=plan)


def mix_ln1(merged, w_o, x, g1, b1):
    T = x.shape[0]
    tm = min(512, T)

    def body(m_ref, w_ref, x_ref, g_ref, b_ref, r_ref, x1_ref):
        for rs in _row_parts(tm):
            r = ALPHA * x_ref[rs, :] + _dot(m_ref[rs, :], w_ref[...])
            r_ref[rs, :] = r
            xhat, _ = _ln_stats(r)
            x1_ref[rs, :] = (xhat * g_ref[...] + b_ref[...]).astype(bf16)

    row = pl.BlockSpec((tm, D), lambda i: (i, 0))
    vec = pl.BlockSpec((1, D), lambda i: (0, 0))
    return pl.pallas_call(
        body, name="mix_ln1", grid=(T // tm,),
        in_specs=[row, _resident((D, D)), row, vec, vec],
        out_specs=[row, row], out_shape=[S((T, D), f32), S((T, D), bf16)], compiler_params=_cp(("parallel",), VMEM_LIMIT),
    )(merged, w_o, x, g1, b1)


FT = 256


def gate_up(x1b, wgT, wuT, plan):
    T = x1b.shape[0]
    tm = min(512, T)

    def body(x_ref, wg_ref, wu_ref, g_ref, u_ref, h_ref):
        x = x_ref[...]
        for n in range(F // FT):
            cs = slice(n * FT, (n + 1) * FT)
            g = _dot_nt(x, wg_ref[cs, :])
            u = _dot_nt(x, wu_ref[cs, :])
            g_ref[:, cs] = g.astype(bf16)
            u_ref[:, cs] = u.astype(bf16)
            h_ref[:, cs] = (g * _sigmoid(g) * u).astype(bf16)

    osp = pl.BlockSpec((tm, F), lambda i: (i, 0))
    return _call(
        body, [x1b, wgT, wuT], name="gate_up", grid=(T // tm,),
        in_specs=[pl.BlockSpec((tm, D), lambda i: (i, 0)), _resident((F, D)), _resident((F, D))],
        out_specs=[osp, osp, osp], out_shape=[S((T, F), bf16)] * 3, vmem=VMEM_LIMIT, plan=plan)


def down_loss(hid, w_down, r1, g1, b1, g2, b2, target):
    T = hid.shape[0]
    tm = min(512, T)

    def body(h_ref, w_ref, r1_ref, g1_ref, b1_ref, g2_ref, b2_ref, t_ref, dr_ref, drb_ref, loss_ref, dg_ref, db_ref):
        @pl.when(pl.program_id(0) == 0)
        def _():
            loss_ref[...] = jnp.zeros_like(loss_ref)
            dg_ref[...] = jnp.zeros_like(dg_ref)
            db_ref[...] = jnp.zeros_like(db_ref)

        for rs in _row_parts(tm):
            xh1, _ = _ln_stats(r1_ref[rs, :])
            x1 = xh1 * g1_ref[...] + b1_ref[...]
            r2 = ALPHA * x1 + _dot(h_ref[rs, :], w_ref[...])
            xh2, rstd2 = _ln_stats(r2)
            err = xh2 * g2_ref[...] + b2_ref[...] - t_ref[rs, :]
            loss_ref[...] += jnp.sum(jnp.mean(err * err, axis=-1, keepdims=True), axis=0, keepdims=True)
            dy = err * (1.0 / D)
            dg_ref[...] += jnp.sum(dy * xh2, axis=0, keepdims=True)
            db_ref[...] += jnp.sum(dy, axis=0, keepdims=True)
            dr = _ln_bwd(dy, xh2, rstd2, g2_ref[...])
            dr_ref[rs, :] = dr
            drb_ref[rs, :] = dr.astype(bf16)

    row = pl.BlockSpec((tm, D), lambda i: (i, 0))
    vec = pl.BlockSpec((1, D), lambda i: (0, 0))
    return pl.pallas_call(
        body, name="down_loss", grid=(T // tm,),
        in_specs=[pl.BlockSpec((tm, F), lambda i: (i, 0)), _resident((F, D)), row, vec, vec, vec, vec, row],
        out_specs=[row, row, pl.BlockSpec((1, 1), lambda i: (0, 0)), vec, vec],
        out_shape=[S((T, D), f32), S((T, D), bf16), S((1, 1), f32), S((1, D), f32), S((1, D), f32)],
        compiler_params=_cp(("arbitrary",), VMEM_LIMIT),
    )(hid, w_down, r1, g1, b1, g2, b2, target)


def ffn_bwd_act(dffn, w_down, gate, up):
    T = dffn.shape[0]
    tm = min(512, T)

    def body(d_ref, w_ref, g_ref, u_ref, dg_ref, du_ref):
        for n in range(F // FT):
            cs = slice(n * FT, (n + 1) * FT)
            for rs in _row_parts(tm):
                dh = _dot_nt(d_ref[rs, :], w_ref[cs, :])
                g, u = g_ref[rs, cs].astype(f32), u_ref[rs, cs].astype(f32)
                sg = _sigmoid(g)
                t = g * sg
                du_ref[rs, cs] = (dh * t).astype(bf16)
                dg_ref[rs, cs] = (dh * u * (sg + t - t * sg)).astype(bf16)

    osp = pl.BlockSpec((tm, F), lambda i: (i, 0))
    return pl.pallas_call(
        body, name="ffn_bwd_act", grid=(T // tm,),
        in_specs=[pl.BlockSpec((tm, D), lambda i: (i, 0)), _resident((F, D)), osp, osp],
        out_specs=[osp, osp], out_shape=[S((T, F), bf16)] * 2, compiler_params=_cp(("parallel",), VMEM_LIMIT),
    )(dffn, w_down, gate, up)


def ffn_bwd_x(dgate, dup, wgT, wuT, dr2, r1, g1, plan):
    T = dr2.shape[0]
    tm = min(512, T)

    def body(dg_ref, du_ref, wg_ref, wu_ref, dr2_ref, r1_ref, g1_ref, dr_ref, drb_ref, dgam_ref, dbet_ref):
        @pl.when(pl.program_id(0) == 0)
        def _():
            dgam_ref[...] = jnp.zeros_like(dgam_ref)
            dbet_ref[...] = jnp.zeros_like(dbet_ref)

        for rs in _row_parts(tm):
            dx1 = ALPHA * dr2_ref[rs, :] + _dot(dg_ref[rs, :], wg_ref[...]) + _dot(du_ref[rs, :], wu_ref[...])
            xh, rstd = _ln_stats(r1_ref[rs, :])
            dgam_ref[...] += jnp.sum(dx1 * xh, axis=0, keepdims=True)
            dbet_ref[...] += jnp.sum(dx1, axis=0, keepdims=True)
            dr = _ln_bwd(dx1, xh, rstd, g1_ref[...])
            dr_ref[rs, :] = dr
            drb_ref[rs, :] = dr.astype(bf16)

    row = pl.BlockSpec((tm, D), lambda i: (i, 0))
    wide = pl.BlockSpec((tm, F), lambda i: (i, 0))
    wsp = _resident((F, D))
    vec = pl.BlockSpec((1, D), lambda i: (0, 0))
    return _call(
        body, [dgate, dup, wgT, wuT, dr2, r1, g1], name="ffn_bwd_x", grid=(T // tm,),
        in_specs=[wide, wide, wsp, wsp, row, row, vec],
        out_specs=[row, row, vec, vec], out_shape=[S((T, D), f32), S((T, D), bf16), S((1, D), f32), S((1, D), f32)],
        vmem=VMEM_LIMIT, plan=plan)


def merge_bwd(dmix, w_o, ya, yb, wso, wco, proj, plan):
    T = dmix.shape[0]
    tm = min(512, T)

    def body(dm_ref, wo_ref, ya_ref, yb_ref, wa_ref, wb_ref, ga_ref, gb_ref, dya_ref, dyb_ref, dga_ref, dgb_ref, sa_ref, sb_ref,
             wa_s, wb_s):
        @pl.when(pl.program_id(0) == 0)
        def _():
            _dense_columns(wa_ref, wa_s)
            _dense_columns(wb_ref, wb_s)

        dmer = _dot_nt(dm_ref[...], wo_ref[...])
        sa, sb = _sigmoid(ga_ref[...]), _sigmoid(gb_ref[...])
        dya_ref[...] = (dmer * sa).astype(bf16)
        dyb_ref[...] = (dmer * sb).astype(bf16)
        dga = dmer * _dot(ya_ref[...], wa_s[...]) * sa * (1.0 - sa)
        dgb = dmer * _dot(yb_ref[...], wb_s[...]) * sb * (1.0 - sb)
        dga_ref[...] = dga.astype(bf16)
        dgb_ref[...] = dgb.astype(bf16)
        sa_ref[...] = jnp.sum(dga, axis=0, keepdims=True)
        sb_ref[...] = jnp.sum(dgb, axis=0, keepdims=True)

    act = pl.BlockSpec((tm, W), lambda i: (i, 0))
    osp = pl.BlockSpec((tm, D), lambda i: (i, 0))
    ssp = pl.BlockSpec((None, 1, D), lambda i: (i, 0, 0))
    return _call(
        body, [dmix, w_o, ya, yb, wso, wco, proj, proj], name="merge_bwd", grid=(T // tm,),
        in_specs=[osp, _resident((D, D)), act, act, _resident((NDEV, W, LANE)), _resident((NDEV, W, LANE)),
                  pl.BlockSpec((tm, D), lambda i: (i, 0)), pl.BlockSpec((tm, D), lambda i: (i, 1))],
        out_specs=[osp, osp, osp, osp, ssp, ssp],
        out_shape=[S((T, D), bf16)] * 4 + [S((T // tm, 1, D), f32)] * 2,
        scratch=[pltpu.VMEM((W, D), bf16), pltpu.VMEM((W, D), bf16)], vmem=VMEM_LIMIT, plan=plan)


def branches_bwd_x(dYA, dYB, wso, wco, plan):
    T = dYA.shape[0]
    tm = min(1024, T)

    def body(da_ref, db_ref, wa_ref, wb_ref, oa_ref, ob_ref, wa_s, wb_s):
        @pl.when(pl.program_id(0) == 0)
        def _():
            _dense_columns(wa_ref, wa_s)
            _dense_columns(wb_ref, wb_s)

        oa_ref[...] = _dot_nt(da_ref[...], wa_s[...])
        ob_ref[...] = _dot_nt(db_ref[...], wb_s[...])

    row = pl.BlockSpec((tm, D), lambda i: (i, 0))
    osp = pl.BlockSpec((tm, W), lambda i: (i, 0))
    return _call(
        body, [dYA, dYB, wso, wco], name="branches_bwd_x", grid=(T // tm,),
        in_specs=[row, row, _resident((NDEV, W, LANE)), _resident((NDEV, W, LANE))],
        out_specs=[osp, osp], out_shape=[S((T, W), f32)] * 2,
        scratch=[pltpu.VMEM((W, D), bf16), pltpu.VMEM((W, D), bf16)], vmem=VMEM_LIMIT, plan=plan)


def branch_bwd_w(act, dY, name):
    T = act.shape[0]
    tk = W // 2

    def body(a_ref, d_ref, o_ref):
        res = _dot_tn(a_ref[...], d_ref[...])
        for k in range(NDEV):
            o_ref[k] = res[:, k * LANE:(k + 1) * LANE].astype(o_ref.dtype)

    return pl.pallas_call(
        body, name=name, grid=(W // tk,),
        in_specs=[pl.BlockSpec((T, tk), lambda i: (0, i)), _resident((T, D))],
        out_specs=pl.BlockSpec((NDEV, tk, LANE), lambda i: (0, i, 0)), out_shape=S((NDEV, W, LANE), GRAD_DT),
        compiler_params=_cp(("parallel",), VMEM_LIMIT),
    )(act, dY)


def glu_bwd(yn, dya, glu_w, glu_b):
    T = yn.shape[0]
    tm = min(512, T)

    def body(y_ref, d_ref, w_ref, b_ref, dy_ref, dsp_ref, g_ref, db_ref):
        @pl.when(pl.program_id(0) == 0)
        def _():
            db_ref[...] = jnp.zeros_like(db_ref)

        y, dya_ = y_ref[...], d_ref[...]
        g = _gelu(y)
        gb = g.astype(bf16)
        s = _sigmoid(_dot(gb, w_ref[...]) + b_ref[...])
        dsp = dya_ * g * s * (1.0 - s)
        dspb = dsp.astype(bf16)
        dg = dya_ * s + _dot_nt(dspb, w_ref[...])
        dy_ref[...] = dg * _gelu_grad(y)
        dsp_ref[...] = dspb
        g_ref[...] = gb
        db_ref[...] += jnp.sum(dsp, axis=0, keepdims=True)

    row = pl.BlockSpec((tm, W), lambda i: (i, 0))
    vec = pl.BlockSpec((1, W), lambda i: (0, 0))
    return pl.pallas_call(
        body, name="glu_bwd", grid=(T // tm,),
        in_specs=[row, row, pl.BlockSpec((W, W), lambda i: (0, 0)), vec],
        out_specs=[row, row, row, vec], out_shape=[S((T, W), f32), S((T, W), bf16), S((T, W), bf16), S((1, W), f32)],
        compiler_params=_cp(("arbitrary",)),
    )(yn, dya, glu_w, glu_b)


def conv_bwd(proj, dyb, conv_w):
    T = proj.shape[0]
    RB = min(512, T)
    nrb = T // RB

    def body(h_ref, c_ref, b_ref, d_ref, w_ref, dh_ref, dc_ref, db_ref, dw_ref, s_ref):
        w0, w1, w2 = w_ref[0:1, :], w_ref[1:2, :], w_ref[2:3, :]

        def blk(i, carry):
            a0, a1, a2, sh, sc, sb = carry
            r0 = pl.multiple_of(i * RB, RB)
            rs = pl.ds(r0, RB)
            h, cg, bg, dyb_ = h_ref[rs, :], c_ref[rs, :], b_ref[rs, :], d_ref[rs, :]
            ch = cg * h
            pr = pl.ds(jnp.maximum(r0 - 8, 0), 8)
            prev = jnp.where(i > 0, c_ref[pr, :] * h_ref[pr, :], 0.0)
            ch1, ch2 = _shift_rows(ch, prev, 1), _shift_rows(ch, prev, 2)
            dbg = dyb_ * (w2 * ch + w1 * ch1 + w0 * ch2)
            db_ref[rs, :] = dbg.astype(bf16)
            dz = dyb_ * bg
            nx = pl.ds(jnp.minimum(r0 + RB, T - 8), 8)
            nxt = jnp.where(i < nrb - 1, d_ref[nx, :] * b_ref[nx, :], 0.0)
            dch = w2 * dz + w1 * _lift_rows(dz, nxt, 1) + w0 * _lift_rows(dz, nxt, 2)
            dcg, dh = dch * h, dch * cg
            dc_ref[rs, :] = dcg.astype(bf16)
            dh_ref[rs, :] = dh.astype(bf16)
            col = lambda v: jnp.sum(v, axis=0, keepdims=True)
            return (a0 + col(dz * ch2), a1 + col(dz * ch1), a2 + col(dz * ch), sh + col(dh), sc + col(dcg), sb + col(dbg))

        zero = jnp.zeros((1, LANE), f32)
        a0, a1, a2, sh, sc, sb = lax.fori_loop(0, nrb, blk, (zero,) * 6)
        dw_ref[0:1, :] = a0
        dw_ref[1:2, :] = a1
        dw_ref[2:3, :] = a2
        s_ref[0:1, :] = sh
        s_ref[1:2, :] = sc
        s_ref[2:3, :] = sb

    nb = W // LANE
    slab = pl.BlockSpec((T, LANE), lambda k: (0, k))
    three = pl.BlockSpec((3, LANE), lambda k: (0, k))
    return pl.pallas_call(
        body, name="conv_bwd", grid=(nb,),
        in_specs=[pl.BlockSpec((T, LANE), lambda k: (0, 4 * nb + k)), pl.BlockSpec((T, LANE), lambda k: (0, 5 * nb + k)),
                  pl.BlockSpec((T, LANE), lambda k: (0, 6 * nb + k)),slab, three],
        out_specs=[slab, slab, slab, three, three],
        out_shape=[S((T, W), bf16)] * 3 + [S((3, W), f32)] * 2, compiler_params=_cp(("parallel",), VMEM_LIMIT),
    )(proj, proj, proj, dyb, conv_w)


def in_proj_bwd_x(parts, win_g, base, scale, name, plan=None):
    T = base.shape[0]
    tm = min(512, T)
    n = len(parts)

    def body(*refs):
        p_refs, w_ref, b_ref, o_ref = refs[:n], refs[n], refs[n + 1], refs[n + 2]
        acc = scale * b_ref[...]
        for p_ref, (_, _, k) in zip(p_refs, parts):
            acc += _dot_nt(p_ref[...], w_ref[k])
        o_ref[...] = acc

    row = pl.BlockSpec((tm, D), lambda i: (i, 0))
    p_specs = [pl.BlockSpec((tm, W), (lambda i, cb=cb: (i, cb))) for _, cb, _ in parts]
    return _call(
        body, [a for a, _, _ in parts] + [win_g, base], name=name, grid=(T // tm,),
        in_specs=p_specs + [_resident((NDEV, D, W)), row],
        out_specs=[row], out_shape=[S((T, D), f32)], vmem=VMEM_LIMIT, plan=plan)


def ssm_param_bwd(lam_re, lam_im, log_dt, fr, fi, br, bi, dbbr, dbbi, dlbr, dlbi):
    def body(lr_ref, li_ref, ldt_ref, fr_ref, fi_ref, br_ref, bi_ref, dr_ref, di_ref, dlbr_ref, dlbi_ref,
             dbr_ref, dbi_ref, dlr_ref, dli_ref, dldt_ref):
        fr_, fi_ = _per_channel(fr_ref[...]), _per_channel(fi_ref[...])
        br_, bi_, dr, di = br_ref[...], bi_ref[...], dr_ref[...], di_ref[...]
        dbr_ref[...] = fr_ * dr + fi_ * di
        dbi_ref[...] = fr_ * di - fi_ * dr
        dfr = jnp.sum((dr * br_ + di * bi_).reshape(NG, GC, NP), axis=1)
        dfi = jnp.sum((di * br_ - dr * bi_).reshape(NG, GC, NP), axis=1)
        _, vjp = jax.vjp(_disc, lr_ref[...], li_ref[...], ldt_ref[...])
        dlr_ref[...], dli_ref[...], dldt = vjp((dlbr_ref[...], dlbi_ref[...], dfr, dfi))
        dldt_ref[...] = _transpose_exact(dldt)

    return pl.pallas_call(
        body, name="ssm_param_bwd",
        out_shape=[S((NG * GC, NP), f32)] * 2 + [S((NG, NP), f32)] * 2 + [S((1, NG), f32)])(
        lam_re, lam_im, log_dt, fr, fi, br, bi, dbbr, dbbi, dlbr, dlbi)


def _adam(w, g, m, v):
    m = ADAM_B1 * m + (1.0 - ADAM_B1) * g
    v = ADAM_B2 * v + (1.0 - ADAM_B2) * (g * g)
    m_hat = m / (1.0 - ADAM_B1 ** ADAM_STEP)
    v_hat = v / (1.0 - ADAM_B2 ** ADAM_STEP)
    return -ADAM_LR * (m_hat / (jnp.sqrt(v_hat) + ADAM_EPS) + ADAM_WD * w), m, v


def adam_update(w, m, v, contrib, name, rows_per_block=None):
    R, C = w.shape
    n = contrib.shape[0]
    tr = min(rows_per_block or R, R)

    def body(w_ref, m_ref, v_ref, c_ref, g_ref, d_ref, nm_ref, nv_ref):
        g = c_ref[0].astype(f32)
        for k in range(1, n):
            g = g + c_ref[k].astype(f32)
        g_ref[...] = g
        d_ref[...], nm_ref[...], nv_ref[...] = _adam(w_ref[...], g, m_ref[...], v_ref[...])

    blk = pl.BlockSpec((tr, C), lambda i: (i, 0))
    return pl.pallas_call(
        body, name=name, grid=(R // tr,), in_specs=[blk, blk, blk, pl.BlockSpec((n, tr, C), lambda i: (0, i, 0))],
        out_specs=[blk] * 4, out_shape=[S((R, C), f32)] * 4, compiler_params=_cp(("parallel",), VMEM_LIMIT),
    )(w, m, v, contrib)


_ROWVEC = (("b_in", IN_COLS), ("ssm_d", W), ("glu_b", W), ("ln1_g", D), ("ln1_b", D), ("ln2_g", D), ("ln2_b", D))
_HALF = NG * GC // 2
_BC_LANE = {"ssm_b_re": 0, "ssm_b_im": NP, "ssm_c_re": 0, "ssm_c_im": NP}
_PACK = {}
_r = 0
for _n, _k in _ROWVEC:
    _PACK[_n] = _r
    _r += _k // LANE
for _n, _rows in (("ssm_lambda", NG), ("scalars", 8), ("ssm_b", _HALF), ("ssm_c", _HALF), ("conv_w", 16)):
    _PACK[_n] = _r
    _r += _rows
for _n in _BC_LANE:
    _PACK[_n] = _PACK[_n[:5]]
PACK_ROWS = _r
assert PACK_ROWS % 8 == 0
_SMALL = ("b_in", "ssm_lambda_re", "ssm_lambda_im", "ssm_log_dt", "ssm_b_re", "ssm_b_im", "ssm_c_re", "ssm_c_im",
          "ssm_d", "glu_b", "ln1_g", "ln1_b", "ln2_g", "ln2_b")


def pack_grads(su, shcb, sga, sgb, dd, dglu_b, dln1_g, dln1_b, dln2_g, dln2_b, dlam_re, dlam_im, dldt, sqerr, dbr, dbi,
               dc_re, dc_im, dconv):
    nI = sga.shape[0]

    def body(su_ref, sh_ref, sga_ref, sgb_ref, dd_ref, gb_ref, l1g_ref, l1b_ref, l2g_ref, l2b_ref, lr_ref, li_ref, dt_ref,
             sq_ref, br_ref, bi_ref, cr_ref, ci_ref, cw_ref, o_ref):
        o_ref[...] = jnp.zeros_like(o_ref)

        def put_row(name, v):
            r0 = _PACK[name]
            for i in range(v.shape[1] // LANE):
                o_ref[r0 + i:r0 + i + 1, :] = v[:, i * LANE:(i + 1) * LANE]

        ga, gb = sga_ref[0], sgb_ref[0]
        for i in range(1, nI):
            ga, gb = ga + sga_ref[i], gb + sgb_ref[i]
        put_row("b_in", jnp.concatenate([su_ref[k] for k in range(W // LANE)]
                                        + [sh_ref[0:1, :], sh_ref[1:2, :], sh_ref[2:3, :], ga, gb], axis=1))
        put_row("ssm_d", jnp.concatenate([dd_ref[k] for k in range(W // LANE)], axis=1))
        put_row("glu_b", gb_ref[...])
        put_row("ln1_g", l1g_ref[...])
        put_row("ln1_b", l1b_ref[...])
        put_row("ln2_g", l2g_ref[...])
        put_row("ln2_b", l2b_ref[...])
        r0 = _PACK["ssm_lambda"]
        o_ref[r0:r0 + NG, 0:NP] = lr_ref[...]
        o_ref[r0:r0 + NG, NP:2 * NP] = li_ref[...]
        r0 = _PACK["scalars"]
        o_ref[r0:r0 + 1, 0:NG] = dt_ref[...]
        o_ref[r0 + 1:r0 + 2, 0:1] = sq_ref[...]
        for name, ref in (("ssm_b_re", br_ref), ("ssm_b_im", bi_ref), ("ssm_c_re", cr_ref), ("ssm_c_im", ci_ref)):
            r0, l0 = _PACK[name], _BC_LANE[name]
            o_ref[r0:r0 + _HALF, l0:l0 + NP] = pltpu.bitcast(ref[...].astype(bf16), f32)
        for cb in range(W // LANE):
            o_ref[_PACK["conv_w"] + 3 * cb:_PACK["conv_w"] + 3 * cb + 3, :] = cw_ref[:, cb * LANE:(cb + 1) * LANE]

    return pl.pallas_call(body, name="pack_grads", out_shape=S((PACK_ROWS, LANE), f32))(
        su, shcb, sga, sgb, dd, dglu_b, dln1_g, dln1_b, dln2_g, dln2_b, dlam_re, dlam_im, dldt, sqerr, dbr, dbi, dc_re, dc_im,
        dconv)


def adam_small(packed_all, params):
    names = list(_SMALL) + ["conv_w"]
    flat = [a for n in names for a in params[n]]

    def body(*refs):
        p_ref = refs[0]
        ins = refs[1:1 + 3 * len(names)]
        outs = refs[1 + 3 * len(names):-2]
        loss_ref, g_ref = refs[-2], refs[-1]
        g_all = p_ref[0]
        for k in range(1, NDEV):
            g_all = g_all + p_ref[k]
        g_ref[...] = g_all

        def rows(name, r0, n, l0=0, lanes=LANE):
            return g_ref[_PACK[name] + r0:_PACK[name] + r0 + n, l0:l0 + lanes]

        def grad_of(name):
            if name in dict(_ROWVEC):
                return jnp.concatenate([rows(name, i, 1) for i in range(dict(_ROWVEC)[name] // LANE)], axis=1)
            if name in ("ssm_lambda_re", "ssm_lambda_im"):
                return rows("ssm_lambda", 0, NG, NP * (name == "ssm_lambda_im"), NP)[None]
            if name == "ssm_log_dt":
                return rows("scalars", 0, 1, 0, NG)
            if name in _BC_LANE:
                r0, l0 = _PACK[name], _BC_LANE[name]
                g = pltpu.bitcast(p_ref[0, r0:r0 + _HALF, l0:l0 + NP], bf16).astype(f32)
                for k in range(1, NDEV):
                    g = g + pltpu.bitcast(p_ref[k, r0:r0 + _HALF, l0:l0 + NP], bf16).astype(f32)
                return g.reshape(1, NG, GC, NP)
            full = jnp.concatenate([rows("conv_w", 3 * cb, 3) for cb in range(W // LANE)], axis=1)
            x, y, c = _coords()
            col0 = (4 * x + 2 * y + c) * (W // NDEV)
            sel = (lax.broadcasted_iota(jnp.int32, (W, W // NDEV), 0)
                   == lax.broadcasted_iota(jnp.int32, (W, W // NDEV), 1) + col0).astype(f32)
            return jnp.dot(full, sel, precision=HIGHEST, preferred_element_type=f32)[None]

        loss_ref[...] = 0.5 * rows("scalars", 1, 1, 0, 1)
        for i, name in enumerate(names):
            w_ref, m_ref, v_ref = ins[3 * i:3 * i + 3]
            g = grad_of(name)
            d, m, v = _adam(w_ref[...], g, m_ref[...], v_ref[...])
            outs[4 * i][...] = g
            outs[4 * i + 1][...] = d
            outs[4 * i + 2][...] = m
            outs[4 * i + 3][...] = v

    out_shape = [S(params[n][0].shape, f32) for n in names for _ in range(4)] + [S((1, 1), f32)]
    res = pl.pallas_call(body, name="adam_small", out_shape=out_shape, scratch_shapes=[pltpu.VMEM((PACK_ROWS, LANE), f32)],
                         compiler_params=_cp(None, VMEM_LIMIT))(packed_all, *flat)
    return {n: res[4 * i:4 * i + 4] for i, n in enumerate(names)}, res[-1]


def _block_diag(wgt):
    eye = jnp.eye(8, dtype=wgt.dtype)
    out = wgt[:, :, :, None, :] * eye[None, :, None, :, None]
    return out.reshape(4, 8 * wgt.shape[2], 8 * wgt.shape[3])


def _diag_blocks(m, a, b):
    m = m.reshape(4, 8, a, 8, b)
    idx = jnp.arange(8)
    return m[:, idx, :, idx, :].transpose(1, 0, 2, 3)


def kernel(x, w_in, b_in, ssm_lambda_re, ssm_lambda_im, ssm_log_dt, ssm_b_re, ssm_b_im, ssm_c_re, ssm_c_im, ssm_d, glu_w, glu_b, w_ssm_out, conv_w, w_conv_out, w_o, ln1_g, ln1_b, w_gate, w_up, w_down, ln2_g, ln2_b, loss_target, m_w_in, m_b_in, m_ssm_lambda_re, m_ssm_lambda_im, m_ssm_log_dt, m_ssm_b_re, m_ssm_b_im, m_ssm_c_re, m_ssm_c_im, m_ssm_d, m_glu_w, m_glu_b, m_w_ssm_out, m_conv_w, m_w_conv_out, m_w_o, m_ln1_g, m_ln1_b, m_w_gate, m_w_up, m_w_down, m_ln2_g, m_ln2_b, v_w_in, v_b_in, v_ssm_lambda_re, v_ssm_lambda_im, v_ssm_log_dt, v_ssm_b_re, v_ssm_b_im, v_ssm_c_re, v_ssm_c_im, v_ssm_d, v_glu_w, v_glu_b, v_w_ssm_out, v_conv_w, v_w_conv_out, v_w_o, v_ln1_g, v_ln1_b, v_w_gate, v_w_up, v_w_down, v_ln2_g, v_ln2_b):
    given = dict(locals())
    xs = x[0]
    target = loss_target[0]

    tr = lambda a: jnp.swapaxes(a[0], 0, 1)
    win_s, glu_s, wso_s, wco_s, wo_s, wgT_s, wuT_s, wd_s = prep_weights(
        [w_in[0], glu_w[0], w_ssm_out[0], w_conv_out[0], w_o[0], tr(w_gate), tr(w_up), w_down[0]])
    (win_g,) = run_plan(GatherPlan([win_s], srcs=(0,)), "gather_w_in_u")

    lam_re, lam_im = ssm_lambda_re[0], ssm_lambda_im[0]
    ldt = ssm_log_dt[0].reshape(NG, 1)
    br2 = jnp.swapaxes(ssm_b_re[0], 1, 2).reshape(NG * GC, NP)
    bi2 = jnp.swapaxes(ssm_b_im[0], 1, 2).reshape(NG * GC, NP)
    lbr, lbi, fr, fi, bbr, bbi = ssm_params(lam_re, lam_im, ldt, br2, bi2)
    bb_t = lambda b: b.reshape(4, 8, GC, NP)
    wb = jnp.concatenate([_block_diag(bb_t(bbr)), _block_diag(bb_t(bbi))], axis=2)
    c_t = lambda c: c.reshape(4, 8, GC, NP).transpose(0, 1, 3, 2)
    wc = jnp.concatenate([_block_diag(c_t(ssm_c_re[0])), -_block_diag(c_t(ssm_c_im[0]))], axis=1)
    wbT, wcT = wb.transpose(0, 2, 1), wc.transpose(0, 2, 1)
    wb, wc, wbT, wcT = wb.astype(bf16), wc.astype(bf16), wbT.astype(bf16), wcT.astype(bf16)
    lbr_s, lbi_s = lbr.reshape(4, 1, SW), lbi.reshape(4, 1, SW)
    dsk = ssm_d[0].reshape(4, 1, LANE)

    u_nat, xb = in_proj_u(xs, win_g, b_in)
    u_p = to_perm(u_nat, 0, "perm_u")
    (y_p,), (win_g, conv_g, glu_g, wso_g, wco_g) = ssm_fwd(
        u_p, wb, wc, lbr_s, lbi_s, dsk,
        Plans([GatherPlan([win_s], srcs=tuple(range(1, NDEV)), into=[win_g]),
               GatherPlan([conv_w[0], glu_s, wso_s, wco_s])]))
    conv_f = conv_g.transpose(1, 0, 2).reshape(3, W)
    (proj,), (wo_g, wgT_g) = in_proj_rest(xb, win_g, b_in, GatherPlan([wo_s, wgT_s]))
    glu_f, wo_f = glu_g.reshape(W, W), wo_g.reshape(D, D)
    (yn,), _ = from_perm(y_p, "unperm_y")
    ya = glu_fwd(yn, glu_f, glu_b)
    yb = conv_fwd(proj, conv_f)
    (merged,), (wuT_g,) = merge_fwd(ya, yb, wso_g, wco_g, proj, GatherPlan([wuT_s]))
    wgT, wuT = wgT_g.reshape(F, D), wuT_g.reshape(F, D)
    r1, x1b = mix_ln1(merged, wo_f, xs, ln1_g, ln1_b)
    (gate, up, hid), (wd_g,) = gate_up(x1b, wgT, wuT, GatherPlan([wd_s]))
    wd_f = wd_g.reshape(F, D)
    dr2, dffn, sqerr, dln2_g, dln2_b = down_loss(hid, wd_f, r1, ln1_g, ln1_b, ln2_g, ln2_b, target)

    half_a, half_b = (0, 3, 5, 6), (1, 2, 4, 7)
    dgate, dup = ffn_bwd_act(dffn, wd_f, gate, up)
    dwd, _ = mm_tn_rows(hid, dffn, "grad_w_down")
    dwd = dwd.reshape(NDEV, FS, D)
    dwgT, (r_wd,) = mm_tn_rows(dgate, x1b, "grad_w_gate", plan=ScatterPlan([dwd], only=half_a))
    dwuT, (r_wd,) = mm_tn_rows(dup, x1b, "grad_w_up", plan=ScatterPlan([dwd], only=half_b, into=[r_wd]))
    dwgT, dwuT = dwgT.reshape(NDEV, FS, D), dwuT.reshape(NDEV, FS, D)
    (dr1, dmix, dln1_g, dln1_b), (r_wgT,) = ffn_bwd_x(dgate, dup, wgT, wuT, dr2, r1, ln1_g, ScatterPlan([dwgT]))
    (dYA, dYB, dga, dgb, sga, sgb), (r_wuT,) = merge_bwd(dmix, wo_f, ya, yb, wso_g, wco_g, proj,
                                                         ScatterPlan([dwuT], only=half_a))
    dwo, _ = mm_tn_rows(merged, dmix, "grad_w_o")
    dwo = dwo.reshape(NDEV, D // NDEV, D)
    (dya, dyb), (r_wuT,) = branches_bwd_x(dYA, dYB, wso_g, wco_g, ScatterPlan([dwuT], only=half_b, into=[r_wuT]))
    dwso = branch_bwd_w(ya, dYA, "grad_w_ssm_out")
    dwco = branch_bwd_w(yb, dYB, "grad_w_conv_out")
    dyn, dsp, gb, dglu_b = glu_bwd(yn, dya, glu_f, glu_b)
    dglu = mm_tn_rows(gb, dsp, "grad_glu_w")[0].reshape(NDEV, W // NDEV, W)
    dh, dcg, dbg, dconv, shcb = conv_bwd(proj, dyb, conv_f)
    dwin = mm_tn(xb, dgb, "grad_w_in_gb", block0=6, nblocks=NDEV)
    dwin = mm_tn(xb, dga, "grad_w_in_ga", block0=4, into=dwin)
    dwin = mm_tn(xb, dbg, "grad_w_in_bg", block0=3, into=dwin)
    dwin = mm_tn(xb, dcg, "grad_w_in_cg", block0=2, into=dwin)
    dwin = mm_tn(xb, dh, "grad_w_in_h", block0=1, into=dwin)
    dy_p = to_perm(dyn, 0, "perm_dy")
    (du_p, dwb, dwcT, dlbr_s, dlbi_s, dd, su), (r_wo, r_wso, r_wco, r_glu, r_win) = ssm_bwd(
        u_p, dy_p, wb, wbT, wcT, lbr_s, lbi_s, dsk,
        Plans([ScatterPlan([dwo, dwso, dwco, dglu]), ScatterPlan([dwin], only=tuple(range(1, NDEV)))]))

    dbb = lambda m: _diag_blocks(m, GC, NP).reshape(NG * GC, NP)
    dbr2, dbi2, dlam_re, dlam_im, dldt = ssm_param_bwd(
        lam_re, lam_im, ldt, fr, fi, br2, bi2, dbb(dwb[:, :, :SW]), dbb(dwb[:, :, SW:]),
        dlbr_s.reshape(NG, NP), dlbi_s.reshape(NG, NP))
    packed = pack_grads(su, shcb, sga, sgb, dd, dglu_b, dln1_g, dln1_b, dln2_g, dln2_b, dlam_re, dlam_im, dldt, sqerr,
                        dbr2, dbi2, dbb(dwcT[:, :, :SW]), -dbb(dwcT[:, :, SW:]), dconv)
    (du,), _ = from_perm(du_p, "unperm_du", bf16)
    dwin = mm_tn(xb, du, "grad_w_in_u", block0=0, into=dwin)

    rest = [(dh, 0, 1), (dcg, 0, 2), (dbg, 0, 3), (dga, 0, 4), (dga, 1, 5), (dgb, 0, 6), (dgb, 1, 7)]
    (gx_rest,), (r_win, small_all) = in_proj_bwd_x(
        rest, win_g, dr1, ALPHA, "in_proj_bwd_x_rest",
        Plans([ScatterPlan([dwin], only=(0,), into=[r_win]), GatherPlan([packed])]))
    (grad_x,), _ = in_proj_bwd_x([(du, 0, 0)], win_g, gx_rest, 1.0, "in_proj_bwd_x_u")

    out = {}

    def put(name, res, back=lambda a: a[None]):
        out["grad_" + name], out["delta_" + name], out["new_m_" + name], out["new_v_" + name] = [back(r) for r in res]

    put("w_in", adam_update(w_in[0], m_w_in[0], v_w_in[0], r_win, "adam_w_in", 256))
    put("glu_w", adam_update(glu_w[0], m_glu_w[0], v_glu_w[0], r_glu, "adam_glu_w"))
    put("w_ssm_out", adam_update(w_ssm_out[0], m_w_ssm_out[0], v_w_ssm_out[0], r_wso, "adam_w_ssm_out"))
    put("w_conv_out", adam_update(w_conv_out[0], m_w_conv_out[0], v_w_conv_out[0], r_wco, "adam_w_conv_out"))
    put("w_o", adam_update(w_o[0], m_w_o[0], v_w_o[0], r_wo, "adam_w_o"))
    put("w_down", adam_update(w_down[0], m_w_down[0], v_w_down[0], r_wd, "adam_w_down", 176))
    untr = lambda a: jnp.swapaxes(a, 0, 1)[None]
    put("w_gate", adam_update(tr(w_gate), tr(m_w_gate), tr(v_w_gate), r_wgT, "adam_w_gate", 176), untr)
    put("w_up", adam_update(tr(w_up), tr(m_w_up), tr(v_w_up), r_wuT, "adam_w_up", 176), untr)
    as_c = lambda a: jnp.swapaxes(a, 2, 3)
    params = {n: (given[n], given["m_" + n], given["v_" + n]) for n in list(_SMALL) + ["conv_w"]}
    for n in ("ssm_b_re", "ssm_b_im"):
        params[n] = tuple(as_c(a) for a in params[n])
    small, loss = adam_small(small_all, params)
    for n, res in small.items():
        put(n, res, as_c if n in ("ssm_b_re", "ssm_b_im") else (lambda a: a))

    names = ["w_in", "b_in", "ssm_lambda_re", "ssm_lambda_im", "ssm_log_dt", "ssm_b_re", "ssm_b_im", "ssm_c_re", "ssm_c_im",
             "ssm_d", "glu_w", "glu_b", "w_ssm_out", "conv_w", "w_conv_out", "w_o", "ln1_g", "ln1_b", "w_gate", "w_up",
             "w_down", "ln2_g", "ln2_b"]
    return (loss.reshape(()), grad_x[None], *[out[p + n] for p in ("grad_", "delta_", "new_m_", "new_v_") for n in names])
```

```python
import functools
import math

import jax
import jax.numpy as jnp
from jax import lax
from jax.experimental import pallas as pl
from jax.experimental.pallas import tpu as pltpu

f32, bf16 = jnp.float32, jnp.bfloat16
S = jax.ShapeDtypeStruct
MESH = pl.DeviceIdType.MESH
HIGHEST = lax.Precision.HIGHEST

D = 1024
W = 512
NG, NP, GC = 32, 64, 16
F = 2816
NDEV = 8
FS = F // NDEV
IN_COLS = 8 * W
ALPHA = 2.0 ** 0.25
LN_EPS = 1e-5
ADAM_LR, ADAM_B1, ADAM_B2, ADAM_EPS, ADAM_WD, ADAM_STEP = 0.001, 0.9, 0.999, 1e-08, 0.01, 10
NC = 32
LANE = 128
SW = 4 * LANE
VMEM_LIMIT = 56 * 1024 * 1024
GRAD_DT = bf16
ANY = pl.BlockSpec(memory_space=pl.ANY)


def _cp(sem=None, vmem=None):
    return pltpu.CompilerParams(dimension_semantics=sem, vmem_limit_bytes=vmem)


def _resident(shape):
    return pl.BlockSpec(shape, lambda i: (0,) * len(shape), pipeline_mode=pl.Buffered(1))


def _dot(a, b):
    return jnp.dot(a, b, preferred_element_type=f32)


def _dot_nt(a, b):
    return lax.dot_general(a, b, (((1,), (1,)), ((), ())), preferred_element_type=f32)


def _dot_tn(a, b):
    return lax.dot_general(a, b, (((0,), (0,)), ((), ())), preferred_element_type=f32)


def _eye(n):
    return (lax.broadcasted_iota(jnp.int32, (n, n), 0) == lax.broadcasted_iota(jnp.int32, (n, n), 1)).astype(f32)


def _transpose_exact(a):
    return lax.dot_general(a, _eye(a.shape[0]), (((0,), (0,)), ((), ())), precision=HIGHEST, preferred_element_type=f32)


def _sigmoid(x):
    return 1.0 / (1.0 + jnp.exp(-x))


_GK = math.sqrt(2.0 / math.pi)


def _gelu(x):
    return 0.5 * x * (1.0 + jnp.tanh(_GK * (x + 0.044715 * x * x * x)))


def _gelu_grad(x):
    th = jnp.tanh(_GK * (x + 0.044715 * x * x * x))
    return 0.5 * (1.0 + th) + 0.5 * x * (1.0 - th * th) * _GK * (1.0 + 3.0 * 0.044715 * x * x)


ROW_PART = 256


def _row_parts(tm):
    return [slice(r, r + min(ROW_PART, tm)) for r in range(0, tm, min(ROW_PART, tm))]


def _ln_stats(r):
    mu = jnp.mean(r, axis=-1, keepdims=True)
    xc = r - mu
    var = jnp.mean(xc * xc, axis=-1, keepdims=True)
    rstd = lax.rsqrt(var + LN_EPS)
    return xc * rstd, rstd


def _ln_bwd(dy, xhat, rstd, g):
    dxh = dy * g
    m1 = jnp.mean(dxh, axis=-1, keepdims=True)
    m2 = jnp.mean(dxh * xhat, axis=-1, keepdims=True)
    return rstd * (dxh - m1 - xhat * m2)


def _coords():
    return lax.axis_index("x"), lax.axis_index("y"), lax.axis_index("c")


def _when(cond, fn):
    if cond is True:
        fn()
    else:
        pl.when(cond)(fn)


class GatherPlan:
    aliases = ()

    def __init__(self, arrs, srcs=None, into=None):
        n = self.n = len(arrs)
        self.srcs = srcs
        self.inputs = list(arrs) + list(into or [])
        if into:
            self.aliases = tuple((n + a, a) for a in range(n))
        self.out_shape = [S((NDEV,) + a.shape, a.dtype) for a in arrs]
        self.sems = [pltpu.SemaphoreType.DMA((n, 7)), pltpu.SemaphoreType.DMA((n, 7)), pltpu.SemaphoreType.DMA((n,))]

    def _has(self, dev):
        if self.srcs is None:
            return True
        idx = 4 * dev[0] + 2 * dev[1] + dev[2]
        return functools.reduce(jnp.logical_or, [idx == s for s in self.srcs])

    def _parts(self, ins, outs, sems):
        n = self.n
        send_sems, recv_sems, loc_sems = sems
        x, y, c = _coords()
        me, sib = (x, y, c), (x, y, 1 - c)
        chips = [(1 - x, y), (x, 1 - y), (1 - x, 1 - y)]

        def slot(a, dev):
            return outs[a].at[4 * dev[0] + 2 * dev[1] + dev[2]]

        def copy(a, k, block, to, src=None):
            return pltpu.make_async_remote_copy(
                src_ref=slot(a, block) if src is None else src, dst_ref=slot(a, block),
                send_sem=send_sems.at[a, k], recv_sem=recv_sems.at[a, k], device_id=to, device_id_type=MESH)

        each = [(j, chip, a) for j, chip in enumerate(chips) for a in range(n)]
        own = self._has(me)
        return dict(
            mine=lambda: [(pltpu.make_async_copy(ins[a], slot(a, me), loc_sems.at[a]), own) for a in range(n)],
            first=lambda: ([(copy(a, 0, me, sib, src=ins[a]), own) for a in range(n)]
                           + [(copy(a, 1 + j, me, (*chip, c), src=ins[a]), own) for j, chip, a in each]),
            landed=lambda: [(copy(a, 1 + j, (*chip, c), me), self._has((*chip, c))) for j, chip, a in each],
            passed=lambda: [(copy(a, 4 + j, (*chip, c), sib), self._has((*chip, c))) for j, chip, a in each],
            from_sib=lambda: ([(copy(a, 0, sib, me), self._has(sib)) for a in range(n)]
                              + [(copy(a, 4 + j, (*chip, 1 - c), me), self._has((*chip, 1 - c))) for j, chip, a in each]))

    def start(self, ins, outs, sems):
        p = self._parts(ins, outs, sems)
        for cp, cond in p["mine"]() + p["first"]():
            _when(cond, cp.start)

    def forward(self, ins, outs, sems):
        p = self._parts(ins, outs, sems)
        for (got, cond), (fwd, _) in zip(p["landed"](), p["passed"]()):
            def relay(got=got, fwd=fwd):
                got.wait_recv()
                fwd.start()

            _when(cond, relay)

    def finish(self, ins, outs, sems):
        p = self._parts(ins, outs, sems)
        for cp, cond in p["from_sib"]():
            _when(cond, cp.wait_recv)
        for cp, cond in p["first"]() + p["passed"]():
            _when(cond, cp.wait_send)
        for cp, cond in p["mine"]():
            _when(cond, cp.wait)


class ScatterPlan:
    aliases = ()

    def __init__(self, gs, only=None, into=None):
        n = self.n = len(gs)
        self.only = only
        self.inputs = list(gs) + list(into or [])
        if into:
            self.aliases = tuple((n + a, a) for a in range(n))
        self.out_shape = [S(g.shape, g.dtype) for g in gs]
        self.sems = [pltpu.SemaphoreType.DMA((n, 7)), pltpu.SemaphoreType.DMA((n, 7)), pltpu.SemaphoreType.DMA((n,))]

    def _owner(self, idx):
        if self.only is None:
            return True
        return functools.reduce(jnp.logical_or, [idx == b for b in self.only])

    def _copies(self, ins, outs, sems):
        n = self.n
        send_sems, recv_sems, loc_sems = sems
        x, y, c = _coords()
        me = 4 * x + 2 * y + c
        mine = self._owner(me)
        copies = [(pltpu.make_async_copy(ins[a].at[me], outs[a].at[me], loc_sems.at[a]), mine, None) for a in range(n)]
        for m in range(1, NDEV):
            px = 1 - x if m & 4 else x
            py = 1 - y if m & 2 else y
            pc = 1 - c if m & 1 else c
            peer = 4 * px + 2 * py + pc
            for a in range(n):
                copies.append((pltpu.make_async_remote_copy(
                    src_ref=ins[a].at[peer], dst_ref=outs[a].at[me],
                    send_sem=send_sems.at[a, m - 1], recv_sem=recv_sems.at[a, m - 1],
                    device_id=(px, py, pc), device_id_type=MESH), self._owner(peer), mine))
        return copies

    def start(self, ins, outs, sems):
        for cp, sends, _ in self._copies(ins, outs, sems):
            _when(sends, cp.start)

    def forward(self, ins, outs, sems):
        pass

    def finish(self, ins, outs, sems):
        for cp, sends, receives in self._copies(ins, outs, sems):
            if receives is None:
                _when(sends, cp.wait)
            else:
                _when(sends, cp.wait_send)
                _when(receives, cp.wait_recv)


class Plans:
    def __init__(self, plans):
        self.plans = plans
        self.inputs = [a for p in plans for a in p.inputs]
        self.out_shape = [s for p in plans for s in p.out_shape]
        self.sems = [s for p in plans for s in p.sems]
        self.aliases, i, o = [], 0, 0
        for p in plans:
            self.aliases += [(i + a, o + b) for a, b in p.aliases]
            i, o = i + len(p.inputs), o + len(p.out_shape)

    def _each(self, what, ins, outs, sems):
        i = o = s = 0
        for p in self.plans:
            ni, no, ns = len(p.inputs), len(p.out_shape), len(p.sems)
            getattr(p, what)(ins[i:i + ni], outs[o:o + no], sems[s:s + ns])
            i, o, s = i + ni, o + no, s + ns

    def start(self, ins, outs, sems):
        self._each("start", ins, outs, sems)

    def forward(self, ins, outs, sems):
        self._each("forward", ins, outs, sems)

    def finish(self, ins, outs, sems):
        self._each("finish", ins, outs, sems)


def _call(body, args, *, name, grid, in_specs, out_specs, out_shape, scratch=(), sem=None, vmem=None, plan=None,
          aliases=None):
    aliases = aliases or {}
    if plan is None:
        outs = pl.pallas_call(body, name=name, grid=grid, in_specs=list(in_specs), out_specs=list(out_specs),
                              out_shape=list(out_shape), scratch_shapes=list(scratch), input_output_aliases=aliases,
                              compiler_params=_cp(sem, vmem))(*args)
        return list(outs), []
    ni, no, ns = len(in_specs), len(out_specs), len(scratch)
    pi, po = len(plan.inputs), len(plan.out_shape)
    aliases = {**aliases, **{ni + a: no + b for a, b in plan.aliases}}

    def wrapped(*refs):
        main_in, p_in = refs[:ni], refs[ni:ni + pi]
        main_out, p_out = refs[ni + pi:ni + pi + no], refs[ni + pi + no:ni + pi + no + po]
        main_scr, p_sems = refs[ni + pi + no + po:ni + pi + no + po + ns], refs[ni + pi + no + po + ns:]
        ids = [pl.program_id(d) for d in range(len(grid))]
        first = functools.reduce(jnp.logical_and, [i == 0 for i in ids])
        last = functools.reduce(jnp.logical_and, [i == g - 1 for i, g in zip(ids, grid)])

        @pl.when(first)
        def _():
            plan.start(p_in, p_out, p_sems)

        @pl.when(last)
        def _():
            plan.forward(p_in, p_out, p_sems)

        body(*main_in, *main_out, *main_scr)

        @pl.when(last)
        def _():
            plan.finish(p_in, p_out, p_sems)

    outs = pl.pallas_call(
        wrapped, name=name, grid=grid, in_specs=list(in_specs) + [ANY] * pi, out_specs=list(out_specs) + [ANY] * po,
        out_shape=list(out_shape) + list(plan.out_shape), scratch_shapes=list(scratch) + list(plan.sems),
        input_output_aliases=aliases, compiler_params=_cp(("arbitrary",) * len(grid), vmem),
    )(*args, *plan.inputs)
    return list(outs[:no]), list(outs[no:])


HBM = pl.BlockSpec(memory_space=pltpu.HBM)
SEM = pl.BlockSpec(memory_space=pltpu.SEMAPHORE)
_IN_FLIGHT = pltpu.SideEffectType.DATAFLOW_SIDE_EFFECTING


def own_slots(dwin, r_win, packed):
    def body(dwin_ref, rwin_ref, packed_ref, rwin_out, all_out, sems):
        del rwin_ref
        x, y, c = _coords()
        me = 4 * x + 2 * y + c
        mine = pltpu.make_async_copy(packed_ref, all_out.at[me], sems.at[0])
        mine.start()

        @pl.when(me == 0)
        def _():
            cp = pltpu.make_async_copy(dwin_ref.at[0], rwin_out.at[0], sems.at[1])
            cp.start()
            cp.wait()

        mine.wait()

    return pl.pallas_call(
        body, name="own_slots", in_specs=[ANY, ANY, ANY], out_specs=[ANY, ANY],
        out_shape=[S(r_win.shape, r_win.dtype), S((NDEV,) + packed.shape, packed.dtype)],
        scratch_shapes=[pltpu.SemaphoreType.DMA((2,))], input_output_aliases={1: 0})(dwin, r_win, packed)


def _tail_copies(dwin_ref, rwin_ref, packed_ref, all_ref, sems):
    sa, ra, sb, rb = sems
    x, y, c = _coords()
    me = 4 * x + 2 * y + c
    copies = []
    for m in range(1, NDEV):
        copies.append((pltpu.make_async_remote_copy(
            src_ref=dwin_ref.at[0], dst_ref=rwin_ref.at[m], send_sem=sa.at[m - 1], recv_sem=ra.at[m - 1],
            device_id=(0, 0, 0), device_id_type=MESH), me == m, me == 0))
        px = 1 - x if m & 4 else x
        py = 1 - y if m & 2 else y
        pc = 1 - c if m & 1 else c
        copies.append((pltpu.make_async_remote_copy(
            src_ref=packed_ref, dst_ref=all_ref.at[me], send_sem=sb.at[m - 1], recv_sem=rb.at[m - 1],
            device_id=(px, py, pc), device_id_type=MESH), True, True))
    return copies


def tail_start(dwin, r_win, packed, gathered):
    def body(dwin_ref, rwin_ref, packed_ref, all_ref, sa, ra, sb, rb, d_t, r_t, p_t, a_t, token):
        del d_t, r_t, p_t, a_t
        for cp, sends, _ in _tail_copies(dwin_ref, rwin_ref, packed_ref, all_ref, (sa, ra, sb, rb)):
            _when(sends, cp.start)
        token[...] = jnp.zeros_like(token)

    sem = pltpu.SemaphoreType.DMA((NDEV - 1,))
    hbm = lambda a: pltpu.HBM(a.shape, a.dtype)
    return pl.pallas_call(
        body, name="tail_start", in_specs=[HBM] * 4,
        out_specs=[SEM] * 4 + [HBM] * 4 + [pl.BlockSpec(memory_space=pltpu.VMEM)],
        out_shape=[sem] * 4 + [hbm(dwin), hbm(r_win), hbm(packed), hbm(gathered), S((8, LANE), f32)],
        input_output_aliases={0: 4, 1: 5, 2: 6, 3: 7}, compiler_params=pltpu.CompilerParams(has_side_effects=_IN_FLIGHT),
    )(*[pltpu.with_memory_space_constraint(a, pltpu.HBM) for a in (dwin, r_win, packed, gathered)])


def tail_wait(sems, bufs, after):
    def body(dwin_ref, rwin_ref, packed_ref, all_ref, sa, ra, sb, rb, after_ref, d_o, r_o, p_o, a_o):
        del after_ref, d_o, r_o, p_o, a_o
        for cp, sends, receives in _tail_copies(dwin_ref, rwin_ref, packed_ref, all_ref, (sa, ra, sb, rb)):
            _when(sends, cp.wait_send)
            _when(receives, cp.wait_recv)

    outs = pl.pallas_call(
        body, name="tail_wait", in_specs=[HBM] * 4 + [SEM] * 4 + [ANY], out_specs=[HBM] * 4,
        out_shape=[pltpu.HBM(b.shape, b.dtype) for b in bufs],
        input_output_aliases={0: 0, 1: 1, 2: 2, 3: 3}, compiler_params=pltpu.CompilerParams(has_side_effects=_IN_FLIGHT),
    )(*bufs, *sems, after)
    return outs[1], outs[3]


def run_plan(plan, name):
    def body(*refs):
        ins, outs, sems = refs[:len(plan.inputs)], refs[len(plan.inputs):len(plan.inputs) + len(plan.out_shape)], \
            refs[len(plan.inputs) + len(plan.out_shape):]
        plan.start(ins, outs, sems)
        plan.forward(ins, outs, sems)
        plan.finish(ins, outs, sems)

    return pl.pallas_call(body, name=name, in_specs=[ANY] * len(plan.inputs), out_specs=[ANY] * len(plan.out_shape),
                          out_shape=list(plan.out_shape), scratch_shapes=list(plan.sems))(*plan.inputs)


def mm_tn(a, b, name, tn=512, into=None, block0=0, nblocks=None):
    T, K = a.shape
    N = b.shape[1]
    tn = min(tn, N)
    nblocks = nblocks or (N // tn if into is None else into.shape[0])

    def body(a_ref, b_ref, *rest):
        rest[-1][...] = _dot_tn(a_ref[...], b_ref[...]).astype(GRAD_DT)

    args, in_specs, aliases = [a, b], [_resident((T, K)), pl.BlockSpec((T, tn), lambda j: (0, j))], {}
    if into is not None:
        args.append(into)
        in_specs.append(ANY)
        aliases = {2: 0}
    (out,), _ = _call(body, args, name=name, grid=(N // tn,), in_specs=in_specs,
                      out_specs=[pl.BlockSpec((None, K, tn), lambda j: (block0 + j, 0, 0))],
                      out_shape=[S((nblocks, K, tn), GRAD_DT)], sem=("parallel",), vmem=VMEM_LIMIT, aliases=aliases)
    return out


def mm_tn_rows(a, b, name, tk=256, plan=None):
    T, K = a.shape
    N = b.shape[1]
    tk = min(tk, K)

    def body(a_ref, b_ref, o_ref):
        o_ref[...] = _dot_tn(a_ref[...], b_ref[...]).astype(GRAD_DT)

    (out,), sent = _call(body, [a, b], name=name, grid=(K // tk,),
                         in_specs=[pl.BlockSpec((T, tk), lambda i: (0, i)), _resident((T, N))],
                         out_specs=[pl.BlockSpec((tk, N), lambda i: (i, 0))], out_shape=[S((K, N), GRAD_DT)],
                         sem=("parallel",), vmem=VMEM_LIMIT, plan=plan)
    return out, sent


def prep_weights(ws):
    def body(*refs):
        for i in range(len(ws)):
            refs[len(ws) + i][...] = refs[i][...].astype(bf16)

    return pl.pallas_call(body, name="prep_weights", out_shape=[S(w.shape, bf16) for w in ws],
                          compiler_params=_cp(None, VMEM_LIMIT))(*ws)


REST_BLOCKS = (4, 5, 6, 7, 1, 2, 3)
REST_COLS = len(REST_BLOCKS) * W


def in_proj_u(x, win_g, b_in):
    T = x.shape[0]
    tm = min(1024, T)

    def body(x_ref, w_ref, b_ref, u_ref, xb_ref):
        xb = x_ref[...].astype(bf16)
        xb_ref[...] = xb
        u_ref[...] = _dot(xb, w_ref[...]) + b_ref[...]

    row = pl.BlockSpec((tm, D), lambda i: (i, 0))
    return pl.pallas_call(
        body, name="in_proj_u", grid=(T // tm,),
        in_specs=[row, pl.BlockSpec((None, D, W), lambda i: (0, 0, 0)), pl.BlockSpec((1, W), lambda i: (0, 0))],
        out_specs=[pl.BlockSpec((tm, W), lambda i: (i, 0)), row],
        out_shape=[S((T, W), f32), S((T, D), bf16)], compiler_params=_cp(("parallel",), VMEM_LIMIT),
    )(x, win_g, b_in)


def in_proj_rest(xb, win_g, b_in, plan):
    T = xb.shape[0]
    tm = min(512, T)

    def body(x_ref, w_ref, b_ref, o_ref):
        xb_ = x_ref[...]
        for i, k in enumerate(REST_BLOCKS):
            o_ref[:, i * W:(i + 1) * W] = _dot(xb_, w_ref[k]) + b_ref[:, k * W:(k + 1) * W]

    return _call(
        body, [xb, win_g, b_in], name="in_proj_rest", grid=(T // tm,),
        in_specs=[pl.BlockSpec((tm, D), lambda i: (i, 0)), _resident((NDEV, D, W)), _resident((1, IN_COLS))],
        out_specs=[pl.BlockSpec((tm, REST_COLS), lambda i: (i, 0))],
        out_shape=[S((T, REST_COLS), f32)], vmem=VMEM_LIMIT, plan=plan)


def to_perm(a, cb0, name):
    T = a.shape[0]
    L = T // NC

    def body(a_ref, o_ref):
        def step(jb, carry):
            j0 = pl.multiple_of(jb * 8, 8)
            for q in range(NC // 8):
                x = jnp.stack([a_ref[pl.ds((8 * q + c) * L + j0, 8), :] for c in range(8)], axis=0)
                y = jnp.swapaxes(x, 0, 1)
                for j in range(8):
                    o_ref[pl.ds((j0 + j) * NC + 8 * q, 8), :] = y[j]
            return carry

        lax.fori_loop(0, L // 8, step, 0)

    return pl.pallas_call(
        body, name=name, grid=(W // LANE,),
        in_specs=[pl.BlockSpec((T, LANE), lambda k: (0, cb0 + k))], out_specs=pl.BlockSpec((T, LANE), lambda k: (0, k)),
        out_shape=S((T, W), f32), compiler_params=_cp(("parallel",), VMEM_LIMIT),
    )(a)


def from_perm(a, name, out_dtype=f32, plan=None):
    T = a.shape[0]
    L = T // NC

    def body(a_ref, o_ref):
        def step(jb, carry):
            j0 = pl.multiple_of(jb * 16, 16)
            for q in range(NC // 8):
                halves = []
                for h in range(2):
                    x = jnp.stack([a_ref[pl.ds((j0 + 8 * h + j) * NC + 8 * q, 8), :] for j in range(8)], axis=0)
                    halves.append(jnp.swapaxes(x, 0, 1))
                for c in range(8):
                    o_ref[pl.ds((8 * q + c) * L + j0, 16), :] = jnp.concatenate(
                        [halves[0][c], halves[1][c]], axis=0).astype(out_dtype)
            return carry

        lax.fori_loop(0, L // 16, step, 0)

    slab = pl.BlockSpec((T, LANE), lambda k: (0, k))
    return _call(body, [a], name=name, grid=(W // LANE,), in_specs=[slab], out_specs=[slab],
                 out_shape=[S((T, W), out_dtype)], sem=("parallel",), vmem=VMEM_LIMIT, plan=plan)


def _disc(lr, li, ldt):
    dt = jnp.exp(ldt)
    mag = jnp.exp(lr * dt)
    lbr = mag * jnp.cos(li * dt)
    lbi = mag * jnp.sin(li * dt)
    den = lr * lr + li * li
    nr = lbr - 1.0
    return lbr, lbi, (nr * lr + lbi * li) / den, (lbi * lr - nr * li) / den


def _per_channel(f):
    return jnp.broadcast_to(f[:, None, :], (NG, GC, NP)).reshape(NG * GC, NP)


def ssm_params(lam_re, lam_im, log_dt, br, bi):
    def body(lr_ref, li_ref, ldt_ref, br_ref, bi_ref, lbr_ref, lbi_ref, fr_ref, fi_ref, bbr_ref, bbi_ref):
        lbr, lbi, fr, fi = _disc(lr_ref[...], li_ref[...], ldt_ref[...])
        lbr_ref[...], lbi_ref[...], fr_ref[...], fi_ref[...] = lbr, lbi, fr, fi
        fr_, fi_, br_, bi_ = _per_channel(fr), _per_channel(fi), br_ref[...], bi_ref[...]
        bbr_ref[...] = fr_ * br_ - fi_ * bi_
        bbi_ref[...] = fr_ * bi_ + fi_ * br_

    return pl.pallas_call(body, name="ssm_params", out_shape=[S((NG, NP), f32)] * 4 + [S((NG * GC, NP), f32)] * 2)(
        lam_re, lam_im, log_dt, br, bi)


SCAN_UNROLL = 4


def _steps(n, body, carry):
    main = n // SCAN_UNROLL

    def trip(t, c):
        for q in range(SCAN_UNROLL):
            c = body(t * SCAN_UNROLL + q, c)
        return c

    carry = lax.fori_loop(0, main, trip, carry)
    for i in range(main * SCAN_UNROLL, n):
        carry = body(i, carry)
    return carry


def _scan_body(T):
    L = T // NC
    RB = min(512, T)
    nsq = int(round(math.log2(L)))
    assert 2 ** nsq == L and T % RB == 0 and L % 16 == 0

    def rows(i):
        return pl.ds(pl.multiple_of(i * RB, RB), RB)

    def tile(j):
        return pl.ds(j * NC if isinstance(j, int) else pl.multiple_of(j * NC, NC), NC)

    def forward_states(u_ref, wb_ref, lbr_ref, lbi_ref, sre, sim, ere, eim):
        def bproj(i, carry):
            bu = _dot(u_ref[rows(i), :].astype(bf16), wb_ref[...])
            sre[rows(i), :] = bu[:, :SW]
            sim[rows(i), :] = bu[:, SW:]
            return carry

        lax.fori_loop(0, T // RB, bproj, 0)
        for lb in range(SW // LANE):
            ls = slice(lb * LANE, (lb + 1) * LANE)
            ar = jnp.broadcast_to(lbr_ref[:, ls], (NC, LANE))
            ai = jnp.broadcast_to(lbi_ref[:, ls], (NC, LANE))

            def step(j, carry):
                xr, xi = carry
                nr = ar * xr - ai * xi + sre[tile(j), ls]
                ni = ar * xi + ai * xr + sim[tile(j), ls]
                sre[tile(j), ls] = nr
                sim[tile(j), ls] = ni
                return nr, ni

            zero = jnp.zeros((NC, LANE), f32)
            _steps(L, step, (zero, zero))
            pr, pi = lbr_ref[:, ls], lbi_ref[:, ls]
            for _ in range(nsq):
                pr, pi = pr * pr - pi * pi, 2.0 * pr * pi
            er = jnp.zeros((1, LANE), f32)
            ei = er
            ere[0:1, ls] = er
            eim[0:1, ls] = ei
            base = (L - 1) * NC
            for c in range(1, NC):
                lr_ = sre[base + c - 1:base + c, ls]
                li_ = sim[base + c - 1:base + c, ls]
                er, ei = lr_ + pr * er - pi * ei, li_ + pr * ei + pi * er
                ere[c:c + 1, ls] = er
                eim[c:c + 1, ls] = ei
            e_r, e_i = ere[:, ls].reshape(NC // 8, 8, LANE), eim[:, ls].reshape(NC // 8, 8, LANE)
            ar8, ai8 = ar[0:8], ai[0:8]

            def fix(j, carry):
                pwr, pwi = carry
                xr = sre[tile(j), ls].reshape(NC // 8, 8, LANE) + (pwr * e_r - pwi * e_i)
                xi = sim[tile(j), ls].reshape(NC // 8, 8, LANE) + (pwr * e_i + pwi * e_r)
                sre[tile(j), ls] = xr.reshape(NC, LANE)
                sim[tile(j), ls] = xi.reshape(NC, LANE)
                return pwr * ar8 - pwi * ai8, pwr * ai8 + pwi * ar8

            _steps(L, fix, (ar8, ai8))

    return L, RB, nsq, rows, tile, forward_states


def ssm_fwd(u_p, wb, wc, lbr, lbi, dsk, plan):
    T = u_p.shape[0]
    L, RB, nsq, rows, tile, forward_states = _scan_body(T)

    def body(u_ref, wb_ref, wc_ref, lbr_ref, lbi_ref, d_ref, y_ref, sre, sim, ere, eim):
        forward_states(u_ref, wb_ref, lbr_ref, lbi_ref, sre, sim, ere, eim)

        def cproj(i, carry):
            y = _dot(sre[rows(i), :].astype(bf16), wc_ref[0:SW, :]) + _dot(sim[rows(i), :].astype(bf16), wc_ref[SW:, :])
            y_ref[rows(i), :] = y + d_ref[...] * u_ref[rows(i), :]
            return carry

        lax.fori_loop(0, T // RB, cproj, 0)

    slab = pl.BlockSpec((T, LANE), lambda k: (0, k))
    return _call(
        body, [u_p, wb, wc, lbr, lbi, dsk], name="ssm_fwd", grid=(W // LANE,),
        in_specs=[slab, pl.BlockSpec((None, LANE, 2 * SW), lambda k: (k, 0, 0)),
                  pl.BlockSpec((None, 2 * SW, LANE), lambda k: (k, 0, 0)),
                  pl.BlockSpec((None, 1, SW), lambda k: (k, 0, 0)), pl.BlockSpec((None, 1, SW), lambda k: (k, 0, 0)),
                  pl.BlockSpec((None, 1, LANE), lambda k: (k, 0, 0))],
        out_specs=[slab], out_shape=[S((T, W), f32)],
        scratch=[pltpu.VMEM((T, SW), f32), pltpu.VMEM((T, SW), f32), pltpu.VMEM((NC, SW), f32), pltpu.VMEM((NC, SW), f32)],
        vmem=VMEM_LIMIT, plan=plan)


def ssm_bwd(u_p, dy_p, wb, wbT, wcT, lbr, lbi, dsk, plan):
    T = u_p.shape[0]
    L, RB, nsq, rows, tile, forward_states = _scan_body(T)

    def body(u_ref, dy_ref, wb_ref, wbT_ref, wcT_ref, lbr_ref, lbi_ref, d_ref,
             du_ref, dwb_ref, dwc_ref, dlr_ref, dli_ref, dd_ref, su_ref, sre, sim, gre, gim, ere, eim):
        forward_states(u_ref, wb_ref, lbr_ref, lbi_ref, sre, sim, ere, eim)

        def dstate(i, carry):
            g = _dot(dy_ref[rows(i), :].astype(bf16), wcT_ref[...])
            gre[rows(i), :] = g[:, :SW]
            gim[rows(i), :] = g[:, SW:]
            return carry

        lax.fori_loop(0, T // RB, dstate, 0)
        row = lax.broadcasted_iota(jnp.int32, (NC, LANE), 0)
        for lb in range(SW // LANE):
            ls = slice(lb * LANE, (lb + 1) * LANE)
            ar = jnp.broadcast_to(lbr_ref[:, ls], (NC, LANE))
            ai = jnp.broadcast_to(lbi_ref[:, ls], (NC, LANE))

            def step(i, carry):
                gr, gi = carry
                j = L - 1 - i
                nr = ar * gr + ai * gi + gre[tile(j), ls]
                ni = ar * gi - ai * gr + gim[tile(j), ls]
                gre[tile(j), ls] = nr
                gim[tile(j), ls] = ni
                return nr, ni

            zero = jnp.zeros((NC, LANE), f32)
            _steps(L, step, (zero, zero))
            pr, pi = lbr_ref[:, ls], -lbi_ref[:, ls]
            for _ in range(nsq):
                pr, pi = pr * pr - pi * pi, 2.0 * pr * pi
            er = jnp.zeros((1, LANE), f32)
            ei = er
            ere[NC - 1:NC, ls] = er
            eim[NC - 1:NC, ls] = ei
            for c in range(NC - 2, -1, -1):
                lr_ = gre[c + 1:c + 2, ls]
                li_ = gim[c + 1:c + 2, ls]
                er, ei = lr_ + pr * er - pi * ei, li_ + pr * ei + pi * er
                ere[c:c + 1, ls] = er
                eim[c:c + 1, ls] = ei
            e_r, e_i = ere[:, ls].reshape(NC // 8, 8, LANE), eim[:, ls].reshape(NC // 8, 8, LANE)
            ar8, ai8 = ar[0:8], ai[0:8]

            def fixed(j, pwr, pwi):
                gr = (gre[tile(j), ls].reshape(NC // 8, 8, LANE) + (pwr * e_r - pwi * e_i)).reshape(NC, LANE)
                gi = (gim[tile(j), ls].reshape(NC // 8, 8, LANE) + (pwr * e_i + pwi * e_r)).reshape(NC, LANE)
                gre[tile(j), ls] = gr
                gim[tile(j), ls] = gi
                return gr, gi

            def fix(i, carry):
                pwr, pwi, accr, acci = carry
                j = L - 1 - i
                gr, gi = fixed(j, pwr, pwi)
                xr, xi = sre[tile(j - 1), ls], sim[tile(j - 1), ls]
                return (pwr * ar8 + pwi * ai8, pwi * ar8 - pwr * ai8,
                        accr + gr * xr + gi * xi, acci + gi * xr - gr * xi)

            pwr, pwi, accr, acci = _steps(L - 1, fix, (ar8, -ai8, zero, zero))
            gr, gi = fixed(0, pwr, pwi)
            xr = jnp.where(row == 0, 0.0, pltpu.roll(sre[tile(L - 1), ls], 1, axis=0))
            xi = jnp.where(row == 0, 0.0, pltpu.roll(sim[tile(L - 1), ls], 1, axis=0))
            accr = accr + gr * xr + gi * xi
            acci = acci + gi * xr - gr * xi
            dlr_ref[:, ls] = jnp.sum(accr, axis=0, keepdims=True)
            dli_ref[:, ls] = jnp.sum(acci, axis=0, keepdims=True)

        dwb_ref[...] = jnp.zeros_like(dwb_ref)
        dwc_ref[...] = jnp.zeros_like(dwc_ref)
        dd_ref[...] = jnp.zeros_like(dd_ref)
        su_ref[...] = jnp.zeros_like(su_ref)

        def finish(i, carry):
            u32, dy32 = u_ref[rows(i), :], dy_ref[rows(i), :]
            ub, dyb = u32.astype(bf16), dy32.astype(bf16)
            gr, gi = gre[rows(i), :].astype(bf16), gim[rows(i), :].astype(bf16)
            du = _dot(gr, wbT_ref[0:SW, :]) + _dot(gi, wbT_ref[SW:, :]) + dy32 * d_ref[...]
            du_ref[rows(i), :] = du
            su_ref[...] += jnp.sum(du, axis=0, keepdims=True)
            dwb_ref[:, 0:SW] += _dot_tn(ub, gr)
            dwb_ref[:, SW:] += _dot_tn(ub, gi)
            dwc_ref[:, 0:SW] += _dot_tn(dyb, sre[rows(i), :].astype(bf16))
            dwc_ref[:, SW:] += _dot_tn(dyb, sim[rows(i), :].astype(bf16))
            dd_ref[...] += jnp.sum(dy32 * u32, axis=0, keepdims=True)
            return carry

        lax.fori_loop(0, T // RB, finish, 0)

    slab = pl.BlockSpec((T, LANE), lambda k: (0, k))
    wide = pl.BlockSpec((None, LANE, 2 * SW), lambda k: (k, 0, 0))
    tall = pl.BlockSpec((None, 2 * SW, LANE), lambda k: (k, 0, 0))
    vec = pl.BlockSpec((None, 1, SW), lambda k: (k, 0, 0))
    vecd = pl.BlockSpec((None, 1, LANE), lambda k: (k, 0, 0))
    nslab = W // LANE
    return _call(
        body, [u_p, dy_p, wb, wbT, wcT, lbr, lbi, dsk], name="ssm_bwd", grid=(nslab,),
        in_specs=[slab, slab, wide, tall, wide, vec, vec, vecd],
        out_specs=[slab, wide, wide, vec, vec, vecd, vecd],
        out_shape=[S((T, W), f32), S((nslab, LANE, 2 * SW), f32), S((nslab, LANE, 2 * SW), f32),
                   S((nslab, 1, SW), f32), S((nslab, 1, SW), f32), S((nslab, 1, LANE), f32), S((nslab, 1, LANE), f32)],
        scratch=[pltpu.VMEM((T, SW), f32)] * 4 + [pltpu.VMEM((NC, SW), f32)] * 2, vmem=VMEM_LIMIT, plan=plan)


def glu_fwd(yn, glu_w, glu_b):
    T = yn.shape[0]
    tm = min(512, T)

    def body(y_ref, w_ref, b_ref, o_ref):
        g = _gelu(y_ref[...])
        o_ref[...] = (g * _sigmoid(_dot(g.astype(bf16), w_ref[...]) + b_ref[...])).astype(bf16)

    return pl.pallas_call(
        body, name="glu_fwd", grid=(T // tm,),
        in_specs=[pl.BlockSpec((tm, W), lambda i: (i, 0)), pl.BlockSpec((W, W), lambda i: (0, 0)), pl.BlockSpec((1, W), lambda i: (0, 0))],
        out_specs=pl.BlockSpec((tm, W), lambda i: (i, 0)), out_shape=S((T, W), bf16), compiler_params=_cp(("parallel",)),
    )(yn, glu_w, glu_b)


def _shift_rows(cur, prev8, k):
    return pltpu.roll(jnp.concatenate([prev8, cur], axis=0), k, axis=0)[8:]


def _lift_rows(cur, next8, k):
    n = cur.shape[0]
    return pltpu.roll(jnp.concatenate([cur, next8], axis=0), n + 8 - k, axis=0)[:n]


def conv_fwd(proj, conv_w):
    T = proj.shape[0]
    RB = min(512, T)

    def body(h_ref, c_ref, b_ref, w_ref, o_ref):
        w0, w1, w2 = w_ref[0:1, :], w_ref[1:2, :], w_ref[2:3, :]

        def blk(i, carry):
            r0 = pl.multiple_of(i * RB, RB)
            rs = pl.ds(r0, RB)
            ch = c_ref[rs, :] * h_ref[rs, :]
            pr = pl.ds(jnp.maximum(r0 - 8, 0), 8)
            prev = jnp.where(i > 0, c_ref[pr, :] * h_ref[pr, :], 0.0)
            z = w2 * ch + w1 * _shift_rows(ch, prev, 1) + w0 * _shift_rows(ch, prev, 2)
            o_ref[rs, :] = (b_ref[rs, :] * z).astype(bf16)
            return carry

        lax.fori_loop(0, T // RB, blk, 0)

    nb = W // LANE
    return pl.pallas_call(
        body, name="conv_fwd", grid=(nb,),
        in_specs=[pl.BlockSpec((T, LANE), lambda k: (0, 4 * nb + k)), pl.BlockSpec((T, LANE), lambda k: (0, 5 * nb + k)),
                  pl.BlockSpec((T, LANE), lambda k: (0, 6 * nb + k)),pl.BlockSpec((3, LANE), lambda k: (0, k))],
        out_specs=pl.BlockSpec((T, LANE), lambda k: (0, k)), out_shape=S((T, W), bf16),
        compiler_params=_cp(("parallel",), VMEM_LIMIT),
    )(proj, proj, proj, conv_w)


def _dense_columns(blocks_ref, dense_ref):
    for k in range(NDEV):
        dense_ref[:, k * LANE:(k + 1) * LANE] = blocks_ref[k]


def merge_fwd(ya, yb, wso, wco, proj, plan):
    T = ya.shape[0]
    tm = min(1024, T)

    def body(ya_ref, yb_ref, wa_ref, wb_ref, ga_ref, gb_ref, o_ref, wa_s, wb_s):
        @pl.when(pl.program_id(0) == 0)
        def _():
            _dense_columns(wa_ref, wa_s)
            _dense_columns(wb_ref, wb_s)

        o_ref[...] = (_sigmoid(ga_ref[...]) * _dot(ya_ref[...], wa_s[...])
                      + _sigmoid(gb_ref[...]) * _dot(yb_ref[...], wb_s[...])).astype(bf16)

    act = pl.BlockSpec((tm, W), lambda i: (i, 0))
    return _call(
        body, [ya, yb, wso, wco, proj, proj], name="merge_fwd", grid=(T // tm,),
        in_specs=[act, act, _resident((NDEV, W, LANE)), _resident((NDEV, W, LANE)),
                  pl.BlockSpec((tm, D), lambda i: (i, 0)), pl.BlockSpec((tm, D), lambda i: (i, 1))],
        out_specs=[pl.BlockSpec((tm, D), lambda i: (i, 0))], out_shape=[S((T, D), bf16)],
        scratch=[pltpu.VMEM((W, D), bf16), pltpu.VMEM((W, D), bf16)], vmem=VMEM_LIMIT, plan=plan)


def mix_ln1(merged, w_o, x, g1, b1):
    T = x.shape[0]
    tm = min(512, T)

    def body(m_ref, w_ref, x_ref, g_ref, b_ref, r_ref, x1_ref):
        for rs in _row_parts(tm):
            r = ALPHA * x_ref[rs, :] + _dot(m_ref[rs, :], w_ref[...])
            r_ref[rs, :] = r
            xhat, _ = _ln_stats(r)
            x1_ref[rs, :] = (xhat * g_ref[...] + b_ref[...]).astype(bf16)

    row = pl.BlockSpec((tm, D), lambda i: (i, 0))
    vec = pl.BlockSpec((1, D), lambda i: (0, 0))
    return pl.pallas_call(
        body, name="mix_ln1", grid=(T // tm,),
        in_specs=[row, _resident((D, D)), row, vec, vec],
        out_specs=[row, row], out_shape=[S((T, D), f32), S((T, D), bf16)], compiler_params=_cp(("parallel",), VMEM_LIMIT),
    )(merged, w_o, x, g1, b1)


FT = 256


def gate_up(x1b, wgT, wuT, plan):
    T = x1b.shape[0]
    tm = min(512, T)

    def body(x_ref, wg_ref, wu_ref, g_ref, u_ref, h_ref):
        x = x_ref[...]
        for n in range(F // FT):
            cs = slice(n * FT, (n + 1) * FT)
            g = _dot_nt(x, wg_ref[cs, :])
            u = _dot_nt(x, wu_ref[cs, :])
            g_ref[:, cs] = g.astype(bf16)
            u_ref[:, cs] = u.astype(bf16)
            h_ref[:, cs] = (g * _sigmoid(g) * u).astype(bf16)

    osp = pl.BlockSpec((tm, F), lambda i: (i, 0))
    return _call(
        body, [x1b, wgT, wuT], name="gate_up", grid=(T // tm,),
        in_specs=[pl.BlockSpec((tm, D), lambda i: (i, 0)), _resident((F, D)), _resident((F, D))],
        out_specs=[osp, osp, osp], out_shape=[S((T, F), bf16)] * 3, vmem=VMEM_LIMIT, plan=plan)


def down_loss(hid, w_down, r1, g1, b1, g2, b2, target):
    T = hid.shape[0]
    tm = min(512, T)

    def body(h_ref, w_ref, r1_ref, g1_ref, b1_ref, g2_ref, b2_ref, t_ref, dr_ref, drb_ref, loss_ref, dg_ref, db_ref):
        @pl.when(pl.program_id(0) == 0)
        def _():
            loss_ref[...] = jnp.zeros_like(loss_ref)
            dg_ref[...] = jnp.zeros_like(dg_ref)
            db_ref[...] = jnp.zeros_like(db_ref)

        for rs in _row_parts(tm):
            xh1, _ = _ln_stats(r1_ref[rs, :])
            x1 = xh1 * g1_ref[...] + b1_ref[...]
            r2 = ALPHA * x1 + _dot(h_ref[rs, :], w_ref[...])
            xh2, rstd2 = _ln_stats(r2)
            err = xh2 * g2_ref[...] + b2_ref[...] - t_ref[rs, :]
            loss_ref[...] += jnp.sum(jnp.mean(err * err, axis=-1, keepdims=True), axis=0, keepdims=True)
            dy = err * (1.0 / D)
            dg_ref[...] += jnp.sum(dy * xh2, axis=0, keepdims=True)
            db_ref[...] += jnp.sum(dy, axis=0, keepdims=True)
            dr = _ln_bwd(dy, xh2, rstd2, g2_ref[...])
            dr_ref[rs, :] = dr
            drb_ref[rs, :] = dr.astype(bf16)

    row = pl.BlockSpec((tm, D), lambda i: (i, 0))
    vec = pl.BlockSpec((1, D), lambda i: (0, 0))
    return pl.pallas_call(
        body, name="down_loss", grid=(T // tm,),
        in_specs=[pl.BlockSpec((tm, F), lambda i: (i, 0)), _resident((F, D)), row, vec, vec, vec, vec, row],
        out_specs=[row, row, pl.BlockSpec((1, 1), lambda i: (0, 0)), vec, vec],
        out_shape=[S((T, D), f32), S((T, D), bf16), S((1, 1), f32), S((1, D), f32), S((1, D), f32)],
        compiler_params=_cp(("arbitrary",), VMEM_LIMIT),
    )(hid, w_down, r1, g1, b1, g2, b2, target)


def ffn_bwd_act(dffn, w_down, gate, up):
    T = dffn.shape[0]
    tm = min(512, T)

    def body(d_ref, w_ref, g_ref, u_ref, dg_ref, du_ref):
        for n in range(F // FT):
            cs = slice(n * FT, (n + 1) * FT)
            for rs in _row_parts(tm):
                dh = _dot_nt(d_ref[rs, :], w_ref[cs, :])
                g, u = g_ref[rs, cs].astype(f32), u_ref[rs, cs].astype(f32)
                sg = _sigmoid(g)
                t = g * sg
                du_ref[rs, cs] = (dh * t).astype(bf16)
                dg_ref[rs, cs] = (dh * u * (sg + t - t * sg)).astype(bf16)

    osp = pl.BlockSpec((tm, F), lambda i: (i, 0))
    return pl.pallas_call(
        body, name="ffn_bwd_act", grid=(T // tm,),
        in_specs=[pl.BlockSpec((tm, D), lambda i: (i, 0)), _resident((F, D)), osp, osp],
        out_specs=[osp, osp], out_shape=[S((T, F), bf16)] * 2, compiler_params=_cp(("parallel",), VMEM_LIMIT),
    )(dffn, w_down, gate, up)


def ffn_bwd_x(dgate, dup, wgT, wuT, dr2, r1, g1, plan):
    T = dr2.shape[0]
    tm = min(512, T)

    def body(dg_ref, du_ref, wg_ref, wu_ref, dr2_ref, r1_ref, g1_ref, dr_ref, drb_ref, dgam_ref, dbet_ref):
        @pl.when(pl.program_id(0) == 0)
        def _():
            dgam_ref[...] = jnp.zeros_like(dgam_ref)
            dbet_ref[...] = jnp.zeros_like(dbet_ref)

        for rs in _row_parts(tm):
            dx1 = ALPHA * dr2_ref[rs, :] + _dot(dg_ref[rs, :], wg_ref[...]) + _dot(du_ref[rs, :], wu_ref[...])
            xh, rstd = _ln_stats(r1_ref[rs, :])
            dgam_ref[...] += jnp.sum(dx1 * xh, axis=0, keepdims=True)
            dbet_ref[...] += jnp.sum(dx1, axis=0, keepdims=True)
            dr = _ln_bwd(dx1, xh, rstd, g1_ref[...])
            dr_ref[rs, :] = dr
            drb_ref[rs, :] = dr.astype(bf16)

    row = pl.BlockSpec((tm, D), lambda i: (i, 0))
    wide = pl.BlockSpec((tm, F), lambda i: (i, 0))
    wsp = _resident((F, D))
    vec = pl.BlockSpec((1, D), lambda i: (0, 0))
    return _call(
        body, [dgate, dup, wgT, wuT, dr2, r1, g1], name="ffn_bwd_x", grid=(T // tm,),
        in_specs=[wide, wide, wsp, wsp, row, row, vec],
        out_specs=[row, row, vec, vec], out_shape=[S((T, D), f32), S((T, D), bf16), S((1, D), f32), S((1, D), f32)],
        vmem=VMEM_LIMIT, plan=plan)


def merge_bwd(dmix, w_o, ya, yb, wso, wco, proj, plan):
    T = dmix.shape[0]
    tm = min(512, T)

    def body(dm_ref, wo_ref, ya_ref, yb_ref, wa_ref, wb_ref, ga_ref, gb_ref, dya_ref, dyb_ref, dga_ref, dgb_ref, sa_ref, sb_ref,
             wa_s, wb_s):
        @pl.when(pl.program_id(0) == 0)
        def _():
            _dense_columns(wa_ref, wa_s)
            _dense_columns(wb_ref, wb_s)

        dmer = _dot_nt(dm_ref[...], wo_ref[...])
        sa, sb = _sigmoid(ga_ref[...]), _sigmoid(gb_ref[...])
        dya_ref[...] = (dmer * sa).astype(bf16)
        dyb_ref[...] = (dmer * sb).astype(bf16)
        dga = dmer * _dot(ya_ref[...], wa_s[...]) * sa * (1.0 - sa)
        dgb = dmer * _dot(yb_ref[...], wb_s[...]) * sb * (1.0 - sb)
        dga_ref[...] = dga.astype(bf16)
        dgb_ref[...] = dgb.astype(bf16)
        sa_ref[...] = jnp.sum(dga, axis=0, keepdims=True)
        sb_ref[...] = jnp.sum(dgb, axis=0, keepdims=True)

    act = pl.BlockSpec((tm, W), lambda i: (i, 0))
    osp = pl.BlockSpec((tm, D), lambda i: (i, 0))
    ssp = pl.BlockSpec((None, 1, D), lambda i: (i, 0, 0))
    return _call(
        body, [dmix, w_o, ya, yb, wso, wco, proj, proj], name="merge_bwd", grid=(T // tm,),
        in_specs=[osp, _resident((D, D)), act, act, _resident((NDEV, W, LANE)), _resident((NDEV, W, LANE)),
                  pl.BlockSpec((tm, D), lambda i: (i, 0)), pl.BlockSpec((tm, D), lambda i: (i, 1))],
        out_specs=[osp, osp, osp, osp, ssp, ssp],
        out_shape=[S((T, D), bf16)] * 4 + [S((T // tm, 1, D), f32)] * 2,
        scratch=[pltpu.VMEM((W, D), bf16), pltpu.VMEM((W, D), bf16)], vmem=VMEM_LIMIT, plan=plan)


def branches_bwd_x(dYA, dYB, wso, wco, plan):
    T = dYA.shape[0]
    tm = min(1024, T)

    def body(da_ref, db_ref, wa_ref, wb_ref, oa_ref, ob_ref, wa_s, wb_s):
        @pl.when(pl.program_id(0) == 0)
        def _():
            _dense_columns(wa_ref, wa_s)
            _dense_columns(wb_ref, wb_s)

        oa_ref[...] = _dot_nt(da_ref[...], wa_s[...])
        ob_ref[...] = _dot_nt(db_ref[...], wb_s[...])

    row = pl.BlockSpec((tm, D), lambda i: (i, 0))
    osp = pl.BlockSpec((tm, W), lambda i: (i, 0))
    return _call(
        body, [dYA, dYB, wso, wco], name="branches_bwd_x", grid=(T // tm,),
        in_specs=[row, row, _resident((NDEV, W, LANE)), _resident((NDEV, W, LANE))],
        out_specs=[osp, osp], out_shape=[S((T, W), f32)] * 2,
        scratch=[pltpu.VMEM((W, D), bf16), pltpu.VMEM((W, D), bf16)], vmem=VMEM_LIMIT, plan=plan)


def branch_bwd_w(act, dY, name):
    T = act.shape[0]
    tk = W // 2

    def body(a_ref, d_ref, o_ref):
        res = _dot_tn(a_ref[...], d_ref[...])
        for k in range(NDEV):
            o_ref[k] = res[:, k * LANE:(k + 1) * LANE].astype(o_ref.dtype)

    return pl.pallas_call(
        body, name=name, grid=(W // tk,),
        in_specs=[pl.BlockSpec((T, tk), lambda i: (0, i)), _resident((T, D))],
        out_specs=pl.BlockSpec((NDEV, tk, LANE), lambda i: (0, i, 0)), out_shape=S((NDEV, W, LANE), GRAD_DT),
        compiler_params=_cp(("parallel",), VMEM_LIMIT),
    )(act, dY)


def glu_bwd(yn, dya, glu_w, glu_b):
    T = yn.shape[0]
    tm = min(512, T)

    def body(y_ref, d_ref, w_ref, b_ref, dy_ref, dsp_ref, g_ref, db_ref):
        @pl.when(pl.program_id(0) == 0)
        def _():
            db_ref[...] = jnp.zeros_like(db_ref)

        y, dya_ = y_ref[...], d_ref[...]
        g = _gelu(y)
        gb = g.astype(bf16)
        s = _sigmoid(_dot(gb, w_ref[...]) + b_ref[...])
        dsp = dya_ * g * s * (1.0 - s)
        dspb = dsp.astype(bf16)
        dg = dya_ * s + _dot_nt(dspb, w_ref[...])
        dy_ref[...] = dg * _gelu_grad(y)
        dsp_ref[...] = dspb
        g_ref[...] = gb
        db_ref[...] += jnp.sum(dsp, axis=0, keepdims=True)

    row = pl.BlockSpec((tm, W), lambda i: (i, 0))
    vec = pl.BlockSpec((1, W), lambda i: (0, 0))
    return pl.pallas_call(
        body, name="glu_bwd", grid=(T // tm,),
        in_specs=[row, row, pl.BlockSpec((W, W), lambda i: (0, 0)), vec],
        out_specs=[row, row, row, vec], out_shape=[S((T, W), f32), S((T, W), bf16), S((T, W), bf16), S((1, W), f32)],
        compiler_params=_cp(("arbitrary",)),
    )(yn, dya, glu_w, glu_b)


def conv_bwd(proj, dyb, conv_w):
    T = proj.shape[0]
    RB = min(512, T)
    nrb = T // RB

    def body(h_ref, c_ref, b_ref, d_ref, w_ref, dh_ref, dc_ref, db_ref, dw_ref, s_ref):
        w0, w1, w2 = w_ref[0:1, :], w_ref[1:2, :], w_ref[2:3, :]

        def blk(i, carry):
            a0, a1, a2, sh, sc, sb = carry
            r0 = pl.multiple_of(i * RB, RB)
            rs = pl.ds(r0, RB)
            h, cg, bg, dyb_ = h_ref[rs, :], c_ref[rs, :], b_ref[rs, :], d_ref[rs, :]
            ch = cg * h
            pr = pl.ds(jnp.maximum(r0 - 8, 0), 8)
            prev = jnp.where(i > 0, c_ref[pr, :] * h_ref[pr, :], 0.0)
            ch1, ch2 = _shift_rows(ch, prev, 1), _shift_rows(ch, prev, 2)
            dbg = dyb_ * (w2 * ch + w1 * ch1 + w0 * ch2)
            db_ref[rs, :] = dbg.astype(bf16)
            dz = dyb_ * bg
            nx = pl.ds(jnp.minimum(r0 + RB, T - 8), 8)
            nxt = jnp.where(i < nrb - 1, d_ref[nx, :] * b_ref[nx, :], 0.0)
            dch = w2 * dz + w1 * _lift_rows(dz, nxt, 1) + w0 * _lift_rows(dz, nxt, 2)
            dcg, dh = dch * h, dch * cg
            dc_ref[rs, :] = dcg.astype(bf16)
            dh_ref[rs, :] = dh.astype(bf16)
            col = lambda v: jnp.sum(v, axis=0, keepdims=True)
            return (a0 + col(dz * ch2), a1 + col(dz * ch1), a2 + col(dz * ch), sh + col(dh), sc + col(dcg), sb + col(dbg))

        zero = jnp.zeros((1, LANE), f32)
        a0, a1, a2, sh, sc, sb = lax.fori_loop(0, nrb, blk, (zero,) * 6)
        dw_ref[0:1, :] = a0
        dw_ref[1:2, :] = a1
        dw_ref[2:3, :] = a2
        s_ref[0:1, :] = sh
        s_ref[1:2, :] = sc
        s_ref[2:3, :] = sb

    nb = W // LANE
    slab = pl.BlockSpec((T, LANE), lambda k: (0, k))
    three = pl.BlockSpec((3, LANE), lambda k: (0, k))
    return pl.pallas_call(
        body, name="conv_bwd", grid=(nb,),
        in_specs=[pl.BlockSpec((T, LANE), lambda k: (0, 4 * nb + k)), pl.BlockSpec((T, LANE), lambda k: (0, 5 * nb + k)),
                  pl.BlockSpec((T, LANE), lambda k: (0, 6 * nb + k)),slab, three],
        out_specs=[slab, slab, slab, three, three],
        out_shape=[S((T, W), bf16)] * 3 + [S((3, W), f32)] * 2, compiler_params=_cp(("parallel",), VMEM_LIMIT),
    )(proj, proj, proj, dyb, conv_w)


def in_proj_bwd_x(parts, win_g, base, scale, name, after=None):
    T = base.shape[0]
    tm = min(512, T)
    n = len(parts)

    def body(*refs):
        p_refs, w_ref, b_ref, o_ref = refs[:n], refs[n], refs[n + 1], refs[-1]
        acc = scale * b_ref[...]
        for p_ref, (_, _, k) in zip(p_refs, parts):
            acc += _dot_nt(p_ref[...], w_ref[k])
        o_ref[...] = acc

    row = pl.BlockSpec((tm, D), lambda i: (i, 0))
    p_specs = [pl.BlockSpec((tm, W), (lambda i, cb=cb: (i, cb))) for _, cb, _ in parts]
    extra = [] if after is None else [after]
    (out,), _ = _call(
        body, [a for a, _, _ in parts] + [win_g, base] + extra, name=name, grid=(T // tm,),
        in_specs=p_specs + [_resident((NDEV, D, W)), row] + [ANY] * len(extra),
        out_specs=[row], out_shape=[S((T, D), f32)], sem=("parallel",), vmem=VMEM_LIMIT)
    return out


def ssm_param_bwd(lam_re, lam_im, log_dt, fr, fi, br, bi, dbbr, dbbi, dlbr, dlbi):
    def body(lr_ref, li_ref, ldt_ref, fr_ref, fi_ref, br_ref, bi_ref, dr_ref, di_ref, dlbr_ref, dlbi_ref,
             dbr_ref, dbi_ref, dlr_ref, dli_ref, dldt_ref):
        fr_, fi_ = _per_channel(fr_ref[...]), _per_channel(fi_ref[...])
        br_, bi_, dr, di = br_ref[...], bi_ref[...], dr_ref[...], di_ref[...]
        dbr_ref[...] = fr_ * dr + fi_ * di
        dbi_ref[...] = fr_ * di - fi_ * dr
        dfr = jnp.sum((dr * br_ + di * bi_).reshape(NG, GC, NP), axis=1)
        dfi = jnp.sum((di * br_ - dr * bi_).reshape(NG, GC, NP), axis=1)
        _, vjp = jax.vjp(_disc, lr_ref[...], li_ref[...], ldt_ref[...])
        dlr_ref[...], dli_ref[...], dldt = vjp((dlbr_ref[...], dlbi_ref[...], dfr, dfi))
        dldt_ref[...] = _transpose_exact(dldt)

    return pl.pallas_call(
        body, name="ssm_param_bwd",
        out_shape=[S((NG * GC, NP), f32)] * 2 + [S((NG, NP), f32)] * 2 + [S((1, NG), f32)])(
        lam_re, lam_im, log_dt, fr, fi, br, bi, dbbr, dbbi, dlbr, dlbi)


def _adam(w, g, m, v):
    m = ADAM_B1 * m + (1.0 - ADAM_B1) * g
    v = ADAM_B2 * v + (1.0 - ADAM_B2) * (g * g)
    m_hat = m / (1.0 - ADAM_B1 ** ADAM_STEP)
    v_hat = v / (1.0 - ADAM_B2 ** ADAM_STEP)
    return -ADAM_LR * (m_hat / (jnp.sqrt(v_hat) + ADAM_EPS) + ADAM_WD * w), m, v


def adam_update(w, m, v, contrib, name, rows_per_block=None):
    R, C = w.shape
    n = contrib.shape[0]
    tr = min(rows_per_block or R, R)

    def body(w_ref, m_ref, v_ref, c_ref, g_ref, d_ref, nm_ref, nv_ref):
        g = c_ref[0].astype(f32)
        for k in range(1, n):
            g = g + c_ref[k].astype(f32)
        g_ref[...] = g
        d_ref[...], nm_ref[...], nv_ref[...] = _adam(w_ref[...], g, m_ref[...], v_ref[...])

    blk = pl.BlockSpec((tr, C), lambda i: (i, 0))
    return pl.pallas_call(
        body, name=name, grid=(R // tr,), in_specs=[blk, blk, blk, pl.BlockSpec((n, tr, C), lambda i: (0, i, 0))],
        out_specs=[blk] * 4, out_shape=[S((R, C), f32)] * 4, compiler_params=_cp(("parallel",), VMEM_LIMIT),
    )(w, m, v, contrib)


_ROWVEC = (("b_in", IN_COLS), ("ssm_d", W), ("glu_b", W), ("ln1_g", D), ("ln1_b", D), ("ln2_g", D), ("ln2_b", D))
_HALF = NG * GC // 2
_BC_LANE = {"ssm_b_re": 0, "ssm_b_im": NP, "ssm_c_re": 0, "ssm_c_im": NP}
_PACK = {}
_r = 0
for _n, _k in _ROWVEC:
    _PACK[_n] = _r
    _r += _k // LANE
for _n, _rows in (("ssm_lambda", NG), ("scalars", 8), ("ssm_b", _HALF), ("ssm_c", _HALF), ("conv_w", 16)):
    _PACK[_n] = _r
    _r += _rows
for _n in _BC_LANE:
    _PACK[_n] = _PACK[_n[:5]]
PACK_ROWS = _r
assert PACK_ROWS % 8 == 0
_SMALL = ("b_in", "ssm_lambda_re", "ssm_lambda_im", "ssm_log_dt", "ssm_b_re", "ssm_b_im", "ssm_c_re", "ssm_c_im",
          "ssm_d", "glu_b", "ln1_g", "ln1_b", "ln2_g", "ln2_b")


def pack_grads(su, shcb, sga, sgb, dd, dglu_b, dln1_g, dln1_b, dln2_g, dln2_b, dlam_re, dlam_im, dldt, sqerr, dbr, dbi,
               dc_re, dc_im, dconv):
    nI = sga.shape[0]

    def body(su_ref, sh_ref, sga_ref, sgb_ref, dd_ref, gb_ref, l1g_ref, l1b_ref, l2g_ref, l2b_ref, lr_ref, li_ref, dt_ref,
             sq_ref, br_ref, bi_ref, cr_ref, ci_ref, cw_ref, o_ref):
        o_ref[...] = jnp.zeros_like(o_ref)

        def put_row(name, v):
            r0 = _PACK[name]
            for i in range(v.shape[1] // LANE):
                o_ref[r0 + i:r0 + i + 1, :] = v[:, i * LANE:(i + 1) * LANE]

        ga, gb = sga_ref[0], sgb_ref[0]
        for i in range(1, nI):
            ga, gb = ga + sga_ref[i], gb + sgb_ref[i]
        put_row("b_in", jnp.concatenate([su_ref[k] for k in range(W // LANE)]
                                        + [sh_ref[0:1, :], sh_ref[1:2, :], sh_ref[2:3, :], ga, gb], axis=1))
        put_row("ssm_d", jnp.concatenate([dd_ref[k] for k in range(W // LANE)], axis=1))
        put_row("glu_b", gb_ref[...])
        put_row("ln1_g", l1g_ref[...])
        put_row("ln1_b", l1b_ref[...])
        put_row("ln2_g", l2g_ref[...])
        put_row("ln2_b", l2b_ref[...])
        r0 = _PACK["ssm_lambda"]
        o_ref[r0:r0 + NG, 0:NP] = lr_ref[...]
        o_ref[r0:r0 + NG, NP:2 * NP] = li_ref[...]
        r0 = _PACK["scalars"]
        o_ref[r0:r0 + 1, 0:NG] = dt_ref[...]
        o_ref[r0 + 1:r0 + 2, 0:1] = sq_ref[...]
        for name, ref in (("ssm_b_re", br_ref), ("ssm_b_im", bi_ref), ("ssm_c_re", cr_ref), ("ssm_c_im", ci_ref)):
            r0, l0 = _PACK[name], _BC_LANE[name]
            o_ref[r0:r0 + _HALF, l0:l0 + NP] = pltpu.bitcast(ref[...].astype(bf16), f32)
        for cb in range(W // LANE):
            o_ref[_PACK["conv_w"] + 3 * cb:_PACK["conv_w"] + 3 * cb + 3, :] = cw_ref[:, cb * LANE:(cb + 1) * LANE]

    return pl.pallas_call(body, name="pack_grads", out_shape=S((PACK_ROWS, LANE), f32))(
        su, shcb, sga, sgb, dd, dglu_b, dln1_g, dln1_b, dln2_g, dln2_b, dlam_re, dlam_im, dldt, sqerr, dbr, dbi, dc_re, dc_im,
        dconv)


def adam_small(packed_all, params):
    names = list(_SMALL) + ["conv_w"]
    flat = [a for n in names for a in params[n]]

    def body(*refs):
        p_ref = refs[0]
        ins = refs[1:1 + 3 * len(names)]
        outs = refs[1 + 3 * len(names):-2]
        loss_ref, g_ref = refs[-2], refs[-1]
        g_all = p_ref[0]
        for k in range(1, NDEV):
            g_all = g_all + p_ref[k]
        g_ref[...] = g_all

        def rows(name, r0, n, l0=0, lanes=LANE):
            return g_ref[_PACK[name] + r0:_PACK[name] + r0 + n, l0:l0 + lanes]

        def grad_of(name):
            if name in dict(_ROWVEC):
                return jnp.concatenate([rows(name, i, 1) for i in range(dict(_ROWVEC)[name] // LANE)], axis=1)
            if name in ("ssm_lambda_re", "ssm_lambda_im"):
                return rows("ssm_lambda", 0, NG, NP * (name == "ssm_lambda_im"), NP)[None]
            if name == "ssm_log_dt":
                return rows("scalars", 0, 1, 0, NG)
            if name in _BC_LANE:
                r0, l0 = _PACK[name], _BC_LANE[name]
                g = pltpu.bitcast(p_ref[0, r0:r0 + _HALF, l0:l0 + NP], bf16).astype(f32)
                for k in range(1, NDEV):
                    g = g + pltpu.bitcast(p_ref[k, r0:r0 + _HALF, l0:l0 + NP], bf16).astype(f32)
                return g.reshape(1, NG, GC, NP)
            full = jnp.concatenate([rows("conv_w", 3 * cb, 3) for cb in range(W // LANE)], axis=1)
            x, y, c = _coords()
            col0 = (4 * x + 2 * y + c) * (W // NDEV)
            sel = (lax.broadcasted_iota(jnp.int32, (W, W // NDEV), 0)
                   == lax.broadcasted_iota(jnp.int32, (W, W // NDEV), 1) + col0).astype(f32)
            return jnp.dot(full, sel, precision=HIGHEST, preferred_element_type=f32)[None]

        loss_ref[...] = 0.5 * rows("scalars", 1, 1, 0, 1)
        for i, name in enumerate(names):
            w_ref, m_ref, v_ref = ins[3 * i:3 * i + 3]
            g = grad_of(name)
            d, m, v = _adam(w_ref[...], g, m_ref[...], v_ref[...])
            outs[4 * i][...] = g
            outs[4 * i + 1][...] = d
            outs[4 * i + 2][...] = m
            outs[4 * i + 3][...] = v

    out_shape = [S(params[n][0].shape, f32) for n in names for _ in range(4)] + [S((1, 1), f32)]
    res = pl.pallas_call(body, name="adam_small", out_shape=out_shape, scratch_shapes=[pltpu.VMEM((PACK_ROWS, LANE), f32)],
                         compiler_params=_cp(None, VMEM_LIMIT))(packed_all, *flat)
    return {n: res[4 * i:4 * i + 4] for i, n in enumerate(names)}, res[-1]


def _block_diag(wgt):
    eye = jnp.eye(8, dtype=wgt.dtype)
    out = wgt[:, :, :, None, :] * eye[None, :, None, :, None]
    return out.reshape(4, 8 * wgt.shape[2], 8 * wgt.shape[3])


def _diag_blocks(m, a, b):
    m = m.reshape(4, 8, a, 8, b)
    idx = jnp.arange(8)
    return m[:, idx, :, idx, :].transpose(1, 0, 2, 3)


def kernel(x, w_in, b_in, ssm_lambda_re, ssm_lambda_im, ssm_log_dt, ssm_b_re, ssm_b_im, ssm_c_re, ssm_c_im, ssm_d, glu_w, glu_b, w_ssm_out, conv_w, w_conv_out, w_o, ln1_g, ln1_b, w_gate, w_up, w_down, ln2_g, ln2_b, loss_target, m_w_in, m_b_in, m_ssm_lambda_re, m_ssm_lambda_im, m_ssm_log_dt, m_ssm_b_re, m_ssm_b_im, m_ssm_c_re, m_ssm_c_im, m_ssm_d, m_glu_w, m_glu_b, m_w_ssm_out, m_conv_w, m_w_conv_out, m_w_o, m_ln1_g, m_ln1_b, m_w_gate, m_w_up, m_w_down, m_ln2_g, m_ln2_b, v_w_in, v_b_in, v_ssm_lambda_re, v_ssm_lambda_im, v_ssm_log_dt, v_ssm_b_re, v_ssm_b_im, v_ssm_c_re, v_ssm_c_im, v_ssm_d, v_glu_w, v_glu_b, v_w_ssm_out, v_conv_w, v_w_conv_out, v_w_o, v_ln1_g, v_ln1_b, v_w_gate, v_w_up, v_w_down, v_ln2_g, v_ln2_b):
    given = dict(locals())
    xs = x[0]
    target = loss_target[0]

    tr = lambda a: jnp.swapaxes(a[0], 0, 1)
    win_s, glu_s, wso_s, wco_s, wo_s, wgT_s, wuT_s, wd_s = prep_weights(
        [w_in[0], glu_w[0], w_ssm_out[0], w_conv_out[0], w_o[0], tr(w_gate), tr(w_up), w_down[0]])
    (win_g,) = run_plan(GatherPlan([win_s], srcs=(0,)), "gather_w_in_u")

    lam_re, lam_im = ssm_lambda_re[0], ssm_lambda_im[0]
    ldt = ssm_log_dt[0].reshape(NG, 1)
    br2 = jnp.swapaxes(ssm_b_re[0], 1, 2).reshape(NG * GC, NP)
    bi2 = jnp.swapaxes(ssm_b_im[0], 1, 2).reshape(NG * GC, NP)
    lbr, lbi, fr, fi, bbr, bbi = ssm_params(lam_re, lam_im, ldt, br2, bi2)
    bb_t = lambda b: b.reshape(4, 8, GC, NP)
    wb = jnp.concatenate([_block_diag(bb_t(bbr)), _block_diag(bb_t(bbi))], axis=2)
    c_t = lambda c: c.reshape(4, 8, GC, NP).transpose(0, 1, 3, 2)
    wc = jnp.concatenate([_block_diag(c_t(ssm_c_re[0])), -_block_diag(c_t(ssm_c_im[0]))], axis=1)
    wbT, wcT = wb.transpose(0, 2, 1), wc.transpose(0, 2, 1)
    wb, wc, wbT, wcT = wb.astype(bf16), wc.astype(bf16), wbT.astype(bf16), wcT.astype(bf16)
    lbr_s, lbi_s = lbr.reshape(4, 1, SW), lbi.reshape(4, 1, SW)
    dsk = ssm_d[0].reshape(4, 1, LANE)

    u_nat, xb = in_proj_u(xs, win_g, b_in)
    u_p = to_perm(u_nat, 0, "perm_u")
    (y_p,), (win_g, conv_g, glu_g, wso_g, wco_g) = ssm_fwd(
        u_p, wb, wc, lbr_s, lbi_s, dsk,
        Plans([GatherPlan([win_s], srcs=tuple(range(1, NDEV)), into=[win_g]),
               GatherPlan([conv_w[0], glu_s, wso_s, wco_s])]))
    conv_f = conv_g.transpose(1, 0, 2).reshape(3, W)
    (proj,), (wo_g, wgT_g) = in_proj_rest(xb, win_g, b_in, GatherPlan([wo_s, wgT_s]))
    glu_f, wo_f = glu_g.reshape(W, W), wo_g.reshape(D, D)
    (yn,), _ = from_perm(y_p, "unperm_y")
    ya = glu_fwd(yn, glu_f, glu_b)
    yb = conv_fwd(proj, conv_f)
    (merged,), (wuT_g,) = merge_fwd(ya, yb, wso_g, wco_g, proj, GatherPlan([wuT_s]))
    wgT, wuT = wgT_g.reshape(F, D), wuT_g.reshape(F, D)
    r1, x1b = mix_ln1(merged, wo_f, xs, ln1_g, ln1_b)
    (gate, up, hid), (wd_g,) = gate_up(x1b, wgT, wuT, GatherPlan([wd_s]))
    wd_f = wd_g.reshape(F, D)
    dr2, dffn, sqerr, dln2_g, dln2_b = down_loss(hid, wd_f, r1, ln1_g, ln1_b, ln2_g, ln2_b, target)

    half_a, half_b = (0, 3, 5, 6), (1, 2, 4, 7)
    dgate, dup = ffn_bwd_act(dffn, wd_f, gate, up)
    dwd, _ = mm_tn_rows(hid, dffn, "grad_w_down")
    dwd = dwd.reshape(NDEV, FS, D)
    dwgT, (r_wd,) = mm_tn_rows(dgate, x1b, "grad_w_gate", plan=ScatterPlan([dwd], only=half_a))
    dwuT, (r_wd,) = mm_tn_rows(dup, x1b, "grad_w_up", plan=ScatterPlan([dwd], only=half_b, into=[r_wd]))
    dwgT, dwuT = dwgT.reshape(NDEV, FS, D), dwuT.reshape(NDEV, FS, D)
    (dr1, dmix, dln1_g, dln1_b), (r_wgT,) = ffn_bwd_x(dgate, dup, wgT, wuT, dr2, r1, ln1_g, ScatterPlan([dwgT]))
    (dYA, dYB, dga, dgb, sga, sgb), (r_wuT,) = merge_bwd(dmix, wo_f, ya, yb, wso_g, wco_g, proj,
                                                         ScatterPlan([dwuT], only=half_a))
    dwo, _ = mm_tn_rows(merged, dmix, "grad_w_o")
    dwo = dwo.reshape(NDEV, D // NDEV, D)
    (dya, dyb), (r_wuT,) = branches_bwd_x(dYA, dYB, wso_g, wco_g, ScatterPlan([dwuT], only=half_b, into=[r_wuT]))
    dwso = branch_bwd_w(ya, dYA, "grad_w_ssm_out")
    dwco = branch_bwd_w(yb, dYB, "grad_w_conv_out")
    dyn, dsp, gb, dglu_b = glu_bwd(yn, dya, glu_f, glu_b)
    dglu = mm_tn_rows(gb, dsp, "grad_glu_w")[0].reshape(NDEV, W // NDEV, W)
    dh, dcg, dbg, dconv, shcb = conv_bwd(proj, dyb, conv_f)
    dwin = mm_tn(xb, dgb, "grad_w_in_gb", block0=6, nblocks=NDEV)
    dwin = mm_tn(xb, dga, "grad_w_in_ga", block0=4, into=dwin)
    dwin = mm_tn(xb, dbg, "grad_w_in_bg", block0=3, into=dwin)
    dwin = mm_tn(xb, dcg, "grad_w_in_cg", block0=2, into=dwin)
    dwin = mm_tn(xb, dh, "grad_w_in_h", block0=1, into=dwin)
    dy_p = to_perm(dyn, 0, "perm_dy")
    (du_p, dwb, dwcT, dlbr_s, dlbi_s, dd, su), (r_wo, r_wso, r_wco, r_glu, r_win) = ssm_bwd(
        u_p, dy_p, wb, wbT, wcT, lbr_s, lbi_s, dsk,
        Plans([ScatterPlan([dwo, dwso, dwco, dglu]), ScatterPlan([dwin], only=tuple(range(1, NDEV)))]))

    dbb = lambda m: _diag_blocks(m, GC, NP).reshape(NG * GC, NP)
    dbr2, dbi2, dlam_re, dlam_im, dldt = ssm_param_bwd(
        lam_re, lam_im, ldt, fr, fi, br2, bi2, dbb(dwb[:, :, :SW]), dbb(dwb[:, :, SW:]),
        dlbr_s.reshape(NG, NP), dlbi_s.reshape(NG, NP))
    packed = pack_grads(su, shcb, sga, sgb, dd, dglu_b, dln1_g, dln1_b, dln2_g, dln2_b, dlam_re, dlam_im, dldt, sqerr,
                        dbr2, dbi2, dbb(dwcT[:, :, :SW]), -dbb(dwcT[:, :, SW:]), dconv)
    (du,), _ = from_perm(du_p, "unperm_du", bf16)
    dwin = mm_tn(xb, du, "grad_w_in_u", block0=0, into=dwin)

    r_win, gathered = own_slots(dwin, r_win, packed)
    *tail, token = tail_start(dwin, r_win, packed, gathered)
    rest = [(dh, 0, 1), (dcg, 0, 2), (dbg, 0, 3), (dga, 0, 4), (dga, 1, 5), (dgb, 0, 6), (dgb, 1, 7)]
    gx_rest = in_proj_bwd_x(rest, win_g, dr1, ALPHA, "in_proj_bwd_x_rest", after=token)
    grad_x = in_proj_bwd_x([(du, 0, 0)], win_g, gx_rest, 1.0, "in_proj_bwd_x_u")

    out = {}

    def put(name, res, back=lambda a: a[None]):
        out["grad_" + name], out["delta_" + name], out["new_m_" + name], out["new_v_" + name] = [back(r) for r in res]

    put("glu_w", adam_update(glu_w[0], m_glu_w[0], v_glu_w[0], r_glu, "adam_glu_w"))
    put("w_ssm_out", adam_update(w_ssm_out[0], m_w_ssm_out[0], v_w_ssm_out[0], r_wso, "adam_w_ssm_out"))
    put("w_conv_out", adam_update(w_conv_out[0], m_w_conv_out[0], v_w_conv_out[0], r_wco, "adam_w_conv_out"))
    put("w_o", adam_update(w_o[0], m_w_o[0], v_w_o[0], r_wo, "adam_w_o"))
    put("w_down", adam_update(w_down[0], m_w_down[0], v_w_down[0], r_wd, "adam_w_down", 176))
    untr = lambda a: jnp.swapaxes(a, 0, 1)[None]
    put("w_gate", adam_update(tr(w_gate), tr(m_w_gate), tr(v_w_gate), r_wgT, "adam_w_gate", 176), untr)
    res_up = adam_update(tr(w_up), tr(m_w_up), tr(v_w_up), r_wuT, "adam_w_up", 176)
    put("w_up", res_up, untr)
    r_win, small_all = tail_wait(tail[:4], tail[4:], res_up[3])
    put("w_in", adam_update(w_in[0], m_w_in[0], v_w_in[0], r_win, "adam_w_in", 256))
    as_c = lambda a: jnp.swapaxes(a, 2, 3)
    params = {n: (given[n], given["m_" + n], given["v_" + n]) for n in list(_SMALL) + ["conv_w"]}
    for n in ("ssm_b_re", "ssm_b_im"):
        params[n] = tuple(as_c(a) for a in params[n])
    small, loss = adam_small(small_all, params)
    for n, res in small.items():
        put(n, res, as_c if n in ("ssm_b_re", "ssm_b_im") else (lambda a: a))

    names = ["w_in", "b_in", "ssm_lambda_re", "ssm_lambda_im", "ssm_log_dt", "ssm_b_re", "ssm_b_im", "ssm_c_re", "ssm_c_im",
             "ssm_d", "glu_w", "glu_b", "w_ssm_out", "conv_w", "w_conv_out", "w_o", "ln1_g", "ln1_b", "w_gate", "w_up",
             "w_down", "ln2_g", "ln2_b"]
    return (loss.reshape(()), grad_x[None], *[out[p + n] for p in ("grad_", "delta_", "new_m_", "new_v_") for n in names])
```

```python
import functools
import math

import jax
import jax.numpy as jnp
from jax import lax
from jax.experimental import pallas as pl
from jax.experimental.pallas import tpu as pltpu

f32, bf16 = jnp.float32, jnp.bfloat16
S = jax.ShapeDtypeStruct
MESH = pl.DeviceIdType.MESH
HIGHEST = lax.Precision.HIGHEST

D = 1024
W = 512
NG, NP, GC = 32, 64, 16
F = 2816
NDEV = 8
FS = F // NDEV
IN_COLS = 8 * W
ALPHA = 2.0 ** 0.25
LN_EPS = 1e-5
ADAM_LR, ADAM_B1, ADAM_B2, ADAM_EPS, ADAM_WD, ADAM_STEP = 0.001, 0.9, 0.999, 1e-08, 0.01, 10
NC = 32
LANE = 128
SW = 4 * LANE
VMEM_LIMIT = 56 * 1024 * 1024
GRAD_DT = bf16
ANY = pl.BlockSpec(memory_space=pl.ANY)


def _cp(sem=None, vmem=None):
    return pltpu.CompilerParams(dimension_semantics=sem, vmem_limit_bytes=vmem)


def _resident(shape):
    return pl.BlockSpec(shape, lambda i: (0,) * len(shape), pipeline_mode=pl.Buffered(1))


def _dot(a, b):
    return jnp.dot(a, b, preferred_element_type=f32)


def _dot_nt(a, b):
    return lax.dot_general(a, b, (((1,), (1,)), ((), ())), preferred_element_type=f32)


def _dot_tn(a, b):
    return lax.dot_general(a, b, (((0,), (0,)), ((), ())), preferred_element_type=f32)


def _eye(n):
    return (lax.broadcasted_iota(jnp.int32, (n, n), 0) == lax.broadcasted_iota(jnp.int32, (n, n), 1)).astype(f32)


def _transpose_exact(a):
    return lax.dot_general(a, _eye(a.shape[0]), (((0,), (0,)), ((), ())), precision=HIGHEST, preferred_element_type=f32)


def _sigmoid(x):
    return 1.0 / (1.0 + jnp.exp(-x))


_GK = math.sqrt(2.0 / math.pi)


def _gelu(x):
    return 0.5 * x * (1.0 + jnp.tanh(_GK * (x + 0.044715 * x * x * x)))


def _gelu_grad(x):
    th = jnp.tanh(_GK * (x + 0.044715 * x * x * x))
    return 0.5 * (1.0 + th) + 0.5 * x * (1.0 - th * th) * _GK * (1.0 + 3.0 * 0.044715 * x * x)


ROW_PART = 256


def _row_parts(tm):
    return [slice(r, r + min(ROW_PART, tm)) for r in range(0, tm, min(ROW_PART, tm))]


def _ln_stats(r):
    mu = jnp.mean(r, axis=-1, keepdims=True)
    xc = r - mu
    var = jnp.mean(xc * xc, axis=-1, keepdims=True)
    rstd = lax.rsqrt(var + LN_EPS)
    return xc * rstd, rstd


def _ln_bwd(dy, xhat, rstd, g):
    dxh = dy * g
    m1 = jnp.mean(dxh, axis=-1, keepdims=True)
    m2 = jnp.mean(dxh * xhat, axis=-1, keepdims=True)
    return rstd * (dxh - m1 - xhat * m2)


def _coords():
    return lax.axis_index("x"), lax.axis_index("y"), lax.axis_index("c")


def _when(cond, fn):
    if cond is True:
        fn()
    else:
        pl.when(cond)(fn)


class GatherPlan:
    aliases = ()

    def __init__(self, arrs, srcs=None, into=None):
        n = self.n = len(arrs)
        self.srcs = srcs
        self.inputs = list(arrs) + list(into or [])
        if into:
            self.aliases = tuple((n + a, a) for a in range(n))
        self.out_shape = [S((NDEV,) + a.shape, a.dtype) for a in arrs]
        self.sems = [pltpu.SemaphoreType.DMA((n, 7)), pltpu.SemaphoreType.DMA((n, 7)), pltpu.SemaphoreType.DMA((n,))]

    def _has(self, dev):
        if self.srcs is None:
            return True
        idx = 4 * dev[0] + 2 * dev[1] + dev[2]
        return functools.reduce(jnp.logical_or, [idx == s for s in self.srcs])

    def _parts(self, ins, outs, sems):
        n = self.n
        send_sems, recv_sems, loc_sems = sems
        x, y, c = _coords()
        me, sib = (x, y, c), (x, y, 1 - c)
        chips = [(1 - x, y), (x, 1 - y), (1 - x, 1 - y)]

        def slot(a, dev):
            return outs[a].at[4 * dev[0] + 2 * dev[1] + dev[2]]

        def copy(a, k, block, to, src=None):
            return pltpu.make_async_remote_copy(
                src_ref=slot(a, block) if src is None else src, dst_ref=slot(a, block),
                send_sem=send_sems.at[a, k], recv_sem=recv_sems.at[a, k], device_id=to, device_id_type=MESH)

        each = [(j, chip, a) for j, chip in enumerate(chips) for a in range(n)]
        own = self._has(me)
        return dict(
            mine=lambda: [(pltpu.make_async_copy(ins[a], slot(a, me), loc_sems.at[a]), own) for a in range(n)],
            first=lambda: ([(copy(a, 0, me, sib, src=ins[a]), own) for a in range(n)]
                           + [(copy(a, 1 + j, me, (*chip, c), src=ins[a]), own) for j, chip, a in each]),
            landed=lambda: [(copy(a, 1 + j, (*chip, c), me), self._has((*chip, c))) for j, chip, a in each],
            passed=lambda: [(copy(a, 4 + j, (*chip, c), sib), self._has((*chip, c))) for j, chip, a in each],
            from_sib=lambda: ([(copy(a, 0, sib, me), self._has(sib)) for a in range(n)]
                              + [(copy(a, 4 + j, (*chip, 1 - c), me), self._has((*chip, 1 - c))) for j, chip, a in each]))

    def start(self, ins, outs, sems):
        p = self._parts(ins, outs, sems)
        for cp, cond in p["mine"]() + p["first"]():
            _when(cond, cp.start)

    def forward(self, ins, outs, sems):
        p = self._parts(ins, outs, sems)
        for (got, cond), (fwd, _) in zip(p["landed"](), p["passed"]()):
            def relay(got=got, fwd=fwd):
                got.wait_recv()
                fwd.start()

            _when(cond, relay)

    def finish(self, ins, outs, sems):
        p = self._parts(ins, outs, sems)
        for cp, cond in p["from_sib"]():
            _when(cond, cp.wait_recv)
        for cp, cond in p["first"]() + p["passed"]():
            _when(cond, cp.wait_send)
        for cp, cond in p["mine"]():
            _when(cond, cp.wait)


class ScatterPlan:
    aliases = ()

    def __init__(self, gs, only=None, into=None):
        n = self.n = len(gs)
        self.only = only
        self.inputs = list(gs) + list(into or [])
        if into:
            self.aliases = tuple((n + a, a) for a in range(n))
        self.out_shape = [S(g.shape, g.dtype) for g in gs]
        self.sems = [pltpu.SemaphoreType.DMA((n, 7)), pltpu.SemaphoreType.DMA((n, 7)), pltpu.SemaphoreType.DMA((n,))]

    def _owner(self, idx):
        if self.only is None:
            return True
        return functools.reduce(jnp.logical_or, [idx == b for b in self.only])

    def _copies(self, ins, outs, sems):
        n = self.n
        send_sems, recv_sems, loc_sems = sems
        x, y, c = _coords()
        me = 4 * x + 2 * y + c
        mine = self._owner(me)
        copies = [(pltpu.make_async_copy(ins[a].at[me], outs[a].at[me], loc_sems.at[a]), mine, None) for a in range(n)]
        for m in range(1, NDEV):
            px = 1 - x if m & 4 else x
            py = 1 - y if m & 2 else y
            pc = 1 - c if m & 1 else c
            peer = 4 * px + 2 * py + pc
            for a in range(n):
                copies.append((pltpu.make_async_remote_copy(
                    src_ref=ins[a].at[peer], dst_ref=outs[a].at[me],
                    send_sem=send_sems.at[a, m - 1], recv_sem=recv_sems.at[a, m - 1],
                    device_id=(px, py, pc), device_id_type=MESH), self._owner(peer), mine))
        return copies

    def start(self, ins, outs, sems):
        for cp, sends, _ in self._copies(ins, outs, sems):
            _when(sends, cp.start)

    def forward(self, ins, outs, sems):
        pass

    def finish(self, ins, outs, sems):
        for cp, sends, receives in self._copies(ins, outs, sems):
            if receives is None:
                _when(sends, cp.wait)
            else:
                _when(sends, cp.wait_send)
                _when(receives, cp.wait_recv)


class Plans:
    def __init__(self, plans):
        self.plans = plans
        self.inputs = [a for p in plans for a in p.inputs]
        self.out_shape = [s for p in plans for s in p.out_shape]
        self.sems = [s for p in plans for s in p.sems]
        self.aliases, i, o = [], 0, 0
        for p in plans:
            self.aliases += [(i + a, o + b) for a, b in p.aliases]
            i, o = i + len(p.inputs), o + len(p.out_shape)

    def _each(self, what, ins, outs, sems):
        i = o = s = 0
        for p in self.plans:
            ni, no, ns = len(p.inputs), len(p.out_shape), len(p.sems)
            getattr(p, what)(ins[i:i + ni], outs[o:o + no], sems[s:s + ns])
            i, o, s = i + ni, o + no, s + ns

    def start(self, ins, outs, sems):
        self._each("start", ins, outs, sems)

    def forward(self, ins, outs, sems):
        self._each("forward", ins, outs, sems)

    def finish(self, ins, outs, sems):
        self._each("finish", ins, outs, sems)


def _call(body, args, *, name, grid, in_specs, out_specs, out_shape, scratch=(), sem=None, vmem=None, plan=None,
          aliases=None):
    aliases = aliases or {}
    if plan is None:
        outs = pl.pallas_call(body, name=name, grid=grid, in_specs=list(in_specs), out_specs=list(out_specs),
                              out_shape=list(out_shape), scratch_shapes=list(scratch), input_output_aliases=aliases,
                              compiler_params=_cp(sem, vmem))(*args)
        return list(outs), []
    ni, no, ns = len(in_specs), len(out_specs), len(scratch)
    pi, po = len(plan.inputs), len(plan.out_shape)
    aliases = {**aliases, **{ni + a: no + b for a, b in plan.aliases}}

    def wrapped(*refs):
        main_in, p_in = refs[:ni], refs[ni:ni + pi]
        main_out, p_out = refs[ni + pi:ni + pi + no], refs[ni + pi + no:ni + pi + no + po]
        main_scr, p_sems = refs[ni + pi + no + po:ni + pi + no + po + ns], refs[ni + pi + no + po + ns:]
        ids = [pl.program_id(d) for d in range(len(grid))]
        first = functools.reduce(jnp.logical_and, [i == 0 for i in ids])
        last = functools.reduce(jnp.logical_and, [i == g - 1 for i, g in zip(ids, grid)])

        @pl.when(first)
        def _():
            plan.start(p_in, p_out, p_sems)

        @pl.when(last)
        def _():
            plan.forward(p_in, p_out, p_sems)

        body(*main_in, *main_out, *main_scr)

        @pl.when(last)
        def _():
            plan.finish(p_in, p_out, p_sems)

    outs = pl.pallas_call(
        wrapped, name=name, grid=grid, in_specs=list(in_specs) + [ANY] * pi, out_specs=list(out_specs) + [ANY] * po,
        out_shape=list(out_shape) + list(plan.out_shape), scratch_shapes=list(scratch) + list(plan.sems),
        input_output_aliases=aliases, compiler_params=_cp(("arbitrary",) * len(grid), vmem),
    )(*args, *plan.inputs)
    return list(outs[:no]), list(outs[no:])


HBM = pl.BlockSpec(memory_space=pltpu.HBM)
SEM = pl.BlockSpec(memory_space=pltpu.SEMAPHORE)
_IN_FLIGHT = pltpu.SideEffectType.DATAFLOW_SIDE_EFFECTING


def _tail_copies(dwin_ref, rwin_ref, packed_ref, all_ref, sems):
    sa, ra, sb, rb = sems
    x, y, c = _coords()
    me = 4 * x + 2 * y + c
    copies = []
    for m in range(1, NDEV):
        copies.append((pltpu.make_async_remote_copy(
            src_ref=dwin_ref.at[0], dst_ref=rwin_ref.at[m], send_sem=sa.at[m - 1], recv_sem=ra.at[m - 1],
            device_id=(0, 0, 0), device_id_type=MESH), me == m, me == 0))
        px = 1 - x if m & 4 else x
        py = 1 - y if m & 2 else y
        pc = 1 - c if m & 1 else c
        copies.append((pltpu.make_async_remote_copy(
            src_ref=packed_ref, dst_ref=all_ref.at[me], send_sem=sb.at[m - 1], recv_sem=rb.at[m - 1],
            device_id=(px, py, pc), device_id_type=MESH), True, True))
    return copies


def tail_start(dwin, r_win, packed, gathered):
    def body(dwin_ref, rwin_ref, packed_ref, all_ref, sa, ra, sb, rb, d_t, r_t, p_t, a_t, token):
        del d_t, r_t, p_t, a_t
        for cp, sends, _ in _tail_copies(dwin_ref, rwin_ref, packed_ref, all_ref, (sa, ra, sb, rb)):
            _when(sends, cp.start)
        token[...] = jnp.zeros_like(token)

    sem = pltpu.SemaphoreType.DMA((NDEV - 1,))
    hbm = lambda a: pltpu.HBM(a.shape, a.dtype)
    return pl.pallas_call(
        body, name="tail_start", in_specs=[HBM] * 4,
        out_specs=[SEM] * 4 + [HBM] * 4 + [pl.BlockSpec(memory_space=pltpu.VMEM)],
        out_shape=[sem] * 4 + [hbm(dwin), hbm(r_win), hbm(packed), hbm(gathered), S((8, LANE), f32)],
        input_output_aliases={0: 4, 1: 5, 2: 6, 3: 7}, compiler_params=pltpu.CompilerParams(has_side_effects=_IN_FLIGHT),
    )(*[pltpu.with_memory_space_constraint(a, pltpu.HBM) for a in (dwin, r_win, packed, gathered)])


def tail_wait(sems, bufs, after):
    def body(dwin_ref, rwin_ref, packed_ref, all_ref, sa, ra, sb, rb, after_ref, d_o, r_o, p_o, a_o):
        del after_ref, d_o, r_o, p_o, a_o
        for cp, sends, receives in _tail_copies(dwin_ref, rwin_ref, packed_ref, all_ref, (sa, ra, sb, rb)):
            _when(sends, cp.wait_send)
            _when(receives, cp.wait_recv)

    outs = pl.pallas_call(
        body, name="tail_wait", in_specs=[HBM] * 4 + [SEM] * 4 + [ANY], out_specs=[HBM] * 4,
        out_shape=[pltpu.HBM(b.shape, b.dtype) for b in bufs],
        input_output_aliases={0: 0, 1: 1, 2: 2, 3: 3}, compiler_params=pltpu.CompilerParams(has_side_effects=_IN_FLIGHT),
    )(*bufs, *sems, after)
    return outs[1], outs[3]


def run_plan(plan, name):
    def body(*refs):
        ins, outs, sems = refs[:len(plan.inputs)], refs[len(plan.inputs):len(plan.inputs) + len(plan.out_shape)], \
            refs[len(plan.inputs) + len(plan.out_shape):]
        plan.start(ins, outs, sems)
        plan.forward(ins, outs, sems)
        plan.finish(ins, outs, sems)

    return pl.pallas_call(body, name=name, in_specs=[ANY] * len(plan.inputs), out_specs=[ANY] * len(plan.out_shape),
                          out_shape=list(plan.out_shape), scratch_shapes=list(plan.sems))(*plan.inputs)


def mm_tn(a, b, name, tn=512, into=None, block0=0, nblocks=None):
    T, K = a.shape
    N = b.shape[1]
    tn = min(tn, N)
    nblocks = nblocks or (N // tn if into is None else into.shape[0])

    def body(a_ref, b_ref, *rest):
        rest[-1][...] = _dot_tn(a_ref[...], b_ref[...]).astype(GRAD_DT)

    args, in_specs, aliases = [a, b], [_resident((T, K)), pl.BlockSpec((T, tn), lambda j: (0, j))], {}
    if into is not None:
        args.append(into)
        in_specs.append(ANY)
        aliases = {2: 0}
    (out,), _ = _call(body, args, name=name, grid=(N // tn,), in_specs=in_specs,
                      out_specs=[pl.BlockSpec((None, K, tn), lambda j: (block0 + j, 0, 0))],
                      out_shape=[S((nblocks, K, tn), GRAD_DT)], sem=("parallel",), vmem=VMEM_LIMIT, aliases=aliases)
    return out


def mm_tn_rows(a, b, name, tk=256, plan=None):
    T, K = a.shape
    N = b.shape[1]
    tk = min(tk, K)

    def body(a_ref, b_ref, o_ref):
        o_ref[...] = _dot_tn(a_ref[...], b_ref[...]).astype(GRAD_DT)

    (out,), sent = _call(body, [a, b], name=name, grid=(K // tk,),
                         in_specs=[pl.BlockSpec((T, tk), lambda i: (0, i)), _resident((T, N))],
                         out_specs=[pl.BlockSpec((tk, N), lambda i: (i, 0))], out_shape=[S((K, N), GRAD_DT)],
                         sem=("parallel",), vmem=VMEM_LIMIT, plan=plan)
    return out, sent


def prep_weights(ws):
    def body(*refs):
        for i in range(len(ws)):
            refs[len(ws) + i][...] = refs[i][...].astype(bf16)

    return pl.pallas_call(body, name="prep_weights", out_shape=[S(w.shape, bf16) for w in ws],
                          compiler_params=_cp(None, VMEM_LIMIT))(*ws)


REST_BLOCKS = (4, 5, 6, 7, 1, 2, 3)
REST_COLS = len(REST_BLOCKS) * W


def in_proj_u(x, win_g, b_in):
    T = x.shape[0]
    tm = min(1024, T)

    def body(x_ref, w_ref, b_ref, u_ref, xb_ref):
        xb = x_ref[...].astype(bf16)
        xb_ref[...] = xb
        u_ref[...] = _dot(xb, w_ref[...]) + b_ref[...]

    row = pl.BlockSpec((tm, D), lambda i: (i, 0))
    return pl.pallas_call(
        body, name="in_proj_u", grid=(T // tm,),
        in_specs=[row, pl.BlockSpec((None, D, W), lambda i: (0, 0, 0)), pl.BlockSpec((1, W), lambda i: (0, 0))],
        out_specs=[pl.BlockSpec((tm, W), lambda i: (i, 0)), row],
        out_shape=[S((T, W), f32), S((T, D), bf16)], compiler_params=_cp(("parallel",), VMEM_LIMIT),
    )(x, win_g, b_in)


def in_proj_rest(xb, win_g, b_in, plan):
    T = xb.shape[0]
    tm = min(512, T)

    def body(x_ref, w_ref, b_ref, o_ref):
        xb_ = x_ref[...]
        for i, k in enumerate(REST_BLOCKS):
            o_ref[:, i * W:(i + 1) * W] = _dot(xb_, w_ref[k]) + b_ref[:, k * W:(k + 1) * W]

    return _call(
        body, [xb, win_g, b_in], name="in_proj_rest", grid=(T // tm,),
        in_specs=[pl.BlockSpec((tm, D), lambda i: (i, 0)), _resident((NDEV, D, W)), _resident((1, IN_COLS))],
        out_specs=[pl.BlockSpec((tm, REST_COLS), lambda i: (i, 0))],
        out_shape=[S((T, REST_COLS), f32)], vmem=VMEM_LIMIT, plan=plan)


def to_perm(a, cb0, name):
    T = a.shape[0]
    L = T // NC

    def body(a_ref, o_ref):
        def step(jb, carry):
            j0 = pl.multiple_of(jb * 8, 8)
            for q in range(NC // 8):
                x = jnp.stack([a_ref[pl.ds((8 * q + c) * L + j0, 8), :] for c in range(8)], axis=0)
                y = jnp.swapaxes(x, 0, 1)
                for j in range(8):
                    o_ref[pl.ds((j0 + j) * NC + 8 * q, 8), :] = y[j]
            return carry

        lax.fori_loop(0, L // 8, step, 0)

    return pl.pallas_call(
        body, name=name, grid=(W // LANE,),
        in_specs=[pl.BlockSpec((T, LANE), lambda k: (0, cb0 + k))], out_specs=pl.BlockSpec((T, LANE), lambda k: (0, k)),
        out_shape=S((T, W), f32), compiler_params=_cp(("parallel",), VMEM_LIMIT),
    )(a)


def from_perm(a, name, out_dtype=f32, plan=None):
    T = a.shape[0]
    L = T // NC

    def body(a_ref, o_ref):
        def step(jb, carry):
            j0 = pl.multiple_of(jb * 16, 16)
            for q in range(NC // 8):
                halves = []
                for h in range(2):
                    x = jnp.stack([a_ref[pl.ds((j0 + 8 * h + j) * NC + 8 * q, 8), :] for j in range(8)], axis=0)
                    halves.append(jnp.swapaxes(x, 0, 1))
                for c in range(8):
                    o_ref[pl.ds((8 * q + c) * L + j0, 16), :] = jnp.concatenate(
                        [halves[0][c], halves[1][c]], axis=0).astype(out_dtype)
            return carry

        lax.fori_loop(0, L // 16, step, 0)

    slab = pl.BlockSpec((T, LANE), lambda k: (0, k))
    return _call(body, [a], name=name, grid=(W // LANE,), in_specs=[slab], out_specs=[slab],
                 out_shape=[S((T, W), out_dtype)], sem=("parallel",), vmem=VMEM_LIMIT, plan=plan)


def _disc(lr, li, ldt):
    dt = jnp.exp(ldt)
    mag = jnp.exp(lr * dt)
    lbr = mag * jnp.cos(li * dt)
    lbi = mag * jnp.sin(li * dt)
    den = lr * lr + li * li
    nr = lbr - 1.0
    return lbr, lbi, (nr * lr + lbi * li) / den, (lbi * lr - nr * li) / den


def _per_channel(f):
    return jnp.broadcast_to(f[:, None, :], (NG, GC, NP)).reshape(NG * GC, NP)


def ssm_params(lam_re, lam_im, log_dt, br, bi):
    def body(lr_ref, li_ref, ldt_ref, br_ref, bi_ref, lbr_ref, lbi_ref, fr_ref, fi_ref, bbr_ref, bbi_ref):
        lbr, lbi, fr, fi = _disc(lr_ref[...], li_ref[...], ldt_ref[...])
        lbr_ref[...], lbi_ref[...], fr_ref[...], fi_ref[...] = lbr, lbi, fr, fi
        fr_, fi_, br_, bi_ = _per_channel(fr), _per_channel(fi), br_ref[...], bi_ref[...]
        bbr_ref[...] = fr_ * br_ - fi_ * bi_
        bbi_ref[...] = fr_ * bi_ + fi_ * br_

    return pl.pallas_call(body, name="ssm_params", out_shape=[S((NG, NP), f32)] * 4 + [S((NG * GC, NP), f32)] * 2)(
        lam_re, lam_im, log_dt, br, bi)


SCAN_UNROLL = 4


def _steps(n, body, carry):
    main = n // SCAN_UNROLL

    def trip(t, c):
        for q in range(SCAN_UNROLL):
            c = body(t * SCAN_UNROLL + q, c)
        return c

    carry = lax.fori_loop(0, main, trip, carry)
    for i in range(main * SCAN_UNROLL, n):
        carry = body(i, carry)
    return carry


def _scan_body(T):
    L = T // NC
    RB = min(512, T)
    nsq = int(round(math.log2(L)))
    assert 2 ** nsq == L and T % RB == 0 and L % 16 == 0

    def rows(i):
        return pl.ds(pl.multiple_of(i * RB, RB), RB)

    def tile(j):
        return pl.ds(j * NC if isinstance(j, int) else pl.multiple_of(j * NC, NC), NC)

    def forward_states(u_ref, wb_ref, lbr_ref, lbi_ref, sre, sim, ere, eim):
        def bproj(i, carry):
            bu = _dot(u_ref[rows(i), :].astype(bf16), wb_ref[...])
            sre[rows(i), :] = bu[:, :SW]
            sim[rows(i), :] = bu[:, SW:]
            return carry

        lax.fori_loop(0, T // RB, bproj, 0)
        for lb in range(SW // LANE):
            ls = slice(lb * LANE, (lb + 1) * LANE)
            ar = jnp.broadcast_to(lbr_ref[:, ls], (NC, LANE))
            ai = jnp.broadcast_to(lbi_ref[:, ls], (NC, LANE))

            def step(j, carry):
                xr, xi = carry
                nr = ar * xr - ai * xi + sre[tile(j), ls]
                ni = ar * xi + ai * xr + sim[tile(j), ls]
                sre[tile(j), ls] = nr
                sim[tile(j), ls] = ni
                return nr, ni

            zero = jnp.zeros((NC, LANE), f32)
            _steps(L, step, (zero, zero))
            pr, pi = lbr_ref[:, ls], lbi_ref[:, ls]
            for _ in range(nsq):
                pr, pi = pr * pr - pi * pi, 2.0 * pr * pi
            er = jnp.zeros((1, LANE), f32)
            ei = er
            ere[0:1, ls] = er
            eim[0:1, ls] = ei
            base = (L - 1) * NC
            for c in range(1, NC):
                lr_ = sre[base + c - 1:base + c, ls]
                li_ = sim[base + c - 1:base + c, ls]
                er, ei = lr_ + pr * er - pi * ei, li_ + pr * ei + pi * er
                ere[c:c + 1, ls] = er
                eim[c:c + 1, ls] = ei
            e_r, e_i = ere[:, ls].reshape(NC // 8, 8, LANE), eim[:, ls].reshape(NC // 8, 8, LANE)
            ar8, ai8 = ar[0:8], ai[0:8]

            def fix(j, carry):
                pwr, pwi = carry
                xr = sre[tile(j), ls].reshape(NC // 8, 8, LANE) + (pwr * e_r - pwi * e_i)
                xi = sim[tile(j), ls].reshape(NC // 8, 8, LANE) + (pwr * e_i + pwi * e_r)
                sre[tile(j), ls] = xr.reshape(NC, LANE)
                sim[tile(j), ls] = xi.reshape(NC, LANE)
                return pwr * ar8 - pwi * ai8, pwr * ai8 + pwi * ar8

            _steps(L, fix, (ar8, ai8))

    return L, RB, nsq, rows, tile, forward_states


def ssm_fwd(u_p, wb, wc, lbr, lbi, dsk, plan):
    T = u_p.shape[0]
    L, RB, nsq, rows, tile, forward_states = _scan_body(T)

    def body(u_ref, wb_ref, wc_ref, lbr_ref, lbi_ref, d_ref, y_ref, sre, sim, ere, eim):
        forward_states(u_ref, wb_ref, lbr_ref, lbi_ref, sre, sim, ere, eim)

        def cproj(i, carry):
            y = _dot(sre[rows(i), :].astype(bf16), wc_ref[0:SW, :]) + _dot(sim[rows(i), :].astype(bf16), wc_ref[SW:, :])
            y_ref[rows(i), :] = y + d_ref[...] * u_ref[rows(i), :]
            return carry

        lax.fori_loop(0, T // RB, cproj, 0)

    slab = pl.BlockSpec((T, LANE), lambda k: (0, k))
    return _call(
        body, [u_p, wb, wc, lbr, lbi, dsk], name="ssm_fwd", grid=(W // LANE,),
        in_specs=[slab, pl.BlockSpec((None, LANE, 2 * SW), lambda k: (k, 0, 0)),
                  pl.BlockSpec((None, 2 * SW, LANE), lambda k: (k, 0, 0)),
                  pl.BlockSpec((None, 1, SW), lambda k: (k, 0, 0)), pl.BlockSpec((None, 1, SW), lambda k: (k, 0, 0)),
                  pl.BlockSpec((None, 1, LANE), lambda k: (k, 0, 0))],
        out_specs=[slab], out_shape=[S((T, W), f32)],
        scratch=[pltpu.VMEM((T, SW), f32), pltpu.VMEM((T, SW), f32), pltpu.VMEM((NC, SW), f32), pltpu.VMEM((NC, SW), f32)],
        vmem=VMEM_LIMIT, plan=plan)


def ssm_bwd(u_p, dy_p, wb, wbT, wcT, lbr, lbi, dsk, plan):
    T = u_p.shape[0]
    L, RB, nsq, rows, tile, forward_states = _scan_body(T)

    def body(u_ref, dy_ref, wb_ref, wbT_ref, wcT_ref, lbr_ref, lbi_ref, d_ref,
             du_ref, dwb_ref, dwc_ref, dlr_ref, dli_ref, dd_ref, su_ref, sre, sim, gre, gim, ere, eim):
        forward_states(u_ref, wb_ref, lbr_ref, lbi_ref, sre, sim, ere, eim)

        def dstate(i, carry):
            g = _dot(dy_ref[rows(i), :].astype(bf16), wcT_ref[...])
            gre[rows(i), :] = g[:, :SW]
            gim[rows(i), :] = g[:, SW:]
            return carry

        lax.fori_loop(0, T // RB, dstate, 0)
        row = lax.broadcasted_iota(jnp.int32, (NC, LANE), 0)
        for lb in range(SW // LANE):
            ls = slice(lb * LANE, (lb + 1) * LANE)
            ar = jnp.broadcast_to(lbr_ref[:, ls], (NC, LANE))
            ai = jnp.broadcast_to(lbi_ref[:, ls], (NC, LANE))

            def step(i, carry):
                gr, gi = carry
                j = L - 1 - i
                nr = ar * gr + ai * gi + gre[tile(j), ls]
                ni = ar * gi - ai * gr + gim[tile(j), ls]
                gre[tile(j), ls] = nr
                gim[tile(j), ls] = ni
                return nr, ni

            zero = jnp.zeros((NC, LANE), f32)
            _steps(L, step, (zero, zero))
            pr, pi = lbr_ref[:, ls], -lbi_ref[:, ls]
            for _ in range(nsq):
                pr, pi = pr * pr - pi * pi, 2.0 * pr * pi
            er = jnp.zeros((1, LANE), f32)
            ei = er
            ere[NC - 1:NC, ls] = er
            eim[NC - 1:NC, ls] = ei
            for c in range(NC - 2, -1, -1):
                lr_ = gre[c + 1:c + 2, ls]
                li_ = gim[c + 1:c + 2, ls]
                er, ei = lr_ + pr * er - pi * ei, li_ + pr * ei + pi * er
                ere[c:c + 1, ls] = er
                eim[c:c + 1, ls] = ei
            e_r, e_i = ere[:, ls].reshape(NC // 8, 8, LANE), eim[:, ls].reshape(NC // 8, 8, LANE)
            ar8, ai8 = ar[0:8], ai[0:8]

            def fixed(j, pwr, pwi):
                gr = (gre[tile(j), ls].reshape(NC // 8, 8, LANE) + (pwr * e_r - pwi * e_i)).reshape(NC, LANE)
                gi = (gim[tile(j), ls].reshape(NC // 8, 8, LANE) + (pwr * e_i + pwi * e_r)).reshape(NC, LANE)
                gre[tile(j), ls] = gr
                gim[tile(j), ls] = gi
                return gr, gi

            def fix(i, carry):
                pwr, pwi, accr, acci = carry
                j = L - 1 - i
                gr, gi = fixed(j, pwr, pwi)
                xr, xi = sre[tile(j - 1), ls], sim[tile(j - 1), ls]
                return (pwr * ar8 + pwi * ai8, pwi * ar8 - pwr * ai8,
                        accr + gr * xr + gi * xi, acci + gi * xr - gr * xi)

            pwr, pwi, accr, acci = _steps(L - 1, fix, (ar8, -ai8, zero, zero))
            gr, gi = fixed(0, pwr, pwi)
            xr = jnp.where(row == 0, 0.0, pltpu.roll(sre[tile(L - 1), ls], 1, axis=0))
            xi = jnp.where(row == 0, 0.0, pltpu.roll(sim[tile(L - 1), ls], 1, axis=0))
            accr = accr + gr * xr + gi * xi
            acci = acci + gi * xr - gr * xi
            dlr_ref[:, ls] = jnp.sum(accr, axis=0, keepdims=True)
            dli_ref[:, ls] = jnp.sum(acci, axis=0, keepdims=True)

        dwb_ref[...] = jnp.zeros_like(dwb_ref)
        dwc_ref[...] = jnp.zeros_like(dwc_ref)
        dd_ref[...] = jnp.zeros_like(dd_ref)
        su_ref[...] = jnp.zeros_like(su_ref)

        def finish(i, carry):
            u32, dy32 = u_ref[rows(i), :], dy_ref[rows(i), :]
            ub, dyb = u32.astype(bf16), dy32.astype(bf16)
            gr, gi = gre[rows(i), :].astype(bf16), gim[rows(i), :].astype(bf16)
            du = _dot(gr, wbT_ref[0:SW, :]) + _dot(gi, wbT_ref[SW:, :]) + dy32 * d_ref[...]
            du_ref[rows(i), :] = du
            su_ref[...] += jnp.sum(du, axis=0, keepdims=True)
            dwb_ref[:, 0:SW] += _dot_tn(ub, gr)
            dwb_ref[:, SW:] += _dot_tn(ub, gi)
            dwc_ref[:, 0:SW] += _dot_tn(dyb, sre[rows(i), :].astype(bf16))
            dwc_ref[:, SW:] += _dot_tn(dyb, sim[rows(i), :].astype(bf16))
            dd_ref[...] += jnp.sum(dy32 * u32, axis=0, keepdims=True)
            return carry

        lax.fori_loop(0, T // RB, finish, 0)

    slab = pl.BlockSpec((T, LANE), lambda k: (0, k))
    wide = pl.BlockSpec((None, LANE, 2 * SW), lambda k: (k, 0, 0))
    tall = pl.BlockSpec((None, 2 * SW, LANE), lambda k: (k, 0, 0))
    vec = pl.BlockSpec((None, 1, SW), lambda k: (k, 0, 0))
    vecd = pl.BlockSpec((None, 1, LANE), lambda k: (k, 0, 0))
    nslab = W // LANE
    return _call(
        body, [u_p, dy_p, wb, wbT, wcT, lbr, lbi, dsk], name="ssm_bwd", grid=(nslab,),
        in_specs=[slab, slab, wide, tall, wide, vec, vec, vecd],
        out_specs=[slab, wide, wide, vec, vec, vecd, vecd],
        out_shape=[S((T, W), f32), S((nslab, LANE, 2 * SW), f32), S((nslab, LANE, 2 * SW), f32),
                   S((nslab, 1, SW), f32), S((nslab, 1, SW), f32), S((nslab, 1, LANE), f32), S((nslab, 1, LANE), f32)],
        scratch=[pltpu.VMEM((T, SW), f32)] * 4 + [pltpu.VMEM((NC, SW), f32)] * 2, vmem=VMEM_LIMIT, plan=plan)


def glu_fwd(yn, glu_w, glu_b):
    T = yn.shape[0]
    tm = min(512, T)

    def body(y_ref, w_ref, b_ref, o_ref):
        g = _gelu(y_ref[...])
        o_ref[...] = (g * _sigmoid(_dot(g.astype(bf16), w_ref[...]) + b_ref[...])).astype(bf16)

    return pl.pallas_call(
        body, name="glu_fwd", grid=(T // tm,),
        in_specs=[pl.BlockSpec((tm, W), lambda i: (i, 0)), pl.BlockSpec((W, W), lambda i: (0, 0)), pl.BlockSpec((1, W), lambda i: (0, 0))],
        out_specs=pl.BlockSpec((tm, W), lambda i: (i, 0)), out_shape=S((T, W), bf16), compiler_params=_cp(("parallel",)),
    )(yn, glu_w, glu_b)


def _shift_rows(cur, prev8, k):
    return pltpu.roll(jnp.concatenate([prev8, cur], axis=0), k, axis=0)[8:]


def _lift_rows(cur, next8, k):
    n = cur.shape[0]
    return pltpu.roll(jnp.concatenate([cur, next8], axis=0), n + 8 - k, axis=0)[:n]


def conv_fwd(proj, conv_w):
    T = proj.shape[0]
    RB = min(512, T)

    def body(h_ref, c_ref, b_ref, w_ref, o_ref):
        w0, w1, w2 = w_ref[0:1, :], w_ref[1:2, :], w_ref[2:3, :]

        def blk(i, carry):
            r0 = pl.multiple_of(i * RB, RB)
            rs = pl.ds(r0, RB)
            ch = c_ref[rs, :] * h_ref[rs, :]
            pr = pl.ds(jnp.maximum(r0 - 8, 0), 8)
            prev = jnp.where(i > 0, c_ref[pr, :] * h_ref[pr, :], 0.0)
            z = w2 * ch + w1 * _shift_rows(ch, prev, 1) + w0 * _shift_rows(ch, prev, 2)
            o_ref[rs, :] = (b_ref[rs, :] * z).astype(bf16)
            return carry

        lax.fori_loop(0, T // RB, blk, 0)

    nb = W // LANE
    return pl.pallas_call(
        body, name="conv_fwd", grid=(nb,),
        in_specs=[pl.BlockSpec((T, LANE), lambda k: (0, 4 * nb + k)), pl.BlockSpec((T, LANE), lambda k: (0, 5 * nb + k)),
                  pl.BlockSpec((T, LANE), lambda k: (0, 6 * nb + k)),pl.BlockSpec((3, LANE), lambda k: (0, k))],
        out_specs=pl.BlockSpec((T, LANE), lambda k: (0, k)), out_shape=S((T, W), bf16),
        compiler_params=_cp(("parallel",), VMEM_LIMIT),
    )(proj, proj, proj, conv_w)


def _dense_columns(blocks_ref, dense_ref):
    for k in range(NDEV):
        dense_ref[:, k * LANE:(k + 1) * LANE] = blocks_ref[k]


def merge_fwd(ya, yb, wso, wco, proj, plan):
    T = ya.shape[0]
    tm = min(1024, T)

    def body(ya_ref, yb_ref, wa_ref, wb_ref, ga_ref, gb_ref, o_ref, wa_s, wb_s):
        @pl.when(pl.program_id(0) == 0)
        def _():
            _dense_columns(wa_ref, wa_s)
            _dense_columns(wb_ref, wb_s)

        o_ref[...] = (_sigmoid(ga_ref[...]) * _dot(ya_ref[...], wa_s[...])
                      + _sigmoid(gb_ref[...]) * _dot(yb_ref[...], wb_s[...])).astype(bf16)

    act = pl.BlockSpec((tm, W), lambda i: (i, 0))
    return _call(
        body, [ya, yb, wso, wco, proj, proj], name="merge_fwd", grid=(T // tm,),
        in_specs=[act, act, _resident((NDEV, W, LANE)), _resident((NDEV, W, LANE)),
                  pl.BlockSpec((tm, D), lambda i: (i, 0)), pl.BlockSpec((tm, D), lambda i: (i, 1))],
        out_specs=[pl.BlockSpec((tm, D), lambda i: (i, 0))], out_shape=[S((T, D), bf16)],
        scratch=[pltpu.VMEM((W, D), bf16), pltpu.VMEM((W, D), bf16)], vmem=VMEM_LIMIT, plan=plan)


def mix_ln1(merged, w_o, x, g1, b1):
    T = x.shape[0]
    tm = min(512, T)

    def body(m_ref, w_ref, x_ref, g_ref, b_ref, r_ref, x1_ref):
        for rs in _row_parts(tm):
            r = ALPHA * x_ref[rs, :] + _dot(m_ref[rs, :], w_ref[...])
            r_ref[rs, :] = r
            xhat, _ = _ln_stats(r)
            x1_ref[rs, :] = (xhat * g_ref[...] + b_ref[...]).astype(bf16)

    row = pl.BlockSpec((tm, D), lambda i: (i, 0))
    vec = pl.BlockSpec((1, D), lambda i: (0, 0))
    return pl.pallas_call(
        body, name="mix_ln1", grid=(T // tm,),
        in_specs=[row, _resident((D, D)), row, vec, vec],
        out_specs=[row, row], out_shape=[S((T, D), f32), S((T, D), bf16)], compiler_params=_cp(("parallel",), VMEM_LIMIT),
    )(merged, w_o, x, g1, b1)


FT = 256


def gate_up(x1b, wgT, wuT, plan):
    T = x1b.shape[0]
    tm = min(512, T)

    def body(x_ref, wg_ref, wu_ref, g_ref, u_ref, h_ref):
        x = x_ref[...]
        for n in range(F // FT):
            cs = slice(n * FT, (n + 1) * FT)
            g = _dot_nt(x, wg_ref[cs, :])
            u = _dot_nt(x, wu_ref[cs, :])
            g_ref[:, cs] = g.astype(bf16)
            u_ref[:, cs] = u.astype(bf16)
            h_ref[:, cs] = (g * _sigmoid(g) * u).astype(bf16)

    osp = pl.BlockSpec((tm, F), lambda i: (i, 0))
    return _call(
        body, [x1b, wgT, wuT], name="gate_up", grid=(T // tm,),
        in_specs=[pl.BlockSpec((tm, D), lambda i: (i, 0)), _resident((F, D)), _resident((F, D))],
        out_specs=[osp, osp, osp], out_shape=[S((T, F), bf16)] * 3, vmem=VMEM_LIMIT, plan=plan)


def down_loss(hid, w_down, r1, g1, b1, g2, b2, target):
    T = hid.shape[0]
    tm = min(512, T)

    def body(h_ref, w_ref, r1_ref, g1_ref, b1_ref, g2_ref, b2_ref, t_ref, dr_ref, drb_ref, loss_ref, dg_ref, db_ref):
        @pl.when(pl.program_id(0) == 0)
        def _():
            loss_ref[...] = jnp.zeros_like(loss_ref)
            dg_ref[...] = jnp.zeros_like(dg_ref)
            db_ref[...] = jnp.zeros_like(db_ref)

        for rs in _row_parts(tm):
            xh1, _ = _ln_stats(r1_ref[rs, :])
            x1 = xh1 * g1_ref[...] + b1_ref[...]
            r2 = ALPHA * x1 + _dot(h_ref[rs, :], w_ref[...])
            xh2, rstd2 = _ln_stats(r2)
            err = xh2 * g2_ref[...] + b2_ref[...] - t_ref[rs, :]
            loss_ref[...] += jnp.sum(jnp.mean(err * err, axis=-1, keepdims=True), axis=0, keepdims=True)
            dy = err * (1.0 / D)
            dg_ref[...] += jnp.sum(dy * xh2, axis=0, keepdims=True)
            db_ref[...] += jnp.sum(dy, axis=0, keepdims=True)
            dr = _ln_bwd(dy, xh2, rstd2, g2_ref[...])
            dr_ref[rs, :] = dr
            drb_ref[rs, :] = dr.astype(bf16)

    row = pl.BlockSpec((tm, D), lambda i: (i, 0))
    vec = pl.BlockSpec((1, D), lambda i: (0, 0))
    return pl.pallas_call(
        body, name="down_loss", grid=(T // tm,),
        in_specs=[pl.BlockSpec((tm, F), lambda i: (i, 0)), _resident((F, D)), row, vec, vec, vec, vec, row],
        out_specs=[row, row, pl.BlockSpec((1, 1), lambda i: (0, 0)), vec, vec],
        out_shape=[S((T, D), f32), S((T, D), bf16), S((1, 1), f32), S((1, D), f32), S((1, D), f32)],
        compiler_params=_cp(("arbitrary",), VMEM_LIMIT),
    )(hid, w_down, r1, g1, b1, g2, b2, target)


def ffn_bwd_act(dffn, w_down, gate, up):
    T = dffn.shape[0]
    tm = min(512, T)

    def body(d_ref, w_ref, g_ref, u_ref, dg_ref, du_ref):
        for n in range(F // FT):
            cs = slice(n * FT, (n + 1) * FT)
            for rs in _row_parts(tm):
                dh = _dot_nt(d_ref[rs, :], w_ref[cs, :])
                g, u = g_ref[rs, cs].astype(f32), u_ref[rs, cs].astype(f32)
                sg = _sigmoid(g)
                t = g * sg
                du_ref[rs, cs] = (dh * t).astype(bf16)
                dg_ref[rs, cs] = (dh * u * (sg + t - t * sg)).astype(bf16)

    osp = pl.BlockSpec((tm, F), lambda i: (i, 0))
    return pl.pallas_call(
        body, name="ffn_bwd_act", grid=(T // tm,),
        in_specs=[pl.BlockSpec((tm, D), lambda i: (i, 0)), _resident((F, D)), osp, osp],
        out_specs=[osp, osp], out_shape=[S((T, F), bf16)] * 2, compiler_params=_cp(("parallel",), VMEM_LIMIT),
    )(dffn, w_down, gate, up)


def ffn_bwd_x(dgate, dup, wgT, wuT, dr2, r1, g1, plan):
    T = dr2.shape[0]
    tm = min(512, T)

    def body(dg_ref, du_ref, wg_ref, wu_ref, dr2_ref, r1_ref, g1_ref, dr_ref, drb_ref, dgam_ref, dbet_ref):
        @pl.when(pl.program_id(0) == 0)
        def _():
            dgam_ref[...] = jnp.zeros_like(dgam_ref)
            dbet_ref[...] = jnp.zeros_like(dbet_ref)

        for rs in _row_parts(tm):
            dx1 = ALPHA * dr2_ref[rs, :] + _dot(dg_ref[rs, :], wg_ref[...]) + _dot(du_ref[rs, :], wu_ref[...])
            xh, rstd = _ln_stats(r1_ref[rs, :])
            dgam_ref[...] += jnp.sum(dx1 * xh, axis=0, keepdims=True)
            dbet_ref[...] += jnp.sum(dx1, axis=0, keepdims=True)
            dr = _ln_bwd(dx1, xh, rstd, g1_ref[...])
            dr_ref[rs, :] = dr
            drb_ref[rs, :] = dr.astype(bf16)

    row = pl.BlockSpec((tm, D), lambda i: (i, 0))
    wide = pl.BlockSpec((tm, F), lambda i: (i, 0))
    wsp = _resident((F, D))
    vec = pl.BlockSpec((1, D), lambda i: (0, 0))
    return _call(
        body, [dgate, dup, wgT, wuT, dr2, r1, g1], name="ffn_bwd_x", grid=(T // tm,),
        in_specs=[wide, wide, wsp, wsp, row, row, vec],
        out_specs=[row, row, vec, vec], out_shape=[S((T, D), f32), S((T, D), bf16), S((1, D), f32), S((1, D), f32)],
        vmem=VMEM_LIMIT, plan=plan)


def merge_bwd(dmix, w_o, ya, yb, wso, wco, proj, plan):
    T = dmix.shape[0]
    tm = min(512, T)

    def body(dm_ref, wo_ref, ya_ref, yb_ref, wa_ref, wb_ref, ga_ref, gb_ref, dya_ref, dyb_ref, dga_ref, dgb_ref, sa_ref, sb_ref,
             wa_s, wb_s):
        @pl.when(pl.program_id(0) == 0)
        def _():
            _dense_columns(wa_ref, wa_s)
            _dense_columns(wb_ref, wb_s)

        dmer = _dot_nt(dm_ref[...], wo_ref[...])
        sa, sb = _sigmoid(ga_ref[...]), _sigmoid(gb_ref[...])
        dya_ref[...] = (dmer * sa).astype(bf16)
        dyb_ref[...] = (dmer * sb).astype(bf16)
        dga = dmer * _dot(ya_ref[...], wa_s[...]) * sa * (1.0 - sa)
        dgb = dmer * _dot(yb_ref[...], wb_s[...]) * sb * (1.0 - sb)
        dga_ref[...] = dga.astype(bf16)
        dgb_ref[...] = dgb.astype(bf16)
        sa_ref[...] = jnp.sum(dga, axis=0, keepdims=True)
        sb_ref[...] = jnp.sum(dgb, axis=0, keepdims=True)

    act = pl.BlockSpec((tm, W), lambda i: (i, 0))
    osp = pl.BlockSpec((tm, D), lambda i: (i, 0))
    ssp = pl.BlockSpec((None, 1, D), lambda i: (i, 0, 0))
    return _call(
        body, [dmix, w_o, ya, yb, wso, wco, proj, proj], name="merge_bwd", grid=(T // tm,),
        in_specs=[osp, _resident((D, D)), act, act, _resident((NDEV, W, LANE)), _resident((NDEV, W, LANE)),
                  pl.BlockSpec((tm, D), lambda i: (i, 0)), pl.BlockSpec((tm, D), lambda i: (i, 1))],
        out_specs=[osp, osp, osp, osp, ssp, ssp],
        out_shape=[S((T, D), bf16)] * 4 + [S((T // tm, 1, D), f32)] * 2,
        scratch=[pltpu.VMEM((W, D), bf16), pltpu.VMEM((W, D), bf16)], vmem=VMEM_LIMIT, plan=plan)


def branches_bwd_x(dYA, dYB, wso, wco, plan):
    T = dYA.shape[0]
    tm = min(1024, T)

    def body(da_ref, db_ref, wa_ref, wb_ref, oa_ref, ob_ref, wa_s, wb_s):
        @pl.when(pl.program_id(0) == 0)
        def _():
            _dense_columns(wa_ref, wa_s)
            _dense_columns(wb_ref, wb_s)

        oa_ref[...] = _dot_nt(da_ref[...], wa_s[...])
        ob_ref[...] = _dot_nt(db_ref[...], wb_s[...])

    row = pl.BlockSpec((tm, D), lambda i: (i, 0))
    osp = pl.BlockSpec((tm, W), lambda i: (i, 0))
    return _call(
        body, [dYA, dYB, wso, wco], name="branches_bwd_x", grid=(T // tm,),
        in_specs=[row, row, _resident((NDEV, W, LANE)), _resident((NDEV, W, LANE))],
        out_specs=[osp, osp], out_shape=[S((T, W), f32)] * 2,
        scratch=[pltpu.VMEM((W, D), bf16), pltpu.VMEM((W, D), bf16)], vmem=VMEM_LIMIT, plan=plan)


def branch_bwd_w(act, dY, name):
    T = act.shape[0]
    tk = W // 2

    def body(a_ref, d_ref, o_ref):
        res = _dot_tn(a_ref[...], d_ref[...])
        for k in range(NDEV):
            o_ref[k] = res[:, k * LANE:(k + 1) * LANE].astype(o_ref.dtype)

    return pl.pallas_call(
        body, name=name, grid=(W // tk,),
        in_specs=[pl.BlockSpec((T, tk), lambda i: (0, i)), _resident((T, D))],
        out_specs=pl.BlockSpec((NDEV, tk, LANE), lambda i: (0, i, 0)), out_shape=S((NDEV, W, LANE), GRAD_DT),
        compiler_params=_cp(("parallel",), VMEM_LIMIT),
    )(act, dY)


def glu_bwd(yn, dya, glu_w, glu_b):
    T = yn.shape[0]
    tm = min(512, T)

    def body(y_ref, d_ref, w_ref, b_ref, dy_ref, dsp_ref, g_ref, db_ref):
        @pl.when(pl.program_id(0) == 0)
        def _():
            db_ref[...] = jnp.zeros_like(db_ref)

        y, dya_ = y_ref[...], d_ref[...]
        g = _gelu(y)
        gb = g.astype(bf16)
        s = _sigmoid(_dot(gb, w_ref[...]) + b_ref[...])
        dsp = dya_ * g * s * (1.0 - s)
        dspb = dsp.astype(bf16)
        dg = dya_ * s + _dot_nt(dspb, w_ref[...])
        dy_ref[...] = dg * _gelu_grad(y)
        dsp_ref[...] = dspb
        g_ref[...] = gb
        db_ref[...] += jnp.sum(dsp, axis=0, keepdims=True)

    row = pl.BlockSpec((tm, W), lambda i: (i, 0))
    vec = pl.BlockSpec((1, W), lambda i: (0, 0))
    return pl.pallas_call(
        body, name="glu_bwd", grid=(T // tm,),
        in_specs=[row, row, pl.BlockSpec((W, W), lambda i: (0, 0)), vec],
        out_specs=[row, row, row, vec], out_shape=[S((T, W), f32), S((T, W), bf16), S((T, W), bf16), S((1, W), f32)],
        compiler_params=_cp(("arbitrary",)),
    )(yn, dya, glu_w, glu_b)


def conv_bwd(proj, dyb, conv_w):
    T = proj.shape[0]
    RB = min(512, T)
    nrb = T // RB

    def body(h_ref, c_ref, b_ref, d_ref, w_ref, dh_ref, dc_ref, db_ref, dw_ref, s_ref):
        w0, w1, w2 = w_ref[0:1, :], w_ref[1:2, :], w_ref[2:3, :]

        def blk(i, carry):
            a0, a1, a2, sh, sc, sb = carry
            r0 = pl.multiple_of(i * RB, RB)
            rs = pl.ds(r0, RB)
            h, cg, bg, dyb_ = h_ref[rs, :], c_ref[rs, :], b_ref[rs, :], d_ref[rs, :]
            ch = cg * h
            pr = pl.ds(jnp.maximum(r0 - 8, 0), 8)
            prev = jnp.where(i > 0, c_ref[pr, :] * h_ref[pr, :], 0.0)
            ch1, ch2 = _shift_rows(ch, prev, 1), _shift_rows(ch, prev, 2)
            dbg = dyb_ * (w2 * ch + w1 * ch1 + w0 * ch2)
            db_ref[rs, :] = dbg.astype(bf16)
            dz = dyb_ * bg
            nx = pl.ds(jnp.minimum(r0 + RB, T - 8), 8)
            nxt = jnp.where(i < nrb - 1, d_ref[nx, :] * b_ref[nx, :], 0.0)
            dch = w2 * dz + w1 * _lift_rows(dz, nxt, 1) + w0 * _lift_rows(dz, nxt, 2)
            dcg, dh = dch * h, dch * cg
            dc_ref[rs, :] = dcg.astype(bf16)
            dh_ref[rs, :] = dh.astype(bf16)
            col = lambda v: jnp.sum(v, axis=0, keepdims=True)
            return (a0 + col(dz * ch2), a1 + col(dz * ch1), a2 + col(dz * ch), sh + col(dh), sc + col(dcg), sb + col(dbg))

        zero = jnp.zeros((1, LANE), f32)
        a0, a1, a2, sh, sc, sb = lax.fori_loop(0, nrb, blk, (zero,) * 6)
        dw_ref[0:1, :] = a0
        dw_ref[1:2, :] = a1
        dw_ref[2:3, :] = a2
        s_ref[0:1, :] = sh
        s_ref[1:2, :] = sc
        s_ref[2:3, :] = sb

    nb = W // LANE
    slab = pl.BlockSpec((T, LANE), lambda k: (0, k))
    three = pl.BlockSpec((3, LANE), lambda k: (0, k))
    return pl.pallas_call(
        body, name="conv_bwd", grid=(nb,),
        in_specs=[pl.BlockSpec((T, LANE), lambda k: (0, 4 * nb + k)), pl.BlockSpec((T, LANE), lambda k: (0, 5 * nb + k)),
                  pl.BlockSpec((T, LANE), lambda k: (0, 6 * nb + k)),slab, three],
        out_specs=[slab, slab, slab, three, three],
        out_shape=[S((T, W), bf16)] * 3 + [S((3, W), f32)] * 2, compiler_params=_cp(("parallel",), VMEM_LIMIT),
    )(proj, proj, proj, dyb, conv_w)


def in_proj_bwd_x(parts, win_g, base, scale, name, after=None):
    T = base.shape[0]
    tm = min(512, T)
    n = len(parts)

    def body(*refs):
        p_refs, w_ref, b_ref, o_ref = refs[:n], refs[n], refs[n + 1], refs[-1]
        acc = scale * b_ref[...]
        for p_ref, (_, _, k) in zip(p_refs, parts):
            acc += _dot_nt(p_ref[...], w_ref[k])
        o_ref[...] = acc

    row = pl.BlockSpec((tm, D), lambda i: (i, 0))
    p_specs = [pl.BlockSpec((tm, W), (lambda i, cb=cb: (i, cb))) for _, cb, _ in parts]
    extra = [] if after is None else [after]
    (out,), _ = _call(
        body, [a for a, _, _ in parts] + [win_g, base] + extra, name=name, grid=(T // tm,),
        in_specs=p_specs + [_resident((NDEV, D, W)), row] + [ANY] * len(extra),
        out_specs=[row], out_shape=[S((T, D), f32)], sem=("parallel",), vmem=VMEM_LIMIT)
    return out


def ssm_param_bwd(lam_re, lam_im, log_dt, fr, fi, br, bi, dbbr, dbbi, dlbr, dlbi):
    def body(lr_ref, li_ref, ldt_ref, fr_ref, fi_ref, br_ref, bi_ref, dr_ref, di_ref, dlbr_ref, dlbi_ref,
             dbr_ref, dbi_ref, dlr_ref, dli_ref, dldt_ref):
        fr_, fi_ = _per_channel(fr_ref[...]), _per_channel(fi_ref[...])
        br_, bi_, dr, di = br_ref[...], bi_ref[...], dr_ref[...], di_ref[...]
        dbr_ref[...] = fr_ * dr + fi_ * di
        dbi_ref[...] = fr_ * di - fi_ * dr
        dfr = jnp.sum((dr * br_ + di * bi_).reshape(NG, GC, NP), axis=1)
        dfi = jnp.sum((di * br_ - dr * bi_).reshape(NG, GC, NP), axis=1)
        _, vjp = jax.vjp(_disc, lr_ref[...], li_ref[...], ldt_ref[...])
        dlr_ref[...], dli_ref[...], dldt = vjp((dlbr_ref[...], dlbi_ref[...], dfr, dfi))
        dldt_ref[...] = _transpose_exact(dldt)

    return pl.pallas_call(
        body, name="ssm_param_bwd",
        out_shape=[S((NG * GC, NP), f32)] * 2 + [S((NG, NP), f32)] * 2 + [S((1, NG), f32)])(
        lam_re, lam_im, log_dt, fr, fi, br, bi, dbbr, dbbi, dlbr, dlbi)


def _adam(w, g, m, v):
    m = ADAM_B1 * m + (1.0 - ADAM_B1) * g
    v = ADAM_B2 * v + (1.0 - ADAM_B2) * (g * g)
    m_hat = m / (1.0 - ADAM_B1 ** ADAM_STEP)
    v_hat = v / (1.0 - ADAM_B2 ** ADAM_STEP)
    return -ADAM_LR * (m_hat / (jnp.sqrt(v_hat) + ADAM_EPS) + ADAM_WD * w), m, v


def adam_update(w, m, v, contrib, name, rows_per_block=None, own0=None):
    R, C = w.shape
    n = contrib.shape[0]
    tr = min(rows_per_block or R, R)

    def body(w_ref, m_ref, v_ref, c_ref, *rest):
        g_ref, d_ref, nm_ref, nv_ref = rest[-4:]
        first = c_ref[0]
        if own0 is not None:
            x, y, c = _coords()
            first = jnp.where(4 * x + 2 * y + c == 0, rest[0][...], first)
        g = first.astype(f32)
        for k in range(1, n):
            g = g + c_ref[k].astype(f32)
        g_ref[...] = g
        d_ref[...], nm_ref[...], nv_ref[...] = _adam(w_ref[...], g, m_ref[...], v_ref[...])

    blk = pl.BlockSpec((tr, C), lambda i: (i, 0))
    extra, extra_specs = ([], []) if own0 is None else ([own0], [pl.BlockSpec((None, tr, C), lambda i: (0, i, 0))])
    return pl.pallas_call(
        body, name=name, grid=(R // tr,),
        in_specs=[blk, blk, blk, pl.BlockSpec((n, tr, C), lambda i: (0, i, 0))] + extra_specs,
        out_specs=[blk] * 4, out_shape=[S((R, C), f32)] * 4, compiler_params=_cp(("parallel",), VMEM_LIMIT),
    )(w, m, v, contrib, *extra)


_ROWVEC = (("b_in", IN_COLS), ("ssm_d", W), ("glu_b", W), ("ln1_g", D), ("ln1_b", D), ("ln2_g", D), ("ln2_b", D))
_HALF = NG * GC // 2
_BC_LANE = {"ssm_b_re": 0, "ssm_b_im": NP, "ssm_c_re": 0, "ssm_c_im": NP}
_PACK = {}
_r = 0
for _n, _k in _ROWVEC:
    _PACK[_n] = _r
    _r += _k // LANE
for _n, _rows in (("ssm_lambda", NG), ("scalars", 8), ("ssm_b", _HALF), ("ssm_c", _HALF), ("conv_w", 16)):
    _PACK[_n] = _r
    _r += _rows
for _n in _BC_LANE:
    _PACK[_n] = _PACK[_n[:5]]
PACK_ROWS = _r
assert PACK_ROWS % 8 == 0
_SMALL = ("b_in", "ssm_lambda_re", "ssm_lambda_im", "ssm_log_dt", "ssm_b_re", "ssm_b_im", "ssm_c_re", "ssm_c_im",
          "ssm_d", "glu_b", "ln1_g", "ln1_b", "ln2_g", "ln2_b")


def pack_grads(su, shcb, sga, sgb, dd, dglu_b, dln1_g, dln1_b, dln2_g, dln2_b, dlam_re, dlam_im, dldt, sqerr, dbr, dbi,
               dc_re, dc_im, dconv):
    nI = sga.shape[0]

    def body(su_ref, sh_ref, sga_ref, sgb_ref, dd_ref, gb_ref, l1g_ref, l1b_ref, l2g_ref, l2b_ref, lr_ref, li_ref, dt_ref,
             sq_ref, br_ref, bi_ref, cr_ref, ci_ref, cw_ref, o_ref):
        o_ref[...] = jnp.zeros_like(o_ref)

        def put_row(name, v):
            r0 = _PACK[name]
            for i in range(v.shape[1] // LANE):
                o_ref[r0 + i:r0 + i + 1, :] = v[:, i * LANE:(i + 1) * LANE]

        ga, gb = sga_ref[0], sgb_ref[0]
        for i in range(1, nI):
            ga, gb = ga + sga_ref[i], gb + sgb_ref[i]
        put_row("b_in", jnp.concatenate([su_ref[k] for k in range(W // LANE)]
                                        + [sh_ref[0:1, :], sh_ref[1:2, :], sh_ref[2:3, :], ga, gb], axis=1))
        put_row("ssm_d", jnp.concatenate([dd_ref[k] for k in range(W // LANE)], axis=1))
        put_row("glu_b", gb_ref[...])
        put_row("ln1_g", l1g_ref[...])
        put_row("ln1_b", l1b_ref[...])
        put_row("ln2_g", l2g_ref[...])
        put_row("ln2_b", l2b_ref[...])
        r0 = _PACK["ssm_lambda"]
        o_ref[r0:r0 + NG, 0:NP] = lr_ref[...]
        o_ref[r0:r0 + NG, NP:2 * NP] = li_ref[...]
        r0 = _PACK["scalars"]
        o_ref[r0:r0 + 1, 0:NG] = dt_ref[...]
        o_ref[r0 + 1:r0 + 2, 0:1] = sq_ref[...]
        for name, ref in (("ssm_b_re", br_ref), ("ssm_b_im", bi_ref), ("ssm_c_re", cr_ref), ("ssm_c_im", ci_ref)):
            r0, l0 = _PACK[name], _BC_LANE[name]
            o_ref[r0:r0 + _HALF, l0:l0 + NP] = pltpu.bitcast(ref[...].astype(bf16), f32)
        for cb in range(W // LANE):
            o_ref[_PACK["conv_w"] + 3 * cb:_PACK["conv_w"] + 3 * cb + 3, :] = cw_ref[:, cb * LANE:(cb + 1) * LANE]

    return pl.pallas_call(body, name="pack_grads", out_shape=S((PACK_ROWS, LANE), f32))(
        su, shcb, sga, sgb, dd, dglu_b, dln1_g, dln1_b, dln2_g, dln2_b, dlam_re, dlam_im, dldt, sqerr, dbr, dbi, dc_re, dc_im,
        dconv)


def adam_small(packed_all, packed_own, params):
    names = list(_SMALL) + ["conv_w"]
    flat = [a for n in names for a in params[n]]

    def body(*refs):
        p_ref, own_ref = refs[0], refs[1]
        ins = refs[2:2 + 3 * len(names)]
        outs = refs[2 + 3 * len(names):-2]
        loss_ref, g_ref = refs[-2], refs[-1]
        x, y, c = _coords()
        me = 4 * x + 2 * y + c

        def part(k, rs=slice(None), ls=slice(None)):
            return jnp.where(me == k, own_ref[rs, ls], p_ref[k, rs, ls])

        g_all = part(0)
        for k in range(1, NDEV):
            g_all = g_all + part(k)
        g_ref[...] = g_all

        def rows(name, r0, n, l0=0, lanes=LANE):
            return g_ref[_PACK[name] + r0:_PACK[name] + r0 + n, l0:l0 + lanes]

        def grad_of(name):
            if name in dict(_ROWVEC):
                return jnp.concatenate([rows(name, i, 1) for i in range(dict(_ROWVEC)[name] // LANE)], axis=1)
            if name in ("ssm_lambda_re", "ssm_lambda_im"):
                return rows("ssm_lambda", 0, NG, NP * (name == "ssm_lambda_im"), NP)[None]
            if name == "ssm_log_dt":
                return rows("scalars", 0, 1, 0, NG)
            if name in _BC_LANE:
                rs, ls = slice(_PACK[name], _PACK[name] + _HALF), slice(_BC_LANE[name], _BC_LANE[name] + NP)
                g = pltpu.bitcast(part(0, rs, ls), bf16).astype(f32)
                for k in range(1, NDEV):
                    g = g + pltpu.bitcast(part(k, rs, ls), bf16).astype(f32)
                return g.reshape(1, NG, GC, NP)
            full = jnp.concatenate([rows("conv_w", 3 * cb, 3) for cb in range(W // LANE)], axis=1)
            x, y, c = _coords()
            col0 = (4 * x + 2 * y + c) * (W // NDEV)
            sel = (lax.broadcasted_iota(jnp.int32, (W, W // NDEV), 0)
                   == lax.broadcasted_iota(jnp.int32, (W, W // NDEV), 1) + col0).astype(f32)
            return jnp.dot(full, sel, precision=HIGHEST, preferred_element_type=f32)[None]

        loss_ref[...] = 0.5 * rows("scalars", 1, 1, 0, 1)
        for i, name in enumerate(names):
            w_ref, m_ref, v_ref = ins[3 * i:3 * i + 3]
            g = grad_of(name)
            d, m, v = _adam(w_ref[...], g, m_ref[...], v_ref[...])
            outs[4 * i][...] = g
            outs[4 * i + 1][...] = d
            outs[4 * i + 2][...] = m
            outs[4 * i + 3][...] = v

    out_shape = [S(params[n][0].shape, f32) for n in names for _ in range(4)] + [S((1, 1), f32)]
    res = pl.pallas_call(body, name="adam_small", out_shape=out_shape, scratch_shapes=[pltpu.VMEM((PACK_ROWS, LANE), f32)],
                         compiler_params=_cp(None, VMEM_LIMIT))(packed_all, packed_own, *flat)
    return {n: res[4 * i:4 * i + 4] for i, n in enumerate(names)}, res[-1]


def _block_diag(wgt):
    eye = jnp.eye(8, dtype=wgt.dtype)
    out = wgt[:, :, :, None, :] * eye[None, :, None, :, None]
    return out.reshape(4, 8 * wgt.shape[2], 8 * wgt.shape[3])


def _diag_blocks(m, a, b):
    m = m.reshape(4, 8, a, 8, b)
    idx = jnp.arange(8)
    return m[:, idx, :, idx, :].transpose(1, 0, 2, 3)


def kernel(x, w_in, b_in, ssm_lambda_re, ssm_lambda_im, ssm_log_dt, ssm_b_re, ssm_b_im, ssm_c_re, ssm_c_im, ssm_d, glu_w, glu_b, w_ssm_out, conv_w, w_conv_out, w_o, ln1_g, ln1_b, w_gate, w_up, w_down, ln2_g, ln2_b, loss_target, m_w_in, m_b_in, m_ssm_lambda_re, m_ssm_lambda_im, m_ssm_log_dt, m_ssm_b_re, m_ssm_b_im, m_ssm_c_re, m_ssm_c_im, m_ssm_d, m_glu_w, m_glu_b, m_w_ssm_out, m_conv_w, m_w_conv_out, m_w_o, m_ln1_g, m_ln1_b, m_w_gate, m_w_up, m_w_down, m_ln2_g, m_ln2_b, v_w_in, v_b_in, v_ssm_lambda_re, v_ssm_lambda_im, v_ssm_log_dt, v_ssm_b_re, v_ssm_b_im, v_ssm_c_re, v_ssm_c_im, v_ssm_d, v_glu_w, v_glu_b, v_w_ssm_out, v_conv_w, v_w_conv_out, v_w_o, v_ln1_g, v_ln1_b, v_w_gate, v_w_up, v_w_down, v_ln2_g, v_ln2_b):
    given = dict(locals())
    xs = x[0]
    target = loss_target[0]

    tr = lambda a: jnp.swapaxes(a[0], 0, 1)
    win_s, glu_s, wso_s, wco_s, wo_s, wgT_s, wuT_s, wd_s = prep_weights(
        [w_in[0], glu_w[0], w_ssm_out[0], w_conv_out[0], w_o[0], tr(w_gate), tr(w_up), w_down[0]])
    (win_g,) = run_plan(GatherPlan([win_s], srcs=(0,)), "gather_w_in_u")

    lam_re, lam_im = ssm_lambda_re[0], ssm_lambda_im[0]
    ldt = ssm_log_dt[0].reshape(NG, 1)
    br2 = jnp.swapaxes(ssm_b_re[0], 1, 2).reshape(NG * GC, NP)
    bi2 = jnp.swapaxes(ssm_b_im[0], 1, 2).reshape(NG * GC, NP)
    lbr, lbi, fr, fi, bbr, bbi = ssm_params(lam_re, lam_im, ldt, br2, bi2)
    bb_t = lambda b: b.reshape(4, 8, GC, NP)
    wb = jnp.concatenate([_block_diag(bb_t(bbr)), _block_diag(bb_t(bbi))], axis=2)
    c_t = lambda c: c.reshape(4, 8, GC, NP).transpose(0, 1, 3, 2)
    wc = jnp.concatenate([_block_diag(c_t(ssm_c_re[0])), -_block_diag(c_t(ssm_c_im[0]))], axis=1)
    wbT, wcT = wb.transpose(0, 2, 1), wc.transpose(0, 2, 1)
    wb, wc, wbT, wcT = wb.astype(bf16), wc.astype(bf16), wbT.astype(bf16), wcT.astype(bf16)
    lbr_s, lbi_s = lbr.reshape(4, 1, SW), lbi.reshape(4, 1, SW)
    dsk = ssm_d[0].reshape(4, 1, LANE)

    u_nat, xb = in_proj_u(xs, win_g, b_in)
    u_p = to_perm(u_nat, 0, "perm_u")
    (y_p,), (win_g, conv_g, glu_g, wso_g, wco_g) = ssm_fwd(
        u_p, wb, wc, lbr_s, lbi_s, dsk,
        Plans([GatherPlan([win_s], srcs=tuple(range(1, NDEV)), into=[win_g]),
               GatherPlan([conv_w[0], glu_s, wso_s, wco_s])]))
    conv_f = conv_g.transpose(1, 0, 2).reshape(3, W)
    (proj,), (wo_g, wgT_g) = in_proj_rest(xb, win_g, b_in, GatherPlan([wo_s, wgT_s]))
    glu_f, wo_f = glu_g.reshape(W, W), wo_g.reshape(D, D)
    (yn,), _ = from_perm(y_p, "unperm_y")
    ya = glu_fwd(yn, glu_f, glu_b)
    yb = conv_fwd(proj, conv_f)
    (merged,), (wuT_g,) = merge_fwd(ya, yb, wso_g, wco_g, proj, GatherPlan([wuT_s]))
    wgT, wuT = wgT_g.reshape(F, D), wuT_g.reshape(F, D)
    r1, x1b = mix_ln1(merged, wo_f, xs, ln1_g, ln1_b)
    (gate, up, hid), (wd_g,) = gate_up(x1b, wgT, wuT, GatherPlan([wd_s]))
    wd_f = wd_g.reshape(F, D)
    dr2, dffn, sqerr, dln2_g, dln2_b = down_loss(hid, wd_f, r1, ln1_g, ln1_b, ln2_g, ln2_b, target)

    half_a, half_b = (0, 3, 5, 6), (1, 2, 4, 7)
    dgate, dup = ffn_bwd_act(dffn, wd_f, gate, up)
    dwd, _ = mm_tn_rows(hid, dffn, "grad_w_down")
    dwd = dwd.reshape(NDEV, FS, D)
    dwgT, (r_wd,) = mm_tn_rows(dgate, x1b, "grad_w_gate", plan=ScatterPlan([dwd], only=half_a))
    dwuT, (r_wd,) = mm_tn_rows(dup, x1b, "grad_w_up", plan=ScatterPlan([dwd], only=half_b, into=[r_wd]))
    dwgT, dwuT = dwgT.reshape(NDEV, FS, D), dwuT.reshape(NDEV, FS, D)
    (dr1, dmix, dln1_g, dln1_b), (r_wgT,) = ffn_bwd_x(dgate, dup, wgT, wuT, dr2, r1, ln1_g, ScatterPlan([dwgT]))
    (dYA, dYB, dga, dgb, sga, sgb), (r_wuT,) = merge_bwd(dmix, wo_f, ya, yb, wso_g, wco_g, proj,
                                                         ScatterPlan([dwuT], only=half_a))
    dwo, _ = mm_tn_rows(merged, dmix, "grad_w_o")
    dwo = dwo.reshape(NDEV, D // NDEV, D)
    (dya, dyb), (r_wuT,) = branches_bwd_x(dYA, dYB, wso_g, wco_g, ScatterPlan([dwuT], only=half_b, into=[r_wuT]))
    dwso = branch_bwd_w(ya, dYA, "grad_w_ssm_out")
    dwco = branch_bwd_w(yb, dYB, "grad_w_conv_out")
    dyn, dsp, gb, dglu_b = glu_bwd(yn, dya, glu_f, glu_b)
    dglu = mm_tn_rows(gb, dsp, "grad_glu_w")[0].reshape(NDEV, W // NDEV, W)
    dh, dcg, dbg, dconv, shcb = conv_bwd(proj, dyb, conv_f)
    dwin = mm_tn(xb, dgb, "grad_w_in_gb", block0=6, nblocks=NDEV)
    dwin = mm_tn(xb, dga, "grad_w_in_ga", block0=4, into=dwin)
    dwin = mm_tn(xb, dbg, "grad_w_in_bg", block0=3, into=dwin)
    dwin = mm_tn(xb, dcg, "grad_w_in_cg", block0=2, into=dwin)
    dwin = mm_tn(xb, dh, "grad_w_in_h", block0=1, into=dwin)
    dy_p = to_perm(dyn, 0, "perm_dy")
    (du_p, dwb, dwcT, dlbr_s, dlbi_s, dd, su), (r_wo, r_wso, r_wco, r_glu, r_win) = ssm_bwd(
        u_p, dy_p, wb, wbT, wcT, lbr_s, lbi_s, dsk,
        Plans([ScatterPlan([dwo, dwso, dwco, dglu]), ScatterPlan([dwin], only=tuple(range(1, NDEV)))]))

    dbb = lambda m: _diag_blocks(m, GC, NP).reshape(NG * GC, NP)
    dbr2, dbi2, dlam_re, dlam_im, dldt = ssm_param_bwd(
        lam_re, lam_im, ldt, fr, fi, br2, bi2, dbb(dwb[:, :, :SW]), dbb(dwb[:, :, SW:]),
        dlbr_s.reshape(NG, NP), dlbi_s.reshape(NG, NP))
    packed = pack_grads(su, shcb, sga, sgb, dd, dglu_b, dln1_g, dln1_b, dln2_g, dln2_b, dlam_re, dlam_im, dldt, sqerr,
                        dbr2, dbi2, dbb(dwcT[:, :, :SW]), -dbb(dwcT[:, :, SW:]), dconv)
    (du,), _ = from_perm(du_p, "unperm_du", bf16)
    dwin = mm_tn(xb, du, "grad_w_in_u", block0=0, into=dwin)

    *tail, token = tail_start(dwin, r_win, packed, lax.empty((NDEV,) + packed.shape, packed.dtype))
    rest = [(dh, 0, 1), (dcg, 0, 2), (dbg, 0, 3), (dga, 0, 4), (dga, 1, 5), (dgb, 0, 6), (dgb, 1, 7)]
    gx_rest = in_proj_bwd_x(rest, win_g, dr1, ALPHA, "in_proj_bwd_x_rest", after=token)
    grad_x = in_proj_bwd_x([(du, 0, 0)], win_g, gx_rest, 1.0, "in_proj_bwd_x_u")

    out = {}

    def put(name, res, back=lambda a: a[None]):
        out["grad_" + name], out["delta_" + name], out["new_m_" + name], out["new_v_" + name] = [back(r) for r in res]

    put("glu_w", adam_update(glu_w[0], m_glu_w[0], v_glu_w[0], r_glu, "adam_glu_w"))
    put("w_ssm_out", adam_update(w_ssm_out[0], m_w_ssm_out[0], v_w_ssm_out[0], r_wso, "adam_w_ssm_out"))
    put("w_conv_out", adam_update(w_conv_out[0], m_w_conv_out[0], v_w_conv_out[0], r_wco, "adam_w_conv_out"))
    put("w_o", adam_update(w_o[0], m_w_o[0], v_w_o[0], r_wo, "adam_w_o"))
    put("w_down", adam_update(w_down[0], m_w_down[0], v_w_down[0], r_wd, "adam_w_down", 176))
    untr = lambda a: jnp.swapaxes(a, 0, 1)[None]
    put("w_gate", adam_update(tr(w_gate), tr(m_w_gate), tr(v_w_gate), r_wgT, "adam_w_gate", 176), untr)
    res_up = adam_update(tr(w_up), tr(m_w_up), tr(v_w_up), r_wuT, "adam_w_up", 176)
    put("w_up", res_up, untr)
    r_win, small_all = tail_wait(tail[:4], tail[4:], res_up[3])
    put("w_in", adam_update(w_in[0], m_w_in[0], v_w_in[0], r_win, "adam_w_in", 256, own0=dwin))
    as_c = lambda a: jnp.swapaxes(a, 2, 3)
    params = {n: (given[n], given["m_" + n], given["v_" + n]) for n in list(_SMALL) + ["conv_w"]}
    for n in ("ssm_b_re", "ssm_b_im"):
        params[n] = tuple(as_c(a) for a in params[n])
    small, loss = adam_small(small_all, packed, params)
    for n, res in small.items():
        put(n, res, as_c if n in ("ssm_b_re", "ssm_b_im") else (lambda a: a))

    names = ["w_in", "b_in", "ssm_lambda_re", "ssm_lambda_im", "ssm_log_dt", "ssm_b_re", "ssm_b_im", "ssm_c_re", "ssm_c_im",
             "ssm_d", "glu_w", "glu_b", "w_ssm_out", "conv_w", "w_conv_out", "w_o", "ln1_g", "ln1_b", "w_gate", "w_up",
             "w_down", "ln2_g", "ln2_b"]
    return (loss.reshape(()), grad_x[None], *[out[p + n] for p in ("grad_", "delta_", "new_m_", "new_v_") for n in names])
```

```python
import functools
import math

import jax
import jax.numpy as jnp
from jax import lax
from jax.experimental import pallas as pl
from jax.experimental.pallas import tpu as pltpu

f32, bf16 = jnp.float32, jnp.bfloat16
S = jax.ShapeDtypeStruct
MESH = pl.DeviceIdType.MESH
HIGHEST = lax.Precision.HIGHEST

D = 1024
W = 512
NG, NP, GC = 32, 64, 16
F = 2816
NDEV = 8
FS = F // NDEV
IN_COLS = 8 * W
ALPHA = 2.0 ** 0.25
LN_EPS = 1e-5
ADAM_LR, ADAM_B1, ADAM_B2, ADAM_EPS, ADAM_WD, ADAM_STEP = 0.001, 0.9, 0.999, 1e-08, 0.01, 10
NC = 32
LANE = 128
SW = 4 * LANE
VMEM_LIMIT = 56 * 1024 * 1024
GRAD_DT = bf16
ANY = pl.BlockSpec(memory_space=pl.ANY)


def _cp(sem=None, vmem=None):
    return pltpu.CompilerParams(dimension_semantics=sem, vmem_limit_bytes=vmem)


def _resident(shape):
    return pl.BlockSpec(shape, lambda i: (0,) * len(shape), pipeline_mode=pl.Buffered(1))


def _dot(a, b):
    return jnp.dot(a, b, preferred_element_type=f32)


def _dot_nt(a, b):
    return lax.dot_general(a, b, (((1,), (1,)), ((), ())), preferred_element_type=f32)


def _dot_tn(a, b):
    return lax.dot_general(a, b, (((0,), (0,)), ((), ())), preferred_element_type=f32)


def _eye(n):
    return (lax.broadcasted_iota(jnp.int32, (n, n), 0) == lax.broadcasted_iota(jnp.int32, (n, n), 1)).astype(f32)


def _transpose_exact(a):
    return lax.dot_general(a, _eye(a.shape[0]), (((0,), (0,)), ((), ())), precision=HIGHEST, preferred_element_type=f32)


def _sigmoid(x):
    return 1.0 / (1.0 + jnp.exp(-x))


_GK = math.sqrt(2.0 / math.pi)


def _gelu(x):
    return 0.5 * x * (1.0 + jnp.tanh(_GK * (x + 0.044715 * x * x * x)))


def _gelu_grad(x):
    th = jnp.tanh(_GK * (x + 0.044715 * x * x * x))
    return 0.5 * (1.0 + th) + 0.5 * x * (1.0 - th * th) * _GK * (1.0 + 3.0 * 0.044715 * x * x)


ROW_PART = 256


def _row_parts(tm):
    return [slice(r, r + min(ROW_PART, tm)) for r in range(0, tm, min(ROW_PART, tm))]


def _ln_stats(r):
    mu = jnp.mean(r, axis=-1, keepdims=True)
    xc = r - mu
    var = jnp.mean(xc * xc, axis=-1, keepdims=True)
    rstd = lax.rsqrt(var + LN_EPS)
    return xc * rstd, rstd


def _ln_bwd(dy, xhat, rstd, g):
    dxh = dy * g
    m1 = jnp.mean(dxh, axis=-1, keepdims=True)
    m2 = jnp.mean(dxh * xhat, axis=-1, keepdims=True)
    return rstd * (dxh - m1 - xhat * m2)


def _coords():
    return lax.axis_index("x"), lax.axis_index("y"), lax.axis_index("c")


def _when(cond, fn):
    if cond is True:
        fn()
    else:
        pl.when(cond)(fn)


class GatherPlan:
    aliases = ()

    def __init__(self, arrs, srcs=None, into=None):
        n = self.n = len(arrs)
        self.srcs = srcs
        self.inputs = list(arrs) + list(into or [])
        if into:
            self.aliases = tuple((n + a, a) for a in range(n))
        self.out_shape = [S((NDEV,) + a.shape, a.dtype) for a in arrs]
        self.sems = [pltpu.SemaphoreType.DMA((n, 7)), pltpu.SemaphoreType.DMA((n, 7)), pltpu.SemaphoreType.DMA((n,))]

    def _has(self, dev):
        if self.srcs is None:
            return True
        idx = 4 * dev[0] + 2 * dev[1] + dev[2]
        return functools.reduce(jnp.logical_or, [idx == s for s in self.srcs])

    def _parts(self, ins, outs, sems):
        n = self.n
        send_sems, recv_sems, loc_sems = sems
        x, y, c = _coords()
        me, sib = (x, y, c), (x, y, 1 - c)
        chips = [(1 - x, y), (x, 1 - y), (1 - x, 1 - y)]

        def slot(a, dev):
            return outs[a].at[4 * dev[0] + 2 * dev[1] + dev[2]]

        def copy(a, k, block, to, src=None):
            return pltpu.make_async_remote_copy(
                src_ref=slot(a, block) if src is None else src, dst_ref=slot(a, block),
                send_sem=send_sems.at[a, k], recv_sem=recv_sems.at[a, k], device_id=to, device_id_type=MESH)

        each = [(j, chip, a) for j, chip in enumerate(chips) for a in range(n)]
        own = self._has(me)
        return dict(
            mine=lambda: [(pltpu.make_async_copy(ins[a], slot(a, me), loc_sems.at[a]), own) for a in range(n)],
            first=lambda: ([(copy(a, 0, me, sib, src=ins[a]), own) for a in range(n)]
                           + [(copy(a, 1 + j, me, (*chip, c), src=ins[a]), own) for j, chip, a in each]),
            landed=lambda: [(copy(a, 1 + j, (*chip, c), me), self._has((*chip, c))) for j, chip, a in each],
            passed=lambda: [(copy(a, 4 + j, (*chip, c), sib), self._has((*chip, c))) for j, chip, a in each],
            from_sib=lambda: ([(copy(a, 0, sib, me), self._has(sib)) for a in range(n)]
                              + [(copy(a, 4 + j, (*chip, 1 - c), me), self._has((*chip, 1 - c))) for j, chip, a in each]))

    def start(self, ins, outs, sems):
        p = self._parts(ins, outs, sems)
        for cp, cond in p["mine"]() + p["first"]():
            _when(cond, cp.start)

    def forward(self, ins, outs, sems):
        p = self._parts(ins, outs, sems)
        for (got, cond), (fwd, _) in zip(p["landed"](), p["passed"]()):
            def relay(got=got, fwd=fwd):
                got.wait_recv()
                fwd.start()

            _when(cond, relay)

    def finish(self, ins, outs, sems):
        p = self._parts(ins, outs, sems)
        for cp, cond in p["from_sib"]():
            _when(cond, cp.wait_recv)
        for cp, cond in p["first"]() + p["passed"]():
            _when(cond, cp.wait_send)
        for cp, cond in p["mine"]():
            _when(cond, cp.wait)


class ScatterPlan:
    aliases = ()

    def __init__(self, gs, only=None, into=None):
        n = self.n = len(gs)
        self.only = only
        self.inputs = list(gs) + list(into or [])
        if into:
            self.aliases = tuple((n + a, a) for a in range(n))
        self.out_shape = [S(g.shape, g.dtype) for g in gs]
        self.sems = [pltpu.SemaphoreType.DMA((n, 7)), pltpu.SemaphoreType.DMA((n, 7)), pltpu.SemaphoreType.DMA((n,))]

    def _owner(self, idx):
        if self.only is None:
            return True
        return functools.reduce(jnp.logical_or, [idx == b for b in self.only])

    def _copies(self, ins, outs, sems):
        n = self.n
        send_sems, recv_sems, loc_sems = sems
        x, y, c = _coords()
        me = 4 * x + 2 * y + c
        mine = self._owner(me)
        copies = [(pltpu.make_async_copy(ins[a].at[me], outs[a].at[me], loc_sems.at[a]), mine, None) for a in range(n)]
        for m in range(1, NDEV):
            px = 1 - x if m & 4 else x
            py = 1 - y if m & 2 else y
            pc = 1 - c if m & 1 else c
            peer = 4 * px + 2 * py + pc
            for a in range(n):
                copies.append((pltpu.make_async_remote_copy(
                    src_ref=ins[a].at[peer], dst_ref=outs[a].at[me],
                    send_sem=send_sems.at[a, m - 1], recv_sem=recv_sems.at[a, m - 1],
                    device_id=(px, py, pc), device_id_type=MESH), self._owner(peer), mine))
        return copies

    def start(self, ins, outs, sems):
        for cp, sends, _ in self._copies(ins, outs, sems):
            _when(sends, cp.start)

    def forward(self, ins, outs, sems):
        pass

    def finish(self, ins, outs, sems):
        for cp, sends, receives in self._copies(ins, outs, sems):
            if receives is None:
                _when(sends, cp.wait)
            else:
                _when(sends, cp.wait_send)
                _when(receives, cp.wait_recv)


class Plans:
    def __init__(self, plans):
        self.plans = plans
        self.inputs = [a for p in plans for a in p.inputs]
        self.out_shape = [s for p in plans for s in p.out_shape]
        self.sems = [s for p in plans for s in p.sems]
        self.aliases, i, o = [], 0, 0
        for p in plans:
            self.aliases += [(i + a, o + b) for a, b in p.aliases]
            i, o = i + len(p.inputs), o + len(p.out_shape)

    def _each(self, what, ins, outs, sems):
        i = o = s = 0
        for p in self.plans:
            ni, no, ns = len(p.inputs), len(p.out_shape), len(p.sems)
            getattr(p, what)(ins[i:i + ni], outs[o:o + no], sems[s:s + ns])
            i, o, s = i + ni, o + no, s + ns

    def start(self, ins, outs, sems):
        self._each("start", ins, outs, sems)

    def forward(self, ins, outs, sems):
        self._each("forward", ins, outs, sems)

    def finish(self, ins, outs, sems):
        self._each("finish", ins, outs, sems)


def _call(body, args, *, name, grid, in_specs, out_specs, out_shape, scratch=(), sem=None, vmem=None, plan=None,
          aliases=None, relay_step=None):
    aliases = aliases or {}
    if plan is None:
        outs = pl.pallas_call(body, name=name, grid=grid, in_specs=list(in_specs), out_specs=list(out_specs),
                              out_shape=list(out_shape), scratch_shapes=list(scratch), input_output_aliases=aliases,
                              compiler_params=_cp(sem, vmem))(*args)
        return list(outs), []
    ni, no, ns = len(in_specs), len(out_specs), len(scratch)
    pi, po = len(plan.inputs), len(plan.out_shape)
    aliases = {**aliases, **{ni + a: no + b for a, b in plan.aliases}}

    def wrapped(*refs):
        main_in, p_in = refs[:ni], refs[ni:ni + pi]
        main_out, p_out = refs[ni + pi:ni + pi + no], refs[ni + pi + no:ni + pi + no + po]
        main_scr, p_sems = refs[ni + pi + no + po:ni + pi + no + po + ns], refs[ni + pi + no + po + ns:]
        ids = [pl.program_id(d) for d in range(len(grid))]
        first = functools.reduce(jnp.logical_and, [i == 0 for i in ids])
        last = functools.reduce(jnp.logical_and, [i == g - 1 for i, g in zip(ids, grid)])

        @pl.when(first)
        def _():
            plan.start(p_in, p_out, p_sems)

        @pl.when(last if relay_step is None else ids[0] == max(relay_step, 0))
        def _():
            plan.forward(p_in, p_out, p_sems)

        body(*main_in, *main_out, *main_scr)

        @pl.when(last)
        def _():
            plan.finish(p_in, p_out, p_sems)

    outs = pl.pallas_call(
        wrapped, name=name, grid=grid, in_specs=list(in_specs) + [ANY] * pi, out_specs=list(out_specs) + [ANY] * po,
        out_shape=list(out_shape) + list(plan.out_shape), scratch_shapes=list(scratch) + list(plan.sems),
        input_output_aliases=aliases, compiler_params=_cp(("arbitrary",) * len(grid), vmem),
    )(*args, *plan.inputs)
    return list(outs[:no]), list(outs[no:])


def run_plan(plan, name):
    def body(*refs):
        ins, outs, sems = refs[:len(plan.inputs)], refs[len(plan.inputs):len(plan.inputs) + len(plan.out_shape)], \
            refs[len(plan.inputs) + len(plan.out_shape):]
        plan.start(ins, outs, sems)
        plan.forward(ins, outs, sems)
        plan.finish(ins, outs, sems)

    return pl.pallas_call(body, name=name, in_specs=[ANY] * len(plan.inputs), out_specs=[ANY] * len(plan.out_shape),
                          out_shape=list(plan.out_shape), scratch_shapes=list(plan.sems))(*plan.inputs)


def mm_tn(a, b, name, tn=512, into=None, block0=0, nblocks=None):
    T, K = a.shape
    N = b.shape[1]
    tn = min(tn, N)
    nblocks = nblocks or (N // tn if into is None else into.shape[0])

    def body(a_ref, b_ref, *rest):
        rest[-1][...] = _dot_tn(a_ref[...], b_ref[...]).astype(GRAD_DT)

    args, in_specs, aliases = [a, b], [_resident((T, K)), pl.BlockSpec((T, tn), lambda j: (0, j))], {}
    if into is not None:
        args.append(into)
        in_specs.append(ANY)
        aliases = {2: 0}
    (out,), _ = _call(body, args, name=name, grid=(N // tn,), in_specs=in_specs,
                      out_specs=[pl.BlockSpec((None, K, tn), lambda j: (block0 + j, 0, 0))],
                      out_shape=[S((nblocks, K, tn), GRAD_DT)], sem=("parallel",), vmem=VMEM_LIMIT, aliases=aliases)
    return out


def mm_tn_rows(a, b, name, tk=256, plan=None):
    T, K = a.shape
    N = b.shape[1]
    tk = min(tk, K)

    def body(a_ref, b_ref, o_ref):
        o_ref[...] = _dot_tn(a_ref[...], b_ref[...]).astype(GRAD_DT)

    (out,), sent = _call(body, [a, b], name=name, grid=(K // tk,),
                         in_specs=[pl.BlockSpec((T, tk), lambda i: (0, i)), _resident((T, N))],
                         out_specs=[pl.BlockSpec((tk, N), lambda i: (i, 0))], out_shape=[S((K, N), GRAD_DT)],
                         sem=("parallel",), vmem=VMEM_LIMIT, plan=plan)
    return out, sent


def prep_weights(ws):
    def body(*refs):
        for i in range(len(ws)):
            refs[len(ws) + i][...] = refs[i][...].astype(bf16)

    return pl.pallas_call(body, name="prep_weights", out_shape=[S(w.shape, bf16) for w in ws],
                          compiler_params=_cp(None, VMEM_LIMIT))(*ws)


REST_BLOCKS = (4, 5, 6, 7, 1, 2, 3)
REST_COLS = len(REST_BLOCKS) * W


def in_proj_u(x, win_g, b_in):
    T = x.shape[0]
    tm = min(1024, T)

    def body(x_ref, w_ref, b_ref, u_ref, xb_ref):
        xb = x_ref[...].astype(bf16)
        xb_ref[...] = xb
        u_ref[...] = _dot(xb, w_ref[...]) + b_ref[...]

    row = pl.BlockSpec((tm, D), lambda i: (i, 0))
    return pl.pallas_call(
        body, name="in_proj_u", grid=(T // tm,),
        in_specs=[row, pl.BlockSpec((None, D, W), lambda i: (0, 0, 0)), pl.BlockSpec((1, W), lambda i: (0, 0))],
        out_specs=[pl.BlockSpec((tm, W), lambda i: (i, 0)), row],
        out_shape=[S((T, W), f32), S((T, D), bf16)], compiler_params=_cp(("parallel",), VMEM_LIMIT),
    )(x, win_g, b_in)


def in_proj_rest(xb, win_g, b_in, plan):
    T = xb.shape[0]
    tm = min(512, T)

    def body(x_ref, w_ref, b_ref, o_ref):
        xb_ = x_ref[...]
        for i, k in enumerate(REST_BLOCKS):
            o_ref[:, i * W:(i + 1) * W] = _dot(xb_, w_ref[k]) + b_ref[:, k * W:(k + 1) * W]

    return _call(
        body, [xb, win_g, b_in], name="in_proj_rest", grid=(T // tm,),
        in_specs=[pl.BlockSpec((tm, D), lambda i: (i, 0)), _resident((NDEV, D, W)), _resident((1, IN_COLS))],
        out_specs=[pl.BlockSpec((tm, REST_COLS), lambda i: (i, 0))],
        out_shape=[S((T, REST_COLS), f32)], vmem=VMEM_LIMIT, plan=plan, relay_step=T // tm - 2)


def to_perm(a, cb0, name):
    T = a.shape[0]
    L = T // NC

    def body(a_ref, o_ref):
        def step(jb, carry):
            j0 = pl.multiple_of(jb * 8, 8)
            for q in range(NC // 8):
                x = jnp.stack([a_ref[pl.ds((8 * q + c) * L + j0, 8), :] for c in range(8)], axis=0)
                y = jnp.swapaxes(x, 0, 1)
                for j in range(8):
                    o_ref[pl.ds((j0 + j) * NC + 8 * q, 8), :] = y[j]
            return carry

        lax.fori_loop(0, L // 8, step, 0)

    return pl.pallas_call(
        body, name=name, grid=(W // LANE,),
        in_specs=[pl.BlockSpec((T, LANE), lambda k: (0, cb0 + k))], out_specs=pl.BlockSpec((T, LANE), lambda k: (0, k)),
        out_shape=S((T, W), f32), compiler_params=_cp(("parallel",), VMEM_LIMIT),
    )(a)


def from_perm(a, name, out_dtype=f32, plan=None):
    T = a.shape[0]
    L = T // NC

    def body(a_ref, o_ref):
        def step(jb, carry):
            j0 = pl.multiple_of(jb * 16, 16)
            for q in range(NC // 8):
                halves = []
                for h in range(2):
                    x = jnp.stack([a_ref[pl.ds((j0 + 8 * h + j) * NC + 8 * q, 8), :] for j in range(8)], axis=0)
                    halves.append(jnp.swapaxes(x, 0, 1))
                for c in range(8):
                    o_ref[pl.ds((8 * q + c) * L + j0, 16), :] = jnp.concatenate(
                        [halves[0][c], halves[1][c]], axis=0).astype(out_dtype)
            return carry

        lax.fori_loop(0, L // 16, step, 0)

    slab = pl.BlockSpec((T, LANE), lambda k: (0, k))
    return _call(body, [a], name=name, grid=(W // LANE,), in_specs=[slab], out_specs=[slab],
                 out_shape=[S((T, W), out_dtype)], sem=("parallel",), vmem=VMEM_LIMIT, plan=plan)


def _disc(lr, li, ldt):
    dt = jnp.exp(ldt)
    mag = jnp.exp(lr * dt)
    lbr = mag * jnp.cos(li * dt)
    lbi = mag * jnp.sin(li * dt)
    den = lr * lr + li * li
    nr = lbr - 1.0
    return lbr, lbi, (nr * lr + lbi * li) / den, (lbi * lr - nr * li) / den


def _per_channel(f):
    return jnp.broadcast_to(f[:, None, :], (NG, GC, NP)).reshape(NG * GC, NP)


def ssm_params(lam_re, lam_im, log_dt, br, bi):
    def body(lr_ref, li_ref, ldt_ref, br_ref, bi_ref, lbr_ref, lbi_ref, fr_ref, fi_ref, bbr_ref, bbi_ref):
        lbr, lbi, fr, fi = _disc(lr_ref[...], li_ref[...], ldt_ref[...])
        lbr_ref[...], lbi_ref[...], fr_ref[...], fi_ref[...] = lbr, lbi, fr, fi
        fr_, fi_, br_, bi_ = _per_channel(fr), _per_channel(fi), br_ref[...], bi_ref[...]
        bbr_ref[...] = fr_ * br_ - fi_ * bi_
        bbi_ref[...] = fr_ * bi_ + fi_ * br_

    return pl.pallas_call(body, name="ssm_params", out_shape=[S((NG, NP), f32)] * 4 + [S((NG * GC, NP), f32)] * 2)(
        lam_re, lam_im, log_dt, br, bi)


SCAN_UNROLL = 4


def _steps(n, body, carry):
    main = n // SCAN_UNROLL

    def trip(t, c):
        for q in range(SCAN_UNROLL):
            c = body(t * SCAN_UNROLL + q, c)
        return c

    carry = lax.fori_loop(0, main, trip, carry)
    for i in range(main * SCAN_UNROLL, n):
        carry = body(i, carry)
    return carry


def _scan_body(T):
    L = T // NC
    RB = min(512, T)
    nsq = int(round(math.log2(L)))
    assert 2 ** nsq == L and T % RB == 0 and L % 16 == 0

    def rows(i):
        return pl.ds(pl.multiple_of(i * RB, RB), RB)

    def tile(j):
        return pl.ds(j * NC if isinstance(j, int) else pl.multiple_of(j * NC, NC), NC)

    def forward_states(u_ref, wb_ref, lbr_ref, lbi_ref, sre, sim, ere, eim):
        def bproj(i, carry):
            bu = _dot(u_ref[rows(i), :].astype(bf16), wb_ref[...])
            sre[rows(i), :] = bu[:, :SW]
            sim[rows(i), :] = bu[:, SW:]
            return carry

        lax.fori_loop(0, T // RB, bproj, 0)
        for lb in range(SW // LANE):
            ls = slice(lb * LANE, (lb + 1) * LANE)
            ar = jnp.broadcast_to(lbr_ref[:, ls], (NC, LANE))
            ai = jnp.broadcast_to(lbi_ref[:, ls], (NC, LANE))

            def step(j, carry):
                xr, xi = carry
                nr = ar * xr - ai * xi + sre[tile(j), ls]
                ni = ar * xi + ai * xr + sim[tile(j), ls]
                sre[tile(j), ls] = nr
                sim[tile(j), ls] = ni
                return nr, ni

            zero = jnp.zeros((NC, LANE), f32)
            _steps(L, step, (zero, zero))
            pr, pi = lbr_ref[:, ls], lbi_ref[:, ls]
            for _ in range(nsq):
                pr, pi = pr * pr - pi * pi, 2.0 * pr * pi
            er = jnp.zeros((1, LANE), f32)
            ei = er
            ere[0:1, ls] = er
            eim[0:1, ls] = ei
            base = (L - 1) * NC
            for c in range(1, NC):
                lr_ = sre[base + c - 1:base + c, ls]
                li_ = sim[base + c - 1:base + c, ls]
                er, ei = lr_ + pr * er - pi * ei, li_ + pr * ei + pi * er
                ere[c:c + 1, ls] = er
                eim[c:c + 1, ls] = ei
            e_r, e_i = ere[:, ls].reshape(NC // 8, 8, LANE), eim[:, ls].reshape(NC // 8, 8, LANE)
            ar8, ai8 = ar[0:8], ai[0:8]

            def fix(j, carry):
                pwr, pwi = carry
                xr = sre[tile(j), ls].reshape(NC // 8, 8, LANE) + (pwr * e_r - pwi * e_i)
                xi = sim[tile(j), ls].reshape(NC // 8, 8, LANE) + (pwr * e_i + pwi * e_r)
                sre[tile(j), ls] = xr.reshape(NC, LANE)
                sim[tile(j), ls] = xi.reshape(NC, LANE)
                return pwr * ar8 - pwi * ai8, pwr * ai8 + pwi * ar8

            _steps(L, fix, (ar8, ai8))

    return L, RB, nsq, rows, tile, forward_states


def ssm_fwd(u_p, wb, wc, lbr, lbi, dsk, plan):
    T = u_p.shape[0]
    L, RB, nsq, rows, tile, forward_states = _scan_body(T)

    def body(u_ref, wb_ref, wc_ref, lbr_ref, lbi_ref, d_ref, y_ref, sre, sim, ere, eim):
        forward_states(u_ref, wb_ref, lbr_ref, lbi_ref, sre, sim, ere, eim)

        def cproj(i, carry):
            y = _dot(sre[rows(i), :].astype(bf16), wc_ref[0:SW, :]) + _dot(sim[rows(i), :].astype(bf16), wc_ref[SW:, :])
            y_ref[rows(i), :] = y + d_ref[...] * u_ref[rows(i), :]
            return carry

        lax.fori_loop(0, T // RB, cproj, 0)

    slab = pl.BlockSpec((T, LANE), lambda k: (0, k))
    return _call(
        body, [u_p, wb, wc, lbr, lbi, dsk], name="ssm_fwd", grid=(W // LANE,),
        in_specs=[slab, pl.BlockSpec((None, LANE, 2 * SW), lambda k: (k, 0, 0)),
                  pl.BlockSpec((None, 2 * SW, LANE), lambda k: (k, 0, 0)),
                  pl.BlockSpec((None, 1, SW), lambda k: (k, 0, 0)), pl.BlockSpec((None, 1, SW), lambda k: (k, 0, 0)),
                  pl.BlockSpec((None, 1, LANE), lambda k: (k, 0, 0))],
        out_specs=[slab], out_shape=[S((T, W), f32)],
        scratch=[pltpu.VMEM((T, SW), f32), pltpu.VMEM((T, SW), f32), pltpu.VMEM((NC, SW), f32), pltpu.VMEM((NC, SW), f32)],
        vmem=VMEM_LIMIT, plan=plan)


def ssm_bwd(u_p, dy_p, wb, wbT, wcT, lbr, lbi, dsk, plan):
    T = u_p.shape[0]
    L, RB, nsq, rows, tile, forward_states = _scan_body(T)

    def body(u_ref, dy_ref, wb_ref, wbT_ref, wcT_ref, lbr_ref, lbi_ref, d_ref,
             du_ref, dwb_ref, dwc_ref, dlr_ref, dli_ref, dd_ref, su_ref, sre, sim, gre, gim, ere, eim):
        forward_states(u_ref, wb_ref, lbr_ref, lbi_ref, sre, sim, ere, eim)

        def dstate(i, carry):
            g = _dot(dy_ref[rows(i), :].astype(bf16), wcT_ref[...])
            gre[rows(i), :] = g[:, :SW]
            gim[rows(i), :] = g[:, SW:]
            return carry

        lax.fori_loop(0, T // RB, dstate, 0)
        row = lax.broadcasted_iota(jnp.int32, (NC, LANE), 0)
        for lb in range(SW // LANE):
            ls = slice(lb * LANE, (lb + 1) * LANE)
            ar = jnp.broadcast_to(lbr_ref[:, ls], (NC, LANE))
            ai = jnp.broadcast_to(lbi_ref[:, ls], (NC, LANE))

            def step(i, carry):
                gr, gi = carry
                j = L - 1 - i
                nr = ar * gr + ai * gi + gre[tile(j), ls]
                ni = ar * gi - ai * gr + gim[tile(j), ls]
                gre[tile(j), ls] = nr
                gim[tile(j), ls] = ni
                return nr, ni

            zero = jnp.zeros((NC, LANE), f32)
            _steps(L, step, (zero, zero))
            pr, pi = lbr_ref[:, ls], -lbi_ref[:, ls]
            for _ in range(nsq):
                pr, pi = pr * pr - pi * pi, 2.0 * pr * pi
            er = jnp.zeros((1, LANE), f32)
            ei = er
            ere[NC - 1:NC, ls] = er
            eim[NC - 1:NC, ls] = ei
            for c in range(NC - 2, -1, -1):
                lr_ = gre[c + 1:c + 2, ls]
                li_ = gim[c + 1:c + 2, ls]
                er, ei = lr_ + pr * er - pi * ei, li_ + pr * ei + pi * er
                ere[c:c + 1, ls] = er
                eim[c:c + 1, ls] = ei
            e_r, e_i = ere[:, ls].reshape(NC // 8, 8, LANE), eim[:, ls].reshape(NC // 8, 8, LANE)
            ar8, ai8 = ar[0:8], ai[0:8]

            def fixed(j, pwr, pwi):
                gr = (gre[tile(j), ls].reshape(NC // 8, 8, LANE) + (pwr * e_r - pwi * e_i)).reshape(NC, LANE)
                gi = (gim[tile(j), ls].reshape(NC // 8, 8, LANE) + (pwr * e_i + pwi * e_r)).reshape(NC, LANE)
                gre[tile(j), ls] = gr
                gim[tile(j), ls] = gi
                return gr, gi

            def fix(i, carry):
                pwr, pwi, accr, acci = carry
                j = L - 1 - i
                gr, gi = fixed(j, pwr, pwi)
                xr, xi = sre[tile(j - 1), ls], sim[tile(j - 1), ls]
                return (pwr * ar8 + pwi * ai8, pwi * ar8 - pwr * ai8,
                        accr + gr * xr + gi * xi, acci + gi * xr - gr * xi)

            pwr, pwi, accr, acci = _steps(L - 1, fix, (ar8, -ai8, zero, zero))
            gr, gi = fixed(0, pwr, pwi)
            xr = jnp.where(row == 0, 0.0, pltpu.roll(sre[tile(L - 1), ls], 1, axis=0))
            xi = jnp.where(row == 0, 0.0, pltpu.roll(sim[tile(L - 1), ls], 1, axis=0))
            accr = accr + gr * xr + gi * xi
            acci = acci + gi * xr - gr * xi
            dlr_ref[:, ls] = jnp.sum(accr, axis=0, keepdims=True)
            dli_ref[:, ls] = jnp.sum(acci, axis=0, keepdims=True)

        dwb_ref[...] = jnp.zeros_like(dwb_ref)
        dwc_ref[...] = jnp.zeros_like(dwc_ref)
        dd_ref[...] = jnp.zeros_like(dd_ref)
        su_ref[...] = jnp.zeros_like(su_ref)

        def finish(i, carry):
            u32, dy32 = u_ref[rows(i), :], dy_ref[rows(i), :]
            ub, dyb = u32.astype(bf16), dy32.astype(bf16)
            gr, gi = gre[rows(i), :].astype(bf16), gim[rows(i), :].astype(bf16)
            du = _dot(gr, wbT_ref[0:SW, :]) + _dot(gi, wbT_ref[SW:, :]) + dy32 * d_ref[...]
            du_ref[rows(i), :] = du
            su_ref[...] += jnp.sum(du, axis=0, keepdims=True)
            dwb_ref[:, 0:SW] += _dot_tn(ub, gr)
            dwb_ref[:, SW:] += _dot_tn(ub, gi)
            dwc_ref[:, 0:SW] += _dot_tn(dyb, sre[rows(i), :].astype(bf16))
            dwc_ref[:, SW:] += _dot_tn(dyb, sim[rows(i), :].astype(bf16))
            dd_ref[...] += jnp.sum(dy32 * u32, axis=0, keepdims=True)
            return carry

        lax.fori_loop(0, T // RB, finish, 0)

    slab = pl.BlockSpec((T, LANE), lambda k: (0, k))
    wide = pl.BlockSpec((None, LANE, 2 * SW), lambda k: (k, 0, 0))
    tall = pl.BlockSpec((None, 2 * SW, LANE), lambda k: (k, 0, 0))
    vec = pl.BlockSpec((None, 1, SW), lambda k: (k, 0, 0))
    vecd = pl.BlockSpec((None, 1, LANE), lambda k: (k, 0, 0))
    nslab = W // LANE
    return _call(
        body, [u_p, dy_p, wb, wbT, wcT, lbr, lbi, dsk], name="ssm_bwd", grid=(nslab,),
        in_specs=[slab, slab, wide, tall, wide, vec, vec, vecd],
        out_specs=[slab, wide, wide, vec, vec, vecd, vecd],
        out_shape=[S((T, W), f32), S((nslab, LANE, 2 * SW), f32), S((nslab, LANE, 2 * SW), f32),
                   S((nslab, 1, SW), f32), S((nslab, 1, SW), f32), S((nslab, 1, LANE), f32), S((nslab, 1, LANE), f32)],
        scratch=[pltpu.VMEM((T, SW), f32)] * 4 + [pltpu.VMEM((NC, SW), f32)] * 2, vmem=VMEM_LIMIT, plan=plan)


def glu_fwd(yn, glu_w, glu_b):
    T = yn.shape[0]
    tm = min(512, T)

    def body(y_ref, w_ref, b_ref, o_ref):
        g = _gelu(y_ref[...])
        o_ref[...] = (g * _sigmoid(_dot(g.astype(bf16), w_ref[...]) + b_ref[...])).astype(bf16)

    return pl.pallas_call(
        body, name="glu_fwd", grid=(T // tm,),
        in_specs=[pl.BlockSpec((tm, W), lambda i: (i, 0)), pl.BlockSpec((W, W), lambda i: (0, 0)), pl.BlockSpec((1, W), lambda i: (0, 0))],
        out_specs=pl.BlockSpec((tm, W), lambda i: (i, 0)), out_shape=S((T, W), bf16), compiler_params=_cp(("parallel",)),
    )(yn, glu_w, glu_b)


def _shift_rows(cur, prev8, k):
    return pltpu.roll(jnp.concatenate([prev8, cur], axis=0), k, axis=0)[8:]


def _lift_rows(cur, next8, k):
    n = cur.shape[0]
    return pltpu.roll(jnp.concatenate([cur, next8], axis=0), n + 8 - k, axis=0)[:n]


def conv_fwd(proj, conv_w):
    T = proj.shape[0]
    RB = min(512, T)

    def body(h_ref, c_ref, b_ref, w_ref, o_ref):
        w0, w1, w2 = w_ref[0:1, :], w_ref[1:2, :], w_ref[2:3, :]

        def blk(i, carry):
            r0 = pl.multiple_of(i * RB, RB)
            rs = pl.ds(r0, RB)
            ch = c_ref[rs, :] * h_ref[rs, :]
            pr = pl.ds(jnp.maximum(r0 - 8, 0), 8)
            prev = jnp.where(i > 0, c_ref[pr, :] * h_ref[pr, :], 0.0)
            z = w2 * ch + w1 * _shift_rows(ch, prev, 1) + w0 * _shift_rows(ch, prev, 2)
            o_ref[rs, :] = (b_ref[rs, :] * z).astype(bf16)
            return carry

        lax.fori_loop(0, T // RB, blk, 0)

    nb = W // LANE
    return pl.pallas_call(
        body, name="conv_fwd", grid=(nb,),
        in_specs=[pl.BlockSpec((T, LANE), lambda k: (0, 4 * nb + k)), pl.BlockSpec((T, LANE), lambda k: (0, 5 * nb + k)),
                  pl.BlockSpec((T, LANE), lambda k: (0, 6 * nb + k)),pl.BlockSpec((3, LANE), lambda k: (0, k))],
        out_specs=pl.BlockSpec((T, LANE), lambda k: (0, k)), out_shape=S((T, W), bf16),
        compiler_params=_cp(("parallel",), VMEM_LIMIT),
    )(proj, proj, proj, conv_w)


def _dense_columns(blocks_ref, dense_ref):
    for k in range(NDEV):
        dense_ref[:, k * LANE:(k + 1) * LANE] = blocks_ref[k]


def merge_fwd(ya, yb, wso, wco, proj, plan):
    T = ya.shape[0]
    tm = min(1024, T)

    def body(ya_ref, yb_ref, wa_ref, wb_ref, ga_ref, gb_ref, o_ref, wa_s, wb_s):
        @pl.when(pl.program_id(0) == 0)
        def _():
            _dense_columns(wa_ref, wa_s)
            _dense_columns(wb_ref, wb_s)

        o_ref[...] = (_sigmoid(ga_ref[...]) * _dot(ya_ref[...], wa_s[...])
                      + _sigmoid(gb_ref[...]) * _dot(yb_ref[...], wb_s[...])).astype(bf16)

    act = pl.BlockSpec((tm, W), lambda i: (i, 0))
    return _call(
        body, [ya, yb, wso, wco, proj, proj], name="merge_fwd", grid=(T // tm,),
        in_specs=[act, act, _resident((NDEV, W, LANE)), _resident((NDEV, W, LANE)),
                  pl.BlockSpec((tm, D), lambda i: (i, 0)), pl.BlockSpec((tm, D), lambda i: (i, 1))],
        out_specs=[pl.BlockSpec((tm, D), lambda i: (i, 0))], out_shape=[S((T, D), bf16)],
        scratch=[pltpu.VMEM((W, D), bf16), pltpu.VMEM((W, D), bf16)], vmem=VMEM_LIMIT, plan=plan)


def mix_ln1(merged, w_o, x, g1, b1, plan):
    T = x.shape[0]
    tm = min(512, T)

    def body(m_ref, w_ref, x_ref, g_ref, b_ref, r_ref, x1_ref):
        for rs in _row_parts(tm):
            r = ALPHA * x_ref[rs, :] + _dot(m_ref[rs, :], w_ref[...])
            r_ref[rs, :] = r
            xhat, _ = _ln_stats(r)
            x1_ref[rs, :] = (xhat * g_ref[...] + b_ref[...]).astype(bf16)

    row = pl.BlockSpec((tm, D), lambda i: (i, 0))
    vec = pl.BlockSpec((1, D), lambda i: (0, 0))
    return _call(
        body, [merged, w_o, x, g1, b1], name="mix_ln1", grid=(T // tm,),
        in_specs=[row, _resident((D, D)), row, vec, vec],
        out_specs=[row, row], out_shape=[S((T, D), f32), S((T, D), bf16)], sem=("parallel",), vmem=VMEM_LIMIT, plan=plan,
        relay_step=T // tm - 2)


FT = 256


def gate_up(x1b, wgT, wuT, plan):
    T = x1b.shape[0]
    tm = min(512, T)

    def body(x_ref, wg_ref, wu_ref, g_ref, u_ref, h_ref):
        x = x_ref[...]
        for n in range(F // FT):
            cs = slice(n * FT, (n + 1) * FT)
            g = _dot_nt(x, wg_ref[cs, :])
            u = _dot_nt(x, wu_ref[cs, :])
            g_ref[:, cs] = g.astype(bf16)
            u_ref[:, cs] = u.astype(bf16)
            h_ref[:, cs] = (g * _sigmoid(g) * u).astype(bf16)

    osp = pl.BlockSpec((tm, F), lambda i: (i, 0))
    return _call(
        body, [x1b, wgT, wuT], name="gate_up", grid=(T // tm,),
        in_specs=[pl.BlockSpec((tm, D), lambda i: (i, 0)), _resident((F, D)), _resident((F, D))],
        out_specs=[osp, osp, osp], out_shape=[S((T, F), bf16)] * 3, vmem=VMEM_LIMIT, plan=plan, relay_step=T // tm - 3)


def down_loss(hid, w_down, r1, g1, b1, g2, b2, target):
    T = hid.shape[0]
    tm = min(512, T)

    def body(h_ref, w_ref, r1_ref, g1_ref, b1_ref, g2_ref, b2_ref, t_ref, dr_ref, drb_ref, loss_ref, dg_ref, db_ref):
        @pl.when(pl.program_id(0) == 0)
        def _():
            loss_ref[...] = jnp.zeros_like(loss_ref)
            dg_ref[...] = jnp.zeros_like(dg_ref)
            db_ref[...] = jnp.zeros_like(db_ref)

        for rs in _row_parts(tm):
            xh1, _ = _ln_stats(r1_ref[rs, :])
            x1 = xh1 * g1_ref[...] + b1_ref[...]
            r2 = ALPHA * x1 + _dot(h_ref[rs, :], w_ref[...])
            xh2, rstd2 = _ln_stats(r2)
            err = xh2 * g2_ref[...] + b2_ref[...] - t_ref[rs, :]
            loss_ref[...] += jnp.sum(jnp.mean(err * err, axis=-1, keepdims=True), axis=0, keepdims=True)
            dy = err * (1.0 / D)
            dg_ref[...] += jnp.sum(dy * xh2, axis=0, keepdims=True)
            db_ref[...] += jnp.sum(dy, axis=0, keepdims=True)
            dr = _ln_bwd(dy, xh2, rstd2, g2_ref[...])
            dr_ref[rs, :] = dr
            drb_ref[rs, :] = dr.astype(bf16)

    row = pl.BlockSpec((tm, D), lambda i: (i, 0))
    vec = pl.BlockSpec((1, D), lambda i: (0, 0))
    return pl.pallas_call(
        body, name="down_loss", grid=(T // tm,),
        in_specs=[pl.BlockSpec((tm, F), lambda i: (i, 0)), _resident((F, D)), row, vec, vec, vec, vec, row],
        out_specs=[row, row, pl.BlockSpec((1, 1), lambda i: (0, 0)), vec, vec],
        out_shape=[S((T, D), f32), S((T, D), bf16), S((1, 1), f32), S((1, D), f32), S((1, D), f32)],
        compiler_params=_cp(("arbitrary",), VMEM_LIMIT),
    )(hid, w_down, r1, g1, b1, g2, b2, target)


def ffn_bwd_act(dffn, w_down, gate, up, plan):
    T = dffn.shape[0]
    tm = min(512, T)

    def body(d_ref, w_ref, g_ref, u_ref, dg_ref, du_ref):
        for n in range(F // FT):
            cs = slice(n * FT, (n + 1) * FT)
            for rs in _row_parts(tm):
                dh = _dot_nt(d_ref[rs, :], w_ref[cs, :])
                g, u = g_ref[rs, cs].astype(f32), u_ref[rs, cs].astype(f32)
                sg = _sigmoid(g)
                t = g * sg
                du_ref[rs, cs] = (dh * t).astype(bf16)
                dg_ref[rs, cs] = (dh * u * (sg + t - t * sg)).astype(bf16)

    osp = pl.BlockSpec((tm, F), lambda i: (i, 0))
    return _call(
        body, [dffn, w_down, gate, up], name="ffn_bwd_act", grid=(T // tm,),
        in_specs=[pl.BlockSpec((tm, D), lambda i: (i, 0)), _resident((F, D)), osp, osp],
        out_specs=[osp, osp], out_shape=[S((T, F), bf16)] * 2, sem=("parallel",), vmem=VMEM_LIMIT, plan=plan)


def ffn_bwd_x(dgate, dup, wgT, wuT, dr2, r1, g1, plan):
    T = dr2.shape[0]
    tm = min(512, T)

    def body(dg_ref, du_ref, wg_ref, wu_ref, dr2_ref, r1_ref, g1_ref, dr_ref, drb_ref, dgam_ref, dbet_ref):
        @pl.when(pl.program_id(0) == 0)
        def _():
            dgam_ref[...] = jnp.zeros_like(dgam_ref)
            dbet_ref[...] = jnp.zeros_like(dbet_ref)

        for rs in _row_parts(tm):
            dx1 = ALPHA * dr2_ref[rs, :] + _dot(dg_ref[rs, :], wg_ref[...]) + _dot(du_ref[rs, :], wu_ref[...])
            xh, rstd = _ln_stats(r1_ref[rs, :])
            dgam_ref[...] += jnp.sum(dx1 * xh, axis=0, keepdims=True)
            dbet_ref[...] += jnp.sum(dx1, axis=0, keepdims=True)
            dr = _ln_bwd(dx1, xh, rstd, g1_ref[...])
            dr_ref[rs, :] = dr
            drb_ref[rs, :] = dr.astype(bf16)

    row = pl.BlockSpec((tm, D), lambda i: (i, 0))
    wide = pl.BlockSpec((tm, F), lambda i: (i, 0))
    wsp = _resident((F, D))
    vec = pl.BlockSpec((1, D), lambda i: (0, 0))
    return _call(
        body, [dgate, dup, wgT, wuT, dr2, r1, g1], name="ffn_bwd_x", grid=(T // tm,),
        in_specs=[wide, wide, wsp, wsp, row, row, vec],
        out_specs=[row, row, vec, vec], out_shape=[S((T, D), f32), S((T, D), bf16), S((1, D), f32), S((1, D), f32)],
        vmem=VMEM_LIMIT, plan=plan)


def merge_bwd(dmix, w_o, ya, yb, wso, wco, proj, plan):
    T = dmix.shape[0]
    tm = min(512, T)

    def body(dm_ref, wo_ref, ya_ref, yb_ref, wa_ref, wb_ref, ga_ref, gb_ref, dya_ref, dyb_ref, dga_ref, dgb_ref, sa_ref, sb_ref,
             wa_s, wb_s):
        @pl.when(pl.program_id(0) == 0)
        def _():
            _dense_columns(wa_ref, wa_s)
            _dense_columns(wb_ref, wb_s)

        dmer = _dot_nt(dm_ref[...], wo_ref[...])
        sa, sb = _sigmoid(ga_ref[...]), _sigmoid(gb_ref[...])
        dya_ref[...] = (dmer * sa).astype(bf16)
        dyb_ref[...] = (dmer * sb).astype(bf16)
        dga = dmer * _dot(ya_ref[...], wa_s[...]) * sa * (1.0 - sa)
        dgb = dmer * _dot(yb_ref[...], wb_s[...]) * sb * (1.0 - sb)
        dga_ref[...] = dga.astype(bf16)
        dgb_ref[...] = dgb.astype(bf16)
        sa_ref[...] = jnp.sum(dga, axis=0, keepdims=True)
        sb_ref[...] = jnp.sum(dgb, axis=0, keepdims=True)

    act = pl.BlockSpec((tm, W), lambda i: (i, 0))
    osp = pl.BlockSpec((tm, D), lambda i: (i, 0))
    ssp = pl.BlockSpec((None, 1, D), lambda i: (i, 0, 0))
    return _call(
        body, [dmix, w_o, ya, yb, wso, wco, proj, proj], name="merge_bwd", grid=(T // tm,),
        in_specs=[osp, _resident((D, D)), act, act, _resident((NDEV, W, LANE)), _resident((NDEV, W, LANE)),
                  pl.BlockSpec((tm, D), lambda i: (i, 0)), pl.BlockSpec((tm, D), lambda i: (i, 1))],
        out_specs=[osp, osp, osp, osp, ssp, ssp],
        out_shape=[S((T, D), bf16)] * 4 + [S((T // tm, 1, D), f32)] * 2,
        scratch=[pltpu.VMEM((W, D), bf16), pltpu.VMEM((W, D), bf16)], vmem=VMEM_LIMIT, plan=plan)


def branches_bwd_x(dYA, dYB, wso, wco, plan):
    T = dYA.shape[0]
    tm = min(1024, T)

    def body(da_ref, db_ref, wa_ref, wb_ref, oa_ref, ob_ref, wa_s, wb_s):
        @pl.when(pl.program_id(0) == 0)
        def _():
            _dense_columns(wa_ref, wa_s)
            _dense_columns(wb_ref, wb_s)

        oa_ref[...] = _dot_nt(da_ref[...], wa_s[...])
        ob_ref[...] = _dot_nt(db_ref[...], wb_s[...])

    row = pl.BlockSpec((tm, D), lambda i: (i, 0))
    osp = pl.BlockSpec((tm, W), lambda i: (i, 0))
    return _call(
        body, [dYA, dYB, wso, wco], name="branches_bwd_x", grid=(T // tm,),
        in_specs=[row, row, _resident((NDEV, W, LANE)), _resident((NDEV, W, LANE))],
        out_specs=[osp, osp], out_shape=[S((T, W), f32)] * 2,
        scratch=[pltpu.VMEM((W, D), bf16), pltpu.VMEM((W, D), bf16)], vmem=VMEM_LIMIT, plan=plan)


def branch_bwd_w(act, dY, name):
    T = act.shape[0]
    tk = W // 2

    def body(a_ref, d_ref, o_ref):
        res = _dot_tn(a_ref[...], d_ref[...])
        for k in range(NDEV):
            o_ref[k] = res[:, k * LANE:(k + 1) * LANE].astype(o_ref.dtype)

    return pl.pallas_call(
        body, name=name, grid=(W // tk,),
        in_specs=[pl.BlockSpec((T, tk), lambda i: (0, i)), _resident((T, D))],
        out_specs=pl.BlockSpec((NDEV, tk, LANE), lambda i: (0, i, 0)), out_shape=S((NDEV, W, LANE), GRAD_DT),
        compiler_params=_cp(("parallel",), VMEM_LIMIT),
    )(act, dY)


def glu_bwd(yn, dya, glu_w, glu_b):
    T = yn.shape[0]
    tm = min(512, T)

    def body(y_ref, d_ref, w_ref, b_ref, dy_ref, dsp_ref, g_ref, db_ref):
        @pl.when(pl.program_id(0) == 0)
        def _():
            db_ref[...] = jnp.zeros_like(db_ref)

        y, dya_ = y_ref[...], d_ref[...]
        g = _gelu(y)
        gb = g.astype(bf16)
        s = _sigmoid(_dot(gb, w_ref[...]) + b_ref[...])
        dsp = dya_ * g * s * (1.0 - s)
        dspb = dsp.astype(bf16)
        dg = dya_ * s + _dot_nt(dspb, w_ref[...])
        dy_ref[...] = dg * _gelu_grad(y)
        dsp_ref[...] = dspb
        g_ref[...] = gb
        db_ref[...] += jnp.sum(dsp, axis=0, keepdims=True)

    row = pl.BlockSpec((tm, W), lambda i: (i, 0))
    vec = pl.BlockSpec((1, W), lambda i: (0, 0))
    return pl.pallas_call(
        body, name="glu_bwd", grid=(T // tm,),
        in_specs=[row, row, pl.BlockSpec((W, W), lambda i: (0, 0)), vec],
        out_specs=[row, row, row, vec], out_shape=[S((T, W), f32), S((T, W), bf16), S((T, W), bf16), S((1, W), f32)],
        compiler_params=_cp(("arbitrary",)),
    )(yn, dya, glu_w, glu_b)


def conv_bwd(proj, dyb, conv_w):
    T = proj.shape[0]
    RB = min(512, T)
    nrb = T // RB

    def body(h_ref, c_ref, b_ref, d_ref, w_ref, dh_ref, dc_ref, db_ref, dw_ref, s_ref):
        w0, w1, w2 = w_ref[0:1, :], w_ref[1:2, :], w_ref[2:3, :]

        def blk(i, carry):
            a0, a1, a2, sh, sc, sb = carry
            r0 = pl.multiple_of(i * RB, RB)
            rs = pl.ds(r0, RB)
            h, cg, bg, dyb_ = h_ref[rs, :], c_ref[rs, :], b_ref[rs, :], d_ref[rs, :]
            ch = cg * h
            pr = pl.ds(jnp.maximum(r0 - 8, 0), 8)
            prev = jnp.where(i > 0, c_ref[pr, :] * h_ref[pr, :], 0.0)
            ch1, ch2 = _shift_rows(ch, prev, 1), _shift_rows(ch, prev, 2)
            dbg = dyb_ * (w2 * ch + w1 * ch1 + w0 * ch2)
            db_ref[rs, :] = dbg.astype(bf16)
            dz = dyb_ * bg
            nx = pl.ds(jnp.minimum(r0 + RB, T - 8), 8)
            nxt = jnp.where(i < nrb - 1, d_ref[nx, :] * b_ref[nx, :], 0.0)
            dch = w2 * dz + w1 * _lift_rows(dz, nxt, 1) + w0 * _lift_rows(dz, nxt, 2)
            dcg, dh = dch * h, dch * cg
            dc_ref[rs, :] = dcg.astype(bf16)
            dh_ref[rs, :] = dh.astype(bf16)
            col = lambda v: jnp.sum(v, axis=0, keepdims=True)
            return (a0 + col(dz * ch2), a1 + col(dz * ch1), a2 + col(dz * ch), sh + col(dh), sc + col(dcg), sb + col(dbg))

        zero = jnp.zeros((1, LANE), f32)
        a0, a1, a2, sh, sc, sb = lax.fori_loop(0, nrb, blk, (zero,) * 6)
        dw_ref[0:1, :] = a0
        dw_ref[1:2, :] = a1
        dw_ref[2:3, :] = a2
        s_ref[0:1, :] = sh
        s_ref[1:2, :] = sc
        s_ref[2:3, :] = sb

    nb = W // LANE
    slab = pl.BlockSpec((T, LANE), lambda k: (0, k))
    three = pl.BlockSpec((3, LANE), lambda k: (0, k))
    return pl.pallas_call(
        body, name="conv_bwd", grid=(nb,),
        in_specs=[pl.BlockSpec((T, LANE), lambda k: (0, 4 * nb + k)), pl.BlockSpec((T, LANE), lambda k: (0, 5 * nb + k)),
                  pl.BlockSpec((T, LANE), lambda k: (0, 6 * nb + k)),slab, three],
        out_specs=[slab, slab, slab, three, three],
        out_shape=[S((T, W), bf16)] * 3 + [S((3, W), f32)] * 2, compiler_params=_cp(("parallel",), VMEM_LIMIT),
    )(proj, proj, proj, dyb, conv_w)


def in_proj_bwd_x(parts, win_g, base, scale, name, plan=None):
    T = base.shape[0]
    tm = min(512, T)
    n = len(parts)

    def body(*refs):
        p_refs, w_ref, b_ref, o_ref = refs[:n], refs[n], refs[n + 1], refs[n + 2]
        acc = scale * b_ref[...]
        for p_ref, (_, _, k) in zip(p_refs, parts):
            acc += _dot_nt(p_ref[...], w_ref[k])
        o_ref[...] = acc

    row = pl.BlockSpec((tm, D), lambda i: (i, 0))
    p_specs = [pl.BlockSpec((tm, W), (lambda i, cb=cb: (i, cb))) for _, cb, _ in parts]
    return _call(
        body, [a for a, _, _ in parts] + [win_g, base], name=name, grid=(T // tm,),
        in_specs=p_specs + [_resident((NDEV, D, W)), row],
        out_specs=[row], out_shape=[S((T, D), f32)], vmem=VMEM_LIMIT, plan=plan)


def ssm_param_bwd(lam_re, lam_im, log_dt, fr, fi, br, bi, dbbr, dbbi, dlbr, dlbi):
    def body(lr_ref, li_ref, ldt_ref, fr_ref, fi_ref, br_ref, bi_ref, dr_ref, di_ref, dlbr_ref, dlbi_ref,
             dbr_ref, dbi_ref, dlr_ref, dli_ref, dldt_ref):
        fr_, fi_ = _per_channel(fr_ref[...]), _per_channel(fi_ref[...])
        br_, bi_, dr, di = br_ref[...], bi_ref[...], dr_ref[...], di_ref[...]
        dbr_ref[...] = fr_ * dr + fi_ * di
        dbi_ref[...] = fr_ * di - fi_ * dr
        dfr = jnp.sum((dr * br_ + di * bi_).reshape(NG, GC, NP), axis=1)
        dfi = jnp.sum((di * br_ - dr * bi_).reshape(NG, GC, NP), axis=1)
        _, vjp = jax.vjp(_disc, lr_ref[...], li_ref[...], ldt_ref[...])
        dlr_ref[...], dli_ref[...], dldt = vjp((dlbr_ref[...], dlbi_ref[...], dfr, dfi))
        dldt_ref[...] = _transpose_exact(dldt)

    return pl.pallas_call(
        body, name="ssm_param_bwd",
        out_shape=[S((NG * GC, NP), f32)] * 2 + [S((NG, NP), f32)] * 2 + [S((1, NG), f32)])(
        lam_re, lam_im, log_dt, fr, fi, br, bi, dbbr, dbbi, dlbr, dlbi)


def _adam(w, g, m, v):
    m = ADAM_B1 * m + (1.0 - ADAM_B1) * g
    v = ADAM_B2 * v + (1.0 - ADAM_B2) * (g * g)
    m_hat = m / (1.0 - ADAM_B1 ** ADAM_STEP)
    v_hat = v / (1.0 - ADAM_B2 ** ADAM_STEP)
    return -ADAM_LR * (m_hat / (jnp.sqrt(v_hat) + ADAM_EPS) + ADAM_WD * w), m, v


def adam_update(w, m, v, contrib, name, rows_per_block=None):
    R, C = w.shape
    n = contrib.shape[0]
    tr = min(rows_per_block or R, R)

    def body(w_ref, m_ref, v_ref, c_ref, g_ref, d_ref, nm_ref, nv_ref):
        g = c_ref[0].astype(f32)
        for k in range(1, n):
            g = g + c_ref[k].astype(f32)
        g_ref[...] = g
        d_ref[...], nm_ref[...], nv_ref[...] = _adam(w_ref[...], g, m_ref[...], v_ref[...])

    blk = pl.BlockSpec((tr, C), lambda i: (i, 0))
    return pl.pallas_call(
        body, name=name, grid=(R // tr,), in_specs=[blk, blk, blk, pl.BlockSpec((n, tr, C), lambda i: (0, i, 0))],
        out_specs=[blk] * 4, out_shape=[S((R, C), f32)] * 4, compiler_params=_cp(("parallel",), VMEM_LIMIT),
    )(w, m, v, contrib)


_ROWVEC = (("b_in", IN_COLS), ("ssm_d", W), ("glu_b", W), ("ln1_g", D), ("ln1_b", D), ("ln2_g", D), ("ln2_b", D))
_HALF = NG * GC // 2
_BC_LANE = {"ssm_b_re": 0, "ssm_b_im": NP, "ssm_c_re": 0, "ssm_c_im": NP}
_PACK = {}
_r = 0
for _n, _k in _ROWVEC:
    _PACK[_n] = _r
    _r += _k // LANE
for _n, _rows in (("ssm_lambda", NG), ("scalars", 8), ("ssm_b", _HALF), ("ssm_c", _HALF), ("conv_w", 16)):
    _PACK[_n] = _r
    _r += _rows
for _n in _BC_LANE:
    _PACK[_n] = _PACK[_n[:5]]
PACK_ROWS = _r
assert PACK_ROWS % 8 == 0
_SMALL = ("b_in", "ssm_lambda_re", "ssm_lambda_im", "ssm_log_dt", "ssm_b_re", "ssm_b_im", "ssm_c_re", "ssm_c_im",
          "ssm_d", "glu_b", "ln1_g", "ln1_b", "ln2_g", "ln2_b")


def pack_grads(su, shcb, sga, sgb, dd, dglu_b, dln1_g, dln1_b, dln2_g, dln2_b, dlam_re, dlam_im, dldt, sqerr, dbr, dbi,
               dc_re, dc_im, dconv):
    nI = sga.shape[0]

    def body(su_ref, sh_ref, sga_ref, sgb_ref, dd_ref, gb_ref, l1g_ref, l1b_ref, l2g_ref, l2b_ref, lr_ref, li_ref, dt_ref,
             sq_ref, br_ref, bi_ref, cr_ref, ci_ref, cw_ref, o_ref):
        o_ref[...] = jnp.zeros_like(o_ref)

        def put_row(name, v):
            r0 = _PACK[name]
            for i in range(v.shape[1] // LANE):
                o_ref[r0 + i:r0 + i + 1, :] = v[:, i * LANE:(i + 1) * LANE]

        ga, gb = sga_ref[0], sgb_ref[0]
        for i in range(1, nI):
            ga, gb = ga + sga_ref[i], gb + sgb_ref[i]
        put_row("b_in", jnp.concatenate([su_ref[k] for k in range(W // LANE)]
                                        + [sh_ref[0:1, :], sh_ref[1:2, :], sh_ref[2:3, :], ga, gb], axis=1))
        put_row("ssm_d", jnp.concatenate([dd_ref[k] for k in range(W // LANE)], axis=1))
        put_row("glu_b", gb_ref[...])
        put_row("ln1_g", l1g_ref[...])
        put_row("ln1_b", l1b_ref[...])
        put_row("ln2_g", l2g_ref[...])
        put_row("ln2_b", l2b_ref[...])
        r0 = _PACK["ssm_lambda"]
        o_ref[r0:r0 + NG, 0:NP] = lr_ref[...]
        o_ref[r0:r0 + NG, NP:2 * NP] = li_ref[...]
        r0 = _PACK["scalars"]
        o_ref[r0:r0 + 1, 0:NG] = dt_ref[...]
        o_ref[r0 + 1:r0 + 2, 0:1] = sq_ref[...]
        for name, ref in (("ssm_b_re", br_ref), ("ssm_b_im", bi_ref), ("ssm_c_re", cr_ref), ("ssm_c_im", ci_ref)):
            r0, l0 = _PACK[name], _BC_LANE[name]
            o_ref[r0:r0 + _HALF, l0:l0 + NP] = pltpu.bitcast(ref[...].astype(bf16), f32)
        for cb in range(W // LANE):
            o_ref[_PACK["conv_w"] + 3 * cb:_PACK["conv_w"] + 3 * cb + 3, :] = cw_ref[:, cb * LANE:(cb + 1) * LANE]

    return pl.pallas_call(body, name="pack_grads", out_shape=S((PACK_ROWS, LANE), f32))(
        su, shcb, sga, sgb, dd, dglu_b, dln1_g, dln1_b, dln2_g, dln2_b, dlam_re, dlam_im, dldt, sqerr, dbr, dbi, dc_re, dc_im,
        dconv)


def adam_small(packed_all, params):
    names = list(_SMALL) + ["conv_w"]
    flat = [a for n in names for a in params[n]]

    def body(*refs):
        p_ref = refs[0]
        ins = refs[1:1 + 3 * len(names)]
        outs = refs[1 + 3 * len(names):-2]
        loss_ref, g_ref = refs[-2], refs[-1]

        def part(k, rs=slice(None), ls=slice(None)):
            return p_ref[k, rs, ls]

        g_all = part(0)
        for k in range(1, NDEV):
            g_all = g_all + part(k)
        g_ref[...] = g_all

        def rows(name, r0, n, l0=0, lanes=LANE):
            return g_ref[_PACK[name] + r0:_PACK[name] + r0 + n, l0:l0 + lanes]

        def grad_of(name):
            if name in dict(_ROWVEC):
                return jnp.concatenate([rows(name, i, 1) for i in range(dict(_ROWVEC)[name] // LANE)], axis=1)
            if name in ("ssm_lambda_re", "ssm_lambda_im"):
                return rows("ssm_lambda", 0, NG, NP * (name == "ssm_lambda_im"), NP)[None]
            if name == "ssm_log_dt":
                return rows("scalars", 0, 1, 0, NG)
            if name in _BC_LANE:
                rs, ls = slice(_PACK[name], _PACK[name] + _HALF), slice(_BC_LANE[name], _BC_LANE[name] + NP)
                g = pltpu.bitcast(part(0, rs, ls), bf16).astype(f32)
                for k in range(1, NDEV):
                    g = g + pltpu.bitcast(part(k, rs, ls), bf16).astype(f32)
                return g.reshape(1, NG, GC, NP)
            full = jnp.concatenate([rows("conv_w", 3 * cb, 3) for cb in range(W // LANE)], axis=1)
            x, y, c = _coords()
            col0 = (4 * x + 2 * y + c) * (W // NDEV)
            sel = (lax.broadcasted_iota(jnp.int32, (W, W // NDEV), 0)
                   == lax.broadcasted_iota(jnp.int32, (W, W // NDEV), 1) + col0).astype(f32)
            return jnp.dot(full, sel, precision=HIGHEST, preferred_element_type=f32)[None]

        loss_ref[...] = 0.5 * rows("scalars", 1, 1, 0, 1)
        for i, name in enumerate(names):
            w_ref, m_ref, v_ref = ins[3 * i:3 * i + 3]
            g = grad_of(name)
            d, m, v = _adam(w_ref[...], g, m_ref[...], v_ref[...])
            outs[4 * i][...] = g
            outs[4 * i + 1][...] = d
            outs[4 * i + 2][...] = m
            outs[4 * i + 3][...] = v

    out_shape = [S(params[n][0].shape, f32) for n in names for _ in range(4)] + [S((1, 1), f32)]
    res = pl.pallas_call(body, name="adam_small", out_shape=out_shape, scratch_shapes=[pltpu.VMEM((PACK_ROWS, LANE), f32)],
                         compiler_params=_cp(None, VMEM_LIMIT))(packed_all, *flat)
    return {n: res[4 * i:4 * i + 4] for i, n in enumerate(names)}, res[-1]


def _block_diag(wgt):
    eye = jnp.eye(8, dtype=wgt.dtype)
    out = wgt[:, :, :, None, :] * eye[None, :, None, :, None]
    return out.reshape(4, 8 * wgt.shape[2], 8 * wgt.shape[3])


def _diag_blocks(m, a, b):
    m = m.reshape(4, 8, a, 8, b)
    idx = jnp.arange(8)
    return m[:, idx, :, idx, :].transpose(1, 0, 2, 3)


def kernel(x, w_in, b_in, ssm_lambda_re, ssm_lambda_im, ssm_log_dt, ssm_b_re, ssm_b_im, ssm_c_re, ssm_c_im, ssm_d, glu_w, glu_b, w_ssm_out, conv_w, w_conv_out, w_o, ln1_g, ln1_b, w_gate, w_up, w_down, ln2_g, ln2_b, loss_target, m_w_in, m_b_in, m_ssm_lambda_re, m_ssm_lambda_im, m_ssm_log_dt, m_ssm_b_re, m_ssm_b_im, m_ssm_c_re, m_ssm_c_im, m_ssm_d, m_glu_w, m_glu_b, m_w_ssm_out, m_conv_w, m_w_conv_out, m_w_o, m_ln1_g, m_ln1_b, m_w_gate, m_w_up, m_w_down, m_ln2_g, m_ln2_b, v_w_in, v_b_in, v_ssm_lambda_re, v_ssm_lambda_im, v_ssm_log_dt, v_ssm_b_re, v_ssm_b_im, v_ssm_c_re, v_ssm_c_im, v_ssm_d, v_glu_w, v_glu_b, v_w_ssm_out, v_conv_w, v_w_conv_out, v_w_o, v_ln1_g, v_ln1_b, v_w_gate, v_w_up, v_w_down, v_ln2_g, v_ln2_b):
    given = dict(locals())
    xs = x[0]
    target = loss_target[0]

    tr = lambda a: jnp.swapaxes(a[0], 0, 1)
    win_s, glu_s, wso_s, wco_s, wo_s, wgT_s, wuT_s, wd_s = prep_weights(
        [w_in[0], glu_w[0], w_ssm_out[0], w_conv_out[0], w_o[0], tr(w_gate), tr(w_up), w_down[0]])
    (win_g,) = run_plan(GatherPlan([win_s], srcs=(0,)), "gather_w_in_u")

    lam_re, lam_im = ssm_lambda_re[0], ssm_lambda_im[0]
    ldt = ssm_log_dt[0].reshape(NG, 1)
    br2 = jnp.swapaxes(ssm_b_re[0], 1, 2).reshape(NG * GC, NP)
    bi2 = jnp.swapaxes(ssm_b_im[0], 1, 2).reshape(NG * GC, NP)
    lbr, lbi, fr, fi, bbr, bbi = ssm_params(lam_re, lam_im, ldt, br2, bi2)
    bb_t = lambda b: b.reshape(4, 8, GC, NP)
    wb = jnp.concatenate([_block_diag(bb_t(bbr)), _block_diag(bb_t(bbi))], axis=2)
    c_t = lambda c: c.reshape(4, 8, GC, NP).transpose(0, 1, 3, 2)
    wc = jnp.concatenate([_block_diag(c_t(ssm_c_re[0])), -_block_diag(c_t(ssm_c_im[0]))], axis=1)
    wbT, wcT = wb.transpose(0, 2, 1), wc.transpose(0, 2, 1)
    wb, wc, wbT, wcT = wb.astype(bf16), wc.astype(bf16), wbT.astype(bf16), wcT.astype(bf16)
    lbr_s, lbi_s = lbr.reshape(4, 1, SW), lbi.reshape(4, 1, SW)
    dsk = ssm_d[0].reshape(4, 1, LANE)

    u_nat, xb = in_proj_u(xs, win_g, b_in)
    u_p = to_perm(u_nat, 0, "perm_u")
    half_a, half_b = (0, 3, 5, 6), (1, 2, 4, 7)
    (y_p,), (win_g, conv_g, glu_g, wso_g) = ssm_fwd(
        u_p, wb, wc, lbr_s, lbi_s, dsk,
        Plans([GatherPlan([win_s], srcs=tuple(range(1, NDEV)), into=[win_g]), GatherPlan([conv_w[0], glu_s, wso_s])]))
    conv_f = conv_g.transpose(1, 0, 2).reshape(3, W)
    (proj,), (wco_g, wo_g, wgT_g) = in_proj_rest(
        xb, win_g, b_in, Plans([GatherPlan([wco_s, wo_s]), GatherPlan([wgT_s], srcs=half_a)]))
    glu_f, wo_f = glu_g.reshape(W, W), wo_g.reshape(D, D)
    (yn,), _ = from_perm(y_p, "unperm_y")
    ya = glu_fwd(yn, glu_f, glu_b)
    yb = conv_fwd(proj, conv_f)
    (merged,), (wgT_g, wuT_g) = merge_fwd(
        ya, yb, wso_g, wco_g, proj,
        Plans([GatherPlan([wgT_s], srcs=half_b, into=[wgT_g]), GatherPlan([wuT_s], srcs=half_a)]))
    (r1, x1b), (wuT_g,) = mix_ln1(merged, wo_f, xs, ln1_g, ln1_b, GatherPlan([wuT_s], srcs=half_b, into=[wuT_g]))
    wgT, wuT = wgT_g.reshape(F, D), wuT_g.reshape(F, D)
    (gate, up, hid), (wd_g,) = gate_up(x1b, wgT, wuT, GatherPlan([wd_s]))
    wd_f = wd_g.reshape(F, D)
    dr2, dffn, sqerr, dln2_g, dln2_b = down_loss(hid, wd_f, r1, ln1_g, ln1_b, ln2_g, ln2_b, target)

    dwd, _ = mm_tn_rows(hid, dffn, "grad_w_down")
    dwd = dwd.reshape(NDEV, FS, D)
    (dgate, dup), (r_wd,) = ffn_bwd_act(dffn, wd_f, gate, up, ScatterPlan([dwd]))
    dwgT, _ = mm_tn_rows(dgate, x1b, "grad_w_gate")
    dwgT = dwgT.reshape(NDEV, FS, D)
    dwuT, (r_wgT,) = mm_tn_rows(dup, x1b, "grad_w_up", plan=ScatterPlan([dwgT], only=half_a))
    dwuT = dwuT.reshape(NDEV, FS, D)
    (dr1, dmix, dln1_g, dln1_b), (r_wgT,) = ffn_bwd_x(dgate, dup, wgT, wuT, dr2, r1, ln1_g,
                                                     ScatterPlan([dwgT], only=half_b, into=[r_wgT]))
    (dYA, dYB, dga, dgb, sga, sgb), (r_wuT,) = merge_bwd(dmix, wo_f, ya, yb, wso_g, wco_g, proj,
                                                         ScatterPlan([dwuT], only=half_a))
    dwo, _ = mm_tn_rows(merged, dmix, "grad_w_o")
    dwo = dwo.reshape(NDEV, D // NDEV, D)
    (dya, dyb), (r_wuT,) = branches_bwd_x(dYA, dYB, wso_g, wco_g, ScatterPlan([dwuT], only=half_b, into=[r_wuT]))
    dwso = branch_bwd_w(ya, dYA, "grad_w_ssm_out")
    dwco = branch_bwd_w(yb, dYB, "grad_w_conv_out")
    dyn, dsp, gb, dglu_b = glu_bwd(yn, dya, glu_f, glu_b)
    dglu = mm_tn_rows(gb, dsp, "grad_glu_w")[0].reshape(NDEV, W // NDEV, W)
    dh, dcg, dbg, dconv, shcb = conv_bwd(proj, dyb, conv_f)
    dwin = mm_tn(xb, dgb, "grad_w_in_gb", block0=6, nblocks=NDEV)
    dwin = mm_tn(xb, dga, "grad_w_in_ga", block0=4, into=dwin)
    dwin = mm_tn(xb, dbg, "grad_w_in_bg", block0=3, into=dwin)
    dwin = mm_tn(xb, dcg, "grad_w_in_cg", block0=2, into=dwin)
    dwin = mm_tn(xb, dh, "grad_w_in_h", block0=1, into=dwin)
    dy_p = to_perm(dyn, 0, "perm_dy")
    (du_p, dwb, dwcT, dlbr_s, dlbi_s, dd, su), (r_wo, r_wso, r_wco, r_glu, r_win) = ssm_bwd(
        u_p, dy_p, wb, wbT, wcT, lbr_s, lbi_s, dsk,
        Plans([ScatterPlan([dwo, dwso, dwco, dglu]), ScatterPlan([dwin], only=tuple(range(1, NDEV)))]))

    dbb = lambda m: _diag_blocks(m, GC, NP).reshape(NG * GC, NP)
    dbr2, dbi2, dlam_re, dlam_im, dldt = ssm_param_bwd(
        lam_re, lam_im, ldt, fr, fi, br2, bi2, dbb(dwb[:, :, :SW]), dbb(dwb[:, :, SW:]),
        dlbr_s.reshape(NG, NP), dlbi_s.reshape(NG, NP))
    packed = pack_grads(su, shcb, sga, sgb, dd, dglu_b, dln1_g, dln1_b, dln2_g, dln2_b, dlam_re, dlam_im, dldt, sqerr,
                        dbr2, dbi2, dbb(dwcT[:, :, :SW]), -dbb(dwcT[:, :, SW:]), dconv)
    (du,), _ = from_perm(du_p, "unperm_du", bf16)
    dwin = mm_tn(xb, du, "grad_w_in_u", block0=0, into=dwin)

    rest = [(dh, 0, 1), (dcg, 0, 2), (dbg, 0, 3), (dga, 0, 4), (dga, 1, 5), (dgb, 0, 6), (dgb, 1, 7)]
    (gx_rest,), (r_win, small_all) = in_proj_bwd_x(
        rest, win_g, dr1, ALPHA, "in_proj_bwd_x_rest",
        Plans([ScatterPlan([dwin], only=(0,), into=[r_win]), GatherPlan([packed])]))
    (grad_x,), _ = in_proj_bwd_x([(du, 0, 0)], win_g, gx_rest, 1.0, "in_proj_bwd_x_u")

    out = {}

    def put(name, res, back=lambda a: a[None]):
        out["grad_" + name], out["delta_" + name], out["new_m_" + name], out["new_v_" + name] = [back(r) for r in res]

    put("w_in", adam_update(w_in[0], m_w_in[0], v_w_in[0], r_win, "adam_w_in", 256))
    put("glu_w", adam_update(glu_w[0], m_glu_w[0], v_glu_w[0], r_glu, "adam_glu_w"))
    put("w_ssm_out", adam_update(w_ssm_out[0], m_w_ssm_out[0], v_w_ssm_out[0], r_wso, "adam_w_ssm_out"))
    put("w_conv_out", adam_update(w_conv_out[0], m_w_conv_out[0], v_w_conv_out[0], r_wco, "adam_w_conv_out"))
    put("w_o", adam_update(w_o[0], m_w_o[0], v_w_o[0], r_wo, "adam_w_o"))
    put("w_down", adam_update(w_down[0], m_w_down[0], v_w_down[0], r_wd, "adam_w_down", 176))
    untr = lambda a: jnp.swapaxes(a, 0, 1)[None]
    put("w_gate", adam_update(tr(w_gate), tr(m_w_gate), tr(v_w_gate), r_wgT, "adam_w_gate", 176), untr)
    put("w_up", adam_update(tr(w_up), tr(m_w_up), tr(v_w_up), r_wuT, "adam_w_up", 176), untr)
    as_c = lambda a: jnp.swapaxes(a, 2, 3)
    params = {n: (given[n], given["m_" + n], given["v_" + n]) for n in list(_SMALL) + ["conv_w"]}
    for n in ("ssm_b_re", "ssm_b_im"):
        params[n] = tuple(as_c(a) for a in params[n])
    small, loss = adam_small(small_all, params)
    for n, res in small.items():
        put(n, res, as_c if n in ("ssm_b_re", "ssm_b_im") else (lambda a: a))

    names = ["w_in", "b_in", "ssm_lambda_re", "ssm_lambda_im", "ssm_log_dt", "ssm_b_re", "ssm_b_im", "ssm_c_re", "ssm_c_im",
             "ssm_d", "glu_w", "glu_b", "w_ssm_out", "conv_w", "w_conv_out", "w_o", "ln1_g", "ln1_b", "w_gate", "w_up",
             "w_down", "ln2_g", "ln2_b"]
    return (loss.reshape(()), grad_x[None], *[out[p + n] for p in ("grad_", "delta_", "new_m_", "new_v_") for n in names])
```

```python
import functools
import math

import jax
import jax.numpy as jnp
from jax import lax
from jax.experimental import pallas as pl
from jax.experimental.pallas import tpu as pltpu

f32, bf16 = jnp.float32, jnp.bfloat16
S = jax.ShapeDtypeStruct
MESH = pl.DeviceIdType.MESH
HIGHEST = lax.Precision.HIGHEST

D = 1024
W = 512
NG, NP, GC = 32, 64, 16
F = 2816
NDEV = 8
FS = F // NDEV
IN_COLS = 8 * W
ALPHA = 2.0 ** 0.25
LN_EPS = 1e-5
ADAM_LR, ADAM_B1, ADAM_B2, ADAM_EPS, ADAM_WD, ADAM_STEP = 0.001, 0.9, 0.999, 1e-08, 0.01, 10
NC = 32
LANE = 128
SW = 4 * LANE
VMEM_LIMIT = 56 * 1024 * 1024
GRAD_DT = bf16
ANY = pl.BlockSpec(memory_space=pl.ANY)


def _cp(sem=None, vmem=None):
    return pltpu.CompilerParams(dimension_semantics=sem, vmem_limit_bytes=vmem)


def _resident(shape):
    return pl.BlockSpec(shape, lambda i: (0,) * len(shape), pipeline_mode=pl.Buffered(1))


def _dot(a, b):
    return jnp.dot(a, b, preferred_element_type=f32)


def _dot_nt(a, b):
    return lax.dot_general(a, b, (((1,), (1,)), ((), ())), preferred_element_type=f32)


def _dot_tn(a, b):
    return lax.dot_general(a, b, (((0,), (0,)), ((), ())), preferred_element_type=f32)


def _eye(n):
    return (lax.broadcasted_iota(jnp.int32, (n, n), 0) == lax.broadcasted_iota(jnp.int32, (n, n), 1)).astype(f32)


def _transpose_exact(a):
    return lax.dot_general(a, _eye(a.shape[0]), (((0,), (0,)), ((), ())), precision=HIGHEST, preferred_element_type=f32)


def _sigmoid(x):
    return 1.0 / (1.0 + jnp.exp(-x))


_GK = math.sqrt(2.0 / math.pi)


def _gelu(x):
    return 0.5 * x * (1.0 + jnp.tanh(_GK * (x + 0.044715 * x * x * x)))


def _gelu_grad(x):
    th = jnp.tanh(_GK * (x + 0.044715 * x * x * x))
    return 0.5 * (1.0 + th) + 0.5 * x * (1.0 - th * th) * _GK * (1.0 + 3.0 * 0.044715 * x * x)


ROW_PART = 256


def _row_parts(tm):
    return [slice(r, r + min(ROW_PART, tm)) for r in range(0, tm, min(ROW_PART, tm))]


def _ln_stats(r):
    mu = jnp.mean(r, axis=-1, keepdims=True)
    xc = r - mu
    var = jnp.mean(xc * xc, axis=-1, keepdims=True)
    rstd = lax.rsqrt(var + LN_EPS)
    return xc * rstd, rstd


def _ln_bwd(dy, xhat, rstd, g):
    dxh = dy * g
    m1 = jnp.mean(dxh, axis=-1, keepdims=True)
    m2 = jnp.mean(dxh * xhat, axis=-1, keepdims=True)
    return rstd * (dxh - m1 - xhat * m2)


def _coords():
    return lax.axis_index("x"), lax.axis_index("y"), lax.axis_index("c")


def _when(cond, fn):
    if cond is True:
        fn()
    else:
        pl.when(cond)(fn)


class GatherPlan:
    aliases = ()

    def __init__(self, arrs, srcs=None, into=None):
        n = self.n = len(arrs)
        self.srcs = srcs
        self.inputs = list(arrs) + list(into or [])
        if into:
            self.aliases = tuple((n + a, a) for a in range(n))
        self.out_shape = [S((NDEV,) + a.shape, a.dtype) for a in arrs]
        self.sems = [pltpu.SemaphoreType.DMA((n, 7)), pltpu.SemaphoreType.DMA((n, 7)), pltpu.SemaphoreType.DMA((n,))]

    def _has(self, dev):
        if self.srcs is None:
            return True
        idx = 4 * dev[0] + 2 * dev[1] + dev[2]
        return functools.reduce(jnp.logical_or, [idx == s for s in self.srcs])

    def _parts(self, ins, outs, sems):
        n = self.n
        send_sems, recv_sems, loc_sems = sems
        x, y, c = _coords()
        me, sib = (x, y, c), (x, y, 1 - c)
        chips = [(1 - x, y), (x, 1 - y), (1 - x, 1 - y)]

        def slot(a, dev):
            return outs[a].at[4 * dev[0] + 2 * dev[1] + dev[2]]

        def copy(a, k, block, to, src=None):
            return pltpu.make_async_remote_copy(
                src_ref=slot(a, block) if src is None else src, dst_ref=slot(a, block),
                send_sem=send_sems.at[a, k], recv_sem=recv_sems.at[a, k], device_id=to, device_id_type=MESH)

        each = [(j, chip, a) for j, chip in enumerate(chips) for a in range(n)]
        own = self._has(me)
        return dict(
            mine=lambda: [(pltpu.make_async_copy(ins[a], slot(a, me), loc_sems.at[a]), own) for a in range(n)],
            first=lambda: ([(copy(a, 0, me, sib, src=ins[a]), own) for a in range(n)]
                           + [(copy(a, 1 + j, me, (*chip, c), src=ins[a]), own) for j, chip, a in each]),
            landed=lambda: [(copy(a, 1 + j, (*chip, c), me), self._has((*chip, c))) for j, chip, a in each],
            passed=lambda: [(copy(a, 4 + j, (*chip, c), sib), self._has((*chip, c))) for j, chip, a in each],
            from_sib=lambda: ([(copy(a, 0, sib, me), self._has(sib)) for a in range(n)]
                              + [(copy(a, 4 + j, (*chip, 1 - c), me), self._has((*chip, 1 - c))) for j, chip, a in each]))

    def start(self, ins, outs, sems):
        p = self._parts(ins, outs, sems)
        for cp, cond in p["mine"]() + p["first"]():
            _when(cond, cp.start)

    def forward(self, ins, outs, sems):
        p = self._parts(ins, outs, sems)
        for (got, cond), (fwd, _) in zip(p["landed"](), p["passed"]()):
            def relay(got=got, fwd=fwd):
                got.wait_recv()
                fwd.start()

            _when(cond, relay)

    def finish(self, ins, outs, sems):
        p = self._parts(ins, outs, sems)
        for cp, cond in p["from_sib"]():
            _when(cond, cp.wait_recv)
        for cp, cond in p["first"]() + p["passed"]():
            _when(cond, cp.wait_send)
        for cp, cond in p["mine"]():
            _when(cond, cp.wait)


class ScatterPlan:
    aliases = ()

    def __init__(self, gs, only=None, into=None):
        n = self.n = len(gs)
        self.only = only
        self.inputs = list(gs) + list(into or [])
        if into:
            self.aliases = tuple((n + a, a) for a in range(n))
        self.out_shape = [S(g.shape, g.dtype) for g in gs]
        self.sems = [pltpu.SemaphoreType.DMA((n, 7)), pltpu.SemaphoreType.DMA((n, 7)), pltpu.SemaphoreType.DMA((n,))]

    def _owner(self, idx):
        if self.only is None:
            return True
        return functools.reduce(jnp.logical_or, [idx == b for b in self.only])

    def _copies(self, ins, outs, sems):
        n = self.n
        send_sems, recv_sems, loc_sems = sems
        x, y, c = _coords()
        me = 4 * x + 2 * y + c
        mine = self._owner(me)
        copies = [(pltpu.make_async_copy(ins[a].at[me], outs[a].at[me], loc_sems.at[a]), mine, None) for a in range(n)]
        for m in range(1, NDEV):
            px = 1 - x if m & 4 else x
            py = 1 - y if m & 2 else y
            pc = 1 - c if m & 1 else c
            peer = 4 * px + 2 * py + pc
            for a in range(n):
                copies.append((pltpu.make_async_remote_copy(
                    src_ref=ins[a].at[peer], dst_ref=outs[a].at[me],
                    send_sem=send_sems.at[a, m - 1], recv_sem=recv_sems.at[a, m - 1],
                    device_id=(px, py, pc), device_id_type=MESH), self._owner(peer), mine))
        return copies

    def start(self, ins, outs, sems):
        for cp, sends, _ in self._copies(ins, outs, sems):
            _when(sends, cp.start)

    def forward(self, ins, outs, sems):
        pass

    def finish(self, ins, outs, sems):
        for cp, sends, receives in self._copies(ins, outs, sems):
            if receives is None:
                _when(sends, cp.wait)
            else:
                _when(sends, cp.wait_send)
                _when(receives, cp.wait_recv)


class Plans:
    def __init__(self, plans):
        self.plans = plans
        self.inputs = [a for p in plans for a in p.inputs]
        self.out_shape = [s for p in plans for s in p.out_shape]
        self.sems = [s for p in plans for s in p.sems]
        self.aliases, i, o = [], 0, 0
        for p in plans:
            self.aliases += [(i + a, o + b) for a, b in p.aliases]
            i, o = i + len(p.inputs), o + len(p.out_shape)

    def _each(self, what, ins, outs, sems):
        i = o = s = 0
        for p in self.plans:
            ni, no, ns = len(p.inputs), len(p.out_shape), len(p.sems)
            getattr(p, what)(ins[i:i + ni], outs[o:o + no], sems[s:s + ns])
            i, o, s = i + ni, o + no, s + ns

    def start(self, ins, outs, sems):
        self._each("start", ins, outs, sems)

    def forward(self, ins, outs, sems):
        self._each("forward", ins, outs, sems)

    def finish(self, ins, outs, sems):
        self._each("finish", ins, outs, sems)


def _call(body, args, *, name, grid, in_specs, out_specs, out_shape, scratch=(), sem=None, vmem=None, plan=None,
          aliases=None, relay_step=None):
    aliases = aliases or {}
    if plan is None:
        outs = pl.pallas_call(body, name=name, grid=grid, in_specs=list(in_specs), out_specs=list(out_specs),
                              out_shape=list(out_shape), scratch_shapes=list(scratch), input_output_aliases=aliases,
                              compiler_params=_cp(sem, vmem))(*args)
        return list(outs), []
    ni, no, ns = len(in_specs), len(out_specs), len(scratch)
    pi, po = len(plan.inputs), len(plan.out_shape)
    aliases = {**aliases, **{ni + a: no + b for a, b in plan.aliases}}

    def wrapped(*refs):
        main_in, p_in = refs[:ni], refs[ni:ni + pi]
        main_out, p_out = refs[ni + pi:ni + pi + no], refs[ni + pi + no:ni + pi + no + po]
        main_scr, p_sems = refs[ni + pi + no + po:ni + pi + no + po + ns], refs[ni + pi + no + po + ns:]
        ids = [pl.program_id(d) for d in range(len(grid))]
        first = functools.reduce(jnp.logical_and, [i == 0 for i in ids])
        last = functools.reduce(jnp.logical_and, [i == g - 1 for i, g in zip(ids, grid)])

        @pl.when(first)
        def _():
            plan.start(p_in, p_out, p_sems)

        @pl.when(last if relay_step is None else ids[0] == max(relay_step, 0))
        def _():
            plan.forward(p_in, p_out, p_sems)

        body(*main_in, *main_out, *main_scr)

        @pl.when(last)
        def _():
            plan.finish(p_in, p_out, p_sems)

    outs = pl.pallas_call(
        wrapped, name=name, grid=grid, in_specs=list(in_specs) + [ANY] * pi, out_specs=list(out_specs) + [ANY] * po,
        out_shape=list(out_shape) + list(plan.out_shape), scratch_shapes=list(scratch) + list(plan.sems),
        input_output_aliases=aliases, compiler_params=_cp(("arbitrary",) * len(grid), vmem),
    )(*args, *plan.inputs)
    return list(outs[:no]), list(outs[no:])


def run_plan(plan, name):
    def body(*refs):
        ins, outs, sems = refs[:len(plan.inputs)], refs[len(plan.inputs):len(plan.inputs) + len(plan.out_shape)], \
            refs[len(plan.inputs) + len(plan.out_shape):]
        plan.start(ins, outs, sems)
        plan.forward(ins, outs, sems)
        plan.finish(ins, outs, sems)

    return pl.pallas_call(body, name=name, in_specs=[ANY] * len(plan.inputs), out_specs=[ANY] * len(plan.out_shape),
                          out_shape=list(plan.out_shape), scratch_shapes=list(plan.sems))(*plan.inputs)


def mm_tn(a, b, name, tn=512, into=None, block0=0, nblocks=None):
    T, K = a.shape
    N = b.shape[1]
    tn = min(tn, N)
    nblocks = nblocks or (N // tn if into is None else into.shape[0])

    def body(a_ref, b_ref, *rest):
        rest[-1][...] = _dot_tn(a_ref[...], b_ref[...]).astype(GRAD_DT)

    args, in_specs, aliases = [a, b], [_resident((T, K)), pl.BlockSpec((T, tn), lambda j: (0, j))], {}
    if into is not None:
        args.append(into)
        in_specs.append(ANY)
        aliases = {2: 0}
    (out,), _ = _call(body, args, name=name, grid=(N // tn,), in_specs=in_specs,
                      out_specs=[pl.BlockSpec((None, K, tn), lambda j: (block0 + j, 0, 0))],
                      out_shape=[S((nblocks, K, tn), GRAD_DT)], sem=("parallel",), vmem=VMEM_LIMIT, aliases=aliases)
    return out


def grad_w_in_rest(xb, dh, dcg, dbg, dga, dgb):
    T = xb.shape[0]
    order = ((0, 0), (1, 1), (2, 2), (3, 3), (4, 3), (5, 4), (6, 4))

    def body(x_ref, *refs):
        o_ref = refs[-1]
        j = pl.program_id(0)
        for step, opnd in order:
            @pl.when(j == step)
            def _(opnd=opnd):
                o_ref[...] = _dot_tn(x_ref[...], refs[opnd][...]).astype(GRAD_DT)

    once = lambda: pl.BlockSpec((T, W), lambda j: (0, 0), pipeline_mode=pl.Buffered(1))
    (out,), _ = _call(
        body, [xb, dh, dcg, dbg, dga, dgb], name="grad_w_in_rest", grid=(len(order),),
        in_specs=[_resident((T, D)), once(), once(), once(),
                  pl.BlockSpec((T, W), lambda j: (0, jnp.clip(j - 3, 0, 1))),
                  pl.BlockSpec((T, W), lambda j: (0, jnp.clip(j - 5, 0, 1)))],
        out_specs=[pl.BlockSpec((None, D, W), lambda j: (1 + j, 0, 0))],
        out_shape=[S((NDEV, D, W), GRAD_DT)], sem=("arbitrary",), vmem=VMEM_LIMIT)
    return out


def mm_tn_rows(a, b, name, tk=256, plan=None):
    T, K = a.shape
    N = b.shape[1]
    tk = min(tk, K)

    def body(a_ref, b_ref, o_ref):
        o_ref[...] = _dot_tn(a_ref[...], b_ref[...]).astype(GRAD_DT)

    (out,), sent = _call(body, [a, b], name=name, grid=(K // tk,),
                         in_specs=[pl.BlockSpec((T, tk), lambda i: (0, i)), _resident((T, N))],
                         out_specs=[pl.BlockSpec((tk, N), lambda i: (i, 0))], out_shape=[S((K, N), GRAD_DT)],
                         sem=("parallel",), vmem=VMEM_LIMIT, plan=plan)
    return out, sent


def prep_weights(ws):
    def body(*refs):
        for i in range(len(ws)):
            refs[len(ws) + i][...] = refs[i][...].astype(bf16)

    return pl.pallas_call(body, name="prep_weights", out_shape=[S(w.shape, bf16) for w in ws],
                          compiler_params=_cp(None, VMEM_LIMIT))(*ws)


REST_BLOCKS = (4, 5, 6, 7, 1, 2, 3)
REST_COLS = len(REST_BLOCKS) * W


def in_proj_u(x, win_g, b_in):
    T = x.shape[0]
    tm = min(1024, T)

    def body(x_ref, w_ref, b_ref, u_ref, xb_ref):
        xb = x_ref[...].astype(bf16)
        xb_ref[...] = xb
        u_ref[...] = _dot(xb, w_ref[...]) + b_ref[...]

    row = pl.BlockSpec((tm, D), lambda i: (i, 0))
    return pl.pallas_call(
        body, name="in_proj_u", grid=(T // tm,),
        in_specs=[row, pl.BlockSpec((None, D, W), lambda i: (0, 0, 0)), pl.BlockSpec((1, W), lambda i: (0, 0))],
        out_specs=[pl.BlockSpec((tm, W), lambda i: (i, 0)), row],
        out_shape=[S((T, W), f32), S((T, D), bf16)], compiler_params=_cp(("parallel",), VMEM_LIMIT),
    )(x, win_g, b_in)


def in_proj_rest(xb, win_g, b_in, plan):
    T = xb.shape[0]
    tm = min(512, T)

    def body(x_ref, w_ref, b_ref, o_ref):
        xb_ = x_ref[...]
        for i, k in enumerate(REST_BLOCKS):
            o_ref[:, i * W:(i + 1) * W] = _dot(xb_, w_ref[k]) + b_ref[:, k * W:(k + 1) * W]

    return _call(
        body, [xb, win_g, b_in], name="in_proj_rest", grid=(T // tm,),
        in_specs=[pl.BlockSpec((tm, D), lambda i: (i, 0)), _resident((NDEV, D, W)), _resident((1, IN_COLS))],
        out_specs=[pl.BlockSpec((tm, REST_COLS), lambda i: (i, 0))],
        out_shape=[S((T, REST_COLS), f32)], vmem=VMEM_LIMIT, plan=plan, relay_step=T // tm - 2)


def to_perm(a, cb0, name):
    T = a.shape[0]
    L = T // NC

    def body(a_ref, o_ref):
        def step(jb, carry):
            j0 = pl.multiple_of(jb * 8, 8)
            for q in range(NC // 8):
                x = jnp.stack([a_ref[pl.ds((8 * q + c) * L + j0, 8), :] for c in range(8)], axis=0)
                y = jnp.swapaxes(x, 0, 1)
                for j in range(8):
                    o_ref[pl.ds((j0 + j) * NC + 8 * q, 8), :] = y[j]
            return carry

        lax.fori_loop(0, L // 8, step, 0)

    return pl.pallas_call(
        body, name=name, grid=(W // LANE,),
        in_specs=[pl.BlockSpec((T, LANE), lambda k: (0, cb0 + k))], out_specs=pl.BlockSpec((T, LANE), lambda k: (0, k)),
        out_shape=S((T, W), f32), compiler_params=_cp(("parallel",), VMEM_LIMIT),
    )(a)


def from_perm(a, name, out_dtype=f32, plan=None):
    T = a.shape[0]
    L = T // NC

    def body(a_ref, o_ref):
        def step(jb, carry):
            j0 = pl.multiple_of(jb * 16, 16)
            for q in range(NC // 8):
                halves = []
                for h in range(2):
                    x = jnp.stack([a_ref[pl.ds((j0 + 8 * h + j) * NC + 8 * q, 8), :] for j in range(8)], axis=0)
                    halves.append(jnp.swapaxes(x, 0, 1))
                for c in range(8):
                    o_ref[pl.ds((8 * q + c) * L + j0, 16), :] = jnp.concatenate(
                        [halves[0][c], halves[1][c]], axis=0).astype(out_dtype)
            return carry

        lax.fori_loop(0, L // 16, step, 0)

    slab = pl.BlockSpec((T, LANE), lambda k: (0, k))
    return _call(body, [a], name=name, grid=(W // LANE,), in_specs=[slab], out_specs=[slab],
                 out_shape=[S((T, W), out_dtype)], sem=("parallel",), vmem=VMEM_LIMIT, plan=plan)


def _disc(lr, li, ldt):
    dt = jnp.exp(ldt)
    mag = jnp.exp(lr * dt)
    lbr = mag * jnp.cos(li * dt)
    lbi = mag * jnp.sin(li * dt)
    den = lr * lr + li * li
    nr = lbr - 1.0
    return lbr, lbi, (nr * lr + lbi * li) / den, (lbi * lr - nr * li) / den


def _per_channel(f):
    return jnp.broadcast_to(f[:, None, :], (NG, GC, NP)).reshape(NG * GC, NP)


def ssm_params(lam_re, lam_im, log_dt, br, bi):
    def body(lr_ref, li_ref, ldt_ref, br_ref, bi_ref, lbr_ref, lbi_ref, fr_ref, fi_ref, bbr_ref, bbi_ref):
        lbr, lbi, fr, fi = _disc(lr_ref[...], li_ref[...], ldt_ref[...])
        lbr_ref[...], lbi_ref[...], fr_ref[...], fi_ref[...] = lbr, lbi, fr, fi
        fr_, fi_, br_, bi_ = _per_channel(fr), _per_channel(fi), br_ref[...], bi_ref[...]
        bbr_ref[...] = fr_ * br_ - fi_ * bi_
        bbi_ref[...] = fr_ * bi_ + fi_ * br_

    return pl.pallas_call(body, name="ssm_params", out_shape=[S((NG, NP), f32)] * 4 + [S((NG * GC, NP), f32)] * 2)(
        lam_re, lam_im, log_dt, br, bi)


SCAN_UNROLL = 4
SCAN_LANES = 2 * LANE


def _steps(n, body, carry):
    main = n // SCAN_UNROLL

    def trip(t, c):
        for q in range(SCAN_UNROLL):
            c = body(t * SCAN_UNROLL + q, c)
        return c

    carry = lax.fori_loop(0, main, trip, carry)
    for i in range(main * SCAN_UNROLL, n):
        carry = body(i, carry)
    return carry


def _scan_body(T):
    L = T // NC
    RB = min(512, T)
    nsq = int(round(math.log2(L)))
    assert 2 ** nsq == L and T % RB == 0 and L % 16 == 0

    def rows(i):
        return pl.ds(pl.multiple_of(i * RB, RB), RB)

    def tile(j):
        return pl.ds(j * NC if isinstance(j, int) else pl.multiple_of(j * NC, NC), NC)

    def forward_states(u_ref, wb_ref, lbr_ref, lbi_ref, sre, sim, ere, eim):
        def bproj(i, carry):
            bu = _dot(u_ref[rows(i), :].astype(bf16), wb_ref[...])
            sre[rows(i), :] = bu[:, :SW]
            sim[rows(i), :] = bu[:, SW:]
            return carry

        lax.fori_loop(0, T // RB, bproj, 0)
        for lb in range(SW // SCAN_LANES):
            ls = slice(lb * SCAN_LANES, (lb + 1) * SCAN_LANES)
            ar = jnp.broadcast_to(lbr_ref[:, ls], (NC, SCAN_LANES))
            ai = jnp.broadcast_to(lbi_ref[:, ls], (NC, SCAN_LANES))

            def step(j, carry):
                xr, xi = carry
                nr = ar * xr - ai * xi + sre[tile(j), ls]
                ni = ar * xi + ai * xr + sim[tile(j), ls]
                sre[tile(j), ls] = nr
                sim[tile(j), ls] = ni
                return nr, ni

            zero = jnp.zeros((NC, SCAN_LANES), f32)
            _steps(L, step, (zero, zero))
            pr, pi = lbr_ref[:, ls], lbi_ref[:, ls]
            for _ in range(nsq):
                pr, pi = pr * pr - pi * pi, 2.0 * pr * pi
            er = jnp.zeros((1, SCAN_LANES), f32)
            ei = er
            ere[0:1, ls] = er
            eim[0:1, ls] = ei
            base = (L - 1) * NC
            for c in range(1, NC):
                lr_ = sre[base + c - 1:base + c, ls]
                li_ = sim[base + c - 1:base + c, ls]
                er, ei = lr_ + pr * er - pi * ei, li_ + pr * ei + pi * er
                ere[c:c + 1, ls] = er
                eim[c:c + 1, ls] = ei
            e_r, e_i = ere[:, ls].reshape(NC // 8, 8, SCAN_LANES), eim[:, ls].reshape(NC // 8, 8, SCAN_LANES)
            ar8, ai8 = ar[0:8], ai[0:8]

            def fix(j, carry):
                pwr, pwi = carry
                xr = sre[tile(j), ls].reshape(NC // 8, 8, SCAN_LANES) + (pwr * e_r - pwi * e_i)
                xi = sim[tile(j), ls].reshape(NC // 8, 8, SCAN_LANES) + (pwr * e_i + pwi * e_r)
                sre[tile(j), ls] = xr.reshape(NC, SCAN_LANES)
                sim[tile(j), ls] = xi.reshape(NC, SCAN_LANES)
                return pwr * ar8 - pwi * ai8, pwr * ai8 + pwi * ar8

            _steps(L, fix, (ar8, ai8))

    return L, RB, nsq, rows, tile, forward_states


def ssm_fwd(u_p, wb, wc, lbr, lbi, dsk, plan):
    T = u_p.shape[0]
    L, RB, nsq, rows, tile, forward_states = _scan_body(T)

    def body(u_ref, wb_ref, wc_ref, lbr_ref, lbi_ref, d_ref, y_ref, sre, sim, ere, eim):
        forward_states(u_ref, wb_ref, lbr_ref, lbi_ref, sre, sim, ere, eim)

        def cproj(i, carry):
            y = _dot(sre[rows(i), :].astype(bf16), wc_ref[0:SW, :]) + _dot(sim[rows(i), :].astype(bf16), wc_ref[SW:, :])
            y_ref[rows(i), :] = y + d_ref[...] * u_ref[rows(i), :]
            return carry

        lax.fori_loop(0, T // RB, cproj, 0)

    slab = pl.BlockSpec((T, LANE), lambda k: (0, k))
    return _call(
        body, [u_p, wb, wc, lbr, lbi, dsk], name="ssm_fwd", grid=(W // LANE,),
        in_specs=[slab, pl.BlockSpec((None, LANE, 2 * SW), lambda k: (k, 0, 0)),
                  pl.BlockSpec((None, 2 * SW, LANE), lambda k: (k, 0, 0)),
                  pl.BlockSpec((None, 1, SW), lambda k: (k, 0, 0)), pl.BlockSpec((None, 1, SW), lambda k: (k, 0, 0)),
                  pl.BlockSpec((None, 1, LANE), lambda k: (k, 0, 0))],
        out_specs=[slab], out_shape=[S((T, W), f32)],
        scratch=[pltpu.VMEM((T, SW), f32), pltpu.VMEM((T, SW), f32), pltpu.VMEM((NC, SW), f32), pltpu.VMEM((NC, SW), f32)],
        vmem=VMEM_LIMIT, plan=plan)


def ssm_bwd(u_p, dy_p, wb, wbT, wcT, lbr, lbi, dsk, plan):
    T = u_p.shape[0]
    L, RB, nsq, rows, tile, forward_states = _scan_body(T)

    def body(u_ref, dy_ref, wb_ref, wbT_ref, wcT_ref, lbr_ref, lbi_ref, d_ref,
             du_ref, dwb_ref, dwc_ref, dlr_ref, dli_ref, dd_ref, su_ref, sre, sim, gre, gim, ere, eim):
        forward_states(u_ref, wb_ref, lbr_ref, lbi_ref, sre, sim, ere, eim)

        def dstate(i, carry):
            g = _dot(dy_ref[rows(i), :].astype(bf16), wcT_ref[...])
            gre[rows(i), :] = g[:, :SW]
            gim[rows(i), :] = g[:, SW:]
            return carry

        lax.fori_loop(0, T // RB, dstate, 0)
        row = lax.broadcasted_iota(jnp.int32, (NC, SCAN_LANES), 0)
        for lb in range(SW // SCAN_LANES):
            ls = slice(lb * SCAN_LANES, (lb + 1) * SCAN_LANES)
            ar = jnp.broadcast_to(lbr_ref[:, ls], (NC, SCAN_LANES))
            ai = jnp.broadcast_to(lbi_ref[:, ls], (NC, SCAN_LANES))

            def step(i, carry):
                gr, gi = carry
                j = L - 1 - i
                nr = ar * gr + ai * gi + gre[tile(j), ls]
                ni = ar * gi - ai * gr + gim[tile(j), ls]
                gre[tile(j), ls] = nr
                gim[tile(j), ls] = ni
                return nr, ni

            zero = jnp.zeros((NC, SCAN_LANES), f32)
            _steps(L, step, (zero, zero))
            pr, pi = lbr_ref[:, ls], -lbi_ref[:, ls]
            for _ in range(nsq):
                pr, pi = pr * pr - pi * pi, 2.0 * pr * pi
            er = jnp.zeros((1, SCAN_LANES), f32)
            ei = er
            ere[NC - 1:NC, ls] = er
            eim[NC - 1:NC, ls] = ei
            for c in range(NC - 2, -1, -1):
                lr_ = gre[c + 1:c + 2, ls]
                li_ = gim[c + 1:c + 2, ls]
                er, ei = lr_ + pr * er - pi * ei, li_ + pr * ei + pi * er
                ere[c:c + 1, ls] = er
                eim[c:c + 1, ls] = ei
            e_r, e_i = ere[:, ls].reshape(NC // 8, 8, SCAN_LANES), eim[:, ls].reshape(NC // 8, 8, SCAN_LANES)
            ar8, ai8 = ar[0:8], ai[0:8]

            def fixed(j, pwr, pwi):
                gr = (gre[tile(j), ls].reshape(NC // 8, 8, SCAN_LANES) + (pwr * e_r - pwi * e_i)).reshape(NC, SCAN_LANES)
                gi = (gim[tile(j), ls].reshape(NC // 8, 8, SCAN_LANES) + (pwr * e_i + pwi * e_r)).reshape(NC, SCAN_LANES)
                gre[tile(j), ls] = gr
                gim[tile(j), ls] = gi
                return gr, gi

            def fix(i, carry):
                pwr, pwi, accr, acci = carry
                j = L - 1 - i
                gr, gi = fixed(j, pwr, pwi)
                xr, xi = sre[tile(j - 1), ls], sim[tile(j - 1), ls]
                return (pwr * ar8 + pwi * ai8, pwi * ar8 - pwr * ai8,
                        accr + gr * xr + gi * xi, acci + gi * xr - gr * xi)

            pwr, pwi, accr, acci = _steps(L - 1, fix, (ar8, -ai8, zero, zero))
            gr, gi = fixed(0, pwr, pwi)
            xr = jnp.where(row == 0, 0.0, pltpu.roll(sre[tile(L - 1), ls], 1, axis=0))
            xi = jnp.where(row == 0, 0.0, pltpu.roll(sim[tile(L - 1), ls], 1, axis=0))
            accr = accr + gr * xr + gi * xi
            acci = acci + gi * xr - gr * xi
            dlr_ref[:, ls] = jnp.sum(accr, axis=0, keepdims=True)
            dli_ref[:, ls] = jnp.sum(acci, axis=0, keepdims=True)

        dwb_ref[...] = jnp.zeros_like(dwb_ref)
        dwc_ref[...] = jnp.zeros_like(dwc_ref)
        dd_ref[...] = jnp.zeros_like(dd_ref)
        su_ref[...] = jnp.zeros_like(su_ref)

        def finish(i, carry):
            u32, dy32 = u_ref[rows(i), :], dy_ref[rows(i), :]
            ub, dyb = u32.astype(bf16), dy32.astype(bf16)
            gr, gi = gre[rows(i), :].astype(bf16), gim[rows(i), :].astype(bf16)
            du = _dot(gr, wbT_ref[0:SW, :]) + _dot(gi, wbT_ref[SW:, :]) + dy32 * d_ref[...]
            du_ref[rows(i), :] = du
            su_ref[...] += jnp.sum(du, axis=0, keepdims=True)
            dwb_ref[:, 0:SW] += _dot_tn(ub, gr)
            dwb_ref[:, SW:] += _dot_tn(ub, gi)
            dwc_ref[:, 0:SW] += _dot_tn(dyb, sre[rows(i), :].astype(bf16))
            dwc_ref[:, SW:] += _dot_tn(dyb, sim[rows(i), :].astype(bf16))
            dd_ref[...] += jnp.sum(dy32 * u32, axis=0, keepdims=True)
            return carry

        lax.fori_loop(0, T // RB, finish, 0)

    slab = pl.BlockSpec((T, LANE), lambda k: (0, k))
    wide = pl.BlockSpec((None, LANE, 2 * SW), lambda k: (k, 0, 0))
    tall = pl.BlockSpec((None, 2 * SW, LANE), lambda k: (k, 0, 0))
    vec = pl.BlockSpec((None, 1, SW), lambda k: (k, 0, 0))
    vecd = pl.BlockSpec((None, 1, LANE), lambda k: (k, 0, 0))
    nslab = W // LANE
    return _call(
        body, [u_p, dy_p, wb, wbT, wcT, lbr, lbi, dsk], name="ssm_bwd", grid=(nslab,),
        in_specs=[slab, slab, wide, tall, wide, vec, vec, vecd],
        out_specs=[slab, wide, wide, vec, vec, vecd, vecd],
        out_shape=[S((T, W), f32), S((nslab, LANE, 2 * SW), f32), S((nslab, LANE, 2 * SW), f32),
                   S((nslab, 1, SW), f32), S((nslab, 1, SW), f32), S((nslab, 1, LANE), f32), S((nslab, 1, LANE), f32)],
        scratch=[pltpu.VMEM((T, SW), f32)] * 4 + [pltpu.VMEM((NC, SW), f32)] * 2, vmem=VMEM_LIMIT, plan=plan)


def glu_fwd(yn, glu_w, glu_b):
    T = yn.shape[0]
    tm = min(512, T)

    def body(y_ref, w_ref, b_ref, o_ref):
        g = _gelu(y_ref[...])
        o_ref[...] = (g * _sigmoid(_dot(g.astype(bf16), w_ref[...]) + b_ref[...])).astype(bf16)

    return pl.pallas_call(
        body, name="glu_fwd", grid=(T // tm,),
        in_specs=[pl.BlockSpec((tm, W), lambda i: (i, 0)), pl.BlockSpec((W, W), lambda i: (0, 0)), pl.BlockSpec((1, W), lambda i: (0, 0))],
        out_specs=pl.BlockSpec((tm, W), lambda i: (i, 0)), out_shape=S((T, W), bf16), compiler_params=_cp(("parallel",)),
    )(yn, glu_w, glu_b)


def _shift_rows(cur, prev8, k):
    return pltpu.roll(jnp.concatenate([prev8, cur], axis=0), k, axis=0)[8:]


def _lift_rows(cur, next8, k):
    n = cur.shape[0]
    return pltpu.roll(jnp.concatenate([cur, next8], axis=0), n + 8 - k, axis=0)[:n]


def conv_fwd(proj, conv_w):
    T = proj.shape[0]
    RB = min(512, T)

    def body(h_ref, c_ref, b_ref, w_ref, o_ref):
        w0, w1, w2 = w_ref[0:1, :], w_ref[1:2, :], w_ref[2:3, :]

        def blk(i, carry):
            r0 = pl.multiple_of(i * RB, RB)
            rs = pl.ds(r0, RB)
            ch = c_ref[rs, :] * h_ref[rs, :]
            pr = pl.ds(jnp.maximum(r0 - 8, 0), 8)
            prev = jnp.where(i > 0, c_ref[pr, :] * h_ref[pr, :], 0.0)
            z = w2 * ch + w1 * _shift_rows(ch, prev, 1) + w0 * _shift_rows(ch, prev, 2)
            o_ref[rs, :] = (b_ref[rs, :] * z).astype(bf16)
            return carry

        lax.fori_loop(0, T // RB, blk, 0)

    nb = W // LANE
    return pl.pallas_call(
        body, name="conv_fwd", grid=(nb,),
        in_specs=[pl.BlockSpec((T, LANE), lambda k: (0, 4 * nb + k)), pl.BlockSpec((T, LANE), lambda k: (0, 5 * nb + k)),
                  pl.BlockSpec((T, LANE), lambda k: (0, 6 * nb + k)),pl.BlockSpec((3, LANE), lambda k: (0, k))],
        out_specs=pl.BlockSpec((T, LANE), lambda k: (0, k)), out_shape=S((T, W), bf16),
        compiler_params=_cp(("parallel",), VMEM_LIMIT),
    )(proj, proj, proj, conv_w)


def _dense_columns(blocks_ref, dense_ref):
    for k in range(NDEV):
        dense_ref[:, k * LANE:(k + 1) * LANE] = blocks_ref[k]


def merge_fwd(ya, yb, wso, wco, proj, plan):
    T = ya.shape[0]
    tm = min(1024, T)

    def body(ya_ref, yb_ref, wa_ref, wb_ref, ga_ref, gb_ref, o_ref, wa_s, wb_s):
        @pl.when(pl.program_id(0) == 0)
        def _():
            _dense_columns(wa_ref, wa_s)
            _dense_columns(wb_ref, wb_s)

        o_ref[...] = (_sigmoid(ga_ref[...]) * _dot(ya_ref[...], wa_s[...])
                      + _sigmoid(gb_ref[...]) * _dot(yb_ref[...], wb_s[...])).astype(bf16)

    act = pl.BlockSpec((tm, W), lambda i: (i, 0))
    return _call(
        body, [ya, yb, wso, wco, proj, proj], name="merge_fwd", grid=(T // tm,),
        in_specs=[act, act, _resident((NDEV, W, LANE)), _resident((NDEV, W, LANE)),
                  pl.BlockSpec((tm, D), lambda i: (i, 0)), pl.BlockSpec((tm, D), lambda i: (i, 1))],
        out_specs=[pl.BlockSpec((tm, D), lambda i: (i, 0))], out_shape=[S((T, D), bf16)],
        scratch=[pltpu.VMEM((W, D), bf16), pltpu.VMEM((W, D), bf16)], vmem=VMEM_LIMIT, plan=plan)


def mix_ln1(merged, w_o, x, g1, b1, plan):
    T = x.shape[0]
    tm = min(512, T)

    def body(m_ref, w_ref, x_ref, g_ref, b_ref, r_ref, x1_ref):
        for rs in _row_parts(tm):
            r = ALPHA * x_ref[rs, :] + _dot(m_ref[rs, :], w_ref[...])
            r_ref[rs, :] = r
            xhat, _ = _ln_stats(r)
            x1_ref[rs, :] = (xhat * g_ref[...] + b_ref[...]).astype(bf16)

    row = pl.BlockSpec((tm, D), lambda i: (i, 0))
    vec = pl.BlockSpec((1, D), lambda i: (0, 0))
    return _call(
        body, [merged, w_o, x, g1, b1], name="mix_ln1", grid=(T // tm,),
        in_specs=[row, _resident((D, D)), row, vec, vec],
        out_specs=[row, row], out_shape=[S((T, D), f32), S((T, D), bf16)], sem=("parallel",), vmem=VMEM_LIMIT, plan=plan,
        relay_step=T // tm - 2)


FT = 256


def gate_up(x1b, wgT, wuT, plan):
    T = x1b.shape[0]
    tm = min(512, T)

    def body(x_ref, wg_ref, wu_ref, g_ref, u_ref, h_ref):
        x = x_ref[...]
        for n in range(F // FT):
            cs = slice(n * FT, (n + 1) * FT)
            g = _dot_nt(x, wg_ref[cs, :])
            u = _dot_nt(x, wu_ref[cs, :])
            g_ref[:, cs] = g.astype(bf16)
            u_ref[:, cs] = u.astype(bf16)
            h_ref[:, cs] = (g * _sigmoid(g) * u).astype(bf16)

    osp = pl.BlockSpec((tm, F), lambda i: (i, 0))
    return _call(
        body, [x1b, wgT, wuT], name="gate_up", grid=(T // tm,),
        in_specs=[pl.BlockSpec((tm, D), lambda i: (i, 0)), _resident((F, D)), _resident((F, D))],
        out_specs=[osp, osp, osp], out_shape=[S((T, F), bf16)] * 3, vmem=VMEM_LIMIT, plan=plan, relay_step=T // tm - 3)


def down_loss(hid, w_down, r1, g1, b1, g2, b2, target):
    T = hid.shape[0]
    tm = min(512, T)

    def body(h_ref, w_ref, r1_ref, g1_ref, b1_ref, g2_ref, b2_ref, t_ref, dr_ref, drb_ref, loss_ref, dg_ref, db_ref):
        @pl.when(pl.program_id(0) == 0)
        def _():
            loss_ref[...] = jnp.zeros_like(loss_ref)
            dg_ref[...] = jnp.zeros_like(dg_ref)
            db_ref[...] = jnp.zeros_like(db_ref)

        for rs in _row_parts(tm):
            xh1, _ = _ln_stats(r1_ref[rs, :])
            x1 = xh1 * g1_ref[...] + b1_ref[...]
            r2 = ALPHA * x1 + _dot(h_ref[rs, :], w_ref[...])
            xh2, rstd2 = _ln_stats(r2)
            err = xh2 * g2_ref[...] + b2_ref[...] - t_ref[rs, :]
            loss_ref[...] += jnp.sum(jnp.mean(err * err, axis=-1, keepdims=True), axis=0, keepdims=True)
            dy = err * (1.0 / D)
            dg_ref[...] += jnp.sum(dy * xh2, axis=0, keepdims=True)
            db_ref[...] += jnp.sum(dy, axis=0, keepdims=True)
            dr = _ln_bwd(dy, xh2, rstd2, g2_ref[...])
            dr_ref[rs, :] = dr
            drb_ref[rs, :] = dr.astype(bf16)

    row = pl.BlockSpec((tm, D), lambda i: (i, 0))
    vec = pl.BlockSpec((1, D), lambda i: (0, 0))
    return pl.pallas_call(
        body, name="down_loss", grid=(T // tm,),
        in_specs=[pl.BlockSpec((tm, F), lambda i: (i, 0)), _resident((F, D)), row, vec, vec, vec, vec, row],
        out_specs=[row, row, pl.BlockSpec((1, 1), lambda i: (0, 0)), vec, vec],
        out_shape=[S((T, D), f32), S((T, D), bf16), S((1, 1), f32), S((1, D), f32), S((1, D), f32)],
        compiler_params=_cp(("arbitrary",), VMEM_LIMIT),
    )(hid, w_down, r1, g1, b1, g2, b2, target)


def ffn_bwd_act(dffn, w_down, gate, up, plan):
    T = dffn.shape[0]
    tm = min(512, T)

    def body(d_ref, w_ref, g_ref, u_ref, dg_ref, du_ref):
        for n in range(F // FT):
            cs = slice(n * FT, (n + 1) * FT)
            for rs in _row_parts(tm):
                dh = _dot_nt(d_ref[rs, :], w_ref[cs, :])
                g, u = g_ref[rs, cs].astype(f32), u_ref[rs, cs].astype(f32)
                sg = _sigmoid(g)
                t = g * sg
                du_ref[rs, cs] = (dh * t).astype(bf16)
                dg_ref[rs, cs] = (dh * u * (sg + t - t * sg)).astype(bf16)

    osp = pl.BlockSpec((tm, F), lambda i: (i, 0))
    return _call(
        body, [dffn, w_down, gate, up], name="ffn_bwd_act", grid=(T // tm,),
        in_specs=[pl.BlockSpec((tm, D), lambda i: (i, 0)), _resident((F, D)), osp, osp],
        out_specs=[osp, osp], out_shape=[S((T, F), bf16)] * 2, sem=("parallel",), vmem=VMEM_LIMIT, plan=plan)


def ffn_bwd_x(dgate, dup, wgT, wuT, dr2, r1, g1, plan):
    T = dr2.shape[0]
    tm = min(512, T)

    def body(dg_ref, du_ref, wg_ref, wu_ref, dr2_ref, r1_ref, g1_ref, dr_ref, drb_ref, dgam_ref, dbet_ref):
        @pl.when(pl.program_id(0) == 0)
        def _():
            dgam_ref[...] = jnp.zeros_like(dgam_ref)
            dbet_ref[...] = jnp.zeros_like(dbet_ref)

        for rs in _row_parts(tm):
            dx1 = ALPHA * dr2_ref[rs, :] + _dot(dg_ref[rs, :], wg_ref[...]) + _dot(du_ref[rs, :], wu_ref[...])
            xh, rstd = _ln_stats(r1_ref[rs, :])
            dgam_ref[...] += jnp.sum(dx1 * xh, axis=0, keepdims=True)
            dbet_ref[...] += jnp.sum(dx1, axis=0, keepdims=True)
            dr = _ln_bwd(dx1, xh, rstd, g1_ref[...])
            dr_ref[rs, :] = dr
            drb_ref[rs, :] = dr.astype(bf16)

    row = pl.BlockSpec((tm, D), lambda i: (i, 0))
    wide = pl.BlockSpec((tm, F), lambda i: (i, 0))
    wsp = _resident((F, D))
    vec = pl.BlockSpec((1, D), lambda i: (0, 0))
    return _call(
        body, [dgate, dup, wgT, wuT, dr2, r1, g1], name="ffn_bwd_x", grid=(T // tm,),
        in_specs=[wide, wide, wsp, wsp, row, row, vec],
        out_specs=[row, row, vec, vec], out_shape=[S((T, D), f32), S((T, D), bf16), S((1, D), f32), S((1, D), f32)],
        vmem=VMEM_LIMIT, plan=plan)


def merge_bwd(dmix, w_o, ya, yb, wso, wco, proj, plan):
    T = dmix.shape[0]
    tm = min(512, T)

    def body(dm_ref, wo_ref, ya_ref, yb_ref, wa_ref, wb_ref, ga_ref, gb_ref, dya_ref, dyb_ref, dga_ref, dgb_ref, sa_ref, sb_ref,
             wa_s, wb_s):
        @pl.when(pl.program_id(0) == 0)
        def _():
            _dense_columns(wa_ref, wa_s)
            _dense_columns(wb_ref, wb_s)

        dmer = _dot_nt(dm_ref[...], wo_ref[...])
        sa, sb = _sigmoid(ga_ref[...]), _sigmoid(gb_ref[...])
        dya_ref[...] = (dmer * sa).astype(bf16)
        dyb_ref[...] = (dmer * sb).astype(bf16)
        dga = dmer * _dot(ya_ref[...], wa_s[...]) * sa * (1.0 - sa)
        dgb = dmer * _dot(yb_ref[...], wb_s[...]) * sb * (1.0 - sb)
        dga_ref[...] = dga.astype(bf16)
        dgb_ref[...] = dgb.astype(bf16)
        sa_ref[...] = jnp.sum(dga, axis=0, keepdims=True)
        sb_ref[...] = jnp.sum(dgb, axis=0, keepdims=True)

    act = pl.BlockSpec((tm, W), lambda i: (i, 0))
    osp = pl.BlockSpec((tm, D), lambda i: (i, 0))
    ssp = pl.BlockSpec((None, 1, D), lambda i: (i, 0, 0))
    return _call(
        body, [dmix, w_o, ya, yb, wso, wco, proj, proj], name="merge_bwd", grid=(T // tm,),
        in_specs=[osp, _resident((D, D)), act, act, _resident((NDEV, W, LANE)), _resident((NDEV, W, LANE)),
                  pl.BlockSpec((tm, D), lambda i: (i, 0)), pl.BlockSpec((tm, D), lambda i: (i, 1))],
        out_specs=[osp, osp, osp, osp, ssp, ssp],
        out_shape=[S((T, D), bf16)] * 4 + [S((T // tm, 1, D), f32)] * 2,
        scratch=[pltpu.VMEM((W, D), bf16), pltpu.VMEM((W, D), bf16)], vmem=VMEM_LIMIT, plan=plan)


def branches_bwd_x(dYA, dYB, wso, wco, plan):
    T = dYA.shape[0]
    tm = min(1024, T)

    def body(da_ref, db_ref, wa_ref, wb_ref, oa_ref, ob_ref, wa_s, wb_s):
        @pl.when(pl.program_id(0) == 0)
        def _():
            _dense_columns(wa_ref, wa_s)
            _dense_columns(wb_ref, wb_s)

        oa_ref[...] = _dot_nt(da_ref[...], wa_s[...])
        ob_ref[...] = _dot_nt(db_ref[...], wb_s[...])

    row = pl.BlockSpec((tm, D), lambda i: (i, 0))
    osp = pl.BlockSpec((tm, W), lambda i: (i, 0))
    return _call(
        body, [dYA, dYB, wso, wco], name="branches_bwd_x", grid=(T // tm,),
        in_specs=[row, row, _resident((NDEV, W, LANE)), _resident((NDEV, W, LANE))],
        out_specs=[osp, osp], out_shape=[S((T, W), f32)] * 2,
        scratch=[pltpu.VMEM((W, D), bf16), pltpu.VMEM((W, D), bf16)], vmem=VMEM_LIMIT, plan=plan)


def branch_bwd_w(act, dY, name):
    T = act.shape[0]
    tk = W // 2

    def body(a_ref, d_ref, o_ref):
        res = _dot_tn(a_ref[...], d_ref[...])
        for k in range(NDEV):
            o_ref[k] = res[:, k * LANE:(k + 1) * LANE].astype(o_ref.dtype)

    return pl.pallas_call(
        body, name=name, grid=(W // tk,),
        in_specs=[pl.BlockSpec((T, tk), lambda i: (0, i)), _resident((T, D))],
        out_specs=pl.BlockSpec((NDEV, tk, LANE), lambda i: (0, i, 0)), out_shape=S((NDEV, W, LANE), GRAD_DT),
        compiler_params=_cp(("parallel",), VMEM_LIMIT),
    )(act, dY)


def glu_bwd(yn, dya, glu_w, glu_b):
    T = yn.shape[0]
    tm = min(512, T)

    def body(y_ref, d_ref, w_ref, b_ref, dy_ref, dsp_ref, g_ref, db_ref):
        @pl.when(pl.program_id(0) == 0)
        def _():
            db_ref[...] = jnp.zeros_like(db_ref)

        y, dya_ = y_ref[...], d_ref[...]
        g = _gelu(y)
        gb = g.astype(bf16)
        s = _sigmoid(_dot(gb, w_ref[...]) + b_ref[...])
        dsp = dya_ * g * s * (1.0 - s)
        dspb = dsp.astype(bf16)
        dg = dya_ * s + _dot_nt(dspb, w_ref[...])
        dy_ref[...] = dg * _gelu_grad(y)
        dsp_ref[...] = dspb
        g_ref[...] = gb
        db_ref[...] += jnp.sum(dsp, axis=0, keepdims=True)

    row = pl.BlockSpec((tm, W), lambda i: (i, 0))
    vec = pl.BlockSpec((1, W), lambda i: (0, 0))
    return pl.pallas_call(
        body, name="glu_bwd", grid=(T // tm,),
        in_specs=[row, row, pl.BlockSpec((W, W), lambda i: (0, 0)), vec],
        out_specs=[row, row, row, vec], out_shape=[S((T, W), f32), S((T, W), bf16), S((T, W), bf16), S((1, W), f32)],
        compiler_params=_cp(("arbitrary",)),
    )(yn, dya, glu_w, glu_b)


def conv_bwd(proj, dyb, conv_w):
    T = proj.shape[0]
    RB = min(512, T)
    nrb = T // RB

    def body(h_ref, c_ref, b_ref, d_ref, w_ref, dh_ref, dc_ref, db_ref, dw_ref, s_ref):
        w0, w1, w2 = w_ref[0:1, :], w_ref[1:2, :], w_ref[2:3, :]

        def blk(i, carry):
            a0, a1, a2, sh, sc, sb = carry
            r0 = pl.multiple_of(i * RB, RB)
            rs = pl.ds(r0, RB)
            h, cg, bg, dyb_ = h_ref[rs, :], c_ref[rs, :], b_ref[rs, :], d_ref[rs, :]
            ch = cg * h
            pr = pl.ds(jnp.maximum(r0 - 8, 0), 8)
            prev = jnp.where(i > 0, c_ref[pr, :] * h_ref[pr, :], 0.0)
            ch1, ch2 = _shift_rows(ch, prev, 1), _shift_rows(ch, prev, 2)
            dbg = dyb_ * (w2 * ch + w1 * ch1 + w0 * ch2)
            db_ref[rs, :] = dbg.astype(bf16)
            dz = dyb_ * bg
            nx = pl.ds(jnp.minimum(r0 + RB, T - 8), 8)
            nxt = jnp.where(i < nrb - 1, d_ref[nx, :] * b_ref[nx, :], 0.0)
            dch = w2 * dz + w1 * _lift_rows(dz, nxt, 1) + w0 * _lift_rows(dz, nxt, 2)
            dcg, dh = dch * h, dch * cg
            dc_ref[rs, :] = dcg.astype(bf16)
            dh_ref[rs, :] = dh.astype(bf16)
            col = lambda v: jnp.sum(v, axis=0, keepdims=True)
            return (a0 + col(dz * ch2), a1 + col(dz * ch1), a2 + col(dz * ch), sh + col(dh), sc + col(dcg), sb + col(dbg))

        zero = jnp.zeros((1, LANE), f32)
        a0, a1, a2, sh, sc, sb = lax.fori_loop(0, nrb, blk, (zero,) * 6)
        dw_ref[0:1, :] = a0
        dw_ref[1:2, :] = a1
        dw_ref[2:3, :] = a2
        s_ref[0:1, :] = sh
        s_ref[1:2, :] = sc
        s_ref[2:3, :] = sb

    nb = W // LANE
    slab = pl.BlockSpec((T, LANE), lambda k: (0, k))
    three = pl.BlockSpec((3, LANE), lambda k: (0, k))
    return pl.pallas_call(
        body, name="conv_bwd", grid=(nb,),
        in_specs=[pl.BlockSpec((T, LANE), lambda k: (0, 4 * nb + k)), pl.BlockSpec((T, LANE), lambda k: (0, 5 * nb + k)),
                  pl.BlockSpec((T, LANE), lambda k: (0, 6 * nb + k)),slab, three],
        out_specs=[slab, slab, slab, three, three],
        out_shape=[S((T, W), bf16)] * 3 + [S((3, W), f32)] * 2, compiler_params=_cp(("parallel",), VMEM_LIMIT),
    )(proj, proj, proj, dyb, conv_w)


def in_proj_bwd_x(parts, win_g, base, scale, name, plan=None):
    T = base.shape[0]
    tm = min(512, T)
    n = len(parts)

    def body(*refs):
        p_refs, w_ref, b_ref, o_ref = refs[:n], refs[n], refs[n + 1], refs[n + 2]
        acc = scale * b_ref[...]
        for p_ref, (_, _, k) in zip(p_refs, parts):
            acc += _dot_nt(p_ref[...], w_ref[k])
        o_ref[...] = acc

    row = pl.BlockSpec((tm, D), lambda i: (i, 0))
    p_specs = [pl.BlockSpec((tm, W), (lambda i, cb=cb: (i, cb))) for _, cb, _ in parts]
    return _call(
        body, [a for a, _, _ in parts] + [win_g, base], name=name, grid=(T // tm,),
        in_specs=p_specs + [_resident((NDEV, D, W)), row],
        out_specs=[row], out_shape=[S((T, D), f32)], vmem=VMEM_LIMIT, plan=plan)


def ssm_param_bwd(lam_re, lam_im, log_dt, fr, fi, br, bi, dwb, dwcT, dlbr, dlbi):
    def body(lr_ref, li_ref, ldt_ref, fr_ref, fi_ref, br_ref, bi_ref, dwb_ref, dwc_ref, dlbr_ref, dlbi_ref,
             dbr_ref, dbi_ref, dlr_ref, dli_ref, dldt_ref, dcr_ref, dci_ref, dr_s, di_s):
        for k in range(W // LANE):
            for gl in range(NG // (W // LANE)):
                rows, src = slice((8 * k + gl) * GC, (8 * k + gl + 1) * GC), slice(gl * GC, (gl + 1) * GC)
                re, im = slice(gl * NP, (gl + 1) * NP), slice(SW + gl * NP, SW + (gl + 1) * NP)
                dr_s[rows, :] = dwb_ref[k, src, re]
                di_s[rows, :] = dwb_ref[k, src, im]
                dcr_ref[rows, :] = dwc_ref[k, src, re]
                dci_ref[rows, :] = -dwc_ref[k, src, im]
        fr_, fi_ = _per_channel(fr_ref[...]), _per_channel(fi_ref[...])
        br_, bi_, dr, di = br_ref[...], bi_ref[...], dr_s[...], di_s[...]
        dbr_ref[...] = fr_ * dr + fi_ * di
        dbi_ref[...] = fr_ * di - fi_ * dr
        dfr = jnp.sum((dr * br_ + di * bi_).reshape(NG, GC, NP), axis=1)
        dfi = jnp.sum((di * br_ - dr * bi_).reshape(NG, GC, NP), axis=1)
        _, vjp = jax.vjp(_disc, lr_ref[...], li_ref[...], ldt_ref[...])
        dlr_ref[...], dli_ref[...], dldt = vjp((dlbr_ref[...], dlbi_ref[...], dfr, dfi))
        dldt_ref[...] = _transpose_exact(dldt)

    blk = S((NG * GC, NP), f32)
    return pl.pallas_call(
        body, name="ssm_param_bwd", out_shape=[blk, blk, S((NG, NP), f32), S((NG, NP), f32), S((1, NG), f32), blk, blk],
        scratch_shapes=[pltpu.VMEM((NG * GC, NP), f32)] * 2)(
        lam_re, lam_im, log_dt, fr, fi, br, bi, dwb, dwcT, dlbr, dlbi)


def _adam(w, g, m, v):
    m = ADAM_B1 * m + (1.0 - ADAM_B1) * g
    v = ADAM_B2 * v + (1.0 - ADAM_B2) * (g * g)
    m_hat = m / (1.0 - ADAM_B1 ** ADAM_STEP)
    v_hat = v / (1.0 - ADAM_B2 ** ADAM_STEP)
    return -ADAM_LR * (m_hat / (jnp.sqrt(v_hat) + ADAM_EPS) + ADAM_WD * w), m, v


def adam_update(w, m, v, contrib, name, rows_per_block=None):
    R, C = w.shape
    n = contrib.shape[0]
    tr = min(rows_per_block or R, R)

    def body(w_ref, m_ref, v_ref, c_ref, g_ref, d_ref, nm_ref, nv_ref):
        g = c_ref[0].astype(f32)
        for k in range(1, n):
            g = g + c_ref[k].astype(f32)
        g_ref[...] = g
        d_ref[...], nm_ref[...], nv_ref[...] = _adam(w_ref[...], g, m_ref[...], v_ref[...])

    blk = pl.BlockSpec((tr, C), lambda i: (i, 0))
    return pl.pallas_call(
        body, name=name, grid=(R // tr,), in_specs=[blk, blk, blk, pl.BlockSpec((n, tr, C), lambda i: (0, i, 0))],
        out_specs=[blk] * 4, out_shape=[S((R, C), f32)] * 4, compiler_params=_cp(("parallel",), VMEM_LIMIT),
    )(w, m, v, contrib)


_ROWVEC = (("b_in", IN_COLS), ("ssm_d", W), ("glu_b", W), ("ln1_g", D), ("ln1_b", D), ("ln2_g", D), ("ln2_b", D))
_HALF = NG * GC // 2
_BC_LANE = {"ssm_b_re": 0, "ssm_b_im": NP, "ssm_c_re": 0, "ssm_c_im": NP}
_PACK = {}
_r = 0
for _n, _k in _ROWVEC:
    _PACK[_n] = _r
    _r += _k // LANE
for _n, _rows in (("ssm_lambda", NG), ("scalars", 8), ("ssm_b", _HALF), ("ssm_c", _HALF), ("conv_w", 16)):
    _PACK[_n] = _r
    _r += _rows
for _n in _BC_LANE:
    _PACK[_n] = _PACK[_n[:5]]
PACK_ROWS = _r
assert PACK_ROWS % 8 == 0
_SMALL = ("b_in", "ssm_lambda_re", "ssm_lambda_im", "ssm_log_dt", "ssm_b_re", "ssm_b_im", "ssm_c_re", "ssm_c_im",
          "ssm_d", "glu_b", "ln1_g", "ln1_b", "ln2_g", "ln2_b")


def pack_grads(su, shcb, sga, sgb, dd, dglu_b, dln1_g, dln1_b, dln2_g, dln2_b, dlam_re, dlam_im, dldt, sqerr, dbr, dbi,
               dc_re, dc_im, dconv):
    nI = sga.shape[0]

    def body(su_ref, sh_ref, sga_ref, sgb_ref, dd_ref, gb_ref, l1g_ref, l1b_ref, l2g_ref, l2b_ref, lr_ref, li_ref, dt_ref,
             sq_ref, br_ref, bi_ref, cr_ref, ci_ref, cw_ref, o_ref):
        o_ref[...] = jnp.zeros_like(o_ref)

        def put_row(name, v):
            r0 = _PACK[name]
            for i in range(v.shape[1] // LANE):
                o_ref[r0 + i:r0 + i + 1, :] = v[:, i * LANE:(i + 1) * LANE]

        ga, gb = sga_ref[0], sgb_ref[0]
        for i in range(1, nI):
            ga, gb = ga + sga_ref[i], gb + sgb_ref[i]
        put_row("b_in", jnp.concatenate([su_ref[k] for k in range(W // LANE)]
                                        + [sh_ref[0:1, :], sh_ref[1:2, :], sh_ref[2:3, :], ga, gb], axis=1))
        put_row("ssm_d", jnp.concatenate([dd_ref[k] for k in range(W // LANE)], axis=1))
        put_row("glu_b", gb_ref[...])
        put_row("ln1_g", l1g_ref[...])
        put_row("ln1_b", l1b_ref[...])
        put_row("ln2_g", l2g_ref[...])
        put_row("ln2_b", l2b_ref[...])
        r0 = _PACK["ssm_lambda"]
        o_ref[r0:r0 + NG, 0:NP] = lr_ref[...]
        o_ref[r0:r0 + NG, NP:2 * NP] = li_ref[...]
        r0 = _PACK["scalars"]
        o_ref[r0:r0 + 1, 0:NG] = dt_ref[...]
        o_ref[r0 + 1:r0 + 2, 0:1] = sq_ref[...]
        for name, ref in (("ssm_b_re", br_ref), ("ssm_b_im", bi_ref), ("ssm_c_re", cr_ref), ("ssm_c_im", ci_ref)):
            r0, l0 = _PACK[name], _BC_LANE[name]
            o_ref[r0:r0 + _HALF, l0:l0 + NP] = pltpu.bitcast(ref[...].astype(bf16), f32)
        for cb in range(W // LANE):
            o_ref[_PACK["conv_w"] + 3 * cb:_PACK["conv_w"] + 3 * cb + 3, :] = cw_ref[:, cb * LANE:(cb + 1) * LANE]

    return pl.pallas_call(body, name="pack_grads", out_shape=S((PACK_ROWS, LANE), f32))(
        su, shcb, sga, sgb, dd, dglu_b, dln1_g, dln1_b, dln2_g, dln2_b, dlam_re, dlam_im, dldt, sqerr, dbr, dbi, dc_re, dc_im,
        dconv)


def adam_small(packed_all, params):
    names = list(_SMALL) + ["conv_w"]
    flat = [a for n in names for a in params[n]]

    def body(*refs):
        p_ref = refs[0]
        ins = refs[1:1 + 3 * len(names)]
        outs = refs[1 + 3 * len(names):-2]
        loss_ref, g_ref = refs[-2], refs[-1]

        def part(k, rs=slice(None), ls=slice(None)):
            return p_ref[k, rs, ls]

        g_all = part(0)
        for k in range(1, NDEV):
            g_all = g_all + part(k)
        g_ref[...] = g_all

        def rows(name, r0, n, l0=0, lanes=LANE):
            return g_ref[_PACK[name] + r0:_PACK[name] + r0 + n, l0:l0 + lanes]

        def grad_of(name):
            if name in dict(_ROWVEC):
                return jnp.concatenate([rows(name, i, 1) for i in range(dict(_ROWVEC)[name] // LANE)], axis=1)
            if name in ("ssm_lambda_re", "ssm_lambda_im"):
                return rows("ssm_lambda", 0, NG, NP * (name == "ssm_lambda_im"), NP)[None]
            if name == "ssm_log_dt":
                return rows("scalars", 0, 1, 0, NG)
            if name in _BC_LANE:
                rs, ls = slice(_PACK[name], _PACK[name] + _HALF), slice(_BC_LANE[name], _BC_LANE[name] + NP)
                g = pltpu.bitcast(part(0, rs, ls), bf16).astype(f32)
                for k in range(1, NDEV):
                    g = g + pltpu.bitcast(part(k, rs, ls), bf16).astype(f32)
                return g.reshape(1, NG, GC, NP)
            full = jnp.concatenate([rows("conv_w", 3 * cb, 3) for cb in range(W // LANE)], axis=1)
            x, y, c = _coords()
            col0 = (4 * x + 2 * y + c) * (W // NDEV)
            sel = (lax.broadcasted_iota(jnp.int32, (W, W // NDEV), 0)
                   == lax.broadcasted_iota(jnp.int32, (W, W // NDEV), 1) + col0).astype(f32)
            return jnp.dot(full, sel, precision=HIGHEST, preferred_element_type=f32)[None]

        loss_ref[...] = 0.5 * rows("scalars", 1, 1, 0, 1)
        for i, name in enumerate(names):
            w_ref, m_ref, v_ref = ins[3 * i:3 * i + 3]
            g = grad_of(name)
            d, m, v = _adam(w_ref[...], g, m_ref[...], v_ref[...])
            outs[4 * i][...] = g
            outs[4 * i + 1][...] = d
            outs[4 * i + 2][...] = m
            outs[4 * i + 3][...] = v

    out_shape = [S(params[n][0].shape, f32) for n in names for _ in range(4)] + [S((1, 1), f32)]
    res = pl.pallas_call(body, name="adam_small", out_shape=out_shape, scratch_shapes=[pltpu.VMEM((PACK_ROWS, LANE), f32)],
                         compiler_params=_cp(None, VMEM_LIMIT))(packed_all, *flat)
    return {n: res[4 * i:4 * i + 4] for i, n in enumerate(names)}, res[-1]


def _block_diag(wgt):
    eye = jnp.eye(8, dtype=wgt.dtype)
    out = wgt[:, :, :, None, :] * eye[None, :, None, :, None]
    return out.reshape(4, 8 * wgt.shape[2], 8 * wgt.shape[3])


def kernel(x, w_in, b_in, ssm_lambda_re, ssm_lambda_im, ssm_log_dt, ssm_b_re, ssm_b_im, ssm_c_re, ssm_c_im, ssm_d, glu_w, glu_b, w_ssm_out, conv_w, w_conv_out, w_o, ln1_g, ln1_b, w_gate, w_up, w_down, ln2_g, ln2_b, loss_target, m_w_in, m_b_in, m_ssm_lambda_re, m_ssm_lambda_im, m_ssm_log_dt, m_ssm_b_re, m_ssm_b_im, m_ssm_c_re, m_ssm_c_im, m_ssm_d, m_glu_w, m_glu_b, m_w_ssm_out, m_conv_w, m_w_conv_out, m_w_o, m_ln1_g, m_ln1_b, m_w_gate, m_w_up, m_w_down, m_ln2_g, m_ln2_b, v_w_in, v_b_in, v_ssm_lambda_re, v_ssm_lambda_im, v_ssm_log_dt, v_ssm_b_re, v_ssm_b_im, v_ssm_c_re, v_ssm_c_im, v_ssm_d, v_glu_w, v_glu_b, v_w_ssm_out, v_conv_w, v_w_conv_out, v_w_o, v_ln1_g, v_ln1_b, v_w_gate, v_w_up, v_w_down, v_ln2_g, v_ln2_b):
    given = dict(locals())
    xs = x[0]
    target = loss_target[0]

    tr = lambda a: jnp.swapaxes(a[0], 0, 1)
    win_s, glu_s, wso_s, wco_s, wo_s, wgT_s, wuT_s, wd_s = prep_weights(
        [w_in[0], glu_w[0], w_ssm_out[0], w_conv_out[0], w_o[0], tr(w_gate), tr(w_up), w_down[0]])
    (win_g,) = run_plan(GatherPlan([win_s], srcs=(0,)), "gather_w_in_u")

    lam_re, lam_im = ssm_lambda_re[0], ssm_lambda_im[0]
    ldt = ssm_log_dt[0].reshape(NG, 1)
    br2 = jnp.swapaxes(ssm_b_re[0], 1, 2).reshape(NG * GC, NP)
    bi2 = jnp.swapaxes(ssm_b_im[0], 1, 2).reshape(NG * GC, NP)
    lbr, lbi, fr, fi, bbr, bbi = ssm_params(lam_re, lam_im, ldt, br2, bi2)
    bb_t = lambda b: b.reshape(4, 8, GC, NP)
    wb = jnp.concatenate([_block_diag(bb_t(bbr)), _block_diag(bb_t(bbi))], axis=2)
    c_t = lambda c: c.reshape(4, 8, GC, NP).transpose(0, 1, 3, 2)
    wc = jnp.concatenate([_block_diag(c_t(ssm_c_re[0])), -_block_diag(c_t(ssm_c_im[0]))], axis=1)
    wbT, wcT = wb.transpose(0, 2, 1), wc.transpose(0, 2, 1)
    wb, wc, wbT, wcT = wb.astype(bf16), wc.astype(bf16), wbT.astype(bf16), wcT.astype(bf16)
    lbr_s, lbi_s = lbr.reshape(4, 1, SW), lbi.reshape(4, 1, SW)
    dsk = ssm_d[0].reshape(4, 1, LANE)

    u_nat, xb = in_proj_u(xs, win_g, b_in)
    u_p = to_perm(u_nat, 0, "perm_u")
    half_a, half_b = (0, 3, 5, 6), (1, 2, 4, 7)
    (y_p,), (win_g, conv_g, glu_g, wso_g) = ssm_fwd(
        u_p, wb, wc, lbr_s, lbi_s, dsk,
        Plans([GatherPlan([win_s], srcs=tuple(range(1, NDEV)), into=[win_g]), GatherPlan([conv_w[0], glu_s, wso_s])]))
    conv_f = conv_g.transpose(1, 0, 2).reshape(3, W)
    (proj,), (wco_g, wo_g, wgT_g) = in_proj_rest(
        xb, win_g, b_in, Plans([GatherPlan([wco_s, wo_s]), GatherPlan([wgT_s], srcs=half_a)]))
    glu_f, wo_f = glu_g.reshape(W, W), wo_g.reshape(D, D)
    (yn,), _ = from_perm(y_p, "unperm_y")
    ya = glu_fwd(yn, glu_f, glu_b)
    yb = conv_fwd(proj, conv_f)
    (merged,), (wgT_g, wuT_g) = merge_fwd(
        ya, yb, wso_g, wco_g, proj,
        Plans([GatherPlan([wgT_s], srcs=half_b, into=[wgT_g]), GatherPlan([wuT_s], srcs=half_a)]))
    (r1, x1b), (wuT_g,) = mix_ln1(merged, wo_f, xs, ln1_g, ln1_b, GatherPlan([wuT_s], srcs=half_b, into=[wuT_g]))
    wgT, wuT = wgT_g.reshape(F, D), wuT_g.reshape(F, D)
    (gate, up, hid), (wd_g,) = gate_up(x1b, wgT, wuT, GatherPlan([wd_s]))
    wd_f = wd_g.reshape(F, D)
    dr2, dffn, sqerr, dln2_g, dln2_b = down_loss(hid, wd_f, r1, ln1_g, ln1_b, ln2_g, ln2_b, target)

    dwd, _ = mm_tn_rows(hid, dffn, "grad_w_down")
    dwd = dwd.reshape(NDEV, FS, D)
    (dgate, dup), (r_wd,) = ffn_bwd_act(dffn, wd_f, gate, up, ScatterPlan([dwd]))
    dwgT, _ = mm_tn_rows(dgate, x1b, "grad_w_gate")
    dwgT = dwgT.reshape(NDEV, FS, D)
    dwuT, (r_wgT,) = mm_tn_rows(dup, x1b, "grad_w_up", plan=ScatterPlan([dwgT], only=half_a))
    dwuT = dwuT.reshape(NDEV, FS, D)
    (dr1, dmix, dln1_g, dln1_b), (r_wgT,) = ffn_bwd_x(dgate, dup, wgT, wuT, dr2, r1, ln1_g,
                                                     ScatterPlan([dwgT], only=half_b, into=[r_wgT]))
    (dYA, dYB, dga, dgb, sga, sgb), (r_wuT,) = merge_bwd(dmix, wo_f, ya, yb, wso_g, wco_g, proj,
                                                         ScatterPlan([dwuT], only=half_a))
    dwo, _ = mm_tn_rows(merged, dmix, "grad_w_o")
    dwo = dwo.reshape(NDEV, D // NDEV, D)
    (dya, dyb), (r_wuT,) = branches_bwd_x(dYA, dYB, wso_g, wco_g, ScatterPlan([dwuT], only=half_b, into=[r_wuT]))
    dwso = branch_bwd_w(ya, dYA, "grad_w_ssm_out")
    dwco = branch_bwd_w(yb, dYB, "grad_w_conv_out")
    dyn, dsp, gb, dglu_b = glu_bwd(yn, dya, glu_f, glu_b)
    dglu = mm_tn_rows(gb, dsp, "grad_glu_w")[0].reshape(NDEV, W // NDEV, W)
    dh, dcg, dbg, dconv, shcb = conv_bwd(proj, dyb, conv_f)
    dwin = grad_w_in_rest(xb, dh, dcg, dbg, dga, dgb)
    dy_p = to_perm(dyn, 0, "perm_dy")
    (du_p, dwb, dwcT, dlbr_s, dlbi_s, dd, su), (r_wo, r_wso, r_wco, r_glu, r_win) = ssm_bwd(
        u_p, dy_p, wb, wbT, wcT, lbr_s, lbi_s, dsk,
        Plans([ScatterPlan([dwo, dwso, dwco, dglu]), ScatterPlan([dwin], only=tuple(range(1, NDEV)))]))

    dbr2, dbi2, dlam_re, dlam_im, dldt, dc_re, dc_im = ssm_param_bwd(
        lam_re, lam_im, ldt, fr, fi, br2, bi2, dwb, dwcT, dlbr_s.reshape(NG, NP), dlbi_s.reshape(NG, NP))
    packed = pack_grads(su, shcb, sga, sgb, dd, dglu_b, dln1_g, dln1_b, dln2_g, dln2_b, dlam_re, dlam_im, dldt, sqerr,
                        dbr2, dbi2, dc_re, dc_im, dconv)
    (du,), _ = from_perm(du_p, "unperm_du", bf16)
    dwin = mm_tn(xb, du, "grad_w_in_u", block0=0, into=dwin)

    rest = [(dh, 0, 1), (dcg, 0, 2), (dbg, 0, 3), (dga, 0, 4), (dga, 1, 5), (dgb, 0, 6), (dgb, 1, 7)]
    (gx_rest,), (r_win, small_all) = in_proj_bwd_x(
        rest, win_g, dr1, ALPHA, "in_proj_bwd_x_rest",
        Plans([ScatterPlan([dwin], only=(0,), into=[r_win]), GatherPlan([packed])]))
    (grad_x,), _ = in_proj_bwd_x([(du, 0, 0)], win_g, gx_rest, 1.0, "in_proj_bwd_x_u")

    out = {}

    def put(name, res, back=lambda a: a[None]):
        out["grad_" + name], out["delta_" + name], out["new_m_" + name], out["new_v_" + name] = [back(r) for r in res]

    put("w_in", adam_update(w_in[0], m_w_in[0], v_w_in[0], r_win, "adam_w_in", 256))
    put("glu_w", adam_update(glu_w[0], m_glu_w[0], v_glu_w[0], r_glu, "adam_glu_w"))
    put("w_ssm_out", adam_update(w_ssm_out[0], m_w_ssm_out[0], v_w_ssm_out[0], r_wso, "adam_w_ssm_out"))
    put("w_conv_out", adam_update(w_conv_out[0], m_w_conv_out[0], v_w_conv_out[0], r_wco, "adam_w_conv_out"))
    put("w_o", adam_update(w_o[0], m_w_o[0], v_w_o[0], r_wo, "adam_w_o"))
    put("w_down", adam_update(w_down[0], m_w_down[0], v_w_down[0], r_wd, "adam_w_down", 176))
    untr = lambda a: jnp.swapaxes(a, 0, 1)[None]
    put("w_gate", adam_update(tr(w_gate), tr(m_w_gate), tr(v_w_gate), r_wgT, "adam_w_gate", 176), untr)
    put("w_up", adam_update(tr(w_up), tr(m_w_up), tr(v_w_up), r_wuT, "adam_w_up", 176), untr)
    as_c = lambda a: jnp.swapaxes(a, 2, 3)
    params = {n: (given[n], given["m_" + n], given["v_" + n]) for n in list(_SMALL) + ["conv_w"]}
    for n in ("ssm_b_re", "ssm_b_im"):
        params[n] = tuple(as_c(a) for a in params[n])
    small, loss = adam_small(small_all, params)
    for n, res in small.items():
        put(n, res, as_c if n in ("ssm_b_re", "ssm_b_im") else (lambda a: a))

    names = ["w_in", "b_in", "ssm_lambda_re", "ssm_lambda_im", "ssm_log_dt", "ssm_b_re", "ssm_b_im", "ssm_c_re", "ssm_c_im",
             "ssm_d", "glu_w", "glu_b", "w_ssm_out", "conv_w", "w_conv_out", "w_o", "ln1_g", "ln1_b", "w_gate", "w_up",
             "w_down", "ln2_g", "ln2_b"]
    return (loss.reshape(()), grad_x[None], *[out[p + n] for p in ("grad_", "delta_", "new_m_", "new_v_") for n in names])
```

```python
import functools
import math

import jax
import jax.numpy as jnp
from jax import lax
from jax.experimental import pallas as pl
from jax.experimental.pallas import tpu as pltpu

f32, bf16 = jnp.float32, jnp.bfloat16
S = jax.ShapeDtypeStruct
MESH = pl.DeviceIdType.MESH
HIGHEST = lax.Precision.HIGHEST

D = 1024
W = 512
NG, NP, GC = 32, 64, 16
F = 2816
NDEV = 8
FS = F // NDEV
IN_COLS = 8 * W
ALPHA = 2.0 ** 0.25
LN_EPS = 1e-5
ADAM_LR, ADAM_B1, ADAM_B2, ADAM_EPS, ADAM_WD, ADAM_STEP = 0.001, 0.9, 0.999, 1e-08, 0.01, 10
NC = 32
LANE = 128
SW = 4 * LANE
VMEM_LIMIT = 56 * 1024 * 1024
GRAD_DT = bf16
ANY = pl.BlockSpec(memory_space=pl.ANY)


def _cp(sem=None, vmem=None):
    return pltpu.CompilerParams(dimension_semantics=sem, vmem_limit_bytes=vmem)


def _resident(shape):
    return pl.BlockSpec(shape, lambda i: (0,) * len(shape), pipeline_mode=pl.Buffered(1))


def _dot(a, b):
    return jnp.dot(a, b, preferred_element_type=f32)


def _dot_nt(a, b):
    return lax.dot_general(a, b, (((1,), (1,)), ((), ())), preferred_element_type=f32)


def _dot_tn(a, b):
    return lax.dot_general(a, b, (((0,), (0,)), ((), ())), preferred_element_type=f32)


def _eye(n):
    return (lax.broadcasted_iota(jnp.int32, (n, n), 0) == lax.broadcasted_iota(jnp.int32, (n, n), 1)).astype(f32)


def _transpose_exact(a):
    return lax.dot_general(a, _eye(a.shape[0]), (((0,), (0,)), ((), ())), precision=HIGHEST, preferred_element_type=f32)


def _sigmoid(x):
    return 1.0 / (1.0 + jnp.exp(-x))


_GK = math.sqrt(2.0 / math.pi)


def _gelu(x):
    return 0.5 * x * (1.0 + jnp.tanh(_GK * (x + 0.044715 * x * x * x)))


def _gelu_grad(x):
    th = jnp.tanh(_GK * (x + 0.044715 * x * x * x))
    return 0.5 * (1.0 + th) + 0.5 * x * (1.0 - th * th) * _GK * (1.0 + 3.0 * 0.044715 * x * x)


ROW_PART = 256


def _row_parts(tm):
    return [slice(r, r + min(ROW_PART, tm)) for r in range(0, tm, min(ROW_PART, tm))]


def _ln_stats(r):
    mu = jnp.mean(r, axis=-1, keepdims=True)
    xc = r - mu
    var = jnp.mean(xc * xc, axis=-1, keepdims=True)
    rstd = lax.rsqrt(var + LN_EPS)
    return xc * rstd, rstd


def _ln_bwd(dy, xhat, rstd, g):
    dxh = dy * g
    m1 = jnp.mean(dxh, axis=-1, keepdims=True)
    m2 = jnp.mean(dxh * xhat, axis=-1, keepdims=True)
    return rstd * (dxh - m1 - xhat * m2)


def _coords():
    return lax.axis_index("x"), lax.axis_index("y"), lax.axis_index("c")


def _when(cond, fn):
    if cond is True:
        fn()
    else:
        pl.when(cond)(fn)


class GatherPlan:
    aliases = ()

    def __init__(self, arrs, srcs=None, into=None):
        n = self.n = len(arrs)
        self.srcs = srcs
        self.inputs = list(arrs) + list(into or [])
        if into:
            self.aliases = tuple((n + a, a) for a in range(n))
        self.out_shape = [S((NDEV,) + a.shape, a.dtype) for a in arrs]
        self.sems = [pltpu.SemaphoreType.DMA((n, 7)), pltpu.SemaphoreType.DMA((n, 7)), pltpu.SemaphoreType.DMA((n,))]

    def _has(self, dev):
        if self.srcs is None:
            return True
        idx = 4 * dev[0] + 2 * dev[1] + dev[2]
        return functools.reduce(jnp.logical_or, [idx == s for s in self.srcs])

    def _parts(self, ins, outs, sems):
        n = self.n
        send_sems, recv_sems, loc_sems = sems
        x, y, c = _coords()
        me, sib = (x, y, c), (x, y, 1 - c)
        chips = [(1 - x, y), (x, 1 - y), (1 - x, 1 - y)]

        def slot(a, dev):
            return outs[a].at[4 * dev[0] + 2 * dev[1] + dev[2]]

        def copy(a, k, block, to, src=None):
            return pltpu.make_async_remote_copy(
                src_ref=slot(a, block) if src is None else src, dst_ref=slot(a, block),
                send_sem=send_sems.at[a, k], recv_sem=recv_sems.at[a, k], device_id=to, device_id_type=MESH)

        each = [(j, chip, a) for j, chip in enumerate(chips) for a in range(n)]
        own = self._has(me)
        return dict(
            mine=lambda: [(pltpu.make_async_copy(ins[a], slot(a, me), loc_sems.at[a]), own) for a in range(n)],
            first=lambda: ([(copy(a, 0, me, sib, src=ins[a]), own) for a in range(n)]
                           + [(copy(a, 1 + j, me, (*chip, c), src=ins[a]), own) for j, chip, a in each]),
            landed=lambda: [(copy(a, 1 + j, (*chip, c), me), self._has((*chip, c))) for j, chip, a in each],
            passed=lambda: [(copy(a, 4 + j, (*chip, c), sib), self._has((*chip, c))) for j, chip, a in each],
            from_sib=lambda: ([(copy(a, 0, sib, me), self._has(sib)) for a in range(n)]
                              + [(copy(a, 4 + j, (*chip, 1 - c), me), self._has((*chip, 1 - c))) for j, chip, a in each]))

    def start(self, ins, outs, sems):
        p = self._parts(ins, outs, sems)
        for cp, cond in p["mine"]() + p["first"]():
            _when(cond, cp.start)

    def forward(self, ins, outs, sems):
        p = self._parts(ins, outs, sems)
        for (got, cond), (fwd, _) in zip(p["landed"](), p["passed"]()):
            def relay(got=got, fwd=fwd):
                got.wait_recv()
                fwd.start()

            _when(cond, relay)

    def finish(self, ins, outs, sems):
        p = self._parts(ins, outs, sems)
        for cp, cond in p["from_sib"]():
            _when(cond, cp.wait_recv)
        for cp, cond in p["first"]() + p["passed"]():
            _when(cond, cp.wait_send)
        for cp, cond in p["mine"]():
            _when(cond, cp.wait)


class ScatterPlan:
    aliases = ()

    def __init__(self, gs, only=None, into=None):
        n = self.n = len(gs)
        self.only = only
        self.inputs = list(gs) + list(into or [])
        if into:
            self.aliases = tuple((n + a, a) for a in range(n))
        self.out_shape = [S(g.shape, g.dtype) for g in gs]
        self.sems = [pltpu.SemaphoreType.DMA((n, 7)), pltpu.SemaphoreType.DMA((n, 7)), pltpu.SemaphoreType.DMA((n,))]

    def _owner(self, idx):
        if self.only is None:
            return True
        return functools.reduce(jnp.logical_or, [idx == b for b in self.only])

    def _copies(self, ins, outs, sems):
        n = self.n
        send_sems, recv_sems, loc_sems = sems
        x, y, c = _coords()
        me = 4 * x + 2 * y + c
        mine = self._owner(me)
        copies = [(pltpu.make_async_copy(ins[a].at[me], outs[a].at[me], loc_sems.at[a]), mine, None) for a in range(n)]
        for m in range(1, NDEV):
            px = 1 - x if m & 4 else x
            py = 1 - y if m & 2 else y
            pc = 1 - c if m & 1 else c
            peer = 4 * px + 2 * py + pc
            for a in range(n):
                copies.append((pltpu.make_async_remote_copy(
                    src_ref=ins[a].at[peer], dst_ref=outs[a].at[me],
                    send_sem=send_sems.at[a, m - 1], recv_sem=recv_sems.at[a, m - 1],
                    device_id=(px, py, pc), device_id_type=MESH), self._owner(peer), mine))
        return copies

    def start(self, ins, outs, sems):
        for cp, sends, _ in self._copies(ins, outs, sems):
            _when(sends, cp.start)

    def forward(self, ins, outs, sems):
        pass

    def finish(self, ins, outs, sems):
        for cp, sends, receives in self._copies(ins, outs, sems):
            if receives is None:
                _when(sends, cp.wait)
            else:
                _when(sends, cp.wait_send)
                _when(receives, cp.wait_recv)


class Plans:
    def __init__(self, plans):
        self.plans = plans
        self.inputs = [a for p in plans for a in p.inputs]
        self.out_shape = [s for p in plans for s in p.out_shape]
        self.sems = [s for p in plans for s in p.sems]
        self.aliases, i, o = [], 0, 0
        for p in plans:
            self.aliases += [(i + a, o + b) for a, b in p.aliases]
            i, o = i + len(p.inputs), o + len(p.out_shape)

    def _each(self, what, ins, outs, sems):
        i = o = s = 0
        for p in self.plans:
            ni, no, ns = len(p.inputs), len(p.out_shape), len(p.sems)
            getattr(p, what)(ins[i:i + ni], outs[o:o + no], sems[s:s + ns])
            i, o, s = i + ni, o + no, s + ns

    def start(self, ins, outs, sems):
        self._each("start", ins, outs, sems)

    def forward(self, ins, outs, sems):
        self._each("forward", ins, outs, sems)

    def finish(self, ins, outs, sems):
        self._each("finish", ins, outs, sems)


def _call(body, args, *, name, grid, in_specs, out_specs, out_shape, scratch=(), sem=None, vmem=None, plan=None,
          aliases=None, relay_step=None):
    aliases = aliases or {}
    if plan is None:
        outs = pl.pallas_call(body, name=name, grid=grid, in_specs=list(in_specs), out_specs=list(out_specs),
                              out_shape=list(out_shape), scratch_shapes=list(scratch), input_output_aliases=aliases,
                              compiler_params=_cp(sem, vmem))(*args)
        return list(outs), []
    ni, no, ns = len(in_specs), len(out_specs), len(scratch)
    pi, po = len(plan.inputs), len(plan.out_shape)
    aliases = {**aliases, **{ni + a: no + b for a, b in plan.aliases}}

    def wrapped(*refs):
        main_in, p_in = refs[:ni], refs[ni:ni + pi]
        main_out, p_out = refs[ni + pi:ni + pi + no], refs[ni + pi + no:ni + pi + no + po]
        main_scr, p_sems = refs[ni + pi + no + po:ni + pi + no + po + ns], refs[ni + pi + no + po + ns:]
        ids = [pl.program_id(d) for d in range(len(grid))]
        first = functools.reduce(jnp.logical_and, [i == 0 for i in ids])
        last = functools.reduce(jnp.logical_and, [i == g - 1 for i, g in zip(ids, grid)])

        @pl.when(first)
        def _():
            plan.start(p_in, p_out, p_sems)

        @pl.when(last if relay_step is None else ids[0] == max(relay_step, 0))
        def _():
            plan.forward(p_in, p_out, p_sems)

        body(*main_in, *main_out, *main_scr)

        @pl.when(last)
        def _():
            plan.finish(p_in, p_out, p_sems)

    outs = pl.pallas_call(
        wrapped, name=name, grid=grid, in_specs=list(in_specs) + [ANY] * pi, out_specs=list(out_specs) + [ANY] * po,
        out_shape=list(out_shape) + list(plan.out_shape), scratch_shapes=list(scratch) + list(plan.sems),
        input_output_aliases=aliases, compiler_params=_cp(("arbitrary",) * len(grid), vmem),
    )(*args, *plan.inputs)
    return list(outs[:no]), list(outs[no:])


def run_plan(plan, name):
    def body(*refs):
        ins, outs, sems = refs[:len(plan.inputs)], refs[len(plan.inputs):len(plan.inputs) + len(plan.out_shape)], \
            refs[len(plan.inputs) + len(plan.out_shape):]
        plan.start(ins, outs, sems)
        plan.forward(ins, outs, sems)
        plan.finish(ins, outs, sems)

    return pl.pallas_call(body, name=name, in_specs=[ANY] * len(plan.inputs), out_specs=[ANY] * len(plan.out_shape),
                          out_shape=list(plan.out_shape), scratch_shapes=list(plan.sems))(*plan.inputs)


def mm_tn(a, b, name, tn=512, into=None, block0=0, nblocks=None):
    T, K = a.shape
    N = b.shape[1]
    tn = min(tn, N)
    nblocks = nblocks or (N // tn if into is None else into.shape[0])

    def body(a_ref, b_ref, *rest):
        rest[-1][...] = _dot_tn(a_ref[...], b_ref[...]).astype(GRAD_DT)

    args, in_specs, aliases = [a, b], [_resident((T, K)), pl.BlockSpec((T, tn), lambda j: (0, j))], {}
    if into is not None:
        args.append(into)
        in_specs.append(ANY)
        aliases = {2: 0}
    (out,), _ = _call(body, args, name=name, grid=(N // tn,), in_specs=in_specs,
                      out_specs=[pl.BlockSpec((None, K, tn), lambda j: (block0 + j, 0, 0))],
                      out_shape=[S((nblocks, K, tn), GRAD_DT)], sem=("parallel",), vmem=VMEM_LIMIT, aliases=aliases)
    return out


def grad_w_in_rest(xb, dh, dcg, dbg, dga, dgb):
    T = xb.shape[0]
    order = ((0, 0), (1, 1), (2, 2), (3, 3), (4, 3), (5, 4), (6, 4))

    def body(x_ref, *refs):
        o_ref = refs[-1]
        j = pl.program_id(0)
        for step, opnd in order:
            @pl.when(j == step)
            def _(opnd=opnd):
                o_ref[...] = _dot_tn(x_ref[...], refs[opnd][...]).astype(GRAD_DT)

    once = lambda: pl.BlockSpec((T, W), lambda j: (0, 0), pipeline_mode=pl.Buffered(1))
    (out,), _ = _call(
        body, [xb, dh, dcg, dbg, dga, dgb], name="grad_w_in_rest", grid=(len(order),),
        in_specs=[_resident((T, D)), once(), once(), once(),
                  pl.BlockSpec((T, W), lambda j: (0, jnp.clip(j - 3, 0, 1))),
                  pl.BlockSpec((T, W), lambda j: (0, jnp.clip(j - 5, 0, 1)))],
        out_specs=[pl.BlockSpec((None, D, W), lambda j: (1 + j, 0, 0))],
        out_shape=[S((NDEV, D, W), GRAD_DT)], sem=("arbitrary",), vmem=VMEM_LIMIT)
    return out


def mm_tn_rows(a, b, name, tk=256, plan=None):
    T, K = a.shape
    N = b.shape[1]
    tk = min(tk, K)

    def body(a_ref, b_ref, o_ref):
        o_ref[...] = _dot_tn(a_ref[...], b_ref[...]).astype(GRAD_DT)

    (out,), sent = _call(body, [a, b], name=name, grid=(K // tk,),
                         in_specs=[pl.BlockSpec((T, tk), lambda i: (0, i)), _resident((T, N))],
                         out_specs=[pl.BlockSpec((tk, N), lambda i: (i, 0))], out_shape=[S((K, N), GRAD_DT)],
                         sem=("parallel",), vmem=VMEM_LIMIT, plan=plan)
    return out, sent


def prep_weights(ws):
    def body(*refs):
        for i in range(len(ws)):
            refs[len(ws) + i][...] = refs[i][...].astype(bf16)

    return pl.pallas_call(body, name="prep_weights", out_shape=[S(w.shape, bf16) for w in ws],
                          compiler_params=_cp(None, VMEM_LIMIT))(*ws)


REST_BLOCKS = (4, 5, 6, 7, 1, 2, 3)
REST_COLS = len(REST_BLOCKS) * W


def in_proj_u(x, win_g, b_in):
    T = x.shape[0]
    tm = min(1024, T)

    def body(x_ref, w_ref, b_ref, u_ref, xb_ref):
        xb = x_ref[...].astype(bf16)
        xb_ref[...] = xb
        u_ref[...] = _dot(xb, w_ref[...]) + b_ref[...]

    row = pl.BlockSpec((tm, D), lambda i: (i, 0))
    return pl.pallas_call(
        body, name="in_proj_u", grid=(T // tm,),
        in_specs=[row, pl.BlockSpec((None, D, W), lambda i: (0, 0, 0)), pl.BlockSpec((1, W), lambda i: (0, 0))],
        out_specs=[pl.BlockSpec((tm, W), lambda i: (i, 0)), row],
        out_shape=[S((T, W), f32), S((T, D), bf16)], compiler_params=_cp(("parallel",), VMEM_LIMIT),
    )(x, win_g, b_in)


def in_proj_rest(xb, win_g, b_in, plan):
    T = xb.shape[0]
    tm = min(512, T)

    def body(x_ref, w_ref, b_ref, o_ref):
        xb_ = x_ref[...]
        for i, k in enumerate(REST_BLOCKS):
            o_ref[:, i * W:(i + 1) * W] = _dot(xb_, w_ref[k]) + b_ref[:, k * W:(k + 1) * W]

    return _call(
        body, [xb, win_g, b_in], name="in_proj_rest", grid=(T // tm,),
        in_specs=[pl.BlockSpec((tm, D), lambda i: (i, 0)), _resident((NDEV, D, W)), _resident((1, IN_COLS))],
        out_specs=[pl.BlockSpec((tm, REST_COLS), lambda i: (i, 0))],
        out_shape=[S((T, REST_COLS), f32)], vmem=VMEM_LIMIT, plan=plan, relay_step=T // tm - 2)


def to_perm(a, cb0, name):
    T = a.shape[0]
    L = T // NC

    def body(a_ref, o_ref):
        def step(jb, carry):
            j0 = pl.multiple_of(jb * 8, 8)
            for q in range(NC // 8):
                x = jnp.stack([a_ref[pl.ds((8 * q + c) * L + j0, 8), :] for c in range(8)], axis=0)
                y = jnp.swapaxes(x, 0, 1)
                for j in range(8):
                    o_ref[pl.ds((j0 + j) * NC + 8 * q, 8), :] = y[j]
            return carry

        lax.fori_loop(0, L // 8, step, 0)

    return pl.pallas_call(
        body, name=name, grid=(W // LANE,),
        in_specs=[pl.BlockSpec((T, LANE), lambda k: (0, cb0 + k))], out_specs=pl.BlockSpec((T, LANE), lambda k: (0, k)),
        out_shape=S((T, W), f32), compiler_params=_cp(("parallel",), VMEM_LIMIT),
    )(a)


def from_perm(a, name, out_dtype=f32, plan=None):
    T = a.shape[0]
    L = T // NC

    def body(a_ref, o_ref):
        def step(jb, carry):
            j0 = pl.multiple_of(jb * 16, 16)
            for q in range(NC // 8):
                halves = []
                for h in range(2):
                    x = jnp.stack([a_ref[pl.ds((j0 + 8 * h + j) * NC + 8 * q, 8), :] for j in range(8)], axis=0)
                    halves.append(jnp.swapaxes(x, 0, 1))
                for c in range(8):
                    o_ref[pl.ds((8 * q + c) * L + j0, 16), :] = jnp.concatenate(
                        [halves[0][c], halves[1][c]], axis=0).astype(out_dtype)
            return carry

        lax.fori_loop(0, L // 16, step, 0)

    slab = pl.BlockSpec((T, LANE), lambda k: (0, k))
    return _call(body, [a], name=name, grid=(W // LANE,), in_specs=[slab], out_specs=[slab],
                 out_shape=[S((T, W), out_dtype)], sem=("parallel",), vmem=VMEM_LIMIT, plan=plan)


def _disc(lr, li, ldt):
    dt = jnp.exp(ldt)
    mag = jnp.exp(lr * dt)
    lbr = mag * jnp.cos(li * dt)
    lbi = mag * jnp.sin(li * dt)
    den = lr * lr + li * li
    nr = lbr - 1.0
    return lbr, lbi, (nr * lr + lbi * li) / den, (lbi * lr - nr * li) / den


def _per_channel(f):
    return jnp.broadcast_to(f[:, None, :], (NG, GC, NP)).reshape(NG * GC, NP)


def ssm_params(lam_re, lam_im, log_dt, br, bi):
    def body(lr_ref, li_ref, ldt_ref, br_ref, bi_ref, lbr_ref, lbi_ref, fr_ref, fi_ref, bbr_ref, bbi_ref):
        lbr, lbi, fr, fi = _disc(lr_ref[...], li_ref[...], ldt_ref[...])
        lbr_ref[...], lbi_ref[...], fr_ref[...], fi_ref[...] = lbr, lbi, fr, fi
        fr_, fi_, br_, bi_ = _per_channel(fr), _per_channel(fi), br_ref[...], bi_ref[...]
        bbr_ref[...] = fr_ * br_ - fi_ * bi_
        bbi_ref[...] = fr_ * bi_ + fi_ * br_

    return pl.pallas_call(body, name="ssm_params", out_shape=[S((NG, NP), f32)] * 4 + [S((NG * GC, NP), f32)] * 2)(
        lam_re, lam_im, log_dt, br, bi)


SCAN_UNROLL = 4
SCAN_LANES = 2 * LANE


def _steps(n, body, carry):
    main = n // SCAN_UNROLL

    def trip(t, c):
        for q in range(SCAN_UNROLL):
            c = body(t * SCAN_UNROLL + q, c)
        return c

    carry = lax.fori_loop(0, main, trip, carry)
    for i in range(main * SCAN_UNROLL, n):
        carry = body(i, carry)
    return carry


def _scan_body(T):
    L = T // NC
    RB = min(512, T)
    nsq = int(round(math.log2(L)))
    assert 2 ** nsq == L and T % RB == 0 and L % 16 == 0

    def rows(i):
        return pl.ds(pl.multiple_of(i * RB, RB), RB)

    def tile(j):
        return pl.ds(j * NC if isinstance(j, int) else pl.multiple_of(j * NC, NC), NC)

    def forward_states(u_ref, wb_ref, lbr_ref, lbi_ref, sre, sim, ere, eim):
        def bproj(i, carry):
            bu = _dot(u_ref[rows(i), :].astype(bf16), wb_ref[...])
            sre[rows(i), :] = bu[:, :SW]
            sim[rows(i), :] = bu[:, SW:]
            return carry

        lax.fori_loop(0, T // RB, bproj, 0)
        for lb in range(SW // SCAN_LANES):
            ls = slice(lb * SCAN_LANES, (lb + 1) * SCAN_LANES)
            ar = jnp.broadcast_to(lbr_ref[:, ls], (NC, SCAN_LANES))
            ai = jnp.broadcast_to(lbi_ref[:, ls], (NC, SCAN_LANES))

            def step(j, carry):
                xr, xi = carry
                nr = ar * xr - ai * xi + sre[tile(j), ls]
                ni = ar * xi + ai * xr + sim[tile(j), ls]
                sre[tile(j), ls] = nr
                sim[tile(j), ls] = ni
                return nr, ni

            zero = jnp.zeros((NC, SCAN_LANES), f32)
            _steps(L, step, (zero, zero))
            pr, pi = lbr_ref[:, ls], lbi_ref[:, ls]
            for _ in range(nsq):
                pr, pi = pr * pr - pi * pi, 2.0 * pr * pi
            er = jnp.zeros((1, SCAN_LANES), f32)
            ei = er
            ere[0:1, ls] = er
            eim[0:1, ls] = ei
            base = (L - 1) * NC
            for c in range(1, NC):
                lr_ = sre[base + c - 1:base + c, ls]
                li_ = sim[base + c - 1:base + c, ls]
                er, ei = lr_ + pr * er - pi * ei, li_ + pr * ei + pi * er
                ere[c:c + 1, ls] = er
                eim[c:c + 1, ls] = ei
            e_r, e_i = ere[:, ls].reshape(NC // 8, 8, SCAN_LANES), eim[:, ls].reshape(NC // 8, 8, SCAN_LANES)
            ar8, ai8 = ar[0:8], ai[0:8]

            def fix(j, carry):
                pwr, pwi = carry
                xr = sre[tile(j), ls].reshape(NC // 8, 8, SCAN_LANES) + (pwr * e_r - pwi * e_i)
                xi = sim[tile(j), ls].reshape(NC // 8, 8, SCAN_LANES) + (pwr * e_i + pwi * e_r)
                sre[tile(j), ls] = xr.reshape(NC, SCAN_LANES)
                sim[tile(j), ls] = xi.reshape(NC, SCAN_LANES)
                return pwr * ar8 - pwi * ai8, pwr * ai8 + pwi * ar8

            _steps(L, fix, (ar8, ai8))

    return L, RB, nsq, rows, tile, forward_states


def ssm_fwd(u_p, wb, wc, lbr, lbi, dsk, plan):
    T = u_p.shape[0]
    L, RB, nsq, rows, tile, forward_states = _scan_body(T)

    def body(u_ref, wb_ref, wc_ref, lbr_ref, lbi_ref, d_ref, y_ref, sre, sim, ere, eim):
        forward_states(u_ref, wb_ref, lbr_ref, lbi_ref, sre, sim, ere, eim)

        def cproj(i, carry):
            y = _dot(sre[rows(i), :].astype(bf16), wc_ref[0:SW, :]) + _dot(sim[rows(i), :].astype(bf16), wc_ref[SW:, :])
            y_ref[rows(i), :] = y + d_ref[...] * u_ref[rows(i), :]
            return carry

        lax.fori_loop(0, T // RB, cproj, 0)

    slab = pl.BlockSpec((T, LANE), lambda k: (0, k))
    return _call(
        body, [u_p, wb, wc, lbr, lbi, dsk], name="ssm_fwd", grid=(W // LANE,),
        in_specs=[slab, pl.BlockSpec((None, LANE, 2 * SW), lambda k: (k, 0, 0)),
                  pl.BlockSpec((None, 2 * SW, LANE), lambda k: (k, 0, 0)),
                  pl.BlockSpec((None, 1, SW), lambda k: (k, 0, 0)), pl.BlockSpec((None, 1, SW), lambda k: (k, 0, 0)),
                  pl.BlockSpec((None, 1, LANE), lambda k: (k, 0, 0))],
        out_specs=[slab], out_shape=[S((T, W), f32)],
        scratch=[pltpu.VMEM((T, SW), f32), pltpu.VMEM((T, SW), f32), pltpu.VMEM((NC, SW), f32), pltpu.VMEM((NC, SW), f32)],
        vmem=VMEM_LIMIT, plan=plan)


def ssm_bwd(u_p, dy_p, wb, wbT, wcT, lbr, lbi, dsk, plan):
    T = u_p.shape[0]
    L, RB, nsq, rows, tile, forward_states = _scan_body(T)

    def body(u_ref, dy_ref, wb_ref, wbT_ref, wcT_ref, lbr_ref, lbi_ref, d_ref,
             du_ref, dwb_ref, dwc_ref, dlr_ref, dli_ref, dd_ref, su_ref, sre, sim, gre, gim, ere, eim):
        forward_states(u_ref, wb_ref, lbr_ref, lbi_ref, sre, sim, ere, eim)

        def dstate(i, carry):
            g = _dot(dy_ref[rows(i), :].astype(bf16), wcT_ref[...])
            gre[rows(i), :] = g[:, :SW]
            gim[rows(i), :] = g[:, SW:]
            return carry

        lax.fori_loop(0, T // RB, dstate, 0)
        row = lax.broadcasted_iota(jnp.int32, (NC, SCAN_LANES), 0)
        for lb in range(SW // SCAN_LANES):
            ls = slice(lb * SCAN_LANES, (lb + 1) * SCAN_LANES)
            ar = jnp.broadcast_to(lbr_ref[:, ls], (NC, SCAN_LANES))
            ai = jnp.broadcast_to(lbi_ref[:, ls], (NC, SCAN_LANES))

            def step(i, carry):
                gr, gi = carry
                j = L - 1 - i
                nr = ar * gr + ai * gi + gre[tile(j), ls]
                ni = ar * gi - ai * gr + gim[tile(j), ls]
                gre[tile(j), ls] = nr
                gim[tile(j), ls] = ni
                return nr, ni

            zero = jnp.zeros((NC, SCAN_LANES), f32)
            _steps(L, step, (zero, zero))
            pr, pi = lbr_ref[:, ls], -lbi_ref[:, ls]
            for _ in range(nsq):
                pr, pi = pr * pr - pi * pi, 2.0 * pr * pi
            er = jnp.zeros((1, SCAN_LANES), f32)
            ei = er
            ere[NC - 1:NC, ls] = er
            eim[NC - 1:NC, ls] = ei
            for c in range(NC - 2, -1, -1):
                lr_ = gre[c + 1:c + 2, ls]
                li_ = gim[c + 1:c + 2, ls]
                er, ei = lr_ + pr * er - pi * ei, li_ + pr * ei + pi * er
                ere[c:c + 1, ls] = er
                eim[c:c + 1, ls] = ei
            e_r, e_i = ere[:, ls].reshape(NC // 8, 8, SCAN_LANES), eim[:, ls].reshape(NC // 8, 8, SCAN_LANES)
            ar8, ai8 = ar[0:8], ai[0:8]

            def fixed(j, pwr, pwi):
                gr = (gre[tile(j), ls].reshape(NC // 8, 8, SCAN_LANES) + (pwr * e_r - pwi * e_i)).reshape(NC, SCAN_LANES)
                gi = (gim[tile(j), ls].reshape(NC // 8, 8, SCAN_LANES) + (pwr * e_i + pwi * e_r)).reshape(NC, SCAN_LANES)
                gre[tile(j), ls] = gr
                gim[tile(j), ls] = gi
                return gr, gi

            def fix(i, carry):
                pwr, pwi, accr, acci = carry
                j = L - 1 - i
                gr, gi = fixed(j, pwr, pwi)
                xr, xi = sre[tile(j - 1), ls], sim[tile(j - 1), ls]
                return (pwr * ar8 + pwi * ai8, pwi * ar8 - pwr * ai8,
                        accr + gr * xr + gi * xi, acci + gi * xr - gr * xi)

            pwr, pwi, accr, acci = _steps(L - 1, fix, (ar8, -ai8, zero, zero))
            gr, gi = fixed(0, pwr, pwi)
            xr = jnp.where(row == 0, 0.0, pltpu.roll(sre[tile(L - 1), ls], 1, axis=0))
            xi = jnp.where(row == 0, 0.0, pltpu.roll(sim[tile(L - 1), ls], 1, axis=0))
            accr = accr + gr * xr + gi * xi
            acci = acci + gi * xr - gr * xi
            dlr_ref[:, ls] = jnp.sum(accr, axis=0, keepdims=True)
            dli_ref[:, ls] = jnp.sum(acci, axis=0, keepdims=True)

        dwb_ref[...] = jnp.zeros_like(dwb_ref)
        dwc_ref[...] = jnp.zeros_like(dwc_ref)
        dd_ref[...] = jnp.zeros_like(dd_ref)
        su_ref[...] = jnp.zeros_like(su_ref)

        def finish(i, carry):
            u32, dy32 = u_ref[rows(i), :], dy_ref[rows(i), :]
            ub, dyb = u32.astype(bf16), dy32.astype(bf16)
            gr, gi = gre[rows(i), :].astype(bf16), gim[rows(i), :].astype(bf16)
            du = _dot(gr, wbT_ref[0:SW, :]) + _dot(gi, wbT_ref[SW:, :]) + dy32 * d_ref[...]
            du_ref[rows(i), :] = du
            su_ref[...] += jnp.sum(du, axis=0, keepdims=True)
            dwb_ref[:, 0:SW] += _dot_tn(ub, gr)
            dwb_ref[:, SW:] += _dot_tn(ub, gi)
            dwc_ref[:, 0:SW] += _dot_tn(dyb, sre[rows(i), :].astype(bf16))
            dwc_ref[:, SW:] += _dot_tn(dyb, sim[rows(i), :].astype(bf16))
            dd_ref[...] += jnp.sum(dy32 * u32, axis=0, keepdims=True)
            return carry

        lax.fori_loop(0, T // RB, finish, 0)

    slab = pl.BlockSpec((T, LANE), lambda k: (0, k))
    wide = pl.BlockSpec((None, LANE, 2 * SW), lambda k: (k, 0, 0))
    tall = pl.BlockSpec((None, 2 * SW, LANE), lambda k: (k, 0, 0))
    vec = pl.BlockSpec((None, 1, SW), lambda k: (k, 0, 0))
    vecd = pl.BlockSpec((None, 1, LANE), lambda k: (k, 0, 0))
    nslab = W // LANE
    return _call(
        body, [u_p, dy_p, wb, wbT, wcT, lbr, lbi, dsk], name="ssm_bwd", grid=(nslab,),
        in_specs=[slab, slab, wide, tall, wide, vec, vec, vecd],
        out_specs=[slab, wide, wide, vec, vec, vecd, vecd],
        out_shape=[S((T, W), f32), S((nslab, LANE, 2 * SW), f32), S((nslab, LANE, 2 * SW), f32),
                   S((nslab, 1, SW), f32), S((nslab, 1, SW), f32), S((nslab, 1, LANE), f32), S((nslab, 1, LANE), f32)],
        scratch=[pltpu.VMEM((T, SW), f32)] * 4 + [pltpu.VMEM((NC, SW), f32)] * 2, vmem=VMEM_LIMIT, plan=plan)


def glu_fwd(yn, glu_w, glu_b):
    T = yn.shape[0]
    tm = min(512, T)

    def body(y_ref, w_ref, b_ref, o_ref):
        g = _gelu(y_ref[...])
        o_ref[...] = (g * _sigmoid(_dot(g.astype(bf16), w_ref[...]) + b_ref[...])).astype(bf16)

    return pl.pallas_call(
        body, name="glu_fwd", grid=(T // tm,),
        in_specs=[pl.BlockSpec((tm, W), lambda i: (i, 0)), pl.BlockSpec((W, W), lambda i: (0, 0)), pl.BlockSpec((1, W), lambda i: (0, 0))],
        out_specs=pl.BlockSpec((tm, W), lambda i: (i, 0)), out_shape=S((T, W), bf16), compiler_params=_cp(("parallel",)),
    )(yn, glu_w, glu_b)


def _shift_rows(cur, prev8, k):
    return pltpu.roll(jnp.concatenate([prev8, cur], axis=0), k, axis=0)[8:]


def _lift_rows(cur, next8, k):
    n = cur.shape[0]
    return pltpu.roll(jnp.concatenate([cur, next8], axis=0), n + 8 - k, axis=0)[:n]


def conv_fwd(proj, conv_w):
    T = proj.shape[0]
    RB = min(512, T)

    def body(h_ref, c_ref, b_ref, w_ref, o_ref):
        w0, w1, w2 = w_ref[0:1, :], w_ref[1:2, :], w_ref[2:3, :]

        def blk(i, carry):
            r0 = pl.multiple_of(i * RB, RB)
            rs = pl.ds(r0, RB)
            ch = c_ref[rs, :] * h_ref[rs, :]
            pr = pl.ds(jnp.maximum(r0 - 8, 0), 8)
            prev = jnp.where(i > 0, c_ref[pr, :] * h_ref[pr, :], 0.0)
            z = w2 * ch + w1 * _shift_rows(ch, prev, 1) + w0 * _shift_rows(ch, prev, 2)
            o_ref[rs, :] = (b_ref[rs, :] * z).astype(bf16)
            return carry

        lax.fori_loop(0, T // RB, blk, 0)

    nb = W // LANE
    return pl.pallas_call(
        body, name="conv_fwd", grid=(nb,),
        in_specs=[pl.BlockSpec((T, LANE), lambda k: (0, 4 * nb + k)), pl.BlockSpec((T, LANE), lambda k: (0, 5 * nb + k)),
                  pl.BlockSpec((T, LANE), lambda k: (0, 6 * nb + k)),pl.BlockSpec((3, LANE), lambda k: (0, k))],
        out_specs=pl.BlockSpec((T, LANE), lambda k: (0, k)), out_shape=S((T, W), bf16),
        compiler_params=_cp(("parallel",), VMEM_LIMIT),
    )(proj, proj, proj, conv_w)


def _dense_columns(blocks_ref, dense_ref):
    for k in range(NDEV):
        dense_ref[:, k * LANE:(k + 1) * LANE] = blocks_ref[k]


def merge_fwd(ya, yb, wso, wco, proj, plan):
    T = ya.shape[0]
    tm = min(1024, T)

    def body(ya_ref, yb_ref, wa_ref, wb_ref, ga_ref, gb_ref, o_ref, wa_s, wb_s):
        @pl.when(pl.program_id(0) == 0)
        def _():
            _dense_columns(wa_ref, wa_s)
            _dense_columns(wb_ref, wb_s)

        o_ref[...] = (_sigmoid(ga_ref[...]) * _dot(ya_ref[...], wa_s[...])
                      + _sigmoid(gb_ref[...]) * _dot(yb_ref[...], wb_s[...])).astype(bf16)

    act = pl.BlockSpec((tm, W), lambda i: (i, 0))
    return _call(
        body, [ya, yb, wso, wco, proj, proj], name="merge_fwd", grid=(T // tm,),
        in_specs=[act, act, _resident((NDEV, W, LANE)), _resident((NDEV, W, LANE)),
                  pl.BlockSpec((tm, D), lambda i: (i, 0)), pl.BlockSpec((tm, D), lambda i: (i, 1))],
        out_specs=[pl.BlockSpec((tm, D), lambda i: (i, 0))], out_shape=[S((T, D), bf16)],
        scratch=[pltpu.VMEM((W, D), bf16), pltpu.VMEM((W, D), bf16)], vmem=VMEM_LIMIT, plan=plan)


def mix_ln1(merged, w_o, x, g1, b1, plan):
    T = x.shape[0]
    tm = min(512, T)

    def body(m_ref, w_ref, x_ref, g_ref, b_ref, r_ref, x1_ref):
        for rs in _row_parts(tm):
            r = ALPHA * x_ref[rs, :] + _dot(m_ref[rs, :], w_ref[...])
            r_ref[rs, :] = r
            xhat, _ = _ln_stats(r)
            x1_ref[rs, :] = (xhat * g_ref[...] + b_ref[...]).astype(bf16)

    row = pl.BlockSpec((tm, D), lambda i: (i, 0))
    vec = pl.BlockSpec((1, D), lambda i: (0, 0))
    return _call(
        body, [merged, w_o, x, g1, b1], name="mix_ln1", grid=(T // tm,),
        in_specs=[row, _resident((D, D)), row, vec, vec],
        out_specs=[row, row], out_shape=[S((T, D), f32), S((T, D), bf16)], sem=("parallel",), vmem=VMEM_LIMIT, plan=plan,
        relay_step=T // tm - 2)


FT = 256


def gate_up(x1b, wgT, wuT, plan):
    T = x1b.shape[0]
    tm = min(512, T)

    def body(x_ref, wg_ref, wu_ref, g_ref, u_ref, h_ref):
        x = x_ref[...]
        for n in range(F // FT):
            cs = slice(n * FT, (n + 1) * FT)
            g = _dot_nt(x, wg_ref[cs, :])
            u = _dot_nt(x, wu_ref[cs, :])
            g_ref[:, cs] = g.astype(bf16)
            u_ref[:, cs] = u.astype(bf16)
            h_ref[:, cs] = (g * _sigmoid(g) * u).astype(bf16)

    osp = pl.BlockSpec((tm, F), lambda i: (i, 0))
    return _call(
        body, [x1b, wgT, wuT], name="gate_up", grid=(T // tm,),
        in_specs=[pl.BlockSpec((tm, D), lambda i: (i, 0)), _resident((F, D)), _resident((F, D))],
        out_specs=[osp, osp, osp], out_shape=[S((T, F), bf16)] * 3, vmem=VMEM_LIMIT, plan=plan, relay_step=T // tm - 3)


def down_loss(hid, w_down, r1, g1, b1, g2, b2, target):
    T = hid.shape[0]
    tm = min(512, T)

    def body(h_ref, w_ref, r1_ref, g1_ref, b1_ref, g2_ref, b2_ref, t_ref, dr_ref, drb_ref, loss_ref, dg_ref, db_ref):
        @pl.when(pl.program_id(0) == 0)
        def _():
            loss_ref[...] = jnp.zeros_like(loss_ref)
            dg_ref[...] = jnp.zeros_like(dg_ref)
            db_ref[...] = jnp.zeros_like(db_ref)

        for rs in _row_parts(tm):
            xh1, _ = _ln_stats(r1_ref[rs, :])
            x1 = xh1 * g1_ref[...] + b1_ref[...]
            r2 = ALPHA * x1 + _dot(h_ref[rs, :], w_ref[...])
            xh2, rstd2 = _ln_stats(r2)
            err = xh2 * g2_ref[...] + b2_ref[...] - t_ref[rs, :]
            loss_ref[...] += jnp.sum(jnp.mean(err * err, axis=-1, keepdims=True), axis=0, keepdims=True)
            dy = err * (1.0 / D)
            dg_ref[...] += jnp.sum(dy * xh2, axis=0, keepdims=True)
            db_ref[...] += jnp.sum(dy, axis=0, keepdims=True)
            dr = _ln_bwd(dy, xh2, rstd2, g2_ref[...])
            dr_ref[rs, :] = dr
            drb_ref[rs, :] = dr.astype(bf16)

    row = pl.BlockSpec((tm, D), lambda i: (i, 0))
    vec = pl.BlockSpec((1, D), lambda i: (0, 0))
    return pl.pallas_call(
        body, name="down_loss", grid=(T // tm,),
        in_specs=[pl.BlockSpec((tm, F), lambda i: (i, 0)), _resident((F, D)), row, vec, vec, vec, vec, row],
        out_specs=[row, row, pl.BlockSpec((1, 1), lambda i: (0, 0)), vec, vec],
        out_shape=[S((T, D), f32), S((T, D), bf16), S((1, 1), f32), S((1, D), f32), S((1, D), f32)],
        compiler_params=_cp(("arbitrary",), VMEM_LIMIT),
    )(hid, w_down, r1, g1, b1, g2, b2, target)


def ffn_bwd_act(dffn, w_down, gate, up, plan):
    T = dffn.shape[0]
    tm = min(512, T)

    def body(d_ref, w_ref, g_ref, u_ref, dg_ref, du_ref):
        for n in range(F // FT):
            cs = slice(n * FT, (n + 1) * FT)
            for rs in _row_parts(tm):
                dh = _dot_nt(d_ref[rs, :], w_ref[cs, :])
                g, u = g_ref[rs, cs].astype(f32), u_ref[rs, cs].astype(f32)
                sg = _sigmoid(g)
                t = g * sg
                du_ref[rs, cs] = (dh * t).astype(bf16)
                dg_ref[rs, cs] = (dh * u * (sg + t - t * sg)).astype(bf16)

    osp = pl.BlockSpec((tm, F), lambda i: (i, 0))
    return _call(
        body, [dffn, w_down, gate, up], name="ffn_bwd_act", grid=(T // tm,),
        in_specs=[pl.BlockSpec((tm, D), lambda i: (i, 0)), _resident((F, D)), osp, osp],
        out_specs=[osp, osp], out_shape=[S((T, F), bf16)] * 2, sem=("parallel",), vmem=VMEM_LIMIT, plan=plan)


def ffn_bwd_x(dgate, dup, wgT, wuT, dr2, r1, g1, plan):
    T = dr2.shape[0]
    tm = min(512, T)

    def body(dg_ref, du_ref, wg_ref, wu_ref, dr2_ref, r1_ref, g1_ref, dr_ref, drb_ref, dgam_ref, dbet_ref):
        @pl.when(pl.program_id(0) == 0)
        def _():
            dgam_ref[...] = jnp.zeros_like(dgam_ref)
            dbet_ref[...] = jnp.zeros_like(dbet_ref)

        for rs in _row_parts(tm):
            dx1 = ALPHA * dr2_ref[rs, :] + _dot(dg_ref[rs, :], wg_ref[...]) + _dot(du_ref[rs, :], wu_ref[...])
            xh, rstd = _ln_stats(r1_ref[rs, :])
            dgam_ref[...] += jnp.sum(dx1 * xh, axis=0, keepdims=True)
            dbet_ref[...] += jnp.sum(dx1, axis=0, keepdims=True)
            dr = _ln_bwd(dx1, xh, rstd, g1_ref[...])
            dr_ref[rs, :] = dr
            drb_ref[rs, :] = dr.astype(bf16)

    row = pl.BlockSpec((tm, D), lambda i: (i, 0))
    wide = pl.BlockSpec((tm, F), lambda i: (i, 0))
    wsp = _resident((F, D))
    vec = pl.BlockSpec((1, D), lambda i: (0, 0))
    return _call(
        body, [dgate, dup, wgT, wuT, dr2, r1, g1], name="ffn_bwd_x", grid=(T // tm,),
        in_specs=[wide, wide, wsp, wsp, row, row, vec],
        out_specs=[row, row, vec, vec], out_shape=[S((T, D), f32), S((T, D), bf16), S((1, D), f32), S((1, D), f32)],
        vmem=VMEM_LIMIT, plan=plan)


def merge_bwd(dmix, w_o, ya, yb, wso, wco, proj, plan):
    T = dmix.shape[0]
    tm = min(512, T)

    def body(dm_ref, wo_ref, ya_ref, yb_ref, wa_ref, wb_ref, ga_ref, gb_ref, dya_ref, dyb_ref, dga_ref, dgb_ref, sa_ref, sb_ref,
             wa_s, wb_s):
        @pl.when(pl.program_id(0) == 0)
        def _():
            _dense_columns(wa_ref, wa_s)
            _dense_columns(wb_ref, wb_s)

        dmer = _dot_nt(dm_ref[...], wo_ref[...])
        sa, sb = _sigmoid(ga_ref[...]), _sigmoid(gb_ref[...])
        dya_ref[...] = (dmer * sa).astype(bf16)
        dyb_ref[...] = (dmer * sb).astype(bf16)
        dga = dmer * _dot(ya_ref[...], wa_s[...]) * sa * (1.0 - sa)
        dgb = dmer * _dot(yb_ref[...], wb_s[...]) * sb * (1.0 - sb)
        dga_ref[...] = dga.astype(bf16)
        dgb_ref[...] = dgb.astype(bf16)
        sa_ref[...] = jnp.sum(dga, axis=0, keepdims=True)
        sb_ref[...] = jnp.sum(dgb, axis=0, keepdims=True)

    act = pl.BlockSpec((tm, W), lambda i: (i, 0))
    osp = pl.BlockSpec((tm, D), lambda i: (i, 0))
    ssp = pl.BlockSpec((None, 1, D), lambda i: (i, 0, 0))
    return _call(
        body, [dmix, w_o, ya, yb, wso, wco, proj, proj], name="merge_bwd", grid=(T // tm,),
        in_specs=[osp, _resident((D, D)), act, act, _resident((NDEV, W, LANE)), _resident((NDEV, W, LANE)),
                  pl.BlockSpec((tm, D), lambda i: (i, 0)), pl.BlockSpec((tm, D), lambda i: (i, 1))],
        out_specs=[osp, osp, osp, osp, ssp, ssp],
        out_shape=[S((T, D), bf16)] * 4 + [S((T // tm, 1, D), f32)] * 2,
        scratch=[pltpu.VMEM((W, D), bf16), pltpu.VMEM((W, D), bf16)], vmem=VMEM_LIMIT, plan=plan)


def branches_bwd_x(dYA, dYB, wso, wco, plan):
    T = dYA.shape[0]
    tm = min(1024, T)

    def body(da_ref, db_ref, wa_ref, wb_ref, oa_ref, ob_ref, wa_s, wb_s):
        @pl.when(pl.program_id(0) == 0)
        def _():
            _dense_columns(wa_ref, wa_s)
            _dense_columns(wb_ref, wb_s)

        oa_ref[...] = _dot_nt(da_ref[...], wa_s[...])
        ob_ref[...] = _dot_nt(db_ref[...], wb_s[...])

    row = pl.BlockSpec((tm, D), lambda i: (i, 0))
    osp = pl.BlockSpec((tm, W), lambda i: (i, 0))
    return _call(
        body, [dYA, dYB, wso, wco], name="branches_bwd_x", grid=(T // tm,),
        in_specs=[row, row, _resident((NDEV, W, LANE)), _resident((NDEV, W, LANE))],
        out_specs=[osp, osp], out_shape=[S((T, W), f32)] * 2,
        scratch=[pltpu.VMEM((W, D), bf16), pltpu.VMEM((W, D), bf16)], vmem=VMEM_LIMIT, plan=plan)


def branch_bwd_w(act, dY, name):
    T = act.shape[0]
    tk = W // 2

    def body(a_ref, d_ref, o_ref):
        res = _dot_tn(a_ref[...], d_ref[...])
        for k in range(NDEV):
            o_ref[k] = res[:, k * LANE:(k + 1) * LANE].astype(o_ref.dtype)

    return pl.pallas_call(
        body, name=name, grid=(W // tk,),
        in_specs=[pl.BlockSpec((T, tk), lambda i: (0, i)), _resident((T, D))],
        out_specs=pl.BlockSpec((NDEV, tk, LANE), lambda i: (0, i, 0)), out_shape=S((NDEV, W, LANE), GRAD_DT),
        compiler_params=_cp(("parallel",), VMEM_LIMIT),
    )(act, dY)


def glu_bwd(yn, dya, glu_w, glu_b):
    T = yn.shape[0]
    tm = min(512, T)

    def body(y_ref, d_ref, w_ref, b_ref, dy_ref, dsp_ref, g_ref, db_ref):
        @pl.when(pl.program_id(0) == 0)
        def _():
            db_ref[...] = jnp.zeros_like(db_ref)

        y, dya_ = y_ref[...], d_ref[...]
        g = _gelu(y)
        gb = g.astype(bf16)
        s = _sigmoid(_dot(gb, w_ref[...]) + b_ref[...])
        dsp = dya_ * g * s * (1.0 - s)
        dspb = dsp.astype(bf16)
        dg = dya_ * s + _dot_nt(dspb, w_ref[...])
        dy_ref[...] = dg * _gelu_grad(y)
        dsp_ref[...] = dspb
        g_ref[...] = gb
        db_ref[...] += jnp.sum(dsp, axis=0, keepdims=True)

    row = pl.BlockSpec((tm, W), lambda i: (i, 0))
    vec = pl.BlockSpec((1, W), lambda i: (0, 0))
    return pl.pallas_call(
        body, name="glu_bwd", grid=(T // tm,),
        in_specs=[row, row, pl.BlockSpec((W, W), lambda i: (0, 0)), vec],
        out_specs=[row, row, row, vec], out_shape=[S((T, W), f32), S((T, W), bf16), S((T, W), bf16), S((1, W), f32)],
        compiler_params=_cp(("arbitrary",)),
    )(yn, dya, glu_w, glu_b)


def conv_bwd(proj, dyb, conv_w):
    T = proj.shape[0]
    RB = min(512, T)
    nrb = T // RB

    def body(h_ref, c_ref, b_ref, d_ref, w_ref, dh_ref, dc_ref, db_ref, dw_ref, s_ref):
        w0, w1, w2 = w_ref[0:1, :], w_ref[1:2, :], w_ref[2:3, :]

        def blk(i, carry):
            a0, a1, a2, sh, sc, sb = carry
            r0 = pl.multiple_of(i * RB, RB)
            rs = pl.ds(r0, RB)
            h, cg, bg, dyb_ = h_ref[rs, :], c_ref[rs, :], b_ref[rs, :], d_ref[rs, :]
            ch = cg * h
            pr = pl.ds(jnp.maximum(r0 - 8, 0), 8)
            prev = jnp.where(i > 0, c_ref[pr, :] * h_ref[pr, :], 0.0)
            ch1, ch2 = _shift_rows(ch, prev, 1), _shift_rows(ch, prev, 2)
            dbg = dyb_ * (w2 * ch + w1 * ch1 + w0 * ch2)
            db_ref[rs, :] = dbg.astype(bf16)
            dz = dyb_ * bg
            nx = pl.ds(jnp.minimum(r0 + RB, T - 8), 8)
            nxt = jnp.where(i < nrb - 1, d_ref[nx, :] * b_ref[nx, :], 0.0)
            dch = w2 * dz + w1 * _lift_rows(dz, nxt, 1) + w0 * _lift_rows(dz, nxt, 2)
            dcg, dh = dch * h, dch * cg
            dc_ref[rs, :] = dcg.astype(bf16)
            dh_ref[rs, :] = dh.astype(bf16)
            col = lambda v: jnp.sum(v, axis=0, keepdims=True)
            return (a0 + col(dz * ch2), a1 + col(dz * ch1), a2 + col(dz * ch), sh + col(dh), sc + col(dcg), sb + col(dbg))

        zero = jnp.zeros((1, LANE), f32)
        a0, a1, a2, sh, sc, sb = lax.fori_loop(0, nrb, blk, (zero,) * 6)
        dw_ref[0:1, :] = a0
        dw_ref[1:2, :] = a1
        dw_ref[2:3, :] = a2
        s_ref[0:1, :] = sh
        s_ref[1:2, :] = sc
        s_ref[2:3, :] = sb

    nb = W // LANE
    slab = pl.BlockSpec((T, LANE), lambda k: (0, k))
    three = pl.BlockSpec((3, LANE), lambda k: (0, k))
    return pl.pallas_call(
        body, name="conv_bwd", grid=(nb,),
        in_specs=[pl.BlockSpec((T, LANE), lambda k: (0, 4 * nb + k)), pl.BlockSpec((T, LANE), lambda k: (0, 5 * nb + k)),
                  pl.BlockSpec((T, LANE), lambda k: (0, 6 * nb + k)),slab, three],
        out_specs=[slab, slab, slab, three, three],
        out_shape=[S((T, W), bf16)] * 3 + [S((3, W), f32)] * 2, compiler_params=_cp(("parallel",), VMEM_LIMIT),
    )(proj, proj, proj, dyb, conv_w)


def in_proj_bwd_x(parts, win_g, base, scale, name, plan=None):
    T = base.shape[0]
    tm = min(512, T)
    n = len(parts)

    def body(*refs):
        p_refs, w_ref, b_ref, o_ref = refs[:n], refs[n], refs[n + 1], refs[n + 2]
        acc = scale * b_ref[...]
        for p_ref, (_, _, k) in zip(p_refs, parts):
            acc += _dot_nt(p_ref[...], w_ref[k])
        o_ref[...] = acc

    row = pl.BlockSpec((tm, D), lambda i: (i, 0))
    p_specs = [pl.BlockSpec((tm, W), (lambda i, cb=cb: (i, cb))) for _, cb, _ in parts]
    return _call(
        body, [a for a, _, _ in parts] + [win_g, base], name=name, grid=(T // tm,),
        in_specs=p_specs + [_resident((NDEV, D, W)), row],
        out_specs=[row], out_shape=[S((T, D), f32)], vmem=VMEM_LIMIT, plan=plan)


def ssm_param_bwd(lam_re, lam_im, log_dt, fr, fi, br, bi, dwb, dwcT, dlbr, dlbi):
    def body(lr_ref, li_ref, ldt_ref, fr_ref, fi_ref, br_ref, bi_ref, dwb_ref, dwc_ref, dlbr_ref, dlbi_ref,
             dbr_ref, dbi_ref, dlr_ref, dli_ref, dldt_ref, dcr_ref, dci_ref, dr_s, di_s):
        for k in range(W // LANE):
            for gl in range(NG // (W // LANE)):
                rows, src = slice((8 * k + gl) * GC, (8 * k + gl + 1) * GC), slice(gl * GC, (gl + 1) * GC)
                re, im = slice(gl * NP, (gl + 1) * NP), slice(SW + gl * NP, SW + (gl + 1) * NP)
                dr_s[rows, :] = dwb_ref[k, src, re]
                di_s[rows, :] = dwb_ref[k, src, im]
                dcr_ref[rows, :] = dwc_ref[k, src, re]
                dci_ref[rows, :] = -dwc_ref[k, src, im]
        fr_, fi_ = _per_channel(fr_ref[...]), _per_channel(fi_ref[...])
        br_, bi_, dr, di = br_ref[...], bi_ref[...], dr_s[...], di_s[...]
        dbr_ref[...] = fr_ * dr + fi_ * di
        dbi_ref[...] = fr_ * di - fi_ * dr
        dfr = jnp.sum((dr * br_ + di * bi_).reshape(NG, GC, NP), axis=1)
        dfi = jnp.sum((di * br_ - dr * bi_).reshape(NG, GC, NP), axis=1)
        _, vjp = jax.vjp(_disc, lr_ref[...], li_ref[...], ldt_ref[...])
        dlr_ref[...], dli_ref[...], dldt = vjp((dlbr_ref[...], dlbi_ref[...], dfr, dfi))
        dldt_ref[...] = _transpose_exact(dldt)

    blk = S((NG * GC, NP), f32)
    return pl.pallas_call(
        body, name="ssm_param_bwd", out_shape=[blk, blk, S((NG, NP), f32), S((NG, NP), f32), S((1, NG), f32), blk, blk],
        scratch_shapes=[pltpu.VMEM((NG * GC, NP), f32)] * 2)(
        lam_re, lam_im, log_dt, fr, fi, br, bi, dwb, dwcT, dlbr, dlbi)


def _adam(w, g, m, v):
    m = ADAM_B1 * m + (1.0 - ADAM_B1) * g
    v = ADAM_B2 * v + (1.0 - ADAM_B2) * (g * g)
    m_hat = m / (1.0 - ADAM_B1 ** ADAM_STEP)
    v_hat = v / (1.0 - ADAM_B2 ** ADAM_STEP)
    return -ADAM_LR * (m_hat / (jnp.sqrt(v_hat) + ADAM_EPS) + ADAM_WD * w), m, v


def adam_update(w, m, v, contrib, name, rows_per_block=None):
    R, C = w.shape
    n = contrib.shape[0]
    tr = min(rows_per_block or R, R)

    def body(w_ref, m_ref, v_ref, c_ref, g_ref, d_ref, nm_ref, nv_ref):
        g = c_ref[0].astype(f32)
        for k in range(1, n):
            g = g + c_ref[k].astype(f32)
        g_ref[...] = g
        d_ref[...], nm_ref[...], nv_ref[...] = _adam(w_ref[...], g, m_ref[...], v_ref[...])

    blk = pl.BlockSpec((tr, C), lambda i: (i, 0))
    return pl.pallas_call(
        body, name=name, grid=(R // tr,), in_specs=[blk, blk, blk, pl.BlockSpec((n, tr, C), lambda i: (0, i, 0))],
        out_specs=[blk] * 4, out_shape=[S((R, C), f32)] * 4, compiler_params=_cp(("parallel",), VMEM_LIMIT),
    )(w, m, v, contrib)


_ROWVEC = (("b_in", IN_COLS), ("ssm_d", W), ("glu_b", W), ("ln1_g", D), ("ln1_b", D), ("ln2_g", D), ("ln2_b", D))
_HALF = NG * GC // 2
_BC_LANE = {"ssm_b_re": 0, "ssm_b_im": NP, "ssm_c_re": 0, "ssm_c_im": NP}
_PACK = {}
_r = 0
for _n, _k in _ROWVEC:
    _PACK[_n] = _r
    _r += _k // LANE
for _n, _rows in (("ssm_lambda", NG), ("scalars", 8), ("ssm_b", _HALF), ("ssm_c", _HALF), ("conv_w", 16)):
    _PACK[_n] = _r
    _r += _rows
for _n in _BC_LANE:
    _PACK[_n] = _PACK[_n[:5]]
PACK_ROWS = _r
assert PACK_ROWS % 8 == 0
_SMALL = ("b_in", "ssm_lambda_re", "ssm_lambda_im", "ssm_log_dt", "ssm_b_re", "ssm_b_im", "ssm_c_re", "ssm_c_im",
          "ssm_d", "glu_b", "ln1_g", "ln1_b", "ln2_g", "ln2_b")


def pack_grads(su, shcb, sga, sgb, dd, dglu_b, dln1_g, dln1_b, dln2_g, dln2_b, dlam_re, dlam_im, dldt, sqerr, dbr, dbi,
               dc_re, dc_im, dconv):
    nI = sga.shape[0]

    def body(su_ref, sh_ref, sga_ref, sgb_ref, dd_ref, gb_ref, l1g_ref, l1b_ref, l2g_ref, l2b_ref, lr_ref, li_ref, dt_ref,
             sq_ref, br_ref, bi_ref, cr_ref, ci_ref, cw_ref, o_ref):
        o_ref[...] = jnp.zeros_like(o_ref)

        def put_row(name, v):
            r0 = _PACK[name]
            for i in range(v.shape[1] // LANE):
                o_ref[r0 + i:r0 + i + 1, :] = v[:, i * LANE:(i + 1) * LANE]

        ga, gb = sga_ref[0], sgb_ref[0]
        for i in range(1, nI):
            ga, gb = ga + sga_ref[i], gb + sgb_ref[i]
        put_row("b_in", jnp.concatenate([su_ref[k] for k in range(W // LANE)]
                                        + [sh_ref[0:1, :], sh_ref[1:2, :], sh_ref[2:3, :], ga, gb], axis=1))
        put_row("ssm_d", jnp.concatenate([dd_ref[k] for k in range(W // LANE)], axis=1))
        put_row("glu_b", gb_ref[...])
        put_row("ln1_g", l1g_ref[...])
        put_row("ln1_b", l1b_ref[...])
        put_row("ln2_g", l2g_ref[...])
        put_row("ln2_b", l2b_ref[...])
        r0 = _PACK["ssm_lambda"]
        o_ref[r0:r0 + NG, 0:NP] = lr_ref[...]
        o_ref[r0:r0 + NG, NP:2 * NP] = li_ref[...]
        r0 = _PACK["scalars"]
        o_ref[r0:r0 + 1, 0:NG] = dt_ref[...]
        o_ref[r0 + 1:r0 + 2, 0:1] = sq_ref[...]
        for name, ref in (("ssm_b_re", br_ref), ("ssm_b_im", bi_ref), ("ssm_c_re", cr_ref), ("ssm_c_im", ci_ref)):
            r0, l0 = _PACK[name], _BC_LANE[name]
            o_ref[r0:r0 + _HALF, l0:l0 + NP] = pltpu.bitcast(ref[...].astype(bf16), f32)
        for cb in range(W // LANE):
            o_ref[_PACK["conv_w"] + 3 * cb:_PACK["conv_w"] + 3 * cb + 3, :] = cw_ref[:, cb * LANE:(cb + 1) * LANE]

    return pl.pallas_call(body, name="pack_grads", out_shape=S((PACK_ROWS, LANE), f32))(
        su, shcb, sga, sgb, dd, dglu_b, dln1_g, dln1_b, dln2_g, dln2_b, dlam_re, dlam_im, dldt, sqerr, dbr, dbi, dc_re, dc_im,
        dconv)


def adam_small(packed_all, params):
    names = list(_SMALL) + ["conv_w"]
    flat = [a for n in names for a in params[n]]

    def body(*refs):
        p_ref = refs[0]
        ins = refs[1:1 + 3 * len(names)]
        outs = refs[1 + 3 * len(names):-2]
        loss_ref, g_ref = refs[-2], refs[-1]

        def part(k, rs=slice(None), ls=slice(None)):
            return p_ref[k, rs, ls]

        g_all = part(0)
        for k in range(1, NDEV):
            g_all = g_all + part(k)
        g_ref[...] = g_all

        def rows(name, r0, n, l0=0, lanes=LANE):
            return g_ref[_PACK[name] + r0:_PACK[name] + r0 + n, l0:l0 + lanes]

        def grad_of(name):
            if name in dict(_ROWVEC):
                return jnp.concatenate([rows(name, i, 1) for i in range(dict(_ROWVEC)[name] // LANE)], axis=1)
            if name in ("ssm_lambda_re", "ssm_lambda_im"):
                return rows("ssm_lambda", 0, NG, NP * (name == "ssm_lambda_im"), NP)[None]
            if name == "ssm_log_dt":
                return rows("scalars", 0, 1, 0, NG)
            if name in _BC_LANE:
                rs, ls = slice(_PACK[name], _PACK[name] + _HALF), slice(_BC_LANE[name], _BC_LANE[name] + NP)
                g = pltpu.bitcast(part(0, rs, ls), bf16).astype(f32)
                for k in range(1, NDEV):
                    g = g + pltpu.bitcast(part(k, rs, ls), bf16).astype(f32)
                return g.reshape(1, NG, GC, NP)
            full = jnp.concatenate([rows("conv_w", 3 * cb, 3) for cb in range(W // LANE)], axis=1)
            x, y, c = _coords()
            col0 = (4 * x + 2 * y + c) * (W // NDEV)
            sel = (lax.broadcasted_iota(jnp.int32, (W, W // NDEV), 0)
                   == lax.broadcasted_iota(jnp.int32, (W, W // NDEV), 1) + col0).astype(f32)
            return jnp.dot(full, sel, precision=HIGHEST, preferred_element_type=f32)[None]

        loss_ref[...] = 0.5 * rows("scalars", 1, 1, 0, 1)
        for i, name in enumerate(names):
            w_ref, m_ref, v_ref = ins[3 * i:3 * i + 3]
            g = grad_of(name)
            d, m, v = _adam(w_ref[...], g, m_ref[...], v_ref[...])
            outs[4 * i][...] = g
            outs[4 * i + 1][...] = d
            outs[4 * i + 2][...] = m
            outs[4 * i + 3][...] = v

    out_shape = [S(params[n][0].shape, f32) for n in names for _ in range(4)] + [S((1, 1), f32)]
    res = pl.pallas_call(body, name="adam_small", out_shape=out_shape, scratch_shapes=[pltpu.VMEM((PACK_ROWS, LANE), f32)],
                         compiler_params=_cp(None, VMEM_LIMIT))(packed_all, *flat)
    return {n: res[4 * i:4 * i + 4] for i, n in enumerate(names)}, res[-1]


def _block_diag(wgt):
    eye = jnp.eye(8, dtype=wgt.dtype)
    out = wgt[:, :, :, None, :] * eye[None, :, None, :, None]
    return out.reshape(4, 8 * wgt.shape[2], 8 * wgt.shape[3])


def kernel(x, w_in, b_in, ssm_lambda_re, ssm_lambda_im, ssm_log_dt, ssm_b_re, ssm_b_im, ssm_c_re, ssm_c_im, ssm_d, glu_w, glu_b, w_ssm_out, conv_w, w_conv_out, w_o, ln1_g, ln1_b, w_gate, w_up, w_down, ln2_g, ln2_b, loss_target, m_w_in, m_b_in, m_ssm_lambda_re, m_ssm_lambda_im, m_ssm_log_dt, m_ssm_b_re, m_ssm_b_im, m_ssm_c_re, m_ssm_c_im, m_ssm_d, m_glu_w, m_glu_b, m_w_ssm_out, m_conv_w, m_w_conv_out, m_w_o, m_ln1_g, m_ln1_b, m_w_gate, m_w_up, m_w_down, m_ln2_g, m_ln2_b, v_w_in, v_b_in, v_ssm_lambda_re, v_ssm_lambda_im, v_ssm_log_dt, v_ssm_b_re, v_ssm_b_im, v_ssm_c_re, v_ssm_c_im, v_ssm_d, v_glu_w, v_glu_b, v_w_ssm_out, v_conv_w, v_w_conv_out, v_w_o, v_ln1_g, v_ln1_b, v_w_gate, v_w_up, v_w_down, v_ln2_g, v_ln2_b):
    given = dict(locals())
    xs = x[0]
    target = loss_target[0]

    tr = lambda a: jnp.swapaxes(a[0], 0, 1)
    win_s, glu_s, wso_s, wco_s, wo_s, wgT_s, wuT_s, wd_s = prep_weights(
        [w_in[0], glu_w[0], w_ssm_out[0], w_conv_out[0], w_o[0], tr(w_gate), tr(w_up), w_down[0]])
    (win_g,) = run_plan(GatherPlan([win_s], srcs=(0,)), "gather_w_in_u")

    lam_re, lam_im = ssm_lambda_re[0], ssm_lambda_im[0]
    ldt = ssm_log_dt[0].reshape(NG, 1)
    br2 = jnp.swapaxes(ssm_b_re[0], 1, 2).reshape(NG * GC, NP)
    bi2 = jnp.swapaxes(ssm_b_im[0], 1, 2).reshape(NG * GC, NP)
    lbr, lbi, fr, fi, bbr, bbi = ssm_params(lam_re, lam_im, ldt, br2, bi2)
    bb_t = lambda b: b.reshape(4, 8, GC, NP)
    wb = jnp.concatenate([_block_diag(bb_t(bbr)), _block_diag(bb_t(bbi))], axis=2)
    c_t = lambda c: c.reshape(4, 8, GC, NP).transpose(0, 1, 3, 2)
    wc = jnp.concatenate([_block_diag(c_t(ssm_c_re[0])), -_block_diag(c_t(ssm_c_im[0]))], axis=1)
    wbT, wcT = wb.transpose(0, 2, 1), wc.transpose(0, 2, 1)
    wb, wc, wbT, wcT = wb.astype(bf16), wc.astype(bf16), wbT.astype(bf16), wcT.astype(bf16)
    lbr_s, lbi_s = lbr.reshape(4, 1, SW), lbi.reshape(4, 1, SW)
    dsk = ssm_d[0].reshape(4, 1, LANE)

    u_nat, xb = in_proj_u(xs, win_g, b_in)
    u_p = to_perm(u_nat, 0, "perm_u")
    half_a, half_b = (0, 3, 5, 6), (1, 2, 4, 7)
    (y_p,), (win_g, conv_g, glu_g, wso_g) = ssm_fwd(
        u_p, wb, wc, lbr_s, lbi_s, dsk,
        Plans([GatherPlan([win_s], srcs=tuple(range(1, NDEV)), into=[win_g]), GatherPlan([conv_w[0], glu_s, wso_s])]))
    conv_f = conv_g.transpose(1, 0, 2).reshape(3, W)
    (proj,), (wco_g, wo_g, wgT_g) = in_proj_rest(
        xb, win_g, b_in, Plans([GatherPlan([wco_s, wo_s]), GatherPlan([wgT_s], srcs=half_a)]))
    glu_f, wo_f = glu_g.reshape(W, W), wo_g.reshape(D, D)
    (yn,), _ = from_perm(y_p, "unperm_y")
    ya = glu_fwd(yn, glu_f, glu_b)
    yb = conv_fwd(proj, conv_f)
    (merged,), (wgT_g, wuT_g) = merge_fwd(
        ya, yb, wso_g, wco_g, proj,
        Plans([GatherPlan([wgT_s], srcs=half_b, into=[wgT_g]), GatherPlan([wuT_s], srcs=half_a)]))
    (r1, x1b), (wuT_g,) = mix_ln1(merged, wo_f, xs, ln1_g, ln1_b, GatherPlan([wuT_s], srcs=half_b, into=[wuT_g]))
    wgT, wuT = wgT_g.reshape(F, D), wuT_g.reshape(F, D)
    (gate, up, hid), (wd_g,) = gate_up(x1b, wgT, wuT, GatherPlan([wd_s]))
    wd_f = wd_g.reshape(F, D)
    dr2, dffn, sqerr, dln2_g, dln2_b = down_loss(hid, wd_f, r1, ln1_g, ln1_b, ln2_g, ln2_b, target)

    dwd, _ = mm_tn_rows(hid, dffn, "grad_w_down")
    dwd = dwd.reshape(NDEV, FS, D)
    (dgate, dup), (r_wd,) = ffn_bwd_act(dffn, wd_f, gate, up, ScatterPlan([dwd], only=half_a))
    dwgT, (r_wd,) = mm_tn_rows(dgate, x1b, "grad_w_gate", plan=ScatterPlan([dwd], only=half_b, into=[r_wd]))
    dwgT = dwgT.reshape(NDEV, FS, D)
    dwuT, (r_wgT,) = mm_tn_rows(dup, x1b, "grad_w_up", plan=ScatterPlan([dwgT], only=half_a))
    dwuT = dwuT.reshape(NDEV, FS, D)
    (dr1, dmix, dln1_g, dln1_b), (r_wgT,) = ffn_bwd_x(dgate, dup, wgT, wuT, dr2, r1, ln1_g,
                                                     ScatterPlan([dwgT], only=half_b, into=[r_wgT]))
    (dYA, dYB, dga, dgb, sga, sgb), (r_wuT,) = merge_bwd(dmix, wo_f, ya, yb, wso_g, wco_g, proj,
                                                         ScatterPlan([dwuT], only=half_a))
    dwo, _ = mm_tn_rows(merged, dmix, "grad_w_o")
    dwo = dwo.reshape(NDEV, D // NDEV, D)
    (dya, dyb), (r_wuT,) = branches_bwd_x(dYA, dYB, wso_g, wco_g, ScatterPlan([dwuT], only=half_b, into=[r_wuT]))
    dwso = branch_bwd_w(ya, dYA, "grad_w_ssm_out")
    dwco = branch_bwd_w(yb, dYB, "grad_w_conv_out")
    dyn, dsp, gb, dglu_b = glu_bwd(yn, dya, glu_f, glu_b)
    dglu = mm_tn_rows(gb, dsp, "grad_glu_w")[0].reshape(NDEV, W // NDEV, W)
    dh, dcg, dbg, dconv, shcb = conv_bwd(proj, dyb, conv_f)
    dwin = grad_w_in_rest(xb, dh, dcg, dbg, dga, dgb)
    dy_p = to_perm(dyn, 0, "perm_dy")
    (du_p, dwb, dwcT, dlbr_s, dlbi_s, dd, su), (r_wo, r_wso, r_wco, r_glu, r_win) = ssm_bwd(
        u_p, dy_p, wb, wbT, wcT, lbr_s, lbi_s, dsk,
        Plans([ScatterPlan([dwo, dwso, dwco, dglu]), ScatterPlan([dwin], only=tuple(range(1, NDEV)))]))

    dbr2, dbi2, dlam_re, dlam_im, dldt, dc_re, dc_im = ssm_param_bwd(
        lam_re, lam_im, ldt, fr, fi, br2, bi2, dwb, dwcT, dlbr_s.reshape(NG, NP), dlbi_s.reshape(NG, NP))
    packed = pack_grads(su, shcb, sga, sgb, dd, dglu_b, dln1_g, dln1_b, dln2_g, dln2_b, dlam_re, dlam_im, dldt, sqerr,
                        dbr2, dbi2, dc_re, dc_im, dconv)
    (du,), _ = from_perm(du_p, "unperm_du", bf16)
    dwin = mm_tn(xb, du, "grad_w_in_u", block0=0, into=dwin)

    rest = [(dh, 0, 1), (dcg, 0, 2), (dbg, 0, 3), (dga, 0, 4), (dga, 1, 5), (dgb, 0, 6), (dgb, 1, 7)]
    (gx_rest,), (r_win, small_all) = in_proj_bwd_x(
        rest, win_g, dr1, ALPHA, "in_proj_bwd_x_rest",
        Plans([ScatterPlan([dwin], only=(0,), into=[r_win]), GatherPlan([packed])]))
    (grad_x,), _ = in_proj_bwd_x([(du, 0, 0)], win_g, gx_rest, 1.0, "in_proj_bwd_x_u")

    out = {}

    def put(name, res, back=lambda a: a[None]):
        out["grad_" + name], out["delta_" + name], out["new_m_" + name], out["new_v_" + name] = [back(r) for r in res]

    put("w_in", adam_update(w_in[0], m_w_in[0], v_w_in[0], r_win, "adam_w_in", 256))
    put("glu_w", adam_update(glu_w[0], m_glu_w[0], v_glu_w[0], r_glu, "adam_glu_w"))
    put("w_ssm_out", adam_update(w_ssm_out[0], m_w_ssm_out[0], v_w_ssm_out[0], r_wso, "adam_w_ssm_out"))
    put("w_conv_out", adam_update(w_conv_out[0], m_w_conv_out[0], v_w_conv_out[0], r_wco, "adam_w_conv_out"))
    put("w_o", adam_update(w_o[0], m_w_o[0], v_w_o[0], r_wo, "adam_w_o"))
    put("w_down", adam_update(w_down[0], m_w_down[0], v_w_down[0], r_wd, "adam_w_down", 176))
    untr = lambda a: jnp.swapaxes(a, 0, 1)[None]
    put("w_gate", adam_update(tr(w_gate), tr(m_w_gate), tr(v_w_gate), r_wgT, "adam_w_gate", 176), untr)
    put("w_up", adam_update(tr(w_up), tr(m_w_up), tr(v_w_up), r_wuT, "adam_w_up", 176), untr)
    as_c = lambda a: jnp.swapaxes(a, 2, 3)
    params = {n: (given[n], given["m_" + n], given["v_" + n]) for n in list(_SMALL) + ["conv_w"]}
    for n in ("ssm_b_re", "ssm_b_im"):
        params[n] = tuple(as_c(a) for a in params[n])
    small, loss = adam_small(small_all, params)
    for n, res in small.items():
        put(n, res, as_c if n in ("ssm_b_re", "ssm_b_im") else (lambda a: a))

    names = ["w_in", "b_in", "ssm_lambda_re", "ssm_lambda_im", "ssm_log_dt", "ssm_b_re", "ssm_b_im", "ssm_c_re", "ssm_c_im",
             "ssm_d", "glu_w", "glu_b", "w_ssm_out", "conv_w", "w_conv_out", "w_o", "ln1_g", "ln1_b", "w_gate", "w_up",
             "w_down", "ln2_g", "ln2_b"]
    return (loss.reshape(()), grad_x[None], *[out[p + n] for p in ("grad_", "delta_", "new_m_", "new_v_") for n in names])
```

```python
import functools
import math

import jax
import jax.numpy as jnp
from jax import lax
from jax.experimental import pallas as pl
from jax.experimental.pallas import tpu as pltpu

f32, bf16 = jnp.float32, jnp.bfloat16
S = jax.ShapeDtypeStruct
MESH = pl.DeviceIdType.MESH
HIGHEST = lax.Precision.HIGHEST

D = 1024
W = 512
NG, NP, GC = 32, 64, 16
F = 2816
NDEV = 8
FS = F // NDEV
IN_COLS = 8 * W
ALPHA = 2.0 ** 0.25
LN_EPS = 1e-5
ADAM_LR, ADAM_B1, ADAM_B2, ADAM_EPS, ADAM_WD, ADAM_STEP = 0.001, 0.9, 0.999, 1e-08, 0.01, 10
NC = 32
LANE = 128
SW = 4 * LANE
VMEM_LIMIT = 56 * 1024 * 1024
GRAD_DT = bf16
ANY = pl.BlockSpec(memory_space=pl.ANY)


def _cp(sem=None, vmem=None):
    return pltpu.CompilerParams(dimension_semantics=sem, vmem_limit_bytes=vmem)


def _resident(shape):
    return pl.BlockSpec(shape, lambda i: (0,) * len(shape), pipeline_mode=pl.Buffered(1))


def _dot(a, b):
    return jnp.dot(a, b, preferred_element_type=f32)


def _dot_nt(a, b):
    return lax.dot_general(a, b, (((1,), (1,)), ((), ())), preferred_element_type=f32)


def _dot_tn(a, b):
    return lax.dot_general(a, b, (((0,), (0,)), ((), ())), preferred_element_type=f32)


def _eye(n):
    return (lax.broadcasted_iota(jnp.int32, (n, n), 0) == lax.broadcasted_iota(jnp.int32, (n, n), 1)).astype(f32)


def _transpose_exact(a):
    return lax.dot_general(a, _eye(a.shape[0]), (((0,), (0,)), ((), ())), precision=HIGHEST, preferred_element_type=f32)


def _sigmoid(x):
    return 1.0 / (1.0 + jnp.exp(-x))


_GK = math.sqrt(2.0 / math.pi)


def _gelu(x):
    return 0.5 * x * (1.0 + jnp.tanh(_GK * (x + 0.044715 * x * x * x)))


def _gelu_grad(x):
    th = jnp.tanh(_GK * (x + 0.044715 * x * x * x))
    return 0.5 * (1.0 + th) + 0.5 * x * (1.0 - th * th) * _GK * (1.0 + 3.0 * 0.044715 * x * x)


ROW_PART = 256


def _row_parts(tm):
    return [slice(r, r + min(ROW_PART, tm)) for r in range(0, tm, min(ROW_PART, tm))]


def _ln_stats(r):
    mu = jnp.mean(r, axis=-1, keepdims=True)
    xc = r - mu
    var = jnp.mean(xc * xc, axis=-1, keepdims=True)
    rstd = lax.rsqrt(var + LN_EPS)
    return xc * rstd, rstd


def _ln_bwd(dy, xhat, rstd, g):
    dxh = dy * g
    m1 = jnp.mean(dxh, axis=-1, keepdims=True)
    m2 = jnp.mean(dxh * xhat, axis=-1, keepdims=True)
    return rstd * (dxh - m1 - xhat * m2)


def _coords():
    return lax.axis_index("x"), lax.axis_index("y"), lax.axis_index("c")


def _when(cond, fn):
    if cond is True:
        fn()
    else:
        pl.when(cond)(fn)


class GatherPlan:
    aliases = ()

    def __init__(self, arrs, srcs=None, into=None):
        n = self.n = len(arrs)
        self.srcs = srcs
        self.inputs = list(arrs) + list(into or [])
        if into:
            self.aliases = tuple((n + a, a) for a in range(n))
        self.out_shape = [S((NDEV,) + a.shape, a.dtype) for a in arrs]
        self.sems = [pltpu.SemaphoreType.DMA((n, 7)), pltpu.SemaphoreType.DMA((n, 7)), pltpu.SemaphoreType.DMA((n,))]

    def _has(self, dev):
        if self.srcs is None:
            return True
        idx = 4 * dev[0] + 2 * dev[1] + dev[2]
        return functools.reduce(jnp.logical_or, [idx == s for s in self.srcs])

    def _parts(self, ins, outs, sems):
        n = self.n
        send_sems, recv_sems, loc_sems = sems
        x, y, c = _coords()
        me, sib = (x, y, c), (x, y, 1 - c)
        chips = [(1 - x, y), (x, 1 - y), (1 - x, 1 - y)]

        def slot(a, dev):
            return outs[a].at[4 * dev[0] + 2 * dev[1] + dev[2]]

        def copy(a, k, block, to, src=None):
            return pltpu.make_async_remote_copy(
                src_ref=slot(a, block) if src is None else src, dst_ref=slot(a, block),
                send_sem=send_sems.at[a, k], recv_sem=recv_sems.at[a, k], device_id=to, device_id_type=MESH)

        each = [(j, chip, a) for j, chip in enumerate(chips) for a in range(n)]
        own = self._has(me)
        return dict(
            mine=lambda: [(pltpu.make_async_copy(ins[a], slot(a, me), loc_sems.at[a]), own) for a in range(n)],
            first=lambda: ([(copy(a, 0, me, sib, src=ins[a]), own) for a in range(n)]
                           + [(copy(a, 1 + j, me, (*chip, c), src=ins[a]), own) for j, chip, a in each]),
            landed=lambda: [(copy(a, 1 + j, (*chip, c), me), self._has((*chip, c))) for j, chip, a in each],
            passed=lambda: [(copy(a, 4 + j, (*chip, c), sib), self._has((*chip, c))) for j, chip, a in each],
            from_sib=lambda: ([(copy(a, 0, sib, me), self._has(sib)) for a in range(n)]
                              + [(copy(a, 4 + j, (*chip, 1 - c), me), self._has((*chip, 1 - c))) for j, chip, a in each]))

    def start(self, ins, outs, sems):
        p = self._parts(ins, outs, sems)
        for cp, cond in p["mine"]() + p["first"]():
            _when(cond, cp.start)

    def forward(self, ins, outs, sems):
        p = self._parts(ins, outs, sems)
        for (got, cond), (fwd, _) in zip(p["landed"](), p["passed"]()):
            def relay(got=got, fwd=fwd):
                got.wait_recv()
                fwd.start()

            _when(cond, relay)

    def finish(self, ins, outs, sems):
        p = self._parts(ins, outs, sems)
        for cp, cond in p["from_sib"]():
            _when(cond, cp.wait_recv)
        for cp, cond in p["first"]() + p["passed"]():
            _when(cond, cp.wait_send)
        for cp, cond in p["mine"]():
            _when(cond, cp.wait)


class ScatterPlan:
    aliases = ()

    def __init__(self, gs, only=None, into=None):
        n = self.n = len(gs)
        self.only = only
        self.inputs = list(gs) + list(into or [])
        if into:
            self.aliases = tuple((n + a, a) for a in range(n))
        self.out_shape = [S(g.shape, g.dtype) for g in gs]
        self.sems = [pltpu.SemaphoreType.DMA((n, 7)), pltpu.SemaphoreType.DMA((n, 7)), pltpu.SemaphoreType.DMA((n,))]

    def _owner(self, idx):
        if self.only is None:
            return True
        return functools.reduce(jnp.logical_or, [idx == b for b in self.only])

    def _copies(self, ins, outs, sems):
        n = self.n
        send_sems, recv_sems, loc_sems = sems
        x, y, c = _coords()
        me = 4 * x + 2 * y + c
        mine = self._owner(me)
        copies = [(pltpu.make_async_copy(ins[a].at[me], outs[a].at[me], loc_sems.at[a]), mine, None) for a in range(n)]
        for m in range(1, NDEV):
            px = 1 - x if m & 4 else x
            py = 1 - y if m & 2 else y
            pc = 1 - c if m & 1 else c
            peer = 4 * px + 2 * py + pc
            for a in range(n):
                copies.append((pltpu.make_async_remote_copy(
                    src_ref=ins[a].at[peer], dst_ref=outs[a].at[me],
                    send_sem=send_sems.at[a, m - 1], recv_sem=recv_sems.at[a, m - 1],
                    device_id=(px, py, pc), device_id_type=MESH), self._owner(peer), mine))
        return copies

    def start(self, ins, outs, sems):
        for cp, sends, _ in self._copies(ins, outs, sems):
            _when(sends, cp.start)

    def forward(self, ins, outs, sems):
        pass

    def finish(self, ins, outs, sems):
        for cp, sends, receives in self._copies(ins, outs, sems):
            if receives is None:
                _when(sends, cp.wait)
            else:
                _when(sends, cp.wait_send)
                _when(receives, cp.wait_recv)


class Plans:
    def __init__(self, plans):
        self.plans = plans
        self.inputs = [a for p in plans for a in p.inputs]
        self.out_shape = [s for p in plans for s in p.out_shape]
        self.sems = [s for p in plans for s in p.sems]
        self.aliases, i, o = [], 0, 0
        for p in plans:
            self.aliases += [(i + a, o + b) for a, b in p.aliases]
            i, o = i + len(p.inputs), o + len(p.out_shape)

    def _each(self, what, ins, outs, sems):
        i = o = s = 0
        for p in self.plans:
            ni, no, ns = len(p.inputs), len(p.out_shape), len(p.sems)
            getattr(p, what)(ins[i:i + ni], outs[o:o + no], sems[s:s + ns])
            i, o, s = i + ni, o + no, s + ns

    def start(self, ins, outs, sems):
        self._each("start", ins, outs, sems)

    def forward(self, ins, outs, sems):
        self._each("forward", ins, outs, sems)

    def finish(self, ins, outs, sems):
        self._each("finish", ins, outs, sems)


def _call(body, args, *, name, grid, in_specs, out_specs, out_shape, scratch=(), sem=None, vmem=None, plan=None,
          aliases=None, relay_step=None):
    aliases = aliases or {}
    if plan is None:
        outs = pl.pallas_call(body, name=name, grid=grid, in_specs=list(in_specs), out_specs=list(out_specs),
                              out_shape=list(out_shape), scratch_shapes=list(scratch), input_output_aliases=aliases,
                              compiler_params=_cp(sem, vmem))(*args)
        return list(outs), []
    ni, no, ns = len(in_specs), len(out_specs), len(scratch)
    pi, po = len(plan.inputs), len(plan.out_shape)
    aliases = {**aliases, **{ni + a: no + b for a, b in plan.aliases}}

    def wrapped(*refs):
        main_in, p_in = refs[:ni], refs[ni:ni + pi]
        main_out, p_out = refs[ni + pi:ni + pi + no], refs[ni + pi + no:ni + pi + no + po]
        main_scr, p_sems = refs[ni + pi + no + po:ni + pi + no + po + ns], refs[ni + pi + no + po + ns:]
        ids = [pl.program_id(d) for d in range(len(grid))]
        first = functools.reduce(jnp.logical_and, [i == 0 for i in ids])
        last = functools.reduce(jnp.logical_and, [i == g - 1 for i, g in zip(ids, grid)])

        @pl.when(first)
        def _():
            plan.start(p_in, p_out, p_sems)

        @pl.when(last if relay_step is None else ids[0] == max(relay_step, 0))
        def _():
            plan.forward(p_in, p_out, p_sems)

        body(*main_in, *main_out, *main_scr)

        @pl.when(last)
        def _():
            plan.finish(p_in, p_out, p_sems)

    outs = pl.pallas_call(
        wrapped, name=name, grid=grid, in_specs=list(in_specs) + [ANY] * pi, out_specs=list(out_specs) + [ANY] * po,
        out_shape=list(out_shape) + list(plan.out_shape), scratch_shapes=list(scratch) + list(plan.sems),
        input_output_aliases=aliases, compiler_params=_cp(("arbitrary",) * len(grid), vmem),
    )(*args, *plan.inputs)
    return list(outs[:no]), list(outs[no:])


def run_plan(plan, name):
    def body(*refs):
        ins, outs, sems = refs[:len(plan.inputs)], refs[len(plan.inputs):len(plan.inputs) + len(plan.out_shape)], \
            refs[len(plan.inputs) + len(plan.out_shape):]
        plan.start(ins, outs, sems)
        plan.forward(ins, outs, sems)
        plan.finish(ins, outs, sems)

    return pl.pallas_call(body, name=name, in_specs=[ANY] * len(plan.inputs), out_specs=[ANY] * len(plan.out_shape),
                          out_shape=list(plan.out_shape), scratch_shapes=list(plan.sems))(*plan.inputs)


def mm_tn(a, b, name, tn=512, into=None, block0=0, nblocks=None):
    T, K = a.shape
    N = b.shape[1]
    tn = min(tn, N)
    nblocks = nblocks or (N // tn if into is None else into.shape[0])

    def body(a_ref, b_ref, *rest):
        rest[-1][...] = _dot_tn(a_ref[...], b_ref[...]).astype(GRAD_DT)

    args, in_specs, aliases = [a, b], [_resident((T, K)), pl.BlockSpec((T, tn), lambda j: (0, j))], {}
    if into is not None:
        args.append(into)
        in_specs.append(ANY)
        aliases = {2: 0}
    (out,), _ = _call(body, args, name=name, grid=(N // tn,), in_specs=in_specs,
                      out_specs=[pl.BlockSpec((None, K, tn), lambda j: (block0 + j, 0, 0))],
                      out_shape=[S((nblocks, K, tn), GRAD_DT)], sem=("parallel",), vmem=VMEM_LIMIT, aliases=aliases)
    return out


def grad_w_in_rest(xb, dh, dcg, dbg, dga, dgb, plan):
    T = xb.shape[0]
    order = ((0, 0), (1, 1), (2, 2), (3, 3), (4, 3), (5, 4), (6, 4))

    def body(x_ref, *refs):
        o_ref = refs[-1]
        j = pl.program_id(0)
        for step, opnd in order:
            @pl.when(j == step)
            def _(opnd=opnd):
                o_ref[...] = _dot_tn(x_ref[...], refs[opnd][...]).astype(GRAD_DT)

    once = lambda: pl.BlockSpec((T, W), lambda j: (0, 0), pipeline_mode=pl.Buffered(1))
    (out,), sent = _call(
        body, [xb, dh, dcg, dbg, dga, dgb], name="grad_w_in_rest", grid=(len(order),),
        in_specs=[_resident((T, D)), once(), once(), once(),
                  pl.BlockSpec((T, W), lambda j: (0, jnp.clip(j - 3, 0, 1))),
                  pl.BlockSpec((T, W), lambda j: (0, jnp.clip(j - 5, 0, 1)))],
        out_specs=[pl.BlockSpec((None, D, W), lambda j: (1 + j, 0, 0))],
        out_shape=[S((NDEV, D, W), GRAD_DT)], sem=("arbitrary",), vmem=VMEM_LIMIT, plan=plan)
    return out, sent


def mm_tn_rows(a, b, name, tk=256, plan=None):
    T, K = a.shape
    N = b.shape[1]
    tk = min(tk, K)

    def body(a_ref, b_ref, o_ref):
        o_ref[...] = _dot_tn(a_ref[...], b_ref[...]).astype(GRAD_DT)

    (out,), sent = _call(body, [a, b], name=name, grid=(K // tk,),
                         in_specs=[pl.BlockSpec((T, tk), lambda i: (0, i)), _resident((T, N))],
                         out_specs=[pl.BlockSpec((tk, N), lambda i: (i, 0))], out_shape=[S((K, N), GRAD_DT)],
                         sem=("parallel",), vmem=VMEM_LIMIT, plan=plan)
    return out, sent


def prep_weights(ws):
    def body(*refs):
        for i in range(len(ws)):
            refs[len(ws) + i][...] = refs[i][...].astype(bf16)

    return pl.pallas_call(body, name="prep_weights", out_shape=[S(w.shape, bf16) for w in ws],
                          compiler_params=_cp(None, VMEM_LIMIT))(*ws)


REST_BLOCKS = (4, 5, 6, 7, 1, 2, 3)
REST_COLS = len(REST_BLOCKS) * W


def in_proj_u(x, win_g, b_in):
    T = x.shape[0]
    tm = min(1024, T)

    def body(x_ref, w_ref, b_ref, u_ref, xb_ref):
        xb = x_ref[...].astype(bf16)
        xb_ref[...] = xb
        u_ref[...] = _dot(xb, w_ref[...]) + b_ref[...]

    row = pl.BlockSpec((tm, D), lambda i: (i, 0))
    return pl.pallas_call(
        body, name="in_proj_u", grid=(T // tm,),
        in_specs=[row, pl.BlockSpec((None, D, W), lambda i: (0, 0, 0)), pl.BlockSpec((1, W), lambda i: (0, 0))],
        out_specs=[pl.BlockSpec((tm, W), lambda i: (i, 0)), row],
        out_shape=[S((T, W), f32), S((T, D), bf16)], compiler_params=_cp(("parallel",), VMEM_LIMIT),
    )(x, win_g, b_in)


def in_proj_rest(xb, win_g, b_in, plan):
    T = xb.shape[0]
    tm = min(512, T)

    def body(x_ref, w_ref, b_ref, o_ref):
        xb_ = x_ref[...]
        for i, k in enumerate(REST_BLOCKS):
            o_ref[:, i * W:(i + 1) * W] = _dot(xb_, w_ref[k]) + b_ref[:, k * W:(k + 1) * W]

    return _call(
        body, [xb, win_g, b_in], name="in_proj_rest", grid=(T // tm,),
        in_specs=[pl.BlockSpec((tm, D), lambda i: (i, 0)), _resident((NDEV, D, W)), _resident((1, IN_COLS))],
        out_specs=[pl.BlockSpec((tm, REST_COLS), lambda i: (i, 0))],
        out_shape=[S((T, REST_COLS), f32)], vmem=VMEM_LIMIT, plan=plan, relay_step=T // tm - 2)


def to_perm(a, cb0, name):
    T = a.shape[0]
    L = T // NC

    def body(a_ref, o_ref):
        def step(jb, carry):
            j0 = pl.multiple_of(jb * 8, 8)
            for q in range(NC // 8):
                x = jnp.stack([a_ref[pl.ds((8 * q + c) * L + j0, 8), :] for c in range(8)], axis=0)
                y = jnp.swapaxes(x, 0, 1)
                for j in range(8):
                    o_ref[pl.ds((j0 + j) * NC + 8 * q, 8), :] = y[j]
            return carry

        lax.fori_loop(0, L // 8, step, 0)

    return pl.pallas_call(
        body, name=name, grid=(W // LANE,),
        in_specs=[pl.BlockSpec((T, LANE), lambda k: (0, cb0 + k))], out_specs=pl.BlockSpec((T, LANE), lambda k: (0, k)),
        out_shape=S((T, W), f32), compiler_params=_cp(("parallel",), VMEM_LIMIT),
    )(a)


def from_perm(a, name, out_dtype=f32, plan=None):
    T = a.shape[0]
    L = T // NC

    def body(a_ref, o_ref):
        def step(jb, carry):
            j0 = pl.multiple_of(jb * 16, 16)
            for q in range(NC // 8):
                halves = []
                for h in range(2):
                    x = jnp.stack([a_ref[pl.ds((j0 + 8 * h + j) * NC + 8 * q, 8), :] for j in range(8)], axis=0)
                    halves.append(jnp.swapaxes(x, 0, 1))
                for c in range(8):
                    o_ref[pl.ds((8 * q + c) * L + j0, 16), :] = jnp.concatenate(
                        [halves[0][c], halves[1][c]], axis=0).astype(out_dtype)
            return carry

        lax.fori_loop(0, L // 16, step, 0)

    slab = pl.BlockSpec((T, LANE), lambda k: (0, k))
    return _call(body, [a], name=name, grid=(W // LANE,), in_specs=[slab], out_specs=[slab],
                 out_shape=[S((T, W), out_dtype)], sem=("parallel",), vmem=VMEM_LIMIT, plan=plan)


def _disc(lr, li, ldt):
    dt = jnp.exp(ldt)
    mag = jnp.exp(lr * dt)
    lbr = mag * jnp.cos(li * dt)
    lbi = mag * jnp.sin(li * dt)
    den = lr * lr + li * li
    nr = lbr - 1.0
    return lbr, lbi, (nr * lr + lbi * li) / den, (lbi * lr - nr * li) / den


def _per_channel(f):
    return jnp.broadcast_to(f[:, None, :], (NG, GC, NP)).reshape(NG * GC, NP)


def ssm_params(lam_re, lam_im, log_dt, br, bi):
    def body(lr_ref, li_ref, ldt_ref, br_ref, bi_ref, lbr_ref, lbi_ref, fr_ref, fi_ref, bbr_ref, bbi_ref):
        lbr, lbi, fr, fi = _disc(lr_ref[...], li_ref[...], ldt_ref[...])
        lbr_ref[...], lbi_ref[...], fr_ref[...], fi_ref[...] = lbr, lbi, fr, fi
        fr_, fi_, br_, bi_ = _per_channel(fr), _per_channel(fi), br_ref[...], bi_ref[...]
        bbr_ref[...] = fr_ * br_ - fi_ * bi_
        bbi_ref[...] = fr_ * bi_ + fi_ * br_

    return pl.pallas_call(body, name="ssm_params", out_shape=[S((NG, NP), f32)] * 4 + [S((NG * GC, NP), f32)] * 2)(
        lam_re, lam_im, log_dt, br, bi)


SCAN_UNROLL = 4
SCAN_LANES = 2 * LANE


def _steps(n, body, carry):
    main = n // SCAN_UNROLL

    def trip(t, c):
        for q in range(SCAN_UNROLL):
            c = body(t * SCAN_UNROLL + q, c)
        return c

    carry = lax.fori_loop(0, main, trip, carry)
    for i in range(main * SCAN_UNROLL, n):
        carry = body(i, carry)
    return carry


def _scan_body(T):
    L = T // NC
    RB = min(512, T)
    nsq = int(round(math.log2(L)))
    assert 2 ** nsq == L and T % RB == 0 and L % 16 == 0

    def rows(i):
        return pl.ds(pl.multiple_of(i * RB, RB), RB)

    def tile(j):
        return pl.ds(j * NC if isinstance(j, int) else pl.multiple_of(j * NC, NC), NC)

    def forward_states(u_ref, wb_ref, lbr_ref, lbi_ref, sre, sim, ere, eim):
        def bproj(i, carry):
            bu = _dot(u_ref[rows(i), :].astype(bf16), wb_ref[...])
            sre[rows(i), :] = bu[:, :SW]
            sim[rows(i), :] = bu[:, SW:]
            return carry

        lax.fori_loop(0, T // RB, bproj, 0)
        for lb in range(SW // SCAN_LANES):
            ls = slice(lb * SCAN_LANES, (lb + 1) * SCAN_LANES)
            ar = jnp.broadcast_to(lbr_ref[:, ls], (NC, SCAN_LANES))
            ai = jnp.broadcast_to(lbi_ref[:, ls], (NC, SCAN_LANES))

            def step(j, carry):
                xr, xi = carry
                nr = ar * xr - ai * xi + sre[tile(j), ls]
                ni = ar * xi + ai * xr + sim[tile(j), ls]
                sre[tile(j), ls] = nr
                sim[tile(j), ls] = ni
                return nr, ni

            zero = jnp.zeros((NC, SCAN_LANES), f32)
            _steps(L, step, (zero, zero))
            pr, pi = lbr_ref[:, ls], lbi_ref[:, ls]
            for _ in range(nsq):
                pr, pi = pr * pr - pi * pi, 2.0 * pr * pi
            er = jnp.zeros((1, SCAN_LANES), f32)
            ei = er
            ere[0:1, ls] = er
            eim[0:1, ls] = ei
            base = (L - 1) * NC
            for c in range(1, NC):
                lr_ = sre[base + c - 1:base + c, ls]
                li_ = sim[base + c - 1:base + c, ls]
                er, ei = lr_ + pr * er - pi * ei, li_ + pr * ei + pi * er
                ere[c:c + 1, ls] = er
                eim[c:c + 1, ls] = ei
            e_r, e_i = ere[:, ls].reshape(NC // 8, 8, SCAN_LANES), eim[:, ls].reshape(NC // 8, 8, SCAN_LANES)
            ar8, ai8 = ar[0:8], ai[0:8]

            def fix(j, carry):
                pwr, pwi = carry
                xr = sre[tile(j), ls].reshape(NC // 8, 8, SCAN_LANES) + (pwr * e_r - pwi * e_i)
                xi = sim[tile(j), ls].reshape(NC // 8, 8, SCAN_LANES) + (pwr * e_i + pwi * e_r)
                sre[tile(j), ls] = xr.reshape(NC, SCAN_LANES)
                sim[tile(j), ls] = xi.reshape(NC, SCAN_LANES)
                return pwr * ar8 - pwi * ai8, pwr * ai8 + pwi * ar8

            _steps(L, fix, (ar8, ai8))

    return L, RB, nsq, rows, tile, forward_states


def ssm_fwd(u_p, wb, wc, lbr, lbi, dsk, plan):
    T = u_p.shape[0]
    L, RB, nsq, rows, tile, forward_states = _scan_body(T)
    nslab = W // LANE

    def body(u_ref, wb_ref, wc_ref, lbr_ref, lbi_ref, d_ref, y_ref, xr_ref, xi_ref, sre, sim, ere, eim):
        forward_states(u_ref, wb_ref, lbr_ref, lbi_ref, sre, sim, ere, eim)

        def cproj(i, carry):
            xr, xi = sre[rows(i), :].astype(bf16), sim[rows(i), :].astype(bf16)
            xr_ref[rows(i), :] = xr
            xi_ref[rows(i), :] = xi
            y = _dot(xr, wc_ref[0:SW, :]) + _dot(xi, wc_ref[SW:, :])
            y_ref[rows(i), :] = y + d_ref[...] * u_ref[rows(i), :]
            return carry

        lax.fori_loop(0, T // RB, cproj, 0)

    slab = pl.BlockSpec((T, LANE), lambda k: (0, k))
    states = pl.BlockSpec((T, SW), lambda k: (0, k))
    return _call(
        body, [u_p, wb, wc, lbr, lbi, dsk], name="ssm_fwd", grid=(nslab,),
        in_specs=[slab, pl.BlockSpec((None, LANE, 2 * SW), lambda k: (k, 0, 0)),
                  pl.BlockSpec((None, 2 * SW, LANE), lambda k: (k, 0, 0)),
                  pl.BlockSpec((None, 1, SW), lambda k: (k, 0, 0)), pl.BlockSpec((None, 1, SW), lambda k: (k, 0, 0)),
                  pl.BlockSpec((None, 1, LANE), lambda k: (k, 0, 0))],
        out_specs=[slab, states, states], out_shape=[S((T, W), f32), S((T, nslab * SW), bf16), S((T, nslab * SW), bf16)],
        scratch=[pltpu.VMEM((T, SW), f32), pltpu.VMEM((T, SW), f32), pltpu.VMEM((NC, SW), f32), pltpu.VMEM((NC, SW), f32)],
        vmem=VMEM_LIMIT, plan=plan)


def ssm_bwd(u_p, dy_p, xr, xi, wbT, wcT, lbr, lbi, dsk, plan):
    T = u_p.shape[0]
    L, RB, nsq, rows, tile, _ = _scan_body(T)

    def body(u_ref, dy_ref, sre, sim, wbT_ref, wcT_ref, lbr_ref, lbi_ref, d_ref,
             du_ref, dwb_ref, dwc_ref, dlr_ref, dli_ref, dd_ref, su_ref, gre, gim, ere, eim):
        def dstate(i, carry):
            g = _dot(dy_ref[rows(i), :].astype(bf16), wcT_ref[...])
            gre[rows(i), :] = g[:, :SW]
            gim[rows(i), :] = g[:, SW:]
            return carry

        lax.fori_loop(0, T // RB, dstate, 0)
        row = lax.broadcasted_iota(jnp.int32, (NC, SCAN_LANES), 0)
        for lb in range(SW // SCAN_LANES):
            ls = slice(lb * SCAN_LANES, (lb + 1) * SCAN_LANES)
            ar = jnp.broadcast_to(lbr_ref[:, ls], (NC, SCAN_LANES))
            ai = jnp.broadcast_to(lbi_ref[:, ls], (NC, SCAN_LANES))

            def step(i, carry):
                gr, gi = carry
                j = L - 1 - i
                nr = ar * gr + ai * gi + gre[tile(j), ls]
                ni = ar * gi - ai * gr + gim[tile(j), ls]
                gre[tile(j), ls] = nr
                gim[tile(j), ls] = ni
                return nr, ni

            zero = jnp.zeros((NC, SCAN_LANES), f32)
            _steps(L, step, (zero, zero))
            pr, pi = lbr_ref[:, ls], -lbi_ref[:, ls]
            for _ in range(nsq):
                pr, pi = pr * pr - pi * pi, 2.0 * pr * pi
            er = jnp.zeros((1, SCAN_LANES), f32)
            ei = er
            ere[NC - 1:NC, ls] = er
            eim[NC - 1:NC, ls] = ei
            for c in range(NC - 2, -1, -1):
                lr_ = gre[c + 1:c + 2, ls]
                li_ = gim[c + 1:c + 2, ls]
                er, ei = lr_ + pr * er - pi * ei, li_ + pr * ei + pi * er
                ere[c:c + 1, ls] = er
                eim[c:c + 1, ls] = ei
            e_r, e_i = ere[:, ls].reshape(NC // 8, 8, SCAN_LANES), eim[:, ls].reshape(NC // 8, 8, SCAN_LANES)
            ar8, ai8 = ar[0:8], ai[0:8]

            def fixed(j, pwr, pwi):
                gr = (gre[tile(j), ls].reshape(NC // 8, 8, SCAN_LANES) + (pwr * e_r - pwi * e_i)).reshape(NC, SCAN_LANES)
                gi = (gim[tile(j), ls].reshape(NC // 8, 8, SCAN_LANES) + (pwr * e_i + pwi * e_r)).reshape(NC, SCAN_LANES)
                gre[tile(j), ls] = gr
                gim[tile(j), ls] = gi
                return gr, gi

            def fix(i, carry):
                pwr, pwi, accr, acci = carry
                j = L - 1 - i
                gr, gi = fixed(j, pwr, pwi)
                xr, xi = sre[tile(j - 1), ls].astype(f32), sim[tile(j - 1), ls].astype(f32)
                return (pwr * ar8 + pwi * ai8, pwi * ar8 - pwr * ai8,
                        accr + gr * xr + gi * xi, acci + gi * xr - gr * xi)

            pwr, pwi, accr, acci = _steps(L - 1, fix, (ar8, -ai8, zero, zero))
            gr, gi = fixed(0, pwr, pwi)
            xr = jnp.where(row == 0, 0.0, pltpu.roll(sre[tile(L - 1), ls].astype(f32), 1, axis=0))
            xi = jnp.where(row == 0, 0.0, pltpu.roll(sim[tile(L - 1), ls].astype(f32), 1, axis=0))
            accr = accr + gr * xr + gi * xi
            acci = acci + gi * xr - gr * xi
            dlr_ref[:, ls] = jnp.sum(accr, axis=0, keepdims=True)
            dli_ref[:, ls] = jnp.sum(acci, axis=0, keepdims=True)

        dwb_ref[...] = jnp.zeros_like(dwb_ref)
        dwc_ref[...] = jnp.zeros_like(dwc_ref)
        dd_ref[...] = jnp.zeros_like(dd_ref)
        su_ref[...] = jnp.zeros_like(su_ref)

        def finish(i, carry):
            u32, dy32 = u_ref[rows(i), :], dy_ref[rows(i), :]
            ub, dyb = u32.astype(bf16), dy32.astype(bf16)
            gr, gi = gre[rows(i), :].astype(bf16), gim[rows(i), :].astype(bf16)
            du = _dot(gr, wbT_ref[0:SW, :]) + _dot(gi, wbT_ref[SW:, :]) + dy32 * d_ref[...]
            du_ref[rows(i), :] = du
            su_ref[...] += jnp.sum(du, axis=0, keepdims=True)
            dwb_ref[:, 0:SW] += _dot_tn(ub, gr)
            dwb_ref[:, SW:] += _dot_tn(ub, gi)
            dwc_ref[:, 0:SW] += _dot_tn(dyb, sre[rows(i), :])
            dwc_ref[:, SW:] += _dot_tn(dyb, sim[rows(i), :])
            dd_ref[...] += jnp.sum(dy32 * u32, axis=0, keepdims=True)
            return carry

        lax.fori_loop(0, T // RB, finish, 0)

    slab = pl.BlockSpec((T, LANE), lambda k: (0, k))
    wide = pl.BlockSpec((None, LANE, 2 * SW), lambda k: (k, 0, 0))
    tall = pl.BlockSpec((None, 2 * SW, LANE), lambda k: (k, 0, 0))
    vec = pl.BlockSpec((None, 1, SW), lambda k: (k, 0, 0))
    vecd = pl.BlockSpec((None, 1, LANE), lambda k: (k, 0, 0))
    states = pl.BlockSpec((T, SW), lambda k: (0, k))
    nslab = W // LANE
    return _call(
        body, [u_p, dy_p, xr, xi, wbT, wcT, lbr, lbi, dsk], name="ssm_bwd", grid=(nslab,),
        in_specs=[slab, slab, states, states, tall, wide, vec, vec, vecd],
        out_specs=[slab, wide, wide, vec, vec, vecd, vecd],
        out_shape=[S((T, W), f32), S((nslab, LANE, 2 * SW), f32), S((nslab, LANE, 2 * SW), f32),
                   S((nslab, 1, SW), f32), S((nslab, 1, SW), f32), S((nslab, 1, LANE), f32), S((nslab, 1, LANE), f32)],
        scratch=[pltpu.VMEM((T, SW), f32)] * 2 + [pltpu.VMEM((NC, SW), f32)] * 2, vmem=VMEM_LIMIT, plan=plan)


def glu_fwd(yn, glu_w, glu_b):
    T = yn.shape[0]
    tm = min(512, T)

    def body(y_ref, w_ref, b_ref, o_ref):
        g = _gelu(y_ref[...])
        o_ref[...] = (g * _sigmoid(_dot(g.astype(bf16), w_ref[...]) + b_ref[...])).astype(bf16)

    return pl.pallas_call(
        body, name="glu_fwd", grid=(T // tm,),
        in_specs=[pl.BlockSpec((tm, W), lambda i: (i, 0)), pl.BlockSpec((W, W), lambda i: (0, 0)), pl.BlockSpec((1, W), lambda i: (0, 0))],
        out_specs=pl.BlockSpec((tm, W), lambda i: (i, 0)), out_shape=S((T, W), bf16), compiler_params=_cp(("parallel",)),
    )(yn, glu_w, glu_b)


def _shift_rows(cur, prev8, k):
    return pltpu.roll(jnp.concatenate([prev8, cur], axis=0), k, axis=0)[8:]


def _lift_rows(cur, next8, k):
    n = cur.shape[0]
    return pltpu.roll(jnp.concatenate([cur, next8], axis=0), n + 8 - k, axis=0)[:n]


def conv_fwd(proj, conv_w):
    T = proj.shape[0]
    RB = min(512, T)

    def body(h_ref, c_ref, b_ref, w_ref, o_ref):
        w0, w1, w2 = w_ref[0:1, :], w_ref[1:2, :], w_ref[2:3, :]

        def blk(i, carry):
            r0 = pl.multiple_of(i * RB, RB)
            rs = pl.ds(r0, RB)
            ch = c_ref[rs, :] * h_ref[rs, :]
            pr = pl.ds(jnp.maximum(r0 - 8, 0), 8)
            prev = jnp.where(i > 0, c_ref[pr, :] * h_ref[pr, :], 0.0)
            z = w2 * ch + w1 * _shift_rows(ch, prev, 1) + w0 * _shift_rows(ch, prev, 2)
            o_ref[rs, :] = (b_ref[rs, :] * z).astype(bf16)
            return carry

        lax.fori_loop(0, T // RB, blk, 0)

    nb = W // LANE
    return pl.pallas_call(
        body, name="conv_fwd", grid=(nb,),
        in_specs=[pl.BlockSpec((T, LANE), lambda k: (0, 4 * nb + k)), pl.BlockSpec((T, LANE), lambda k: (0, 5 * nb + k)),
                  pl.BlockSpec((T, LANE), lambda k: (0, 6 * nb + k)),pl.BlockSpec((3, LANE), lambda k: (0, k))],
        out_specs=pl.BlockSpec((T, LANE), lambda k: (0, k)), out_shape=S((T, W), bf16),
        compiler_params=_cp(("parallel",), VMEM_LIMIT),
    )(proj, proj, proj, conv_w)


def _dense_columns(blocks_ref, dense_ref):
    for k in range(NDEV):
        dense_ref[:, k * LANE:(k + 1) * LANE] = blocks_ref[k]


def merge_fwd(ya, yb, wso, wco, proj, plan):
    T = ya.shape[0]
    tm = min(1024, T)

    def body(ya_ref, yb_ref, wa_ref, wb_ref, ga_ref, gb_ref, o_ref, wa_s, wb_s):
        @pl.when(pl.program_id(0) == 0)
        def _():
            _dense_columns(wa_ref, wa_s)
            _dense_columns(wb_ref, wb_s)

        o_ref[...] = (_sigmoid(ga_ref[...]) * _dot(ya_ref[...], wa_s[...])
                      + _sigmoid(gb_ref[...]) * _dot(yb_ref[...], wb_s[...])).astype(bf16)

    act = pl.BlockSpec((tm, W), lambda i: (i, 0))
    return _call(
        body, [ya, yb, wso, wco, proj, proj], name="merge_fwd", grid=(T // tm,),
        in_specs=[act, act, _resident((NDEV, W, LANE)), _resident((NDEV, W, LANE)),
                  pl.BlockSpec((tm, D), lambda i: (i, 0)), pl.BlockSpec((tm, D), lambda i: (i, 1))],
        out_specs=[pl.BlockSpec((tm, D), lambda i: (i, 0))], out_shape=[S((T, D), bf16)],
        scratch=[pltpu.VMEM((W, D), bf16), pltpu.VMEM((W, D), bf16)], vmem=VMEM_LIMIT, plan=plan)


def mix_ln1(merged, w_o, x, g1, b1, plan):
    T = x.shape[0]
    tm = min(512, T)

    def body(m_ref, w_ref, x_ref, g_ref, b_ref, r_ref, x1_ref):
        for rs in _row_parts(tm):
            r = ALPHA * x_ref[rs, :] + _dot(m_ref[rs, :], w_ref[...])
            r_ref[rs, :] = r
            xhat, _ = _ln_stats(r)
            x1_ref[rs, :] = (xhat * g_ref[...] + b_ref[...]).astype(bf16)

    row = pl.BlockSpec((tm, D), lambda i: (i, 0))
    vec = pl.BlockSpec((1, D), lambda i: (0, 0))
    return _call(
        body, [merged, w_o, x, g1, b1], name="mix_ln1", grid=(T // tm,),
        in_specs=[row, _resident((D, D)), row, vec, vec],
        out_specs=[row, row], out_shape=[S((T, D), f32), S((T, D), bf16)], sem=("parallel",), vmem=VMEM_LIMIT, plan=plan,
        relay_step=T // tm - 2)


FT = 256


def gate_up(x1b, wgT, wuT, plan):
    T = x1b.shape[0]
    tm = min(512, T)

    def body(x_ref, wg_ref, wu_ref, g_ref, u_ref, h_ref):
        x = x_ref[...]
        for n in range(F // FT):
            cs = slice(n * FT, (n + 1) * FT)
            g = _dot_nt(x, wg_ref[cs, :])
            u = _dot_nt(x, wu_ref[cs, :])
            g_ref[:, cs] = g.astype(bf16)
            u_ref[:, cs] = u.astype(bf16)
            h_ref[:, cs] = (g * _sigmoid(g) * u).astype(bf16)

    osp = pl.BlockSpec((tm, F), lambda i: (i, 0))
    return _call(
        body, [x1b, wgT, wuT], name="gate_up", grid=(T // tm,),
        in_specs=[pl.BlockSpec((tm, D), lambda i: (i, 0)), _resident((F, D)), _resident((F, D))],
        out_specs=[osp, osp, osp], out_shape=[S((T, F), bf16)] * 3, vmem=VMEM_LIMIT, plan=plan, relay_step=T // tm - 3)


def down_loss(hid, w_down, r1, g1, b1, g2, b2, target):
    T = hid.shape[0]
    tm = min(512, T)

    def body(h_ref, w_ref, r1_ref, g1_ref, b1_ref, g2_ref, b2_ref, t_ref, dr_ref, drb_ref, loss_ref, dg_ref, db_ref):
        @pl.when(pl.program_id(0) == 0)
        def _():
            loss_ref[...] = jnp.zeros_like(loss_ref)
            dg_ref[...] = jnp.zeros_like(dg_ref)
            db_ref[...] = jnp.zeros_like(db_ref)

        for rs in _row_parts(tm):
            xh1, _ = _ln_stats(r1_ref[rs, :])
            x1 = xh1 * g1_ref[...] + b1_ref[...]
            r2 = ALPHA * x1 + _dot(h_ref[rs, :], w_ref[...])
            xh2, rstd2 = _ln_stats(r2)
            err = xh2 * g2_ref[...] + b2_ref[...] - t_ref[rs, :]
            loss_ref[...] += jnp.sum(jnp.mean(err * err, axis=-1, keepdims=True), axis=0, keepdims=True)
            dy = err * (1.0 / D)
            dg_ref[...] += jnp.sum(dy * xh2, axis=0, keepdims=True)
            db_ref[...] += jnp.sum(dy, axis=0, keepdims=True)
            dr = _ln_bwd(dy, xh2, rstd2, g2_ref[...])
            dr_ref[rs, :] = dr
            drb_ref[rs, :] = dr.astype(bf16)

    row = pl.BlockSpec((tm, D), lambda i: (i, 0))
    vec = pl.BlockSpec((1, D), lambda i: (0, 0))
    return pl.pallas_call(
        body, name="down_loss", grid=(T // tm,),
        in_specs=[pl.BlockSpec((tm, F), lambda i: (i, 0)), _resident((F, D)), row, vec, vec, vec, vec, row],
        out_specs=[row, row, pl.BlockSpec((1, 1), lambda i: (0, 0)), vec, vec],
        out_shape=[S((T, D), f32), S((T, D), bf16), S((1, 1), f32), S((1, D), f32), S((1, D), f32)],
        compiler_params=_cp(("arbitrary",), VMEM_LIMIT),
    )(hid, w_down, r1, g1, b1, g2, b2, target)


def ffn_bwd_act(dffn, w_down, gate, up, plan):
    T = dffn.shape[0]
    tm = min(512, T)

    def body(d_ref, w_ref, g_ref, u_ref, dg_ref, du_ref):
        for n in range(F // FT):
            cs = slice(n * FT, (n + 1) * FT)
            for rs in _row_parts(tm):
                dh = _dot_nt(d_ref[rs, :], w_ref[cs, :])
                g, u = g_ref[rs, cs].astype(f32), u_ref[rs, cs].astype(f32)
                sg = _sigmoid(g)
                t = g * sg
                du_ref[rs, cs] = (dh * t).astype(bf16)
                dg_ref[rs, cs] = (dh * u * (sg + t - t * sg)).astype(bf16)

    osp = pl.BlockSpec((tm, F), lambda i: (i, 0))
    return _call(
        body, [dffn, w_down, gate, up], name="ffn_bwd_act", grid=(T // tm,),
        in_specs=[pl.BlockSpec((tm, D), lambda i: (i, 0)), _resident((F, D)), osp, osp],
        out_specs=[osp, osp], out_shape=[S((T, F), bf16)] * 2, sem=("parallel",), vmem=VMEM_LIMIT, plan=plan)


def ffn_bwd_x(dgate, dup, wgT, wuT, dr2, r1, g1, plan):
    T = dr2.shape[0]
    tm = min(512, T)

    def body(dg_ref, du_ref, wg_ref, wu_ref, dr2_ref, r1_ref, g1_ref, dr_ref, drb_ref, dgam_ref, dbet_ref):
        @pl.when(pl.program_id(0) == 0)
        def _():
            dgam_ref[...] = jnp.zeros_like(dgam_ref)
            dbet_ref[...] = jnp.zeros_like(dbet_ref)

        for rs in _row_parts(tm):
            dx1 = ALPHA * dr2_ref[rs, :] + _dot(dg_ref[rs, :], wg_ref[...]) + _dot(du_ref[rs, :], wu_ref[...])
            xh, rstd = _ln_stats(r1_ref[rs, :])
            dgam_ref[...] += jnp.sum(dx1 * xh, axis=0, keepdims=True)
            dbet_ref[...] += jnp.sum(dx1, axis=0, keepdims=True)
            dr = _ln_bwd(dx1, xh, rstd, g1_ref[...])
            dr_ref[rs, :] = dr
            drb_ref[rs, :] = dr.astype(bf16)

    row = pl.BlockSpec((tm, D), lambda i: (i, 0))
    wide = pl.BlockSpec((tm, F), lambda i: (i, 0))
    wsp = _resident((F, D))
    vec = pl.BlockSpec((1, D), lambda i: (0, 0))
    return _call(
        body, [dgate, dup, wgT, wuT, dr2, r1, g1], name="ffn_bwd_x", grid=(T // tm,),
        in_specs=[wide, wide, wsp, wsp, row, row, vec],
        out_specs=[row, row, vec, vec], out_shape=[S((T, D), f32), S((T, D), bf16), S((1, D), f32), S((1, D), f32)],
        vmem=VMEM_LIMIT, plan=plan)


def merge_bwd(dmix, w_o, ya, yb, wso, wco, proj, plan):
    T = dmix.shape[0]
    tm = min(512, T)

    def body(dm_ref, wo_ref, ya_ref, yb_ref, wa_ref, wb_ref, ga_ref, gb_ref, dya_ref, dyb_ref, dga_ref, dgb_ref, sa_ref, sb_ref,
             wa_s, wb_s):
        @pl.when(pl.program_id(0) == 0)
        def _():
            _dense_columns(wa_ref, wa_s)
            _dense_columns(wb_ref, wb_s)

        dmer = _dot_nt(dm_ref[...], wo_ref[...])
        sa, sb = _sigmoid(ga_ref[...]), _sigmoid(gb_ref[...])
        dya_ref[...] = (dmer * sa).astype(bf16)
        dyb_ref[...] = (dmer * sb).astype(bf16)
        dga = dmer * _dot(ya_ref[...], wa_s[...]) * sa * (1.0 - sa)
        dgb = dmer * _dot(yb_ref[...], wb_s[...]) * sb * (1.0 - sb)
        dga_ref[...] = dga.astype(bf16)
        dgb_ref[...] = dgb.astype(bf16)
        sa_ref[...] = jnp.sum(dga, axis=0, keepdims=True)
        sb_ref[...] = jnp.sum(dgb, axis=0, keepdims=True)

    act = pl.BlockSpec((tm, W), lambda i: (i, 0))
    osp = pl.BlockSpec((tm, D), lambda i: (i, 0))
    ssp = pl.BlockSpec((None, 1, D), lambda i: (i, 0, 0))
    return _call(
        body, [dmix, w_o, ya, yb, wso, wco, proj, proj], name="merge_bwd", grid=(T // tm,),
        in_specs=[osp, _resident((D, D)), act, act, _resident((NDEV, W, LANE)), _resident((NDEV, W, LANE)),
                  pl.BlockSpec((tm, D), lambda i: (i, 0)), pl.BlockSpec((tm, D), lambda i: (i, 1))],
        out_specs=[osp, osp, osp, osp, ssp, ssp],
        out_shape=[S((T, D), bf16)] * 4 + [S((T // tm, 1, D), f32)] * 2,
        scratch=[pltpu.VMEM((W, D), bf16), pltpu.VMEM((W, D), bf16)], vmem=VMEM_LIMIT, plan=plan)


def branches_bwd_x(dYA, dYB, wso, wco, plan):
    T = dYA.shape[0]
    tm = min(1024, T)

    def body(da_ref, db_ref, wa_ref, wb_ref, oa_ref, ob_ref, wa_s, wb_s):
        @pl.when(pl.program_id(0) == 0)
        def _():
            _dense_columns(wa_ref, wa_s)
            _dense_columns(wb_ref, wb_s)

        oa_ref[...] = _dot_nt(da_ref[...], wa_s[...])
        ob_ref[...] = _dot_nt(db_ref[...], wb_s[...])

    row = pl.BlockSpec((tm, D), lambda i: (i, 0))
    osp = pl.BlockSpec((tm, W), lambda i: (i, 0))
    return _call(
        body, [dYA, dYB, wso, wco], name="branches_bwd_x", grid=(T // tm,),
        in_specs=[row, row, _resident((NDEV, W, LANE)), _resident((NDEV, W, LANE))],
        out_specs=[osp, osp], out_shape=[S((T, W), f32)] * 2,
        scratch=[pltpu.VMEM((W, D), bf16), pltpu.VMEM((W, D), bf16)], vmem=VMEM_LIMIT, plan=plan)


def branch_bwd_w(act, dY, name):
    T = act.shape[0]
    tk = W // 2

    def body(a_ref, d_ref, o_ref):
        res = _dot_tn(a_ref[...], d_ref[...])
        for k in range(NDEV):
            o_ref[k] = res[:, k * LANE:(k + 1) * LANE].astype(o_ref.dtype)

    return pl.pallas_call(
        body, name=name, grid=(W // tk,),
        in_specs=[pl.BlockSpec((T, tk), lambda i: (0, i)), _resident((T, D))],
        out_specs=pl.BlockSpec((NDEV, tk, LANE), lambda i: (0, i, 0)), out_shape=S((NDEV, W, LANE), GRAD_DT),
        compiler_params=_cp(("parallel",), VMEM_LIMIT),
    )(act, dY)


def glu_bwd(yn, dya, glu_w, glu_b, plan):
    T = yn.shape[0]
    tm = min(512, T)

    def body(y_ref, d_ref, w_ref, b_ref, dy_ref, dsp_ref, g_ref, db_ref):
        @pl.when(pl.program_id(0) == 0)
        def _():
            db_ref[...] = jnp.zeros_like(db_ref)

        y, dya_ = y_ref[...], d_ref[...]
        g = _gelu(y)
        gb = g.astype(bf16)
        s = _sigmoid(_dot(gb, w_ref[...]) + b_ref[...])
        dsp = dya_ * g * s * (1.0 - s)
        dspb = dsp.astype(bf16)
        dg = dya_ * s + _dot_nt(dspb, w_ref[...])
        dy_ref[...] = dg * _gelu_grad(y)
        dsp_ref[...] = dspb
        g_ref[...] = gb
        db_ref[...] += jnp.sum(dsp, axis=0, keepdims=True)

    row = pl.BlockSpec((tm, W), lambda i: (i, 0))
    vec = pl.BlockSpec((1, W), lambda i: (0, 0))
    return _call(
        body, [yn, dya, glu_w, glu_b], name="glu_bwd", grid=(T // tm,),
        in_specs=[row, row, pl.BlockSpec((W, W), lambda i: (0, 0)), vec],
        out_specs=[row, row, row, vec], out_shape=[S((T, W), f32), S((T, W), bf16), S((T, W), bf16), S((1, W), f32)],
        sem=("arbitrary",), plan=plan)


def conv_bwd(proj, dyb, conv_w, plan):
    T = proj.shape[0]
    RB = min(512, T)
    nrb = T // RB

    def body(h_ref, c_ref, b_ref, d_ref, w_ref, dh_ref, dc_ref, db_ref, dw_ref, s_ref):
        w0, w1, w2 = w_ref[0:1, :], w_ref[1:2, :], w_ref[2:3, :]

        def blk(i, carry):
            a0, a1, a2, sh, sc, sb = carry
            r0 = pl.multiple_of(i * RB, RB)
            rs = pl.ds(r0, RB)
            h, cg, bg, dyb_ = h_ref[rs, :], c_ref[rs, :], b_ref[rs, :], d_ref[rs, :]
            ch = cg * h
            pr = pl.ds(jnp.maximum(r0 - 8, 0), 8)
            prev = jnp.where(i > 0, c_ref[pr, :] * h_ref[pr, :], 0.0)
            ch1, ch2 = _shift_rows(ch, prev, 1), _shift_rows(ch, prev, 2)
            dbg = dyb_ * (w2 * ch + w1 * ch1 + w0 * ch2)
            db_ref[rs, :] = dbg.astype(bf16)
            dz = dyb_ * bg
            nx = pl.ds(jnp.minimum(r0 + RB, T - 8), 8)
            nxt = jnp.where(i < nrb - 1, d_ref[nx, :] * b_ref[nx, :], 0.0)
            dch = w2 * dz + w1 * _lift_rows(dz, nxt, 1) + w0 * _lift_rows(dz, nxt, 2)
            dcg, dh = dch * h, dch * cg
            dc_ref[rs, :] = dcg.astype(bf16)
            dh_ref[rs, :] = dh.astype(bf16)
            col = lambda v: jnp.sum(v, axis=0, keepdims=True)
            return (a0 + col(dz * ch2), a1 + col(dz * ch1), a2 + col(dz * ch), sh + col(dh), sc + col(dcg), sb + col(dbg))

        zero = jnp.zeros((1, LANE), f32)
        a0, a1, a2, sh, sc, sb = lax.fori_loop(0, nrb, blk, (zero,) * 6)
        dw_ref[0:1, :] = a0
        dw_ref[1:2, :] = a1
        dw_ref[2:3, :] = a2
        s_ref[0:1, :] = sh
        s_ref[1:2, :] = sc
        s_ref[2:3, :] = sb

    nb = W // LANE
    slab = pl.BlockSpec((T, LANE), lambda k: (0, k))
    three = pl.BlockSpec((3, LANE), lambda k: (0, k))
    return _call(
        body, [proj, proj, proj, dyb, conv_w], name="conv_bwd", grid=(nb,),
        in_specs=[pl.BlockSpec((T, LANE), lambda k: (0, 4 * nb + k)), pl.BlockSpec((T, LANE), lambda k: (0, 5 * nb + k)),
                  pl.BlockSpec((T, LANE), lambda k: (0, 6 * nb + k)), slab, three],
        out_specs=[slab, slab, slab, three, three],
        out_shape=[S((T, W), bf16)] * 3 + [S((3, W), f32)] * 2, sem=("parallel",), vmem=VMEM_LIMIT, plan=plan)


def in_proj_bwd_x(parts, win_g, base, scale, name, plan=None):
    T = base.shape[0]
    tm = min(512, T)
    n = len(parts)

    def body(*refs):
        p_refs, w_ref, b_ref, o_ref = refs[:n], refs[n], refs[n + 1], refs[n + 2]
        acc = scale * b_ref[...]
        for p_ref, (_, _, k) in zip(p_refs, parts):
            acc += _dot_nt(p_ref[...], w_ref[k])
        o_ref[...] = acc

    row = pl.BlockSpec((tm, D), lambda i: (i, 0))
    p_specs = [pl.BlockSpec((tm, W), (lambda i, cb=cb: (i, cb))) for _, cb, _ in parts]
    return _call(
        body, [a for a, _, _ in parts] + [win_g, base], name=name, grid=(T // tm,),
        in_specs=p_specs + [_resident((NDEV, D, W)), row],
        out_specs=[row], out_shape=[S((T, D), f32)], vmem=VMEM_LIMIT, plan=plan)


def ssm_param_bwd(lam_re, lam_im, log_dt, fr, fi, br, bi, dwb, dwcT, dlbr, dlbi):
    def body(lr_ref, li_ref, ldt_ref, fr_ref, fi_ref, br_ref, bi_ref, dwb_ref, dwc_ref, dlbr_ref, dlbi_ref,
             dbr_ref, dbi_ref, dlr_ref, dli_ref, dldt_ref, dcr_ref, dci_ref, dr_s, di_s):
        for k in range(W // LANE):
            for gl in range(NG // (W // LANE)):
                rows, src = slice((8 * k + gl) * GC, (8 * k + gl + 1) * GC), slice(gl * GC, (gl + 1) * GC)
                re, im = slice(gl * NP, (gl + 1) * NP), slice(SW + gl * NP, SW + (gl + 1) * NP)
                dr_s[rows, :] = dwb_ref[k, src, re]
                di_s[rows, :] = dwb_ref[k, src, im]
                dcr_ref[rows, :] = dwc_ref[k, src, re]
                dci_ref[rows, :] = -dwc_ref[k, src, im]
        fr_, fi_ = _per_channel(fr_ref[...]), _per_channel(fi_ref[...])
        br_, bi_, dr, di = br_ref[...], bi_ref[...], dr_s[...], di_s[...]
        dbr_ref[...] = fr_ * dr + fi_ * di
        dbi_ref[...] = fr_ * di - fi_ * dr
        dfr = jnp.sum((dr * br_ + di * bi_).reshape(NG, GC, NP), axis=1)
        dfi = jnp.sum((di * br_ - dr * bi_).reshape(NG, GC, NP), axis=1)
        _, vjp = jax.vjp(_disc, lr_ref[...], li_ref[...], ldt_ref[...])
        dlr_ref[...], dli_ref[...], dldt = vjp((dlbr_ref[...], dlbi_ref[...], dfr, dfi))
        dldt_ref[...] = _transpose_exact(dldt)

    blk = S((NG * GC, NP), f32)
    return pl.pallas_call(
        body, name="ssm_param_bwd", out_shape=[blk, blk, S((NG, NP), f32), S((NG, NP), f32), S((1, NG), f32), blk, blk],
        scratch_shapes=[pltpu.VMEM((NG * GC, NP), f32)] * 2)(
        lam_re, lam_im, log_dt, fr, fi, br, bi, dwb, dwcT, dlbr, dlbi)


def _adam(w, g, m, v):
    m = ADAM_B1 * m + (1.0 - ADAM_B1) * g
    v = ADAM_B2 * v + (1.0 - ADAM_B2) * (g * g)
    m_hat = m / (1.0 - ADAM_B1 ** ADAM_STEP)
    v_hat = v / (1.0 - ADAM_B2 ** ADAM_STEP)
    return -ADAM_LR * (m_hat / (jnp.sqrt(v_hat) + ADAM_EPS) + ADAM_WD * w), m, v


def adam_update(w, m, v, contrib, name, rows_per_block=None):
    R, C = w.shape
    n = contrib.shape[0]
    tr = min(rows_per_block or R, R)

    def body(w_ref, m_ref, v_ref, c_ref, g_ref, d_ref, nm_ref, nv_ref):
        g = c_ref[0].astype(f32)
        for k in range(1, n):
            g = g + c_ref[k].astype(f32)
        g_ref[...] = g
        d_ref[...], nm_ref[...], nv_ref[...] = _adam(w_ref[...], g, m_ref[...], v_ref[...])

    blk = pl.BlockSpec((tr, C), lambda i: (i, 0))
    return pl.pallas_call(
        body, name=name, grid=(R // tr,), in_specs=[blk, blk, blk, pl.BlockSpec((n, tr, C), lambda i: (0, i, 0))],
        out_specs=[blk] * 4, out_shape=[S((R, C), f32)] * 4, compiler_params=_cp(("parallel",), VMEM_LIMIT),
    )(w, m, v, contrib)


_ROWVEC = (("b_in", IN_COLS), ("ssm_d", W), ("glu_b", W), ("ln1_g", D), ("ln1_b", D), ("ln2_g", D), ("ln2_b", D))
_HALF = NG * GC // 2
_BC_LANE = {"ssm_b_re": 0, "ssm_b_im": NP, "ssm_c_re": 0, "ssm_c_im": NP}
_PACK = {}
_r = 0
for _n, _k in _ROWVEC:
    _PACK[_n] = _r
    _r += _k // LANE
for _n, _rows in (("ssm_lambda", NG), ("scalars", 8), ("ssm_b", _HALF), ("ssm_c", _HALF), ("conv_w", 16)):
    _PACK[_n] = _r
    _r += _rows
for _n in _BC_LANE:
    _PACK[_n] = _PACK[_n[:5]]
PACK_ROWS = _r
assert PACK_ROWS % 8 == 0
_SMALL = ("b_in", "ssm_lambda_re", "ssm_lambda_im", "ssm_log_dt", "ssm_b_re", "ssm_b_im", "ssm_c_re", "ssm_c_im",
          "ssm_d", "glu_b", "ln1_g", "ln1_b", "ln2_g", "ln2_b")


def pack_grads(su, shcb, sga, sgb, dd, dglu_b, dln1_g, dln1_b, dln2_g, dln2_b, dlam_re, dlam_im, dldt, sqerr, dbr, dbi,
               dc_re, dc_im, dconv):
    nI = sga.shape[0]

    def body(su_ref, sh_ref, sga_ref, sgb_ref, dd_ref, gb_ref, l1g_ref, l1b_ref, l2g_ref, l2b_ref, lr_ref, li_ref, dt_ref,
             sq_ref, br_ref, bi_ref, cr_ref, ci_ref, cw_ref, o_ref):
        o_ref[...] = jnp.zeros_like(o_ref)

        def put_row(name, v):
            r0 = _PACK[name]
            for i in range(v.shape[1] // LANE):
                o_ref[r0 + i:r0 + i + 1, :] = v[:, i * LANE:(i + 1) * LANE]

        ga, gb = sga_ref[0], sgb_ref[0]
        for i in range(1, nI):
            ga, gb = ga + sga_ref[i], gb + sgb_ref[i]
        put_row("b_in", jnp.concatenate([su_ref[k] for k in range(W // LANE)]
                                        + [sh_ref[0:1, :], sh_ref[1:2, :], sh_ref[2:3, :], ga, gb], axis=1))
        put_row("ssm_d", jnp.concatenate([dd_ref[k] for k in range(W // LANE)], axis=1))
        put_row("glu_b", gb_ref[...])
        put_row("ln1_g", l1g_ref[...])
        put_row("ln1_b", l1b_ref[...])
        put_row("ln2_g", l2g_ref[...])
        put_row("ln2_b", l2b_ref[...])
        r0 = _PACK["ssm_lambda"]
        o_ref[r0:r0 + NG, 0:NP] = lr_ref[...]
        o_ref[r0:r0 + NG, NP:2 * NP] = li_ref[...]
        r0 = _PACK["scalars"]
        o_ref[r0:r0 + 1, 0:NG] = dt_ref[...]
        o_ref[r0 + 1:r0 + 2, 0:1] = sq_ref[...]
        for name, ref in (("ssm_b_re", br_ref), ("ssm_b_im", bi_ref), ("ssm_c_re", cr_ref), ("ssm_c_im", ci_ref)):
            r0, l0 = _PACK[name], _BC_LANE[name]
            o_ref[r0:r0 + _HALF, l0:l0 + NP] = pltpu.bitcast(ref[...].astype(bf16), f32)
        for cb in range(W // LANE):
            o_ref[_PACK["conv_w"] + 3 * cb:_PACK["conv_w"] + 3 * cb + 3, :] = cw_ref[:, cb * LANE:(cb + 1) * LANE]

    return pl.pallas_call(body, name="pack_grads", out_shape=S((PACK_ROWS, LANE), f32))(
        su, shcb, sga, sgb, dd, dglu_b, dln1_g, dln1_b, dln2_g, dln2_b, dlam_re, dlam_im, dldt, sqerr, dbr, dbi, dc_re, dc_im,
        dconv)


def adam_small(packed_all, params):
    names = list(_SMALL) + ["conv_w"]
    flat = [a for n in names for a in params[n]]

    def body(*refs):
        p_ref = refs[0]
        ins = refs[1:1 + 3 * len(names)]
        outs = refs[1 + 3 * len(names):-2]
        loss_ref, g_ref = refs[-2], refs[-1]

        def part(k, rs=slice(None), ls=slice(None)):
            return p_ref[k, rs, ls]

        g_all = part(0)
        for k in range(1, NDEV):
            g_all = g_all + part(k)
        g_ref[...] = g_all

        def rows(name, r0, n, l0=0, lanes=LANE):
            return g_ref[_PACK[name] + r0:_PACK[name] + r0 + n, l0:l0 + lanes]

        def grad_of(name):
            if name in dict(_ROWVEC):
                return jnp.concatenate([rows(name, i, 1) for i in range(dict(_ROWVEC)[name] // LANE)], axis=1)
            if name in ("ssm_lambda_re", "ssm_lambda_im"):
                return rows("ssm_lambda", 0, NG, NP * (name == "ssm_lambda_im"), NP)[None]
            if name == "ssm_log_dt":
                return rows("scalars", 0, 1, 0, NG)
            if name in _BC_LANE:
                rs, ls = slice(_PACK[name], _PACK[name] + _HALF), slice(_BC_LANE[name], _BC_LANE[name] + NP)
                g = pltpu.bitcast(part(0, rs, ls), bf16).astype(f32)
                for k in range(1, NDEV):
                    g = g + pltpu.bitcast(part(k, rs, ls), bf16).astype(f32)
                return g.reshape(1, NG, GC, NP)
            full = jnp.concatenate([rows("conv_w", 3 * cb, 3) for cb in range(W // LANE)], axis=1)
            x, y, c = _coords()
            col0 = (4 * x + 2 * y + c) * (W // NDEV)
            sel = (lax.broadcasted_iota(jnp.int32, (W, W // NDEV), 0)
                   == lax.broadcasted_iota(jnp.int32, (W, W // NDEV), 1) + col0).astype(f32)
            return jnp.dot(full, sel, precision=HIGHEST, preferred_element_type=f32)[None]

        loss_ref[...] = 0.5 * rows("scalars", 1, 1, 0, 1)
        for i, name in enumerate(names):
            w_ref, m_ref, v_ref = ins[3 * i:3 * i + 3]
            g = grad_of(name)
            d, m, v = _adam(w_ref[...], g, m_ref[...], v_ref[...])
            outs[4 * i][...] = g
            outs[4 * i + 1][...] = d
            outs[4 * i + 2][...] = m
            outs[4 * i + 3][...] = v

    out_shape = [S(params[n][0].shape, f32) for n in names for _ in range(4)] + [S((1, 1), f32)]
    res = pl.pallas_call(body, name="adam_small", out_shape=out_shape, scratch_shapes=[pltpu.VMEM((PACK_ROWS, LANE), f32)],
                         compiler_params=_cp(None, VMEM_LIMIT))(packed_all, *flat)
    return {n: res[4 * i:4 * i + 4] for i, n in enumerate(names)}, res[-1]


def _block_diag(wgt):
    eye = jnp.eye(8, dtype=wgt.dtype)
    out = wgt[:, :, :, None, :] * eye[None, :, None, :, None]
    return out.reshape(4, 8 * wgt.shape[2], 8 * wgt.shape[3])


def kernel(x, w_in, b_in, ssm_lambda_re, ssm_lambda_im, ssm_log_dt, ssm_b_re, ssm_b_im, ssm_c_re, ssm_c_im, ssm_d, glu_w, glu_b, w_ssm_out, conv_w, w_conv_out, w_o, ln1_g, ln1_b, w_gate, w_up, w_down, ln2_g, ln2_b, loss_target, m_w_in, m_b_in, m_ssm_lambda_re, m_ssm_lambda_im, m_ssm_log_dt, m_ssm_b_re, m_ssm_b_im, m_ssm_c_re, m_ssm_c_im, m_ssm_d, m_glu_w, m_glu_b, m_w_ssm_out, m_conv_w, m_w_conv_out, m_w_o, m_ln1_g, m_ln1_b, m_w_gate, m_w_up, m_w_down, m_ln2_g, m_ln2_b, v_w_in, v_b_in, v_ssm_lambda_re, v_ssm_lambda_im, v_ssm_log_dt, v_ssm_b_re, v_ssm_b_im, v_ssm_c_re, v_ssm_c_im, v_ssm_d, v_glu_w, v_glu_b, v_w_ssm_out, v_conv_w, v_w_conv_out, v_w_o, v_ln1_g, v_ln1_b, v_w_gate, v_w_up, v_w_down, v_ln2_g, v_ln2_b):
    given = dict(locals())
    xs = x[0]
    target = loss_target[0]

    tr = lambda a: jnp.swapaxes(a[0], 0, 1)
    win_s, glu_s, wso_s, wco_s, wo_s, wgT_s, wuT_s, wd_s = prep_weights(
        [w_in[0], glu_w[0], w_ssm_out[0], w_conv_out[0], w_o[0], tr(w_gate), tr(w_up), w_down[0]])
    (win_g,) = run_plan(GatherPlan([win_s], srcs=(0,)), "gather_w_in_u")

    lam_re, lam_im = ssm_lambda_re[0], ssm_lambda_im[0]
    ldt = ssm_log_dt[0].reshape(NG, 1)
    br2 = jnp.swapaxes(ssm_b_re[0], 1, 2).reshape(NG * GC, NP)
    bi2 = jnp.swapaxes(ssm_b_im[0], 1, 2).reshape(NG * GC, NP)
    lbr, lbi, fr, fi, bbr, bbi = ssm_params(lam_re, lam_im, ldt, br2, bi2)
    bb_t = lambda b: b.reshape(4, 8, GC, NP)
    wb = jnp.concatenate([_block_diag(bb_t(bbr)), _block_diag(bb_t(bbi))], axis=2)
    c_t = lambda c: c.reshape(4, 8, GC, NP).transpose(0, 1, 3, 2)
    wc = jnp.concatenate([_block_diag(c_t(ssm_c_re[0])), -_block_diag(c_t(ssm_c_im[0]))], axis=1)
    wbT, wcT = wb.transpose(0, 2, 1), wc.transpose(0, 2, 1)
    wb, wc, wbT, wcT = wb.astype(bf16), wc.astype(bf16), wbT.astype(bf16), wcT.astype(bf16)
    lbr_s, lbi_s = lbr.reshape(4, 1, SW), lbi.reshape(4, 1, SW)
    dsk = ssm_d[0].reshape(4, 1, LANE)

    u_nat, xb = in_proj_u(xs, win_g, b_in)
    u_p = to_perm(u_nat, 0, "perm_u")
    half_a, half_b = (0, 3, 5, 6), (1, 2, 4, 7)
    (y_p, xr_p, xi_p), (win_g, conv_g, glu_g, wso_g) = ssm_fwd(
        u_p, wb, wc, lbr_s, lbi_s, dsk,
        Plans([GatherPlan([win_s], srcs=tuple(range(1, NDEV)), into=[win_g]), GatherPlan([conv_w[0], glu_s, wso_s])]))
    conv_f = conv_g.transpose(1, 0, 2).reshape(3, W)
    (proj,), (wco_g, wo_g, wgT_g) = in_proj_rest(
        xb, win_g, b_in, Plans([GatherPlan([wco_s, wo_s]), GatherPlan([wgT_s], srcs=half_a)]))
    glu_f, wo_f = glu_g.reshape(W, W), wo_g.reshape(D, D)
    (yn,), _ = from_perm(y_p, "unperm_y")
    ya = glu_fwd(yn, glu_f, glu_b)
    yb = conv_fwd(proj, conv_f)
    (merged,), (wgT_g, wuT_g) = merge_fwd(
        ya, yb, wso_g, wco_g, proj,
        Plans([GatherPlan([wgT_s], srcs=half_b, into=[wgT_g]), GatherPlan([wuT_s], srcs=half_a)]))
    (r1, x1b), (wuT_g,) = mix_ln1(merged, wo_f, xs, ln1_g, ln1_b, GatherPlan([wuT_s], srcs=half_b, into=[wuT_g]))
    wgT, wuT = wgT_g.reshape(F, D), wuT_g.reshape(F, D)
    (gate, up, hid), (wd_g,) = gate_up(x1b, wgT, wuT, GatherPlan([wd_s]))
    wd_f = wd_g.reshape(F, D)
    dr2, dffn, sqerr, dln2_g, dln2_b = down_loss(hid, wd_f, r1, ln1_g, ln1_b, ln2_g, ln2_b, target)

    dwd, _ = mm_tn_rows(hid, dffn, "grad_w_down")
    dwd = dwd.reshape(NDEV, FS, D)
    (dgate, dup), (r_wd,) = ffn_bwd_act(dffn, wd_f, gate, up, ScatterPlan([dwd], only=half_a))
    dwgT, (r_wd,) = mm_tn_rows(dgate, x1b, "grad_w_gate", plan=ScatterPlan([dwd], only=half_b, into=[r_wd]))
    dwgT = dwgT.reshape(NDEV, FS, D)
    dwuT, (r_wgT,) = mm_tn_rows(dup, x1b, "grad_w_up", plan=ScatterPlan([dwgT], only=half_a))
    dwuT = dwuT.reshape(NDEV, FS, D)
    (dr1, dmix, dln1_g, dln1_b), (r_wgT,) = ffn_bwd_x(dgate, dup, wgT, wuT, dr2, r1, ln1_g,
                                                     ScatterPlan([dwgT], only=half_b, into=[r_wgT]))
    (dYA, dYB, dga, dgb, sga, sgb), (r_wuT,) = merge_bwd(dmix, wo_f, ya, yb, wso_g, wco_g, proj,
                                                         ScatterPlan([dwuT], only=half_a))
    dwo, _ = mm_tn_rows(merged, dmix, "grad_w_o")
    dwo = dwo.reshape(NDEV, D // NDEV, D)
    (dya, dyb), (r_wuT,) = branches_bwd_x(dYA, dYB, wso_g, wco_g, ScatterPlan([dwuT], only=half_b, into=[r_wuT]))
    dwso = branch_bwd_w(ya, dYA, "grad_w_ssm_out")
    dwco = branch_bwd_w(yb, dYB, "grad_w_conv_out")
    (dyn, dsp, gb, dglu_b), (r_wso,) = glu_bwd(yn, dya, glu_f, glu_b, ScatterPlan([dwso]))
    dglu = mm_tn_rows(gb, dsp, "grad_glu_w")[0].reshape(NDEV, W // NDEV, W)
    (dh, dcg, dbg, dconv, shcb), (r_wco,) = conv_bwd(proj, dyb, conv_f, ScatterPlan([dwco]))
    dwin, (r_wo, r_glu) = grad_w_in_rest(xb, dh, dcg, dbg, dga, dgb, ScatterPlan([dwo, dglu]))
    dy_p = to_perm(dyn, 0, "perm_dy")
    (du_p, dwb, dwcT, dlbr_s, dlbi_s, dd, su), (r_win,) = ssm_bwd(
        u_p, dy_p, xr_p, xi_p, wbT, wcT, lbr_s, lbi_s, dsk, ScatterPlan([dwin], only=tuple(range(1, NDEV))))

    dbr2, dbi2, dlam_re, dlam_im, dldt, dc_re, dc_im = ssm_param_bwd(
        lam_re, lam_im, ldt, fr, fi, br2, bi2, dwb, dwcT, dlbr_s.reshape(NG, NP), dlbi_s.reshape(NG, NP))
    packed = pack_grads(su, shcb, sga, sgb, dd, dglu_b, dln1_g, dln1_b, dln2_g, dln2_b, dlam_re, dlam_im, dldt, sqerr,
                        dbr2, dbi2, dc_re, dc_im, dconv)
    (du,), _ = from_perm(du_p, "unperm_du", bf16)
    dwin = mm_tn(xb, du, "grad_w_in_u", block0=0, into=dwin)

    rest = [(dh, 0, 1), (dcg, 0, 2), (dbg, 0, 3), (dga, 0, 4), (dga, 1, 5), (dgb, 0, 6), (dgb, 1, 7)]
    (gx_rest,), (r_win, small_all) = in_proj_bwd_x(
        rest, win_g, dr1, ALPHA, "in_proj_bwd_x_rest",
        Plans([ScatterPlan([dwin], only=(0,), into=[r_win]), GatherPlan([packed])]))
    (grad_x,), _ = in_proj_bwd_x([(du, 0, 0)], win_g, gx_rest, 1.0, "in_proj_bwd_x_u")

    out = {}

    def put(name, res, back=lambda a: a[None]):
        out["grad_" + name], out["delta_" + name], out["new_m_" + name], out["new_v_" + name] = [back(r) for r in res]

    put("w_in", adam_update(w_in[0], m_w_in[0], v_w_in[0], r_win, "adam_w_in", 256))
    put("glu_w", adam_update(glu_w[0], m_glu_w[0], v_glu_w[0], r_glu, "adam_glu_w"))
    put("w_ssm_out", adam_update(w_ssm_out[0], m_w_ssm_out[0], v_w_ssm_out[0], r_wso, "adam_w_ssm_out"))
    put("w_conv_out", adam_update(w_conv_out[0], m_w_conv_out[0], v_w_conv_out[0], r_wco, "adam_w_conv_out"))
    put("w_o", adam_update(w_o[0], m_w_o[0], v_w_o[0], r_wo, "adam_w_o"))
    put("w_down", adam_update(w_down[0], m_w_down[0], v_w_down[0], r_wd, "adam_w_down", 176))
    untr = lambda a: jnp.swapaxes(a, 0, 1)[None]
    put("w_gate", adam_update(tr(w_gate), tr(m_w_gate), tr(v_w_gate), r_wgT, "adam_w_gate", 176), untr)
    put("w_up", adam_update(tr(w_up), tr(m_w_up), tr(v_w_up), r_wuT, "adam_w_up", 176), untr)
    as_c = lambda a: jnp.swapaxes(a, 2, 3)
    params = {n: (given[n], given["m_" + n], given["v_" + n]) for n in list(_SMALL) + ["conv_w"]}
    for n in ("ssm_b_re", "ssm_b_im"):
        params[n] = tuple(as_c(a) for a in params[n])
    small, loss = adam_small(small_all, params)
    for n, res in small.items():
        put(n, res, as_c if n in ("ssm_b_re", "ssm_b_im") else (lambda a: a))

    names = ["w_in", "b_in", "ssm_lambda_re", "ssm_lambda_im", "ssm_log_dt", "ssm_b_re", "ssm_b_im", "ssm_c_re", "ssm_c_im",
             "ssm_d", "glu_w", "glu_b", "w_ssm_out", "conv_w", "w_conv_out", "w_o", "ln1_g", "ln1_b", "w_gate", "w_up",
             "w_down", "ln2_g", "ln2_b"]
    return (loss.reshape(()), grad_x[None], *[out[p + n] for p in ("grad_", "delta_", "new_m_", "new_v_") for n in names])
```

```python
import functools
import math

import jax
import jax.numpy as jnp
from jax import lax
from jax.experimental import pallas as pl
from jax.experimental.pallas import tpu as pltpu

f32, bf16 = jnp.float32, jnp.bfloat16
S = jax.ShapeDtypeStruct
MESH = pl.DeviceIdType.MESH
HIGHEST = lax.Precision.HIGHEST

D = 1024
W = 512
NG, NP, GC = 32, 64, 16
F = 2816
NDEV = 8
FS = F // NDEV
IN_COLS = 8 * W
ALPHA = 2.0 ** 0.25
LN_EPS = 1e-5
ADAM_LR, ADAM_B1, ADAM_B2, ADAM_EPS, ADAM_WD, ADAM_STEP = 0.001, 0.9, 0.999, 1e-08, 0.01, 10
NC = 32
LANE = 128
SW = 4 * LANE
VMEM_LIMIT = 56 * 1024 * 1024
GRAD_DT = bf16
ANY = pl.BlockSpec(memory_space=pl.ANY)


def _cp(sem=None, vmem=None):
    return pltpu.CompilerParams(dimension_semantics=sem, vmem_limit_bytes=vmem)


def _resident(shape):
    return pl.BlockSpec(shape, lambda i: (0,) * len(shape), pipeline_mode=pl.Buffered(1))


def _dot(a, b):
    return jnp.dot(a, b, preferred_element_type=f32)


def _dot_nt(a, b):
    return lax.dot_general(a, b, (((1,), (1,)), ((), ())), preferred_element_type=f32)


def _dot_tn(a, b):
    return lax.dot_general(a, b, (((0,), (0,)), ((), ())), preferred_element_type=f32)


def _eye(n):
    return (lax.broadcasted_iota(jnp.int32, (n, n), 0) == lax.broadcasted_iota(jnp.int32, (n, n), 1)).astype(f32)


def _transpose_exact(a):
    return lax.dot_general(a, _eye(a.shape[0]), (((0,), (0,)), ((), ())), precision=HIGHEST, preferred_element_type=f32)


def _sigmoid(x):
    return 1.0 / (1.0 + jnp.exp(-x))


_GK = math.sqrt(2.0 / math.pi)


def _gelu(x):
    return 0.5 * x * (1.0 + jnp.tanh(_GK * (x + 0.044715 * x * x * x)))


def _gelu_grad(x):
    th = jnp.tanh(_GK * (x + 0.044715 * x * x * x))
    return 0.5 * (1.0 + th) + 0.5 * x * (1.0 - th * th) * _GK * (1.0 + 3.0 * 0.044715 * x * x)


ROW_PART = 256


def _row_parts(tm):
    return [slice(r, r + min(ROW_PART, tm)) for r in range(0, tm, min(ROW_PART, tm))]


def _ln_stats(r):
    mu = jnp.mean(r, axis=-1, keepdims=True)
    xc = r - mu
    var = jnp.mean(xc * xc, axis=-1, keepdims=True)
    rstd = lax.rsqrt(var + LN_EPS)
    return xc * rstd, rstd


def _ln_bwd(dy, xhat, rstd, g):
    dxh = dy * g
    m1 = jnp.mean(dxh, axis=-1, keepdims=True)
    m2 = jnp.mean(dxh * xhat, axis=-1, keepdims=True)
    return rstd * (dxh - m1 - xhat * m2)


def _coords():
    return lax.axis_index("x"), lax.axis_index("y"), lax.axis_index("c")


def _when(cond, fn):
    if cond is True:
        fn()
    else:
        pl.when(cond)(fn)


class GatherPlan:
    aliases = ()

    def __init__(self, arrs, srcs=None, into=None):
        n = self.n = len(arrs)
        self.srcs = srcs
        self.inputs = list(arrs) + list(into or [])
        if into:
            self.aliases = tuple((n + a, a) for a in range(n))
        self.out_shape = [S((NDEV,) + a.shape, a.dtype) for a in arrs]
        self.sems = [pltpu.SemaphoreType.DMA((n, 7)), pltpu.SemaphoreType.DMA((n, 7)), pltpu.SemaphoreType.DMA((n,))]

    def _has(self, dev):
        if self.srcs is None:
            return True
        idx = 4 * dev[0] + 2 * dev[1] + dev[2]
        return functools.reduce(jnp.logical_or, [idx == s for s in self.srcs])

    def _parts(self, ins, outs, sems):
        n = self.n
        send_sems, recv_sems, loc_sems = sems
        x, y, c = _coords()
        me, sib = (x, y, c), (x, y, 1 - c)
        chips = [(1 - x, y), (x, 1 - y), (1 - x, 1 - y)]

        def slot(a, dev):
            return outs[a].at[4 * dev[0] + 2 * dev[1] + dev[2]]

        def copy(a, k, block, to, src=None):
            return pltpu.make_async_remote_copy(
                src_ref=slot(a, block) if src is None else src, dst_ref=slot(a, block),
                send_sem=send_sems.at[a, k], recv_sem=recv_sems.at[a, k], device_id=to, device_id_type=MESH)

        each = [(j, chip, a) for j, chip in enumerate(chips) for a in range(n)]
        own = self._has(me)
        return dict(
            mine=lambda: [(pltpu.make_async_copy(ins[a], slot(a, me), loc_sems.at[a]), own) for a in range(n)],
            first=lambda: ([(copy(a, 0, me, sib, src=ins[a]), own) for a in range(n)]
                           + [(copy(a, 1 + j, me, (*chip, c), src=ins[a]), own) for j, chip, a in each]),
            landed=lambda: [(copy(a, 1 + j, (*chip, c), me), self._has((*chip, c))) for j, chip, a in each],
            passed=lambda: [(copy(a, 4 + j, (*chip, c), sib), self._has((*chip, c))) for j, chip, a in each],
            from_sib=lambda: ([(copy(a, 0, sib, me), self._has(sib)) for a in range(n)]
                              + [(copy(a, 4 + j, (*chip, 1 - c), me), self._has((*chip, 1 - c))) for j, chip, a in each]))

    def start(self, ins, outs, sems):
        p = self._parts(ins, outs, sems)
        for cp, cond in p["mine"]() + p["first"]():
            _when(cond, cp.start)

    def forward(self, ins, outs, sems):
        p = self._parts(ins, outs, sems)
        for (got, cond), (fwd, _) in zip(p["landed"](), p["passed"]()):
            def relay(got=got, fwd=fwd):
                got.wait_recv()
                fwd.start()

            _when(cond, relay)

    def finish(self, ins, outs, sems):
        p = self._parts(ins, outs, sems)
        for cp, cond in p["from_sib"]():
            _when(cond, cp.wait_recv)
        for cp, cond in p["first"]() + p["passed"]():
            _when(cond, cp.wait_send)
        for cp, cond in p["mine"]():
            _when(cond, cp.wait)


class ScatterPlan:
    aliases = ()

    def __init__(self, gs, only=None, into=None, whole=False):
        n = self.n = len(gs)
        self.only = only
        self.whole = whole
        self.inputs = list(gs) + list(into or [])
        if into:
            self.aliases = tuple((n + a, a) for a in range(n))
        self.out_shape = [S((NDEV,) + g.shape if whole else g.shape, g.dtype) for g in gs]
        self.sems = [pltpu.SemaphoreType.DMA((n, 7)), pltpu.SemaphoreType.DMA((n, 7)), pltpu.SemaphoreType.DMA((n,))]

    def _owner(self, idx):
        if self.only is None:
            return True
        return functools.reduce(jnp.logical_or, [idx == b for b in self.only])

    def _copies(self, ins, outs, sems):
        n = self.n
        send_sems, recv_sems, loc_sems = sems
        x, y, c = _coords()
        me = 4 * x + 2 * y + c
        mine = self._owner(me)
        block = (lambda a, k: ins[a]) if self.whole else (lambda a, k: ins[a].at[k])
        copies = [(pltpu.make_async_copy(block(a, me), outs[a].at[me], loc_sems.at[a]), mine, None) for a in range(n)]
        for m in range(1, NDEV):
            px = 1 - x if m & 4 else x
            py = 1 - y if m & 2 else y
            pc = 1 - c if m & 1 else c
            peer = 4 * px + 2 * py + pc
            for a in range(n):
                copies.append((pltpu.make_async_remote_copy(
                    src_ref=block(a, peer), dst_ref=outs[a].at[me],
                    send_sem=send_sems.at[a, m - 1], recv_sem=recv_sems.at[a, m - 1],
                    device_id=(px, py, pc), device_id_type=MESH), self._owner(peer), mine))
        return copies

    def start(self, ins, outs, sems):
        for cp, sends, _ in self._copies(ins, outs, sems):
            _when(sends, cp.start)

    def forward(self, ins, outs, sems):
        pass

    def finish(self, ins, outs, sems):
        for cp, sends, receives in self._copies(ins, outs, sems):
            if receives is None:
                _when(sends, cp.wait)
            else:
                _when(sends, cp.wait_send)
                _when(receives, cp.wait_recv)


class Plans:
    def __init__(self, plans):
        self.plans = plans
        self.inputs = [a for p in plans for a in p.inputs]
        self.out_shape = [s for p in plans for s in p.out_shape]
        self.sems = [s for p in plans for s in p.sems]
        self.aliases, i, o = [], 0, 0
        for p in plans:
            self.aliases += [(i + a, o + b) for a, b in p.aliases]
            i, o = i + len(p.inputs), o + len(p.out_shape)

    def _each(self, what, ins, outs, sems):
        i = o = s = 0
        for p in self.plans:
            ni, no, ns = len(p.inputs), len(p.out_shape), len(p.sems)
            getattr(p, what)(ins[i:i + ni], outs[o:o + no], sems[s:s + ns])
            i, o, s = i + ni, o + no, s + ns

    def start(self, ins, outs, sems):
        self._each("start", ins, outs, sems)

    def forward(self, ins, outs, sems):
        self._each("forward", ins, outs, sems)

    def finish(self, ins, outs, sems):
        self._each("finish", ins, outs, sems)


def _call(body, args, *, name, grid, in_specs, out_specs, out_shape, scratch=(), sem=None, vmem=None, plan=None,
          aliases=None, relay_step=None):
    aliases = aliases or {}
    if plan is None:
        outs = pl.pallas_call(body, name=name, grid=grid, in_specs=list(in_specs), out_specs=list(out_specs),
                              out_shape=list(out_shape), scratch_shapes=list(scratch), input_output_aliases=aliases,
                              compiler_params=_cp(sem, vmem))(*args)
        return list(outs), []
    ni, no, ns = len(in_specs), len(out_specs), len(scratch)
    pi, po = len(plan.inputs), len(plan.out_shape)
    aliases = {**aliases, **{ni + a: no + b for a, b in plan.aliases}}

    def wrapped(*refs):
        main_in, p_in = refs[:ni], refs[ni:ni + pi]
        main_out, p_out = refs[ni + pi:ni + pi + no], refs[ni + pi + no:ni + pi + no + po]
        main_scr, p_sems = refs[ni + pi + no + po:ni + pi + no + po + ns], refs[ni + pi + no + po + ns:]
        ids = [pl.program_id(d) for d in range(len(grid))]
        first = functools.reduce(jnp.logical_and, [i == 0 for i in ids])
        last = functools.reduce(jnp.logical_and, [i == g - 1 for i, g in zip(ids, grid)])

        @pl.when(first)
        def _():
            plan.start(p_in, p_out, p_sems)

        @pl.when(last if relay_step is None else ids[0] == max(relay_step, 0))
        def _():
            plan.forward(p_in, p_out, p_sems)

        body(*main_in, *main_out, *main_scr)

        @pl.when(last)
        def _():
            plan.finish(p_in, p_out, p_sems)

    outs = pl.pallas_call(
        wrapped, name=name, grid=grid, in_specs=list(in_specs) + [ANY] * pi, out_specs=list(out_specs) + [ANY] * po,
        out_shape=list(out_shape) + list(plan.out_shape), scratch_shapes=list(scratch) + list(plan.sems),
        input_output_aliases=aliases, compiler_params=_cp(("arbitrary",) * len(grid), vmem),
    )(*args, *plan.inputs)
    return list(outs[:no]), list(outs[no:])


def run_plan(plan, name):
    def body(*refs):
        ins, outs, sems = refs[:len(plan.inputs)], refs[len(plan.inputs):len(plan.inputs) + len(plan.out_shape)], \
            refs[len(plan.inputs) + len(plan.out_shape):]
        plan.start(ins, outs, sems)
        plan.forward(ins, outs, sems)
        plan.finish(ins, outs, sems)

    return pl.pallas_call(body, name=name, in_specs=[ANY] * len(plan.inputs), out_specs=[ANY] * len(plan.out_shape),
                          out_shape=list(plan.out_shape), scratch_shapes=list(plan.sems))(*plan.inputs)


def mm_tn(a, b, name, tn=512, into=None, block0=0, nblocks=None):
    T, K = a.shape
    N = b.shape[1]
    tn = min(tn, N)
    nblocks = nblocks or (N // tn if into is None else into.shape[0])

    def body(a_ref, b_ref, *rest):
        rest[-1][...] = _dot_tn(a_ref[...], b_ref[...]).astype(GRAD_DT)

    args, in_specs, aliases = [a, b], [_resident((T, K)), pl.BlockSpec((T, tn), lambda j: (0, j))], {}
    if into is not None:
        args.append(into)
        in_specs.append(ANY)
        aliases = {2: 0}
    (out,), _ = _call(body, args, name=name, grid=(N // tn,), in_specs=in_specs,
                      out_specs=[pl.BlockSpec((None, K, tn), lambda j: (block0 + j, 0, 0))],
                      out_shape=[S((nblocks, K, tn), GRAD_DT)], sem=("parallel",), vmem=VMEM_LIMIT, aliases=aliases)
    return out


def grad_w_in_rest(xb, dh, dcg, dbg, dga, dgb, plan):
    T = xb.shape[0]
    order = ((0, 0), (1, 1), (2, 2), (3, 3), (4, 3), (5, 4), (6, 4))

    def body(x_ref, *refs):
        o_ref = refs[-1]
        j = pl.program_id(0)
        for step, opnd in order:
            @pl.when(j == step)
            def _(opnd=opnd):
                o_ref[...] = _dot_tn(x_ref[...], refs[opnd][...]).astype(GRAD_DT)

    once = lambda: pl.BlockSpec((T, W), lambda j: (0, 0), pipeline_mode=pl.Buffered(1))
    (out,), sent = _call(
        body, [xb, dh, dcg, dbg, dga, dgb], name="grad_w_in_rest", grid=(len(order),),
        in_specs=[_resident((T, D)), once(), once(), once(),
                  pl.BlockSpec((T, W), lambda j: (0, jnp.clip(j - 3, 0, 1))),
                  pl.BlockSpec((T, W), lambda j: (0, jnp.clip(j - 5, 0, 1)))],
        out_specs=[pl.BlockSpec((None, D, W), lambda j: (1 + j, 0, 0))],
        out_shape=[S((NDEV, D, W), GRAD_DT)], sem=("arbitrary",), vmem=VMEM_LIMIT, plan=plan)
    return out, sent


def mm_tn_rows(a, b, name, tk=256, plan=None):
    T, K = a.shape
    N = b.shape[1]
    tk = min(tk, K)

    def body(a_ref, b_ref, o_ref):
        o_ref[...] = _dot_tn(a_ref[...], b_ref[...]).astype(GRAD_DT)

    (out,), sent = _call(body, [a, b], name=name, grid=(K // tk,),
                         in_specs=[pl.BlockSpec((T, tk), lambda i: (0, i)), _resident((T, N))],
                         out_specs=[pl.BlockSpec((tk, N), lambda i: (i, 0))], out_shape=[S((K, N), GRAD_DT)],
                         sem=("parallel",), vmem=VMEM_LIMIT, plan=plan)
    return out, sent


def prep_weights(ws):
    def body(*refs):
        for i in range(len(ws)):
            refs[len(ws) + i][...] = refs[i][...].astype(bf16)

    return pl.pallas_call(body, name="prep_weights", out_shape=[S(w.shape, bf16) for w in ws],
                          compiler_params=_cp(None, VMEM_LIMIT))(*ws)


REST_BLOCKS = (4, 5, 6, 7, 1, 2, 3)
REST_COLS = len(REST_BLOCKS) * W


def in_proj_u(x, win_g, b_in):
    T = x.shape[0]
    tm = min(1024, T)

    def body(x_ref, w_ref, b_ref, u_ref, xb_ref):
        xb = x_ref[...].astype(bf16)
        xb_ref[...] = xb
        u_ref[...] = _dot(xb, w_ref[...]) + b_ref[...]

    row = pl.BlockSpec((tm, D), lambda i: (i, 0))
    return pl.pallas_call(
        body, name="in_proj_u", grid=(T // tm,),
        in_specs=[row, pl.BlockSpec((None, D, W), lambda i: (0, 0, 0)), pl.BlockSpec((1, W), lambda i: (0, 0))],
        out_specs=[pl.BlockSpec((tm, W), lambda i: (i, 0)), row],
        out_shape=[S((T, W), f32), S((T, D), bf16)], compiler_params=_cp(("parallel",), VMEM_LIMIT),
    )(x, win_g, b_in)


def in_proj_rest(xb, win_g, b_in, plan):
    T = xb.shape[0]
    tm = min(512, T)

    def body(x_ref, w_ref, b_ref, o_ref):
        xb_ = x_ref[...]
        for i, k in enumerate(REST_BLOCKS):
            o_ref[:, i * W:(i + 1) * W] = _dot(xb_, w_ref[k]) + b_ref[:, k * W:(k + 1) * W]

    return _call(
        body, [xb, win_g, b_in], name="in_proj_rest", grid=(T // tm,),
        in_specs=[pl.BlockSpec((tm, D), lambda i: (i, 0)), _resident((NDEV, D, W)), _resident((1, IN_COLS))],
        out_specs=[pl.BlockSpec((tm, REST_COLS), lambda i: (i, 0))],
        out_shape=[S((T, REST_COLS), f32)], vmem=VMEM_LIMIT, plan=plan, relay_step=T // tm - 2)


def to_perm(a, cb0, name):
    T = a.shape[0]
    L = T // NC

    def body(a_ref, o_ref):
        def step(jb, carry):
            j0 = pl.multiple_of(jb * 8, 8)
            for q in range(NC // 8):
                x = jnp.stack([a_ref[pl.ds((8 * q + c) * L + j0, 8), :] for c in range(8)], axis=0)
                y = jnp.swapaxes(x, 0, 1)
                for j in range(8):
                    o_ref[pl.ds((j0 + j) * NC + 8 * q, 8), :] = y[j]
            return carry

        lax.fori_loop(0, L // 8, step, 0)

    return pl.pallas_call(
        body, name=name, grid=(W // LANE,),
        in_specs=[pl.BlockSpec((T, LANE), lambda k: (0, cb0 + k))], out_specs=pl.BlockSpec((T, LANE), lambda k: (0, k)),
        out_shape=S((T, W), f32), compiler_params=_cp(("parallel",), VMEM_LIMIT),
    )(a)


def from_perm(a, name, out_dtype=f32, plan=None):
    T = a.shape[0]
    L = T // NC

    def body(a_ref, o_ref):
        def step(jb, carry):
            j0 = pl.multiple_of(jb * 16, 16)
            for q in range(NC // 8):
                halves = []
                for h in range(2):
                    x = jnp.stack([a_ref[pl.ds((j0 + 8 * h + j) * NC + 8 * q, 8), :] for j in range(8)], axis=0)
                    halves.append(jnp.swapaxes(x, 0, 1))
                for c in range(8):
                    o_ref[pl.ds((8 * q + c) * L + j0, 16), :] = jnp.concatenate(
                        [halves[0][c], halves[1][c]], axis=0).astype(out_dtype)
            return carry

        lax.fori_loop(0, L // 16, step, 0)

    slab = pl.BlockSpec((T, LANE), lambda k: (0, k))
    return _call(body, [a], name=name, grid=(W // LANE,), in_specs=[slab], out_specs=[slab],
                 out_shape=[S((T, W), out_dtype)], sem=("parallel",), vmem=VMEM_LIMIT, plan=plan)


def _disc(lr, li, ldt):
    dt = jnp.exp(ldt)
    mag = jnp.exp(lr * dt)
    lbr = mag * jnp.cos(li * dt)
    lbi = mag * jnp.sin(li * dt)
    den = lr * lr + li * li
    nr = lbr - 1.0
    return lbr, lbi, (nr * lr + lbi * li) / den, (lbi * lr - nr * li) / den


def _per_channel(f):
    return jnp.broadcast_to(f[:, None, :], (NG, GC, NP)).reshape(NG * GC, NP)


def ssm_params(lam_re, lam_im, log_dt, br, bi):
    def body(lr_ref, li_ref, ldt_ref, br_ref, bi_ref, lbr_ref, lbi_ref, fr_ref, fi_ref, bbr_ref, bbi_ref):
        lbr, lbi, fr, fi = _disc(lr_ref[...], li_ref[...], ldt_ref[...])
        lbr_ref[...], lbi_ref[...], fr_ref[...], fi_ref[...] = lbr, lbi, fr, fi
        fr_, fi_, br_, bi_ = _per_channel(fr), _per_channel(fi), br_ref[...], bi_ref[...]
        bbr_ref[...] = fr_ * br_ - fi_ * bi_
        bbi_ref[...] = fr_ * bi_ + fi_ * br_

    return pl.pallas_call(body, name="ssm_params", out_shape=[S((NG, NP), f32)] * 4 + [S((NG * GC, NP), f32)] * 2)(
        lam_re, lam_im, log_dt, br, bi)


SCAN_UNROLL = 4
SCAN_LANES = 2 * LANE


def _steps(n, body, carry):
    main = n // SCAN_UNROLL

    def trip(t, c):
        for q in range(SCAN_UNROLL):
            c = body(t * SCAN_UNROLL + q, c)
        return c

    carry = lax.fori_loop(0, main, trip, carry)
    for i in range(main * SCAN_UNROLL, n):
        carry = body(i, carry)
    return carry


def _scan_body(T):
    L = T // NC
    RB = min(512, T)
    nsq = int(round(math.log2(L)))
    assert 2 ** nsq == L and T % RB == 0 and L % 16 == 0

    def rows(i):
        return pl.ds(pl.multiple_of(i * RB, RB), RB)

    def tile(j):
        return pl.ds(j * NC if isinstance(j, int) else pl.multiple_of(j * NC, NC), NC)

    def forward_states(u_ref, wb_ref, lbr_ref, lbi_ref, sre, sim, ere, eim):
        def bproj(i, carry):
            bu = _dot(u_ref[rows(i), :].astype(bf16), wb_ref[...])
            sre[rows(i), :] = bu[:, :SW]
            sim[rows(i), :] = bu[:, SW:]
            return carry

        lax.fori_loop(0, T // RB, bproj, 0)
        for lb in range(SW // SCAN_LANES):
            ls = slice(lb * SCAN_LANES, (lb + 1) * SCAN_LANES)
            ar = jnp.broadcast_to(lbr_ref[:, ls], (NC, SCAN_LANES))
            ai = jnp.broadcast_to(lbi_ref[:, ls], (NC, SCAN_LANES))

            def step(j, carry):
                xr, xi = carry
                nr = ar * xr - ai * xi + sre[tile(j), ls]
                ni = ar * xi + ai * xr + sim[tile(j), ls]
                sre[tile(j), ls] = nr
                sim[tile(j), ls] = ni
                return nr, ni

            zero = jnp.zeros((NC, SCAN_LANES), f32)
            _steps(L, step, (zero, zero))
            pr, pi = lbr_ref[:, ls], lbi_ref[:, ls]
            for _ in range(nsq):
                pr, pi = pr * pr - pi * pi, 2.0 * pr * pi
            er = jnp.zeros((1, SCAN_LANES), f32)
            ei = er
            ere[0:1, ls] = er
            eim[0:1, ls] = ei
            base = (L - 1) * NC
            for c in range(1, NC):
                lr_ = sre[base + c - 1:base + c, ls]
                li_ = sim[base + c - 1:base + c, ls]
                er, ei = lr_ + pr * er - pi * ei, li_ + pr * ei + pi * er
                ere[c:c + 1, ls] = er
                eim[c:c + 1, ls] = ei
            e_r, e_i = ere[:, ls].reshape(NC // 8, 8, SCAN_LANES), eim[:, ls].reshape(NC // 8, 8, SCAN_LANES)
            ar8, ai8 = ar[0:8], ai[0:8]

            def fix(j, carry):
                pwr, pwi = carry
                xr = sre[tile(j), ls].reshape(NC // 8, 8, SCAN_LANES) + (pwr * e_r - pwi * e_i)
                xi = sim[tile(j), ls].reshape(NC // 8, 8, SCAN_LANES) + (pwr * e_i + pwi * e_r)
                sre[tile(j), ls] = xr.reshape(NC, SCAN_LANES)
                sim[tile(j), ls] = xi.reshape(NC, SCAN_LANES)
                return pwr * ar8 - pwi * ai8, pwr * ai8 + pwi * ar8

            _steps(L, fix, (ar8, ai8))

    return L, RB, nsq, rows, tile, forward_states


def ssm_fwd(u_p, wb, wc, lbr, lbi, dsk, plan):
    T = u_p.shape[0]
    L, RB, nsq, rows, tile, forward_states = _scan_body(T)
    nslab = W // LANE

    def body(u_ref, wb_ref, wc_ref, lbr_ref, lbi_ref, d_ref, y_ref, xr_ref, xi_ref, sre, sim, ere, eim):
        forward_states(u_ref, wb_ref, lbr_ref, lbi_ref, sre, sim, ere, eim)

        def cproj(i, carry):
            xr, xi = sre[rows(i), :].astype(bf16), sim[rows(i), :].astype(bf16)
            xr_ref[rows(i), :] = xr
            xi_ref[rows(i), :] = xi
            y = _dot(xr, wc_ref[0:SW, :]) + _dot(xi, wc_ref[SW:, :])
            y_ref[rows(i), :] = y + d_ref[...] * u_ref[rows(i), :]
            return carry

        lax.fori_loop(0, T // RB, cproj, 0)

    slab = pl.BlockSpec((T, LANE), lambda k: (0, k))
    states = pl.BlockSpec((T, SW), lambda k: (0, k))
    return _call(
        body, [u_p, wb, wc, lbr, lbi, dsk], name="ssm_fwd", grid=(nslab,),
        in_specs=[slab, pl.BlockSpec((None, LANE, 2 * SW), lambda k: (k, 0, 0)),
                  pl.BlockSpec((None, 2 * SW, LANE), lambda k: (k, 0, 0)),
                  pl.BlockSpec((None, 1, SW), lambda k: (k, 0, 0)), pl.BlockSpec((None, 1, SW), lambda k: (k, 0, 0)),
                  pl.BlockSpec((None, 1, LANE), lambda k: (k, 0, 0))],
        out_specs=[slab, states, states], out_shape=[S((T, W), f32), S((T, nslab * SW), bf16), S((T, nslab * SW), bf16)],
        scratch=[pltpu.VMEM((T, SW), f32), pltpu.VMEM((T, SW), f32), pltpu.VMEM((NC, SW), f32), pltpu.VMEM((NC, SW), f32)],
        vmem=VMEM_LIMIT, plan=plan)


def ssm_bwd(u_p, dy_p, xr, xi, wbT, wcT, lbr, lbi, dsk, plan):
    T = u_p.shape[0]
    L, RB, nsq, rows, tile, _ = _scan_body(T)

    def body(u_ref, dy_ref, sre, sim, wbT_ref, wcT_ref, lbr_ref, lbi_ref, d_ref,
             du_ref, dwb_ref, dwc_ref, dlr_ref, dli_ref, dd_ref, su_ref, gre, gim, ere, eim):
        def dstate(i, carry):
            g = _dot(dy_ref[rows(i), :].astype(bf16), wcT_ref[...])
            gre[rows(i), :] = g[:, :SW]
            gim[rows(i), :] = g[:, SW:]
            return carry

        lax.fori_loop(0, T // RB, dstate, 0)
        row = lax.broadcasted_iota(jnp.int32, (NC, SCAN_LANES), 0)
        for lb in range(SW // SCAN_LANES):
            ls = slice(lb * SCAN_LANES, (lb + 1) * SCAN_LANES)
            ar = jnp.broadcast_to(lbr_ref[:, ls], (NC, SCAN_LANES))
            ai = jnp.broadcast_to(lbi_ref[:, ls], (NC, SCAN_LANES))

            def step(i, carry):
                gr, gi = carry
                j = L - 1 - i
                nr = ar * gr + ai * gi + gre[tile(j), ls]
                ni = ar * gi - ai * gr + gim[tile(j), ls]
                gre[tile(j), ls] = nr
                gim[tile(j), ls] = ni
                return nr, ni

            zero = jnp.zeros((NC, SCAN_LANES), f32)
            _steps(L, step, (zero, zero))
            pr, pi = lbr_ref[:, ls], -lbi_ref[:, ls]
            for _ in range(nsq):
                pr, pi = pr * pr - pi * pi, 2.0 * pr * pi
            er = jnp.zeros((1, SCAN_LANES), f32)
            ei = er
            ere[NC - 1:NC, ls] = er
            eim[NC - 1:NC, ls] = ei
            for c in range(NC - 2, -1, -1):
                lr_ = gre[c + 1:c + 2, ls]
                li_ = gim[c + 1:c + 2, ls]
                er, ei = lr_ + pr * er - pi * ei, li_ + pr * ei + pi * er
                ere[c:c + 1, ls] = er
                eim[c:c + 1, ls] = ei
            e_r, e_i = ere[:, ls].reshape(NC // 8, 8, SCAN_LANES), eim[:, ls].reshape(NC // 8, 8, SCAN_LANES)
            ar8, ai8 = ar[0:8], ai[0:8]

            def fixed(j, pwr, pwi):
                gr = (gre[tile(j), ls].reshape(NC // 8, 8, SCAN_LANES) + (pwr * e_r - pwi * e_i)).reshape(NC, SCAN_LANES)
                gi = (gim[tile(j), ls].reshape(NC // 8, 8, SCAN_LANES) + (pwr * e_i + pwi * e_r)).reshape(NC, SCAN_LANES)
                gre[tile(j), ls] = gr
                gim[tile(j), ls] = gi
                return gr, gi

            def fix(i, carry):
                pwr, pwi, accr, acci = carry
                j = L - 1 - i
                gr, gi = fixed(j, pwr, pwi)
                xr, xi = sre[tile(j - 1), ls].astype(f32), sim[tile(j - 1), ls].astype(f32)
                return (pwr * ar8 + pwi * ai8, pwi * ar8 - pwr * ai8,
                        accr + gr * xr + gi * xi, acci + gi * xr - gr * xi)

            pwr, pwi, accr, acci = _steps(L - 1, fix, (ar8, -ai8, zero, zero))
            gr, gi = fixed(0, pwr, pwi)
            xr = jnp.where(row == 0, 0.0, pltpu.roll(sre[tile(L - 1), ls].astype(f32), 1, axis=0))
            xi = jnp.where(row == 0, 0.0, pltpu.roll(sim[tile(L - 1), ls].astype(f32), 1, axis=0))
            accr = accr + gr * xr + gi * xi
            acci = acci + gi * xr - gr * xi
            dlr_ref[:, ls] = jnp.sum(accr, axis=0, keepdims=True)
            dli_ref[:, ls] = jnp.sum(acci, axis=0, keepdims=True)

        dwb_ref[...] = jnp.zeros_like(dwb_ref)
        dwc_ref[...] = jnp.zeros_like(dwc_ref)
        dd_ref[...] = jnp.zeros_like(dd_ref)
        su_ref[...] = jnp.zeros_like(su_ref)

        def finish(i, carry):
            u32, dy32 = u_ref[rows(i), :], dy_ref[rows(i), :]
            ub, dyb = u32.astype(bf16), dy32.astype(bf16)
            gr, gi = gre[rows(i), :].astype(bf16), gim[rows(i), :].astype(bf16)
            du = _dot(gr, wbT_ref[0:SW, :]) + _dot(gi, wbT_ref[SW:, :]) + dy32 * d_ref[...]
            du_ref[rows(i), :] = du
            su_ref[...] += jnp.sum(du, axis=0, keepdims=True)
            dwb_ref[:, 0:SW] += _dot_tn(ub, gr)
            dwb_ref[:, SW:] += _dot_tn(ub, gi)
            dwc_ref[:, 0:SW] += _dot_tn(dyb, sre[rows(i), :])
            dwc_ref[:, SW:] += _dot_tn(dyb, sim[rows(i), :])
            dd_ref[...] += jnp.sum(dy32 * u32, axis=0, keepdims=True)
            return carry

        lax.fori_loop(0, T // RB, finish, 0)

    slab = pl.BlockSpec((T, LANE), lambda k: (0, k))
    wide = pl.BlockSpec((None, LANE, 2 * SW), lambda k: (k, 0, 0))
    tall = pl.BlockSpec((None, 2 * SW, LANE), lambda k: (k, 0, 0))
    vec = pl.BlockSpec((None, 1, SW), lambda k: (k, 0, 0))
    vecd = pl.BlockSpec((None, 1, LANE), lambda k: (k, 0, 0))
    states = pl.BlockSpec((T, SW), lambda k: (0, k))
    nslab = W // LANE
    return _call(
        body, [u_p, dy_p, xr, xi, wbT, wcT, lbr, lbi, dsk], name="ssm_bwd", grid=(nslab,),
        in_specs=[slab, slab, states, states, tall, wide, vec, vec, vecd],
        out_specs=[slab, wide, wide, vec, vec, vecd, vecd],
        out_shape=[S((T, W), f32), S((nslab, LANE, 2 * SW), f32), S((nslab, LANE, 2 * SW), f32),
                   S((nslab, 1, SW), f32), S((nslab, 1, SW), f32), S((nslab, 1, LANE), f32), S((nslab, 1, LANE), f32)],
        scratch=[pltpu.VMEM((T, SW), f32)] * 2 + [pltpu.VMEM((NC, SW), f32)] * 2, vmem=VMEM_LIMIT, plan=plan)


def glu_fwd(yn, glu_w, glu_b):
    T = yn.shape[0]
    tm = min(512, T)

    def body(y_ref, w_ref, b_ref, o_ref):
        g = _gelu(y_ref[...])
        o_ref[...] = (g * _sigmoid(_dot(g.astype(bf16), w_ref[...]) + b_ref[...])).astype(bf16)

    return pl.pallas_call(
        body, name="glu_fwd", grid=(T // tm,),
        in_specs=[pl.BlockSpec((tm, W), lambda i: (i, 0)), pl.BlockSpec((W, W), lambda i: (0, 0)), pl.BlockSpec((1, W), lambda i: (0, 0))],
        out_specs=pl.BlockSpec((tm, W), lambda i: (i, 0)), out_shape=S((T, W), bf16), compiler_params=_cp(("parallel",)),
    )(yn, glu_w, glu_b)


def _shift_rows(cur, prev8, k):
    return pltpu.roll(jnp.concatenate([prev8, cur], axis=0), k, axis=0)[8:]


def _lift_rows(cur, next8, k):
    n = cur.shape[0]
    return pltpu.roll(jnp.concatenate([cur, next8], axis=0), n + 8 - k, axis=0)[:n]


def conv_fwd(proj, conv_w):
    T = proj.shape[0]
    RB = min(512, T)

    def body(h_ref, c_ref, b_ref, w_ref, o_ref):
        w0, w1, w2 = w_ref[0:1, :], w_ref[1:2, :], w_ref[2:3, :]

        def blk(i, carry):
            r0 = pl.multiple_of(i * RB, RB)
            rs = pl.ds(r0, RB)
            ch = c_ref[rs, :] * h_ref[rs, :]
            pr = pl.ds(jnp.maximum(r0 - 8, 0), 8)
            prev = jnp.where(i > 0, c_ref[pr, :] * h_ref[pr, :], 0.0)
            z = w2 * ch + w1 * _shift_rows(ch, prev, 1) + w0 * _shift_rows(ch, prev, 2)
            o_ref[rs, :] = (b_ref[rs, :] * z).astype(bf16)
            return carry

        lax.fori_loop(0, T // RB, blk, 0)

    nb = W // LANE
    return pl.pallas_call(
        body, name="conv_fwd", grid=(nb,),
        in_specs=[pl.BlockSpec((T, LANE), lambda k: (0, 4 * nb + k)), pl.BlockSpec((T, LANE), lambda k: (0, 5 * nb + k)),
                  pl.BlockSpec((T, LANE), lambda k: (0, 6 * nb + k)),pl.BlockSpec((3, LANE), lambda k: (0, k))],
        out_specs=pl.BlockSpec((T, LANE), lambda k: (0, k)), out_shape=S((T, W), bf16),
        compiler_params=_cp(("parallel",), VMEM_LIMIT),
    )(proj, proj, proj, conv_w)


def _dense_columns(blocks_ref, dense_ref):
    for k in range(NDEV):
        dense_ref[:, k * LANE:(k + 1) * LANE] = blocks_ref[k]


def merge_fwd(ya, yb, wso, wco, proj, plan):
    T = ya.shape[0]
    tm = min(1024, T)

    def body(ya_ref, yb_ref, wa_ref, wb_ref, ga_ref, gb_ref, o_ref, wa_s, wb_s):
        @pl.when(pl.program_id(0) == 0)
        def _():
            _dense_columns(wa_ref, wa_s)
            _dense_columns(wb_ref, wb_s)

        o_ref[...] = (_sigmoid(ga_ref[...]) * _dot(ya_ref[...], wa_s[...])
                      + _sigmoid(gb_ref[...]) * _dot(yb_ref[...], wb_s[...])).astype(bf16)

    act = pl.BlockSpec((tm, W), lambda i: (i, 0))
    return _call(
        body, [ya, yb, wso, wco, proj, proj], name="merge_fwd", grid=(T // tm,),
        in_specs=[act, act, _resident((NDEV, W, LANE)), _resident((NDEV, W, LANE)),
                  pl.BlockSpec((tm, D), lambda i: (i, 0)), pl.BlockSpec((tm, D), lambda i: (i, 1))],
        out_specs=[pl.BlockSpec((tm, D), lambda i: (i, 0))], out_shape=[S((T, D), bf16)],
        scratch=[pltpu.VMEM((W, D), bf16), pltpu.VMEM((W, D), bf16)], vmem=VMEM_LIMIT, plan=plan)


def mix_ln1(merged, w_o, x, g1, b1, plan):
    T = x.shape[0]
    tm = min(512, T)

    def body(m_ref, w_ref, x_ref, g_ref, b_ref, r_ref, x1_ref):
        for rs in _row_parts(tm):
            r = ALPHA * x_ref[rs, :] + _dot(m_ref[rs, :], w_ref[...])
            r_ref[rs, :] = r
            xhat, _ = _ln_stats(r)
            x1_ref[rs, :] = (xhat * g_ref[...] + b_ref[...]).astype(bf16)

    row = pl.BlockSpec((tm, D), lambda i: (i, 0))
    vec = pl.BlockSpec((1, D), lambda i: (0, 0))
    return _call(
        body, [merged, w_o, x, g1, b1], name="mix_ln1", grid=(T // tm,),
        in_specs=[row, _resident((D, D)), row, vec, vec],
        out_specs=[row, row], out_shape=[S((T, D), f32), S((T, D), bf16)], sem=("parallel",), vmem=VMEM_LIMIT, plan=plan,
        relay_step=T // tm - 2)


FT = 256


def gate_up(x1b, wgT, wuT, plan):
    T = x1b.shape[0]
    tm = min(512, T)

    def body(x_ref, wg_ref, wu_ref, g_ref, u_ref, h_ref):
        x = x_ref[...]
        for n in range(F // FT):
            cs = slice(n * FT, (n + 1) * FT)
            g = _dot_nt(x, wg_ref[cs, :])
            u = _dot_nt(x, wu_ref[cs, :])
            g_ref[:, cs] = g.astype(bf16)
            u_ref[:, cs] = u.astype(bf16)
            h_ref[:, cs] = (g * _sigmoid(g) * u).astype(bf16)

    osp = pl.BlockSpec((tm, F), lambda i: (i, 0))
    return _call(
        body, [x1b, wgT, wuT], name="gate_up", grid=(T // tm,),
        in_specs=[pl.BlockSpec((tm, D), lambda i: (i, 0)), _resident((F, D)), _resident((F, D))],
        out_specs=[osp, osp, osp], out_shape=[S((T, F), bf16)] * 3, vmem=VMEM_LIMIT, plan=plan, relay_step=T // tm - 3)


def down_loss(hid, w_down, r1, g1, b1, g2, b2, target):
    T = hid.shape[0]
    tm = min(512, T)

    def body(h_ref, w_ref, r1_ref, g1_ref, b1_ref, g2_ref, b2_ref, t_ref, dr_ref, drb_ref, loss_ref, dg_ref, db_ref):
        @pl.when(pl.program_id(0) == 0)
        def _():
            loss_ref[...] = jnp.zeros_like(loss_ref)
            dg_ref[...] = jnp.zeros_like(dg_ref)
            db_ref[...] = jnp.zeros_like(db_ref)

        for rs in _row_parts(tm):
            xh1, _ = _ln_stats(r1_ref[rs, :])
            x1 = xh1 * g1_ref[...] + b1_ref[...]
            r2 = ALPHA * x1 + _dot(h_ref[rs, :], w_ref[...])
            xh2, rstd2 = _ln_stats(r2)
            err = xh2 * g2_ref[...] + b2_ref[...] - t_ref[rs, :]
            loss_ref[...] += jnp.sum(jnp.mean(err * err, axis=-1, keepdims=True), axis=0, keepdims=True)
            dy = err * (1.0 / D)
            dg_ref[...] += jnp.sum(dy * xh2, axis=0, keepdims=True)
            db_ref[...] += jnp.sum(dy, axis=0, keepdims=True)
            dr = _ln_bwd(dy, xh2, rstd2, g2_ref[...])
            dr_ref[rs, :] = dr
            drb_ref[rs, :] = dr.astype(bf16)

    row = pl.BlockSpec((tm, D), lambda i: (i, 0))
    vec = pl.BlockSpec((1, D), lambda i: (0, 0))
    return pl.pallas_call(
        body, name="down_loss", grid=(T // tm,),
        in_specs=[pl.BlockSpec((tm, F), lambda i: (i, 0)), _resident((F, D)), row, vec, vec, vec, vec, row],
        out_specs=[row, row, pl.BlockSpec((1, 1), lambda i: (0, 0)), vec, vec],
        out_shape=[S((T, D), f32), S((T, D), bf16), S((1, 1), f32), S((1, D), f32), S((1, D), f32)],
        compiler_params=_cp(("arbitrary",), VMEM_LIMIT),
    )(hid, w_down, r1, g1, b1, g2, b2, target)


def ffn_bwd_act(dffn, w_down, gate, up, plan):
    T = dffn.shape[0]
    tm = min(512, T)

    def body(d_ref, w_ref, g_ref, u_ref, dg_ref, du_ref):
        for n in range(F // FT):
            cs = slice(n * FT, (n + 1) * FT)
            for rs in _row_parts(tm):
                dh = _dot_nt(d_ref[rs, :], w_ref[cs, :])
                g, u = g_ref[rs, cs].astype(f32), u_ref[rs, cs].astype(f32)
                sg = _sigmoid(g)
                t = g * sg
                du_ref[rs, cs] = (dh * t).astype(bf16)
                dg_ref[rs, cs] = (dh * u * (sg + t - t * sg)).astype(bf16)

    osp = pl.BlockSpec((tm, F), lambda i: (i, 0))
    return _call(
        body, [dffn, w_down, gate, up], name="ffn_bwd_act", grid=(T // tm,),
        in_specs=[pl.BlockSpec((tm, D), lambda i: (i, 0)), _resident((F, D)), osp, osp],
        out_specs=[osp, osp], out_shape=[S((T, F), bf16)] * 2, sem=("parallel",), vmem=VMEM_LIMIT, plan=plan)


def ffn_bwd_x(dgate, dup, wgT, wuT, dr2, r1, g1, plan):
    T = dr2.shape[0]
    tm = min(512, T)

    def body(dg_ref, du_ref, wg_ref, wu_ref, dr2_ref, r1_ref, g1_ref, dr_ref, drb_ref, dgam_ref, dbet_ref):
        @pl.when(pl.program_id(0) == 0)
        def _():
            dgam_ref[...] = jnp.zeros_like(dgam_ref)
            dbet_ref[...] = jnp.zeros_like(dbet_ref)

        for rs in _row_parts(tm):
            dx1 = ALPHA * dr2_ref[rs, :] + _dot(dg_ref[rs, :], wg_ref[...]) + _dot(du_ref[rs, :], wu_ref[...])
            xh, rstd = _ln_stats(r1_ref[rs, :])
            dgam_ref[...] += jnp.sum(dx1 * xh, axis=0, keepdims=True)
            dbet_ref[...] += jnp.sum(dx1, axis=0, keepdims=True)
            dr = _ln_bwd(dx1, xh, rstd, g1_ref[...])
            dr_ref[rs, :] = dr
            drb_ref[rs, :] = dr.astype(bf16)

    row = pl.BlockSpec((tm, D), lambda i: (i, 0))
    wide = pl.BlockSpec((tm, F), lambda i: (i, 0))
    wsp = _resident((F, D))
    vec = pl.BlockSpec((1, D), lambda i: (0, 0))
    return _call(
        body, [dgate, dup, wgT, wuT, dr2, r1, g1], name="ffn_bwd_x", grid=(T // tm,),
        in_specs=[wide, wide, wsp, wsp, row, row, vec],
        out_specs=[row, row, vec, vec], out_shape=[S((T, D), f32), S((T, D), bf16), S((1, D), f32), S((1, D), f32)],
        vmem=VMEM_LIMIT, plan=plan)


def merge_bwd(dmix, w_o, ya, yb, wso, wco, proj, plan):
    T = dmix.shape[0]
    tm = min(512, T)

    def body(dm_ref, wo_ref, ya_ref, yb_ref, wa_ref, wb_ref, ga_ref, gb_ref, dya_ref, dyb_ref, dga_ref, dgb_ref, sa_ref, sb_ref,
             wa_s, wb_s):
        @pl.when(pl.program_id(0) == 0)
        def _():
            _dense_columns(wa_ref, wa_s)
            _dense_columns(wb_ref, wb_s)

        dmer = _dot_nt(dm_ref[...], wo_ref[...])
        sa, sb = _sigmoid(ga_ref[...]), _sigmoid(gb_ref[...])
        dya_ref[...] = (dmer * sa).astype(bf16)
        dyb_ref[...] = (dmer * sb).astype(bf16)
        dga = dmer * _dot(ya_ref[...], wa_s[...]) * sa * (1.0 - sa)
        dgb = dmer * _dot(yb_ref[...], wb_s[...]) * sb * (1.0 - sb)
        dga_ref[...] = dga.astype(bf16)
        dgb_ref[...] = dgb.astype(bf16)
        sa_ref[...] = jnp.sum(dga, axis=0, keepdims=True)
        sb_ref[...] = jnp.sum(dgb, axis=0, keepdims=True)

    act = pl.BlockSpec((tm, W), lambda i: (i, 0))
    osp = pl.BlockSpec((tm, D), lambda i: (i, 0))
    ssp = pl.BlockSpec((None, 1, D), lambda i: (i, 0, 0))
    return _call(
        body, [dmix, w_o, ya, yb, wso, wco, proj, proj], name="merge_bwd", grid=(T // tm,),
        in_specs=[osp, _resident((D, D)), act, act, _resident((NDEV, W, LANE)), _resident((NDEV, W, LANE)),
                  pl.BlockSpec((tm, D), lambda i: (i, 0)), pl.BlockSpec((tm, D), lambda i: (i, 1))],
        out_specs=[osp, osp, osp, osp, ssp, ssp],
        out_shape=[S((T, D), bf16)] * 4 + [S((T // tm, 1, D), f32)] * 2,
        scratch=[pltpu.VMEM((W, D), bf16), pltpu.VMEM((W, D), bf16)], vmem=VMEM_LIMIT, plan=plan)


def branches_bwd_x(dYA, dYB, wso, wco, plan):
    T = dYA.shape[0]
    tm = min(1024, T)

    def body(da_ref, db_ref, wa_ref, wb_ref, oa_ref, ob_ref, wa_s, wb_s):
        @pl.when(pl.program_id(0) == 0)
        def _():
            _dense_columns(wa_ref, wa_s)
            _dense_columns(wb_ref, wb_s)

        oa_ref[...] = _dot_nt(da_ref[...], wa_s[...])
        ob_ref[...] = _dot_nt(db_ref[...], wb_s[...])

    row = pl.BlockSpec((tm, D), lambda i: (i, 0))
    osp = pl.BlockSpec((tm, W), lambda i: (i, 0))
    return _call(
        body, [dYA, dYB, wso, wco], name="branches_bwd_x", grid=(T // tm,),
        in_specs=[row, row, _resident((NDEV, W, LANE)), _resident((NDEV, W, LANE))],
        out_specs=[osp, osp], out_shape=[S((T, W), f32)] * 2,
        scratch=[pltpu.VMEM((W, D), bf16), pltpu.VMEM((W, D), bf16)], vmem=VMEM_LIMIT, plan=plan)


def branch_bwd_w(act, dY, name):
    T = act.shape[0]
    tk = W // 2

    def body(a_ref, d_ref, o_ref):
        res = _dot_tn(a_ref[...], d_ref[...])
        for k in range(NDEV):
            o_ref[k] = res[:, k * LANE:(k + 1) * LANE].astype(o_ref.dtype)

    return pl.pallas_call(
        body, name=name, grid=(W // tk,),
        in_specs=[pl.BlockSpec((T, tk), lambda i: (0, i)), _resident((T, D))],
        out_specs=pl.BlockSpec((NDEV, tk, LANE), lambda i: (0, i, 0)), out_shape=S((NDEV, W, LANE), GRAD_DT),
        compiler_params=_cp(("parallel",), VMEM_LIMIT),
    )(act, dY)


def glu_bwd(yn, dya, glu_w, glu_b, plan):
    T = yn.shape[0]
    tm = min(512, T)

    def body(y_ref, d_ref, w_ref, b_ref, dy_ref, dsp_ref, g_ref, db_ref):
        @pl.when(pl.program_id(0) == 0)
        def _():
            db_ref[...] = jnp.zeros_like(db_ref)

        y, dya_ = y_ref[...], d_ref[...]
        g = _gelu(y)
        gb = g.astype(bf16)
        s = _sigmoid(_dot(gb, w_ref[...]) + b_ref[...])
        dsp = dya_ * g * s * (1.0 - s)
        dspb = dsp.astype(bf16)
        dg = dya_ * s + _dot_nt(dspb, w_ref[...])
        dy_ref[...] = dg * _gelu_grad(y)
        dsp_ref[...] = dspb
        g_ref[...] = gb
        db_ref[...] += jnp.sum(dsp, axis=0, keepdims=True)

    row = pl.BlockSpec((tm, W), lambda i: (i, 0))
    vec = pl.BlockSpec((1, W), lambda i: (0, 0))
    return _call(
        body, [yn, dya, glu_w, glu_b], name="glu_bwd", grid=(T // tm,),
        in_specs=[row, row, pl.BlockSpec((W, W), lambda i: (0, 0)), vec],
        out_specs=[row, row, row, vec], out_shape=[S((T, W), f32), S((T, W), bf16), S((T, W), bf16), S((1, W), f32)],
        sem=("arbitrary",), plan=plan)


def conv_bwd(proj, dyb, conv_w, plan):
    T = proj.shape[0]
    RB = min(512, T)
    nrb = T // RB

    def body(h_ref, c_ref, b_ref, d_ref, w_ref, dh_ref, dc_ref, db_ref, dw_ref, s_ref):
        w0, w1, w2 = w_ref[0:1, :], w_ref[1:2, :], w_ref[2:3, :]

        def blk(i, carry):
            a0, a1, a2, sh, sc, sb = carry
            r0 = pl.multiple_of(i * RB, RB)
            rs = pl.ds(r0, RB)
            h, cg, bg, dyb_ = h_ref[rs, :], c_ref[rs, :], b_ref[rs, :], d_ref[rs, :]
            ch = cg * h
            pr = pl.ds(jnp.maximum(r0 - 8, 0), 8)
            prev = jnp.where(i > 0, c_ref[pr, :] * h_ref[pr, :], 0.0)
            ch1, ch2 = _shift_rows(ch, prev, 1), _shift_rows(ch, prev, 2)
            dbg = dyb_ * (w2 * ch + w1 * ch1 + w0 * ch2)
            db_ref[rs, :] = dbg.astype(bf16)
            dz = dyb_ * bg
            nx = pl.ds(jnp.minimum(r0 + RB, T - 8), 8)
            nxt = jnp.where(i < nrb - 1, d_ref[nx, :] * b_ref[nx, :], 0.0)
            dch = w2 * dz + w1 * _lift_rows(dz, nxt, 1) + w0 * _lift_rows(dz, nxt, 2)
            dcg, dh = dch * h, dch * cg
            dc_ref[rs, :] = dcg.astype(bf16)
            dh_ref[rs, :] = dh.astype(bf16)
            col = lambda v: jnp.sum(v, axis=0, keepdims=True)
            return (a0 + col(dz * ch2), a1 + col(dz * ch1), a2 + col(dz * ch), sh + col(dh), sc + col(dcg), sb + col(dbg))

        zero = jnp.zeros((1, LANE), f32)
        a0, a1, a2, sh, sc, sb = lax.fori_loop(0, nrb, blk, (zero,) * 6)
        dw_ref[0:1, :] = a0
        dw_ref[1:2, :] = a1
        dw_ref[2:3, :] = a2
        s_ref[0:1, :] = sh
        s_ref[1:2, :] = sc
        s_ref[2:3, :] = sb

    nb = W // LANE
    slab = pl.BlockSpec((T, LANE), lambda k: (0, k))
    three = pl.BlockSpec((3, LANE), lambda k: (0, k))
    return _call(
        body, [proj, proj, proj, dyb, conv_w], name="conv_bwd", grid=(nb,),
        in_specs=[pl.BlockSpec((T, LANE), lambda k: (0, 4 * nb + k)), pl.BlockSpec((T, LANE), lambda k: (0, 5 * nb + k)),
                  pl.BlockSpec((T, LANE), lambda k: (0, 6 * nb + k)), slab, three],
        out_specs=[slab, slab, slab, three, three],
        out_shape=[S((T, W), bf16)] * 3 + [S((3, W), f32)] * 2, sem=("parallel",), vmem=VMEM_LIMIT, plan=plan)


def in_proj_bwd_x(parts, win_g, base, scale, name, plan=None):
    T = base.shape[0]
    tm = min(512, T)
    n = len(parts)

    def body(*refs):
        p_refs, w_ref, b_ref, o_ref = refs[:n], refs[n], refs[n + 1], refs[n + 2]
        acc = scale * b_ref[...]
        for p_ref, (_, _, k) in zip(p_refs, parts):
            acc += _dot_nt(p_ref[...], w_ref[k])
        o_ref[...] = acc

    row = pl.BlockSpec((tm, D), lambda i: (i, 0))
    p_specs = [pl.BlockSpec((tm, W), (lambda i, cb=cb: (i, cb))) for _, cb, _ in parts]
    return _call(
        body, [a for a, _, _ in parts] + [win_g, base], name=name, grid=(T // tm,),
        in_specs=p_specs + [_resident((NDEV, D, W)), row],
        out_specs=[row], out_shape=[S((T, D), f32)], vmem=VMEM_LIMIT, plan=plan)


def ssm_param_bwd(lam_re, lam_im, log_dt, fr, fi, br, bi, dwb, dwcT, dlbr, dlbi):
    def body(lr_ref, li_ref, ldt_ref, fr_ref, fi_ref, br_ref, bi_ref, dwb_ref, dwc_ref, dlbr_ref, dlbi_ref,
             dbr_ref, dbi_ref, dlr_ref, dli_ref, dldt_ref, dcr_ref, dci_ref, dr_s, di_s):
        for k in range(W // LANE):
            for gl in range(NG // (W // LANE)):
                rows, src = slice((8 * k + gl) * GC, (8 * k + gl + 1) * GC), slice(gl * GC, (gl + 1) * GC)
                re, im = slice(gl * NP, (gl + 1) * NP), slice(SW + gl * NP, SW + (gl + 1) * NP)
                dr_s[rows, :] = dwb_ref[k, src, re]
                di_s[rows, :] = dwb_ref[k, src, im]
                dcr_ref[rows, :] = dwc_ref[k, src, re]
                dci_ref[rows, :] = -dwc_ref[k, src, im]
        fr_, fi_ = _per_channel(fr_ref[...]), _per_channel(fi_ref[...])
        br_, bi_, dr, di = br_ref[...], bi_ref[...], dr_s[...], di_s[...]
        dbr_ref[...] = fr_ * dr + fi_ * di
        dbi_ref[...] = fr_ * di - fi_ * dr
        dfr = jnp.sum((dr * br_ + di * bi_).reshape(NG, GC, NP), axis=1)
        dfi = jnp.sum((di * br_ - dr * bi_).reshape(NG, GC, NP), axis=1)
        _, vjp = jax.vjp(_disc, lr_ref[...], li_ref[...], ldt_ref[...])
        dlr_ref[...], dli_ref[...], dldt = vjp((dlbr_ref[...], dlbi_ref[...], dfr, dfi))
        dldt_ref[...] = _transpose_exact(dldt)

    blk = S((NG * GC, NP), f32)
    return pl.pallas_call(
        body, name="ssm_param_bwd", out_shape=[blk, blk, S((NG, NP), f32), S((NG, NP), f32), S((1, NG), f32), blk, blk],
        scratch_shapes=[pltpu.VMEM((NG * GC, NP), f32)] * 2)(
        lam_re, lam_im, log_dt, fr, fi, br, bi, dwb, dwcT, dlbr, dlbi)


def _adam(w, g, m, v):
    m = ADAM_B1 * m + (1.0 - ADAM_B1) * g
    v = ADAM_B2 * v + (1.0 - ADAM_B2) * (g * g)
    m_hat = m / (1.0 - ADAM_B1 ** ADAM_STEP)
    v_hat = v / (1.0 - ADAM_B2 ** ADAM_STEP)
    return -ADAM_LR * (m_hat / (jnp.sqrt(v_hat) + ADAM_EPS) + ADAM_WD * w), m, v


def _sum_in_order(c_ref):
    g = c_ref[0].astype(f32)
    for k in range(1, c_ref.shape[0]):
        g = g + c_ref[k].astype(f32)
    return g


def sum_blocks(contrib, name):
    def body(c_ref, o_ref):
        o_ref[...] = _sum_in_order(c_ref)

    return pl.pallas_call(body, name=name, out_shape=S(contrib.shape[1:], f32))(contrib)


def adam_update(w, m, v, contrib, name, rows_per_block=None, summed_on_0=None, plan=None):
    R, C = w.shape
    n = contrib.shape[0]
    tr = min(rows_per_block or R, R)

    def body(w_ref, m_ref, v_ref, c_ref, *refs):
        g_ref, d_ref, nm_ref, nv_ref = refs[-4:]
        g = _sum_in_order(c_ref)
        if summed_on_0 is not None:
            x, y, c = _coords()
            g = jnp.where(4 * x + 2 * y + c == 0, refs[0][...], g)
        g_ref[...] = g
        d_ref[...], nm_ref[...], nv_ref[...] = _adam(w_ref[...], g, m_ref[...], v_ref[...])

    blk = pl.BlockSpec((tr, C), lambda i: (i, 0))
    extra = [] if summed_on_0 is None else [summed_on_0]
    return _call(
        body, [w, m, v, contrib] + extra, name=name, grid=(R // tr,),
        in_specs=[blk, blk, blk, pl.BlockSpec((n, tr, C), lambda i: (0, i, 0))] + [blk] * len(extra),
        out_specs=[blk] * 4, out_shape=[S((R, C), f32)] * 4, sem=("parallel",), vmem=VMEM_LIMIT, plan=plan)


_ROWVEC = (("b_in", IN_COLS), ("ssm_d", W), ("glu_b", W), ("ln1_g", D), ("ln1_b", D), ("ln2_g", D), ("ln2_b", D))
_HALF = NG * GC // 2
_BC_LANE = {"ssm_b_re": 0, "ssm_b_im": NP, "ssm_c_re": 0, "ssm_c_im": NP}
_PACK = {}
_r = 0
for _n, _k in _ROWVEC:
    _PACK[_n] = _r
    _r += _k // LANE
for _n, _rows in (("ssm_lambda", NG), ("scalars", 8), ("ssm_b", _HALF), ("ssm_c", _HALF), ("conv_w", 16)):
    _PACK[_n] = _r
    _r += _rows
for _n in _BC_LANE:
    _PACK[_n] = _PACK[_n[:5]]
PACK_ROWS = _r
assert PACK_ROWS % 8 == 0
_SMALL = ("b_in", "ssm_lambda_re", "ssm_lambda_im", "ssm_log_dt", "ssm_b_re", "ssm_b_im", "ssm_c_re", "ssm_c_im",
          "ssm_d", "glu_b", "ln1_g", "ln1_b", "ln2_g", "ln2_b")


def pack_grads(su, shcb, sga, sgb, dd, dglu_b, dln1_g, dln1_b, dln2_g, dln2_b, dlam_re, dlam_im, dldt, sqerr, dbr, dbi,
               dc_re, dc_im, dconv):
    nI = sga.shape[0]

    def body(su_ref, sh_ref, sga_ref, sgb_ref, dd_ref, gb_ref, l1g_ref, l1b_ref, l2g_ref, l2b_ref, lr_ref, li_ref, dt_ref,
             sq_ref, br_ref, bi_ref, cr_ref, ci_ref, cw_ref, o_ref):
        o_ref[...] = jnp.zeros_like(o_ref)

        def put_row(name, v):
            r0 = _PACK[name]
            for i in range(v.shape[1] // LANE):
                o_ref[r0 + i:r0 + i + 1, :] = v[:, i * LANE:(i + 1) * LANE]

        ga, gb = sga_ref[0], sgb_ref[0]
        for i in range(1, nI):
            ga, gb = ga + sga_ref[i], gb + sgb_ref[i]
        put_row("b_in", jnp.concatenate([su_ref[k] for k in range(W // LANE)]
                                        + [sh_ref[0:1, :], sh_ref[1:2, :], sh_ref[2:3, :], ga, gb], axis=1))
        put_row("ssm_d", jnp.concatenate([dd_ref[k] for k in range(W // LANE)], axis=1))
        put_row("glu_b", gb_ref[...])
        put_row("ln1_g", l1g_ref[...])
        put_row("ln1_b", l1b_ref[...])
        put_row("ln2_g", l2g_ref[...])
        put_row("ln2_b", l2b_ref[...])
        r0 = _PACK["ssm_lambda"]
        o_ref[r0:r0 + NG, 0:NP] = lr_ref[...]
        o_ref[r0:r0 + NG, NP:2 * NP] = li_ref[...]
        r0 = _PACK["scalars"]
        o_ref[r0:r0 + 1, 0:NG] = dt_ref[...]
        o_ref[r0 + 1:r0 + 2, 0:1] = sq_ref[...]
        for name, ref in (("ssm_b_re", br_ref), ("ssm_b_im", bi_ref), ("ssm_c_re", cr_ref), ("ssm_c_im", ci_ref)):
            r0, l0 = _PACK[name], _BC_LANE[name]
            o_ref[r0:r0 + _HALF, l0:l0 + NP] = pltpu.bitcast(ref[...].astype(bf16), f32)
        for cb in range(W // LANE):
            o_ref[_PACK["conv_w"] + 3 * cb:_PACK["conv_w"] + 3 * cb + 3, :] = cw_ref[:, cb * LANE:(cb + 1) * LANE]

    return pl.pallas_call(body, name="pack_grads", out_shape=S((PACK_ROWS, LANE), f32))(
        su, shcb, sga, sgb, dd, dglu_b, dln1_g, dln1_b, dln2_g, dln2_b, dlam_re, dlam_im, dldt, sqerr, dbr, dbi, dc_re, dc_im,
        dconv)


def adam_small(packed_all, params):
    names = list(_SMALL) + ["conv_w"]
    flat = [a for n in names for a in params[n]]

    def body(*refs):
        p_ref = refs[0]
        ins = refs[1:1 + 3 * len(names)]
        outs = refs[1 + 3 * len(names):-2]
        loss_ref, g_ref = refs[-2], refs[-1]

        def part(k, rs=slice(None), ls=slice(None)):
            return p_ref[k, rs, ls]

        g_all = part(0)
        for k in range(1, NDEV):
            g_all = g_all + part(k)
        g_ref[...] = g_all

        def rows(name, r0, n, l0=0, lanes=LANE):
            return g_ref[_PACK[name] + r0:_PACK[name] + r0 + n, l0:l0 + lanes]

        def grad_of(name):
            if name in dict(_ROWVEC):
                return jnp.concatenate([rows(name, i, 1) for i in range(dict(_ROWVEC)[name] // LANE)], axis=1)
            if name in ("ssm_lambda_re", "ssm_lambda_im"):
                return rows("ssm_lambda", 0, NG, NP * (name == "ssm_lambda_im"), NP)[None]
            if name == "ssm_log_dt":
                return rows("scalars", 0, 1, 0, NG)
            if name in _BC_LANE:
                rs, ls = slice(_PACK[name], _PACK[name] + _HALF), slice(_BC_LANE[name], _BC_LANE[name] + NP)
                g = pltpu.bitcast(part(0, rs, ls), bf16).astype(f32)
                for k in range(1, NDEV):
                    g = g + pltpu.bitcast(part(k, rs, ls), bf16).astype(f32)
                return g.reshape(1, NG, GC, NP)
            full = jnp.concatenate([rows("conv_w", 3 * cb, 3) for cb in range(W // LANE)], axis=1)
            x, y, c = _coords()
            col0 = (4 * x + 2 * y + c) * (W // NDEV)
            sel = (lax.broadcasted_iota(jnp.int32, (W, W // NDEV), 0)
                   == lax.broadcasted_iota(jnp.int32, (W, W // NDEV), 1) + col0).astype(f32)
            return jnp.dot(full, sel, precision=HIGHEST, preferred_element_type=f32)[None]

        loss_ref[...] = 0.5 * rows("scalars", 1, 1, 0, 1)
        for i, name in enumerate(names):
            w_ref, m_ref, v_ref = ins[3 * i:3 * i + 3]
            g = grad_of(name)
            d, m, v = _adam(w_ref[...], g, m_ref[...], v_ref[...])
            outs[4 * i][...] = g
            outs[4 * i + 1][...] = d
            outs[4 * i + 2][...] = m
            outs[4 * i + 3][...] = v

    out_shape = [S(params[n][0].shape, f32) for n in names for _ in range(4)] + [S((1, 1), f32)]
    res = pl.pallas_call(body, name="adam_small", out_shape=out_shape, scratch_shapes=[pltpu.VMEM((PACK_ROWS, LANE), f32)],
                         compiler_params=_cp(None, VMEM_LIMIT))(packed_all, *flat)
    return {n: res[4 * i:4 * i + 4] for i, n in enumerate(names)}, res[-1]


def _block_diag(wgt):
    eye = jnp.eye(8, dtype=wgt.dtype)
    out = wgt[:, :, :, None, :] * eye[None, :, None, :, None]
    return out.reshape(4, 8 * wgt.shape[2], 8 * wgt.shape[3])


def kernel(x, w_in, b_in, ssm_lambda_re, ssm_lambda_im, ssm_log_dt, ssm_b_re, ssm_b_im, ssm_c_re, ssm_c_im, ssm_d, glu_w, glu_b, w_ssm_out, conv_w, w_conv_out, w_o, ln1_g, ln1_b, w_gate, w_up, w_down, ln2_g, ln2_b, loss_target, m_w_in, m_b_in, m_ssm_lambda_re, m_ssm_lambda_im, m_ssm_log_dt, m_ssm_b_re, m_ssm_b_im, m_ssm_c_re, m_ssm_c_im, m_ssm_d, m_glu_w, m_glu_b, m_w_ssm_out, m_conv_w, m_w_conv_out, m_w_o, m_ln1_g, m_ln1_b, m_w_gate, m_w_up, m_w_down, m_ln2_g, m_ln2_b, v_w_in, v_b_in, v_ssm_lambda_re, v_ssm_lambda_im, v_ssm_log_dt, v_ssm_b_re, v_ssm_b_im, v_ssm_c_re, v_ssm_c_im, v_ssm_d, v_glu_w, v_glu_b, v_w_ssm_out, v_conv_w, v_w_conv_out, v_w_o, v_ln1_g, v_ln1_b, v_w_gate, v_w_up, v_w_down, v_ln2_g, v_ln2_b):
    given = dict(locals())
    xs = x[0]
    target = loss_target[0]

    tr = lambda a: jnp.swapaxes(a[0], 0, 1)
    win_s, glu_s, wso_s, wco_s, wo_s, wgT_s, wuT_s, wd_s = prep_weights(
        [w_in[0], glu_w[0], w_ssm_out[0], w_conv_out[0], w_o[0], tr(w_gate), tr(w_up), w_down[0]])
    (win_g,) = run_plan(GatherPlan([win_s], srcs=(0,)), "gather_w_in_u")

    lam_re, lam_im = ssm_lambda_re[0], ssm_lambda_im[0]
    ldt = ssm_log_dt[0].reshape(NG, 1)
    br2 = jnp.swapaxes(ssm_b_re[0], 1, 2).reshape(NG * GC, NP)
    bi2 = jnp.swapaxes(ssm_b_im[0], 1, 2).reshape(NG * GC, NP)
    lbr, lbi, fr, fi, bbr, bbi = ssm_params(lam_re, lam_im, ldt, br2, bi2)
    bb_t = lambda b: b.reshape(4, 8, GC, NP)
    wb = jnp.concatenate([_block_diag(bb_t(bbr)), _block_diag(bb_t(bbi))], axis=2)
    c_t = lambda c: c.reshape(4, 8, GC, NP).transpose(0, 1, 3, 2)
    wc = jnp.concatenate([_block_diag(c_t(ssm_c_re[0])), -_block_diag(c_t(ssm_c_im[0]))], axis=1)
    wbT, wcT = wb.transpose(0, 2, 1), wc.transpose(0, 2, 1)
    wb, wc, wbT, wcT = wb.astype(bf16), wc.astype(bf16), wbT.astype(bf16), wcT.astype(bf16)
    lbr_s, lbi_s = lbr.reshape(4, 1, SW), lbi.reshape(4, 1, SW)
    dsk = ssm_d[0].reshape(4, 1, LANE)

    u_nat, xb = in_proj_u(xs, win_g, b_in)
    u_p = to_perm(u_nat, 0, "perm_u")
    half_a, half_b = (0, 3, 5, 6), (1, 2, 4, 7)
    (y_p, xr_p, xi_p), (win_g, conv_g, glu_g, wso_g) = ssm_fwd(
        u_p, wb, wc, lbr_s, lbi_s, dsk,
        Plans([GatherPlan([win_s], srcs=tuple(range(1, NDEV)), into=[win_g]), GatherPlan([conv_w[0], glu_s, wso_s])]))
    conv_f = conv_g.transpose(1, 0, 2).reshape(3, W)
    (proj,), (wco_g, wo_g, wgT_g) = in_proj_rest(
        xb, win_g, b_in, Plans([GatherPlan([wco_s, wo_s]), GatherPlan([wgT_s], srcs=half_a)]))
    glu_f, wo_f = glu_g.reshape(W, W), wo_g.reshape(D, D)
    (yn,), _ = from_perm(y_p, "unperm_y")
    ya = glu_fwd(yn, glu_f, glu_b)
    yb = conv_fwd(proj, conv_f)
    (merged,), (wgT_g, wuT_g) = merge_fwd(
        ya, yb, wso_g, wco_g, proj,
        Plans([GatherPlan([wgT_s], srcs=half_b, into=[wgT_g]), GatherPlan([wuT_s], srcs=half_a)]))
    (r1, x1b), (wuT_g,) = mix_ln1(merged, wo_f, xs, ln1_g, ln1_b, GatherPlan([wuT_s], srcs=half_b, into=[wuT_g]))
    wgT, wuT = wgT_g.reshape(F, D), wuT_g.reshape(F, D)
    (gate, up, hid), (wd_g,) = gate_up(x1b, wgT, wuT, GatherPlan([wd_s]))
    wd_f = wd_g.reshape(F, D)
    dr2, dffn, sqerr, dln2_g, dln2_b = down_loss(hid, wd_f, r1, ln1_g, ln1_b, ln2_g, ln2_b, target)

    dwd, _ = mm_tn_rows(hid, dffn, "grad_w_down")
    dwd = dwd.reshape(NDEV, FS, D)
    (dgate, dup), (r_wd,) = ffn_bwd_act(dffn, wd_f, gate, up, ScatterPlan([dwd], only=half_a))
    dwgT, (r_wd,) = mm_tn_rows(dgate, x1b, "grad_w_gate", plan=ScatterPlan([dwd], only=half_b, into=[r_wd]))
    dwgT = dwgT.reshape(NDEV, FS, D)
    dwuT, (r_wgT,) = mm_tn_rows(dup, x1b, "grad_w_up", plan=ScatterPlan([dwgT], only=half_a))
    dwuT = dwuT.reshape(NDEV, FS, D)
    (dr1, dmix, dln1_g, dln1_b), (r_wgT,) = ffn_bwd_x(dgate, dup, wgT, wuT, dr2, r1, ln1_g,
                                                     ScatterPlan([dwgT], only=half_b, into=[r_wgT]))
    (dYA, dYB, dga, dgb, sga, sgb), (r_wuT,) = merge_bwd(dmix, wo_f, ya, yb, wso_g, wco_g, proj,
                                                         ScatterPlan([dwuT], only=half_a))
    dwo, _ = mm_tn_rows(merged, dmix, "grad_w_o")
    dwo = dwo.reshape(NDEV, D // NDEV, D)
    (dya, dyb), (r_wuT,) = branches_bwd_x(dYA, dYB, wso_g, wco_g, ScatterPlan([dwuT], only=half_b, into=[r_wuT]))
    dwso = branch_bwd_w(ya, dYA, "grad_w_ssm_out")
    dwco = branch_bwd_w(yb, dYB, "grad_w_conv_out")
    (dyn, dsp, gb, dglu_b), (r_wso,) = glu_bwd(yn, dya, glu_f, glu_b, ScatterPlan([dwso]))
    dglu = mm_tn_rows(gb, dsp, "grad_glu_w")[0].reshape(NDEV, W // NDEV, W)
    (dh, dcg, dbg, dconv, shcb), (r_wco,) = conv_bwd(proj, dyb, conv_f, ScatterPlan([dwco]))
    dwin, (r_wo, r_glu) = grad_w_in_rest(xb, dh, dcg, dbg, dga, dgb, ScatterPlan([dwo, dglu]))
    dy_p = to_perm(dyn, 0, "perm_dy")
    (du_p, dwb, dwcT, dlbr_s, dlbi_s, dd, su), (r_win,) = ssm_bwd(
        u_p, dy_p, xr_p, xi_p, wbT, wcT, lbr_s, lbi_s, dsk, ScatterPlan([dwin], only=tuple(range(1, NDEV))))

    dbr2, dbi2, dlam_re, dlam_im, dldt, dc_re, dc_im = ssm_param_bwd(
        lam_re, lam_im, ldt, fr, fi, br2, bi2, dwb, dwcT, dlbr_s.reshape(NG, NP), dlbi_s.reshape(NG, NP))
    packed = pack_grads(su, shcb, sga, sgb, dd, dglu_b, dln1_g, dln1_b, dln2_g, dln2_b, dlam_re, dlam_im, dldt, sqerr,
                        dbr2, dbi2, dc_re, dc_im, dconv)
    (du,), _ = from_perm(du_p, "unperm_du", bf16)
    dwin_u = mm_tn(xb, du, "grad_w_in_u").reshape(NDEV, D // NDEV, W)

    rest = [(dh, 0, 1), (dcg, 0, 2), (dbg, 0, 3), (dga, 0, 4), (dga, 1, 5), (dgb, 0, 6), (dgb, 1, 7)]
    (gx_rest,), (r_win_u, small_all) = in_proj_bwd_x(
        rest, win_g, dr1, ALPHA, "in_proj_bwd_x_rest", Plans([ScatterPlan([dwin_u]), GatherPlan([packed])]))
    my_rows = sum_blocks(r_win_u, "sum_w_in_u")

    out = {}

    def put(name, res, back=lambda a: a[None]):
        out["grad_" + name], out["delta_" + name], out["new_m_" + name], out["new_v_" + name] = [back(r) for r in res]

    res_wd, (win_u_sum,) = adam_update(w_down[0], m_w_down[0], v_w_down[0], r_wd, "adam_w_down", 176,
                                       plan=ScatterPlan([my_rows], only=(0,), whole=True))
    put("w_down", res_wd)
    (grad_x,), _ = in_proj_bwd_x([(du, 0, 0)], win_g, gx_rest, 1.0, "in_proj_bwd_x_u")
    put("w_in", adam_update(w_in[0], m_w_in[0], v_w_in[0], r_win, "adam_w_in", 256,
                            summed_on_0=win_u_sum.reshape(D, W))[0])
    put("glu_w", adam_update(glu_w[0], m_glu_w[0], v_glu_w[0], r_glu, "adam_glu_w")[0])
    put("w_ssm_out", adam_update(w_ssm_out[0], m_w_ssm_out[0], v_w_ssm_out[0], r_wso, "adam_w_ssm_out")[0])
    put("w_conv_out", adam_update(w_conv_out[0], m_w_conv_out[0], v_w_conv_out[0], r_wco, "adam_w_conv_out")[0])
    put("w_o", adam_update(w_o[0], m_w_o[0], v_w_o[0], r_wo, "adam_w_o")[0])
    untr = lambda a: jnp.swapaxes(a, 0, 1)[None]
    put("w_gate", adam_update(tr(w_gate), tr(m_w_gate), tr(v_w_gate), r_wgT, "adam_w_gate", 176)[0], untr)
    put("w_up", adam_update(tr(w_up), tr(m_w_up), tr(v_w_up), r_wuT, "adam_w_up", 176)[0], untr)
    as_c = lambda a: jnp.swapaxes(a, 2, 3)
    params = {n: (given[n], given["m_" + n], given["v_" + n]) for n in list(_SMALL) + ["conv_w"]}
    for n in ("ssm_b_re", "ssm_b_im"):
        params[n] = tuple(as_c(a) for a in params[n])
    small, loss = adam_small(small_all, params)
    for n, res in small.items():
        put(n, res, as_c if n in ("ssm_b_re", "ssm_b_im") else (lambda a: a))

    names = ["w_in", "b_in", "ssm_lambda_re", "ssm_lambda_im", "ssm_log_dt", "ssm_b_re", "ssm_b_im", "ssm_c_re", "ssm_c_im",
             "ssm_d", "glu_w", "glu_b", "w_ssm_out", "conv_w", "w_conv_out", "w_o", "ln1_g", "ln1_b", "w_gate", "w_up",
             "w_down", "ln2_g", "ln2_b"]
    return (loss.reshape(()), grad_x[None], *[out[p + n] for p in ("grad_", "delta_", "new_m_", "new_v_") for n in names])
```

```python
import functools
import math

import jax
import jax.numpy as jnp
from jax import lax
from jax.experimental import pallas as pl
from jax.experimental.pallas import tpu as pltpu

f32, bf16 = jnp.float32, jnp.bfloat16
S = jax.ShapeDtypeStruct
MESH = pl.DeviceIdType.MESH
HIGHEST = lax.Precision.HIGHEST

D = 1024
W = 512
NG, NP, GC = 32, 64, 16
F = 2816
NDEV = 8
FS = F // NDEV
IN_COLS = 8 * W
ALPHA = 2.0 ** 0.25
LN_EPS = 1e-5
ADAM_LR, ADAM_B1, ADAM_B2, ADAM_EPS, ADAM_WD, ADAM_STEP = 0.001, 0.9, 0.999, 1e-08, 0.01, 10
NC = 32
LANE = 128
SW = 4 * LANE
VMEM_LIMIT = 56 * 1024 * 1024
GRAD_DT = bf16
ANY = pl.BlockSpec(memory_space=pl.ANY)


def _cp(sem=None, vmem=None):
    return pltpu.CompilerParams(dimension_semantics=sem, vmem_limit_bytes=vmem)


def _resident(shape):
    return pl.BlockSpec(shape, lambda i: (0,) * len(shape), pipeline_mode=pl.Buffered(1))


def _dot(a, b):
    return jnp.dot(a, b, preferred_element_type=f32)


def _dot_nt(a, b):
    return lax.dot_general(a, b, (((1,), (1,)), ((), ())), preferred_element_type=f32)


def _dot_tn(a, b):
    return lax.dot_general(a, b, (((0,), (0,)), ((), ())), preferred_element_type=f32)


def _eye(n):
    return (lax.broadcasted_iota(jnp.int32, (n, n), 0) == lax.broadcasted_iota(jnp.int32, (n, n), 1)).astype(f32)


def _transpose_exact(a):
    return lax.dot_general(a, _eye(a.shape[0]), (((0,), (0,)), ((), ())), precision=HIGHEST, preferred_element_type=f32)


def _sigmoid(x):
    return 1.0 / (1.0 + jnp.exp(-x))


_GK = math.sqrt(2.0 / math.pi)


def _gelu(x):
    return 0.5 * x * (1.0 + jnp.tanh(_GK * (x + 0.044715 * x * x * x)))


def _gelu_grad(x):
    th = jnp.tanh(_GK * (x + 0.044715 * x * x * x))
    return 0.5 * (1.0 + th) + 0.5 * x * (1.0 - th * th) * _GK * (1.0 + 3.0 * 0.044715 * x * x)


ROW_PART = 256


def _row_parts(tm):
    return [slice(r, r + min(ROW_PART, tm)) for r in range(0, tm, min(ROW_PART, tm))]


def _ln_stats(r):
    mu = jnp.mean(r, axis=-1, keepdims=True)
    xc = r - mu
    var = jnp.mean(xc * xc, axis=-1, keepdims=True)
    rstd = lax.rsqrt(var + LN_EPS)
    return xc * rstd, rstd


def _ln_bwd(dy, xhat, rstd, g):
    dxh = dy * g
    m1 = jnp.mean(dxh, axis=-1, keepdims=True)
    m2 = jnp.mean(dxh * xhat, axis=-1, keepdims=True)
    return rstd * (dxh - m1 - xhat * m2)


def _coords():
    return lax.axis_index("x"), lax.axis_index("y"), lax.axis_index("c")


def _when(cond, fn):
    if cond is True:
        fn()
    else:
        pl.when(cond)(fn)


class GatherPlan:
    aliases = ()

    def __init__(self, arrs, srcs=None, into=None):
        n = self.n = len(arrs)
        self.srcs = srcs
        self.inputs = list(arrs) + list(into or [])
        if into:
            self.aliases = tuple((n + a, a) for a in range(n))
        self.out_shape = [S((NDEV,) + a.shape, a.dtype) for a in arrs]
        self.sems = [pltpu.SemaphoreType.DMA((n, 7)), pltpu.SemaphoreType.DMA((n, 7)), pltpu.SemaphoreType.DMA((n,))]

    def _has(self, dev):
        if self.srcs is None:
            return True
        idx = 4 * dev[0] + 2 * dev[1] + dev[2]
        return functools.reduce(jnp.logical_or, [idx == s for s in self.srcs])

    def _parts(self, ins, outs, sems):
        n = self.n
        send_sems, recv_sems, loc_sems = sems
        x, y, c = _coords()
        me, sib = (x, y, c), (x, y, 1 - c)
        chips = [(1 - x, y), (x, 1 - y), (1 - x, 1 - y)]

        def slot(a, dev):
            return outs[a].at[4 * dev[0] + 2 * dev[1] + dev[2]]

        def copy(a, k, block, to, src=None):
            return pltpu.make_async_remote_copy(
                src_ref=slot(a, block) if src is None else src, dst_ref=slot(a, block),
                send_sem=send_sems.at[a, k], recv_sem=recv_sems.at[a, k], device_id=to, device_id_type=MESH)

        each = [(j, chip, a) for j, chip in enumerate(chips) for a in range(n)]
        own = self._has(me)
        return dict(
            mine=lambda: [(pltpu.make_async_copy(ins[a], slot(a, me), loc_sems.at[a]), own) for a in range(n)],
            first=lambda: ([(copy(a, 0, me, sib, src=ins[a]), own) for a in range(n)]
                           + [(copy(a, 1 + j, me, (*chip, c), src=ins[a]), own) for j, chip, a in each]),
            landed=lambda: [(copy(a, 1 + j, (*chip, c), me), self._has((*chip, c))) for j, chip, a in each],
            passed=lambda: [(copy(a, 4 + j, (*chip, c), sib), self._has((*chip, c))) for j, chip, a in each],
            from_sib=lambda: ([(copy(a, 0, sib, me), self._has(sib)) for a in range(n)]
                              + [(copy(a, 4 + j, (*chip, 1 - c), me), self._has((*chip, 1 - c))) for j, chip, a in each]))

    def start(self, ins, outs, sems):
        p = self._parts(ins, outs, sems)
        for cp, cond in p["mine"]() + p["first"]():
            _when(cond, cp.start)

    def forward(self, ins, outs, sems):
        p = self._parts(ins, outs, sems)
        for (got, cond), (fwd, _) in zip(p["landed"](), p["passed"]()):
            def relay(got=got, fwd=fwd):
                got.wait_recv()
                fwd.start()

            _when(cond, relay)

    def finish(self, ins, outs, sems):
        p = self._parts(ins, outs, sems)
        for cp, cond in p["from_sib"]():
            _when(cond, cp.wait_recv)
        for cp, cond in p["first"]() + p["passed"]():
            _when(cond, cp.wait_send)
        for cp, cond in p["mine"]():
            _when(cond, cp.wait)


class ScatterPlan:
    aliases = ()

    def __init__(self, gs, only=None, into=None, whole=False):
        n = self.n = len(gs)
        self.only = only
        self.whole = whole
        self.inputs = list(gs) + list(into or [])
        if into:
            self.aliases = tuple((n + a, a) for a in range(n))
        self.out_shape = [S((NDEV,) + g.shape if whole else g.shape, g.dtype) for g in gs]
        self.sems = [pltpu.SemaphoreType.DMA((n, 7)), pltpu.SemaphoreType.DMA((n, 7)), pltpu.SemaphoreType.DMA((n,))]

    def _owner(self, idx):
        if self.only is None:
            return True
        return functools.reduce(jnp.logical_or, [idx == b for b in self.only])

    def _copies(self, ins, outs, sems):
        n = self.n
        send_sems, recv_sems, loc_sems = sems
        x, y, c = _coords()
        me = 4 * x + 2 * y + c
        mine = self._owner(me)
        block = (lambda a, k: ins[a]) if self.whole else (lambda a, k: ins[a].at[k])
        copies = [(pltpu.make_async_copy(block(a, me), outs[a].at[me], loc_sems.at[a]), mine, None) for a in range(n)]
        for m in range(1, NDEV):
            px = 1 - x if m & 4 else x
            py = 1 - y if m & 2 else y
            pc = 1 - c if m & 1 else c
            peer = 4 * px + 2 * py + pc
            for a in range(n):
                copies.append((pltpu.make_async_remote_copy(
                    src_ref=block(a, peer), dst_ref=outs[a].at[me],
                    send_sem=send_sems.at[a, m - 1], recv_sem=recv_sems.at[a, m - 1],
                    device_id=(px, py, pc), device_id_type=MESH), self._owner(peer), mine))
        return copies

    def start(self, ins, outs, sems):
        for cp, sends, _ in self._copies(ins, outs, sems):
            _when(sends, cp.start)

    def forward(self, ins, outs, sems):
        pass

    def finish(self, ins, outs, sems):
        for cp, sends, receives in self._copies(ins, outs, sems):
            if receives is None:
                _when(sends, cp.wait)
            else:
                _when(sends, cp.wait_send)
                _when(receives, cp.wait_recv)


class Plans:
    def __init__(self, plans):
        self.plans = plans
        self.inputs = [a for p in plans for a in p.inputs]
        self.out_shape = [s for p in plans for s in p.out_shape]
        self.sems = [s for p in plans for s in p.sems]
        self.aliases, i, o = [], 0, 0
        for p in plans:
            self.aliases += [(i + a, o + b) for a, b in p.aliases]
            i, o = i + len(p.inputs), o + len(p.out_shape)

    def _each(self, what, ins, outs, sems):
        i = o = s = 0
        for p in self.plans:
            ni, no, ns = len(p.inputs), len(p.out_shape), len(p.sems)
            getattr(p, what)(ins[i:i + ni], outs[o:o + no], sems[s:s + ns])
            i, o, s = i + ni, o + no, s + ns

    def start(self, ins, outs, sems):
        self._each("start", ins, outs, sems)

    def forward(self, ins, outs, sems):
        self._each("forward", ins, outs, sems)

    def finish(self, ins, outs, sems):
        self._each("finish", ins, outs, sems)


def _call(body, args, *, name, grid, in_specs, out_specs, out_shape, scratch=(), sem=None, vmem=None, plan=None,
          aliases=None, relay_step=None):
    aliases = aliases or {}
    if plan is None:
        outs = pl.pallas_call(body, name=name, grid=grid, in_specs=list(in_specs), out_specs=list(out_specs),
                              out_shape=list(out_shape), scratch_shapes=list(scratch), input_output_aliases=aliases,
                              compiler_params=_cp(sem, vmem))(*args)
        return list(outs), []
    ni, no, ns = len(in_specs), len(out_specs), len(scratch)
    pi, po = len(plan.inputs), len(plan.out_shape)
    aliases = {**aliases, **{ni + a: no + b for a, b in plan.aliases}}

    def wrapped(*refs):
        main_in, p_in = refs[:ni], refs[ni:ni + pi]
        main_out, p_out = refs[ni + pi:ni + pi + no], refs[ni + pi + no:ni + pi + no + po]
        main_scr, p_sems = refs[ni + pi + no + po:ni + pi + no + po + ns], refs[ni + pi + no + po + ns:]
        ids = [pl.program_id(d) for d in range(len(grid))]
        first = functools.reduce(jnp.logical_and, [i == 0 for i in ids])
        last = functools.reduce(jnp.logical_and, [i == g - 1 for i, g in zip(ids, grid)])

        @pl.when(first)
        def _():
            plan.start(p_in, p_out, p_sems)

        @pl.when(last if relay_step is None else ids[0] == max(relay_step, 0))
        def _():
            plan.forward(p_in, p_out, p_sems)

        body(*main_in, *main_out, *main_scr)

        @pl.when(last)
        def _():
            plan.finish(p_in, p_out, p_sems)

    outs = pl.pallas_call(
        wrapped, name=name, grid=grid, in_specs=list(in_specs) + [ANY] * pi, out_specs=list(out_specs) + [ANY] * po,
        out_shape=list(out_shape) + list(plan.out_shape), scratch_shapes=list(scratch) + list(plan.sems),
        input_output_aliases=aliases, compiler_params=_cp(("arbitrary",) * len(grid), vmem),
    )(*args, *plan.inputs)
    return list(outs[:no]), list(outs[no:])


def run_plan(plan, name):
    def body(*refs):
        ins, outs, sems = refs[:len(plan.inputs)], refs[len(plan.inputs):len(plan.inputs) + len(plan.out_shape)], \
            refs[len(plan.inputs) + len(plan.out_shape):]
        plan.start(ins, outs, sems)
        plan.forward(ins, outs, sems)
        plan.finish(ins, outs, sems)

    return pl.pallas_call(body, name=name, in_specs=[ANY] * len(plan.inputs), out_specs=[ANY] * len(plan.out_shape),
                          out_shape=list(plan.out_shape), scratch_shapes=list(plan.sems))(*plan.inputs)


def mm_tn(a, b, name, tn=512, into=None, block0=0, nblocks=None):
    T, K = a.shape
    N = b.shape[1]
    tn = min(tn, N)
    nblocks = nblocks or (N // tn if into is None else into.shape[0])

    def body(a_ref, b_ref, *rest):
        rest[-1][...] = _dot_tn(a_ref[...], b_ref[...]).astype(GRAD_DT)

    args, in_specs, aliases = [a, b], [_resident((T, K)), pl.BlockSpec((T, tn), lambda j: (0, j))], {}
    if into is not None:
        args.append(into)
        in_specs.append(ANY)
        aliases = {2: 0}
    (out,), _ = _call(body, args, name=name, grid=(N // tn,), in_specs=in_specs,
                      out_specs=[pl.BlockSpec((None, K, tn), lambda j: (block0 + j, 0, 0))],
                      out_shape=[S((nblocks, K, tn), GRAD_DT)], sem=("parallel",), vmem=VMEM_LIMIT, aliases=aliases)
    return out


def grad_w_in_rest(xb, dh, dcg, dbg, dga, dgb, plan):
    T = xb.shape[0]
    order = ((0, 0), (1, 1), (2, 2), (3, 3), (4, 3), (5, 4), (6, 4))

    def body(x_ref, *refs):
        o_ref = refs[-1]
        j = pl.program_id(0)
        for step, opnd in order:
            @pl.when(j == step)
            def _(opnd=opnd):
                o_ref[...] = _dot_tn(x_ref[...], refs[opnd][...]).astype(GRAD_DT)

    once = lambda: pl.BlockSpec((T, W), lambda j: (0, 0), pipeline_mode=pl.Buffered(1))
    (out,), sent = _call(
        body, [xb, dh, dcg, dbg, dga, dgb], name="grad_w_in_rest", grid=(len(order),),
        in_specs=[_resident((T, D)), once(), once(), once(),
                  pl.BlockSpec((T, W), lambda j: (0, jnp.clip(j - 3, 0, 1))),
                  pl.BlockSpec((T, W), lambda j: (0, jnp.clip(j - 5, 0, 1)))],
        out_specs=[pl.BlockSpec((None, D, W), lambda j: (1 + j, 0, 0))],
        out_shape=[S((NDEV, D, W), GRAD_DT)], sem=("arbitrary",), vmem=VMEM_LIMIT, plan=plan)
    return out, sent


def mm_tn_rows(a, b, name, tk=256, plan=None):
    T, K = a.shape
    N = b.shape[1]
    tk = min(tk, K)

    def body(a_ref, b_ref, o_ref):
        o_ref[...] = _dot_tn(a_ref[...], b_ref[...]).astype(GRAD_DT)

    (out,), sent = _call(body, [a, b], name=name, grid=(K // tk,),
                         in_specs=[pl.BlockSpec((T, tk), lambda i: (0, i)), _resident((T, N))],
                         out_specs=[pl.BlockSpec((tk, N), lambda i: (i, 0))], out_shape=[S((K, N), GRAD_DT)],
                         sem=("parallel",), vmem=VMEM_LIMIT, plan=plan)
    return out, sent


def prep_weights(ws):
    def body(*refs):
        for i in range(len(ws)):
            refs[len(ws) + i][...] = refs[i][...].astype(bf16)

    return pl.pallas_call(body, name="prep_weights", out_shape=[S(w.shape, bf16) for w in ws],
                          compiler_params=_cp(None, VMEM_LIMIT))(*ws)


REST_BLOCKS = (4, 5, 6, 7, 1, 2, 3)
REST_COLS = len(REST_BLOCKS) * W


def in_proj_u(x, win_g, b_in):
    T = x.shape[0]
    tm = min(1024, T)

    def body(x_ref, w_ref, b_ref, u_ref, xb_ref):
        xb = x_ref[...].astype(bf16)
        xb_ref[...] = xb
        u_ref[...] = _dot(xb, w_ref[...]) + b_ref[...]

    row = pl.BlockSpec((tm, D), lambda i: (i, 0))
    return pl.pallas_call(
        body, name="in_proj_u", grid=(T // tm,),
        in_specs=[row, pl.BlockSpec((None, D, W), lambda i: (0, 0, 0)), pl.BlockSpec((1, W), lambda i: (0, 0))],
        out_specs=[pl.BlockSpec((tm, W), lambda i: (i, 0)), row],
        out_shape=[S((T, W), f32), S((T, D), bf16)], compiler_params=_cp(("parallel",), VMEM_LIMIT),
    )(x, win_g, b_in)


def in_proj_rest(xb, win_g, b_in, plan):
    T = xb.shape[0]
    tm = min(512, T)

    def body(x_ref, w_ref, b_ref, o_ref):
        xb_ = x_ref[...]
        for i, k in enumerate(REST_BLOCKS):
            o_ref[:, i * W:(i + 1) * W] = _dot(xb_, w_ref[k]) + b_ref[:, k * W:(k + 1) * W]

    return _call(
        body, [xb, win_g, b_in], name="in_proj_rest", grid=(T // tm,),
        in_specs=[pl.BlockSpec((tm, D), lambda i: (i, 0)), _resident((NDEV, D, W)), _resident((1, IN_COLS))],
        out_specs=[pl.BlockSpec((tm, REST_COLS), lambda i: (i, 0))],
        out_shape=[S((T, REST_COLS), f32)], vmem=VMEM_LIMIT, plan=plan, relay_step=T // tm - 2)


def to_perm(a, cb0, name):
    T = a.shape[0]
    L = T // NC

    def body(a_ref, o_ref):
        def step(jb, carry):
            j0 = pl.multiple_of(jb * 8, 8)
            for q in range(NC // 8):
                x = jnp.stack([a_ref[pl.ds((8 * q + c) * L + j0, 8), :] for c in range(8)], axis=0)
                y = jnp.swapaxes(x, 0, 1)
                for j in range(8):
                    o_ref[pl.ds((j0 + j) * NC + 8 * q, 8), :] = y[j]
            return carry

        lax.fori_loop(0, L // 8, step, 0)

    return pl.pallas_call(
        body, name=name, grid=(W // LANE,),
        in_specs=[pl.BlockSpec((T, LANE), lambda k: (0, cb0 + k))], out_specs=pl.BlockSpec((T, LANE), lambda k: (0, k)),
        out_shape=S((T, W), f32), compiler_params=_cp(("parallel",), VMEM_LIMIT),
    )(a)


def from_perm(a, name, out_dtype=f32, plan=None):
    T = a.shape[0]
    L = T // NC

    def body(a_ref, o_ref):
        def step(jb, carry):
            j0 = pl.multiple_of(jb * 16, 16)
            for q in range(NC // 8):
                halves = []
                for h in range(2):
                    x = jnp.stack([a_ref[pl.ds((j0 + 8 * h + j) * NC + 8 * q, 8), :] for j in range(8)], axis=0)
                    halves.append(jnp.swapaxes(x, 0, 1))
                for c in range(8):
                    o_ref[pl.ds((8 * q + c) * L + j0, 16), :] = jnp.concatenate(
                        [halves[0][c], halves[1][c]], axis=0).astype(out_dtype)
            return carry

        lax.fori_loop(0, L // 16, step, 0)

    slab = pl.BlockSpec((T, LANE), lambda k: (0, k))
    return _call(body, [a], name=name, grid=(W // LANE,), in_specs=[slab], out_specs=[slab],
                 out_shape=[S((T, W), out_dtype)], sem=("parallel",), vmem=VMEM_LIMIT, plan=plan)


def _disc(lr, li, ldt):
    dt = jnp.exp(ldt)
    mag = jnp.exp(lr * dt)
    lbr = mag * jnp.cos(li * dt)
    lbi = mag * jnp.sin(li * dt)
    den = lr * lr + li * li
    nr = lbr - 1.0
    return lbr, lbi, (nr * lr + lbi * li) / den, (lbi * lr - nr * li) / den


def _per_channel(f):
    return jnp.broadcast_to(f[:, None, :], (NG, GC, NP)).reshape(NG * GC, NP)


def ssm_params(lam_re, lam_im, log_dt, br, bi):
    def body(lr_ref, li_ref, ldt_ref, br_ref, bi_ref, lbr_ref, lbi_ref, fr_ref, fi_ref, bbr_ref, bbi_ref):
        lbr, lbi, fr, fi = _disc(lr_ref[...], li_ref[...], ldt_ref[...])
        lbr_ref[...], lbi_ref[...], fr_ref[...], fi_ref[...] = lbr, lbi, fr, fi
        fr_, fi_, br_, bi_ = _per_channel(fr), _per_channel(fi), br_ref[...], bi_ref[...]
        bbr_ref[...] = fr_ * br_ - fi_ * bi_
        bbi_ref[...] = fr_ * bi_ + fi_ * br_

    return pl.pallas_call(body, name="ssm_params", out_shape=[S((NG, NP), f32)] * 4 + [S((NG * GC, NP), f32)] * 2)(
        lam_re, lam_im, log_dt, br, bi)


SCAN_UNROLL = 4
SCAN_LANES = 2 * LANE


def _steps(n, body, carry):
    main = n // SCAN_UNROLL

    def trip(t, c):
        for q in range(SCAN_UNROLL):
            c = body(t * SCAN_UNROLL + q, c)
        return c

    carry = lax.fori_loop(0, main, trip, carry)
    for i in range(main * SCAN_UNROLL, n):
        carry = body(i, carry)
    return carry


def _scan_body(T):
    L = T // NC
    RB = min(512, T)
    nsq = int(round(math.log2(L)))
    assert 2 ** nsq == L and T % RB == 0 and L % 16 == 0

    def rows(i):
        return pl.ds(pl.multiple_of(i * RB, RB), RB)

    def tile(j):
        return pl.ds(j * NC if isinstance(j, int) else pl.multiple_of(j * NC, NC), NC)

    def forward_states(u_ref, wb_ref, lbr_ref, lbi_ref, sre, sim, ere, eim):
        def bproj(i, carry):
            bu = _dot(u_ref[rows(i), :].astype(bf16), wb_ref[...])
            sre[rows(i), :] = bu[:, :SW]
            sim[rows(i), :] = bu[:, SW:]
            return carry

        lax.fori_loop(0, T // RB, bproj, 0)
        for lb in range(SW // SCAN_LANES):
            ls = slice(lb * SCAN_LANES, (lb + 1) * SCAN_LANES)
            ar = jnp.broadcast_to(lbr_ref[:, ls], (NC, SCAN_LANES))
            ai = jnp.broadcast_to(lbi_ref[:, ls], (NC, SCAN_LANES))

            def step(j, carry):
                xr, xi = carry
                nr = ar * xr - ai * xi + sre[tile(j), ls]
                ni = ar * xi + ai * xr + sim[tile(j), ls]
                sre[tile(j), ls] = nr
                sim[tile(j), ls] = ni
                return nr, ni

            zero = jnp.zeros((NC, SCAN_LANES), f32)
            _steps(L, step, (zero, zero))
            pr, pi = lbr_ref[:, ls], lbi_ref[:, ls]
            for _ in range(nsq):
                pr, pi = pr * pr - pi * pi, 2.0 * pr * pi
            er = jnp.zeros((1, SCAN_LANES), f32)
            ei = er
            ere[0:1, ls] = er
            eim[0:1, ls] = ei
            base = (L - 1) * NC
            for c in range(1, NC):
                lr_ = sre[base + c - 1:base + c, ls]
                li_ = sim[base + c - 1:base + c, ls]
                er, ei = lr_ + pr * er - pi * ei, li_ + pr * ei + pi * er
                ere[c:c + 1, ls] = er
                eim[c:c + 1, ls] = ei
            e_r, e_i = ere[:, ls].reshape(NC // 8, 8, SCAN_LANES), eim[:, ls].reshape(NC // 8, 8, SCAN_LANES)
            ar8, ai8 = ar[0:8], ai[0:8]

            def fix(j, carry):
                pwr, pwi = carry
                xr = sre[tile(j), ls].reshape(NC // 8, 8, SCAN_LANES) + (pwr * e_r - pwi * e_i)
                xi = sim[tile(j), ls].reshape(NC // 8, 8, SCAN_LANES) + (pwr * e_i + pwi * e_r)
                sre[tile(j), ls] = xr.reshape(NC, SCAN_LANES)
                sim[tile(j), ls] = xi.reshape(NC, SCAN_LANES)
                return pwr * ar8 - pwi * ai8, pwr * ai8 + pwi * ar8

            _steps(L, fix, (ar8, ai8))

    return L, RB, nsq, rows, tile, forward_states


def ssm_fwd(u_p, wb, wc, lbr, lbi, dsk, plan):
    T = u_p.shape[0]
    L, RB, nsq, rows, tile, forward_states = _scan_body(T)
    nslab = W // LANE

    def body(u_ref, wb_ref, wc_ref, lbr_ref, lbi_ref, d_ref, y_ref, xr_ref, xi_ref, sre, sim, ere, eim):
        forward_states(u_ref, wb_ref, lbr_ref, lbi_ref, sre, sim, ere, eim)

        def cproj(i, carry):
            xr, xi = sre[rows(i), :].astype(bf16), sim[rows(i), :].astype(bf16)
            xr_ref[rows(i), :] = xr
            xi_ref[rows(i), :] = xi
            y = _dot(xr, wc_ref[0:SW, :]) + _dot(xi, wc_ref[SW:, :])
            y_ref[rows(i), :] = y + d_ref[...] * u_ref[rows(i), :]
            return carry

        lax.fori_loop(0, T // RB, cproj, 0)

    slab = pl.BlockSpec((T, LANE), lambda k: (0, k))
    states = pl.BlockSpec((T, SW), lambda k: (0, k))
    return _call(
        body, [u_p, wb, wc, lbr, lbi, dsk], name="ssm_fwd", grid=(nslab,),
        in_specs=[slab, pl.BlockSpec((None, LANE, 2 * SW), lambda k: (k, 0, 0)),
                  pl.BlockSpec((None, 2 * SW, LANE), lambda k: (k, 0, 0)),
                  pl.BlockSpec((None, 1, SW), lambda k: (k, 0, 0)), pl.BlockSpec((None, 1, SW), lambda k: (k, 0, 0)),
                  pl.BlockSpec((None, 1, LANE), lambda k: (k, 0, 0))],
        out_specs=[slab, states, states], out_shape=[S((T, W), f32), S((T, nslab * SW), bf16), S((T, nslab * SW), bf16)],
        scratch=[pltpu.VMEM((T, SW), f32), pltpu.VMEM((T, SW), f32), pltpu.VMEM((NC, SW), f32), pltpu.VMEM((NC, SW), f32)],
        vmem=VMEM_LIMIT, plan=plan)


def ssm_bwd(u_p, dy_p, xr, xi, wbT, wcT, lbr, lbi, dsk, plan):
    T = u_p.shape[0]
    L, RB, nsq, rows, tile, _ = _scan_body(T)

    def body(u_ref, dy_ref, sre, sim, wbT_ref, wcT_ref, lbr_ref, lbi_ref, d_ref,
             du_ref, dwb_ref, dwc_ref, dlr_ref, dli_ref, dd_ref, su_ref, gre, gim, ere, eim):
        def dstate(i, carry):
            g = _dot(dy_ref[rows(i), :].astype(bf16), wcT_ref[...])
            gre[rows(i), :] = g[:, :SW]
            gim[rows(i), :] = g[:, SW:]
            return carry

        lax.fori_loop(0, T // RB, dstate, 0)
        row = lax.broadcasted_iota(jnp.int32, (NC, SCAN_LANES), 0)
        for lb in range(SW // SCAN_LANES):
            ls = slice(lb * SCAN_LANES, (lb + 1) * SCAN_LANES)
            ar = jnp.broadcast_to(lbr_ref[:, ls], (NC, SCAN_LANES))
            ai = jnp.broadcast_to(lbi_ref[:, ls], (NC, SCAN_LANES))

            def step(i, carry):
                gr, gi = carry
                j = L - 1 - i
                nr = ar * gr + ai * gi + gre[tile(j), ls]
                ni = ar * gi - ai * gr + gim[tile(j), ls]
                gre[tile(j), ls] = nr
                gim[tile(j), ls] = ni
                return nr, ni

            zero = jnp.zeros((NC, SCAN_LANES), f32)
            _steps(L, step, (zero, zero))
            pr, pi = lbr_ref[:, ls], -lbi_ref[:, ls]
            for _ in range(nsq):
                pr, pi = pr * pr - pi * pi, 2.0 * pr * pi
            er = jnp.zeros((1, SCAN_LANES), f32)
            ei = er
            ere[NC - 1:NC, ls] = er
            eim[NC - 1:NC, ls] = ei
            for c in range(NC - 2, -1, -1):
                lr_ = gre[c + 1:c + 2, ls]
                li_ = gim[c + 1:c + 2, ls]
                er, ei = lr_ + pr * er - pi * ei, li_ + pr * ei + pi * er
                ere[c:c + 1, ls] = er
                eim[c:c + 1, ls] = ei
            e_r, e_i = ere[:, ls].reshape(NC // 8, 8, SCAN_LANES), eim[:, ls].reshape(NC // 8, 8, SCAN_LANES)
            ar8, ai8 = ar[0:8], ai[0:8]

            def fixed(j, pwr, pwi):
                gr = (gre[tile(j), ls].reshape(NC // 8, 8, SCAN_LANES) + (pwr * e_r - pwi * e_i)).reshape(NC, SCAN_LANES)
                gi = (gim[tile(j), ls].reshape(NC // 8, 8, SCAN_LANES) + (pwr * e_i + pwi * e_r)).reshape(NC, SCAN_LANES)
                gre[tile(j), ls] = gr
                gim[tile(j), ls] = gi
                return gr, gi

            def fix(i, carry):
                pwr, pwi, accr, acci = carry
                j = L - 1 - i
                gr, gi = fixed(j, pwr, pwi)
                xr, xi = sre[tile(j - 1), ls].astype(f32), sim[tile(j - 1), ls].astype(f32)
                return (pwr * ar8 + pwi * ai8, pwi * ar8 - pwr * ai8,
                        accr + gr * xr + gi * xi, acci + gi * xr - gr * xi)

            pwr, pwi, accr, acci = _steps(L - 1, fix, (ar8, -ai8, zero, zero))
            gr, gi = fixed(0, pwr, pwi)
            xr = jnp.where(row == 0, 0.0, pltpu.roll(sre[tile(L - 1), ls].astype(f32), 1, axis=0))
            xi = jnp.where(row == 0, 0.0, pltpu.roll(sim[tile(L - 1), ls].astype(f32), 1, axis=0))
            accr = accr + gr * xr + gi * xi
            acci = acci + gi * xr - gr * xi
            dlr_ref[:, ls] = jnp.sum(accr, axis=0, keepdims=True)
            dli_ref[:, ls] = jnp.sum(acci, axis=0, keepdims=True)

        dwb_ref[...] = jnp.zeros_like(dwb_ref)
        dwc_ref[...] = jnp.zeros_like(dwc_ref)
        dd_ref[...] = jnp.zeros_like(dd_ref)
        su_ref[...] = jnp.zeros_like(su_ref)

        def finish(i, carry):
            u32, dy32 = u_ref[rows(i), :], dy_ref[rows(i), :]
            ub, dyb = u32.astype(bf16), dy32.astype(bf16)
            gr, gi = gre[rows(i), :].astype(bf16), gim[rows(i), :].astype(bf16)
            du = _dot(gr, wbT_ref[0:SW, :]) + _dot(gi, wbT_ref[SW:, :]) + dy32 * d_ref[...]
            du_ref[rows(i), :] = du
            su_ref[...] += jnp.sum(du, axis=0, keepdims=True)
            dwb_ref[:, 0:SW] += _dot_tn(ub, gr)
            dwb_ref[:, SW:] += _dot_tn(ub, gi)
            dwc_ref[:, 0:SW] += _dot_tn(dyb, sre[rows(i), :])
            dwc_ref[:, SW:] += _dot_tn(dyb, sim[rows(i), :])
            dd_ref[...] += jnp.sum(dy32 * u32, axis=0, keepdims=True)
            return carry

        lax.fori_loop(0, T // RB, finish, 0)

    slab = pl.BlockSpec((T, LANE), lambda k: (0, k))
    wide = pl.BlockSpec((None, LANE, 2 * SW), lambda k: (k, 0, 0))
    tall = pl.BlockSpec((None, 2 * SW, LANE), lambda k: (k, 0, 0))
    vec = pl.BlockSpec((None, 1, SW), lambda k: (k, 0, 0))
    vecd = pl.BlockSpec((None, 1, LANE), lambda k: (k, 0, 0))
    states = pl.BlockSpec((T, SW), lambda k: (0, k))
    nslab = W // LANE
    return _call(
        body, [u_p, dy_p, xr, xi, wbT, wcT, lbr, lbi, dsk], name="ssm_bwd", grid=(nslab,),
        in_specs=[slab, slab, states, states, tall, wide, vec, vec, vecd],
        out_specs=[slab, wide, wide, vec, vec, vecd, vecd],
        out_shape=[S((T, W), f32), S((nslab, LANE, 2 * SW), f32), S((nslab, LANE, 2 * SW), f32),
                   S((nslab, 1, SW), f32), S((nslab, 1, SW), f32), S((nslab, 1, LANE), f32), S((nslab, 1, LANE), f32)],
        scratch=[pltpu.VMEM((T, SW), f32)] * 2 + [pltpu.VMEM((NC, SW), f32)] * 2, vmem=VMEM_LIMIT, plan=plan)


def glu_fwd(yn, glu_w, glu_b):
    T = yn.shape[0]
    tm = min(512, T)

    def body(y_ref, w_ref, b_ref, o_ref):
        g = _gelu(y_ref[...])
        o_ref[...] = (g * _sigmoid(_dot(g.astype(bf16), w_ref[...]) + b_ref[...])).astype(bf16)

    return pl.pallas_call(
        body, name="glu_fwd", grid=(T // tm,),
        in_specs=[pl.BlockSpec((tm, W), lambda i: (i, 0)), pl.BlockSpec((W, W), lambda i: (0, 0)), pl.BlockSpec((1, W), lambda i: (0, 0))],
        out_specs=pl.BlockSpec((tm, W), lambda i: (i, 0)), out_shape=S((T, W), bf16), compiler_params=_cp(("parallel",)),
    )(yn, glu_w, glu_b)


def _shift_rows(cur, prev8, k):
    return pltpu.roll(jnp.concatenate([prev8, cur], axis=0), k, axis=0)[8:]


def _lift_rows(cur, next8, k):
    n = cur.shape[0]
    return pltpu.roll(jnp.concatenate([cur, next8], axis=0), n + 8 - k, axis=0)[:n]


def conv_fwd(proj, conv_w):
    T = proj.shape[0]
    RB = min(512, T)

    def body(h_ref, c_ref, b_ref, w_ref, o_ref):
        w0, w1, w2 = w_ref[0:1, :], w_ref[1:2, :], w_ref[2:3, :]

        def blk(i, carry):
            r0 = pl.multiple_of(i * RB, RB)
            rs = pl.ds(r0, RB)
            ch = c_ref[rs, :] * h_ref[rs, :]
            pr = pl.ds(jnp.maximum(r0 - 8, 0), 8)
            prev = jnp.where(i > 0, c_ref[pr, :] * h_ref[pr, :], 0.0)
            z = w2 * ch + w1 * _shift_rows(ch, prev, 1) + w0 * _shift_rows(ch, prev, 2)
            o_ref[rs, :] = (b_ref[rs, :] * z).astype(bf16)
            return carry

        lax.fori_loop(0, T // RB, blk, 0)

    nb = W // LANE
    return pl.pallas_call(
        body, name="conv_fwd", grid=(nb,),
        in_specs=[pl.BlockSpec((T, LANE), lambda k: (0, 4 * nb + k)), pl.BlockSpec((T, LANE), lambda k: (0, 5 * nb + k)),
                  pl.BlockSpec((T, LANE), lambda k: (0, 6 * nb + k)),pl.BlockSpec((3, LANE), lambda k: (0, k))],
        out_specs=pl.BlockSpec((T, LANE), lambda k: (0, k)), out_shape=S((T, W), bf16),
        compiler_params=_cp(("parallel",), VMEM_LIMIT),
    )(proj, proj, proj, conv_w)


def _dense_columns(blocks_ref, dense_ref):
    for k in range(NDEV):
        dense_ref[:, k * LANE:(k + 1) * LANE] = blocks_ref[k]


def merge_fwd(ya, yb, wso, wco, proj, plan):
    T = ya.shape[0]
    tm = min(1024, T)

    def body(ya_ref, yb_ref, wa_ref, wb_ref, ga_ref, gb_ref, o_ref, wa_s, wb_s):
        @pl.when(pl.program_id(0) == 0)
        def _():
            _dense_columns(wa_ref, wa_s)
            _dense_columns(wb_ref, wb_s)

        o_ref[...] = (_sigmoid(ga_ref[...]) * _dot(ya_ref[...], wa_s[...])
                      + _sigmoid(gb_ref[...]) * _dot(yb_ref[...], wb_s[...])).astype(bf16)

    act = pl.BlockSpec((tm, W), lambda i: (i, 0))
    return _call(
        body, [ya, yb, wso, wco, proj, proj], name="merge_fwd", grid=(T // tm,),
        in_specs=[act, act, _resident((NDEV, W, LANE)), _resident((NDEV, W, LANE)),
                  pl.BlockSpec((tm, D), lambda i: (i, 0)), pl.BlockSpec((tm, D), lambda i: (i, 1))],
        out_specs=[pl.BlockSpec((tm, D), lambda i: (i, 0))], out_shape=[S((T, D), bf16)],
        scratch=[pltpu.VMEM((W, D), bf16), pltpu.VMEM((W, D), bf16)], vmem=VMEM_LIMIT, plan=plan)


def mix_ln1(merged, w_o, x, g1, b1, plan):
    T = x.shape[0]
    tm = min(512, T)

    def body(m_ref, w_ref, x_ref, g_ref, b_ref, r_ref, x1_ref):
        for rs in _row_parts(tm):
            r = ALPHA * x_ref[rs, :] + _dot(m_ref[rs, :], w_ref[...])
            r_ref[rs, :] = r
            xhat, _ = _ln_stats(r)
            x1_ref[rs, :] = (xhat * g_ref[...] + b_ref[...]).astype(bf16)

    row = pl.BlockSpec((tm, D), lambda i: (i, 0))
    vec = pl.BlockSpec((1, D), lambda i: (0, 0))
    return _call(
        body, [merged, w_o, x, g1, b1], name="mix_ln1", grid=(T // tm,),
        in_specs=[row, _resident((D, D)), row, vec, vec],
        out_specs=[row, row], out_shape=[S((T, D), f32), S((T, D), bf16)], sem=("parallel",), vmem=VMEM_LIMIT, plan=plan,
        relay_step=T // tm - 2)


FT = 256


def gate_up(x1b, wgT, wuT, plan):
    T = x1b.shape[0]
    tm = min(512, T)

    def body(x_ref, wg_ref, wu_ref, g_ref, u_ref, h_ref):
        x = x_ref[...]
        for n in range(F // FT):
            cs = slice(n * FT, (n + 1) * FT)
            g = _dot_nt(x, wg_ref[cs, :])
            u = _dot_nt(x, wu_ref[cs, :])
            g_ref[:, cs] = g.astype(bf16)
            u_ref[:, cs] = u.astype(bf16)
            h_ref[:, cs] = (g * _sigmoid(g) * u).astype(bf16)

    osp = pl.BlockSpec((tm, F), lambda i: (i, 0))
    return _call(
        body, [x1b, wgT, wuT], name="gate_up", grid=(T // tm,),
        in_specs=[pl.BlockSpec((tm, D), lambda i: (i, 0)), _resident((F, D)), _resident((F, D))],
        out_specs=[osp, osp, osp], out_shape=[S((T, F), bf16)] * 3, vmem=VMEM_LIMIT, plan=plan, relay_step=T // tm - 3)


def down_loss(hid, w_down, r1, g1, b1, g2, b2, target):
    T = hid.shape[0]
    tm = min(512, T)

    def body(h_ref, w_ref, r1_ref, g1_ref, b1_ref, g2_ref, b2_ref, t_ref, dr_ref, drb_ref, loss_ref, dg_ref, db_ref):
        @pl.when(pl.program_id(0) == 0)
        def _():
            loss_ref[...] = jnp.zeros_like(loss_ref)
            dg_ref[...] = jnp.zeros_like(dg_ref)
            db_ref[...] = jnp.zeros_like(db_ref)

        for rs in _row_parts(tm):
            xh1, _ = _ln_stats(r1_ref[rs, :])
            x1 = xh1 * g1_ref[...] + b1_ref[...]
            r2 = ALPHA * x1 + _dot(h_ref[rs, :], w_ref[...])
            xh2, rstd2 = _ln_stats(r2)
            err = xh2 * g2_ref[...] + b2_ref[...] - t_ref[rs, :]
            loss_ref[...] += jnp.sum(jnp.mean(err * err, axis=-1, keepdims=True), axis=0, keepdims=True)
            dy = err * (1.0 / D)
            dg_ref[...] += jnp.sum(dy * xh2, axis=0, keepdims=True)
            db_ref[...] += jnp.sum(dy, axis=0, keepdims=True)
            dr = _ln_bwd(dy, xh2, rstd2, g2_ref[...])
            dr_ref[rs, :] = dr
            drb_ref[rs, :] = dr.astype(bf16)

    row = pl.BlockSpec((tm, D), lambda i: (i, 0))
    vec = pl.BlockSpec((1, D), lambda i: (0, 0))
    return pl.pallas_call(
        body, name="down_loss", grid=(T // tm,),
        in_specs=[pl.BlockSpec((tm, F), lambda i: (i, 0)), _resident((F, D)), row, vec, vec, vec, vec, row],
        out_specs=[row, row, pl.BlockSpec((1, 1), lambda i: (0, 0)), vec, vec],
        out_shape=[S((T, D), f32), S((T, D), bf16), S((1, 1), f32), S((1, D), f32), S((1, D), f32)],
        compiler_params=_cp(("arbitrary",), VMEM_LIMIT),
    )(hid, w_down, r1, g1, b1, g2, b2, target)


def ffn_bwd_act(dffn, w_down, gate, up, plan):
    T = dffn.shape[0]
    tm = min(512, T)

    def body(d_ref, w_ref, g_ref, u_ref, dg_ref, du_ref):
        for n in range(F // FT):
            cs = slice(n * FT, (n + 1) * FT)
            for rs in _row_parts(tm):
                dh = _dot_nt(d_ref[rs, :], w_ref[cs, :])
                g, u = g_ref[rs, cs].astype(f32), u_ref[rs, cs].astype(f32)
                sg = _sigmoid(g)
                t = g * sg
                du_ref[rs, cs] = (dh * t).astype(bf16)
                dg_ref[rs, cs] = (dh * u * (sg + t - t * sg)).astype(bf16)

    osp = pl.BlockSpec((tm, F), lambda i: (i, 0))
    return _call(
        body, [dffn, w_down, gate, up], name="ffn_bwd_act", grid=(T // tm,),
        in_specs=[pl.BlockSpec((tm, D), lambda i: (i, 0)), _resident((F, D)), osp, osp],
        out_specs=[osp, osp], out_shape=[S((T, F), bf16)] * 2, sem=("parallel",), vmem=VMEM_LIMIT, plan=plan)


def ffn_bwd_x(dgate, dup, wgT, wuT, dr2, r1, g1, plan):
    T = dr2.shape[0]
    tm = min(512, T)

    def body(dg_ref, du_ref, wg_ref, wu_ref, dr2_ref, r1_ref, g1_ref, dr_ref, drb_ref, dgam_ref, dbet_ref):
        @pl.when(pl.program_id(0) == 0)
        def _():
            dgam_ref[...] = jnp.zeros_like(dgam_ref)
            dbet_ref[...] = jnp.zeros_like(dbet_ref)

        for rs in _row_parts(tm):
            dx1 = ALPHA * dr2_ref[rs, :] + _dot(dg_ref[rs, :], wg_ref[...]) + _dot(du_ref[rs, :], wu_ref[...])
            xh, rstd = _ln_stats(r1_ref[rs, :])
            dgam_ref[...] += jnp.sum(dx1 * xh, axis=0, keepdims=True)
            dbet_ref[...] += jnp.sum(dx1, axis=0, keepdims=True)
            dr = _ln_bwd(dx1, xh, rstd, g1_ref[...])
            dr_ref[rs, :] = dr
            drb_ref[rs, :] = dr.astype(bf16)

    row = pl.BlockSpec((tm, D), lambda i: (i, 0))
    wide = pl.BlockSpec((tm, F), lambda i: (i, 0))
    wsp = _resident((F, D))
    vec = pl.BlockSpec((1, D), lambda i: (0, 0))
    return _call(
        body, [dgate, dup, wgT, wuT, dr2, r1, g1], name="ffn_bwd_x", grid=(T // tm,),
        in_specs=[wide, wide, wsp, wsp, row, row, vec],
        out_specs=[row, row, vec, vec], out_shape=[S((T, D), f32), S((T, D), bf16), S((1, D), f32), S((1, D), f32)],
        vmem=VMEM_LIMIT, plan=plan)


def merge_bwd(dmix, w_o, ya, yb, wso, wco, proj, plan):
    T = dmix.shape[0]
    tm = min(512, T)

    def body(dm_ref, wo_ref, ya_ref, yb_ref, wa_ref, wb_ref, ga_ref, gb_ref, dya_ref, dyb_ref, dga_ref, dgb_ref, sa_ref, sb_ref,
             wa_s, wb_s):
        @pl.when(pl.program_id(0) == 0)
        def _():
            _dense_columns(wa_ref, wa_s)
            _dense_columns(wb_ref, wb_s)

        dmer = _dot_nt(dm_ref[...], wo_ref[...])
        sa, sb = _sigmoid(ga_ref[...]), _sigmoid(gb_ref[...])
        dya_ref[...] = (dmer * sa).astype(bf16)
        dyb_ref[...] = (dmer * sb).astype(bf16)
        dga = dmer * _dot(ya_ref[...], wa_s[...]) * sa * (1.0 - sa)
        dgb = dmer * _dot(yb_ref[...], wb_s[...]) * sb * (1.0 - sb)
        dga_ref[...] = dga.astype(bf16)
        dgb_ref[...] = dgb.astype(bf16)
        sa_ref[...] = jnp.sum(dga, axis=0, keepdims=True)
        sb_ref[...] = jnp.sum(dgb, axis=0, keepdims=True)

    act = pl.BlockSpec((tm, W), lambda i: (i, 0))
    osp = pl.BlockSpec((tm, D), lambda i: (i, 0))
    ssp = pl.BlockSpec((None, 1, D), lambda i: (i, 0, 0))
    return _call(
        body, [dmix, w_o, ya, yb, wso, wco, proj, proj], name="merge_bwd", grid=(T // tm,),
        in_specs=[osp, _resident((D, D)), act, act, _resident((NDEV, W, LANE)), _resident((NDEV, W, LANE)),
                  pl.BlockSpec((tm, D), lambda i: (i, 0)), pl.BlockSpec((tm, D), lambda i: (i, 1))],
        out_specs=[osp, osp, osp, osp, ssp, ssp],
        out_shape=[S((T, D), bf16)] * 4 + [S((T // tm, 1, D), f32)] * 2,
        scratch=[pltpu.VMEM((W, D), bf16), pltpu.VMEM((W, D), bf16)], vmem=VMEM_LIMIT, plan=plan)


def branches_bwd_x(dYA, dYB, wso, wco, plan):
    T = dYA.shape[0]
    tm = min(1024, T)

    def body(da_ref, db_ref, wa_ref, wb_ref, oa_ref, ob_ref, wa_s, wb_s):
        @pl.when(pl.program_id(0) == 0)
        def _():
            _dense_columns(wa_ref, wa_s)
            _dense_columns(wb_ref, wb_s)

        oa_ref[...] = _dot_nt(da_ref[...], wa_s[...])
        ob_ref[...] = _dot_nt(db_ref[...], wb_s[...])

    row = pl.BlockSpec((tm, D), lambda i: (i, 0))
    osp = pl.BlockSpec((tm, W), lambda i: (i, 0))
    return _call(
        body, [dYA, dYB, wso, wco], name="branches_bwd_x", grid=(T // tm,),
        in_specs=[row, row, _resident((NDEV, W, LANE)), _resident((NDEV, W, LANE))],
        out_specs=[osp, osp], out_shape=[S((T, W), f32)] * 2,
        scratch=[pltpu.VMEM((W, D), bf16), pltpu.VMEM((W, D), bf16)], vmem=VMEM_LIMIT, plan=plan)


def branch_bwd_w(act, dY, name):
    T = act.shape[0]
    tk = W // 2

    def body(a_ref, d_ref, o_ref):
        res = _dot_tn(a_ref[...], d_ref[...])
        for k in range(NDEV):
            o_ref[k] = res[:, k * LANE:(k + 1) * LANE].astype(o_ref.dtype)

    return pl.pallas_call(
        body, name=name, grid=(W // tk,),
        in_specs=[pl.BlockSpec((T, tk), lambda i: (0, i)), _resident((T, D))],
        out_specs=pl.BlockSpec((NDEV, tk, LANE), lambda i: (0, i, 0)), out_shape=S((NDEV, W, LANE), GRAD_DT),
        compiler_params=_cp(("parallel",), VMEM_LIMIT),
    )(act, dY)


def glu_bwd(yn, dya, glu_w, glu_b, plan):
    T = yn.shape[0]
    tm = min(512, T)

    def body(y_ref, d_ref, w_ref, b_ref, dy_ref, dsp_ref, g_ref, db_ref):
        @pl.when(pl.program_id(0) == 0)
        def _():
            db_ref[...] = jnp.zeros_like(db_ref)

        y, dya_ = y_ref[...], d_ref[...]
        g = _gelu(y)
        gb = g.astype(bf16)
        s = _sigmoid(_dot(gb, w_ref[...]) + b_ref[...])
        dsp = dya_ * g * s * (1.0 - s)
        dspb = dsp.astype(bf16)
        dg = dya_ * s + _dot_nt(dspb, w_ref[...])
        dy_ref[...] = dg * _gelu_grad(y)
        dsp_ref[...] = dspb
        g_ref[...] = gb
        db_ref[...] += jnp.sum(dsp, axis=0, keepdims=True)

    row = pl.BlockSpec((tm, W), lambda i: (i, 0))
    vec = pl.BlockSpec((1, W), lambda i: (0, 0))
    return _call(
        body, [yn, dya, glu_w, glu_b], name="glu_bwd", grid=(T // tm,),
        in_specs=[row, row, pl.BlockSpec((W, W), lambda i: (0, 0)), vec],
        out_specs=[row, row, row, vec], out_shape=[S((T, W), f32), S((T, W), bf16), S((T, W), bf16), S((1, W), f32)],
        sem=("arbitrary",), plan=plan)


def conv_bwd(proj, dyb, conv_w, plan):
    T = proj.shape[0]
    RB = min(512, T)
    nrb = T // RB

    def body(h_ref, c_ref, b_ref, d_ref, w_ref, dh_ref, dc_ref, db_ref, dw_ref, s_ref):
        w0, w1, w2 = w_ref[0:1, :], w_ref[1:2, :], w_ref[2:3, :]

        def blk(i, carry):
            a0, a1, a2, sh, sc, sb = carry
            r0 = pl.multiple_of(i * RB, RB)
            rs = pl.ds(r0, RB)
            h, cg, bg, dyb_ = h_ref[rs, :], c_ref[rs, :], b_ref[rs, :], d_ref[rs, :]
            ch = cg * h
            pr = pl.ds(jnp.maximum(r0 - 8, 0), 8)
            prev = jnp.where(i > 0, c_ref[pr, :] * h_ref[pr, :], 0.0)
            ch1, ch2 = _shift_rows(ch, prev, 1), _shift_rows(ch, prev, 2)
            dbg = dyb_ * (w2 * ch + w1 * ch1 + w0 * ch2)
            db_ref[rs, :] = dbg.astype(bf16)
            dz = dyb_ * bg
            nx = pl.ds(jnp.minimum(r0 + RB, T - 8), 8)
            nxt = jnp.where(i < nrb - 1, d_ref[nx, :] * b_ref[nx, :], 0.0)
            dch = w2 * dz + w1 * _lift_rows(dz, nxt, 1) + w0 * _lift_rows(dz, nxt, 2)
            dcg, dh = dch * h, dch * cg
            dc_ref[rs, :] = dcg.astype(bf16)
            dh_ref[rs, :] = dh.astype(bf16)
            col = lambda v: jnp.sum(v, axis=0, keepdims=True)
            return (a0 + col(dz * ch2), a1 + col(dz * ch1), a2 + col(dz * ch), sh + col(dh), sc + col(dcg), sb + col(dbg))

        zero = jnp.zeros((1, LANE), f32)
        a0, a1, a2, sh, sc, sb = lax.fori_loop(0, nrb, blk, (zero,) * 6)
        dw_ref[0:1, :] = a0
        dw_ref[1:2, :] = a1
        dw_ref[2:3, :] = a2
        s_ref[0:1, :] = sh
        s_ref[1:2, :] = sc
        s_ref[2:3, :] = sb

    nb = W // LANE
    slab = pl.BlockSpec((T, LANE), lambda k: (0, k))
    three = pl.BlockSpec((3, LANE), lambda k: (0, k))
    return _call(
        body, [proj, proj, proj, dyb, conv_w], name="conv_bwd", grid=(nb,),
        in_specs=[pl.BlockSpec((T, LANE), lambda k: (0, 4 * nb + k)), pl.BlockSpec((T, LANE), lambda k: (0, 5 * nb + k)),
                  pl.BlockSpec((T, LANE), lambda k: (0, 6 * nb + k)), slab, three],
        out_specs=[slab, slab, slab, three, three],
        out_shape=[S((T, W), bf16)] * 3 + [S((3, W), f32)] * 2, sem=("parallel",), vmem=VMEM_LIMIT, plan=plan)


def in_proj_bwd_x(parts, win_g, base, scale, name, plan=None):
    T = base.shape[0]
    tm = min(512, T)
    n = len(parts)

    def body(*refs):
        p_refs, w_ref, b_ref, o_ref = refs[:n], refs[n], refs[n + 1], refs[n + 2]
        acc = scale * b_ref[...]
        for p_ref, (_, _, k) in zip(p_refs, parts):
            acc += _dot_nt(p_ref[...], w_ref[k])
        o_ref[...] = acc

    row = pl.BlockSpec((tm, D), lambda i: (i, 0))
    p_specs = [pl.BlockSpec((tm, W), (lambda i, cb=cb: (i, cb))) for _, cb, _ in parts]
    return _call(
        body, [a for a, _, _ in parts] + [win_g, base], name=name, grid=(T // tm,),
        in_specs=p_specs + [_resident((NDEV, D, W)), row],
        out_specs=[row], out_shape=[S((T, D), f32)], vmem=VMEM_LIMIT, plan=plan)


def ssm_param_bwd(lam_re, lam_im, log_dt, fr, fi, br, bi, dwb, dwcT, dlbr, dlbi):
    def body(lr_ref, li_ref, ldt_ref, fr_ref, fi_ref, br_ref, bi_ref, dwb_ref, dwc_ref, dlbr_ref, dlbi_ref,
             dbr_ref, dbi_ref, dlr_ref, dli_ref, dldt_ref, dcr_ref, dci_ref, dr_s, di_s):
        for k in range(W // LANE):
            for gl in range(NG // (W // LANE)):
                rows, src = slice((8 * k + gl) * GC, (8 * k + gl + 1) * GC), slice(gl * GC, (gl + 1) * GC)
                re, im = slice(gl * NP, (gl + 1) * NP), slice(SW + gl * NP, SW + (gl + 1) * NP)
                dr_s[rows, :] = dwb_ref[k, src, re]
                di_s[rows, :] = dwb_ref[k, src, im]
                dcr_ref[rows, :] = dwc_ref[k, src, re]
                dci_ref[rows, :] = -dwc_ref[k, src, im]
        fr_, fi_ = _per_channel(fr_ref[...]), _per_channel(fi_ref[...])
        br_, bi_, dr, di = br_ref[...], bi_ref[...], dr_s[...], di_s[...]
        dbr_ref[...] = fr_ * dr + fi_ * di
        dbi_ref[...] = fr_ * di - fi_ * dr
        dfr = jnp.sum((dr * br_ + di * bi_).reshape(NG, GC, NP), axis=1)
        dfi = jnp.sum((di * br_ - dr * bi_).reshape(NG, GC, NP), axis=1)
        _, vjp = jax.vjp(_disc, lr_ref[...], li_ref[...], ldt_ref[...])
        dlr_ref[...], dli_ref[...], dldt = vjp((dlbr_ref[...], dlbi_ref[...], dfr, dfi))
        dldt_ref[...] = _transpose_exact(dldt)

    blk = S((NG * GC, NP), f32)
    return pl.pallas_call(
        body, name="ssm_param_bwd", out_shape=[blk, blk, S((NG, NP), f32), S((NG, NP), f32), S((1, NG), f32), blk, blk],
        scratch_shapes=[pltpu.VMEM((NG * GC, NP), f32)] * 2)(
        lam_re, lam_im, log_dt, fr, fi, br, bi, dwb, dwcT, dlbr, dlbi)


def _adam(w, g, m, v):
    m = ADAM_B1 * m + (1.0 - ADAM_B1) * g
    v = ADAM_B2 * v + (1.0 - ADAM_B2) * (g * g)
    m_hat = m / (1.0 - ADAM_B1 ** ADAM_STEP)
    v_hat = v / (1.0 - ADAM_B2 ** ADAM_STEP)
    return -ADAM_LR * (m_hat / (jnp.sqrt(v_hat) + ADAM_EPS) + ADAM_WD * w), m, v


def _sum_in_order(c_ref):
    g = c_ref[0].astype(f32)
    for k in range(1, c_ref.shape[0]):
        g = g + c_ref[k].astype(f32)
    return g


def sum_blocks(contrib, name):
    def body(c_ref, o_ref):
        o_ref[...] = _sum_in_order(c_ref)

    return pl.pallas_call(body, name=name, out_shape=S(contrib.shape[1:], f32))(contrib)


def adam_update(w, m, v, contrib, name, rows_per_block=None, summed_on_0=None, plan=None):
    R, C = w.shape
    n = contrib.shape[0]
    tr = min(rows_per_block or R, R)

    def body(w_ref, m_ref, v_ref, c_ref, *refs):
        g_ref, d_ref, nm_ref, nv_ref = refs[-4:]
        g = _sum_in_order(c_ref)
        if summed_on_0 is not None:
            x, y, c = _coords()
            g = jnp.where(4 * x + 2 * y + c == 0, refs[0][...], g)
        g_ref[...] = g
        d_ref[...], nm_ref[...], nv_ref[...] = _adam(w_ref[...], g, m_ref[...], v_ref[...])

    blk = pl.BlockSpec((tr, C), lambda i: (i, 0))
    extra = [] if summed_on_0 is None else [summed_on_0]
    return _call(
        body, [w, m, v, contrib] + extra, name=name, grid=(R // tr,),
        in_specs=[blk, blk, blk, pl.BlockSpec((n, tr, C), lambda i: (0, i, 0))] + [blk] * len(extra),
        out_specs=[blk] * 4, out_shape=[S((R, C), f32)] * 4, sem=("parallel",), vmem=VMEM_LIMIT, plan=plan)


_ROWVEC = (("b_in", IN_COLS), ("ssm_d", W), ("glu_b", W), ("ln1_g", D), ("ln1_b", D), ("ln2_g", D), ("ln2_b", D))
_HALF = NG * GC // 2
_BC_LANE = {"ssm_b_re": 0, "ssm_b_im": NP, "ssm_c_re": 0, "ssm_c_im": NP}
_PACK = {}
_r = 0
for _n, _k in _ROWVEC:
    _PACK[_n] = _r
    _r += _k // LANE
for _n, _rows in (("ssm_lambda", NG), ("scalars", 8), ("ssm_b", _HALF), ("ssm_c", _HALF), ("conv_w", 16)):
    _PACK[_n] = _r
    _r += _rows
for _n in _BC_LANE:
    _PACK[_n] = _PACK[_n[:5]]
PACK_ROWS = _r
assert PACK_ROWS % 8 == 0
_SMALL = ("b_in", "ssm_lambda_re", "ssm_lambda_im", "ssm_log_dt", "ssm_b_re", "ssm_b_im", "ssm_c_re", "ssm_c_im",
          "ssm_d", "glu_b", "ln1_g", "ln1_b", "ln2_g", "ln2_b")


def pack_grads(su, shcb, sga, sgb, dd, dglu_b, dln1_g, dln1_b, dln2_g, dln2_b, dlam_re, dlam_im, dldt, sqerr, dbr, dbi,
               dc_re, dc_im, dconv):
    nI = sga.shape[0]

    def body(su_ref, sh_ref, sga_ref, sgb_ref, dd_ref, gb_ref, l1g_ref, l1b_ref, l2g_ref, l2b_ref, lr_ref, li_ref, dt_ref,
             sq_ref, br_ref, bi_ref, cr_ref, ci_ref, cw_ref, o_ref):
        o_ref[...] = jnp.zeros_like(o_ref)

        def put_row(name, v):
            r0 = _PACK[name]
            for i in range(v.shape[1] // LANE):
                o_ref[r0 + i:r0 + i + 1, :] = v[:, i * LANE:(i + 1) * LANE]

        ga, gb = sga_ref[0], sgb_ref[0]
        for i in range(1, nI):
            ga, gb = ga + sga_ref[i], gb + sgb_ref[i]
        put_row("b_in", jnp.concatenate([su_ref[k] for k in range(W // LANE)]
                                        + [sh_ref[0:1, :], sh_ref[1:2, :], sh_ref[2:3, :], ga, gb], axis=1))
        put_row("ssm_d", jnp.concatenate([dd_ref[k] for k in range(W // LANE)], axis=1))
        put_row("glu_b", gb_ref[...])
        put_row("ln1_g", l1g_ref[...])
        put_row("ln1_b", l1b_ref[...])
        put_row("ln2_g", l2g_ref[...])
        put_row("ln2_b", l2b_ref[...])
        r0 = _PACK["ssm_lambda"]
        o_ref[r0:r0 + NG, 0:NP] = lr_ref[...]
        o_ref[r0:r0 + NG, NP:2 * NP] = li_ref[...]
        r0 = _PACK["scalars"]
        o_ref[r0:r0 + 1, 0:NG] = dt_ref[...]
        o_ref[r0 + 1:r0 + 2, 0:1] = sq_ref[...]
        for name, ref in (("ssm_b_re", br_ref), ("ssm_b_im", bi_ref), ("ssm_c_re", cr_ref), ("ssm_c_im", ci_ref)):
            r0, l0 = _PACK[name], _BC_LANE[name]
            o_ref[r0:r0 + _HALF, l0:l0 + NP] = pltpu.bitcast(ref[...].astype(bf16), f32)
        for cb in range(W // LANE):
            o_ref[_PACK["conv_w"] + 3 * cb:_PACK["conv_w"] + 3 * cb + 3, :] = cw_ref[:, cb * LANE:(cb + 1) * LANE]

    return pl.pallas_call(body, name="pack_grads", out_shape=S((PACK_ROWS, LANE), f32))(
        su, shcb, sga, sgb, dd, dglu_b, dln1_g, dln1_b, dln2_g, dln2_b, dlam_re, dlam_im, dldt, sqerr, dbr, dbi, dc_re, dc_im,
        dconv)


def adam_small(packed_all, params):
    names = list(_SMALL) + ["conv_w"]
    flat = [a for n in names for a in params[n]]

    def body(*refs):
        p_ref = refs[0]
        ins = refs[1:1 + 3 * len(names)]
        outs = refs[1 + 3 * len(names):-2]
        loss_ref, g_ref = refs[-2], refs[-1]

        def part(k, rs=slice(None), ls=slice(None)):
            return p_ref[k, rs, ls]

        g_all = part(0)
        for k in range(1, NDEV):
            g_all = g_all + part(k)
        g_ref[...] = g_all

        def rows(name, r0, n, l0=0, lanes=LANE):
            return g_ref[_PACK[name] + r0:_PACK[name] + r0 + n, l0:l0 + lanes]

        def grad_of(name):
            if name in dict(_ROWVEC):
                return jnp.concatenate([rows(name, i, 1) for i in range(dict(_ROWVEC)[name] // LANE)], axis=1)
            if name in ("ssm_lambda_re", "ssm_lambda_im"):
                return rows("ssm_lambda", 0, NG, NP * (name == "ssm_lambda_im"), NP)[None]
            if name == "ssm_log_dt":
                return rows("scalars", 0, 1, 0, NG)
            if name in _BC_LANE:
                rs, ls = slice(_PACK[name], _PACK[name] + _HALF), slice(_BC_LANE[name], _BC_LANE[name] + NP)
                g = pltpu.bitcast(part(0, rs, ls), bf16).astype(f32)
                for k in range(1, NDEV):
                    g = g + pltpu.bitcast(part(k, rs, ls), bf16).astype(f32)
                return g.reshape(1, NG, GC, NP)
            full = jnp.concatenate([rows("conv_w", 3 * cb, 3) for cb in range(W // LANE)], axis=1)
            x, y, c = _coords()
            col0 = (4 * x + 2 * y + c) * (W // NDEV)
            sel = (lax.broadcasted_iota(jnp.int32, (W, W // NDEV), 0)
                   == lax.broadcasted_iota(jnp.int32, (W, W // NDEV), 1) + col0).astype(f32)
            return jnp.dot(full, sel, precision=HIGHEST, preferred_element_type=f32)[None]

        loss_ref[...] = 0.5 * rows("scalars", 1, 1, 0, 1)
        for i, name in enumerate(names):
            w_ref, m_ref, v_ref = ins[3 * i:3 * i + 3]
            g = grad_of(name)
            d, m, v = _adam(w_ref[...], g, m_ref[...], v_ref[...])
            outs[4 * i][...] = g
            outs[4 * i + 1][...] = d
            outs[4 * i + 2][...] = m
            outs[4 * i + 3][...] = v

    out_shape = [S(params[n][0].shape, f32) for n in names for _ in range(4)] + [S((1, 1), f32)]
    res = pl.pallas_call(body, name="adam_small", out_shape=out_shape, scratch_shapes=[pltpu.VMEM((PACK_ROWS, LANE), f32)],
                         compiler_params=_cp(None, VMEM_LIMIT))(packed_all, *flat)
    return {n: res[4 * i:4 * i + 4] for i, n in enumerate(names)}, res[-1]


def _block_diag(wgt):
    eye = jnp.eye(8, dtype=wgt.dtype)
    out = wgt[:, :, :, None, :] * eye[None, :, None, :, None]
    return out.reshape(4, 8 * wgt.shape[2], 8 * wgt.shape[3])


def kernel(x, w_in, b_in, ssm_lambda_re, ssm_lambda_im, ssm_log_dt, ssm_b_re, ssm_b_im, ssm_c_re, ssm_c_im, ssm_d, glu_w, glu_b, w_ssm_out, conv_w, w_conv_out, w_o, ln1_g, ln1_b, w_gate, w_up, w_down, ln2_g, ln2_b, loss_target, m_w_in, m_b_in, m_ssm_lambda_re, m_ssm_lambda_im, m_ssm_log_dt, m_ssm_b_re, m_ssm_b_im, m_ssm_c_re, m_ssm_c_im, m_ssm_d, m_glu_w, m_glu_b, m_w_ssm_out, m_conv_w, m_w_conv_out, m_w_o, m_ln1_g, m_ln1_b, m_w_gate, m_w_up, m_w_down, m_ln2_g, m_ln2_b, v_w_in, v_b_in, v_ssm_lambda_re, v_ssm_lambda_im, v_ssm_log_dt, v_ssm_b_re, v_ssm_b_im, v_ssm_c_re, v_ssm_c_im, v_ssm_d, v_glu_w, v_glu_b, v_w_ssm_out, v_conv_w, v_w_conv_out, v_w_o, v_ln1_g, v_ln1_b, v_w_gate, v_w_up, v_w_down, v_ln2_g, v_ln2_b):
    given = dict(locals())
    xs = x[0]
    target = loss_target[0]

    tr = lambda a: jnp.swapaxes(a[0], 0, 1)
    win_s, glu_s, wso_s, wco_s, wo_s, wgT_s, wuT_s, wd_s = prep_weights(
        [w_in[0], glu_w[0], w_ssm_out[0], w_conv_out[0], w_o[0], tr(w_gate), tr(w_up), w_down[0]])
    (win_g,) = run_plan(GatherPlan([win_s], srcs=(0,)), "gather_w_in_u")

    lam_re, lam_im = ssm_lambda_re[0], ssm_lambda_im[0]
    ldt = ssm_log_dt[0].reshape(NG, 1)
    br2 = jnp.swapaxes(ssm_b_re[0], 1, 2).reshape(NG * GC, NP)
    bi2 = jnp.swapaxes(ssm_b_im[0], 1, 2).reshape(NG * GC, NP)
    lbr, lbi, fr, fi, bbr, bbi = ssm_params(lam_re, lam_im, ldt, br2, bi2)
    bb_t = lambda b: b.reshape(4, 8, GC, NP)
    wb = jnp.concatenate([_block_diag(bb_t(bbr)), _block_diag(bb_t(bbi))], axis=2)
    c_t = lambda c: c.reshape(4, 8, GC, NP).transpose(0, 1, 3, 2)
    wc = jnp.concatenate([_block_diag(c_t(ssm_c_re[0])), -_block_diag(c_t(ssm_c_im[0]))], axis=1)
    wbT, wcT = wb.transpose(0, 2, 1), wc.transpose(0, 2, 1)
    wb, wc, wbT, wcT = wb.astype(bf16), wc.astype(bf16), wbT.astype(bf16), wcT.astype(bf16)
    lbr_s, lbi_s = lbr.reshape(4, 1, SW), lbi.reshape(4, 1, SW)
    dsk = ssm_d[0].reshape(4, 1, LANE)

    u_nat, xb = in_proj_u(xs, win_g, b_in)
    u_p = to_perm(u_nat, 0, "perm_u")
    half_a, half_b = (0, 3, 5, 6), (1, 2, 4, 7)
    (y_p, xr_p, xi_p), (win_g, conv_g, glu_g, wso_g, wuT_g) = ssm_fwd(
        u_p, wb, wc, lbr_s, lbi_s, dsk,
        Plans([GatherPlan([win_s], srcs=tuple(range(1, NDEV)), into=[win_g]), GatherPlan([conv_w[0], glu_s, wso_s]),
               GatherPlan([wuT_s], srcs=half_a)]))
    conv_f = conv_g.transpose(1, 0, 2).reshape(3, W)
    (proj,), (wco_g, wo_g, wgT_g) = in_proj_rest(
        xb, win_g, b_in, Plans([GatherPlan([wco_s, wo_s]), GatherPlan([wgT_s], srcs=half_a)]))
    glu_f, wo_f = glu_g.reshape(W, W), wo_g.reshape(D, D)
    (yn,), _ = from_perm(y_p, "unperm_y")
    ya = glu_fwd(yn, glu_f, glu_b)
    yb = conv_fwd(proj, conv_f)
    (merged,), (wgT_g,) = merge_fwd(ya, yb, wso_g, wco_g, proj, GatherPlan([wgT_s], srcs=half_b, into=[wgT_g]))
    (r1, x1b), (wuT_g,) = mix_ln1(merged, wo_f, xs, ln1_g, ln1_b, GatherPlan([wuT_s], srcs=half_b, into=[wuT_g]))
    wgT, wuT = wgT_g.reshape(F, D), wuT_g.reshape(F, D)
    (gate, up, hid), (wd_g,) = gate_up(x1b, wgT, wuT, GatherPlan([wd_s]))
    wd_f = wd_g.reshape(F, D)
    dr2, dffn, sqerr, dln2_g, dln2_b = down_loss(hid, wd_f, r1, ln1_g, ln1_b, ln2_g, ln2_b, target)

    dwd, _ = mm_tn_rows(hid, dffn, "grad_w_down")
    dwd = dwd.reshape(NDEV, FS, D)
    (dgate, dup), (r_wd,) = ffn_bwd_act(dffn, wd_f, gate, up, ScatterPlan([dwd], only=half_a))
    dwgT, (r_wd,) = mm_tn_rows(dgate, x1b, "grad_w_gate", plan=ScatterPlan([dwd], only=half_b, into=[r_wd]))
    dwgT = dwgT.reshape(NDEV, FS, D)
    dwuT, (r_wgT,) = mm_tn_rows(dup, x1b, "grad_w_up", plan=ScatterPlan([dwgT], only=half_a))
    dwuT = dwuT.reshape(NDEV, FS, D)
    (dr1, dmix, dln1_g, dln1_b), (r_wgT, r_wuT) = ffn_bwd_x(
        dgate, dup, wgT, wuT, dr2, r1, ln1_g,
        Plans([ScatterPlan([dwgT], only=half_b, into=[r_wgT]), ScatterPlan([dwuT], only=half_a)]))
    (dYA, dYB, dga, dgb, sga, sgb), (r_wuT,) = merge_bwd(dmix, wo_f, ya, yb, wso_g, wco_g, proj,
                                                         ScatterPlan([dwuT], only=half_b, into=[r_wuT]))
    dwo, _ = mm_tn_rows(merged, dmix, "grad_w_o")
    dwo = dwo.reshape(NDEV, D // NDEV, D)
    (dya, dyb), _ = branches_bwd_x(dYA, dYB, wso_g, wco_g, None)
    dwso = branch_bwd_w(ya, dYA, "grad_w_ssm_out")
    dwco = branch_bwd_w(yb, dYB, "grad_w_conv_out")
    (dyn, dsp, gb, dglu_b), (r_wso,) = glu_bwd(yn, dya, glu_f, glu_b, ScatterPlan([dwso]))
    dglu = mm_tn_rows(gb, dsp, "grad_glu_w")[0].reshape(NDEV, W // NDEV, W)
    (dh, dcg, dbg, dconv, shcb), (r_wco,) = conv_bwd(proj, dyb, conv_f, ScatterPlan([dwco]))
    dwin, (r_wo, r_glu) = grad_w_in_rest(xb, dh, dcg, dbg, dga, dgb, ScatterPlan([dwo, dglu]))
    dy_p = to_perm(dyn, 0, "perm_dy")
    (du_p, dwb, dwcT, dlbr_s, dlbi_s, dd, su), (r_win,) = ssm_bwd(
        u_p, dy_p, xr_p, xi_p, wbT, wcT, lbr_s, lbi_s, dsk, ScatterPlan([dwin], only=tuple(range(1, NDEV))))

    dbr2, dbi2, dlam_re, dlam_im, dldt, dc_re, dc_im = ssm_param_bwd(
        lam_re, lam_im, ldt, fr, fi, br2, bi2, dwb, dwcT, dlbr_s.reshape(NG, NP), dlbi_s.reshape(NG, NP))
    packed = pack_grads(su, shcb, sga, sgb, dd, dglu_b, dln1_g, dln1_b, dln2_g, dln2_b, dlam_re, dlam_im, dldt, sqerr,
                        dbr2, dbi2, dc_re, dc_im, dconv)
    (du,), _ = from_perm(du_p, "unperm_du", bf16)
    dwin_u = mm_tn(xb, du, "grad_w_in_u").reshape(NDEV, D // NDEV, W)

    rest = [(dh, 0, 1), (dcg, 0, 2), (dbg, 0, 3), (dga, 0, 4), (dga, 1, 5), (dgb, 0, 6), (dgb, 1, 7)]
    (gx_rest,), (r_win_u, small_all) = in_proj_bwd_x(
        rest, win_g, dr1, ALPHA, "in_proj_bwd_x_rest", Plans([ScatterPlan([dwin_u]), GatherPlan([packed])]))
    my_rows = sum_blocks(r_win_u, "sum_w_in_u")

    out = {}

    def put(name, res, back=lambda a: a[None]):
        out["grad_" + name], out["delta_" + name], out["new_m_" + name], out["new_v_" + name] = [back(r) for r in res]

    res_wd, (win_u_sum,) = adam_update(w_down[0], m_w_down[0], v_w_down[0], r_wd, "adam_w_down", 176,
                                       plan=ScatterPlan([my_rows], only=(0,), whole=True))
    put("w_down", res_wd)
    (grad_x,), _ = in_proj_bwd_x([(du, 0, 0)], win_g, gx_rest, 1.0, "in_proj_bwd_x_u")
    put("w_in", adam_update(w_in[0], m_w_in[0], v_w_in[0], r_win, "adam_w_in", 256,
                            summed_on_0=win_u_sum.reshape(D, W))[0])
    put("glu_w", adam_update(glu_w[0], m_glu_w[0], v_glu_w[0], r_glu, "adam_glu_w")[0])
    put("w_ssm_out", adam_update(w_ssm_out[0], m_w_ssm_out[0], v_w_ssm_out[0], r_wso, "adam_w_ssm_out")[0])
    put("w_conv_out", adam_update(w_conv_out[0], m_w_conv_out[0], v_w_conv_out[0], r_wco, "adam_w_conv_out")[0])
    put("w_o", adam_update(w_o[0], m_w_o[0], v_w_o[0], r_wo, "adam_w_o")[0])
    untr = lambda a: jnp.swapaxes(a, 0, 1)[None]
    put("w_gate", adam_update(tr(w_gate), tr(m_w_gate), tr(v_w_gate), r_wgT, "adam_w_gate", 176)[0], untr)
    put("w_up", adam_update(tr(w_up), tr(m_w_up), tr(v_w_up), r_wuT, "adam_w_up", 176)[0], untr)
    as_c = lambda a: jnp.swapaxes(a, 2, 3)
    params = {n: (given[n], given["m_" + n], given["v_" + n]) for n in list(_SMALL) + ["conv_w"]}
    for n in ("ssm_b_re", "ssm_b_im"):
        params[n] = tuple(as_c(a) for a in params[n])
    small, loss = adam_small(small_all, params)
    for n, res in small.items():
        put(n, res, as_c if n in ("ssm_b_re", "ssm_b_im") else (lambda a: a))

    names = ["w_in", "b_in", "ssm_lambda_re", "ssm_lambda_im", "ssm_log_dt", "ssm_b_re", "ssm_b_im", "ssm_c_re", "ssm_c_im",
             "ssm_d", "glu_w", "glu_b", "w_ssm_out", "conv_w", "w_conv_out", "w_o", "ln1_g", "ln1_b", "w_gate", "w_up",
             "w_down", "ln2_g", "ln2_b"]
    return (loss.reshape(()), grad_x[None], *[out[p + n] for p in ("grad_", "delta_", "new_m_", "new_v_") for n in names])
```

```python
import functools
import math

import jax
import jax.numpy as jnp
from jax import lax
from jax.experimental import pallas as pl
from jax.experimental.pallas import tpu as pltpu

f32, bf16 = jnp.float32, jnp.bfloat16
S = jax.ShapeDtypeStruct
MESH = pl.DeviceIdType.MESH
HIGHEST = lax.Precision.HIGHEST

D = 1024
W = 512
NG, NP, GC = 32, 64, 16
F = 2816
NDEV = 8
FS = F // NDEV
IN_COLS = 8 * W
ALPHA = 2.0 ** 0.25
LN_EPS = 1e-5
ADAM_LR, ADAM_B1, ADAM_B2, ADAM_EPS, ADAM_WD, ADAM_STEP = 0.001, 0.9, 0.999, 1e-08, 0.01, 10
NC = 32
LANE = 128
SW = 4 * LANE
VMEM_LIMIT = 56 * 1024 * 1024
GRAD_DT = bf16
ANY = pl.BlockSpec(memory_space=pl.ANY)


def _cp(sem=None, vmem=None):
    return pltpu.CompilerParams(dimension_semantics=sem, vmem_limit_bytes=vmem)


def _resident(shape):
    return pl.BlockSpec(shape, lambda i: (0,) * len(shape), pipeline_mode=pl.Buffered(1))


def _dot(a, b):
    return jnp.dot(a, b, preferred_element_type=f32)


def _dot_nt(a, b):
    return lax.dot_general(a, b, (((1,), (1,)), ((), ())), preferred_element_type=f32)


def _dot_tn(a, b):
    return lax.dot_general(a, b, (((0,), (0,)), ((), ())), preferred_element_type=f32)


def _eye(n):
    return (lax.broadcasted_iota(jnp.int32, (n, n), 0) == lax.broadcasted_iota(jnp.int32, (n, n), 1)).astype(f32)


def _transpose_exact(a):
    return lax.dot_general(a, _eye(a.shape[0]), (((0,), (0,)), ((), ())), precision=HIGHEST, preferred_element_type=f32)


def _sigmoid(x):
    return 1.0 / (1.0 + jnp.exp(-x))


_GK = math.sqrt(2.0 / math.pi)


def _gelu(x):
    return 0.5 * x * (1.0 + jnp.tanh(_GK * (x + 0.044715 * x * x * x)))


def _gelu_grad(x):
    th = jnp.tanh(_GK * (x + 0.044715 * x * x * x))
    return 0.5 * (1.0 + th) + 0.5 * x * (1.0 - th * th) * _GK * (1.0 + 3.0 * 0.044715 * x * x)


ROW_PART = 256


def _row_parts(tm):
    return [slice(r, r + min(ROW_PART, tm)) for r in range(0, tm, min(ROW_PART, tm))]


def _ln_stats(r):
    mu = jnp.mean(r, axis=-1, keepdims=True)
    xc = r - mu
    var = jnp.mean(xc * xc, axis=-1, keepdims=True)
    rstd = lax.rsqrt(var + LN_EPS)
    return xc * rstd, rstd


def _ln_bwd(dy, xhat, rstd, g):
    dxh = dy * g
    m1 = jnp.mean(dxh, axis=-1, keepdims=True)
    m2 = jnp.mean(dxh * xhat, axis=-1, keepdims=True)
    return rstd * (dxh - m1 - xhat * m2)


def _coords():
    return lax.axis_index("x"), lax.axis_index("y"), lax.axis_index("c")


def _when(cond, fn):
    if cond is True:
        fn()
    else:
        pl.when(cond)(fn)


class GatherPlan:
    aliases = ()

    def __init__(self, arrs, srcs=None, into=None):
        n = self.n = len(arrs)
        self.srcs = srcs
        self.inputs = list(arrs) + list(into or [])
        if into:
            self.aliases = tuple((n + a, a) for a in range(n))
        self.out_shape = [S((NDEV,) + a.shape, a.dtype) for a in arrs]
        self.sems = [pltpu.SemaphoreType.DMA((n, 7)), pltpu.SemaphoreType.DMA((n, 7)), pltpu.SemaphoreType.DMA((n,))]

    def _has(self, dev):
        if self.srcs is None:
            return True
        idx = 4 * dev[0] + 2 * dev[1] + dev[2]
        return functools.reduce(jnp.logical_or, [idx == s for s in self.srcs])

    def _parts(self, ins, outs, sems):
        n = self.n
        send_sems, recv_sems, loc_sems = sems
        x, y, c = _coords()
        me, sib = (x, y, c), (x, y, 1 - c)
        chips = [(1 - x, y), (x, 1 - y), (1 - x, 1 - y)]

        def slot(a, dev):
            return outs[a].at[4 * dev[0] + 2 * dev[1] + dev[2]]

        def copy(a, k, block, to, src=None):
            return pltpu.make_async_remote_copy(
                src_ref=slot(a, block) if src is None else src, dst_ref=slot(a, block),
                send_sem=send_sems.at[a, k], recv_sem=recv_sems.at[a, k], device_id=to, device_id_type=MESH)

        each = [(j, chip, a) for j, chip in enumerate(chips) for a in range(n)]
        own = self._has(me)
        return dict(
            mine=lambda: [(pltpu.make_async_copy(ins[a], slot(a, me), loc_sems.at[a]), own) for a in range(n)],
            first=lambda: ([(copy(a, 0, me, sib, src=ins[a]), own) for a in range(n)]
                           + [(copy(a, 1 + j, me, (*chip, c), src=ins[a]), own) for j, chip, a in each]),
            landed=lambda: [(copy(a, 1 + j, (*chip, c), me), self._has((*chip, c))) for j, chip, a in each],
            passed=lambda: [(copy(a, 4 + j, (*chip, c), sib), self._has((*chip, c))) for j, chip, a in each],
            from_sib=lambda: ([(copy(a, 0, sib, me), self._has(sib)) for a in range(n)]
                              + [(copy(a, 4 + j, (*chip, 1 - c), me), self._has((*chip, 1 - c))) for j, chip, a in each]))

    def start(self, ins, outs, sems):
        p = self._parts(ins, outs, sems)
        for cp, cond in p["mine"]() + p["first"]():
            _when(cond, cp.start)

    def forward(self, ins, outs, sems):
        p = self._parts(ins, outs, sems)
        for (got, cond), (fwd, _) in zip(p["landed"](), p["passed"]()):
            def relay(got=got, fwd=fwd):
                got.wait_recv()
                fwd.start()

            _when(cond, relay)

    def finish(self, ins, outs, sems):
        p = self._parts(ins, outs, sems)
        for cp, cond in p["from_sib"]():
            _when(cond, cp.wait_recv)
        for cp, cond in p["first"]() + p["passed"]():
            _when(cond, cp.wait_send)
        for cp, cond in p["mine"]():
            _when(cond, cp.wait)


class ScatterPlan:
    aliases = ()

    def __init__(self, gs, only=None, into=None, whole=False):
        n = self.n = len(gs)
        self.only = only
        self.whole = whole
        self.inputs = list(gs) + list(into or [])
        if into:
            self.aliases = tuple((n + a, a) for a in range(n))
        self.out_shape = [S((NDEV,) + g.shape if whole else g.shape, g.dtype) for g in gs]
        self.sems = [pltpu.SemaphoreType.DMA((n, 7)), pltpu.SemaphoreType.DMA((n, 7)), pltpu.SemaphoreType.DMA((n,))]

    def _owner(self, idx):
        if self.only is None:
            return True
        return functools.reduce(jnp.logical_or, [idx == b for b in self.only])

    def _copies(self, ins, outs, sems):
        n = self.n
        send_sems, recv_sems, loc_sems = sems
        x, y, c = _coords()
        me = 4 * x + 2 * y + c
        mine = self._owner(me)
        block = (lambda a, k: ins[a]) if self.whole else (lambda a, k: ins[a].at[k])
        copies = [(pltpu.make_async_copy(block(a, me), outs[a].at[me], loc_sems.at[a]), mine, None) for a in range(n)]
        for m in range(1, NDEV):
            px = 1 - x if m & 4 else x
            py = 1 - y if m & 2 else y
            pc = 1 - c if m & 1 else c
            peer = 4 * px + 2 * py + pc
            for a in range(n):
                copies.append((pltpu.make_async_remote_copy(
                    src_ref=block(a, peer), dst_ref=outs[a].at[me],
                    send_sem=send_sems.at[a, m - 1], recv_sem=recv_sems.at[a, m - 1],
                    device_id=(px, py, pc), device_id_type=MESH), self._owner(peer), mine))
        return copies

    def start(self, ins, outs, sems):
        for cp, sends, _ in self._copies(ins, outs, sems):
            _when(sends, cp.start)

    def forward(self, ins, outs, sems):
        pass

    def finish(self, ins, outs, sems):
        for cp, sends, receives in self._copies(ins, outs, sems):
            if receives is None:
                _when(sends, cp.wait)
            else:
                _when(sends, cp.wait_send)
                _when(receives, cp.wait_recv)


class Plans:
    def __init__(self, plans):
        self.plans = plans
        self.inputs = [a for p in plans for a in p.inputs]
        self.out_shape = [s for p in plans for s in p.out_shape]
        self.sems = [s for p in plans for s in p.sems]
        self.aliases, i, o = [], 0, 0
        for p in plans:
            self.aliases += [(i + a, o + b) for a, b in p.aliases]
            i, o = i + len(p.inputs), o + len(p.out_shape)

    def _each(self, what, ins, outs, sems):
        i = o = s = 0
        for p in self.plans:
            ni, no, ns = len(p.inputs), len(p.out_shape), len(p.sems)
            getattr(p, what)(ins[i:i + ni], outs[o:o + no], sems[s:s + ns])
            i, o, s = i + ni, o + no, s + ns

    def start(self, ins, outs, sems):
        self._each("start", ins, outs, sems)

    def forward(self, ins, outs, sems):
        self._each("forward", ins, outs, sems)

    def finish(self, ins, outs, sems):
        self._each("finish", ins, outs, sems)


def _call(body, args, *, name, grid, in_specs, out_specs, out_shape, scratch=(), sem=None, vmem=None, plan=None,
          aliases=None, relay_step=None):
    aliases = aliases or {}
    if plan is None:
        outs = pl.pallas_call(body, name=name, grid=grid, in_specs=list(in_specs), out_specs=list(out_specs),
                              out_shape=list(out_shape), scratch_shapes=list(scratch), input_output_aliases=aliases,
                              compiler_params=_cp(sem, vmem))(*args)
        return list(outs), []
    ni, no, ns = len(in_specs), len(out_specs), len(scratch)
    pi, po = len(plan.inputs), len(plan.out_shape)
    aliases = {**aliases, **{ni + a: no + b for a, b in plan.aliases}}

    def wrapped(*refs):
        main_in, p_in = refs[:ni], refs[ni:ni + pi]
        main_out, p_out = refs[ni + pi:ni + pi + no], refs[ni + pi + no:ni + pi + no + po]
        main_scr, p_sems = refs[ni + pi + no + po:ni + pi + no + po + ns], refs[ni + pi + no + po + ns:]
        ids = [pl.program_id(d) for d in range(len(grid))]
        first = functools.reduce(jnp.logical_and, [i == 0 for i in ids])
        last = functools.reduce(jnp.logical_and, [i == g - 1 for i, g in zip(ids, grid)])

        @pl.when(first)
        def _():
            plan.start(p_in, p_out, p_sems)

        @pl.when(last if relay_step is None else ids[0] == max(relay_step, 0))
        def _():
            plan.forward(p_in, p_out, p_sems)

        body(*main_in, *main_out, *main_scr)

        @pl.when(last)
        def _():
            plan.finish(p_in, p_out, p_sems)

    outs = pl.pallas_call(
        wrapped, name=name, grid=grid, in_specs=list(in_specs) + [ANY] * pi, out_specs=list(out_specs) + [ANY] * po,
        out_shape=list(out_shape) + list(plan.out_shape), scratch_shapes=list(scratch) + list(plan.sems),
        input_output_aliases=aliases, compiler_params=_cp(("arbitrary",) * len(grid), vmem),
    )(*args, *plan.inputs)
    return list(outs[:no]), list(outs[no:])


def run_plan(plan, name):
    def body(*refs):
        ins, outs, sems = refs[:len(plan.inputs)], refs[len(plan.inputs):len(plan.inputs) + len(plan.out_shape)], \
            refs[len(plan.inputs) + len(plan.out_shape):]
        plan.start(ins, outs, sems)
        plan.forward(ins, outs, sems)
        plan.finish(ins, outs, sems)

    return pl.pallas_call(body, name=name, in_specs=[ANY] * len(plan.inputs), out_specs=[ANY] * len(plan.out_shape),
                          out_shape=list(plan.out_shape), scratch_shapes=list(plan.sems))(*plan.inputs)


def mm_tn(a, b, name, tn=512, into=None, block0=0, nblocks=None):
    T, K = a.shape
    N = b.shape[1]
    tn = min(tn, N)
    nblocks = nblocks or (N // tn if into is None else into.shape[0])

    def body(a_ref, b_ref, *rest):
        rest[-1][...] = _dot_tn(a_ref[...], b_ref[...]).astype(GRAD_DT)

    args, in_specs, aliases = [a, b], [_resident((T, K)), pl.BlockSpec((T, tn), lambda j: (0, j))], {}
    if into is not None:
        args.append(into)
        in_specs.append(ANY)
        aliases = {2: 0}
    (out,), _ = _call(body, args, name=name, grid=(N // tn,), in_specs=in_specs,
                      out_specs=[pl.BlockSpec((None, K, tn), lambda j: (block0 + j, 0, 0))],
                      out_shape=[S((nblocks, K, tn), GRAD_DT)], sem=("parallel",), vmem=VMEM_LIMIT, aliases=aliases)
    return out


def grad_w_in_rest(xb, dh, dcg, dbg, dga, dgb, plan):
    T = xb.shape[0]
    order = ((0, 0), (1, 1), (2, 2), (3, 3), (4, 3), (5, 4), (6, 4))

    def body(x_ref, *refs):
        o_ref = refs[-1]
        j = pl.program_id(0)
        for step, opnd in order:
            @pl.when(j == step)
            def _(opnd=opnd):
                o_ref[...] = _dot_tn(x_ref[...], refs[opnd][...]).astype(GRAD_DT)

    once = lambda: pl.BlockSpec((T, W), lambda j: (0, 0), pipeline_mode=pl.Buffered(1))
    (out,), sent = _call(
        body, [xb, dh, dcg, dbg, dga, dgb], name="grad_w_in_rest", grid=(len(order),),
        in_specs=[_resident((T, D)), once(), once(), once(),
                  pl.BlockSpec((T, W), lambda j: (0, jnp.clip(j - 3, 0, 1))),
                  pl.BlockSpec((T, W), lambda j: (0, jnp.clip(j - 5, 0, 1)))],
        out_specs=[pl.BlockSpec((None, D, W), lambda j: (1 + j, 0, 0))],
        out_shape=[S((NDEV, D, W), GRAD_DT)], sem=("arbitrary",), vmem=VMEM_LIMIT, plan=plan)
    return out, sent


def mm_tn_rows(a, b, name, tk=256, plan=None):
    T, K = a.shape
    N = b.shape[1]
    tk = min(tk, K)

    def body(a_ref, b_ref, o_ref):
        o_ref[...] = _dot_tn(a_ref[...], b_ref[...]).astype(GRAD_DT)

    (out,), sent = _call(body, [a, b], name=name, grid=(K // tk,),
                         in_specs=[pl.BlockSpec((T, tk), lambda i: (0, i)), _resident((T, N))],
                         out_specs=[pl.BlockSpec((tk, N), lambda i: (i, 0))], out_shape=[S((K, N), GRAD_DT)],
                         sem=("parallel",), vmem=VMEM_LIMIT, plan=plan)
    return out, sent


def prep_weights(ws, transposed=()):
    def body(*refs):
        for i in range(len(ws)):
            w = refs[i][...]
            refs[len(ws) + i][...] = (w.T if i in transposed else w).astype(bf16)

    return pl.pallas_call(
        body, name="prep_weights",
        out_shape=[S(w.shape[::-1] if i in transposed else w.shape, bf16) for i, w in enumerate(ws)],
        compiler_params=_cp(None, VMEM_LIMIT))(*ws)


REST_BLOCKS = (4, 5, 6, 7, 1, 2, 3)
REST_COLS = len(REST_BLOCKS) * W


def in_proj_u(x, win_g, b_in, plan):
    T = x.shape[0]
    tm = min(1024, T)

    def body(x_ref, w_ref, b_ref, u_ref, xb_ref):
        xb = x_ref[...].astype(bf16)
        xb_ref[...] = xb
        u_ref[...] = _dot(xb, w_ref[...]) + b_ref[...]

    row = pl.BlockSpec((tm, D), lambda i: (i, 0))
    return _call(
        body, [x, win_g, b_in], name="in_proj_u", grid=(T // tm,),
        in_specs=[row, pl.BlockSpec((None, D, W), lambda i: (0, 0, 0)), pl.BlockSpec((1, W), lambda i: (0, 0))],
        out_specs=[pl.BlockSpec((tm, W), lambda i: (i, 0)), row],
        out_shape=[S((T, W), f32), S((T, D), bf16)], sem=("parallel",), vmem=VMEM_LIMIT, plan=plan)


def in_proj_rest(xb, win_g, b_in, plan):
    T = xb.shape[0]
    tm = min(512, T)

    def body(x_ref, w_ref, b_ref, o_ref):
        xb_ = x_ref[...]
        for i, k in enumerate(REST_BLOCKS):
            o_ref[:, i * W:(i + 1) * W] = _dot(xb_, w_ref[k]) + b_ref[:, k * W:(k + 1) * W]

    return _call(
        body, [xb, win_g, b_in], name="in_proj_rest", grid=(T // tm,),
        in_specs=[pl.BlockSpec((tm, D), lambda i: (i, 0)), _resident((NDEV, D, W)), _resident((1, IN_COLS))],
        out_specs=[pl.BlockSpec((tm, REST_COLS), lambda i: (i, 0))],
        out_shape=[S((T, REST_COLS), f32)], vmem=VMEM_LIMIT, plan=plan, relay_step=T // tm - 2)


def to_perm(a, cb0, name):
    T = a.shape[0]
    L = T // NC

    def body(a_ref, o_ref):
        def step(jb, carry):
            j0 = pl.multiple_of(jb * 8, 8)
            for q in range(NC // 8):
                x = jnp.stack([a_ref[pl.ds((8 * q + c) * L + j0, 8), :] for c in range(8)], axis=0)
                y = jnp.swapaxes(x, 0, 1)
                for j in range(8):
                    o_ref[pl.ds((j0 + j) * NC + 8 * q, 8), :] = y[j]
            return carry

        lax.fori_loop(0, L // 8, step, 0)

    return pl.pallas_call(
        body, name=name, grid=(W // LANE,),
        in_specs=[pl.BlockSpec((T, LANE), lambda k: (0, cb0 + k))], out_specs=pl.BlockSpec((T, LANE), lambda k: (0, k)),
        out_shape=S((T, W), f32), compiler_params=_cp(("parallel",), VMEM_LIMIT),
    )(a)


def from_perm(a, name, out_dtype=f32, plan=None):
    T = a.shape[0]
    L = T // NC

    def body(a_ref, o_ref):
        def step(jb, carry):
            j0 = pl.multiple_of(jb * 16, 16)
            for q in range(NC // 8):
                halves = []
                for h in range(2):
                    x = jnp.stack([a_ref[pl.ds((j0 + 8 * h + j) * NC + 8 * q, 8), :] for j in range(8)], axis=0)
                    halves.append(jnp.swapaxes(x, 0, 1))
                for c in range(8):
                    o_ref[pl.ds((8 * q + c) * L + j0, 16), :] = jnp.concatenate(
                        [halves[0][c], halves[1][c]], axis=0).astype(out_dtype)
            return carry

        lax.fori_loop(0, L // 16, step, 0)

    slab = pl.BlockSpec((T, LANE), lambda k: (0, k))
    return _call(body, [a], name=name, grid=(W // LANE,), in_specs=[slab], out_specs=[slab],
                 out_shape=[S((T, W), out_dtype)], sem=("parallel",), vmem=VMEM_LIMIT, plan=plan)


def _disc(lr, li, ldt):
    dt = jnp.exp(ldt)
    mag = jnp.exp(lr * dt)
    lbr = mag * jnp.cos(li * dt)
    lbi = mag * jnp.sin(li * dt)
    den = lr * lr + li * li
    nr = lbr - 1.0
    return lbr, lbi, (nr * lr + lbi * li) / den, (lbi * lr - nr * li) / den


def _per_channel(f):
    return jnp.broadcast_to(f[:, None, :], (NG, GC, NP)).reshape(NG * GC, NP)


def ssm_params(lam_re, lam_im, log_dt, br, bi):
    def body(lr_ref, li_ref, ldt_ref, br_ref, bi_ref, lbr_ref, lbi_ref, fr_ref, fi_ref, bbr_ref, bbi_ref):
        lbr, lbi, fr, fi = _disc(lr_ref[...], li_ref[...], ldt_ref[...])
        lbr_ref[...], lbi_ref[...], fr_ref[...], fi_ref[...] = lbr, lbi, fr, fi
        fr_, fi_, br_, bi_ = _per_channel(fr), _per_channel(fi), br_ref[...], bi_ref[...]
        bbr_ref[...] = fr_ * br_ - fi_ * bi_
        bbi_ref[...] = fr_ * bi_ + fi_ * br_

    return pl.pallas_call(body, name="ssm_params", out_shape=[S((NG, NP), f32)] * 4 + [S((NG * GC, NP), f32)] * 2)(
        lam_re, lam_im, log_dt, br, bi)


SCAN_UNROLL = 4
SCAN_LANES = 2 * LANE


def _steps(n, body, carry):
    main = n // SCAN_UNROLL

    def trip(t, c):
        for q in range(SCAN_UNROLL):
            c = body(t * SCAN_UNROLL + q, c)
        return c

    carry = lax.fori_loop(0, main, trip, carry)
    for i in range(main * SCAN_UNROLL, n):
        carry = body(i, carry)
    return carry


def _scan_body(T):
    L = T // NC
    RB = min(512, T)
    nsq = int(round(math.log2(L)))
    assert 2 ** nsq == L and T % RB == 0 and L % 16 == 0

    def rows(i):
        return pl.ds(pl.multiple_of(i * RB, RB), RB)

    def tile(j):
        return pl.ds(j * NC if isinstance(j, int) else pl.multiple_of(j * NC, NC), NC)

    def forward_states(u_ref, wb_ref, lbr_ref, lbi_ref, sre, sim, ere, eim):
        def bproj(i, carry):
            bu = _dot(u_ref[rows(i), :].astype(bf16), wb_ref[...])
            sre[rows(i), :] = bu[:, :SW]
            sim[rows(i), :] = bu[:, SW:]
            return carry

        lax.fori_loop(0, T // RB, bproj, 0)
        for lb in range(SW // SCAN_LANES):
            ls = slice(lb * SCAN_LANES, (lb + 1) * SCAN_LANES)
            ar = jnp.broadcast_to(lbr_ref[:, ls], (NC, SCAN_LANES))
            ai = jnp.broadcast_to(lbi_ref[:, ls], (NC, SCAN_LANES))

            def step(j, carry):
                xr, xi = carry
                nr = ar * xr - ai * xi + sre[tile(j), ls]
                ni = ar * xi + ai * xr + sim[tile(j), ls]
                sre[tile(j), ls] = nr
                sim[tile(j), ls] = ni
                return nr, ni

            zero = jnp.zeros((NC, SCAN_LANES), f32)
            _steps(L, step, (zero, zero))
            pr, pi = lbr_ref[:, ls], lbi_ref[:, ls]
            for _ in range(nsq):
                pr, pi = pr * pr - pi * pi, 2.0 * pr * pi
            er = jnp.zeros((1, SCAN_LANES), f32)
            ei = er
            ere[0:1, ls] = er
            eim[0:1, ls] = ei
            base = (L - 1) * NC
            for c in range(1, NC):
                lr_ = sre[base + c - 1:base + c, ls]
                li_ = sim[base + c - 1:base + c, ls]
                er, ei = lr_ + pr * er - pi * ei, li_ + pr * ei + pi * er
                ere[c:c + 1, ls] = er
                eim[c:c + 1, ls] = ei
            e_r, e_i = ere[:, ls].reshape(NC // 8, 8, SCAN_LANES), eim[:, ls].reshape(NC // 8, 8, SCAN_LANES)
            ar8, ai8 = ar[0:8], ai[0:8]

            def fix(j, carry):
                pwr, pwi = carry
                xr = sre[tile(j), ls].reshape(NC // 8, 8, SCAN_LANES) + (pwr * e_r - pwi * e_i)
                xi = sim[tile(j), ls].reshape(NC // 8, 8, SCAN_LANES) + (pwr * e_i + pwi * e_r)
                sre[tile(j), ls] = xr.reshape(NC, SCAN_LANES)
                sim[tile(j), ls] = xi.reshape(NC, SCAN_LANES)
                return pwr * ar8 - pwi * ai8, pwr * ai8 + pwi * ar8

            _steps(L, fix, (ar8, ai8))

    return L, RB, nsq, rows, tile, forward_states


def ssm_fwd(u_p, wb, wc, lbr, lbi, dsk, plan):
    T = u_p.shape[0]
    L, RB, nsq, rows, tile, forward_states = _scan_body(T)
    nslab = W // LANE

    def body(u_ref, wb_ref, wc_ref, lbr_ref, lbi_ref, d_ref, y_ref, xr_ref, xi_ref, sre, sim, ere, eim):
        forward_states(u_ref, wb_ref, lbr_ref, lbi_ref, sre, sim, ere, eim)

        def cproj(i, carry):
            xr, xi = sre[rows(i), :].astype(bf16), sim[rows(i), :].astype(bf16)
            xr_ref[rows(i), :] = xr
            xi_ref[rows(i), :] = xi
            y = _dot(xr, wc_ref[0:SW, :]) + _dot(xi, wc_ref[SW:, :])
            y_ref[rows(i), :] = y + d_ref[...] * u_ref[rows(i), :]
            return carry

        lax.fori_loop(0, T // RB, cproj, 0)

    slab = pl.BlockSpec((T, LANE), lambda k: (0, k))
    states = pl.BlockSpec((T, SW), lambda k: (0, k))
    return _call(
        body, [u_p, wb, wc, lbr, lbi, dsk], name="ssm_fwd", grid=(nslab,),
        in_specs=[slab, pl.BlockSpec((None, LANE, 2 * SW), lambda k: (k, 0, 0)),
                  pl.BlockSpec((None, 2 * SW, LANE), lambda k: (k, 0, 0)),
                  pl.BlockSpec((None, 1, SW), lambda k: (k, 0, 0)), pl.BlockSpec((None, 1, SW), lambda k: (k, 0, 0)),
                  pl.BlockSpec((None, 1, LANE), lambda k: (k, 0, 0))],
        out_specs=[slab, states, states], out_shape=[S((T, W), f32), S((T, nslab * SW), bf16), S((T, nslab * SW), bf16)],
        scratch=[pltpu.VMEM((T, SW), f32), pltpu.VMEM((T, SW), f32), pltpu.VMEM((NC, SW), f32), pltpu.VMEM((NC, SW), f32)],
        vmem=VMEM_LIMIT, plan=plan)


def ssm_bwd(u_p, dy_p, xr, xi, wbT, wcT, lbr, lbi, dsk, plan):
    T = u_p.shape[0]
    L, RB, nsq, rows, tile, _ = _scan_body(T)

    def body(u_ref, dy_ref, sre, sim, wbT_ref, wcT_ref, lbr_ref, lbi_ref, d_ref,
             du_ref, dwb_ref, dwc_ref, dlr_ref, dli_ref, dd_ref, su_ref, gre, gim, ere, eim):
        def dstate(i, carry):
            g = _dot(dy_ref[rows(i), :].astype(bf16), wcT_ref[...])
            gre[rows(i), :] = g[:, :SW]
            gim[rows(i), :] = g[:, SW:]
            return carry

        lax.fori_loop(0, T // RB, dstate, 0)
        row = lax.broadcasted_iota(jnp.int32, (NC, SCAN_LANES), 0)
        for lb in range(SW // SCAN_LANES):
            ls = slice(lb * SCAN_LANES, (lb + 1) * SCAN_LANES)
            ar = jnp.broadcast_to(lbr_ref[:, ls], (NC, SCAN_LANES))
            ai = jnp.broadcast_to(lbi_ref[:, ls], (NC, SCAN_LANES))

            def step(i, carry):
                gr, gi = carry
                j = L - 1 - i
                nr = ar * gr + ai * gi + gre[tile(j), ls]
                ni = ar * gi - ai * gr + gim[tile(j), ls]
                gre[tile(j), ls] = nr
                gim[tile(j), ls] = ni
                return nr, ni

            zero = jnp.zeros((NC, SCAN_LANES), f32)
            _steps(L, step, (zero, zero))
            pr, pi = lbr_ref[:, ls], -lbi_ref[:, ls]
            for _ in range(nsq):
                pr, pi = pr * pr - pi * pi, 2.0 * pr * pi
            er = jnp.zeros((1, SCAN_LANES), f32)
            ei = er
            ere[NC - 1:NC, ls] = er
            eim[NC - 1:NC, ls] = ei
            for c in range(NC - 2, -1, -1):
                lr_ = gre[c + 1:c + 2, ls]
                li_ = gim[c + 1:c + 2, ls]
                er, ei = lr_ + pr * er - pi * ei, li_ + pr * ei + pi * er
                ere[c:c + 1, ls] = er
                eim[c:c + 1, ls] = ei
            e_r, e_i = ere[:, ls].reshape(NC // 8, 8, SCAN_LANES), eim[:, ls].reshape(NC // 8, 8, SCAN_LANES)
            ar8, ai8 = ar[0:8], ai[0:8]

            def fixed(j, pwr, pwi):
                gr = (gre[tile(j), ls].reshape(NC // 8, 8, SCAN_LANES) + (pwr * e_r - pwi * e_i)).reshape(NC, SCAN_LANES)
                gi = (gim[tile(j), ls].reshape(NC // 8, 8, SCAN_LANES) + (pwr * e_i + pwi * e_r)).reshape(NC, SCAN_LANES)
                gre[tile(j), ls] = gr
                gim[tile(j), ls] = gi
                return gr, gi

            def fix(i, carry):
                pwr, pwi, accr, acci = carry
                j = L - 1 - i
                gr, gi = fixed(j, pwr, pwi)
                xr, xi = sre[tile(j - 1), ls].astype(f32), sim[tile(j - 1), ls].astype(f32)
                return (pwr * ar8 + pwi * ai8, pwi * ar8 - pwr * ai8,
                        accr + gr * xr + gi * xi, acci + gi * xr - gr * xi)

            pwr, pwi, accr, acci = _steps(L - 1, fix, (ar8, -ai8, zero, zero))
            gr, gi = fixed(0, pwr, pwi)
            xr = jnp.where(row == 0, 0.0, pltpu.roll(sre[tile(L - 1), ls].astype(f32), 1, axis=0))
            xi = jnp.where(row == 0, 0.0, pltpu.roll(sim[tile(L - 1), ls].astype(f32), 1, axis=0))
            accr = accr + gr * xr + gi * xi
            acci = acci + gi * xr - gr * xi
            dlr_ref[:, ls] = jnp.sum(accr, axis=0, keepdims=True)
            dli_ref[:, ls] = jnp.sum(acci, axis=0, keepdims=True)

        dwb_ref[...] = jnp.zeros_like(dwb_ref)
        dwc_ref[...] = jnp.zeros_like(dwc_ref)
        dd_ref[...] = jnp.zeros_like(dd_ref)
        su_ref[...] = jnp.zeros_like(su_ref)

        def finish(i, carry):
            u32, dy32 = u_ref[rows(i), :], dy_ref[rows(i), :]
            ub, dyb = u32.astype(bf16), dy32.astype(bf16)
            gr, gi = gre[rows(i), :].astype(bf16), gim[rows(i), :].astype(bf16)
            du = _dot(gr, wbT_ref[0:SW, :]) + _dot(gi, wbT_ref[SW:, :]) + dy32 * d_ref[...]
            du_ref[rows(i), :] = du
            su_ref[...] += jnp.sum(du, axis=0, keepdims=True)
            dwb_ref[:, 0:SW] += _dot_tn(ub, gr)
            dwb_ref[:, SW:] += _dot_tn(ub, gi)
            dwc_ref[:, 0:SW] += _dot_tn(dyb, sre[rows(i), :])
            dwc_ref[:, SW:] += _dot_tn(dyb, sim[rows(i), :])
            dd_ref[...] += jnp.sum(dy32 * u32, axis=0, keepdims=True)
            return carry

        lax.fori_loop(0, T // RB, finish, 0)

    slab = pl.BlockSpec((T, LANE), lambda k: (0, k))
    wide = pl.BlockSpec((None, LANE, 2 * SW), lambda k: (k, 0, 0))
    tall = pl.BlockSpec((None, 2 * SW, LANE), lambda k: (k, 0, 0))
    vec = pl.BlockSpec((None, 1, SW), lambda k: (k, 0, 0))
    vecd = pl.BlockSpec((None, 1, LANE), lambda k: (k, 0, 0))
    states = pl.BlockSpec((T, SW), lambda k: (0, k))
    nslab = W // LANE
    return _call(
        body, [u_p, dy_p, xr, xi, wbT, wcT, lbr, lbi, dsk], name="ssm_bwd", grid=(nslab,),
        in_specs=[slab, slab, states, states, tall, wide, vec, vec, vecd],
        out_specs=[slab, wide, wide, vec, vec, vecd, vecd],
        out_shape=[S((T, W), f32), S((nslab, LANE, 2 * SW), f32), S((nslab, LANE, 2 * SW), f32),
                   S((nslab, 1, SW), f32), S((nslab, 1, SW), f32), S((nslab, 1, LANE), f32), S((nslab, 1, LANE), f32)],
        scratch=[pltpu.VMEM((T, SW), f32)] * 2 + [pltpu.VMEM((NC, SW), f32)] * 2, vmem=VMEM_LIMIT, plan=plan)


def glu_fwd(yn, glu_w, glu_b):
    T = yn.shape[0]
    tm = min(512, T)

    def body(y_ref, w_ref, b_ref, o_ref):
        g = _gelu(y_ref[...])
        o_ref[...] = (g * _sigmoid(_dot(g.astype(bf16), w_ref[...]) + b_ref[...])).astype(bf16)

    return pl.pallas_call(
        body, name="glu_fwd", grid=(T // tm,),
        in_specs=[pl.BlockSpec((tm, W), lambda i: (i, 0)), pl.BlockSpec((W, W), lambda i: (0, 0)), pl.BlockSpec((1, W), lambda i: (0, 0))],
        out_specs=pl.BlockSpec((tm, W), lambda i: (i, 0)), out_shape=S((T, W), bf16), compiler_params=_cp(("parallel",)),
    )(yn, glu_w, glu_b)


def _shift_rows(cur, prev8, k):
    return pltpu.roll(jnp.concatenate([prev8, cur], axis=0), k, axis=0)[8:]


def _lift_rows(cur, next8, k):
    n = cur.shape[0]
    return pltpu.roll(jnp.concatenate([cur, next8], axis=0), n + 8 - k, axis=0)[:n]


def conv_fwd(proj, conv_w):
    T = proj.shape[0]
    RB = min(512, T)

    def body(h_ref, c_ref, b_ref, w_ref, o_ref):
        w0, w1, w2 = w_ref[0:1, :], w_ref[1:2, :], w_ref[2:3, :]

        def blk(i, carry):
            r0 = pl.multiple_of(i * RB, RB)
            rs = pl.ds(r0, RB)
            ch = c_ref[rs, :] * h_ref[rs, :]
            pr = pl.ds(jnp.maximum(r0 - 8, 0), 8)
            prev = jnp.where(i > 0, c_ref[pr, :] * h_ref[pr, :], 0.0)
            z = w2 * ch + w1 * _shift_rows(ch, prev, 1) + w0 * _shift_rows(ch, prev, 2)
            o_ref[rs, :] = (b_ref[rs, :] * z).astype(bf16)
            return carry

        lax.fori_loop(0, T // RB, blk, 0)

    nb = W // LANE
    return pl.pallas_call(
        body, name="conv_fwd", grid=(nb,),
        in_specs=[pl.BlockSpec((T, LANE), lambda k: (0, 4 * nb + k)), pl.BlockSpec((T, LANE), lambda k: (0, 5 * nb + k)),
                  pl.BlockSpec((T, LANE), lambda k: (0, 6 * nb + k)),pl.BlockSpec((3, LANE), lambda k: (0, k))],
        out_specs=pl.BlockSpec((T, LANE), lambda k: (0, k)), out_shape=S((T, W), bf16),
        compiler_params=_cp(("parallel",), VMEM_LIMIT),
    )(proj, proj, proj, conv_w)


def _dense_columns(blocks_ref, dense_ref):
    for k in range(NDEV):
        dense_ref[:, k * LANE:(k + 1) * LANE] = blocks_ref[k]


def merge_fwd(ya, yb, wso, wco, proj, plan):
    T = ya.shape[0]
    tm = min(1024, T)

    def body(ya_ref, yb_ref, wa_ref, wb_ref, ga_ref, gb_ref, o_ref, wa_s, wb_s):
        @pl.when(pl.program_id(0) == 0)
        def _():
            _dense_columns(wa_ref, wa_s)
            _dense_columns(wb_ref, wb_s)

        o_ref[...] = (_sigmoid(ga_ref[...]) * _dot(ya_ref[...], wa_s[...])
                      + _sigmoid(gb_ref[...]) * _dot(yb_ref[...], wb_s[...])).astype(bf16)

    act = pl.BlockSpec((tm, W), lambda i: (i, 0))
    return _call(
        body, [ya, yb, wso, wco, proj, proj], name="merge_fwd", grid=(T // tm,),
        in_specs=[act, act, _resident((NDEV, W, LANE)), _resident((NDEV, W, LANE)),
                  pl.BlockSpec((tm, D), lambda i: (i, 0)), pl.BlockSpec((tm, D), lambda i: (i, 1))],
        out_specs=[pl.BlockSpec((tm, D), lambda i: (i, 0))], out_shape=[S((T, D), bf16)],
        scratch=[pltpu.VMEM((W, D), bf16), pltpu.VMEM((W, D), bf16)], vmem=VMEM_LIMIT, plan=plan)


def mix_ln1(merged, w_o, x, g1, b1, plan):
    T = x.shape[0]
    tm = min(512, T)

    def body(m_ref, w_ref, x_ref, g_ref, b_ref, r_ref, x1_ref):
        for rs in _row_parts(tm):
            r = ALPHA * x_ref[rs, :] + _dot(m_ref[rs, :], w_ref[...])
            r_ref[rs, :] = r
            xhat, _ = _ln_stats(r)
            x1_ref[rs, :] = (xhat * g_ref[...] + b_ref[...]).astype(bf16)

    row = pl.BlockSpec((tm, D), lambda i: (i, 0))
    vec = pl.BlockSpec((1, D), lambda i: (0, 0))
    return _call(
        body, [merged, w_o, x, g1, b1], name="mix_ln1", grid=(T // tm,),
        in_specs=[row, _resident((D, D)), row, vec, vec],
        out_specs=[row, row], out_shape=[S((T, D), f32), S((T, D), bf16)], sem=("parallel",), vmem=VMEM_LIMIT, plan=plan,
        relay_step=T // tm - 2)


FT = 256


def gate_up(x1b, wgT, wuT, plan):
    T = x1b.shape[0]
    tm = min(512, T)

    def body(x_ref, wg_ref, wu_ref, g_ref, u_ref, h_ref):
        x = x_ref[...]
        for n in range(F // FT):
            cs = slice(n * FT, (n + 1) * FT)
            g = _dot_nt(x, wg_ref[cs, :])
            u = _dot_nt(x, wu_ref[cs, :])
            g_ref[:, cs] = g.astype(bf16)
            u_ref[:, cs] = u.astype(bf16)
            h_ref[:, cs] = (g * _sigmoid(g) * u).astype(bf16)

    osp = pl.BlockSpec((tm, F), lambda i: (i, 0))
    return _call(
        body, [x1b, wgT, wuT], name="gate_up", grid=(T // tm,),
        in_specs=[pl.BlockSpec((tm, D), lambda i: (i, 0)), _resident((F, D)), _resident((F, D))],
        out_specs=[osp, osp, osp], out_shape=[S((T, F), bf16)] * 3, vmem=VMEM_LIMIT, plan=plan, relay_step=T // tm - 3)


def down_loss(hid, w_down, r1, g1, b1, g2, b2, target):
    T = hid.shape[0]
    tm = min(512, T)

    def body(h_ref, w_ref, r1_ref, g1_ref, b1_ref, g2_ref, b2_ref, t_ref, dr_ref, drb_ref, loss_ref, dg_ref, db_ref):
        @pl.when(pl.program_id(0) == 0)
        def _():
            loss_ref[...] = jnp.zeros_like(loss_ref)
            dg_ref[...] = jnp.zeros_like(dg_ref)
            db_ref[...] = jnp.zeros_like(db_ref)

        for rs in _row_parts(tm):
            xh1, _ = _ln_stats(r1_ref[rs, :])
            x1 = xh1 * g1_ref[...] + b1_ref[...]
            r2 = ALPHA * x1 + _dot(h_ref[rs, :], w_ref[...])
            xh2, rstd2 = _ln_stats(r2)
            err = xh2 * g2_ref[...] + b2_ref[...] - t_ref[rs, :]
            loss_ref[...] += jnp.sum(jnp.mean(err * err, axis=-1, keepdims=True), axis=0, keepdims=True)
            dy = err * (1.0 / D)
            dg_ref[...] += jnp.sum(dy * xh2, axis=0, keepdims=True)
            db_ref[...] += jnp.sum(dy, axis=0, keepdims=True)
            dr = _ln_bwd(dy, xh2, rstd2, g2_ref[...])
            dr_ref[rs, :] = dr
            drb_ref[rs, :] = dr.astype(bf16)

    row = pl.BlockSpec((tm, D), lambda i: (i, 0))
    vec = pl.BlockSpec((1, D), lambda i: (0, 0))
    return pl.pallas_call(
        body, name="down_loss", grid=(T // tm,),
        in_specs=[pl.BlockSpec((tm, F), lambda i: (i, 0)), _resident((F, D)), row, vec, vec, vec, vec, row],
        out_specs=[row, row, pl.BlockSpec((1, 1), lambda i: (0, 0)), vec, vec],
        out_shape=[S((T, D), f32), S((T, D), bf16), S((1, 1), f32), S((1, D), f32), S((1, D), f32)],
        compiler_params=_cp(("arbitrary",), VMEM_LIMIT),
    )(hid, w_down, r1, g1, b1, g2, b2, target)


def ffn_bwd_act(dffn, w_down, gate, up, plan):
    T = dffn.shape[0]
    tm = min(512, T)

    def body(d_ref, w_ref, g_ref, u_ref, dg_ref, du_ref):
        for n in range(F // FT):
            cs = slice(n * FT, (n + 1) * FT)
            for rs in _row_parts(tm):
                dh = _dot_nt(d_ref[rs, :], w_ref[cs, :])
                g, u = g_ref[rs, cs].astype(f32), u_ref[rs, cs].astype(f32)
                sg = _sigmoid(g)
                t = g * sg
                du_ref[rs, cs] = (dh * t).astype(bf16)
                dg_ref[rs, cs] = (dh * u * (sg + t - t * sg)).astype(bf16)

    osp = pl.BlockSpec((tm, F), lambda i: (i, 0))
    return _call(
        body, [dffn, w_down, gate, up], name="ffn_bwd_act", grid=(T // tm,),
        in_specs=[pl.BlockSpec((tm, D), lambda i: (i, 0)), _resident((F, D)), osp, osp],
        out_specs=[osp, osp], out_shape=[S((T, F), bf16)] * 2, sem=("parallel",), vmem=VMEM_LIMIT, plan=plan)


def ffn_bwd_x(dgate, dup, wgT, wuT, dr2, r1, g1, plan):
    T = dr2.shape[0]
    tm = min(512, T)

    def body(dg_ref, du_ref, wg_ref, wu_ref, dr2_ref, r1_ref, g1_ref, dr_ref, drb_ref, dgam_ref, dbet_ref):
        @pl.when(pl.program_id(0) == 0)
        def _():
            dgam_ref[...] = jnp.zeros_like(dgam_ref)
            dbet_ref[...] = jnp.zeros_like(dbet_ref)

        for rs in _row_parts(tm):
            dx1 = ALPHA * dr2_ref[rs, :] + _dot(dg_ref[rs, :], wg_ref[...]) + _dot(du_ref[rs, :], wu_ref[...])
            xh, rstd = _ln_stats(r1_ref[rs, :])
            dgam_ref[...] += jnp.sum(dx1 * xh, axis=0, keepdims=True)
            dbet_ref[...] += jnp.sum(dx1, axis=0, keepdims=True)
            dr = _ln_bwd(dx1, xh, rstd, g1_ref[...])
            dr_ref[rs, :] = dr
            drb_ref[rs, :] = dr.astype(bf16)

    row = pl.BlockSpec((tm, D), lambda i: (i, 0))
    wide = pl.BlockSpec((tm, F), lambda i: (i, 0))
    wsp = _resident((F, D))
    vec = pl.BlockSpec((1, D), lambda i: (0, 0))
    return _call(
        body, [dgate, dup, wgT, wuT, dr2, r1, g1], name="ffn_bwd_x", grid=(T // tm,),
        in_specs=[wide, wide, wsp, wsp, row, row, vec],
        out_specs=[row, row, vec, vec], out_shape=[S((T, D), f32), S((T, D), bf16), S((1, D), f32), S((1, D), f32)],
        vmem=VMEM_LIMIT, plan=plan)


def merge_bwd(dmix, w_o, ya, yb, wso, wco, proj, plan):
    T = dmix.shape[0]
    tm = min(512, T)

    def body(dm_ref, wo_ref, ya_ref, yb_ref, wa_ref, wb_ref, ga_ref, gb_ref, dya_ref, dyb_ref, dga_ref, dgb_ref, sa_ref, sb_ref,
             wa_s, wb_s):
        @pl.when(pl.program_id(0) == 0)
        def _():
            _dense_columns(wa_ref, wa_s)
            _dense_columns(wb_ref, wb_s)

        dmer = _dot_nt(dm_ref[...], wo_ref[...])
        sa, sb = _sigmoid(ga_ref[...]), _sigmoid(gb_ref[...])
        dya_ref[...] = (dmer * sa).astype(bf16)
        dyb_ref[...] = (dmer * sb).astype(bf16)
        dga = dmer * _dot(ya_ref[...], wa_s[...]) * sa * (1.0 - sa)
        dgb = dmer * _dot(yb_ref[...], wb_s[...]) * sb * (1.0 - sb)
        dga_ref[...] = dga.astype(bf16)
        dgb_ref[...] = dgb.astype(bf16)
        sa_ref[...] = jnp.sum(dga, axis=0, keepdims=True)
        sb_ref[...] = jnp.sum(dgb, axis=0, keepdims=True)

    act = pl.BlockSpec((tm, W), lambda i: (i, 0))
    osp = pl.BlockSpec((tm, D), lambda i: (i, 0))
    ssp = pl.BlockSpec((None, 1, D), lambda i: (i, 0, 0))
    return _call(
        body, [dmix, w_o, ya, yb, wso, wco, proj, proj], name="merge_bwd", grid=(T // tm,),
        in_specs=[osp, _resident((D, D)), act, act, _resident((NDEV, W, LANE)), _resident((NDEV, W, LANE)),
                  pl.BlockSpec((tm, D), lambda i: (i, 0)), pl.BlockSpec((tm, D), lambda i: (i, 1))],
        out_specs=[osp, osp, osp, osp, ssp, ssp],
        out_shape=[S((T, D), bf16)] * 4 + [S((T // tm, 1, D), f32)] * 2,
        scratch=[pltpu.VMEM((W, D), bf16), pltpu.VMEM((W, D), bf16)], vmem=VMEM_LIMIT, plan=plan)


def branches_bwd_x(dYA, dYB, wso, wco, plan):
    T = dYA.shape[0]
    tm = min(1024, T)

    def body(da_ref, db_ref, wa_ref, wb_ref, oa_ref, ob_ref, wa_s, wb_s):
        @pl.when(pl.program_id(0) == 0)
        def _():
            _dense_columns(wa_ref, wa_s)
            _dense_columns(wb_ref, wb_s)

        oa_ref[...] = _dot_nt(da_ref[...], wa_s[...])
        ob_ref[...] = _dot_nt(db_ref[...], wb_s[...])

    row = pl.BlockSpec((tm, D), lambda i: (i, 0))
    osp = pl.BlockSpec((tm, W), lambda i: (i, 0))
    return _call(
        body, [dYA, dYB, wso, wco], name="branches_bwd_x", grid=(T // tm,),
        in_specs=[row, row, _resident((NDEV, W, LANE)), _resident((NDEV, W, LANE))],
        out_specs=[osp, osp], out_shape=[S((T, W), f32)] * 2,
        scratch=[pltpu.VMEM((W, D), bf16), pltpu.VMEM((W, D), bf16)], vmem=VMEM_LIMIT, plan=plan)


def branch_bwd_w(act, dY, name):
    T = act.shape[0]
    tk = W // 2

    def body(a_ref, d_ref, o_ref):
        res = _dot_tn(a_ref[...], d_ref[...])
        for k in range(NDEV):
            o_ref[k] = res[:, k * LANE:(k + 1) * LANE].astype(o_ref.dtype)

    return pl.pallas_call(
        body, name=name, grid=(W // tk,),
        in_specs=[pl.BlockSpec((T, tk), lambda i: (0, i)), _resident((T, D))],
        out_specs=pl.BlockSpec((NDEV, tk, LANE), lambda i: (0, i, 0)), out_shape=S((NDEV, W, LANE), GRAD_DT),
        compiler_params=_cp(("parallel",), VMEM_LIMIT),
    )(act, dY)


def glu_bwd(yn, dya, glu_w, glu_b, plan):
    T = yn.shape[0]
    tm = min(512, T)

    def body(y_ref, d_ref, w_ref, b_ref, dy_ref, dsp_ref, g_ref, db_ref):
        @pl.when(pl.program_id(0) == 0)
        def _():
            db_ref[...] = jnp.zeros_like(db_ref)

        y, dya_ = y_ref[...], d_ref[...]
        g = _gelu(y)
        gb = g.astype(bf16)
        s = _sigmoid(_dot(gb, w_ref[...]) + b_ref[...])
        dsp = dya_ * g * s * (1.0 - s)
        dspb = dsp.astype(bf16)
        dg = dya_ * s + _dot_nt(dspb, w_ref[...])
        dy_ref[...] = dg * _gelu_grad(y)
        dsp_ref[...] = dspb
        g_ref[...] = gb
        db_ref[...] += jnp.sum(dsp, axis=0, keepdims=True)

    row = pl.BlockSpec((tm, W), lambda i: (i, 0))
    vec = pl.BlockSpec((1, W), lambda i: (0, 0))
    return _call(
        body, [yn, dya, glu_w, glu_b], name="glu_bwd", grid=(T // tm,),
        in_specs=[row, row, pl.BlockSpec((W, W), lambda i: (0, 0)), vec],
        out_specs=[row, row, row, vec], out_shape=[S((T, W), f32), S((T, W), bf16), S((T, W), bf16), S((1, W), f32)],
        sem=("arbitrary",), plan=plan)


def conv_bwd(proj, dyb, conv_w, plan):
    T = proj.shape[0]
    RB = min(512, T)
    nrb = T // RB

    def body(h_ref, c_ref, b_ref, d_ref, w_ref, dh_ref, dc_ref, db_ref, dw_ref, s_ref):
        w0, w1, w2 = w_ref[0:1, :], w_ref[1:2, :], w_ref[2:3, :]

        def blk(i, carry):
            a0, a1, a2, sh, sc, sb = carry
            r0 = pl.multiple_of(i * RB, RB)
            rs = pl.ds(r0, RB)
            h, cg, bg, dyb_ = h_ref[rs, :], c_ref[rs, :], b_ref[rs, :], d_ref[rs, :]
            ch = cg * h
            pr = pl.ds(jnp.maximum(r0 - 8, 0), 8)
            prev = jnp.where(i > 0, c_ref[pr, :] * h_ref[pr, :], 0.0)
            ch1, ch2 = _shift_rows(ch, prev, 1), _shift_rows(ch, prev, 2)
            dbg = dyb_ * (w2 * ch + w1 * ch1 + w0 * ch2)
            db_ref[rs, :] = dbg.astype(bf16)
            dz = dyb_ * bg
            nx = pl.ds(jnp.minimum(r0 + RB, T - 8), 8)
            nxt = jnp.where(i < nrb - 1, d_ref[nx, :] * b_ref[nx, :], 0.0)
            dch = w2 * dz + w1 * _lift_rows(dz, nxt, 1) + w0 * _lift_rows(dz, nxt, 2)
            dcg, dh = dch * h, dch * cg
            dc_ref[rs, :] = dcg.astype(bf16)
            dh_ref[rs, :] = dh.astype(bf16)
            col = lambda v: jnp.sum(v, axis=0, keepdims=True)
            return (a0 + col(dz * ch2), a1 + col(dz * ch1), a2 + col(dz * ch), sh + col(dh), sc + col(dcg), sb + col(dbg))

        zero = jnp.zeros((1, LANE), f32)
        a0, a1, a2, sh, sc, sb = lax.fori_loop(0, nrb, blk, (zero,) * 6)
        dw_ref[0:1, :] = a0
        dw_ref[1:2, :] = a1
        dw_ref[2:3, :] = a2
        s_ref[0:1, :] = sh
        s_ref[1:2, :] = sc
        s_ref[2:3, :] = sb

    nb = W // LANE
    slab = pl.BlockSpec((T, LANE), lambda k: (0, k))
    three = pl.BlockSpec((3, LANE), lambda k: (0, k))
    return _call(
        body, [proj, proj, proj, dyb, conv_w], name="conv_bwd", grid=(nb,),
        in_specs=[pl.BlockSpec((T, LANE), lambda k: (0, 4 * nb + k)), pl.BlockSpec((T, LANE), lambda k: (0, 5 * nb + k)),
                  pl.BlockSpec((T, LANE), lambda k: (0, 6 * nb + k)), slab, three],
        out_specs=[slab, slab, slab, three, three],
        out_shape=[S((T, W), bf16)] * 3 + [S((3, W), f32)] * 2, sem=("parallel",), vmem=VMEM_LIMIT, plan=plan)


def in_proj_bwd_x(parts, win_g, base, scale, name, plan=None):
    T = base.shape[0]
    tm = min(512, T)
    n = len(parts)

    def body(*refs):
        p_refs, w_ref, b_ref, o_ref = refs[:n], refs[n], refs[n + 1], refs[n + 2]
        acc = scale * b_ref[...]
        for p_ref, (_, _, k) in zip(p_refs, parts):
            acc += _dot_nt(p_ref[...], w_ref[k])
        o_ref[...] = acc

    row = pl.BlockSpec((tm, D), lambda i: (i, 0))
    p_specs = [pl.BlockSpec((tm, W), (lambda i, cb=cb: (i, cb))) for _, cb, _ in parts]
    return _call(
        body, [a for a, _, _ in parts] + [win_g, base], name=name, grid=(T // tm,),
        in_specs=p_specs + [_resident((NDEV, D, W)), row],
        out_specs=[row], out_shape=[S((T, D), f32)], vmem=VMEM_LIMIT, plan=plan)


def ssm_param_bwd(lam_re, lam_im, log_dt, fr, fi, br, bi, dwb, dwcT, dlbr, dlbi):
    def body(lr_ref, li_ref, ldt_ref, fr_ref, fi_ref, br_ref, bi_ref, dwb_ref, dwc_ref, dlbr_ref, dlbi_ref,
             dbr_ref, dbi_ref, dlr_ref, dli_ref, dldt_ref, dcr_ref, dci_ref, dr_s, di_s):
        for k in range(W // LANE):
            for gl in range(NG // (W // LANE)):
                rows, src = slice((8 * k + gl) * GC, (8 * k + gl + 1) * GC), slice(gl * GC, (gl + 1) * GC)
                re, im = slice(gl * NP, (gl + 1) * NP), slice(SW + gl * NP, SW + (gl + 1) * NP)
                dr_s[rows, :] = dwb_ref[k, src, re]
                di_s[rows, :] = dwb_ref[k, src, im]
                dcr_ref[rows, :] = dwc_ref[k, src, re]
                dci_ref[rows, :] = -dwc_ref[k, src, im]
        fr_, fi_ = _per_channel(fr_ref[...]), _per_channel(fi_ref[...])
        br_, bi_, dr, di = br_ref[...], bi_ref[...], dr_s[...], di_s[...]
        dbr_ref[...] = fr_ * dr + fi_ * di
        dbi_ref[...] = fr_ * di - fi_ * dr
        dfr = jnp.sum((dr * br_ + di * bi_).reshape(NG, GC, NP), axis=1)
        dfi = jnp.sum((di * br_ - dr * bi_).reshape(NG, GC, NP), axis=1)
        _, vjp = jax.vjp(_disc, lr_ref[...], li_ref[...], ldt_ref[...])
        dlr_ref[...], dli_ref[...], dldt = vjp((dlbr_ref[...], dlbi_ref[...], dfr, dfi))
        dldt_ref[...] = _transpose_exact(dldt)

    blk = S((NG * GC, NP), f32)
    return pl.pallas_call(
        body, name="ssm_param_bwd", out_shape=[blk, blk, S((NG, NP), f32), S((NG, NP), f32), S((1, NG), f32), blk, blk],
        scratch_shapes=[pltpu.VMEM((NG * GC, NP), f32)] * 2)(
        lam_re, lam_im, log_dt, fr, fi, br, bi, dwb, dwcT, dlbr, dlbi)


def _adam(w, g, m, v):
    m = ADAM_B1 * m + (1.0 - ADAM_B1) * g
    v = ADAM_B2 * v + (1.0 - ADAM_B2) * (g * g)
    m_hat = m / (1.0 - ADAM_B1 ** ADAM_STEP)
    v_hat = v / (1.0 - ADAM_B2 ** ADAM_STEP)
    return -ADAM_LR * (m_hat / (jnp.sqrt(v_hat) + ADAM_EPS) + ADAM_WD * w), m, v


def _sum_in_order(c_ref):
    g = c_ref[0].astype(f32)
    for k in range(1, c_ref.shape[0]):
        g = g + c_ref[k].astype(f32)
    return g


def sum_blocks(contrib, name):
    def body(c_ref, o_ref):
        o_ref[...] = _sum_in_order(c_ref)

    return pl.pallas_call(body, name=name, out_shape=S(contrib.shape[1:], f32))(contrib)


def adam_update(w, m, v, contrib, name, rows_per_block=None, summed_on_0=None, plan=None, contrib_t=False):
    R, C = w.shape
    n = contrib.shape[0]
    tr = min(rows_per_block or R, R)

    def body(w_ref, m_ref, v_ref, c_ref, *refs):
        g_ref, d_ref, nm_ref, nv_ref = refs[-4:]
        g = _sum_in_order(c_ref)
        if contrib_t:
            g = g.T
        if summed_on_0 is not None:
            x, y, c = _coords()
            g = jnp.where(4 * x + 2 * y + c == 0, refs[0][...], g)
        g_ref[...] = g
        d_ref[...], nm_ref[...], nv_ref[...] = _adam(w_ref[...], g, m_ref[...], v_ref[...])

    blk = pl.BlockSpec((tr, C), lambda i: (i, 0))
    extra = [] if summed_on_0 is None else [summed_on_0]
    c_spec = pl.BlockSpec((n, C, tr), lambda i: (0, 0, i)) if contrib_t else pl.BlockSpec((n, tr, C), lambda i: (0, i, 0))
    return _call(
        body, [w, m, v, contrib] + extra, name=name, grid=(R // tr,),
        in_specs=[blk, blk, blk, c_spec] + [blk] * len(extra),
        out_specs=[blk] * 4, out_shape=[S((R, C), f32)] * 4, sem=("parallel",), vmem=VMEM_LIMIT, plan=plan)


_ROWVEC = (("b_in", IN_COLS), ("ssm_d", W), ("glu_b", W), ("ln1_g", D), ("ln1_b", D), ("ln2_g", D), ("ln2_b", D))
_HALF = NG * GC // 2
_BC_LANE = {"ssm_b_re": 0, "ssm_b_im": NP, "ssm_c_re": 0, "ssm_c_im": NP}
_PACK = {}
_r = 0
for _n, _k in _ROWVEC:
    _PACK[_n] = _r
    _r += _k // LANE
for _n, _rows in (("ssm_lambda", NG), ("scalars", 8), ("ssm_b", _HALF), ("ssm_c", _HALF), ("conv_w", 16)):
    _PACK[_n] = _r
    _r += _rows
for _n in _BC_LANE:
    _PACK[_n] = _PACK[_n[:5]]
PACK_ROWS = _r
assert PACK_ROWS % 8 == 0
_SMALL = ("b_in", "ssm_lambda_re", "ssm_lambda_im", "ssm_log_dt", "ssm_b_re", "ssm_b_im", "ssm_c_re", "ssm_c_im",
          "ssm_d", "glu_b", "ln1_g", "ln1_b", "ln2_g", "ln2_b")


def pack_grads(su, shcb, sga, sgb, dd, dglu_b, dln1_g, dln1_b, dln2_g, dln2_b, dlam_re, dlam_im, dldt, sqerr, dbr, dbi,
               dc_re, dc_im, dconv):
    nI = sga.shape[0]

    def body(su_ref, sh_ref, sga_ref, sgb_ref, dd_ref, gb_ref, l1g_ref, l1b_ref, l2g_ref, l2b_ref, lr_ref, li_ref, dt_ref,
             sq_ref, br_ref, bi_ref, cr_ref, ci_ref, cw_ref, o_ref):
        o_ref[...] = jnp.zeros_like(o_ref)

        def put_row(name, v):
            r0 = _PACK[name]
            for i in range(v.shape[1] // LANE):
                o_ref[r0 + i:r0 + i + 1, :] = v[:, i * LANE:(i + 1) * LANE]

        ga, gb = sga_ref[0], sgb_ref[0]
        for i in range(1, nI):
            ga, gb = ga + sga_ref[i], gb + sgb_ref[i]
        put_row("b_in", jnp.concatenate([su_ref[k] for k in range(W // LANE)]
                                        + [sh_ref[0:1, :], sh_ref[1:2, :], sh_ref[2:3, :], ga, gb], axis=1))
        put_row("ssm_d", jnp.concatenate([dd_ref[k] for k in range(W // LANE)], axis=1))
        put_row("glu_b", gb_ref[...])
        put_row("ln1_g", l1g_ref[...])
        put_row("ln1_b", l1b_ref[...])
        put_row("ln2_g", l2g_ref[...])
        put_row("ln2_b", l2b_ref[...])
        r0 = _PACK["ssm_lambda"]
        o_ref[r0:r0 + NG, 0:NP] = lr_ref[...]
        o_ref[r0:r0 + NG, NP:2 * NP] = li_ref[...]
        r0 = _PACK["scalars"]
        o_ref[r0:r0 + 1, 0:NG] = dt_ref[...]
        o_ref[r0 + 1:r0 + 2, 0:1] = sq_ref[...]
        for name, ref in (("ssm_b_re", br_ref), ("ssm_b_im", bi_ref), ("ssm_c_re", cr_ref), ("ssm_c_im", ci_ref)):
            r0, l0 = _PACK[name], _BC_LANE[name]
            o_ref[r0:r0 + _HALF, l0:l0 + NP] = pltpu.bitcast(ref[...].astype(bf16), f32)
        for cb in range(W // LANE):
            o_ref[_PACK["conv_w"] + 3 * cb:_PACK["conv_w"] + 3 * cb + 3, :] = cw_ref[:, cb * LANE:(cb + 1) * LANE]

    return pl.pallas_call(body, name="pack_grads", out_shape=S((PACK_ROWS, LANE), f32))(
        su, shcb, sga, sgb, dd, dglu_b, dln1_g, dln1_b, dln2_g, dln2_b, dlam_re, dlam_im, dldt, sqerr, dbr, dbi, dc_re, dc_im,
        dconv)


def adam_small(packed_all, params):
    names = list(_SMALL) + ["conv_w"]
    flat = [a for n in names for a in params[n]]

    def body(*refs):
        p_ref = refs[0]
        ins = refs[1:1 + 3 * len(names)]
        outs = refs[1 + 3 * len(names):-2]
        loss_ref, g_ref = refs[-2], refs[-1]

        def part(k, rs=slice(None), ls=slice(None)):
            return p_ref[k, rs, ls]

        g_all = part(0)
        for k in range(1, NDEV):
            g_all = g_all + part(k)
        g_ref[...] = g_all

        def rows(name, r0, n, l0=0, lanes=LANE):
            return g_ref[_PACK[name] + r0:_PACK[name] + r0 + n, l0:l0 + lanes]

        def grad_of(name):
            if name in dict(_ROWVEC):
                return jnp.concatenate([rows(name, i, 1) for i in range(dict(_ROWVEC)[name] // LANE)], axis=1)
            if name in ("ssm_lambda_re", "ssm_lambda_im"):
                return rows("ssm_lambda", 0, NG, NP * (name == "ssm_lambda_im"), NP)[None]
            if name == "ssm_log_dt":
                return rows("scalars", 0, 1, 0, NG)
            if name in _BC_LANE:
                rs, ls = slice(_PACK[name], _PACK[name] + _HALF), slice(_BC_LANE[name], _BC_LANE[name] + NP)
                g = pltpu.bitcast(part(0, rs, ls), bf16).astype(f32)
                for k in range(1, NDEV):
                    g = g + pltpu.bitcast(part(k, rs, ls), bf16).astype(f32)
                return g.reshape(1, NG, GC, NP)
            full = jnp.concatenate([rows("conv_w", 3 * cb, 3) for cb in range(W // LANE)], axis=1)
            x, y, c = _coords()
            col0 = (4 * x + 2 * y + c) * (W // NDEV)
            sel = (lax.broadcasted_iota(jnp.int32, (W, W // NDEV), 0)
                   == lax.broadcasted_iota(jnp.int32, (W, W // NDEV), 1) + col0).astype(f32)
            return jnp.dot(full, sel, precision=HIGHEST, preferred_element_type=f32)[None]

        loss_ref[...] = 0.5 * rows("scalars", 1, 1, 0, 1)
        for i, name in enumerate(names):
            w_ref, m_ref, v_ref = ins[3 * i:3 * i + 3]
            g = grad_of(name)
            d, m, v = _adam(w_ref[...], g, m_ref[...], v_ref[...])
            outs[4 * i][...] = g
            outs[4 * i + 1][...] = d
            outs[4 * i + 2][...] = m
            outs[4 * i + 3][...] = v

    out_shape = [S(params[n][0].shape, f32) for n in names for _ in range(4)] + [S((1, 1), f32)]
    res = pl.pallas_call(body, name="adam_small", out_shape=out_shape, scratch_shapes=[pltpu.VMEM((PACK_ROWS, LANE), f32)],
                         compiler_params=_cp(None, VMEM_LIMIT))(packed_all, *flat)
    return {n: res[4 * i:4 * i + 4] for i, n in enumerate(names)}, res[-1]


def _block_diag(wgt):
    eye = jnp.eye(8, dtype=wgt.dtype)
    out = wgt[:, :, :, None, :] * eye[None, :, None, :, None]
    return out.reshape(4, 8 * wgt.shape[2], 8 * wgt.shape[3])


def kernel(x, w_in, b_in, ssm_lambda_re, ssm_lambda_im, ssm_log_dt, ssm_b_re, ssm_b_im, ssm_c_re, ssm_c_im, ssm_d, glu_w, glu_b, w_ssm_out, conv_w, w_conv_out, w_o, ln1_g, ln1_b, w_gate, w_up, w_down, ln2_g, ln2_b, loss_target, m_w_in, m_b_in, m_ssm_lambda_re, m_ssm_lambda_im, m_ssm_log_dt, m_ssm_b_re, m_ssm_b_im, m_ssm_c_re, m_ssm_c_im, m_ssm_d, m_glu_w, m_glu_b, m_w_ssm_out, m_conv_w, m_w_conv_out, m_w_o, m_ln1_g, m_ln1_b, m_w_gate, m_w_up, m_w_down, m_ln2_g, m_ln2_b, v_w_in, v_b_in, v_ssm_lambda_re, v_ssm_lambda_im, v_ssm_log_dt, v_ssm_b_re, v_ssm_b_im, v_ssm_c_re, v_ssm_c_im, v_ssm_d, v_glu_w, v_glu_b, v_w_ssm_out, v_conv_w, v_w_conv_out, v_w_o, v_ln1_g, v_ln1_b, v_w_gate, v_w_up, v_w_down, v_ln2_g, v_ln2_b):
    given = dict(locals())
    xs = x[0]
    target = loss_target[0]

    win_s, glu_s, wso_s, wco_s, wo_s, wgT_s, wuT_s, wd_s = prep_weights(
        [w_in[0], glu_w[0], w_ssm_out[0], w_conv_out[0], w_o[0], w_gate[0], w_up[0], w_down[0]], transposed=(5, 6))
    (win_g,) = run_plan(GatherPlan([win_s], srcs=(0,)), "gather_w_in_u")

    lam_re, lam_im = ssm_lambda_re[0], ssm_lambda_im[0]
    ldt = ssm_log_dt[0].reshape(NG, 1)
    br2 = jnp.swapaxes(ssm_b_re[0], 1, 2).reshape(NG * GC, NP)
    bi2 = jnp.swapaxes(ssm_b_im[0], 1, 2).reshape(NG * GC, NP)
    lbr, lbi, fr, fi, bbr, bbi = ssm_params(lam_re, lam_im, ldt, br2, bi2)
    bb_t = lambda b: b.reshape(4, 8, GC, NP)
    wb = jnp.concatenate([_block_diag(bb_t(bbr)), _block_diag(bb_t(bbi))], axis=2)
    c_t = lambda c: c.reshape(4, 8, GC, NP).transpose(0, 1, 3, 2)
    wc = jnp.concatenate([_block_diag(c_t(ssm_c_re[0])), -_block_diag(c_t(ssm_c_im[0]))], axis=1)
    wbT, wcT = wb.transpose(0, 2, 1), wc.transpose(0, 2, 1)
    wb, wc, wbT, wcT = wb.astype(bf16), wc.astype(bf16), wbT.astype(bf16), wcT.astype(bf16)
    lbr_s, lbi_s = lbr.reshape(4, 1, SW), lbi.reshape(4, 1, SW)
    dsk = ssm_d[0].reshape(4, 1, LANE)

    half_a, half_b = (0, 3, 5, 6), (1, 2, 4, 7)
    (u_nat, xb), (conv_g, glu_g, wso_g) = in_proj_u(xs, win_g, b_in, GatherPlan([conv_w[0], glu_s, wso_s]))
    u_p = to_perm(u_nat, 0, "perm_u")
    (y_p, xr_p, xi_p), (win_g, wuT_g) = ssm_fwd(
        u_p, wb, wc, lbr_s, lbi_s, dsk,
        Plans([GatherPlan([win_s], srcs=tuple(range(1, NDEV)), into=[win_g]), GatherPlan([wuT_s], srcs=half_a)]))
    conv_f = conv_g.transpose(1, 0, 2).reshape(3, W)
    (proj,), (wco_g, wo_g, wgT_g) = in_proj_rest(
        xb, win_g, b_in, Plans([GatherPlan([wco_s, wo_s]), GatherPlan([wgT_s], srcs=half_a)]))
    glu_f, wo_f = glu_g.reshape(W, W), wo_g.reshape(D, D)
    (yn,), _ = from_perm(y_p, "unperm_y")
    ya = glu_fwd(yn, glu_f, glu_b)
    yb = conv_fwd(proj, conv_f)
    (merged,), (wgT_g,) = merge_fwd(ya, yb, wso_g, wco_g, proj, GatherPlan([wgT_s], srcs=half_b, into=[wgT_g]))
    (r1, x1b), (wuT_g,) = mix_ln1(merged, wo_f, xs, ln1_g, ln1_b, GatherPlan([wuT_s], srcs=half_b, into=[wuT_g]))
    wgT, wuT = wgT_g.reshape(F, D), wuT_g.reshape(F, D)
    (gate, up, hid), (wd_g,) = gate_up(x1b, wgT, wuT, GatherPlan([wd_s]))
    wd_f = wd_g.reshape(F, D)
    dr2, dffn, sqerr, dln2_g, dln2_b = down_loss(hid, wd_f, r1, ln1_g, ln1_b, ln2_g, ln2_b, target)

    dwd, _ = mm_tn_rows(hid, dffn, "grad_w_down")
    dwd = dwd.reshape(NDEV, FS, D)
    (dgate, dup), (r_wd,) = ffn_bwd_act(dffn, wd_f, gate, up, ScatterPlan([dwd], only=half_a))
    dwgT, (r_wd,) = mm_tn_rows(dgate, x1b, "grad_w_gate", plan=ScatterPlan([dwd], only=half_b, into=[r_wd]))
    dwgT = dwgT.reshape(NDEV, FS, D)
    dwuT, (r_wgT,) = mm_tn_rows(dup, x1b, "grad_w_up", plan=ScatterPlan([dwgT], only=half_a))
    dwuT = dwuT.reshape(NDEV, FS, D)
    (dr1, dmix, dln1_g, dln1_b), (r_wgT, r_wuT) = ffn_bwd_x(
        dgate, dup, wgT, wuT, dr2, r1, ln1_g,
        Plans([ScatterPlan([dwgT], only=half_b, into=[r_wgT]), ScatterPlan([dwuT], only=half_a)]))
    (dYA, dYB, dga, dgb, sga, sgb), (r_wuT,) = merge_bwd(dmix, wo_f, ya, yb, wso_g, wco_g, proj,
                                                         ScatterPlan([dwuT], only=half_b, into=[r_wuT]))
    dwo, _ = mm_tn_rows(merged, dmix, "grad_w_o")
    dwo = dwo.reshape(NDEV, D // NDEV, D)
    (dya, dyb), _ = branches_bwd_x(dYA, dYB, wso_g, wco_g, None)
    dwso = branch_bwd_w(ya, dYA, "grad_w_ssm_out")
    dwco = branch_bwd_w(yb, dYB, "grad_w_conv_out")
    (dyn, dsp, gb, dglu_b), (r_wso,) = glu_bwd(yn, dya, glu_f, glu_b, ScatterPlan([dwso]))
    dglu = mm_tn_rows(gb, dsp, "grad_glu_w")[0].reshape(NDEV, W // NDEV, W)
    (dh, dcg, dbg, dconv, shcb), (r_wco,) = conv_bwd(proj, dyb, conv_f, ScatterPlan([dwco]))
    dwin, (r_wo, r_glu) = grad_w_in_rest(xb, dh, dcg, dbg, dga, dgb, ScatterPlan([dwo, dglu]))
    dy_p = to_perm(dyn, 0, "perm_dy")
    (du_p, dwb, dwcT, dlbr_s, dlbi_s, dd, su), (r_win,) = ssm_bwd(
        u_p, dy_p, xr_p, xi_p, wbT, wcT, lbr_s, lbi_s, dsk, ScatterPlan([dwin], only=tuple(range(1, NDEV))))

    dbr2, dbi2, dlam_re, dlam_im, dldt, dc_re, dc_im = ssm_param_bwd(
        lam_re, lam_im, ldt, fr, fi, br2, bi2, dwb, dwcT, dlbr_s.reshape(NG, NP), dlbi_s.reshape(NG, NP))
    packed = pack_grads(su, shcb, sga, sgb, dd, dglu_b, dln1_g, dln1_b, dln2_g, dln2_b, dlam_re, dlam_im, dldt, sqerr,
                        dbr2, dbi2, dc_re, dc_im, dconv)
    (du,), _ = from_perm(du_p, "unperm_du", bf16)
    dwin_u = mm_tn(xb, du, "grad_w_in_u").reshape(NDEV, D // NDEV, W)

    rest = [(dh, 0, 1), (dcg, 0, 2), (dbg, 0, 3), (dga, 0, 4), (dga, 1, 5), (dgb, 0, 6), (dgb, 1, 7)]
    (gx_rest,), (r_win_u, small_all) = in_proj_bwd_x(
        rest, win_g, dr1, ALPHA, "in_proj_bwd_x_rest", Plans([ScatterPlan([dwin_u]), GatherPlan([packed])]))
    my_rows = sum_blocks(r_win_u, "sum_w_in_u")

    out = {}

    def put(name, res, back=lambda a: a[None]):
        out["grad_" + name], out["delta_" + name], out["new_m_" + name], out["new_v_" + name] = [back(r) for r in res]

    res_wd, (win_u_sum,) = adam_update(w_down[0], m_w_down[0], v_w_down[0], r_wd, "adam_w_down", 176,
                                       plan=ScatterPlan([my_rows], only=(0,), whole=True))
    put("w_down", res_wd)
    (grad_x,), _ = in_proj_bwd_x([(du, 0, 0)], win_g, gx_rest, 1.0, "in_proj_bwd_x_u")
    put("w_in", adam_update(w_in[0], m_w_in[0], v_w_in[0], r_win, "adam_w_in", 256,
                            summed_on_0=win_u_sum.reshape(D, W))[0])
    put("glu_w", adam_update(glu_w[0], m_glu_w[0], v_glu_w[0], r_glu, "adam_glu_w")[0])
    put("w_ssm_out", adam_update(w_ssm_out[0], m_w_ssm_out[0], v_w_ssm_out[0], r_wso, "adam_w_ssm_out")[0])
    put("w_conv_out", adam_update(w_conv_out[0], m_w_conv_out[0], v_w_conv_out[0], r_wco, "adam_w_conv_out")[0])
    put("w_o", adam_update(w_o[0], m_w_o[0], v_w_o[0], r_wo, "adam_w_o")[0])
    put("w_gate", adam_update(w_gate[0], m_w_gate[0], v_w_gate[0], r_wgT, "adam_w_gate", 512, contrib_t=True)[0])
    put("w_up", adam_update(w_up[0], m_w_up[0], v_w_up[0], r_wuT, "adam_w_up", 512, contrib_t=True)[0])
    as_c = lambda a: jnp.swapaxes(a, 2, 3)
    params = {n: (given[n], given["m_" + n], given["v_" + n]) for n in list(_SMALL) + ["conv_w"]}
    for n in ("ssm_b_re", "ssm_b_im"):
        params[n] = tuple(as_c(a) for a in params[n])
    small, loss = adam_small(small_all, params)
    for n, res in small.items():
        put(n, res, as_c if n in ("ssm_b_re", "ssm_b_im") else (lambda a: a))

    names = ["w_in", "b_in", "ssm_lambda_re", "ssm_lambda_im", "ssm_log_dt", "ssm_b_re", "ssm_b_im", "ssm_c_re", "ssm_c_im",
             "ssm_d", "glu_w", "glu_b", "w_ssm_out", "conv_w", "w_conv_out", "w_o", "ln1_g", "ln1_b", "w_gate", "w_up",
             "w_down", "ln2_g", "ln2_b"]
    return (loss.reshape(()), grad_x[None], *[out[p + n] for p in ("grad_", "delta_", "new_m_", "new_v_") for n in names])
```

```python
import functools
import math

import jax
import jax.numpy as jnp
from jax import lax
from jax.experimental import pallas as pl
from jax.experimental.pallas import tpu as pltpu

f32, bf16 = jnp.float32, jnp.bfloat16
S = jax.ShapeDtypeStruct
MESH = pl.DeviceIdType.MESH
HIGHEST = lax.Precision.HIGHEST

D = 1024
W = 512
NG, NP, GC = 32, 64, 16
F = 2816
NDEV = 8
FS = F // NDEV
IN_COLS = 8 * W
ALPHA = 2.0 ** 0.25
LN_EPS = 1e-5
ADAM_LR, ADAM_B1, ADAM_B2, ADAM_EPS, ADAM_WD, ADAM_STEP = 0.001, 0.9, 0.999, 1e-08, 0.01, 10
NC = 32
LANE = 128
SW = 4 * LANE
VMEM_LIMIT = 56 * 1024 * 1024
GRAD_DT = bf16
ANY = pl.BlockSpec(memory_space=pl.ANY)


def _cp(sem=None, vmem=None):
    return pltpu.CompilerParams(dimension_semantics=sem, vmem_limit_bytes=vmem)


def _resident(shape):
    return pl.BlockSpec(shape, lambda i: (0,) * len(shape), pipeline_mode=pl.Buffered(1))


def _dot(a, b):
    return jnp.dot(a, b, preferred_element_type=f32)


def _dot_nt(a, b):
    return lax.dot_general(a, b, (((1,), (1,)), ((), ())), preferred_element_type=f32)


def _dot_tn(a, b):
    return lax.dot_general(a, b, (((0,), (0,)), ((), ())), preferred_element_type=f32)


def _eye(n):
    return (lax.broadcasted_iota(jnp.int32, (n, n), 0) == lax.broadcasted_iota(jnp.int32, (n, n), 1)).astype(f32)


def _transpose_exact(a):
    return lax.dot_general(a, _eye(a.shape[0]), (((0,), (0,)), ((), ())), precision=HIGHEST, preferred_element_type=f32)


def _sigmoid(x):
    return 1.0 / (1.0 + jnp.exp(-x))


_GK = math.sqrt(2.0 / math.pi)


def _gelu(x):
    return 0.5 * x * (1.0 + jnp.tanh(_GK * (x + 0.044715 * x * x * x)))


def _gelu_grad(x):
    th = jnp.tanh(_GK * (x + 0.044715 * x * x * x))
    return 0.5 * (1.0 + th) + 0.5 * x * (1.0 - th * th) * _GK * (1.0 + 3.0 * 0.044715 * x * x)


ROW_PART = 256


def _row_parts(tm):
    return [slice(r, r + min(ROW_PART, tm)) for r in range(0, tm, min(ROW_PART, tm))]


def _ln_stats(r):
    mu = jnp.mean(r, axis=-1, keepdims=True)
    xc = r - mu
    var = jnp.mean(xc * xc, axis=-1, keepdims=True)
    rstd = lax.rsqrt(var + LN_EPS)
    return xc * rstd, rstd


def _ln_bwd(dy, xhat, rstd, g):
    dxh = dy * g
    m1 = jnp.mean(dxh, axis=-1, keepdims=True)
    m2 = jnp.mean(dxh * xhat, axis=-1, keepdims=True)
    return rstd * (dxh - m1 - xhat * m2)


def _coords():
    return lax.axis_index("x"), lax.axis_index("y"), lax.axis_index("c")


def _when(cond, fn):
    if cond is True:
        fn()
    else:
        pl.when(cond)(fn)


class GatherPlan:
    aliases = ()

    def __init__(self, arrs, srcs=None, into=None):
        n = self.n = len(arrs)
        self.srcs = srcs
        self.inputs = list(arrs) + list(into or [])
        if into:
            self.aliases = tuple((n + a, a) for a in range(n))
        self.out_shape = [S((NDEV,) + a.shape, a.dtype) for a in arrs]
        self.sems = [pltpu.SemaphoreType.DMA((n, 7)), pltpu.SemaphoreType.DMA((n, 7)), pltpu.SemaphoreType.DMA((n,))]

    def _has(self, dev):
        if self.srcs is None:
            return True
        idx = 4 * dev[0] + 2 * dev[1] + dev[2]
        return functools.reduce(jnp.logical_or, [idx == s for s in self.srcs])

    def _parts(self, ins, outs, sems):
        n = self.n
        send_sems, recv_sems, loc_sems = sems
        x, y, c = _coords()
        me, sib = (x, y, c), (x, y, 1 - c)
        chips = [(1 - x, y), (x, 1 - y), (1 - x, 1 - y)]

        def slot(a, dev):
            return outs[a].at[4 * dev[0] + 2 * dev[1] + dev[2]]

        def copy(a, k, block, to, src=None):
            return pltpu.make_async_remote_copy(
                src_ref=slot(a, block) if src is None else src, dst_ref=slot(a, block),
                send_sem=send_sems.at[a, k], recv_sem=recv_sems.at[a, k], device_id=to, device_id_type=MESH)

        each = [(j, chip, a) for j, chip in enumerate(chips) for a in range(n)]
        own = self._has(me)
        return dict(
            mine=lambda: [(pltpu.make_async_copy(ins[a], slot(a, me), loc_sems.at[a]), own) for a in range(n)],
            first=lambda: ([(copy(a, 0, me, sib, src=ins[a]), own) for a in range(n)]
                           + [(copy(a, 1 + j, me, (*chip, c), src=ins[a]), own) for j, chip, a in each]),
            landed=lambda: [(copy(a, 1 + j, (*chip, c), me), self._has((*chip, c))) for j, chip, a in each],
            passed=lambda: [(copy(a, 4 + j, (*chip, c), sib), self._has((*chip, c))) for j, chip, a in each],
            from_sib=lambda: ([(copy(a, 0, sib, me), self._has(sib)) for a in range(n)]
                              + [(copy(a, 4 + j, (*chip, 1 - c), me), self._has((*chip, 1 - c))) for j, chip, a in each]))

    def start(self, ins, outs, sems):
        p = self._parts(ins, outs, sems)
        for cp, cond in p["mine"]() + p["first"]():
            _when(cond, cp.start)

    def forward(self, ins, outs, sems):
        p = self._parts(ins, outs, sems)
        for (got, cond), (fwd, _) in zip(p["landed"](), p["passed"]()):
            def relay(got=got, fwd=fwd):
                got.wait_recv()
                fwd.start()

            _when(cond, relay)

    def finish(self, ins, outs, sems):
        p = self._parts(ins, outs, sems)
        for cp, cond in p["from_sib"]():
            _when(cond, cp.wait_recv)
        for cp, cond in p["first"]() + p["passed"]():
            _when(cond, cp.wait_send)
        for cp, cond in p["mine"]():
            _when(cond, cp.wait)


class ScatterPlan:
    aliases = ()

    def __init__(self, gs, only=None, into=None, whole=False):
        n = self.n = len(gs)
        self.only = only
        self.whole = whole
        self.inputs = list(gs) + list(into or [])
        if into:
            self.aliases = tuple((n + a, a) for a in range(n))
        self.out_shape = [S((NDEV,) + g.shape if whole else g.shape, g.dtype) for g in gs]
        self.sems = [pltpu.SemaphoreType.DMA((n, 7)), pltpu.SemaphoreType.DMA((n, 7)), pltpu.SemaphoreType.DMA((n,))]

    def _owner(self, idx):
        if self.only is None:
            return True
        return functools.reduce(jnp.logical_or, [idx == b for b in self.only])

    def _copies(self, ins, outs, sems):
        n = self.n
        send_sems, recv_sems, loc_sems = sems
        x, y, c = _coords()
        me = 4 * x + 2 * y + c
        mine = self._owner(me)
        block = (lambda a, k: ins[a]) if self.whole else (lambda a, k: ins[a].at[k])
        copies = [(pltpu.make_async_copy(block(a, me), outs[a].at[me], loc_sems.at[a]), mine, None) for a in range(n)]
        for m in range(1, NDEV):
            px = 1 - x if m & 4 else x
            py = 1 - y if m & 2 else y
            pc = 1 - c if m & 1 else c
            peer = 4 * px + 2 * py + pc
            for a in range(n):
                copies.append((pltpu.make_async_remote_copy(
                    src_ref=block(a, peer), dst_ref=outs[a].at[me],
                    send_sem=send_sems.at[a, m - 1], recv_sem=recv_sems.at[a, m - 1],
                    device_id=(px, py, pc), device_id_type=MESH), self._owner(peer), mine))
        return copies

    def start(self, ins, outs, sems):
        for cp, sends, _ in self._copies(ins, outs, sems):
            _when(sends, cp.start)

    def forward(self, ins, outs, sems):
        pass

    def finish(self, ins, outs, sems):
        for cp, sends, receives in self._copies(ins, outs, sems):
            if receives is None:
                _when(sends, cp.wait)
            else:
                _when(sends, cp.wait_send)
                _when(receives, cp.wait_recv)


class Plans:
    def __init__(self, plans):
        self.plans = plans
        self.inputs = [a for p in plans for a in p.inputs]
        self.out_shape = [s for p in plans for s in p.out_shape]
        self.sems = [s for p in plans for s in p.sems]
        self.aliases, i, o = [], 0, 0
        for p in plans:
            self.aliases += [(i + a, o + b) for a, b in p.aliases]
            i, o = i + len(p.inputs), o + len(p.out_shape)

    def _each(self, what, ins, outs, sems):
        i = o = s = 0
        for p in self.plans:
            ni, no, ns = len(p.inputs), len(p.out_shape), len(p.sems)
            getattr(p, what)(ins[i:i + ni], outs[o:o + no], sems[s:s + ns])
            i, o, s = i + ni, o + no, s + ns

    def start(self, ins, outs, sems):
        self._each("start", ins, outs, sems)

    def forward(self, ins, outs, sems):
        self._each("forward", ins, outs, sems)

    def finish(self, ins, outs, sems):
        self._each("finish", ins, outs, sems)


def _call(body, args, *, name, grid, in_specs, out_specs, out_shape, scratch=(), sem=None, vmem=None, plan=None,
          aliases=None, relay_step=None):
    aliases = aliases or {}
    if plan is None:
        outs = pl.pallas_call(body, name=name, grid=grid, in_specs=list(in_specs), out_specs=list(out_specs),
                              out_shape=list(out_shape), scratch_shapes=list(scratch), input_output_aliases=aliases,
                              compiler_params=_cp(sem, vmem))(*args)
        return list(outs), []
    ni, no, ns = len(in_specs), len(out_specs), len(scratch)
    pi, po = len(plan.inputs), len(plan.out_shape)
    aliases = {**aliases, **{ni + a: no + b for a, b in plan.aliases}}

    def wrapped(*refs):
        main_in, p_in = refs[:ni], refs[ni:ni + pi]
        main_out, p_out = refs[ni + pi:ni + pi + no], refs[ni + pi + no:ni + pi + no + po]
        main_scr, p_sems = refs[ni + pi + no + po:ni + pi + no + po + ns], refs[ni + pi + no + po + ns:]
        ids = [pl.program_id(d) for d in range(len(grid))]
        first = functools.reduce(jnp.logical_and, [i == 0 for i in ids])
        last = functools.reduce(jnp.logical_and, [i == g - 1 for i, g in zip(ids, grid)])

        @pl.when(first)
        def _():
            plan.start(p_in, p_out, p_sems)

        @pl.when(last if relay_step is None else ids[0] == max(relay_step, 0))
        def _():
            plan.forward(p_in, p_out, p_sems)

        body(*main_in, *main_out, *main_scr)

        @pl.when(last)
        def _():
            plan.finish(p_in, p_out, p_sems)

    outs = pl.pallas_call(
        wrapped, name=name, grid=grid, in_specs=list(in_specs) + [ANY] * pi, out_specs=list(out_specs) + [ANY] * po,
        out_shape=list(out_shape) + list(plan.out_shape), scratch_shapes=list(scratch) + list(plan.sems),
        input_output_aliases=aliases, compiler_params=_cp(("arbitrary",) * len(grid), vmem),
    )(*args, *plan.inputs)
    return list(outs[:no]), list(outs[no:])


def run_plan(plan, name):
    def body(*refs):
        ins, outs, sems = refs[:len(plan.inputs)], refs[len(plan.inputs):len(plan.inputs) + len(plan.out_shape)], \
            refs[len(plan.inputs) + len(plan.out_shape):]
        plan.start(ins, outs, sems)
        plan.forward(ins, outs, sems)
        plan.finish(ins, outs, sems)

    return pl.pallas_call(body, name=name, in_specs=[ANY] * len(plan.inputs), out_specs=[ANY] * len(plan.out_shape),
                          out_shape=list(plan.out_shape), scratch_shapes=list(plan.sems))(*plan.inputs)


def mm_tn(a, b, name, tn=512, into=None, block0=0, nblocks=None):
    T, K = a.shape
    N = b.shape[1]
    tn = min(tn, N)
    nblocks = nblocks or (N // tn if into is None else into.shape[0])

    def body(a_ref, b_ref, *rest):
        rest[-1][...] = _dot_tn(a_ref[...], b_ref[...]).astype(GRAD_DT)

    args, in_specs, aliases = [a, b], [_resident((T, K)), pl.BlockSpec((T, tn), lambda j: (0, j))], {}
    if into is not None:
        args.append(into)
        in_specs.append(ANY)
        aliases = {2: 0}
    (out,), _ = _call(body, args, name=name, grid=(N // tn,), in_specs=in_specs,
                      out_specs=[pl.BlockSpec((None, K, tn), lambda j: (block0 + j, 0, 0))],
                      out_shape=[S((nblocks, K, tn), GRAD_DT)], sem=("parallel",), vmem=VMEM_LIMIT, aliases=aliases)
    return out


def grad_w_in_rest(xb, dh, dcg, dbg, dga, dgb, plan):
    T = xb.shape[0]
    order = ((0, 0), (1, 1), (2, 2), (3, 3), (4, 3), (5, 4), (6, 4))

    def body(x_ref, *refs):
        o_ref = refs[-1]
        j = pl.program_id(0)
        for step, opnd in order:
            @pl.when(j == step)
            def _(opnd=opnd):
                o_ref[...] = _dot_tn(x_ref[...], refs[opnd][...]).astype(GRAD_DT)

    once = lambda: pl.BlockSpec((T, W), lambda j: (0, 0), pipeline_mode=pl.Buffered(1))
    (out,), sent = _call(
        body, [xb, dh, dcg, dbg, dga, dgb], name="grad_w_in_rest", grid=(len(order),),
        in_specs=[_resident((T, D)), once(), once(), once(),
                  pl.BlockSpec((T, W), lambda j: (0, jnp.clip(j - 3, 0, 1))),
                  pl.BlockSpec((T, W), lambda j: (0, jnp.clip(j - 5, 0, 1)))],
        out_specs=[pl.BlockSpec((None, D, W), lambda j: (1 + j, 0, 0))],
        out_shape=[S((NDEV, D, W), GRAD_DT)], sem=("arbitrary",), vmem=VMEM_LIMIT, plan=plan)
    return out, sent


def mm_tn_rows(a, b, name, tk=256, plan=None):
    T, K = a.shape
    N = b.shape[1]
    tk = min(tk, K)

    def body(a_ref, b_ref, o_ref):
        o_ref[...] = _dot_tn(a_ref[...], b_ref[...]).astype(GRAD_DT)

    (out,), sent = _call(body, [a, b], name=name, grid=(K // tk,),
                         in_specs=[pl.BlockSpec((T, tk), lambda i: (0, i)), _resident((T, N))],
                         out_specs=[pl.BlockSpec((tk, N), lambda i: (i, 0))], out_shape=[S((K, N), GRAD_DT)],
                         sem=("parallel",), vmem=VMEM_LIMIT, plan=plan)
    return out, sent


def prep_weights(ws):
    def body(*refs):
        for i in range(len(ws)):
            refs[len(ws) + i][...] = refs[i][...].astype(bf16)

    return pl.pallas_call(body, name="prep_weights", out_shape=[S(w.shape, bf16) for w in ws],
                          compiler_params=_cp(None, VMEM_LIMIT))(*ws)


REST_BLOCKS = (4, 5, 6, 7, 1, 2, 3)
REST_COLS = len(REST_BLOCKS) * W


def in_proj_u(x, win_g, b_in, plan):
    T = x.shape[0]
    tm = min(1024, T)

    def body(x_ref, w_ref, b_ref, u_ref, xb_ref):
        xb = x_ref[...].astype(bf16)
        xb_ref[...] = xb
        u_ref[...] = _dot(xb, w_ref[...]) + b_ref[...]

    row = pl.BlockSpec((tm, D), lambda i: (i, 0))
    return _call(
        body, [x, win_g, b_in], name="in_proj_u", grid=(T // tm,),
        in_specs=[row, pl.BlockSpec((None, D, W), lambda i: (0, 0, 0)), pl.BlockSpec((1, W), lambda i: (0, 0))],
        out_specs=[pl.BlockSpec((tm, W), lambda i: (i, 0)), row],
        out_shape=[S((T, W), f32), S((T, D), bf16)], sem=("parallel",), vmem=VMEM_LIMIT, plan=plan)


def in_proj_rest(xb, win_g, b_in, plan):
    T = xb.shape[0]
    tm = min(512, T)

    def body(x_ref, w_ref, b_ref, o_ref):
        xb_ = x_ref[...]
        for i, k in enumerate(REST_BLOCKS):
            o_ref[:, i * W:(i + 1) * W] = _dot(xb_, w_ref[k]) + b_ref[:, k * W:(k + 1) * W]

    return _call(
        body, [xb, win_g, b_in], name="in_proj_rest", grid=(T // tm,),
        in_specs=[pl.BlockSpec((tm, D), lambda i: (i, 0)), _resident((NDEV, D, W)), _resident((1, IN_COLS))],
        out_specs=[pl.BlockSpec((tm, REST_COLS), lambda i: (i, 0))],
        out_shape=[S((T, REST_COLS), f32)], vmem=VMEM_LIMIT, plan=plan, relay_step=T // tm - 2)


def to_perm(a, cb0, name):
    T = a.shape[0]
    L = T // NC

    def body(a_ref, o_ref):
        def step(jb, carry):
            j0 = pl.multiple_of(jb * 8, 8)
            for q in range(NC // 8):
                x = jnp.stack([a_ref[pl.ds((8 * q + c) * L + j0, 8), :] for c in range(8)], axis=0)
                y = jnp.swapaxes(x, 0, 1)
                for j in range(8):
                    o_ref[pl.ds((j0 + j) * NC + 8 * q, 8), :] = y[j]
            return carry

        lax.fori_loop(0, L // 8, step, 0)

    return pl.pallas_call(
        body, name=name, grid=(W // LANE,),
        in_specs=[pl.BlockSpec((T, LANE), lambda k: (0, cb0 + k))], out_specs=pl.BlockSpec((T, LANE), lambda k: (0, k)),
        out_shape=S((T, W), f32), compiler_params=_cp(("parallel",), VMEM_LIMIT),
    )(a)


def from_perm(a, name, out_dtype=f32, plan=None):
    T = a.shape[0]
    L = T // NC

    def body(a_ref, o_ref):
        def step(jb, carry):
            j0 = pl.multiple_of(jb * 16, 16)
            for q in range(NC // 8):
                halves = []
                for h in range(2):
                    x = jnp.stack([a_ref[pl.ds((j0 + 8 * h + j) * NC + 8 * q, 8), :] for j in range(8)], axis=0)
                    halves.append(jnp.swapaxes(x, 0, 1))
                for c in range(8):
                    o_ref[pl.ds((8 * q + c) * L + j0, 16), :] = jnp.concatenate(
                        [halves[0][c], halves[1][c]], axis=0).astype(out_dtype)
            return carry

        lax.fori_loop(0, L // 16, step, 0)

    slab = pl.BlockSpec((T, LANE), lambda k: (0, k))
    return _call(body, [a], name=name, grid=(W // LANE,), in_specs=[slab], out_specs=[slab],
                 out_shape=[S((T, W), out_dtype)], sem=("parallel",), vmem=VMEM_LIMIT, plan=plan)


def _disc(lr, li, ldt):
    dt = jnp.exp(ldt)
    mag = jnp.exp(lr * dt)
    lbr = mag * jnp.cos(li * dt)
    lbi = mag * jnp.sin(li * dt)
    den = lr * lr + li * li
    nr = lbr - 1.0
    return lbr, lbi, (nr * lr + lbi * li) / den, (lbi * lr - nr * li) / den


def _per_channel(f):
    return jnp.broadcast_to(f[:, None, :], (NG, GC, NP)).reshape(NG * GC, NP)


def ssm_params(lam_re, lam_im, log_dt, br, bi):
    def body(lr_ref, li_ref, ldt_ref, br_ref, bi_ref, lbr_ref, lbi_ref, fr_ref, fi_ref, bbr_ref, bbi_ref):
        lbr, lbi, fr, fi = _disc(lr_ref[...], li_ref[...], ldt_ref[...])
        lbr_ref[...], lbi_ref[...], fr_ref[...], fi_ref[...] = lbr, lbi, fr, fi
        fr_, fi_, br_, bi_ = _per_channel(fr), _per_channel(fi), br_ref[...], bi_ref[...]
        bbr_ref[...] = fr_ * br_ - fi_ * bi_
        bbi_ref[...] = fr_ * bi_ + fi_ * br_

    return pl.pallas_call(body, name="ssm_params", out_shape=[S((NG, NP), f32)] * 4 + [S((NG * GC, NP), f32)] * 2)(
        lam_re, lam_im, log_dt, br, bi)


SCAN_UNROLL = 4
SCAN_LANES = 2 * LANE


def _steps(n, body, carry):
    main = n // SCAN_UNROLL

    def trip(t, c):
        for q in range(SCAN_UNROLL):
            c = body(t * SCAN_UNROLL + q, c)
        return c

    carry = lax.fori_loop(0, main, trip, carry)
    for i in range(main * SCAN_UNROLL, n):
        carry = body(i, carry)
    return carry


def _scan_body(T):
    L = T // NC
    RB = min(512, T)
    nsq = int(round(math.log2(L)))
    assert 2 ** nsq == L and T % RB == 0 and L % 16 == 0

    def rows(i):
        return pl.ds(pl.multiple_of(i * RB, RB), RB)

    def tile(j):
        return pl.ds(j * NC if isinstance(j, int) else pl.multiple_of(j * NC, NC), NC)

    def forward_states(u_ref, wb_ref, lbr_ref, lbi_ref, sre, sim, ere, eim):
        def bproj(i, carry):
            bu = _dot(u_ref[rows(i), :].astype(bf16), wb_ref[...])
            sre[rows(i), :] = bu[:, :SW]
            sim[rows(i), :] = bu[:, SW:]
            return carry

        lax.fori_loop(0, T // RB, bproj, 0)
        for lb in range(SW // SCAN_LANES):
            ls = slice(lb * SCAN_LANES, (lb + 1) * SCAN_LANES)
            ar = jnp.broadcast_to(lbr_ref[:, ls], (NC, SCAN_LANES))
            ai = jnp.broadcast_to(lbi_ref[:, ls], (NC, SCAN_LANES))

            def step(j, carry):
                xr, xi = carry
                nr = ar * xr - ai * xi + sre[tile(j), ls]
                ni = ar * xi + ai * xr + sim[tile(j), ls]
                sre[tile(j), ls] = nr
                sim[tile(j), ls] = ni
                return nr, ni

            zero = jnp.zeros((NC, SCAN_LANES), f32)
            _steps(L, step, (zero, zero))
            pr, pi = lbr_ref[:, ls], lbi_ref[:, ls]
            for _ in range(nsq):
                pr, pi = pr * pr - pi * pi, 2.0 * pr * pi
            er = jnp.zeros((1, SCAN_LANES), f32)
            ei = er
            ere[0:1, ls] = er
            eim[0:1, ls] = ei
            base = (L - 1) * NC
            for c in range(1, NC):
                lr_ = sre[base + c - 1:base + c, ls]
                li_ = sim[base + c - 1:base + c, ls]
                er, ei = lr_ + pr * er - pi * ei, li_ + pr * ei + pi * er
                ere[c:c + 1, ls] = er
                eim[c:c + 1, ls] = ei
            e_r, e_i = ere[:, ls].reshape(NC // 8, 8, SCAN_LANES), eim[:, ls].reshape(NC // 8, 8, SCAN_LANES)
            ar8, ai8 = ar[0:8], ai[0:8]

            def fix(j, carry):
                pwr, pwi = carry
                xr = sre[tile(j), ls].reshape(NC // 8, 8, SCAN_LANES) + (pwr * e_r - pwi * e_i)
                xi = sim[tile(j), ls].reshape(NC // 8, 8, SCAN_LANES) + (pwr * e_i + pwi * e_r)
                sre[tile(j), ls] = xr.reshape(NC, SCAN_LANES)
                sim[tile(j), ls] = xi.reshape(NC, SCAN_LANES)
                return pwr * ar8 - pwi * ai8, pwr * ai8 + pwi * ar8

            _steps(L, fix, (ar8, ai8))

    return L, RB, nsq, rows, tile, forward_states


def ssm_fwd(u_p, wb, wc, lbr, lbi, dsk, plan):
    T = u_p.shape[0]
    L, RB, nsq, rows, tile, forward_states = _scan_body(T)
    nslab = W // LANE

    def body(u_ref, wb_ref, wc_ref, lbr_ref, lbi_ref, d_ref, y_ref, xr_ref, xi_ref, sre, sim, ere, eim):
        forward_states(u_ref, wb_ref, lbr_ref, lbi_ref, sre, sim, ere, eim)

        def cproj(i, carry):
            xr, xi = sre[rows(i), :].astype(bf16), sim[rows(i), :].astype(bf16)
            xr_ref[rows(i), :] = xr
            xi_ref[rows(i), :] = xi
            y = _dot(xr, wc_ref[0:SW, :]) + _dot(xi, wc_ref[SW:, :])
            y_ref[rows(i), :] = y + d_ref[...] * u_ref[rows(i), :]
            return carry

        lax.fori_loop(0, T // RB, cproj, 0)

    slab = pl.BlockSpec((T, LANE), lambda k: (0, k))
    states = pl.BlockSpec((T, SW), lambda k: (0, k))
    return _call(
        body, [u_p, wb, wc, lbr, lbi, dsk], name="ssm_fwd", grid=(nslab,),
        in_specs=[slab, pl.BlockSpec((None, LANE, 2 * SW), lambda k: (k, 0, 0)),
                  pl.BlockSpec((None, 2 * SW, LANE), lambda k: (k, 0, 0)),
                  pl.BlockSpec((None, 1, SW), lambda k: (k, 0, 0)), pl.BlockSpec((None, 1, SW), lambda k: (k, 0, 0)),
                  pl.BlockSpec((None, 1, LANE), lambda k: (k, 0, 0))],
        out_specs=[slab, states, states], out_shape=[S((T, W), f32), S((T, nslab * SW), bf16), S((T, nslab * SW), bf16)],
        scratch=[pltpu.VMEM((T, SW), f32), pltpu.VMEM((T, SW), f32), pltpu.VMEM((NC, SW), f32), pltpu.VMEM((NC, SW), f32)],
        vmem=VMEM_LIMIT, plan=plan)


def ssm_bwd(u_p, dy_p, xr, xi, wbT, wcT, lbr, lbi, dsk, plan):
    T = u_p.shape[0]
    L, RB, nsq, rows, tile, _ = _scan_body(T)

    def body(u_ref, dy_ref, sre, sim, wbT_ref, wcT_ref, lbr_ref, lbi_ref, d_ref,
             du_ref, dwb_ref, dwc_ref, dlr_ref, dli_ref, dd_ref, su_ref, gre, gim, ere, eim):
        def dstate(i, carry):
            g = _dot(dy_ref[rows(i), :].astype(bf16), wcT_ref[...])
            gre[rows(i), :] = g[:, :SW]
            gim[rows(i), :] = g[:, SW:]
            return carry

        lax.fori_loop(0, T // RB, dstate, 0)
        row = lax.broadcasted_iota(jnp.int32, (NC, SCAN_LANES), 0)
        for lb in range(SW // SCAN_LANES):
            ls = slice(lb * SCAN_LANES, (lb + 1) * SCAN_LANES)
            ar = jnp.broadcast_to(lbr_ref[:, ls], (NC, SCAN_LANES))
            ai = jnp.broadcast_to(lbi_ref[:, ls], (NC, SCAN_LANES))

            def step(i, carry):
                gr, gi = carry
                j = L - 1 - i
                nr = ar * gr + ai * gi + gre[tile(j), ls]
                ni = ar * gi - ai * gr + gim[tile(j), ls]
                gre[tile(j), ls] = nr
                gim[tile(j), ls] = ni
                return nr, ni

            zero = jnp.zeros((NC, SCAN_LANES), f32)
            _steps(L, step, (zero, zero))
            pr, pi = lbr_ref[:, ls], -lbi_ref[:, ls]
            for _ in range(nsq):
                pr, pi = pr * pr - pi * pi, 2.0 * pr * pi
            er = jnp.zeros((1, SCAN_LANES), f32)
            ei = er
            ere[NC - 1:NC, ls] = er
            eim[NC - 1:NC, ls] = ei
            for c in range(NC - 2, -1, -1):
                lr_ = gre[c + 1:c + 2, ls]
                li_ = gim[c + 1:c + 2, ls]
                er, ei = lr_ + pr * er - pi * ei, li_ + pr * ei + pi * er
                ere[c:c + 1, ls] = er
                eim[c:c + 1, ls] = ei
            e_r, e_i = ere[:, ls].reshape(NC // 8, 8, SCAN_LANES), eim[:, ls].reshape(NC // 8, 8, SCAN_LANES)
            ar8, ai8 = ar[0:8], ai[0:8]

            def fixed(j, pwr, pwi):
                gr = (gre[tile(j), ls].reshape(NC // 8, 8, SCAN_LANES) + (pwr * e_r - pwi * e_i)).reshape(NC, SCAN_LANES)
                gi = (gim[tile(j), ls].reshape(NC // 8, 8, SCAN_LANES) + (pwr * e_i + pwi * e_r)).reshape(NC, SCAN_LANES)
                gre[tile(j), ls] = gr
                gim[tile(j), ls] = gi
                return gr, gi

            def fix(i, carry):
                pwr, pwi, accr, acci = carry
                j = L - 1 - i
                gr, gi = fixed(j, pwr, pwi)
                xr, xi = sre[tile(j - 1), ls].astype(f32), sim[tile(j - 1), ls].astype(f32)
                return (pwr * ar8 + pwi * ai8, pwi * ar8 - pwr * ai8,
                        accr + gr * xr + gi * xi, acci + gi * xr - gr * xi)

            pwr, pwi, accr, acci = _steps(L - 1, fix, (ar8, -ai8, zero, zero))
            gr, gi = fixed(0, pwr, pwi)
            xr = jnp.where(row == 0, 0.0, pltpu.roll(sre[tile(L - 1), ls].astype(f32), 1, axis=0))
            xi = jnp.where(row == 0, 0.0, pltpu.roll(sim[tile(L - 1), ls].astype(f32), 1, axis=0))
            accr = accr + gr * xr + gi * xi
            acci = acci + gi * xr - gr * xi
            dlr_ref[:, ls] = jnp.sum(accr, axis=0, keepdims=True)
            dli_ref[:, ls] = jnp.sum(acci, axis=0, keepdims=True)

        dwb_ref[...] = jnp.zeros_like(dwb_ref)
        dwc_ref[...] = jnp.zeros_like(dwc_ref)
        dd_ref[...] = jnp.zeros_like(dd_ref)
        su_ref[...] = jnp.zeros_like(su_ref)

        def finish(i, carry):
            u32, dy32 = u_ref[rows(i), :], dy_ref[rows(i), :]
            ub, dyb = u32.astype(bf16), dy32.astype(bf16)
            gr, gi = gre[rows(i), :].astype(bf16), gim[rows(i), :].astype(bf16)
            du = _dot(gr, wbT_ref[0:SW, :]) + _dot(gi, wbT_ref[SW:, :]) + dy32 * d_ref[...]
            du_ref[rows(i), :] = du
            su_ref[...] += jnp.sum(du, axis=0, keepdims=True)
            dwb_ref[:, 0:SW] += _dot_tn(ub, gr)
            dwb_ref[:, SW:] += _dot_tn(ub, gi)
            dwc_ref[:, 0:SW] += _dot_tn(dyb, sre[rows(i), :])
            dwc_ref[:, SW:] += _dot_tn(dyb, sim[rows(i), :])
            dd_ref[...] += jnp.sum(dy32 * u32, axis=0, keepdims=True)
            return carry

        lax.fori_loop(0, T // RB, finish, 0)

    slab = pl.BlockSpec((T, LANE), lambda k: (0, k))
    wide = pl.BlockSpec((None, LANE, 2 * SW), lambda k: (k, 0, 0))
    tall = pl.BlockSpec((None, 2 * SW, LANE), lambda k: (k, 0, 0))
    vec = pl.BlockSpec((None, 1, SW), lambda k: (k, 0, 0))
    vecd = pl.BlockSpec((None, 1, LANE), lambda k: (k, 0, 0))
    states = pl.BlockSpec((T, SW), lambda k: (0, k))
    nslab = W // LANE
    return _call(
        body, [u_p, dy_p, xr, xi, wbT, wcT, lbr, lbi, dsk], name="ssm_bwd", grid=(nslab,),
        in_specs=[slab, slab, states, states, tall, wide, vec, vec, vecd],
        out_specs=[slab, wide, wide, vec, vec, vecd, vecd],
        out_shape=[S((T, W), f32), S((nslab, LANE, 2 * SW), f32), S((nslab, LANE, 2 * SW), f32),
                   S((nslab, 1, SW), f32), S((nslab, 1, SW), f32), S((nslab, 1, LANE), f32), S((nslab, 1, LANE), f32)],
        scratch=[pltpu.VMEM((T, SW), f32)] * 2 + [pltpu.VMEM((NC, SW), f32)] * 2, vmem=VMEM_LIMIT, plan=plan)


def glu_fwd(yn, glu_w, glu_b):
    T = yn.shape[0]
    tm = min(512, T)

    def body(y_ref, w_ref, b_ref, o_ref):
        g = _gelu(y_ref[...])
        o_ref[...] = (g * _sigmoid(_dot(g.astype(bf16), w_ref[...]) + b_ref[...])).astype(bf16)

    return pl.pallas_call(
        body, name="glu_fwd", grid=(T // tm,),
        in_specs=[pl.BlockSpec((tm, W), lambda i: (i, 0)), pl.BlockSpec((W, W), lambda i: (0, 0)), pl.BlockSpec((1, W), lambda i: (0, 0))],
        out_specs=pl.BlockSpec((tm, W), lambda i: (i, 0)), out_shape=S((T, W), bf16), compiler_params=_cp(("parallel",)),
    )(yn, glu_w, glu_b)


def _shift_rows(cur, prev8, k):
    return pltpu.roll(jnp.concatenate([prev8, cur], axis=0), k, axis=0)[8:]


def _lift_rows(cur, next8, k):
    n = cur.shape[0]
    return pltpu.roll(jnp.concatenate([cur, next8], axis=0), n + 8 - k, axis=0)[:n]


def conv_fwd(proj, conv_w):
    T = proj.shape[0]
    RB = min(512, T)

    def body(h_ref, c_ref, b_ref, w_ref, o_ref):
        w0, w1, w2 = w_ref[0:1, :], w_ref[1:2, :], w_ref[2:3, :]

        def blk(i, carry):
            r0 = pl.multiple_of(i * RB, RB)
            rs = pl.ds(r0, RB)
            ch = c_ref[rs, :] * h_ref[rs, :]
            pr = pl.ds(jnp.maximum(r0 - 8, 0), 8)
            prev = jnp.where(i > 0, c_ref[pr, :] * h_ref[pr, :], 0.0)
            z = w2 * ch + w1 * _shift_rows(ch, prev, 1) + w0 * _shift_rows(ch, prev, 2)
            o_ref[rs, :] = (b_ref[rs, :] * z).astype(bf16)
            return carry

        lax.fori_loop(0, T // RB, blk, 0)

    nb = W // LANE
    return pl.pallas_call(
        body, name="conv_fwd", grid=(nb,),
        in_specs=[pl.BlockSpec((T, LANE), lambda k: (0, 4 * nb + k)), pl.BlockSpec((T, LANE), lambda k: (0, 5 * nb + k)),
                  pl.BlockSpec((T, LANE), lambda k: (0, 6 * nb + k)),pl.BlockSpec((3, LANE), lambda k: (0, k))],
        out_specs=pl.BlockSpec((T, LANE), lambda k: (0, k)), out_shape=S((T, W), bf16),
        compiler_params=_cp(("parallel",), VMEM_LIMIT),
    )(proj, proj, proj, conv_w)


def _dense_columns(blocks_ref, dense_ref):
    for k in range(NDEV):
        dense_ref[:, k * LANE:(k + 1) * LANE] = blocks_ref[k]


def merge_fwd(ya, yb, wso, wco, proj, plan):
    T = ya.shape[0]
    tm = min(1024, T)

    def body(ya_ref, yb_ref, wa_ref, wb_ref, ga_ref, gb_ref, o_ref, wa_s, wb_s):
        @pl.when(pl.program_id(0) == 0)
        def _():
            _dense_columns(wa_ref, wa_s)
            _dense_columns(wb_ref, wb_s)

        o_ref[...] = (_sigmoid(ga_ref[...]) * _dot(ya_ref[...], wa_s[...])
                      + _sigmoid(gb_ref[...]) * _dot(yb_ref[...], wb_s[...])).astype(bf16)

    act = pl.BlockSpec((tm, W), lambda i: (i, 0))
    return _call(
        body, [ya, yb, wso, wco, proj, proj], name="merge_fwd", grid=(T // tm,),
        in_specs=[act, act, _resident((NDEV, W, LANE)), _resident((NDEV, W, LANE)),
                  pl.BlockSpec((tm, D), lambda i: (i, 0)), pl.BlockSpec((tm, D), lambda i: (i, 1))],
        out_specs=[pl.BlockSpec((tm, D), lambda i: (i, 0))], out_shape=[S((T, D), bf16)],
        scratch=[pltpu.VMEM((W, D), bf16), pltpu.VMEM((W, D), bf16)], vmem=VMEM_LIMIT, plan=plan)


def mix_ln1(merged, w_o, x, g1, b1, plan):
    T = x.shape[0]
    tm = min(512, T)

    def body(m_ref, w_ref, x_ref, g_ref, b_ref, r_ref, x1_ref):
        for rs in _row_parts(tm):
            r = ALPHA * x_ref[rs, :] + _dot(m_ref[rs, :], w_ref[...])
            r_ref[rs, :] = r
            xhat, _ = _ln_stats(r)
            x1_ref[rs, :] = (xhat * g_ref[...] + b_ref[...]).astype(bf16)

    row = pl.BlockSpec((tm, D), lambda i: (i, 0))
    vec = pl.BlockSpec((1, D), lambda i: (0, 0))
    return _call(
        body, [merged, w_o, x, g1, b1], name="mix_ln1", grid=(T // tm,),
        in_specs=[row, _resident((D, D)), row, vec, vec],
        out_specs=[row, row], out_shape=[S((T, D), f32), S((T, D), bf16)], sem=("parallel",), vmem=VMEM_LIMIT, plan=plan,
        relay_step=T // tm - 2)


FT = 256


def gate_up(x1b, wgT, wuT, plan):
    T = x1b.shape[0]
    tm = min(512, T)

    def body(x_ref, wg_ref, wu_ref, g_ref, u_ref, h_ref):
        x = x_ref[...]
        for n in range(F // FT):
            cs = slice(n * FT, (n + 1) * FT)
            g = _dot_nt(x, wg_ref[cs, :])
            u = _dot_nt(x, wu_ref[cs, :])
            g_ref[:, cs] = g.astype(bf16)
            u_ref[:, cs] = u.astype(bf16)
            h_ref[:, cs] = (g * _sigmoid(g) * u).astype(bf16)

    osp = pl.BlockSpec((tm, F), lambda i: (i, 0))
    return _call(
        body, [x1b, wgT, wuT], name="gate_up", grid=(T // tm,),
        in_specs=[pl.BlockSpec((tm, D), lambda i: (i, 0)), _resident((F, D)), _resident((F, D))],
        out_specs=[osp, osp, osp], out_shape=[S((T, F), bf16)] * 3, vmem=VMEM_LIMIT, plan=plan, relay_step=T // tm - 3)


def down_loss(hid, w_down, r1, g1, b1, g2, b2, target):
    T = hid.shape[0]
    tm = min(512, T)

    def body(h_ref, w_ref, r1_ref, g1_ref, b1_ref, g2_ref, b2_ref, t_ref, dr_ref, drb_ref, loss_ref, dg_ref, db_ref):
        @pl.when(pl.program_id(0) == 0)
        def _():
            loss_ref[...] = jnp.zeros_like(loss_ref)
            dg_ref[...] = jnp.zeros_like(dg_ref)
            db_ref[...] = jnp.zeros_like(db_ref)

        for rs in _row_parts(tm):
            xh1, _ = _ln_stats(r1_ref[rs, :])
            x1 = xh1 * g1_ref[...] + b1_ref[...]
            r2 = ALPHA * x1 + _dot(h_ref[rs, :], w_ref[...])
            xh2, rstd2 = _ln_stats(r2)
            err = xh2 * g2_ref[...] + b2_ref[...] - t_ref[rs, :]
            loss_ref[...] += jnp.sum(jnp.mean(err * err, axis=-1, keepdims=True), axis=0, keepdims=True)
            dy = err * (1.0 / D)
            dg_ref[...] += jnp.sum(dy * xh2, axis=0, keepdims=True)
            db_ref[...] += jnp.sum(dy, axis=0, keepdims=True)
            dr = _ln_bwd(dy, xh2, rstd2, g2_ref[...])
            dr_ref[rs, :] = dr
            drb_ref[rs, :] = dr.astype(bf16)

    row = pl.BlockSpec((tm, D), lambda i: (i, 0))
    vec = pl.BlockSpec((1, D), lambda i: (0, 0))
    return pl.pallas_call(
        body, name="down_loss", grid=(T // tm,),
        in_specs=[pl.BlockSpec((tm, F), lambda i: (i, 0)), _resident((F, D)), row, vec, vec, vec, vec, row],
        out_specs=[row, row, pl.BlockSpec((1, 1), lambda i: (0, 0)), vec, vec],
        out_shape=[S((T, D), f32), S((T, D), bf16), S((1, 1), f32), S((1, D), f32), S((1, D), f32)],
        compiler_params=_cp(("arbitrary",), VMEM_LIMIT),
    )(hid, w_down, r1, g1, b1, g2, b2, target)


def ffn_bwd_act(dffn, w_down, gate, up, plan):
    T = dffn.shape[0]
    tm = min(512, T)

    def body(d_ref, w_ref, g_ref, u_ref, dg_ref, du_ref):
        for n in range(F // FT):
            cs = slice(n * FT, (n + 1) * FT)
            for rs in _row_parts(tm):
                dh = _dot_nt(d_ref[rs, :], w_ref[cs, :])
                g, u = g_ref[rs, cs].astype(f32), u_ref[rs, cs].astype(f32)
                sg = _sigmoid(g)
                t = g * sg
                du_ref[rs, cs] = (dh * t).astype(bf16)
                dg_ref[rs, cs] = (dh * u * (sg + t - t * sg)).astype(bf16)

    osp = pl.BlockSpec((tm, F), lambda i: (i, 0))
    return _call(
        body, [dffn, w_down, gate, up], name="ffn_bwd_act", grid=(T // tm,),
        in_specs=[pl.BlockSpec((tm, D), lambda i: (i, 0)), _resident((F, D)), osp, osp],
        out_specs=[osp, osp], out_shape=[S((T, F), bf16)] * 2, sem=("parallel",), vmem=VMEM_LIMIT, plan=plan)


def ffn_bwd_x(dgate, dup, wgT, wuT, dr2, r1, g1, plan):
    T = dr2.shape[0]
    tm = min(512, T)

    def body(dg_ref, du_ref, wg_ref, wu_ref, dr2_ref, r1_ref, g1_ref, dr_ref, drb_ref, dgam_ref, dbet_ref):
        @pl.when(pl.program_id(0) == 0)
        def _():
            dgam_ref[...] = jnp.zeros_like(dgam_ref)
            dbet_ref[...] = jnp.zeros_like(dbet_ref)

        for rs in _row_parts(tm):
            dx1 = ALPHA * dr2_ref[rs, :] + _dot(dg_ref[rs, :], wg_ref[...]) + _dot(du_ref[rs, :], wu_ref[...])
            xh, rstd = _ln_stats(r1_ref[rs, :])
            dgam_ref[...] += jnp.sum(dx1 * xh, axis=0, keepdims=True)
            dbet_ref[...] += jnp.sum(dx1, axis=0, keepdims=True)
            dr = _ln_bwd(dx1, xh, rstd, g1_ref[...])
            dr_ref[rs, :] = dr
            drb_ref[rs, :] = dr.astype(bf16)

    row = pl.BlockSpec((tm, D), lambda i: (i, 0))
    wide = pl.BlockSpec((tm, F), lambda i: (i, 0))
    wsp = _resident((F, D))
    vec = pl.BlockSpec((1, D), lambda i: (0, 0))
    return _call(
        body, [dgate, dup, wgT, wuT, dr2, r1, g1], name="ffn_bwd_x", grid=(T // tm,),
        in_specs=[wide, wide, wsp, wsp, row, row, vec],
        out_specs=[row, row, vec, vec], out_shape=[S((T, D), f32), S((T, D), bf16), S((1, D), f32), S((1, D), f32)],
        vmem=VMEM_LIMIT, plan=plan)


def merge_bwd(dmix, w_o, ya, yb, wso, wco, proj, plan):
    T = dmix.shape[0]
    tm = min(512, T)

    def body(dm_ref, wo_ref, ya_ref, yb_ref, wa_ref, wb_ref, ga_ref, gb_ref, dya_ref, dyb_ref, dga_ref, dgb_ref, sa_ref, sb_ref,
             wa_s, wb_s):
        @pl.when(pl.program_id(0) == 0)
        def _():
            _dense_columns(wa_ref, wa_s)
            _dense_columns(wb_ref, wb_s)

        dmer = _dot_nt(dm_ref[...], wo_ref[...])
        sa, sb = _sigmoid(ga_ref[...]), _sigmoid(gb_ref[...])
        dya_ref[...] = (dmer * sa).astype(bf16)
        dyb_ref[...] = (dmer * sb).astype(bf16)
        dga = dmer * _dot(ya_ref[...], wa_s[...]) * sa * (1.0 - sa)
        dgb = dmer * _dot(yb_ref[...], wb_s[...]) * sb * (1.0 - sb)
        dga_ref[...] = dga.astype(bf16)
        dgb_ref[...] = dgb.astype(bf16)
        sa_ref[...] = jnp.sum(dga, axis=0, keepdims=True)
        sb_ref[...] = jnp.sum(dgb, axis=0, keepdims=True)

    act = pl.BlockSpec((tm, W), lambda i: (i, 0))
    osp = pl.BlockSpec((tm, D), lambda i: (i, 0))
    ssp = pl.BlockSpec((None, 1, D), lambda i: (i, 0, 0))
    return _call(
        body, [dmix, w_o, ya, yb, wso, wco, proj, proj], name="merge_bwd", grid=(T // tm,),
        in_specs=[osp, _resident((D, D)), act, act, _resident((NDEV, W, LANE)), _resident((NDEV, W, LANE)),
                  pl.BlockSpec((tm, D), lambda i: (i, 0)), pl.BlockSpec((tm, D), lambda i: (i, 1))],
        out_specs=[osp, osp, osp, osp, ssp, ssp],
        out_shape=[S((T, D), bf16)] * 4 + [S((T // tm, 1, D), f32)] * 2,
        scratch=[pltpu.VMEM((W, D), bf16), pltpu.VMEM((W, D), bf16)], vmem=VMEM_LIMIT, plan=plan)


def branches_bwd_x(dYA, dYB, wso, wco, plan):
    T = dYA.shape[0]
    tm = min(1024, T)

    def body(da_ref, db_ref, wa_ref, wb_ref, oa_ref, ob_ref, wa_s, wb_s):
        @pl.when(pl.program_id(0) == 0)
        def _():
            _dense_columns(wa_ref, wa_s)
            _dense_columns(wb_ref, wb_s)

        oa_ref[...] = _dot_nt(da_ref[...], wa_s[...])
        ob_ref[...] = _dot_nt(db_ref[...], wb_s[...])

    row = pl.BlockSpec((tm, D), lambda i: (i, 0))
    osp = pl.BlockSpec((tm, W), lambda i: (i, 0))
    return _call(
        body, [dYA, dYB, wso, wco], name="branches_bwd_x", grid=(T // tm,),
        in_specs=[row, row, _resident((NDEV, W, LANE)), _resident((NDEV, W, LANE))],
        out_specs=[osp, osp], out_shape=[S((T, W), f32)] * 2,
        scratch=[pltpu.VMEM((W, D), bf16), pltpu.VMEM((W, D), bf16)], vmem=VMEM_LIMIT, plan=plan)


def branch_bwd_w(act, dY, name):
    T = act.shape[0]
    tk = W // 2

    def body(a_ref, d_ref, o_ref):
        res = _dot_tn(a_ref[...], d_ref[...])
        for k in range(NDEV):
            o_ref[k] = res[:, k * LANE:(k + 1) * LANE].astype(o_ref.dtype)

    return pl.pallas_call(
        body, name=name, grid=(W // tk,),
        in_specs=[pl.BlockSpec((T, tk), lambda i: (0, i)), _resident((T, D))],
        out_specs=pl.BlockSpec((NDEV, tk, LANE), lambda i: (0, i, 0)), out_shape=S((NDEV, W, LANE), GRAD_DT),
        compiler_params=_cp(("parallel",), VMEM_LIMIT),
    )(act, dY)


def glu_bwd(yn, dya, glu_w, glu_b, plan):
    T = yn.shape[0]
    tm = min(512, T)

    def body(y_ref, d_ref, w_ref, b_ref, dy_ref, dsp_ref, g_ref, db_ref):
        @pl.when(pl.program_id(0) == 0)
        def _():
            db_ref[...] = jnp.zeros_like(db_ref)

        y, dya_ = y_ref[...], d_ref[...]
        g = _gelu(y)
        gb = g.astype(bf16)
        s = _sigmoid(_dot(gb, w_ref[...]) + b_ref[...])
        dsp = dya_ * g * s * (1.0 - s)
        dspb = dsp.astype(bf16)
        dg = dya_ * s + _dot_nt(dspb, w_ref[...])
        dy_ref[...] = dg * _gelu_grad(y)
        dsp_ref[...] = dspb
        g_ref[...] = gb
        db_ref[...] += jnp.sum(dsp, axis=0, keepdims=True)

    row = pl.BlockSpec((tm, W), lambda i: (i, 0))
    vec = pl.BlockSpec((1, W), lambda i: (0, 0))
    return _call(
        body, [yn, dya, glu_w, glu_b], name="glu_bwd", grid=(T // tm,),
        in_specs=[row, row, pl.BlockSpec((W, W), lambda i: (0, 0)), vec],
        out_specs=[row, row, row, vec], out_shape=[S((T, W), f32), S((T, W), bf16), S((T, W), bf16), S((1, W), f32)],
        sem=("arbitrary",), plan=plan)


def conv_bwd(proj, dyb, conv_w, plan):
    T = proj.shape[0]
    RB = min(512, T)
    nrb = T // RB

    def body(h_ref, c_ref, b_ref, d_ref, w_ref, dh_ref, dc_ref, db_ref, dw_ref, s_ref):
        w0, w1, w2 = w_ref[0:1, :], w_ref[1:2, :], w_ref[2:3, :]

        def blk(i, carry):
            a0, a1, a2, sh, sc, sb = carry
            r0 = pl.multiple_of(i * RB, RB)
            rs = pl.ds(r0, RB)
            h, cg, bg, dyb_ = h_ref[rs, :], c_ref[rs, :], b_ref[rs, :], d_ref[rs, :]
            ch = cg * h
            pr = pl.ds(jnp.maximum(r0 - 8, 0), 8)
            prev = jnp.where(i > 0, c_ref[pr, :] * h_ref[pr, :], 0.0)
            ch1, ch2 = _shift_rows(ch, prev, 1), _shift_rows(ch, prev, 2)
            dbg = dyb_ * (w2 * ch + w1 * ch1 + w0 * ch2)
            db_ref[rs, :] = dbg.astype(bf16)
            dz = dyb_ * bg
            nx = pl.ds(jnp.minimum(r0 + RB, T - 8), 8)
            nxt = jnp.where(i < nrb - 1, d_ref[nx, :] * b_ref[nx, :], 0.0)
            dch = w2 * dz + w1 * _lift_rows(dz, nxt, 1) + w0 * _lift_rows(dz, nxt, 2)
            dcg, dh = dch * h, dch * cg
            dc_ref[rs, :] = dcg.astype(bf16)
            dh_ref[rs, :] = dh.astype(bf16)
            col = lambda v: jnp.sum(v, axis=0, keepdims=True)
            return (a0 + col(dz * ch2), a1 + col(dz * ch1), a2 + col(dz * ch), sh + col(dh), sc + col(dcg), sb + col(dbg))

        zero = jnp.zeros((1, LANE), f32)
        a0, a1, a2, sh, sc, sb = lax.fori_loop(0, nrb, blk, (zero,) * 6)
        dw_ref[0:1, :] = a0
        dw_ref[1:2, :] = a1
        dw_ref[2:3, :] = a2
        s_ref[0:1, :] = sh
        s_ref[1:2, :] = sc
        s_ref[2:3, :] = sb

    nb = W // LANE
    slab = pl.BlockSpec((T, LANE), lambda k: (0, k))
    three = pl.BlockSpec((3, LANE), lambda k: (0, k))
    return _call(
        body, [proj, proj, proj, dyb, conv_w], name="conv_bwd", grid=(nb,),
        in_specs=[pl.BlockSpec((T, LANE), lambda k: (0, 4 * nb + k)), pl.BlockSpec((T, LANE), lambda k: (0, 5 * nb + k)),
                  pl.BlockSpec((T, LANE), lambda k: (0, 6 * nb + k)), slab, three],
        out_specs=[slab, slab, slab, three, three],
        out_shape=[S((T, W), bf16)] * 3 + [S((3, W), f32)] * 2, sem=("parallel",), vmem=VMEM_LIMIT, plan=plan)


def in_proj_bwd_x(parts, win_g, base, scale, name, plan=None):
    T = base.shape[0]
    tm = min(512, T)
    n = len(parts)

    def body(*refs):
        p_refs, w_ref, b_ref, o_ref = refs[:n], refs[n], refs[n + 1], refs[n + 2]
        acc = scale * b_ref[...]
        for p_ref, (_, _, k) in zip(p_refs, parts):
            acc += _dot_nt(p_ref[...], w_ref[k])
        o_ref[...] = acc

    row = pl.BlockSpec((tm, D), lambda i: (i, 0))
    p_specs = [pl.BlockSpec((tm, W), (lambda i, cb=cb: (i, cb))) for _, cb, _ in parts]
    return _call(
        body, [a for a, _, _ in parts] + [win_g, base], name=name, grid=(T // tm,),
        in_specs=p_specs + [_resident((NDEV, D, W)), row],
        out_specs=[row], out_shape=[S((T, D), f32)], vmem=VMEM_LIMIT, plan=plan)


def ssm_param_bwd(lam_re, lam_im, log_dt, fr, fi, br, bi, dwb, dwcT, dlbr, dlbi):
    def body(lr_ref, li_ref, ldt_ref, fr_ref, fi_ref, br_ref, bi_ref, dwb_ref, dwc_ref, dlbr_ref, dlbi_ref,
             dbr_ref, dbi_ref, dlr_ref, dli_ref, dldt_ref, dcr_ref, dci_ref, dr_s, di_s):
        for k in range(W // LANE):
            for gl in range(NG // (W // LANE)):
                rows, src = slice((8 * k + gl) * GC, (8 * k + gl + 1) * GC), slice(gl * GC, (gl + 1) * GC)
                re, im = slice(gl * NP, (gl + 1) * NP), slice(SW + gl * NP, SW + (gl + 1) * NP)
                dr_s[rows, :] = dwb_ref[k, src, re]
                di_s[rows, :] = dwb_ref[k, src, im]
                dcr_ref[rows, :] = dwc_ref[k, src, re]
                dci_ref[rows, :] = -dwc_ref[k, src, im]
        fr_, fi_ = _per_channel(fr_ref[...]), _per_channel(fi_ref[...])
        br_, bi_, dr, di = br_ref[...], bi_ref[...], dr_s[...], di_s[...]
        dbr_ref[...] = fr_ * dr + fi_ * di
        dbi_ref[...] = fr_ * di - fi_ * dr
        dfr = jnp.sum((dr * br_ + di * bi_).reshape(NG, GC, NP), axis=1)
        dfi = jnp.sum((di * br_ - dr * bi_).reshape(NG, GC, NP), axis=1)
        _, vjp = jax.vjp(_disc, lr_ref[...], li_ref[...], ldt_ref[...])
        dlr_ref[...], dli_ref[...], dldt = vjp((dlbr_ref[...], dlbi_ref[...], dfr, dfi))
        dldt_ref[...] = _transpose_exact(dldt)

    blk = S((NG * GC, NP), f32)
    return pl.pallas_call(
        body, name="ssm_param_bwd", out_shape=[blk, blk, S((NG, NP), f32), S((NG, NP), f32), S((1, NG), f32), blk, blk],
        scratch_shapes=[pltpu.VMEM((NG * GC, NP), f32)] * 2)(
        lam_re, lam_im, log_dt, fr, fi, br, bi, dwb, dwcT, dlbr, dlbi)


def _adam(w, g, m, v):
    m = ADAM_B1 * m + (1.0 - ADAM_B1) * g
    v = ADAM_B2 * v + (1.0 - ADAM_B2) * (g * g)
    m_hat = m / (1.0 - ADAM_B1 ** ADAM_STEP)
    v_hat = v / (1.0 - ADAM_B2 ** ADAM_STEP)
    return -ADAM_LR * (m_hat / (jnp.sqrt(v_hat) + ADAM_EPS) + ADAM_WD * w), m, v


def _sum_in_order(c_ref):
    g = c_ref[0].astype(f32)
    for k in range(1, c_ref.shape[0]):
        g = g + c_ref[k].astype(f32)
    return g


def sum_blocks(contrib, name):
    def body(c_ref, o_ref):
        o_ref[...] = _sum_in_order(c_ref)

    return pl.pallas_call(body, name=name, out_shape=S(contrib.shape[1:], f32))(contrib)


def adam_update(w, m, v, contrib, name, rows_per_block=None, summed_on_0=None, plan=None):
    R, C = w.shape
    n = contrib.shape[0]
    tr = min(rows_per_block or R, R)

    def body(w_ref, m_ref, v_ref, c_ref, *refs):
        g_ref, d_ref, nm_ref, nv_ref = refs[-4:]
        g = _sum_in_order(c_ref)
        if summed_on_0 is not None:
            x, y, c = _coords()
            g = jnp.where(4 * x + 2 * y + c == 0, refs[0][...], g)
        g_ref[...] = g
        d_ref[...], nm_ref[...], nv_ref[...] = _adam(w_ref[...], g, m_ref[...], v_ref[...])

    blk = pl.BlockSpec((tr, C), lambda i: (i, 0))
    extra = [] if summed_on_0 is None else [summed_on_0]
    return _call(
        body, [w, m, v, contrib] + extra, name=name, grid=(R // tr,),
        in_specs=[blk, blk, blk, pl.BlockSpec((n, tr, C), lambda i: (0, i, 0))] + [blk] * len(extra),
        out_specs=[blk] * 4, out_shape=[S((R, C), f32)] * 4, sem=("parallel",), vmem=VMEM_LIMIT, plan=plan)


_ROWVEC = (("b_in", IN_COLS), ("ssm_d", W), ("glu_b", W), ("ln1_g", D), ("ln1_b", D), ("ln2_g", D), ("ln2_b", D))
_HALF = NG * GC // 2
_BC_LANE = {"ssm_b_re": 0, "ssm_b_im": NP, "ssm_c_re": 0, "ssm_c_im": NP}
_PACK = {}
_r = 0
for _n, _k in _ROWVEC:
    _PACK[_n] = _r
    _r += _k // LANE
for _n, _rows in (("ssm_lambda", NG), ("scalars", 8), ("ssm_b", _HALF), ("ssm_c", _HALF), ("conv_w", 16)):
    _PACK[_n] = _r
    _r += _rows
for _n in _BC_LANE:
    _PACK[_n] = _PACK[_n[:5]]
PACK_ROWS = _r
assert PACK_ROWS % 8 == 0
_SMALL = ("b_in", "ssm_lambda_re", "ssm_lambda_im", "ssm_log_dt", "ssm_b_re", "ssm_b_im", "ssm_c_re", "ssm_c_im",
          "ssm_d", "glu_b", "ln1_g", "ln1_b", "ln2_g", "ln2_b")


def pack_grads(su, shcb, sga, sgb, dd, dglu_b, dln1_g, dln1_b, dln2_g, dln2_b, dlam_re, dlam_im, dldt, sqerr, dbr, dbi,
               dc_re, dc_im, dconv):
    nI = sga.shape[0]

    def body(su_ref, sh_ref, sga_ref, sgb_ref, dd_ref, gb_ref, l1g_ref, l1b_ref, l2g_ref, l2b_ref, lr_ref, li_ref, dt_ref,
             sq_ref, br_ref, bi_ref, cr_ref, ci_ref, cw_ref, o_ref):
        o_ref[...] = jnp.zeros_like(o_ref)

        def put_row(name, v):
            r0 = _PACK[name]
            for i in range(v.shape[1] // LANE):
                o_ref[r0 + i:r0 + i + 1, :] = v[:, i * LANE:(i + 1) * LANE]

        ga, gb = sga_ref[0], sgb_ref[0]
        for i in range(1, nI):
            ga, gb = ga + sga_ref[i], gb + sgb_ref[i]
        put_row("b_in", jnp.concatenate([su_ref[k] for k in range(W // LANE)]
                                        + [sh_ref[0:1, :], sh_ref[1:2, :], sh_ref[2:3, :], ga, gb], axis=1))
        put_row("ssm_d", jnp.concatenate([dd_ref[k] for k in range(W // LANE)], axis=1))
        put_row("glu_b", gb_ref[...])
        put_row("ln1_g", l1g_ref[...])
        put_row("ln1_b", l1b_ref[...])
        put_row("ln2_g", l2g_ref[...])
        put_row("ln2_b", l2b_ref[...])
        r0 = _PACK["ssm_lambda"]
        o_ref[r0:r0 + NG, 0:NP] = lr_ref[...]
        o_ref[r0:r0 + NG, NP:2 * NP] = li_ref[...]
        r0 = _PACK["scalars"]
        o_ref[r0:r0 + 1, 0:NG] = dt_ref[...]
        o_ref[r0 + 1:r0 + 2, 0:1] = sq_ref[...]
        for name, ref in (("ssm_b_re", br_ref), ("ssm_b_im", bi_ref), ("ssm_c_re", cr_ref), ("ssm_c_im", ci_ref)):
            r0, l0 = _PACK[name], _BC_LANE[name]
            o_ref[r0:r0 + _HALF, l0:l0 + NP] = pltpu.bitcast(ref[...].astype(bf16), f32)
        for cb in range(W // LANE):
            o_ref[_PACK["conv_w"] + 3 * cb:_PACK["conv_w"] + 3 * cb + 3, :] = cw_ref[:, cb * LANE:(cb + 1) * LANE]

    return pl.pallas_call(body, name="pack_grads", out_shape=S((PACK_ROWS, LANE), f32))(
        su, shcb, sga, sgb, dd, dglu_b, dln1_g, dln1_b, dln2_g, dln2_b, dlam_re, dlam_im, dldt, sqerr, dbr, dbi, dc_re, dc_im,
        dconv)


def adam_small(packed_all, params):
    names = list(_SMALL) + ["conv_w"]
    flat = [a for n in names for a in params[n]]

    def body(*refs):
        p_ref = refs[0]
        ins = refs[1:1 + 3 * len(names)]
        outs = refs[1 + 3 * len(names):-2]
        loss_ref, g_ref = refs[-2], refs[-1]

        def part(k, rs=slice(None), ls=slice(None)):
            return p_ref[k, rs, ls]

        g_all = part(0)
        for k in range(1, NDEV):
            g_all = g_all + part(k)
        g_ref[...] = g_all

        def rows(name, r0, n, l0=0, lanes=LANE):
            return g_ref[_PACK[name] + r0:_PACK[name] + r0 + n, l0:l0 + lanes]

        def grad_of(name):
            if name in dict(_ROWVEC):
                return jnp.concatenate([rows(name, i, 1) for i in range(dict(_ROWVEC)[name] // LANE)], axis=1)
            if name in ("ssm_lambda_re", "ssm_lambda_im"):
                return rows("ssm_lambda", 0, NG, NP * (name == "ssm_lambda_im"), NP)[None]
            if name == "ssm_log_dt":
                return rows("scalars", 0, 1, 0, NG)
            if name in _BC_LANE:
                rs, ls = slice(_PACK[name], _PACK[name] + _HALF), slice(_BC_LANE[name], _BC_LANE[name] + NP)
                g = pltpu.bitcast(part(0, rs, ls), bf16).astype(f32)
                for k in range(1, NDEV):
                    g = g + pltpu.bitcast(part(k, rs, ls), bf16).astype(f32)
                return g.reshape(1, NG, GC, NP)
            full = jnp.concatenate([rows("conv_w", 3 * cb, 3) for cb in range(W // LANE)], axis=1)
            x, y, c = _coords()
            col0 = (4 * x + 2 * y + c) * (W // NDEV)
            sel = (lax.broadcasted_iota(jnp.int32, (W, W // NDEV), 0)
                   == lax.broadcasted_iota(jnp.int32, (W, W // NDEV), 1) + col0).astype(f32)
            return jnp.dot(full, sel, precision=HIGHEST, preferred_element_type=f32)[None]

        loss_ref[...] = 0.5 * rows("scalars", 1, 1, 0, 1)
        for i, name in enumerate(names):
            w_ref, m_ref, v_ref = ins[3 * i:3 * i + 3]
            g = grad_of(name)
            d, m, v = _adam(w_ref[...], g, m_ref[...], v_ref[...])
            outs[4 * i][...] = g
            outs[4 * i + 1][...] = d
            outs[4 * i + 2][...] = m
            outs[4 * i + 3][...] = v

    out_shape = [S(params[n][0].shape, f32) for n in names for _ in range(4)] + [S((1, 1), f32)]
    res = pl.pallas_call(body, name="adam_small", out_shape=out_shape, scratch_shapes=[pltpu.VMEM((PACK_ROWS, LANE), f32)],
                         compiler_params=_cp(None, VMEM_LIMIT))(packed_all, *flat)
    return {n: res[4 * i:4 * i + 4] for i, n in enumerate(names)}, res[-1]


def _block_diag(wgt):
    eye = jnp.eye(8, dtype=wgt.dtype)
    out = wgt[:, :, :, None, :] * eye[None, :, None, :, None]
    return out.reshape(4, 8 * wgt.shape[2], 8 * wgt.shape[3])


def kernel(x, w_in, b_in, ssm_lambda_re, ssm_lambda_im, ssm_log_dt, ssm_b_re, ssm_b_im, ssm_c_re, ssm_c_im, ssm_d, glu_w, glu_b, w_ssm_out, conv_w, w_conv_out, w_o, ln1_g, ln1_b, w_gate, w_up, w_down, ln2_g, ln2_b, loss_target, m_w_in, m_b_in, m_ssm_lambda_re, m_ssm_lambda_im, m_ssm_log_dt, m_ssm_b_re, m_ssm_b_im, m_ssm_c_re, m_ssm_c_im, m_ssm_d, m_glu_w, m_glu_b, m_w_ssm_out, m_conv_w, m_w_conv_out, m_w_o, m_ln1_g, m_ln1_b, m_w_gate, m_w_up, m_w_down, m_ln2_g, m_ln2_b, v_w_in, v_b_in, v_ssm_lambda_re, v_ssm_lambda_im, v_ssm_log_dt, v_ssm_b_re, v_ssm_b_im, v_ssm_c_re, v_ssm_c_im, v_ssm_d, v_glu_w, v_glu_b, v_w_ssm_out, v_conv_w, v_w_conv_out, v_w_o, v_ln1_g, v_ln1_b, v_w_gate, v_w_up, v_w_down, v_ln2_g, v_ln2_b):
    given = dict(locals())
    xs = x[0]
    target = loss_target[0]

    tr = lambda a: jnp.swapaxes(a[0], 0, 1)
    win_s, glu_s, wso_s, wco_s, wo_s, wgT_s, wuT_s, wd_s = prep_weights(
        [w_in[0], glu_w[0], w_ssm_out[0], w_conv_out[0], w_o[0], tr(w_gate), tr(w_up), w_down[0]])
    (win_g,) = run_plan(GatherPlan([win_s], srcs=(0,)), "gather_w_in_u")

    lam_re, lam_im = ssm_lambda_re[0], ssm_lambda_im[0]
    ldt = ssm_log_dt[0].reshape(NG, 1)
    br2 = jnp.swapaxes(ssm_b_re[0], 1, 2).reshape(NG * GC, NP)
    bi2 = jnp.swapaxes(ssm_b_im[0], 1, 2).reshape(NG * GC, NP)
    lbr, lbi, fr, fi, bbr, bbi = ssm_params(lam_re, lam_im, ldt, br2, bi2)
    bb_t = lambda b: b.reshape(4, 8, GC, NP)
    wb = jnp.concatenate([_block_diag(bb_t(bbr)), _block_diag(bb_t(bbi))], axis=2)
    c_t = lambda c: c.reshape(4, 8, GC, NP).transpose(0, 1, 3, 2)
    wc = jnp.concatenate([_block_diag(c_t(ssm_c_re[0])), -_block_diag(c_t(ssm_c_im[0]))], axis=1)
    wbT, wcT = wb.transpose(0, 2, 1), wc.transpose(0, 2, 1)
    wb, wc, wbT, wcT = wb.astype(bf16), wc.astype(bf16), wbT.astype(bf16), wcT.astype(bf16)
    lbr_s, lbi_s = lbr.reshape(4, 1, SW), lbi.reshape(4, 1, SW)
    dsk = ssm_d[0].reshape(4, 1, LANE)

    half_a, half_b = (0, 3, 5, 6), (1, 2, 4, 7)
    (u_nat, xb), (conv_g, glu_g, wso_g) = in_proj_u(xs, win_g, b_in, GatherPlan([conv_w[0], glu_s, wso_s]))
    u_p = to_perm(u_nat, 0, "perm_u")
    (y_p, xr_p, xi_p), (win_g, wuT_g) = ssm_fwd(
        u_p, wb, wc, lbr_s, lbi_s, dsk,
        Plans([GatherPlan([win_s], srcs=tuple(range(1, NDEV)), into=[win_g]), GatherPlan([wuT_s], srcs=half_a)]))
    conv_f = conv_g.transpose(1, 0, 2).reshape(3, W)
    (proj,), (wco_g, wo_g, wgT_g) = in_proj_rest(
        xb, win_g, b_in, Plans([GatherPlan([wco_s, wo_s]), GatherPlan([wgT_s], srcs=half_a)]))
    glu_f, wo_f = glu_g.reshape(W, W), wo_g.reshape(D, D)
    (yn,), _ = from_perm(y_p, "unperm_y")
    ya = glu_fwd(yn, glu_f, glu_b)
    yb = conv_fwd(proj, conv_f)
    (merged,), (wgT_g,) = merge_fwd(ya, yb, wso_g, wco_g, proj, GatherPlan([wgT_s], srcs=half_b, into=[wgT_g]))
    (r1, x1b), (wuT_g,) = mix_ln1(merged, wo_f, xs, ln1_g, ln1_b, GatherPlan([wuT_s], srcs=half_b, into=[wuT_g]))
    wgT, wuT = wgT_g.reshape(F, D), wuT_g.reshape(F, D)
    (gate, up, hid), (wd_g,) = gate_up(x1b, wgT, wuT, GatherPlan([wd_s]))
    wd_f = wd_g.reshape(F, D)
    dr2, dffn, sqerr, dln2_g, dln2_b = down_loss(hid, wd_f, r1, ln1_g, ln1_b, ln2_g, ln2_b, target)

    dwd, _ = mm_tn_rows(hid, dffn, "grad_w_down")
    dwd = dwd.reshape(NDEV, FS, D)
    (dgate, dup), (r_wd,) = ffn_bwd_act(dffn, wd_f, gate, up, ScatterPlan([dwd], only=half_a))
    dwgT, (r_wd,) = mm_tn_rows(dgate, x1b, "grad_w_gate", plan=ScatterPlan([dwd], only=half_b, into=[r_wd]))
    dwgT = dwgT.reshape(NDEV, FS, D)
    dwuT, (r_wgT,) = mm_tn_rows(dup, x1b, "grad_w_up", plan=ScatterPlan([dwgT], only=half_a))
    dwuT = dwuT.reshape(NDEV, FS, D)
    (dr1, dmix, dln1_g, dln1_b), (r_wgT, r_wuT) = ffn_bwd_x(
        dgate, dup, wgT, wuT, dr2, r1, ln1_g,
        Plans([ScatterPlan([dwgT], only=half_b, into=[r_wgT]), ScatterPlan([dwuT], only=half_a)]))
    (dYA, dYB, dga, dgb, sga, sgb), (r_wuT,) = merge_bwd(dmix, wo_f, ya, yb, wso_g, wco_g, proj,
                                                         ScatterPlan([dwuT], only=half_b, into=[r_wuT]))
    dwo, _ = mm_tn_rows(merged, dmix, "grad_w_o")
    dwo = dwo.reshape(NDEV, D // NDEV, D)
    (dya, dyb), _ = branches_bwd_x(dYA, dYB, wso_g, wco_g, None)
    dwso = branch_bwd_w(ya, dYA, "grad_w_ssm_out")
    dwco = branch_bwd_w(yb, dYB, "grad_w_conv_out")
    (dyn, dsp, gb, dglu_b), (r_wso,) = glu_bwd(yn, dya, glu_f, glu_b, ScatterPlan([dwso]))
    dglu = mm_tn_rows(gb, dsp, "grad_glu_w")[0].reshape(NDEV, W // NDEV, W)
    (dh, dcg, dbg, dconv, shcb), (r_wco,) = conv_bwd(proj, dyb, conv_f, ScatterPlan([dwco]))
    dwin, (r_wo, r_glu) = grad_w_in_rest(xb, dh, dcg, dbg, dga, dgb, ScatterPlan([dwo, dglu]))
    dy_p = to_perm(dyn, 0, "perm_dy")
    (du_p, dwb, dwcT, dlbr_s, dlbi_s, dd, su), (r_win,) = ssm_bwd(
        u_p, dy_p, xr_p, xi_p, wbT, wcT, lbr_s, lbi_s, dsk, ScatterPlan([dwin], only=tuple(range(1, NDEV))))

    dbr2, dbi2, dlam_re, dlam_im, dldt, dc_re, dc_im = ssm_param_bwd(
        lam_re, lam_im, ldt, fr, fi, br2, bi2, dwb, dwcT, dlbr_s.reshape(NG, NP), dlbi_s.reshape(NG, NP))
    packed = pack_grads(su, shcb, sga, sgb, dd, dglu_b, dln1_g, dln1_b, dln2_g, dln2_b, dlam_re, dlam_im, dldt, sqerr,
                        dbr2, dbi2, dc_re, dc_im, dconv)
    (du,), _ = from_perm(du_p, "unperm_du", bf16)
    dwin_u = mm_tn(xb, du, "grad_w_in_u").reshape(NDEV, D // NDEV, W)

    rest = [(dh, 0, 1), (dcg, 0, 2), (dbg, 0, 3), (dga, 0, 4), (dga, 1, 5), (dgb, 0, 6), (dgb, 1, 7)]
    (gx_a,), (r_win_u, small_all) = in_proj_bwd_x(
        rest[:4], win_g, dr1, ALPHA, "in_proj_bwd_x_a", Plans([ScatterPlan([dwin_u]), GatherPlan([packed])]))
    my_rows = sum_blocks(r_win_u, "sum_w_in_u")
    (gx_b,), (win_u_sum,) = in_proj_bwd_x(
        rest[4:], win_g, gx_a, 1.0, "in_proj_bwd_x_b", ScatterPlan([my_rows], only=(0,), whole=True))
    (grad_x,), _ = in_proj_bwd_x([(du, 0, 0)], win_g, gx_b, 1.0, "in_proj_bwd_x_u")

    out = {}

    def put(name, res, back=lambda a: a[None]):
        out["grad_" + name], out["delta_" + name], out["new_m_" + name], out["new_v_" + name] = [back(r) for r in res]

    put("w_down", adam_update(w_down[0], m_w_down[0], v_w_down[0], r_wd, "adam_w_down", 176)[0])
    put("w_in", adam_update(w_in[0], m_w_in[0], v_w_in[0], r_win, "adam_w_in", 128,
                            summed_on_0=win_u_sum.reshape(D, W))[0])
    put("glu_w", adam_update(glu_w[0], m_glu_w[0], v_glu_w[0], r_glu, "adam_glu_w")[0])
    put("w_ssm_out", adam_update(w_ssm_out[0], m_w_ssm_out[0], v_w_ssm_out[0], r_wso, "adam_w_ssm_out")[0])
    put("w_conv_out", adam_update(w_conv_out[0], m_w_conv_out[0], v_w_conv_out[0], r_wco, "adam_w_conv_out")[0])
    put("w_o", adam_update(w_o[0], m_w_o[0], v_w_o[0], r_wo, "adam_w_o", 32)[0])
    untr = lambda a: jnp.swapaxes(a, 0, 1)[None]
    put("w_gate", adam_update(tr(w_gate), tr(m_w_gate), tr(v_w_gate), r_wgT, "adam_w_gate", 176)[0], untr)
    put("w_up", adam_update(tr(w_up), tr(m_w_up), tr(v_w_up), r_wuT, "adam_w_up", 176)[0], untr)
    as_c = lambda a: jnp.swapaxes(a, 2, 3)
    params = {n: (given[n], given["m_" + n], given["v_" + n]) for n in list(_SMALL) + ["conv_w"]}
    for n in ("ssm_b_re", "ssm_b_im"):
        params[n] = tuple(as_c(a) for a in params[n])
    small, loss = adam_small(small_all, params)
    for n, res in small.items():
        put(n, res, as_c if n in ("ssm_b_re", "ssm_b_im") else (lambda a: a))

    names = ["w_in", "b_in", "ssm_lambda_re", "ssm_lambda_im", "ssm_log_dt", "ssm_b_re", "ssm_b_im", "ssm_c_re", "ssm_c_im",
             "ssm_d", "glu_w", "glu_b", "w_ssm_out", "conv_w", "w_conv_out", "w_o", "ln1_g", "ln1_b", "w_gate", "w_up",
             "w_down", "ln2_g", "ln2_b"]
    return (loss.reshape(()), grad_x[None], *[out[p + n] for p in ("grad_", "delta_", "new_m_", "new_v_") for n in names])
```

```python
import functools
import math

import jax
import jax.numpy as jnp
from jax import lax
from jax.experimental import pallas as pl
from jax.experimental.pallas import tpu as pltpu

f32, bf16 = jnp.float32, jnp.bfloat16
S = jax.ShapeDtypeStruct
MESH = pl.DeviceIdType.MESH
HIGHEST = lax.Precision.HIGHEST

D = 1024
W = 512
NG, NP, GC = 32, 64, 16
F = 2816
NDEV = 8
FS = F // NDEV
IN_COLS = 8 * W
ALPHA = 2.0 ** 0.25
LN_EPS = 1e-5
ADAM_LR, ADAM_B1, ADAM_B2, ADAM_EPS, ADAM_WD, ADAM_STEP = 0.001, 0.9, 0.999, 1e-08, 0.01, 10
NC = 32
LANE = 128
SW = 4 * LANE
VMEM_LIMIT = 56 * 1024 * 1024
GRAD_DT = bf16
ANY = pl.BlockSpec(memory_space=pl.ANY)


def _cp(sem=None, vmem=None):
    return pltpu.CompilerParams(dimension_semantics=sem, vmem_limit_bytes=vmem)


def _resident(shape):
    return pl.BlockSpec(shape, lambda i: (0,) * len(shape), pipeline_mode=pl.Buffered(1))


def _dot(a, b):
    return jnp.dot(a, b, preferred_element_type=f32)


def _dot_nt(a, b):
    return lax.dot_general(a, b, (((1,), (1,)), ((), ())), preferred_element_type=f32)


def _dot_tn(a, b):
    return lax.dot_general(a, b, (((0,), (0,)), ((), ())), preferred_element_type=f32)


def _eye(n):
    return (lax.broadcasted_iota(jnp.int32, (n, n), 0) == lax.broadcasted_iota(jnp.int32, (n, n), 1)).astype(f32)


def _transpose_exact(a):
    return lax.dot_general(a, _eye(a.shape[0]), (((0,), (0,)), ((), ())), precision=HIGHEST, preferred_element_type=f32)


def _sigmoid(x):
    return 1.0 / (1.0 + jnp.exp(-x))


_GK = math.sqrt(2.0 / math.pi)


def _gelu(x):
    return 0.5 * x * (1.0 + jnp.tanh(_GK * (x + 0.044715 * x * x * x)))


def _gelu_grad(x):
    th = jnp.tanh(_GK * (x + 0.044715 * x * x * x))
    return 0.5 * (1.0 + th) + 0.5 * x * (1.0 - th * th) * _GK * (1.0 + 3.0 * 0.044715 * x * x)


ROW_PART = 256


def _row_parts(tm):
    return [slice(r, r + min(ROW_PART, tm)) for r in range(0, tm, min(ROW_PART, tm))]


def _ln_stats(r):
    mu = jnp.mean(r, axis=-1, keepdims=True)
    xc = r - mu
    var = jnp.mean(xc * xc, axis=-1, keepdims=True)
    rstd = lax.rsqrt(var + LN_EPS)
    return xc * rstd, rstd


def _ln_bwd(dy, xhat, rstd, g):
    dxh = dy * g
    m1 = jnp.mean(dxh, axis=-1, keepdims=True)
    m2 = jnp.mean(dxh * xhat, axis=-1, keepdims=True)
    return rstd * (dxh - m1 - xhat * m2)


def _coords():
    return lax.axis_index("x"), lax.axis_index("y"), lax.axis_index("c")


def _when(cond, fn):
    if cond is True:
        fn()
    else:
        pl.when(cond)(fn)


class GatherPlan:
    aliases = ()

    def __init__(self, arrs, srcs=None, into=None):
        n = self.n = len(arrs)
        self.srcs = srcs
        self.inputs = list(arrs) + list(into or [])
        if into:
            self.aliases = tuple((n + a, a) for a in range(n))
        self.out_shape = [S((NDEV,) + a.shape, a.dtype) for a in arrs]
        self.sems = [pltpu.SemaphoreType.DMA((n, 7)), pltpu.SemaphoreType.DMA((n, 7)), pltpu.SemaphoreType.DMA((n,))]

    def _has(self, dev):
        if self.srcs is None:
            return True
        idx = 4 * dev[0] + 2 * dev[1] + dev[2]
        return functools.reduce(jnp.logical_or, [idx == s for s in self.srcs])

    def _parts(self, ins, outs, sems):
        n = self.n
        send_sems, recv_sems, loc_sems = sems
        x, y, c = _coords()
        me, sib = (x, y, c), (x, y, 1 - c)
        chips = [(1 - x, y), (x, 1 - y), (1 - x, 1 - y)]

        def slot(a, dev):
            return outs[a].at[4 * dev[0] + 2 * dev[1] + dev[2]]

        def copy(a, k, block, to, src=None):
            return pltpu.make_async_remote_copy(
                src_ref=slot(a, block) if src is None else src, dst_ref=slot(a, block),
                send_sem=send_sems.at[a, k], recv_sem=recv_sems.at[a, k], device_id=to, device_id_type=MESH)

        each = [(j, chip, a) for j, chip in enumerate(chips) for a in range(n)]
        own = self._has(me)
        return dict(
            mine=lambda: [(pltpu.make_async_copy(ins[a], slot(a, me), loc_sems.at[a]), own) for a in range(n)],
            first=lambda: ([(copy(a, 0, me, sib, src=ins[a]), own) for a in range(n)]
                           + [(copy(a, 1 + j, me, (*chip, c), src=ins[a]), own) for j, chip, a in each]),
            landed=lambda: [(copy(a, 1 + j, (*chip, c), me), self._has((*chip, c))) for j, chip, a in each],
            passed=lambda: [(copy(a, 4 + j, (*chip, c), sib), self._has((*chip, c))) for j, chip, a in each],
            from_sib=lambda: ([(copy(a, 0, sib, me), self._has(sib)) for a in range(n)]
                              + [(copy(a, 4 + j, (*chip, 1 - c), me), self._has((*chip, 1 - c))) for j, chip, a in each]))

    def start(self, ins, outs, sems):
        p = self._parts(ins, outs, sems)
        for cp, cond in p["mine"]() + p["first"]():
            _when(cond, cp.start)

    def forward(self, ins, outs, sems):
        p = self._parts(ins, outs, sems)
        for (got, cond), (fwd, _) in zip(p["landed"](), p["passed"]()):
            def relay(got=got, fwd=fwd):
                got.wait_recv()
                fwd.start()

            _when(cond, relay)

    def finish(self, ins, outs, sems):
        p = self._parts(ins, outs, sems)
        for cp, cond in p["from_sib"]():
            _when(cond, cp.wait_recv)
        for cp, cond in p["first"]() + p["passed"]():
            _when(cond, cp.wait_send)
        for cp, cond in p["mine"]():
            _when(cond, cp.wait)


class ScatterPlan:
    aliases = ()

    def __init__(self, gs, only=None, into=None, whole=False):
        n = self.n = len(gs)
        self.only = only
        self.whole = whole
        self.inputs = list(gs) + list(into or [])
        if into:
            self.aliases = tuple((n + a, a) for a in range(n))
        self.out_shape = [S((NDEV,) + g.shape if whole else g.shape, g.dtype) for g in gs]
        self.sems = [pltpu.SemaphoreType.DMA((n, 7)), pltpu.SemaphoreType.DMA((n, 7)), pltpu.SemaphoreType.DMA((n,))]

    def _owner(self, idx):
        if self.only is None:
            return True
        return functools.reduce(jnp.logical_or, [idx == b for b in self.only])

    def _copies(self, ins, outs, sems):
        n = self.n
        send_sems, recv_sems, loc_sems = sems
        x, y, c = _coords()
        me = 4 * x + 2 * y + c
        mine = self._owner(me)
        block = (lambda a, k: ins[a]) if self.whole else (lambda a, k: ins[a].at[k])
        copies = [(pltpu.make_async_copy(block(a, me), outs[a].at[me], loc_sems.at[a]), mine, None) for a in range(n)]
        for m in range(1, NDEV):
            px = 1 - x if m & 4 else x
            py = 1 - y if m & 2 else y
            pc = 1 - c if m & 1 else c
            peer = 4 * px + 2 * py + pc
            for a in range(n):
                copies.append((pltpu.make_async_remote_copy(
                    src_ref=block(a, peer), dst_ref=outs[a].at[me],
                    send_sem=send_sems.at[a, m - 1], recv_sem=recv_sems.at[a, m - 1],
                    device_id=(px, py, pc), device_id_type=MESH), self._owner(peer), mine))
        return copies

    def start(self, ins, outs, sems):
        for cp, sends, _ in self._copies(ins, outs, sems):
            _when(sends, cp.start)

    def forward(self, ins, outs, sems):
        pass

    def finish(self, ins, outs, sems):
        for cp, sends, receives in self._copies(ins, outs, sems):
            if receives is None:
                _when(sends, cp.wait)
            else:
                _when(sends, cp.wait_send)
                _when(receives, cp.wait_recv)


class Plans:
    def __init__(self, plans):
        self.plans = plans
        self.inputs = [a for p in plans for a in p.inputs]
        self.out_shape = [s for p in plans for s in p.out_shape]
        self.sems = [s for p in plans for s in p.sems]
        self.aliases, i, o = [], 0, 0
        for p in plans:
            self.aliases += [(i + a, o + b) for a, b in p.aliases]
            i, o = i + len(p.inputs), o + len(p.out_shape)

    def _each(self, what, ins, outs, sems):
        i = o = s = 0
        for p in self.plans:
            ni, no, ns = len(p.inputs), len(p.out_shape), len(p.sems)
            getattr(p, what)(ins[i:i + ni], outs[o:o + no], sems[s:s + ns])
            i, o, s = i + ni, o + no, s + ns

    def start(self, ins, outs, sems):
        self._each("start", ins, outs, sems)

    def forward(self, ins, outs, sems):
        self._each("forward", ins, outs, sems)

    def finish(self, ins, outs, sems):
        self._each("finish", ins, outs, sems)


def _call(body, args, *, name, grid, in_specs, out_specs, out_shape, scratch=(), sem=None, vmem=None, plan=None,
          aliases=None, relay_step=None):
    aliases = aliases or {}
    if plan is None:
        outs = pl.pallas_call(body, name=name, grid=grid, in_specs=list(in_specs), out_specs=list(out_specs),
                              out_shape=list(out_shape), scratch_shapes=list(scratch), input_output_aliases=aliases,
                              compiler_params=_cp(sem, vmem))(*args)
        return list(outs), []
    ni, no, ns = len(in_specs), len(out_specs), len(scratch)
    pi, po = len(plan.inputs), len(plan.out_shape)
    aliases = {**aliases, **{ni + a: no + b for a, b in plan.aliases}}

    def wrapped(*refs):
        main_in, p_in = refs[:ni], refs[ni:ni + pi]
        main_out, p_out = refs[ni + pi:ni + pi + no], refs[ni + pi + no:ni + pi + no + po]
        main_scr, p_sems = refs[ni + pi + no + po:ni + pi + no + po + ns], refs[ni + pi + no + po + ns:]
        ids = [pl.program_id(d) for d in range(len(grid))]
        first = functools.reduce(jnp.logical_and, [i == 0 for i in ids])
        last = functools.reduce(jnp.logical_and, [i == g - 1 for i, g in zip(ids, grid)])

        @pl.when(first)
        def _():
            plan.start(p_in, p_out, p_sems)

        @pl.when(last if relay_step is None else ids[0] == max(relay_step, 0))
        def _():
            plan.forward(p_in, p_out, p_sems)

        body(*main_in, *main_out, *main_scr)

        @pl.when(last)
        def _():
            plan.finish(p_in, p_out, p_sems)

    outs = pl.pallas_call(
        wrapped, name=name, grid=grid, in_specs=list(in_specs) + [ANY] * pi, out_specs=list(out_specs) + [ANY] * po,
        out_shape=list(out_shape) + list(plan.out_shape), scratch_shapes=list(scratch) + list(plan.sems),
        input_output_aliases=aliases, compiler_params=_cp(("arbitrary",) * len(grid), vmem),
    )(*args, *plan.inputs)
    return list(outs[:no]), list(outs[no:])


def broadcast_from_0(w, name):
    R = w.shape[0]
    h = R // 2

    def body(w_ref, o_ref, send, recv, loc):
        x, y, c = _coords()
        blk = o_ref.at[0]

        def copy(k, i, to):
            rows = blk.at[pl.ds(i * h, h)]
            return pltpu.make_async_remote_copy(src_ref=rows, dst_ref=rows, send_sem=send.at[k], recv_sem=recv.at[k],
                                                device_id=to, device_id_type=MESH)

        to_x = [copy(0, 0, (1, 0, 0)), copy(2, 1, (1, 0, 0))]
        to_y = [copy(1, 1, (0, 1, 0)), copy(3, 0, (0, 1, 0))]
        x_on, y_on = copy(4, 0, (1, 1, 0)), copy(5, 1, (1, 1, 0))
        sib = [copy(6, 0, (x, y, 1)), copy(7, 1, (x, y, 1))]
        south = c == 0

        @pl.when(south & (x == 0) & (y == 0))
        def _():
            mine = pltpu.make_async_copy(w_ref, blk, loc.at[0])
            mine.start()
            mine.wait()
            for cp in (to_x[0], to_y[0], sib[0], sib[1]):
                cp.start()
            to_x[0].wait_send()
            to_y[0].wait_send()
            to_x[1].start()
            to_y[1].start()
            for cp in (to_x[1], to_y[1], sib[0], sib[1]):
                cp.wait_send()

        def neighbour(got, on, first):
            got[0].wait_recv()
            on.start()
            sib[first].start()
            got[1].wait_recv()
            sib[1 - first].start()
            for cp in (on, sib[0], sib[1]):
                cp.wait_send()

        pl.when(south & (x == 1) & (y == 0))(lambda: neighbour(to_x, x_on, 0))
        pl.when(south & (x == 0) & (y == 1))(lambda: neighbour(to_y, y_on, 1))

        @pl.when(south & (x == 1) & (y == 1))
        def _():
            x_on.wait_recv()
            sib[0].start()
            y_on.wait_recv()
            sib[1].start()
            sib[0].wait_send()
            sib[1].wait_send()

        @pl.when(c == 1)
        def _():
            sib[0].wait_recv()
            sib[1].wait_recv()

    return pl.pallas_call(
        body, name=name, in_specs=[ANY], out_specs=ANY, out_shape=S((NDEV,) + w.shape, w.dtype),
        scratch_shapes=[pltpu.SemaphoreType.DMA((8,)), pltpu.SemaphoreType.DMA((8,)), pltpu.SemaphoreType.DMA((1,))])(w)


def mm_tn(a, b, name, tn=512, into=None, block0=0, nblocks=None):
    T, K = a.shape
    N = b.shape[1]
    tn = min(tn, N)
    nblocks = nblocks or (N // tn if into is None else into.shape[0])

    def body(a_ref, b_ref, *rest):
        rest[-1][...] = _dot_tn(a_ref[...], b_ref[...]).astype(GRAD_DT)

    args, in_specs, aliases = [a, b], [_resident((T, K)), pl.BlockSpec((T, tn), lambda j: (0, j))], {}
    if into is not None:
        args.append(into)
        in_specs.append(ANY)
        aliases = {2: 0}
    (out,), _ = _call(body, args, name=name, grid=(N // tn,), in_specs=in_specs,
                      out_specs=[pl.BlockSpec((None, K, tn), lambda j: (block0 + j, 0, 0))],
                      out_shape=[S((nblocks, K, tn), GRAD_DT)], sem=("parallel",), vmem=VMEM_LIMIT, aliases=aliases)
    return out


def grad_w_in_rest(xb, dh, dcg, dbg, dga, dgb, plan):
    T = xb.shape[0]
    order = ((0, 0), (1, 1), (2, 2), (3, 3), (4, 3), (5, 4), (6, 4))

    def body(x_ref, *refs):
        o_ref = refs[-1]
        j = pl.program_id(0)
        for step, opnd in order:
            @pl.when(j == step)
            def _(opnd=opnd):
                o_ref[...] = _dot_tn(x_ref[...], refs[opnd][...]).astype(GRAD_DT)

    once = lambda: pl.BlockSpec((T, W), lambda j: (0, 0), pipeline_mode=pl.Buffered(1))
    (out,), sent = _call(
        body, [xb, dh, dcg, dbg, dga, dgb], name="grad_w_in_rest", grid=(len(order),),
        in_specs=[_resident((T, D)), once(), once(), once(),
                  pl.BlockSpec((T, W), lambda j: (0, jnp.clip(j - 3, 0, 1))),
                  pl.BlockSpec((T, W), lambda j: (0, jnp.clip(j - 5, 0, 1)))],
        out_specs=[pl.BlockSpec((None, D, W), lambda j: (1 + j, 0, 0))],
        out_shape=[S((NDEV, D, W), GRAD_DT)], sem=("arbitrary",), vmem=VMEM_LIMIT, plan=plan)
    return out, sent


def mm_tn_rows(a, b, name, tk=256, plan=None):
    T, K = a.shape
    N = b.shape[1]
    tk = min(tk, K)

    def body(a_ref, b_ref, o_ref):
        o_ref[...] = _dot_tn(a_ref[...], b_ref[...]).astype(GRAD_DT)

    (out,), sent = _call(body, [a, b], name=name, grid=(K // tk,),
                         in_specs=[pl.BlockSpec((T, tk), lambda i: (0, i)), _resident((T, N))],
                         out_specs=[pl.BlockSpec((tk, N), lambda i: (i, 0))], out_shape=[S((K, N), GRAD_DT)],
                         sem=("parallel",), vmem=VMEM_LIMIT, plan=plan)
    return out, sent


def prep_weights(ws):
    def body(*refs):
        for i in range(len(ws)):
            refs[len(ws) + i][...] = refs[i][...].astype(bf16)

    return pl.pallas_call(body, name="prep_weights", out_shape=[S(w.shape, bf16) for w in ws],
                          compiler_params=_cp(None, VMEM_LIMIT))(*ws)


REST_BLOCKS = (4, 5, 6, 7, 1, 2, 3)
REST_COLS = len(REST_BLOCKS) * W


def in_proj_u(x, win_g, b_in):
    T = x.shape[0]
    tm = min(1024, T)

    def body(x_ref, w_ref, b_ref, u_ref, xb_ref):
        xb = x_ref[...].astype(bf16)
        xb_ref[...] = xb
        u_ref[...] = _dot(xb, w_ref[...]) + b_ref[...]

    row = pl.BlockSpec((tm, D), lambda i: (i, 0))
    return pl.pallas_call(
        body, name="in_proj_u", grid=(T // tm,),
        in_specs=[row, pl.BlockSpec((None, D, W), lambda i: (0, 0, 0)), pl.BlockSpec((1, W), lambda i: (0, 0))],
        out_specs=[pl.BlockSpec((tm, W), lambda i: (i, 0)), row],
        out_shape=[S((T, W), f32), S((T, D), bf16)], compiler_params=_cp(("parallel",), VMEM_LIMIT),
    )(x, win_g, b_in)


def in_proj_rest(xb, win_g, b_in, plan):
    T = xb.shape[0]
    tm = min(512, T)

    def body(x_ref, w_ref, b_ref, o_ref):
        xb_ = x_ref[...]
        for i, k in enumerate(REST_BLOCKS):
            o_ref[:, i * W:(i + 1) * W] = _dot(xb_, w_ref[k]) + b_ref[:, k * W:(k + 1) * W]

    return _call(
        body, [xb, win_g, b_in], name="in_proj_rest", grid=(T // tm,),
        in_specs=[pl.BlockSpec((tm, D), lambda i: (i, 0)), _resident((NDEV, D, W)), _resident((1, IN_COLS))],
        out_specs=[pl.BlockSpec((tm, REST_COLS), lambda i: (i, 0))],
        out_shape=[S((T, REST_COLS), f32)], vmem=VMEM_LIMIT, plan=plan, relay_step=T // tm - 2)


def to_perm(a, cb0, name):
    T = a.shape[0]
    L = T // NC

    def body(a_ref, o_ref):
        def step(jb, carry):
            j0 = pl.multiple_of(jb * 8, 8)
            for q in range(NC // 8):
                x = jnp.stack([a_ref[pl.ds((8 * q + c) * L + j0, 8), :] for c in range(8)], axis=0)
                y = jnp.swapaxes(x, 0, 1)
                for j in range(8):
                    o_ref[pl.ds((j0 + j) * NC + 8 * q, 8), :] = y[j]
            return carry

        lax.fori_loop(0, L // 8, step, 0)

    return pl.pallas_call(
        body, name=name, grid=(W // LANE,),
        in_specs=[pl.BlockSpec((T, LANE), lambda k: (0, cb0 + k))], out_specs=pl.BlockSpec((T, LANE), lambda k: (0, k)),
        out_shape=S((T, W), f32), compiler_params=_cp(("parallel",), VMEM_LIMIT),
    )(a)


def from_perm(a, name, out_dtype=f32, plan=None):
    T = a.shape[0]
    L = T // NC

    def body(a_ref, o_ref):
        def step(jb, carry):
            j0 = pl.multiple_of(jb * 16, 16)
            for q in range(NC // 8):
                halves = []
                for h in range(2):
                    x = jnp.stack([a_ref[pl.ds((j0 + 8 * h + j) * NC + 8 * q, 8), :] for j in range(8)], axis=0)
                    halves.append(jnp.swapaxes(x, 0, 1))
                for c in range(8):
                    o_ref[pl.ds((8 * q + c) * L + j0, 16), :] = jnp.concatenate(
                        [halves[0][c], halves[1][c]], axis=0).astype(out_dtype)
            return carry

        lax.fori_loop(0, L // 16, step, 0)

    slab = pl.BlockSpec((T, LANE), lambda k: (0, k))
    return _call(body, [a], name=name, grid=(W // LANE,), in_specs=[slab], out_specs=[slab],
                 out_shape=[S((T, W), out_dtype)], sem=("parallel",), vmem=VMEM_LIMIT, plan=plan)


def _disc(lr, li, ldt):
    dt = jnp.exp(ldt)
    mag = jnp.exp(lr * dt)
    lbr = mag * jnp.cos(li * dt)
    lbi = mag * jnp.sin(li * dt)
    den = lr * lr + li * li
    nr = lbr - 1.0
    return lbr, lbi, (nr * lr + lbi * li) / den, (lbi * lr - nr * li) / den


def _per_channel(f):
    return jnp.broadcast_to(f[:, None, :], (NG, GC, NP)).reshape(NG * GC, NP)


def ssm_params(lam_re, lam_im, log_dt, br, bi):
    def body(lr_ref, li_ref, ldt_ref, br_ref, bi_ref, lbr_ref, lbi_ref, fr_ref, fi_ref, bbr_ref, bbi_ref):
        lbr, lbi, fr, fi = _disc(lr_ref[...], li_ref[...], ldt_ref[...])
        lbr_ref[...], lbi_ref[...], fr_ref[...], fi_ref[...] = lbr, lbi, fr, fi
        fr_, fi_, br_, bi_ = _per_channel(fr), _per_channel(fi), br_ref[...], bi_ref[...]
        bbr_ref[...] = fr_ * br_ - fi_ * bi_
        bbi_ref[...] = fr_ * bi_ + fi_ * br_

    return pl.pallas_call(body, name="ssm_params", out_shape=[S((NG, NP), f32)] * 4 + [S((NG * GC, NP), f32)] * 2)(
        lam_re, lam_im, log_dt, br, bi)


SCAN_UNROLL = 4
SCAN_LANES = 2 * LANE


def _steps(n, body, carry):
    main = n // SCAN_UNROLL

    def trip(t, c):
        for q in range(SCAN_UNROLL):
            c = body(t * SCAN_UNROLL + q, c)
        return c

    carry = lax.fori_loop(0, main, trip, carry)
    for i in range(main * SCAN_UNROLL, n):
        carry = body(i, carry)
    return carry


def _scan_body(T):
    L = T // NC
    RB = min(512, T)
    nsq = int(round(math.log2(L)))
    assert 2 ** nsq == L and T % RB == 0 and L % 16 == 0

    def rows(i):
        return pl.ds(pl.multiple_of(i * RB, RB), RB)

    def tile(j):
        return pl.ds(j * NC if isinstance(j, int) else pl.multiple_of(j * NC, NC), NC)

    def forward_states(u_ref, wb_ref, lbr_ref, lbi_ref, sre, sim, ere, eim):
        def bproj(i, carry):
            bu = _dot(u_ref[rows(i), :].astype(bf16), wb_ref[...])
            sre[rows(i), :] = bu[:, :SW]
            sim[rows(i), :] = bu[:, SW:]
            return carry

        lax.fori_loop(0, T // RB, bproj, 0)
        for lb in range(SW // SCAN_LANES):
            ls = slice(lb * SCAN_LANES, (lb + 1) * SCAN_LANES)
            ar = jnp.broadcast_to(lbr_ref[:, ls], (NC, SCAN_LANES))
            ai = jnp.broadcast_to(lbi_ref[:, ls], (NC, SCAN_LANES))

            def step(j, carry):
                xr, xi = carry
                nr = ar * xr - ai * xi + sre[tile(j), ls]
                ni = ar * xi + ai * xr + sim[tile(j), ls]
                sre[tile(j), ls] = nr
                sim[tile(j), ls] = ni
                return nr, ni

            zero = jnp.zeros((NC, SCAN_LANES), f32)
            _steps(L, step, (zero, zero))
            pr, pi = lbr_ref[:, ls], lbi_ref[:, ls]
            for _ in range(nsq):
                pr, pi = pr * pr - pi * pi, 2.0 * pr * pi
            er = jnp.zeros((1, SCAN_LANES), f32)
            ei = er
            ere[0:1, ls] = er
            eim[0:1, ls] = ei
            base = (L - 1) * NC
            for c in range(1, NC):
                lr_ = sre[base + c - 1:base + c, ls]
                li_ = sim[base + c - 1:base + c, ls]
                er, ei = lr_ + pr * er - pi * ei, li_ + pr * ei + pi * er
                ere[c:c + 1, ls] = er
                eim[c:c + 1, ls] = ei
            e_r, e_i = ere[:, ls].reshape(NC // 8, 8, SCAN_LANES), eim[:, ls].reshape(NC // 8, 8, SCAN_LANES)
            ar8, ai8 = ar[0:8], ai[0:8]

            def fix(j, carry):
                pwr, pwi = carry
                xr = sre[tile(j), ls].reshape(NC // 8, 8, SCAN_LANES) + (pwr * e_r - pwi * e_i)
                xi = sim[tile(j), ls].reshape(NC // 8, 8, SCAN_LANES) + (pwr * e_i + pwi * e_r)
                sre[tile(j), ls] = xr.reshape(NC, SCAN_LANES)
                sim[tile(j), ls] = xi.reshape(NC, SCAN_LANES)
                return pwr * ar8 - pwi * ai8, pwr * ai8 + pwi * ar8

            _steps(L, fix, (ar8, ai8))

    return L, RB, nsq, rows, tile, forward_states


def ssm_fwd(u_p, wb, wc, lbr, lbi, dsk, plan):
    T = u_p.shape[0]
    L, RB, nsq, rows, tile, forward_states = _scan_body(T)
    nslab = W // LANE

    def body(u_ref, wb_ref, wc_ref, lbr_ref, lbi_ref, d_ref, y_ref, xr_ref, xi_ref, sre, sim, ere, eim):
        forward_states(u_ref, wb_ref, lbr_ref, lbi_ref, sre, sim, ere, eim)

        def cproj(i, carry):
            xr, xi = sre[rows(i), :].astype(bf16), sim[rows(i), :].astype(bf16)
            xr_ref[rows(i), :] = xr
            xi_ref[rows(i), :] = xi
            y = _dot(xr, wc_ref[0:SW, :]) + _dot(xi, wc_ref[SW:, :])
            y_ref[rows(i), :] = y + d_ref[...] * u_ref[rows(i), :]
            return carry

        lax.fori_loop(0, T // RB, cproj, 0)

    slab = pl.BlockSpec((T, LANE), lambda k: (0, k))
    states = pl.BlockSpec((T, SW), lambda k: (0, k))
    return _call(
        body, [u_p, wb, wc, lbr, lbi, dsk], name="ssm_fwd", grid=(nslab,),
        in_specs=[slab, pl.BlockSpec((None, LANE, 2 * SW), lambda k: (k, 0, 0)),
                  pl.BlockSpec((None, 2 * SW, LANE), lambda k: (k, 0, 0)),
                  pl.BlockSpec((None, 1, SW), lambda k: (k, 0, 0)), pl.BlockSpec((None, 1, SW), lambda k: (k, 0, 0)),
                  pl.BlockSpec((None, 1, LANE), lambda k: (k, 0, 0))],
        out_specs=[slab, states, states], out_shape=[S((T, W), f32), S((T, nslab * SW), bf16), S((T, nslab * SW), bf16)],
        scratch=[pltpu.VMEM((T, SW), f32), pltpu.VMEM((T, SW), f32), pltpu.VMEM((NC, SW), f32), pltpu.VMEM((NC, SW), f32)],
        vmem=VMEM_LIMIT, plan=plan)


def ssm_bwd(u_p, dy_p, xr, xi, wbT, wcT, lbr, lbi, dsk, plan):
    T = u_p.shape[0]
    L, RB, nsq, rows, tile, _ = _scan_body(T)

    def body(u_ref, dy_ref, sre, sim, wbT_ref, wcT_ref, lbr_ref, lbi_ref, d_ref,
             du_ref, dwb_ref, dwc_ref, dlr_ref, dli_ref, dd_ref, su_ref, gre, gim, ere, eim):
        def dstate(i, carry):
            g = _dot(dy_ref[rows(i), :].astype(bf16), wcT_ref[...])
            gre[rows(i), :] = g[:, :SW]
            gim[rows(i), :] = g[:, SW:]
            return carry

        lax.fori_loop(0, T // RB, dstate, 0)
        row = lax.broadcasted_iota(jnp.int32, (NC, SCAN_LANES), 0)
        for lb in range(SW // SCAN_LANES):
            ls = slice(lb * SCAN_LANES, (lb + 1) * SCAN_LANES)
            ar = jnp.broadcast_to(lbr_ref[:, ls], (NC, SCAN_LANES))
            ai = jnp.broadcast_to(lbi_ref[:, ls], (NC, SCAN_LANES))

            def step(i, carry):
                gr, gi = carry
                j = L - 1 - i
                nr = ar * gr + ai * gi + gre[tile(j), ls]
                ni = ar * gi - ai * gr + gim[tile(j), ls]
                gre[tile(j), ls] = nr
                gim[tile(j), ls] = ni
                return nr, ni

            zero = jnp.zeros((NC, SCAN_LANES), f32)
            _steps(L, step, (zero, zero))
            pr, pi = lbr_ref[:, ls], -lbi_ref[:, ls]
            for _ in range(nsq):
                pr, pi = pr * pr - pi * pi, 2.0 * pr * pi
            er = jnp.zeros((1, SCAN_LANES), f32)
            ei = er
            ere[NC - 1:NC, ls] = er
            eim[NC - 1:NC, ls] = ei
            for c in range(NC - 2, -1, -1):
                lr_ = gre[c + 1:c + 2, ls]
                li_ = gim[c + 1:c + 2, ls]
                er, ei = lr_ + pr * er - pi * ei, li_ + pr * ei + pi * er
                ere[c:c + 1, ls] = er
                eim[c:c + 1, ls] = ei
            e_r, e_i = ere[:, ls].reshape(NC // 8, 8, SCAN_LANES), eim[:, ls].reshape(NC // 8, 8, SCAN_LANES)
            ar8, ai8 = ar[0:8], ai[0:8]

            def fixed(j, pwr, pwi):
                gr = (gre[tile(j), ls].reshape(NC // 8, 8, SCAN_LANES) + (pwr * e_r - pwi * e_i)).reshape(NC, SCAN_LANES)
                gi = (gim[tile(j), ls].reshape(NC // 8, 8, SCAN_LANES) + (pwr * e_i + pwi * e_r)).reshape(NC, SCAN_LANES)
                gre[tile(j), ls] = gr
                gim[tile(j), ls] = gi
                return gr, gi

            def fix(i, carry):
                pwr, pwi, accr, acci = carry
                j = L - 1 - i
                gr, gi = fixed(j, pwr, pwi)
                xr, xi = sre[tile(j - 1), ls].astype(f32), sim[tile(j - 1), ls].astype(f32)
                return (pwr * ar8 + pwi * ai8, pwi * ar8 - pwr * ai8,
                        accr + gr * xr + gi * xi, acci + gi * xr - gr * xi)

            pwr, pwi, accr, acci = _steps(L - 1, fix, (ar8, -ai8, zero, zero))
            gr, gi = fixed(0, pwr, pwi)
            xr = jnp.where(row == 0, 0.0, pltpu.roll(sre[tile(L - 1), ls].astype(f32), 1, axis=0))
            xi = jnp.where(row == 0, 0.0, pltpu.roll(sim[tile(L - 1), ls].astype(f32), 1, axis=0))
            accr = accr + gr * xr + gi * xi
            acci = acci + gi * xr - gr * xi
            dlr_ref[:, ls] = jnp.sum(accr, axis=0, keepdims=True)
            dli_ref[:, ls] = jnp.sum(acci, axis=0, keepdims=True)

        dwb_ref[...] = jnp.zeros_like(dwb_ref)
        dwc_ref[...] = jnp.zeros_like(dwc_ref)
        dd_ref[...] = jnp.zeros_like(dd_ref)
        su_ref[...] = jnp.zeros_like(su_ref)

        def finish(i, carry):
            u32, dy32 = u_ref[rows(i), :], dy_ref[rows(i), :]
            ub, dyb = u32.astype(bf16), dy32.astype(bf16)
            gr, gi = gre[rows(i), :].astype(bf16), gim[rows(i), :].astype(bf16)
            du = _dot(gr, wbT_ref[0:SW, :]) + _dot(gi, wbT_ref[SW:, :]) + dy32 * d_ref[...]
            du_ref[rows(i), :] = du
            su_ref[...] += jnp.sum(du, axis=0, keepdims=True)
            dwb_ref[:, 0:SW] += _dot_tn(ub, gr)
            dwb_ref[:, SW:] += _dot_tn(ub, gi)
            dwc_ref[:, 0:SW] += _dot_tn(dyb, sre[rows(i), :])
            dwc_ref[:, SW:] += _dot_tn(dyb, sim[rows(i), :])
            dd_ref[...] += jnp.sum(dy32 * u32, axis=0, keepdims=True)
            return carry

        lax.fori_loop(0, T // RB, finish, 0)

    slab = pl.BlockSpec((T, LANE), lambda k: (0, k))
    wide = pl.BlockSpec((None, LANE, 2 * SW), lambda k: (k, 0, 0))
    tall = pl.BlockSpec((None, 2 * SW, LANE), lambda k: (k, 0, 0))
    vec = pl.BlockSpec((None, 1, SW), lambda k: (k, 0, 0))
    vecd = pl.BlockSpec((None, 1, LANE), lambda k: (k, 0, 0))
    states = pl.BlockSpec((T, SW), lambda k: (0, k))
    nslab = W // LANE
    return _call(
        body, [u_p, dy_p, xr, xi, wbT, wcT, lbr, lbi, dsk], name="ssm_bwd", grid=(nslab,),
        in_specs=[slab, slab, states, states, tall, wide, vec, vec, vecd],
        out_specs=[slab, wide, wide, vec, vec, vecd, vecd],
        out_shape=[S((T, W), f32), S((nslab, LANE, 2 * SW), f32), S((nslab, LANE, 2 * SW), f32),
                   S((nslab, 1, SW), f32), S((nslab, 1, SW), f32), S((nslab, 1, LANE), f32), S((nslab, 1, LANE), f32)],
        scratch=[pltpu.VMEM((T, SW), f32)] * 2 + [pltpu.VMEM((NC, SW), f32)] * 2, vmem=VMEM_LIMIT, plan=plan)


def glu_fwd(yn, glu_w, glu_b):
    T = yn.shape[0]
    tm = min(512, T)

    def body(y_ref, w_ref, b_ref, o_ref):
        g = _gelu(y_ref[...])
        o_ref[...] = (g * _sigmoid(_dot(g.astype(bf16), w_ref[...]) + b_ref[...])).astype(bf16)

    return pl.pallas_call(
        body, name="glu_fwd", grid=(T // tm,),
        in_specs=[pl.BlockSpec((tm, W), lambda i: (i, 0)), pl.BlockSpec((W, W), lambda i: (0, 0)), pl.BlockSpec((1, W), lambda i: (0, 0))],
        out_specs=pl.BlockSpec((tm, W), lambda i: (i, 0)), out_shape=S((T, W), bf16), compiler_params=_cp(("parallel",)),
    )(yn, glu_w, glu_b)


def _shift_rows(cur, prev8, k):
    return pltpu.roll(jnp.concatenate([prev8, cur], axis=0), k, axis=0)[8:]


def _lift_rows(cur, next8, k):
    n = cur.shape[0]
    return pltpu.roll(jnp.concatenate([cur, next8], axis=0), n + 8 - k, axis=0)[:n]


def conv_fwd(proj, conv_w):
    T = proj.shape[0]
    RB = min(512, T)

    def body(h_ref, c_ref, b_ref, w_ref, o_ref):
        w0, w1, w2 = w_ref[0:1, :], w_ref[1:2, :], w_ref[2:3, :]

        def blk(i, carry):
            r0 = pl.multiple_of(i * RB, RB)
            rs = pl.ds(r0, RB)
            ch = c_ref[rs, :] * h_ref[rs, :]
            pr = pl.ds(jnp.maximum(r0 - 8, 0), 8)
            prev = jnp.where(i > 0, c_ref[pr, :] * h_ref[pr, :], 0.0)
            z = w2 * ch + w1 * _shift_rows(ch, prev, 1) + w0 * _shift_rows(ch, prev, 2)
            o_ref[rs, :] = (b_ref[rs, :] * z).astype(bf16)
            return carry

        lax.fori_loop(0, T // RB, blk, 0)

    nb = W // LANE
    return pl.pallas_call(
        body, name="conv_fwd", grid=(nb,),
        in_specs=[pl.BlockSpec((T, LANE), lambda k: (0, 4 * nb + k)), pl.BlockSpec((T, LANE), lambda k: (0, 5 * nb + k)),
                  pl.BlockSpec((T, LANE), lambda k: (0, 6 * nb + k)),pl.BlockSpec((3, LANE), lambda k: (0, k))],
        out_specs=pl.BlockSpec((T, LANE), lambda k: (0, k)), out_shape=S((T, W), bf16),
        compiler_params=_cp(("parallel",), VMEM_LIMIT),
    )(proj, proj, proj, conv_w)


def _dense_columns(blocks_ref, dense_ref):
    for k in range(NDEV):
        dense_ref[:, k * LANE:(k + 1) * LANE] = blocks_ref[k]


def merge_fwd(ya, yb, wso, wco, proj, plan):
    T = ya.shape[0]
    tm = min(1024, T)

    def body(ya_ref, yb_ref, wa_ref, wb_ref, ga_ref, gb_ref, o_ref, wa_s, wb_s):
        @pl.when(pl.program_id(0) == 0)
        def _():
            _dense_columns(wa_ref, wa_s)
            _dense_columns(wb_ref, wb_s)

        o_ref[...] = (_sigmoid(ga_ref[...]) * _dot(ya_ref[...], wa_s[...])
                      + _sigmoid(gb_ref[...]) * _dot(yb_ref[...], wb_s[...])).astype(bf16)

    act = pl.BlockSpec((tm, W), lambda i: (i, 0))
    return _call(
        body, [ya, yb, wso, wco, proj, proj], name="merge_fwd", grid=(T // tm,),
        in_specs=[act, act, _resident((NDEV, W, LANE)), _resident((NDEV, W, LANE)),
                  pl.BlockSpec((tm, D), lambda i: (i, 0)), pl.BlockSpec((tm, D), lambda i: (i, 1))],
        out_specs=[pl.BlockSpec((tm, D), lambda i: (i, 0))], out_shape=[S((T, D), bf16)],
        scratch=[pltpu.VMEM((W, D), bf16), pltpu.VMEM((W, D), bf16)], vmem=VMEM_LIMIT, plan=plan)


def mix_ln1(merged, w_o, x, g1, b1, plan):
    T = x.shape[0]
    tm = min(512, T)

    def body(m_ref, w_ref, x_ref, g_ref, b_ref, r_ref, x1_ref):
        for rs in _row_parts(tm):
            r = ALPHA * x_ref[rs, :] + _dot(m_ref[rs, :], w_ref[...])
            r_ref[rs, :] = r
            xhat, _ = _ln_stats(r)
            x1_ref[rs, :] = (xhat * g_ref[...] + b_ref[...]).astype(bf16)

    row = pl.BlockSpec((tm, D), lambda i: (i, 0))
    vec = pl.BlockSpec((1, D), lambda i: (0, 0))
    return _call(
        body, [merged, w_o, x, g1, b1], name="mix_ln1", grid=(T // tm,),
        in_specs=[row, _resident((D, D)), row, vec, vec],
        out_specs=[row, row], out_shape=[S((T, D), f32), S((T, D), bf16)], sem=("parallel",), vmem=VMEM_LIMIT, plan=plan,
        relay_step=T // tm - 2)


FT = 256


def gate_up(x1b, wgT, wuT, plan):
    T = x1b.shape[0]
    tm = min(512, T)

    def body(x_ref, wg_ref, wu_ref, g_ref, u_ref, h_ref):
        x = x_ref[...]
        for n in range(F // FT):
            cs = slice(n * FT, (n + 1) * FT)
            g = _dot_nt(x, wg_ref[cs, :])
            u = _dot_nt(x, wu_ref[cs, :])
            g_ref[:, cs] = g.astype(bf16)
            u_ref[:, cs] = u.astype(bf16)
            h_ref[:, cs] = (g * _sigmoid(g) * u).astype(bf16)

    osp = pl.BlockSpec((tm, F), lambda i: (i, 0))
    return _call(
        body, [x1b, wgT, wuT], name="gate_up", grid=(T // tm,),
        in_specs=[pl.BlockSpec((tm, D), lambda i: (i, 0)), _resident((F, D)), _resident((F, D))],
        out_specs=[osp, osp, osp], out_shape=[S((T, F), bf16)] * 3, vmem=VMEM_LIMIT, plan=plan, relay_step=T // tm - 3)


def down_loss(hid, w_down, r1, g1, b1, g2, b2, target):
    T = hid.shape[0]
    tm = min(512, T)

    def body(h_ref, w_ref, r1_ref, g1_ref, b1_ref, g2_ref, b2_ref, t_ref, dr_ref, drb_ref, loss_ref, dg_ref, db_ref):
        @pl.when(pl.program_id(0) == 0)
        def _():
            loss_ref[...] = jnp.zeros_like(loss_ref)
            dg_ref[...] = jnp.zeros_like(dg_ref)
            db_ref[...] = jnp.zeros_like(db_ref)

        for rs in _row_parts(tm):
            xh1, _ = _ln_stats(r1_ref[rs, :])
            x1 = xh1 * g1_ref[...] + b1_ref[...]
            r2 = ALPHA * x1 + _dot(h_ref[rs, :], w_ref[...])
            xh2, rstd2 = _ln_stats(r2)
            err = xh2 * g2_ref[...] + b2_ref[...] - t_ref[rs, :]
            loss_ref[...] += jnp.sum(jnp.mean(err * err, axis=-1, keepdims=True), axis=0, keepdims=True)
            dy = err * (1.0 / D)
            dg_ref[...] += jnp.sum(dy * xh2, axis=0, keepdims=True)
            db_ref[...] += jnp.sum(dy, axis=0, keepdims=True)
            dr = _ln_bwd(dy, xh2, rstd2, g2_ref[...])
            dr_ref[rs, :] = dr
            drb_ref[rs, :] = dr.astype(bf16)

    row = pl.BlockSpec((tm, D), lambda i: (i, 0))
    vec = pl.BlockSpec((1, D), lambda i: (0, 0))
    return pl.pallas_call(
        body, name="down_loss", grid=(T // tm,),
        in_specs=[pl.BlockSpec((tm, F), lambda i: (i, 0)), _resident((F, D)), row, vec, vec, vec, vec, row],
        out_specs=[row, row, pl.BlockSpec((1, 1), lambda i: (0, 0)), vec, vec],
        out_shape=[S((T, D), f32), S((T, D), bf16), S((1, 1), f32), S((1, D), f32), S((1, D), f32)],
        compiler_params=_cp(("arbitrary",), VMEM_LIMIT),
    )(hid, w_down, r1, g1, b1, g2, b2, target)


def ffn_bwd_act(dffn, w_down, gate, up, plan):
    T = dffn.shape[0]
    tm = min(512, T)

    def body(d_ref, w_ref, g_ref, u_ref, dg_ref, du_ref):
        for n in range(F // FT):
            cs = slice(n * FT, (n + 1) * FT)
            for rs in _row_parts(tm):
                dh = _dot_nt(d_ref[rs, :], w_ref[cs, :])
                g, u = g_ref[rs, cs].astype(f32), u_ref[rs, cs].astype(f32)
                sg = _sigmoid(g)
                t = g * sg
                du_ref[rs, cs] = (dh * t).astype(bf16)
                dg_ref[rs, cs] = (dh * u * (sg + t - t * sg)).astype(bf16)

    osp = pl.BlockSpec((tm, F), lambda i: (i, 0))
    return _call(
        body, [dffn, w_down, gate, up], name="ffn_bwd_act", grid=(T // tm,),
        in_specs=[pl.BlockSpec((tm, D), lambda i: (i, 0)), _resident((F, D)), osp, osp],
        out_specs=[osp, osp], out_shape=[S((T, F), bf16)] * 2, sem=("parallel",), vmem=VMEM_LIMIT, plan=plan)


def ffn_bwd_x(dgate, dup, wgT, wuT, dr2, r1, g1, plan):
    T = dr2.shape[0]
    tm = min(512, T)

    def body(dg_ref, du_ref, wg_ref, wu_ref, dr2_ref, r1_ref, g1_ref, dr_ref, drb_ref, dgam_ref, dbet_ref):
        @pl.when(pl.program_id(0) == 0)
        def _():
            dgam_ref[...] = jnp.zeros_like(dgam_ref)
            dbet_ref[...] = jnp.zeros_like(dbet_ref)

        for rs in _row_parts(tm):
            dx1 = ALPHA * dr2_ref[rs, :] + _dot(dg_ref[rs, :], wg_ref[...]) + _dot(du_ref[rs, :], wu_ref[...])
            xh, rstd = _ln_stats(r1_ref[rs, :])
            dgam_ref[...] += jnp.sum(dx1 * xh, axis=0, keepdims=True)
            dbet_ref[...] += jnp.sum(dx1, axis=0, keepdims=True)
            dr = _ln_bwd(dx1, xh, rstd, g1_ref[...])
            dr_ref[rs, :] = dr
            drb_ref[rs, :] = dr.astype(bf16)

    row = pl.BlockSpec((tm, D), lambda i: (i, 0))
    wide = pl.BlockSpec((tm, F), lambda i: (i, 0))
    wsp = _resident((F, D))
    vec = pl.BlockSpec((1, D), lambda i: (0, 0))
    return _call(
        body, [dgate, dup, wgT, wuT, dr2, r1, g1], name="ffn_bwd_x", grid=(T // tm,),
        in_specs=[wide, wide, wsp, wsp, row, row, vec],
        out_specs=[row, row, vec, vec], out_shape=[S((T, D), f32), S((T, D), bf16), S((1, D), f32), S((1, D), f32)],
        vmem=VMEM_LIMIT, plan=plan)


def merge_bwd(dmix, w_o, ya, yb, wso, wco, proj, plan):
    T = dmix.shape[0]
    tm = min(512, T)

    def body(dm_ref, wo_ref, ya_ref, yb_ref, wa_ref, wb_ref, ga_ref, gb_ref, dya_ref, dyb_ref, dga_ref, dgb_ref, sa_ref, sb_ref,
             wa_s, wb_s):
        @pl.when(pl.program_id(0) == 0)
        def _():
            _dense_columns(wa_ref, wa_s)
            _dense_columns(wb_ref, wb_s)

        dmer = _dot_nt(dm_ref[...], wo_ref[...])
        sa, sb = _sigmoid(ga_ref[...]), _sigmoid(gb_ref[...])
        dya_ref[...] = (dmer * sa).astype(bf16)
        dyb_ref[...] = (dmer * sb).astype(bf16)
        dga = dmer * _dot(ya_ref[...], wa_s[...]) * sa * (1.0 - sa)
        dgb = dmer * _dot(yb_ref[...], wb_s[...]) * sb * (1.0 - sb)
        dga_ref[...] = dga.astype(bf16)
        dgb_ref[...] = dgb.astype(bf16)
        sa_ref[...] = jnp.sum(dga, axis=0, keepdims=True)
        sb_ref[...] = jnp.sum(dgb, axis=0, keepdims=True)

    act = pl.BlockSpec((tm, W), lambda i: (i, 0))
    osp = pl.BlockSpec((tm, D), lambda i: (i, 0))
    ssp = pl.BlockSpec((None, 1, D), lambda i: (i, 0, 0))
    return _call(
        body, [dmix, w_o, ya, yb, wso, wco, proj, proj], name="merge_bwd", grid=(T // tm,),
        in_specs=[osp, _resident((D, D)), act, act, _resident((NDEV, W, LANE)), _resident((NDEV, W, LANE)),
                  pl.BlockSpec((tm, D), lambda i: (i, 0)), pl.BlockSpec((tm, D), lambda i: (i, 1))],
        out_specs=[osp, osp, osp, osp, ssp, ssp],
        out_shape=[S((T, D), bf16)] * 4 + [S((T // tm, 1, D), f32)] * 2,
        scratch=[pltpu.VMEM((W, D), bf16), pltpu.VMEM((W, D), bf16)], vmem=VMEM_LIMIT, plan=plan)


def branches_bwd_x(dYA, dYB, wso, wco, plan):
    T = dYA.shape[0]
    tm = min(1024, T)

    def body(da_ref, db_ref, wa_ref, wb_ref, oa_ref, ob_ref, wa_s, wb_s):
        @pl.when(pl.program_id(0) == 0)
        def _():
            _dense_columns(wa_ref, wa_s)
            _dense_columns(wb_ref, wb_s)

        oa_ref[...] = _dot_nt(da_ref[...], wa_s[...])
        ob_ref[...] = _dot_nt(db_ref[...], wb_s[...])

    row = pl.BlockSpec((tm, D), lambda i: (i, 0))
    osp = pl.BlockSpec((tm, W), lambda i: (i, 0))
    return _call(
        body, [dYA, dYB, wso, wco], name="branches_bwd_x", grid=(T // tm,),
        in_specs=[row, row, _resident((NDEV, W, LANE)), _resident((NDEV, W, LANE))],
        out_specs=[osp, osp], out_shape=[S((T, W), f32)] * 2,
        scratch=[pltpu.VMEM((W, D), bf16), pltpu.VMEM((W, D), bf16)], vmem=VMEM_LIMIT, plan=plan)


def branch_bwd_w(act, dY, name):
    T = act.shape[0]
    tk = W // 2

    def body(a_ref, d_ref, o_ref):
        res = _dot_tn(a_ref[...], d_ref[...])
        for k in range(NDEV):
            o_ref[k] = res[:, k * LANE:(k + 1) * LANE].astype(o_ref.dtype)

    return pl.pallas_call(
        body, name=name, grid=(W // tk,),
        in_specs=[pl.BlockSpec((T, tk), lambda i: (0, i)), _resident((T, D))],
        out_specs=pl.BlockSpec((NDEV, tk, LANE), lambda i: (0, i, 0)), out_shape=S((NDEV, W, LANE), GRAD_DT),
        compiler_params=_cp(("parallel",), VMEM_LIMIT),
    )(act, dY)


def glu_bwd(yn, dya, glu_w, glu_b, plan):
    T = yn.shape[0]
    tm = min(512, T)

    def body(y_ref, d_ref, w_ref, b_ref, dy_ref, dsp_ref, g_ref, db_ref):
        @pl.when(pl.program_id(0) == 0)
        def _():
            db_ref[...] = jnp.zeros_like(db_ref)

        y, dya_ = y_ref[...], d_ref[...]
        g = _gelu(y)
        gb = g.astype(bf16)
        s = _sigmoid(_dot(gb, w_ref[...]) + b_ref[...])
        dsp = dya_ * g * s * (1.0 - s)
        dspb = dsp.astype(bf16)
        dg = dya_ * s + _dot_nt(dspb, w_ref[...])
        dy_ref[...] = dg * _gelu_grad(y)
        dsp_ref[...] = dspb
        g_ref[...] = gb
        db_ref[...] += jnp.sum(dsp, axis=0, keepdims=True)

    row = pl.BlockSpec((tm, W), lambda i: (i, 0))
    vec = pl.BlockSpec((1, W), lambda i: (0, 0))
    return _call(
        body, [yn, dya, glu_w, glu_b], name="glu_bwd", grid=(T // tm,),
        in_specs=[row, row, pl.BlockSpec((W, W), lambda i: (0, 0)), vec],
        out_specs=[row, row, row, vec], out_shape=[S((T, W), f32), S((T, W), bf16), S((T, W), bf16), S((1, W), f32)],
        sem=("arbitrary",), plan=plan)


def conv_bwd(proj, dyb, conv_w, plan):
    T = proj.shape[0]
    RB = min(512, T)
    nrb = T // RB

    def body(h_ref, c_ref, b_ref, d_ref, w_ref, dh_ref, dc_ref, db_ref, dw_ref, s_ref):
        w0, w1, w2 = w_ref[0:1, :], w_ref[1:2, :], w_ref[2:3, :]

        def blk(i, carry):
            a0, a1, a2, sh, sc, sb = carry
            r0 = pl.multiple_of(i * RB, RB)
            rs = pl.ds(r0, RB)
            h, cg, bg, dyb_ = h_ref[rs, :], c_ref[rs, :], b_ref[rs, :], d_ref[rs, :]
            ch = cg * h
            pr = pl.ds(jnp.maximum(r0 - 8, 0), 8)
            prev = jnp.where(i > 0, c_ref[pr, :] * h_ref[pr, :], 0.0)
            ch1, ch2 = _shift_rows(ch, prev, 1), _shift_rows(ch, prev, 2)
            dbg = dyb_ * (w2 * ch + w1 * ch1 + w0 * ch2)
            db_ref[rs, :] = dbg.astype(bf16)
            dz = dyb_ * bg
            nx = pl.ds(jnp.minimum(r0 + RB, T - 8), 8)
            nxt = jnp.where(i < nrb - 1, d_ref[nx, :] * b_ref[nx, :], 0.0)
            dch = w2 * dz + w1 * _lift_rows(dz, nxt, 1) + w0 * _lift_rows(dz, nxt, 2)
            dcg, dh = dch * h, dch * cg
            dc_ref[rs, :] = dcg.astype(bf16)
            dh_ref[rs, :] = dh.astype(bf16)
            col = lambda v: jnp.sum(v, axis=0, keepdims=True)
            return (a0 + col(dz * ch2), a1 + col(dz * ch1), a2 + col(dz * ch), sh + col(dh), sc + col(dcg), sb + col(dbg))

        zero = jnp.zeros((1, LANE), f32)
        a0, a1, a2, sh, sc, sb = lax.fori_loop(0, nrb, blk, (zero,) * 6)
        dw_ref[0:1, :] = a0
        dw_ref[1:2, :] = a1
        dw_ref[2:3, :] = a2
        s_ref[0:1, :] = sh
        s_ref[1:2, :] = sc
        s_ref[2:3, :] = sb

    nb = W // LANE
    slab = pl.BlockSpec((T, LANE), lambda k: (0, k))
    three = pl.BlockSpec((3, LANE), lambda k: (0, k))
    return _call(
        body, [proj, proj, proj, dyb, conv_w], name="conv_bwd", grid=(nb,),
        in_specs=[pl.BlockSpec((T, LANE), lambda k: (0, 4 * nb + k)), pl.BlockSpec((T, LANE), lambda k: (0, 5 * nb + k)),
                  pl.BlockSpec((T, LANE), lambda k: (0, 6 * nb + k)), slab, three],
        out_specs=[slab, slab, slab, three, three],
        out_shape=[S((T, W), bf16)] * 3 + [S((3, W), f32)] * 2, sem=("parallel",), vmem=VMEM_LIMIT, plan=plan)


def in_proj_bwd_x(parts, win_g, base, scale, name, plan=None):
    T = base.shape[0]
    tm = min(512, T)
    n = len(parts)

    def body(*refs):
        p_refs, w_ref, b_ref, o_ref = refs[:n], refs[n], refs[n + 1], refs[n + 2]
        acc = scale * b_ref[...]
        for p_ref, (_, _, k) in zip(p_refs, parts):
            acc += _dot_nt(p_ref[...], w_ref[k])
        o_ref[...] = acc

    row = pl.BlockSpec((tm, D), lambda i: (i, 0))
    p_specs = [pl.BlockSpec((tm, W), (lambda i, cb=cb: (i, cb))) for _, cb, _ in parts]
    return _call(
        body, [a for a, _, _ in parts] + [win_g, base], name=name, grid=(T // tm,),
        in_specs=p_specs + [_resident((NDEV, D, W)), row],
        out_specs=[row], out_shape=[S((T, D), f32)], vmem=VMEM_LIMIT, plan=plan)


def ssm_param_bwd(lam_re, lam_im, log_dt, fr, fi, br, bi, dwb, dwcT, dlbr, dlbi):
    def body(lr_ref, li_ref, ldt_ref, fr_ref, fi_ref, br_ref, bi_ref, dwb_ref, dwc_ref, dlbr_ref, dlbi_ref,
             dbr_ref, dbi_ref, dlr_ref, dli_ref, dldt_ref, dcr_ref, dci_ref, dr_s, di_s):
        for k in range(W // LANE):
            for gl in range(NG // (W // LANE)):
                rows, src = slice((8 * k + gl) * GC, (8 * k + gl + 1) * GC), slice(gl * GC, (gl + 1) * GC)
                re, im = slice(gl * NP, (gl + 1) * NP), slice(SW + gl * NP, SW + (gl + 1) * NP)
                dr_s[rows, :] = dwb_ref[k, src, re]
                di_s[rows, :] = dwb_ref[k, src, im]
                dcr_ref[rows, :] = dwc_ref[k, src, re]
                dci_ref[rows, :] = -dwc_ref[k, src, im]
        fr_, fi_ = _per_channel(fr_ref[...]), _per_channel(fi_ref[...])
        br_, bi_, dr, di = br_ref[...], bi_ref[...], dr_s[...], di_s[...]
        dbr_ref[...] = fr_ * dr + fi_ * di
        dbi_ref[...] = fr_ * di - fi_ * dr
        dfr = jnp.sum((dr * br_ + di * bi_).reshape(NG, GC, NP), axis=1)
        dfi = jnp.sum((di * br_ - dr * bi_).reshape(NG, GC, NP), axis=1)
        _, vjp = jax.vjp(_disc, lr_ref[...], li_ref[...], ldt_ref[...])
        dlr_ref[...], dli_ref[...], dldt = vjp((dlbr_ref[...], dlbi_ref[...], dfr, dfi))
        dldt_ref[...] = _transpose_exact(dldt)

    blk = S((NG * GC, NP), f32)
    return pl.pallas_call(
        body, name="ssm_param_bwd", out_shape=[blk, blk, S((NG, NP), f32), S((NG, NP), f32), S((1, NG), f32), blk, blk],
        scratch_shapes=[pltpu.VMEM((NG * GC, NP), f32)] * 2)(
        lam_re, lam_im, log_dt, fr, fi, br, bi, dwb, dwcT, dlbr, dlbi)


def _adam(w, g, m, v):
    m = ADAM_B1 * m + (1.0 - ADAM_B1) * g
    v = ADAM_B2 * v + (1.0 - ADAM_B2) * (g * g)
    m_hat = m / (1.0 - ADAM_B1 ** ADAM_STEP)
    v_hat = v / (1.0 - ADAM_B2 ** ADAM_STEP)
    return -ADAM_LR * (m_hat / (jnp.sqrt(v_hat) + ADAM_EPS) + ADAM_WD * w), m, v


def _sum_in_order(c_ref):
    g = c_ref[0].astype(f32)
    for k in range(1, c_ref.shape[0]):
        g = g + c_ref[k].astype(f32)
    return g


def sum_blocks(contrib, name):
    def body(c_ref, o_ref):
        o_ref[...] = _sum_in_order(c_ref)

    return pl.pallas_call(body, name=name, out_shape=S(contrib.shape[1:], f32))(contrib)


def adam_update(w, m, v, contrib, name, rows_per_block=None, summed_on_0=None, plan=None):
    R, C = w.shape
    n = contrib.shape[0]
    tr = min(rows_per_block or R, R)

    def body(w_ref, m_ref, v_ref, c_ref, *refs):
        g_ref, d_ref, nm_ref, nv_ref = refs[-4:]
        g = _sum_in_order(c_ref)
        if summed_on_0 is not None:
            x, y, c = _coords()
            g = jnp.where(4 * x + 2 * y + c == 0, refs[0][...], g)
        g_ref[...] = g
        d_ref[...], nm_ref[...], nv_ref[...] = _adam(w_ref[...], g, m_ref[...], v_ref[...])

    blk = pl.BlockSpec((tr, C), lambda i: (i, 0))
    extra = [] if summed_on_0 is None else [summed_on_0]
    return _call(
        body, [w, m, v, contrib] + extra, name=name, grid=(R // tr,),
        in_specs=[blk, blk, blk, pl.BlockSpec((n, tr, C), lambda i: (0, i, 0))] + [blk] * len(extra),
        out_specs=[blk] * 4, out_shape=[S((R, C), f32)] * 4, sem=("parallel",), vmem=VMEM_LIMIT, plan=plan)


_ROWVEC = (("b_in", IN_COLS), ("ssm_d", W), ("glu_b", W), ("ln1_g", D), ("ln1_b", D), ("ln2_g", D), ("ln2_b", D))
_HALF = NG * GC // 2
_BC_LANE = {"ssm_b_re": 0, "ssm_b_im": NP, "ssm_c_re": 0, "ssm_c_im": NP}
_PACK = {}
_r = 0
for _n, _k in _ROWVEC:
    _PACK[_n] = _r
    _r += _k // LANE
for _n, _rows in (("ssm_lambda", NG), ("scalars", 8), ("ssm_b", _HALF), ("ssm_c", _HALF), ("conv_w", 16)):
    _PACK[_n] = _r
    _r += _rows
for _n in _BC_LANE:
    _PACK[_n] = _PACK[_n[:5]]
PACK_ROWS = _r
assert PACK_ROWS % 8 == 0
_SMALL = ("b_in", "ssm_lambda_re", "ssm_lambda_im", "ssm_log_dt", "ssm_b_re", "ssm_b_im", "ssm_c_re", "ssm_c_im",
          "ssm_d", "glu_b", "ln1_g", "ln1_b", "ln2_g", "ln2_b")


def pack_grads(su, shcb, sga, sgb, dd, dglu_b, dln1_g, dln1_b, dln2_g, dln2_b, dlam_re, dlam_im, dldt, sqerr, dbr, dbi,
               dc_re, dc_im, dconv):
    nI = sga.shape[0]

    def body(su_ref, sh_ref, sga_ref, sgb_ref, dd_ref, gb_ref, l1g_ref, l1b_ref, l2g_ref, l2b_ref, lr_ref, li_ref, dt_ref,
             sq_ref, br_ref, bi_ref, cr_ref, ci_ref, cw_ref, o_ref):
        o_ref[...] = jnp.zeros_like(o_ref)

        def put_row(name, v):
            r0 = _PACK[name]
            for i in range(v.shape[1] // LANE):
                o_ref[r0 + i:r0 + i + 1, :] = v[:, i * LANE:(i + 1) * LANE]

        ga, gb = sga_ref[0], sgb_ref[0]
        for i in range(1, nI):
            ga, gb = ga + sga_ref[i], gb + sgb_ref[i]
        put_row("b_in", jnp.concatenate([su_ref[k] for k in range(W // LANE)]
                                        + [sh_ref[0:1, :], sh_ref[1:2, :], sh_ref[2:3, :], ga, gb], axis=1))
        put_row("ssm_d", jnp.concatenate([dd_ref[k] for k in range(W // LANE)], axis=1))
        put_row("glu_b", gb_ref[...])
        put_row("ln1_g", l1g_ref[...])
        put_row("ln1_b", l1b_ref[...])
        put_row("ln2_g", l2g_ref[...])
        put_row("ln2_b", l2b_ref[...])
        r0 = _PACK["ssm_lambda"]
        o_ref[r0:r0 + NG, 0:NP] = lr_ref[...]
        o_ref[r0:r0 + NG, NP:2 * NP] = li_ref[...]
        r0 = _PACK["scalars"]
        o_ref[r0:r0 + 1, 0:NG] = dt_ref[...]
        o_ref[r0 + 1:r0 + 2, 0:1] = sq_ref[...]
        for name, ref in (("ssm_b_re", br_ref), ("ssm_b_im", bi_ref), ("ssm_c_re", cr_ref), ("ssm_c_im", ci_ref)):
            r0, l0 = _PACK[name], _BC_LANE[name]
            o_ref[r0:r0 + _HALF, l0:l0 + NP] = pltpu.bitcast(ref[...].astype(bf16), f32)
        for cb in range(W // LANE):
            o_ref[_PACK["conv_w"] + 3 * cb:_PACK["conv_w"] + 3 * cb + 3, :] = cw_ref[:, cb * LANE:(cb + 1) * LANE]

    return pl.pallas_call(body, name="pack_grads", out_shape=S((PACK_ROWS, LANE), f32))(
        su, shcb, sga, sgb, dd, dglu_b, dln1_g, dln1_b, dln2_g, dln2_b, dlam_re, dlam_im, dldt, sqerr, dbr, dbi, dc_re, dc_im,
        dconv)


def adam_small(packed_all, params):
    names = list(_SMALL) + ["conv_w"]
    flat = [a for n in names for a in params[n]]

    def body(*refs):
        p_ref = refs[0]
        ins = refs[1:1 + 3 * len(names)]
        outs = refs[1 + 3 * len(names):-2]
        loss_ref, g_ref = refs[-2], refs[-1]

        def part(k, rs=slice(None), ls=slice(None)):
            return p_ref[k, rs, ls]

        g_all = part(0)
        for k in range(1, NDEV):
            g_all = g_all + part(k)
        g_ref[...] = g_all

        def rows(name, r0, n, l0=0, lanes=LANE):
            return g_ref[_PACK[name] + r0:_PACK[name] + r0 + n, l0:l0 + lanes]

        def grad_of(name):
            if name in dict(_ROWVEC):
                return jnp.concatenate([rows(name, i, 1) for i in range(dict(_ROWVEC)[name] // LANE)], axis=1)
            if name in ("ssm_lambda_re", "ssm_lambda_im"):
                return rows("ssm_lambda", 0, NG, NP * (name == "ssm_lambda_im"), NP)[None]
            if name == "ssm_log_dt":
                return rows("scalars", 0, 1, 0, NG)
            if name in _BC_LANE:
                rs, ls = slice(_PACK[name], _PACK[name] + _HALF), slice(_BC_LANE[name], _BC_LANE[name] + NP)
                g = pltpu.bitcast(part(0, rs, ls), bf16).astype(f32)
                for k in range(1, NDEV):
                    g = g + pltpu.bitcast(part(k, rs, ls), bf16).astype(f32)
                return g.reshape(1, NG, GC, NP)
            full = jnp.concatenate([rows("conv_w", 3 * cb, 3) for cb in range(W // LANE)], axis=1)
            x, y, c = _coords()
            col0 = (4 * x + 2 * y + c) * (W // NDEV)
            sel = (lax.broadcasted_iota(jnp.int32, (W, W // NDEV), 0)
                   == lax.broadcasted_iota(jnp.int32, (W, W // NDEV), 1) + col0).astype(f32)
            return jnp.dot(full, sel, precision=HIGHEST, preferred_element_type=f32)[None]

        loss_ref[...] = 0.5 * rows("scalars", 1, 1, 0, 1)
        for i, name in enumerate(names):
            w_ref, m_ref, v_ref = ins[3 * i:3 * i + 3]
            g = grad_of(name)
            d, m, v = _adam(w_ref[...], g, m_ref[...], v_ref[...])
            outs[4 * i][...] = g
            outs[4 * i + 1][...] = d
            outs[4 * i + 2][...] = m
            outs[4 * i + 3][...] = v

    out_shape = [S(params[n][0].shape, f32) for n in names for _ in range(4)] + [S((1, 1), f32)]
    res = pl.pallas_call(body, name="adam_small", out_shape=out_shape, scratch_shapes=[pltpu.VMEM((PACK_ROWS, LANE), f32)],
                         compiler_params=_cp(None, VMEM_LIMIT))(packed_all, *flat)
    return {n: res[4 * i:4 * i + 4] for i, n in enumerate(names)}, res[-1]


def _block_diag(wgt):
    eye = jnp.eye(8, dtype=wgt.dtype)
    out = wgt[:, :, :, None, :] * eye[None, :, None, :, None]
    return out.reshape(4, 8 * wgt.shape[2], 8 * wgt.shape[3])


def kernel(x, w_in, b_in, ssm_lambda_re, ssm_lambda_im, ssm_log_dt, ssm_b_re, ssm_b_im, ssm_c_re, ssm_c_im, ssm_d, glu_w, glu_b, w_ssm_out, conv_w, w_conv_out, w_o, ln1_g, ln1_b, w_gate, w_up, w_down, ln2_g, ln2_b, loss_target, m_w_in, m_b_in, m_ssm_lambda_re, m_ssm_lambda_im, m_ssm_log_dt, m_ssm_b_re, m_ssm_b_im, m_ssm_c_re, m_ssm_c_im, m_ssm_d, m_glu_w, m_glu_b, m_w_ssm_out, m_conv_w, m_w_conv_out, m_w_o, m_ln1_g, m_ln1_b, m_w_gate, m_w_up, m_w_down, m_ln2_g, m_ln2_b, v_w_in, v_b_in, v_ssm_lambda_re, v_ssm_lambda_im, v_ssm_log_dt, v_ssm_b_re, v_ssm_b_im, v_ssm_c_re, v_ssm_c_im, v_ssm_d, v_glu_w, v_glu_b, v_w_ssm_out, v_conv_w, v_w_conv_out, v_w_o, v_ln1_g, v_ln1_b, v_w_gate, v_w_up, v_w_down, v_ln2_g, v_ln2_b):
    given = dict(locals())
    xs = x[0]
    target = loss_target[0]

    tr = lambda a: jnp.swapaxes(a[0], 0, 1)
    win_s, glu_s, wso_s, wco_s, wo_s, wgT_s, wuT_s, wd_s = prep_weights(
        [w_in[0], glu_w[0], w_ssm_out[0], w_conv_out[0], w_o[0], tr(w_gate), tr(w_up), w_down[0]])
    win_g = broadcast_from_0(win_s, "gather_w_in_u")

    lam_re, lam_im = ssm_lambda_re[0], ssm_lambda_im[0]
    ldt = ssm_log_dt[0].reshape(NG, 1)
    br2 = jnp.swapaxes(ssm_b_re[0], 1, 2).reshape(NG * GC, NP)
    bi2 = jnp.swapaxes(ssm_b_im[0], 1, 2).reshape(NG * GC, NP)
    lbr, lbi, fr, fi, bbr, bbi = ssm_params(lam_re, lam_im, ldt, br2, bi2)
    bb_t = lambda b: b.reshape(4, 8, GC, NP)
    wb = jnp.concatenate([_block_diag(bb_t(bbr)), _block_diag(bb_t(bbi))], axis=2)
    c_t = lambda c: c.reshape(4, 8, GC, NP).transpose(0, 1, 3, 2)
    wc = jnp.concatenate([_block_diag(c_t(ssm_c_re[0])), -_block_diag(c_t(ssm_c_im[0]))], axis=1)
    wbT, wcT = wb.transpose(0, 2, 1), wc.transpose(0, 2, 1)
    wb, wc, wbT, wcT = wb.astype(bf16), wc.astype(bf16), wbT.astype(bf16), wcT.astype(bf16)
    lbr_s, lbi_s = lbr.reshape(4, 1, SW), lbi.reshape(4, 1, SW)
    dsk = ssm_d[0].reshape(4, 1, LANE)

    u_nat, xb = in_proj_u(xs, win_g, b_in)
    u_p = to_perm(u_nat, 0, "perm_u")
    half_a, half_b = (0, 3, 5, 6), (1, 2, 4, 7)
    (y_p, xr_p, xi_p), (win_g, conv_g, glu_g, wso_g, wuT_g) = ssm_fwd(
        u_p, wb, wc, lbr_s, lbi_s, dsk,
        Plans([GatherPlan([win_s], srcs=tuple(range(1, NDEV)), into=[win_g]), GatherPlan([conv_w[0], glu_s, wso_s]),
               GatherPlan([wuT_s], srcs=half_a)]))
    conv_f = conv_g.transpose(1, 0, 2).reshape(3, W)
    (proj,), (wco_g, wo_g, wgT_g) = in_proj_rest(
        xb, win_g, b_in, Plans([GatherPlan([wco_s, wo_s]), GatherPlan([wgT_s], srcs=half_a)]))
    glu_f, wo_f = glu_g.reshape(W, W), wo_g.reshape(D, D)
    (yn,), _ = from_perm(y_p, "unperm_y")
    ya = glu_fwd(yn, glu_f, glu_b)
    yb = conv_fwd(proj, conv_f)
    (merged,), (wgT_g,) = merge_fwd(ya, yb, wso_g, wco_g, proj, GatherPlan([wgT_s], srcs=half_b, into=[wgT_g]))
    (r1, x1b), (wuT_g,) = mix_ln1(merged, wo_f, xs, ln1_g, ln1_b, GatherPlan([wuT_s], srcs=half_b, into=[wuT_g]))
    wgT, wuT = wgT_g.reshape(F, D), wuT_g.reshape(F, D)
    (gate, up, hid), (wd_g,) = gate_up(x1b, wgT, wuT, GatherPlan([wd_s]))
    wd_f = wd_g.reshape(F, D)
    dr2, dffn, sqerr, dln2_g, dln2_b = down_loss(hid, wd_f, r1, ln1_g, ln1_b, ln2_g, ln2_b, target)

    dwd, _ = mm_tn_rows(hid, dffn, "grad_w_down")
    dwd = dwd.reshape(NDEV, FS, D)
    (dgate, dup), (r_wd,) = ffn_bwd_act(dffn, wd_f, gate, up, ScatterPlan([dwd], only=half_a))
    dwgT, (r_wd,) = mm_tn_rows(dgate, x1b, "grad_w_gate", plan=ScatterPlan([dwd], only=half_b, into=[r_wd]))
    dwgT = dwgT.reshape(NDEV, FS, D)
    dwuT, (r_wgT,) = mm_tn_rows(dup, x1b, "grad_w_up", plan=ScatterPlan([dwgT], only=half_a))
    dwuT = dwuT.reshape(NDEV, FS, D)
    (dr1, dmix, dln1_g, dln1_b), (r_wgT, r_wuT) = ffn_bwd_x(
        dgate, dup, wgT, wuT, dr2, r1, ln1_g,
        Plans([ScatterPlan([dwgT], only=half_b, into=[r_wgT]), ScatterPlan([dwuT], only=half_a)]))
    (dYA, dYB, dga, dgb, sga, sgb), (r_wuT,) = merge_bwd(dmix, wo_f, ya, yb, wso_g, wco_g, proj,
                                                         ScatterPlan([dwuT], only=half_b, into=[r_wuT]))
    dwo, _ = mm_tn_rows(merged, dmix, "grad_w_o")
    dwo = dwo.reshape(NDEV, D // NDEV, D)
    (dya, dyb), _ = branches_bwd_x(dYA, dYB, wso_g, wco_g, None)
    dwso = branch_bwd_w(ya, dYA, "grad_w_ssm_out")
    dwco = branch_bwd_w(yb, dYB, "grad_w_conv_out")
    (dyn, dsp, gb, dglu_b), (r_wso,) = glu_bwd(yn, dya, glu_f, glu_b, ScatterPlan([dwso]))
    dglu = mm_tn_rows(gb, dsp, "grad_glu_w")[0].reshape(NDEV, W // NDEV, W)
    (dh, dcg, dbg, dconv, shcb), (r_wco,) = conv_bwd(proj, dyb, conv_f, ScatterPlan([dwco]))
    dwin, (r_wo, r_glu) = grad_w_in_rest(xb, dh, dcg, dbg, dga, dgb, ScatterPlan([dwo, dglu]))
    dy_p = to_perm(dyn, 0, "perm_dy")
    (du_p, dwb, dwcT, dlbr_s, dlbi_s, dd, su), (r_win,) = ssm_bwd(
        u_p, dy_p, xr_p, xi_p, wbT, wcT, lbr_s, lbi_s, dsk, ScatterPlan([dwin], only=tuple(range(1, NDEV))))

    dbr2, dbi2, dlam_re, dlam_im, dldt, dc_re, dc_im = ssm_param_bwd(
        lam_re, lam_im, ldt, fr, fi, br2, bi2, dwb, dwcT, dlbr_s.reshape(NG, NP), dlbi_s.reshape(NG, NP))
    packed = pack_grads(su, shcb, sga, sgb, dd, dglu_b, dln1_g, dln1_b, dln2_g, dln2_b, dlam_re, dlam_im, dldt, sqerr,
                        dbr2, dbi2, dc_re, dc_im, dconv)
    (du,), _ = from_perm(du_p, "unperm_du", bf16)
    dwin_u = mm_tn(xb, du, "grad_w_in_u").reshape(NDEV, D // NDEV, W)

    rest = [(dh, 0, 1), (dcg, 0, 2), (dbg, 0, 3), (dga, 0, 4), (dga, 1, 5), (dgb, 0, 6), (dgb, 1, 7)]
    (gx_rest,), (r_win_u, small_all) = in_proj_bwd_x(
        rest, win_g, dr1, ALPHA, "in_proj_bwd_x_rest", Plans([ScatterPlan([dwin_u]), GatherPlan([packed])]))
    my_rows = sum_blocks(r_win_u, "sum_w_in_u")

    out = {}

    def put(name, res, back=lambda a: a[None]):
        out["grad_" + name], out["delta_" + name], out["new_m_" + name], out["new_v_" + name] = [back(r) for r in res]

    res_wd, (win_u_sum,) = adam_update(w_down[0], m_w_down[0], v_w_down[0], r_wd, "adam_w_down", 176,
                                       plan=ScatterPlan([my_rows], only=(0,), whole=True))
    put("w_down", res_wd)
    (grad_x,), _ = in_proj_bwd_x([(du, 0, 0)], win_g, gx_rest, 1.0, "in_proj_bwd_x_u")
    put("w_in", adam_update(w_in[0], m_w_in[0], v_w_in[0], r_win, "adam_w_in", 256,
                            summed_on_0=win_u_sum.reshape(D, W))[0])
    put("glu_w", adam_update(glu_w[0], m_glu_w[0], v_glu_w[0], r_glu, "adam_glu_w")[0])
    put("w_ssm_out", adam_update(w_ssm_out[0], m_w_ssm_out[0], v_w_ssm_out[0], r_wso, "adam_w_ssm_out")[0])
    put("w_conv_out", adam_update(w_conv_out[0], m_w_conv_out[0], v_w_conv_out[0], r_wco, "adam_w_conv_out")[0])
    put("w_o", adam_update(w_o[0], m_w_o[0], v_w_o[0], r_wo, "adam_w_o")[0])
    untr = lambda a: jnp.swapaxes(a, 0, 1)[None]
    put("w_gate", adam_update(tr(w_gate), tr(m_w_gate), tr(v_w_gate), r_wgT, "adam_w_gate", 176)[0], untr)
    put("w_up", adam_update(tr(w_up), tr(m_w_up), tr(v_w_up), r_wuT, "adam_w_up", 176)[0], untr)
    as_c = lambda a: jnp.swapaxes(a, 2, 3)
    params = {n: (given[n], given["m_" + n], given["v_" + n]) for n in list(_SMALL) + ["conv_w"]}
    for n in ("ssm_b_re", "ssm_b_im"):
        params[n] = tuple(as_c(a) for a in params[n])
    small, loss = adam_small(small_all, params)
    for n, res in small.items():
        put(n, res, as_c if n in ("ssm_b_re", "ssm_b_im") else (lambda a: a))

    names = ["w_in", "b_in", "ssm_lambda_re", "ssm_lambda_im", "ssm_log_dt", "ssm_b_re", "ssm_b_im", "ssm_c_re", "ssm_c_im",
             "ssm_d", "glu_w", "glu_b", "w_ssm_out", "conv_w", "w_conv_out", "w_o", "ln1_g", "ln1_b", "w_gate", "w_up",
             "w_down", "ln2_g", "ln2_b"]
    return (loss.reshape(()), grad_x[None], *[out[p + n] for p in ("grad_", "delta_", "new_m_", "new_v_") for n in names])
```

```python
import functools
import math

import jax
import jax.numpy as jnp
from jax import lax
from jax.experimental import pallas as pl
from jax.experimental.pallas import tpu as pltpu

f32, bf16 = jnp.float32, jnp.bfloat16
S = jax.ShapeDtypeStruct
MESH = pl.DeviceIdType.MESH
HIGHEST = lax.Precision.HIGHEST

D = 1024
W = 512
NG, NP, GC = 32, 64, 16
F = 2816
NDEV = 8
FS = F // NDEV
IN_COLS = 8 * W
ALPHA = 2.0 ** 0.25
LN_EPS = 1e-5
ADAM_LR, ADAM_B1, ADAM_B2, ADAM_EPS, ADAM_WD, ADAM_STEP = 0.001, 0.9, 0.999, 1e-08, 0.01, 10
NC = 32
LANE = 128
SW = 4 * LANE
VMEM_LIMIT = 56 * 1024 * 1024
GRAD_DT = bf16
ANY = pl.BlockSpec(memory_space=pl.ANY)


def _cp(sem=None, vmem=None):
    return pltpu.CompilerParams(dimension_semantics=sem, vmem_limit_bytes=vmem)


def _resident(shape):
    return pl.BlockSpec(shape, lambda i: (0,) * len(shape), pipeline_mode=pl.Buffered(1))


def _dot(a, b):
    return jnp.dot(a, b, preferred_element_type=f32)


def _dot_nt(a, b):
    return lax.dot_general(a, b, (((1,), (1,)), ((), ())), preferred_element_type=f32)


def _dot_tn(a, b):
    return lax.dot_general(a, b, (((0,), (0,)), ((), ())), preferred_element_type=f32)


def _eye(n):
    return (lax.broadcasted_iota(jnp.int32, (n, n), 0) == lax.broadcasted_iota(jnp.int32, (n, n), 1)).astype(f32)


def _transpose_exact(a):
    return lax.dot_general(a, _eye(a.shape[0]), (((0,), (0,)), ((), ())), precision=HIGHEST, preferred_element_type=f32)


def _sigmoid(x):
    return 1.0 / (1.0 + jnp.exp(-x))


_GK = math.sqrt(2.0 / math.pi)


def _gelu(x):
    return 0.5 * x * (1.0 + jnp.tanh(_GK * (x + 0.044715 * x * x * x)))


def _gelu_grad(x):
    th = jnp.tanh(_GK * (x + 0.044715 * x * x * x))
    return 0.5 * (1.0 + th) + 0.5 * x * (1.0 - th * th) * _GK * (1.0 + 3.0 * 0.044715 * x * x)


ROW_PART = 256


def _row_parts(tm):
    return [slice(r, r + min(ROW_PART, tm)) for r in range(0, tm, min(ROW_PART, tm))]


def _ln_stats(r):
    mu = jnp.mean(r, axis=-1, keepdims=True)
    xc = r - mu
    var = jnp.mean(xc * xc, axis=-1, keepdims=True)
    rstd = lax.rsqrt(var + LN_EPS)
    return xc * rstd, rstd


def _ln_bwd(dy, xhat, rstd, g):
    dxh = dy * g
    m1 = jnp.mean(dxh, axis=-1, keepdims=True)
    m2 = jnp.mean(dxh * xhat, axis=-1, keepdims=True)
    return rstd * (dxh - m1 - xhat * m2)


def _coords():
    return lax.axis_index("x"), lax.axis_index("y"), lax.axis_index("c")


def _when(cond, fn):
    if cond is True:
        fn()
    else:
        pl.when(cond)(fn)


class GatherPlan:
    aliases = ()

    def __init__(self, arrs, srcs=None, into=None):
        n = self.n = len(arrs)
        self.srcs = srcs
        self.inputs = list(arrs) + list(into or [])
        if into:
            self.aliases = tuple((n + a, a) for a in range(n))
        self.out_shape = [S((NDEV,) + a.shape, a.dtype) for a in arrs]
        self.sems = [pltpu.SemaphoreType.DMA((n, 7)), pltpu.SemaphoreType.DMA((n, 7)), pltpu.SemaphoreType.DMA((n,))]

    def _has(self, dev):
        if self.srcs is None:
            return True
        idx = 4 * dev[0] + 2 * dev[1] + dev[2]
        return functools.reduce(jnp.logical_or, [idx == s for s in self.srcs])

    def _parts(self, ins, outs, sems):
        n = self.n
        send_sems, recv_sems, loc_sems = sems
        x, y, c = _coords()
        me, sib = (x, y, c), (x, y, 1 - c)
        chips = [(1 - x, y), (x, 1 - y), (1 - x, 1 - y)]

        def slot(a, dev):
            return outs[a].at[4 * dev[0] + 2 * dev[1] + dev[2]]

        def copy(a, k, block, to, src=None):
            return pltpu.make_async_remote_copy(
                src_ref=slot(a, block) if src is None else src, dst_ref=slot(a, block),
                send_sem=send_sems.at[a, k], recv_sem=recv_sems.at[a, k], device_id=to, device_id_type=MESH)

        each = [(j, chip, a) for j, chip in enumerate(chips) for a in range(n)]
        own = self._has(me)
        return dict(
            mine=lambda: [(pltpu.make_async_copy(ins[a], slot(a, me), loc_sems.at[a]), own) for a in range(n)],
            first=lambda: ([(copy(a, 0, me, sib, src=ins[a]), own) for a in range(n)]
                           + [(copy(a, 1 + j, me, (*chip, c), src=ins[a]), own) for j, chip, a in each]),
            landed=lambda: [(copy(a, 1 + j, (*chip, c), me), self._has((*chip, c))) for j, chip, a in each],
            passed=lambda: [(copy(a, 4 + j, (*chip, c), sib), self._has((*chip, c))) for j, chip, a in each],
            from_sib=lambda: ([(copy(a, 0, sib, me), self._has(sib)) for a in range(n)]
                              + [(copy(a, 4 + j, (*chip, 1 - c), me), self._has((*chip, 1 - c))) for j, chip, a in each]))

    def start(self, ins, outs, sems):
        p = self._parts(ins, outs, sems)
        for cp, cond in p["mine"]() + p["first"]():
            _when(cond, cp.start)

    def forward(self, ins, outs, sems):
        p = self._parts(ins, outs, sems)
        for (got, cond), (fwd, _) in zip(p["landed"](), p["passed"]()):
            def relay(got=got, fwd=fwd):
                got.wait_recv()
                fwd.start()

            _when(cond, relay)

    def finish(self, ins, outs, sems):
        p = self._parts(ins, outs, sems)
        for cp, cond in p["from_sib"]():
            _when(cond, cp.wait_recv)
        for cp, cond in p["first"]() + p["passed"]():
            _when(cond, cp.wait_send)
        for cp, cond in p["mine"]():
            _when(cond, cp.wait)


class ScatterPlan:
    aliases = ()

    def __init__(self, gs, only=None, into=None, whole=False):
        n = self.n = len(gs)
        self.only = only
        self.whole = whole
        self.inputs = list(gs) + list(into or [])
        if into:
            self.aliases = tuple((n + a, a) for a in range(n))
        self.out_shape = [S((NDEV,) + g.shape if whole else g.shape, g.dtype) for g in gs]
        self.sems = [pltpu.SemaphoreType.DMA((n, 7)), pltpu.SemaphoreType.DMA((n, 7)), pltpu.SemaphoreType.DMA((n,))]

    def _owner(self, idx):
        if self.only is None:
            return True
        return functools.reduce(jnp.logical_or, [idx == b for b in self.only])

    def _copies(self, ins, outs, sems):
        n = self.n
        send_sems, recv_sems, loc_sems = sems
        x, y, c = _coords()
        me = 4 * x + 2 * y + c
        mine = self._owner(me)
        block = (lambda a, k: ins[a]) if self.whole else (lambda a, k: ins[a].at[k])
        copies = [(pltpu.make_async_copy(block(a, me), outs[a].at[me], loc_sems.at[a]), mine, None) for a in range(n)]
        for m in range(1, NDEV):
            px = 1 - x if m & 4 else x
            py = 1 - y if m & 2 else y
            pc = 1 - c if m & 1 else c
            peer = 4 * px + 2 * py + pc
            for a in range(n):
                copies.append((pltpu.make_async_remote_copy(
                    src_ref=block(a, peer), dst_ref=outs[a].at[me],
                    send_sem=send_sems.at[a, m - 1], recv_sem=recv_sems.at[a, m - 1],
                    device_id=(px, py, pc), device_id_type=MESH), self._owner(peer), mine))
        return copies

    def start(self, ins, outs, sems):
        for cp, sends, _ in self._copies(ins, outs, sems):
            _when(sends, cp.start)

    def forward(self, ins, outs, sems):
        pass

    def finish(self, ins, outs, sems):
        for cp, sends, receives in self._copies(ins, outs, sems):
            if receives is None:
                _when(sends, cp.wait)
            else:
                _when(sends, cp.wait_send)
                _when(receives, cp.wait_recv)


class Plans:
    def __init__(self, plans):
        self.plans = plans
        self.inputs = [a for p in plans for a in p.inputs]
        self.out_shape = [s for p in plans for s in p.out_shape]
        self.sems = [s for p in plans for s in p.sems]
        self.aliases, i, o = [], 0, 0
        for p in plans:
            self.aliases += [(i + a, o + b) for a, b in p.aliases]
            i, o = i + len(p.inputs), o + len(p.out_shape)

    def _each(self, what, ins, outs, sems):
        i = o = s = 0
        for p in self.plans:
            ni, no, ns = len(p.inputs), len(p.out_shape), len(p.sems)
            getattr(p, what)(ins[i:i + ni], outs[o:o + no], sems[s:s + ns])
            i, o, s = i + ni, o + no, s + ns

    def start(self, ins, outs, sems):
        self._each("start", ins, outs, sems)

    def forward(self, ins, outs, sems):
        self._each("forward", ins, outs, sems)

    def finish(self, ins, outs, sems):
        self._each("finish", ins, outs, sems)


def _call(body, args, *, name, grid, in_specs, out_specs, out_shape, scratch=(), sem=None, vmem=None, plan=None,
          aliases=None, relay_step=None):
    aliases = aliases or {}
    if plan is None:
        outs = pl.pallas_call(body, name=name, grid=grid, in_specs=list(in_specs), out_specs=list(out_specs),
                              out_shape=list(out_shape), scratch_shapes=list(scratch), input_output_aliases=aliases,
                              compiler_params=_cp(sem, vmem))(*args)
        return list(outs), []
    ni, no, ns = len(in_specs), len(out_specs), len(scratch)
    pi, po = len(plan.inputs), len(plan.out_shape)
    aliases = {**aliases, **{ni + a: no + b for a, b in plan.aliases}}

    def wrapped(*refs):
        main_in, p_in = refs[:ni], refs[ni:ni + pi]
        main_out, p_out = refs[ni + pi:ni + pi + no], refs[ni + pi + no:ni + pi + no + po]
        main_scr, p_sems = refs[ni + pi + no + po:ni + pi + no + po + ns], refs[ni + pi + no + po + ns:]
        ids = [pl.program_id(d) for d in range(len(grid))]
        first = functools.reduce(jnp.logical_and, [i == 0 for i in ids])
        last = functools.reduce(jnp.logical_and, [i == g - 1 for i, g in zip(ids, grid)])

        @pl.when(first)
        def _():
            plan.start(p_in, p_out, p_sems)

        @pl.when(last if relay_step is None else ids[0] == max(relay_step, 0))
        def _():
            plan.forward(p_in, p_out, p_sems)

        body(*main_in, *main_out, *main_scr)

        @pl.when(last)
        def _():
            plan.finish(p_in, p_out, p_sems)

    outs = pl.pallas_call(
        wrapped, name=name, grid=grid, in_specs=list(in_specs) + [ANY] * pi, out_specs=list(out_specs) + [ANY] * po,
        out_shape=list(out_shape) + list(plan.out_shape), scratch_shapes=list(scratch) + list(plan.sems),
        input_output_aliases=aliases, compiler_params=_cp(("arbitrary",) * len(grid), vmem),
    )(*args, *plan.inputs)
    return list(outs[:no]), list(outs[no:])


def run_plan(plan, name):
    def body(*refs):
        ins, outs, sems = refs[:len(plan.inputs)], refs[len(plan.inputs):len(plan.inputs) + len(plan.out_shape)], \
            refs[len(plan.inputs) + len(plan.out_shape):]
        plan.start(ins, outs, sems)
        plan.forward(ins, outs, sems)
        plan.finish(ins, outs, sems)

    return pl.pallas_call(body, name=name, in_specs=[ANY] * len(plan.inputs), out_specs=[ANY] * len(plan.out_shape),
                          out_shape=list(plan.out_shape), scratch_shapes=list(plan.sems))(*plan.inputs)


def mm_tn(a, b, name, tn=512):
    T, K = a.shape
    N = b.shape[1]
    tn = min(tn, N)

    def body(a_ref, b_ref, o_ref):
        o_ref[...] = _dot_tn(a_ref[...], b_ref[...]).astype(GRAD_DT)

    (out,), _ = _call(body, [a, b], name=name, grid=(N // tn,),
                      in_specs=[_resident((T, K)), pl.BlockSpec((T, tn), lambda j: (0, j))],
                      out_specs=[pl.BlockSpec((None, K, tn), lambda j: (j, 0, 0))],
                      out_shape=[S((N // tn, K, tn), GRAD_DT)], sem=("parallel",), vmem=VMEM_LIMIT)
    return out


def grad_w_in_rest(xb, dh, dcg, dbg, dga, dgb, plan):
    T = xb.shape[0]
    order = ((0, 0), (1, 1), (2, 2), (3, 3), (4, 3), (5, 4), (6, 4))

    def body(x_ref, *refs):
        o_ref = refs[-1]
        j = pl.program_id(0)
        for step, opnd in order:
            @pl.when(j == step)
            def _(opnd=opnd):
                o_ref[...] = _dot_tn(x_ref[...], refs[opnd][...]).astype(GRAD_DT)

    once = lambda: pl.BlockSpec((T, W), lambda j: (0, 0), pipeline_mode=pl.Buffered(1))
    (out,), sent = _call(
        body, [xb, dh, dcg, dbg, dga, dgb], name="grad_w_in_rest", grid=(len(order),),
        in_specs=[_resident((T, D)), once(), once(), once(),
                  pl.BlockSpec((T, W), lambda j: (0, jnp.clip(j - 3, 0, 1))),
                  pl.BlockSpec((T, W), lambda j: (0, jnp.clip(j - 5, 0, 1)))],
        out_specs=[pl.BlockSpec((None, D, W), lambda j: (1 + j, 0, 0))],
        out_shape=[S((NDEV, D, W), GRAD_DT)], sem=("arbitrary",), vmem=VMEM_LIMIT, plan=plan)
    return out, sent


def mm_tn_rows(a, b, name, tk=256, plan=None):
    T, K = a.shape
    N = b.shape[1]
    tk = min(tk, K)

    def body(a_ref, b_ref, o_ref):
        o_ref[...] = _dot_tn(a_ref[...], b_ref[...]).astype(GRAD_DT)

    (out,), sent = _call(body, [a, b], name=name, grid=(K // tk,),
                         in_specs=[pl.BlockSpec((T, tk), lambda i: (0, i)), _resident((T, N))],
                         out_specs=[pl.BlockSpec((tk, N), lambda i: (i, 0))], out_shape=[S((K, N), GRAD_DT)],
                         sem=("parallel",), vmem=VMEM_LIMIT, plan=plan)
    return out, sent


def prep_weights(ws):
    def body(*refs):
        for i in range(len(ws)):
            refs[len(ws) + i][...] = refs[i][...].astype(bf16)

    return pl.pallas_call(body, name="prep_weights", out_shape=[S(w.shape, bf16) for w in ws],
                          compiler_params=_cp(None, VMEM_LIMIT))(*ws)


REST_BLOCKS = (4, 5, 6, 7, 1, 2, 3)
REST_COLS = len(REST_BLOCKS) * W


def in_proj_u(x, win_g, b_in):
    T = x.shape[0]
    tm = min(1024, T)

    def body(x_ref, w_ref, b_ref, u_ref, xb_ref):
        xb = x_ref[...].astype(bf16)
        xb_ref[...] = xb
        u_ref[...] = _dot(xb, w_ref[...]) + b_ref[...]

    row = pl.BlockSpec((tm, D), lambda i: (i, 0))
    return pl.pallas_call(
        body, name="in_proj_u", grid=(T // tm,),
        in_specs=[row, pl.BlockSpec((None, D, W), lambda i: (0, 0, 0)), pl.BlockSpec((1, W), lambda i: (0, 0))],
        out_specs=[pl.BlockSpec((tm, W), lambda i: (i, 0)), row],
        out_shape=[S((T, W), f32), S((T, D), bf16)], compiler_params=_cp(("parallel",), VMEM_LIMIT),
    )(x, win_g, b_in)


def in_proj_rest(xb, win_g, b_in, plan):
    T = xb.shape[0]
    tm = min(512, T)

    def body(x_ref, w_ref, b_ref, o_ref):
        xb_ = x_ref[...]
        for i, k in enumerate(REST_BLOCKS):
            o_ref[:, i * W:(i + 1) * W] = _dot(xb_, w_ref[k]) + b_ref[:, k * W:(k + 1) * W]

    return _call(
        body, [xb, win_g, b_in], name="in_proj_rest", grid=(T // tm,),
        in_specs=[pl.BlockSpec((tm, D), lambda i: (i, 0)), _resident((NDEV, D, W)), _resident((1, IN_COLS))],
        out_specs=[pl.BlockSpec((tm, REST_COLS), lambda i: (i, 0))],
        out_shape=[S((T, REST_COLS), f32)], vmem=VMEM_LIMIT, plan=plan, relay_step=T // tm - 2)


def _to_scan_order(a_ref, o_ref):
    L = a_ref.shape[0] // NC

    def step(jb, carry):
        j0 = pl.multiple_of(jb * 8, 8)
        for q in range(NC // 8):
            x = jnp.stack([a_ref[pl.ds((8 * q + c) * L + j0, 8), :] for c in range(8)], axis=0)
            y = jnp.swapaxes(x, 0, 1)
            for j in range(8):
                o_ref[pl.ds((j0 + j) * NC + 8 * q, 8), :] = y[j]
        return carry

    lax.fori_loop(0, L // 8, step, 0)


def _to_time_order(a_ref, o_ref):
    L = a_ref.shape[0] // NC

    def step(jb, carry):
        j0 = pl.multiple_of(jb * 16, 16)
        for q in range(NC // 8):
            halves = []
            for h in range(2):
                x = jnp.stack([a_ref[pl.ds((j0 + 8 * h + j) * NC + 8 * q, 8), :] for j in range(8)], axis=0)
                halves.append(jnp.swapaxes(x, 0, 1))
            for c in range(8):
                o_ref[pl.ds((8 * q + c) * L + j0, 16), :] = jnp.concatenate(
                    [halves[0][c], halves[1][c]], axis=0).astype(o_ref.dtype)
        return carry

    lax.fori_loop(0, L // 16, step, 0)


def _disc(lr, li, ldt):
    dt = jnp.exp(ldt)
    mag = jnp.exp(lr * dt)
    lbr = mag * jnp.cos(li * dt)
    lbi = mag * jnp.sin(li * dt)
    den = lr * lr + li * li
    nr = lbr - 1.0
    return lbr, lbi, (nr * lr + lbi * li) / den, (lbi * lr - nr * li) / den


def _per_channel(f):
    return jnp.broadcast_to(f[:, None, :], (NG, GC, NP)).reshape(NG * GC, NP)


def ssm_params(lam_re, lam_im, log_dt, br, bi):
    def body(lr_ref, li_ref, ldt_ref, br_ref, bi_ref, lbr_ref, lbi_ref, fr_ref, fi_ref, bbr_ref, bbi_ref):
        lbr, lbi, fr, fi = _disc(lr_ref[...], li_ref[...], ldt_ref[...])
        lbr_ref[...], lbi_ref[...], fr_ref[...], fi_ref[...] = lbr, lbi, fr, fi
        fr_, fi_, br_, bi_ = _per_channel(fr), _per_channel(fi), br_ref[...], bi_ref[...]
        bbr_ref[...] = fr_ * br_ - fi_ * bi_
        bbi_ref[...] = fr_ * bi_ + fi_ * br_

    return pl.pallas_call(body, name="ssm_params", out_shape=[S((NG, NP), f32)] * 4 + [S((NG * GC, NP), f32)] * 2)(
        lam_re, lam_im, log_dt, br, bi)


SCAN_UNROLL = 4
SCAN_LANES = 2 * LANE


def _steps(n, body, carry):
    main = n // SCAN_UNROLL

    def trip(t, c):
        for q in range(SCAN_UNROLL):
            c = body(t * SCAN_UNROLL + q, c)
        return c

    carry = lax.fori_loop(0, main, trip, carry)
    for i in range(main * SCAN_UNROLL, n):
        carry = body(i, carry)
    return carry


def _scan_body(T):
    L = T // NC
    RB = min(512, T)
    nsq = int(round(math.log2(L)))
    assert 2 ** nsq == L and T % RB == 0 and L % 16 == 0

    def rows(i):
        return pl.ds(pl.multiple_of(i * RB, RB), RB)

    def tile(j):
        return pl.ds(j * NC if isinstance(j, int) else pl.multiple_of(j * NC, NC), NC)

    def forward_states(u_ref, wb_ref, lbr_ref, lbi_ref, sre, sim, ere, eim):
        def bproj(i, carry):
            bu = _dot(u_ref[rows(i), :].astype(bf16), wb_ref[...])
            sre[rows(i), :] = bu[:, :SW]
            sim[rows(i), :] = bu[:, SW:]
            return carry

        lax.fori_loop(0, T // RB, bproj, 0)
        for lb in range(SW // SCAN_LANES):
            ls = slice(lb * SCAN_LANES, (lb + 1) * SCAN_LANES)
            ar = jnp.broadcast_to(lbr_ref[:, ls], (NC, SCAN_LANES))
            ai = jnp.broadcast_to(lbi_ref[:, ls], (NC, SCAN_LANES))

            def step(j, carry):
                xr, xi = carry
                nr = ar * xr - ai * xi + sre[tile(j), ls]
                ni = ar * xi + ai * xr + sim[tile(j), ls]
                sre[tile(j), ls] = nr
                sim[tile(j), ls] = ni
                return nr, ni

            zero = jnp.zeros((NC, SCAN_LANES), f32)
            _steps(L, step, (zero, zero))
            pr, pi = lbr_ref[:, ls], lbi_ref[:, ls]
            for _ in range(nsq):
                pr, pi = pr * pr - pi * pi, 2.0 * pr * pi
            er = jnp.zeros((1, SCAN_LANES), f32)
            ei = er
            ere[0:1, ls] = er
            eim[0:1, ls] = ei
            base = (L - 1) * NC
            for c in range(1, NC):
                lr_ = sre[base + c - 1:base + c, ls]
                li_ = sim[base + c - 1:base + c, ls]
                er, ei = lr_ + pr * er - pi * ei, li_ + pr * ei + pi * er
                ere[c:c + 1, ls] = er
                eim[c:c + 1, ls] = ei
            e_r, e_i = ere[:, ls].reshape(NC // 8, 8, SCAN_LANES), eim[:, ls].reshape(NC // 8, 8, SCAN_LANES)
            ar8, ai8 = ar[0:8], ai[0:8]

            def fix(j, carry):
                pwr, pwi = carry
                xr = sre[tile(j), ls].reshape(NC // 8, 8, SCAN_LANES) + (pwr * e_r - pwi * e_i)
                xi = sim[tile(j), ls].reshape(NC // 8, 8, SCAN_LANES) + (pwr * e_i + pwi * e_r)
                sre[tile(j), ls] = xr.reshape(NC, SCAN_LANES)
                sim[tile(j), ls] = xi.reshape(NC, SCAN_LANES)
                return pwr * ar8 - pwi * ai8, pwr * ai8 + pwi * ar8

            _steps(L, fix, (ar8, ai8))

    return L, RB, nsq, rows, tile, forward_states


def ssm_fwd(u, wb, wc, lbr, lbi, dsk, plan):
    T = u.shape[0]
    L, RB, nsq, rows, tile, forward_states = _scan_body(T)
    nslab = W // LANE

    def body(u_ref, wb_ref, wc_ref, lbr_ref, lbi_ref, d_ref, y_ref, up_ref, xr_ref, xi_ref, sre, sim, ere, eim, yp):
        _to_scan_order(u_ref, up_ref)
        forward_states(up_ref, wb_ref, lbr_ref, lbi_ref, sre, sim, ere, eim)

        def cproj(i, carry):
            xr, xi = sre[rows(i), :].astype(bf16), sim[rows(i), :].astype(bf16)
            xr_ref[rows(i), :] = xr
            xi_ref[rows(i), :] = xi
            y = _dot(xr, wc_ref[0:SW, :]) + _dot(xi, wc_ref[SW:, :])
            yp[rows(i), :] = y + d_ref[...] * up_ref[rows(i), :]
            return carry

        lax.fori_loop(0, T // RB, cproj, 0)
        _to_time_order(yp, y_ref)

    slab = pl.BlockSpec((T, LANE), lambda k: (0, k))
    states = pl.BlockSpec((T, SW), lambda k: (0, k))
    return _call(
        body, [u, wb, wc, lbr, lbi, dsk], name="ssm_fwd", grid=(nslab,),
        in_specs=[slab, pl.BlockSpec((None, LANE, 2 * SW), lambda k: (k, 0, 0)),
                  pl.BlockSpec((None, 2 * SW, LANE), lambda k: (k, 0, 0)),
                  pl.BlockSpec((None, 1, SW), lambda k: (k, 0, 0)), pl.BlockSpec((None, 1, SW), lambda k: (k, 0, 0)),
                  pl.BlockSpec((None, 1, LANE), lambda k: (k, 0, 0))],
        out_specs=[slab, slab, states, states],
        out_shape=[S((T, W), f32), S((T, W), f32), S((T, nslab * SW), bf16), S((T, nslab * SW), bf16)],
        scratch=[pltpu.VMEM((T, SW), f32), pltpu.VMEM((T, SW), f32), pltpu.VMEM((NC, SW), f32), pltpu.VMEM((NC, SW), f32),
                 pltpu.VMEM((T, LANE), f32)],
        vmem=VMEM_LIMIT, plan=plan)


def ssm_bwd(u_p, dy, xr, xi, wbT, wcT, lbr, lbi, dsk, plan):
    T = u_p.shape[0]
    L, RB, nsq, rows, tile, _ = _scan_body(T)

    def body(u_ref, dyt_ref, sre, sim, wbT_ref, wcT_ref, lbr_ref, lbi_ref, d_ref,
             dut_ref, dwb_ref, dwc_ref, dlr_ref, dli_ref, dd_ref, su_ref, gre, gim, ere, eim, dy_ref, du_ref):
        _to_scan_order(dyt_ref, dy_ref)

        def dstate(i, carry):
            g = _dot(dy_ref[rows(i), :].astype(bf16), wcT_ref[...])
            gre[rows(i), :] = g[:, :SW]
            gim[rows(i), :] = g[:, SW:]
            return carry

        lax.fori_loop(0, T // RB, dstate, 0)
        row = lax.broadcasted_iota(jnp.int32, (NC, SCAN_LANES), 0)
        for lb in range(SW // SCAN_LANES):
            ls = slice(lb * SCAN_LANES, (lb + 1) * SCAN_LANES)
            ar = jnp.broadcast_to(lbr_ref[:, ls], (NC, SCAN_LANES))
            ai = jnp.broadcast_to(lbi_ref[:, ls], (NC, SCAN_LANES))

            def step(i, carry):
                gr, gi = carry
                j = L - 1 - i
                nr = ar * gr + ai * gi + gre[tile(j), ls]
                ni = ar * gi - ai * gr + gim[tile(j), ls]
                gre[tile(j), ls] = nr
                gim[tile(j), ls] = ni
                return nr, ni

            zero = jnp.zeros((NC, SCAN_LANES), f32)
            _steps(L, step, (zero, zero))
            pr, pi = lbr_ref[:, ls], -lbi_ref[:, ls]
            for _ in range(nsq):
                pr, pi = pr * pr - pi * pi, 2.0 * pr * pi
            er = jnp.zeros((1, SCAN_LANES), f32)
            ei = er
            ere[NC - 1:NC, ls] = er
            eim[NC - 1:NC, ls] = ei
            for c in range(NC - 2, -1, -1):
                lr_ = gre[c + 1:c + 2, ls]
                li_ = gim[c + 1:c + 2, ls]
                er, ei = lr_ + pr * er - pi * ei, li_ + pr * ei + pi * er
                ere[c:c + 1, ls] = er
                eim[c:c + 1, ls] = ei
            e_r, e_i = ere[:, ls].reshape(NC // 8, 8, SCAN_LANES), eim[:, ls].reshape(NC // 8, 8, SCAN_LANES)
            ar8, ai8 = ar[0:8], ai[0:8]

            def fixed(j, pwr, pwi):
                gr = (gre[tile(j), ls].reshape(NC // 8, 8, SCAN_LANES) + (pwr * e_r - pwi * e_i)).reshape(NC, SCAN_LANES)
                gi = (gim[tile(j), ls].reshape(NC // 8, 8, SCAN_LANES) + (pwr * e_i + pwi * e_r)).reshape(NC, SCAN_LANES)
                gre[tile(j), ls] = gr
                gim[tile(j), ls] = gi
                return gr, gi

            def fix(i, carry):
                pwr, pwi, accr, acci = carry
                j = L - 1 - i
                gr, gi = fixed(j, pwr, pwi)
                xr, xi = sre[tile(j - 1), ls].astype(f32), sim[tile(j - 1), ls].astype(f32)
                return (pwr * ar8 + pwi * ai8, pwi * ar8 - pwr * ai8,
                        accr + gr * xr + gi * xi, acci + gi * xr - gr * xi)

            pwr, pwi, accr, acci = _steps(L - 1, fix, (ar8, -ai8, zero, zero))
            gr, gi = fixed(0, pwr, pwi)
            xr = jnp.where(row == 0, 0.0, pltpu.roll(sre[tile(L - 1), ls].astype(f32), 1, axis=0))
            xi = jnp.where(row == 0, 0.0, pltpu.roll(sim[tile(L - 1), ls].astype(f32), 1, axis=0))
            accr = accr + gr * xr + gi * xi
            acci = acci + gi * xr - gr * xi
            dlr_ref[:, ls] = jnp.sum(accr, axis=0, keepdims=True)
            dli_ref[:, ls] = jnp.sum(acci, axis=0, keepdims=True)

        dwb_ref[...] = jnp.zeros_like(dwb_ref)
        dwc_ref[...] = jnp.zeros_like(dwc_ref)
        dd_ref[...] = jnp.zeros_like(dd_ref)
        su_ref[...] = jnp.zeros_like(su_ref)

        def finish(i, carry):
            u32, dy32 = u_ref[rows(i), :], dy_ref[rows(i), :]
            ub, dyb = u32.astype(bf16), dy32.astype(bf16)
            gr, gi = gre[rows(i), :].astype(bf16), gim[rows(i), :].astype(bf16)
            du = _dot(gr, wbT_ref[0:SW, :]) + _dot(gi, wbT_ref[SW:, :]) + dy32 * d_ref[...]
            du_ref[rows(i), :] = du
            su_ref[...] += jnp.sum(du, axis=0, keepdims=True)
            dwb_ref[:, 0:SW] += _dot_tn(ub, gr)
            dwb_ref[:, SW:] += _dot_tn(ub, gi)
            dwc_ref[:, 0:SW] += _dot_tn(dyb, sre[rows(i), :])
            dwc_ref[:, SW:] += _dot_tn(dyb, sim[rows(i), :])
            dd_ref[...] += jnp.sum(dy32 * u32, axis=0, keepdims=True)
            return carry

        lax.fori_loop(0, T // RB, finish, 0)
        _to_time_order(du_ref, dut_ref)

    slab = pl.BlockSpec((T, LANE), lambda k: (0, k))
    wide = pl.BlockSpec((None, LANE, 2 * SW), lambda k: (k, 0, 0))
    tall = pl.BlockSpec((None, 2 * SW, LANE), lambda k: (k, 0, 0))
    vec = pl.BlockSpec((None, 1, SW), lambda k: (k, 0, 0))
    vecd = pl.BlockSpec((None, 1, LANE), lambda k: (k, 0, 0))
    states = pl.BlockSpec((T, SW), lambda k: (0, k))
    nslab = W // LANE
    return _call(
        body, [u_p, dy, xr, xi, wbT, wcT, lbr, lbi, dsk], name="ssm_bwd", grid=(nslab,),
        in_specs=[slab, slab, states, states, tall, wide, vec, vec, vecd],
        out_specs=[slab, wide, wide, vec, vec, vecd, vecd],
        out_shape=[S((T, W), bf16), S((nslab, LANE, 2 * SW), f32), S((nslab, LANE, 2 * SW), f32),
                   S((nslab, 1, SW), f32), S((nslab, 1, SW), f32), S((nslab, 1, LANE), f32), S((nslab, 1, LANE), f32)],
        scratch=[pltpu.VMEM((T, SW), f32)] * 2 + [pltpu.VMEM((NC, SW), f32)] * 2 + [pltpu.VMEM((T, LANE), f32)] * 2,
        vmem=VMEM_LIMIT, plan=plan)


def glu_fwd(yn, glu_w, glu_b):
    T = yn.shape[0]
    tm = min(512, T)

    def body(y_ref, w_ref, b_ref, o_ref):
        g = _gelu(y_ref[...])
        o_ref[...] = (g * _sigmoid(_dot(g.astype(bf16), w_ref[...]) + b_ref[...])).astype(bf16)

    return pl.pallas_call(
        body, name="glu_fwd", grid=(T // tm,),
        in_specs=[pl.BlockSpec((tm, W), lambda i: (i, 0)), pl.BlockSpec((W, W), lambda i: (0, 0)), pl.BlockSpec((1, W), lambda i: (0, 0))],
        out_specs=pl.BlockSpec((tm, W), lambda i: (i, 0)), out_shape=S((T, W), bf16), compiler_params=_cp(("parallel",)),
    )(yn, glu_w, glu_b)


def _shift_rows(cur, prev8, k):
    return pltpu.roll(jnp.concatenate([prev8, cur], axis=0), k, axis=0)[8:]


def _lift_rows(cur, next8, k):
    n = cur.shape[0]
    return pltpu.roll(jnp.concatenate([cur, next8], axis=0), n + 8 - k, axis=0)[:n]


def conv_fwd(proj, conv_w):
    T = proj.shape[0]
    RB = min(512, T)

    def body(h_ref, c_ref, b_ref, w_ref, o_ref):
        w0, w1, w2 = w_ref[0:1, :], w_ref[1:2, :], w_ref[2:3, :]

        def blk(i, carry):
            r0 = pl.multiple_of(i * RB, RB)
            rs = pl.ds(r0, RB)
            ch = c_ref[rs, :] * h_ref[rs, :]
            pr = pl.ds(jnp.maximum(r0 - 8, 0), 8)
            prev = jnp.where(i > 0, c_ref[pr, :] * h_ref[pr, :], 0.0)
            z = w2 * ch + w1 * _shift_rows(ch, prev, 1) + w0 * _shift_rows(ch, prev, 2)
            o_ref[rs, :] = (b_ref[rs, :] * z).astype(bf16)
            return carry

        lax.fori_loop(0, T // RB, blk, 0)

    nb = W // LANE
    return pl.pallas_call(
        body, name="conv_fwd", grid=(nb,),
        in_specs=[pl.BlockSpec((T, LANE), lambda k: (0, 4 * nb + k)), pl.BlockSpec((T, LANE), lambda k: (0, 5 * nb + k)),
                  pl.BlockSpec((T, LANE), lambda k: (0, 6 * nb + k)),pl.BlockSpec((3, LANE), lambda k: (0, k))],
        out_specs=pl.BlockSpec((T, LANE), lambda k: (0, k)), out_shape=S((T, W), bf16),
        compiler_params=_cp(("parallel",), VMEM_LIMIT),
    )(proj, proj, proj, conv_w)


def _dense_columns(blocks_ref, dense_ref):
    for k in range(NDEV):
        dense_ref[:, k * LANE:(k + 1) * LANE] = blocks_ref[k]


def merge_fwd(ya, yb, wso, wco, proj, plan):
    T = ya.shape[0]
    tm = min(1024, T)

    def body(ya_ref, yb_ref, wa_ref, wb_ref, ga_ref, gb_ref, o_ref, wa_s, wb_s):
        @pl.when(pl.program_id(0) == 0)
        def _():
            _dense_columns(wa_ref, wa_s)
            _dense_columns(wb_ref, wb_s)

        o_ref[...] = (_sigmoid(ga_ref[...]) * _dot(ya_ref[...], wa_s[...])
                      + _sigmoid(gb_ref[...]) * _dot(yb_ref[...], wb_s[...])).astype(bf16)

    act = pl.BlockSpec((tm, W), lambda i: (i, 0))
    return _call(
        body, [ya, yb, wso, wco, proj, proj], name="merge_fwd", grid=(T // tm,),
        in_specs=[act, act, _resident((NDEV, W, LANE)), _resident((NDEV, W, LANE)),
                  pl.BlockSpec((tm, D), lambda i: (i, 0)), pl.BlockSpec((tm, D), lambda i: (i, 1))],
        out_specs=[pl.BlockSpec((tm, D), lambda i: (i, 0))], out_shape=[S((T, D), bf16)],
        scratch=[pltpu.VMEM((W, D), bf16), pltpu.VMEM((W, D), bf16)], vmem=VMEM_LIMIT, plan=plan)


def mix_ln1(merged, w_o, x, g1, b1, plan):
    T = x.shape[0]
    tm = min(512, T)

    def body(m_ref, w_ref, x_ref, g_ref, b_ref, r_ref, x1_ref):
        for rs in _row_parts(tm):
            r = ALPHA * x_ref[rs, :] + _dot(m_ref[rs, :], w_ref[...])
            r_ref[rs, :] = r
            xhat, _ = _ln_stats(r)
            x1_ref[rs, :] = (xhat * g_ref[...] + b_ref[...]).astype(bf16)

    row = pl.BlockSpec((tm, D), lambda i: (i, 0))
    vec = pl.BlockSpec((1, D), lambda i: (0, 0))
    return _call(
        body, [merged, w_o, x, g1, b1], name="mix_ln1", grid=(T // tm,),
        in_specs=[row, _resident((D, D)), row, vec, vec],
        out_specs=[row, row], out_shape=[S((T, D), f32), S((T, D), bf16)], sem=("parallel",), vmem=VMEM_LIMIT, plan=plan,
        relay_step=T // tm - 2)


FT = 256


def gate_up(x1b, wgT, wuT, plan):
    T = x1b.shape[0]
    tm = min(512, T)

    def body(x_ref, wg_ref, wu_ref, g_ref, u_ref, h_ref):
        x = x_ref[...]
        for n in range(F // FT):
            cs = slice(n * FT, (n + 1) * FT)
            g = _dot_nt(x, wg_ref[cs, :])
            u = _dot_nt(x, wu_ref[cs, :])
            g_ref[:, cs] = g.astype(bf16)
            u_ref[:, cs] = u.astype(bf16)
            h_ref[:, cs] = (g * _sigmoid(g) * u).astype(bf16)

    osp = pl.BlockSpec((tm, F), lambda i: (i, 0))
    return _call(
        body, [x1b, wgT, wuT], name="gate_up", grid=(T // tm,),
        in_specs=[pl.BlockSpec((tm, D), lambda i: (i, 0)), _resident((F, D)), _resident((F, D))],
        out_specs=[osp, osp, osp], out_shape=[S((T, F), bf16)] * 3, vmem=VMEM_LIMIT, plan=plan, relay_step=T // tm - 3)


def down_loss(hid, w_down, r1, g1, b1, g2, b2, target):
    T = hid.shape[0]
    tm = min(512, T)

    def body(h_ref, w_ref, r1_ref, g1_ref, b1_ref, g2_ref, b2_ref, t_ref, dr_ref, drb_ref, loss_ref, dg_ref, db_ref):
        @pl.when(pl.program_id(0) == 0)
        def _():
            loss_ref[...] = jnp.zeros_like(loss_ref)
            dg_ref[...] = jnp.zeros_like(dg_ref)
            db_ref[...] = jnp.zeros_like(db_ref)

        for rs in _row_parts(tm):
            xh1, _ = _ln_stats(r1_ref[rs, :])
            x1 = xh1 * g1_ref[...] + b1_ref[...]
            r2 = ALPHA * x1 + _dot(h_ref[rs, :], w_ref[...])
            xh2, rstd2 = _ln_stats(r2)
            err = xh2 * g2_ref[...] + b2_ref[...] - t_ref[rs, :]
            loss_ref[...] += jnp.sum(jnp.mean(err * err, axis=-1, keepdims=True), axis=0, keepdims=True)
            dy = err * (1.0 / D)
            dg_ref[...] += jnp.sum(dy * xh2, axis=0, keepdims=True)
            db_ref[...] += jnp.sum(dy, axis=0, keepdims=True)
            dr = _ln_bwd(dy, xh2, rstd2, g2_ref[...])
            dr_ref[rs, :] = dr
            drb_ref[rs, :] = dr.astype(bf16)

    row = pl.BlockSpec((tm, D), lambda i: (i, 0))
    vec = pl.BlockSpec((1, D), lambda i: (0, 0))
    return pl.pallas_call(
        body, name="down_loss", grid=(T // tm,),
        in_specs=[pl.BlockSpec((tm, F), lambda i: (i, 0)), _resident((F, D)), row, vec, vec, vec, vec, row],
        out_specs=[row, row, pl.BlockSpec((1, 1), lambda i: (0, 0)), vec, vec],
        out_shape=[S((T, D), f32), S((T, D), bf16), S((1, 1), f32), S((1, D), f32), S((1, D), f32)],
        compiler_params=_cp(("arbitrary",), VMEM_LIMIT),
    )(hid, w_down, r1, g1, b1, g2, b2, target)


def ffn_bwd_act(dffn, w_down, gate, up, plan):
    T = dffn.shape[0]
    tm = min(512, T)

    def body(d_ref, w_ref, g_ref, u_ref, dg_ref, du_ref):
        for n in range(F // FT):
            cs = slice(n * FT, (n + 1) * FT)
            for rs in _row_parts(tm):
                dh = _dot_nt(d_ref[rs, :], w_ref[cs, :])
                g, u = g_ref[rs, cs].astype(f32), u_ref[rs, cs].astype(f32)
                sg = _sigmoid(g)
                t = g * sg
                du_ref[rs, cs] = (dh * t).astype(bf16)
                dg_ref[rs, cs] = (dh * u * (sg + t - t * sg)).astype(bf16)

    osp = pl.BlockSpec((tm, F), lambda i: (i, 0))
    return _call(
        body, [dffn, w_down, gate, up], name="ffn_bwd_act", grid=(T // tm,),
        in_specs=[pl.BlockSpec((tm, D), lambda i: (i, 0)), _resident((F, D)), osp, osp],
        out_specs=[osp, osp], out_shape=[S((T, F), bf16)] * 2, sem=("parallel",), vmem=VMEM_LIMIT, plan=plan)


def ffn_bwd_x(dgate, dup, wgT, wuT, dr2, r1, g1, plan):
    T = dr2.shape[0]
    tm = min(512, T)

    def body(dg_ref, du_ref, wg_ref, wu_ref, dr2_ref, r1_ref, g1_ref, dr_ref, drb_ref, dgam_ref, dbet_ref):
        @pl.when(pl.program_id(0) == 0)
        def _():
            dgam_ref[...] = jnp.zeros_like(dgam_ref)
            dbet_ref[...] = jnp.zeros_like(dbet_ref)

        for rs in _row_parts(tm):
            dx1 = ALPHA * dr2_ref[rs, :] + _dot(dg_ref[rs, :], wg_ref[...]) + _dot(du_ref[rs, :], wu_ref[...])
            xh, rstd = _ln_stats(r1_ref[rs, :])
            dgam_ref[...] += jnp.sum(dx1 * xh, axis=0, keepdims=True)
            dbet_ref[...] += jnp.sum(dx1, axis=0, keepdims=True)
            dr = _ln_bwd(dx1, xh, rstd, g1_ref[...])
            dr_ref[rs, :] = dr
            drb_ref[rs, :] = dr.astype(bf16)

    row = pl.BlockSpec((tm, D), lambda i: (i, 0))
    wide = pl.BlockSpec((tm, F), lambda i: (i, 0))
    wsp = _resident((F, D))
    vec = pl.BlockSpec((1, D), lambda i: (0, 0))
    return _call(
        body, [dgate, dup, wgT, wuT, dr2, r1, g1], name="ffn_bwd_x", grid=(T // tm,),
        in_specs=[wide, wide, wsp, wsp, row, row, vec],
        out_specs=[row, row, vec, vec], out_shape=[S((T, D), f32), S((T, D), bf16), S((1, D), f32), S((1, D), f32)],
        vmem=VMEM_LIMIT, plan=plan)


def merge_bwd(dmix, w_o, ya, yb, wso, wco, proj, plan):
    T = dmix.shape[0]
    tm = min(512, T)

    def body(dm_ref, wo_ref, ya_ref, yb_ref, wa_ref, wb_ref, ga_ref, gb_ref, dya_ref, dyb_ref, dga_ref, dgb_ref, sa_ref, sb_ref,
             wa_s, wb_s):
        @pl.when(pl.program_id(0) == 0)
        def _():
            _dense_columns(wa_ref, wa_s)
            _dense_columns(wb_ref, wb_s)

        dmer = _dot_nt(dm_ref[...], wo_ref[...])
        sa, sb = _sigmoid(ga_ref[...]), _sigmoid(gb_ref[...])
        dya_ref[...] = (dmer * sa).astype(bf16)
        dyb_ref[...] = (dmer * sb).astype(bf16)
        dga = dmer * _dot(ya_ref[...], wa_s[...]) * sa * (1.0 - sa)
        dgb = dmer * _dot(yb_ref[...], wb_s[...]) * sb * (1.0 - sb)
        dga_ref[...] = dga.astype(bf16)
        dgb_ref[...] = dgb.astype(bf16)
        sa_ref[...] = jnp.sum(dga, axis=0, keepdims=True)
        sb_ref[...] = jnp.sum(dgb, axis=0, keepdims=True)

    act = pl.BlockSpec((tm, W), lambda i: (i, 0))
    osp = pl.BlockSpec((tm, D), lambda i: (i, 0))
    ssp = pl.BlockSpec((None, 1, D), lambda i: (i, 0, 0))
    return _call(
        body, [dmix, w_o, ya, yb, wso, wco, proj, proj], name="merge_bwd", grid=(T // tm,),
        in_specs=[osp, _resident((D, D)), act, act, _resident((NDEV, W, LANE)), _resident((NDEV, W, LANE)),
                  pl.BlockSpec((tm, D), lambda i: (i, 0)), pl.BlockSpec((tm, D), lambda i: (i, 1))],
        out_specs=[osp, osp, osp, osp, ssp, ssp],
        out_shape=[S((T, D), bf16)] * 4 + [S((T // tm, 1, D), f32)] * 2,
        scratch=[pltpu.VMEM((W, D), bf16), pltpu.VMEM((W, D), bf16)], vmem=VMEM_LIMIT, plan=plan)


def branches_bwd_x(dYA, dYB, wso, wco, plan):
    T = dYA.shape[0]
    tm = min(1024, T)

    def body(da_ref, db_ref, wa_ref, wb_ref, oa_ref, ob_ref, wa_s, wb_s):
        @pl.when(pl.program_id(0) == 0)
        def _():
            _dense_columns(wa_ref, wa_s)
            _dense_columns(wb_ref, wb_s)

        oa_ref[...] = _dot_nt(da_ref[...], wa_s[...])
        ob_ref[...] = _dot_nt(db_ref[...], wb_s[...])

    row = pl.BlockSpec((tm, D), lambda i: (i, 0))
    osp = pl.BlockSpec((tm, W), lambda i: (i, 0))
    return _call(
        body, [dYA, dYB, wso, wco], name="branches_bwd_x", grid=(T // tm,),
        in_specs=[row, row, _resident((NDEV, W, LANE)), _resident((NDEV, W, LANE))],
        out_specs=[osp, osp], out_shape=[S((T, W), f32)] * 2,
        scratch=[pltpu.VMEM((W, D), bf16), pltpu.VMEM((W, D), bf16)], vmem=VMEM_LIMIT, plan=plan)


def branch_bwd_w(act, dY, name):
    T = act.shape[0]
    tk = W // 2

    def body(a_ref, d_ref, o_ref):
        res = _dot_tn(a_ref[...], d_ref[...])
        for k in range(NDEV):
            o_ref[k] = res[:, k * LANE:(k + 1) * LANE].astype(o_ref.dtype)

    return pl.pallas_call(
        body, name=name, grid=(W // tk,),
        in_specs=[pl.BlockSpec((T, tk), lambda i: (0, i)), _resident((T, D))],
        out_specs=pl.BlockSpec((NDEV, tk, LANE), lambda i: (0, i, 0)), out_shape=S((NDEV, W, LANE), GRAD_DT),
        compiler_params=_cp(("parallel",), VMEM_LIMIT),
    )(act, dY)


def glu_bwd(yn, dya, glu_w, glu_b, plan):
    T = yn.shape[0]
    tm = min(512, T)

    def body(y_ref, d_ref, w_ref, b_ref, dy_ref, dsp_ref, g_ref, db_ref):
        @pl.when(pl.program_id(0) == 0)
        def _():
            db_ref[...] = jnp.zeros_like(db_ref)

        y, dya_ = y_ref[...], d_ref[...]
        g = _gelu(y)
        gb = g.astype(bf16)
        s = _sigmoid(_dot(gb, w_ref[...]) + b_ref[...])
        dsp = dya_ * g * s * (1.0 - s)
        dspb = dsp.astype(bf16)
        dg = dya_ * s + _dot_nt(dspb, w_ref[...])
        dy_ref[...] = dg * _gelu_grad(y)
        dsp_ref[...] = dspb
        g_ref[...] = gb
        db_ref[...] += jnp.sum(dsp, axis=0, keepdims=True)

    row = pl.BlockSpec((tm, W), lambda i: (i, 0))
    vec = pl.BlockSpec((1, W), lambda i: (0, 0))
    return _call(
        body, [yn, dya, glu_w, glu_b], name="glu_bwd", grid=(T // tm,),
        in_specs=[row, row, pl.BlockSpec((W, W), lambda i: (0, 0)), vec],
        out_specs=[row, row, row, vec], out_shape=[S((T, W), f32), S((T, W), bf16), S((T, W), bf16), S((1, W), f32)],
        sem=("arbitrary",), plan=plan)


def conv_bwd(proj, dyb, conv_w, plan):
    T = proj.shape[0]
    RB = min(512, T)
    nrb = T // RB

    def body(h_ref, c_ref, b_ref, d_ref, w_ref, dh_ref, dc_ref, db_ref, dw_ref, s_ref):
        w0, w1, w2 = w_ref[0:1, :], w_ref[1:2, :], w_ref[2:3, :]

        def blk(i, carry):
            a0, a1, a2, sh, sc, sb = carry
            r0 = pl.multiple_of(i * RB, RB)
            rs = pl.ds(r0, RB)
            h, cg, bg, dyb_ = h_ref[rs, :], c_ref[rs, :], b_ref[rs, :], d_ref[rs, :]
            ch = cg * h
            pr = pl.ds(jnp.maximum(r0 - 8, 0), 8)
            prev = jnp.where(i > 0, c_ref[pr, :] * h_ref[pr, :], 0.0)
            ch1, ch2 = _shift_rows(ch, prev, 1), _shift_rows(ch, prev, 2)
            dbg = dyb_ * (w2 * ch + w1 * ch1 + w0 * ch2)
            db_ref[rs, :] = dbg.astype(bf16)
            dz = dyb_ * bg
            nx = pl.ds(jnp.minimum(r0 + RB, T - 8), 8)
            nxt = jnp.where(i < nrb - 1, d_ref[nx, :] * b_ref[nx, :], 0.0)
            dch = w2 * dz + w1 * _lift_rows(dz, nxt, 1) + w0 * _lift_rows(dz, nxt, 2)
            dcg, dh = dch * h, dch * cg
            dc_ref[rs, :] = dcg.astype(bf16)
            dh_ref[rs, :] = dh.astype(bf16)
            col = lambda v: jnp.sum(v, axis=0, keepdims=True)
            return (a0 + col(dz * ch2), a1 + col(dz * ch1), a2 + col(dz * ch), sh + col(dh), sc + col(dcg), sb + col(dbg))

        zero = jnp.zeros((1, LANE), f32)
        a0, a1, a2, sh, sc, sb = lax.fori_loop(0, nrb, blk, (zero,) * 6)
        dw_ref[0:1, :] = a0
        dw_ref[1:2, :] = a1
        dw_ref[2:3, :] = a2
        s_ref[0:1, :] = sh
        s_ref[1:2, :] = sc
        s_ref[2:3, :] = sb

    nb = W // LANE
    slab = pl.BlockSpec((T, LANE), lambda k: (0, k))
    three = pl.BlockSpec((3, LANE), lambda k: (0, k))
    return _call(
        body, [proj, proj, proj, dyb, conv_w], name="conv_bwd", grid=(nb,),
        in_specs=[pl.BlockSpec((T, LANE), lambda k: (0, 4 * nb + k)), pl.BlockSpec((T, LANE), lambda k: (0, 5 * nb + k)),
                  pl.BlockSpec((T, LANE), lambda k: (0, 6 * nb + k)), slab, three],
        out_specs=[slab, slab, slab, three, three],
        out_shape=[S((T, W), bf16)] * 3 + [S((3, W), f32)] * 2, sem=("parallel",), vmem=VMEM_LIMIT, plan=plan)


def in_proj_bwd_x(parts, win_g, base, scale, name, plan=None):
    T = base.shape[0]
    tm = min(512, T)
    n = len(parts)

    def body(*refs):
        p_refs, w_ref, b_ref, o_ref = refs[:n], refs[n], refs[n + 1], refs[n + 2]
        acc = scale * b_ref[...]
        for p_ref, (_, _, k) in zip(p_refs, parts):
            acc += _dot_nt(p_ref[...], w_ref[k])
        o_ref[...] = acc

    row = pl.BlockSpec((tm, D), lambda i: (i, 0))
    p_specs = [pl.BlockSpec((tm, W), (lambda i, cb=cb: (i, cb))) for _, cb, _ in parts]
    return _call(
        body, [a for a, _, _ in parts] + [win_g, base], name=name, grid=(T // tm,),
        in_specs=p_specs + [_resident((NDEV, D, W)), row],
        out_specs=[row], out_shape=[S((T, D), f32)], vmem=VMEM_LIMIT, plan=plan)


def ssm_param_bwd(lam_re, lam_im, log_dt, fr, fi, br, bi, dwb, dwcT, dlbr, dlbi):
    def body(lr_ref, li_ref, ldt_ref, fr_ref, fi_ref, br_ref, bi_ref, dwb_ref, dwc_ref, dlbr_ref, dlbi_ref,
             dbr_ref, dbi_ref, dlr_ref, dli_ref, dldt_ref, dcr_ref, dci_ref, dr_s, di_s):
        for k in range(W // LANE):
            for gl in range(NG // (W // LANE)):
                rows, src = slice((8 * k + gl) * GC, (8 * k + gl + 1) * GC), slice(gl * GC, (gl + 1) * GC)
                re, im = slice(gl * NP, (gl + 1) * NP), slice(SW + gl * NP, SW + (gl + 1) * NP)
                dr_s[rows, :] = dwb_ref[k, src, re]
                di_s[rows, :] = dwb_ref[k, src, im]
                dcr_ref[rows, :] = dwc_ref[k, src, re]
                dci_ref[rows, :] = -dwc_ref[k, src, im]
        fr_, fi_ = _per_channel(fr_ref[...]), _per_channel(fi_ref[...])
        br_, bi_, dr, di = br_ref[...], bi_ref[...], dr_s[...], di_s[...]
        dbr_ref[...] = fr_ * dr + fi_ * di
        dbi_ref[...] = fr_ * di - fi_ * dr
        dfr = jnp.sum((dr * br_ + di * bi_).reshape(NG, GC, NP), axis=1)
        dfi = jnp.sum((di * br_ - dr * bi_).reshape(NG, GC, NP), axis=1)
        _, vjp = jax.vjp(_disc, lr_ref[...], li_ref[...], ldt_ref[...])
        dlr_ref[...], dli_ref[...], dldt = vjp((dlbr_ref[...], dlbi_ref[...], dfr, dfi))
        dldt_ref[...] = _transpose_exact(dldt)

    blk = S((NG * GC, NP), f32)
    return pl.pallas_call(
        body, name="ssm_param_bwd", out_shape=[blk, blk, S((NG, NP), f32), S((NG, NP), f32), S((1, NG), f32), blk, blk],
        scratch_shapes=[pltpu.VMEM((NG * GC, NP), f32)] * 2)(
        lam_re, lam_im, log_dt, fr, fi, br, bi, dwb, dwcT, dlbr, dlbi)


def _adam(w, g, m, v):
    m = ADAM_B1 * m + (1.0 - ADAM_B1) * g
    v = ADAM_B2 * v + (1.0 - ADAM_B2) * (g * g)
    m_hat = m / (1.0 - ADAM_B1 ** ADAM_STEP)
    v_hat = v / (1.0 - ADAM_B2 ** ADAM_STEP)
    return -ADAM_LR * (m_hat / (jnp.sqrt(v_hat) + ADAM_EPS) + ADAM_WD * w), m, v


def _sum_in_order(c_ref):
    g = c_ref[0].astype(f32)
    for k in range(1, c_ref.shape[0]):
        g = g + c_ref[k].astype(f32)
    return g


def sum_blocks(contrib, name):
    def body(c_ref, o_ref):
        o_ref[...] = _sum_in_order(c_ref)

    return pl.pallas_call(body, name=name, out_shape=S(contrib.shape[1:], f32))(contrib)


def adam_update(w, m, v, contrib, name, rows_per_block=None, summed_on_0=None, plan=None):
    R, C = w.shape
    n = contrib.shape[0]
    tr = min(rows_per_block or R, R)

    def body(w_ref, m_ref, v_ref, c_ref, *refs):
        g_ref, d_ref, nm_ref, nv_ref = refs[-4:]
        g = _sum_in_order(c_ref)
        if summed_on_0 is not None:
            x, y, c = _coords()
            g = jnp.where(4 * x + 2 * y + c == 0, refs[0][...], g)
        g_ref[...] = g
        d_ref[...], nm_ref[...], nv_ref[...] = _adam(w_ref[...], g, m_ref[...], v_ref[...])

    blk = pl.BlockSpec((tr, C), lambda i: (i, 0))
    extra = [] if summed_on_0 is None else [summed_on_0]
    return _call(
        body, [w, m, v, contrib] + extra, name=name, grid=(R // tr,),
        in_specs=[blk, blk, blk, pl.BlockSpec((n, tr, C), lambda i: (0, i, 0))] + [blk] * len(extra),
        out_specs=[blk] * 4, out_shape=[S((R, C), f32)] * 4, sem=("parallel",), vmem=VMEM_LIMIT, plan=plan)


_ROWVEC = (("b_in", IN_COLS), ("ssm_d", W), ("glu_b", W), ("ln1_g", D), ("ln1_b", D), ("ln2_g", D), ("ln2_b", D))
_HALF = NG * GC // 2
_BC_LANE = {"ssm_b_re": 0, "ssm_b_im": NP, "ssm_c_re": 0, "ssm_c_im": NP}
_PACK = {}
_r = 0
for _n, _k in _ROWVEC:
    _PACK[_n] = _r
    _r += _k // LANE
for _n, _rows in (("ssm_lambda", NG), ("scalars", 8), ("ssm_b", _HALF), ("ssm_c", _HALF), ("conv_w", 16)):
    _PACK[_n] = _r
    _r += _rows
for _n in _BC_LANE:
    _PACK[_n] = _PACK[_n[:5]]
PACK_ROWS = _r
assert PACK_ROWS % 8 == 0
_SMALL = ("b_in", "ssm_lambda_re", "ssm_lambda_im", "ssm_log_dt", "ssm_b_re", "ssm_b_im", "ssm_c_re", "ssm_c_im",
          "ssm_d", "glu_b", "ln1_g", "ln1_b", "ln2_g", "ln2_b")


def pack_grads(su, shcb, sga, sgb, dd, dglu_b, dln1_g, dln1_b, dln2_g, dln2_b, dlam_re, dlam_im, dldt, sqerr, dbr, dbi,
               dc_re, dc_im, dconv):
    nI = sga.shape[0]

    def body(su_ref, sh_ref, sga_ref, sgb_ref, dd_ref, gb_ref, l1g_ref, l1b_ref, l2g_ref, l2b_ref, lr_ref, li_ref, dt_ref,
             sq_ref, br_ref, bi_ref, cr_ref, ci_ref, cw_ref, o_ref):
        o_ref[...] = jnp.zeros_like(o_ref)

        def put_row(name, v):
            r0 = _PACK[name]
            for i in range(v.shape[1] // LANE):
                o_ref[r0 + i:r0 + i + 1, :] = v[:, i * LANE:(i + 1) * LANE]

        ga, gb = sga_ref[0], sgb_ref[0]
        for i in range(1, nI):
            ga, gb = ga + sga_ref[i], gb + sgb_ref[i]
        put_row("b_in", jnp.concatenate([su_ref[k] for k in range(W // LANE)]
                                        + [sh_ref[0:1, :], sh_ref[1:2, :], sh_ref[2:3, :], ga, gb], axis=1))
        put_row("ssm_d", jnp.concatenate([dd_ref[k] for k in range(W // LANE)], axis=1))
        put_row("glu_b", gb_ref[...])
        put_row("ln1_g", l1g_ref[...])
        put_row("ln1_b", l1b_ref[...])
        put_row("ln2_g", l2g_ref[...])
        put_row("ln2_b", l2b_ref[...])
        r0 = _PACK["ssm_lambda"]
        o_ref[r0:r0 + NG, 0:NP] = lr_ref[...]
        o_ref[r0:r0 + NG, NP:2 * NP] = li_ref[...]
        r0 = _PACK["scalars"]
        o_ref[r0:r0 + 1, 0:NG] = dt_ref[...]
        o_ref[r0 + 1:r0 + 2, 0:1] = sq_ref[...]
        for name, ref in (("ssm_b_re", br_ref), ("ssm_b_im", bi_ref), ("ssm_c_re", cr_ref), ("ssm_c_im", ci_ref)):
            r0, l0 = _PACK[name], _BC_LANE[name]
            o_ref[r0:r0 + _HALF, l0:l0 + NP] = pltpu.bitcast(ref[...].astype(bf16), f32)
        for cb in range(W // LANE):
            o_ref[_PACK["conv_w"] + 3 * cb:_PACK["conv_w"] + 3 * cb + 3, :] = cw_ref[:, cb * LANE:(cb + 1) * LANE]

    return pl.pallas_call(body, name="pack_grads", out_shape=S((PACK_ROWS, LANE), f32))(
        su, shcb, sga, sgb, dd, dglu_b, dln1_g, dln1_b, dln2_g, dln2_b, dlam_re, dlam_im, dldt, sqerr, dbr, dbi, dc_re, dc_im,
        dconv)


def adam_small(packed_all, params):
    names = list(_SMALL) + ["conv_w"]
    flat = [a for n in names for a in params[n]]

    def body(*refs):
        p_ref = refs[0]
        ins = refs[1:1 + 3 * len(names)]
        outs = refs[1 + 3 * len(names):-2]
        loss_ref, g_ref = refs[-2], refs[-1]

        def part(k, rs=slice(None), ls=slice(None)):
            return p_ref[k, rs, ls]

        g_all = part(0)
        for k in range(1, NDEV):
            g_all = g_all + part(k)
        g_ref[...] = g_all

        def rows(name, r0, n, l0=0, lanes=LANE):
            return g_ref[_PACK[name] + r0:_PACK[name] + r0 + n, l0:l0 + lanes]

        def grad_of(name):
            if name in dict(_ROWVEC):
                return jnp.concatenate([rows(name, i, 1) for i in range(dict(_ROWVEC)[name] // LANE)], axis=1)
            if name in ("ssm_lambda_re", "ssm_lambda_im"):
                return rows("ssm_lambda", 0, NG, NP * (name == "ssm_lambda_im"), NP)[None]
            if name == "ssm_log_dt":
                return rows("scalars", 0, 1, 0, NG)
            if name in _BC_LANE:
                rs, ls = slice(_PACK[name], _PACK[name] + _HALF), slice(_BC_LANE[name], _BC_LANE[name] + NP)
                g = pltpu.bitcast(part(0, rs, ls), bf16).astype(f32)
                for k in range(1, NDEV):
                    g = g + pltpu.bitcast(part(k, rs, ls), bf16).astype(f32)
                return g.reshape(1, NG, GC, NP)
            full = jnp.concatenate([rows("conv_w", 3 * cb, 3) for cb in range(W // LANE)], axis=1)
            x, y, c = _coords()
            col0 = (4 * x + 2 * y + c) * (W // NDEV)
            sel = (lax.broadcasted_iota(jnp.int32, (W, W // NDEV), 0)
                   == lax.broadcasted_iota(jnp.int32, (W, W // NDEV), 1) + col0).astype(f32)
            return jnp.dot(full, sel, precision=HIGHEST, preferred_element_type=f32)[None]

        loss_ref[...] = 0.5 * rows("scalars", 1, 1, 0, 1)
        for i, name in enumerate(names):
            w_ref, m_ref, v_ref = ins[3 * i:3 * i + 3]
            g = grad_of(name)
            d, m, v = _adam(w_ref[...], g, m_ref[...], v_ref[...])
            outs[4 * i][...] = g
            outs[4 * i + 1][...] = d
            outs[4 * i + 2][...] = m
            outs[4 * i + 3][...] = v

    out_shape = [S(params[n][0].shape, f32) for n in names for _ in range(4)] + [S((1, 1), f32)]
    res = pl.pallas_call(body, name="adam_small", out_shape=out_shape, scratch_shapes=[pltpu.VMEM((PACK_ROWS, LANE), f32)],
                         compiler_params=_cp(None, VMEM_LIMIT))(packed_all, *flat)
    return {n: res[4 * i:4 * i + 4] for i, n in enumerate(names)}, res[-1]


def _block_diag(wgt):
    eye = jnp.eye(8, dtype=wgt.dtype)
    out = wgt[:, :, :, None, :] * eye[None, :, None, :, None]
    return out.reshape(4, 8 * wgt.shape[2], 8 * wgt.shape[3])


def kernel(x, w_in, b_in, ssm_lambda_re, ssm_lambda_im, ssm_log_dt, ssm_b_re, ssm_b_im, ssm_c_re, ssm_c_im, ssm_d, glu_w, glu_b, w_ssm_out, conv_w, w_conv_out, w_o, ln1_g, ln1_b, w_gate, w_up, w_down, ln2_g, ln2_b, loss_target, m_w_in, m_b_in, m_ssm_lambda_re, m_ssm_lambda_im, m_ssm_log_dt, m_ssm_b_re, m_ssm_b_im, m_ssm_c_re, m_ssm_c_im, m_ssm_d, m_glu_w, m_glu_b, m_w_ssm_out, m_conv_w, m_w_conv_out, m_w_o, m_ln1_g, m_ln1_b, m_w_gate, m_w_up, m_w_down, m_ln2_g, m_ln2_b, v_w_in, v_b_in, v_ssm_lambda_re, v_ssm_lambda_im, v_ssm_log_dt, v_ssm_b_re, v_ssm_b_im, v_ssm_c_re, v_ssm_c_im, v_ssm_d, v_glu_w, v_glu_b, v_w_ssm_out, v_conv_w, v_w_conv_out, v_w_o, v_ln1_g, v_ln1_b, v_w_gate, v_w_up, v_w_down, v_ln2_g, v_ln2_b):
    given = dict(locals())
    xs = x[0]
    target = loss_target[0]

    tr = lambda a: jnp.swapaxes(a[0], 0, 1)
    win_s, glu_s, wso_s, wco_s, wo_s, wgT_s, wuT_s, wd_s = prep_weights(
        [w_in[0], glu_w[0], w_ssm_out[0], w_conv_out[0], w_o[0], tr(w_gate), tr(w_up), w_down[0]])
    (win_g,) = run_plan(GatherPlan([win_s], srcs=(0,)), "gather_w_in_u")

    lam_re, lam_im = ssm_lambda_re[0], ssm_lambda_im[0]
    ldt = ssm_log_dt[0].reshape(NG, 1)
    br2 = jnp.swapaxes(ssm_b_re[0], 1, 2).reshape(NG * GC, NP)
    bi2 = jnp.swapaxes(ssm_b_im[0], 1, 2).reshape(NG * GC, NP)
    lbr, lbi, fr, fi, bbr, bbi = ssm_params(lam_re, lam_im, ldt, br2, bi2)
    bb_t = lambda b: b.reshape(4, 8, GC, NP)
    wb = jnp.concatenate([_block_diag(bb_t(bbr)), _block_diag(bb_t(bbi))], axis=2)
    c_t = lambda c: c.reshape(4, 8, GC, NP).transpose(0, 1, 3, 2)
    wc = jnp.concatenate([_block_diag(c_t(ssm_c_re[0])), -_block_diag(c_t(ssm_c_im[0]))], axis=1)
    wbT, wcT = wb.transpose(0, 2, 1), wc.transpose(0, 2, 1)
    wb, wc, wbT, wcT = wb.astype(bf16), wc.astype(bf16), wbT.astype(bf16), wcT.astype(bf16)
    lbr_s, lbi_s = lbr.reshape(4, 1, SW), lbi.reshape(4, 1, SW)
    dsk = ssm_d[0].reshape(4, 1, LANE)

    u_nat, xb = in_proj_u(xs, win_g, b_in)
    half_a, half_b = (0, 3, 5, 6), (1, 2, 4, 7)
    (yn, u_p, xr_p, xi_p), (win_g, conv_g, glu_g, wso_g, wuT_g) = ssm_fwd(
        u_nat, wb, wc, lbr_s, lbi_s, dsk,
        Plans([GatherPlan([win_s], srcs=tuple(range(1, NDEV)), into=[win_g]), GatherPlan([conv_w[0], glu_s, wso_s]),
               GatherPlan([wuT_s], srcs=half_a)]))
    conv_f = conv_g.transpose(1, 0, 2).reshape(3, W)
    (proj,), (wco_g, wo_g, wgT_g) = in_proj_rest(
        xb, win_g, b_in, Plans([GatherPlan([wco_s, wo_s]), GatherPlan([wgT_s], srcs=half_a)]))
    glu_f, wo_f = glu_g.reshape(W, W), wo_g.reshape(D, D)
    ya = glu_fwd(yn, glu_f, glu_b)
    yb = conv_fwd(proj, conv_f)
    (merged,), (wgT_g,) = merge_fwd(ya, yb, wso_g, wco_g, proj, GatherPlan([wgT_s], srcs=half_b, into=[wgT_g]))
    (r1, x1b), (wuT_g,) = mix_ln1(merged, wo_f, xs, ln1_g, ln1_b, GatherPlan([wuT_s], srcs=half_b, into=[wuT_g]))
    wgT, wuT = wgT_g.reshape(F, D), wuT_g.reshape(F, D)
    (gate, up, hid), (wd_g,) = gate_up(x1b, wgT, wuT, GatherPlan([wd_s]))
    wd_f = wd_g.reshape(F, D)
    dr2, dffn, sqerr, dln2_g, dln2_b = down_loss(hid, wd_f, r1, ln1_g, ln1_b, ln2_g, ln2_b, target)

    dwd, _ = mm_tn_rows(hid, dffn, "grad_w_down")
    dwd = dwd.reshape(NDEV, FS, D)
    (dgate, dup), (r_wd,) = ffn_bwd_act(dffn, wd_f, gate, up, ScatterPlan([dwd], only=half_a))
    dwgT, (r_wd,) = mm_tn_rows(dgate, x1b, "grad_w_gate", plan=ScatterPlan([dwd], only=half_b, into=[r_wd]))
    dwgT = dwgT.reshape(NDEV, FS, D)
    dwuT, (r_wgT,) = mm_tn_rows(dup, x1b, "grad_w_up", plan=ScatterPlan([dwgT], only=half_a))
    dwuT = dwuT.reshape(NDEV, FS, D)
    (dr1, dmix, dln1_g, dln1_b), (r_wgT, r_wuT) = ffn_bwd_x(
        dgate, dup, wgT, wuT, dr2, r1, ln1_g,
        Plans([ScatterPlan([dwgT], only=half_b, into=[r_wgT]), ScatterPlan([dwuT], only=half_a)]))
    (dYA, dYB, dga, dgb, sga, sgb), (r_wuT,) = merge_bwd(dmix, wo_f, ya, yb, wso_g, wco_g, proj,
                                                         ScatterPlan([dwuT], only=half_b, into=[r_wuT]))
    dwo, _ = mm_tn_rows(merged, dmix, "grad_w_o")
    dwo = dwo.reshape(NDEV, D // NDEV, D)
    (dya, dyb), _ = branches_bwd_x(dYA, dYB, wso_g, wco_g, None)
    dwso = branch_bwd_w(ya, dYA, "grad_w_ssm_out")
    dwco = branch_bwd_w(yb, dYB, "grad_w_conv_out")
    (dyn, dsp, gb, dglu_b), (r_wso,) = glu_bwd(yn, dya, glu_f, glu_b, ScatterPlan([dwso]))
    dglu = mm_tn_rows(gb, dsp, "grad_glu_w")[0].reshape(NDEV, W // NDEV, W)
    (dh, dcg, dbg, dconv, shcb), (r_wco,) = conv_bwd(proj, dyb, conv_f, ScatterPlan([dwco]))
    dwin, (r_wo, r_glu) = grad_w_in_rest(xb, dh, dcg, dbg, dga, dgb, ScatterPlan([dwo, dglu]))
    (du, dwb, dwcT, dlbr_s, dlbi_s, dd, su), (r_win,) = ssm_bwd(
        u_p, dyn, xr_p, xi_p, wbT, wcT, lbr_s, lbi_s, dsk, ScatterPlan([dwin], only=tuple(range(1, NDEV))))

    dbr2, dbi2, dlam_re, dlam_im, dldt, dc_re, dc_im = ssm_param_bwd(
        lam_re, lam_im, ldt, fr, fi, br2, bi2, dwb, dwcT, dlbr_s.reshape(NG, NP), dlbi_s.reshape(NG, NP))
    packed = pack_grads(su, shcb, sga, sgb, dd, dglu_b, dln1_g, dln1_b, dln2_g, dln2_b, dlam_re, dlam_im, dldt, sqerr,
                        dbr2, dbi2, dc_re, dc_im, dconv)
    dwin_u = mm_tn(xb, du, "grad_w_in_u").reshape(NDEV, D // NDEV, W)

    rest = [(dh, 0, 1), (dcg, 0, 2), (dbg, 0, 3), (dga, 0, 4), (dga, 1, 5), (dgb, 0, 6), (dgb, 1, 7)]
    (gx_rest,), (r_win_u, small_all) = in_proj_bwd_x(
        rest, win_g, dr1, ALPHA, "in_proj_bwd_x_rest", Plans([ScatterPlan([dwin_u]), GatherPlan([packed])]))
    my_rows = sum_blocks(r_win_u, "sum_w_in_u")

    out = {}

    def put(name, res, back=lambda a: a[None]):
        out["grad_" + name], out["delta_" + name], out["new_m_" + name], out["new_v_" + name] = [back(r) for r in res]

    res_wd, (win_u_sum,) = adam_update(w_down[0], m_w_down[0], v_w_down[0], r_wd, "adam_w_down", 176,
                                       plan=ScatterPlan([my_rows], only=(0,), whole=True))
    put("w_down", res_wd)
    (grad_x,), _ = in_proj_bwd_x([(du, 0, 0)], win_g, gx_rest, 1.0, "in_proj_bwd_x_u")
    put("w_in", adam_update(w_in[0], m_w_in[0], v_w_in[0], r_win, "adam_w_in", 256,
                            summed_on_0=win_u_sum.reshape(D, W))[0])
    put("glu_w", adam_update(glu_w[0], m_glu_w[0], v_glu_w[0], r_glu, "adam_glu_w")[0])
    put("w_ssm_out", adam_update(w_ssm_out[0], m_w_ssm_out[0], v_w_ssm_out[0], r_wso, "adam_w_ssm_out")[0])
    put("w_conv_out", adam_update(w_conv_out[0], m_w_conv_out[0], v_w_conv_out[0], r_wco, "adam_w_conv_out")[0])
    put("w_o", adam_update(w_o[0], m_w_o[0], v_w_o[0], r_wo, "adam_w_o")[0])
    untr = lambda a: jnp.swapaxes(a, 0, 1)[None]
    put("w_gate", adam_update(tr(w_gate), tr(m_w_gate), tr(v_w_gate), r_wgT, "adam_w_gate", 176)[0], untr)
    put("w_up", adam_update(tr(w_up), tr(m_w_up), tr(v_w_up), r_wuT, "adam_w_up", 176)[0], untr)
    as_c = lambda a: jnp.swapaxes(a, 2, 3)
    params = {n: (given[n], given["m_" + n], given["v_" + n]) for n in list(_SMALL) + ["conv_w"]}
    for n in ("ssm_b_re", "ssm_b_im"):
        params[n] = tuple(as_c(a) for a in params[n])
    small, loss = adam_small(small_all, params)
    for n, res in small.items():
        put(n, res, as_c if n in ("ssm_b_re", "ssm_b_im") else (lambda a: a))

    names = ["w_in", "b_in", "ssm_lambda_re", "ssm_lambda_im", "ssm_log_dt", "ssm_b_re", "ssm_b_im", "ssm_c_re", "ssm_c_im",
             "ssm_d", "glu_w", "glu_b", "w_ssm_out", "conv_w", "w_conv_out", "w_o", "ln1_g", "ln1_b", "w_gate", "w_up",
             "w_down", "ln2_g", "ln2_b"]
    return (loss.reshape(()), grad_x[None], *[out[p + n] for p in ("grad_", "delta_", "new_m_", "new_v_") for n in names])
```

```python
import functools
import math

import jax
import jax.numpy as jnp
from jax import lax
from jax.experimental import pallas as pl
from jax.experimental.pallas import tpu as pltpu

f32, bf16 = jnp.float32, jnp.bfloat16
S = jax.ShapeDtypeStruct
MESH = pl.DeviceIdType.MESH
HIGHEST = lax.Precision.HIGHEST

D = 1024
W = 512
NG, NP, GC = 32, 64, 16
F = 2816
NDEV = 8
FS = F // NDEV
IN_COLS = 8 * W
ALPHA = 2.0 ** 0.25
LN_EPS = 1e-5
ADAM_LR, ADAM_B1, ADAM_B2, ADAM_EPS, ADAM_WD, ADAM_STEP = 0.001, 0.9, 0.999, 1e-08, 0.01, 10
NC = 32
LANE = 128
SW = 4 * LANE
VMEM_LIMIT = 56 * 1024 * 1024
GRAD_DT = bf16
ANY = pl.BlockSpec(memory_space=pl.ANY)


def _cp(sem=None, vmem=None):
    return pltpu.CompilerParams(dimension_semantics=sem, vmem_limit_bytes=vmem)


def _resident(shape):
    return pl.BlockSpec(shape, lambda i: (0,) * len(shape), pipeline_mode=pl.Buffered(1))


def _dot(a, b):
    return jnp.dot(a, b, preferred_element_type=f32)


def _dot_nt(a, b):
    return lax.dot_general(a, b, (((1,), (1,)), ((), ())), preferred_element_type=f32)


def _dot_tn(a, b):
    return lax.dot_general(a, b, (((0,), (0,)), ((), ())), preferred_element_type=f32)


def _eye(n):
    return (lax.broadcasted_iota(jnp.int32, (n, n), 0) == lax.broadcasted_iota(jnp.int32, (n, n), 1)).astype(f32)


def _transpose_exact(a):
    return lax.dot_general(a, _eye(a.shape[0]), (((0,), (0,)), ((), ())), precision=HIGHEST, preferred_element_type=f32)


def _sigmoid(x):
    return 1.0 / (1.0 + jnp.exp(-x))


_GK = math.sqrt(2.0 / math.pi)


def _gelu(x):
    return 0.5 * x * (1.0 + jnp.tanh(_GK * (x + 0.044715 * x * x * x)))


def _gelu_grad(x):
    th = jnp.tanh(_GK * (x + 0.044715 * x * x * x))
    return 0.5 * (1.0 + th) + 0.5 * x * (1.0 - th * th) * _GK * (1.0 + 3.0 * 0.044715 * x * x)


ROW_PART = 256


def _row_parts(tm):
    return [slice(r, r + min(ROW_PART, tm)) for r in range(0, tm, min(ROW_PART, tm))]


def _ln_stats(r):
    mu = jnp.mean(r, axis=-1, keepdims=True)
    xc = r - mu
    var = jnp.mean(xc * xc, axis=-1, keepdims=True)
    rstd = lax.rsqrt(var + LN_EPS)
    return xc * rstd, rstd


def _ln_bwd(dy, xhat, rstd, g):
    dxh = dy * g
    m1 = jnp.mean(dxh, axis=-1, keepdims=True)
    m2 = jnp.mean(dxh * xhat, axis=-1, keepdims=True)
    return rstd * (dxh - m1 - xhat * m2)


def _coords():
    return lax.axis_index("x"), lax.axis_index("y"), lax.axis_index("c")


def _when(cond, fn):
    if cond is True:
        fn()
    else:
        pl.when(cond)(fn)


class GatherPlan:
    aliases = ()

    def __init__(self, arrs, srcs=None, into=None):
        n = self.n = len(arrs)
        self.srcs = srcs
        self.inputs = list(arrs) + list(into or [])
        if into:
            self.aliases = tuple((n + a, a) for a in range(n))
        self.out_shape = [S((NDEV,) + a.shape, a.dtype) for a in arrs]
        self.sems = [pltpu.SemaphoreType.DMA((n, 7)), pltpu.SemaphoreType.DMA((n, 7)), pltpu.SemaphoreType.DMA((n,))]

    def _has(self, dev):
        if self.srcs is None:
            return True
        idx = 4 * dev[0] + 2 * dev[1] + dev[2]
        return functools.reduce(jnp.logical_or, [idx == s for s in self.srcs])

    def _parts(self, ins, outs, sems):
        n = self.n
        send_sems, recv_sems, loc_sems = sems
        x, y, c = _coords()
        me, sib = (x, y, c), (x, y, 1 - c)
        chips = [(1 - x, y), (x, 1 - y), (1 - x, 1 - y)]

        def slot(a, dev):
            return outs[a].at[4 * dev[0] + 2 * dev[1] + dev[2]]

        def copy(a, k, block, to, src=None):
            return pltpu.make_async_remote_copy(
                src_ref=slot(a, block) if src is None else src, dst_ref=slot(a, block),
                send_sem=send_sems.at[a, k], recv_sem=recv_sems.at[a, k], device_id=to, device_id_type=MESH)

        each = [(j, chip, a) for j, chip in enumerate(chips) for a in range(n)]
        own = self._has(me)
        return dict(
            mine=lambda: [(pltpu.make_async_copy(ins[a], slot(a, me), loc_sems.at[a]), own) for a in range(n)],
            first=lambda: ([(copy(a, 0, me, sib, src=ins[a]), own) for a in range(n)]
                           + [(copy(a, 1 + j, me, (*chip, c), src=ins[a]), own) for j, chip, a in each]),
            landed=lambda: [(copy(a, 1 + j, (*chip, c), me), self._has((*chip, c))) for j, chip, a in each],
            passed=lambda: [(copy(a, 4 + j, (*chip, c), sib), self._has((*chip, c))) for j, chip, a in each],
            from_sib=lambda: ([(copy(a, 0, sib, me), self._has(sib)) for a in range(n)]
                              + [(copy(a, 4 + j, (*chip, 1 - c), me), self._has((*chip, 1 - c))) for j, chip, a in each]))

    def start(self, ins, outs, sems):
        p = self._parts(ins, outs, sems)
        for cp, cond in p["mine"]() + p["first"]():
            _when(cond, cp.start)

    def forward(self, ins, outs, sems):
        p = self._parts(ins, outs, sems)
        for (got, cond), (fwd, _) in zip(p["landed"](), p["passed"]()):
            def relay(got=got, fwd=fwd):
                got.wait_recv()
                fwd.start()

            _when(cond, relay)

    def finish(self, ins, outs, sems):
        p = self._parts(ins, outs, sems)
        for cp, cond in p["from_sib"]():
            _when(cond, cp.wait_recv)
        for cp, cond in p["first"]() + p["passed"]():
            _when(cond, cp.wait_send)
        for cp, cond in p["mine"]():
            _when(cond, cp.wait)


class ScatterPlan:
    aliases = ()

    def __init__(self, gs, only=None, into=None, whole=False):
        n = self.n = len(gs)
        self.only = only
        self.whole = whole
        self.inputs = list(gs) + list(into or [])
        if into:
            self.aliases = tuple((n + a, a) for a in range(n))
        self.out_shape = [S((NDEV,) + g.shape if whole else g.shape, g.dtype) for g in gs]
        self.sems = [pltpu.SemaphoreType.DMA((n, 7)), pltpu.SemaphoreType.DMA((n, 7)), pltpu.SemaphoreType.DMA((n,))]

    def _owner(self, idx):
        if self.only is None:
            return True
        return functools.reduce(jnp.logical_or, [idx == b for b in self.only])

    def _copies(self, ins, outs, sems):
        n = self.n
        send_sems, recv_sems, loc_sems = sems
        x, y, c = _coords()
        me = 4 * x + 2 * y + c
        mine = self._owner(me)
        block = (lambda a, k: ins[a]) if self.whole else (lambda a, k: ins[a].at[k])
        copies = [(pltpu.make_async_copy(block(a, me), outs[a].at[me], loc_sems.at[a]), mine, None) for a in range(n)]
        for m in range(1, NDEV):
            px = 1 - x if m & 4 else x
            py = 1 - y if m & 2 else y
            pc = 1 - c if m & 1 else c
            peer = 4 * px + 2 * py + pc
            for a in range(n):
                copies.append((pltpu.make_async_remote_copy(
                    src_ref=block(a, peer), dst_ref=outs[a].at[me],
                    send_sem=send_sems.at[a, m - 1], recv_sem=recv_sems.at[a, m - 1],
                    device_id=(px, py, pc), device_id_type=MESH), self._owner(peer), mine))
        return copies

    def start(self, ins, outs, sems):
        for cp, sends, _ in self._copies(ins, outs, sems):
            _when(sends, cp.start)

    def forward(self, ins, outs, sems):
        pass

    def finish(self, ins, outs, sems):
        for cp, sends, receives in self._copies(ins, outs, sems):
            if receives is None:
                _when(sends, cp.wait)
            else:
                _when(sends, cp.wait_send)
                _when(receives, cp.wait_recv)


class Plans:
    def __init__(self, plans):
        self.plans = plans
        self.inputs = [a for p in plans for a in p.inputs]
        self.out_shape = [s for p in plans for s in p.out_shape]
        self.sems = [s for p in plans for s in p.sems]
        self.aliases, i, o = [], 0, 0
        for p in plans:
            self.aliases += [(i + a, o + b) for a, b in p.aliases]
            i, o = i + len(p.inputs), o + len(p.out_shape)

    def _each(self, what, ins, outs, sems):
        i = o = s = 0
        for p in self.plans:
            ni, no, ns = len(p.inputs), len(p.out_shape), len(p.sems)
            getattr(p, what)(ins[i:i + ni], outs[o:o + no], sems[s:s + ns])
            i, o, s = i + ni, o + no, s + ns

    def start(self, ins, outs, sems):
        self._each("start", ins, outs, sems)

    def forward(self, ins, outs, sems):
        self._each("forward", ins, outs, sems)

    def finish(self, ins, outs, sems):
        self._each("finish", ins, outs, sems)


def _call(body, args, *, name, grid, in_specs, out_specs, out_shape, scratch=(), sem=None, vmem=None, plan=None,
          relay_step=None):
    if plan is None:
        outs = pl.pallas_call(body, name=name, grid=grid, in_specs=list(in_specs), out_specs=list(out_specs),
                              out_shape=list(out_shape), scratch_shapes=list(scratch),
                              compiler_params=_cp(sem, vmem))(*args)
        return list(outs), []
    ni, no, ns = len(in_specs), len(out_specs), len(scratch)
    pi, po = len(plan.inputs), len(plan.out_shape)
    aliases = {ni + a: no + b for a, b in plan.aliases}

    def wrapped(*refs):
        main_in, p_in = refs[:ni], refs[ni:ni + pi]
        main_out, p_out = refs[ni + pi:ni + pi + no], refs[ni + pi + no:ni + pi + no + po]
        main_scr, p_sems = refs[ni + pi + no + po:ni + pi + no + po + ns], refs[ni + pi + no + po + ns:]
        ids = [pl.program_id(d) for d in range(len(grid))]
        first = functools.reduce(jnp.logical_and, [i == 0 for i in ids])
        last = functools.reduce(jnp.logical_and, [i == g - 1 for i, g in zip(ids, grid)])

        @pl.when(first)
        def _():
            plan.start(p_in, p_out, p_sems)

        @pl.when(last if relay_step is None else ids[0] == max(relay_step, 0))
        def _():
            plan.forward(p_in, p_out, p_sems)

        body(*main_in, *main_out, *main_scr)

        @pl.when(last)
        def _():
            plan.finish(p_in, p_out, p_sems)

    outs = pl.pallas_call(
        wrapped, name=name, grid=grid, in_specs=list(in_specs) + [ANY] * pi, out_specs=list(out_specs) + [ANY] * po,
        out_shape=list(out_shape) + list(plan.out_shape), scratch_shapes=list(scratch) + list(plan.sems),
        input_output_aliases=aliases, compiler_params=_cp(("arbitrary",) * len(grid), vmem),
    )(*args, *plan.inputs)
    return list(outs[:no]), list(outs[no:])


def run_plan(plan, name):
    def body(*refs):
        ins, outs, sems = refs[:len(plan.inputs)], refs[len(plan.inputs):len(plan.inputs) + len(plan.out_shape)], \
            refs[len(plan.inputs) + len(plan.out_shape):]
        plan.start(ins, outs, sems)
        plan.forward(ins, outs, sems)
        plan.finish(ins, outs, sems)

    return pl.pallas_call(body, name=name, in_specs=[ANY] * len(plan.inputs), out_specs=[ANY] * len(plan.out_shape),
                          out_shape=list(plan.out_shape), scratch_shapes=list(plan.sems))(*plan.inputs)


def mm_tn(a, b, name, tn=512):
    T, K = a.shape
    N = b.shape[1]
    tn = min(tn, N)

    def body(a_ref, b_ref, o_ref):
        o_ref[...] = _dot_tn(a_ref[...], b_ref[...]).astype(GRAD_DT)

    (out,), _ = _call(body, [a, b], name=name, grid=(N // tn,),
                      in_specs=[_resident((T, K)), pl.BlockSpec((T, tn), lambda j: (0, j))],
                      out_specs=[pl.BlockSpec((None, K, tn), lambda j: (j, 0, 0))],
                      out_shape=[S((N // tn, K, tn), GRAD_DT)], sem=("parallel",), vmem=VMEM_LIMIT)
    return out


def grad_w_in_rest(xb, dh, dcg, dbg, dga, dgb, plan):
    T = xb.shape[0]
    order = ((0, 0), (1, 1), (2, 2), (3, 3), (4, 3), (5, 4), (6, 4))

    def body(x_ref, *refs):
        o_ref = refs[-1]
        j = pl.program_id(0)
        for step, opnd in order:
            @pl.when(j == step)
            def _(opnd=opnd):
                o_ref[...] = _dot_tn(x_ref[...], refs[opnd][...]).astype(GRAD_DT)

    once = lambda: pl.BlockSpec((T, W), lambda j: (0, 0), pipeline_mode=pl.Buffered(1))
    (out,), sent = _call(
        body, [xb, dh, dcg, dbg, dga, dgb], name="grad_w_in_rest", grid=(len(order),),
        in_specs=[_resident((T, D)), once(), once(), once(),
                  pl.BlockSpec((T, W), lambda j: (0, jnp.clip(j - 3, 0, 1))),
                  pl.BlockSpec((T, W), lambda j: (0, jnp.clip(j - 5, 0, 1)))],
        out_specs=[pl.BlockSpec((None, D, W), lambda j: (1 + j, 0, 0))],
        out_shape=[S((NDEV, D, W), GRAD_DT)], sem=("arbitrary",), vmem=VMEM_LIMIT, plan=plan)
    return out, sent


def mm_tn_rows(a, b, name, tk=256, plan=None):
    T, K = a.shape
    N = b.shape[1]
    tk = min(tk, K)

    def body(a_ref, b_ref, o_ref):
        o_ref[...] = _dot_tn(a_ref[...], b_ref[...]).astype(GRAD_DT)

    (out,), sent = _call(body, [a, b], name=name, grid=(K // tk,),
                         in_specs=[pl.BlockSpec((T, tk), lambda i: (0, i)), _resident((T, N))],
                         out_specs=[pl.BlockSpec((tk, N), lambda i: (i, 0))], out_shape=[S((K, N), GRAD_DT)],
                         sem=("parallel",), vmem=VMEM_LIMIT, plan=plan)
    return out, sent


def prep_weights(ws):
    def body(*refs):
        for i in range(len(ws)):
            refs[len(ws) + i][...] = refs[i][...].astype(bf16)

    return pl.pallas_call(body, name="prep_weights", out_shape=[S(w.shape, bf16) for w in ws],
                          compiler_params=_cp(None, VMEM_LIMIT))(*ws)


REST_BLOCKS = (4, 5, 6, 7, 1, 2, 3)
REST_COLS = len(REST_BLOCKS) * W


def in_proj_u(x, win_g, b_in):
    T = x.shape[0]
    tm = min(1024, T)

    def body(x_ref, w_ref, b_ref, u_ref, xb_ref):
        xb = x_ref[...].astype(bf16)
        xb_ref[...] = xb
        u_ref[...] = _dot(xb, w_ref[...]) + b_ref[...]

    row = pl.BlockSpec((tm, D), lambda i: (i, 0))
    return pl.pallas_call(
        body, name="in_proj_u", grid=(T // tm,),
        in_specs=[row, pl.BlockSpec((None, D, W), lambda i: (0, 0, 0)), pl.BlockSpec((1, W), lambda i: (0, 0))],
        out_specs=[pl.BlockSpec((tm, W), lambda i: (i, 0)), row],
        out_shape=[S((T, W), f32), S((T, D), bf16)], compiler_params=_cp(("parallel",), VMEM_LIMIT),
    )(x, win_g, b_in)


def in_proj_rest(xb, win_g, b_in, plan):
    T = xb.shape[0]
    tm = min(512, T)

    def body(x_ref, w_ref, b_ref, o_ref):
        xb_ = x_ref[...]
        for i, k in enumerate(REST_BLOCKS):
            o_ref[:, i * W:(i + 1) * W] = _dot(xb_, w_ref[k]) + b_ref[:, k * W:(k + 1) * W]

    return _call(
        body, [xb, win_g, b_in], name="in_proj_rest", grid=(T // tm,),
        in_specs=[pl.BlockSpec((tm, D), lambda i: (i, 0)), _resident((NDEV, D, W)), _resident((1, IN_COLS))],
        out_specs=[pl.BlockSpec((tm, REST_COLS), lambda i: (i, 0))],
        out_shape=[S((T, REST_COLS), f32)], vmem=VMEM_LIMIT, plan=plan, relay_step=T // tm - 2)


def _to_scan_order(a_ref, o_ref):
    L = a_ref.shape[0] // NC

    def step(jb, carry):
        j0 = pl.multiple_of(jb * 8, 8)
        for q in range(NC // 8):
            x = jnp.stack([a_ref[pl.ds((8 * q + c) * L + j0, 8), :] for c in range(8)], axis=0)
            y = jnp.swapaxes(x, 0, 1)
            for j in range(8):
                o_ref[pl.ds((j0 + j) * NC + 8 * q, 8), :] = y[j]
        return carry

    lax.fori_loop(0, L // 8, step, 0)


def _to_time_order(a_ref, o_ref):
    L = a_ref.shape[0] // NC

    def step(jb, carry):
        j0 = pl.multiple_of(jb * 16, 16)
        for q in range(NC // 8):
            halves = []
            for h in range(2):
                x = jnp.stack([a_ref[pl.ds((j0 + 8 * h + j) * NC + 8 * q, 8), :] for j in range(8)], axis=0)
                halves.append(jnp.swapaxes(x, 0, 1))
            for c in range(8):
                o_ref[pl.ds((8 * q + c) * L + j0, 16), :] = jnp.concatenate(
                    [halves[0][c], halves[1][c]], axis=0).astype(o_ref.dtype)
        return carry

    lax.fori_loop(0, L // 16, step, 0)


def _disc(lr, li, ldt):
    dt = jnp.exp(ldt)
    mag = jnp.exp(lr * dt)
    lbr = mag * jnp.cos(li * dt)
    lbi = mag * jnp.sin(li * dt)
    den = lr * lr + li * li
    nr = lbr - 1.0
    return lbr, lbi, (nr * lr + lbi * li) / den, (lbi * lr - nr * li) / den


def _per_channel(f):
    return jnp.broadcast_to(f[:, None, :], (NG, GC, NP)).reshape(NG * GC, NP)


def ssm_params(lam_re, lam_im, log_dt, br, bi):
    def body(lr_ref, li_ref, ldt_ref, br_ref, bi_ref, lbr_ref, lbi_ref, fr_ref, fi_ref, bbr_ref, bbi_ref):
        lbr, lbi, fr, fi = _disc(lr_ref[...], li_ref[...], ldt_ref[...])
        lbr_ref[...], lbi_ref[...], fr_ref[...], fi_ref[...] = lbr, lbi, fr, fi
        fr_, fi_, br_, bi_ = _per_channel(fr), _per_channel(fi), br_ref[...], bi_ref[...]
        bbr_ref[...] = fr_ * br_ - fi_ * bi_
        bbi_ref[...] = fr_ * bi_ + fi_ * br_

    return pl.pallas_call(body, name="ssm_params", out_shape=[S((NG, NP), f32)] * 4 + [S((NG * GC, NP), f32)] * 2)(
        lam_re, lam_im, log_dt, br, bi)


SCAN_UNROLL = 4
SCAN_LANES = 2 * LANE


def _steps(n, body, carry):
    main = n // SCAN_UNROLL

    def trip(t, c):
        for q in range(SCAN_UNROLL):
            c = body(t * SCAN_UNROLL + q, c)
        return c

    carry = lax.fori_loop(0, main, trip, carry)
    for i in range(main * SCAN_UNROLL, n):
        carry = body(i, carry)
    return carry


def _scan_body(T):
    L = T // NC
    RB = min(512, T)
    nsq = int(round(math.log2(L)))
    assert 2 ** nsq == L and T % RB == 0 and L % 16 == 0

    def rows(i):
        return pl.ds(pl.multiple_of(i * RB, RB), RB)

    def tile(j):
        return pl.ds(j * NC if isinstance(j, int) else pl.multiple_of(j * NC, NC), NC)

    def forward_states(u_ref, wb_ref, lbr_ref, lbi_ref, sre, sim, ere, eim):
        def bproj(i, carry):
            bu = _dot(u_ref[rows(i), :].astype(bf16), wb_ref[...])
            sre[rows(i), :] = bu[:, :SW]
            sim[rows(i), :] = bu[:, SW:]
            return carry

        lax.fori_loop(0, T // RB, bproj, 0)
        for lb in range(SW // SCAN_LANES):
            ls = slice(lb * SCAN_LANES, (lb + 1) * SCAN_LANES)
            ar = jnp.broadcast_to(lbr_ref[:, ls], (NC, SCAN_LANES))
            ai = jnp.broadcast_to(lbi_ref[:, ls], (NC, SCAN_LANES))

            def step(j, carry):
                xr, xi = carry
                nr = ar * xr - ai * xi + sre[tile(j), ls]
                ni = ar * xi + ai * xr + sim[tile(j), ls]
                sre[tile(j), ls] = nr
                sim[tile(j), ls] = ni
                return nr, ni

            zero = jnp.zeros((NC, SCAN_LANES), f32)
            _steps(L, step, (zero, zero))
            pr, pi = lbr_ref[:, ls], lbi_ref[:, ls]
            for _ in range(nsq):
                pr, pi = pr * pr - pi * pi, 2.0 * pr * pi
            er = jnp.zeros((1, SCAN_LANES), f32)
            ei = er
            ere[0:1, ls] = er
            eim[0:1, ls] = ei
            base = (L - 1) * NC
            for c in range(1, NC):
                lr_ = sre[base + c - 1:base + c, ls]
                li_ = sim[base + c - 1:base + c, ls]
                er, ei = lr_ + pr * er - pi * ei, li_ + pr * ei + pi * er
                ere[c:c + 1, ls] = er
                eim[c:c + 1, ls] = ei
            e_r, e_i = ere[:, ls].reshape(NC // 8, 8, SCAN_LANES), eim[:, ls].reshape(NC // 8, 8, SCAN_LANES)
            ar8, ai8 = ar[0:8], ai[0:8]

            def fix(j, carry):
                pwr, pwi = carry
                xr = sre[tile(j), ls].reshape(NC // 8, 8, SCAN_LANES) + (pwr * e_r - pwi * e_i)
                xi = sim[tile(j), ls].reshape(NC // 8, 8, SCAN_LANES) + (pwr * e_i + pwi * e_r)
                sre[tile(j), ls] = xr.reshape(NC, SCAN_LANES)
                sim[tile(j), ls] = xi.reshape(NC, SCAN_LANES)
                return pwr * ar8 - pwi * ai8, pwr * ai8 + pwi * ar8

            _steps(L, fix, (ar8, ai8))

    return L, RB, nsq, rows, tile, forward_states


def ssm_fwd(u, wb, wc, lbr, lbi, dsk, plan):
    T = u.shape[0]
    L, RB, nsq, rows, tile, forward_states = _scan_body(T)
    nslab = W // LANE

    def body(u_ref, wb_ref, wc_ref, lbr_ref, lbi_ref, d_ref, y_ref, up_ref, xr_ref, xi_ref, sre, sim, ere, eim, yp):
        _to_scan_order(u_ref, up_ref)
        forward_states(up_ref, wb_ref, lbr_ref, lbi_ref, sre, sim, ere, eim)

        def cproj(i, carry):
            xr, xi = sre[rows(i), :].astype(bf16), sim[rows(i), :].astype(bf16)
            xr_ref[rows(i), :] = xr
            xi_ref[rows(i), :] = xi
            y = _dot(xr, wc_ref[0:SW, :]) + _dot(xi, wc_ref[SW:, :])
            yp[rows(i), :] = y + d_ref[...] * up_ref[rows(i), :]
            return carry

        lax.fori_loop(0, T // RB, cproj, 0)
        _to_time_order(yp, y_ref)

    slab = pl.BlockSpec((T, LANE), lambda k: (0, k))
    states = pl.BlockSpec((T, SW), lambda k: (0, k))
    return _call(
        body, [u, wb, wc, lbr, lbi, dsk], name="ssm_fwd", grid=(nslab,),
        in_specs=[slab, pl.BlockSpec((None, LANE, 2 * SW), lambda k: (k, 0, 0)),
                  pl.BlockSpec((None, 2 * SW, LANE), lambda k: (k, 0, 0)),
                  pl.BlockSpec((None, 1, SW), lambda k: (k, 0, 0)), pl.BlockSpec((None, 1, SW), lambda k: (k, 0, 0)),
                  pl.BlockSpec((None, 1, LANE), lambda k: (k, 0, 0))],
        out_specs=[slab, slab, states, states],
        out_shape=[S((T, W), f32), S((T, W), f32), S((T, nslab * SW), bf16), S((T, nslab * SW), bf16)],
        scratch=[pltpu.VMEM((T, SW), f32), pltpu.VMEM((T, SW), f32), pltpu.VMEM((NC, SW), f32), pltpu.VMEM((NC, SW), f32),
                 pltpu.VMEM((T, LANE), f32)],
        vmem=VMEM_LIMIT, plan=plan)


def ssm_bwd(u_p, dy, xr, xi, wbT, wcT, lbr, lbi, dsk, plan):
    T = u_p.shape[0]
    L, RB, nsq, rows, tile, _ = _scan_body(T)

    def body(u_ref, dyt_ref, sre, sim, wbT_ref, wcT_ref, lbr_ref, lbi_ref, d_ref,
             dut_ref, dwb_ref, dwc_ref, dlr_ref, dli_ref, dd_ref, su_ref, gre, gim, ere, eim, dy_ref, du_ref):
        _to_scan_order(dyt_ref, dy_ref)

        def dstate(i, carry):
            g = _dot(dy_ref[rows(i), :].astype(bf16), wcT_ref[...])
            gre[rows(i), :] = g[:, :SW]
            gim[rows(i), :] = g[:, SW:]
            return carry

        lax.fori_loop(0, T // RB, dstate, 0)
        row = lax.broadcasted_iota(jnp.int32, (NC, SCAN_LANES), 0)
        for lb in range(SW // SCAN_LANES):
            ls = slice(lb * SCAN_LANES, (lb + 1) * SCAN_LANES)
            ar = jnp.broadcast_to(lbr_ref[:, ls], (NC, SCAN_LANES))
            ai = jnp.broadcast_to(lbi_ref[:, ls], (NC, SCAN_LANES))

            def step(i, carry):
                gr, gi = carry
                j = L - 1 - i
                nr = ar * gr + ai * gi + gre[tile(j), ls]
                ni = ar * gi - ai * gr + gim[tile(j), ls]
                gre[tile(j), ls] = nr
                gim[tile(j), ls] = ni
                return nr, ni

            zero = jnp.zeros((NC, SCAN_LANES), f32)
            _steps(L, step, (zero, zero))
            pr, pi = lbr_ref[:, ls], -lbi_ref[:, ls]
            for _ in range(nsq):
                pr, pi = pr * pr - pi * pi, 2.0 * pr * pi
            er = jnp.zeros((1, SCAN_LANES), f32)
            ei = er
            ere[NC - 1:NC, ls] = er
            eim[NC - 1:NC, ls] = ei
            for c in range(NC - 2, -1, -1):
                lr_ = gre[c + 1:c + 2, ls]
                li_ = gim[c + 1:c + 2, ls]
                er, ei = lr_ + pr * er - pi * ei, li_ + pr * ei + pi * er
                ere[c:c + 1, ls] = er
                eim[c:c + 1, ls] = ei
            e_r, e_i = ere[:, ls].reshape(NC // 8, 8, SCAN_LANES), eim[:, ls].reshape(NC // 8, 8, SCAN_LANES)
            ar8, ai8 = ar[0:8], ai[0:8]

            def fixed(j, pwr, pwi):
                gr = (gre[tile(j), ls].reshape(NC // 8, 8, SCAN_LANES) + (pwr * e_r - pwi * e_i)).reshape(NC, SCAN_LANES)
                gi = (gim[tile(j), ls].reshape(NC // 8, 8, SCAN_LANES) + (pwr * e_i + pwi * e_r)).reshape(NC, SCAN_LANES)
                gre[tile(j), ls] = gr
                gim[tile(j), ls] = gi
                return gr, gi

            def fix(i, carry):
                pwr, pwi, accr, acci = carry
                j = L - 1 - i
                gr, gi = fixed(j, pwr, pwi)
                xr, xi = sre[tile(j - 1), ls].astype(f32), sim[tile(j - 1), ls].astype(f32)
                return (pwr * ar8 + pwi * ai8, pwi * ar8 - pwr * ai8,
                        accr + gr * xr + gi * xi, acci + gi * xr - gr * xi)

            pwr, pwi, accr, acci = _steps(L - 1, fix, (ar8, -ai8, zero, zero))
            gr, gi = fixed(0, pwr, pwi)
            xr = jnp.where(row == 0, 0.0, pltpu.roll(sre[tile(L - 1), ls].astype(f32), 1, axis=0))
            xi = jnp.where(row == 0, 0.0, pltpu.roll(sim[tile(L - 1), ls].astype(f32), 1, axis=0))
            accr = accr + gr * xr + gi * xi
            acci = acci + gi * xr - gr * xi
            dlr_ref[:, ls] = jnp.sum(accr, axis=0, keepdims=True)
            dli_ref[:, ls] = jnp.sum(acci, axis=0, keepdims=True)

        dwb_ref[...] = jnp.zeros_like(dwb_ref)
        dwc_ref[...] = jnp.zeros_like(dwc_ref)
        dd_ref[...] = jnp.zeros_like(dd_ref)
        su_ref[...] = jnp.zeros_like(su_ref)

        def finish(i, carry):
            u32, dy32 = u_ref[rows(i), :], dy_ref[rows(i), :]
            ub, dyb = u32.astype(bf16), dy32.astype(bf16)
            gr, gi = gre[rows(i), :].astype(bf16), gim[rows(i), :].astype(bf16)
            du = _dot(gr, wbT_ref[0:SW, :]) + _dot(gi, wbT_ref[SW:, :]) + dy32 * d_ref[...]
            du_ref[rows(i), :] = du
            su_ref[...] += jnp.sum(du, axis=0, keepdims=True)
            dwb_ref[:, 0:SW] += _dot_tn(ub, gr)
            dwb_ref[:, SW:] += _dot_tn(ub, gi)
            dwc_ref[:, 0:SW] += _dot_tn(dyb, sre[rows(i), :])
            dwc_ref[:, SW:] += _dot_tn(dyb, sim[rows(i), :])
            dd_ref[...] += jnp.sum(dy32 * u32, axis=0, keepdims=True)
            return carry

        lax.fori_loop(0, T // RB, finish, 0)
        _to_time_order(du_ref, dut_ref)

    slab = pl.BlockSpec((T, LANE), lambda k: (0, k))
    wide = pl.BlockSpec((None, LANE, 2 * SW), lambda k: (k, 0, 0))
    tall = pl.BlockSpec((None, 2 * SW, LANE), lambda k: (k, 0, 0))
    vec = pl.BlockSpec((None, 1, SW), lambda k: (k, 0, 0))
    vecd = pl.BlockSpec((None, 1, LANE), lambda k: (k, 0, 0))
    states = pl.BlockSpec((T, SW), lambda k: (0, k))
    nslab = W // LANE
    return _call(
        body, [u_p, dy, xr, xi, wbT, wcT, lbr, lbi, dsk], name="ssm_bwd", grid=(nslab,),
        in_specs=[slab, slab, states, states, tall, wide, vec, vec, vecd],
        out_specs=[slab, wide, wide, vec, vec, vecd, vecd],
        out_shape=[S((T, W), bf16), S((nslab, LANE, 2 * SW), f32), S((nslab, LANE, 2 * SW), f32),
                   S((nslab, 1, SW), f32), S((nslab, 1, SW), f32), S((nslab, 1, LANE), f32), S((nslab, 1, LANE), f32)],
        scratch=[pltpu.VMEM((T, SW), f32)] * 2 + [pltpu.VMEM((NC, SW), f32)] * 2 + [pltpu.VMEM((T, LANE), f32)] * 2,
        vmem=VMEM_LIMIT, plan=plan)


def glu_fwd(yn, glu_w, glu_b):
    T = yn.shape[0]
    tm = min(512, T)

    def body(y_ref, w_ref, b_ref, o_ref):
        g = _gelu(y_ref[...])
        o_ref[...] = (g * _sigmoid(_dot(g.astype(bf16), w_ref[...]) + b_ref[...])).astype(bf16)

    return pl.pallas_call(
        body, name="glu_fwd", grid=(T // tm,),
        in_specs=[pl.BlockSpec((tm, W), lambda i: (i, 0)), pl.BlockSpec((W, W), lambda i: (0, 0)), pl.BlockSpec((1, W), lambda i: (0, 0))],
        out_specs=pl.BlockSpec((tm, W), lambda i: (i, 0)), out_shape=S((T, W), bf16), compiler_params=_cp(("parallel",)),
    )(yn, glu_w, glu_b)


def _shift_rows(cur, prev8, k):
    return pltpu.roll(jnp.concatenate([prev8, cur], axis=0), k, axis=0)[8:]


def _lift_rows(cur, next8, k):
    n = cur.shape[0]
    return pltpu.roll(jnp.concatenate([cur, next8], axis=0), n + 8 - k, axis=0)[:n]


def conv_fwd(proj, conv_w):
    T = proj.shape[0]
    RB = min(512, T)

    def body(h_ref, c_ref, b_ref, w_ref, o_ref):
        w0, w1, w2 = w_ref[0:1, :], w_ref[1:2, :], w_ref[2:3, :]

        def blk(i, carry):
            r0 = pl.multiple_of(i * RB, RB)
            rs = pl.ds(r0, RB)
            ch = c_ref[rs, :] * h_ref[rs, :]
            pr = pl.ds(jnp.maximum(r0 - 8, 0), 8)
            prev = jnp.where(i > 0, c_ref[pr, :] * h_ref[pr, :], 0.0)
            z = w2 * ch + w1 * _shift_rows(ch, prev, 1) + w0 * _shift_rows(ch, prev, 2)
            o_ref[rs, :] = (b_ref[rs, :] * z).astype(bf16)
            return carry

        lax.fori_loop(0, T // RB, blk, 0)

    nb = W // LANE
    return pl.pallas_call(
        body, name="conv_fwd", grid=(nb,),
        in_specs=[pl.BlockSpec((T, LANE), lambda k: (0, 4 * nb + k)), pl.BlockSpec((T, LANE), lambda k: (0, 5 * nb + k)),
                  pl.BlockSpec((T, LANE), lambda k: (0, 6 * nb + k)),pl.BlockSpec((3, LANE), lambda k: (0, k))],
        out_specs=pl.BlockSpec((T, LANE), lambda k: (0, k)), out_shape=S((T, W), bf16),
        compiler_params=_cp(("parallel",), VMEM_LIMIT),
    )(proj, proj, proj, conv_w)


def _dense_columns(blocks_ref, dense_ref):
    for k in range(NDEV):
        dense_ref[:, k * LANE:(k + 1) * LANE] = blocks_ref[k]


def merge_fwd(ya, yb, wso, wco, proj, plan):
    T = ya.shape[0]
    tm = min(1024, T)

    def body(ya_ref, yb_ref, wa_ref, wb_ref, ga_ref, gb_ref, o_ref, wa_s, wb_s):
        @pl.when(pl.program_id(0) == 0)
        def _():
            _dense_columns(wa_ref, wa_s)
            _dense_columns(wb_ref, wb_s)

        o_ref[...] = (_sigmoid(ga_ref[...]) * _dot(ya_ref[...], wa_s[...])
                      + _sigmoid(gb_ref[...]) * _dot(yb_ref[...], wb_s[...])).astype(bf16)

    act = pl.BlockSpec((tm, W), lambda i: (i, 0))
    return _call(
        body, [ya, yb, wso, wco, proj, proj], name="merge_fwd", grid=(T // tm,),
        in_specs=[act, act, _resident((NDEV, W, LANE)), _resident((NDEV, W, LANE)),
                  pl.BlockSpec((tm, D), lambda i: (i, 0)), pl.BlockSpec((tm, D), lambda i: (i, 1))],
        out_specs=[pl.BlockSpec((tm, D), lambda i: (i, 0))], out_shape=[S((T, D), bf16)],
        scratch=[pltpu.VMEM((W, D), bf16), pltpu.VMEM((W, D), bf16)], vmem=VMEM_LIMIT, plan=plan)


def mix_ln1(merged, w_o, x, g1, b1, plan):
    T = x.shape[0]
    tm = min(512, T)

    def body(m_ref, w_ref, x_ref, g_ref, b_ref, r_ref, x1_ref):
        for rs in _row_parts(tm):
            r = ALPHA * x_ref[rs, :] + _dot(m_ref[rs, :], w_ref[...])
            r_ref[rs, :] = r
            xhat, _ = _ln_stats(r)
            x1_ref[rs, :] = (xhat * g_ref[...] + b_ref[...]).astype(bf16)

    row = pl.BlockSpec((tm, D), lambda i: (i, 0))
    vec = pl.BlockSpec((1, D), lambda i: (0, 0))
    return _call(
        body, [merged, w_o, x, g1, b1], name="mix_ln1", grid=(T // tm,),
        in_specs=[row, _resident((D, D)), row, vec, vec],
        out_specs=[row, row], out_shape=[S((T, D), f32), S((T, D), bf16)], sem=("parallel",), vmem=VMEM_LIMIT, plan=plan,
        relay_step=T // tm - 2)


FT = 256


def gate_up(x1b, wgT, wuT, plan):
    T = x1b.shape[0]
    tm = min(512, T)

    def body(x_ref, wg_ref, wu_ref, g_ref, u_ref, h_ref):
        x = x_ref[...]
        for n in range(F // FT):
            cs = slice(n * FT, (n + 1) * FT)
            g = _dot_nt(x, wg_ref[cs, :])
            u = _dot_nt(x, wu_ref[cs, :])
            g_ref[:, cs] = g.astype(bf16)
            u_ref[:, cs] = u.astype(bf16)
            h_ref[:, cs] = (g * _sigmoid(g) * u).astype(bf16)

    osp = pl.BlockSpec((tm, F), lambda i: (i, 0))
    return _call(
        body, [x1b, wgT, wuT], name="gate_up", grid=(T // tm,),
        in_specs=[pl.BlockSpec((tm, D), lambda i: (i, 0)), _resident((F, D)), _resident((F, D))],
        out_specs=[osp, osp, osp], out_shape=[S((T, F), bf16)] * 3, vmem=VMEM_LIMIT, plan=plan, relay_step=T // tm - 3)


def down_loss(hid, w_down, r1, g1, b1, g2, b2, target):
    T = hid.shape[0]
    tm = min(512, T)

    def body(h_ref, w_ref, r1_ref, g1_ref, b1_ref, g2_ref, b2_ref, t_ref, dr_ref, drb_ref, loss_ref, dg_ref, db_ref):
        @pl.when(pl.program_id(0) == 0)
        def _():
            loss_ref[...] = jnp.zeros_like(loss_ref)
            dg_ref[...] = jnp.zeros_like(dg_ref)
            db_ref[...] = jnp.zeros_like(db_ref)

        for rs in _row_parts(tm):
            xh1, _ = _ln_stats(r1_ref[rs, :])
            x1 = xh1 * g1_ref[...] + b1_ref[...]
            r2 = ALPHA * x1 + _dot(h_ref[rs, :], w_ref[...])
            xh2, rstd2 = _ln_stats(r2)
            err = xh2 * g2_ref[...] + b2_ref[...] - t_ref[rs, :]
            loss_ref[...] += jnp.sum(jnp.mean(err * err, axis=-1, keepdims=True), axis=0, keepdims=True)
            dy = err * (1.0 / D)
            dg_ref[...] += jnp.sum(dy * xh2, axis=0, keepdims=True)
            db_ref[...] += jnp.sum(dy, axis=0, keepdims=True)
            dr = _ln_bwd(dy, xh2, rstd2, g2_ref[...])
            dr_ref[rs, :] = dr
            drb_ref[rs, :] = dr.astype(bf16)

    row = pl.BlockSpec((tm, D), lambda i: (i, 0))
    vec = pl.BlockSpec((1, D), lambda i: (0, 0))
    return pl.pallas_call(
        body, name="down_loss", grid=(T // tm,),
        in_specs=[pl.BlockSpec((tm, F), lambda i: (i, 0)), _resident((F, D)), row, vec, vec, vec, vec, row],
        out_specs=[row, row, pl.BlockSpec((1, 1), lambda i: (0, 0)), vec, vec],
        out_shape=[S((T, D), f32), S((T, D), bf16), S((1, 1), f32), S((1, D), f32), S((1, D), f32)],
        compiler_params=_cp(("arbitrary",), VMEM_LIMIT),
    )(hid, w_down, r1, g1, b1, g2, b2, target)


def ffn_bwd_act(dffn, w_down, gate, up, plan):
    T = dffn.shape[0]
    tm = min(512, T)

    def body(d_ref, w_ref, g_ref, u_ref, dg_ref, du_ref):
        for n in range(F // FT):
            cs = slice(n * FT, (n + 1) * FT)
            for rs in _row_parts(tm):
                dh = _dot_nt(d_ref[rs, :], w_ref[cs, :])
                g, u = g_ref[rs, cs].astype(f32), u_ref[rs, cs].astype(f32)
                sg = _sigmoid(g)
                t = g * sg
                du_ref[rs, cs] = (dh * t).astype(bf16)
                dg_ref[rs, cs] = (dh * u * (sg + t - t * sg)).astype(bf16)

    osp = pl.BlockSpec((tm, F), lambda i: (i, 0))
    return _call(
        body, [dffn, w_down, gate, up], name="ffn_bwd_act", grid=(T // tm,),
        in_specs=[pl.BlockSpec((tm, D), lambda i: (i, 0)), _resident((F, D)), osp, osp],
        out_specs=[osp, osp], out_shape=[S((T, F), bf16)] * 2, sem=("parallel",), vmem=VMEM_LIMIT, plan=plan)


def ffn_bwd_x(dgate, dup, wgT, wuT, dr2, r1, g1, plan):
    T = dr2.shape[0]
    tm = min(512, T)

    def body(dg_ref, du_ref, wg_ref, wu_ref, dr2_ref, r1_ref, g1_ref, dr_ref, drb_ref, dgam_ref, dbet_ref):
        @pl.when(pl.program_id(0) == 0)
        def _():
            dgam_ref[...] = jnp.zeros_like(dgam_ref)
            dbet_ref[...] = jnp.zeros_like(dbet_ref)

        for rs in _row_parts(tm):
            dx1 = ALPHA * dr2_ref[rs, :] + _dot(dg_ref[rs, :], wg_ref[...]) + _dot(du_ref[rs, :], wu_ref[...])
            xh, rstd = _ln_stats(r1_ref[rs, :])
            dgam_ref[...] += jnp.sum(dx1 * xh, axis=0, keepdims=True)
            dbet_ref[...] += jnp.sum(dx1, axis=0, keepdims=True)
            dr = _ln_bwd(dx1, xh, rstd, g1_ref[...])
            dr_ref[rs, :] = dr
            drb_ref[rs, :] = dr.astype(bf16)

    row = pl.BlockSpec((tm, D), lambda i: (i, 0))
    wide = pl.BlockSpec((tm, F), lambda i: (i, 0))
    wsp = _resident((F, D))
    vec = pl.BlockSpec((1, D), lambda i: (0, 0))
    return _call(
        body, [dgate, dup, wgT, wuT, dr2, r1, g1], name="ffn_bwd_x", grid=(T // tm,),
        in_specs=[wide, wide, wsp, wsp, row, row, vec],
        out_specs=[row, row, vec, vec], out_shape=[S((T, D), f32), S((T, D), bf16), S((1, D), f32), S((1, D), f32)],
        vmem=VMEM_LIMIT, plan=plan)


def merge_bwd(dmix, w_o, ya, yb, wso, wco, proj, plan):
    T = dmix.shape[0]
    tm = min(512, T)

    def body(dm_ref, wo_ref, ya_ref, yb_ref, wa_ref, wb_ref, ga_ref, gb_ref, dya_ref, dyb_ref, dga_ref, dgb_ref, sa_ref, sb_ref,
             wa_s, wb_s):
        @pl.when(pl.program_id(0) == 0)
        def _():
            _dense_columns(wa_ref, wa_s)
            _dense_columns(wb_ref, wb_s)

        dmer = _dot_nt(dm_ref[...], wo_ref[...])
        sa, sb = _sigmoid(ga_ref[...]), _sigmoid(gb_ref[...])
        dya_ref[...] = (dmer * sa).astype(bf16)
        dyb_ref[...] = (dmer * sb).astype(bf16)
        dga = dmer * _dot(ya_ref[...], wa_s[...]) * sa * (1.0 - sa)
        dgb = dmer * _dot(yb_ref[...], wb_s[...]) * sb * (1.0 - sb)
        dga_ref[...] = dga.astype(bf16)
        dgb_ref[...] = dgb.astype(bf16)
        sa_ref[...] = jnp.sum(dga, axis=0, keepdims=True)
        sb_ref[...] = jnp.sum(dgb, axis=0, keepdims=True)

    act = pl.BlockSpec((tm, W), lambda i: (i, 0))
    osp = pl.BlockSpec((tm, D), lambda i: (i, 0))
    ssp = pl.BlockSpec((None, 1, D), lambda i: (i, 0, 0))
    return _call(
        body, [dmix, w_o, ya, yb, wso, wco, proj, proj], name="merge_bwd", grid=(T // tm,),
        in_specs=[osp, _resident((D, D)), act, act, _resident((NDEV, W, LANE)), _resident((NDEV, W, LANE)),
                  pl.BlockSpec((tm, D), lambda i: (i, 0)), pl.BlockSpec((tm, D), lambda i: (i, 1))],
        out_specs=[osp, osp, osp, osp, ssp, ssp],
        out_shape=[S((T, D), bf16)] * 4 + [S((T // tm, 1, D), f32)] * 2,
        scratch=[pltpu.VMEM((W, D), bf16), pltpu.VMEM((W, D), bf16)], vmem=VMEM_LIMIT, plan=plan)


def branches_bwd(dYA, dYB, ya, yb, wso, wco):
    T = dYA.shape[0]
    tm = min(1024, T)

    def body(da_ref, db_ref, ya_ref, yb_ref, wa_ref, wb_ref, oa_ref, ob_ref, ga_ref, gb_ref, wa_s, wb_s, acc_a, acc_b):
        @pl.when(pl.program_id(0) == 0)
        def _():
            _dense_columns(wa_ref, wa_s)
            _dense_columns(wb_ref, wb_s)
            acc_a[...] = jnp.zeros_like(acc_a)
            acc_b[...] = jnp.zeros_like(acc_b)

        oa_ref[...] = _dot_nt(da_ref[...], wa_s[...])
        ob_ref[...] = _dot_nt(db_ref[...], wb_s[...])
        acc_a[...] += _dot_tn(ya_ref[...], da_ref[...])
        acc_b[...] += _dot_tn(yb_ref[...], db_ref[...])

        @pl.when(pl.program_id(0) == pl.num_programs(0) - 1)
        def _():
            for k in range(NDEV):
                ga_ref[k] = acc_a[:, k * LANE:(k + 1) * LANE].astype(GRAD_DT)
                gb_ref[k] = acc_b[:, k * LANE:(k + 1) * LANE].astype(GRAD_DT)

    row = pl.BlockSpec((tm, D), lambda i: (i, 0))
    osp = pl.BlockSpec((tm, W), lambda i: (i, 0))
    blocks = pl.BlockSpec((NDEV, W, LANE), lambda i: (0, 0, 0))
    outs, _ = _call(
        body, [dYA, dYB, ya, yb, wso, wco], name="branches_bwd", grid=(T // tm,),
        in_specs=[row, row, osp, osp, _resident((NDEV, W, LANE)), _resident((NDEV, W, LANE))],
        out_specs=[osp, osp, blocks, blocks], out_shape=[S((T, W), f32)] * 2 + [S((NDEV, W, LANE), GRAD_DT)] * 2,
        scratch=[pltpu.VMEM((W, D), bf16)] * 2 + [pltpu.VMEM((W, D), f32)] * 2, sem=("arbitrary",), vmem=VMEM_LIMIT)
    return outs


def glu_bwd(yn, dya, glu_w, glu_b, plan):
    T = yn.shape[0]
    tm = min(512, T)

    def body(y_ref, d_ref, w_ref, b_ref, dy_ref, dsp_ref, g_ref, db_ref):
        @pl.when(pl.program_id(0) == 0)
        def _():
            db_ref[...] = jnp.zeros_like(db_ref)

        y, dya_ = y_ref[...], d_ref[...]
        g = _gelu(y)
        gb = g.astype(bf16)
        s = _sigmoid(_dot(gb, w_ref[...]) + b_ref[...])
        dsp = dya_ * g * s * (1.0 - s)
        dspb = dsp.astype(bf16)
        dg = dya_ * s + _dot_nt(dspb, w_ref[...])
        dy_ref[...] = dg * _gelu_grad(y)
        dsp_ref[...] = dspb
        g_ref[...] = gb
        db_ref[...] += jnp.sum(dsp, axis=0, keepdims=True)

    row = pl.BlockSpec((tm, W), lambda i: (i, 0))
    vec = pl.BlockSpec((1, W), lambda i: (0, 0))
    return _call(
        body, [yn, dya, glu_w, glu_b], name="glu_bwd", grid=(T // tm,),
        in_specs=[row, row, pl.BlockSpec((W, W), lambda i: (0, 0)), vec],
        out_specs=[row, row, row, vec], out_shape=[S((T, W), f32), S((T, W), bf16), S((T, W), bf16), S((1, W), f32)],
        sem=("arbitrary",), plan=plan)


def conv_bwd(proj, dyb, conv_w, plan):
    T = proj.shape[0]
    RB = min(512, T)
    nrb = T // RB

    def body(h_ref, c_ref, b_ref, d_ref, w_ref, dh_ref, dc_ref, db_ref, dw_ref, s_ref):
        w0, w1, w2 = w_ref[0:1, :], w_ref[1:2, :], w_ref[2:3, :]

        def blk(i, carry):
            a0, a1, a2, sh, sc, sb = carry
            r0 = pl.multiple_of(i * RB, RB)
            rs = pl.ds(r0, RB)
            h, cg, bg, dyb_ = h_ref[rs, :], c_ref[rs, :], b_ref[rs, :], d_ref[rs, :]
            ch = cg * h
            pr = pl.ds(jnp.maximum(r0 - 8, 0), 8)
            prev = jnp.where(i > 0, c_ref[pr, :] * h_ref[pr, :], 0.0)
            ch1, ch2 = _shift_rows(ch, prev, 1), _shift_rows(ch, prev, 2)
            dbg = dyb_ * (w2 * ch + w1 * ch1 + w0 * ch2)
            db_ref[rs, :] = dbg.astype(bf16)
            dz = dyb_ * bg
            nx = pl.ds(jnp.minimum(r0 + RB, T - 8), 8)
            nxt = jnp.where(i < nrb - 1, d_ref[nx, :] * b_ref[nx, :], 0.0)
            dch = w2 * dz + w1 * _lift_rows(dz, nxt, 1) + w0 * _lift_rows(dz, nxt, 2)
            dcg, dh = dch * h, dch * cg
            dc_ref[rs, :] = dcg.astype(bf16)
            dh_ref[rs, :] = dh.astype(bf16)
            col = lambda v: jnp.sum(v, axis=0, keepdims=True)
            return (a0 + col(dz * ch2), a1 + col(dz * ch1), a2 + col(dz * ch), sh + col(dh), sc + col(dcg), sb + col(dbg))

        zero = jnp.zeros((1, LANE), f32)
        a0, a1, a2, sh, sc, sb = lax.fori_loop(0, nrb, blk, (zero,) * 6)
        dw_ref[0:1, :] = a0
        dw_ref[1:2, :] = a1
        dw_ref[2:3, :] = a2
        s_ref[0:1, :] = sh
        s_ref[1:2, :] = sc
        s_ref[2:3, :] = sb

    nb = W // LANE
    slab = pl.BlockSpec((T, LANE), lambda k: (0, k))
    three = pl.BlockSpec((3, LANE), lambda k: (0, k))
    return _call(
        body, [proj, proj, proj, dyb, conv_w], name="conv_bwd", grid=(nb,),
        in_specs=[pl.BlockSpec((T, LANE), lambda k: (0, 4 * nb + k)), pl.BlockSpec((T, LANE), lambda k: (0, 5 * nb + k)),
                  pl.BlockSpec((T, LANE), lambda k: (0, 6 * nb + k)), slab, three],
        out_specs=[slab, slab, slab, three, three],
        out_shape=[S((T, W), bf16)] * 3 + [S((3, W), f32)] * 2, sem=("parallel",), vmem=VMEM_LIMIT, plan=plan)


def in_proj_bwd_x(parts, win_g, base, scale, name, plan=None):
    T = base.shape[0]
    tm = min(512, T)
    n = len(parts)

    def body(*refs):
        p_refs, w_ref, b_ref, o_ref = refs[:n], refs[n], refs[n + 1], refs[n + 2]
        acc = scale * b_ref[...]
        for p_ref, (_, _, k) in zip(p_refs, parts):
            acc += _dot_nt(p_ref[...], w_ref[k])
        o_ref[...] = acc

    row = pl.BlockSpec((tm, D), lambda i: (i, 0))
    p_specs = [pl.BlockSpec((tm, W), (lambda i, cb=cb: (i, cb))) for _, cb, _ in parts]
    return _call(
        body, [a for a, _, _ in parts] + [win_g, base], name=name, grid=(T // tm,),
        in_specs=p_specs + [_resident((NDEV, D, W)), row],
        out_specs=[row], out_shape=[S((T, D), f32)], vmem=VMEM_LIMIT, plan=plan)


def ssm_param_bwd(lam_re, lam_im, log_dt, fr, fi, br, bi, dwb, dwcT, dlbr, dlbi):
    def body(lr_ref, li_ref, ldt_ref, fr_ref, fi_ref, br_ref, bi_ref, dwb_ref, dwc_ref, dlbr_ref, dlbi_ref,
             dbr_ref, dbi_ref, dlr_ref, dli_ref, dldt_ref, dcr_ref, dci_ref, dr_s, di_s):
        for k in range(W // LANE):
            for gl in range(NG // (W // LANE)):
                rows, src = slice((8 * k + gl) * GC, (8 * k + gl + 1) * GC), slice(gl * GC, (gl + 1) * GC)
                re, im = slice(gl * NP, (gl + 1) * NP), slice(SW + gl * NP, SW + (gl + 1) * NP)
                dr_s[rows, :] = dwb_ref[k, src, re]
                di_s[rows, :] = dwb_ref[k, src, im]
                dcr_ref[rows, :] = dwc_ref[k, src, re]
                dci_ref[rows, :] = -dwc_ref[k, src, im]
        fr_, fi_ = _per_channel(fr_ref[...]), _per_channel(fi_ref[...])
        br_, bi_, dr, di = br_ref[...], bi_ref[...], dr_s[...], di_s[...]
        dbr_ref[...] = fr_ * dr + fi_ * di
        dbi_ref[...] = fr_ * di - fi_ * dr
        dfr = jnp.sum((dr * br_ + di * bi_).reshape(NG, GC, NP), axis=1)
        dfi = jnp.sum((di * br_ - dr * bi_).reshape(NG, GC, NP), axis=1)
        _, vjp = jax.vjp(_disc, lr_ref[...], li_ref[...], ldt_ref[...])
        dlr_ref[...], dli_ref[...], dldt = vjp((dlbr_ref[...], dlbi_ref[...], dfr, dfi))
        dldt_ref[...] = _transpose_exact(dldt)

    blk = S((NG * GC, NP), f32)
    return pl.pallas_call(
        body, name="ssm_param_bwd", out_shape=[blk, blk, S((NG, NP), f32), S((NG, NP), f32), S((1, NG), f32), blk, blk],
        scratch_shapes=[pltpu.VMEM((NG * GC, NP), f32)] * 2)(
        lam_re, lam_im, log_dt, fr, fi, br, bi, dwb, dwcT, dlbr, dlbi)


def _adam(w, g, m, v):
    m = ADAM_B1 * m + (1.0 - ADAM_B1) * g
    v = ADAM_B2 * v + (1.0 - ADAM_B2) * (g * g)
    m_hat = m / (1.0 - ADAM_B1 ** ADAM_STEP)
    v_hat = v / (1.0 - ADAM_B2 ** ADAM_STEP)
    return -ADAM_LR * (m_hat / (jnp.sqrt(v_hat) + ADAM_EPS) + ADAM_WD * w), m, v


def _sum_in_order(c_ref):
    g = c_ref[0].astype(f32)
    for k in range(1, c_ref.shape[0]):
        g = g + c_ref[k].astype(f32)
    return g


def sum_blocks(contrib, name):
    def body(c_ref, o_ref):
        o_ref[...] = _sum_in_order(c_ref)

    return pl.pallas_call(body, name=name, out_shape=S(contrib.shape[1:], f32))(contrib)


def adam_update(w, m, v, contrib, name, rows_per_block=None, summed_on_0=None, plan=None):
    R, C = w.shape
    n = contrib.shape[0]
    tr = min(rows_per_block or R, R)

    def body(w_ref, m_ref, v_ref, c_ref, *refs):
        g_ref, d_ref, nm_ref, nv_ref = refs[-4:]
        g = _sum_in_order(c_ref)
        if summed_on_0 is not None:
            x, y, c = _coords()
            g = jnp.where(4 * x + 2 * y + c == 0, refs[0][...], g)
        g_ref[...] = g
        d_ref[...], nm_ref[...], nv_ref[...] = _adam(w_ref[...], g, m_ref[...], v_ref[...])

    blk = pl.BlockSpec((tr, C), lambda i: (i, 0))
    extra = [] if summed_on_0 is None else [summed_on_0]
    return _call(
        body, [w, m, v, contrib] + extra, name=name, grid=(R // tr,),
        in_specs=[blk, blk, blk, pl.BlockSpec((n, tr, C), lambda i: (0, i, 0))] + [blk] * len(extra),
        out_specs=[blk] * 4, out_shape=[S((R, C), f32)] * 4, sem=("parallel",), vmem=VMEM_LIMIT, plan=plan)


_ROWVEC = (("b_in", IN_COLS), ("ssm_d", W), ("glu_b", W), ("ln1_g", D), ("ln1_b", D), ("ln2_g", D), ("ln2_b", D))
_HALF = NG * GC // 2
_BC_LANE = {"ssm_b_re": 0, "ssm_b_im": NP, "ssm_c_re": 0, "ssm_c_im": NP}
_PACK = {}
_r = 0
for _n, _k in _ROWVEC:
    _PACK[_n] = _r
    _r += _k // LANE
for _n, _rows in (("ssm_lambda", NG), ("scalars", 8), ("ssm_b", _HALF), ("ssm_c", _HALF), ("conv_w", 16)):
    _PACK[_n] = _r
    _r += _rows
for _n in _BC_LANE:
    _PACK[_n] = _PACK[_n[:5]]
PACK_ROWS = _r
assert PACK_ROWS % 8 == 0
_SMALL = ("b_in", "ssm_lambda_re", "ssm_lambda_im", "ssm_log_dt", "ssm_b_re", "ssm_b_im", "ssm_c_re", "ssm_c_im",
          "ssm_d", "glu_b", "ln1_g", "ln1_b", "ln2_g", "ln2_b")


def pack_grads(su, shcb, sga, sgb, dd, dglu_b, dln1_g, dln1_b, dln2_g, dln2_b, dlam_re, dlam_im, dldt, sqerr, dbr, dbi,
               dc_re, dc_im, dconv):
    nI = sga.shape[0]

    def body(su_ref, sh_ref, sga_ref, sgb_ref, dd_ref, gb_ref, l1g_ref, l1b_ref, l2g_ref, l2b_ref, lr_ref, li_ref, dt_ref,
             sq_ref, br_ref, bi_ref, cr_ref, ci_ref, cw_ref, o_ref):
        o_ref[...] = jnp.zeros_like(o_ref)

        def put_row(name, v):
            r0 = _PACK[name]
            for i in range(v.shape[1] // LANE):
                o_ref[r0 + i:r0 + i + 1, :] = v[:, i * LANE:(i + 1) * LANE]

        ga, gb = sga_ref[0], sgb_ref[0]
        for i in range(1, nI):
            ga, gb = ga + sga_ref[i], gb + sgb_ref[i]
        put_row("b_in", jnp.concatenate([su_ref[k] for k in range(W // LANE)]
                                        + [sh_ref[0:1, :], sh_ref[1:2, :], sh_ref[2:3, :], ga, gb], axis=1))
        put_row("ssm_d", jnp.concatenate([dd_ref[k] for k in range(W // LANE)], axis=1))
        put_row("glu_b", gb_ref[...])
        put_row("ln1_g", l1g_ref[...])
        put_row("ln1_b", l1b_ref[...])
        put_row("ln2_g", l2g_ref[...])
        put_row("ln2_b", l2b_ref[...])
        r0 = _PACK["ssm_lambda"]
        o_ref[r0:r0 + NG, 0:NP] = lr_ref[...]
        o_ref[r0:r0 + NG, NP:2 * NP] = li_ref[...]
        r0 = _PACK["scalars"]
        o_ref[r0:r0 + 1, 0:NG] = dt_ref[...]
        o_ref[r0 + 1:r0 + 2, 0:1] = sq_ref[...]
        for name, ref in (("ssm_b_re", br_ref), ("ssm_b_im", bi_ref), ("ssm_c_re", cr_ref), ("ssm_c_im", ci_ref)):
            r0, l0 = _PACK[name], _BC_LANE[name]
            o_ref[r0:r0 + _HALF, l0:l0 + NP] = pltpu.bitcast(ref[...].astype(bf16), f32)
        for cb in range(W // LANE):
            o_ref[_PACK["conv_w"] + 3 * cb:_PACK["conv_w"] + 3 * cb + 3, :] = cw_ref[:, cb * LANE:(cb + 1) * LANE]

    return pl.pallas_call(body, name="pack_grads", out_shape=S((PACK_ROWS, LANE), f32))(
        su, shcb, sga, sgb, dd, dglu_b, dln1_g, dln1_b, dln2_g, dln2_b, dlam_re, dlam_im, dldt, sqerr, dbr, dbi, dc_re, dc_im,
        dconv)


def adam_small(packed_all, params):
    names = list(_SMALL) + ["conv_w"]
    flat = [a for n in names for a in params[n]]

    def body(*refs):
        p_ref = refs[0]
        ins = refs[1:1 + 3 * len(names)]
        outs = refs[1 + 3 * len(names):-2]
        loss_ref, g_ref = refs[-2], refs[-1]

        def part(k, rs=slice(None), ls=slice(None)):
            return p_ref[k, rs, ls]

        g_all = part(0)
        for k in range(1, NDEV):
            g_all = g_all + part(k)
        g_ref[...] = g_all

        def rows(name, r0, n, l0=0, lanes=LANE):
            return g_ref[_PACK[name] + r0:_PACK[name] + r0 + n, l0:l0 + lanes]

        def grad_of(name):
            if name in dict(_ROWVEC):
                return jnp.concatenate([rows(name, i, 1) for i in range(dict(_ROWVEC)[name] // LANE)], axis=1)
            if name in ("ssm_lambda_re", "ssm_lambda_im"):
                return rows("ssm_lambda", 0, NG, NP * (name == "ssm_lambda_im"), NP)[None]
            if name == "ssm_log_dt":
                return rows("scalars", 0, 1, 0, NG)
            if name in _BC_LANE:
                rs, ls = slice(_PACK[name], _PACK[name] + _HALF), slice(_BC_LANE[name], _BC_LANE[name] + NP)
                g = pltpu.bitcast(part(0, rs, ls), bf16).astype(f32)
                for k in range(1, NDEV):
                    g = g + pltpu.bitcast(part(k, rs, ls), bf16).astype(f32)
                return g.reshape(1, NG, GC, NP)
            full = jnp.concatenate([rows("conv_w", 3 * cb, 3) for cb in range(W // LANE)], axis=1)
            x, y, c = _coords()
            col0 = (4 * x + 2 * y + c) * (W // NDEV)
            sel = (lax.broadcasted_iota(jnp.int32, (W, W // NDEV), 0)
                   == lax.broadcasted_iota(jnp.int32, (W, W // NDEV), 1) + col0).astype(f32)
            return jnp.dot(full, sel, precision=HIGHEST, preferred_element_type=f32)[None]

        loss_ref[...] = 0.5 * rows("scalars", 1, 1, 0, 1)
        for i, name in enumerate(names):
            w_ref, m_ref, v_ref = ins[3 * i:3 * i + 3]
            g = grad_of(name)
            d, m, v = _adam(w_ref[...], g, m_ref[...], v_ref[...])
            outs[4 * i][...] = g
            outs[4 * i + 1][...] = d
            outs[4 * i + 2][...] = m
            outs[4 * i + 3][...] = v

    out_shape = [S(params[n][0].shape, f32) for n in names for _ in range(4)] + [S((1, 1), f32)]
    res = pl.pallas_call(body, name="adam_small", out_shape=out_shape, scratch_shapes=[pltpu.VMEM((PACK_ROWS, LANE), f32)],
                         compiler_params=_cp(None, VMEM_LIMIT))(packed_all, *flat)
    return {n: res[4 * i:4 * i + 4] for i, n in enumerate(names)}, res[-1]


def _block_diag(wgt):
    eye = jnp.eye(8, dtype=wgt.dtype)
    out = wgt[:, :, :, None, :] * eye[None, :, None, :, None]
    return out.reshape(4, 8 * wgt.shape[2], 8 * wgt.shape[3])


def kernel(x, w_in, b_in, ssm_lambda_re, ssm_lambda_im, ssm_log_dt, ssm_b_re, ssm_b_im, ssm_c_re, ssm_c_im, ssm_d, glu_w, glu_b, w_ssm_out, conv_w, w_conv_out, w_o, ln1_g, ln1_b, w_gate, w_up, w_down, ln2_g, ln2_b, loss_target, m_w_in, m_b_in, m_ssm_lambda_re, m_ssm_lambda_im, m_ssm_log_dt, m_ssm_b_re, m_ssm_b_im, m_ssm_c_re, m_ssm_c_im, m_ssm_d, m_glu_w, m_glu_b, m_w_ssm_out, m_conv_w, m_w_conv_out, m_w_o, m_ln1_g, m_ln1_b, m_w_gate, m_w_up, m_w_down, m_ln2_g, m_ln2_b, v_w_in, v_b_in, v_ssm_lambda_re, v_ssm_lambda_im, v_ssm_log_dt, v_ssm_b_re, v_ssm_b_im, v_ssm_c_re, v_ssm_c_im, v_ssm_d, v_glu_w, v_glu_b, v_w_ssm_out, v_conv_w, v_w_conv_out, v_w_o, v_ln1_g, v_ln1_b, v_w_gate, v_w_up, v_w_down, v_ln2_g, v_ln2_b):
    given = dict(locals())
    xs = x[0]
    target = loss_target[0]

    tr = lambda a: jnp.swapaxes(a[0], 0, 1)
    win_s, glu_s, wso_s, wco_s, wo_s, wgT_s, wuT_s, wd_s = prep_weights(
        [w_in[0], glu_w[0], w_ssm_out[0], w_conv_out[0], w_o[0], tr(w_gate), tr(w_up), w_down[0]])
    (win_g,) = run_plan(GatherPlan([win_s], srcs=(0,)), "gather_w_in_u")

    lam_re, lam_im = ssm_lambda_re[0], ssm_lambda_im[0]
    ldt = ssm_log_dt[0].reshape(NG, 1)
    br2 = jnp.swapaxes(ssm_b_re[0], 1, 2).reshape(NG * GC, NP)
    bi2 = jnp.swapaxes(ssm_b_im[0], 1, 2).reshape(NG * GC, NP)
    lbr, lbi, fr, fi, bbr, bbi = ssm_params(lam_re, lam_im, ldt, br2, bi2)
    bb_t = lambda b: b.reshape(4, 8, GC, NP)
    wb = jnp.concatenate([_block_diag(bb_t(bbr)), _block_diag(bb_t(bbi))], axis=2)
    c_t = lambda c: c.reshape(4, 8, GC, NP).transpose(0, 1, 3, 2)
    wc = jnp.concatenate([_block_diag(c_t(ssm_c_re[0])), -_block_diag(c_t(ssm_c_im[0]))], axis=1)
    wbT, wcT = wb.transpose(0, 2, 1), wc.transpose(0, 2, 1)
    wb, wc, wbT, wcT = wb.astype(bf16), wc.astype(bf16), wbT.astype(bf16), wcT.astype(bf16)
    lbr_s, lbi_s = lbr.reshape(4, 1, SW), lbi.reshape(4, 1, SW)
    dsk = ssm_d[0].reshape(4, 1, LANE)

    u_nat, xb = in_proj_u(xs, win_g, b_in)
    half_a, half_b = (0, 3, 5, 6), (1, 2, 4, 7)
    (yn, u_p, xr_p, xi_p), (win_g, conv_g, glu_g, wso_g, wuT_g) = ssm_fwd(
        u_nat, wb, wc, lbr_s, lbi_s, dsk,
        Plans([GatherPlan([win_s], srcs=tuple(range(1, NDEV)), into=[win_g]), GatherPlan([conv_w[0], glu_s, wso_s]),
               GatherPlan([wuT_s], srcs=half_a)]))
    conv_f = conv_g.transpose(1, 0, 2).reshape(3, W)
    (proj,), (wco_g, wo_g, wgT_g) = in_proj_rest(
        xb, win_g, b_in, Plans([GatherPlan([wco_s, wo_s]), GatherPlan([wgT_s], srcs=half_a)]))
    glu_f, wo_f = glu_g.reshape(W, W), wo_g.reshape(D, D)
    ya = glu_fwd(yn, glu_f, glu_b)
    yb = conv_fwd(proj, conv_f)
    (merged,), (wgT_g,) = merge_fwd(ya, yb, wso_g, wco_g, proj, GatherPlan([wgT_s], srcs=half_b, into=[wgT_g]))
    (r1, x1b), (wuT_g,) = mix_ln1(merged, wo_f, xs, ln1_g, ln1_b, GatherPlan([wuT_s], srcs=half_b, into=[wuT_g]))
    wgT, wuT = wgT_g.reshape(F, D), wuT_g.reshape(F, D)
    (gate, up, hid), (wd_g,) = gate_up(x1b, wgT, wuT, GatherPlan([wd_s]))
    wd_f = wd_g.reshape(F, D)
    dr2, dffn, sqerr, dln2_g, dln2_b = down_loss(hid, wd_f, r1, ln1_g, ln1_b, ln2_g, ln2_b, target)

    dwd, _ = mm_tn_rows(hid, dffn, "grad_w_down")
    dwd = dwd.reshape(NDEV, FS, D)
    (dgate, dup), (r_wd,) = ffn_bwd_act(dffn, wd_f, gate, up, ScatterPlan([dwd], only=half_a))
    dwgT, (r_wd,) = mm_tn_rows(dgate, x1b, "grad_w_gate", plan=ScatterPlan([dwd], only=half_b, into=[r_wd]))
    dwgT = dwgT.reshape(NDEV, FS, D)
    dwuT, (r_wgT,) = mm_tn_rows(dup, x1b, "grad_w_up", plan=ScatterPlan([dwgT], only=half_a))
    dwuT = dwuT.reshape(NDEV, FS, D)
    (dr1, dmix, dln1_g, dln1_b), (r_wgT, r_wuT) = ffn_bwd_x(
        dgate, dup, wgT, wuT, dr2, r1, ln1_g,
        Plans([ScatterPlan([dwgT], only=half_b, into=[r_wgT]), ScatterPlan([dwuT], only=half_a)]))
    (dYA, dYB, dga, dgb, sga, sgb), (r_wuT,) = merge_bwd(dmix, wo_f, ya, yb, wso_g, wco_g, proj,
                                                         ScatterPlan([dwuT], only=half_b, into=[r_wuT]))
    dwo, _ = mm_tn_rows(merged, dmix, "grad_w_o")
    dwo = dwo.reshape(NDEV, D // NDEV, D)
    dya, dyb, dwso, dwco = branches_bwd(dYA, dYB, ya, yb, wso_g, wco_g)
    (dyn, dsp, gb, dglu_b), (r_wso,) = glu_bwd(yn, dya, glu_f, glu_b, ScatterPlan([dwso]))
    dglu = mm_tn_rows(gb, dsp, "grad_glu_w")[0].reshape(NDEV, W // NDEV, W)
    (dh, dcg, dbg, dconv, shcb), (r_wco,) = conv_bwd(proj, dyb, conv_f, ScatterPlan([dwco]))
    dwin, (r_wo, r_glu) = grad_w_in_rest(xb, dh, dcg, dbg, dga, dgb, ScatterPlan([dwo, dglu]))
    (du, dwb, dwcT, dlbr_s, dlbi_s, dd, su), (r_win,) = ssm_bwd(
        u_p, dyn, xr_p, xi_p, wbT, wcT, lbr_s, lbi_s, dsk, ScatterPlan([dwin], only=tuple(range(1, NDEV))))

    dbr2, dbi2, dlam_re, dlam_im, dldt, dc_re, dc_im = ssm_param_bwd(
        lam_re, lam_im, ldt, fr, fi, br2, bi2, dwb, dwcT, dlbr_s.reshape(NG, NP), dlbi_s.reshape(NG, NP))
    packed = pack_grads(su, shcb, sga, sgb, dd, dglu_b, dln1_g, dln1_b, dln2_g, dln2_b, dlam_re, dlam_im, dldt, sqerr,
                        dbr2, dbi2, dc_re, dc_im, dconv)
    dwin_u = mm_tn(xb, du, "grad_w_in_u").reshape(NDEV, D // NDEV, W)

    rest = [(dh, 0, 1), (dcg, 0, 2), (dbg, 0, 3), (dga, 0, 4), (dga, 1, 5), (dgb, 0, 6), (dgb, 1, 7)]
    (gx_rest,), (r_win_u, small_all) = in_proj_bwd_x(
        rest, win_g, dr1, ALPHA, "in_proj_bwd_x_rest", Plans([ScatterPlan([dwin_u]), GatherPlan([packed])]))
    my_rows = sum_blocks(r_win_u, "sum_w_in_u")

    out = {}

    def put(name, res, back=lambda a: a[None]):
        out["grad_" + name], out["delta_" + name], out["new_m_" + name], out["new_v_" + name] = [back(r) for r in res]

    res_wd, (win_u_sum,) = adam_update(w_down[0], m_w_down[0], v_w_down[0], r_wd, "adam_w_down", 176,
                                       plan=ScatterPlan([my_rows], only=(0,), whole=True))
    put("w_down", res_wd)
    (grad_x,), _ = in_proj_bwd_x([(du, 0, 0)], win_g, gx_rest, 1.0, "in_proj_bwd_x_u")
    put("w_in", adam_update(w_in[0], m_w_in[0], v_w_in[0], r_win, "adam_w_in", 256,
                            summed_on_0=win_u_sum.reshape(D, W))[0])
    put("glu_w", adam_update(glu_w[0], m_glu_w[0], v_glu_w[0], r_glu, "adam_glu_w")[0])
    put("w_ssm_out", adam_update(w_ssm_out[0], m_w_ssm_out[0], v_w_ssm_out[0], r_wso, "adam_w_ssm_out")[0])
    put("w_conv_out", adam_update(w_conv_out[0], m_w_conv_out[0], v_w_conv_out[0], r_wco, "adam_w_conv_out")[0])
    put("w_o", adam_update(w_o[0], m_w_o[0], v_w_o[0], r_wo, "adam_w_o")[0])
    untr = lambda a: jnp.swapaxes(a, 0, 1)[None]
    put("w_gate", adam_update(tr(w_gate), tr(m_w_gate), tr(v_w_gate), r_wgT, "adam_w_gate", 176)[0], untr)
    put("w_up", adam_update(tr(w_up), tr(m_w_up), tr(v_w_up), r_wuT, "adam_w_up", 176)[0], untr)
    as_c = lambda a: jnp.swapaxes(a, 2, 3)
    params = {n: (given[n], given["m_" + n], given["v_" + n]) for n in list(_SMALL) + ["conv_w"]}
    for n in ("ssm_b_re", "ssm_b_im"):
        params[n] = tuple(as_c(a) for a in params[n])
    small, loss = adam_small(small_all, params)
    for n, res in small.items():
        put(n, res, as_c if n in ("ssm_b_re", "ssm_b_im") else (lambda a: a))

    names = ["w_in", "b_in", "ssm_lambda_re", "ssm_lambda_im", "ssm_log_dt", "ssm_b_re", "ssm_b_im", "ssm_c_re", "ssm_c_im",
             "ssm_d", "glu_w", "glu_b", "w_ssm_out", "conv_w", "w_conv_out", "w_o", "ln1_g", "ln1_b", "w_gate", "w_up",
             "w_down", "ln2_g", "ln2_b"]
    return (loss.reshape(()), grad_x[None], *[out[p + n] for p in ("grad_", "delta_", "new_m_", "new_v_") for n in names])
```

```python
import functools
import math

import jax
import jax.numpy as jnp
from jax import lax
from jax.experimental import pallas as pl
from jax.experimental.pallas import tpu as pltpu

f32, bf16 = jnp.float32, jnp.bfloat16
S = jax.ShapeDtypeStruct
MESH = pl.DeviceIdType.MESH
HIGHEST = lax.Precision.HIGHEST

D = 1024
W = 512
NG, NP, GC = 32, 64, 16
F = 2816
NDEV = 8
FS = F // NDEV
IN_COLS = 8 * W
ALPHA = 2.0 ** 0.25
LN_EPS = 1e-5
ADAM_LR, ADAM_B1, ADAM_B2, ADAM_EPS, ADAM_WD, ADAM_STEP = 0.001, 0.9, 0.999, 1e-08, 0.01, 10
NC = 32
LANE = 128
SW = 4 * LANE
VMEM_LIMIT = 56 * 1024 * 1024
GRAD_DT = bf16
ANY = pl.BlockSpec(memory_space=pl.ANY)


def _cp(sem=None, vmem=None):
    return pltpu.CompilerParams(dimension_semantics=sem, vmem_limit_bytes=vmem)


def _resident(shape):
    return pl.BlockSpec(shape, lambda i: (0,) * len(shape), pipeline_mode=pl.Buffered(1))


def _dot(a, b):
    return jnp.dot(a, b, preferred_element_type=f32)


def _dot_nt(a, b):
    return lax.dot_general(a, b, (((1,), (1,)), ((), ())), preferred_element_type=f32)


def _dot_tn(a, b):
    return lax.dot_general(a, b, (((0,), (0,)), ((), ())), preferred_element_type=f32)


def _eye(n):
    return (lax.broadcasted_iota(jnp.int32, (n, n), 0) == lax.broadcasted_iota(jnp.int32, (n, n), 1)).astype(f32)


def _transpose_exact(a):
    return lax.dot_general(a, _eye(a.shape[0]), (((0,), (0,)), ((), ())), precision=HIGHEST, preferred_element_type=f32)


def _sigmoid(x):
    return 1.0 / (1.0 + jnp.exp(-x))


_GK = math.sqrt(2.0 / math.pi)


def _gelu(x):
    return 0.5 * x * (1.0 + jnp.tanh(_GK * (x + 0.044715 * x * x * x)))


def _gelu_grad(x):
    th = jnp.tanh(_GK * (x + 0.044715 * x * x * x))
    return 0.5 * (1.0 + th) + 0.5 * x * (1.0 - th * th) * _GK * (1.0 + 3.0 * 0.044715 * x * x)


ROW_PART = 256


def _row_parts(tm):
    return [slice(r, r + min(ROW_PART, tm)) for r in range(0, tm, min(ROW_PART, tm))]


def _ln_stats(r):
    mu = jnp.mean(r, axis=-1, keepdims=True)
    xc = r - mu
    var = jnp.mean(xc * xc, axis=-1, keepdims=True)
    rstd = lax.rsqrt(var + LN_EPS)
    return xc * rstd, rstd


def _ln_bwd(dy, xhat, rstd, g):
    dxh = dy * g
    m1 = jnp.mean(dxh, axis=-1, keepdims=True)
    m2 = jnp.mean(dxh * xhat, axis=-1, keepdims=True)
    return rstd * (dxh - m1 - xhat * m2)


def _coords():
    return lax.axis_index("x"), lax.axis_index("y"), lax.axis_index("c")


def _when(cond, fn):
    if cond is True:
        fn()
    else:
        pl.when(cond)(fn)


class GatherPlan:
    aliases = ()

    def __init__(self, arrs, srcs=None, into=None):
        n = self.n = len(arrs)
        self.srcs = srcs
        self.inputs = list(arrs) + list(into or [])
        if into:
            self.aliases = tuple((n + a, a) for a in range(n))
        self.out_shape = [S((NDEV,) + a.shape, a.dtype) for a in arrs]
        self.sems = [pltpu.SemaphoreType.DMA((n, 7)), pltpu.SemaphoreType.DMA((n, 7)), pltpu.SemaphoreType.DMA((n,))]

    def _has(self, dev):
        if self.srcs is None:
            return True
        idx = 4 * dev[0] + 2 * dev[1] + dev[2]
        return functools.reduce(jnp.logical_or, [idx == s for s in self.srcs])

    def _parts(self, ins, outs, sems):
        n = self.n
        send_sems, recv_sems, loc_sems = sems
        x, y, c = _coords()
        me, sib = (x, y, c), (x, y, 1 - c)
        chips = [(1 - x, y), (x, 1 - y), (1 - x, 1 - y)]

        def slot(a, dev):
            return outs[a].at[4 * dev[0] + 2 * dev[1] + dev[2]]

        def copy(a, k, block, to, src=None):
            return pltpu.make_async_remote_copy(
                src_ref=slot(a, block) if src is None else src, dst_ref=slot(a, block),
                send_sem=send_sems.at[a, k], recv_sem=recv_sems.at[a, k], device_id=to, device_id_type=MESH)

        each = [(j, chip, a) for j, chip in enumerate(chips) for a in range(n)]
        own = self._has(me)
        return dict(
            mine=lambda: [(pltpu.make_async_copy(ins[a], slot(a, me), loc_sems.at[a]), own) for a in range(n)],
            first=lambda: ([(copy(a, 0, me, sib, src=ins[a]), own) for a in range(n)]
                           + [(copy(a, 1 + j, me, (*chip, c), src=ins[a]), own) for j, chip, a in each]),
            landed=lambda: [(copy(a, 1 + j, (*chip, c), me), self._has((*chip, c))) for j, chip, a in each],
            passed=lambda: [(copy(a, 4 + j, (*chip, c), sib), self._has((*chip, c))) for j, chip, a in each],
            from_sib=lambda: ([(copy(a, 0, sib, me), self._has(sib)) for a in range(n)]
                              + [(copy(a, 4 + j, (*chip, 1 - c), me), self._has((*chip, 1 - c))) for j, chip, a in each]))

    def start(self, ins, outs, sems):
        p = self._parts(ins, outs, sems)
        for cp, cond in p["mine"]() + p["first"]():
            _when(cond, cp.start)

    def forward(self, ins, outs, sems):
        p = self._parts(ins, outs, sems)
        for (got, cond), (fwd, _) in zip(p["landed"](), p["passed"]()):
            def relay(got=got, fwd=fwd):
                got.wait_recv()
                fwd.start()

            _when(cond, relay)

    def finish(self, ins, outs, sems):
        p = self._parts(ins, outs, sems)
        for cp, cond in p["from_sib"]():
            _when(cond, cp.wait_recv)
        for cp, cond in p["first"]() + p["passed"]():
            _when(cond, cp.wait_send)
        for cp, cond in p["mine"]():
            _when(cond, cp.wait)


class ScatterPlan:
    aliases = ()

    def __init__(self, gs, only=None, into=None, whole=False):
        n = self.n = len(gs)
        self.only = only
        self.whole = whole
        self.inputs = list(gs) + list(into or [])
        if into:
            self.aliases = tuple((n + a, a) for a in range(n))
        self.out_shape = [S((NDEV,) + g.shape if whole else g.shape, g.dtype) for g in gs]
        self.sems = [pltpu.SemaphoreType.DMA((n, 7)), pltpu.SemaphoreType.DMA((n, 7)), pltpu.SemaphoreType.DMA((n,))]

    def _owner(self, idx):
        if self.only is None:
            return True
        return functools.reduce(jnp.logical_or, [idx == b for b in self.only])

    def _copies(self, ins, outs, sems):
        n = self.n
        send_sems, recv_sems, loc_sems = sems
        x, y, c = _coords()
        me = 4 * x + 2 * y + c
        mine = self._owner(me)
        block = (lambda a, k: ins[a]) if self.whole else (lambda a, k: ins[a].at[k])
        copies = [(pltpu.make_async_copy(block(a, me), outs[a].at[me], loc_sems.at[a]), mine, None) for a in range(n)]
        for m in range(1, NDEV):
            px = 1 - x if m & 4 else x
            py = 1 - y if m & 2 else y
            pc = 1 - c if m & 1 else c
            peer = 4 * px + 2 * py + pc
            for a in range(n):
                copies.append((pltpu.make_async_remote_copy(
                    src_ref=block(a, peer), dst_ref=outs[a].at[me],
                    send_sem=send_sems.at[a, m - 1], recv_sem=recv_sems.at[a, m - 1],
                    device_id=(px, py, pc), device_id_type=MESH), self._owner(peer), mine))
        return copies

    def start(self, ins, outs, sems):
        for cp, sends, _ in self._copies(ins, outs, sems):
            _when(sends, cp.start)

    def forward(self, ins, outs, sems):
        pass

    def finish(self, ins, outs, sems):
        for cp, sends, receives in self._copies(ins, outs, sems):
            if receives is None:
                _when(sends, cp.wait)
            else:
                _when(sends, cp.wait_send)
                _when(receives, cp.wait_recv)


class Plans:
    def __init__(self, plans):
        self.plans = plans
        self.inputs = [a for p in plans for a in p.inputs]
        self.out_shape = [s for p in plans for s in p.out_shape]
        self.sems = [s for p in plans for s in p.sems]
        self.aliases, i, o = [], 0, 0
        for p in plans:
            self.aliases += [(i + a, o + b) for a, b in p.aliases]
            i, o = i + len(p.inputs), o + len(p.out_shape)

    def _each(self, what, ins, outs, sems):
        i = o = s = 0
        for p in self.plans:
            ni, no, ns = len(p.inputs), len(p.out_shape), len(p.sems)
            getattr(p, what)(ins[i:i + ni], outs[o:o + no], sems[s:s + ns])
            i, o, s = i + ni, o + no, s + ns

    def start(self, ins, outs, sems):
        self._each("start", ins, outs, sems)

    def forward(self, ins, outs, sems):
        self._each("forward", ins, outs, sems)

    def finish(self, ins, outs, sems):
        self._each("finish", ins, outs, sems)


def _call(body, args, *, name, grid, in_specs, out_specs, out_shape, scratch=(), sem=None, vmem=None, plan=None,
          relay_step=None):
    if plan is None:
        outs = pl.pallas_call(body, name=name, grid=grid, in_specs=list(in_specs), out_specs=list(out_specs),
                              out_shape=list(out_shape), scratch_shapes=list(scratch),
                              compiler_params=_cp(sem, vmem))(*args)
        return list(outs), []
    ni, no, ns = len(in_specs), len(out_specs), len(scratch)
    pi, po = len(plan.inputs), len(plan.out_shape)
    aliases = {ni + a: no + b for a, b in plan.aliases}

    def wrapped(*refs):
        main_in, p_in = refs[:ni], refs[ni:ni + pi]
        main_out, p_out = refs[ni + pi:ni + pi + no], refs[ni + pi + no:ni + pi + no + po]
        main_scr, p_sems = refs[ni + pi + no + po:ni + pi + no + po + ns], refs[ni + pi + no + po + ns:]
        ids = [pl.program_id(d) for d in range(len(grid))]
        first = functools.reduce(jnp.logical_and, [i == 0 for i in ids])
        last = functools.reduce(jnp.logical_and, [i == g - 1 for i, g in zip(ids, grid)])

        @pl.when(first)
        def _():
            plan.start(p_in, p_out, p_sems)

        @pl.when(last if relay_step is None else ids[0] == max(relay_step, 0))
        def _():
            plan.forward(p_in, p_out, p_sems)

        body(*main_in, *main_out, *main_scr)

        @pl.when(last)
        def _():
            plan.finish(p_in, p_out, p_sems)

    outs = pl.pallas_call(
        wrapped, name=name, grid=grid, in_specs=list(in_specs) + [ANY] * pi, out_specs=list(out_specs) + [ANY] * po,
        out_shape=list(out_shape) + list(plan.out_shape), scratch_shapes=list(scratch) + list(plan.sems),
        input_output_aliases=aliases, compiler_params=_cp(("arbitrary",) * len(grid), vmem),
    )(*args, *plan.inputs)
    return list(outs[:no]), list(outs[no:])


def run_plan(plan, name):
    def body(*refs):
        ins, outs, sems = refs[:len(plan.inputs)], refs[len(plan.inputs):len(plan.inputs) + len(plan.out_shape)], \
            refs[len(plan.inputs) + len(plan.out_shape):]
        plan.start(ins, outs, sems)
        plan.forward(ins, outs, sems)
        plan.finish(ins, outs, sems)

    return pl.pallas_call(body, name=name, in_specs=[ANY] * len(plan.inputs), out_specs=[ANY] * len(plan.out_shape),
                          out_shape=list(plan.out_shape), scratch_shapes=list(plan.sems))(*plan.inputs)


def mm_tn(a, b, name, tn=512):
    T, K = a.shape
    N = b.shape[1]
    tn = min(tn, N)

    def body(a_ref, b_ref, o_ref):
        o_ref[...] = _dot_tn(a_ref[...], b_ref[...]).astype(GRAD_DT)

    (out,), _ = _call(body, [a, b], name=name, grid=(N // tn,),
                      in_specs=[_resident((T, K)), pl.BlockSpec((T, tn), lambda j: (0, j))],
                      out_specs=[pl.BlockSpec((None, K, tn), lambda j: (j, 0, 0))],
                      out_shape=[S((N // tn, K, tn), GRAD_DT)], sem=("parallel",), vmem=VMEM_LIMIT)
    return out


def grad_w_in_rest(xb, dh, dcg, dbg, dga, dgb, plan):
    T = xb.shape[0]
    order = ((0, 0), (1, 1), (2, 2), (3, 3), (4, 3), (5, 4), (6, 4))

    def body(x_ref, *refs):
        o_ref = refs[-1]
        j = pl.program_id(0)
        for step, opnd in order:
            @pl.when(j == step)
            def _(opnd=opnd):
                o_ref[...] = _dot_tn(x_ref[...], refs[opnd][...]).astype(GRAD_DT)

    once = lambda: pl.BlockSpec((T, W), lambda j: (0, 0), pipeline_mode=pl.Buffered(1))
    (out,), sent = _call(
        body, [xb, dh, dcg, dbg, dga, dgb], name="grad_w_in_rest", grid=(len(order),),
        in_specs=[_resident((T, D)), once(), once(), once(),
                  pl.BlockSpec((T, W), lambda j: (0, jnp.clip(j - 3, 0, 1))),
                  pl.BlockSpec((T, W), lambda j: (0, jnp.clip(j - 5, 0, 1)))],
        out_specs=[pl.BlockSpec((None, D, W), lambda j: (1 + j, 0, 0))],
        out_shape=[S((NDEV, D, W), GRAD_DT)], sem=("arbitrary",), vmem=VMEM_LIMIT, plan=plan)
    return out, sent


def mm_tn_rows(a, b, name, tk=256, plan=None):
    T, K = a.shape
    N = b.shape[1]
    tk = min(tk, K)

    def body(a_ref, b_ref, o_ref):
        o_ref[...] = _dot_tn(a_ref[...], b_ref[...]).astype(GRAD_DT)

    (out,), sent = _call(body, [a, b], name=name, grid=(K // tk,),
                         in_specs=[pl.BlockSpec((T, tk), lambda i: (0, i)), _resident((T, N))],
                         out_specs=[pl.BlockSpec((tk, N), lambda i: (i, 0))], out_shape=[S((K, N), GRAD_DT)],
                         sem=("parallel",), vmem=VMEM_LIMIT, plan=plan)
    return out, sent


def prep_weights(ws):
    def body(*refs):
        for i in range(len(ws)):
            refs[len(ws) + i][...] = refs[i][...].astype(bf16)

    return pl.pallas_call(body, name="prep_weights", out_shape=[S(w.shape, bf16) for w in ws],
                          compiler_params=_cp(None, VMEM_LIMIT))(*ws)


REST_BLOCKS = (4, 5, 6, 7, 1, 2, 3)
REST_COLS = len(REST_BLOCKS) * W


def in_proj_u(x, win_g, b_in):
    T = x.shape[0]
    tm = min(1024, T)

    def body(x_ref, w_ref, b_ref, u_ref, xb_ref):
        xb = x_ref[...].astype(bf16)
        xb_ref[...] = xb
        u_ref[...] = _dot(xb, w_ref[...]) + b_ref[...]

    row = pl.BlockSpec((tm, D), lambda i: (i, 0))
    return pl.pallas_call(
        body, name="in_proj_u", grid=(T // tm,),
        in_specs=[row, pl.BlockSpec((None, D, W), lambda i: (0, 0, 0)), pl.BlockSpec((1, W), lambda i: (0, 0))],
        out_specs=[pl.BlockSpec((tm, W), lambda i: (i, 0)), row],
        out_shape=[S((T, W), f32), S((T, D), bf16)], compiler_params=_cp(("parallel",), VMEM_LIMIT),
    )(x, win_g, b_in)


def in_proj_rest(xb, win_g, b_in, plan):
    T = xb.shape[0]
    tm = min(512, T)

    def body(x_ref, w_ref, b_ref, o_ref):
        xb_ = x_ref[...]
        for i, k in enumerate(REST_BLOCKS):
            o_ref[:, i * W:(i + 1) * W] = _dot(xb_, w_ref[k]) + b_ref[:, k * W:(k + 1) * W]

    return _call(
        body, [xb, win_g, b_in], name="in_proj_rest", grid=(T // tm,),
        in_specs=[pl.BlockSpec((tm, D), lambda i: (i, 0)), _resident((NDEV, D, W)), _resident((1, IN_COLS))],
        out_specs=[pl.BlockSpec((tm, REST_COLS), lambda i: (i, 0))],
        out_shape=[S((T, REST_COLS), f32)], vmem=VMEM_LIMIT, plan=plan, relay_step=T // tm - 2)


def _to_scan_order(a_ref, o_ref):
    L = a_ref.shape[0] // NC

    def step(jb, carry):
        j0 = pl.multiple_of(jb * 8, 8)
        for q in range(NC // 8):
            x = jnp.stack([a_ref[pl.ds((8 * q + c) * L + j0, 8), :] for c in range(8)], axis=0)
            y = jnp.swapaxes(x, 0, 1)
            for j in range(8):
                o_ref[pl.ds((j0 + j) * NC + 8 * q, 8), :] = y[j]
        return carry

    lax.fori_loop(0, L // 8, step, 0)


def _to_time_order(a_ref, o_ref):
    L = a_ref.shape[0] // NC

    def step(jb, carry):
        j0 = pl.multiple_of(jb * 16, 16)
        for q in range(NC // 8):
            halves = []
            for h in range(2):
                x = jnp.stack([a_ref[pl.ds((j0 + 8 * h + j) * NC + 8 * q, 8), :] for j in range(8)], axis=0)
                halves.append(jnp.swapaxes(x, 0, 1))
            for c in range(8):
                o_ref[pl.ds((8 * q + c) * L + j0, 16), :] = jnp.concatenate(
                    [halves[0][c], halves[1][c]], axis=0).astype(o_ref.dtype)
        return carry

    lax.fori_loop(0, L // 16, step, 0)


def _disc(lr, li, ldt):
    dt = jnp.exp(ldt)
    mag = jnp.exp(lr * dt)
    lbr = mag * jnp.cos(li * dt)
    lbi = mag * jnp.sin(li * dt)
    den = lr * lr + li * li
    nr = lbr - 1.0
    return lbr, lbi, (nr * lr + lbi * li) / den, (lbi * lr - nr * li) / den


def _per_channel(f):
    return jnp.broadcast_to(f[:, None, :], (NG, GC, NP)).reshape(NG * GC, NP)


def ssm_params(lam_re, lam_im, log_dt, br, bi):
    def body(lr_ref, li_ref, ldt_ref, br_ref, bi_ref, lbr_ref, lbi_ref, fr_ref, fi_ref, bbr_ref, bbi_ref):
        lbr, lbi, fr, fi = _disc(lr_ref[...], li_ref[...], ldt_ref[...])
        lbr_ref[...], lbi_ref[...], fr_ref[...], fi_ref[...] = lbr, lbi, fr, fi
        fr_, fi_, br_, bi_ = _per_channel(fr), _per_channel(fi), br_ref[...], bi_ref[...]
        bbr_ref[...] = fr_ * br_ - fi_ * bi_
        bbi_ref[...] = fr_ * bi_ + fi_ * br_

    return pl.pallas_call(body, name="ssm_params", out_shape=[S((NG, NP), f32)] * 4 + [S((NG * GC, NP), f32)] * 2)(
        lam_re, lam_im, log_dt, br, bi)


SCAN_UNROLL = 4
SCAN_LANES = 2 * LANE


def _steps(n, body, carry):
    main = n // SCAN_UNROLL

    def trip(t, c):
        for q in range(SCAN_UNROLL):
            c = body(t * SCAN_UNROLL + q, c)
        return c

    carry = lax.fori_loop(0, main, trip, carry)
    for i in range(main * SCAN_UNROLL, n):
        carry = body(i, carry)
    return carry


def _scan_body(T):
    L = T // NC
    RB = min(512, T)
    nsq = int(round(math.log2(L)))
    assert 2 ** nsq == L and T % RB == 0 and L % 16 == 0

    def rows(i):
        return pl.ds(pl.multiple_of(i * RB, RB), RB)

    def tile(j):
        return pl.ds(j * NC if isinstance(j, int) else pl.multiple_of(j * NC, NC), NC)

    def forward_states(u_ref, wb_ref, lbr_ref, lbi_ref, sre, sim, ere, eim):
        def bproj(i, carry):
            bu = _dot(u_ref[rows(i), :].astype(bf16), wb_ref[...])
            sre[rows(i), :] = bu[:, :SW]
            sim[rows(i), :] = bu[:, SW:]
            return carry

        lax.fori_loop(0, T // RB, bproj, 0)
        for lb in range(SW // SCAN_LANES):
            ls = slice(lb * SCAN_LANES, (lb + 1) * SCAN_LANES)
            ar = jnp.broadcast_to(lbr_ref[:, ls], (NC, SCAN_LANES))
            ai = jnp.broadcast_to(lbi_ref[:, ls], (NC, SCAN_LANES))

            def step(j, carry):
                xr, xi = carry
                nr = ar * xr - ai * xi + sre[tile(j), ls]
                ni = ar * xi + ai * xr + sim[tile(j), ls]
                sre[tile(j), ls] = nr
                sim[tile(j), ls] = ni
                return nr, ni

            zero = jnp.zeros((NC, SCAN_LANES), f32)
            _steps(L, step, (zero, zero))
            pr, pi = lbr_ref[:, ls], lbi_ref[:, ls]
            for _ in range(nsq):
                pr, pi = pr * pr - pi * pi, 2.0 * pr * pi
            er = jnp.zeros((1, SCAN_LANES), f32)
            ei = er
            ere[0:1, ls] = er
            eim[0:1, ls] = ei
            base = (L - 1) * NC
            for c in range(1, NC):
                lr_ = sre[base + c - 1:base + c, ls]
                li_ = sim[base + c - 1:base + c, ls]
                er, ei = lr_ + pr * er - pi * ei, li_ + pr * ei + pi * er
                ere[c:c + 1, ls] = er
                eim[c:c + 1, ls] = ei
            e_r, e_i = ere[:, ls].reshape(NC // 8, 8, SCAN_LANES), eim[:, ls].reshape(NC // 8, 8, SCAN_LANES)
            ar8, ai8 = ar[0:8], ai[0:8]

            def fix(j, carry):
                pwr, pwi = carry
                xr = sre[tile(j), ls].reshape(NC // 8, 8, SCAN_LANES) + (pwr * e_r - pwi * e_i)
                xi = sim[tile(j), ls].reshape(NC // 8, 8, SCAN_LANES) + (pwr * e_i + pwi * e_r)
                sre[tile(j), ls] = xr.reshape(NC, SCAN_LANES)
                sim[tile(j), ls] = xi.reshape(NC, SCAN_LANES)
                return pwr * ar8 - pwi * ai8, pwr * ai8 + pwi * ar8

            _steps(L, fix, (ar8, ai8))

    return L, RB, nsq, rows, tile, forward_states


def ssm_fwd(u, wb, wc, lbr, lbi, dsk, plan):
    T = u.shape[0]
    L, RB, nsq, rows, tile, forward_states = _scan_body(T)
    nslab = W // LANE

    def body(u_ref, wb_ref, wc_ref, lbr_ref, lbi_ref, d_ref, y_ref, up_ref, xr_ref, xi_ref, sre, sim, ere, eim, yp):
        _to_scan_order(u_ref, up_ref)
        forward_states(up_ref, wb_ref, lbr_ref, lbi_ref, sre, sim, ere, eim)

        def cproj(i, carry):
            xr, xi = sre[rows(i), :].astype(bf16), sim[rows(i), :].astype(bf16)
            xr_ref[rows(i), :] = xr
            xi_ref[rows(i), :] = xi
            y = _dot(xr, wc_ref[0:SW, :]) + _dot(xi, wc_ref[SW:, :])
            yp[rows(i), :] = y + d_ref[...] * up_ref[rows(i), :]
            return carry

        lax.fori_loop(0, T // RB, cproj, 0)
        _to_time_order(yp, y_ref)

    slab = pl.BlockSpec((T, LANE), lambda k: (0, k))
    states = pl.BlockSpec((T, SW), lambda k: (0, k))
    return _call(
        body, [u, wb, wc, lbr, lbi, dsk], name="ssm_fwd", grid=(nslab,),
        in_specs=[slab, pl.BlockSpec((None, LANE, 2 * SW), lambda k: (k, 0, 0)),
                  pl.BlockSpec((None, 2 * SW, LANE), lambda k: (k, 0, 0)),
                  pl.BlockSpec((None, 1, SW), lambda k: (k, 0, 0)), pl.BlockSpec((None, 1, SW), lambda k: (k, 0, 0)),
                  pl.BlockSpec((None, 1, LANE), lambda k: (k, 0, 0))],
        out_specs=[slab, slab, states, states],
        out_shape=[S((T, W), f32), S((T, W), f32), S((T, nslab * SW), bf16), S((T, nslab * SW), bf16)],
        scratch=[pltpu.VMEM((T, SW), f32), pltpu.VMEM((T, SW), f32), pltpu.VMEM((NC, SW), f32), pltpu.VMEM((NC, SW), f32),
                 pltpu.VMEM((T, LANE), f32)],
        vmem=VMEM_LIMIT, plan=plan)


def ssm_bwd(u_p, dy, xr, xi, wbT, wcT, lbr, lbi, dsk, plan):
    T = u_p.shape[0]
    L, RB, nsq, rows, tile, _ = _scan_body(T)

    def body(u_ref, dyt_ref, sre, sim, wbT_ref, wcT_ref, lbr_ref, lbi_ref, d_ref,
             dut_ref, dwb_ref, dwc_ref, dlr_ref, dli_ref, dd_ref, su_ref, gre, gim, ere, eim, dy_ref, du_ref):
        _to_scan_order(dyt_ref, dy_ref)

        def dstate(i, carry):
            g = _dot(dy_ref[rows(i), :].astype(bf16), wcT_ref[...])
            gre[rows(i), :] = g[:, :SW]
            gim[rows(i), :] = g[:, SW:]
            return carry

        lax.fori_loop(0, T // RB, dstate, 0)
        row = lax.broadcasted_iota(jnp.int32, (NC, SCAN_LANES), 0)
        for lb in range(SW // SCAN_LANES):
            ls = slice(lb * SCAN_LANES, (lb + 1) * SCAN_LANES)
            ar = jnp.broadcast_to(lbr_ref[:, ls], (NC, SCAN_LANES))
            ai = jnp.broadcast_to(lbi_ref[:, ls], (NC, SCAN_LANES))

            def step(i, carry):
                gr, gi = carry
                j = L - 1 - i
                nr = ar * gr + ai * gi + gre[tile(j), ls]
                ni = ar * gi - ai * gr + gim[tile(j), ls]
                gre[tile(j), ls] = nr
                gim[tile(j), ls] = ni
                return nr, ni

            zero = jnp.zeros((NC, SCAN_LANES), f32)
            _steps(L, step, (zero, zero))
            pr, pi = lbr_ref[:, ls], -lbi_ref[:, ls]
            for _ in range(nsq):
                pr, pi = pr * pr - pi * pi, 2.0 * pr * pi
            er = jnp.zeros((1, SCAN_LANES), f32)
            ei = er
            ere[NC - 1:NC, ls] = er
            eim[NC - 1:NC, ls] = ei
            for c in range(NC - 2, -1, -1):
                lr_ = gre[c + 1:c + 2, ls]
                li_ = gim[c + 1:c + 2, ls]
                er, ei = lr_ + pr * er - pi * ei, li_ + pr * ei + pi * er
                ere[c:c + 1, ls] = er
                eim[c:c + 1, ls] = ei
            e_r, e_i = ere[:, ls].reshape(NC // 8, 8, SCAN_LANES), eim[:, ls].reshape(NC // 8, 8, SCAN_LANES)
            ar8, ai8 = ar[0:8], ai[0:8]

            def fixed(j, pwr, pwi):
                gr = (gre[tile(j), ls].reshape(NC // 8, 8, SCAN_LANES) + (pwr * e_r - pwi * e_i)).reshape(NC, SCAN_LANES)
                gi = (gim[tile(j), ls].reshape(NC // 8, 8, SCAN_LANES) + (pwr * e_i + pwi * e_r)).reshape(NC, SCAN_LANES)
                gre[tile(j), ls] = gr
                gim[tile(j), ls] = gi
                return gr, gi

            def fix(i, carry):
                pwr, pwi, accr, acci = carry
                j = L - 1 - i
                gr, gi = fixed(j, pwr, pwi)
                xr, xi = sre[tile(j - 1), ls].astype(f32), sim[tile(j - 1), ls].astype(f32)
                return (pwr * ar8 + pwi * ai8, pwi * ar8 - pwr * ai8,
                        accr + gr * xr + gi * xi, acci + gi * xr - gr * xi)

            pwr, pwi, accr, acci = _steps(L - 1, fix, (ar8, -ai8, zero, zero))
            gr, gi = fixed(0, pwr, pwi)
            xr = jnp.where(row == 0, 0.0, pltpu.roll(sre[tile(L - 1), ls].astype(f32), 1, axis=0))
            xi = jnp.where(row == 0, 0.0, pltpu.roll(sim[tile(L - 1), ls].astype(f32), 1, axis=0))
            accr = accr + gr * xr + gi * xi
            acci = acci + gi * xr - gr * xi
            dlr_ref[:, ls] = jnp.sum(accr, axis=0, keepdims=True)
            dli_ref[:, ls] = jnp.sum(acci, axis=0, keepdims=True)

        dwb_ref[...] = jnp.zeros_like(dwb_ref)
        dwc_ref[...] = jnp.zeros_like(dwc_ref)
        dd_ref[...] = jnp.zeros_like(dd_ref)
        su_ref[...] = jnp.zeros_like(su_ref)

        def finish(i, carry):
            u32, dy32 = u_ref[rows(i), :], dy_ref[rows(i), :]
            ub, dyb = u32.astype(bf16), dy32.astype(bf16)
            gr, gi = gre[rows(i), :].astype(bf16), gim[rows(i), :].astype(bf16)
            du = _dot(gr, wbT_ref[0:SW, :]) + _dot(gi, wbT_ref[SW:, :]) + dy32 * d_ref[...]
            du_ref[rows(i), :] = du
            su_ref[...] += jnp.sum(du, axis=0, keepdims=True)
            dwb_ref[:, 0:SW] += _dot_tn(ub, gr)
            dwb_ref[:, SW:] += _dot_tn(ub, gi)
            dwc_ref[:, 0:SW] += _dot_tn(dyb, sre[rows(i), :])
            dwc_ref[:, SW:] += _dot_tn(dyb, sim[rows(i), :])
            dd_ref[...] += jnp.sum(dy32 * u32, axis=0, keepdims=True)
            return carry

        lax.fori_loop(0, T // RB, finish, 0)
        _to_time_order(du_ref, dut_ref)

    slab = pl.BlockSpec((T, LANE), lambda k: (0, k))
    wide = pl.BlockSpec((None, LANE, 2 * SW), lambda k: (k, 0, 0))
    tall = pl.BlockSpec((None, 2 * SW, LANE), lambda k: (k, 0, 0))
    vec = pl.BlockSpec((None, 1, SW), lambda k: (k, 0, 0))
    vecd = pl.BlockSpec((None, 1, LANE), lambda k: (k, 0, 0))
    states = pl.BlockSpec((T, SW), lambda k: (0, k))
    nslab = W // LANE
    return _call(
        body, [u_p, dy, xr, xi, wbT, wcT, lbr, lbi, dsk], name="ssm_bwd", grid=(nslab,),
        in_specs=[slab, slab, states, states, tall, wide, vec, vec, vecd],
        out_specs=[slab, wide, wide, vec, vec, vecd, vecd],
        out_shape=[S((T, W), bf16), S((nslab, LANE, 2 * SW), f32), S((nslab, LANE, 2 * SW), f32),
                   S((nslab, 1, SW), f32), S((nslab, 1, SW), f32), S((nslab, 1, LANE), f32), S((nslab, 1, LANE), f32)],
        scratch=[pltpu.VMEM((T, SW), f32)] * 2 + [pltpu.VMEM((NC, SW), f32)] * 2 + [pltpu.VMEM((T, LANE), f32)] * 2,
        vmem=VMEM_LIMIT, plan=plan)


def glu_fwd(yn, glu_w, glu_b):
    T = yn.shape[0]
    tm = min(512, T)

    def body(y_ref, w_ref, b_ref, o_ref):
        g = _gelu(y_ref[...])
        o_ref[...] = (g * _sigmoid(_dot(g.astype(bf16), w_ref[...]) + b_ref[...])).astype(bf16)

    return pl.pallas_call(
        body, name="glu_fwd", grid=(T // tm,),
        in_specs=[pl.BlockSpec((tm, W), lambda i: (i, 0)), pl.BlockSpec((W, W), lambda i: (0, 0)), pl.BlockSpec((1, W), lambda i: (0, 0))],
        out_specs=pl.BlockSpec((tm, W), lambda i: (i, 0)), out_shape=S((T, W), bf16), compiler_params=_cp(("parallel",)),
    )(yn, glu_w, glu_b)


def _shift_rows(cur, prev8, k):
    return pltpu.roll(jnp.concatenate([prev8, cur], axis=0), k, axis=0)[8:]


def _lift_rows(cur, next8, k):
    n = cur.shape[0]
    return pltpu.roll(jnp.concatenate([cur, next8], axis=0), n + 8 - k, axis=0)[:n]


def conv_fwd(proj, conv_w):
    T = proj.shape[0]
    RB = min(512, T)

    def body(h_ref, c_ref, b_ref, w_ref, o_ref):
        w0, w1, w2 = w_ref[0:1, :], w_ref[1:2, :], w_ref[2:3, :]

        def blk(i, carry):
            r0 = pl.multiple_of(i * RB, RB)
            rs = pl.ds(r0, RB)
            ch = c_ref[rs, :] * h_ref[rs, :]
            pr = pl.ds(jnp.maximum(r0 - 8, 0), 8)
            prev = jnp.where(i > 0, c_ref[pr, :] * h_ref[pr, :], 0.0)
            z = w2 * ch + w1 * _shift_rows(ch, prev, 1) + w0 * _shift_rows(ch, prev, 2)
            o_ref[rs, :] = (b_ref[rs, :] * z).astype(bf16)
            return carry

        lax.fori_loop(0, T // RB, blk, 0)

    nb = W // LANE
    return pl.pallas_call(
        body, name="conv_fwd", grid=(nb,),
        in_specs=[pl.BlockSpec((T, LANE), lambda k: (0, 4 * nb + k)), pl.BlockSpec((T, LANE), lambda k: (0, 5 * nb + k)),
                  pl.BlockSpec((T, LANE), lambda k: (0, 6 * nb + k)),pl.BlockSpec((3, LANE), lambda k: (0, k))],
        out_specs=pl.BlockSpec((T, LANE), lambda k: (0, k)), out_shape=S((T, W), bf16),
        compiler_params=_cp(("parallel",), VMEM_LIMIT),
    )(proj, proj, proj, conv_w)


def _dense_columns(blocks_ref, dense_ref):
    for k in range(NDEV):
        dense_ref[:, k * LANE:(k + 1) * LANE] = blocks_ref[k]


def merge_fwd(ya, yb, wso, wco, proj, plan):
    T = ya.shape[0]
    tm = min(1024, T)

    def body(ya_ref, yb_ref, wa_ref, wb_ref, ga_ref, gb_ref, o_ref, wa_s, wb_s):
        @pl.when(pl.program_id(0) == 0)
        def _():
            _dense_columns(wa_ref, wa_s)
            _dense_columns(wb_ref, wb_s)

        o_ref[...] = (_sigmoid(ga_ref[...]) * _dot(ya_ref[...], wa_s[...])
                      + _sigmoid(gb_ref[...]) * _dot(yb_ref[...], wb_s[...])).astype(bf16)

    act = pl.BlockSpec((tm, W), lambda i: (i, 0))
    return _call(
        body, [ya, yb, wso, wco, proj, proj], name="merge_fwd", grid=(T // tm,),
        in_specs=[act, act, _resident((NDEV, W, LANE)), _resident((NDEV, W, LANE)),
                  pl.BlockSpec((tm, D), lambda i: (i, 0)), pl.BlockSpec((tm, D), lambda i: (i, 1))],
        out_specs=[pl.BlockSpec((tm, D), lambda i: (i, 0))], out_shape=[S((T, D), bf16)],
        scratch=[pltpu.VMEM((W, D), bf16), pltpu.VMEM((W, D), bf16)], vmem=VMEM_LIMIT, plan=plan)


def mix_ln1(merged, w_o, x, g1, b1, plan):
    T = x.shape[0]
    tm = min(512, T)

    def body(m_ref, w_ref, x_ref, g_ref, b_ref, r_ref, x1_ref):
        for rs in _row_parts(tm):
            r = ALPHA * x_ref[rs, :] + _dot(m_ref[rs, :], w_ref[...])
            r_ref[rs, :] = r
            xhat, _ = _ln_stats(r)
            x1_ref[rs, :] = (xhat * g_ref[...] + b_ref[...]).astype(bf16)

    row = pl.BlockSpec((tm, D), lambda i: (i, 0))
    vec = pl.BlockSpec((1, D), lambda i: (0, 0))
    return _call(
        body, [merged, w_o, x, g1, b1], name="mix_ln1", grid=(T // tm,),
        in_specs=[row, _resident((D, D)), row, vec, vec],
        out_specs=[row, row], out_shape=[S((T, D), f32), S((T, D), bf16)], sem=("parallel",), vmem=VMEM_LIMIT, plan=plan,
        relay_step=T // tm - 2)


FT = 256


def gate_up(x1b, wgT, wuT, plan):
    T = x1b.shape[0]
    tm = min(512, T)

    def body(x_ref, wg_ref, wu_ref, g_ref, u_ref, h_ref):
        x = x_ref[...]
        for n in range(F // FT):
            cs = slice(n * FT, (n + 1) * FT)
            g = _dot_nt(x, wg_ref[cs, :])
            u = _dot_nt(x, wu_ref[cs, :])
            g_ref[:, cs] = g.astype(bf16)
            u_ref[:, cs] = u.astype(bf16)
            h_ref[:, cs] = (g * _sigmoid(g) * u).astype(bf16)

    osp = pl.BlockSpec((tm, F), lambda i: (i, 0))
    return _call(
        body, [x1b, wgT, wuT], name="gate_up", grid=(T // tm,),
        in_specs=[pl.BlockSpec((tm, D), lambda i: (i, 0)), _resident((F, D)), _resident((F, D))],
        out_specs=[osp, osp, osp], out_shape=[S((T, F), bf16)] * 3, vmem=VMEM_LIMIT, plan=plan, relay_step=T // tm - 3)


def down_loss(hid, w_down, r1, g1, b1, g2, b2, target):
    T = hid.shape[0]
    tm = min(512, T)

    def body(h_ref, w_ref, r1_ref, g1_ref, b1_ref, g2_ref, b2_ref, t_ref, dr_ref, drb_ref, loss_ref, dg_ref, db_ref):
        @pl.when(pl.program_id(0) == 0)
        def _():
            loss_ref[...] = jnp.zeros_like(loss_ref)
            dg_ref[...] = jnp.zeros_like(dg_ref)
            db_ref[...] = jnp.zeros_like(db_ref)

        for rs in _row_parts(tm):
            xh1, _ = _ln_stats(r1_ref[rs, :])
            x1 = xh1 * g1_ref[...] + b1_ref[...]
            r2 = ALPHA * x1 + _dot(h_ref[rs, :], w_ref[...])
            xh2, rstd2 = _ln_stats(r2)
            err = xh2 * g2_ref[...] + b2_ref[...] - t_ref[rs, :]
            loss_ref[...] += jnp.sum(jnp.mean(err * err, axis=-1, keepdims=True), axis=0, keepdims=True)
            dy = err * (1.0 / D)
            dg_ref[...] += jnp.sum(dy * xh2, axis=0, keepdims=True)
            db_ref[...] += jnp.sum(dy, axis=0, keepdims=True)
            dr = _ln_bwd(dy, xh2, rstd2, g2_ref[...])
            dr_ref[rs, :] = dr
            drb_ref[rs, :] = dr.astype(bf16)

    row = pl.BlockSpec((tm, D), lambda i: (i, 0))
    vec = pl.BlockSpec((1, D), lambda i: (0, 0))
    return pl.pallas_call(
        body, name="down_loss", grid=(T // tm,),
        in_specs=[pl.BlockSpec((tm, F), lambda i: (i, 0)), _resident((F, D)), row, vec, vec, vec, vec, row],
        out_specs=[row, row, pl.BlockSpec((1, 1), lambda i: (0, 0)), vec, vec],
        out_shape=[S((T, D), f32), S((T, D), bf16), S((1, 1), f32), S((1, D), f32), S((1, D), f32)],
        compiler_params=_cp(("arbitrary",), VMEM_LIMIT),
    )(hid, w_down, r1, g1, b1, g2, b2, target)


def ffn_bwd_act(dffn, w_down, gate, up, plan):
    T = dffn.shape[0]
    tm = min(512, T)

    def body(d_ref, w_ref, g_ref, u_ref, dg_ref, du_ref):
        for n in range(F // FT):
            cs = slice(n * FT, (n + 1) * FT)
            for rs in _row_parts(tm):
                dh = _dot_nt(d_ref[rs, :], w_ref[cs, :])
                g, u = g_ref[rs, cs].astype(f32), u_ref[rs, cs].astype(f32)
                sg = _sigmoid(g)
                t = g * sg
                du_ref[rs, cs] = (dh * t).astype(bf16)
                dg_ref[rs, cs] = (dh * u * (sg + t - t * sg)).astype(bf16)

    osp = pl.BlockSpec((tm, F), lambda i: (i, 0))
    return _call(
        body, [dffn, w_down, gate, up], name="ffn_bwd_act", grid=(T // tm,),
        in_specs=[pl.BlockSpec((tm, D), lambda i: (i, 0)), _resident((F, D)), osp, osp],
        out_specs=[osp, osp], out_shape=[S((T, F), bf16)] * 2, sem=("parallel",), vmem=VMEM_LIMIT, plan=plan)


def ffn_bwd_x(dgate, dup, wgT, wuT, dr2, r1, g1, plan):
    T = dr2.shape[0]
    tm = min(512, T)

    def body(dg_ref, du_ref, wg_ref, wu_ref, dr2_ref, r1_ref, g1_ref, dr_ref, drb_ref, dgam_ref, dbet_ref):
        @pl.when(pl.program_id(0) == 0)
        def _():
            dgam_ref[...] = jnp.zeros_like(dgam_ref)
            dbet_ref[...] = jnp.zeros_like(dbet_ref)

        for rs in _row_parts(tm):
            dx1 = ALPHA * dr2_ref[rs, :] + _dot(dg_ref[rs, :], wg_ref[...]) + _dot(du_ref[rs, :], wu_ref[...])
            xh, rstd = _ln_stats(r1_ref[rs, :])
            dgam_ref[...] += jnp.sum(dx1 * xh, axis=0, keepdims=True)
            dbet_ref[...] += jnp.sum(dx1, axis=0, keepdims=True)
            dr = _ln_bwd(dx1, xh, rstd, g1_ref[...])
            dr_ref[rs, :] = dr
            drb_ref[rs, :] = dr.astype(bf16)

    row = pl.BlockSpec((tm, D), lambda i: (i, 0))
    wide = pl.BlockSpec((tm, F), lambda i: (i, 0))
    wsp = _resident((F, D))
    vec = pl.BlockSpec((1, D), lambda i: (0, 0))
    return _call(
        body, [dgate, dup, wgT, wuT, dr2, r1, g1], name="ffn_bwd_x", grid=(T // tm,),
        in_specs=[wide, wide, wsp, wsp, row, row, vec],
        out_specs=[row, row, vec, vec], out_shape=[S((T, D), f32), S((T, D), bf16), S((1, D), f32), S((1, D), f32)],
        vmem=VMEM_LIMIT, plan=plan)


def merge_bwd(dmix, w_o, merged, ya, yb, wso, wco, proj, plan):
    T = dmix.shape[0]
    tm = min(512, T)

    def body(dm_ref, wo_ref, m_ref, ya_ref, yb_ref, wa_ref, wb_ref, ga_ref, gb_ref,
             dya_ref, dyb_ref, dga_ref, dgb_ref, sa_ref, sb_ref, dwo_ref, wa_s, wb_s, acc):
        @pl.when(pl.program_id(0) == 0)
        def _():
            _dense_columns(wa_ref, wa_s)
            _dense_columns(wb_ref, wb_s)
            acc[...] = jnp.zeros_like(acc)

        acc[...] += _dot_tn(m_ref[...], dm_ref[...])

        @pl.when(pl.program_id(0) == pl.num_programs(0) - 1)
        def _():
            dwo_ref[...] = acc[...].astype(GRAD_DT)

        dmer = _dot_nt(dm_ref[...], wo_ref[...])
        sa, sb = _sigmoid(ga_ref[...]), _sigmoid(gb_ref[...])
        dya_ref[...] = (dmer * sa).astype(bf16)
        dyb_ref[...] = (dmer * sb).astype(bf16)
        dga = dmer * _dot(ya_ref[...], wa_s[...]) * sa * (1.0 - sa)
        dgb = dmer * _dot(yb_ref[...], wb_s[...]) * sb * (1.0 - sb)
        dga_ref[...] = dga.astype(bf16)
        dgb_ref[...] = dgb.astype(bf16)
        sa_ref[...] = jnp.sum(dga, axis=0, keepdims=True)
        sb_ref[...] = jnp.sum(dgb, axis=0, keepdims=True)

    act = pl.BlockSpec((tm, W), lambda i: (i, 0))
    osp = pl.BlockSpec((tm, D), lambda i: (i, 0))
    ssp = pl.BlockSpec((None, 1, D), lambda i: (i, 0, 0))
    return _call(
        body, [dmix, w_o, merged, ya, yb, wso, wco, proj, proj], name="merge_bwd", grid=(T // tm,),
        in_specs=[osp, _resident((D, D)), osp, act, act, _resident((NDEV, W, LANE)), _resident((NDEV, W, LANE)),
                  pl.BlockSpec((tm, D), lambda i: (i, 0)), pl.BlockSpec((tm, D), lambda i: (i, 1))],
        out_specs=[osp, osp, osp, osp, ssp, ssp, pl.BlockSpec((D, D), lambda i: (0, 0))],
        out_shape=[S((T, D), bf16)] * 4 + [S((T // tm, 1, D), f32)] * 2 + [S((D, D), GRAD_DT)],
        scratch=[pltpu.VMEM((W, D), bf16), pltpu.VMEM((W, D), bf16), pltpu.VMEM((D, D), f32)], vmem=VMEM_LIMIT, plan=plan)


def branches_bwd(dYA, dYB, ya, yb, wso, wco):
    T = dYA.shape[0]
    tm = min(1024, T)

    def body(da_ref, db_ref, ya_ref, yb_ref, wa_ref, wb_ref, oa_ref, ob_ref, ga_ref, gb_ref, wa_s, wb_s, acc_a, acc_b):
        @pl.when(pl.program_id(0) == 0)
        def _():
            _dense_columns(wa_ref, wa_s)
            _dense_columns(wb_ref, wb_s)
            acc_a[...] = jnp.zeros_like(acc_a)
            acc_b[...] = jnp.zeros_like(acc_b)

        oa_ref[...] = _dot_nt(da_ref[...], wa_s[...])
        ob_ref[...] = _dot_nt(db_ref[...], wb_s[...])
        acc_a[...] += _dot_tn(ya_ref[...], da_ref[...])
        acc_b[...] += _dot_tn(yb_ref[...], db_ref[...])

        @pl.when(pl.program_id(0) == pl.num_programs(0) - 1)
        def _():
            for k in range(NDEV):
                ga_ref[k] = acc_a[:, k * LANE:(k + 1) * LANE].astype(GRAD_DT)
                gb_ref[k] = acc_b[:, k * LANE:(k + 1) * LANE].astype(GRAD_DT)

    row = pl.BlockSpec((tm, D), lambda i: (i, 0))
    osp = pl.BlockSpec((tm, W), lambda i: (i, 0))
    blocks = pl.BlockSpec((NDEV, W, LANE), lambda i: (0, 0, 0))
    outs, _ = _call(
        body, [dYA, dYB, ya, yb, wso, wco], name="branches_bwd", grid=(T // tm,),
        in_specs=[row, row, osp, osp, _resident((NDEV, W, LANE)), _resident((NDEV, W, LANE))],
        out_specs=[osp, osp, blocks, blocks], out_shape=[S((T, W), f32)] * 2 + [S((NDEV, W, LANE), GRAD_DT)] * 2,
        scratch=[pltpu.VMEM((W, D), bf16)] * 2 + [pltpu.VMEM((W, D), f32)] * 2, sem=("arbitrary",), vmem=VMEM_LIMIT)
    return outs


def glu_bwd(yn, dya, glu_w, glu_b, plan):
    T = yn.shape[0]
    tm = min(512, T)

    def body(y_ref, d_ref, w_ref, b_ref, dy_ref, db_ref, dw_ref, acc):
        @pl.when(pl.program_id(0) == 0)
        def _():
            db_ref[...] = jnp.zeros_like(db_ref)
            acc[...] = jnp.zeros_like(acc)

        y, dya_ = y_ref[...], d_ref[...]
        g = _gelu(y)
        gb = g.astype(bf16)
        s = _sigmoid(_dot(gb, w_ref[...]) + b_ref[...])
        dsp = dya_ * g * s * (1.0 - s)
        dspb = dsp.astype(bf16)
        dg = dya_ * s + _dot_nt(dspb, w_ref[...])
        dy_ref[...] = dg * _gelu_grad(y)
        db_ref[...] += jnp.sum(dsp, axis=0, keepdims=True)
        acc[...] += _dot_tn(gb, dspb)

        @pl.when(pl.program_id(0) == pl.num_programs(0) - 1)
        def _():
            dw_ref[...] = acc[...].astype(GRAD_DT)

    row = pl.BlockSpec((tm, W), lambda i: (i, 0))
    vec = pl.BlockSpec((1, W), lambda i: (0, 0))
    mat = pl.BlockSpec((W, W), lambda i: (0, 0))
    return _call(
        body, [yn, dya, glu_w, glu_b], name="glu_bwd", grid=(T // tm,),
        in_specs=[row, row, mat, vec],
        out_specs=[row, vec, mat], out_shape=[S((T, W), f32), S((1, W), f32), S((W, W), GRAD_DT)],
        scratch=[pltpu.VMEM((W, W), f32)], sem=("arbitrary",), plan=plan)


def conv_bwd(proj, dyb, conv_w, plan):
    T = proj.shape[0]
    RB = min(512, T)
    nrb = T // RB

    def body(h_ref, c_ref, b_ref, d_ref, w_ref, dh_ref, dc_ref, db_ref, dw_ref, s_ref):
        w0, w1, w2 = w_ref[0:1, :], w_ref[1:2, :], w_ref[2:3, :]

        def blk(i, carry):
            a0, a1, a2, sh, sc, sb = carry
            r0 = pl.multiple_of(i * RB, RB)
            rs = pl.ds(r0, RB)
            h, cg, bg, dyb_ = h_ref[rs, :], c_ref[rs, :], b_ref[rs, :], d_ref[rs, :]
            ch = cg * h
            pr = pl.ds(jnp.maximum(r0 - 8, 0), 8)
            prev = jnp.where(i > 0, c_ref[pr, :] * h_ref[pr, :], 0.0)
            ch1, ch2 = _shift_rows(ch, prev, 1), _shift_rows(ch, prev, 2)
            dbg = dyb_ * (w2 * ch + w1 * ch1 + w0 * ch2)
            db_ref[rs, :] = dbg.astype(bf16)
            dz = dyb_ * bg
            nx = pl.ds(jnp.minimum(r0 + RB, T - 8), 8)
            nxt = jnp.where(i < nrb - 1, d_ref[nx, :] * b_ref[nx, :], 0.0)
            dch = w2 * dz + w1 * _lift_rows(dz, nxt, 1) + w0 * _lift_rows(dz, nxt, 2)
            dcg, dh = dch * h, dch * cg
            dc_ref[rs, :] = dcg.astype(bf16)
            dh_ref[rs, :] = dh.astype(bf16)
            col = lambda v: jnp.sum(v, axis=0, keepdims=True)
            return (a0 + col(dz * ch2), a1 + col(dz * ch1), a2 + col(dz * ch), sh + col(dh), sc + col(dcg), sb + col(dbg))

        zero = jnp.zeros((1, LANE), f32)
        a0, a1, a2, sh, sc, sb = lax.fori_loop(0, nrb, blk, (zero,) * 6)
        dw_ref[0:1, :] = a0
        dw_ref[1:2, :] = a1
        dw_ref[2:3, :] = a2
        s_ref[0:1, :] = sh
        s_ref[1:2, :] = sc
        s_ref[2:3, :] = sb

    nb = W // LANE
    slab = pl.BlockSpec((T, LANE), lambda k: (0, k))
    three = pl.BlockSpec((3, LANE), lambda k: (0, k))
    return _call(
        body, [proj, proj, proj, dyb, conv_w], name="conv_bwd", grid=(nb,),
        in_specs=[pl.BlockSpec((T, LANE), lambda k: (0, 4 * nb + k)), pl.BlockSpec((T, LANE), lambda k: (0, 5 * nb + k)),
                  pl.BlockSpec((T, LANE), lambda k: (0, 6 * nb + k)), slab, three],
        out_specs=[slab, slab, slab, three, three],
        out_shape=[S((T, W), bf16)] * 3 + [S((3, W), f32)] * 2, sem=("parallel",), vmem=VMEM_LIMIT, plan=plan)


def in_proj_bwd_x(parts, win_g, base, scale, name, plan=None):
    T = base.shape[0]
    tm = min(512, T)
    n = len(parts)

    def body(*refs):
        p_refs, w_ref, b_ref, o_ref = refs[:n], refs[n], refs[n + 1], refs[n + 2]
        acc = scale * b_ref[...]
        for p_ref, (_, _, k) in zip(p_refs, parts):
            acc += _dot_nt(p_ref[...], w_ref[k])
        o_ref[...] = acc

    row = pl.BlockSpec((tm, D), lambda i: (i, 0))
    p_specs = [pl.BlockSpec((tm, W), (lambda i, cb=cb: (i, cb))) for _, cb, _ in parts]
    return _call(
        body, [a for a, _, _ in parts] + [win_g, base], name=name, grid=(T // tm,),
        in_specs=p_specs + [_resident((NDEV, D, W)), row],
        out_specs=[row], out_shape=[S((T, D), f32)], vmem=VMEM_LIMIT, plan=plan)


def ssm_param_bwd(lam_re, lam_im, log_dt, fr, fi, br, bi, dwb, dwcT, dlbr, dlbi):
    def body(lr_ref, li_ref, ldt_ref, fr_ref, fi_ref, br_ref, bi_ref, dwb_ref, dwc_ref, dlbr_ref, dlbi_ref,
             dbr_ref, dbi_ref, dlr_ref, dli_ref, dldt_ref, dcr_ref, dci_ref, dr_s, di_s):
        for k in range(W // LANE):
            for gl in range(NG // (W // LANE)):
                rows, src = slice((8 * k + gl) * GC, (8 * k + gl + 1) * GC), slice(gl * GC, (gl + 1) * GC)
                re, im = slice(gl * NP, (gl + 1) * NP), slice(SW + gl * NP, SW + (gl + 1) * NP)
                dr_s[rows, :] = dwb_ref[k, src, re]
                di_s[rows, :] = dwb_ref[k, src, im]
                dcr_ref[rows, :] = dwc_ref[k, src, re]
                dci_ref[rows, :] = -dwc_ref[k, src, im]
        fr_, fi_ = _per_channel(fr_ref[...]), _per_channel(fi_ref[...])
        br_, bi_, dr, di = br_ref[...], bi_ref[...], dr_s[...], di_s[...]
        dbr_ref[...] = fr_ * dr + fi_ * di
        dbi_ref[...] = fr_ * di - fi_ * dr
        dfr = jnp.sum((dr * br_ + di * bi_).reshape(NG, GC, NP), axis=1)
        dfi = jnp.sum((di * br_ - dr * bi_).reshape(NG, GC, NP), axis=1)
        _, vjp = jax.vjp(_disc, lr_ref[...], li_ref[...], ldt_ref[...])
        dlr_ref[...], dli_ref[...], dldt = vjp((dlbr_ref[...], dlbi_ref[...], dfr, dfi))
        dldt_ref[...] = _transpose_exact(dldt)

    blk = S((NG * GC, NP), f32)
    return pl.pallas_call(
        body, name="ssm_param_bwd", out_shape=[blk, blk, S((NG, NP), f32), S((NG, NP), f32), S((1, NG), f32), blk, blk],
        scratch_shapes=[pltpu.VMEM((NG * GC, NP), f32)] * 2)(
        lam_re, lam_im, log_dt, fr, fi, br, bi, dwb, dwcT, dlbr, dlbi)


def _adam(w, g, m, v):
    m = ADAM_B1 * m + (1.0 - ADAM_B1) * g
    v = ADAM_B2 * v + (1.0 - ADAM_B2) * (g * g)
    m_hat = m / (1.0 - ADAM_B1 ** ADAM_STEP)
    v_hat = v / (1.0 - ADAM_B2 ** ADAM_STEP)
    return -ADAM_LR * (m_hat / (jnp.sqrt(v_hat) + ADAM_EPS) + ADAM_WD * w), m, v


def _sum_in_order(c_ref):
    g = c_ref[0].astype(f32)
    for k in range(1, c_ref.shape[0]):
        g = g + c_ref[k].astype(f32)
    return g


def sum_blocks(contrib, name):
    def body(c_ref, o_ref):
        o_ref[...] = _sum_in_order(c_ref)

    return pl.pallas_call(body, name=name, out_shape=S(contrib.shape[1:], f32))(contrib)


def adam_update(w, m, v, contrib, name, rows_per_block=None, summed_on_0=None, plan=None):
    R, C = w.shape
    n = contrib.shape[0]
    tr = min(rows_per_block or R, R)

    def body(w_ref, m_ref, v_ref, c_ref, *refs):
        g_ref, d_ref, nm_ref, nv_ref = refs[-4:]
        g = _sum_in_order(c_ref)
        if summed_on_0 is not None:
            x, y, c = _coords()
            g = jnp.where(4 * x + 2 * y + c == 0, refs[0][...], g)
        g_ref[...] = g
        d_ref[...], nm_ref[...], nv_ref[...] = _adam(w_ref[...], g, m_ref[...], v_ref[...])

    blk = pl.BlockSpec((tr, C), lambda i: (i, 0))
    extra = [] if summed_on_0 is None else [summed_on_0]
    return _call(
        body, [w, m, v, contrib] + extra, name=name, grid=(R // tr,),
        in_specs=[blk, blk, blk, pl.BlockSpec((n, tr, C), lambda i: (0, i, 0))] + [blk] * len(extra),
        out_specs=[blk] * 4, out_shape=[S((R, C), f32)] * 4, sem=("parallel",), vmem=VMEM_LIMIT, plan=plan)


_ROWVEC = (("b_in", IN_COLS), ("ssm_d", W), ("glu_b", W), ("ln1_g", D), ("ln1_b", D), ("ln2_g", D), ("ln2_b", D))
_HALF = NG * GC // 2
_BC_LANE = {"ssm_b_re": 0, "ssm_b_im": NP, "ssm_c_re": 0, "ssm_c_im": NP}
_PACK = {}
_r = 0
for _n, _k in _ROWVEC:
    _PACK[_n] = _r
    _r += _k // LANE
for _n, _rows in (("ssm_lambda", NG), ("scalars", 8), ("ssm_b", _HALF), ("ssm_c", _HALF), ("conv_w", 16)):
    _PACK[_n] = _r
    _r += _rows
for _n in _BC_LANE:
    _PACK[_n] = _PACK[_n[:5]]
PACK_ROWS = _r
assert PACK_ROWS % 8 == 0
_SMALL = ("b_in", "ssm_lambda_re", "ssm_lambda_im", "ssm_log_dt", "ssm_b_re", "ssm_b_im", "ssm_c_re", "ssm_c_im",
          "ssm_d", "glu_b", "ln1_g", "ln1_b", "ln2_g", "ln2_b")


def pack_grads(su, shcb, sga, sgb, dd, dglu_b, dln1_g, dln1_b, dln2_g, dln2_b, dlam_re, dlam_im, dldt, sqerr, dbr, dbi,
               dc_re, dc_im, dconv):
    nI = sga.shape[0]

    def body(su_ref, sh_ref, sga_ref, sgb_ref, dd_ref, gb_ref, l1g_ref, l1b_ref, l2g_ref, l2b_ref, lr_ref, li_ref, dt_ref,
             sq_ref, br_ref, bi_ref, cr_ref, ci_ref, cw_ref, o_ref):
        o_ref[...] = jnp.zeros_like(o_ref)

        def put_row(name, v):
            r0 = _PACK[name]
            for i in range(v.shape[1] // LANE):
                o_ref[r0 + i:r0 + i + 1, :] = v[:, i * LANE:(i + 1) * LANE]

        ga, gb = sga_ref[0], sgb_ref[0]
        for i in range(1, nI):
            ga, gb = ga + sga_ref[i], gb + sgb_ref[i]
        put_row("b_in", jnp.concatenate([su_ref[k] for k in range(W // LANE)]
                                        + [sh_ref[0:1, :], sh_ref[1:2, :], sh_ref[2:3, :], ga, gb], axis=1))
        put_row("ssm_d", jnp.concatenate([dd_ref[k] for k in range(W // LANE)], axis=1))
        put_row("glu_b", gb_ref[...])
        put_row("ln1_g", l1g_ref[...])
        put_row("ln1_b", l1b_ref[...])
        put_row("ln2_g", l2g_ref[...])
        put_row("ln2_b", l2b_ref[...])
        r0 = _PACK["ssm_lambda"]
        o_ref[r0:r0 + NG, 0:NP] = lr_ref[...]
        o_ref[r0:r0 + NG, NP:2 * NP] = li_ref[...]
        r0 = _PACK["scalars"]
        o_ref[r0:r0 + 1, 0:NG] = dt_ref[...]
        o_ref[r0 + 1:r0 + 2, 0:1] = sq_ref[...]
        for name, ref in (("ssm_b_re", br_ref), ("ssm_b_im", bi_ref), ("ssm_c_re", cr_ref), ("ssm_c_im", ci_ref)):
            r0, l0 = _PACK[name], _BC_LANE[name]
            o_ref[r0:r0 + _HALF, l0:l0 + NP] = pltpu.bitcast(ref[...].astype(bf16), f32)
        for cb in range(W // LANE):
            o_ref[_PACK["conv_w"] + 3 * cb:_PACK["conv_w"] + 3 * cb + 3, :] = cw_ref[:, cb * LANE:(cb + 1) * LANE]

    return pl.pallas_call(body, name="pack_grads", out_shape=S((PACK_ROWS, LANE), f32))(
        su, shcb, sga, sgb, dd, dglu_b, dln1_g, dln1_b, dln2_g, dln2_b, dlam_re, dlam_im, dldt, sqerr, dbr, dbi, dc_re, dc_im,
        dconv)


def adam_small(packed_all, params):
    names = list(_SMALL) + ["conv_w"]
    flat = [a for n in names for a in params[n]]

    def body(*refs):
        p_ref = refs[0]
        ins = refs[1:1 + 3 * len(names)]
        outs = refs[1 + 3 * len(names):-2]
        loss_ref, g_ref = refs[-2], refs[-1]

        def part(k, rs=slice(None), ls=slice(None)):
            return p_ref[k, rs, ls]

        g_all = part(0)
        for k in range(1, NDEV):
            g_all = g_all + part(k)
        g_ref[...] = g_all

        def rows(name, r0, n, l0=0, lanes=LANE):
            return g_ref[_PACK[name] + r0:_PACK[name] + r0 + n, l0:l0 + lanes]

        def grad_of(name):
            if name in dict(_ROWVEC):
                return jnp.concatenate([rows(name, i, 1) for i in range(dict(_ROWVEC)[name] // LANE)], axis=1)
            if name in ("ssm_lambda_re", "ssm_lambda_im"):
                return rows("ssm_lambda", 0, NG, NP * (name == "ssm_lambda_im"), NP)[None]
            if name == "ssm_log_dt":
                return rows("scalars", 0, 1, 0, NG)
            if name in _BC_LANE:
                rs, ls = slice(_PACK[name], _PACK[name] + _HALF), slice(_BC_LANE[name], _BC_LANE[name] + NP)
                g = pltpu.bitcast(part(0, rs, ls), bf16).astype(f32)
                for k in range(1, NDEV):
                    g = g + pltpu.bitcast(part(k, rs, ls), bf16).astype(f32)
                return g.reshape(1, NG, GC, NP)
            full = jnp.concatenate([rows("conv_w", 3 * cb, 3) for cb in range(W // LANE)], axis=1)
            x, y, c = _coords()
            col0 = (4 * x + 2 * y + c) * (W // NDEV)
            sel = (lax.broadcasted_iota(jnp.int32, (W, W // NDEV), 0)
                   == lax.broadcasted_iota(jnp.int32, (W, W // NDEV), 1) + col0).astype(f32)
            return jnp.dot(full, sel, precision=HIGHEST, preferred_element_type=f32)[None]

        loss_ref[...] = 0.5 * rows("scalars", 1, 1, 0, 1)
        for i, name in enumerate(names):
            w_ref, m_ref, v_ref = ins[3 * i:3 * i + 3]
            g = grad_of(name)
            d, m, v = _adam(w_ref[...], g, m_ref[...], v_ref[...])
            outs[4 * i][...] = g
            outs[4 * i + 1][...] = d
            outs[4 * i + 2][...] = m
            outs[4 * i + 3][...] = v

    out_shape = [S(params[n][0].shape, f32) for n in names for _ in range(4)] + [S((1, 1), f32)]
    res = pl.pallas_call(body, name="adam_small", out_shape=out_shape, scratch_shapes=[pltpu.VMEM((PACK_ROWS, LANE), f32)],
                         compiler_params=_cp(None, VMEM_LIMIT))(packed_all, *flat)
    return {n: res[4 * i:4 * i + 4] for i, n in enumerate(names)}, res[-1]


def _block_diag(wgt):
    eye = jnp.eye(8, dtype=wgt.dtype)
    out = wgt[:, :, :, None, :] * eye[None, :, None, :, None]
    return out.reshape(4, 8 * wgt.shape[2], 8 * wgt.shape[3])


def kernel(x, w_in, b_in, ssm_lambda_re, ssm_lambda_im, ssm_log_dt, ssm_b_re, ssm_b_im, ssm_c_re, ssm_c_im, ssm_d, glu_w, glu_b, w_ssm_out, conv_w, w_conv_out, w_o, ln1_g, ln1_b, w_gate, w_up, w_down, ln2_g, ln2_b, loss_target, m_w_in, m_b_in, m_ssm_lambda_re, m_ssm_lambda_im, m_ssm_log_dt, m_ssm_b_re, m_ssm_b_im, m_ssm_c_re, m_ssm_c_im, m_ssm_d, m_glu_w, m_glu_b, m_w_ssm_out, m_conv_w, m_w_conv_out, m_w_o, m_ln1_g, m_ln1_b, m_w_gate, m_w_up, m_w_down, m_ln2_g, m_ln2_b, v_w_in, v_b_in, v_ssm_lambda_re, v_ssm_lambda_im, v_ssm_log_dt, v_ssm_b_re, v_ssm_b_im, v_ssm_c_re, v_ssm_c_im, v_ssm_d, v_glu_w, v_glu_b, v_w_ssm_out, v_conv_w, v_w_conv_out, v_w_o, v_ln1_g, v_ln1_b, v_w_gate, v_w_up, v_w_down, v_ln2_g, v_ln2_b):
    given = dict(locals())
    xs = x[0]
    target = loss_target[0]

    tr = lambda a: jnp.swapaxes(a[0], 0, 1)
    win_s, glu_s, wso_s, wco_s, wo_s, wgT_s, wuT_s, wd_s = prep_weights(
        [w_in[0], glu_w[0], w_ssm_out[0], w_conv_out[0], w_o[0], tr(w_gate), tr(w_up), w_down[0]])
    (win_g,) = run_plan(GatherPlan([win_s], srcs=(0,)), "gather_w_in_u")

    lam_re, lam_im = ssm_lambda_re[0], ssm_lambda_im[0]
    ldt = ssm_log_dt[0].reshape(NG, 1)
    br2 = jnp.swapaxes(ssm_b_re[0], 1, 2).reshape(NG * GC, NP)
    bi2 = jnp.swapaxes(ssm_b_im[0], 1, 2).reshape(NG * GC, NP)
    lbr, lbi, fr, fi, bbr, bbi = ssm_params(lam_re, lam_im, ldt, br2, bi2)
    bb_t = lambda b: b.reshape(4, 8, GC, NP)
    wb = jnp.concatenate([_block_diag(bb_t(bbr)), _block_diag(bb_t(bbi))], axis=2)
    c_t = lambda c: c.reshape(4, 8, GC, NP).transpose(0, 1, 3, 2)
    wc = jnp.concatenate([_block_diag(c_t(ssm_c_re[0])), -_block_diag(c_t(ssm_c_im[0]))], axis=1)
    wbT, wcT = wb.transpose(0, 2, 1), wc.transpose(0, 2, 1)
    wb, wc, wbT, wcT = wb.astype(bf16), wc.astype(bf16), wbT.astype(bf16), wcT.astype(bf16)
    lbr_s, lbi_s = lbr.reshape(4, 1, SW), lbi.reshape(4, 1, SW)
    dsk = ssm_d[0].reshape(4, 1, LANE)

    u_nat, xb = in_proj_u(xs, win_g, b_in)
    half_a, half_b = (0, 3, 5, 6), (1, 2, 4, 7)
    (yn, u_p, xr_p, xi_p), (win_g, conv_g, glu_g, wso_g, wuT_g) = ssm_fwd(
        u_nat, wb, wc, lbr_s, lbi_s, dsk,
        Plans([GatherPlan([win_s], srcs=tuple(range(1, NDEV)), into=[win_g]), GatherPlan([conv_w[0], glu_s, wso_s]),
               GatherPlan([wuT_s], srcs=half_a)]))
    conv_f = conv_g.transpose(1, 0, 2).reshape(3, W)
    (proj,), (wco_g, wo_g, wgT_g) = in_proj_rest(
        xb, win_g, b_in, Plans([GatherPlan([wco_s, wo_s]), GatherPlan([wgT_s], srcs=half_a)]))
    glu_f, wo_f = glu_g.reshape(W, W), wo_g.reshape(D, D)
    ya = glu_fwd(yn, glu_f, glu_b)
    yb = conv_fwd(proj, conv_f)
    (merged,), (wgT_g,) = merge_fwd(ya, yb, wso_g, wco_g, proj, GatherPlan([wgT_s], srcs=half_b, into=[wgT_g]))
    (r1, x1b), (wuT_g,) = mix_ln1(merged, wo_f, xs, ln1_g, ln1_b, GatherPlan([wuT_s], srcs=half_b, into=[wuT_g]))
    wgT, wuT = wgT_g.reshape(F, D), wuT_g.reshape(F, D)
    (gate, up, hid), (wd_g,) = gate_up(x1b, wgT, wuT, GatherPlan([wd_s]))
    wd_f = wd_g.reshape(F, D)
    dr2, dffn, sqerr, dln2_g, dln2_b = down_loss(hid, wd_f, r1, ln1_g, ln1_b, ln2_g, ln2_b, target)

    dwd, _ = mm_tn_rows(hid, dffn, "grad_w_down")
    dwd = dwd.reshape(NDEV, FS, D)
    (dgate, dup), (r_wd,) = ffn_bwd_act(dffn, wd_f, gate, up, ScatterPlan([dwd], only=half_a))
    dwgT, (r_wd,) = mm_tn_rows(dgate, x1b, "grad_w_gate", plan=ScatterPlan([dwd], only=half_b, into=[r_wd]))
    dwgT = dwgT.reshape(NDEV, FS, D)
    dwuT, (r_wgT,) = mm_tn_rows(dup, x1b, "grad_w_up", plan=ScatterPlan([dwgT], only=half_a))
    dwuT = dwuT.reshape(NDEV, FS, D)
    (dr1, dmix, dln1_g, dln1_b), (r_wgT, r_wuT) = ffn_bwd_x(
        dgate, dup, wgT, wuT, dr2, r1, ln1_g,
        Plans([ScatterPlan([dwgT], only=half_b, into=[r_wgT]), ScatterPlan([dwuT], only=half_a)]))
    (dYA, dYB, dga, dgb, sga, sgb, dwo), (r_wuT,) = merge_bwd(dmix, wo_f, merged, ya, yb, wso_g, wco_g, proj,
                                                              ScatterPlan([dwuT], only=half_b, into=[r_wuT]))
    dwo = dwo.reshape(NDEV, D // NDEV, D)
    dya, dyb, dwso, dwco = branches_bwd(dYA, dYB, ya, yb, wso_g, wco_g)
    (dyn, dglu_b, dglu), (r_wso,) = glu_bwd(yn, dya, glu_f, glu_b, ScatterPlan([dwso]))
    dglu = dglu.reshape(NDEV, W // NDEV, W)
    (dh, dcg, dbg, dconv, shcb), (r_wco,) = conv_bwd(proj, dyb, conv_f, ScatterPlan([dwco]))
    dwin, (r_wo, r_glu) = grad_w_in_rest(xb, dh, dcg, dbg, dga, dgb, ScatterPlan([dwo, dglu]))
    (du, dwb, dwcT, dlbr_s, dlbi_s, dd, su), (r_win,) = ssm_bwd(
        u_p, dyn, xr_p, xi_p, wbT, wcT, lbr_s, lbi_s, dsk, ScatterPlan([dwin], only=tuple(range(1, NDEV))))

    dbr2, dbi2, dlam_re, dlam_im, dldt, dc_re, dc_im = ssm_param_bwd(
        lam_re, lam_im, ldt, fr, fi, br2, bi2, dwb, dwcT, dlbr_s.reshape(NG, NP), dlbi_s.reshape(NG, NP))
    packed = pack_grads(su, shcb, sga, sgb, dd, dglu_b, dln1_g, dln1_b, dln2_g, dln2_b, dlam_re, dlam_im, dldt, sqerr,
                        dbr2, dbi2, dc_re, dc_im, dconv)
    dwin_u = mm_tn(xb, du, "grad_w_in_u").reshape(NDEV, D // NDEV, W)

    rest = [(dh, 0, 1), (dcg, 0, 2), (dbg, 0, 3), (dga, 0, 4), (dga, 1, 5), (dgb, 0, 6), (dgb, 1, 7)]
    (gx_rest,), (r_win_u, small_all) = in_proj_bwd_x(
        rest, win_g, dr1, ALPHA, "in_proj_bwd_x_rest", Plans([ScatterPlan([dwin_u]), GatherPlan([packed])]))
    my_rows = sum_blocks(r_win_u, "sum_w_in_u")

    out = {}

    def put(name, res, back=lambda a: a[None]):
        out["grad_" + name], out["delta_" + name], out["new_m_" + name], out["new_v_" + name] = [back(r) for r in res]

    res_wd, (win_u_sum,) = adam_update(w_down[0], m_w_down[0], v_w_down[0], r_wd, "adam_w_down", 176,
                                       plan=ScatterPlan([my_rows], only=(0,), whole=True))
    put("w_down", res_wd)
    (grad_x,), _ = in_proj_bwd_x([(du, 0, 0)], win_g, gx_rest, 1.0, "in_proj_bwd_x_u")
    put("w_in", adam_update(w_in[0], m_w_in[0], v_w_in[0], r_win, "adam_w_in", 256,
                            summed_on_0=win_u_sum.reshape(D, W))[0])
    put("glu_w", adam_update(glu_w[0], m_glu_w[0], v_glu_w[0], r_glu, "adam_glu_w")[0])
    put("w_ssm_out", adam_update(w_ssm_out[0], m_w_ssm_out[0], v_w_ssm_out[0], r_wso, "adam_w_ssm_out")[0])
    put("w_conv_out", adam_update(w_conv_out[0], m_w_conv_out[0], v_w_conv_out[0], r_wco, "adam_w_conv_out")[0])
    put("w_o", adam_update(w_o[0], m_w_o[0], v_w_o[0], r_wo, "adam_w_o")[0])
    untr = lambda a: jnp.swapaxes(a, 0, 1)[None]
    put("w_gate", adam_update(tr(w_gate), tr(m_w_gate), tr(v_w_gate), r_wgT, "adam_w_gate", 176)[0], untr)
    put("w_up", adam_update(tr(w_up), tr(m_w_up), tr(v_w_up), r_wuT, "adam_w_up", 176)[0], untr)
    as_c = lambda a: jnp.swapaxes(a, 2, 3)
    params = {n: (given[n], given["m_" + n], given["v_" + n]) for n in list(_SMALL) + ["conv_w"]}
    for n in ("ssm_b_re", "ssm_b_im"):
        params[n] = tuple(as_c(a) for a in params[n])
    small, loss = adam_small(small_all, params)
    for n, res in small.items():
        put(n, res, as_c if n in ("ssm_b_re", "ssm_b_im") else (lambda a: a))

    names = ["w_in", "b_in", "ssm_lambda_re", "ssm_lambda_im", "ssm_log_dt", "ssm_b_re", "ssm_b_im", "ssm_c_re", "ssm_c_im",
             "ssm_d", "glu_w", "glu_b", "w_ssm_out", "conv_w", "w_conv_out", "w_o", "ln1_g", "ln1_b", "w_gate", "w_up",
             "w_down", "ln2_g", "ln2_b"]
    return (loss.reshape(()), grad_x[None], *[out[p + n] for p in ("grad_", "delta_", "new_m_", "new_v_") for n in names])
```

```python
import functools
import math

import jax
import jax.numpy as jnp
from jax import lax
from jax.experimental import pallas as pl
from jax.experimental.pallas import tpu as pltpu

f32, bf16 = jnp.float32, jnp.bfloat16
S = jax.ShapeDtypeStruct
MESH = pl.DeviceIdType.MESH
HIGHEST = lax.Precision.HIGHEST

D = 1024
W = 512
NG, NP, GC = 32, 64, 16
F = 2816
NDEV = 8
FS = F // NDEV
IN_COLS = 8 * W
ALPHA = 2.0 ** 0.25
LN_EPS = 1e-5
ADAM_LR, ADAM_B1, ADAM_B2, ADAM_EPS, ADAM_WD, ADAM_STEP = 0.001, 0.9, 0.999, 1e-08, 0.01, 10
NC = 32
LANE = 128
SW = 4 * LANE
VMEM_LIMIT = 56 * 1024 * 1024
GRAD_DT = bf16
ANY = pl.BlockSpec(memory_space=pl.ANY)


def _cp(sem=None, vmem=None):
    return pltpu.CompilerParams(dimension_semantics=sem, vmem_limit_bytes=vmem)


def _resident(shape):
    return pl.BlockSpec(shape, lambda i: (0,) * len(shape), pipeline_mode=pl.Buffered(1))


def _dot(a, b):
    return jnp.dot(a, b, preferred_element_type=f32)


def _dot_nt(a, b):
    return lax.dot_general(a, b, (((1,), (1,)), ((), ())), preferred_element_type=f32)


def _dot_tn(a, b):
    return lax.dot_general(a, b, (((0,), (0,)), ((), ())), preferred_element_type=f32)


def _eye(n):
    return (lax.broadcasted_iota(jnp.int32, (n, n), 0) == lax.broadcasted_iota(jnp.int32, (n, n), 1)).astype(f32)


def _transpose_exact(a):
    return lax.dot_general(a, _eye(a.shape[0]), (((0,), (0,)), ((), ())), precision=HIGHEST, preferred_element_type=f32)


def _sigmoid(x):
    return 1.0 / (1.0 + jnp.exp(-x))


_GK = math.sqrt(2.0 / math.pi)


def _gelu(x):
    return 0.5 * x * (1.0 + jnp.tanh(_GK * (x + 0.044715 * x * x * x)))


def _gelu_grad(x):
    th = jnp.tanh(_GK * (x + 0.044715 * x * x * x))
    return 0.5 * (1.0 + th) + 0.5 * x * (1.0 - th * th) * _GK * (1.0 + 3.0 * 0.044715 * x * x)


ROW_PART = 256


def _row_parts(tm):
    return [slice(r, r + min(ROW_PART, tm)) for r in range(0, tm, min(ROW_PART, tm))]


def _ln_stats(r):
    mu = jnp.mean(r, axis=-1, keepdims=True)
    xc = r - mu
    var = jnp.mean(xc * xc, axis=-1, keepdims=True)
    rstd = lax.rsqrt(var + LN_EPS)
    return xc * rstd, rstd


def _ln_bwd(dy, xhat, rstd, g):
    dxh = dy * g
    m1 = jnp.mean(dxh, axis=-1, keepdims=True)
    m2 = jnp.mean(dxh * xhat, axis=-1, keepdims=True)
    return rstd * (dxh - m1 - xhat * m2)


def _coords():
    return lax.axis_index("x"), lax.axis_index("y"), lax.axis_index("c")


def _when(cond, fn):
    if cond is True:
        fn()
    else:
        pl.when(cond)(fn)


class GatherPlan:
    aliases = ()

    def __init__(self, arrs, srcs=None, into=None):
        n = self.n = len(arrs)
        self.srcs = srcs
        self.inputs = list(arrs) + list(into or [])
        if into:
            self.aliases = tuple((n + a, a) for a in range(n))
        self.out_shape = [S((NDEV,) + a.shape, a.dtype) for a in arrs]
        self.sems = [pltpu.SemaphoreType.DMA((n, 7)), pltpu.SemaphoreType.DMA((n, 7)), pltpu.SemaphoreType.DMA((n,))]

    def _has(self, dev):
        if self.srcs is None:
            return True
        idx = 4 * dev[0] + 2 * dev[1] + dev[2]
        return functools.reduce(jnp.logical_or, [idx == s for s in self.srcs])

    def _parts(self, ins, outs, sems):
        n = self.n
        send_sems, recv_sems, loc_sems = sems
        x, y, c = _coords()
        me, sib = (x, y, c), (x, y, 1 - c)
        chips = [(1 - x, y), (x, 1 - y), (1 - x, 1 - y)]

        def slot(a, dev):
            return outs[a].at[4 * dev[0] + 2 * dev[1] + dev[2]]

        def copy(a, k, block, to, src=None):
            return pltpu.make_async_remote_copy(
                src_ref=slot(a, block) if src is None else src, dst_ref=slot(a, block),
                send_sem=send_sems.at[a, k], recv_sem=recv_sems.at[a, k], device_id=to, device_id_type=MESH)

        each = [(j, chip, a) for j, chip in enumerate(chips) for a in range(n)]
        own = self._has(me)
        return dict(
            mine=lambda: [(pltpu.make_async_copy(ins[a], slot(a, me), loc_sems.at[a]), own) for a in range(n)],
            first=lambda: ([(copy(a, 0, me, sib, src=ins[a]), own) for a in range(n)]
                           + [(copy(a, 1 + j, me, (*chip, c), src=ins[a]), own) for j, chip, a in each]),
            landed=lambda: [(copy(a, 1 + j, (*chip, c), me), self._has((*chip, c))) for j, chip, a in each],
            passed=lambda: [(copy(a, 4 + j, (*chip, c), sib), self._has((*chip, c))) for j, chip, a in each],
            from_sib=lambda: ([(copy(a, 0, sib, me), self._has(sib)) for a in range(n)]
                              + [(copy(a, 4 + j, (*chip, 1 - c), me), self._has((*chip, 1 - c))) for j, chip, a in each]))

    def start(self, ins, outs, sems):
        p = self._parts(ins, outs, sems)
        for cp, cond in p["mine"]() + p["first"]():
            _when(cond, cp.start)

    def forward(self, ins, outs, sems):
        p = self._parts(ins, outs, sems)
        for (got, cond), (fwd, _) in zip(p["landed"](), p["passed"]()):
            def relay(got=got, fwd=fwd):
                got.wait_recv()
                fwd.start()

            _when(cond, relay)

    def finish(self, ins, outs, sems):
        p = self._parts(ins, outs, sems)
        for cp, cond in p["from_sib"]():
            _when(cond, cp.wait_recv)
        for cp, cond in p["first"]() + p["passed"]():
            _when(cond, cp.wait_send)
        for cp, cond in p["mine"]():
            _when(cond, cp.wait)


class ScatterPlan:
    aliases = ()

    def __init__(self, gs, only=None, into=None, whole=False):
        n = self.n = len(gs)
        self.only = only
        self.whole = whole
        self.inputs = list(gs) + list(into or [])
        if into:
            self.aliases = tuple((n + a, a) for a in range(n))
        self.out_shape = [S((NDEV,) + g.shape if whole else g.shape, g.dtype) for g in gs]
        self.sems = [pltpu.SemaphoreType.DMA((n, 7)), pltpu.SemaphoreType.DMA((n, 7)), pltpu.SemaphoreType.DMA((n,))]

    def _owner(self, idx):
        if self.only is None:
            return True
        return functools.reduce(jnp.logical_or, [idx == b for b in self.only])

    def _copies(self, ins, outs, sems):
        n = self.n
        send_sems, recv_sems, loc_sems = sems
        x, y, c = _coords()
        me = 4 * x + 2 * y + c
        mine = self._owner(me)
        block = (lambda a, k: ins[a]) if self.whole else (lambda a, k: ins[a].at[k])
        copies = [(pltpu.make_async_copy(block(a, me), outs[a].at[me], loc_sems.at[a]), mine, None) for a in range(n)]
        for m in range(1, NDEV):
            px = 1 - x if m & 4 else x
            py = 1 - y if m & 2 else y
            pc = 1 - c if m & 1 else c
            peer = 4 * px + 2 * py + pc
            for a in range(n):
                copies.append((pltpu.make_async_remote_copy(
                    src_ref=block(a, peer), dst_ref=outs[a].at[me],
                    send_sem=send_sems.at[a, m - 1], recv_sem=recv_sems.at[a, m - 1],
                    device_id=(px, py, pc), device_id_type=MESH), self._owner(peer), mine))
        return copies

    def start(self, ins, outs, sems):
        for cp, sends, _ in self._copies(ins, outs, sems):
            _when(sends, cp.start)

    def forward(self, ins, outs, sems):
        pass

    def finish(self, ins, outs, sems):
        for cp, sends, receives in self._copies(ins, outs, sems):
            if receives is None:
                _when(sends, cp.wait)
            else:
                _when(sends, cp.wait_send)
                _when(receives, cp.wait_recv)


class Plans:
    def __init__(self, plans):
        self.plans = plans
        self.inputs = [a for p in plans for a in p.inputs]
        self.out_shape = [s for p in plans for s in p.out_shape]
        self.sems = [s for p in plans for s in p.sems]
        self.aliases, i, o = [], 0, 0
        for p in plans:
            self.aliases += [(i + a, o + b) for a, b in p.aliases]
            i, o = i + len(p.inputs), o + len(p.out_shape)

    def _each(self, what, ins, outs, sems):
        i = o = s = 0
        for p in self.plans:
            ni, no, ns = len(p.inputs), len(p.out_shape), len(p.sems)
            getattr(p, what)(ins[i:i + ni], outs[o:o + no], sems[s:s + ns])
            i, o, s = i + ni, o + no, s + ns

    def start(self, ins, outs, sems):
        self._each("start", ins, outs, sems)

    def forward(self, ins, outs, sems):
        self._each("forward", ins, outs, sems)

    def finish(self, ins, outs, sems):
        self._each("finish", ins, outs, sems)


def _call(body, args, *, name, grid, in_specs, out_specs, out_shape, scratch=(), sem=None, vmem=None, plan=None,
          relay_step=None):
    if plan is None:
        outs = pl.pallas_call(body, name=name, grid=grid, in_specs=list(in_specs), out_specs=list(out_specs),
                              out_shape=list(out_shape), scratch_shapes=list(scratch),
                              compiler_params=_cp(sem, vmem))(*args)
        return list(outs), []
    ni, no, ns = len(in_specs), len(out_specs), len(scratch)
    pi, po = len(plan.inputs), len(plan.out_shape)
    aliases = {ni + a: no + b for a, b in plan.aliases}

    def wrapped(*refs):
        main_in, p_in = refs[:ni], refs[ni:ni + pi]
        main_out, p_out = refs[ni + pi:ni + pi + no], refs[ni + pi + no:ni + pi + no + po]
        main_scr, p_sems = refs[ni + pi + no + po:ni + pi + no + po + ns], refs[ni + pi + no + po + ns:]
        ids = [pl.program_id(d) for d in range(len(grid))]
        first = functools.reduce(jnp.logical_and, [i == 0 for i in ids])
        last = functools.reduce(jnp.logical_and, [i == g - 1 for i, g in zip(ids, grid)])

        @pl.when(first)
        def _():
            plan.start(p_in, p_out, p_sems)

        @pl.when(last if relay_step is None else ids[0] == max(relay_step, 0))
        def _():
            plan.forward(p_in, p_out, p_sems)

        body(*main_in, *main_out, *main_scr)

        @pl.when(last)
        def _():
            plan.finish(p_in, p_out, p_sems)

    outs = pl.pallas_call(
        wrapped, name=name, grid=grid, in_specs=list(in_specs) + [ANY] * pi, out_specs=list(out_specs) + [ANY] * po,
        out_shape=list(out_shape) + list(plan.out_shape), scratch_shapes=list(scratch) + list(plan.sems),
        input_output_aliases=aliases, compiler_params=_cp(("arbitrary",) * len(grid), vmem),
    )(*args, *plan.inputs)
    return list(outs[:no]), list(outs[no:])


def run_plan(plan, name):
    def body(*refs):
        ins, outs, sems = refs[:len(plan.inputs)], refs[len(plan.inputs):len(plan.inputs) + len(plan.out_shape)], \
            refs[len(plan.inputs) + len(plan.out_shape):]
        plan.start(ins, outs, sems)
        plan.forward(ins, outs, sems)
        plan.finish(ins, outs, sems)

    return pl.pallas_call(body, name=name, in_specs=[ANY] * len(plan.inputs), out_specs=[ANY] * len(plan.out_shape),
                          out_shape=list(plan.out_shape), scratch_shapes=list(plan.sems))(*plan.inputs)


def mm_tn(a, b, name, tn=512):
    T, K = a.shape
    N = b.shape[1]
    tn = min(tn, N)

    def body(a_ref, b_ref, o_ref):
        o_ref[...] = _dot_tn(a_ref[...], b_ref[...]).astype(GRAD_DT)

    (out,), _ = _call(body, [a, b], name=name, grid=(N // tn,),
                      in_specs=[_resident((T, K)), pl.BlockSpec((T, tn), lambda j: (0, j))],
                      out_specs=[pl.BlockSpec((None, K, tn), lambda j: (j, 0, 0))],
                      out_shape=[S((N // tn, K, tn), GRAD_DT)], sem=("parallel",), vmem=VMEM_LIMIT)
    return out


def grad_w_in_rest(xb, dh, dcg, dbg, dga, dgb, plan):
    T = xb.shape[0]
    order = ((0, 0), (1, 1), (2, 2), (3, 3), (4, 3), (5, 4), (6, 4))

    def body(x_ref, *refs):
        o_ref = refs[-1]
        j = pl.program_id(0)
        for step, opnd in order:
            @pl.when(j == step)
            def _(opnd=opnd):
                o_ref[...] = _dot_tn(x_ref[...], refs[opnd][...]).astype(GRAD_DT)

    once = lambda: pl.BlockSpec((T, W), lambda j: (0, 0), pipeline_mode=pl.Buffered(1))
    (out,), sent = _call(
        body, [xb, dh, dcg, dbg, dga, dgb], name="grad_w_in_rest", grid=(len(order),),
        in_specs=[_resident((T, D)), once(), once(), once(),
                  pl.BlockSpec((T, W), lambda j: (0, jnp.clip(j - 3, 0, 1))),
                  pl.BlockSpec((T, W), lambda j: (0, jnp.clip(j - 5, 0, 1)))],
        out_specs=[pl.BlockSpec((None, D, W), lambda j: (1 + j, 0, 0))],
        out_shape=[S((NDEV, D, W), GRAD_DT)], sem=("arbitrary",), vmem=VMEM_LIMIT, plan=plan)
    return out, sent


def mm_tn_rows(a, b, name, tk=256, plan=None):
    T, K = a.shape
    N = b.shape[1]
    tk = min(tk, K)

    def body(a_ref, b_ref, o_ref):
        o_ref[...] = _dot_tn(a_ref[...], b_ref[...]).astype(GRAD_DT)

    (out,), sent = _call(body, [a, b], name=name, grid=(K // tk,),
                         in_specs=[pl.BlockSpec((T, tk), lambda i: (0, i)), _resident((T, N))],
                         out_specs=[pl.BlockSpec((tk, N), lambda i: (i, 0))], out_shape=[S((K, N), GRAD_DT)],
                         sem=("parallel",), vmem=VMEM_LIMIT, plan=plan)
    return out, sent


def prep_weights(ws):
    def body(*refs):
        for i in range(len(ws)):
            refs[len(ws) + i][...] = refs[i][...].astype(bf16)

    return pl.pallas_call(body, name="prep_weights", out_shape=[S(w.shape, bf16) for w in ws],
                          compiler_params=_cp(None, VMEM_LIMIT))(*ws)


REST_BLOCKS = (4, 5, 6, 7, 1, 2, 3)
REST_COLS = len(REST_BLOCKS) * W


def in_proj_u(x, win_g, b_in):
    T = x.shape[0]
    tm = min(1024, T)

    def body(x_ref, w_ref, b_ref, u_ref, xb_ref):
        xb = x_ref[...].astype(bf16)
        xb_ref[...] = xb
        u_ref[...] = _dot(xb, w_ref[...]) + b_ref[...]

    row = pl.BlockSpec((tm, D), lambda i: (i, 0))
    return pl.pallas_call(
        body, name="in_proj_u", grid=(T // tm,),
        in_specs=[row, pl.BlockSpec((None, D, W), lambda i: (0, 0, 0)), pl.BlockSpec((1, W), lambda i: (0, 0))],
        out_specs=[pl.BlockSpec((tm, W), lambda i: (i, 0)), row],
        out_shape=[S((T, W), f32), S((T, D), bf16)], compiler_params=_cp(("parallel",), VMEM_LIMIT),
    )(x, win_g, b_in)


def in_proj_rest(xb, win_g, b_in, plan):
    T = xb.shape[0]
    tm = min(512, T)

    def body(x_ref, w_ref, b_ref, o_ref):
        xb_ = x_ref[...]
        for i, k in enumerate(REST_BLOCKS):
            o_ref[:, i * W:(i + 1) * W] = _dot(xb_, w_ref[k]) + b_ref[:, k * W:(k + 1) * W]

    return _call(
        body, [xb, win_g, b_in], name="in_proj_rest", grid=(T // tm,),
        in_specs=[pl.BlockSpec((tm, D), lambda i: (i, 0)), _resident((NDEV, D, W)), _resident((1, IN_COLS))],
        out_specs=[pl.BlockSpec((tm, REST_COLS), lambda i: (i, 0))],
        out_shape=[S((T, REST_COLS), f32)], vmem=VMEM_LIMIT, plan=plan, relay_step=T // tm - 2)


def _to_scan_order(a_ref, o_ref):
    L = a_ref.shape[0] // NC

    def step(jb, carry):
        j0 = pl.multiple_of(jb * 8, 8)
        for q in range(NC // 8):
            x = jnp.stack([a_ref[pl.ds((8 * q + c) * L + j0, 8), :] for c in range(8)], axis=0)
            y = jnp.swapaxes(x, 0, 1)
            for j in range(8):
                o_ref[pl.ds((j0 + j) * NC + 8 * q, 8), :] = y[j]
        return carry

    lax.fori_loop(0, L // 8, step, 0)


def _to_time_order(a_ref, o_ref):
    L = a_ref.shape[0] // NC

    def step(jb, carry):
        j0 = pl.multiple_of(jb * 16, 16)
        for q in range(NC // 8):
            halves = []
            for h in range(2):
                x = jnp.stack([a_ref[pl.ds((j0 + 8 * h + j) * NC + 8 * q, 8), :] for j in range(8)], axis=0)
                halves.append(jnp.swapaxes(x, 0, 1))
            for c in range(8):
                o_ref[pl.ds((8 * q + c) * L + j0, 16), :] = jnp.concatenate(
                    [halves[0][c], halves[1][c]], axis=0).astype(o_ref.dtype)
        return carry

    lax.fori_loop(0, L // 16, step, 0)


def _disc(lr, li, ldt):
    dt = jnp.exp(ldt)
    mag = jnp.exp(lr * dt)
    lbr = mag * jnp.cos(li * dt)
    lbi = mag * jnp.sin(li * dt)
    den = lr * lr + li * li
    nr = lbr - 1.0
    return lbr, lbi, (nr * lr + lbi * li) / den, (lbi * lr - nr * li) / den


def _per_channel(f):
    return jnp.broadcast_to(f[:, None, :], (NG, GC, NP)).reshape(NG * GC, NP)


def ssm_params(lam_re, lam_im, log_dt, br, bi):
    def body(lr_ref, li_ref, ldt_ref, br_ref, bi_ref, lbr_ref, lbi_ref, fr_ref, fi_ref, bbr_ref, bbi_ref):
        lbr, lbi, fr, fi = _disc(lr_ref[...], li_ref[...], ldt_ref[...])
        lbr_ref[...], lbi_ref[...], fr_ref[...], fi_ref[...] = lbr, lbi, fr, fi
        fr_, fi_, br_, bi_ = _per_channel(fr), _per_channel(fi), br_ref[...], bi_ref[...]
        bbr_ref[...] = fr_ * br_ - fi_ * bi_
        bbi_ref[...] = fr_ * bi_ + fi_ * br_

    return pl.pallas_call(body, name="ssm_params", out_shape=[S((NG, NP), f32)] * 4 + [S((NG * GC, NP), f32)] * 2)(
        lam_re, lam_im, log_dt, br, bi)


SCAN_UNROLL = 4
SCAN_LANES = 2 * LANE


def _steps(n, body, carry):
    main = n // SCAN_UNROLL

    def trip(t, c):
        for q in range(SCAN_UNROLL):
            c = body(t * SCAN_UNROLL + q, c)
        return c

    carry = lax.fori_loop(0, main, trip, carry)
    for i in range(main * SCAN_UNROLL, n):
        carry = body(i, carry)
    return carry


def _scan_body(T):
    L = T // NC
    RB = min(512, T)
    nsq = int(round(math.log2(L)))
    assert 2 ** nsq == L and T % RB == 0 and L % 16 == 0

    def rows(i):
        return pl.ds(pl.multiple_of(i * RB, RB), RB)

    def tile(j):
        return pl.ds(j * NC if isinstance(j, int) else pl.multiple_of(j * NC, NC), NC)

    def forward_states(u_ref, wb_ref, lbr_ref, lbi_ref, sre, sim, ere, eim):
        def bproj(i, carry):
            bu = _dot(u_ref[rows(i), :].astype(bf16), wb_ref[...])
            sre[rows(i), :] = bu[:, :SW]
            sim[rows(i), :] = bu[:, SW:]
            return carry

        lax.fori_loop(0, T // RB, bproj, 0)
        for lb in range(SW // SCAN_LANES):
            ls = slice(lb * SCAN_LANES, (lb + 1) * SCAN_LANES)
            ar = jnp.broadcast_to(lbr_ref[:, ls], (NC, SCAN_LANES))
            ai = jnp.broadcast_to(lbi_ref[:, ls], (NC, SCAN_LANES))

            def step(j, carry):
                xr, xi = carry
                nr = ar * xr - ai * xi + sre[tile(j), ls]
                ni = ar * xi + ai * xr + sim[tile(j), ls]
                sre[tile(j), ls] = nr
                sim[tile(j), ls] = ni
                return nr, ni

            zero = jnp.zeros((NC, SCAN_LANES), f32)
            _steps(L, step, (zero, zero))
            pr, pi = lbr_ref[:, ls], lbi_ref[:, ls]
            for _ in range(nsq):
                pr, pi = pr * pr - pi * pi, 2.0 * pr * pi
            er = jnp.zeros((1, SCAN_LANES), f32)
            ei = er
            ere[0:1, ls] = er
            eim[0:1, ls] = ei
            base = (L - 1) * NC
            for c in range(1, NC):
                lr_ = sre[base + c - 1:base + c, ls]
                li_ = sim[base + c - 1:base + c, ls]
                er, ei = lr_ + pr * er - pi * ei, li_ + pr * ei + pi * er
                ere[c:c + 1, ls] = er
                eim[c:c + 1, ls] = ei
            e_r, e_i = ere[:, ls].reshape(NC // 8, 8, SCAN_LANES), eim[:, ls].reshape(NC // 8, 8, SCAN_LANES)
            ar8, ai8 = ar[0:8], ai[0:8]

            def fix(j, carry):
                pwr, pwi = carry
                xr = sre[tile(j), ls].reshape(NC // 8, 8, SCAN_LANES) + (pwr * e_r - pwi * e_i)
                xi = sim[tile(j), ls].reshape(NC // 8, 8, SCAN_LANES) + (pwr * e_i + pwi * e_r)
                sre[tile(j), ls] = xr.reshape(NC, SCAN_LANES)
                sim[tile(j), ls] = xi.reshape(NC, SCAN_LANES)
                return pwr * ar8 - pwi * ai8, pwr * ai8 + pwi * ar8

            _steps(L, fix, (ar8, ai8))

    return L, RB, nsq, rows, tile, forward_states


def ssm_fwd(u, wb, wc, lbr, lbi, dsk, plan):
    T = u.shape[0]
    L, RB, nsq, rows, tile, forward_states = _scan_body(T)
    nslab = W // LANE

    def body(u_ref, wb_ref, wc_ref, lbr_ref, lbi_ref, d_ref, y_ref, up_ref, xr_ref, xi_ref, sre, sim, ere, eim, yp):
        _to_scan_order(u_ref, up_ref)
        forward_states(up_ref, wb_ref, lbr_ref, lbi_ref, sre, sim, ere, eim)

        def cproj(i, carry):
            xr, xi = sre[rows(i), :].astype(bf16), sim[rows(i), :].astype(bf16)
            xr_ref[rows(i), :] = xr
            xi_ref[rows(i), :] = xi
            y = _dot(xr, wc_ref[0:SW, :]) + _dot(xi, wc_ref[SW:, :])
            yp[rows(i), :] = y + d_ref[...] * up_ref[rows(i), :]
            return carry

        lax.fori_loop(0, T // RB, cproj, 0)
        _to_time_order(yp, y_ref)

    slab = pl.BlockSpec((T, LANE), lambda k: (0, k))
    states = pl.BlockSpec((T, SW), lambda k: (0, k))
    return _call(
        body, [u, wb, wc, lbr, lbi, dsk], name="ssm_fwd", grid=(nslab,),
        in_specs=[slab, pl.BlockSpec((None, LANE, 2 * SW), lambda k: (k, 0, 0)),
                  pl.BlockSpec((None, 2 * SW, LANE), lambda k: (k, 0, 0)),
                  pl.BlockSpec((None, 1, SW), lambda k: (k, 0, 0)), pl.BlockSpec((None, 1, SW), lambda k: (k, 0, 0)),
                  pl.BlockSpec((None, 1, LANE), lambda k: (k, 0, 0))],
        out_specs=[slab, slab, states, states],
        out_shape=[S((T, W), f32), S((T, W), f32), S((T, nslab * SW), bf16), S((T, nslab * SW), bf16)],
        scratch=[pltpu.VMEM((T, SW), f32), pltpu.VMEM((T, SW), f32), pltpu.VMEM((NC, SW), f32), pltpu.VMEM((NC, SW), f32),
                 pltpu.VMEM((T, LANE), f32)],
        vmem=VMEM_LIMIT, plan=plan)


def ssm_bwd(u_p, dy, xr, xi, wbT, wcT, lbr, lbi, dsk, plan):
    T = u_p.shape[0]
    L, RB, nsq, rows, tile, _ = _scan_body(T)

    def body(u_ref, dyt_ref, sre, sim, wbT_ref, wcT_ref, lbr_ref, lbi_ref, d_ref,
             dut_ref, dwb_ref, dwc_ref, dlr_ref, dli_ref, dd_ref, su_ref, gre, gim, ere, eim, dy_ref, du_ref):
        _to_scan_order(dyt_ref, dy_ref)

        def dstate(i, carry):
            g = _dot(dy_ref[rows(i), :].astype(bf16), wcT_ref[...])
            gre[rows(i), :] = g[:, :SW]
            gim[rows(i), :] = g[:, SW:]
            return carry

        lax.fori_loop(0, T // RB, dstate, 0)
        row = lax.broadcasted_iota(jnp.int32, (NC, SCAN_LANES), 0)
        for lb in range(SW // SCAN_LANES):
            ls = slice(lb * SCAN_LANES, (lb + 1) * SCAN_LANES)
            ar = jnp.broadcast_to(lbr_ref[:, ls], (NC, SCAN_LANES))
            ai = jnp.broadcast_to(lbi_ref[:, ls], (NC, SCAN_LANES))

            def step(i, carry):
                gr, gi = carry
                j = L - 1 - i
                nr = ar * gr + ai * gi + gre[tile(j), ls]
                ni = ar * gi - ai * gr + gim[tile(j), ls]
                gre[tile(j), ls] = nr
                gim[tile(j), ls] = ni
                return nr, ni

            zero = jnp.zeros((NC, SCAN_LANES), f32)
            _steps(L, step, (zero, zero))
            pr, pi = lbr_ref[:, ls], -lbi_ref[:, ls]
            for _ in range(nsq):
                pr, pi = pr * pr - pi * pi, 2.0 * pr * pi
            er = jnp.zeros((1, SCAN_LANES), f32)
            ei = er
            ere[NC - 1:NC, ls] = er
            eim[NC - 1:NC, ls] = ei
            for c in range(NC - 2, -1, -1):
                lr_ = gre[c + 1:c + 2, ls]
                li_ = gim[c + 1:c + 2, ls]
                er, ei = lr_ + pr * er - pi * ei, li_ + pr * ei + pi * er
                ere[c:c + 1, ls] = er
                eim[c:c + 1, ls] = ei
            e_r, e_i = ere[:, ls].reshape(NC // 8, 8, SCAN_LANES), eim[:, ls].reshape(NC // 8, 8, SCAN_LANES)
            ar8, ai8 = ar[0:8], ai[0:8]

            def fixed(j, pwr, pwi):
                gr = (gre[tile(j), ls].reshape(NC // 8, 8, SCAN_LANES) + (pwr * e_r - pwi * e_i)).reshape(NC, SCAN_LANES)
                gi = (gim[tile(j), ls].reshape(NC // 8, 8, SCAN_LANES) + (pwr * e_i + pwi * e_r)).reshape(NC, SCAN_LANES)
                gre[tile(j), ls] = gr
                gim[tile(j), ls] = gi
                return gr, gi

            def fix(i, carry):
                pwr, pwi, accr, acci = carry
                j = L - 1 - i
                gr, gi = fixed(j, pwr, pwi)
                xr, xi = sre[tile(j - 1), ls].astype(f32), sim[tile(j - 1), ls].astype(f32)
                return (pwr * ar8 + pwi * ai8, pwi * ar8 - pwr * ai8,
                        accr + gr * xr + gi * xi, acci + gi * xr - gr * xi)

            pwr, pwi, accr, acci = _steps(L - 1, fix, (ar8, -ai8, zero, zero))
            gr, gi = fixed(0, pwr, pwi)
            xr = jnp.where(row == 0, 0.0, pltpu.roll(sre[tile(L - 1), ls].astype(f32), 1, axis=0))
            xi = jnp.where(row == 0, 0.0, pltpu.roll(sim[tile(L - 1), ls].astype(f32), 1, axis=0))
            accr = accr + gr * xr + gi * xi
            acci = acci + gi * xr - gr * xi
            dlr_ref[:, ls] = jnp.sum(accr, axis=0, keepdims=True)
            dli_ref[:, ls] = jnp.sum(acci, axis=0, keepdims=True)

        dwb_ref[...] = jnp.zeros_like(dwb_ref)
        dwc_ref[...] = jnp.zeros_like(dwc_ref)
        dd_ref[...] = jnp.zeros_like(dd_ref)
        su_ref[...] = jnp.zeros_like(su_ref)

        def finish(i, carry):
            u32, dy32 = u_ref[rows(i), :], dy_ref[rows(i), :]
            ub, dyb = u32.astype(bf16), dy32.astype(bf16)
            gr, gi = gre[rows(i), :].astype(bf16), gim[rows(i), :].astype(bf16)
            du = _dot(gr, wbT_ref[0:SW, :]) + _dot(gi, wbT_ref[SW:, :]) + dy32 * d_ref[...]
            du_ref[rows(i), :] = du
            su_ref[...] += jnp.sum(du, axis=0, keepdims=True)
            dwb_ref[:, 0:SW] += _dot_tn(ub, gr)
            dwb_ref[:, SW:] += _dot_tn(ub, gi)
            dwc_ref[:, 0:SW] += _dot_tn(dyb, sre[rows(i), :])
            dwc_ref[:, SW:] += _dot_tn(dyb, sim[rows(i), :])
            dd_ref[...] += jnp.sum(dy32 * u32, axis=0, keepdims=True)
            return carry

        lax.fori_loop(0, T // RB, finish, 0)
        _to_time_order(du_ref, dut_ref)

    slab = pl.BlockSpec((T, LANE), lambda k: (0, k))
    wide = pl.BlockSpec((None, LANE, 2 * SW), lambda k: (k, 0, 0))
    tall = pl.BlockSpec((None, 2 * SW, LANE), lambda k: (k, 0, 0))
    vec = pl.BlockSpec((None, 1, SW), lambda k: (k, 0, 0))
    vecd = pl.BlockSpec((None, 1, LANE), lambda k: (k, 0, 0))
    states = pl.BlockSpec((T, SW), lambda k: (0, k))
    nslab = W // LANE
    return _call(
        body, [u_p, dy, xr, xi, wbT, wcT, lbr, lbi, dsk], name="ssm_bwd", grid=(nslab,),
        in_specs=[slab, slab, states, states, tall, wide, vec, vec, vecd],
        out_specs=[slab, wide, wide, vec, vec, vecd, vecd],
        out_shape=[S((T, W), bf16), S((nslab, LANE, 2 * SW), f32), S((nslab, LANE, 2 * SW), f32),
                   S((nslab, 1, SW), f32), S((nslab, 1, SW), f32), S((nslab, 1, LANE), f32), S((nslab, 1, LANE), f32)],
        scratch=[pltpu.VMEM((T, SW), f32)] * 2 + [pltpu.VMEM((NC, SW), f32)] * 2 + [pltpu.VMEM((T, LANE), f32)] * 2,
        vmem=VMEM_LIMIT, plan=plan)


def _shift_rows(cur, prev8, k):
    return pltpu.roll(jnp.concatenate([prev8, cur], axis=0), k, axis=0)[8:]


def _lift_rows(cur, next8, k):
    n = cur.shape[0]
    return pltpu.roll(jnp.concatenate([cur, next8], axis=0), n + 8 - k, axis=0)[:n]


def conv_fwd(proj, conv_w):
    T = proj.shape[0]
    RB = min(512, T)

    def body(h_ref, c_ref, b_ref, w_ref, o_ref):
        w0, w1, w2 = w_ref[0:1, :], w_ref[1:2, :], w_ref[2:3, :]

        def blk(i, carry):
            r0 = pl.multiple_of(i * RB, RB)
            rs = pl.ds(r0, RB)
            ch = c_ref[rs, :] * h_ref[rs, :]
            pr = pl.ds(jnp.maximum(r0 - 8, 0), 8)
            prev = jnp.where(i > 0, c_ref[pr, :] * h_ref[pr, :], 0.0)
            z = w2 * ch + w1 * _shift_rows(ch, prev, 1) + w0 * _shift_rows(ch, prev, 2)
            o_ref[rs, :] = (b_ref[rs, :] * z).astype(bf16)
            return carry

        lax.fori_loop(0, T // RB, blk, 0)

    nb = W // LANE
    return pl.pallas_call(
        body, name="conv_fwd", grid=(nb,),
        in_specs=[pl.BlockSpec((T, LANE), lambda k: (0, 4 * nb + k)), pl.BlockSpec((T, LANE), lambda k: (0, 5 * nb + k)),
                  pl.BlockSpec((T, LANE), lambda k: (0, 6 * nb + k)),pl.BlockSpec((3, LANE), lambda k: (0, k))],
        out_specs=pl.BlockSpec((T, LANE), lambda k: (0, k)), out_shape=S((T, W), bf16),
        compiler_params=_cp(("parallel",), VMEM_LIMIT),
    )(proj, proj, proj, conv_w)


def _dense_columns(blocks_ref, dense_ref):
    for k in range(NDEV):
        dense_ref[:, k * LANE:(k + 1) * LANE] = blocks_ref[k]


def merge_fwd(yn, glu_w, glu_b, yb, wso, wco, proj, plan):
    T = yn.shape[0]
    tm = min(1024, T)

    def body(y_ref, gw_ref, gbias_ref, yb_ref, wa_ref, wb_ref, ga_ref, gb_ref, o_ref, ya_ref, wa_s, wb_s):
        @pl.when(pl.program_id(0) == 0)
        def _():
            _dense_columns(wa_ref, wa_s)
            _dense_columns(wb_ref, wb_s)

        for rs in _row_parts(tm):
            g = _gelu(y_ref[rs, :])
            ya = (g * _sigmoid(_dot(g.astype(bf16), gw_ref[...]) + gbias_ref[...])).astype(bf16)
            ya_ref[rs, :] = ya
            o_ref[rs, :] = (_sigmoid(ga_ref[rs, :]) * _dot(ya, wa_s[...])
                            + _sigmoid(gb_ref[rs, :]) * _dot(yb_ref[rs, :], wb_s[...])).astype(bf16)

    act = pl.BlockSpec((tm, W), lambda i: (i, 0))
    return _call(
        body, [yn, glu_w, glu_b, yb, wso, wco, proj, proj], name="merge_fwd", grid=(T // tm,),
        in_specs=[act, pl.BlockSpec((W, W), lambda i: (0, 0)), pl.BlockSpec((1, W), lambda i: (0, 0)), act,
                  _resident((NDEV, W, LANE)), _resident((NDEV, W, LANE)),
                  pl.BlockSpec((tm, D), lambda i: (i, 0)), pl.BlockSpec((tm, D), lambda i: (i, 1))],
        out_specs=[pl.BlockSpec((tm, D), lambda i: (i, 0)), act], out_shape=[S((T, D), bf16), S((T, W), bf16)],
        scratch=[pltpu.VMEM((W, D), bf16), pltpu.VMEM((W, D), bf16)], vmem=VMEM_LIMIT, plan=plan)


def mix_ln1(merged, w_o, x, g1, b1, plan):
    T = x.shape[0]
    tm = min(512, T)

    def body(m_ref, w_ref, x_ref, g_ref, b_ref, r_ref, x1_ref):
        for rs in _row_parts(tm):
            r = ALPHA * x_ref[rs, :] + _dot(m_ref[rs, :], w_ref[...])
            r_ref[rs, :] = r
            xhat, _ = _ln_stats(r)
            x1_ref[rs, :] = (xhat * g_ref[...] + b_ref[...]).astype(bf16)

    row = pl.BlockSpec((tm, D), lambda i: (i, 0))
    vec = pl.BlockSpec((1, D), lambda i: (0, 0))
    return _call(
        body, [merged, w_o, x, g1, b1], name="mix_ln1", grid=(T // tm,),
        in_specs=[row, _resident((D, D)), row, vec, vec],
        out_specs=[row, row], out_shape=[S((T, D), f32), S((T, D), bf16)], sem=("parallel",), vmem=VMEM_LIMIT, plan=plan,
        relay_step=T // tm - 2)


FT = 256


def gate_up(x1b, wgT, wuT, plan):
    T = x1b.shape[0]
    tm = min(512, T)

    def body(x_ref, wg_ref, wu_ref, g_ref, u_ref, h_ref):
        x = x_ref[...]
        for n in range(F // FT):
            cs = slice(n * FT, (n + 1) * FT)
            g = _dot_nt(x, wg_ref[cs, :])
            u = _dot_nt(x, wu_ref[cs, :])
            g_ref[:, cs] = g.astype(bf16)
            u_ref[:, cs] = u.astype(bf16)
            h_ref[:, cs] = (g * _sigmoid(g) * u).astype(bf16)

    osp = pl.BlockSpec((tm, F), lambda i: (i, 0))
    return _call(
        body, [x1b, wgT, wuT], name="gate_up", grid=(T // tm,),
        in_specs=[pl.BlockSpec((tm, D), lambda i: (i, 0)), _resident((F, D)), _resident((F, D))],
        out_specs=[osp, osp, osp], out_shape=[S((T, F), bf16)] * 3, vmem=VMEM_LIMIT, plan=plan, relay_step=T // tm - 3)


def down_loss(hid, w_down, r1, g1, b1, g2, b2, target):
    T = hid.shape[0]
    tm = min(512, T)

    def body(h_ref, w_ref, r1_ref, g1_ref, b1_ref, g2_ref, b2_ref, t_ref, dr_ref, drb_ref, loss_ref, dg_ref, db_ref):
        @pl.when(pl.program_id(0) == 0)
        def _():
            loss_ref[...] = jnp.zeros_like(loss_ref)
            dg_ref[...] = jnp.zeros_like(dg_ref)
            db_ref[...] = jnp.zeros_like(db_ref)

        for rs in _row_parts(tm):
            xh1, _ = _ln_stats(r1_ref[rs, :])
            x1 = xh1 * g1_ref[...] + b1_ref[...]
            r2 = ALPHA * x1 + _dot(h_ref[rs, :], w_ref[...])
            xh2, rstd2 = _ln_stats(r2)
            err = xh2 * g2_ref[...] + b2_ref[...] - t_ref[rs, :]
            loss_ref[...] += jnp.sum(jnp.mean(err * err, axis=-1, keepdims=True), axis=0, keepdims=True)
            dy = err * (1.0 / D)
            dg_ref[...] += jnp.sum(dy * xh2, axis=0, keepdims=True)
            db_ref[...] += jnp.sum(dy, axis=0, keepdims=True)
            dr = _ln_bwd(dy, xh2, rstd2, g2_ref[...])
            dr_ref[rs, :] = dr
            drb_ref[rs, :] = dr.astype(bf16)

    row = pl.BlockSpec((tm, D), lambda i: (i, 0))
    vec = pl.BlockSpec((1, D), lambda i: (0, 0))
    return pl.pallas_call(
        body, name="down_loss", grid=(T // tm,),
        in_specs=[pl.BlockSpec((tm, F), lambda i: (i, 0)), _resident((F, D)), row, vec, vec, vec, vec, row],
        out_specs=[row, row, pl.BlockSpec((1, 1), lambda i: (0, 0)), vec, vec],
        out_shape=[S((T, D), f32), S((T, D), bf16), S((1, 1), f32), S((1, D), f32), S((1, D), f32)],
        compiler_params=_cp(("arbitrary",), VMEM_LIMIT),
    )(hid, w_down, r1, g1, b1, g2, b2, target)


def ffn_bwd_act(dffn, w_down, gate, up, plan):
    T = dffn.shape[0]
    tm = min(512, T)

    def body(d_ref, w_ref, g_ref, u_ref, dg_ref, du_ref):
        for n in range(F // FT):
            cs = slice(n * FT, (n + 1) * FT)
            for rs in _row_parts(tm):
                dh = _dot_nt(d_ref[rs, :], w_ref[cs, :])
                g, u = g_ref[rs, cs].astype(f32), u_ref[rs, cs].astype(f32)
                sg = _sigmoid(g)
                t = g * sg
                du_ref[rs, cs] = (dh * t).astype(bf16)
                dg_ref[rs, cs] = (dh * u * (sg + t - t * sg)).astype(bf16)

    osp = pl.BlockSpec((tm, F), lambda i: (i, 0))
    return _call(
        body, [dffn, w_down, gate, up], name="ffn_bwd_act", grid=(T // tm,),
        in_specs=[pl.BlockSpec((tm, D), lambda i: (i, 0)), _resident((F, D)), osp, osp],
        out_specs=[osp, osp], out_shape=[S((T, F), bf16)] * 2, sem=("parallel",), vmem=VMEM_LIMIT, plan=plan)


def ffn_bwd_x(dgate, dup, wgT, wuT, dr2, r1, g1, plan):
    T = dr2.shape[0]
    tm = min(512, T)

    def body(dg_ref, du_ref, wg_ref, wu_ref, dr2_ref, r1_ref, g1_ref, dr_ref, drb_ref, dgam_ref, dbet_ref):
        @pl.when(pl.program_id(0) == 0)
        def _():
            dgam_ref[...] = jnp.zeros_like(dgam_ref)
            dbet_ref[...] = jnp.zeros_like(dbet_ref)

        for rs in _row_parts(tm):
            dx1 = ALPHA * dr2_ref[rs, :] + _dot(dg_ref[rs, :], wg_ref[...]) + _dot(du_ref[rs, :], wu_ref[...])
            xh, rstd = _ln_stats(r1_ref[rs, :])
            dgam_ref[...] += jnp.sum(dx1 * xh, axis=0, keepdims=True)
            dbet_ref[...] += jnp.sum(dx1, axis=0, keepdims=True)
            dr = _ln_bwd(dx1, xh, rstd, g1_ref[...])
            dr_ref[rs, :] = dr
            drb_ref[rs, :] = dr.astype(bf16)

    row = pl.BlockSpec((tm, D), lambda i: (i, 0))
    wide = pl.BlockSpec((tm, F), lambda i: (i, 0))
    wsp = _resident((F, D))
    vec = pl.BlockSpec((1, D), lambda i: (0, 0))
    return _call(
        body, [dgate, dup, wgT, wuT, dr2, r1, g1], name="ffn_bwd_x", grid=(T // tm,),
        in_specs=[wide, wide, wsp, wsp, row, row, vec],
        out_specs=[row, row, vec, vec], out_shape=[S((T, D), f32), S((T, D), bf16), S((1, D), f32), S((1, D), f32)],
        vmem=VMEM_LIMIT, plan=plan)


def merge_bwd(dmix, w_o, merged, ya, yb, wso, wco, proj, plan):
    T = dmix.shape[0]
    tm = min(512, T)

    def body(dm_ref, wo_ref, m_ref, ya_ref, yb_ref, wa_ref, wb_ref, ga_ref, gb_ref,
             dya_ref, dyb_ref, dga_ref, dgb_ref, sa_ref, sb_ref, dwo_ref, wa_s, wb_s, acc):
        @pl.when(pl.program_id(0) == 0)
        def _():
            _dense_columns(wa_ref, wa_s)
            _dense_columns(wb_ref, wb_s)
            acc[...] = jnp.zeros_like(acc)

        acc[...] += _dot_tn(m_ref[...], dm_ref[...])

        @pl.when(pl.program_id(0) == pl.num_programs(0) - 1)
        def _():
            dwo_ref[...] = acc[...].astype(GRAD_DT)

        dmer = _dot_nt(dm_ref[...], wo_ref[...])
        sa, sb = _sigmoid(ga_ref[...]), _sigmoid(gb_ref[...])
        dya_ref[...] = (dmer * sa).astype(bf16)
        dyb_ref[...] = (dmer * sb).astype(bf16)
        dga = dmer * _dot(ya_ref[...], wa_s[...]) * sa * (1.0 - sa)
        dgb = dmer * _dot(yb_ref[...], wb_s[...]) * sb * (1.0 - sb)
        dga_ref[...] = dga.astype(bf16)
        dgb_ref[...] = dgb.astype(bf16)
        sa_ref[...] = jnp.sum(dga, axis=0, keepdims=True)
        sb_ref[...] = jnp.sum(dgb, axis=0, keepdims=True)

    act = pl.BlockSpec((tm, W), lambda i: (i, 0))
    osp = pl.BlockSpec((tm, D), lambda i: (i, 0))
    ssp = pl.BlockSpec((None, 1, D), lambda i: (i, 0, 0))
    return _call(
        body, [dmix, w_o, merged, ya, yb, wso, wco, proj, proj], name="merge_bwd", grid=(T // tm,),
        in_specs=[osp, _resident((D, D)), osp, act, act, _resident((NDEV, W, LANE)), _resident((NDEV, W, LANE)),
                  pl.BlockSpec((tm, D), lambda i: (i, 0)), pl.BlockSpec((tm, D), lambda i: (i, 1))],
        out_specs=[osp, osp, osp, osp, ssp, ssp, pl.BlockSpec((D, D), lambda i: (0, 0))],
        out_shape=[S((T, D), bf16)] * 4 + [S((T // tm, 1, D), f32)] * 2 + [S((D, D), GRAD_DT)],
        scratch=[pltpu.VMEM((W, D), bf16), pltpu.VMEM((W, D), bf16), pltpu.VMEM((D, D), f32)], vmem=VMEM_LIMIT, plan=plan)


def branches_bwd(dYA, dYB, ya, yb, wso, wco):
    T = dYA.shape[0]
    tm = min(1024, T)

    def body(da_ref, db_ref, ya_ref, yb_ref, wa_ref, wb_ref, oa_ref, ob_ref, ga_ref, gb_ref, wa_s, wb_s, acc_a, acc_b):
        @pl.when(pl.program_id(0) == 0)
        def _():
            _dense_columns(wa_ref, wa_s)
            _dense_columns(wb_ref, wb_s)
            acc_a[...] = jnp.zeros_like(acc_a)
            acc_b[...] = jnp.zeros_like(acc_b)

        oa_ref[...] = _dot_nt(da_ref[...], wa_s[...])
        ob_ref[...] = _dot_nt(db_ref[...], wb_s[...])
        acc_a[...] += _dot_tn(ya_ref[...], da_ref[...])
        acc_b[...] += _dot_tn(yb_ref[...], db_ref[...])

        @pl.when(pl.program_id(0) == pl.num_programs(0) - 1)
        def _():
            for k in range(NDEV):
                ga_ref[k] = acc_a[:, k * LANE:(k + 1) * LANE].astype(GRAD_DT)
                gb_ref[k] = acc_b[:, k * LANE:(k + 1) * LANE].astype(GRAD_DT)

    row = pl.BlockSpec((tm, D), lambda i: (i, 0))
    osp = pl.BlockSpec((tm, W), lambda i: (i, 0))
    blocks = pl.BlockSpec((NDEV, W, LANE), lambda i: (0, 0, 0))
    outs, _ = _call(
        body, [dYA, dYB, ya, yb, wso, wco], name="branches_bwd", grid=(T // tm,),
        in_specs=[row, row, osp, osp, _resident((NDEV, W, LANE)), _resident((NDEV, W, LANE))],
        out_specs=[osp, osp, blocks, blocks], out_shape=[S((T, W), f32)] * 2 + [S((NDEV, W, LANE), GRAD_DT)] * 2,
        scratch=[pltpu.VMEM((W, D), bf16)] * 2 + [pltpu.VMEM((W, D), f32)] * 2, sem=("arbitrary",), vmem=VMEM_LIMIT)
    return outs


def glu_bwd(yn, dya, glu_w, glu_b, plan):
    T = yn.shape[0]
    tm = min(512, T)

    def body(y_ref, d_ref, w_ref, b_ref, dy_ref, db_ref, dw_ref, acc):
        @pl.when(pl.program_id(0) == 0)
        def _():
            db_ref[...] = jnp.zeros_like(db_ref)
            acc[...] = jnp.zeros_like(acc)

        y, dya_ = y_ref[...], d_ref[...]
        g = _gelu(y)
        gb = g.astype(bf16)
        s = _sigmoid(_dot(gb, w_ref[...]) + b_ref[...])
        dsp = dya_ * g * s * (1.0 - s)
        dspb = dsp.astype(bf16)
        dg = dya_ * s + _dot_nt(dspb, w_ref[...])
        dy_ref[...] = dg * _gelu_grad(y)
        db_ref[...] += jnp.sum(dsp, axis=0, keepdims=True)
        acc[...] += _dot_tn(gb, dspb)

        @pl.when(pl.program_id(0) == pl.num_programs(0) - 1)
        def _():
            dw_ref[...] = acc[...].astype(GRAD_DT)

    row = pl.BlockSpec((tm, W), lambda i: (i, 0))
    vec = pl.BlockSpec((1, W), lambda i: (0, 0))
    mat = pl.BlockSpec((W, W), lambda i: (0, 0))
    return _call(
        body, [yn, dya, glu_w, glu_b], name="glu_bwd", grid=(T // tm,),
        in_specs=[row, row, mat, vec],
        out_specs=[row, vec, mat], out_shape=[S((T, W), f32), S((1, W), f32), S((W, W), GRAD_DT)],
        scratch=[pltpu.VMEM((W, W), f32)], sem=("arbitrary",), plan=plan)


def conv_bwd(proj, dyb, conv_w, plan):
    T = proj.shape[0]
    RB = min(512, T)
    nrb = T // RB

    def body(h_ref, c_ref, b_ref, d_ref, w_ref, dh_ref, dc_ref, db_ref, dw_ref, s_ref):
        w0, w1, w2 = w_ref[0:1, :], w_ref[1:2, :], w_ref[2:3, :]

        def blk(i, carry):
            a0, a1, a2, sh, sc, sb = carry
            r0 = pl.multiple_of(i * RB, RB)
            rs = pl.ds(r0, RB)
            h, cg, bg, dyb_ = h_ref[rs, :], c_ref[rs, :], b_ref[rs, :], d_ref[rs, :]
            ch = cg * h
            pr = pl.ds(jnp.maximum(r0 - 8, 0), 8)
            prev = jnp.where(i > 0, c_ref[pr, :] * h_ref[pr, :], 0.0)
            ch1, ch2 = _shift_rows(ch, prev, 1), _shift_rows(ch, prev, 2)
            dbg = dyb_ * (w2 * ch + w1 * ch1 + w0 * ch2)
            db_ref[rs, :] = dbg.astype(bf16)
            dz = dyb_ * bg
            nx = pl.ds(jnp.minimum(r0 + RB, T - 8), 8)
            nxt = jnp.where(i < nrb - 1, d_ref[nx, :] * b_ref[nx, :], 0.0)
            dch = w2 * dz + w1 * _lift_rows(dz, nxt, 1) + w0 * _lift_rows(dz, nxt, 2)
            dcg, dh = dch * h, dch * cg
            dc_ref[rs, :] = dcg.astype(bf16)
            dh_ref[rs, :] = dh.astype(bf16)
            col = lambda v: jnp.sum(v, axis=0, keepdims=True)
            return (a0 + col(dz * ch2), a1 + col(dz * ch1), a2 + col(dz * ch), sh + col(dh), sc + col(dcg), sb + col(dbg))

        zero = jnp.zeros((1, LANE), f32)
        a0, a1, a2, sh, sc, sb = lax.fori_loop(0, nrb, blk, (zero,) * 6)
        dw_ref[0:1, :] = a0
        dw_ref[1:2, :] = a1
        dw_ref[2:3, :] = a2
        s_ref[0:1, :] = sh
        s_ref[1:2, :] = sc
        s_ref[2:3, :] = sb

    nb = W // LANE
    slab = pl.BlockSpec((T, LANE), lambda k: (0, k))
    three = pl.BlockSpec((3, LANE), lambda k: (0, k))
    return _call(
        body, [proj, proj, proj, dyb, conv_w], name="conv_bwd", grid=(nb,),
        in_specs=[pl.BlockSpec((T, LANE), lambda k: (0, 4 * nb + k)), pl.BlockSpec((T, LANE), lambda k: (0, 5 * nb + k)),
                  pl.BlockSpec((T, LANE), lambda k: (0, 6 * nb + k)), slab, three],
        out_specs=[slab, slab, slab, three, three],
        out_shape=[S((T, W), bf16)] * 3 + [S((3, W), f32)] * 2, sem=("parallel",), vmem=VMEM_LIMIT, plan=plan)


def in_proj_bwd_x(parts, win_g, base, scale, name, plan=None):
    T = base.shape[0]
    tm = min(512, T)
    n = len(parts)

    def body(*refs):
        p_refs, w_ref, b_ref, o_ref = refs[:n], refs[n], refs[n + 1], refs[n + 2]
        acc = scale * b_ref[...]
        for p_ref, (_, _, k) in zip(p_refs, parts):
            acc += _dot_nt(p_ref[...], w_ref[k])
        o_ref[...] = acc

    row = pl.BlockSpec((tm, D), lambda i: (i, 0))
    p_specs = [pl.BlockSpec((tm, W), (lambda i, cb=cb: (i, cb))) for _, cb, _ in parts]
    return _call(
        body, [a for a, _, _ in parts] + [win_g, base], name=name, grid=(T // tm,),
        in_specs=p_specs + [_resident((NDEV, D, W)), row],
        out_specs=[row], out_shape=[S((T, D), f32)], vmem=VMEM_LIMIT, plan=plan)


def ssm_param_bwd(lam_re, lam_im, log_dt, fr, fi, br, bi, dwb, dwcT, dlbr, dlbi):
    def body(lr_ref, li_ref, ldt_ref, fr_ref, fi_ref, br_ref, bi_ref, dwb_ref, dwc_ref, dlbr_ref, dlbi_ref,
             dbr_ref, dbi_ref, dlr_ref, dli_ref, dldt_ref, dcr_ref, dci_ref, dr_s, di_s):
        for k in range(W // LANE):
            for gl in range(NG // (W // LANE)):
                rows, src = slice((8 * k + gl) * GC, (8 * k + gl + 1) * GC), slice(gl * GC, (gl + 1) * GC)
                re, im = slice(gl * NP, (gl + 1) * NP), slice(SW + gl * NP, SW + (gl + 1) * NP)
                dr_s[rows, :] = dwb_ref[k, src, re]
                di_s[rows, :] = dwb_ref[k, src, im]
                dcr_ref[rows, :] = dwc_ref[k, src, re]
                dci_ref[rows, :] = -dwc_ref[k, src, im]
        fr_, fi_ = _per_channel(fr_ref[...]), _per_channel(fi_ref[...])
        br_, bi_, dr, di = br_ref[...], bi_ref[...], dr_s[...], di_s[...]
        dbr_ref[...] = fr_ * dr + fi_ * di
        dbi_ref[...] = fr_ * di - fi_ * dr
        dfr = jnp.sum((dr * br_ + di * bi_).reshape(NG, GC, NP), axis=1)
        dfi = jnp.sum((di * br_ - dr * bi_).reshape(NG, GC, NP), axis=1)
        _, vjp = jax.vjp(_disc, lr_ref[...], li_ref[...], ldt_ref[...])
        dlr_ref[...], dli_ref[...], dldt = vjp((dlbr_ref[...], dlbi_ref[...], dfr, dfi))
        dldt_ref[...] = _transpose_exact(dldt)

    blk = S((NG * GC, NP), f32)
    return pl.pallas_call(
        body, name="ssm_param_bwd", out_shape=[blk, blk, S((NG, NP), f32), S((NG, NP), f32), S((1, NG), f32), blk, blk],
        scratch_shapes=[pltpu.VMEM((NG * GC, NP), f32)] * 2)(
        lam_re, lam_im, log_dt, fr, fi, br, bi, dwb, dwcT, dlbr, dlbi)


def _adam(w, g, m, v):
    m = ADAM_B1 * m + (1.0 - ADAM_B1) * g
    v = ADAM_B2 * v + (1.0 - ADAM_B2) * (g * g)
    m_hat = m / (1.0 - ADAM_B1 ** ADAM_STEP)
    v_hat = v / (1.0 - ADAM_B2 ** ADAM_STEP)
    return -ADAM_LR * (m_hat / (jnp.sqrt(v_hat) + ADAM_EPS) + ADAM_WD * w), m, v


def _sum_in_order(c_ref):
    g = c_ref[0].astype(f32)
    for k in range(1, c_ref.shape[0]):
        g = g + c_ref[k].astype(f32)
    return g


def sum_blocks(contrib, name):
    def body(c_ref, o_ref):
        o_ref[...] = _sum_in_order(c_ref)

    return pl.pallas_call(body, name=name, out_shape=S(contrib.shape[1:], f32))(contrib)


def adam_update(w, m, v, contrib, name, rows_per_block=None, summed_on_0=None, plan=None):
    R, C = w.shape
    n = contrib.shape[0]
    tr = min(rows_per_block or R, R)

    def body(w_ref, m_ref, v_ref, c_ref, *refs):
        g_ref, d_ref, nm_ref, nv_ref = refs[-4:]
        g = _sum_in_order(c_ref)
        if summed_on_0 is not None:
            x, y, c = _coords()
            g = jnp.where(4 * x + 2 * y + c == 0, refs[0][...], g)
        g_ref[...] = g
        d_ref[...], nm_ref[...], nv_ref[...] = _adam(w_ref[...], g, m_ref[...], v_ref[...])

    blk = pl.BlockSpec((tr, C), lambda i: (i, 0))
    extra = [] if summed_on_0 is None else [summed_on_0]
    return _call(
        body, [w, m, v, contrib] + extra, name=name, grid=(R // tr,),
        in_specs=[blk, blk, blk, pl.BlockSpec((n, tr, C), lambda i: (0, i, 0))] + [blk] * len(extra),
        out_specs=[blk] * 4, out_shape=[S((R, C), f32)] * 4, sem=("parallel",), vmem=VMEM_LIMIT, plan=plan)


_ROWVEC = (("b_in", IN_COLS), ("ssm_d", W), ("glu_b", W), ("ln1_g", D), ("ln1_b", D), ("ln2_g", D), ("ln2_b", D))
_HALF = NG * GC // 2
_BC_LANE = {"ssm_b_re": 0, "ssm_b_im": NP, "ssm_c_re": 0, "ssm_c_im": NP}
_PACK = {}
_r = 0
for _n, _k in _ROWVEC:
    _PACK[_n] = _r
    _r += _k // LANE
for _n, _rows in (("ssm_lambda", NG), ("scalars", 8), ("ssm_b", _HALF), ("ssm_c", _HALF), ("conv_w", 16)):
    _PACK[_n] = _r
    _r += _rows
for _n in _BC_LANE:
    _PACK[_n] = _PACK[_n[:5]]
PACK_ROWS = _r
assert PACK_ROWS % 8 == 0
_SMALL = ("b_in", "ssm_lambda_re", "ssm_lambda_im", "ssm_log_dt", "ssm_b_re", "ssm_b_im", "ssm_c_re", "ssm_c_im",
          "ssm_d", "glu_b", "ln1_g", "ln1_b", "ln2_g", "ln2_b")


def pack_grads(su, shcb, sga, sgb, dd, dglu_b, dln1_g, dln1_b, dln2_g, dln2_b, dlam_re, dlam_im, dldt, sqerr, dbr, dbi,
               dc_re, dc_im, dconv):
    nI = sga.shape[0]

    def body(su_ref, sh_ref, sga_ref, sgb_ref, dd_ref, gb_ref, l1g_ref, l1b_ref, l2g_ref, l2b_ref, lr_ref, li_ref, dt_ref,
             sq_ref, br_ref, bi_ref, cr_ref, ci_ref, cw_ref, o_ref):
        o_ref[...] = jnp.zeros_like(o_ref)

        def put_row(name, v):
            r0 = _PACK[name]
            for i in range(v.shape[1] // LANE):
                o_ref[r0 + i:r0 + i + 1, :] = v[:, i * LANE:(i + 1) * LANE]

        ga, gb = sga_ref[0], sgb_ref[0]
        for i in range(1, nI):
            ga, gb = ga + sga_ref[i], gb + sgb_ref[i]
        put_row("b_in", jnp.concatenate([su_ref[k] for k in range(W // LANE)]
                                        + [sh_ref[0:1, :], sh_ref[1:2, :], sh_ref[2:3, :], ga, gb], axis=1))
        put_row("ssm_d", jnp.concatenate([dd_ref[k] for k in range(W // LANE)], axis=1))
        put_row("glu_b", gb_ref[...])
        put_row("ln1_g", l1g_ref[...])
        put_row("ln1_b", l1b_ref[...])
        put_row("ln2_g", l2g_ref[...])
        put_row("ln2_b", l2b_ref[...])
        r0 = _PACK["ssm_lambda"]
        o_ref[r0:r0 + NG, 0:NP] = lr_ref[...]
        o_ref[r0:r0 + NG, NP:2 * NP] = li_ref[...]
        r0 = _PACK["scalars"]
        o_ref[r0:r0 + 1, 0:NG] = dt_ref[...]
        o_ref[r0 + 1:r0 + 2, 0:1] = sq_ref[...]
        for name, ref in (("ssm_b_re", br_ref), ("ssm_b_im", bi_ref), ("ssm_c_re", cr_ref), ("ssm_c_im", ci_ref)):
            r0, l0 = _PACK[name], _BC_LANE[name]
            o_ref[r0:r0 + _HALF, l0:l0 + NP] = pltpu.bitcast(ref[...].astype(bf16), f32)
        for cb in range(W // LANE):
            o_ref[_PACK["conv_w"] + 3 * cb:_PACK["conv_w"] + 3 * cb + 3, :] = cw_ref[:, cb * LANE:(cb + 1) * LANE]

    return pl.pallas_call(body, name="pack_grads", out_shape=S((PACK_ROWS, LANE), f32))(
        su, shcb, sga, sgb, dd, dglu_b, dln1_g, dln1_b, dln2_g, dln2_b, dlam_re, dlam_im, dldt, sqerr, dbr, dbi, dc_re, dc_im,
        dconv)


def adam_small(packed_all, params):
    names = list(_SMALL) + ["conv_w"]
    flat = [a for n in names for a in params[n]]

    def body(*refs):
        p_ref = refs[0]
        ins = refs[1:1 + 3 * len(names)]
        outs = refs[1 + 3 * len(names):-2]
        loss_ref, g_ref = refs[-2], refs[-1]

        def part(k, rs=slice(None), ls=slice(None)):
            return p_ref[k, rs, ls]

        g_all = part(0)
        for k in range(1, NDEV):
            g_all = g_all + part(k)
        g_ref[...] = g_all

        def rows(name, r0, n, l0=0, lanes=LANE):
            return g_ref[_PACK[name] + r0:_PACK[name] + r0 + n, l0:l0 + lanes]

        def grad_of(name):
            if name in dict(_ROWVEC):
                return jnp.concatenate([rows(name, i, 1) for i in range(dict(_ROWVEC)[name] // LANE)], axis=1)
            if name in ("ssm_lambda_re", "ssm_lambda_im"):
                return rows("ssm_lambda", 0, NG, NP * (name == "ssm_lambda_im"), NP)[None]
            if name == "ssm_log_dt":
                return rows("scalars", 0, 1, 0, NG)
            if name in _BC_LANE:
                rs, ls = slice(_PACK[name], _PACK[name] + _HALF), slice(_BC_LANE[name], _BC_LANE[name] + NP)
                g = pltpu.bitcast(part(0, rs, ls), bf16).astype(f32)
                for k in range(1, NDEV):
                    g = g + pltpu.bitcast(part(k, rs, ls), bf16).astype(f32)
                return g.reshape(1, NG, GC, NP)
            full = jnp.concatenate([rows("conv_w", 3 * cb, 3) for cb in range(W // LANE)], axis=1)
            x, y, c = _coords()
            col0 = (4 * x + 2 * y + c) * (W // NDEV)
            sel = (lax.broadcasted_iota(jnp.int32, (W, W // NDEV), 0)
                   == lax.broadcasted_iota(jnp.int32, (W, W // NDEV), 1) + col0).astype(f32)
            return jnp.dot(full, sel, precision=HIGHEST, preferred_element_type=f32)[None]

        loss_ref[...] = 0.5 * rows("scalars", 1, 1, 0, 1)
        for i, name in enumerate(names):
            w_ref, m_ref, v_ref = ins[3 * i:3 * i + 3]
            g = grad_of(name)
            d, m, v = _adam(w_ref[...], g, m_ref[...], v_ref[...])
            outs[4 * i][...] = g
            outs[4 * i + 1][...] = d
            outs[4 * i + 2][...] = m
            outs[4 * i + 3][...] = v

    out_shape = [S(params[n][0].shape, f32) for n in names for _ in range(4)] + [S((1, 1), f32)]
    res = pl.pallas_call(body, name="adam_small", out_shape=out_shape, scratch_shapes=[pltpu.VMEM((PACK_ROWS, LANE), f32)],
                         compiler_params=_cp(None, VMEM_LIMIT))(packed_all, *flat)
    return {n: res[4 * i:4 * i + 4] for i, n in enumerate(names)}, res[-1]


def _block_diag(wgt):
    eye = jnp.eye(8, dtype=wgt.dtype)
    out = wgt[:, :, :, None, :] * eye[None, :, None, :, None]
    return out.reshape(4, 8 * wgt.shape[2], 8 * wgt.shape[3])


def kernel(x, w_in, b_in, ssm_lambda_re, ssm_lambda_im, ssm_log_dt, ssm_b_re, ssm_b_im, ssm_c_re, ssm_c_im, ssm_d, glu_w, glu_b, w_ssm_out, conv_w, w_conv_out, w_o, ln1_g, ln1_b, w_gate, w_up, w_down, ln2_g, ln2_b, loss_target, m_w_in, m_b_in, m_ssm_lambda_re, m_ssm_lambda_im, m_ssm_log_dt, m_ssm_b_re, m_ssm_b_im, m_ssm_c_re, m_ssm_c_im, m_ssm_d, m_glu_w, m_glu_b, m_w_ssm_out, m_conv_w, m_w_conv_out, m_w_o, m_ln1_g, m_ln1_b, m_w_gate, m_w_up, m_w_down, m_ln2_g, m_ln2_b, v_w_in, v_b_in, v_ssm_lambda_re, v_ssm_lambda_im, v_ssm_log_dt, v_ssm_b_re, v_ssm_b_im, v_ssm_c_re, v_ssm_c_im, v_ssm_d, v_glu_w, v_glu_b, v_w_ssm_out, v_conv_w, v_w_conv_out, v_w_o, v_ln1_g, v_ln1_b, v_w_gate, v_w_up, v_w_down, v_ln2_g, v_ln2_b):
    given = dict(locals())
    xs = x[0]
    target = loss_target[0]

    tr = lambda a: jnp.swapaxes(a[0], 0, 1)
    win_s, glu_s, wso_s, wco_s, wo_s, wgT_s, wuT_s, wd_s = prep_weights(
        [w_in[0], glu_w[0], w_ssm_out[0], w_conv_out[0], w_o[0], tr(w_gate), tr(w_up), w_down[0]])
    (win_g,) = run_plan(GatherPlan([win_s], srcs=(0,)), "gather_w_in_u")

    lam_re, lam_im = ssm_lambda_re[0], ssm_lambda_im[0]
    ldt = ssm_log_dt[0].reshape(NG, 1)
    br2 = jnp.swapaxes(ssm_b_re[0], 1, 2).reshape(NG * GC, NP)
    bi2 = jnp.swapaxes(ssm_b_im[0], 1, 2).reshape(NG * GC, NP)
    lbr, lbi, fr, fi, bbr, bbi = ssm_params(lam_re, lam_im, ldt, br2, bi2)
    bb_t = lambda b: b.reshape(4, 8, GC, NP)
    wb = jnp.concatenate([_block_diag(bb_t(bbr)), _block_diag(bb_t(bbi))], axis=2)
    c_t = lambda c: c.reshape(4, 8, GC, NP).transpose(0, 1, 3, 2)
    wc = jnp.concatenate([_block_diag(c_t(ssm_c_re[0])), -_block_diag(c_t(ssm_c_im[0]))], axis=1)
    wbT, wcT = wb.transpose(0, 2, 1), wc.transpose(0, 2, 1)
    wb, wc, wbT, wcT = wb.astype(bf16), wc.astype(bf16), wbT.astype(bf16), wcT.astype(bf16)
    lbr_s, lbi_s = lbr.reshape(4, 1, SW), lbi.reshape(4, 1, SW)
    dsk = ssm_d[0].reshape(4, 1, LANE)

    u_nat, xb = in_proj_u(xs, win_g, b_in)
    half_a, half_b = (0, 3, 5, 6), (1, 2, 4, 7)
    (yn, u_p, xr_p, xi_p), (win_g, conv_g, glu_g, wso_g, wuT_g) = ssm_fwd(
        u_nat, wb, wc, lbr_s, lbi_s, dsk,
        Plans([GatherPlan([win_s], srcs=tuple(range(1, NDEV)), into=[win_g]), GatherPlan([conv_w[0], glu_s, wso_s]),
               GatherPlan([wuT_s], srcs=half_a)]))
    conv_f = conv_g.transpose(1, 0, 2).reshape(3, W)
    (proj,), (wco_g, wo_g, wgT_g) = in_proj_rest(
        xb, win_g, b_in, Plans([GatherPlan([wco_s, wo_s]), GatherPlan([wgT_s], srcs=half_a)]))
    glu_f, wo_f = glu_g.reshape(W, W), wo_g.reshape(D, D)
    yb = conv_fwd(proj, conv_f)
    (merged, ya), (wgT_g,) = merge_fwd(yn, glu_f, glu_b, yb, wso_g, wco_g, proj,
                                       GatherPlan([wgT_s], srcs=half_b, into=[wgT_g]))
    (r1, x1b), (wuT_g,) = mix_ln1(merged, wo_f, xs, ln1_g, ln1_b, GatherPlan([wuT_s], srcs=half_b, into=[wuT_g]))
    wgT, wuT = wgT_g.reshape(F, D), wuT_g.reshape(F, D)
    (gate, up, hid), (wd_g,) = gate_up(x1b, wgT, wuT, GatherPlan([wd_s]))
    wd_f = wd_g.reshape(F, D)
    dr2, dffn, sqerr, dln2_g, dln2_b = down_loss(hid, wd_f, r1, ln1_g, ln1_b, ln2_g, ln2_b, target)

    dwd, _ = mm_tn_rows(hid, dffn, "grad_w_down")
    dwd = dwd.reshape(NDEV, FS, D)
    (dgate, dup), (r_wd,) = ffn_bwd_act(dffn, wd_f, gate, up, ScatterPlan([dwd], only=half_a))
    dwgT, (r_wd,) = mm_tn_rows(dgate, x1b, "grad_w_gate", plan=ScatterPlan([dwd], only=half_b, into=[r_wd]))
    dwgT = dwgT.reshape(NDEV, FS, D)
    dwuT, (r_wgT,) = mm_tn_rows(dup, x1b, "grad_w_up", plan=ScatterPlan([dwgT], only=half_a))
    dwuT = dwuT.reshape(NDEV, FS, D)
    (dr1, dmix, dln1_g, dln1_b), (r_wgT, r_wuT) = ffn_bwd_x(
        dgate, dup, wgT, wuT, dr2, r1, ln1_g,
        Plans([ScatterPlan([dwgT], only=half_b, into=[r_wgT]), ScatterPlan([dwuT], only=half_a)]))
    (dYA, dYB, dga, dgb, sga, sgb, dwo), (r_wuT,) = merge_bwd(dmix, wo_f, merged, ya, yb, wso_g, wco_g, proj,
                                                              ScatterPlan([dwuT], only=half_b, into=[r_wuT]))
    dwo = dwo.reshape(NDEV, D // NDEV, D)
    dya, dyb, dwso, dwco = branches_bwd(dYA, dYB, ya, yb, wso_g, wco_g)
    (dyn, dglu_b, dglu), (r_wso,) = glu_bwd(yn, dya, glu_f, glu_b, ScatterPlan([dwso]))
    dglu = dglu.reshape(NDEV, W // NDEV, W)
    (dh, dcg, dbg, dconv, shcb), (r_wco,) = conv_bwd(proj, dyb, conv_f, ScatterPlan([dwco]))
    dwin, (r_wo, r_glu) = grad_w_in_rest(xb, dh, dcg, dbg, dga, dgb, ScatterPlan([dwo, dglu]))
    (du, dwb, dwcT, dlbr_s, dlbi_s, dd, su), (r_win,) = ssm_bwd(
        u_p, dyn, xr_p, xi_p, wbT, wcT, lbr_s, lbi_s, dsk, ScatterPlan([dwin], only=tuple(range(1, NDEV))))

    dbr2, dbi2, dlam_re, dlam_im, dldt, dc_re, dc_im = ssm_param_bwd(
        lam_re, lam_im, ldt, fr, fi, br2, bi2, dwb, dwcT, dlbr_s.reshape(NG, NP), dlbi_s.reshape(NG, NP))
    packed = pack_grads(su, shcb, sga, sgb, dd, dglu_b, dln1_g, dln1_b, dln2_g, dln2_b, dlam_re, dlam_im, dldt, sqerr,
                        dbr2, dbi2, dc_re, dc_im, dconv)
    dwin_u = mm_tn(xb, du, "grad_w_in_u").reshape(NDEV, D // NDEV, W)

    rest = [(dh, 0, 1), (dcg, 0, 2), (dbg, 0, 3), (dga, 0, 4), (dga, 1, 5), (dgb, 0, 6), (dgb, 1, 7)]
    (gx_rest,), (r_win_u, small_all) = in_proj_bwd_x(
        rest, win_g, dr1, ALPHA, "in_proj_bwd_x_rest", Plans([ScatterPlan([dwin_u]), GatherPlan([packed])]))
    my_rows = sum_blocks(r_win_u, "sum_w_in_u")

    out = {}

    def put(name, res, back=lambda a: a[None]):
        out["grad_" + name], out["delta_" + name], out["new_m_" + name], out["new_v_" + name] = [back(r) for r in res]

    res_wd, (win_u_sum,) = adam_update(w_down[0], m_w_down[0], v_w_down[0], r_wd, "adam_w_down", 176,
                                       plan=ScatterPlan([my_rows], only=(0,), whole=True))
    put("w_down", res_wd)
    (grad_x,), _ = in_proj_bwd_x([(du, 0, 0)], win_g, gx_rest, 1.0, "in_proj_bwd_x_u")
    put("w_in", adam_update(w_in[0], m_w_in[0], v_w_in[0], r_win, "adam_w_in", 256,
                            summed_on_0=win_u_sum.reshape(D, W))[0])
    put("glu_w", adam_update(glu_w[0], m_glu_w[0], v_glu_w[0], r_glu, "adam_glu_w")[0])
    put("w_ssm_out", adam_update(w_ssm_out[0], m_w_ssm_out[0], v_w_ssm_out[0], r_wso, "adam_w_ssm_out")[0])
    put("w_conv_out", adam_update(w_conv_out[0], m_w_conv_out[0], v_w_conv_out[0], r_wco, "adam_w_conv_out")[0])
    put("w_o", adam_update(w_o[0], m_w_o[0], v_w_o[0], r_wo, "adam_w_o")[0])
    untr = lambda a: jnp.swapaxes(a, 0, 1)[None]
    put("w_gate", adam_update(tr(w_gate), tr(m_w_gate), tr(v_w_gate), r_wgT, "adam_w_gate", 176)[0], untr)
    put("w_up", adam_update(tr(w_up), tr(m_w_up), tr(v_w_up), r_wuT, "adam_w_up", 176)[0], untr)
    as_c = lambda a: jnp.swapaxes(a, 2, 3)
    params = {n: (given[n], given["m_" + n], given["v_" + n]) for n in list(_SMALL) + ["conv_w"]}
    for n in ("ssm_b_re", "ssm_b_im"):
        params[n] = tuple(as_c(a) for a in params[n])
    small, loss = adam_small(small_all, params)
    for n, res in small.items():
        put(n, res, as_c if n in ("ssm_b_re", "ssm_b_im") else (lambda a: a))

    names = ["w_in", "b_in", "ssm_lambda_re", "ssm_lambda_im", "ssm_log_dt", "ssm_b_re", "ssm_b_im", "ssm_c_re", "ssm_c_im",
             "ssm_d", "glu_w", "glu_b", "w_ssm_out", "conv_w", "w_conv_out", "w_o", "ln1_g", "ln1_b", "w_gate", "w_up",
             "w_down", "ln2_g", "ln2_b"]
    return (loss.reshape(()), grad_x[None], *[out[p + n] for p in ("grad_", "delta_", "new_m_", "new_v_") for n in names])
```

```python
import functools
import math

import jax
import jax.numpy as jnp
from jax import lax
from jax.experimental import pallas as pl
from jax.experimental.pallas import tpu as pltpu

f32, bf16 = jnp.float32, jnp.bfloat16
S = jax.ShapeDtypeStruct
MESH = pl.DeviceIdType.MESH
HIGHEST = lax.Precision.HIGHEST

D = 1024
W = 512
NG, NP, GC = 32, 64, 16
F = 2816
NDEV = 8
FS = F // NDEV
IN_COLS = 8 * W
ALPHA = 2.0 ** 0.25
LN_EPS = 1e-5
ADAM_LR, ADAM_B1, ADAM_B2, ADAM_EPS, ADAM_WD, ADAM_STEP = 0.001, 0.9, 0.999, 1e-08, 0.01, 10
NC = 32
LANE = 128
SW = 4 * LANE
VMEM_LIMIT = 56 * 1024 * 1024
GRAD_DT = bf16
ANY = pl.BlockSpec(memory_space=pl.ANY)


def _cp(sem=None, vmem=None):
    return pltpu.CompilerParams(dimension_semantics=sem, vmem_limit_bytes=vmem)


def _resident(shape):
    return pl.BlockSpec(shape, lambda i: (0,) * len(shape), pipeline_mode=pl.Buffered(1))


def _dot(a, b):
    return jnp.dot(a, b, preferred_element_type=f32)


def _dot_nt(a, b):
    return lax.dot_general(a, b, (((1,), (1,)), ((), ())), preferred_element_type=f32)


def _dot_tn(a, b):
    return lax.dot_general(a, b, (((0,), (0,)), ((), ())), preferred_element_type=f32)


def _eye(n):
    return (lax.broadcasted_iota(jnp.int32, (n, n), 0) == lax.broadcasted_iota(jnp.int32, (n, n), 1)).astype(f32)


def _transpose_exact(a):
    return lax.dot_general(a, _eye(a.shape[0]), (((0,), (0,)), ((), ())), precision=HIGHEST, preferred_element_type=f32)


def _sigmoid(x):
    return 1.0 / (1.0 + jnp.exp(-x))


_GK = math.sqrt(2.0 / math.pi)


def _gelu(x):
    return 0.5 * x * (1.0 + jnp.tanh(_GK * (x + 0.044715 * x * x * x)))


def _gelu_grad(x):
    th = jnp.tanh(_GK * (x + 0.044715 * x * x * x))
    return 0.5 * (1.0 + th) + 0.5 * x * (1.0 - th * th) * _GK * (1.0 + 3.0 * 0.044715 * x * x)


ROW_PART = 256


def _row_parts(tm):
    return [slice(r, r + min(ROW_PART, tm)) for r in range(0, tm, min(ROW_PART, tm))]


def _ln_stats(r):
    mu = jnp.mean(r, axis=-1, keepdims=True)
    xc = r - mu
    var = jnp.mean(xc * xc, axis=-1, keepdims=True)
    rstd = lax.rsqrt(var + LN_EPS)
    return xc * rstd, rstd


def _ln_bwd(dy, xhat, rstd, g):
    dxh = dy * g
    m1 = jnp.mean(dxh, axis=-1, keepdims=True)
    m2 = jnp.mean(dxh * xhat, axis=-1, keepdims=True)
    return rstd * (dxh - m1 - xhat * m2)


def _coords():
    return lax.axis_index("x"), lax.axis_index("y"), lax.axis_index("c")


def _when(cond, fn):
    if cond is True:
        fn()
    else:
        pl.when(cond)(fn)


class GatherPlan:
    aliases = ()

    def __init__(self, arrs, srcs=None, into=None):
        n = self.n = len(arrs)
        self.srcs = srcs
        self.inputs = list(arrs) + list(into or [])
        if into:
            self.aliases = tuple((n + a, a) for a in range(n))
        self.out_shape = [S((NDEV,) + a.shape, a.dtype) for a in arrs]
        self.sems = [pltpu.SemaphoreType.DMA((n, 7)), pltpu.SemaphoreType.DMA((n, 7)), pltpu.SemaphoreType.DMA((n,))]

    def _has(self, dev):
        if self.srcs is None:
            return True
        idx = 4 * dev[0] + 2 * dev[1] + dev[2]
        return functools.reduce(jnp.logical_or, [idx == s for s in self.srcs])

    def _parts(self, ins, outs, sems):
        n = self.n
        send_sems, recv_sems, loc_sems = sems
        x, y, c = _coords()
        me, sib = (x, y, c), (x, y, 1 - c)
        chips = [(1 - x, y), (x, 1 - y), (1 - x, 1 - y)]

        def slot(a, dev):
            return outs[a].at[4 * dev[0] + 2 * dev[1] + dev[2]]

        def copy(a, k, block, to, src=None):
            return pltpu.make_async_remote_copy(
                src_ref=slot(a, block) if src is None else src, dst_ref=slot(a, block),
                send_sem=send_sems.at[a, k], recv_sem=recv_sems.at[a, k], device_id=to, device_id_type=MESH)

        each = [(j, chip, a) for j, chip in enumerate(chips) for a in range(n)]
        own = self._has(me)
        return dict(
            mine=lambda: [(pltpu.make_async_copy(ins[a], slot(a, me), loc_sems.at[a]), own) for a in range(n)],
            first=lambda: ([(copy(a, 0, me, sib, src=ins[a]), own) for a in range(n)]
                           + [(copy(a, 1 + j, me, (*chip, c), src=ins[a]), own) for j, chip, a in each]),
            landed=lambda: [(copy(a, 1 + j, (*chip, c), me), self._has((*chip, c))) for j, chip, a in each],
            passed=lambda: [(copy(a, 4 + j, (*chip, c), sib), self._has((*chip, c))) for j, chip, a in each],
            from_sib=lambda: ([(copy(a, 0, sib, me), self._has(sib)) for a in range(n)]
                              + [(copy(a, 4 + j, (*chip, 1 - c), me), self._has((*chip, 1 - c))) for j, chip, a in each]))

    def start(self, ins, outs, sems):
        p = self._parts(ins, outs, sems)
        for cp, cond in p["mine"]() + p["first"]():
            _when(cond, cp.start)

    def forward(self, ins, outs, sems):
        p = self._parts(ins, outs, sems)
        for (got, cond), (fwd, _) in zip(p["landed"](), p["passed"]()):
            def relay(got=got, fwd=fwd):
                got.wait_recv()
                fwd.start()

            _when(cond, relay)

    def finish(self, ins, outs, sems):
        p = self._parts(ins, outs, sems)
        for cp, cond in p["from_sib"]():
            _when(cond, cp.wait_recv)
        for cp, cond in p["first"]() + p["passed"]():
            _when(cond, cp.wait_send)
        for cp, cond in p["mine"]():
            _when(cond, cp.wait)


class ScatterPlan:
    aliases = ()

    def __init__(self, gs, only=None, into=None, whole=False):
        n = self.n = len(gs)
        self.only = only
        self.whole = whole
        self.inputs = list(gs) + list(into or [])
        if into:
            self.aliases = tuple((n + a, a) for a in range(n))
        self.out_shape = [S((NDEV,) + g.shape if whole else g.shape, g.dtype) for g in gs]
        self.sems = [pltpu.SemaphoreType.DMA((n, 7)), pltpu.SemaphoreType.DMA((n, 7)), pltpu.SemaphoreType.DMA((n,))]

    def _owner(self, idx):
        if self.only is None:
            return True
        return functools.reduce(jnp.logical_or, [idx == b for b in self.only])

    def _copies(self, ins, outs, sems):
        n = self.n
        send_sems, recv_sems, loc_sems = sems
        x, y, c = _coords()
        me = 4 * x + 2 * y + c
        mine = self._owner(me)
        block = (lambda a, k: ins[a]) if self.whole else (lambda a, k: ins[a].at[k])
        copies = [(pltpu.make_async_copy(block(a, me), outs[a].at[me], loc_sems.at[a]), mine, None) for a in range(n)]
        for m in range(1, NDEV):
            px = 1 - x if m & 4 else x
            py = 1 - y if m & 2 else y
            pc = 1 - c if m & 1 else c
            peer = 4 * px + 2 * py + pc
            for a in range(n):
                copies.append((pltpu.make_async_remote_copy(
                    src_ref=block(a, peer), dst_ref=outs[a].at[me],
                    send_sem=send_sems.at[a, m - 1], recv_sem=recv_sems.at[a, m - 1],
                    device_id=(px, py, pc), device_id_type=MESH), self._owner(peer), mine))
        return copies

    def start(self, ins, outs, sems):
        for cp, sends, _ in self._copies(ins, outs, sems):
            _when(sends, cp.start)

    def forward(self, ins, outs, sems):
        pass

    def finish(self, ins, outs, sems):
        for cp, sends, receives in self._copies(ins, outs, sems):
            if receives is None:
                _when(sends, cp.wait)
            else:
                _when(sends, cp.wait_send)
                _when(receives, cp.wait_recv)


class Plans:
    def __init__(self, plans):
        self.plans = plans
        self.inputs = [a for p in plans for a in p.inputs]
        self.out_shape = [s for p in plans for s in p.out_shape]
        self.sems = [s for p in plans for s in p.sems]
        self.aliases, i, o = [], 0, 0
        for p in plans:
            self.aliases += [(i + a, o + b) for a, b in p.aliases]
            i, o = i + len(p.inputs), o + len(p.out_shape)

    def _each(self, what, ins, outs, sems):
        i = o = s = 0
        for p in self.plans:
            ni, no, ns = len(p.inputs), len(p.out_shape), len(p.sems)
            getattr(p, what)(ins[i:i + ni], outs[o:o + no], sems[s:s + ns])
            i, o, s = i + ni, o + no, s + ns

    def start(self, ins, outs, sems):
        self._each("start", ins, outs, sems)

    def forward(self, ins, outs, sems):
        self._each("forward", ins, outs, sems)

    def finish(self, ins, outs, sems):
        self._each("finish", ins, outs, sems)


def _call(body, args, *, name, grid, in_specs, out_specs, out_shape, scratch=(), sem=None, vmem=None, plan=None,
          relay_step=None, relay_after=False):
    if plan is None:
        outs = pl.pallas_call(body, name=name, grid=grid, in_specs=list(in_specs), out_specs=list(out_specs),
                              out_shape=list(out_shape), scratch_shapes=list(scratch),
                              compiler_params=_cp(sem, vmem))(*args)
        return list(outs), []
    ni, no, ns = len(in_specs), len(out_specs), len(scratch)
    pi, po = len(plan.inputs), len(plan.out_shape)
    aliases = {ni + a: no + b for a, b in plan.aliases}

    def wrapped(*refs):
        main_in, p_in = refs[:ni], refs[ni:ni + pi]
        main_out, p_out = refs[ni + pi:ni + pi + no], refs[ni + pi + no:ni + pi + no + po]
        main_scr, p_sems = refs[ni + pi + no + po:ni + pi + no + po + ns], refs[ni + pi + no + po + ns:]
        ids = [pl.program_id(d) for d in range(len(grid))]
        first = functools.reduce(jnp.logical_and, [i == 0 for i in ids])
        last = functools.reduce(jnp.logical_and, [i == g - 1 for i, g in zip(ids, grid)])

        @pl.when(first)
        def _():
            plan.start(p_in, p_out, p_sems)

        if not relay_after:
            @pl.when(last if relay_step is None else ids[0] == max(relay_step, 0))
            def _():
                plan.forward(p_in, p_out, p_sems)

        body(*main_in, *main_out, *main_scr)

        @pl.when(last)
        def _():
            if relay_after:
                plan.forward(p_in, p_out, p_sems)
            plan.finish(p_in, p_out, p_sems)

    outs = pl.pallas_call(
        wrapped, name=name, grid=grid, in_specs=list(in_specs) + [ANY] * pi, out_specs=list(out_specs) + [ANY] * po,
        out_shape=list(out_shape) + list(plan.out_shape), scratch_shapes=list(scratch) + list(plan.sems),
        input_output_aliases=aliases, compiler_params=_cp(("arbitrary",) * len(grid), vmem),
    )(*args, *plan.inputs)
    return list(outs[:no]), list(outs[no:])


def run_plan(plan, name):
    def body(*refs):
        ins, outs, sems = refs[:len(plan.inputs)], refs[len(plan.inputs):len(plan.inputs) + len(plan.out_shape)], \
            refs[len(plan.inputs) + len(plan.out_shape):]
        plan.start(ins, outs, sems)
        plan.forward(ins, outs, sems)
        plan.finish(ins, outs, sems)

    return pl.pallas_call(body, name=name, in_specs=[ANY] * len(plan.inputs), out_specs=[ANY] * len(plan.out_shape),
                          out_shape=list(plan.out_shape), scratch_shapes=list(plan.sems))(*plan.inputs)


def mm_tn(a, b, name, tn=512):
    T, K = a.shape
    N = b.shape[1]
    tn = min(tn, N)

    def body(a_ref, b_ref, o_ref):
        o_ref[...] = _dot_tn(a_ref[...], b_ref[...]).astype(GRAD_DT)

    (out,), _ = _call(body, [a, b], name=name, grid=(N // tn,),
                      in_specs=[_resident((T, K)), pl.BlockSpec((T, tn), lambda j: (0, j))],
                      out_specs=[pl.BlockSpec((None, K, tn), lambda j: (j, 0, 0))],
                      out_shape=[S((N // tn, K, tn), GRAD_DT)], sem=("parallel",), vmem=VMEM_LIMIT)
    return out


def grad_w_in_rest(xb, dh, dcg, dbg, dga, dgb, plan):
    T = xb.shape[0]
    order = ((0, 0), (1, 1), (2, 2), (3, 3), (4, 3), (5, 4), (6, 4))

    def body(x_ref, *refs):
        o_ref = refs[-1]
        j = pl.program_id(0)
        for step, opnd in order:
            @pl.when(j == step)
            def _(opnd=opnd):
                o_ref[...] = _dot_tn(x_ref[...], refs[opnd][...]).astype(GRAD_DT)

    once = lambda: pl.BlockSpec((T, W), lambda j: (0, 0), pipeline_mode=pl.Buffered(1))
    (out,), sent = _call(
        body, [xb, dh, dcg, dbg, dga, dgb], name="grad_w_in_rest", grid=(len(order),),
        in_specs=[_resident((T, D)), once(), once(), once(),
                  pl.BlockSpec((T, W), lambda j: (0, jnp.clip(j - 3, 0, 1))),
                  pl.BlockSpec((T, W), lambda j: (0, jnp.clip(j - 5, 0, 1)))],
        out_specs=[pl.BlockSpec((None, D, W), lambda j: (1 + j, 0, 0))],
        out_shape=[S((NDEV, D, W), GRAD_DT)], sem=("arbitrary",), vmem=VMEM_LIMIT, plan=plan)
    return out, sent


def mm_tn_rows(a, b, name, tk=256, plan=None):
    T, K = a.shape
    N = b.shape[1]
    tk = min(tk, K)

    def body(a_ref, b_ref, o_ref):
        o_ref[...] = _dot_tn(a_ref[...], b_ref[...]).astype(GRAD_DT)

    (out,), sent = _call(body, [a, b], name=name, grid=(K // tk,),
                         in_specs=[pl.BlockSpec((T, tk), lambda i: (0, i)), _resident((T, N))],
                         out_specs=[pl.BlockSpec((tk, N), lambda i: (i, 0))], out_shape=[S((K, N), GRAD_DT)],
                         sem=("parallel",), vmem=VMEM_LIMIT, plan=plan)
    return out, sent


def prep_weights(ws):
    def body(*refs):
        for i in range(len(ws)):
            refs[len(ws) + i][...] = refs[i][...].astype(bf16)

    return pl.pallas_call(body, name="prep_weights", out_shape=[S(w.shape, bf16) for w in ws],
                          compiler_params=_cp(None, VMEM_LIMIT))(*ws)


REST_BLOCKS = (4, 5, 6, 7, 1, 2, 3)
REST_COLS = len(REST_BLOCKS) * W


def in_proj_u(x, win_g, b_in):
    T = x.shape[0]
    tm = min(1024, T)

    def body(x_ref, w_ref, b_ref, u_ref, xb_ref):
        xb = x_ref[...].astype(bf16)
        xb_ref[...] = xb
        u_ref[...] = _dot(xb, w_ref[...]) + b_ref[...]

    row = pl.BlockSpec((tm, D), lambda i: (i, 0))
    return pl.pallas_call(
        body, name="in_proj_u", grid=(T // tm,),
        in_specs=[row, pl.BlockSpec((None, D, W), lambda i: (0, 0, 0)), pl.BlockSpec((1, W), lambda i: (0, 0))],
        out_specs=[pl.BlockSpec((tm, W), lambda i: (i, 0)), row],
        out_shape=[S((T, W), f32), S((T, D), bf16)], compiler_params=_cp(("parallel",), VMEM_LIMIT),
    )(x, win_g, b_in)


def in_proj_rest(xb, win_g, b_in, plan):
    T = xb.shape[0]
    tm = min(512, T)

    def body(x_ref, w_ref, b_ref, o_ref):
        xb_ = x_ref[...]
        for i, k in enumerate(REST_BLOCKS):
            o_ref[:, i * W:(i + 1) * W] = _dot(xb_, w_ref[k]) + b_ref[:, k * W:(k + 1) * W]

    return _call(
        body, [xb, win_g, b_in], name="in_proj_rest", grid=(T // tm,),
        in_specs=[pl.BlockSpec((tm, D), lambda i: (i, 0)), _resident((NDEV, D, W)), _resident((1, IN_COLS))],
        out_specs=[pl.BlockSpec((tm, REST_COLS), lambda i: (i, 0))],
        out_shape=[S((T, REST_COLS), f32)], vmem=VMEM_LIMIT, plan=plan, relay_step=T // tm - 2)


def _to_scan_order(a_ref, o_ref):
    L = a_ref.shape[0] // NC

    def step(jb, carry):
        j0 = pl.multiple_of(jb * 8, 8)
        for q in range(NC // 8):
            x = jnp.stack([a_ref[pl.ds((8 * q + c) * L + j0, 8), :] for c in range(8)], axis=0)
            y = jnp.swapaxes(x, 0, 1)
            for j in range(8):
                o_ref[pl.ds((j0 + j) * NC + 8 * q, 8), :] = y[j]
        return carry

    lax.fori_loop(0, L // 8, step, 0)


def _to_time_order(a_ref, o_ref):
    L = a_ref.shape[0] // NC

    def step(jb, carry):
        j0 = pl.multiple_of(jb * 16, 16)
        for q in range(NC // 8):
            halves = []
            for h in range(2):
                x = jnp.stack([a_ref[pl.ds((j0 + 8 * h + j) * NC + 8 * q, 8), :] for j in range(8)], axis=0)
                halves.append(jnp.swapaxes(x, 0, 1))
            for c in range(8):
                o_ref[pl.ds((8 * q + c) * L + j0, 16), :] = jnp.concatenate(
                    [halves[0][c], halves[1][c]], axis=0).astype(o_ref.dtype)
        return carry

    lax.fori_loop(0, L // 16, step, 0)


def _disc(lr, li, ldt):
    dt = jnp.exp(ldt)
    mag = jnp.exp(lr * dt)
    lbr = mag * jnp.cos(li * dt)
    lbi = mag * jnp.sin(li * dt)
    den = lr * lr + li * li
    nr = lbr - 1.0
    return lbr, lbi, (nr * lr + lbi * li) / den, (lbi * lr - nr * li) / den


def _per_channel(f):
    return jnp.broadcast_to(f[:, None, :], (NG, GC, NP)).reshape(NG * GC, NP)


def ssm_params(lam_re, lam_im, log_dt, br, bi):
    def body(lr_ref, li_ref, ldt_ref, br_ref, bi_ref, lbr_ref, lbi_ref, fr_ref, fi_ref, bbr_ref, bbi_ref):
        lbr, lbi, fr, fi = _disc(lr_ref[...], li_ref[...], ldt_ref[...])
        lbr_ref[...], lbi_ref[...], fr_ref[...], fi_ref[...] = lbr, lbi, fr, fi
        fr_, fi_, br_, bi_ = _per_channel(fr), _per_channel(fi), br_ref[...], bi_ref[...]
        bbr_ref[...] = fr_ * br_ - fi_ * bi_
        bbi_ref[...] = fr_ * bi_ + fi_ * br_

    return pl.pallas_call(body, name="ssm_params", out_shape=[S((NG, NP), f32)] * 4 + [S((NG * GC, NP), f32)] * 2)(
        lam_re, lam_im, log_dt, br, bi)


SCAN_UNROLL = 4
SCAN_LANES = 2 * LANE


def _steps(n, body, carry):
    main = n // SCAN_UNROLL

    def trip(t, c):
        for q in range(SCAN_UNROLL):
            c = body(t * SCAN_UNROLL + q, c)
        return c

    carry = lax.fori_loop(0, main, trip, carry)
    for i in range(main * SCAN_UNROLL, n):
        carry = body(i, carry)
    return carry


def _scan_body(T):
    L = T // NC
    RB = min(512, T)
    nsq = int(round(math.log2(L)))
    assert 2 ** nsq == L and T % RB == 0 and L % 16 == 0

    def rows(i):
        return pl.ds(pl.multiple_of(i * RB, RB), RB)

    def tile(j):
        return pl.ds(j * NC if isinstance(j, int) else pl.multiple_of(j * NC, NC), NC)

    def forward_states(u_ref, wb_ref, lbr_ref, lbi_ref, sre, sim, ere, eim):
        def bproj(i, carry):
            bu = _dot(u_ref[rows(i), :].astype(bf16), wb_ref[...])
            sre[rows(i), :] = bu[:, :SW]
            sim[rows(i), :] = bu[:, SW:]
            return carry

        lax.fori_loop(0, T // RB, bproj, 0)
        for lb in range(SW // SCAN_LANES):
            ls = slice(lb * SCAN_LANES, (lb + 1) * SCAN_LANES)
            ar = jnp.broadcast_to(lbr_ref[:, ls], (NC, SCAN_LANES))
            ai = jnp.broadcast_to(lbi_ref[:, ls], (NC, SCAN_LANES))

            def step(j, carry):
                xr, xi = carry
                nr = ar * xr - ai * xi + sre[tile(j), ls]
                ni = ar * xi + ai * xr + sim[tile(j), ls]
                sre[tile(j), ls] = nr
                sim[tile(j), ls] = ni
                return nr, ni

            zero = jnp.zeros((NC, SCAN_LANES), f32)
            _steps(L, step, (zero, zero))
            pr, pi = lbr_ref[:, ls], lbi_ref[:, ls]
            for _ in range(nsq):
                pr, pi = pr * pr - pi * pi, 2.0 * pr * pi
            er = jnp.zeros((1, SCAN_LANES), f32)
            ei = er
            ere[0:1, ls] = er
            eim[0:1, ls] = ei
            base = (L - 1) * NC
            for c in range(1, NC):
                lr_ = sre[base + c - 1:base + c, ls]
                li_ = sim[base + c - 1:base + c, ls]
                er, ei = lr_ + pr * er - pi * ei, li_ + pr * ei + pi * er
                ere[c:c + 1, ls] = er
                eim[c:c + 1, ls] = ei
            e_r, e_i = ere[:, ls].reshape(NC // 8, 8, SCAN_LANES), eim[:, ls].reshape(NC // 8, 8, SCAN_LANES)
            ar8, ai8 = ar[0:8], ai[0:8]

            def fix(j, carry):
                pwr, pwi = carry
                xr = sre[tile(j), ls].reshape(NC // 8, 8, SCAN_LANES) + (pwr * e_r - pwi * e_i)
                xi = sim[tile(j), ls].reshape(NC // 8, 8, SCAN_LANES) + (pwr * e_i + pwi * e_r)
                sre[tile(j), ls] = xr.reshape(NC, SCAN_LANES)
                sim[tile(j), ls] = xi.reshape(NC, SCAN_LANES)
                return pwr * ar8 - pwi * ai8, pwr * ai8 + pwi * ar8

            _steps(L, fix, (ar8, ai8))

    return L, RB, nsq, rows, tile, forward_states


def ssm_fwd(u, wb, wc, lbr, lbi, dsk, plan):
    T = u.shape[0]
    L, RB, nsq, rows, tile, forward_states = _scan_body(T)
    nslab = W // LANE

    def body(u_ref, wb_ref, wc_ref, lbr_ref, lbi_ref, d_ref, y_ref, up_ref, xr_ref, xi_ref, sre, sim, ere, eim, yp):
        _to_scan_order(u_ref, up_ref)
        forward_states(up_ref, wb_ref, lbr_ref, lbi_ref, sre, sim, ere, eim)

        def cproj(i, carry):
            xr, xi = sre[rows(i), :].astype(bf16), sim[rows(i), :].astype(bf16)
            xr_ref[rows(i), :] = xr
            xi_ref[rows(i), :] = xi
            y = _dot(xr, wc_ref[0:SW, :]) + _dot(xi, wc_ref[SW:, :])
            yp[rows(i), :] = y + d_ref[...] * up_ref[rows(i), :]
            return carry

        lax.fori_loop(0, T // RB, cproj, 0)
        _to_time_order(yp, y_ref)

    slab = pl.BlockSpec((T, LANE), lambda k: (0, k))
    states = pl.BlockSpec((T, SW), lambda k: (0, k))
    return _call(
        body, [u, wb, wc, lbr, lbi, dsk], name="ssm_fwd", grid=(nslab,),
        in_specs=[slab, pl.BlockSpec((None, LANE, 2 * SW), lambda k: (k, 0, 0)),
                  pl.BlockSpec((None, 2 * SW, LANE), lambda k: (k, 0, 0)),
                  pl.BlockSpec((None, 1, SW), lambda k: (k, 0, 0)), pl.BlockSpec((None, 1, SW), lambda k: (k, 0, 0)),
                  pl.BlockSpec((None, 1, LANE), lambda k: (k, 0, 0))],
        out_specs=[slab, slab, states, states],
        out_shape=[S((T, W), f32), S((T, W), f32), S((T, nslab * SW), bf16), S((T, nslab * SW), bf16)],
        scratch=[pltpu.VMEM((T, SW), f32), pltpu.VMEM((T, SW), f32), pltpu.VMEM((NC, SW), f32), pltpu.VMEM((NC, SW), f32),
                 pltpu.VMEM((T, LANE), f32)],
        vmem=VMEM_LIMIT, plan=plan, relay_after=True)


def ssm_bwd(u_p, dy, xr, xi, wbT, wcT, lbr, lbi, dsk, plan):
    T = u_p.shape[0]
    L, RB, nsq, rows, tile, _ = _scan_body(T)

    def body(u_ref, dyt_ref, sre, sim, wbT_ref, wcT_ref, lbr_ref, lbi_ref, d_ref,
             dut_ref, dwb_ref, dwc_ref, dlr_ref, dli_ref, dd_ref, su_ref, gre, gim, ere, eim, dy_ref, du_ref):
        _to_scan_order(dyt_ref, dy_ref)

        def dstate(i, carry):
            g = _dot(dy_ref[rows(i), :].astype(bf16), wcT_ref[...])
            gre[rows(i), :] = g[:, :SW]
            gim[rows(i), :] = g[:, SW:]
            return carry

        lax.fori_loop(0, T // RB, dstate, 0)
        row = lax.broadcasted_iota(jnp.int32, (NC, SCAN_LANES), 0)
        for lb in range(SW // SCAN_LANES):
            ls = slice(lb * SCAN_LANES, (lb + 1) * SCAN_LANES)
            ar = jnp.broadcast_to(lbr_ref[:, ls], (NC, SCAN_LANES))
            ai = jnp.broadcast_to(lbi_ref[:, ls], (NC, SCAN_LANES))

            def step(i, carry):
                gr, gi = carry
                j = L - 1 - i
                nr = ar * gr + ai * gi + gre[tile(j), ls]
                ni = ar * gi - ai * gr + gim[tile(j), ls]
                gre[tile(j), ls] = nr
                gim[tile(j), ls] = ni
                return nr, ni

            zero = jnp.zeros((NC, SCAN_LANES), f32)
            _steps(L, step, (zero, zero))
            pr, pi = lbr_ref[:, ls], -lbi_ref[:, ls]
            for _ in range(nsq):
                pr, pi = pr * pr - pi * pi, 2.0 * pr * pi
            er = jnp.zeros((1, SCAN_LANES), f32)
            ei = er
            ere[NC - 1:NC, ls] = er
            eim[NC - 1:NC, ls] = ei
            for c in range(NC - 2, -1, -1):
                lr_ = gre[c + 1:c + 2, ls]
                li_ = gim[c + 1:c + 2, ls]
                er, ei = lr_ + pr * er - pi * ei, li_ + pr * ei + pi * er
                ere[c:c + 1, ls] = er
                eim[c:c + 1, ls] = ei
            e_r, e_i = ere[:, ls].reshape(NC // 8, 8, SCAN_LANES), eim[:, ls].reshape(NC // 8, 8, SCAN_LANES)
            ar8, ai8 = ar[0:8], ai[0:8]

            def fixed(j, pwr, pwi):
                gr = (gre[tile(j), ls].reshape(NC // 8, 8, SCAN_LANES) + (pwr * e_r - pwi * e_i)).reshape(NC, SCAN_LANES)
                gi = (gim[tile(j), ls].reshape(NC // 8, 8, SCAN_LANES) + (pwr * e_i + pwi * e_r)).reshape(NC, SCAN_LANES)
                gre[tile(j), ls] = gr
                gim[tile(j), ls] = gi
                return gr, gi

            def fix(i, carry):
                pwr, pwi, accr, acci = carry
                j = L - 1 - i
                gr, gi = fixed(j, pwr, pwi)
                xr, xi = sre[tile(j - 1), ls].astype(f32), sim[tile(j - 1), ls].astype(f32)
                return (pwr * ar8 + pwi * ai8, pwi * ar8 - pwr * ai8,
                        accr + gr * xr + gi * xi, acci + gi * xr - gr * xi)

            pwr, pwi, accr, acci = _steps(L - 1, fix, (ar8, -ai8, zero, zero))
            gr, gi = fixed(0, pwr, pwi)
            xr = jnp.where(row == 0, 0.0, pltpu.roll(sre[tile(L - 1), ls].astype(f32), 1, axis=0))
            xi = jnp.where(row == 0, 0.0, pltpu.roll(sim[tile(L - 1), ls].astype(f32), 1, axis=0))
            accr = accr + gr * xr + gi * xi
            acci = acci + gi * xr - gr * xi
            dlr_ref[:, ls] = jnp.sum(accr, axis=0, keepdims=True)
            dli_ref[:, ls] = jnp.sum(acci, axis=0, keepdims=True)

        dwb_ref[...] = jnp.zeros_like(dwb_ref)
        dwc_ref[...] = jnp.zeros_like(dwc_ref)
        dd_ref[...] = jnp.zeros_like(dd_ref)
        su_ref[...] = jnp.zeros_like(su_ref)

        def finish(i, carry):
            u32, dy32 = u_ref[rows(i), :], dy_ref[rows(i), :]
            ub, dyb = u32.astype(bf16), dy32.astype(bf16)
            gr, gi = gre[rows(i), :].astype(bf16), gim[rows(i), :].astype(bf16)
            du = _dot(gr, wbT_ref[0:SW, :]) + _dot(gi, wbT_ref[SW:, :]) + dy32 * d_ref[...]
            du_ref[rows(i), :] = du
            su_ref[...] += jnp.sum(du, axis=0, keepdims=True)
            dwb_ref[:, 0:SW] += _dot_tn(ub, gr)
            dwb_ref[:, SW:] += _dot_tn(ub, gi)
            dwc_ref[:, 0:SW] += _dot_tn(dyb, sre[rows(i), :])
            dwc_ref[:, SW:] += _dot_tn(dyb, sim[rows(i), :])
            dd_ref[...] += jnp.sum(dy32 * u32, axis=0, keepdims=True)
            return carry

        lax.fori_loop(0, T // RB, finish, 0)
        _to_time_order(du_ref, dut_ref)

    slab = pl.BlockSpec((T, LANE), lambda k: (0, k))
    wide = pl.BlockSpec((None, LANE, 2 * SW), lambda k: (k, 0, 0))
    tall = pl.BlockSpec((None, 2 * SW, LANE), lambda k: (k, 0, 0))
    vec = pl.BlockSpec((None, 1, SW), lambda k: (k, 0, 0))
    vecd = pl.BlockSpec((None, 1, LANE), lambda k: (k, 0, 0))
    states = pl.BlockSpec((T, SW), lambda k: (0, k))
    nslab = W // LANE
    return _call(
        body, [u_p, dy, xr, xi, wbT, wcT, lbr, lbi, dsk], name="ssm_bwd", grid=(nslab,),
        in_specs=[slab, slab, states, states, tall, wide, vec, vec, vecd],
        out_specs=[slab, wide, wide, vec, vec, vecd, vecd],
        out_shape=[S((T, W), bf16), S((nslab, LANE, 2 * SW), f32), S((nslab, LANE, 2 * SW), f32),
                   S((nslab, 1, SW), f32), S((nslab, 1, SW), f32), S((nslab, 1, LANE), f32), S((nslab, 1, LANE), f32)],
        scratch=[pltpu.VMEM((T, SW), f32)] * 2 + [pltpu.VMEM((NC, SW), f32)] * 2 + [pltpu.VMEM((T, LANE), f32)] * 2,
        vmem=VMEM_LIMIT, plan=plan)


def _shift_rows(cur, prev8, k):
    return pltpu.roll(jnp.concatenate([prev8, cur], axis=0), k, axis=0)[8:]


def _lift_rows(cur, next8, k):
    n = cur.shape[0]
    return pltpu.roll(jnp.concatenate([cur, next8], axis=0), n + 8 - k, axis=0)[:n]


def conv_fwd(proj, conv_w):
    T = proj.shape[0]
    RB = min(512, T)

    def body(h_ref, c_ref, b_ref, w_ref, o_ref):
        w0, w1, w2 = w_ref[0:1, :], w_ref[1:2, :], w_ref[2:3, :]

        def blk(i, carry):
            r0 = pl.multiple_of(i * RB, RB)
            rs = pl.ds(r0, RB)
            ch = c_ref[rs, :] * h_ref[rs, :]
            pr = pl.ds(jnp.maximum(r0 - 8, 0), 8)
            prev = jnp.where(i > 0, c_ref[pr, :] * h_ref[pr, :], 0.0)
            z = w2 * ch + w1 * _shift_rows(ch, prev, 1) + w0 * _shift_rows(ch, prev, 2)
            o_ref[rs, :] = (b_ref[rs, :] * z).astype(bf16)
            return carry

        lax.fori_loop(0, T // RB, blk, 0)

    nb = W // LANE
    return pl.pallas_call(
        body, name="conv_fwd", grid=(nb,),
        in_specs=[pl.BlockSpec((T, LANE), lambda k: (0, 4 * nb + k)), pl.BlockSpec((T, LANE), lambda k: (0, 5 * nb + k)),
                  pl.BlockSpec((T, LANE), lambda k: (0, 6 * nb + k)),pl.BlockSpec((3, LANE), lambda k: (0, k))],
        out_specs=pl.BlockSpec((T, LANE), lambda k: (0, k)), out_shape=S((T, W), bf16),
        compiler_params=_cp(("parallel",), VMEM_LIMIT),
    )(proj, proj, proj, conv_w)


def _dense_columns(blocks_ref, dense_ref):
    for k in range(NDEV):
        dense_ref[:, k * LANE:(k + 1) * LANE] = blocks_ref[k]


def merge_fwd(yn, glu_w, glu_b, yb, wso, wco, proj, plan):
    T = yn.shape[0]
    tm = min(1024, T)

    def body(y_ref, gw_ref, gbias_ref, yb_ref, wa_ref, wb_ref, ga_ref, gb_ref, o_ref, ya_ref, wa_s, wb_s):
        @pl.when(pl.program_id(0) == 0)
        def _():
            _dense_columns(wa_ref, wa_s)
            _dense_columns(wb_ref, wb_s)

        for rs in _row_parts(tm):
            g = _gelu(y_ref[rs, :])
            ya = (g * _sigmoid(_dot(g.astype(bf16), gw_ref[...]) + gbias_ref[...])).astype(bf16)
            ya_ref[rs, :] = ya
            o_ref[rs, :] = (_sigmoid(ga_ref[rs, :]) * _dot(ya, wa_s[...])
                            + _sigmoid(gb_ref[rs, :]) * _dot(yb_ref[rs, :], wb_s[...])).astype(bf16)

    act = pl.BlockSpec((tm, W), lambda i: (i, 0))
    return _call(
        body, [yn, glu_w, glu_b, yb, wso, wco, proj, proj], name="merge_fwd", grid=(T // tm,),
        in_specs=[act, pl.BlockSpec((W, W), lambda i: (0, 0)), pl.BlockSpec((1, W), lambda i: (0, 0)), act,
                  _resident((NDEV, W, LANE)), _resident((NDEV, W, LANE)),
                  pl.BlockSpec((tm, D), lambda i: (i, 0)), pl.BlockSpec((tm, D), lambda i: (i, 1))],
        out_specs=[pl.BlockSpec((tm, D), lambda i: (i, 0)), act], out_shape=[S((T, D), bf16), S((T, W), bf16)],
        scratch=[pltpu.VMEM((W, D), bf16), pltpu.VMEM((W, D), bf16)], vmem=VMEM_LIMIT, plan=plan)


def mix_ln1(merged, w_o, x, g1, b1, plan):
    T = x.shape[0]
    tm = min(512, T)

    def body(m_ref, w_ref, x_ref, g_ref, b_ref, r_ref, x1_ref):
        for rs in _row_parts(tm):
            r = ALPHA * x_ref[rs, :] + _dot(m_ref[rs, :], w_ref[...])
            r_ref[rs, :] = r
            xhat, _ = _ln_stats(r)
            x1_ref[rs, :] = (xhat * g_ref[...] + b_ref[...]).astype(bf16)

    row = pl.BlockSpec((tm, D), lambda i: (i, 0))
    vec = pl.BlockSpec((1, D), lambda i: (0, 0))
    return _call(
        body, [merged, w_o, x, g1, b1], name="mix_ln1", grid=(T // tm,),
        in_specs=[row, _resident((D, D)), row, vec, vec],
        out_specs=[row, row], out_shape=[S((T, D), f32), S((T, D), bf16)], sem=("parallel",), vmem=VMEM_LIMIT, plan=plan,
        relay_step=T // tm - 2)


FT = 256


def gate_up(x1b, wgT, wuT, plan):
    T = x1b.shape[0]
    tm = min(512, T)

    def body(x_ref, wg_ref, wu_ref, g_ref, u_ref, h_ref):
        x = x_ref[...]
        for n in range(F // FT):
            cs = slice(n * FT, (n + 1) * FT)
            g = _dot_nt(x, wg_ref[cs, :])
            u = _dot_nt(x, wu_ref[cs, :])
            g_ref[:, cs] = g.astype(bf16)
            u_ref[:, cs] = u.astype(bf16)
            h_ref[:, cs] = (g * _sigmoid(g) * u).astype(bf16)

    osp = pl.BlockSpec((tm, F), lambda i: (i, 0))
    return _call(
        body, [x1b, wgT, wuT], name="gate_up", grid=(T // tm,),
        in_specs=[pl.BlockSpec((tm, D), lambda i: (i, 0)), _resident((F, D)), _resident((F, D))],
        out_specs=[osp, osp, osp], out_shape=[S((T, F), bf16)] * 3, vmem=VMEM_LIMIT, plan=plan, relay_step=T // tm - 3)


def down_loss(hid, w_down, r1, g1, b1, g2, b2, target):
    T = hid.shape[0]
    tm = min(512, T)

    def body(h_ref, w_ref, r1_ref, g1_ref, b1_ref, g2_ref, b2_ref, t_ref, dr_ref, drb_ref, loss_ref, dg_ref, db_ref):
        @pl.when(pl.program_id(0) == 0)
        def _():
            loss_ref[...] = jnp.zeros_like(loss_ref)
            dg_ref[...] = jnp.zeros_like(dg_ref)
            db_ref[...] = jnp.zeros_like(db_ref)

        for rs in _row_parts(tm):
            xh1, _ = _ln_stats(r1_ref[rs, :])
            x1 = xh1 * g1_ref[...] + b1_ref[...]
            r2 = ALPHA * x1 + _dot(h_ref[rs, :], w_ref[...])
            xh2, rstd2 = _ln_stats(r2)
            err = xh2 * g2_ref[...] + b2_ref[...] - t_ref[rs, :]
            loss_ref[...] += jnp.sum(jnp.mean(err * err, axis=-1, keepdims=True), axis=0, keepdims=True)
            dy = err * (1.0 / D)
            dg_ref[...] += jnp.sum(dy * xh2, axis=0, keepdims=True)
            db_ref[...] += jnp.sum(dy, axis=0, keepdims=True)
            dr = _ln_bwd(dy, xh2, rstd2, g2_ref[...])
            dr_ref[rs, :] = dr
            drb_ref[rs, :] = dr.astype(bf16)

    row = pl.BlockSpec((tm, D), lambda i: (i, 0))
    vec = pl.BlockSpec((1, D), lambda i: (0, 0))
    return pl.pallas_call(
        body, name="down_loss", grid=(T // tm,),
        in_specs=[pl.BlockSpec((tm, F), lambda i: (i, 0)), _resident((F, D)), row, vec, vec, vec, vec, row],
        out_specs=[row, row, pl.BlockSpec((1, 1), lambda i: (0, 0)), vec, vec],
        out_shape=[S((T, D), f32), S((T, D), bf16), S((1, 1), f32), S((1, D), f32), S((1, D), f32)],
        compiler_params=_cp(("arbitrary",), VMEM_LIMIT),
    )(hid, w_down, r1, g1, b1, g2, b2, target)


def ffn_bwd_act(dffn, w_down, gate, up, plan):
    T = dffn.shape[0]
    tm = min(512, T)

    def body(d_ref, w_ref, g_ref, u_ref, dg_ref, du_ref):
        for n in range(F // FT):
            cs = slice(n * FT, (n + 1) * FT)
            for rs in _row_parts(tm):
                dh = _dot_nt(d_ref[rs, :], w_ref[cs, :])
                g, u = g_ref[rs, cs].astype(f32), u_ref[rs, cs].astype(f32)
                sg = _sigmoid(g)
                t = g * sg
                du_ref[rs, cs] = (dh * t).astype(bf16)
                dg_ref[rs, cs] = (dh * u * (sg + t - t * sg)).astype(bf16)

    osp = pl.BlockSpec((tm, F), lambda i: (i, 0))
    return _call(
        body, [dffn, w_down, gate, up], name="ffn_bwd_act", grid=(T // tm,),
        in_specs=[pl.BlockSpec((tm, D), lambda i: (i, 0)), _resident((F, D)), osp, osp],
        out_specs=[osp, osp], out_shape=[S((T, F), bf16)] * 2, sem=("parallel",), vmem=VMEM_LIMIT, plan=plan)


def ffn_bwd_x(dgate, dup, wgT, wuT, dr2, r1, g1, plan):
    T = dr2.shape[0]
    tm = min(512, T)

    def body(dg_ref, du_ref, wg_ref, wu_ref, dr2_ref, r1_ref, g1_ref, dr_ref, drb_ref, dgam_ref, dbet_ref):
        @pl.when(pl.program_id(0) == 0)
        def _():
            dgam_ref[...] = jnp.zeros_like(dgam_ref)
            dbet_ref[...] = jnp.zeros_like(dbet_ref)

        for rs in _row_parts(tm):
            dx1 = ALPHA * dr2_ref[rs, :] + _dot(dg_ref[rs, :], wg_ref[...]) + _dot(du_ref[rs, :], wu_ref[...])
            xh, rstd = _ln_stats(r1_ref[rs, :])
            dgam_ref[...] += jnp.sum(dx1 * xh, axis=0, keepdims=True)
            dbet_ref[...] += jnp.sum(dx1, axis=0, keepdims=True)
            dr = _ln_bwd(dx1, xh, rstd, g1_ref[...])
            dr_ref[rs, :] = dr
            drb_ref[rs, :] = dr.astype(bf16)

    row = pl.BlockSpec((tm, D), lambda i: (i, 0))
    wide = pl.BlockSpec((tm, F), lambda i: (i, 0))
    wsp = _resident((F, D))
    vec = pl.BlockSpec((1, D), lambda i: (0, 0))
    return _call(
        body, [dgate, dup, wgT, wuT, dr2, r1, g1], name="ffn_bwd_x", grid=(T // tm,),
        in_specs=[wide, wide, wsp, wsp, row, row, vec],
        out_specs=[row, row, vec, vec], out_shape=[S((T, D), f32), S((T, D), bf16), S((1, D), f32), S((1, D), f32)],
        vmem=VMEM_LIMIT, plan=plan)


def merge_bwd(dmix, w_o, merged, ya, yb, wso, wco, proj, plan):
    T = dmix.shape[0]
    tm = min(512, T)

    def body(dm_ref, wo_ref, m_ref, ya_ref, yb_ref, wa_ref, wb_ref, ga_ref, gb_ref,
             dya_ref, dyb_ref, dga_ref, dgb_ref, sa_ref, sb_ref, dwo_ref, wa_s, wb_s, acc):
        @pl.when(pl.program_id(0) == 0)
        def _():
            _dense_columns(wa_ref, wa_s)
            _dense_columns(wb_ref, wb_s)
            acc[...] = jnp.zeros_like(acc)

        acc[...] += _dot_tn(m_ref[...], dm_ref[...])

        @pl.when(pl.program_id(0) == pl.num_programs(0) - 1)
        def _():
            dwo_ref[...] = acc[...].astype(GRAD_DT)

        dmer = _dot_nt(dm_ref[...], wo_ref[...])
        sa, sb = _sigmoid(ga_ref[...]), _sigmoid(gb_ref[...])
        dya_ref[...] = (dmer * sa).astype(bf16)
        dyb_ref[...] = (dmer * sb).astype(bf16)
        dga = dmer * _dot(ya_ref[...], wa_s[...]) * sa * (1.0 - sa)
        dgb = dmer * _dot(yb_ref[...], wb_s[...]) * sb * (1.0 - sb)
        dga_ref[...] = dga.astype(bf16)
        dgb_ref[...] = dgb.astype(bf16)
        sa_ref[...] = jnp.sum(dga, axis=0, keepdims=True)
        sb_ref[...] = jnp.sum(dgb, axis=0, keepdims=True)

    act = pl.BlockSpec((tm, W), lambda i: (i, 0))
    osp = pl.BlockSpec((tm, D), lambda i: (i, 0))
    ssp = pl.BlockSpec((None, 1, D), lambda i: (i, 0, 0))
    return _call(
        body, [dmix, w_o, merged, ya, yb, wso, wco, proj, proj], name="merge_bwd", grid=(T // tm,),
        in_specs=[osp, _resident((D, D)), osp, act, act, _resident((NDEV, W, LANE)), _resident((NDEV, W, LANE)),
                  pl.BlockSpec((tm, D), lambda i: (i, 0)), pl.BlockSpec((tm, D), lambda i: (i, 1))],
        out_specs=[osp, osp, osp, osp, ssp, ssp, pl.BlockSpec((D, D), lambda i: (0, 0))],
        out_shape=[S((T, D), bf16)] * 4 + [S((T // tm, 1, D), f32)] * 2 + [S((D, D), GRAD_DT)],
        scratch=[pltpu.VMEM((W, D), bf16), pltpu.VMEM((W, D), bf16), pltpu.VMEM((D, D), f32)], vmem=VMEM_LIMIT, plan=plan)


def branches_bwd(dYA, dYB, ya, yb, wso, wco):
    T = dYA.shape[0]
    tm = min(1024, T)

    def body(da_ref, db_ref, ya_ref, yb_ref, wa_ref, wb_ref, oa_ref, ob_ref, ga_ref, gb_ref, wa_s, wb_s, acc_a, acc_b):
        @pl.when(pl.program_id(0) == 0)
        def _():
            _dense_columns(wa_ref, wa_s)
            _dense_columns(wb_ref, wb_s)
            acc_a[...] = jnp.zeros_like(acc_a)
            acc_b[...] = jnp.zeros_like(acc_b)

        oa_ref[...] = _dot_nt(da_ref[...], wa_s[...])
        ob_ref[...] = _dot_nt(db_ref[...], wb_s[...])
        acc_a[...] += _dot_tn(ya_ref[...], da_ref[...])
        acc_b[...] += _dot_tn(yb_ref[...], db_ref[...])

        @pl.when(pl.program_id(0) == pl.num_programs(0) - 1)
        def _():
            for k in range(NDEV):
                ga_ref[k] = acc_a[:, k * LANE:(k + 1) * LANE].astype(GRAD_DT)
                gb_ref[k] = acc_b[:, k * LANE:(k + 1) * LANE].astype(GRAD_DT)

    row = pl.BlockSpec((tm, D), lambda i: (i, 0))
    osp = pl.BlockSpec((tm, W), lambda i: (i, 0))
    blocks = pl.BlockSpec((NDEV, W, LANE), lambda i: (0, 0, 0))
    outs, _ = _call(
        body, [dYA, dYB, ya, yb, wso, wco], name="branches_bwd", grid=(T // tm,),
        in_specs=[row, row, osp, osp, _resident((NDEV, W, LANE)), _resident((NDEV, W, LANE))],
        out_specs=[osp, osp, blocks, blocks], out_shape=[S((T, W), f32)] * 2 + [S((NDEV, W, LANE), GRAD_DT)] * 2,
        scratch=[pltpu.VMEM((W, D), bf16)] * 2 + [pltpu.VMEM((W, D), f32)] * 2, sem=("arbitrary",), vmem=VMEM_LIMIT)
    return outs


def glu_bwd(yn, dya, glu_w, glu_b, plan):
    T = yn.shape[0]
    tm = min(512, T)

    def body(y_ref, d_ref, w_ref, b_ref, dy_ref, db_ref, dw_ref, acc):
        @pl.when(pl.program_id(0) == 0)
        def _():
            db_ref[...] = jnp.zeros_like(db_ref)
            acc[...] = jnp.zeros_like(acc)

        y, dya_ = y_ref[...], d_ref[...]
        g = _gelu(y)
        gb = g.astype(bf16)
        s = _sigmoid(_dot(gb, w_ref[...]) + b_ref[...])
        dsp = dya_ * g * s * (1.0 - s)
        dspb = dsp.astype(bf16)
        dg = dya_ * s + _dot_nt(dspb, w_ref[...])
        dy_ref[...] = dg * _gelu_grad(y)
        db_ref[...] += jnp.sum(dsp, axis=0, keepdims=True)
        acc[...] += _dot_tn(gb, dspb)

        @pl.when(pl.program_id(0) == pl.num_programs(0) - 1)
        def _():
            dw_ref[...] = acc[...].astype(GRAD_DT)

    row = pl.BlockSpec((tm, W), lambda i: (i, 0))
    vec = pl.BlockSpec((1, W), lambda i: (0, 0))
    mat = pl.BlockSpec((W, W), lambda i: (0, 0))
    return _call(
        body, [yn, dya, glu_w, glu_b], name="glu_bwd", grid=(T // tm,),
        in_specs=[row, row, mat, vec],
        out_specs=[row, vec, mat], out_shape=[S((T, W), f32), S((1, W), f32), S((W, W), GRAD_DT)],
        scratch=[pltpu.VMEM((W, W), f32)], sem=("arbitrary",), plan=plan)


def conv_bwd(proj, dyb, conv_w, plan):
    T = proj.shape[0]
    RB = min(512, T)
    nrb = T // RB

    def body(h_ref, c_ref, b_ref, d_ref, w_ref, dh_ref, dc_ref, db_ref, dw_ref, s_ref):
        w0, w1, w2 = w_ref[0:1, :], w_ref[1:2, :], w_ref[2:3, :]

        def blk(i, carry):
            a0, a1, a2, sh, sc, sb = carry
            r0 = pl.multiple_of(i * RB, RB)
            rs = pl.ds(r0, RB)
            h, cg, bg, dyb_ = h_ref[rs, :], c_ref[rs, :], b_ref[rs, :], d_ref[rs, :]
            ch = cg * h
            pr = pl.ds(jnp.maximum(r0 - 8, 0), 8)
            prev = jnp.where(i > 0, c_ref[pr, :] * h_ref[pr, :], 0.0)
            ch1, ch2 = _shift_rows(ch, prev, 1), _shift_rows(ch, prev, 2)
            dbg = dyb_ * (w2 * ch + w1 * ch1 + w0 * ch2)
            db_ref[rs, :] = dbg.astype(bf16)
            dz = dyb_ * bg
            nx = pl.ds(jnp.minimum(r0 + RB, T - 8), 8)
            nxt = jnp.where(i < nrb - 1, d_ref[nx, :] * b_ref[nx, :], 0.0)
            dch = w2 * dz + w1 * _lift_rows(dz, nxt, 1) + w0 * _lift_rows(dz, nxt, 2)
            dcg, dh = dch * h, dch * cg
            dc_ref[rs, :] = dcg.astype(bf16)
            dh_ref[rs, :] = dh.astype(bf16)
            col = lambda v: jnp.sum(v, axis=0, keepdims=True)
            return (a0 + col(dz * ch2), a1 + col(dz * ch1), a2 + col(dz * ch), sh + col(dh), sc + col(dcg), sb + col(dbg))

        zero = jnp.zeros((1, LANE), f32)
        a0, a1, a2, sh, sc, sb = lax.fori_loop(0, nrb, blk, (zero,) * 6)
        dw_ref[0:1, :] = a0
        dw_ref[1:2, :] = a1
        dw_ref[2:3, :] = a2
        s_ref[0:1, :] = sh
        s_ref[1:2, :] = sc
        s_ref[2:3, :] = sb

    nb = W // LANE
    slab = pl.BlockSpec((T, LANE), lambda k: (0, k))
    three = pl.BlockSpec((3, LANE), lambda k: (0, k))
    return _call(
        body, [proj, proj, proj, dyb, conv_w], name="conv_bwd", grid=(nb,),
        in_specs=[pl.BlockSpec((T, LANE), lambda k: (0, 4 * nb + k)), pl.BlockSpec((T, LANE), lambda k: (0, 5 * nb + k)),
                  pl.BlockSpec((T, LANE), lambda k: (0, 6 * nb + k)), slab, three],
        out_specs=[slab, slab, slab, three, three],
        out_shape=[S((T, W), bf16)] * 3 + [S((3, W), f32)] * 2, sem=("parallel",), vmem=VMEM_LIMIT, plan=plan)


def in_proj_bwd_x(parts, win_g, base, scale, name, plan=None):
    T = base.shape[0]
    tm = min(512, T)
    n = len(parts)

    def body(*refs):
        p_refs, w_ref, b_ref, o_ref = refs[:n], refs[n], refs[n + 1], refs[n + 2]
        acc = scale * b_ref[...]
        for p_ref, (_, _, k) in zip(p_refs, parts):
            acc += _dot_nt(p_ref[...], w_ref[k])
        o_ref[...] = acc

    row = pl.BlockSpec((tm, D), lambda i: (i, 0))
    p_specs = [pl.BlockSpec((tm, W), (lambda i, cb=cb: (i, cb))) for _, cb, _ in parts]
    return _call(
        body, [a for a, _, _ in parts] + [win_g, base], name=name, grid=(T // tm,),
        in_specs=p_specs + [_resident((NDEV, D, W)), row],
        out_specs=[row], out_shape=[S((T, D), f32)], vmem=VMEM_LIMIT, plan=plan)


def ssm_param_bwd(lam_re, lam_im, log_dt, fr, fi, br, bi, dwb, dwcT, dlbr, dlbi):
    def body(lr_ref, li_ref, ldt_ref, fr_ref, fi_ref, br_ref, bi_ref, dwb_ref, dwc_ref, dlbr_ref, dlbi_ref,
             dbr_ref, dbi_ref, dlr_ref, dli_ref, dldt_ref, dcr_ref, dci_ref, dr_s, di_s):
        for k in range(W // LANE):
            for gl in range(NG // (W // LANE)):
                rows, src = slice((8 * k + gl) * GC, (8 * k + gl + 1) * GC), slice(gl * GC, (gl + 1) * GC)
                re, im = slice(gl * NP, (gl + 1) * NP), slice(SW + gl * NP, SW + (gl + 1) * NP)
                dr_s[rows, :] = dwb_ref[k, src, re]
                di_s[rows, :] = dwb_ref[k, src, im]
                dcr_ref[rows, :] = dwc_ref[k, src, re]
                dci_ref[rows, :] = -dwc_ref[k, src, im]
        fr_, fi_ = _per_channel(fr_ref[...]), _per_channel(fi_ref[...])
        br_, bi_, dr, di = br_ref[...], bi_ref[...], dr_s[...], di_s[...]
        dbr_ref[...] = fr_ * dr + fi_ * di
        dbi_ref[...] = fr_ * di - fi_ * dr
        dfr = jnp.sum((dr * br_ + di * bi_).reshape(NG, GC, NP), axis=1)
        dfi = jnp.sum((di * br_ - dr * bi_).reshape(NG, GC, NP), axis=1)
        _, vjp = jax.vjp(_disc, lr_ref[...], li_ref[...], ldt_ref[...])
        dlr_ref[...], dli_ref[...], dldt = vjp((dlbr_ref[...], dlbi_ref[...], dfr, dfi))
        dldt_ref[...] = _transpose_exact(dldt)

    blk = S((NG * GC, NP), f32)
    return pl.pallas_call(
        body, name="ssm_param_bwd", out_shape=[blk, blk, S((NG, NP), f32), S((NG, NP), f32), S((1, NG), f32), blk, blk],
        scratch_shapes=[pltpu.VMEM((NG * GC, NP), f32)] * 2)(
        lam_re, lam_im, log_dt, fr, fi, br, bi, dwb, dwcT, dlbr, dlbi)


def _adam(w, g, m, v):
    m = ADAM_B1 * m + (1.0 - ADAM_B1) * g
    v = ADAM_B2 * v + (1.0 - ADAM_B2) * (g * g)
    m_hat = m / (1.0 - ADAM_B1 ** ADAM_STEP)
    v_hat = v / (1.0 - ADAM_B2 ** ADAM_STEP)
    return -ADAM_LR * (m_hat / (jnp.sqrt(v_hat) + ADAM_EPS) + ADAM_WD * w), m, v


def _sum_in_order(c_ref):
    g = c_ref[0].astype(f32)
    for k in range(1, c_ref.shape[0]):
        g = g + c_ref[k].astype(f32)
    return g


def sum_blocks(contrib, name):
    def body(c_ref, o_ref):
        o_ref[...] = _sum_in_order(c_ref)

    return pl.pallas_call(body, name=name, out_shape=S(contrib.shape[1:], f32))(contrib)


def adam_update(w, m, v, contrib, name, rows_per_block=None, summed_on_0=None, plan=None):
    R, C = w.shape
    n = contrib.shape[0]
    tr = min(rows_per_block or R, R)

    def body(w_ref, m_ref, v_ref, c_ref, *refs):
        g_ref, d_ref, nm_ref, nv_ref = refs[-4:]
        g = _sum_in_order(c_ref)
        if summed_on_0 is not None:
            x, y, c = _coords()
            g = jnp.where(4 * x + 2 * y + c == 0, refs[0][...], g)
        g_ref[...] = g
        d_ref[...], nm_ref[...], nv_ref[...] = _adam(w_ref[...], g, m_ref[...], v_ref[...])

    blk = pl.BlockSpec((tr, C), lambda i: (i, 0))
    extra = [] if summed_on_0 is None else [summed_on_0]
    return _call(
        body, [w, m, v, contrib] + extra, name=name, grid=(R // tr,),
        in_specs=[blk, blk, blk, pl.BlockSpec((n, tr, C), lambda i: (0, i, 0))] + [blk] * len(extra),
        out_specs=[blk] * 4, out_shape=[S((R, C), f32)] * 4, sem=("parallel",), vmem=VMEM_LIMIT, plan=plan)


_ROWVEC = (("b_in", IN_COLS), ("ssm_d", W), ("glu_b", W), ("ln1_g", D), ("ln1_b", D), ("ln2_g", D), ("ln2_b", D))
_HALF = NG * GC // 2
_BC_LANE = {"ssm_b_re": 0, "ssm_b_im": NP, "ssm_c_re": 0, "ssm_c_im": NP}
_PACK = {}
_r = 0
for _n, _k in _ROWVEC:
    _PACK[_n] = _r
    _r += _k // LANE
for _n, _rows in (("ssm_lambda", NG), ("scalars", 8), ("ssm_b", _HALF), ("ssm_c", _HALF), ("conv_w", 16)):
    _PACK[_n] = _r
    _r += _rows
for _n in _BC_LANE:
    _PACK[_n] = _PACK[_n[:5]]
PACK_ROWS = _r
assert PACK_ROWS % 8 == 0
_SMALL = ("b_in", "ssm_lambda_re", "ssm_lambda_im", "ssm_log_dt", "ssm_b_re", "ssm_b_im", "ssm_c_re", "ssm_c_im",
          "ssm_d", "glu_b", "ln1_g", "ln1_b", "ln2_g", "ln2_b")


def pack_grads(su, shcb, sga, sgb, dd, dglu_b, dln1_g, dln1_b, dln2_g, dln2_b, dlam_re, dlam_im, dldt, sqerr, dbr, dbi,
               dc_re, dc_im, dconv):
    nI = sga.shape[0]

    def body(su_ref, sh_ref, sga_ref, sgb_ref, dd_ref, gb_ref, l1g_ref, l1b_ref, l2g_ref, l2b_ref, lr_ref, li_ref, dt_ref,
             sq_ref, br_ref, bi_ref, cr_ref, ci_ref, cw_ref, o_ref):
        o_ref[...] = jnp.zeros_like(o_ref)

        def put_row(name, v):
            r0 = _PACK[name]
            for i in range(v.shape[1] // LANE):
                o_ref[r0 + i:r0 + i + 1, :] = v[:, i * LANE:(i + 1) * LANE]

        ga, gb = sga_ref[0], sgb_ref[0]
        for i in range(1, nI):
            ga, gb = ga + sga_ref[i], gb + sgb_ref[i]
        put_row("b_in", jnp.concatenate([su_ref[k] for k in range(W // LANE)]
                                        + [sh_ref[0:1, :], sh_ref[1:2, :], sh_ref[2:3, :], ga, gb], axis=1))
        put_row("ssm_d", jnp.concatenate([dd_ref[k] for k in range(W // LANE)], axis=1))
        put_row("glu_b", gb_ref[...])
        put_row("ln1_g", l1g_ref[...])
        put_row("ln1_b", l1b_ref[...])
        put_row("ln2_g", l2g_ref[...])
        put_row("ln2_b", l2b_ref[...])
        r0 = _PACK["ssm_lambda"]
        o_ref[r0:r0 + NG, 0:NP] = lr_ref[...]
        o_ref[r0:r0 + NG, NP:2 * NP] = li_ref[...]
        r0 = _PACK["scalars"]
        o_ref[r0:r0 + 1, 0:NG] = dt_ref[...]
        o_ref[r0 + 1:r0 + 2, 0:1] = sq_ref[...]
        for name, ref in (("ssm_b_re", br_ref), ("ssm_b_im", bi_ref), ("ssm_c_re", cr_ref), ("ssm_c_im", ci_ref)):
            r0, l0 = _PACK[name], _BC_LANE[name]
            o_ref[r0:r0 + _HALF, l0:l0 + NP] = pltpu.bitcast(ref[...].astype(bf16), f32)
        for cb in range(W // LANE):
            o_ref[_PACK["conv_w"] + 3 * cb:_PACK["conv_w"] + 3 * cb + 3, :] = cw_ref[:, cb * LANE:(cb + 1) * LANE]

    return pl.pallas_call(body, name="pack_grads", out_shape=S((PACK_ROWS, LANE), f32))(
        su, shcb, sga, sgb, dd, dglu_b, dln1_g, dln1_b, dln2_g, dln2_b, dlam_re, dlam_im, dldt, sqerr, dbr, dbi, dc_re, dc_im,
        dconv)


def adam_small(packed_all, params):
    names = list(_SMALL) + ["conv_w"]
    flat = [a for n in names for a in params[n]]

    def body(*refs):
        p_ref = refs[0]
        ins = refs[1:1 + 3 * len(names)]
        outs = refs[1 + 3 * len(names):-2]
        loss_ref, g_ref = refs[-2], refs[-1]

        def part(k, rs=slice(None), ls=slice(None)):
            return p_ref[k, rs, ls]

        g_all = part(0)
        for k in range(1, NDEV):
            g_all = g_all + part(k)
        g_ref[...] = g_all

        def rows(name, r0, n, l0=0, lanes=LANE):
            return g_ref[_PACK[name] + r0:_PACK[name] + r0 + n, l0:l0 + lanes]

        def grad_of(name):
            if name in dict(_ROWVEC):
                return jnp.concatenate([rows(name, i, 1) for i in range(dict(_ROWVEC)[name] // LANE)], axis=1)
            if name in ("ssm_lambda_re", "ssm_lambda_im"):
                return rows("ssm_lambda", 0, NG, NP * (name == "ssm_lambda_im"), NP)[None]
            if name == "ssm_log_dt":
                return rows("scalars", 0, 1, 0, NG)
            if name in _BC_LANE:
                rs, ls = slice(_PACK[name], _PACK[name] + _HALF), slice(_BC_LANE[name], _BC_LANE[name] + NP)
                g = pltpu.bitcast(part(0, rs, ls), bf16).astype(f32)
                for k in range(1, NDEV):
                    g = g + pltpu.bitcast(part(k, rs, ls), bf16).astype(f32)
                return g.reshape(1, NG, GC, NP)
            full = jnp.concatenate([rows("conv_w", 3 * cb, 3) for cb in range(W // LANE)], axis=1)
            x, y, c = _coords()
            col0 = (4 * x + 2 * y + c) * (W // NDEV)
            sel = (lax.broadcasted_iota(jnp.int32, (W, W // NDEV), 0)
                   == lax.broadcasted_iota(jnp.int32, (W, W // NDEV), 1) + col0).astype(f32)
            return jnp.dot(full, sel, precision=HIGHEST, preferred_element_type=f32)[None]

        loss_ref[...] = 0.5 * rows("scalars", 1, 1, 0, 1)
        for i, name in enumerate(names):
            w_ref, m_ref, v_ref = ins[3 * i:3 * i + 3]
            g = grad_of(name)
            d, m, v = _adam(w_ref[...], g, m_ref[...], v_ref[...])
            outs[4 * i][...] = g
            outs[4 * i + 1][...] = d
            outs[4 * i + 2][...] = m
            outs[4 * i + 3][...] = v

    out_shape = [S(params[n][0].shape, f32) for n in names for _ in range(4)] + [S((1, 1), f32)]
    res = pl.pallas_call(body, name="adam_small", out_shape=out_shape, scratch_shapes=[pltpu.VMEM((PACK_ROWS, LANE), f32)],
                         compiler_params=_cp(None, VMEM_LIMIT))(packed_all, *flat)
    return {n: res[4 * i:4 * i + 4] for i, n in enumerate(names)}, res[-1]


def _block_diag(wgt):
    eye = jnp.eye(8, dtype=wgt.dtype)
    out = wgt[:, :, :, None, :] * eye[None, :, None, :, None]
    return out.reshape(4, 8 * wgt.shape[2], 8 * wgt.shape[3])


def kernel(x, w_in, b_in, ssm_lambda_re, ssm_lambda_im, ssm_log_dt, ssm_b_re, ssm_b_im, ssm_c_re, ssm_c_im, ssm_d, glu_w, glu_b, w_ssm_out, conv_w, w_conv_out, w_o, ln1_g, ln1_b, w_gate, w_up, w_down, ln2_g, ln2_b, loss_target, m_w_in, m_b_in, m_ssm_lambda_re, m_ssm_lambda_im, m_ssm_log_dt, m_ssm_b_re, m_ssm_b_im, m_ssm_c_re, m_ssm_c_im, m_ssm_d, m_glu_w, m_glu_b, m_w_ssm_out, m_conv_w, m_w_conv_out, m_w_o, m_ln1_g, m_ln1_b, m_w_gate, m_w_up, m_w_down, m_ln2_g, m_ln2_b, v_w_in, v_b_in, v_ssm_lambda_re, v_ssm_lambda_im, v_ssm_log_dt, v_ssm_b_re, v_ssm_b_im, v_ssm_c_re, v_ssm_c_im, v_ssm_d, v_glu_w, v_glu_b, v_w_ssm_out, v_conv_w, v_w_conv_out, v_w_o, v_ln1_g, v_ln1_b, v_w_gate, v_w_up, v_w_down, v_ln2_g, v_ln2_b):
    given = dict(locals())
    xs = x[0]
    target = loss_target[0]

    tr = lambda a: jnp.swapaxes(a[0], 0, 1)
    win_s, glu_s, wso_s, wco_s, wo_s, wgT_s, wuT_s, wd_s = prep_weights(
        [w_in[0], glu_w[0], w_ssm_out[0], w_conv_out[0], w_o[0], tr(w_gate), tr(w_up), w_down[0]])
    (win_g,) = run_plan(GatherPlan([win_s], srcs=(0,)), "gather_w_in_u")

    lam_re, lam_im = ssm_lambda_re[0], ssm_lambda_im[0]
    ldt = ssm_log_dt[0].reshape(NG, 1)
    br2 = jnp.swapaxes(ssm_b_re[0], 1, 2).reshape(NG * GC, NP)
    bi2 = jnp.swapaxes(ssm_b_im[0], 1, 2).reshape(NG * GC, NP)
    lbr, lbi, fr, fi, bbr, bbi = ssm_params(lam_re, lam_im, ldt, br2, bi2)
    bb_t = lambda b: b.reshape(4, 8, GC, NP)
    wb = jnp.concatenate([_block_diag(bb_t(bbr)), _block_diag(bb_t(bbi))], axis=2)
    c_t = lambda c: c.reshape(4, 8, GC, NP).transpose(0, 1, 3, 2)
    wc = jnp.concatenate([_block_diag(c_t(ssm_c_re[0])), -_block_diag(c_t(ssm_c_im[0]))], axis=1)
    wbT, wcT = wb.transpose(0, 2, 1), wc.transpose(0, 2, 1)
    wb, wc, wbT, wcT = wb.astype(bf16), wc.astype(bf16), wbT.astype(bf16), wcT.astype(bf16)
    lbr_s, lbi_s = lbr.reshape(4, 1, SW), lbi.reshape(4, 1, SW)
    dsk = ssm_d[0].reshape(4, 1, LANE)

    u_nat, xb = in_proj_u(xs, win_g, b_in)
    half_a, half_b = (0, 3, 5, 6), (1, 2, 4, 7)
    (yn, u_p, xr_p, xi_p), (win_g, conv_g, glu_g, wso_g, wuT_g) = ssm_fwd(
        u_nat, wb, wc, lbr_s, lbi_s, dsk,
        Plans([GatherPlan([win_s], srcs=tuple(range(1, NDEV)), into=[win_g]), GatherPlan([conv_w[0], glu_s, wso_s]),
               GatherPlan([wuT_s], srcs=half_a)]))
    conv_f = conv_g.transpose(1, 0, 2).reshape(3, W)
    (proj,), (wco_g, wo_g, wgT_g) = in_proj_rest(
        xb, win_g, b_in, Plans([GatherPlan([wco_s, wo_s]), GatherPlan([wgT_s], srcs=half_a)]))
    glu_f, wo_f = glu_g.reshape(W, W), wo_g.reshape(D, D)
    yb = conv_fwd(proj, conv_f)
    (merged, ya), (wgT_g,) = merge_fwd(yn, glu_f, glu_b, yb, wso_g, wco_g, proj,
                                       GatherPlan([wgT_s], srcs=half_b, into=[wgT_g]))
    (r1, x1b), (wuT_g,) = mix_ln1(merged, wo_f, xs, ln1_g, ln1_b, GatherPlan([wuT_s], srcs=half_b, into=[wuT_g]))
    wgT, wuT = wgT_g.reshape(F, D), wuT_g.reshape(F, D)
    (gate, up, hid), (wd_g,) = gate_up(x1b, wgT, wuT, GatherPlan([wd_s]))
    wd_f = wd_g.reshape(F, D)
    dr2, dffn, sqerr, dln2_g, dln2_b = down_loss(hid, wd_f, r1, ln1_g, ln1_b, ln2_g, ln2_b, target)

    dwd, _ = mm_tn_rows(hid, dffn, "grad_w_down")
    dwd = dwd.reshape(NDEV, FS, D)
    (dgate, dup), (r_wd,) = ffn_bwd_act(dffn, wd_f, gate, up, ScatterPlan([dwd], only=half_a))
    dwgT, (r_wd,) = mm_tn_rows(dgate, x1b, "grad_w_gate", plan=ScatterPlan([dwd], only=half_b, into=[r_wd]))
    dwgT = dwgT.reshape(NDEV, FS, D)
    dwuT, (r_wgT,) = mm_tn_rows(dup, x1b, "grad_w_up", plan=ScatterPlan([dwgT], only=half_a))
    dwuT = dwuT.reshape(NDEV, FS, D)
    (dr1, dmix, dln1_g, dln1_b), (r_wgT, r_wuT) = ffn_bwd_x(
        dgate, dup, wgT, wuT, dr2, r1, ln1_g,
        Plans([ScatterPlan([dwgT], only=half_b, into=[r_wgT]), ScatterPlan([dwuT], only=half_a)]))
    (dYA, dYB, dga, dgb, sga, sgb, dwo), (r_wuT,) = merge_bwd(dmix, wo_f, merged, ya, yb, wso_g, wco_g, proj,
                                                              ScatterPlan([dwuT], only=half_b, into=[r_wuT]))
    dwo = dwo.reshape(NDEV, D // NDEV, D)
    dya, dyb, dwso, dwco = branches_bwd(dYA, dYB, ya, yb, wso_g, wco_g)
    (dyn, dglu_b, dglu), (r_wso,) = glu_bwd(yn, dya, glu_f, glu_b, ScatterPlan([dwso]))
    dglu = dglu.reshape(NDEV, W // NDEV, W)
    (dh, dcg, dbg, dconv, shcb), (r_wco,) = conv_bwd(proj, dyb, conv_f, ScatterPlan([dwco]))
    dwin, (r_wo, r_glu) = grad_w_in_rest(xb, dh, dcg, dbg, dga, dgb, ScatterPlan([dwo, dglu]))
    (du, dwb, dwcT, dlbr_s, dlbi_s, dd, su), (r_win,) = ssm_bwd(
        u_p, dyn, xr_p, xi_p, wbT, wcT, lbr_s, lbi_s, dsk, ScatterPlan([dwin], only=tuple(range(1, NDEV))))

    dbr2, dbi2, dlam_re, dlam_im, dldt, dc_re, dc_im = ssm_param_bwd(
        lam_re, lam_im, ldt, fr, fi, br2, bi2, dwb, dwcT, dlbr_s.reshape(NG, NP), dlbi_s.reshape(NG, NP))
    packed = pack_grads(su, shcb, sga, sgb, dd, dglu_b, dln1_g, dln1_b, dln2_g, dln2_b, dlam_re, dlam_im, dldt, sqerr,
                        dbr2, dbi2, dc_re, dc_im, dconv)
    dwin_u = mm_tn(xb, du, "grad_w_in_u").reshape(NDEV, D // NDEV, W)

    rest = [(dh, 0, 1), (dcg, 0, 2), (dbg, 0, 3), (dga, 0, 4), (dga, 1, 5), (dgb, 0, 6), (dgb, 1, 7)]
    (gx_rest,), (r_win_u, small_all) = in_proj_bwd_x(
        rest, win_g, dr1, ALPHA, "in_proj_bwd_x_rest", Plans([ScatterPlan([dwin_u]), GatherPlan([packed])]))
    my_rows = sum_blocks(r_win_u, "sum_w_in_u")

    out = {}

    def put(name, res, back=lambda a: a[None]):
        out["grad_" + name], out["delta_" + name], out["new_m_" + name], out["new_v_" + name] = [back(r) for r in res]

    res_wd, (win_u_sum,) = adam_update(w_down[0], m_w_down[0], v_w_down[0], r_wd, "adam_w_down", 176,
                                       plan=ScatterPlan([my_rows], only=(0,), whole=True))
    put("w_down", res_wd)
    (grad_x,), _ = in_proj_bwd_x([(du, 0, 0)], win_g, gx_rest, 1.0, "in_proj_bwd_x_u")
    put("w_in", adam_update(w_in[0], m_w_in[0], v_w_in[0], r_win, "adam_w_in", 256,
                            summed_on_0=win_u_sum.reshape(D, W))[0])
    put("glu_w", adam_update(glu_w[0], m_glu_w[0], v_glu_w[0], r_glu, "adam_glu_w")[0])
    put("w_ssm_out", adam_update(w_ssm_out[0], m_w_ssm_out[0], v_w_ssm_out[0], r_wso, "adam_w_ssm_out")[0])
    put("w_conv_out", adam_update(w_conv_out[0], m_w_conv_out[0], v_w_conv_out[0], r_wco, "adam_w_conv_out")[0])
    put("w_o", adam_update(w_o[0], m_w_o[0], v_w_o[0], r_wo, "adam_w_o")[0])
    untr = lambda a: jnp.swapaxes(a, 0, 1)[None]
    put("w_gate", adam_update(tr(w_gate), tr(m_w_gate), tr(v_w_gate), r_wgT, "adam_w_gate", 176)[0], untr)
    put("w_up", adam_update(tr(w_up), tr(m_w_up), tr(v_w_up), r_wuT, "adam_w_up", 176)[0], untr)
    as_c = lambda a: jnp.swapaxes(a, 2, 3)
    params = {n: (given[n], given["m_" + n], given["v_" + n]) for n in list(_SMALL) + ["conv_w"]}
    for n in ("ssm_b_re", "ssm_b_im"):
        params[n] = tuple(as_c(a) for a in params[n])
    small, loss = adam_small(small_all, params)
    for n, res in small.items():
        put(n, res, as_c if n in ("ssm_b_re", "ssm_b_im") else (lambda a: a))

    names = ["w_in", "b_in", "ssm_lambda_re", "ssm_lambda_im", "ssm_log_dt", "ssm_b_re", "ssm_b_im", "ssm_c_re", "ssm_c_im",
             "ssm_d", "glu_w", "glu_b", "w_ssm_out", "conv_w", "w_conv_out", "w_o", "ln1_g", "ln1_b", "w_gate", "w_up",
             "w_down", "ln2_g", "ln2_b"]
    return (loss.reshape(()), grad_x[None], *[out[p + n] for p in ("grad_", "delta_", "new_m_", "new_v_") for n in names])
```

```python
import functools
import math

import jax
import jax.numpy as jnp
from jax import lax
from jax.experimental import pallas as pl
from jax.experimental.pallas import tpu as pltpu

f32, bf16 = jnp.float32, jnp.bfloat16
S = jax.ShapeDtypeStruct
MESH = pl.DeviceIdType.MESH
HIGHEST = lax.Precision.HIGHEST

D = 1024
W = 512
NG, NP, GC = 32, 64, 16
F = 2816
NDEV = 8
FS = F // NDEV
IN_COLS = 8 * W
ALPHA = 2.0 ** 0.25
LN_EPS = 1e-5
ADAM_LR, ADAM_B1, ADAM_B2, ADAM_EPS, ADAM_WD, ADAM_STEP = 0.001, 0.9, 0.999, 1e-08, 0.01, 10
NC = 32
LANE = 128
SW = 4 * LANE
VMEM_LIMIT = 56 * 1024 * 1024
GRAD_DT = bf16
ANY = pl.BlockSpec(memory_space=pl.ANY)


def _cp(sem=None, vmem=None):
    return pltpu.CompilerParams(dimension_semantics=sem, vmem_limit_bytes=vmem)


def _resident(shape):
    return pl.BlockSpec(shape, lambda i: (0,) * len(shape), pipeline_mode=pl.Buffered(1))


def _dot(a, b):
    return jnp.dot(a, b, preferred_element_type=f32)


def _dot_nt(a, b):
    return lax.dot_general(a, b, (((1,), (1,)), ((), ())), preferred_element_type=f32)


def _dot_tn(a, b):
    return lax.dot_general(a, b, (((0,), (0,)), ((), ())), preferred_element_type=f32)


def _eye(n):
    return (lax.broadcasted_iota(jnp.int32, (n, n), 0) == lax.broadcasted_iota(jnp.int32, (n, n), 1)).astype(f32)


def _transpose_exact(a):
    return lax.dot_general(a, _eye(a.shape[0]), (((0,), (0,)), ((), ())), precision=HIGHEST, preferred_element_type=f32)


def _sigmoid(x):
    return 1.0 / (1.0 + jnp.exp(-x))


_GK = math.sqrt(2.0 / math.pi)


def _gelu(x):
    return 0.5 * x * (1.0 + jnp.tanh(_GK * (x + 0.044715 * x * x * x)))


def _gelu_grad(x):
    th = jnp.tanh(_GK * (x + 0.044715 * x * x * x))
    return 0.5 * (1.0 + th) + 0.5 * x * (1.0 - th * th) * _GK * (1.0 + 3.0 * 0.044715 * x * x)


ROW_PART = 256


def _row_parts(tm):
    return [slice(r, r + min(ROW_PART, tm)) for r in range(0, tm, min(ROW_PART, tm))]


def _ln_stats(r):
    mu = jnp.mean(r, axis=-1, keepdims=True)
    xc = r - mu
    var = jnp.mean(xc * xc, axis=-1, keepdims=True)
    rstd = lax.rsqrt(var + LN_EPS)
    return xc * rstd, rstd


def _ln_bwd(dy, xhat, rstd, g):
    dxh = dy * g
    m1 = jnp.mean(dxh, axis=-1, keepdims=True)
    m2 = jnp.mean(dxh * xhat, axis=-1, keepdims=True)
    return rstd * (dxh - m1 - xhat * m2)


def _coords():
    return lax.axis_index("x"), lax.axis_index("y"), lax.axis_index("c")


def _when(cond, fn):
    if cond is True:
        fn()
    else:
        pl.when(cond)(fn)


class GatherPlan:
    aliases = ()

    def __init__(self, arrs, srcs=None, into=None):
        n = self.n = len(arrs)
        self.srcs = srcs
        self.inputs = list(arrs) + list(into or [])
        if into:
            self.aliases = tuple((n + a, a) for a in range(n))
        self.out_shape = [S((NDEV,) + a.shape, a.dtype) for a in arrs]
        self.sems = [pltpu.SemaphoreType.DMA((n, 7)), pltpu.SemaphoreType.DMA((n, 7)), pltpu.SemaphoreType.DMA((n,))]

    def _has(self, dev):
        if self.srcs is None:
            return True
        idx = 4 * dev[0] + 2 * dev[1] + dev[2]
        return functools.reduce(jnp.logical_or, [idx == s for s in self.srcs])

    def _parts(self, ins, outs, sems):
        n = self.n
        send_sems, recv_sems, loc_sems = sems
        x, y, c = _coords()
        me, sib = (x, y, c), (x, y, 1 - c)
        chips = [(1 - x, y), (x, 1 - y), (1 - x, 1 - y)]

        def slot(a, dev):
            return outs[a].at[4 * dev[0] + 2 * dev[1] + dev[2]]

        def copy(a, k, block, to, src=None):
            return pltpu.make_async_remote_copy(
                src_ref=slot(a, block) if src is None else src, dst_ref=slot(a, block),
                send_sem=send_sems.at[a, k], recv_sem=recv_sems.at[a, k], device_id=to, device_id_type=MESH)

        each = [(j, chip, a) for j, chip in enumerate(chips) for a in range(n)]
        own = self._has(me)
        return dict(
            mine=lambda: [(pltpu.make_async_copy(ins[a], slot(a, me), loc_sems.at[a]), own) for a in range(n)],
            first=lambda: ([(copy(a, 0, me, sib, src=ins[a]), own) for a in range(n)]
                           + [(copy(a, 1 + j, me, (*chip, c), src=ins[a]), own) for j, chip, a in each]),
            landed=lambda: [(copy(a, 1 + j, (*chip, c), me), self._has((*chip, c))) for j, chip, a in each],
            passed=lambda: [(copy(a, 4 + j, (*chip, c), sib), self._has((*chip, c))) for j, chip, a in each],
            from_sib=lambda: ([(copy(a, 0, sib, me), self._has(sib)) for a in range(n)]
                              + [(copy(a, 4 + j, (*chip, 1 - c), me), self._has((*chip, 1 - c))) for j, chip, a in each]))

    def start(self, ins, outs, sems):
        p = self._parts(ins, outs, sems)
        for cp, cond in p["mine"]() + p["first"]():
            _when(cond, cp.start)

    def forward(self, ins, outs, sems):
        p = self._parts(ins, outs, sems)
        for (got, cond), (fwd, _) in zip(p["landed"](), p["passed"]()):
            def relay(got=got, fwd=fwd):
                got.wait_recv()
                fwd.start()

            _when(cond, relay)

    def finish(self, ins, outs, sems):
        p = self._parts(ins, outs, sems)
        for cp, cond in p["from_sib"]():
            _when(cond, cp.wait_recv)
        for cp, cond in p["first"]() + p["passed"]():
            _when(cond, cp.wait_send)
        for cp, cond in p["mine"]():
            _when(cond, cp.wait)


class ScatterPlan:
    aliases = ()

    def __init__(self, gs, only=None, into=None, whole=False):
        n = self.n = len(gs)
        self.only = only
        self.whole = whole
        self.inputs = list(gs) + list(into or [])
        if into:
            self.aliases = tuple((n + a, a) for a in range(n))
        self.out_shape = [S((NDEV,) + g.shape if whole else g.shape, g.dtype) for g in gs]
        self.sems = [pltpu.SemaphoreType.DMA((n, 7)), pltpu.SemaphoreType.DMA((n, 7)), pltpu.SemaphoreType.DMA((n,))]

    def _owner(self, idx):
        if self.only is None:
            return True
        return functools.reduce(jnp.logical_or, [idx == b for b in self.only])

    def _copies(self, ins, outs, sems):
        n = self.n
        send_sems, recv_sems, loc_sems = sems
        x, y, c = _coords()
        me = 4 * x + 2 * y + c
        mine = self._owner(me)
        block = (lambda a, k: ins[a]) if self.whole else (lambda a, k: ins[a].at[k])
        copies = [(pltpu.make_async_copy(block(a, me), outs[a].at[me], loc_sems.at[a]), mine, None) for a in range(n)]
        for m in range(1, NDEV):
            px = 1 - x if m & 4 else x
            py = 1 - y if m & 2 else y
            pc = 1 - c if m & 1 else c
            peer = 4 * px + 2 * py + pc
            for a in range(n):
                copies.append((pltpu.make_async_remote_copy(
                    src_ref=block(a, peer), dst_ref=outs[a].at[me],
                    send_sem=send_sems.at[a, m - 1], recv_sem=recv_sems.at[a, m - 1],
                    device_id=(px, py, pc), device_id_type=MESH), self._owner(peer), mine))
        return copies

    def start(self, ins, outs, sems):
        for cp, sends, _ in self._copies(ins, outs, sems):
            _when(sends, cp.start)

    def forward(self, ins, outs, sems):
        pass

    def finish(self, ins, outs, sems):
        for cp, sends, receives in self._copies(ins, outs, sems):
            if receives is None:
                _when(sends, cp.wait)
            else:
                _when(sends, cp.wait_send)
                _when(receives, cp.wait_recv)


class Plans:
    def __init__(self, plans):
        self.plans = plans
        self.inputs = [a for p in plans for a in p.inputs]
        self.out_shape = [s for p in plans for s in p.out_shape]
        self.sems = [s for p in plans for s in p.sems]
        self.aliases, i, o = [], 0, 0
        for p in plans:
            self.aliases += [(i + a, o + b) for a, b in p.aliases]
            i, o = i + len(p.inputs), o + len(p.out_shape)

    def _each(self, what, ins, outs, sems):
        i = o = s = 0
        for p in self.plans:
            ni, no, ns = len(p.inputs), len(p.out_shape), len(p.sems)
            getattr(p, what)(ins[i:i + ni], outs[o:o + no], sems[s:s + ns])
            i, o, s = i + ni, o + no, s + ns

    def start(self, ins, outs, sems):
        self._each("start", ins, outs, sems)

    def forward(self, ins, outs, sems):
        self._each("forward", ins, outs, sems)

    def finish(self, ins, outs, sems):
        self._each("finish", ins, outs, sems)


def _call(body, args, *, name, grid, in_specs, out_specs, out_shape, scratch=(), sem=None, vmem=None, plan=None,
          relay_step=None, relay_after=False):
    if plan is None:
        outs = pl.pallas_call(body, name=name, grid=grid, in_specs=list(in_specs), out_specs=list(out_specs),
                              out_shape=list(out_shape), scratch_shapes=list(scratch),
                              compiler_params=_cp(sem, vmem))(*args)
        return list(outs), []
    ni, no, ns = len(in_specs), len(out_specs), len(scratch)
    pi, po = len(plan.inputs), len(plan.out_shape)
    aliases = {ni + a: no + b for a, b in plan.aliases}

    def wrapped(*refs):
        main_in, p_in = refs[:ni], refs[ni:ni + pi]
        main_out, p_out = refs[ni + pi:ni + pi + no], refs[ni + pi + no:ni + pi + no + po]
        main_scr, p_sems = refs[ni + pi + no + po:ni + pi + no + po + ns], refs[ni + pi + no + po + ns:]
        ids = [pl.program_id(d) for d in range(len(grid))]
        first = functools.reduce(jnp.logical_and, [i == 0 for i in ids])
        last = functools.reduce(jnp.logical_and, [i == g - 1 for i, g in zip(ids, grid)])

        @pl.when(first)
        def _():
            plan.start(p_in, p_out, p_sems)

        if not relay_after:
            @pl.when(last if relay_step is None else ids[0] == max(relay_step, 0))
            def _():
                plan.forward(p_in, p_out, p_sems)

        body(*main_in, *main_out, *main_scr)

        @pl.when(last)
        def _():
            if relay_after:
                plan.forward(p_in, p_out, p_sems)
            plan.finish(p_in, p_out, p_sems)

    outs = pl.pallas_call(
        wrapped, name=name, grid=grid, in_specs=list(in_specs) + [ANY] * pi, out_specs=list(out_specs) + [ANY] * po,
        out_shape=list(out_shape) + list(plan.out_shape), scratch_shapes=list(scratch) + list(plan.sems),
        input_output_aliases=aliases, compiler_params=_cp(("arbitrary",) * len(grid), vmem),
    )(*args, *plan.inputs)
    return list(outs[:no]), list(outs[no:])


def run_plan(plan, name):
    def body(*refs):
        ins, outs, sems = refs[:len(plan.inputs)], refs[len(plan.inputs):len(plan.inputs) + len(plan.out_shape)], \
            refs[len(plan.inputs) + len(plan.out_shape):]
        plan.start(ins, outs, sems)
        plan.forward(ins, outs, sems)
        plan.finish(ins, outs, sems)

    return pl.pallas_call(body, name=name, in_specs=[ANY] * len(plan.inputs), out_specs=[ANY] * len(plan.out_shape),
                          out_shape=list(plan.out_shape), scratch_shapes=list(plan.sems))(*plan.inputs)


def mm_tn(a, b, name, tn=512):
    T, K = a.shape
    N = b.shape[1]
    tn = min(tn, N)

    def body(a_ref, b_ref, o_ref):
        o_ref[...] = _dot_tn(a_ref[...], b_ref[...]).astype(GRAD_DT)

    (out,), _ = _call(body, [a, b], name=name, grid=(N // tn,),
                      in_specs=[_resident((T, K)), pl.BlockSpec((T, tn), lambda j: (0, j))],
                      out_specs=[pl.BlockSpec((None, K, tn), lambda j: (j, 0, 0))],
                      out_shape=[S((N // tn, K, tn), GRAD_DT)], sem=("parallel",), vmem=VMEM_LIMIT)
    return out


def grad_w_in_rest(xb, dh, dcg, dbg, dga, dgb, plan):
    T = xb.shape[0]
    order = ((0, 0), (1, 1), (2, 2), (3, 3), (4, 3), (5, 4), (6, 4))

    def body(x_ref, *refs):
        o_ref = refs[-1]
        j = pl.program_id(0)
        for step, opnd in order:
            @pl.when(j == step)
            def _(opnd=opnd):
                o_ref[...] = _dot_tn(x_ref[...], refs[opnd][...]).astype(GRAD_DT)

    once = lambda: pl.BlockSpec((T, W), lambda j: (0, 0), pipeline_mode=pl.Buffered(1))
    (out,), sent = _call(
        body, [xb, dh, dcg, dbg, dga, dgb], name="grad_w_in_rest", grid=(len(order),),
        in_specs=[_resident((T, D)), once(), once(), once(),
                  pl.BlockSpec((T, W), lambda j: (0, jnp.clip(j - 3, 0, 1))),
                  pl.BlockSpec((T, W), lambda j: (0, jnp.clip(j - 5, 0, 1)))],
        out_specs=[pl.BlockSpec((None, D, W), lambda j: (1 + j, 0, 0))],
        out_shape=[S((NDEV, D, W), GRAD_DT)], sem=("arbitrary",), vmem=VMEM_LIMIT, plan=plan)
    return out, sent


def mm_tn_rows(a, b, name, tk=256, plan=None):
    T, K = a.shape
    N = b.shape[1]
    tk = min(tk, K)

    def body(a_ref, b_ref, o_ref):
        o_ref[...] = _dot_tn(a_ref[...], b_ref[...]).astype(GRAD_DT)

    (out,), sent = _call(body, [a, b], name=name, grid=(K // tk,),
                         in_specs=[pl.BlockSpec((T, tk), lambda i: (0, i)), _resident((T, N))],
                         out_specs=[pl.BlockSpec((tk, N), lambda i: (i, 0))], out_shape=[S((K, N), GRAD_DT)],
                         sem=("parallel",), vmem=VMEM_LIMIT, plan=plan)
    return out, sent


def prep_weights(ws):
    def body(*refs):
        for i in range(len(ws)):
            refs[len(ws) + i][...] = refs[i][...].astype(bf16)

    return pl.pallas_call(body, name="prep_weights", out_shape=[S(w.shape, bf16) for w in ws],
                          compiler_params=_cp(None, VMEM_LIMIT))(*ws)


REST_BLOCKS = (4, 5, 6, 7, 1, 2, 3)
REST_COLS = len(REST_BLOCKS) * W


def in_proj_u(x, win_g, b_in):
    T = x.shape[0]
    tm = min(1024, T)

    def body(x_ref, w_ref, b_ref, u_ref, xb_ref):
        xb = x_ref[...].astype(bf16)
        xb_ref[...] = xb
        u_ref[...] = _dot(xb, w_ref[...]) + b_ref[...]

    row = pl.BlockSpec((tm, D), lambda i: (i, 0))
    return pl.pallas_call(
        body, name="in_proj_u", grid=(T // tm,),
        in_specs=[row, pl.BlockSpec((None, D, W), lambda i: (0, 0, 0)), pl.BlockSpec((1, W), lambda i: (0, 0))],
        out_specs=[pl.BlockSpec((tm, W), lambda i: (i, 0)), row],
        out_shape=[S((T, W), f32), S((T, D), bf16)], compiler_params=_cp(("parallel",), VMEM_LIMIT),
    )(x, win_g, b_in)


def in_proj_rest(xb, win_g, b_in, plan):
    T = xb.shape[0]
    tm = min(512, T)

    def body(x_ref, w_ref, b_ref, o_ref):
        xb_ = x_ref[...]
        for i, k in enumerate(REST_BLOCKS):
            o_ref[:, i * W:(i + 1) * W] = _dot(xb_, w_ref[k]) + b_ref[:, k * W:(k + 1) * W]

    return _call(
        body, [xb, win_g, b_in], name="in_proj_rest", grid=(T // tm,),
        in_specs=[pl.BlockSpec((tm, D), lambda i: (i, 0)), _resident((NDEV, D, W)), _resident((1, IN_COLS))],
        out_specs=[pl.BlockSpec((tm, REST_COLS), lambda i: (i, 0))],
        out_shape=[S((T, REST_COLS), f32)], vmem=VMEM_LIMIT, plan=plan, relay_after=True)


def _to_scan_order(a_ref, o_ref):
    L = a_ref.shape[0] // NC

    def step(jb, carry):
        j0 = pl.multiple_of(jb * 8, 8)
        for q in range(NC // 8):
            x = jnp.stack([a_ref[pl.ds((8 * q + c) * L + j0, 8), :] for c in range(8)], axis=0)
            y = jnp.swapaxes(x, 0, 1)
            for j in range(8):
                o_ref[pl.ds((j0 + j) * NC + 8 * q, 8), :] = y[j]
        return carry

    lax.fori_loop(0, L // 8, step, 0)


def _to_time_order(a_ref, o_ref):
    L = a_ref.shape[0] // NC

    def step(jb, carry):
        j0 = pl.multiple_of(jb * 16, 16)
        for q in range(NC // 8):
            halves = []
            for h in range(2):
                x = jnp.stack([a_ref[pl.ds((j0 + 8 * h + j) * NC + 8 * q, 8), :] for j in range(8)], axis=0)
                halves.append(jnp.swapaxes(x, 0, 1))
            for c in range(8):
                o_ref[pl.ds((8 * q + c) * L + j0, 16), :] = jnp.concatenate(
                    [halves[0][c], halves[1][c]], axis=0).astype(o_ref.dtype)
        return carry

    lax.fori_loop(0, L // 16, step, 0)


def _disc(lr, li, ldt):
    dt = jnp.exp(ldt)
    mag = jnp.exp(lr * dt)
    lbr = mag * jnp.cos(li * dt)
    lbi = mag * jnp.sin(li * dt)
    den = lr * lr + li * li
    nr = lbr - 1.0
    return lbr, lbi, (nr * lr + lbi * li) / den, (lbi * lr - nr * li) / den


def _per_channel(f):
    return jnp.broadcast_to(f[:, None, :], (NG, GC, NP)).reshape(NG * GC, NP)


def ssm_params(lam_re, lam_im, log_dt, br, bi):
    def body(lr_ref, li_ref, ldt_ref, br_ref, bi_ref, lbr_ref, lbi_ref, fr_ref, fi_ref, bbr_ref, bbi_ref):
        lbr, lbi, fr, fi = _disc(lr_ref[...], li_ref[...], ldt_ref[...])
        lbr_ref[...], lbi_ref[...], fr_ref[...], fi_ref[...] = lbr, lbi, fr, fi
        fr_, fi_, br_, bi_ = _per_channel(fr), _per_channel(fi), br_ref[...], bi_ref[...]
        bbr_ref[...] = fr_ * br_ - fi_ * bi_
        bbi_ref[...] = fr_ * bi_ + fi_ * br_

    return pl.pallas_call(body, name="ssm_params", out_shape=[S((NG, NP), f32)] * 4 + [S((NG * GC, NP), f32)] * 2)(
        lam_re, lam_im, log_dt, br, bi)


SCAN_UNROLL = 4
SCAN_LANES = 2 * LANE


def _steps(n, body, carry):
    main = n // SCAN_UNROLL

    def trip(t, c):
        for q in range(SCAN_UNROLL):
            c = body(t * SCAN_UNROLL + q, c)
        return c

    carry = lax.fori_loop(0, main, trip, carry)
    for i in range(main * SCAN_UNROLL, n):
        carry = body(i, carry)
    return carry


def _scan_body(T):
    L = T // NC
    RB = min(512, T)
    nsq = int(round(math.log2(L)))
    assert 2 ** nsq == L and T % RB == 0 and L % 16 == 0

    def rows(i):
        return pl.ds(pl.multiple_of(i * RB, RB), RB)

    def tile(j):
        return pl.ds(j * NC if isinstance(j, int) else pl.multiple_of(j * NC, NC), NC)

    def forward_states(u_ref, wb_ref, lbr_ref, lbi_ref, sre, sim, ere, eim):
        def bproj(i, carry):
            bu = _dot(u_ref[rows(i), :].astype(bf16), wb_ref[...])
            sre[rows(i), :] = bu[:, :SW]
            sim[rows(i), :] = bu[:, SW:]
            return carry

        lax.fori_loop(0, T // RB, bproj, 0)
        for lb in range(SW // SCAN_LANES):
            ls = slice(lb * SCAN_LANES, (lb + 1) * SCAN_LANES)
            ar = jnp.broadcast_to(lbr_ref[:, ls], (NC, SCAN_LANES))
            ai = jnp.broadcast_to(lbi_ref[:, ls], (NC, SCAN_LANES))

            def step(j, carry):
                xr, xi = carry
                nr = ar * xr - ai * xi + sre[tile(j), ls]
                ni = ar * xi + ai * xr + sim[tile(j), ls]
                sre[tile(j), ls] = nr
                sim[tile(j), ls] = ni
                return nr, ni

            zero = jnp.zeros((NC, SCAN_LANES), f32)
            _steps(L, step, (zero, zero))
            pr, pi = lbr_ref[:, ls], lbi_ref[:, ls]
            for _ in range(nsq):
                pr, pi = pr * pr - pi * pi, 2.0 * pr * pi
            er = jnp.zeros((1, SCAN_LANES), f32)
            ei = er
            ere[0:1, ls] = er
            eim[0:1, ls] = ei
            base = (L - 1) * NC
            for c in range(1, NC):
                lr_ = sre[base + c - 1:base + c, ls]
                li_ = sim[base + c - 1:base + c, ls]
                er, ei = lr_ + pr * er - pi * ei, li_ + pr * ei + pi * er
                ere[c:c + 1, ls] = er
                eim[c:c + 1, ls] = ei
            e_r, e_i = ere[:, ls].reshape(NC // 8, 8, SCAN_LANES), eim[:, ls].reshape(NC // 8, 8, SCAN_LANES)
            ar8, ai8 = ar[0:8], ai[0:8]

            def fix(j, carry):
                pwr, pwi = carry
                xr = sre[tile(j), ls].reshape(NC // 8, 8, SCAN_LANES) + (pwr * e_r - pwi * e_i)
                xi = sim[tile(j), ls].reshape(NC // 8, 8, SCAN_LANES) + (pwr * e_i + pwi * e_r)
                sre[tile(j), ls] = xr.reshape(NC, SCAN_LANES)
                sim[tile(j), ls] = xi.reshape(NC, SCAN_LANES)
                return pwr * ar8 - pwi * ai8, pwr * ai8 + pwi * ar8

            _steps(L, fix, (ar8, ai8))

    return L, RB, nsq, rows, tile, forward_states


def ssm_fwd(u, wb, wc, lbr, lbi, dsk, plan):
    T = u.shape[0]
    L, RB, nsq, rows, tile, forward_states = _scan_body(T)
    nslab = W // LANE

    def body(u_ref, wb_ref, wc_ref, lbr_ref, lbi_ref, d_ref, y_ref, up_ref, xr_ref, xi_ref, sre, sim, ere, eim, yp):
        _to_scan_order(u_ref, up_ref)
        forward_states(up_ref, wb_ref, lbr_ref, lbi_ref, sre, sim, ere, eim)

        def cproj(i, carry):
            xr, xi = sre[rows(i), :].astype(bf16), sim[rows(i), :].astype(bf16)
            xr_ref[rows(i), :] = xr
            xi_ref[rows(i), :] = xi
            y = _dot(xr, wc_ref[0:SW, :]) + _dot(xi, wc_ref[SW:, :])
            yp[rows(i), :] = y + d_ref[...] * up_ref[rows(i), :]
            return carry

        lax.fori_loop(0, T // RB, cproj, 0)
        _to_time_order(yp, y_ref)

    slab = pl.BlockSpec((T, LANE), lambda k: (0, k))
    states = pl.BlockSpec((T, SW), lambda k: (0, k))
    return _call(
        body, [u, wb, wc, lbr, lbi, dsk], name="ssm_fwd", grid=(nslab,),
        in_specs=[slab, pl.BlockSpec((None, LANE, 2 * SW), lambda k: (k, 0, 0)),
                  pl.BlockSpec((None, 2 * SW, LANE), lambda k: (k, 0, 0)),
                  pl.BlockSpec((None, 1, SW), lambda k: (k, 0, 0)), pl.BlockSpec((None, 1, SW), lambda k: (k, 0, 0)),
                  pl.BlockSpec((None, 1, LANE), lambda k: (k, 0, 0))],
        out_specs=[slab, slab, states, states],
        out_shape=[S((T, W), f32), S((T, W), f32), S((T, nslab * SW), bf16), S((T, nslab * SW), bf16)],
        scratch=[pltpu.VMEM((T, SW), f32), pltpu.VMEM((T, SW), f32), pltpu.VMEM((NC, SW), f32), pltpu.VMEM((NC, SW), f32),
                 pltpu.VMEM((T, LANE), f32)],
        vmem=VMEM_LIMIT, plan=plan, relay_after=True)


def ssm_bwd(u_p, dy, xr, xi, wbT, wcT, lbr, lbi, dsk, plan):
    T = u_p.shape[0]
    L, RB, nsq, rows, tile, _ = _scan_body(T)

    def body(u_ref, dyt_ref, sre, sim, wbT_ref, wcT_ref, lbr_ref, lbi_ref, d_ref,
             dut_ref, dwb_ref, dwc_ref, dlr_ref, dli_ref, dd_ref, su_ref, gre, gim, ere, eim, dy_ref, du_ref):
        _to_scan_order(dyt_ref, dy_ref)

        def dstate(i, carry):
            g = _dot(dy_ref[rows(i), :].astype(bf16), wcT_ref[...])
            gre[rows(i), :] = g[:, :SW]
            gim[rows(i), :] = g[:, SW:]
            return carry

        lax.fori_loop(0, T // RB, dstate, 0)
        row = lax.broadcasted_iota(jnp.int32, (NC, SCAN_LANES), 0)
        for lb in range(SW // SCAN_LANES):
            ls = slice(lb * SCAN_LANES, (lb + 1) * SCAN_LANES)
            ar = jnp.broadcast_to(lbr_ref[:, ls], (NC, SCAN_LANES))
            ai = jnp.broadcast_to(lbi_ref[:, ls], (NC, SCAN_LANES))

            def step(i, carry):
                gr, gi = carry
                j = L - 1 - i
                nr = ar * gr + ai * gi + gre[tile(j), ls]
                ni = ar * gi - ai * gr + gim[tile(j), ls]
                gre[tile(j), ls] = nr
                gim[tile(j), ls] = ni
                return nr, ni

            zero = jnp.zeros((NC, SCAN_LANES), f32)
            _steps(L, step, (zero, zero))
            pr, pi = lbr_ref[:, ls], -lbi_ref[:, ls]
            for _ in range(nsq):
                pr, pi = pr * pr - pi * pi, 2.0 * pr * pi
            er = jnp.zeros((1, SCAN_LANES), f32)
            ei = er
            ere[NC - 1:NC, ls] = er
            eim[NC - 1:NC, ls] = ei
            for c in range(NC - 2, -1, -1):
                lr_ = gre[c + 1:c + 2, ls]
                li_ = gim[c + 1:c + 2, ls]
                er, ei = lr_ + pr * er - pi * ei, li_ + pr * ei + pi * er
                ere[c:c + 1, ls] = er
                eim[c:c + 1, ls] = ei
            e_r, e_i = ere[:, ls].reshape(NC // 8, 8, SCAN_LANES), eim[:, ls].reshape(NC // 8, 8, SCAN_LANES)
            ar8, ai8 = ar[0:8], ai[0:8]

            def fixed(j, pwr, pwi):
                gr = (gre[tile(j), ls].reshape(NC // 8, 8, SCAN_LANES) + (pwr * e_r - pwi * e_i)).reshape(NC, SCAN_LANES)
                gi = (gim[tile(j), ls].reshape(NC // 8, 8, SCAN_LANES) + (pwr * e_i + pwi * e_r)).reshape(NC, SCAN_LANES)
                gre[tile(j), ls] = gr
                gim[tile(j), ls] = gi
                return gr, gi

            def fix(i, carry):
                pwr, pwi, accr, acci = carry
                j = L - 1 - i
                gr, gi = fixed(j, pwr, pwi)
                xr, xi = sre[tile(j - 1), ls].astype(f32), sim[tile(j - 1), ls].astype(f32)
                return (pwr * ar8 + pwi * ai8, pwi * ar8 - pwr * ai8,
                        accr + gr * xr + gi * xi, acci + gi * xr - gr * xi)

            pwr, pwi, accr, acci = _steps(L - 1, fix, (ar8, -ai8, zero, zero))
            gr, gi = fixed(0, pwr, pwi)
            xr = jnp.where(row == 0, 0.0, pltpu.roll(sre[tile(L - 1), ls].astype(f32), 1, axis=0))
            xi = jnp.where(row == 0, 0.0, pltpu.roll(sim[tile(L - 1), ls].astype(f32), 1, axis=0))
            accr = accr + gr * xr + gi * xi
            acci = acci + gi * xr - gr * xi
            dlr_ref[:, ls] = jnp.sum(accr, axis=0, keepdims=True)
            dli_ref[:, ls] = jnp.sum(acci, axis=0, keepdims=True)

        dwb_ref[...] = jnp.zeros_like(dwb_ref)
        dwc_ref[...] = jnp.zeros_like(dwc_ref)
        dd_ref[...] = jnp.zeros_like(dd_ref)
        su_ref[...] = jnp.zeros_like(su_ref)

        def finish(i, carry):
            u32, dy32 = u_ref[rows(i), :], dy_ref[rows(i), :]
            ub, dyb = u32.astype(bf16), dy32.astype(bf16)
            gr, gi = gre[rows(i), :].astype(bf16), gim[rows(i), :].astype(bf16)
            du = _dot(gr, wbT_ref[0:SW, :]) + _dot(gi, wbT_ref[SW:, :]) + dy32 * d_ref[...]
            du_ref[rows(i), :] = du
            su_ref[...] += jnp.sum(du, axis=0, keepdims=True)
            dwb_ref[:, 0:SW] += _dot_tn(ub, gr)
            dwb_ref[:, SW:] += _dot_tn(ub, gi)
            dwc_ref[:, 0:SW] += _dot_tn(dyb, sre[rows(i), :])
            dwc_ref[:, SW:] += _dot_tn(dyb, sim[rows(i), :])
            dd_ref[...] += jnp.sum(dy32 * u32, axis=0, keepdims=True)
            return carry

        lax.fori_loop(0, T // RB, finish, 0)
        _to_time_order(du_ref, dut_ref)

    slab = pl.BlockSpec((T, LANE), lambda k: (0, k))
    wide = pl.BlockSpec((None, LANE, 2 * SW), lambda k: (k, 0, 0))
    tall = pl.BlockSpec((None, 2 * SW, LANE), lambda k: (k, 0, 0))
    vec = pl.BlockSpec((None, 1, SW), lambda k: (k, 0, 0))
    vecd = pl.BlockSpec((None, 1, LANE), lambda k: (k, 0, 0))
    states = pl.BlockSpec((T, SW), lambda k: (0, k))
    nslab = W // LANE
    return _call(
        body, [u_p, dy, xr, xi, wbT, wcT, lbr, lbi, dsk], name="ssm_bwd", grid=(nslab,),
        in_specs=[slab, slab, states, states, tall, wide, vec, vec, vecd],
        out_specs=[slab, wide, wide, vec, vec, vecd, vecd],
        out_shape=[S((T, W), bf16), S((nslab, LANE, 2 * SW), f32), S((nslab, LANE, 2 * SW), f32),
                   S((nslab, 1, SW), f32), S((nslab, 1, SW), f32), S((nslab, 1, LANE), f32), S((nslab, 1, LANE), f32)],
        scratch=[pltpu.VMEM((T, SW), f32)] * 2 + [pltpu.VMEM((NC, SW), f32)] * 2 + [pltpu.VMEM((T, LANE), f32)] * 2,
        vmem=VMEM_LIMIT, plan=plan)


def _shift_rows(cur, prev8, k):
    return pltpu.roll(jnp.concatenate([prev8, cur], axis=0), k, axis=0)[8:]


def _lift_rows(cur, next8, k):
    n = cur.shape[0]
    return pltpu.roll(jnp.concatenate([cur, next8], axis=0), n + 8 - k, axis=0)[:n]


def conv_fwd(proj, conv_w):
    T = proj.shape[0]
    RB = min(512, T)

    def body(h_ref, c_ref, b_ref, w_ref, o_ref):
        w0, w1, w2 = w_ref[0:1, :], w_ref[1:2, :], w_ref[2:3, :]

        def blk(i, carry):
            r0 = pl.multiple_of(i * RB, RB)
            rs = pl.ds(r0, RB)
            ch = c_ref[rs, :] * h_ref[rs, :]
            pr = pl.ds(jnp.maximum(r0 - 8, 0), 8)
            prev = jnp.where(i > 0, c_ref[pr, :] * h_ref[pr, :], 0.0)
            z = w2 * ch + w1 * _shift_rows(ch, prev, 1) + w0 * _shift_rows(ch, prev, 2)
            o_ref[rs, :] = (b_ref[rs, :] * z).astype(bf16)
            return carry

        lax.fori_loop(0, T // RB, blk, 0)

    nb = W // LANE
    return pl.pallas_call(
        body, name="conv_fwd", grid=(nb,),
        in_specs=[pl.BlockSpec((T, LANE), lambda k: (0, 4 * nb + k)), pl.BlockSpec((T, LANE), lambda k: (0, 5 * nb + k)),
                  pl.BlockSpec((T, LANE), lambda k: (0, 6 * nb + k)),pl.BlockSpec((3, LANE), lambda k: (0, k))],
        out_specs=pl.BlockSpec((T, LANE), lambda k: (0, k)), out_shape=S((T, W), bf16),
        compiler_params=_cp(("parallel",), VMEM_LIMIT),
    )(proj, proj, proj, conv_w)


def _dense_columns(blocks_ref, dense_ref):
    for k in range(NDEV):
        dense_ref[:, k * LANE:(k + 1) * LANE] = blocks_ref[k]


def merge_fwd(yn, glu_w, glu_b, yb, wso, wco, proj, plan):
    T = yn.shape[0]
    tm = min(1024, T)

    def body(y_ref, gw_ref, gbias_ref, yb_ref, wa_ref, wb_ref, ga_ref, gb_ref, o_ref, ya_ref, wa_s, wb_s):
        @pl.when(pl.program_id(0) == 0)
        def _():
            _dense_columns(wa_ref, wa_s)
            _dense_columns(wb_ref, wb_s)

        for rs in _row_parts(tm):
            g = _gelu(y_ref[rs, :])
            ya = (g * _sigmoid(_dot(g.astype(bf16), gw_ref[...]) + gbias_ref[...])).astype(bf16)
            ya_ref[rs, :] = ya
            o_ref[rs, :] = (_sigmoid(ga_ref[rs, :]) * _dot(ya, wa_s[...])
                            + _sigmoid(gb_ref[rs, :]) * _dot(yb_ref[rs, :], wb_s[...])).astype(bf16)

    act = pl.BlockSpec((tm, W), lambda i: (i, 0))
    return _call(
        body, [yn, glu_w, glu_b, yb, wso, wco, proj, proj], name="merge_fwd", grid=(T // tm,),
        in_specs=[act, pl.BlockSpec((W, W), lambda i: (0, 0)), pl.BlockSpec((1, W), lambda i: (0, 0)), act,
                  _resident((NDEV, W, LANE)), _resident((NDEV, W, LANE)),
                  pl.BlockSpec((tm, D), lambda i: (i, 0)), pl.BlockSpec((tm, D), lambda i: (i, 1))],
        out_specs=[pl.BlockSpec((tm, D), lambda i: (i, 0)), act], out_shape=[S((T, D), bf16), S((T, W), bf16)],
        scratch=[pltpu.VMEM((W, D), bf16), pltpu.VMEM((W, D), bf16)], vmem=VMEM_LIMIT, plan=plan)


def mix_ln1(merged, w_o, x, g1, b1, plan):
    T = x.shape[0]
    tm = min(512, T)

    def body(m_ref, w_ref, x_ref, g_ref, b_ref, r_ref, x1_ref):
        for rs in _row_parts(tm):
            r = ALPHA * x_ref[rs, :] + _dot(m_ref[rs, :], w_ref[...])
            r_ref[rs, :] = r
            xhat, _ = _ln_stats(r)
            x1_ref[rs, :] = (xhat * g_ref[...] + b_ref[...]).astype(bf16)

    row = pl.BlockSpec((tm, D), lambda i: (i, 0))
    vec = pl.BlockSpec((1, D), lambda i: (0, 0))
    return _call(
        body, [merged, w_o, x, g1, b1], name="mix_ln1", grid=(T // tm,),
        in_specs=[row, _resident((D, D)), row, vec, vec],
        out_specs=[row, row], out_shape=[S((T, D), f32), S((T, D), bf16)], sem=("parallel",), vmem=VMEM_LIMIT, plan=plan,
        relay_after=True)


FT = 256


def gate_up(x1b, wgT, wuT, plan):
    T = x1b.shape[0]
    tm = min(512, T)

    def body(x_ref, wg_ref, wu_ref, g_ref, u_ref, h_ref):
        x = x_ref[...]
        for n in range(F // FT):
            cs = slice(n * FT, (n + 1) * FT)
            g = _dot_nt(x, wg_ref[cs, :])
            u = _dot_nt(x, wu_ref[cs, :])
            g_ref[:, cs] = g.astype(bf16)
            u_ref[:, cs] = u.astype(bf16)
            h_ref[:, cs] = (g * _sigmoid(g) * u).astype(bf16)

    osp = pl.BlockSpec((tm, F), lambda i: (i, 0))
    return _call(
        body, [x1b, wgT, wuT], name="gate_up", grid=(T // tm,),
        in_specs=[pl.BlockSpec((tm, D), lambda i: (i, 0)), _resident((F, D)), _resident((F, D))],
        out_specs=[osp, osp, osp], out_shape=[S((T, F), bf16)] * 3, vmem=VMEM_LIMIT, plan=plan, relay_step=T // tm - 3)


def down_loss(hid, w_down, r1, g1, b1, g2, b2, target):
    T = hid.shape[0]
    tm = min(512, T)

    def body(h_ref, w_ref, r1_ref, g1_ref, b1_ref, g2_ref, b2_ref, t_ref, dr_ref, drb_ref, loss_ref, dg_ref, db_ref):
        @pl.when(pl.program_id(0) == 0)
        def _():
            loss_ref[...] = jnp.zeros_like(loss_ref)
            dg_ref[...] = jnp.zeros_like(dg_ref)
            db_ref[...] = jnp.zeros_like(db_ref)

        for rs in _row_parts(tm):
            xh1, _ = _ln_stats(r1_ref[rs, :])
            x1 = xh1 * g1_ref[...] + b1_ref[...]
            r2 = ALPHA * x1 + _dot(h_ref[rs, :], w_ref[...])
            xh2, rstd2 = _ln_stats(r2)
            err = xh2 * g2_ref[...] + b2_ref[...] - t_ref[rs, :]
            loss_ref[...] += jnp.sum(jnp.mean(err * err, axis=-1, keepdims=True), axis=0, keepdims=True)
            dy = err * (1.0 / D)
            dg_ref[...] += jnp.sum(dy * xh2, axis=0, keepdims=True)
            db_ref[...] += jnp.sum(dy, axis=0, keepdims=True)
            dr = _ln_bwd(dy, xh2, rstd2, g2_ref[...])
            dr_ref[rs, :] = dr
            drb_ref[rs, :] = dr.astype(bf16)

    row = pl.BlockSpec((tm, D), lambda i: (i, 0))
    vec = pl.BlockSpec((1, D), lambda i: (0, 0))
    return pl.pallas_call(
        body, name="down_loss", grid=(T // tm,),
        in_specs=[pl.BlockSpec((tm, F), lambda i: (i, 0)), _resident((F, D)), row, vec, vec, vec, vec, row],
        out_specs=[row, row, pl.BlockSpec((1, 1), lambda i: (0, 0)), vec, vec],
        out_shape=[S((T, D), f32), S((T, D), bf16), S((1, 1), f32), S((1, D), f32), S((1, D), f32)],
        compiler_params=_cp(("arbitrary",), VMEM_LIMIT),
    )(hid, w_down, r1, g1, b1, g2, b2, target)


def ffn_bwd_act(dffn, w_down, gate, up, plan):
    T = dffn.shape[0]
    tm = min(512, T)

    def body(d_ref, w_ref, g_ref, u_ref, dg_ref, du_ref):
        for n in range(F // FT):
            cs = slice(n * FT, (n + 1) * FT)
            for rs in _row_parts(tm):
                dh = _dot_nt(d_ref[rs, :], w_ref[cs, :])
                g, u = g_ref[rs, cs].astype(f32), u_ref[rs, cs].astype(f32)
                sg = _sigmoid(g)
                t = g * sg
                du_ref[rs, cs] = (dh * t).astype(bf16)
                dg_ref[rs, cs] = (dh * u * (sg + t - t * sg)).astype(bf16)

    osp = pl.BlockSpec((tm, F), lambda i: (i, 0))
    return _call(
        body, [dffn, w_down, gate, up], name="ffn_bwd_act", grid=(T // tm,),
        in_specs=[pl.BlockSpec((tm, D), lambda i: (i, 0)), _resident((F, D)), osp, osp],
        out_specs=[osp, osp], out_shape=[S((T, F), bf16)] * 2, sem=("parallel",), vmem=VMEM_LIMIT, plan=plan)


def ffn_bwd_x(dgate, dup, wgT, wuT, dr2, r1, g1, plan):
    T = dr2.shape[0]
    tm = min(512, T)

    def body(dg_ref, du_ref, wg_ref, wu_ref, dr2_ref, r1_ref, g1_ref, dr_ref, drb_ref, dgam_ref, dbet_ref):
        @pl.when(pl.program_id(0) == 0)
        def _():
            dgam_ref[...] = jnp.zeros_like(dgam_ref)
            dbet_ref[...] = jnp.zeros_like(dbet_ref)

        for rs in _row_parts(tm):
            dx1 = ALPHA * dr2_ref[rs, :] + _dot(dg_ref[rs, :], wg_ref[...]) + _dot(du_ref[rs, :], wu_ref[...])
            xh, rstd = _ln_stats(r1_ref[rs, :])
            dgam_ref[...] += jnp.sum(dx1 * xh, axis=0, keepdims=True)
            dbet_ref[...] += jnp.sum(dx1, axis=0, keepdims=True)
            dr = _ln_bwd(dx1, xh, rstd, g1_ref[...])
            dr_ref[rs, :] = dr
            drb_ref[rs, :] = dr.astype(bf16)

    row = pl.BlockSpec((tm, D), lambda i: (i, 0))
    wide = pl.BlockSpec((tm, F), lambda i: (i, 0))
    wsp = _resident((F, D))
    vec = pl.BlockSpec((1, D), lambda i: (0, 0))
    return _call(
        body, [dgate, dup, wgT, wuT, dr2, r1, g1], name="ffn_bwd_x", grid=(T // tm,),
        in_specs=[wide, wide, wsp, wsp, row, row, vec],
        out_specs=[row, row, vec, vec], out_shape=[S((T, D), f32), S((T, D), bf16), S((1, D), f32), S((1, D), f32)],
        vmem=VMEM_LIMIT, plan=plan)


def merge_bwd(dmix, w_o, merged, ya, yb, wso, wco, proj, plan):
    T = dmix.shape[0]
    tm = min(512, T)

    def body(dm_ref, wo_ref, m_ref, ya_ref, yb_ref, wa_ref, wb_ref, ga_ref, gb_ref,
             dya_ref, dyb_ref, dga_ref, dgb_ref, sa_ref, sb_ref, dwo_ref, wa_s, wb_s, acc):
        @pl.when(pl.program_id(0) == 0)
        def _():
            _dense_columns(wa_ref, wa_s)
            _dense_columns(wb_ref, wb_s)
            acc[...] = jnp.zeros_like(acc)

        acc[...] += _dot_tn(m_ref[...], dm_ref[...])

        @pl.when(pl.program_id(0) == pl.num_programs(0) - 1)
        def _():
            dwo_ref[...] = acc[...].astype(GRAD_DT)

        dmer = _dot_nt(dm_ref[...], wo_ref[...])
        sa, sb = _sigmoid(ga_ref[...]), _sigmoid(gb_ref[...])
        dya_ref[...] = (dmer * sa).astype(bf16)
        dyb_ref[...] = (dmer * sb).astype(bf16)
        dga = dmer * _dot(ya_ref[...], wa_s[...]) * sa * (1.0 - sa)
        dgb = dmer * _dot(yb_ref[...], wb_s[...]) * sb * (1.0 - sb)
        dga_ref[...] = dga.astype(bf16)
        dgb_ref[...] = dgb.astype(bf16)
        sa_ref[...] = jnp.sum(dga, axis=0, keepdims=True)
        sb_ref[...] = jnp.sum(dgb, axis=0, keepdims=True)

    act = pl.BlockSpec((tm, W), lambda i: (i, 0))
    osp = pl.BlockSpec((tm, D), lambda i: (i, 0))
    ssp = pl.BlockSpec((None, 1, D), lambda i: (i, 0, 0))
    return _call(
        body, [dmix, w_o, merged, ya, yb, wso, wco, proj, proj], name="merge_bwd", grid=(T // tm,),
        in_specs=[osp, _resident((D, D)), osp, act, act, _resident((NDEV, W, LANE)), _resident((NDEV, W, LANE)),
                  pl.BlockSpec((tm, D), lambda i: (i, 0)), pl.BlockSpec((tm, D), lambda i: (i, 1))],
        out_specs=[osp, osp, osp, osp, ssp, ssp, pl.BlockSpec((D, D), lambda i: (0, 0))],
        out_shape=[S((T, D), bf16)] * 4 + [S((T // tm, 1, D), f32)] * 2 + [S((D, D), GRAD_DT)],
        scratch=[pltpu.VMEM((W, D), bf16), pltpu.VMEM((W, D), bf16), pltpu.VMEM((D, D), f32)], vmem=VMEM_LIMIT, plan=plan)


def branches_bwd(dYA, dYB, ya, yb, wso, wco):
    T = dYA.shape[0]
    tm = min(1024, T)

    def body(da_ref, db_ref, ya_ref, yb_ref, wa_ref, wb_ref, oa_ref, ob_ref, ga_ref, gb_ref, wa_s, wb_s, acc_a, acc_b):
        @pl.when(pl.program_id(0) == 0)
        def _():
            _dense_columns(wa_ref, wa_s)
            _dense_columns(wb_ref, wb_s)
            acc_a[...] = jnp.zeros_like(acc_a)
            acc_b[...] = jnp.zeros_like(acc_b)

        oa_ref[...] = _dot_nt(da_ref[...], wa_s[...])
        ob_ref[...] = _dot_nt(db_ref[...], wb_s[...])
        acc_a[...] += _dot_tn(ya_ref[...], da_ref[...])
        acc_b[...] += _dot_tn(yb_ref[...], db_ref[...])

        @pl.when(pl.program_id(0) == pl.num_programs(0) - 1)
        def _():
            for k in range(NDEV):
                ga_ref[k] = acc_a[:, k * LANE:(k + 1) * LANE].astype(GRAD_DT)
                gb_ref[k] = acc_b[:, k * LANE:(k + 1) * LANE].astype(GRAD_DT)

    row = pl.BlockSpec((tm, D), lambda i: (i, 0))
    osp = pl.BlockSpec((tm, W), lambda i: (i, 0))
    blocks = pl.BlockSpec((NDEV, W, LANE), lambda i: (0, 0, 0))
    outs, _ = _call(
        body, [dYA, dYB, ya, yb, wso, wco], name="branches_bwd", grid=(T // tm,),
        in_specs=[row, row, osp, osp, _resident((NDEV, W, LANE)), _resident((NDEV, W, LANE))],
        out_specs=[osp, osp, blocks, blocks], out_shape=[S((T, W), f32)] * 2 + [S((NDEV, W, LANE), GRAD_DT)] * 2,
        scratch=[pltpu.VMEM((W, D), bf16)] * 2 + [pltpu.VMEM((W, D), f32)] * 2, sem=("arbitrary",), vmem=VMEM_LIMIT)
    return outs


def glu_bwd(yn, dya, glu_w, glu_b, plan):
    T = yn.shape[0]
    tm = min(512, T)

    def body(y_ref, d_ref, w_ref, b_ref, dy_ref, db_ref, dw_ref, acc):
        @pl.when(pl.program_id(0) == 0)
        def _():
            db_ref[...] = jnp.zeros_like(db_ref)
            acc[...] = jnp.zeros_like(acc)

        y, dya_ = y_ref[...], d_ref[...]
        g = _gelu(y)
        gb = g.astype(bf16)
        s = _sigmoid(_dot(gb, w_ref[...]) + b_ref[...])
        dsp = dya_ * g * s * (1.0 - s)
        dspb = dsp.astype(bf16)
        dg = dya_ * s + _dot_nt(dspb, w_ref[...])
        dy_ref[...] = dg * _gelu_grad(y)
        db_ref[...] += jnp.sum(dsp, axis=0, keepdims=True)
        acc[...] += _dot_tn(gb, dspb)

        @pl.when(pl.program_id(0) == pl.num_programs(0) - 1)
        def _():
            dw_ref[...] = acc[...].astype(GRAD_DT)

    row = pl.BlockSpec((tm, W), lambda i: (i, 0))
    vec = pl.BlockSpec((1, W), lambda i: (0, 0))
    mat = pl.BlockSpec((W, W), lambda i: (0, 0))
    return _call(
        body, [yn, dya, glu_w, glu_b], name="glu_bwd", grid=(T // tm,),
        in_specs=[row, row, mat, vec],
        out_specs=[row, vec, mat], out_shape=[S((T, W), f32), S((1, W), f32), S((W, W), GRAD_DT)],
        scratch=[pltpu.VMEM((W, W), f32)], sem=("arbitrary",), plan=plan)


def conv_bwd(proj, dyb, conv_w, plan):
    T = proj.shape[0]
    RB = min(512, T)
    nrb = T // RB

    def body(h_ref, c_ref, b_ref, d_ref, w_ref, dh_ref, dc_ref, db_ref, dw_ref, s_ref):
        w0, w1, w2 = w_ref[0:1, :], w_ref[1:2, :], w_ref[2:3, :]

        def blk(i, carry):
            a0, a1, a2, sh, sc, sb = carry
            r0 = pl.multiple_of(i * RB, RB)
            rs = pl.ds(r0, RB)
            h, cg, bg, dyb_ = h_ref[rs, :], c_ref[rs, :], b_ref[rs, :], d_ref[rs, :]
            ch = cg * h
            pr = pl.ds(jnp.maximum(r0 - 8, 0), 8)
            prev = jnp.where(i > 0, c_ref[pr, :] * h_ref[pr, :], 0.0)
            ch1, ch2 = _shift_rows(ch, prev, 1), _shift_rows(ch, prev, 2)
            dbg = dyb_ * (w2 * ch + w1 * ch1 + w0 * ch2)
            db_ref[rs, :] = dbg.astype(bf16)
            dz = dyb_ * bg
            nx = pl.ds(jnp.minimum(r0 + RB, T - 8), 8)
            nxt = jnp.where(i < nrb - 1, d_ref[nx, :] * b_ref[nx, :], 0.0)
            dch = w2 * dz + w1 * _lift_rows(dz, nxt, 1) + w0 * _lift_rows(dz, nxt, 2)
            dcg, dh = dch * h, dch * cg
            dc_ref[rs, :] = dcg.astype(bf16)
            dh_ref[rs, :] = dh.astype(bf16)
            col = lambda v: jnp.sum(v, axis=0, keepdims=True)
            return (a0 + col(dz * ch2), a1 + col(dz * ch1), a2 + col(dz * ch), sh + col(dh), sc + col(dcg), sb + col(dbg))

        zero = jnp.zeros((1, LANE), f32)
        a0, a1, a2, sh, sc, sb = lax.fori_loop(0, nrb, blk, (zero,) * 6)
        dw_ref[0:1, :] = a0
        dw_ref[1:2, :] = a1
        dw_ref[2:3, :] = a2
        s_ref[0:1, :] = sh
        s_ref[1:2, :] = sc
        s_ref[2:3, :] = sb

    nb = W // LANE
    slab = pl.BlockSpec((T, LANE), lambda k: (0, k))
    three = pl.BlockSpec((3, LANE), lambda k: (0, k))
    return _call(
        body, [proj, proj, proj, dyb, conv_w], name="conv_bwd", grid=(nb,),
        in_specs=[pl.BlockSpec((T, LANE), lambda k: (0, 4 * nb + k)), pl.BlockSpec((T, LANE), lambda k: (0, 5 * nb + k)),
                  pl.BlockSpec((T, LANE), lambda k: (0, 6 * nb + k)), slab, three],
        out_specs=[slab, slab, slab, three, three],
        out_shape=[S((T, W), bf16)] * 3 + [S((3, W), f32)] * 2, sem=("parallel",), vmem=VMEM_LIMIT, plan=plan)


def in_proj_bwd_x(parts, win_g, base, scale, name, plan=None):
    T = base.shape[0]
    tm = min(512, T)
    n = len(parts)

    def body(*refs):
        p_refs, w_ref, b_ref, o_ref = refs[:n], refs[n], refs[n + 1], refs[n + 2]
        acc = scale * b_ref[...]
        for p_ref, (_, _, k) in zip(p_refs, parts):
            acc += _dot_nt(p_ref[...], w_ref[k])
        o_ref[...] = acc

    row = pl.BlockSpec((tm, D), lambda i: (i, 0))
    p_specs = [pl.BlockSpec((tm, W), (lambda i, cb=cb: (i, cb))) for _, cb, _ in parts]
    return _call(
        body, [a for a, _, _ in parts] + [win_g, base], name=name, grid=(T // tm,),
        in_specs=p_specs + [_resident((NDEV, D, W)), row],
        out_specs=[row], out_shape=[S((T, D), f32)], vmem=VMEM_LIMIT, plan=plan)


def ssm_param_bwd(lam_re, lam_im, log_dt, fr, fi, br, bi, dwb, dwcT, dlbr, dlbi):
    def body(lr_ref, li_ref, ldt_ref, fr_ref, fi_ref, br_ref, bi_ref, dwb_ref, dwc_ref, dlbr_ref, dlbi_ref,
             dbr_ref, dbi_ref, dlr_ref, dli_ref, dldt_ref, dcr_ref, dci_ref, dr_s, di_s):
        for k in range(W // LANE):
            for gl in range(NG // (W // LANE)):
                rows, src = slice((8 * k + gl) * GC, (8 * k + gl + 1) * GC), slice(gl * GC, (gl + 1) * GC)
                re, im = slice(gl * NP, (gl + 1) * NP), slice(SW + gl * NP, SW + (gl + 1) * NP)
                dr_s[rows, :] = dwb_ref[k, src, re]
                di_s[rows, :] = dwb_ref[k, src, im]
                dcr_ref[rows, :] = dwc_ref[k, src, re]
                dci_ref[rows, :] = -dwc_ref[k, src, im]
        fr_, fi_ = _per_channel(fr_ref[...]), _per_channel(fi_ref[...])
        br_, bi_, dr, di = br_ref[...], bi_ref[...], dr_s[...], di_s[...]
        dbr_ref[...] = fr_ * dr + fi_ * di
        dbi_ref[...] = fr_ * di - fi_ * dr
        dfr = jnp.sum((dr * br_ + di * bi_).reshape(NG, GC, NP), axis=1)
        dfi = jnp.sum((di * br_ - dr * bi_).reshape(NG, GC, NP), axis=1)
        _, vjp = jax.vjp(_disc, lr_ref[...], li_ref[...], ldt_ref[...])
        dlr_ref[...], dli_ref[...], dldt = vjp((dlbr_ref[...], dlbi_ref[...], dfr, dfi))
        dldt_ref[...] = _transpose_exact(dldt)

    blk = S((NG * GC, NP), f32)
    return pl.pallas_call(
        body, name="ssm_param_bwd", out_shape=[blk, blk, S((NG, NP), f32), S((NG, NP), f32), S((1, NG), f32), blk, blk],
        scratch_shapes=[pltpu.VMEM((NG * GC, NP), f32)] * 2)(
        lam_re, lam_im, log_dt, fr, fi, br, bi, dwb, dwcT, dlbr, dlbi)


def _adam(w, g, m, v):
    m = ADAM_B1 * m + (1.0 - ADAM_B1) * g
    v = ADAM_B2 * v + (1.0 - ADAM_B2) * (g * g)
    m_hat = m / (1.0 - ADAM_B1 ** ADAM_STEP)
    v_hat = v / (1.0 - ADAM_B2 ** ADAM_STEP)
    return -ADAM_LR * (m_hat / (jnp.sqrt(v_hat) + ADAM_EPS) + ADAM_WD * w), m, v


def _sum_in_order(c_ref):
    g = c_ref[0].astype(f32)
    for k in range(1, c_ref.shape[0]):
        g = g + c_ref[k].astype(f32)
    return g


def sum_blocks(contrib, name):
    def body(c_ref, o_ref):
        o_ref[...] = _sum_in_order(c_ref)

    return pl.pallas_call(body, name=name, out_shape=S(contrib.shape[1:], f32))(contrib)


def adam_update(w, m, v, contrib, name, rows_per_block=None, summed_on_0=None, plan=None):
    R, C = w.shape
    n = contrib.shape[0]
    tr = min(rows_per_block or R, R)

    def body(w_ref, m_ref, v_ref, c_ref, *refs):
        g_ref, d_ref, nm_ref, nv_ref = refs[-4:]
        g = _sum_in_order(c_ref)
        if summed_on_0 is not None:
            x, y, c = _coords()
            g = jnp.where(4 * x + 2 * y + c == 0, refs[0][...], g)
        g_ref[...] = g
        d_ref[...], nm_ref[...], nv_ref[...] = _adam(w_ref[...], g, m_ref[...], v_ref[...])

    blk = pl.BlockSpec((tr, C), lambda i: (i, 0))
    extra = [] if summed_on_0 is None else [summed_on_0]
    return _call(
        body, [w, m, v, contrib] + extra, name=name, grid=(R // tr,),
        in_specs=[blk, blk, blk, pl.BlockSpec((n, tr, C), lambda i: (0, i, 0))] + [blk] * len(extra),
        out_specs=[blk] * 4, out_shape=[S((R, C), f32)] * 4, sem=("parallel",), vmem=VMEM_LIMIT, plan=plan)


_ROWVEC = (("b_in", IN_COLS), ("ssm_d", W), ("glu_b", W), ("ln1_g", D), ("ln1_b", D), ("ln2_g", D), ("ln2_b", D))
_HALF = NG * GC // 2
_BC_LANE = {"ssm_b_re": 0, "ssm_b_im": NP, "ssm_c_re": 0, "ssm_c_im": NP}
_PACK = {}
_r = 0
for _n, _k in _ROWVEC:
    _PACK[_n] = _r
    _r += _k // LANE
for _n, _rows in (("ssm_lambda", NG), ("scalars", 8), ("ssm_b", _HALF), ("ssm_c", _HALF), ("conv_w", 16)):
    _PACK[_n] = _r
    _r += _rows
for _n in _BC_LANE:
    _PACK[_n] = _PACK[_n[:5]]
PACK_ROWS = _r
assert PACK_ROWS % 8 == 0
_SMALL = ("b_in", "ssm_lambda_re", "ssm_lambda_im", "ssm_log_dt", "ssm_b_re", "ssm_b_im", "ssm_c_re", "ssm_c_im",
          "ssm_d", "glu_b", "ln1_g", "ln1_b", "ln2_g", "ln2_b")


def pack_grads(su, shcb, sga, sgb, dd, dglu_b, dln1_g, dln1_b, dln2_g, dln2_b, dlam_re, dlam_im, dldt, sqerr, dbr, dbi,
               dc_re, dc_im, dconv):
    nI = sga.shape[0]

    def body(su_ref, sh_ref, sga_ref, sgb_ref, dd_ref, gb_ref, l1g_ref, l1b_ref, l2g_ref, l2b_ref, lr_ref, li_ref, dt_ref,
             sq_ref, br_ref, bi_ref, cr_ref, ci_ref, cw_ref, o_ref):
        o_ref[...] = jnp.zeros_like(o_ref)

        def put_row(name, v):
            r0 = _PACK[name]
            for i in range(v.shape[1] // LANE):
                o_ref[r0 + i:r0 + i + 1, :] = v[:, i * LANE:(i + 1) * LANE]

        ga, gb = sga_ref[0], sgb_ref[0]
        for i in range(1, nI):
            ga, gb = ga + sga_ref[i], gb + sgb_ref[i]
        put_row("b_in", jnp.concatenate([su_ref[k] for k in range(W // LANE)]
                                        + [sh_ref[0:1, :], sh_ref[1:2, :], sh_ref[2:3, :], ga, gb], axis=1))
        put_row("ssm_d", jnp.concatenate([dd_ref[k] for k in range(W // LANE)], axis=1))
        put_row("glu_b", gb_ref[...])
        put_row("ln1_g", l1g_ref[...])
        put_row("ln1_b", l1b_ref[...])
        put_row("ln2_g", l2g_ref[...])
        put_row("ln2_b", l2b_ref[...])
        r0 = _PACK["ssm_lambda"]
        o_ref[r0:r0 + NG, 0:NP] = lr_ref[...]
        o_ref[r0:r0 + NG, NP:2 * NP] = li_ref[...]
        r0 = _PACK["scalars"]
        o_ref[r0:r0 + 1, 0:NG] = dt_ref[...]
        o_ref[r0 + 1:r0 + 2, 0:1] = sq_ref[...]
        for name, ref in (("ssm_b_re", br_ref), ("ssm_b_im", bi_ref), ("ssm_c_re", cr_ref), ("ssm_c_im", ci_ref)):
            r0, l0 = _PACK[name], _BC_LANE[name]
            o_ref[r0:r0 + _HALF, l0:l0 + NP] = pltpu.bitcast(ref[...].astype(bf16), f32)
        for cb in range(W // LANE):
            o_ref[_PACK["conv_w"] + 3 * cb:_PACK["conv_w"] + 3 * cb + 3, :] = cw_ref[:, cb * LANE:(cb + 1) * LANE]

    return pl.pallas_call(body, name="pack_grads", out_shape=S((PACK_ROWS, LANE), f32))(
        su, shcb, sga, sgb, dd, dglu_b, dln1_g, dln1_b, dln2_g, dln2_b, dlam_re, dlam_im, dldt, sqerr, dbr, dbi, dc_re, dc_im,
        dconv)


def adam_small(packed_all, params):
    names = list(_SMALL) + ["conv_w"]
    flat = [a for n in names for a in params[n]]

    def body(*refs):
        p_ref = refs[0]
        ins = refs[1:1 + 3 * len(names)]
        outs = refs[1 + 3 * len(names):-2]
        loss_ref, g_ref = refs[-2], refs[-1]

        def part(k, rs=slice(None), ls=slice(None)):
            return p_ref[k, rs, ls]

        g_all = part(0)
        for k in range(1, NDEV):
            g_all = g_all + part(k)
        g_ref[...] = g_all

        def rows(name, r0, n, l0=0, lanes=LANE):
            return g_ref[_PACK[name] + r0:_PACK[name] + r0 + n, l0:l0 + lanes]

        def grad_of(name):
            if name in dict(_ROWVEC):
                return jnp.concatenate([rows(name, i, 1) for i in range(dict(_ROWVEC)[name] // LANE)], axis=1)
            if name in ("ssm_lambda_re", "ssm_lambda_im"):
                return rows("ssm_lambda", 0, NG, NP * (name == "ssm_lambda_im"), NP)[None]
            if name == "ssm_log_dt":
                return rows("scalars", 0, 1, 0, NG)
            if name in _BC_LANE:
                rs, ls = slice(_PACK[name], _PACK[name] + _HALF), slice(_BC_LANE[name], _BC_LANE[name] + NP)
                g = pltpu.bitcast(part(0, rs, ls), bf16).astype(f32)
                for k in range(1, NDEV):
                    g = g + pltpu.bitcast(part(k, rs, ls), bf16).astype(f32)
                return g.reshape(1, NG, GC, NP)
            full = jnp.concatenate([rows("conv_w", 3 * cb, 3) for cb in range(W // LANE)], axis=1)
            x, y, c = _coords()
            col0 = (4 * x + 2 * y + c) * (W // NDEV)
            sel = (lax.broadcasted_iota(jnp.int32, (W, W // NDEV), 0)
                   == lax.broadcasted_iota(jnp.int32, (W, W // NDEV), 1) + col0).astype(f32)
            return jnp.dot(full, sel, precision=HIGHEST, preferred_element_type=f32)[None]

        loss_ref[...] = 0.5 * rows("scalars", 1, 1, 0, 1)
        for i, name in enumerate(names):
            w_ref, m_ref, v_ref = ins[3 * i:3 * i + 3]
            g = grad_of(name)
            d, m, v = _adam(w_ref[...], g, m_ref[...], v_ref[...])
            outs[4 * i][...] = g
            outs[4 * i + 1][...] = d
            outs[4 * i + 2][...] = m
            outs[4 * i + 3][...] = v

    out_shape = [S(params[n][0].shape, f32) for n in names for _ in range(4)] + [S((1, 1), f32)]
    res = pl.pallas_call(body, name="adam_small", out_shape=out_shape, scratch_shapes=[pltpu.VMEM((PACK_ROWS, LANE), f32)],
                         compiler_params=_cp(None, VMEM_LIMIT))(packed_all, *flat)
    return {n: res[4 * i:4 * i + 4] for i, n in enumerate(names)}, res[-1]


def _block_diag(wgt):
    eye = jnp.eye(8, dtype=wgt.dtype)
    out = wgt[:, :, :, None, :] * eye[None, :, None, :, None]
    return out.reshape(4, 8 * wgt.shape[2], 8 * wgt.shape[3])


def kernel(x, w_in, b_in, ssm_lambda_re, ssm_lambda_im, ssm_log_dt, ssm_b_re, ssm_b_im, ssm_c_re, ssm_c_im, ssm_d, glu_w, glu_b, w_ssm_out, conv_w, w_conv_out, w_o, ln1_g, ln1_b, w_gate, w_up, w_down, ln2_g, ln2_b, loss_target, m_w_in, m_b_in, m_ssm_lambda_re, m_ssm_lambda_im, m_ssm_log_dt, m_ssm_b_re, m_ssm_b_im, m_ssm_c_re, m_ssm_c_im, m_ssm_d, m_glu_w, m_glu_b, m_w_ssm_out, m_conv_w, m_w_conv_out, m_w_o, m_ln1_g, m_ln1_b, m_w_gate, m_w_up, m_w_down, m_ln2_g, m_ln2_b, v_w_in, v_b_in, v_ssm_lambda_re, v_ssm_lambda_im, v_ssm_log_dt, v_ssm_b_re, v_ssm_b_im, v_ssm_c_re, v_ssm_c_im, v_ssm_d, v_glu_w, v_glu_b, v_w_ssm_out, v_conv_w, v_w_conv_out, v_w_o, v_ln1_g, v_ln1_b, v_w_gate, v_w_up, v_w_down, v_ln2_g, v_ln2_b):
    given = dict(locals())
    xs = x[0]
    target = loss_target[0]

    tr = lambda a: jnp.swapaxes(a[0], 0, 1)
    win_s, glu_s, wso_s, wco_s, wo_s, wgT_s, wuT_s, wd_s = prep_weights(
        [w_in[0], glu_w[0], w_ssm_out[0], w_conv_out[0], w_o[0], tr(w_gate), tr(w_up), w_down[0]])
    (win_g,) = run_plan(GatherPlan([win_s], srcs=(0,)), "gather_w_in_u")

    lam_re, lam_im = ssm_lambda_re[0], ssm_lambda_im[0]
    ldt = ssm_log_dt[0].reshape(NG, 1)
    br2 = jnp.swapaxes(ssm_b_re[0], 1, 2).reshape(NG * GC, NP)
    bi2 = jnp.swapaxes(ssm_b_im[0], 1, 2).reshape(NG * GC, NP)
    lbr, lbi, fr, fi, bbr, bbi = ssm_params(lam_re, lam_im, ldt, br2, bi2)
    bb_t = lambda b: b.reshape(4, 8, GC, NP)
    wb = jnp.concatenate([_block_diag(bb_t(bbr)), _block_diag(bb_t(bbi))], axis=2)
    c_t = lambda c: c.reshape(4, 8, GC, NP).transpose(0, 1, 3, 2)
    wc = jnp.concatenate([_block_diag(c_t(ssm_c_re[0])), -_block_diag(c_t(ssm_c_im[0]))], axis=1)
    wbT, wcT = wb.transpose(0, 2, 1), wc.transpose(0, 2, 1)
    wb, wc, wbT, wcT = wb.astype(bf16), wc.astype(bf16), wbT.astype(bf16), wcT.astype(bf16)
    lbr_s, lbi_s = lbr.reshape(4, 1, SW), lbi.reshape(4, 1, SW)
    dsk = ssm_d[0].reshape(4, 1, LANE)

    u_nat, xb = in_proj_u(xs, win_g, b_in)
    half_a, half_b = (0, 3, 5, 6), (1, 2, 4, 7)
    (yn, u_p, xr_p, xi_p), (win_g, conv_g, glu_g, wso_g, wuT_g) = ssm_fwd(
        u_nat, wb, wc, lbr_s, lbi_s, dsk,
        Plans([GatherPlan([win_s], srcs=tuple(range(1, NDEV)), into=[win_g]), GatherPlan([conv_w[0], glu_s, wso_s]),
               GatherPlan([wuT_s], srcs=half_a)]))
    conv_f = conv_g.transpose(1, 0, 2).reshape(3, W)
    (proj,), (wco_g, wo_g, wgT_g) = in_proj_rest(
        xb, win_g, b_in, Plans([GatherPlan([wco_s, wo_s]), GatherPlan([wgT_s], srcs=half_a)]))
    glu_f, wo_f = glu_g.reshape(W, W), wo_g.reshape(D, D)
    yb = conv_fwd(proj, conv_f)
    (merged, ya), (wgT_g,) = merge_fwd(yn, glu_f, glu_b, yb, wso_g, wco_g, proj,
                                       GatherPlan([wgT_s], srcs=half_b, into=[wgT_g]))
    (r1, x1b), (wuT_g,) = mix_ln1(merged, wo_f, xs, ln1_g, ln1_b, GatherPlan([wuT_s], srcs=half_b, into=[wuT_g]))
    wgT, wuT = wgT_g.reshape(F, D), wuT_g.reshape(F, D)
    (gate, up, hid), (wd_g,) = gate_up(x1b, wgT, wuT, GatherPlan([wd_s]))
    wd_f = wd_g.reshape(F, D)
    dr2, dffn, sqerr, dln2_g, dln2_b = down_loss(hid, wd_f, r1, ln1_g, ln1_b, ln2_g, ln2_b, target)

    dwd, _ = mm_tn_rows(hid, dffn, "grad_w_down")
    dwd = dwd.reshape(NDEV, FS, D)
    (dgate, dup), (r_wd,) = ffn_bwd_act(dffn, wd_f, gate, up, ScatterPlan([dwd], only=half_a))
    dwgT, (r_wd,) = mm_tn_rows(dgate, x1b, "grad_w_gate", plan=ScatterPlan([dwd], only=half_b, into=[r_wd]))
    dwgT = dwgT.reshape(NDEV, FS, D)
    dwuT, (r_wgT,) = mm_tn_rows(dup, x1b, "grad_w_up", plan=ScatterPlan([dwgT], only=half_a))
    dwuT = dwuT.reshape(NDEV, FS, D)
    (dr1, dmix, dln1_g, dln1_b), (r_wgT, r_wuT) = ffn_bwd_x(
        dgate, dup, wgT, wuT, dr2, r1, ln1_g,
        Plans([ScatterPlan([dwgT], only=half_b, into=[r_wgT]), ScatterPlan([dwuT], only=half_a)]))
    (dYA, dYB, dga, dgb, sga, sgb, dwo), (r_wuT,) = merge_bwd(dmix, wo_f, merged, ya, yb, wso_g, wco_g, proj,
                                                              ScatterPlan([dwuT], only=half_b, into=[r_wuT]))
    dwo = dwo.reshape(NDEV, D // NDEV, D)
    dya, dyb, dwso, dwco = branches_bwd(dYA, dYB, ya, yb, wso_g, wco_g)
    (dyn, dglu_b, dglu), (r_wso,) = glu_bwd(yn, dya, glu_f, glu_b, ScatterPlan([dwso]))
    dglu = dglu.reshape(NDEV, W // NDEV, W)
    (dh, dcg, dbg, dconv, shcb), (r_wco,) = conv_bwd(proj, dyb, conv_f, ScatterPlan([dwco]))
    dwin, (r_wo, r_glu) = grad_w_in_rest(xb, dh, dcg, dbg, dga, dgb, ScatterPlan([dwo, dglu]))
    (du, dwb, dwcT, dlbr_s, dlbi_s, dd, su), (r_win,) = ssm_bwd(
        u_p, dyn, xr_p, xi_p, wbT, wcT, lbr_s, lbi_s, dsk, ScatterPlan([dwin], only=tuple(range(1, NDEV))))

    dbr2, dbi2, dlam_re, dlam_im, dldt, dc_re, dc_im = ssm_param_bwd(
        lam_re, lam_im, ldt, fr, fi, br2, bi2, dwb, dwcT, dlbr_s.reshape(NG, NP), dlbi_s.reshape(NG, NP))
    packed = pack_grads(su, shcb, sga, sgb, dd, dglu_b, dln1_g, dln1_b, dln2_g, dln2_b, dlam_re, dlam_im, dldt, sqerr,
                        dbr2, dbi2, dc_re, dc_im, dconv)
    dwin_u = mm_tn(xb, du, "grad_w_in_u").reshape(NDEV, D // NDEV, W)

    rest = [(dh, 0, 1), (dcg, 0, 2), (dbg, 0, 3), (dga, 0, 4), (dga, 1, 5), (dgb, 0, 6), (dgb, 1, 7)]
    (gx_rest,), (r_win_u, small_all) = in_proj_bwd_x(
        rest, win_g, dr1, ALPHA, "in_proj_bwd_x_rest", Plans([ScatterPlan([dwin_u]), GatherPlan([packed])]))
    my_rows = sum_blocks(r_win_u, "sum_w_in_u")

    out = {}

    def put(name, res, back=lambda a: a[None]):
        out["grad_" + name], out["delta_" + name], out["new_m_" + name], out["new_v_" + name] = [back(r) for r in res]

    res_wd, (win_u_sum,) = adam_update(w_down[0], m_w_down[0], v_w_down[0], r_wd, "adam_w_down", 176,
                                       plan=ScatterPlan([my_rows], only=(0,), whole=True))
    put("w_down", res_wd)
    (grad_x,), _ = in_proj_bwd_x([(du, 0, 0)], win_g, gx_rest, 1.0, "in_proj_bwd_x_u")
    put("w_in", adam_update(w_in[0], m_w_in[0], v_w_in[0], r_win, "adam_w_in", 256,
                            summed_on_0=win_u_sum.reshape(D, W))[0])
    put("glu_w", adam_update(glu_w[0], m_glu_w[0], v_glu_w[0], r_glu, "adam_glu_w")[0])
    put("w_ssm_out", adam_update(w_ssm_out[0], m_w_ssm_out[0], v_w_ssm_out[0], r_wso, "adam_w_ssm_out")[0])
    put("w_conv_out", adam_update(w_conv_out[0], m_w_conv_out[0], v_w_conv_out[0], r_wco, "adam_w_conv_out")[0])
    put("w_o", adam_update(w_o[0], m_w_o[0], v_w_o[0], r_wo, "adam_w_o")[0])
    untr = lambda a: jnp.swapaxes(a, 0, 1)[None]
    put("w_gate", adam_update(tr(w_gate), tr(m_w_gate), tr(v_w_gate), r_wgT, "adam_w_gate", 176)[0], untr)
    put("w_up", adam_update(tr(w_up), tr(m_w_up), tr(v_w_up), r_wuT, "adam_w_up", 176)[0], untr)
    as_c = lambda a: jnp.swapaxes(a, 2, 3)
    params = {n: (given[n], given["m_" + n], given["v_" + n]) for n in list(_SMALL) + ["conv_w"]}
    for n in ("ssm_b_re", "ssm_b_im"):
        params[n] = tuple(as_c(a) for a in params[n])
    small, loss = adam_small(small_all, params)
    for n, res in small.items():
        put(n, res, as_c if n in ("ssm_b_re", "ssm_b_im") else (lambda a: a))

    names = ["w_in", "b_in", "ssm_lambda_re", "ssm_lambda_im", "ssm_log_dt", "ssm_b_re", "ssm_b_im", "ssm_c_re", "ssm_c_im",
             "ssm_d", "glu_w", "glu_b", "w_ssm_out", "conv_w", "w_conv_out", "w_o", "ln1_g", "ln1_b", "w_gate", "w_up",
             "w_down", "ln2_g", "ln2_b"]
    return (loss.reshape(()), grad_x[None], *[out[p + n] for p in ("grad_", "delta_", "new_m_", "new_v_") for n in names])
```

```python
import functools
import math

import jax
import jax.numpy as jnp
from jax import lax
from jax.experimental import pallas as pl
from jax.experimental.pallas import tpu as pltpu

f32, bf16 = jnp.float32, jnp.bfloat16
S = jax.ShapeDtypeStruct
MESH = pl.DeviceIdType.MESH
HIGHEST = lax.Precision.HIGHEST

D = 1024
W = 512
NG, NP, GC = 32, 64, 16
F = 2816
NDEV = 8
FS = F // NDEV
IN_COLS = 8 * W
ALPHA = 2.0 ** 0.25
LN_EPS = 1e-5
ADAM_LR, ADAM_B1, ADAM_B2, ADAM_EPS, ADAM_WD, ADAM_STEP = 0.001, 0.9, 0.999, 1e-08, 0.01, 10
NC = 32
LANE = 128
SW = 4 * LANE
VMEM_LIMIT = 56 * 1024 * 1024
GRAD_DT = bf16
ANY = pl.BlockSpec(memory_space=pl.ANY)


def _cp(sem=None, vmem=None):
    return pltpu.CompilerParams(dimension_semantics=sem, vmem_limit_bytes=vmem)


def _resident(shape):
    return pl.BlockSpec(shape, lambda i: (0,) * len(shape), pipeline_mode=pl.Buffered(1))


def _dot(a, b):
    return jnp.dot(a, b, preferred_element_type=f32)


def _dot_nt(a, b):
    return lax.dot_general(a, b, (((1,), (1,)), ((), ())), preferred_element_type=f32)


def _dot_tn(a, b):
    return lax.dot_general(a, b, (((0,), (0,)), ((), ())), preferred_element_type=f32)


def _eye(n):
    return (lax.broadcasted_iota(jnp.int32, (n, n), 0) == lax.broadcasted_iota(jnp.int32, (n, n), 1)).astype(f32)


def _transpose_exact(a):
    return lax.dot_general(a, _eye(a.shape[0]), (((0,), (0,)), ((), ())), precision=HIGHEST, preferred_element_type=f32)


def _sigmoid(x):
    return 1.0 / (1.0 + jnp.exp(-x))


_GK = math.sqrt(2.0 / math.pi)


def _gelu(x):
    return 0.5 * x * (1.0 + jnp.tanh(_GK * (x + 0.044715 * x * x * x)))


def _gelu_grad(x):
    th = jnp.tanh(_GK * (x + 0.044715 * x * x * x))
    return 0.5 * (1.0 + th) + 0.5 * x * (1.0 - th * th) * _GK * (1.0 + 3.0 * 0.044715 * x * x)


ROW_PART = 256


def _row_parts(tm):
    return [slice(r, r + min(ROW_PART, tm)) for r in range(0, tm, min(ROW_PART, tm))]


def _ln_stats(r):
    mu = jnp.mean(r, axis=-1, keepdims=True)
    xc = r - mu
    var = jnp.mean(xc * xc, axis=-1, keepdims=True)
    rstd = lax.rsqrt(var + LN_EPS)
    return xc * rstd, rstd


def _ln_bwd(dy, xhat, rstd, g):
    dxh = dy * g
    m1 = jnp.mean(dxh, axis=-1, keepdims=True)
    m2 = jnp.mean(dxh * xhat, axis=-1, keepdims=True)
    return rstd * (dxh - m1 - xhat * m2)


def _coords():
    return lax.axis_index("x"), lax.axis_index("y"), lax.axis_index("c")


def _when(cond, fn):
    if cond is True:
        fn()
    else:
        pl.when(cond)(fn)


class GatherPlan:
    aliases = ()

    def __init__(self, arrs, srcs=None, into=None):
        n = self.n = len(arrs)
        self.srcs = srcs
        self.inputs = list(arrs) + list(into or [])
        if into:
            self.aliases = tuple((n + a, a) for a in range(n))
        self.out_shape = [S((NDEV,) + a.shape, a.dtype) for a in arrs]
        self.sems = [pltpu.SemaphoreType.DMA((n, 7)), pltpu.SemaphoreType.DMA((n, 7)), pltpu.SemaphoreType.DMA((n,))]

    def _has(self, dev):
        if self.srcs is None:
            return True
        idx = 4 * dev[0] + 2 * dev[1] + dev[2]
        return functools.reduce(jnp.logical_or, [idx == s for s in self.srcs])

    def _parts(self, ins, outs, sems):
        n = self.n
        send_sems, recv_sems, loc_sems = sems
        x, y, c = _coords()
        me, sib = (x, y, c), (x, y, 1 - c)
        chips = [(1 - x, y), (x, 1 - y), (1 - x, 1 - y)]

        def slot(a, dev):
            return outs[a].at[4 * dev[0] + 2 * dev[1] + dev[2]]

        def copy(a, k, block, to, src=None):
            return pltpu.make_async_remote_copy(
                src_ref=slot(a, block) if src is None else src, dst_ref=slot(a, block),
                send_sem=send_sems.at[a, k], recv_sem=recv_sems.at[a, k], device_id=to, device_id_type=MESH)

        each = [(j, chip, a) for j, chip in enumerate(chips) for a in range(n)]
        own = self._has(me)
        return dict(
            mine=lambda: [(pltpu.make_async_copy(ins[a], slot(a, me), loc_sems.at[a]), own) for a in range(n)],
            first=lambda: ([(copy(a, 0, me, sib, src=ins[a]), own) for a in range(n)]
                           + [(copy(a, 1 + j, me, (*chip, c), src=ins[a]), own) for j, chip, a in each]),
            landed=lambda: [(copy(a, 1 + j, (*chip, c), me), self._has((*chip, c))) for j, chip, a in each],
            passed=lambda: [(copy(a, 4 + j, (*chip, c), sib), self._has((*chip, c))) for j, chip, a in each],
            from_sib=lambda: ([(copy(a, 0, sib, me), self._has(sib)) for a in range(n)]
                              + [(copy(a, 4 + j, (*chip, 1 - c), me), self._has((*chip, 1 - c))) for j, chip, a in each]))

    def start(self, ins, outs, sems):
        p = self._parts(ins, outs, sems)
        for cp, cond in p["mine"]() + p["first"]():
            _when(cond, cp.start)

    def forward(self, ins, outs, sems):
        p = self._parts(ins, outs, sems)
        for (got, cond), (fwd, _) in zip(p["landed"](), p["passed"]()):
            def relay(got=got, fwd=fwd):
                got.wait_recv()
                fwd.start()

            _when(cond, relay)

    def finish(self, ins, outs, sems):
        p = self._parts(ins, outs, sems)
        for cp, cond in p["from_sib"]():
            _when(cond, cp.wait_recv)
        for cp, cond in p["first"]() + p["passed"]():
            _when(cond, cp.wait_send)
        for cp, cond in p["mine"]():
            _when(cond, cp.wait)


class ScatterPlan:
    aliases = ()

    def __init__(self, gs, only=None, into=None, whole=False):
        n = self.n = len(gs)
        self.only = only
        self.whole = whole
        self.inputs = list(gs) + list(into or [])
        if into:
            self.aliases = tuple((n + a, a) for a in range(n))
        self.out_shape = [S((NDEV,) + g.shape if whole else g.shape, g.dtype) for g in gs]
        self.sems = [pltpu.SemaphoreType.DMA((n, 7)), pltpu.SemaphoreType.DMA((n, 7)), pltpu.SemaphoreType.DMA((n,))]

    def _owner(self, idx):
        if self.only is None:
            return True
        return functools.reduce(jnp.logical_or, [idx == b for b in self.only])

    def _copies(self, ins, outs, sems):
        n = self.n
        send_sems, recv_sems, loc_sems = sems
        x, y, c = _coords()
        me = 4 * x + 2 * y + c
        mine = self._owner(me)
        block = (lambda a, k: ins[a]) if self.whole else (lambda a, k: ins[a].at[k])
        copies = [(pltpu.make_async_copy(block(a, me), outs[a].at[me], loc_sems.at[a]), mine, None) for a in range(n)]
        for m in range(1, NDEV):
            px = 1 - x if m & 4 else x
            py = 1 - y if m & 2 else y
            pc = 1 - c if m & 1 else c
            peer = 4 * px + 2 * py + pc
            for a in range(n):
                copies.append((pltpu.make_async_remote_copy(
                    src_ref=block(a, peer), dst_ref=outs[a].at[me],
                    send_sem=send_sems.at[a, m - 1], recv_sem=recv_sems.at[a, m - 1],
                    device_id=(px, py, pc), device_id_type=MESH), self._owner(peer), mine))
        return copies

    def start(self, ins, outs, sems):
        for cp, sends, _ in self._copies(ins, outs, sems):
            _when(sends, cp.start)

    def forward(self, ins, outs, sems):
        pass

    def finish(self, ins, outs, sems):
        for cp, sends, receives in self._copies(ins, outs, sems):
            if receives is None:
                _when(sends, cp.wait)
            else:
                _when(sends, cp.wait_send)
                _when(receives, cp.wait_recv)


class Plans:
    def __init__(self, plans):
        self.plans = plans
        self.inputs = [a for p in plans for a in p.inputs]
        self.out_shape = [s for p in plans for s in p.out_shape]
        self.sems = [s for p in plans for s in p.sems]
        self.aliases, i, o = [], 0, 0
        for p in plans:
            self.aliases += [(i + a, o + b) for a, b in p.aliases]
            i, o = i + len(p.inputs), o + len(p.out_shape)

    def _each(self, what, ins, outs, sems):
        i = o = s = 0
        for p in self.plans:
            ni, no, ns = len(p.inputs), len(p.out_shape), len(p.sems)
            getattr(p, what)(ins[i:i + ni], outs[o:o + no], sems[s:s + ns])
            i, o, s = i + ni, o + no, s + ns

    def start(self, ins, outs, sems):
        self._each("start", ins, outs, sems)

    def forward(self, ins, outs, sems):
        self._each("forward", ins, outs, sems)

    def finish(self, ins, outs, sems):
        self._each("finish", ins, outs, sems)


def _call(body, args, *, name, grid, in_specs, out_specs, out_shape, scratch=(), sem=None, vmem=None, plan=None,
          relay_step=None, relay_after=False):
    if plan is None:
        outs = pl.pallas_call(body, name=name, grid=grid, in_specs=list(in_specs), out_specs=list(out_specs),
                              out_shape=list(out_shape), scratch_shapes=list(scratch),
                              compiler_params=_cp(sem, vmem))(*args)
        return list(outs), []
    ni, no, ns = len(in_specs), len(out_specs), len(scratch)
    pi, po = len(plan.inputs), len(plan.out_shape)
    aliases = {ni + a: no + b for a, b in plan.aliases}

    def wrapped(*refs):
        main_in, p_in = refs[:ni], refs[ni:ni + pi]
        main_out, p_out = refs[ni + pi:ni + pi + no], refs[ni + pi + no:ni + pi + no + po]
        main_scr, p_sems = refs[ni + pi + no + po:ni + pi + no + po + ns], refs[ni + pi + no + po + ns:]
        ids = [pl.program_id(d) for d in range(len(grid))]
        first = functools.reduce(jnp.logical_and, [i == 0 for i in ids])
        last = functools.reduce(jnp.logical_and, [i == g - 1 for i, g in zip(ids, grid)])

        @pl.when(first)
        def _():
            plan.start(p_in, p_out, p_sems)

        if not relay_after:
            @pl.when(last if relay_step is None else ids[0] == max(relay_step, 0))
            def _():
                plan.forward(p_in, p_out, p_sems)

        body(*main_in, *main_out, *main_scr)

        @pl.when(last)
        def _():
            if relay_after:
                plan.forward(p_in, p_out, p_sems)
            plan.finish(p_in, p_out, p_sems)

    outs = pl.pallas_call(
        wrapped, name=name, grid=grid, in_specs=list(in_specs) + [ANY] * pi, out_specs=list(out_specs) + [ANY] * po,
        out_shape=list(out_shape) + list(plan.out_shape), scratch_shapes=list(scratch) + list(plan.sems),
        input_output_aliases=aliases, compiler_params=_cp(("arbitrary",) * len(grid), vmem),
    )(*args, *plan.inputs)
    return list(outs[:no]), list(outs[no:])


def mm_tn(a, b, name, tn=512):
    T, K = a.shape
    N = b.shape[1]
    tn = min(tn, N)

    def body(a_ref, b_ref, o_ref):
        o_ref[...] = _dot_tn(a_ref[...], b_ref[...]).astype(GRAD_DT)

    (out,), _ = _call(body, [a, b], name=name, grid=(N // tn,),
                      in_specs=[_resident((T, K)), pl.BlockSpec((T, tn), lambda j: (0, j))],
                      out_specs=[pl.BlockSpec((None, K, tn), lambda j: (j, 0, 0))],
                      out_shape=[S((N // tn, K, tn), GRAD_DT)], sem=("parallel",), vmem=VMEM_LIMIT)
    return out


def grad_w_in_rest(xb, dh, dcg, dbg, dga, dgb, plan):
    T = xb.shape[0]
    order = ((0, 0), (1, 1), (2, 2), (3, 3), (4, 3), (5, 4), (6, 4))

    def body(x_ref, *refs):
        o_ref = refs[-1]
        j = pl.program_id(0)
        for step, opnd in order:
            @pl.when(j == step)
            def _(opnd=opnd):
                o_ref[...] = _dot_tn(x_ref[...], refs[opnd][...]).astype(GRAD_DT)

    once = lambda: pl.BlockSpec((T, W), lambda j: (0, 0), pipeline_mode=pl.Buffered(1))
    (out,), sent = _call(
        body, [xb, dh, dcg, dbg, dga, dgb], name="grad_w_in_rest", grid=(len(order),),
        in_specs=[_resident((T, D)), once(), once(), once(),
                  pl.BlockSpec((T, W), lambda j: (0, jnp.clip(j - 3, 0, 1))),
                  pl.BlockSpec((T, W), lambda j: (0, jnp.clip(j - 5, 0, 1)))],
        out_specs=[pl.BlockSpec((None, D, W), lambda j: (1 + j, 0, 0))],
        out_shape=[S((NDEV, D, W), GRAD_DT)], sem=("arbitrary",), vmem=VMEM_LIMIT, plan=plan)
    return out, sent


def mm_tn_rows(a, b, name, tk=256, plan=None):
    T, K = a.shape
    N = b.shape[1]
    tk = min(tk, K)

    def body(a_ref, b_ref, o_ref):
        o_ref[...] = _dot_tn(a_ref[...], b_ref[...]).astype(GRAD_DT)

    (out,), sent = _call(body, [a, b], name=name, grid=(K // tk,),
                         in_specs=[pl.BlockSpec((T, tk), lambda i: (0, i)), _resident((T, N))],
                         out_specs=[pl.BlockSpec((tk, N), lambda i: (i, 0))], out_shape=[S((K, N), GRAD_DT)],
                         sem=("parallel",), vmem=VMEM_LIMIT, plan=plan)
    return out, sent


def prep_weights(ws, name, plan=None):
    def body(*refs):
        for i in range(len(ws)):
            refs[len(ws) + i][...] = refs[i][...].astype(bf16)

    whole = [pl.BlockSpec(w.shape, lambda i, n=w.ndim: (0,) * n) for w in ws]
    return _call(body, list(ws), name=name, grid=(1,), in_specs=whole, out_specs=whole,
                 out_shape=[S(w.shape, bf16) for w in ws], sem=("arbitrary",), vmem=VMEM_LIMIT, plan=plan)


REST_BLOCKS = (4, 5, 6, 7, 1, 2, 3)
REST_COLS = len(REST_BLOCKS) * W


def in_proj_u(x, win_g, b_in):
    T = x.shape[0]
    tm = min(1024, T)

    def body(x_ref, w_ref, b_ref, u_ref, xb_ref):
        xb = x_ref[...].astype(bf16)
        xb_ref[...] = xb
        u_ref[...] = _dot(xb, w_ref[...]) + b_ref[...]

    row = pl.BlockSpec((tm, D), lambda i: (i, 0))
    return pl.pallas_call(
        body, name="in_proj_u", grid=(T // tm,),
        in_specs=[row, pl.BlockSpec((None, D, W), lambda i: (0, 0, 0)), pl.BlockSpec((1, W), lambda i: (0, 0))],
        out_specs=[pl.BlockSpec((tm, W), lambda i: (i, 0)), row],
        out_shape=[S((T, W), f32), S((T, D), bf16)], compiler_params=_cp(("parallel",), VMEM_LIMIT),
    )(x, win_g, b_in)


def in_proj_rest(xb, win_g, b_in, plan):
    T = xb.shape[0]
    tm = min(512, T)

    def body(x_ref, w_ref, b_ref, o_ref):
        xb_ = x_ref[...]
        for i, k in enumerate(REST_BLOCKS):
            o_ref[:, i * W:(i + 1) * W] = _dot(xb_, w_ref[k]) + b_ref[:, k * W:(k + 1) * W]

    return _call(
        body, [xb, win_g, b_in], name="in_proj_rest", grid=(T // tm,),
        in_specs=[pl.BlockSpec((tm, D), lambda i: (i, 0)), _resident((NDEV, D, W)), _resident((1, IN_COLS))],
        out_specs=[pl.BlockSpec((tm, REST_COLS), lambda i: (i, 0))],
        out_shape=[S((T, REST_COLS), f32)], vmem=VMEM_LIMIT, plan=plan, relay_after=True)


def _to_scan_order(a_ref, o_ref):
    L = a_ref.shape[0] // NC

    def step(jb, carry):
        j0 = pl.multiple_of(jb * 8, 8)
        for q in range(NC // 8):
            x = jnp.stack([a_ref[pl.ds((8 * q + c) * L + j0, 8), :] for c in range(8)], axis=0)
            y = jnp.swapaxes(x, 0, 1)
            for j in range(8):
                o_ref[pl.ds((j0 + j) * NC + 8 * q, 8), :] = y[j]
        return carry

    lax.fori_loop(0, L // 8, step, 0)


def _to_time_order(a_ref, o_ref):
    L = a_ref.shape[0] // NC

    def step(jb, carry):
        j0 = pl.multiple_of(jb * 16, 16)
        for q in range(NC // 8):
            halves = []
            for h in range(2):
                x = jnp.stack([a_ref[pl.ds((j0 + 8 * h + j) * NC + 8 * q, 8), :] for j in range(8)], axis=0)
                halves.append(jnp.swapaxes(x, 0, 1))
            for c in range(8):
                o_ref[pl.ds((8 * q + c) * L + j0, 16), :] = jnp.concatenate(
                    [halves[0][c], halves[1][c]], axis=0).astype(o_ref.dtype)
        return carry

    lax.fori_loop(0, L // 16, step, 0)


def _disc(lr, li, ldt):
    dt = jnp.exp(ldt)
    mag = jnp.exp(lr * dt)
    lbr = mag * jnp.cos(li * dt)
    lbi = mag * jnp.sin(li * dt)
    den = lr * lr + li * li
    nr = lbr - 1.0
    return lbr, lbi, (nr * lr + lbi * li) / den, (lbi * lr - nr * li) / den


def _per_channel(f):
    return jnp.broadcast_to(f[:, None, :], (NG, GC, NP)).reshape(NG * GC, NP)


def ssm_params(lam_re, lam_im, log_dt, br, bi):
    def body(lr_ref, li_ref, ldt_ref, br_ref, bi_ref, lbr_ref, lbi_ref, fr_ref, fi_ref, bbr_ref, bbi_ref):
        lbr, lbi, fr, fi = _disc(lr_ref[...], li_ref[...], ldt_ref[...])
        lbr_ref[...], lbi_ref[...], fr_ref[...], fi_ref[...] = lbr, lbi, fr, fi
        fr_, fi_, br_, bi_ = _per_channel(fr), _per_channel(fi), br_ref[...], bi_ref[...]
        bbr_ref[...] = fr_ * br_ - fi_ * bi_
        bbi_ref[...] = fr_ * bi_ + fi_ * br_

    return pl.pallas_call(body, name="ssm_params", out_shape=[S((NG, NP), f32)] * 4 + [S((NG * GC, NP), f32)] * 2)(
        lam_re, lam_im, log_dt, br, bi)


SCAN_UNROLL = 4
SCAN_LANES = 2 * LANE


def _steps(n, body, carry):
    main = n // SCAN_UNROLL

    def trip(t, c):
        for q in range(SCAN_UNROLL):
            c = body(t * SCAN_UNROLL + q, c)
        return c

    carry = lax.fori_loop(0, main, trip, carry)
    for i in range(main * SCAN_UNROLL, n):
        carry = body(i, carry)
    return carry


def _scan_body(T):
    L = T // NC
    RB = min(512, T)
    nsq = int(round(math.log2(L)))
    assert 2 ** nsq == L and T % RB == 0 and L % 16 == 0

    def rows(i):
        return pl.ds(pl.multiple_of(i * RB, RB), RB)

    def tile(j):
        return pl.ds(j * NC if isinstance(j, int) else pl.multiple_of(j * NC, NC), NC)

    def forward_states(u_ref, wb_ref, lbr_ref, lbi_ref, sre, sim, ere, eim):
        def bproj(i, carry):
            bu = _dot(u_ref[rows(i), :].astype(bf16), wb_ref[...])
            sre[rows(i), :] = bu[:, :SW]
            sim[rows(i), :] = bu[:, SW:]
            return carry

        lax.fori_loop(0, T // RB, bproj, 0)
        for lb in range(SW // SCAN_LANES):
            ls = slice(lb * SCAN_LANES, (lb + 1) * SCAN_LANES)
            ar = jnp.broadcast_to(lbr_ref[:, ls], (NC, SCAN_LANES))
            ai = jnp.broadcast_to(lbi_ref[:, ls], (NC, SCAN_LANES))

            def step(j, carry):
                xr, xi = carry
                nr = ar * xr - ai * xi + sre[tile(j), ls]
                ni = ar * xi + ai * xr + sim[tile(j), ls]
                sre[tile(j), ls] = nr
                sim[tile(j), ls] = ni
                return nr, ni

            zero = jnp.zeros((NC, SCAN_LANES), f32)
            _steps(L, step, (zero, zero))
            pr, pi = lbr_ref[:, ls], lbi_ref[:, ls]
            for _ in range(nsq):
                pr, pi = pr * pr - pi * pi, 2.0 * pr * pi
            er = jnp.zeros((1, SCAN_LANES), f32)
            ei = er
            ere[0:1, ls] = er
            eim[0:1, ls] = ei
            base = (L - 1) * NC
            for c in range(1, NC):
                lr_ = sre[base + c - 1:base + c, ls]
                li_ = sim[base + c - 1:base + c, ls]
                er, ei = lr_ + pr * er - pi * ei, li_ + pr * ei + pi * er
                ere[c:c + 1, ls] = er
                eim[c:c + 1, ls] = ei
            e_r, e_i = ere[:, ls].reshape(NC // 8, 8, SCAN_LANES), eim[:, ls].reshape(NC // 8, 8, SCAN_LANES)
            ar8, ai8 = ar[0:8], ai[0:8]

            def fix(j, carry):
                pwr, pwi = carry
                xr = sre[tile(j), ls].reshape(NC // 8, 8, SCAN_LANES) + (pwr * e_r - pwi * e_i)
                xi = sim[tile(j), ls].reshape(NC // 8, 8, SCAN_LANES) + (pwr * e_i + pwi * e_r)
                sre[tile(j), ls] = xr.reshape(NC, SCAN_LANES)
                sim[tile(j), ls] = xi.reshape(NC, SCAN_LANES)
                return pwr * ar8 - pwi * ai8, pwr * ai8 + pwi * ar8

            _steps(L, fix, (ar8, ai8))

    return L, RB, nsq, rows, tile, forward_states


def ssm_fwd(u, wb, wc, lbr, lbi, dsk, plan):
    T = u.shape[0]
    L, RB, nsq, rows, tile, forward_states = _scan_body(T)
    nslab = W // LANE

    def body(u_ref, wb_ref, wc_ref, lbr_ref, lbi_ref, d_ref, y_ref, up_ref, xr_ref, xi_ref, sre, sim, ere, eim, yp):
        _to_scan_order(u_ref, up_ref)
        forward_states(up_ref, wb_ref, lbr_ref, lbi_ref, sre, sim, ere, eim)

        def cproj(i, carry):
            xr, xi = sre[rows(i), :].astype(bf16), sim[rows(i), :].astype(bf16)
            xr_ref[rows(i), :] = xr
            xi_ref[rows(i), :] = xi
            y = _dot(xr, wc_ref[0:SW, :]) + _dot(xi, wc_ref[SW:, :])
            yp[rows(i), :] = y + d_ref[...] * up_ref[rows(i), :]
            return carry

        lax.fori_loop(0, T // RB, cproj, 0)
        _to_time_order(yp, y_ref)

    slab = pl.BlockSpec((T, LANE), lambda k: (0, k))
    states = pl.BlockSpec((T, SW), lambda k: (0, k))
    return _call(
        body, [u, wb, wc, lbr, lbi, dsk], name="ssm_fwd", grid=(nslab,),
        in_specs=[slab, pl.BlockSpec((None, LANE, 2 * SW), lambda k: (k, 0, 0)),
                  pl.BlockSpec((None, 2 * SW, LANE), lambda k: (k, 0, 0)),
                  pl.BlockSpec((None, 1, SW), lambda k: (k, 0, 0)), pl.BlockSpec((None, 1, SW), lambda k: (k, 0, 0)),
                  pl.BlockSpec((None, 1, LANE), lambda k: (k, 0, 0))],
        out_specs=[slab, slab, states, states],
        out_shape=[S((T, W), f32), S((T, W), f32), S((T, nslab * SW), bf16), S((T, nslab * SW), bf16)],
        scratch=[pltpu.VMEM((T, SW), f32), pltpu.VMEM((T, SW), f32), pltpu.VMEM((NC, SW), f32), pltpu.VMEM((NC, SW), f32),
                 pltpu.VMEM((T, LANE), f32)],
        vmem=VMEM_LIMIT, plan=plan, relay_after=True)


def ssm_bwd(u_p, dy, xr, xi, wbT, wcT, lbr, lbi, dsk, plan):
    T = u_p.shape[0]
    L, RB, nsq, rows, tile, _ = _scan_body(T)

    def body(u_ref, dyt_ref, sre, sim, wbT_ref, wcT_ref, lbr_ref, lbi_ref, d_ref,
             dut_ref, dwb_ref, dwc_ref, dlr_ref, dli_ref, dd_ref, su_ref, gre, gim, ere, eim, dy_ref, du_ref):
        _to_scan_order(dyt_ref, dy_ref)

        def dstate(i, carry):
            g = _dot(dy_ref[rows(i), :].astype(bf16), wcT_ref[...])
            gre[rows(i), :] = g[:, :SW]
            gim[rows(i), :] = g[:, SW:]
            return carry

        lax.fori_loop(0, T // RB, dstate, 0)
        row = lax.broadcasted_iota(jnp.int32, (NC, SCAN_LANES), 0)
        for lb in range(SW // SCAN_LANES):
            ls = slice(lb * SCAN_LANES, (lb + 1) * SCAN_LANES)
            ar = jnp.broadcast_to(lbr_ref[:, ls], (NC, SCAN_LANES))
            ai = jnp.broadcast_to(lbi_ref[:, ls], (NC, SCAN_LANES))

            def step(i, carry):
                gr, gi = carry
                j = L - 1 - i
                nr = ar * gr + ai * gi + gre[tile(j), ls]
                ni = ar * gi - ai * gr + gim[tile(j), ls]
                gre[tile(j), ls] = nr
                gim[tile(j), ls] = ni
                return nr, ni

            zero = jnp.zeros((NC, SCAN_LANES), f32)
            _steps(L, step, (zero, zero))
            pr, pi = lbr_ref[:, ls], -lbi_ref[:, ls]
            for _ in range(nsq):
                pr, pi = pr * pr - pi * pi, 2.0 * pr * pi
            er = jnp.zeros((1, SCAN_LANES), f32)
            ei = er
            ere[NC - 1:NC, ls] = er
            eim[NC - 1:NC, ls] = ei
            for c in range(NC - 2, -1, -1):
                lr_ = gre[c + 1:c + 2, ls]
                li_ = gim[c + 1:c + 2, ls]
                er, ei = lr_ + pr * er - pi * ei, li_ + pr * ei + pi * er
                ere[c:c + 1, ls] = er
                eim[c:c + 1, ls] = ei
            e_r, e_i = ere[:, ls].reshape(NC // 8, 8, SCAN_LANES), eim[:, ls].reshape(NC // 8, 8, SCAN_LANES)
            ar8, ai8 = ar[0:8], ai[0:8]

            def fixed(j, pwr, pwi):
                gr = (gre[tile(j), ls].reshape(NC // 8, 8, SCAN_LANES) + (pwr * e_r - pwi * e_i)).reshape(NC, SCAN_LANES)
                gi = (gim[tile(j), ls].reshape(NC // 8, 8, SCAN_LANES) + (pwr * e_i + pwi * e_r)).reshape(NC, SCAN_LANES)
                gre[tile(j), ls] = gr
                gim[tile(j), ls] = gi
                return gr, gi

            def fix(i, carry):
                pwr, pwi, accr, acci = carry
                j = L - 1 - i
                gr, gi = fixed(j, pwr, pwi)
                xr, xi = sre[tile(j - 1), ls].astype(f32), sim[tile(j - 1), ls].astype(f32)
                return (pwr * ar8 + pwi * ai8, pwi * ar8 - pwr * ai8,
                        accr + gr * xr + gi * xi, acci + gi * xr - gr * xi)

            pwr, pwi, accr, acci = _steps(L - 1, fix, (ar8, -ai8, zero, zero))
            gr, gi = fixed(0, pwr, pwi)
            xr = jnp.where(row == 0, 0.0, pltpu.roll(sre[tile(L - 1), ls].astype(f32), 1, axis=0))
            xi = jnp.where(row == 0, 0.0, pltpu.roll(sim[tile(L - 1), ls].astype(f32), 1, axis=0))
            accr = accr + gr * xr + gi * xi
            acci = acci + gi * xr - gr * xi
            dlr_ref[:, ls] = jnp.sum(accr, axis=0, keepdims=True)
            dli_ref[:, ls] = jnp.sum(acci, axis=0, keepdims=True)

        dwb_ref[...] = jnp.zeros_like(dwb_ref)
        dwc_ref[...] = jnp.zeros_like(dwc_ref)
        dd_ref[...] = jnp.zeros_like(dd_ref)
        su_ref[...] = jnp.zeros_like(su_ref)

        def finish(i, carry):
            u32, dy32 = u_ref[rows(i), :], dy_ref[rows(i), :]
            ub, dyb = u32.astype(bf16), dy32.astype(bf16)
            gr, gi = gre[rows(i), :].astype(bf16), gim[rows(i), :].astype(bf16)
            du = _dot(gr, wbT_ref[0:SW, :]) + _dot(gi, wbT_ref[SW:, :]) + dy32 * d_ref[...]
            du_ref[rows(i), :] = du
            su_ref[...] += jnp.sum(du, axis=0, keepdims=True)
            dwb_ref[:, 0:SW] += _dot_tn(ub, gr)
            dwb_ref[:, SW:] += _dot_tn(ub, gi)
            dwc_ref[:, 0:SW] += _dot_tn(dyb, sre[rows(i), :])
            dwc_ref[:, SW:] += _dot_tn(dyb, sim[rows(i), :])
            dd_ref[...] += jnp.sum(dy32 * u32, axis=0, keepdims=True)
            return carry

        lax.fori_loop(0, T // RB, finish, 0)
        _to_time_order(du_ref, dut_ref)

    slab = pl.BlockSpec((T, LANE), lambda k: (0, k))
    wide = pl.BlockSpec((None, LANE, 2 * SW), lambda k: (k, 0, 0))
    tall = pl.BlockSpec((None, 2 * SW, LANE), lambda k: (k, 0, 0))
    vec = pl.BlockSpec((None, 1, SW), lambda k: (k, 0, 0))
    vecd = pl.BlockSpec((None, 1, LANE), lambda k: (k, 0, 0))
    states = pl.BlockSpec((T, SW), lambda k: (0, k))
    nslab = W // LANE
    return _call(
        body, [u_p, dy, xr, xi, wbT, wcT, lbr, lbi, dsk], name="ssm_bwd", grid=(nslab,),
        in_specs=[slab, slab, states, states, tall, wide, vec, vec, vecd],
        out_specs=[slab, wide, wide, vec, vec, vecd, vecd],
        out_shape=[S((T, W), bf16), S((nslab, LANE, 2 * SW), f32), S((nslab, LANE, 2 * SW), f32),
                   S((nslab, 1, SW), f32), S((nslab, 1, SW), f32), S((nslab, 1, LANE), f32), S((nslab, 1, LANE), f32)],
        scratch=[pltpu.VMEM((T, SW), f32)] * 2 + [pltpu.VMEM((NC, SW), f32)] * 2 + [pltpu.VMEM((T, LANE), f32)] * 2,
        vmem=VMEM_LIMIT, plan=plan)


def _shift_rows(cur, prev8, k):
    return pltpu.roll(jnp.concatenate([prev8, cur], axis=0), k, axis=0)[8:]


def _lift_rows(cur, next8, k):
    n = cur.shape[0]
    return pltpu.roll(jnp.concatenate([cur, next8], axis=0), n + 8 - k, axis=0)[:n]


def conv_fwd(proj, conv_w):
    T = proj.shape[0]
    RB = min(512, T)

    def body(h_ref, c_ref, b_ref, w_ref, o_ref):
        w0, w1, w2 = w_ref[0:1, :], w_ref[1:2, :], w_ref[2:3, :]

        def blk(i, carry):
            r0 = pl.multiple_of(i * RB, RB)
            rs = pl.ds(r0, RB)
            ch = c_ref[rs, :] * h_ref[rs, :]
            pr = pl.ds(jnp.maximum(r0 - 8, 0), 8)
            prev = jnp.where(i > 0, c_ref[pr, :] * h_ref[pr, :], 0.0)
            z = w2 * ch + w1 * _shift_rows(ch, prev, 1) + w0 * _shift_rows(ch, prev, 2)
            o_ref[rs, :] = (b_ref[rs, :] * z).astype(bf16)
            return carry

        lax.fori_loop(0, T // RB, blk, 0)

    nb = W // LANE
    return pl.pallas_call(
        body, name="conv_fwd", grid=(nb,),
        in_specs=[pl.BlockSpec((T, LANE), lambda k: (0, 4 * nb + k)), pl.BlockSpec((T, LANE), lambda k: (0, 5 * nb + k)),
                  pl.BlockSpec((T, LANE), lambda k: (0, 6 * nb + k)),pl.BlockSpec((3, LANE), lambda k: (0, k))],
        out_specs=pl.BlockSpec((T, LANE), lambda k: (0, k)), out_shape=S((T, W), bf16),
        compiler_params=_cp(("parallel",), VMEM_LIMIT),
    )(proj, proj, proj, conv_w)


def _dense_columns(blocks_ref, dense_ref):
    for k in range(NDEV):
        dense_ref[:, k * LANE:(k + 1) * LANE] = blocks_ref[k]


def merge_fwd(yn, glu_w, glu_b, yb, wso, wco, proj, plan):
    T = yn.shape[0]
    tm = min(1024, T)

    def body(y_ref, gw_ref, gbias_ref, yb_ref, wa_ref, wb_ref, ga_ref, gb_ref, o_ref, ya_ref, wa_s, wb_s):
        @pl.when(pl.program_id(0) == 0)
        def _():
            _dense_columns(wa_ref, wa_s)
            _dense_columns(wb_ref, wb_s)

        for rs in _row_parts(tm):
            g = _gelu(y_ref[rs, :])
            ya = (g * _sigmoid(_dot(g.astype(bf16), gw_ref[...]) + gbias_ref[...])).astype(bf16)
            ya_ref[rs, :] = ya
            o_ref[rs, :] = (_sigmoid(ga_ref[rs, :]) * _dot(ya, wa_s[...])
                            + _sigmoid(gb_ref[rs, :]) * _dot(yb_ref[rs, :], wb_s[...])).astype(bf16)

    act = pl.BlockSpec((tm, W), lambda i: (i, 0))
    return _call(
        body, [yn, glu_w, glu_b, yb, wso, wco, proj, proj], name="merge_fwd", grid=(T // tm,),
        in_specs=[act, pl.BlockSpec((W, W), lambda i: (0, 0)), pl.BlockSpec((1, W), lambda i: (0, 0)), act,
                  _resident((NDEV, W, LANE)), _resident((NDEV, W, LANE)),
                  pl.BlockSpec((tm, D), lambda i: (i, 0)), pl.BlockSpec((tm, D), lambda i: (i, 1))],
        out_specs=[pl.BlockSpec((tm, D), lambda i: (i, 0)), act], out_shape=[S((T, D), bf16), S((T, W), bf16)],
        scratch=[pltpu.VMEM((W, D), bf16), pltpu.VMEM((W, D), bf16)], vmem=VMEM_LIMIT, plan=plan)


def mix_ln1(merged, w_o, x, g1, b1, plan):
    T = x.shape[0]
    tm = min(512, T)

    def body(m_ref, w_ref, x_ref, g_ref, b_ref, r_ref, x1_ref):
        for rs in _row_parts(tm):
            r = ALPHA * x_ref[rs, :] + _dot(m_ref[rs, :], w_ref[...])
            r_ref[rs, :] = r
            xhat, _ = _ln_stats(r)
            x1_ref[rs, :] = (xhat * g_ref[...] + b_ref[...]).astype(bf16)

    row = pl.BlockSpec((tm, D), lambda i: (i, 0))
    vec = pl.BlockSpec((1, D), lambda i: (0, 0))
    return _call(
        body, [merged, w_o, x, g1, b1], name="mix_ln1", grid=(T // tm,),
        in_specs=[row, _resident((D, D)), row, vec, vec],
        out_specs=[row, row], out_shape=[S((T, D), f32), S((T, D), bf16)], sem=("parallel",), vmem=VMEM_LIMIT, plan=plan,
        relay_after=True)


FT = 256


def gate_up(x1b, wgT, wuT, plan):
    T = x1b.shape[0]
    tm = min(512, T)

    def body(x_ref, wg_ref, wu_ref, g_ref, u_ref, h_ref):
        x = x_ref[...]
        for n in range(F // FT):
            cs = slice(n * FT, (n + 1) * FT)
            g = _dot_nt(x, wg_ref[cs, :])
            u = _dot_nt(x, wu_ref[cs, :])
            g_ref[:, cs] = g.astype(bf16)
            u_ref[:, cs] = u.astype(bf16)
            h_ref[:, cs] = (g * _sigmoid(g) * u).astype(bf16)

    osp = pl.BlockSpec((tm, F), lambda i: (i, 0))
    return _call(
        body, [x1b, wgT, wuT], name="gate_up", grid=(T // tm,),
        in_specs=[pl.BlockSpec((tm, D), lambda i: (i, 0)), _resident((F, D)), _resident((F, D))],
        out_specs=[osp, osp, osp], out_shape=[S((T, F), bf16)] * 3, vmem=VMEM_LIMIT, plan=plan, relay_step=T // tm - 3)


def down_loss(hid, w_down, r1, g1, b1, g2, b2, target):
    T = hid.shape[0]
    tm = min(512, T)

    def body(h_ref, w_ref, r1_ref, g1_ref, b1_ref, g2_ref, b2_ref, t_ref, dr_ref, drb_ref, loss_ref, dg_ref, db_ref):
        @pl.when(pl.program_id(0) == 0)
        def _():
            loss_ref[...] = jnp.zeros_like(loss_ref)
            dg_ref[...] = jnp.zeros_like(dg_ref)
            db_ref[...] = jnp.zeros_like(db_ref)

        for rs in _row_parts(tm):
            xh1, _ = _ln_stats(r1_ref[rs, :])
            x1 = xh1 * g1_ref[...] + b1_ref[...]
            r2 = ALPHA * x1 + _dot(h_ref[rs, :], w_ref[...])
            xh2, rstd2 = _ln_stats(r2)
            err = xh2 * g2_ref[...] + b2_ref[...] - t_ref[rs, :]
            loss_ref[...] += jnp.sum(jnp.mean(err * err, axis=-1, keepdims=True), axis=0, keepdims=True)
            dy = err * (1.0 / D)
            dg_ref[...] += jnp.sum(dy * xh2, axis=0, keepdims=True)
            db_ref[...] += jnp.sum(dy, axis=0, keepdims=True)
            dr = _ln_bwd(dy, xh2, rstd2, g2_ref[...])
            dr_ref[rs, :] = dr
            drb_ref[rs, :] = dr.astype(bf16)

    row = pl.BlockSpec((tm, D), lambda i: (i, 0))
    vec = pl.BlockSpec((1, D), lambda i: (0, 0))
    return pl.pallas_call(
        body, name="down_loss", grid=(T // tm,),
        in_specs=[pl.BlockSpec((tm, F), lambda i: (i, 0)), _resident((F, D)), row, vec, vec, vec, vec, row],
        out_specs=[row, row, pl.BlockSpec((1, 1), lambda i: (0, 0)), vec, vec],
        out_shape=[S((T, D), f32), S((T, D), bf16), S((1, 1), f32), S((1, D), f32), S((1, D), f32)],
        compiler_params=_cp(("arbitrary",), VMEM_LIMIT),
    )(hid, w_down, r1, g1, b1, g2, b2, target)


def ffn_bwd_act(dffn, w_down, gate, up, plan):
    T = dffn.shape[0]
    tm = min(512, T)

    def body(d_ref, w_ref, g_ref, u_ref, dg_ref, du_ref):
        for n in range(F // FT):
            cs = slice(n * FT, (n + 1) * FT)
            for rs in _row_parts(tm):
                dh = _dot_nt(d_ref[rs, :], w_ref[cs, :])
                g, u = g_ref[rs, cs].astype(f32), u_ref[rs, cs].astype(f32)
                sg = _sigmoid(g)
                t = g * sg
                du_ref[rs, cs] = (dh * t).astype(bf16)
                dg_ref[rs, cs] = (dh * u * (sg + t - t * sg)).astype(bf16)

    osp = pl.BlockSpec((tm, F), lambda i: (i, 0))
    return _call(
        body, [dffn, w_down, gate, up], name="ffn_bwd_act", grid=(T // tm,),
        in_specs=[pl.BlockSpec((tm, D), lambda i: (i, 0)), _resident((F, D)), osp, osp],
        out_specs=[osp, osp], out_shape=[S((T, F), bf16)] * 2, sem=("parallel",), vmem=VMEM_LIMIT, plan=plan)


def ffn_bwd_x(dgate, dup, wgT, wuT, dr2, r1, g1, plan):
    T = dr2.shape[0]
    tm = min(512, T)

    def body(dg_ref, du_ref, wg_ref, wu_ref, dr2_ref, r1_ref, g1_ref, dr_ref, drb_ref, dgam_ref, dbet_ref):
        @pl.when(pl.program_id(0) == 0)
        def _():
            dgam_ref[...] = jnp.zeros_like(dgam_ref)
            dbet_ref[...] = jnp.zeros_like(dbet_ref)

        for rs in _row_parts(tm):
            dx1 = ALPHA * dr2_ref[rs, :] + _dot(dg_ref[rs, :], wg_ref[...]) + _dot(du_ref[rs, :], wu_ref[...])
            xh, rstd = _ln_stats(r1_ref[rs, :])
            dgam_ref[...] += jnp.sum(dx1 * xh, axis=0, keepdims=True)
            dbet_ref[...] += jnp.sum(dx1, axis=0, keepdims=True)
            dr = _ln_bwd(dx1, xh, rstd, g1_ref[...])
            dr_ref[rs, :] = dr
            drb_ref[rs, :] = dr.astype(bf16)

    row = pl.BlockSpec((tm, D), lambda i: (i, 0))
    wide = pl.BlockSpec((tm, F), lambda i: (i, 0))
    wsp = _resident((F, D))
    vec = pl.BlockSpec((1, D), lambda i: (0, 0))
    return _call(
        body, [dgate, dup, wgT, wuT, dr2, r1, g1], name="ffn_bwd_x", grid=(T // tm,),
        in_specs=[wide, wide, wsp, wsp, row, row, vec],
        out_specs=[row, row, vec, vec], out_shape=[S((T, D), f32), S((T, D), bf16), S((1, D), f32), S((1, D), f32)],
        vmem=VMEM_LIMIT, plan=plan)


def merge_bwd(dmix, w_o, merged, ya, yb, wso, wco, proj, plan):
    T = dmix.shape[0]
    tm = min(512, T)

    def body(dm_ref, wo_ref, m_ref, ya_ref, yb_ref, wa_ref, wb_ref, ga_ref, gb_ref,
             dya_ref, dyb_ref, dga_ref, dgb_ref, sa_ref, sb_ref, dwo_ref, wa_s, wb_s, acc):
        @pl.when(pl.program_id(0) == 0)
        def _():
            _dense_columns(wa_ref, wa_s)
            _dense_columns(wb_ref, wb_s)
            acc[...] = jnp.zeros_like(acc)

        acc[...] += _dot_tn(m_ref[...], dm_ref[...])

        @pl.when(pl.program_id(0) == pl.num_programs(0) - 1)
        def _():
            dwo_ref[...] = acc[...].astype(GRAD_DT)

        dmer = _dot_nt(dm_ref[...], wo_ref[...])
        sa, sb = _sigmoid(ga_ref[...]), _sigmoid(gb_ref[...])
        dya_ref[...] = (dmer * sa).astype(bf16)
        dyb_ref[...] = (dmer * sb).astype(bf16)
        dga = dmer * _dot(ya_ref[...], wa_s[...]) * sa * (1.0 - sa)
        dgb = dmer * _dot(yb_ref[...], wb_s[...]) * sb * (1.0 - sb)
        dga_ref[...] = dga.astype(bf16)
        dgb_ref[...] = dgb.astype(bf16)
        sa_ref[...] = jnp.sum(dga, axis=0, keepdims=True)
        sb_ref[...] = jnp.sum(dgb, axis=0, keepdims=True)

    act = pl.BlockSpec((tm, W), lambda i: (i, 0))
    osp = pl.BlockSpec((tm, D), lambda i: (i, 0))
    ssp = pl.BlockSpec((None, 1, D), lambda i: (i, 0, 0))
    return _call(
        body, [dmix, w_o, merged, ya, yb, wso, wco, proj, proj], name="merge_bwd", grid=(T // tm,),
        in_specs=[osp, _resident((D, D)), osp, act, act, _resident((NDEV, W, LANE)), _resident((NDEV, W, LANE)),
                  pl.BlockSpec((tm, D), lambda i: (i, 0)), pl.BlockSpec((tm, D), lambda i: (i, 1))],
        out_specs=[osp, osp, osp, osp, ssp, ssp, pl.BlockSpec((D, D), lambda i: (0, 0))],
        out_shape=[S((T, D), bf16)] * 4 + [S((T // tm, 1, D), f32)] * 2 + [S((D, D), GRAD_DT)],
        scratch=[pltpu.VMEM((W, D), bf16), pltpu.VMEM((W, D), bf16), pltpu.VMEM((D, D), f32)], vmem=VMEM_LIMIT, plan=plan)


def branches_bwd(dYA, dYB, ya, yb, wso, wco):
    T = dYA.shape[0]
    tm = min(1024, T)

    def body(da_ref, db_ref, ya_ref, yb_ref, wa_ref, wb_ref, oa_ref, ob_ref, ga_ref, gb_ref, wa_s, wb_s, acc_a, acc_b):
        @pl.when(pl.program_id(0) == 0)
        def _():
            _dense_columns(wa_ref, wa_s)
            _dense_columns(wb_ref, wb_s)
            acc_a[...] = jnp.zeros_like(acc_a)
            acc_b[...] = jnp.zeros_like(acc_b)

        oa_ref[...] = _dot_nt(da_ref[...], wa_s[...])
        ob_ref[...] = _dot_nt(db_ref[...], wb_s[...])
        acc_a[...] += _dot_tn(ya_ref[...], da_ref[...])
        acc_b[...] += _dot_tn(yb_ref[...], db_ref[...])

        @pl.when(pl.program_id(0) == pl.num_programs(0) - 1)
        def _():
            for k in range(NDEV):
                ga_ref[k] = acc_a[:, k * LANE:(k + 1) * LANE].astype(GRAD_DT)
                gb_ref[k] = acc_b[:, k * LANE:(k + 1) * LANE].astype(GRAD_DT)

    row = pl.BlockSpec((tm, D), lambda i: (i, 0))
    osp = pl.BlockSpec((tm, W), lambda i: (i, 0))
    blocks = pl.BlockSpec((NDEV, W, LANE), lambda i: (0, 0, 0))
    outs, _ = _call(
        body, [dYA, dYB, ya, yb, wso, wco], name="branches_bwd", grid=(T // tm,),
        in_specs=[row, row, osp, osp, _resident((NDEV, W, LANE)), _resident((NDEV, W, LANE))],
        out_specs=[osp, osp, blocks, blocks], out_shape=[S((T, W), f32)] * 2 + [S((NDEV, W, LANE), GRAD_DT)] * 2,
        scratch=[pltpu.VMEM((W, D), bf16)] * 2 + [pltpu.VMEM((W, D), f32)] * 2, sem=("arbitrary",), vmem=VMEM_LIMIT)
    return outs


def glu_bwd(yn, dya, glu_w, glu_b, plan):
    T = yn.shape[0]
    tm = min(512, T)

    def body(y_ref, d_ref, w_ref, b_ref, dy_ref, db_ref, dw_ref, acc):
        @pl.when(pl.program_id(0) == 0)
        def _():
            db_ref[...] = jnp.zeros_like(db_ref)
            acc[...] = jnp.zeros_like(acc)

        y, dya_ = y_ref[...], d_ref[...]
        g = _gelu(y)
        gb = g.astype(bf16)
        s = _sigmoid(_dot(gb, w_ref[...]) + b_ref[...])
        dsp = dya_ * g * s * (1.0 - s)
        dspb = dsp.astype(bf16)
        dg = dya_ * s + _dot_nt(dspb, w_ref[...])
        dy_ref[...] = dg * _gelu_grad(y)
        db_ref[...] += jnp.sum(dsp, axis=0, keepdims=True)
        acc[...] += _dot_tn(gb, dspb)

        @pl.when(pl.program_id(0) == pl.num_programs(0) - 1)
        def _():
            dw_ref[...] = acc[...].astype(GRAD_DT)

    row = pl.BlockSpec((tm, W), lambda i: (i, 0))
    vec = pl.BlockSpec((1, W), lambda i: (0, 0))
    mat = pl.BlockSpec((W, W), lambda i: (0, 0))
    return _call(
        body, [yn, dya, glu_w, glu_b], name="glu_bwd", grid=(T // tm,),
        in_specs=[row, row, mat, vec],
        out_specs=[row, vec, mat], out_shape=[S((T, W), f32), S((1, W), f32), S((W, W), GRAD_DT)],
        scratch=[pltpu.VMEM((W, W), f32)], sem=("arbitrary",), plan=plan)


def conv_bwd(proj, dyb, conv_w, plan):
    T = proj.shape[0]
    RB = min(512, T)
    nrb = T // RB

    def body(h_ref, c_ref, b_ref, d_ref, w_ref, dh_ref, dc_ref, db_ref, dw_ref, s_ref):
        w0, w1, w2 = w_ref[0:1, :], w_ref[1:2, :], w_ref[2:3, :]

        def blk(i, carry):
            a0, a1, a2, sh, sc, sb = carry
            r0 = pl.multiple_of(i * RB, RB)
            rs = pl.ds(r0, RB)
            h, cg, bg, dyb_ = h_ref[rs, :], c_ref[rs, :], b_ref[rs, :], d_ref[rs, :]
            ch = cg * h
            pr = pl.ds(jnp.maximum(r0 - 8, 0), 8)
            prev = jnp.where(i > 0, c_ref[pr, :] * h_ref[pr, :], 0.0)
            ch1, ch2 = _shift_rows(ch, prev, 1), _shift_rows(ch, prev, 2)
            dbg = dyb_ * (w2 * ch + w1 * ch1 + w0 * ch2)
            db_ref[rs, :] = dbg.astype(bf16)
            dz = dyb_ * bg
            nx = pl.ds(jnp.minimum(r0 + RB, T - 8), 8)
            nxt = jnp.where(i < nrb - 1, d_ref[nx, :] * b_ref[nx, :], 0.0)
            dch = w2 * dz + w1 * _lift_rows(dz, nxt, 1) + w0 * _lift_rows(dz, nxt, 2)
            dcg, dh = dch * h, dch * cg
            dc_ref[rs, :] = dcg.astype(bf16)
            dh_ref[rs, :] = dh.astype(bf16)
            col = lambda v: jnp.sum(v, axis=0, keepdims=True)
            return (a0 + col(dz * ch2), a1 + col(dz * ch1), a2 + col(dz * ch), sh + col(dh), sc + col(dcg), sb + col(dbg))

        zero = jnp.zeros((1, LANE), f32)
        a0, a1, a2, sh, sc, sb = lax.fori_loop(0, nrb, blk, (zero,) * 6)
        dw_ref[0:1, :] = a0
        dw_ref[1:2, :] = a1
        dw_ref[2:3, :] = a2
        s_ref[0:1, :] = sh
        s_ref[1:2, :] = sc
        s_ref[2:3, :] = sb

    nb = W // LANE
    slab = pl.BlockSpec((T, LANE), lambda k: (0, k))
    three = pl.BlockSpec((3, LANE), lambda k: (0, k))
    return _call(
        body, [proj, proj, proj, dyb, conv_w], name="conv_bwd", grid=(nb,),
        in_specs=[pl.BlockSpec((T, LANE), lambda k: (0, 4 * nb + k)), pl.BlockSpec((T, LANE), lambda k: (0, 5 * nb + k)),
                  pl.BlockSpec((T, LANE), lambda k: (0, 6 * nb + k)), slab, three],
        out_specs=[slab, slab, slab, three, three],
        out_shape=[S((T, W), bf16)] * 3 + [S((3, W), f32)] * 2, sem=("parallel",), vmem=VMEM_LIMIT, plan=plan)


def in_proj_bwd_x(parts, win_g, base, scale, name, plan=None):
    T = base.shape[0]
    tm = min(512, T)
    n = len(parts)

    def body(*refs):
        p_refs, w_ref, b_ref, o_ref = refs[:n], refs[n], refs[n + 1], refs[n + 2]
        acc = scale * b_ref[...]
        for p_ref, (_, _, k) in zip(p_refs, parts):
            acc += _dot_nt(p_ref[...], w_ref[k])
        o_ref[...] = acc

    row = pl.BlockSpec((tm, D), lambda i: (i, 0))
    p_specs = [pl.BlockSpec((tm, W), (lambda i, cb=cb: (i, cb))) for _, cb, _ in parts]
    return _call(
        body, [a for a, _, _ in parts] + [win_g, base], name=name, grid=(T // tm,),
        in_specs=p_specs + [_resident((NDEV, D, W)), row],
        out_specs=[row], out_shape=[S((T, D), f32)], vmem=VMEM_LIMIT, plan=plan)


def ssm_param_bwd(lam_re, lam_im, log_dt, fr, fi, br, bi, dwb, dwcT, dlbr, dlbi):
    def body(lr_ref, li_ref, ldt_ref, fr_ref, fi_ref, br_ref, bi_ref, dwb_ref, dwc_ref, dlbr_ref, dlbi_ref,
             dbr_ref, dbi_ref, dlr_ref, dli_ref, dldt_ref, dcr_ref, dci_ref, dr_s, di_s):
        for k in range(W // LANE):
            for gl in range(NG // (W // LANE)):
                rows, src = slice((8 * k + gl) * GC, (8 * k + gl + 1) * GC), slice(gl * GC, (gl + 1) * GC)
                re, im = slice(gl * NP, (gl + 1) * NP), slice(SW + gl * NP, SW + (gl + 1) * NP)
                dr_s[rows, :] = dwb_ref[k, src, re]
                di_s[rows, :] = dwb_ref[k, src, im]
                dcr_ref[rows, :] = dwc_ref[k, src, re]
                dci_ref[rows, :] = -dwc_ref[k, src, im]
        fr_, fi_ = _per_channel(fr_ref[...]), _per_channel(fi_ref[...])
        br_, bi_, dr, di = br_ref[...], bi_ref[...], dr_s[...], di_s[...]
        dbr_ref[...] = fr_ * dr + fi_ * di
        dbi_ref[...] = fr_ * di - fi_ * dr
        dfr = jnp.sum((dr * br_ + di * bi_).reshape(NG, GC, NP), axis=1)
        dfi = jnp.sum((di * br_ - dr * bi_).reshape(NG, GC, NP), axis=1)
        _, vjp = jax.vjp(_disc, lr_ref[...], li_ref[...], ldt_ref[...])
        dlr_ref[...], dli_ref[...], dldt = vjp((dlbr_ref[...], dlbi_ref[...], dfr, dfi))
        dldt_ref[...] = _transpose_exact(dldt)

    blk = S((NG * GC, NP), f32)
    return pl.pallas_call(
        body, name="ssm_param_bwd", out_shape=[blk, blk, S((NG, NP), f32), S((NG, NP), f32), S((1, NG), f32), blk, blk],
        scratch_shapes=[pltpu.VMEM((NG * GC, NP), f32)] * 2)(
        lam_re, lam_im, log_dt, fr, fi, br, bi, dwb, dwcT, dlbr, dlbi)


def _adam(w, g, m, v):
    m = ADAM_B1 * m + (1.0 - ADAM_B1) * g
    v = ADAM_B2 * v + (1.0 - ADAM_B2) * (g * g)
    m_hat = m / (1.0 - ADAM_B1 ** ADAM_STEP)
    v_hat = v / (1.0 - ADAM_B2 ** ADAM_STEP)
    return -ADAM_LR * (m_hat / (jnp.sqrt(v_hat) + ADAM_EPS) + ADAM_WD * w), m, v


def _sum_in_order(c_ref):
    g = c_ref[0].astype(f32)
    for k in range(1, c_ref.shape[0]):
        g = g + c_ref[k].astype(f32)
    return g


def sum_blocks(contrib, name):
    def body(c_ref, o_ref):
        o_ref[...] = _sum_in_order(c_ref)

    return pl.pallas_call(body, name=name, out_shape=S(contrib.shape[1:], f32))(contrib)


def adam_update(w, m, v, contrib, name, rows_per_block=None, summed_on_0=None, plan=None):
    R, C = w.shape
    n = contrib.shape[0]
    tr = min(rows_per_block or R, R)

    def body(w_ref, m_ref, v_ref, c_ref, *refs):
        g_ref, d_ref, nm_ref, nv_ref = refs[-4:]
        g = _sum_in_order(c_ref)
        if summed_on_0 is not None:
            x, y, c = _coords()
            g = jnp.where(4 * x + 2 * y + c == 0, refs[0][...], g)
        g_ref[...] = g
        d_ref[...], nm_ref[...], nv_ref[...] = _adam(w_ref[...], g, m_ref[...], v_ref[...])

    blk = pl.BlockSpec((tr, C), lambda i: (i, 0))
    extra = [] if summed_on_0 is None else [summed_on_0]
    return _call(
        body, [w, m, v, contrib] + extra, name=name, grid=(R // tr,),
        in_specs=[blk, blk, blk, pl.BlockSpec((n, tr, C), lambda i: (0, i, 0))] + [blk] * len(extra),
        out_specs=[blk] * 4, out_shape=[S((R, C), f32)] * 4, sem=("parallel",), vmem=VMEM_LIMIT, plan=plan)


_ROWVEC = (("b_in", IN_COLS), ("ssm_d", W), ("glu_b", W), ("ln1_g", D), ("ln1_b", D), ("ln2_g", D), ("ln2_b", D))
_HALF = NG * GC // 2
_BC_LANE = {"ssm_b_re": 0, "ssm_b_im": NP, "ssm_c_re": 0, "ssm_c_im": NP}
_PACK = {}
_r = 0
for _n, _k in _ROWVEC:
    _PACK[_n] = _r
    _r += _k // LANE
for _n, _rows in (("ssm_lambda", NG), ("scalars", 8), ("ssm_b", _HALF), ("ssm_c", _HALF), ("conv_w", 16)):
    _PACK[_n] = _r
    _r += _rows
for _n in _BC_LANE:
    _PACK[_n] = _PACK[_n[:5]]
PACK_ROWS = _r
assert PACK_ROWS % 8 == 0
_SMALL = ("b_in", "ssm_lambda_re", "ssm_lambda_im", "ssm_log_dt", "ssm_b_re", "ssm_b_im", "ssm_c_re", "ssm_c_im",
          "ssm_d", "glu_b", "ln1_g", "ln1_b", "ln2_g", "ln2_b")


def pack_grads(su, shcb, sga, sgb, dd, dglu_b, dln1_g, dln1_b, dln2_g, dln2_b, dlam_re, dlam_im, dldt, sqerr, dbr, dbi,
               dc_re, dc_im, dconv):
    nI = sga.shape[0]

    def body(su_ref, sh_ref, sga_ref, sgb_ref, dd_ref, gb_ref, l1g_ref, l1b_ref, l2g_ref, l2b_ref, lr_ref, li_ref, dt_ref,
             sq_ref, br_ref, bi_ref, cr_ref, ci_ref, cw_ref, o_ref):
        o_ref[...] = jnp.zeros_like(o_ref)

        def put_row(name, v):
            r0 = _PACK[name]
            for i in range(v.shape[1] // LANE):
                o_ref[r0 + i:r0 + i + 1, :] = v[:, i * LANE:(i + 1) * LANE]

        ga, gb = sga_ref[0], sgb_ref[0]
        for i in range(1, nI):
            ga, gb = ga + sga_ref[i], gb + sgb_ref[i]
        put_row("b_in", jnp.concatenate([su_ref[k] for k in range(W // LANE)]
                                        + [sh_ref[0:1, :], sh_ref[1:2, :], sh_ref[2:3, :], ga, gb], axis=1))
        put_row("ssm_d", jnp.concatenate([dd_ref[k] for k in range(W // LANE)], axis=1))
        put_row("glu_b", gb_ref[...])
        put_row("ln1_g", l1g_ref[...])
        put_row("ln1_b", l1b_ref[...])
        put_row("ln2_g", l2g_ref[...])
        put_row("ln2_b", l2b_ref[...])
        r0 = _PACK["ssm_lambda"]
        o_ref[r0:r0 + NG, 0:NP] = lr_ref[...]
        o_ref[r0:r0 + NG, NP:2 * NP] = li_ref[...]
        r0 = _PACK["scalars"]
        o_ref[r0:r0 + 1, 0:NG] = dt_ref[...]
        o_ref[r0 + 1:r0 + 2, 0:1] = sq_ref[...]
        for name, ref in (("ssm_b_re", br_ref), ("ssm_b_im", bi_ref), ("ssm_c_re", cr_ref), ("ssm_c_im", ci_ref)):
            r0, l0 = _PACK[name], _BC_LANE[name]
            o_ref[r0:r0 + _HALF, l0:l0 + NP] = pltpu.bitcast(ref[...].astype(bf16), f32)
        for cb in range(W // LANE):
            o_ref[_PACK["conv_w"] + 3 * cb:_PACK["conv_w"] + 3 * cb + 3, :] = cw_ref[:, cb * LANE:(cb + 1) * LANE]

    return pl.pallas_call(body, name="pack_grads", out_shape=S((PACK_ROWS, LANE), f32))(
        su, shcb, sga, sgb, dd, dglu_b, dln1_g, dln1_b, dln2_g, dln2_b, dlam_re, dlam_im, dldt, sqerr, dbr, dbi, dc_re, dc_im,
        dconv)


def adam_small(packed_all, params):
    names = list(_SMALL) + ["conv_w"]
    flat = [a for n in names for a in params[n]]

    def body(*refs):
        p_ref = refs[0]
        ins = refs[1:1 + 3 * len(names)]
        outs = refs[1 + 3 * len(names):-2]
        loss_ref, g_ref = refs[-2], refs[-1]

        def part(k, rs=slice(None), ls=slice(None)):
            return p_ref[k, rs, ls]

        g_all = part(0)
        for k in range(1, NDEV):
            g_all = g_all + part(k)
        g_ref[...] = g_all

        def rows(name, r0, n, l0=0, lanes=LANE):
            return g_ref[_PACK[name] + r0:_PACK[name] + r0 + n, l0:l0 + lanes]

        def grad_of(name):
            if name in dict(_ROWVEC):
                return jnp.concatenate([rows(name, i, 1) for i in range(dict(_ROWVEC)[name] // LANE)], axis=1)
            if name in ("ssm_lambda_re", "ssm_lambda_im"):
                return rows("ssm_lambda", 0, NG, NP * (name == "ssm_lambda_im"), NP)[None]
            if name == "ssm_log_dt":
                return rows("scalars", 0, 1, 0, NG)
            if name in _BC_LANE:
                rs, ls = slice(_PACK[name], _PACK[name] + _HALF), slice(_BC_LANE[name], _BC_LANE[name] + NP)
                g = pltpu.bitcast(part(0, rs, ls), bf16).astype(f32)
                for k in range(1, NDEV):
                    g = g + pltpu.bitcast(part(k, rs, ls), bf16).astype(f32)
                return g.reshape(1, NG, GC, NP)
            full = jnp.concatenate([rows("conv_w", 3 * cb, 3) for cb in range(W // LANE)], axis=1)
            x, y, c = _coords()
            col0 = (4 * x + 2 * y + c) * (W // NDEV)
            sel = (lax.broadcasted_iota(jnp.int32, (W, W // NDEV), 0)
                   == lax.broadcasted_iota(jnp.int32, (W, W // NDEV), 1) + col0).astype(f32)
            return jnp.dot(full, sel, precision=HIGHEST, preferred_element_type=f32)[None]

        loss_ref[...] = 0.5 * rows("scalars", 1, 1, 0, 1)
        for i, name in enumerate(names):
            w_ref, m_ref, v_ref = ins[3 * i:3 * i + 3]
            g = grad_of(name)
            d, m, v = _adam(w_ref[...], g, m_ref[...], v_ref[...])
            outs[4 * i][...] = g
            outs[4 * i + 1][...] = d
            outs[4 * i + 2][...] = m
            outs[4 * i + 3][...] = v

    out_shape = [S(params[n][0].shape, f32) for n in names for _ in range(4)] + [S((1, 1), f32)]
    res = pl.pallas_call(body, name="adam_small", out_shape=out_shape, scratch_shapes=[pltpu.VMEM((PACK_ROWS, LANE), f32)],
                         compiler_params=_cp(None, VMEM_LIMIT))(packed_all, *flat)
    return {n: res[4 * i:4 * i + 4] for i, n in enumerate(names)}, res[-1]


def _block_diag(wgt):
    eye = jnp.eye(8, dtype=wgt.dtype)
    out = wgt[:, :, :, None, :] * eye[None, :, None, :, None]
    return out.reshape(4, 8 * wgt.shape[2], 8 * wgt.shape[3])


def kernel(x, w_in, b_in, ssm_lambda_re, ssm_lambda_im, ssm_log_dt, ssm_b_re, ssm_b_im, ssm_c_re, ssm_c_im, ssm_d, glu_w, glu_b, w_ssm_out, conv_w, w_conv_out, w_o, ln1_g, ln1_b, w_gate, w_up, w_down, ln2_g, ln2_b, loss_target, m_w_in, m_b_in, m_ssm_lambda_re, m_ssm_lambda_im, m_ssm_log_dt, m_ssm_b_re, m_ssm_b_im, m_ssm_c_re, m_ssm_c_im, m_ssm_d, m_glu_w, m_glu_b, m_w_ssm_out, m_conv_w, m_w_conv_out, m_w_o, m_ln1_g, m_ln1_b, m_w_gate, m_w_up, m_w_down, m_ln2_g, m_ln2_b, v_w_in, v_b_in, v_ssm_lambda_re, v_ssm_lambda_im, v_ssm_log_dt, v_ssm_b_re, v_ssm_b_im, v_ssm_c_re, v_ssm_c_im, v_ssm_d, v_glu_w, v_glu_b, v_w_ssm_out, v_conv_w, v_w_conv_out, v_w_o, v_ln1_g, v_ln1_b, v_w_gate, v_w_up, v_w_down, v_ln2_g, v_ln2_b):
    given = dict(locals())
    xs = x[0]
    target = loss_target[0]

    tr = lambda a: jnp.swapaxes(a[0], 0, 1)
    (win_s,), _ = prep_weights([w_in[0]], "prep_w_in")
    (glu_s, wso_s, wco_s, wo_s, wgT_s, wuT_s, wd_s), (win_g,) = prep_weights(
        [glu_w[0], w_ssm_out[0], w_conv_out[0], w_o[0], tr(w_gate), tr(w_up), w_down[0]], "prep_weights",
        GatherPlan([win_s], srcs=(0,)))

    lam_re, lam_im = ssm_lambda_re[0], ssm_lambda_im[0]
    ldt = ssm_log_dt[0].reshape(NG, 1)
    br2 = jnp.swapaxes(ssm_b_re[0], 1, 2).reshape(NG * GC, NP)
    bi2 = jnp.swapaxes(ssm_b_im[0], 1, 2).reshape(NG * GC, NP)
    lbr, lbi, fr, fi, bbr, bbi = ssm_params(lam_re, lam_im, ldt, br2, bi2)
    bb_t = lambda b: b.reshape(4, 8, GC, NP)
    wb = jnp.concatenate([_block_diag(bb_t(bbr)), _block_diag(bb_t(bbi))], axis=2)
    c_t = lambda c: c.reshape(4, 8, GC, NP).transpose(0, 1, 3, 2)
    wc = jnp.concatenate([_block_diag(c_t(ssm_c_re[0])), -_block_diag(c_t(ssm_c_im[0]))], axis=1)
    wbT, wcT = wb.transpose(0, 2, 1), wc.transpose(0, 2, 1)
    wb, wc, wbT, wcT = wb.astype(bf16), wc.astype(bf16), wbT.astype(bf16), wcT.astype(bf16)
    lbr_s, lbi_s = lbr.reshape(4, 1, SW), lbi.reshape(4, 1, SW)
    dsk = ssm_d[0].reshape(4, 1, LANE)

    u_nat, xb = in_proj_u(xs, win_g, b_in)
    half_a, half_b = (0, 3, 5, 6), (1, 2, 4, 7)
    (yn, u_p, xr_p, xi_p), (win_g, conv_g, glu_g, wso_g, wuT_g) = ssm_fwd(
        u_nat, wb, wc, lbr_s, lbi_s, dsk,
        Plans([GatherPlan([win_s], srcs=tuple(range(1, NDEV)), into=[win_g]), GatherPlan([conv_w[0], glu_s, wso_s]),
               GatherPlan([wuT_s], srcs=half_a)]))
    conv_f = conv_g.transpose(1, 0, 2).reshape(3, W)
    (proj,), (wco_g, wo_g, wgT_g) = in_proj_rest(
        xb, win_g, b_in, Plans([GatherPlan([wco_s, wo_s]), GatherPlan([wgT_s], srcs=half_a)]))
    glu_f, wo_f = glu_g.reshape(W, W), wo_g.reshape(D, D)
    yb = conv_fwd(proj, conv_f)
    (merged, ya), (wgT_g,) = merge_fwd(yn, glu_f, glu_b, yb, wso_g, wco_g, proj,
                                       GatherPlan([wgT_s], srcs=half_b, into=[wgT_g]))
    (r1, x1b), (wuT_g,) = mix_ln1(merged, wo_f, xs, ln1_g, ln1_b, GatherPlan([wuT_s], srcs=half_b, into=[wuT_g]))
    wgT, wuT = wgT_g.reshape(F, D), wuT_g.reshape(F, D)
    (gate, up, hid), (wd_g,) = gate_up(x1b, wgT, wuT, GatherPlan([wd_s]))
    wd_f = wd_g.reshape(F, D)
    dr2, dffn, sqerr, dln2_g, dln2_b = down_loss(hid, wd_f, r1, ln1_g, ln1_b, ln2_g, ln2_b, target)

    dwd, _ = mm_tn_rows(hid, dffn, "grad_w_down")
    dwd = dwd.reshape(NDEV, FS, D)
    (dgate, dup), (r_wd,) = ffn_bwd_act(dffn, wd_f, gate, up, ScatterPlan([dwd], only=half_a))
    dwgT, (r_wd,) = mm_tn_rows(dgate, x1b, "grad_w_gate", plan=ScatterPlan([dwd], only=half_b, into=[r_wd]))
    dwgT = dwgT.reshape(NDEV, FS, D)
    dwuT, (r_wgT,) = mm_tn_rows(dup, x1b, "grad_w_up", plan=ScatterPlan([dwgT], only=half_a))
    dwuT = dwuT.reshape(NDEV, FS, D)
    (dr1, dmix, dln1_g, dln1_b), (r_wgT, r_wuT) = ffn_bwd_x(
        dgate, dup, wgT, wuT, dr2, r1, ln1_g,
        Plans([ScatterPlan([dwgT], only=half_b, into=[r_wgT]), ScatterPlan([dwuT], only=half_a)]))
    (dYA, dYB, dga, dgb, sga, sgb, dwo), (r_wuT,) = merge_bwd(dmix, wo_f, merged, ya, yb, wso_g, wco_g, proj,
                                                              ScatterPlan([dwuT], only=half_b, into=[r_wuT]))
    dwo = dwo.reshape(NDEV, D // NDEV, D)
    dya, dyb, dwso, dwco = branches_bwd(dYA, dYB, ya, yb, wso_g, wco_g)
    (dyn, dglu_b, dglu), (r_wso,) = glu_bwd(yn, dya, glu_f, glu_b, ScatterPlan([dwso]))
    dglu = dglu.reshape(NDEV, W // NDEV, W)
    (dh, dcg, dbg, dconv, shcb), (r_wco,) = conv_bwd(proj, dyb, conv_f, ScatterPlan([dwco]))
    dwin, (r_wo, r_glu) = grad_w_in_rest(xb, dh, dcg, dbg, dga, dgb, ScatterPlan([dwo, dglu]))
    (du, dwb, dwcT, dlbr_s, dlbi_s, dd, su), (r_win,) = ssm_bwd(
        u_p, dyn, xr_p, xi_p, wbT, wcT, lbr_s, lbi_s, dsk, ScatterPlan([dwin], only=tuple(range(1, NDEV))))

    dbr2, dbi2, dlam_re, dlam_im, dldt, dc_re, dc_im = ssm_param_bwd(
        lam_re, lam_im, ldt, fr, fi, br2, bi2, dwb, dwcT, dlbr_s.reshape(NG, NP), dlbi_s.reshape(NG, NP))
    packed = pack_grads(su, shcb, sga, sgb, dd, dglu_b, dln1_g, dln1_b, dln2_g, dln2_b, dlam_re, dlam_im, dldt, sqerr,
                        dbr2, dbi2, dc_re, dc_im, dconv)
    dwin_u = mm_tn(xb, du, "grad_w_in_u").reshape(NDEV, D // NDEV, W)

    rest = [(dh, 0, 1), (dcg, 0, 2), (dbg, 0, 3), (dga, 0, 4), (dga, 1, 5), (dgb, 0, 6), (dgb, 1, 7)]
    (gx_rest,), (r_win_u, small_all) = in_proj_bwd_x(
        rest, win_g, dr1, ALPHA, "in_proj_bwd_x_rest", Plans([ScatterPlan([dwin_u]), GatherPlan([packed])]))
    my_rows = sum_blocks(r_win_u, "sum_w_in_u")

    out = {}

    def put(name, res, back=lambda a: a[None]):
        out["grad_" + name], out["delta_" + name], out["new_m_" + name], out["new_v_" + name] = [back(r) for r in res]

    res_wd, (win_u_sum,) = adam_update(w_down[0], m_w_down[0], v_w_down[0], r_wd, "adam_w_down", 176,
                                       plan=ScatterPlan([my_rows], only=(0,), whole=True))
    put("w_down", res_wd)
    (grad_x,), _ = in_proj_bwd_x([(du, 0, 0)], win_g, gx_rest, 1.0, "in_proj_bwd_x_u")
    put("w_in", adam_update(w_in[0], m_w_in[0], v_w_in[0], r_win, "adam_w_in", 256,
                            summed_on_0=win_u_sum.reshape(D, W))[0])
    put("glu_w", adam_update(glu_w[0], m_glu_w[0], v_glu_w[0], r_glu, "adam_glu_w")[0])
    put("w_ssm_out", adam_update(w_ssm_out[0], m_w_ssm_out[0], v_w_ssm_out[0], r_wso, "adam_w_ssm_out")[0])
    put("w_conv_out", adam_update(w_conv_out[0], m_w_conv_out[0], v_w_conv_out[0], r_wco, "adam_w_conv_out")[0])
    put("w_o", adam_update(w_o[0], m_w_o[0], v_w_o[0], r_wo, "adam_w_o")[0])
    untr = lambda a: jnp.swapaxes(a, 0, 1)[None]
    put("w_gate", adam_update(tr(w_gate), tr(m_w_gate), tr(v_w_gate), r_wgT, "adam_w_gate", 176)[0], untr)
    put("w_up", adam_update(tr(w_up), tr(m_w_up), tr(v_w_up), r_wuT, "adam_w_up", 176)[0], untr)
    as_c = lambda a: jnp.swapaxes(a, 2, 3)
    params = {n: (given[n], given["m_" + n], given["v_" + n]) for n in list(_SMALL) + ["conv_w"]}
    for n in ("ssm_b_re", "ssm_b_im"):
        params[n] = tuple(as_c(a) for a in params[n])
    small, loss = adam_small(small_all, params)
    for n, res in small.items():
        put(n, res, as_c if n in ("ssm_b_re", "ssm_b_im") else (lambda a: a))

    names = ["w_in", "b_in", "ssm_lambda_re", "ssm_lambda_im", "ssm_log_dt", "ssm_b_re", "ssm_b_im", "ssm_c_re", "ssm_c_im",
             "ssm_d", "glu_w", "glu_b", "w_ssm_out", "conv_w", "w_conv_out", "w_o", "ln1_g", "ln1_b", "w_gate", "w_up",
             "w_down", "ln2_g", "ln2_b"]
    return (loss.reshape(()), grad_x[None], *[out[p + n] for p in ("grad_", "delta_", "new_m_", "new_v_") for n in names])
```

```python
import functools
import math

import jax
import jax.numpy as jnp
from jax import lax
from jax.experimental import pallas as pl
from jax.experimental.pallas import tpu as pltpu

f32, bf16 = jnp.float32, jnp.bfloat16
S = jax.ShapeDtypeStruct
MESH = pl.DeviceIdType.MESH
HIGHEST = lax.Precision.HIGHEST

D = 1024
W = 512
NG, NP, GC = 32, 64, 16
F = 2816
NDEV = 8
FS = F // NDEV
IN_COLS = 8 * W
ALPHA = 2.0 ** 0.25
LN_EPS = 1e-5
ADAM_LR, ADAM_B1, ADAM_B2, ADAM_EPS, ADAM_WD, ADAM_STEP = 0.001, 0.9, 0.999, 1e-08, 0.01, 10
NC = 32
LANE = 128
SW = 4 * LANE
VMEM_LIMIT = 56 * 1024 * 1024
GRAD_DT = bf16
ANY = pl.BlockSpec(memory_space=pl.ANY)


def _cp(sem=None, vmem=None):
    return pltpu.CompilerParams(dimension_semantics=sem, vmem_limit_bytes=vmem)


def _resident(shape):
    return pl.BlockSpec(shape, lambda i: (0,) * len(shape), pipeline_mode=pl.Buffered(1))


def _dot(a, b):
    return jnp.dot(a, b, preferred_element_type=f32)


def _dot_nt(a, b):
    return lax.dot_general(a, b, (((1,), (1,)), ((), ())), preferred_element_type=f32)


def _dot_tn(a, b):
    return lax.dot_general(a, b, (((0,), (0,)), ((), ())), preferred_element_type=f32)


def _eye(n):
    return (lax.broadcasted_iota(jnp.int32, (n, n), 0) == lax.broadcasted_iota(jnp.int32, (n, n), 1)).astype(f32)


def _transpose_exact(a):
    return lax.dot_general(a, _eye(a.shape[0]), (((0,), (0,)), ((), ())), precision=HIGHEST, preferred_element_type=f32)


def _sigmoid(x):
    return 1.0 / (1.0 + jnp.exp(-x))


_GK = math.sqrt(2.0 / math.pi)


def _gelu(x):
    return 0.5 * x * (1.0 + jnp.tanh(_GK * (x + 0.044715 * x * x * x)))


def _gelu_grad(x):
    th = jnp.tanh(_GK * (x + 0.044715 * x * x * x))
    return 0.5 * (1.0 + th) + 0.5 * x * (1.0 - th * th) * _GK * (1.0 + 3.0 * 0.044715 * x * x)


ROW_PART = 256


def _row_parts(tm):
    return [slice(r, r + min(ROW_PART, tm)) for r in range(0, tm, min(ROW_PART, tm))]


def _ln_stats(r):
    mu = jnp.mean(r, axis=-1, keepdims=True)
    xc = r - mu
    var = jnp.mean(xc * xc, axis=-1, keepdims=True)
    rstd = lax.rsqrt(var + LN_EPS)
    return xc * rstd, rstd


def _ln_bwd(dy, xhat, rstd, g):
    dxh = dy * g
    m1 = jnp.mean(dxh, axis=-1, keepdims=True)
    m2 = jnp.mean(dxh * xhat, axis=-1, keepdims=True)
    return rstd * (dxh - m1 - xhat * m2)


def _coords():
    return lax.axis_index("x"), lax.axis_index("y"), lax.axis_index("c")


def _when(cond, fn):
    if cond is True:
        fn()
    else:
        pl.when(cond)(fn)


class GatherPlan:
    aliases = ()

    def __init__(self, arrs, srcs=None, into=None):
        n = self.n = len(arrs)
        self.srcs = srcs
        self.inputs = list(arrs) + list(into or [])
        if into:
            self.aliases = tuple((n + a, a) for a in range(n))
        self.out_shape = [S((NDEV,) + a.shape, a.dtype) for a in arrs]
        self.sems = [pltpu.SemaphoreType.DMA((n, 7)), pltpu.SemaphoreType.DMA((n, 7)), pltpu.SemaphoreType.DMA((n,))]

    def _has(self, dev):
        if self.srcs is None:
            return True
        idx = 4 * dev[0] + 2 * dev[1] + dev[2]
        return functools.reduce(jnp.logical_or, [idx == s for s in self.srcs])

    def _parts(self, ins, outs, sems):
        n = self.n
        send_sems, recv_sems, loc_sems = sems
        x, y, c = _coords()
        me, sib = (x, y, c), (x, y, 1 - c)
        chips = [(1 - x, y), (x, 1 - y), (1 - x, 1 - y)]

        def slot(a, dev):
            return outs[a].at[4 * dev[0] + 2 * dev[1] + dev[2]]

        def copy(a, k, block, to, src=None):
            return pltpu.make_async_remote_copy(
                src_ref=slot(a, block) if src is None else src, dst_ref=slot(a, block),
                send_sem=send_sems.at[a, k], recv_sem=recv_sems.at[a, k], device_id=to, device_id_type=MESH)

        each = [(j, chip, a) for j, chip in enumerate(chips) for a in range(n)]
        own = self._has(me)
        return dict(
            mine=lambda: [(pltpu.make_async_copy(ins[a], slot(a, me), loc_sems.at[a]), own) for a in range(n)],
            first=lambda: ([(copy(a, 0, me, sib, src=ins[a]), own) for a in range(n)]
                           + [(copy(a, 1 + j, me, (*chip, c), src=ins[a]), own) for j, chip, a in each]),
            landed=lambda: [(copy(a, 1 + j, (*chip, c), me), self._has((*chip, c))) for j, chip, a in each],
            passed=lambda: [(copy(a, 4 + j, (*chip, c), sib), self._has((*chip, c))) for j, chip, a in each],
            from_sib=lambda: ([(copy(a, 0, sib, me), self._has(sib)) for a in range(n)]
                              + [(copy(a, 4 + j, (*chip, 1 - c), me), self._has((*chip, 1 - c))) for j, chip, a in each]))

    def start(self, ins, outs, sems):
        p = self._parts(ins, outs, sems)
        for cp, cond in p["mine"]() + p["first"]():
            _when(cond, cp.start)

    def forward(self, ins, outs, sems):
        p = self._parts(ins, outs, sems)
        for (got, cond), (fwd, _) in zip(p["landed"](), p["passed"]()):
            def relay(got=got, fwd=fwd):
                got.wait_recv()
                fwd.start()

            _when(cond, relay)

    def finish(self, ins, outs, sems):
        p = self._parts(ins, outs, sems)
        for cp, cond in p["from_sib"]():
            _when(cond, cp.wait_recv)
        for cp, cond in p["first"]() + p["passed"]():
            _when(cond, cp.wait_send)
        for cp, cond in p["mine"]():
            _when(cond, cp.wait)


class ScatterPlan:
    aliases = ()

    def __init__(self, gs, only=None, into=None, whole=False):
        n = self.n = len(gs)
        self.only = only
        self.whole = whole
        self.inputs = list(gs) + list(into or [])
        if into:
            self.aliases = tuple((n + a, a) for a in range(n))
        self.out_shape = [S((NDEV,) + g.shape if whole else g.shape, g.dtype) for g in gs]
        self.sems = [pltpu.SemaphoreType.DMA((n, 7)), pltpu.SemaphoreType.DMA((n, 7)), pltpu.SemaphoreType.DMA((n,))]

    def _owner(self, idx):
        if self.only is None:
            return True
        return functools.reduce(jnp.logical_or, [idx == b for b in self.only])

    def _copies(self, ins, outs, sems):
        n = self.n
        send_sems, recv_sems, loc_sems = sems
        x, y, c = _coords()
        me = 4 * x + 2 * y + c
        mine = self._owner(me)
        block = (lambda a, k: ins[a]) if self.whole else (lambda a, k: ins[a].at[k])
        copies = [(pltpu.make_async_copy(block(a, me), outs[a].at[me], loc_sems.at[a]), mine, None) for a in range(n)]
        for m in range(1, NDEV):
            px = 1 - x if m & 4 else x
            py = 1 - y if m & 2 else y
            pc = 1 - c if m & 1 else c
            peer = 4 * px + 2 * py + pc
            for a in range(n):
                copies.append((pltpu.make_async_remote_copy(
                    src_ref=block(a, peer), dst_ref=outs[a].at[me],
                    send_sem=send_sems.at[a, m - 1], recv_sem=recv_sems.at[a, m - 1],
                    device_id=(px, py, pc), device_id_type=MESH), self._owner(peer), mine))
        return copies

    def start(self, ins, outs, sems):
        for cp, sends, _ in self._copies(ins, outs, sems):
            _when(sends, cp.start)

    def forward(self, ins, outs, sems):
        pass

    def finish(self, ins, outs, sems):
        for cp, sends, receives in self._copies(ins, outs, sems):
            if receives is None:
                _when(sends, cp.wait)
            else:
                _when(sends, cp.wait_send)
                _when(receives, cp.wait_recv)


class Plans:
    def __init__(self, plans):
        self.plans = plans
        self.inputs = [a for p in plans for a in p.inputs]
        self.out_shape = [s for p in plans for s in p.out_shape]
        self.sems = [s for p in plans for s in p.sems]
        self.aliases, i, o = [], 0, 0
        for p in plans:
            self.aliases += [(i + a, o + b) for a, b in p.aliases]
            i, o = i + len(p.inputs), o + len(p.out_shape)

    def _each(self, what, ins, outs, sems):
        i = o = s = 0
        for p in self.plans:
            ni, no, ns = len(p.inputs), len(p.out_shape), len(p.sems)
            getattr(p, what)(ins[i:i + ni], outs[o:o + no], sems[s:s + ns])
            i, o, s = i + ni, o + no, s + ns

    def start(self, ins, outs, sems):
        self._each("start", ins, outs, sems)

    def forward(self, ins, outs, sems):
        self._each("forward", ins, outs, sems)

    def finish(self, ins, outs, sems):
        self._each("finish", ins, outs, sems)


def _call(body, args, *, name, grid, in_specs, out_specs, out_shape, scratch=(), sem=None, vmem=None, plan=None):
    if plan is None:
        outs = pl.pallas_call(body, name=name, grid=grid, in_specs=list(in_specs), out_specs=list(out_specs),
                              out_shape=list(out_shape), scratch_shapes=list(scratch),
                              compiler_params=_cp(sem, vmem))(*args)
        return list(outs), []
    ni, no, ns = len(in_specs), len(out_specs), len(scratch)
    pi, po = len(plan.inputs), len(plan.out_shape)
    aliases = {ni + a: no + b for a, b in plan.aliases}

    def wrapped(*refs):
        main_in, p_in = refs[:ni], refs[ni:ni + pi]
        main_out, p_out = refs[ni + pi:ni + pi + no], refs[ni + pi + no:ni + pi + no + po]
        main_scr, p_sems = refs[ni + pi + no + po:ni + pi + no + po + ns], refs[ni + pi + no + po + ns:]
        ids = [pl.program_id(d) for d in range(len(grid))]
        first = functools.reduce(jnp.logical_and, [i == 0 for i in ids])
        last = functools.reduce(jnp.logical_and, [i == g - 1 for i, g in zip(ids, grid)])

        @pl.when(first)
        def _():
            plan.start(p_in, p_out, p_sems)

        body(*main_in, *main_out, *main_scr)

        @pl.when(last)
        def _():
            plan.forward(p_in, p_out, p_sems)
            plan.finish(p_in, p_out, p_sems)

    outs = pl.pallas_call(
        wrapped, name=name, grid=grid, in_specs=list(in_specs) + [ANY] * pi, out_specs=list(out_specs) + [ANY] * po,
        out_shape=list(out_shape) + list(plan.out_shape), scratch_shapes=list(scratch) + list(plan.sems),
        input_output_aliases=aliases, compiler_params=_cp(("arbitrary",) * len(grid), vmem),
    )(*args, *plan.inputs)
    return list(outs[:no]), list(outs[no:])


def mm_tn(a, b, name, tn=512):
    T, K = a.shape
    N = b.shape[1]
    tn = min(tn, N)

    def body(a_ref, b_ref, o_ref):
        o_ref[...] = _dot_tn(a_ref[...], b_ref[...]).astype(GRAD_DT)

    (out,), _ = _call(body, [a, b], name=name, grid=(N // tn,),
                      in_specs=[_resident((T, K)), pl.BlockSpec((T, tn), lambda j: (0, j))],
                      out_specs=[pl.BlockSpec((None, K, tn), lambda j: (j, 0, 0))],
                      out_shape=[S((N // tn, K, tn), GRAD_DT)], sem=("parallel",), vmem=VMEM_LIMIT)
    return out


def grad_w_in_rest(xb, dh, dcg, dbg, dga, dgb, plan):
    T = xb.shape[0]
    order = ((0, 0), (1, 1), (2, 2), (3, 3), (4, 3), (5, 4), (6, 4))

    def body(x_ref, *refs):
        o_ref = refs[-1]
        j = pl.program_id(0)
        for step, opnd in order:
            @pl.when(j == step)
            def _(opnd=opnd):
                o_ref[...] = _dot_tn(x_ref[...], refs[opnd][...]).astype(GRAD_DT)

    once = lambda: pl.BlockSpec((T, W), lambda j: (0, 0), pipeline_mode=pl.Buffered(1))
    (out,), sent = _call(
        body, [xb, dh, dcg, dbg, dga, dgb], name="grad_w_in_rest", grid=(len(order),),
        in_specs=[_resident((T, D)), once(), once(), once(),
                  pl.BlockSpec((T, W), lambda j: (0, jnp.clip(j - 3, 0, 1))),
                  pl.BlockSpec((T, W), lambda j: (0, jnp.clip(j - 5, 0, 1)))],
        out_specs=[pl.BlockSpec((None, D, W), lambda j: (1 + j, 0, 0))],
        out_shape=[S((NDEV, D, W), GRAD_DT)], sem=("arbitrary",), vmem=VMEM_LIMIT, plan=plan)
    return out, sent


def mm_tn_rows(a, b, name, tk=256, plan=None):
    T, K = a.shape
    N = b.shape[1]
    tk = min(tk, K)

    def body(a_ref, b_ref, o_ref):
        o_ref[...] = _dot_tn(a_ref[...], b_ref[...]).astype(GRAD_DT)

    (out,), sent = _call(body, [a, b], name=name, grid=(K // tk,),
                         in_specs=[pl.BlockSpec((T, tk), lambda i: (0, i)), _resident((T, N))],
                         out_specs=[pl.BlockSpec((tk, N), lambda i: (i, 0))], out_shape=[S((K, N), GRAD_DT)],
                         sem=("parallel",), vmem=VMEM_LIMIT, plan=plan)
    return out, sent


def prep_weights(ws, name, plan=None):
    def body(*refs):
        for i in range(len(ws)):
            refs[len(ws) + i][...] = refs[i][...].astype(bf16)

    whole = [pl.BlockSpec(w.shape, lambda i, n=w.ndim: (0,) * n) for w in ws]
    return _call(body, list(ws), name=name, grid=(1,), in_specs=whole, out_specs=whole,
                 out_shape=[S(w.shape, bf16) for w in ws], sem=("arbitrary",), vmem=VMEM_LIMIT, plan=plan)


REST_BLOCKS = (4, 5, 6, 7, 1, 2, 3)
REST_COLS = len(REST_BLOCKS) * W


def in_proj_u(x, win_g, b_in):
    T = x.shape[0]
    tm = min(1024, T)

    def body(x_ref, w_ref, b_ref, u_ref, xb_ref):
        xb = x_ref[...].astype(bf16)
        xb_ref[...] = xb
        u_ref[...] = _dot(xb, w_ref[...]) + b_ref[...]

    row = pl.BlockSpec((tm, D), lambda i: (i, 0))
    return pl.pallas_call(
        body, name="in_proj_u", grid=(T // tm,),
        in_specs=[row, pl.BlockSpec((None, D, W), lambda i: (0, 0, 0)), pl.BlockSpec((1, W), lambda i: (0, 0))],
        out_specs=[pl.BlockSpec((tm, W), lambda i: (i, 0)), row],
        out_shape=[S((T, W), f32), S((T, D), bf16)], compiler_params=_cp(("parallel",), VMEM_LIMIT),
    )(x, win_g, b_in)


def in_proj_rest(xb, win_g, b_in, plan):
    T = xb.shape[0]
    tm = min(512, T)

    def body(x_ref, w_ref, b_ref, o_ref):
        xb_ = x_ref[...]
        for i, k in enumerate(REST_BLOCKS):
            o_ref[:, i * W:(i + 1) * W] = _dot(xb_, w_ref[k]) + b_ref[:, k * W:(k + 1) * W]

    return _call(
        body, [xb, win_g, b_in], name="in_proj_rest", grid=(T // tm,),
        in_specs=[pl.BlockSpec((tm, D), lambda i: (i, 0)), _resident((NDEV, D, W)), _resident((1, IN_COLS))],
        out_specs=[pl.BlockSpec((tm, REST_COLS), lambda i: (i, 0))],
        out_shape=[S((T, REST_COLS), f32)], vmem=VMEM_LIMIT, plan=plan)


def _to_scan_order(a_ref, o_ref):
    L = a_ref.shape[0] // NC

    def step(jb, carry):
        j0 = pl.multiple_of(jb * 8, 8)
        for q in range(NC // 8):
            x = jnp.stack([a_ref[pl.ds((8 * q + c) * L + j0, 8), :] for c in range(8)], axis=0)
            y = jnp.swapaxes(x, 0, 1)
            for j in range(8):
                o_ref[pl.ds((j0 + j) * NC + 8 * q, 8), :] = y[j]
        return carry

    lax.fori_loop(0, L // 8, step, 0)


def _to_time_order(a_ref, o_ref):
    L = a_ref.shape[0] // NC

    def step(jb, carry):
        j0 = pl.multiple_of(jb * 16, 16)
        for q in range(NC // 8):
            halves = []
            for h in range(2):
                x = jnp.stack([a_ref[pl.ds((j0 + 8 * h + j) * NC + 8 * q, 8), :] for j in range(8)], axis=0)
                halves.append(jnp.swapaxes(x, 0, 1))
            for c in range(8):
                o_ref[pl.ds((8 * q + c) * L + j0, 16), :] = jnp.concatenate(
                    [halves[0][c], halves[1][c]], axis=0).astype(o_ref.dtype)
        return carry

    lax.fori_loop(0, L // 16, step, 0)


def _disc(lr, li, ldt):
    dt = jnp.exp(ldt)
    mag = jnp.exp(lr * dt)
    lbr = mag * jnp.cos(li * dt)
    lbi = mag * jnp.sin(li * dt)
    den = lr * lr + li * li
    nr = lbr - 1.0
    return lbr, lbi, (nr * lr + lbi * li) / den, (lbi * lr - nr * li) / den


def _per_channel(f):
    return jnp.broadcast_to(f[:, None, :], (NG, GC, NP)).reshape(NG * GC, NP)


def ssm_params(lam_re, lam_im, log_dt, br, bi):
    def body(lr_ref, li_ref, ldt_ref, br_ref, bi_ref, lbr_ref, lbi_ref, fr_ref, fi_ref, bbr_ref, bbi_ref):
        lbr, lbi, fr, fi = _disc(lr_ref[...], li_ref[...], ldt_ref[...])
        lbr_ref[...], lbi_ref[...], fr_ref[...], fi_ref[...] = lbr, lbi, fr, fi
        fr_, fi_, br_, bi_ = _per_channel(fr), _per_channel(fi), br_ref[...], bi_ref[...]
        bbr_ref[...] = fr_ * br_ - fi_ * bi_
        bbi_ref[...] = fr_ * bi_ + fi_ * br_

    return pl.pallas_call(body, name="ssm_params", out_shape=[S((NG, NP), f32)] * 4 + [S((NG * GC, NP), f32)] * 2)(
        lam_re, lam_im, log_dt, br, bi)


SCAN_UNROLL = 4
SCAN_LANES = 2 * LANE


def _steps(n, body, carry):
    main = n // SCAN_UNROLL

    def trip(t, c):
        for q in range(SCAN_UNROLL):
            c = body(t * SCAN_UNROLL + q, c)
        return c

    carry = lax.fori_loop(0, main, trip, carry)
    for i in range(main * SCAN_UNROLL, n):
        carry = body(i, carry)
    return carry


def _scan_body(T):
    L = T // NC
    RB = min(512, T)
    nsq = int(round(math.log2(L)))
    assert 2 ** nsq == L and T % RB == 0 and L % 16 == 0

    def rows(i):
        return pl.ds(pl.multiple_of(i * RB, RB), RB)

    def tile(j):
        return pl.ds(j * NC if isinstance(j, int) else pl.multiple_of(j * NC, NC), NC)

    def forward_states(u_ref, wb_ref, lbr_ref, lbi_ref, sre, sim, ere, eim):
        def bproj(i, carry):
            bu = _dot(u_ref[rows(i), :].astype(bf16), wb_ref[...])
            sre[rows(i), :] = bu[:, :SW]
            sim[rows(i), :] = bu[:, SW:]
            return carry

        lax.fori_loop(0, T // RB, bproj, 0)
        for lb in range(SW // SCAN_LANES):
            ls = slice(lb * SCAN_LANES, (lb + 1) * SCAN_LANES)
            ar = jnp.broadcast_to(lbr_ref[:, ls], (NC, SCAN_LANES))
            ai = jnp.broadcast_to(lbi_ref[:, ls], (NC, SCAN_LANES))

            def step(j, carry):
                xr, xi = carry
                nr = ar * xr - ai * xi + sre[tile(j), ls]
                ni = ar * xi + ai * xr + sim[tile(j), ls]
                sre[tile(j), ls] = nr
                sim[tile(j), ls] = ni
                return nr, ni

            zero = jnp.zeros((NC, SCAN_LANES), f32)
            _steps(L, step, (zero, zero))
            pr, pi = lbr_ref[:, ls], lbi_ref[:, ls]
            for _ in range(nsq):
                pr, pi = pr * pr - pi * pi, 2.0 * pr * pi
            er = jnp.zeros((1, SCAN_LANES), f32)
            ei = er
            ere[0:1, ls] = er
            eim[0:1, ls] = ei
            base = (L - 1) * NC
            for c in range(1, NC):
                lr_ = sre[base + c - 1:base + c, ls]
                li_ = sim[base + c - 1:base + c, ls]
                er, ei = lr_ + pr * er - pi * ei, li_ + pr * ei + pi * er
                ere[c:c + 1, ls] = er
                eim[c:c + 1, ls] = ei
            e_r, e_i = ere[:, ls].reshape(NC // 8, 8, SCAN_LANES), eim[:, ls].reshape(NC // 8, 8, SCAN_LANES)
            ar8, ai8 = ar[0:8], ai[0:8]

            def fix(j, carry):
                pwr, pwi = carry
                xr = sre[tile(j), ls].reshape(NC // 8, 8, SCAN_LANES) + (pwr * e_r - pwi * e_i)
                xi = sim[tile(j), ls].reshape(NC // 8, 8, SCAN_LANES) + (pwr * e_i + pwi * e_r)
                sre[tile(j), ls] = xr.reshape(NC, SCAN_LANES)
                sim[tile(j), ls] = xi.reshape(NC, SCAN_LANES)
                return pwr * ar8 - pwi * ai8, pwr * ai8 + pwi * ar8

            _steps(L, fix, (ar8, ai8))

    return L, RB, nsq, rows, tile, forward_states


def ssm_fwd(u, wb, wc, lbr, lbi, dsk, plan):
    T = u.shape[0]
    L, RB, nsq, rows, tile, forward_states = _scan_body(T)
    nslab = W // LANE

    def body(u_ref, wb_ref, wc_ref, lbr_ref, lbi_ref, d_ref, y_ref, up_ref, xr_ref, xi_ref, sre, sim, ere, eim, yp):
        _to_scan_order(u_ref, up_ref)
        forward_states(up_ref, wb_ref, lbr_ref, lbi_ref, sre, sim, ere, eim)

        def cproj(i, carry):
            xr, xi = sre[rows(i), :].astype(bf16), sim[rows(i), :].astype(bf16)
            xr_ref[rows(i), :] = xr
            xi_ref[rows(i), :] = xi
            y = _dot(xr, wc_ref[0:SW, :]) + _dot(xi, wc_ref[SW:, :])
            yp[rows(i), :] = y + d_ref[...] * up_ref[rows(i), :]
            return carry

        lax.fori_loop(0, T // RB, cproj, 0)
        _to_time_order(yp, y_ref)

    slab = pl.BlockSpec((T, LANE), lambda k: (0, k))
    states = pl.BlockSpec((T, SW), lambda k: (0, k))
    return _call(
        body, [u, wb, wc, lbr, lbi, dsk], name="ssm_fwd", grid=(nslab,),
        in_specs=[slab, pl.BlockSpec((None, LANE, 2 * SW), lambda k: (k, 0, 0)),
                  pl.BlockSpec((None, 2 * SW, LANE), lambda k: (k, 0, 0)),
                  pl.BlockSpec((None, 1, SW), lambda k: (k, 0, 0)), pl.BlockSpec((None, 1, SW), lambda k: (k, 0, 0)),
                  pl.BlockSpec((None, 1, LANE), lambda k: (k, 0, 0))],
        out_specs=[slab, slab, states, states],
        out_shape=[S((T, W), f32), S((T, W), f32), S((T, nslab * SW), bf16), S((T, nslab * SW), bf16)],
        scratch=[pltpu.VMEM((T, SW), f32), pltpu.VMEM((T, SW), f32), pltpu.VMEM((NC, SW), f32), pltpu.VMEM((NC, SW), f32),
                 pltpu.VMEM((T, LANE), f32)],
        vmem=VMEM_LIMIT, plan=plan)


def ssm_bwd(u_p, dy, xr, xi, wbT, wcT, lbr, lbi, dsk, plan):
    T = u_p.shape[0]
    L, RB, nsq, rows, tile, _ = _scan_body(T)

    def body(u_ref, dyt_ref, sre, sim, wbT_ref, wcT_ref, lbr_ref, lbi_ref, d_ref,
             dut_ref, dwb_ref, dwc_ref, dlr_ref, dli_ref, dd_ref, su_ref, gre, gim, ere, eim, dy_ref, du_ref):
        _to_scan_order(dyt_ref, dy_ref)

        def dstate(i, carry):
            g = _dot(dy_ref[rows(i), :].astype(bf16), wcT_ref[...])
            gre[rows(i), :] = g[:, :SW]
            gim[rows(i), :] = g[:, SW:]
            return carry

        lax.fori_loop(0, T // RB, dstate, 0)
        row = lax.broadcasted_iota(jnp.int32, (NC, SCAN_LANES), 0)
        for lb in range(SW // SCAN_LANES):
            ls = slice(lb * SCAN_LANES, (lb + 1) * SCAN_LANES)
            ar = jnp.broadcast_to(lbr_ref[:, ls], (NC, SCAN_LANES))
            ai = jnp.broadcast_to(lbi_ref[:, ls], (NC, SCAN_LANES))

            def step(i, carry):
                gr, gi = carry
                j = L - 1 - i
                nr = ar * gr + ai * gi + gre[tile(j), ls]
                ni = ar * gi - ai * gr + gim[tile(j), ls]
                gre[tile(j), ls] = nr
                gim[tile(j), ls] = ni
                return nr, ni

            zero = jnp.zeros((NC, SCAN_LANES), f32)
            _steps(L, step, (zero, zero))
            pr, pi = lbr_ref[:, ls], -lbi_ref[:, ls]
            for _ in range(nsq):
                pr, pi = pr * pr - pi * pi, 2.0 * pr * pi
            er = jnp.zeros((1, SCAN_LANES), f32)
            ei = er
            ere[NC - 1:NC, ls] = er
            eim[NC - 1:NC, ls] = ei
            for c in range(NC - 2, -1, -1):
                lr_ = gre[c + 1:c + 2, ls]
                li_ = gim[c + 1:c + 2, ls]
                er, ei = lr_ + pr * er - pi * ei, li_ + pr * ei + pi * er
                ere[c:c + 1, ls] = er
                eim[c:c + 1, ls] = ei
            e_r, e_i = ere[:, ls].reshape(NC // 8, 8, SCAN_LANES), eim[:, ls].reshape(NC // 8, 8, SCAN_LANES)
            ar8, ai8 = ar[0:8], ai[0:8]

            def fixed(j, pwr, pwi):
                gr = (gre[tile(j), ls].reshape(NC // 8, 8, SCAN_LANES) + (pwr * e_r - pwi * e_i)).reshape(NC, SCAN_LANES)
                gi = (gim[tile(j), ls].reshape(NC // 8, 8, SCAN_LANES) + (pwr * e_i + pwi * e_r)).reshape(NC, SCAN_LANES)
                gre[tile(j), ls] = gr
                gim[tile(j), ls] = gi
                return gr, gi

            def fix(i, carry):
                pwr, pwi, accr, acci = carry
                j = L - 1 - i
                gr, gi = fixed(j, pwr, pwi)
                xr, xi = sre[tile(j - 1), ls].astype(f32), sim[tile(j - 1), ls].astype(f32)
                return (pwr * ar8 + pwi * ai8, pwi * ar8 - pwr * ai8,
                        accr + gr * xr + gi * xi, acci + gi * xr - gr * xi)

            pwr, pwi, accr, acci = _steps(L - 1, fix, (ar8, -ai8, zero, zero))
            gr, gi = fixed(0, pwr, pwi)
            xr = jnp.where(row == 0, 0.0, pltpu.roll(sre[tile(L - 1), ls].astype(f32), 1, axis=0))
            xi = jnp.where(row == 0, 0.0, pltpu.roll(sim[tile(L - 1), ls].astype(f32), 1, axis=0))
            accr = accr + gr * xr + gi * xi
            acci = acci + gi * xr - gr * xi
            dlr_ref[:, ls] = jnp.sum(accr, axis=0, keepdims=True)
            dli_ref[:, ls] = jnp.sum(acci, axis=0, keepdims=True)

        dwb_ref[...] = jnp.zeros_like(dwb_ref)
        dwc_ref[...] = jnp.zeros_like(dwc_ref)
        dd_ref[...] = jnp.zeros_like(dd_ref)
        su_ref[...] = jnp.zeros_like(su_ref)

        def finish(i, carry):
            u32, dy32 = u_ref[rows(i), :], dy_ref[rows(i), :]
            ub, dyb = u32.astype(bf16), dy32.astype(bf16)
            gr, gi = gre[rows(i), :].astype(bf16), gim[rows(i), :].astype(bf16)
            du = _dot(gr, wbT_ref[0:SW, :]) + _dot(gi, wbT_ref[SW:, :]) + dy32 * d_ref[...]
            du_ref[rows(i), :] = du
            su_ref[...] += jnp.sum(du, axis=0, keepdims=True)
            dwb_ref[:, 0:SW] += _dot_tn(ub, gr)
            dwb_ref[:, SW:] += _dot_tn(ub, gi)
            dwc_ref[:, 0:SW] += _dot_tn(dyb, sre[rows(i), :])
            dwc_ref[:, SW:] += _dot_tn(dyb, sim[rows(i), :])
            dd_ref[...] += jnp.sum(dy32 * u32, axis=0, keepdims=True)
            return carry

        lax.fori_loop(0, T // RB, finish, 0)
        _to_time_order(du_ref, dut_ref)

    slab = pl.BlockSpec((T, LANE), lambda k: (0, k))
    wide = pl.BlockSpec((None, LANE, 2 * SW), lambda k: (k, 0, 0))
    tall = pl.BlockSpec((None, 2 * SW, LANE), lambda k: (k, 0, 0))
    vec = pl.BlockSpec((None, 1, SW), lambda k: (k, 0, 0))
    vecd = pl.BlockSpec((None, 1, LANE), lambda k: (k, 0, 0))
    states = pl.BlockSpec((T, SW), lambda k: (0, k))
    nslab = W // LANE
    return _call(
        body, [u_p, dy, xr, xi, wbT, wcT, lbr, lbi, dsk], name="ssm_bwd", grid=(nslab,),
        in_specs=[slab, slab, states, states, tall, wide, vec, vec, vecd],
        out_specs=[slab, wide, wide, vec, vec, vecd, vecd],
        out_shape=[S((T, W), bf16), S((nslab, LANE, 2 * SW), f32), S((nslab, LANE, 2 * SW), f32),
                   S((nslab, 1, SW), f32), S((nslab, 1, SW), f32), S((nslab, 1, LANE), f32), S((nslab, 1, LANE), f32)],
        scratch=[pltpu.VMEM((T, SW), f32)] * 2 + [pltpu.VMEM((NC, SW), f32)] * 2 + [pltpu.VMEM((T, LANE), f32)] * 2,
        vmem=VMEM_LIMIT, plan=plan)


def _shift_rows(cur, prev8, k):
    return pltpu.roll(jnp.concatenate([prev8, cur], axis=0), k, axis=0)[8:]


def _lift_rows(cur, next8, k):
    n = cur.shape[0]
    return pltpu.roll(jnp.concatenate([cur, next8], axis=0), n + 8 - k, axis=0)[:n]


def conv_fwd(proj, conv_w):
    T = proj.shape[0]
    RB = min(512, T)

    def body(h_ref, c_ref, b_ref, w_ref, o_ref):
        w0, w1, w2 = w_ref[0:1, :], w_ref[1:2, :], w_ref[2:3, :]

        def blk(i, carry):
            r0 = pl.multiple_of(i * RB, RB)
            rs = pl.ds(r0, RB)
            ch = c_ref[rs, :] * h_ref[rs, :]
            pr = pl.ds(jnp.maximum(r0 - 8, 0), 8)
            prev = jnp.where(i > 0, c_ref[pr, :] * h_ref[pr, :], 0.0)
            z = w2 * ch + w1 * _shift_rows(ch, prev, 1) + w0 * _shift_rows(ch, prev, 2)
            o_ref[rs, :] = (b_ref[rs, :] * z).astype(bf16)
            return carry

        lax.fori_loop(0, T // RB, blk, 0)

    nb = W // LANE
    return pl.pallas_call(
        body, name="conv_fwd", grid=(nb,),
        in_specs=[pl.BlockSpec((T, LANE), lambda k: (0, 4 * nb + k)), pl.BlockSpec((T, LANE), lambda k: (0, 5 * nb + k)),
                  pl.BlockSpec((T, LANE), lambda k: (0, 6 * nb + k)),pl.BlockSpec((3, LANE), lambda k: (0, k))],
        out_specs=pl.BlockSpec((T, LANE), lambda k: (0, k)), out_shape=S((T, W), bf16),
        compiler_params=_cp(("parallel",), VMEM_LIMIT),
    )(proj, proj, proj, conv_w)


def _dense_columns(blocks_ref, dense_ref):
    for k in range(NDEV):
        dense_ref[:, k * LANE:(k + 1) * LANE] = blocks_ref[k]


def merge_fwd(yn, glu_w, glu_b, yb, wso, wco, proj, plan):
    T = yn.shape[0]
    tm = min(1024, T)

    def body(y_ref, gw_ref, gbias_ref, yb_ref, wa_ref, wb_ref, ga_ref, gb_ref, o_ref, ya_ref, wa_s, wb_s):
        @pl.when(pl.program_id(0) == 0)
        def _():
            _dense_columns(wa_ref, wa_s)
            _dense_columns(wb_ref, wb_s)

        for rs in _row_parts(tm):
            g = _gelu(y_ref[rs, :])
            ya = (g * _sigmoid(_dot(g.astype(bf16), gw_ref[...]) + gbias_ref[...])).astype(bf16)
            ya_ref[rs, :] = ya
            o_ref[rs, :] = (_sigmoid(ga_ref[rs, :]) * _dot(ya, wa_s[...])
                            + _sigmoid(gb_ref[rs, :]) * _dot(yb_ref[rs, :], wb_s[...])).astype(bf16)

    act = pl.BlockSpec((tm, W), lambda i: (i, 0))
    return _call(
        body, [yn, glu_w, glu_b, yb, wso, wco, proj, proj], name="merge_fwd", grid=(T // tm,),
        in_specs=[act, pl.BlockSpec((W, W), lambda i: (0, 0)), pl.BlockSpec((1, W), lambda i: (0, 0)), act,
                  _resident((NDEV, W, LANE)), _resident((NDEV, W, LANE)),
                  pl.BlockSpec((tm, D), lambda i: (i, 0)), pl.BlockSpec((tm, D), lambda i: (i, 1))],
        out_specs=[pl.BlockSpec((tm, D), lambda i: (i, 0)), act], out_shape=[S((T, D), bf16), S((T, W), bf16)],
        scratch=[pltpu.VMEM((W, D), bf16), pltpu.VMEM((W, D), bf16)], vmem=VMEM_LIMIT, plan=plan)


def mix_ln1(merged, w_o, x, g1, b1, plan):
    T = x.shape[0]
    tm = min(512, T)

    def body(m_ref, w_ref, x_ref, g_ref, b_ref, r_ref, x1_ref):
        for rs in _row_parts(tm):
            r = ALPHA * x_ref[rs, :] + _dot(m_ref[rs, :], w_ref[...])
            r_ref[rs, :] = r
            xhat, _ = _ln_stats(r)
            x1_ref[rs, :] = (xhat * g_ref[...] + b_ref[...]).astype(bf16)

    row = pl.BlockSpec((tm, D), lambda i: (i, 0))
    vec = pl.BlockSpec((1, D), lambda i: (0, 0))
    return _call(
        body, [merged, w_o, x, g1, b1], name="mix_ln1", grid=(T // tm,),
        in_specs=[row, _resident((D, D)), row, vec, vec],
        out_specs=[row, row], out_shape=[S((T, D), f32), S((T, D), bf16)], sem=("parallel",), vmem=VMEM_LIMIT, plan=plan)


FT = 256


def gate_up(x1b, wgT, wuT, plan):
    T = x1b.shape[0]
    tm = min(512, T)

    def body(x_ref, wg_ref, wu_ref, g_ref, u_ref, h_ref):
        x = x_ref[...]
        for n in range(F // FT):
            cs = slice(n * FT, (n + 1) * FT)
            g = _dot_nt(x, wg_ref[cs, :])
            u = _dot_nt(x, wu_ref[cs, :])
            g_ref[:, cs] = g.astype(bf16)
            u_ref[:, cs] = u.astype(bf16)
            h_ref[:, cs] = (g * _sigmoid(g) * u).astype(bf16)

    osp = pl.BlockSpec((tm, F), lambda i: (i, 0))
    return _call(
        body, [x1b, wgT, wuT], name="gate_up", grid=(T // tm,),
        in_specs=[pl.BlockSpec((tm, D), lambda i: (i, 0)), _resident((F, D)), _resident((F, D))],
        out_specs=[osp, osp, osp], out_shape=[S((T, F), bf16)] * 3, vmem=VMEM_LIMIT, plan=plan)


def down_loss(hid, w_down, r1, g1, b1, g2, b2, target):
    T = hid.shape[0]
    tm = min(512, T)

    def body(h_ref, w_ref, r1_ref, g1_ref, b1_ref, g2_ref, b2_ref, t_ref, dr_ref, drb_ref, loss_ref, dg_ref, db_ref):
        @pl.when(pl.program_id(0) == 0)
        def _():
            loss_ref[...] = jnp.zeros_like(loss_ref)
            dg_ref[...] = jnp.zeros_like(dg_ref)
            db_ref[...] = jnp.zeros_like(db_ref)

        for rs in _row_parts(tm):
            xh1, _ = _ln_stats(r1_ref[rs, :])
            x1 = xh1 * g1_ref[...] + b1_ref[...]
            r2 = ALPHA * x1 + _dot(h_ref[rs, :], w_ref[...])
            xh2, rstd2 = _ln_stats(r2)
            err = xh2 * g2_ref[...] + b2_ref[...] - t_ref[rs, :]
            loss_ref[...] += jnp.sum(jnp.mean(err * err, axis=-1, keepdims=True), axis=0, keepdims=True)
            dy = err * (1.0 / D)
            dg_ref[...] += jnp.sum(dy * xh2, axis=0, keepdims=True)
            db_ref[...] += jnp.sum(dy, axis=0, keepdims=True)
            dr = _ln_bwd(dy, xh2, rstd2, g2_ref[...])
            dr_ref[rs, :] = dr
            drb_ref[rs, :] = dr.astype(bf16)

    row = pl.BlockSpec((tm, D), lambda i: (i, 0))
    vec = pl.BlockSpec((1, D), lambda i: (0, 0))
    return pl.pallas_call(
        body, name="down_loss", grid=(T // tm,),
        in_specs=[pl.BlockSpec((tm, F), lambda i: (i, 0)), _resident((F, D)), row, vec, vec, vec, vec, row],
        out_specs=[row, row, pl.BlockSpec((1, 1), lambda i: (0, 0)), vec, vec],
        out_shape=[S((T, D), f32), S((T, D), bf16), S((1, 1), f32), S((1, D), f32), S((1, D), f32)],
        compiler_params=_cp(("arbitrary",), VMEM_LIMIT),
    )(hid, w_down, r1, g1, b1, g2, b2, target)


def ffn_bwd_act(dffn, w_down, gate, up, plan):
    T = dffn.shape[0]
    tm = min(512, T)

    def body(d_ref, w_ref, g_ref, u_ref, dg_ref, du_ref):
        for n in range(F // FT):
            cs = slice(n * FT, (n + 1) * FT)
            for rs in _row_parts(tm):
                dh = _dot_nt(d_ref[rs, :], w_ref[cs, :])
                g, u = g_ref[rs, cs].astype(f32), u_ref[rs, cs].astype(f32)
                sg = _sigmoid(g)
                t = g * sg
                du_ref[rs, cs] = (dh * t).astype(bf16)
                dg_ref[rs, cs] = (dh * u * (sg + t - t * sg)).astype(bf16)

    osp = pl.BlockSpec((tm, F), lambda i: (i, 0))
    return _call(
        body, [dffn, w_down, gate, up], name="ffn_bwd_act", grid=(T // tm,),
        in_specs=[pl.BlockSpec((tm, D), lambda i: (i, 0)), _resident((F, D)), osp, osp],
        out_specs=[osp, osp], out_shape=[S((T, F), bf16)] * 2, sem=("parallel",), vmem=VMEM_LIMIT, plan=plan)


def ffn_bwd_x(dgate, dup, wgT, wuT, dr2, r1, g1, plan):
    T = dr2.shape[0]
    tm = min(512, T)

    def body(dg_ref, du_ref, wg_ref, wu_ref, dr2_ref, r1_ref, g1_ref, dr_ref, drb_ref, dgam_ref, dbet_ref):
        @pl.when(pl.program_id(0) == 0)
        def _():
            dgam_ref[...] = jnp.zeros_like(dgam_ref)
            dbet_ref[...] = jnp.zeros_like(dbet_ref)

        for rs in _row_parts(tm):
            dx1 = ALPHA * dr2_ref[rs, :] + _dot(dg_ref[rs, :], wg_ref[...]) + _dot(du_ref[rs, :], wu_ref[...])
            xh, rstd = _ln_stats(r1_ref[rs, :])
            dgam_ref[...] += jnp.sum(dx1 * xh, axis=0, keepdims=True)
            dbet_ref[...] += jnp.sum(dx1, axis=0, keepdims=True)
            dr = _ln_bwd(dx1, xh, rstd, g1_ref[...])
            dr_ref[rs, :] = dr
            drb_ref[rs, :] = dr.astype(bf16)

    row = pl.BlockSpec((tm, D), lambda i: (i, 0))
    wide = pl.BlockSpec((tm, F), lambda i: (i, 0))
    wsp = _resident((F, D))
    vec = pl.BlockSpec((1, D), lambda i: (0, 0))
    return _call(
        body, [dgate, dup, wgT, wuT, dr2, r1, g1], name="ffn_bwd_x", grid=(T // tm,),
        in_specs=[wide, wide, wsp, wsp, row, row, vec],
        out_specs=[row, row, vec, vec], out_shape=[S((T, D), f32), S((T, D), bf16), S((1, D), f32), S((1, D), f32)],
        vmem=VMEM_LIMIT, plan=plan)


def merge_bwd(dmix, w_o, merged, ya, yb, wso, wco, proj, plan):
    T = dmix.shape[0]
    tm = min(512, T)

    def body(dm_ref, wo_ref, m_ref, ya_ref, yb_ref, wa_ref, wb_ref, ga_ref, gb_ref,
             dya_ref, dyb_ref, dga_ref, dgb_ref, sa_ref, sb_ref, dwo_ref, wa_s, wb_s, acc):
        @pl.when(pl.program_id(0) == 0)
        def _():
            _dense_columns(wa_ref, wa_s)
            _dense_columns(wb_ref, wb_s)
            acc[...] = jnp.zeros_like(acc)

        acc[...] += _dot_tn(m_ref[...], dm_ref[...])

        @pl.when(pl.program_id(0) == pl.num_programs(0) - 1)
        def _():
            dwo_ref[...] = acc[...].astype(GRAD_DT)

        dmer = _dot_nt(dm_ref[...], wo_ref[...])
        sa, sb = _sigmoid(ga_ref[...]), _sigmoid(gb_ref[...])
        dya_ref[...] = (dmer * sa).astype(bf16)
        dyb_ref[...] = (dmer * sb).astype(bf16)
        dga = dmer * _dot(ya_ref[...], wa_s[...]) * sa * (1.0 - sa)
        dgb = dmer * _dot(yb_ref[...], wb_s[...]) * sb * (1.0 - sb)
        dga_ref[...] = dga.astype(bf16)
        dgb_ref[...] = dgb.astype(bf16)
        sa_ref[...] = jnp.sum(dga, axis=0, keepdims=True)
        sb_ref[...] = jnp.sum(dgb, axis=0, keepdims=True)

    act = pl.BlockSpec((tm, W), lambda i: (i, 0))
    osp = pl.BlockSpec((tm, D), lambda i: (i, 0))
    ssp = pl.BlockSpec((None, 1, D), lambda i: (i, 0, 0))
    return _call(
        body, [dmix, w_o, merged, ya, yb, wso, wco, proj, proj], name="merge_bwd", grid=(T // tm,),
        in_specs=[osp, _resident((D, D)), osp, act, act, _resident((NDEV, W, LANE)), _resident((NDEV, W, LANE)),
                  pl.BlockSpec((tm, D), lambda i: (i, 0)), pl.BlockSpec((tm, D), lambda i: (i, 1))],
        out_specs=[osp, osp, osp, osp, ssp, ssp, pl.BlockSpec((D, D), lambda i: (0, 0))],
        out_shape=[S((T, D), bf16)] * 4 + [S((T // tm, 1, D), f32)] * 2 + [S((D, D), GRAD_DT)],
        scratch=[pltpu.VMEM((W, D), bf16), pltpu.VMEM((W, D), bf16), pltpu.VMEM((D, D), f32)], vmem=VMEM_LIMIT, plan=plan)


def branches_bwd(dYA, dYB, ya, yb, wso, wco):
    T = dYA.shape[0]
    tm = min(1024, T)

    def body(da_ref, db_ref, ya_ref, yb_ref, wa_ref, wb_ref, oa_ref, ob_ref, ga_ref, gb_ref, wa_s, wb_s, acc_a, acc_b):
        @pl.when(pl.program_id(0) == 0)
        def _():
            _dense_columns(wa_ref, wa_s)
            _dense_columns(wb_ref, wb_s)
            acc_a[...] = jnp.zeros_like(acc_a)
            acc_b[...] = jnp.zeros_like(acc_b)

        oa_ref[...] = _dot_nt(da_ref[...], wa_s[...])
        ob_ref[...] = _dot_nt(db_ref[...], wb_s[...])
        acc_a[...] += _dot_tn(ya_ref[...], da_ref[...])
        acc_b[...] += _dot_tn(yb_ref[...], db_ref[...])

        @pl.when(pl.program_id(0) == pl.num_programs(0) - 1)
        def _():
            for k in range(NDEV):
                ga_ref[k] = acc_a[:, k * LANE:(k + 1) * LANE].astype(GRAD_DT)
                gb_ref[k] = acc_b[:, k * LANE:(k + 1) * LANE].astype(GRAD_DT)

    row = pl.BlockSpec((tm, D), lambda i: (i, 0))
    osp = pl.BlockSpec((tm, W), lambda i: (i, 0))
    blocks = pl.BlockSpec((NDEV, W, LANE), lambda i: (0, 0, 0))
    outs, _ = _call(
        body, [dYA, dYB, ya, yb, wso, wco], name="branches_bwd", grid=(T // tm,),
        in_specs=[row, row, osp, osp, _resident((NDEV, W, LANE)), _resident((NDEV, W, LANE))],
        out_specs=[osp, osp, blocks, blocks], out_shape=[S((T, W), f32)] * 2 + [S((NDEV, W, LANE), GRAD_DT)] * 2,
        scratch=[pltpu.VMEM((W, D), bf16)] * 2 + [pltpu.VMEM((W, D), f32)] * 2, sem=("arbitrary",), vmem=VMEM_LIMIT)
    return outs


def glu_bwd(yn, dya, glu_w, glu_b, plan):
    T = yn.shape[0]
    tm = min(512, T)

    def body(y_ref, d_ref, w_ref, b_ref, dy_ref, db_ref, dw_ref, acc):
        @pl.when(pl.program_id(0) == 0)
        def _():
            db_ref[...] = jnp.zeros_like(db_ref)
            acc[...] = jnp.zeros_like(acc)

        y, dya_ = y_ref[...], d_ref[...]
        g = _gelu(y)
        gb = g.astype(bf16)
        s = _sigmoid(_dot(gb, w_ref[...]) + b_ref[...])
        dsp = dya_ * g * s * (1.0 - s)
        dspb = dsp.astype(bf16)
        dg = dya_ * s + _dot_nt(dspb, w_ref[...])
        dy_ref[...] = dg * _gelu_grad(y)
        db_ref[...] += jnp.sum(dsp, axis=0, keepdims=True)
        acc[...] += _dot_tn(gb, dspb)

        @pl.when(pl.program_id(0) == pl.num_programs(0) - 1)
        def _():
            dw_ref[...] = acc[...].astype(GRAD_DT)

    row = pl.BlockSpec((tm, W), lambda i: (i, 0))
    vec = pl.BlockSpec((1, W), lambda i: (0, 0))
    mat = pl.BlockSpec((W, W), lambda i: (0, 0))
    return _call(
        body, [yn, dya, glu_w, glu_b], name="glu_bwd", grid=(T // tm,),
        in_specs=[row, row, mat, vec],
        out_specs=[row, vec, mat], out_shape=[S((T, W), f32), S((1, W), f32), S((W, W), GRAD_DT)],
        scratch=[pltpu.VMEM((W, W), f32)], sem=("arbitrary",), plan=plan)


def conv_bwd(proj, dyb, conv_w, plan):
    T = proj.shape[0]
    RB = min(512, T)
    nrb = T // RB

    def body(h_ref, c_ref, b_ref, d_ref, w_ref, dh_ref, dc_ref, db_ref, dw_ref, s_ref):
        w0, w1, w2 = w_ref[0:1, :], w_ref[1:2, :], w_ref[2:3, :]

        def blk(i, carry):
            a0, a1, a2, sh, sc, sb = carry
            r0 = pl.multiple_of(i * RB, RB)
            rs = pl.ds(r0, RB)
            h, cg, bg, dyb_ = h_ref[rs, :], c_ref[rs, :], b_ref[rs, :], d_ref[rs, :]
            ch = cg * h
            pr = pl.ds(jnp.maximum(r0 - 8, 0), 8)
            prev = jnp.where(i > 0, c_ref[pr, :] * h_ref[pr, :], 0.0)
            ch1, ch2 = _shift_rows(ch, prev, 1), _shift_rows(ch, prev, 2)
            dbg = dyb_ * (w2 * ch + w1 * ch1 + w0 * ch2)
            db_ref[rs, :] = dbg.astype(bf16)
            dz = dyb_ * bg
            nx = pl.ds(jnp.minimum(r0 + RB, T - 8), 8)
            nxt = jnp.where(i < nrb - 1, d_ref[nx, :] * b_ref[nx, :], 0.0)
            dch = w2 * dz + w1 * _lift_rows(dz, nxt, 1) + w0 * _lift_rows(dz, nxt, 2)
            dcg, dh = dch * h, dch * cg
            dc_ref[rs, :] = dcg.astype(bf16)
            dh_ref[rs, :] = dh.astype(bf16)
            col = lambda v: jnp.sum(v, axis=0, keepdims=True)
            return (a0 + col(dz * ch2), a1 + col(dz * ch1), a2 + col(dz * ch), sh + col(dh), sc + col(dcg), sb + col(dbg))

        zero = jnp.zeros((1, LANE), f32)
        a0, a1, a2, sh, sc, sb = lax.fori_loop(0, nrb, blk, (zero,) * 6)
        dw_ref[0:1, :] = a0
        dw_ref[1:2, :] = a1
        dw_ref[2:3, :] = a2
        s_ref[0:1, :] = sh
        s_ref[1:2, :] = sc
        s_ref[2:3, :] = sb

    nb = W // LANE
    slab = pl.BlockSpec((T, LANE), lambda k: (0, k))
    three = pl.BlockSpec((3, LANE), lambda k: (0, k))
    return _call(
        body, [proj, proj, proj, dyb, conv_w], name="conv_bwd", grid=(nb,),
        in_specs=[pl.BlockSpec((T, LANE), lambda k: (0, 4 * nb + k)), pl.BlockSpec((T, LANE), lambda k: (0, 5 * nb + k)),
                  pl.BlockSpec((T, LANE), lambda k: (0, 6 * nb + k)), slab, three],
        out_specs=[slab, slab, slab, three, three],
        out_shape=[S((T, W), bf16)] * 3 + [S((3, W), f32)] * 2, sem=("parallel",), vmem=VMEM_LIMIT, plan=plan)


def in_proj_bwd_x(parts, win_g, base, scale, name, plan=None):
    T = base.shape[0]
    tm = min(512, T)
    n = len(parts)

    def body(*refs):
        p_refs, w_ref, b_ref, o_ref = refs[:n], refs[n], refs[n + 1], refs[n + 2]
        acc = scale * b_ref[...]
        for p_ref, (_, _, k) in zip(p_refs, parts):
            acc += _dot_nt(p_ref[...], w_ref[k])
        o_ref[...] = acc

    row = pl.BlockSpec((tm, D), lambda i: (i, 0))
    p_specs = [pl.BlockSpec((tm, W), (lambda i, cb=cb: (i, cb))) for _, cb, _ in parts]
    return _call(
        body, [a for a, _, _ in parts] + [win_g, base], name=name, grid=(T // tm,),
        in_specs=p_specs + [_resident((NDEV, D, W)), row],
        out_specs=[row], out_shape=[S((T, D), f32)], vmem=VMEM_LIMIT, plan=plan)


def ssm_param_bwd(lam_re, lam_im, log_dt, fr, fi, br, bi, dwb, dwcT, dlbr, dlbi):
    def body(lr_ref, li_ref, ldt_ref, fr_ref, fi_ref, br_ref, bi_ref, dwb_ref, dwc_ref, dlbr_ref, dlbi_ref,
             dbr_ref, dbi_ref, dlr_ref, dli_ref, dldt_ref, dcr_ref, dci_ref, dr_s, di_s):
        for k in range(W // LANE):
            for gl in range(NG // (W // LANE)):
                rows, src = slice((8 * k + gl) * GC, (8 * k + gl + 1) * GC), slice(gl * GC, (gl + 1) * GC)
                re, im = slice(gl * NP, (gl + 1) * NP), slice(SW + gl * NP, SW + (gl + 1) * NP)
                dr_s[rows, :] = dwb_ref[k, src, re]
                di_s[rows, :] = dwb_ref[k, src, im]
                dcr_ref[rows, :] = dwc_ref[k, src, re]
                dci_ref[rows, :] = -dwc_ref[k, src, im]
        fr_, fi_ = _per_channel(fr_ref[...]), _per_channel(fi_ref[...])
        br_, bi_, dr, di = br_ref[...], bi_ref[...], dr_s[...], di_s[...]
        dbr_ref[...] = fr_ * dr + fi_ * di
        dbi_ref[...] = fr_ * di - fi_ * dr
        dfr = jnp.sum((dr * br_ + di * bi_).reshape(NG, GC, NP), axis=1)
        dfi = jnp.sum((di * br_ - dr * bi_).reshape(NG, GC, NP), axis=1)
        _, vjp = jax.vjp(_disc, lr_ref[...], li_ref[...], ldt_ref[...])
        dlr_ref[...], dli_ref[...], dldt = vjp((dlbr_ref[...], dlbi_ref[...], dfr, dfi))
        dldt_ref[...] = _transpose_exact(dldt)

    blk = S((NG * GC, NP), f32)
    return pl.pallas_call(
        body, name="ssm_param_bwd", out_shape=[blk, blk, S((NG, NP), f32), S((NG, NP), f32), S((1, NG), f32), blk, blk],
        scratch_shapes=[pltpu.VMEM((NG * GC, NP), f32)] * 2)(
        lam_re, lam_im, log_dt, fr, fi, br, bi, dwb, dwcT, dlbr, dlbi)


def _adam(w, g, m, v):
    m = ADAM_B1 * m + (1.0 - ADAM_B1) * g
    v = ADAM_B2 * v + (1.0 - ADAM_B2) * (g * g)
    m_hat = m / (1.0 - ADAM_B1 ** ADAM_STEP)
    v_hat = v / (1.0 - ADAM_B2 ** ADAM_STEP)
    return -ADAM_LR * (m_hat / (jnp.sqrt(v_hat) + ADAM_EPS) + ADAM_WD * w), m, v


def _sum_in_order(c_ref):
    g = c_ref[0].astype(f32)
    for k in range(1, c_ref.shape[0]):
        g = g + c_ref[k].astype(f32)
    return g


def sum_blocks(contrib, name):
    def body(c_ref, o_ref):
        o_ref[...] = _sum_in_order(c_ref)

    return pl.pallas_call(body, name=name, out_shape=S(contrib.shape[1:], f32))(contrib)


def adam_update(w, m, v, contrib, name, rows_per_block=None, summed_on_0=None, plan=None):
    R, C = w.shape
    n = contrib.shape[0]
    tr = min(rows_per_block or R, R)

    def body(w_ref, m_ref, v_ref, c_ref, *refs):
        g_ref, d_ref, nm_ref, nv_ref = refs[-4:]
        g = _sum_in_order(c_ref)
        if summed_on_0 is not None:
            x, y, c = _coords()
            g = jnp.where(4 * x + 2 * y + c == 0, refs[0][...], g)
        g_ref[...] = g
        d_ref[...], nm_ref[...], nv_ref[...] = _adam(w_ref[...], g, m_ref[...], v_ref[...])

    blk = pl.BlockSpec((tr, C), lambda i: (i, 0))
    extra = [] if summed_on_0 is None else [summed_on_0]
    return _call(
        body, [w, m, v, contrib] + extra, name=name, grid=(R // tr,),
        in_specs=[blk, blk, blk, pl.BlockSpec((n, tr, C), lambda i: (0, i, 0))] + [blk] * len(extra),
        out_specs=[blk] * 4, out_shape=[S((R, C), f32)] * 4, sem=("parallel",), vmem=VMEM_LIMIT, plan=plan)


_ROWVEC = (("b_in", IN_COLS), ("ssm_d", W), ("glu_b", W), ("ln1_g", D), ("ln1_b", D), ("ln2_g", D), ("ln2_b", D))
_HALF = NG * GC // 2
_BC_LANE = {"ssm_b_re": 0, "ssm_b_im": NP, "ssm_c_re": 0, "ssm_c_im": NP}
_PACK = {}
_r = 0
for _n, _k in _ROWVEC:
    _PACK[_n] = _r
    _r += _k // LANE
for _n, _rows in (("ssm_lambda", NG), ("scalars", 8), ("ssm_b", _HALF), ("ssm_c", _HALF), ("conv_w", 16)):
    _PACK[_n] = _r
    _r += _rows
for _n in _BC_LANE:
    _PACK[_n] = _PACK[_n[:5]]
PACK_ROWS = _r
assert PACK_ROWS % 8 == 0
_SMALL = ("b_in", "ssm_lambda_re", "ssm_lambda_im", "ssm_log_dt", "ssm_b_re", "ssm_b_im", "ssm_c_re", "ssm_c_im",
          "ssm_d", "glu_b", "ln1_g", "ln1_b", "ln2_g", "ln2_b")


def pack_grads(su, shcb, sga, sgb, dd, dglu_b, dln1_g, dln1_b, dln2_g, dln2_b, dlam_re, dlam_im, dldt, sqerr, dbr, dbi,
               dc_re, dc_im, dconv):
    nI = sga.shape[0]

    def body(su_ref, sh_ref, sga_ref, sgb_ref, dd_ref, gb_ref, l1g_ref, l1b_ref, l2g_ref, l2b_ref, lr_ref, li_ref, dt_ref,
             sq_ref, br_ref, bi_ref, cr_ref, ci_ref, cw_ref, o_ref):
        o_ref[...] = jnp.zeros_like(o_ref)

        def put_row(name, v):
            r0 = _PACK[name]
            for i in range(v.shape[1] // LANE):
                o_ref[r0 + i:r0 + i + 1, :] = v[:, i * LANE:(i + 1) * LANE]

        ga, gb = sga_ref[0], sgb_ref[0]
        for i in range(1, nI):
            ga, gb = ga + sga_ref[i], gb + sgb_ref[i]
        put_row("b_in", jnp.concatenate([su_ref[k] for k in range(W // LANE)]
                                        + [sh_ref[0:1, :], sh_ref[1:2, :], sh_ref[2:3, :], ga, gb], axis=1))
        put_row("ssm_d", jnp.concatenate([dd_ref[k] for k in range(W // LANE)], axis=1))
        put_row("glu_b", gb_ref[...])
        put_row("ln1_g", l1g_ref[...])
        put_row("ln1_b", l1b_ref[...])
        put_row("ln2_g", l2g_ref[...])
        put_row("ln2_b", l2b_ref[...])
        r0 = _PACK["ssm_lambda"]
        o_ref[r0:r0 + NG, 0:NP] = lr_ref[...]
        o_ref[r0:r0 + NG, NP:2 * NP] = li_ref[...]
        r0 = _PACK["scalars"]
        o_ref[r0:r0 + 1, 0:NG] = dt_ref[...]
        o_ref[r0 + 1:r0 + 2, 0:1] = sq_ref[...]
        for name, ref in (("ssm_b_re", br_ref), ("ssm_b_im", bi_ref), ("ssm_c_re", cr_ref), ("ssm_c_im", ci_ref)):
            r0, l0 = _PACK[name], _BC_LANE[name]
            o_ref[r0:r0 + _HALF, l0:l0 + NP] = pltpu.bitcast(ref[...].astype(bf16), f32)
        for cb in range(W // LANE):
            o_ref[_PACK["conv_w"] + 3 * cb:_PACK["conv_w"] + 3 * cb + 3, :] = cw_ref[:, cb * LANE:(cb + 1) * LANE]

    return pl.pallas_call(body, name="pack_grads", out_shape=S((PACK_ROWS, LANE), f32))(
        su, shcb, sga, sgb, dd, dglu_b, dln1_g, dln1_b, dln2_g, dln2_b, dlam_re, dlam_im, dldt, sqerr, dbr, dbi, dc_re, dc_im,
        dconv)


def adam_small(packed_all, params):
    names = list(_SMALL) + ["conv_w"]
    flat = [a for n in names for a in params[n]]

    def body(*refs):
        p_ref = refs[0]
        ins = refs[1:1 + 3 * len(names)]
        outs = refs[1 + 3 * len(names):-2]
        loss_ref, g_ref = refs[-2], refs[-1]

        def part(k, rs=slice(None), ls=slice(None)):
            return p_ref[k, rs, ls]

        g_all = part(0)
        for k in range(1, NDEV):
            g_all = g_all + part(k)
        g_ref[...] = g_all

        def rows(name, r0, n, l0=0, lanes=LANE):
            return g_ref[_PACK[name] + r0:_PACK[name] + r0 + n, l0:l0 + lanes]

        def grad_of(name):
            if name in dict(_ROWVEC):
                return jnp.concatenate([rows(name, i, 1) for i in range(dict(_ROWVEC)[name] // LANE)], axis=1)
            if name in ("ssm_lambda_re", "ssm_lambda_im"):
                return rows("ssm_lambda", 0, NG, NP * (name == "ssm_lambda_im"), NP)[None]
            if name == "ssm_log_dt":
                return rows("scalars", 0, 1, 0, NG)
            if name in _BC_LANE:
                rs, ls = slice(_PACK[name], _PACK[name] + _HALF), slice(_BC_LANE[name], _BC_LANE[name] + NP)
                g = pltpu.bitcast(part(0, rs, ls), bf16).astype(f32)
                for k in range(1, NDEV):
                    g = g + pltpu.bitcast(part(k, rs, ls), bf16).astype(f32)
                return g.reshape(1, NG, GC, NP)
            full = jnp.concatenate([rows("conv_w", 3 * cb, 3) for cb in range(W // LANE)], axis=1)
            x, y, c = _coords()
            col0 = (4 * x + 2 * y + c) * (W // NDEV)
            sel = (lax.broadcasted_iota(jnp.int32, (W, W // NDEV), 0)
                   == lax.broadcasted_iota(jnp.int32, (W, W // NDEV), 1) + col0).astype(f32)
            return jnp.dot(full, sel, precision=HIGHEST, preferred_element_type=f32)[None]

        loss_ref[...] = 0.5 * rows("scalars", 1, 1, 0, 1)
        for i, name in enumerate(names):
            w_ref, m_ref, v_ref = ins[3 * i:3 * i + 3]
            g = grad_of(name)
            d, m, v = _adam(w_ref[...], g, m_ref[...], v_ref[...])
            outs[4 * i][...] = g
            outs[4 * i + 1][...] = d
            outs[4 * i + 2][...] = m
            outs[4 * i + 3][...] = v

    out_shape = [S(params[n][0].shape, f32) for n in names for _ in range(4)] + [S((1, 1), f32)]
    res = pl.pallas_call(body, name="adam_small", out_shape=out_shape, scratch_shapes=[pltpu.VMEM((PACK_ROWS, LANE), f32)],
                         compiler_params=_cp(None, VMEM_LIMIT))(packed_all, *flat)
    return {n: res[4 * i:4 * i + 4] for i, n in enumerate(names)}, res[-1]


def _block_diag(wgt):
    eye = jnp.eye(8, dtype=wgt.dtype)
    out = wgt[:, :, :, None, :] * eye[None, :, None, :, None]
    return out.reshape(4, 8 * wgt.shape[2], 8 * wgt.shape[3])


def kernel(x, w_in, b_in, ssm_lambda_re, ssm_lambda_im, ssm_log_dt, ssm_b_re, ssm_b_im, ssm_c_re, ssm_c_im, ssm_d, glu_w, glu_b, w_ssm_out, conv_w, w_conv_out, w_o, ln1_g, ln1_b, w_gate, w_up, w_down, ln2_g, ln2_b, loss_target, m_w_in, m_b_in, m_ssm_lambda_re, m_ssm_lambda_im, m_ssm_log_dt, m_ssm_b_re, m_ssm_b_im, m_ssm_c_re, m_ssm_c_im, m_ssm_d, m_glu_w, m_glu_b, m_w_ssm_out, m_conv_w, m_w_conv_out, m_w_o, m_ln1_g, m_ln1_b, m_w_gate, m_w_up, m_w_down, m_ln2_g, m_ln2_b, v_w_in, v_b_in, v_ssm_lambda_re, v_ssm_lambda_im, v_ssm_log_dt, v_ssm_b_re, v_ssm_b_im, v_ssm_c_re, v_ssm_c_im, v_ssm_d, v_glu_w, v_glu_b, v_w_ssm_out, v_conv_w, v_w_conv_out, v_w_o, v_ln1_g, v_ln1_b, v_w_gate, v_w_up, v_w_down, v_ln2_g, v_ln2_b):
    given = dict(locals())
    xs = x[0]
    target = loss_target[0]

    tr = lambda a: jnp.swapaxes(a[0], 0, 1)
    (win_s,), _ = prep_weights([w_in[0]], "prep_w_in")
    (glu_s, wso_s, wco_s, wo_s, wgT_s, wuT_s, wd_s), (win_g,) = prep_weights(
        [glu_w[0], w_ssm_out[0], w_conv_out[0], w_o[0], tr(w_gate), tr(w_up), w_down[0]], "prep_weights",
        GatherPlan([win_s], srcs=(0,)))

    lam_re, lam_im = ssm_lambda_re[0], ssm_lambda_im[0]
    ldt = ssm_log_dt[0].reshape(NG, 1)
    br2 = jnp.swapaxes(ssm_b_re[0], 1, 2).reshape(NG * GC, NP)
    bi2 = jnp.swapaxes(ssm_b_im[0], 1, 2).reshape(NG * GC, NP)
    lbr, lbi, fr, fi, bbr, bbi = ssm_params(lam_re, lam_im, ldt, br2, bi2)
    bb_t = lambda b: b.reshape(4, 8, GC, NP)
    wb = jnp.concatenate([_block_diag(bb_t(bbr)), _block_diag(bb_t(bbi))], axis=2)
    c_t = lambda c: c.reshape(4, 8, GC, NP).transpose(0, 1, 3, 2)
    wc = jnp.concatenate([_block_diag(c_t(ssm_c_re[0])), -_block_diag(c_t(ssm_c_im[0]))], axis=1)
    wbT, wcT = wb.transpose(0, 2, 1), wc.transpose(0, 2, 1)
    wb, wc, wbT, wcT = wb.astype(bf16), wc.astype(bf16), wbT.astype(bf16), wcT.astype(bf16)
    lbr_s, lbi_s = lbr.reshape(4, 1, SW), lbi.reshape(4, 1, SW)
    dsk = ssm_d[0].reshape(4, 1, LANE)

    u_nat, xb = in_proj_u(xs, win_g, b_in)
    half_a, half_b = (0, 3, 5, 6), (1, 2, 4, 7)
    (yn, u_p, xr_p, xi_p), (win_g, conv_g, glu_g, wso_g, wuT_g) = ssm_fwd(
        u_nat, wb, wc, lbr_s, lbi_s, dsk,
        Plans([GatherPlan([win_s], srcs=tuple(range(1, NDEV)), into=[win_g]), GatherPlan([conv_w[0], glu_s, wso_s]),
               GatherPlan([wuT_s], srcs=half_a)]))
    conv_f = conv_g.transpose(1, 0, 2).reshape(3, W)
    (proj,), (wco_g, wo_g, wgT_g) = in_proj_rest(
        xb, win_g, b_in, Plans([GatherPlan([wco_s, wo_s]), GatherPlan([wgT_s], srcs=half_a)]))
    glu_f, wo_f = glu_g.reshape(W, W), wo_g.reshape(D, D)
    yb = conv_fwd(proj, conv_f)
    (merged, ya), (wgT_g,) = merge_fwd(yn, glu_f, glu_b, yb, wso_g, wco_g, proj,
                                       GatherPlan([wgT_s], srcs=half_b, into=[wgT_g]))
    (r1, x1b), (wuT_g,) = mix_ln1(merged, wo_f, xs, ln1_g, ln1_b, GatherPlan([wuT_s], srcs=half_b, into=[wuT_g]))
    wgT, wuT = wgT_g.reshape(F, D), wuT_g.reshape(F, D)
    (gate, up, hid), (wd_g,) = gate_up(x1b, wgT, wuT, GatherPlan([wd_s]))
    wd_f = wd_g.reshape(F, D)
    dr2, dffn, sqerr, dln2_g, dln2_b = down_loss(hid, wd_f, r1, ln1_g, ln1_b, ln2_g, ln2_b, target)

    dwd, _ = mm_tn_rows(hid, dffn, "grad_w_down")
    dwd = dwd.reshape(NDEV, FS, D)
    (dgate, dup), (r_wd,) = ffn_bwd_act(dffn, wd_f, gate, up, ScatterPlan([dwd], only=half_a))
    dwgT, (r_wd,) = mm_tn_rows(dgate, x1b, "grad_w_gate", plan=ScatterPlan([dwd], only=half_b, into=[r_wd]))
    dwgT = dwgT.reshape(NDEV, FS, D)
    dwuT, (r_wgT,) = mm_tn_rows(dup, x1b, "grad_w_up", plan=ScatterPlan([dwgT], only=half_a))
    dwuT = dwuT.reshape(NDEV, FS, D)
    (dr1, dmix, dln1_g, dln1_b), (r_wgT, r_wuT) = ffn_bwd_x(
        dgate, dup, wgT, wuT, dr2, r1, ln1_g,
        Plans([ScatterPlan([dwgT], only=half_b, into=[r_wgT]), ScatterPlan([dwuT], only=half_a)]))
    (dYA, dYB, dga, dgb, sga, sgb, dwo), (r_wuT,) = merge_bwd(dmix, wo_f, merged, ya, yb, wso_g, wco_g, proj,
                                                              ScatterPlan([dwuT], only=half_b, into=[r_wuT]))
    dwo = dwo.reshape(NDEV, D // NDEV, D)
    dya, dyb, dwso, dwco = branches_bwd(dYA, dYB, ya, yb, wso_g, wco_g)
    (dyn, dglu_b, dglu), (r_wso,) = glu_bwd(yn, dya, glu_f, glu_b, ScatterPlan([dwso]))
    dglu = dglu.reshape(NDEV, W // NDEV, W)
    (dh, dcg, dbg, dconv, shcb), (r_wco,) = conv_bwd(proj, dyb, conv_f, ScatterPlan([dwco]))
    dwin, (r_wo, r_glu) = grad_w_in_rest(xb, dh, dcg, dbg, dga, dgb, ScatterPlan([dwo, dglu]))
    (du, dwb, dwcT, dlbr_s, dlbi_s, dd, su), (r_win,) = ssm_bwd(
        u_p, dyn, xr_p, xi_p, wbT, wcT, lbr_s, lbi_s, dsk, ScatterPlan([dwin], only=tuple(range(1, NDEV))))

    dbr2, dbi2, dlam_re, dlam_im, dldt, dc_re, dc_im = ssm_param_bwd(
        lam_re, lam_im, ldt, fr, fi, br2, bi2, dwb, dwcT, dlbr_s.reshape(NG, NP), dlbi_s.reshape(NG, NP))
    packed = pack_grads(su, shcb, sga, sgb, dd, dglu_b, dln1_g, dln1_b, dln2_g, dln2_b, dlam_re, dlam_im, dldt, sqerr,
                        dbr2, dbi2, dc_re, dc_im, dconv)
    dwin_u = mm_tn(xb, du, "grad_w_in_u").reshape(NDEV, D // NDEV, W)

    rest = [(dh, 0, 1), (dcg, 0, 2), (dbg, 0, 3), (dga, 0, 4), (dga, 1, 5), (dgb, 0, 6), (dgb, 1, 7)]
    (gx_rest,), (r_win_u, small_all) = in_proj_bwd_x(
        rest, win_g, dr1, ALPHA, "in_proj_bwd_x_rest", Plans([ScatterPlan([dwin_u]), GatherPlan([packed])]))
    my_rows = sum_blocks(r_win_u, "sum_w_in_u")

    out = {}

    def put(name, res, back=lambda a: a[None]):
        out["grad_" + name], out["delta_" + name], out["new_m_" + name], out["new_v_" + name] = [back(r) for r in res]

    res_wd, (win_u_sum,) = adam_update(w_down[0], m_w_down[0], v_w_down[0], r_wd, "adam_w_down", 176,
                                       plan=ScatterPlan([my_rows], only=(0,), whole=True))
    put("w_down", res_wd)
    (grad_x,), _ = in_proj_bwd_x([(du, 0, 0)], win_g, gx_rest, 1.0, "in_proj_bwd_x_u")
    put("w_in", adam_update(w_in[0], m_w_in[0], v_w_in[0], r_win, "adam_w_in", 256,
                            summed_on_0=win_u_sum.reshape(D, W))[0])
    put("glu_w", adam_update(glu_w[0], m_glu_w[0], v_glu_w[0], r_glu, "adam_glu_w")[0])
    put("w_ssm_out", adam_update(w_ssm_out[0], m_w_ssm_out[0], v_w_ssm_out[0], r_wso, "adam_w_ssm_out")[0])
    put("w_conv_out", adam_update(w_conv_out[0], m_w_conv_out[0], v_w_conv_out[0], r_wco, "adam_w_conv_out")[0])
    put("w_o", adam_update(w_o[0], m_w_o[0], v_w_o[0], r_wo, "adam_w_o")[0])
    untr = lambda a: jnp.swapaxes(a, 0, 1)[None]
    put("w_gate", adam_update(tr(w_gate), tr(m_w_gate), tr(v_w_gate), r_wgT, "adam_w_gate", 176)[0], untr)
    put("w_up", adam_update(tr(w_up), tr(m_w_up), tr(v_w_up), r_wuT, "adam_w_up", 176)[0], untr)
    as_c = lambda a: jnp.swapaxes(a, 2, 3)
    params = {n: (given[n], given["m_" + n], given["v_" + n]) for n in list(_SMALL) + ["conv_w"]}
    for n in ("ssm_b_re", "ssm_b_im"):
        params[n] = tuple(as_c(a) for a in params[n])
    small, loss = adam_small(small_all, params)
    for n, res in small.items():
        put(n, res, as_c if n in ("ssm_b_re", "ssm_b_im") else (lambda a: a))

    names = ["w_in", "b_in", "ssm_lambda_re", "ssm_lambda_im", "ssm_log_dt", "ssm_b_re", "ssm_b_im", "ssm_c_re", "ssm_c_im",
             "ssm_d", "glu_w", "glu_b", "w_ssm_out", "conv_w", "w_conv_out", "w_o", "ln1_g", "ln1_b", "w_gate", "w_up",
             "w_down", "ln2_g", "ln2_b"]
    return (loss.reshape(()), grad_x[None], *[out[p + n] for p in ("grad_", "delta_", "new_m_", "new_v_") for n in names])
```
